```python
import jax, jax.numpy as jnp
from jax import lax
import numpy as np

D_MODEL = 1024
BATCH = 8
SEQ = 2048
DEPTH = 1

MLA_HEADS = 8
Q_LORA_RANK = 384
KV_LORA_RANK = 128
QK_NOPE_DIM = 64
QK_ROPE_DIM = 32
V_HEAD_DIM = 64
QK_HEAD_DIM = QK_NOPE_DIM + QK_ROPE_DIM
MLA_WIDTH = MLA_HEADS * V_HEAD_DIM
Q_BLOCK = 128
ROPE_THETA = 10000.0
SGU_GROUPS = 8
SGU_GROUP_DIM = 64
SGU_WIDTH = SGU_GROUPS * SGU_GROUP_DIM
CHUNK = 128
RMS_EPS = 1e-6
LN_EPS = 1e-5
DN_ALPHA = (2.0 * DEPTH) ** 0.25
DN_BETA = (8.0 * DEPTH) ** -0.25
IN_SPLITS = (Q_LORA_RANK, KV_LORA_RANK, QK_ROPE_DIM, MLA_WIDTH,
             SGU_WIDTH, SGU_WIDTH, SGU_WIDTH, D_MODEL, D_MODEL)
IN_WIDTH = sum(IN_SPLITS)

kernel_name = "hybrid_mla_sgu_gated_deepnorm"


def rms_norm(x, g):
    xf = x.astype(jnp.float32)
    y = xf * lax.rsqrt(jnp.mean(xf * xf, axis=-1, keepdims=True) + RMS_EPS)
    return (y * g.astype(jnp.float32)).astype(x.dtype)


def layer_norm(x, g, b):
    xf = x.astype(jnp.float32)
    mu = jnp.mean(xf, axis=-1, keepdims=True)
    xc = xf - mu
    var = jnp.mean(xc * xc, axis=-1, keepdims=True)
    y = xc * lax.rsqrt(var + LN_EPS) * g.astype(jnp.float32) + b.astype(jnp.float32)
    return y.astype(x.dtype)


def rope_tables(positions):
    inv_freq = ROPE_THETA ** (-jnp.arange(0, QK_ROPE_DIM, 2, dtype=jnp.float32) / QK_ROPE_DIM)
    ang = positions.astype(jnp.float32)[..., None] * inv_freq
    return jnp.cos(ang)[:, :, None, :], jnp.sin(ang)[:, :, None, :]


def apply_rope(x, cos, sin):
    xf = x.astype(jnp.float32)
    half = QK_ROPE_DIM // 2
    x1, x2 = xf[..., :half], xf[..., half:]
    return jnp.concatenate([x1 * cos - x2 * sin, x2 * cos + x1 * sin], axis=-1).astype(x.dtype)


def split_columns(h):
    parts, start = [], 0
    for size in IN_SPLITS:
        parts.append(h[..., start:start + size])
        start += size
    return parts


def mla_attention(c_q, c_kv, k_pe, cos, sin, g_q, w_uq, g_kv, w_ukv):
    q = jnp.einsum('bsr,rhd->bshd', rms_norm(c_q, g_q), w_uq)
    q = jnp.concatenate([q[..., :QK_NOPE_DIM], apply_rope(q[..., QK_NOPE_DIM:], cos, sin)], axis=-1)
    kv = jnp.einsum('bsr,rhd->bshd', rms_norm(c_kv, g_kv), w_ukv)
    k_nope, v = kv[..., :QK_NOPE_DIM], kv[..., QK_NOPE_DIM:]
    k_pe = apply_rope(k_pe[:, :, None, :], cos, sin)
    k = jnp.concatenate([k_nope, jnp.broadcast_to(k_pe, k_nope.shape[:-1] + (QK_ROPE_DIM,))], axis=-1)
    scale = QK_HEAD_DIM ** -0.5
    seq = q.shape[1]
    outs = []
    for start in range(0, seq, Q_BLOCK):
        end = start + Q_BLOCK
        s = jnp.einsum('bqhd,bkhd->bhqk', q[:, start:end], k[:, :end]).astype(jnp.float32) * scale
        causal = jnp.arange(end)[None, :] <= (start + jnp.arange(Q_BLOCK))[:, None]
        p = jax.nn.softmax(jnp.where(causal, s, -jnp.inf), axis=-1)
        outs.append(jnp.einsum('bhqk,bkhd->bqhd', p.astype(v.dtype), v[:, :end]))
    o = jnp.concatenate(outs, axis=1)
    return o.reshape(o.shape[0], seq, MLA_WIDTH)


def spatial_gating(u, v, ln_g, ln_b, w_s, b_s):
    u = jax.nn.gelu(u)
    v = layer_norm(jax.nn.gelu(v), ln_g, ln_b)
    bsz, seq, _ = v.shape
    vc = v.reshape(bsz, seq // CHUNK, CHUNK, SGU_GROUPS, SGU_GROUP_DIM)
    mixed = jnp.einsum('gts,bcsgd->bctgd', jnp.tril(w_s), vc) + b_s.T[:, :, None]
    return u * mixed.reshape(bsz, seq, SGU_WIDTH)


def _fwd_setup_inputs(seed: int = 0) -> dict:
    key = jax.random.key(seed)
    ks = jax.random.split(key, 17)
    nrm = jax.random.normal
    f32 = jnp.float32
    x = nrm(ks[0], (BATCH, SEQ, D_MODEL), f32)
    offset = jax.random.randint(ks[1], (BATCH, 1), 0, 4096, dtype=jnp.int32)
    positions = (offset + jnp.arange(SEQ, dtype=jnp.int32)[None, :]).astype(jnp.int32)
    w_in = nrm(ks[2], (DEPTH, D_MODEL, IN_WIDTH), f32) * D_MODEL ** -0.5
    b_in = 0.02 * nrm(ks[3], (DEPTH, IN_WIDTH), f32)
    g_q = 1.0 + 0.02 * nrm(ks[4], (DEPTH, Q_LORA_RANK), f32)
    w_uq = nrm(ks[5], (DEPTH, Q_LORA_RANK, MLA_HEADS, QK_HEAD_DIM), f32) * Q_LORA_RANK ** -0.5
    g_kv = 1.0 + 0.02 * nrm(ks[6], (DEPTH, KV_LORA_RANK), f32)
    w_ukv = nrm(ks[7], (DEPTH, KV_LORA_RANK, MLA_HEADS, QK_NOPE_DIM + V_HEAD_DIM), f32) * KV_LORA_RANK ** -0.5
    w_oa = nrm(ks[8], (DEPTH, MLA_WIDTH, D_MODEL), f32) * (MLA_WIDTH ** -0.5 * DN_BETA)
    sgu_ln_g = 1.0 + 0.02 * nrm(ks[9], (DEPTH, SGU_WIDTH), f32)
    sgu_ln_b = 0.02 * nrm(ks[10], (DEPTH, SGU_WIDTH), f32)
    w_s = nrm(ks[11], (DEPTH, SGU_GROUPS, CHUNK, CHUNK), f32) * CHUNK ** -0.5
    b_s = 1.0 + 0.02 * nrm(ks[12], (DEPTH, SGU_GROUPS, CHUNK), f32)
    w_ob = nrm(ks[13], (DEPTH, SGU_WIDTH, D_MODEL), f32) * (SGU_WIDTH ** -0.5 * DN_BETA)
    w_out = nrm(ks[14], (DEPTH, D_MODEL, D_MODEL), f32) * (D_MODEL ** -0.5 * DN_BETA)
    ln_g = 1.0 + 0.02 * nrm(ks[15], (DEPTH, D_MODEL), f32)
    ln_b = 0.02 * nrm(ks[16], (DEPTH, D_MODEL), f32)
    return {"x": x, "positions": positions, "w_in": w_in, "b_in": b_in,
            "g_q": g_q, "w_uq": w_uq, "g_kv": g_kv, "w_ukv": w_ukv, "w_oa": w_oa,
            "sgu_ln_g": sgu_ln_g, "sgu_ln_b": sgu_ln_b, "w_s": w_s, "b_s": b_s,
            "w_ob": w_ob, "w_out": w_out, "ln_g": ln_g, "ln_b": ln_b}


def _fwd_reference(x, positions, w_in, b_in, g_q, w_uq, g_kv, w_ukv, w_oa,
              sgu_ln_g, sgu_ln_b, w_s, b_s, w_ob, w_out, ln_g, ln_b):
    cos, sin = rope_tables(positions)
    for l in range(DEPTH):
        h = jnp.einsum('bsd,dn->bsn', x, w_in[l]) + b_in[l]
        c_q, c_kv, k_pe, z_a, u, v, z_b, g_a, g_b = split_columns(h)
        y_a = mla_attention(c_q, c_kv, k_pe, cos, sin, g_q[l], w_uq[l], g_kv[l], w_ukv[l]) * jax.nn.silu(z_a)
        y_b = spatial_gating(u, v, sgu_ln_g[l], sgu_ln_b[l], w_s[l], b_s[l]) * jax.nn.silu(z_b)
        merged = (jax.nn.sigmoid(g_a) * jnp.einsum('bsc,cd->bsd', y_a, w_oa[l])
                  + jax.nn.sigmoid(g_b) * jnp.einsum('bsc,cd->bsd', y_b, w_ob[l]))
        x = layer_norm(DN_ALPHA * x + jnp.einsum('bsd,de->bse', merged, w_out[l]), ln_g[l], ln_b[l])
    return x


import jax as _jax
import jax.numpy as _jnp

TWIN_FORMAT = 'train_step'
FWD_PARAMS = ['x', 'positions', 'w_in', 'b_in', 'g_q', 'w_uq', 'g_kv', 'w_ukv', 'w_oa', 'sgu_ln_g', 'sgu_ln_b', 'w_s', 'b_s', 'w_ob', 'w_out', 'ln_g', 'ln_b']
TWIN_WEIGHTS = ['w_in', 'b_in', 'g_q', 'w_uq', 'g_kv', 'w_ukv', 'w_oa', 'sgu_ln_g', 'sgu_ln_b', 'w_s', 'b_s', 'w_ob', 'w_out', 'ln_g', 'ln_b']
TWIN_DIFF_INPUT = 'x'
TWIN_INPUTS = ['x', 'positions', 'w_in', 'b_in', 'g_q', 'w_uq', 'g_kv', 'w_ukv', 'w_oa', 'sgu_ln_g', 'sgu_ln_b', 'w_s', 'b_s', 'w_ob', 'w_out', 'ln_g', 'ln_b', 'loss_target', 'm_w_in', 'm_b_in', 'm_g_q', 'm_w_uq', 'm_g_kv', 'm_w_ukv', 'm_w_oa', 'm_sgu_ln_g', 'm_sgu_ln_b', 'm_w_s', 'm_b_s', 'm_w_ob', 'm_w_out', 'm_ln_g', 'm_ln_b', 'v_w_in', 'v_b_in', 'v_g_q', 'v_w_uq', 'v_g_kv', 'v_w_ukv', 'v_w_oa', 'v_sgu_ln_g', 'v_sgu_ln_b', 'v_w_s', 'v_b_s', 'v_w_ob', 'v_w_out', 'v_ln_g', 'v_ln_b']
TWIN_OUTPUTS = ['loss', 'grad_x', 'grad_w_in', 'grad_b_in', 'grad_g_q', 'grad_w_uq', 'grad_g_kv', 'grad_w_ukv', 'grad_w_oa', 'grad_sgu_ln_g', 'grad_sgu_ln_b', 'grad_w_s', 'grad_b_s', 'grad_w_ob', 'grad_w_out', 'grad_ln_g', 'grad_ln_b', 'delta_w_in', 'delta_b_in', 'delta_g_q', 'delta_w_uq', 'delta_g_kv', 'delta_w_ukv', 'delta_w_oa', 'delta_sgu_ln_g', 'delta_sgu_ln_b', 'delta_w_s', 'delta_b_s', 'delta_w_ob', 'delta_w_out', 'delta_ln_g', 'delta_ln_b', 'new_m_w_in', 'new_m_b_in', 'new_m_g_q', 'new_m_w_uq', 'new_m_g_kv', 'new_m_w_ukv', 'new_m_w_oa', 'new_m_sgu_ln_g', 'new_m_sgu_ln_b', 'new_m_w_s', 'new_m_b_s', 'new_m_w_ob', 'new_m_w_out', 'new_m_ln_g', 'new_m_ln_b', 'new_v_w_in', 'new_v_b_in', 'new_v_g_q', 'new_v_w_uq', 'new_v_g_kv', 'new_v_w_ukv', 'new_v_w_oa', 'new_v_sgu_ln_g', 'new_v_sgu_ln_b', 'new_v_w_s', 'new_v_b_s', 'new_v_w_ob', 'new_v_w_out', 'new_v_ln_g', 'new_v_ln_b']
TWIN_LEAF_KINDS = {'loss': 'loss', 'grad_x': 'grad_x', 'grad_w_in': 'grad_w', 'grad_b_in': 'grad_w', 'grad_g_q': 'grad_w', 'grad_w_uq': 'grad_w', 'grad_g_kv': 'grad_w', 'grad_w_ukv': 'grad_w', 'grad_w_oa': 'grad_w', 'grad_sgu_ln_g': 'grad_w', 'grad_sgu_ln_b': 'grad_w', 'grad_w_s': 'grad_w', 'grad_b_s': 'grad_w', 'grad_w_ob': 'grad_w', 'grad_w_out': 'grad_w', 'grad_ln_g': 'grad_w', 'grad_ln_b': 'grad_w', 'delta_w_in': 'delta_w', 'delta_b_in': 'delta_w', 'delta_g_q': 'delta_w', 'delta_w_uq': 'delta_w', 'delta_g_kv': 'delta_w', 'delta_w_ukv': 'delta_w', 'delta_w_oa': 'delta_w', 'delta_sgu_ln_g': 'delta_w', 'delta_sgu_ln_b': 'delta_w', 'delta_w_s': 'delta_w', 'delta_b_s': 'delta_w', 'delta_w_ob': 'delta_w', 'delta_w_out': 'delta_w', 'delta_ln_g': 'delta_w', 'delta_ln_b': 'delta_w', 'new_m_w_in': 'new_m', 'new_m_b_in': 'new_m', 'new_m_g_q': 'new_m', 'new_m_w_uq': 'new_m', 'new_m_g_kv': 'new_m', 'new_m_w_ukv': 'new_m', 'new_m_w_oa': 'new_m', 'new_m_sgu_ln_g': 'new_m', 'new_m_sgu_ln_b': 'new_m', 'new_m_w_s': 'new_m', 'new_m_b_s': 'new_m', 'new_m_w_ob': 'new_m', 'new_m_w_out': 'new_m', 'new_m_ln_g': 'new_m', 'new_m_ln_b': 'new_m', 'new_v_w_in': 'new_v', 'new_v_b_in': 'new_v', 'new_v_g_q': 'new_v', 'new_v_w_uq': 'new_v', 'new_v_g_kv': 'new_v', 'new_v_w_ukv': 'new_v', 'new_v_w_oa': 'new_v', 'new_v_sgu_ln_g': 'new_v', 'new_v_sgu_ln_b': 'new_v', 'new_v_w_s': 'new_v', 'new_v_b_s': 'new_v', 'new_v_w_ob': 'new_v', 'new_v_w_out': 'new_v', 'new_v_ln_g': 'new_v', 'new_v_ln_b': 'new_v'}


def _forward(args):
    return _fwd_reference(*[args[k] for k in FWD_PARAMS])


def _output_shape():
    out = _jax.eval_shape(lambda: _forward(_fwd_setup_inputs(0)))
    return out.shape, out.dtype

N_MICROBATCH = 1
ADAM_LR = 0.001
ADAM_B1 = 0.9
ADAM_B2 = 0.999
ADAM_EPS = 1e-08
ADAM_WD = 0.01
ADAM_STEP = 10
PER_EXAMPLE_BATCH_AXIS = {'x': 0, 'positions': 0, 'loss_target': 0}
SHARED_INPUTS = []
_WEIGHT_DTYPES = {'w_in': _jnp.float32, 'b_in': _jnp.float32, 'g_q': _jnp.float32, 'w_uq': _jnp.float32, 'g_kv': _jnp.float32, 'w_ukv': _jnp.float32, 'w_oa': _jnp.float32, 'sgu_ln_g': _jnp.float32, 'sgu_ln_b': _jnp.float32, 'w_s': _jnp.float32, 'b_s': _jnp.float32, 'w_ob': _jnp.float32, 'w_out': _jnp.float32, 'ln_g': _jnp.float32, 'ln_b': _jnp.float32}
MOMENT_SCALE = {'w_in': 7.822950e-03, 'b_in': 1.097803e-02, 'g_q': 3.827587e-03, 'w_uq': 2.636676e-03, 'g_kv': 1.006150e-02, 'w_ukv': 3.329439e-03, 'w_oa': 4.625231e-03, 'sgu_ln_g': 8.109886e-03, 'sgu_ln_b': 7.707967e-03, 'w_s': 5.584947e-03, 'b_s': 7.783704e-03, 'w_ob': 1.646882e-02, 'w_out': 1.689010e-02, 'ln_g': 1.599155e+01, 'ln_b': 3.013666e-01}


def _to_microbatches(a, axis):
    t = _jnp.moveaxis(a, axis, 0)
    t = t.reshape((N_MICROBATCH, t.shape[0] // N_MICROBATCH) + t.shape[1:])
    return _jnp.moveaxis(t, 1, axis + 1)


def setup_inputs(seed: int = 0) -> dict:
    inp = _fwd_setup_inputs(seed)
    key = _jax.random.fold_in(_jax.random.key(seed), 7919)
    shape, _ = _output_shape()
    out = dict(inp)
    out["loss_target"] = _jax.random.normal(_jax.random.fold_in(key, 0), shape, _jnp.float32)
    for i, name in enumerate(TWIN_WEIGHTS):
        w = inp[name].astype(_jnp.float32)
        if MOMENT_SCALE is None:
            s = _jnp.sqrt(_jnp.mean(_jnp.square(w)) + 1e-30)
        else:
            s = MOMENT_SCALE[name]
        km, kv = _jax.random.split(_jax.random.fold_in(key, i + 1))
        out[name] = w
        out["m_" + name] = s * _jax.random.normal(km, w.shape, _jnp.float32)
        out["v_" + name] = (s * s) * _jax.random.uniform(kv, w.shape, _jnp.float32, 0.5, 1.5)
    if N_MICROBATCH > 1:
        for name, axis in PER_EXAMPLE_BATCH_AXIS.items():
            out[name] = _to_microbatches(out[name], axis)
    return {'x': out['x'], 'positions': out['positions'], 'w_in': out['w_in'], 'b_in': out['b_in'], 'g_q': out['g_q'], 'w_uq': out['w_uq'], 'g_kv': out['g_kv'], 'w_ukv': out['w_ukv'], 'w_oa': out['w_oa'], 'sgu_ln_g': out['sgu_ln_g'], 'sgu_ln_b': out['sgu_ln_b'], 'w_s': out['w_s'], 'b_s': out['b_s'], 'w_ob': out['w_ob'], 'w_out': out['w_out'], 'ln_g': out['ln_g'], 'ln_b': out['ln_b'], 'loss_target': out['loss_target'], 'm_w_in': out['m_w_in'], 'm_b_in': out['m_b_in'], 'm_g_q': out['m_g_q'], 'm_w_uq': out['m_w_uq'], 'm_g_kv': out['m_g_kv'], 'm_w_ukv': out['m_w_ukv'], 'm_w_oa': out['m_w_oa'], 'm_sgu_ln_g': out['m_sgu_ln_g'], 'm_sgu_ln_b': out['m_sgu_ln_b'], 'm_w_s': out['m_w_s'], 'm_b_s': out['m_b_s'], 'm_w_ob': out['m_w_ob'], 'm_w_out': out['m_w_out'], 'm_ln_g': out['m_ln_g'], 'm_ln_b': out['m_ln_b'], 'v_w_in': out['v_w_in'], 'v_b_in': out['v_b_in'], 'v_g_q': out['v_g_q'], 'v_w_uq': out['v_w_uq'], 'v_g_kv': out['v_g_kv'], 'v_w_ukv': out['v_w_ukv'], 'v_w_oa': out['v_w_oa'], 'v_sgu_ln_g': out['v_sgu_ln_g'], 'v_sgu_ln_b': out['v_sgu_ln_b'], 'v_w_s': out['v_w_s'], 'v_b_s': out['v_b_s'], 'v_w_ob': out['v_w_ob'], 'v_w_out': out['v_w_out'], 'v_ln_g': out['v_ln_g'], 'v_ln_b': out['v_ln_b']}


def _loss(weights, diff, rest, loss_target):
    with _jax.named_scope("forward"):
        args = {**rest, TWIN_DIFF_INPUT: diff, **{k: w.astype(_WEIGHT_DTYPES[k]) for k, w in weights.items()}}
        y = _forward(args)
    with _jax.named_scope("loss_head"):
        err = _jnp.square(y.astype(_jnp.float32) - loss_target)
        return 0.5 * _jnp.sum(_jnp.mean(err, axis=-1)) if err.ndim else 0.5 * err


def _adamw(w, g, m, v):
    m = ADAM_B1 * m + (1.0 - ADAM_B1) * g
    v = ADAM_B2 * v + (1.0 - ADAM_B2) * _jnp.square(g)
    m_hat = m / (1.0 - ADAM_B1 ** ADAM_STEP)
    v_hat = v / (1.0 - ADAM_B2 ** ADAM_STEP)
    delta = -ADAM_LR * (m_hat / (_jnp.sqrt(v_hat) + ADAM_EPS) + ADAM_WD * w)
    return delta, m, v


def reference(x, positions, w_in, b_in, g_q, w_uq, g_kv, w_ukv, w_oa, sgu_ln_g, sgu_ln_b, w_s, b_s, w_ob, w_out, ln_g, ln_b, loss_target, m_w_in, m_b_in, m_g_q, m_w_uq, m_g_kv, m_w_ukv, m_w_oa, m_sgu_ln_g, m_sgu_ln_b, m_w_s, m_b_s, m_w_ob, m_w_out, m_ln_g, m_ln_b, v_w_in, v_b_in, v_g_q, v_w_uq, v_g_kv, v_w_ukv, v_w_oa, v_sgu_ln_g, v_sgu_ln_b, v_w_s, v_b_s, v_w_ob, v_w_out, v_ln_g, v_ln_b):
    given = dict(x=x, positions=positions, w_in=w_in, b_in=b_in, g_q=g_q, w_uq=w_uq, g_kv=g_kv, w_ukv=w_ukv, w_oa=w_oa, sgu_ln_g=sgu_ln_g, sgu_ln_b=sgu_ln_b, w_s=w_s, b_s=b_s, w_ob=w_ob, w_out=w_out, ln_g=ln_g, ln_b=ln_b, loss_target=loss_target, m_w_in=m_w_in, m_b_in=m_b_in, m_g_q=m_g_q, m_w_uq=m_w_uq, m_g_kv=m_g_kv, m_w_ukv=m_w_ukv, m_w_oa=m_w_oa, m_sgu_ln_g=m_sgu_ln_g, m_sgu_ln_b=m_sgu_ln_b, m_w_s=m_w_s, m_b_s=m_b_s, m_w_ob=m_w_ob, m_w_out=m_w_out, m_ln_g=m_ln_g, m_ln_b=m_ln_b, v_w_in=v_w_in, v_b_in=v_b_in, v_g_q=v_g_q, v_w_uq=v_w_uq, v_g_kv=v_g_kv, v_w_ukv=v_w_ukv, v_w_oa=v_w_oa, v_sgu_ln_g=v_sgu_ln_g, v_sgu_ln_b=v_sgu_ln_b, v_w_s=v_w_s, v_b_s=v_b_s, v_w_ob=v_w_ob, v_w_out=v_w_out, v_ln_g=v_ln_g, v_ln_b=v_ln_b)
    weights = {n: given[n] for n in TWIN_WEIGHTS}
    shared = {n: given[n] for n in SHARED_INPUTS}
    per_example = {n: given[n] for n in ['x', 'positions']}
    grad_fn = _jax.value_and_grad(_loss, argnums=(0, 1))

    def one_microbatch(ex, loss_target):
        ex = dict(ex)
        diff = ex.pop(TWIN_DIFF_INPUT)
        return grad_fn(weights, diff, {**shared, **ex}, loss_target)

    if N_MICROBATCH == 1:
        loss, (grad_w, grad_x) = one_microbatch(per_example, given["loss_target"])
    else:
        def body(carry, xs):
            loss_sum, grad_sum = carry
            l_k, (gw_k, gx_k) = one_microbatch(xs[0], xs[1])
            with _jax.named_scope("update"):
                return (loss_sum + l_k, _jax.tree.map(_jnp.add, grad_sum, gw_k)), gx_k

        init = (_jnp.zeros((), _jnp.float32), _jax.tree.map(_jnp.zeros_like, weights))
        (loss, grad_w), grad_x = _jax.lax.scan(body, init, (per_example, given["loss_target"]))
    with _jax.named_scope("update"):
        delta_w, new_m, new_v = {}, {}, {}
        for n in TWIN_WEIGHTS:
            delta_w[n], new_m[n], new_v[n] = _adamw(weights[n], grad_w[n], given["m_" + n], given["v_" + n])
    return (loss, grad_x, *[grad_w[n] for n in TWIN_WEIGHTS], *[delta_w[n] for n in TWIN_WEIGHTS],
            *[new_m[n] for n in TWIN_WEIGHTS], *[new_v[n] for n in TWIN_WEIGHTS])
```

```python
import functools
import math

import jax
import jax.numpy as jnp
from jax import lax
from jax.experimental import pallas as pl
from jax.experimental.pallas import tpu as pltpu

F32 = jnp.float32
BF16 = jnp.bfloat16

D_MODEL = 1024
HEADS = 8
Q_RANK = 384
KV_RANK = 128
NOPE = 64
ROPE = 32
V_DIM = 64
QK_DIM = NOPE + ROPE
HEAD_PAD = 128
MLA_W = HEADS * V_DIM
SGU_W = 512
GROUPS = 8
CHUNK = 128
IN_W = 4640
RMS_EPS = 1e-6
LN_EPS = 1e-5
ALPHA = 2.0 ** 0.25
ROPE_THETA = 10000.0
SCALE = QK_DIM ** -0.5

GATE_W = 2 * D_MODEL
MID_W = 4 * SGU_W
LAT_W = Q_RANK + KV_RANK + HEAD_PAD
PAD_W = GATE_W + MID_W + LAT_W

ROW_TILE = 256
ATT_BLOCK = 256
VMEM_LIMIT = 56 * 1024 * 1024

ADAM_LR = 0.001
ADAM_B1 = 0.9
ADAM_B2 = 0.999
ADAM_EPS = 1e-08
ADAM_WD = 0.01
ADAM_STEP = 10


def _dot(a, b):
    return jnp.dot(a, b, preferred_element_type=F32)


def _dot_nt(a, b):
    return lax.dot_general(a, b, (((1,), (1,)), ((), ())), preferred_element_type=F32)


def _dot_tn(a, b):
    return lax.dot_general(a, b, (((0,), (0,)), ((), ())), preferred_element_type=F32)


def _sigmoid(z):
    return 1.0 / (1.0 + jnp.exp(-z))


_GELU_C = math.sqrt(2.0 / math.pi)


def _gelu_and_grad(x):
    x2 = x * x
    t = jnp.tanh(_GELU_C * (x + 0.044715 * x * x2))
    g = 0.5 * x * (1.0 + t)
    dg = 0.5 * (1.0 + t) + 0.5 * x * (1.0 - t * t) * (_GELU_C * (1.0 + 3.0 * 0.044715 * x2))
    return g, dg


def _silu_and_grad(z):
    s = _sigmoid(z)
    return z * s, s * (1.0 + z * (1.0 - s))


def _rope(xb, c, sl, sh):
    return xb * c + pltpu.roll(xb, 112, 1) * sl + pltpu.roll(xb, 16, 1) * sh


def _rope_t(dy, c, sl, sh):
    return dy * c + pltpu.roll(dy * sl, 16, 1) + pltpu.roll(dy * sh, 112, 1)


def _params(sem=("arbitrary",)):
    return pltpu.CompilerParams(dimension_semantics=sem, vmem_limit_bytes=VMEM_LIMIT)


def _row_spec(tile, width):
    return pl.BlockSpec((tile, width), lambda i: (i, 0))


def _full_spec(shape):
    nd = len(shape)
    return pl.BlockSpec(shape, lambda i: (0,) * nd)


def _fwd_pre(x, w_pad, b_pad, g_q, wuq, g_kv, wk, wv, rc, rsl, rsh):
    s = x.shape[0]
    ts = ROW_TILE

    def body(x_ref, w_ref, b_ref, gq_ref, wuq_ref, gkv_ref, wk_ref, wv_ref, rc_ref, rsl_ref, rsh_ref,
             hg_ref, hm_ref, hl_ref, q_ref, k_ref, v_ref, xt_ref):
        xb = x_ref[...].astype(BF16)
        xt_ref[...] = xb.T
        h = _dot(xb, w_ref[...]) + b_ref[...]
        hg_ref[...] = h[:, :GATE_W]
        hm_ref[...] = h[:, GATE_W:GATE_W + MID_W]
        hl = h[:, GATE_W + MID_W:]
        hl_ref[...] = hl
        c, sl, sh = rc_ref[...], rsl_ref[...], rsh_ref[...]
        cq = hl[:, :Q_RANK]
        cqn = cq * lax.rsqrt(jnp.mean(cq * cq, axis=-1, keepdims=True) + RMS_EPS) * gq_ref[...]
        q = _dot(cqn.astype(BF16), wuq_ref[...])
        ckv = hl[:, Q_RANK:Q_RANK + KV_RANK]
        ckvn = (ckv * lax.rsqrt(jnp.mean(ckv * ckv, axis=-1, keepdims=True) + RMS_EPS) * gkv_ref[...]).astype(BF16)
        k = _dot(ckvn, wk_ref[...])
        v_ref[...] = _dot(ckvn, wv_ref[...]).astype(BF16)
        kpe = _rope(hl[:, Q_RANK + KV_RANK:], c, sl, sh)
        for hd in range(HEADS):
            lanes = slice(hd * HEAD_PAD, (hd + 1) * HEAD_PAD)
            q_ref[:, lanes] = _rope(q[:, lanes], c, sl, sh).astype(BF16)
            k_ref[:, lanes] = (k[:, lanes] + kpe).astype(BF16)

    qk_w = HEADS * HEAD_PAD
    return pl.pallas_call(
        body, name="fwd_pre", grid=(s // ts,),
        in_specs=[_row_spec(ts, D_MODEL), _full_spec(w_pad.shape), _full_spec(b_pad.shape), _full_spec(g_q.shape),
                  _full_spec(wuq.shape), _full_spec(g_kv.shape), _full_spec(wk.shape), _full_spec(wv.shape),
                  _row_spec(ts, HEAD_PAD), _row_spec(ts, HEAD_PAD), _row_spec(ts, HEAD_PAD)],
        out_specs=[_row_spec(ts, GATE_W), _row_spec(ts, MID_W), _row_spec(ts, LAT_W), _row_spec(ts, qk_w),
                   _row_spec(ts, qk_w), _row_spec(ts, MLA_W), pl.BlockSpec((D_MODEL, ts), lambda i: (0, i))],
        out_shape=[jax.ShapeDtypeStruct((s, GATE_W), F32), jax.ShapeDtypeStruct((s, MID_W), F32),
                   jax.ShapeDtypeStruct((s, LAT_W), F32), jax.ShapeDtypeStruct((s, qk_w), BF16),
                   jax.ShapeDtypeStruct((s, qk_w), BF16), jax.ShapeDtypeStruct((s, MLA_W), BF16),
                   jax.ShapeDtypeStruct((D_MODEL, s), BF16)],
        compiler_params=_params(),
    )(x, w_pad, b_pad, g_q, wuq, g_kv, wk, wv, rc, rsl, rsh)


def _attn_fwd(q, k, v):
    s = q.shape[0]
    tb = ATT_BLOCK
    pairs = HEADS // 2

    def body(q_ref, k_ref, v_ref, o_ref, lse_ref):
        i = pl.program_id(1)
        row = lax.broadcasted_iota(jnp.int32, (tb, tb), 0)
        col = lax.broadcasted_iota(jnp.int32, (tb, tb), 1)
        lane = lax.broadcasted_iota(jnp.int32, (tb, 2 * V_DIM), 1)
        res = []
        for hh in range(2):
            lanes = slice(hh * HEAD_PAD, (hh + 1) * HEAD_PAD)
            qh = q_ref[:, lanes]

            def step(j, carry, masked, lanes=lanes, qh=qh):
                m, l, acc = carry
                ks = k_ref[pl.ds(pl.multiple_of(j * tb, tb), tb), lanes]
                vs = v_ref[pl.ds(pl.multiple_of(j * tb, tb), tb), :]
                sc = _dot_nt(qh, ks) * SCALE
                if masked:
                    sc = jnp.where(col <= row, sc, -jnp.inf)
                m_new = jnp.maximum(m, jnp.max(sc, axis=-1, keepdims=True))
                a = jnp.exp(m - m_new)
                p = jnp.exp(sc - m_new)
                l = a * l + jnp.sum(p, axis=-1, keepdims=True)
                acc = a * acc + _dot(p.astype(BF16), vs)
                return m_new, l, acc

            init = (jnp.full((tb, 1), -jnp.inf, F32), jnp.zeros((tb, 1), F32), jnp.zeros((tb, 2 * V_DIM), F32))
            carry = lax.fori_loop(0, i, functools.partial(step, masked=False), init)
            m, l, acc = step(i, carry, True)
            res.append((acc / l, m + jnp.log(l)))
        o_ref[...] = jnp.where(lane < V_DIM, res[0][0], res[1][0])
        lse_ref[...] = jnp.where(lane < V_DIM, res[0][1], res[1][1])

    return pl.pallas_call(
        body, name="attn_fwd", grid=(pairs, s // tb),
        in_specs=[pl.BlockSpec((tb, 2 * HEAD_PAD), lambda p, i: (i, p)),
                  pl.BlockSpec((s, 2 * HEAD_PAD), lambda p, i: (0, p)),
                  pl.BlockSpec((s, 2 * V_DIM), lambda p, i: (0, p))],
        out_specs=[pl.BlockSpec((tb, 2 * V_DIM), lambda p, i: (i, p)),
                   pl.BlockSpec((tb, 2 * V_DIM), lambda p, i: (i, p))],
        out_shape=[jax.ShapeDtypeStruct((s, MLA_W), F32), jax.ShapeDtypeStruct((s, MLA_W), F32)],
        compiler_params=_params(("arbitrary", "arbitrary")),
    )(q, k, v)


def _attn_bwd(q, k, v, o, lse, do):
    s = q.shape[0]
    tb = ATT_BLOCK
    nb = s // tb
    pairs = HEADS // 2

    def body(q_ref, k_ref, v_ref, o_ref, lse_ref, do_ref, dq_ref, dk_ref, dv_ref):
        j = pl.program_id(1)
        row = lax.broadcasted_iota(jnp.int32, (tb, tb), 0)
        col = lax.broadcasted_iota(jnp.int32, (tb, tb), 1)
        lane = lax.broadcasted_iota(jnp.int32, (tb, 2 * V_DIM), 1)

        @pl.when(j == 0)
        def _():
            dq_ref[...] = jnp.zeros_like(dq_ref)

        koff = pl.multiple_of(j * tb, tb)
        vb = v_ref[pl.ds(koff, tb), :]
        dvs = []
        for hh in range(2):
            lanes = slice(hh * HEAD_PAD, (hh + 1) * HEAD_PAD)
            kb = k_ref[pl.ds(koff, tb), lanes]
            mine = (lane < V_DIM) if hh == 0 else (lane >= V_DIM)

            def step(i, carry, masked, lanes=lanes, kb=kb, mine=mine, hh=hh):
                dk_acc, dv_acc = carry
                qoff = pl.multiple_of(i * tb, tb)
                qh = q_ref[pl.ds(qoff, tb), lanes]
                dob = do_ref[pl.ds(qoff, tb), :]
                dom = jnp.where(mine, dob, jnp.zeros_like(dob))
                ob = o_ref[pl.ds(qoff, tb), :]
                delta = jnp.sum(dom.astype(F32) * ob, axis=-1, keepdims=True)
                lse_h = lse_ref[pl.ds(qoff, tb), hh * V_DIM:hh * V_DIM + 1]
                sc = _dot_nt(qh, kb) * SCALE
                if masked:
                    sc = jnp.where(col <= row, sc, -jnp.inf)
                p = jnp.exp(sc - lse_h)
                dp = _dot_nt(dom, vb)
                ds = (p * (dp - delta) * SCALE).astype(BF16)
                dv_acc = dv_acc + _dot_tn(p.astype(BF16), dob)
                dk_acc = dk_acc + _dot_tn(ds, qh)
                dq_ref[pl.ds(qoff, tb), lanes] += _dot(ds, kb)
                return dk_acc, dv_acc

            init = (jnp.zeros((tb, HEAD_PAD), F32), jnp.zeros((tb, 2 * V_DIM), F32))
            carry = step(j, init, True)
            dk_acc, dv_acc = lax.fori_loop(j + 1, nb, functools.partial(step, masked=False), carry)
            dk_ref[:, lanes] = dk_acc
            dvs.append(dv_acc)
        dv_ref[...] = jnp.where(lane < V_DIM, dvs[0], dvs[1])

    return pl.pallas_call(
        body, name="attn_bwd", grid=(pairs, nb),
        in_specs=[pl.BlockSpec((s, 2 * HEAD_PAD), lambda p, j: (0, p)),
                  pl.BlockSpec((s, 2 * HEAD_PAD), lambda p, j: (0, p)),
                  pl.BlockSpec((s, 2 * V_DIM), lambda p, j: (0, p)),
                  pl.BlockSpec((s, 2 * V_DIM), lambda p, j: (0, p)),
                  pl.BlockSpec((s, 2 * V_DIM), lambda p, j: (0, p)),
                  pl.BlockSpec((s, 2 * V_DIM), lambda p, j: (0, p))],
        out_specs=[pl.BlockSpec((s, 2 * HEAD_PAD), lambda p, j: (0, p)),
                   pl.BlockSpec((tb, 2 * HEAD_PAD), lambda p, j: (j, p)),
                   pl.BlockSpec((tb, 2 * V_DIM), lambda p, j: (j, p))],
        out_shape=[jax.ShapeDtypeStruct((s, HEADS * HEAD_PAD), F32), jax.ShapeDtypeStruct((s, HEADS * HEAD_PAD), F32),
                   jax.ShapeDtypeStruct((s, MLA_W), F32)],
        compiler_params=_params(("arbitrary", "arbitrary")),
    )(q, k, v, o, lse, do)


def _split3(a):
    hi = a.astype(BF16)
    r1 = a - hi.astype(F32)
    mid = r1.astype(BF16)
    lo = (r1 - mid.astype(F32)).astype(BF16)
    return hi, mid, lo


def _mid(x, tgt, o, hm, hg, woa, wob, wout, ln_g, ln_b, sg_g, sg_b, w_s, bsb):
    s = x.shape[0]
    ts = ROW_TILE
    nsteps = s // ts
    nch = ts // CHUNK
    npair = GROUPS // 2

    def body(x_ref, t_ref, o_ref, hm_ref, hg_ref, woa_ref, wob_ref, wout_ref, lng_ref, lnb_ref, sgg_ref, sgb_ref,
             ws_ref, bsb_ref,
             dr_ref, dhg_ref, dhm_ref, do_ref,
             dwout_ref, dwoa_ref, dwob_ref, dws_ref, dbs_ref, dlng_ref, dlnb_ref, dsgg_ref, dsgb_ref, loss_ref,
             dbacc_ref):
        i = pl.program_id(0)

        @pl.when(i == 0)
        def _():
            for r in (dwout_ref, dwoa_ref, dwob_ref, dws_ref, dlng_ref, dlnb_ref, dsgg_ref, dsgb_ref, loss_ref,
                      dbacc_ref):
                r[...] = jnp.zeros_like(r)

        lane = lax.broadcasted_iota(jnp.int32, (CHUNK, CHUNK), 1)
        left = lane < V_DIM
        tril = lax.broadcasted_iota(jnp.int32, (CHUNK, CHUNK), 0) >= lane
        ms = [jnp.where(tril, ws_ref[g], 0.0).astype(BF16) for g in range(GROUPS)]

        z_a = hm_ref[:, 0:SGU_W]
        u = hm_ref[:, SGU_W:2 * SGU_W]
        v = hm_ref[:, 2 * SGU_W:3 * SGU_W]
        z_b = hm_ref[:, 3 * SGU_W:4 * SGU_W]
        o = o_ref[...]
        sa, dsa = _silu_and_grad(z_a)
        y_a = (o * sa).astype(BF16)
        gu, dgu = _gelu_and_grad(u)
        gv, dgv = _gelu_and_grad(v)
        mu = jnp.mean(gv, axis=-1, keepdims=True)
        vc = gv - mu
        rstd_v = lax.rsqrt(jnp.mean(vc * vc, axis=-1, keepdims=True) + LN_EPS)
        vhat = vc * rstd_v
        vn = (vhat * sgg_ref[...] + sgb_ref[...]).astype(BF16)
        rows = []
        for c in range(nch):
            blocks = []
            for p in range(npair):
                blk = vn[c * CHUNK:(c + 1) * CHUNK, p * CHUNK:(p + 1) * CHUNK]
                blocks.append(jnp.where(left, _dot(ms[2 * p], blk), _dot(ms[2 * p + 1], blk)))
            rows.append(jnp.concatenate(blocks, axis=1) + bsb_ref[...])
        mixed = jnp.concatenate(rows, axis=0)
        sgu = gu * mixed
        sb, dsb = _silu_and_grad(z_b)
        y_b = (sgu * sb).astype(BF16)
        pa = _dot(y_a, woa_ref[...])
        pb = _dot(y_b, wob_ref[...])
        sga = _sigmoid(hg_ref[:, :D_MODEL])
        sgb = _sigmoid(hg_ref[:, D_MODEL:])
        m2 = (sga * pa + sgb * pb).astype(BF16)
        r = ALPHA * x_ref[...] + _dot(m2, wout_ref[...])
        rmu = jnp.mean(r, axis=-1, keepdims=True)
        rc = r - rmu
        rstd = lax.rsqrt(jnp.mean(rc * rc, axis=-1, keepdims=True) + LN_EPS)
        xhat = rc * rstd
        y = xhat * lng_ref[...] + lnb_ref[...]
        err = y - t_ref[...]
        loss_ref[...] += jnp.full(loss_ref.shape, 0.5 / D_MODEL, F32) * jnp.sum(err * err)

        dy = err * (1.0 / D_MODEL)
        dlng_ref[...] += jnp.sum(dy * xhat, axis=0, keepdims=True)
        dlnb_ref[...] += jnp.sum(dy, axis=0, keepdims=True)
        dxh = dy * lng_ref[...]
        dr = rstd * (dxh - jnp.mean(dxh, axis=-1, keepdims=True) - xhat * jnp.mean(dxh * xhat, axis=-1, keepdims=True))
        dr_ref[...] = dr
        drb = dr.astype(BF16)
        dwout_ref[...] += _dot_tn(m2, drb)
        dm2 = _dot_nt(drb, wout_ref[...])
        dhg_ref[:, :D_MODEL] = (dm2 * pa * sga * (1.0 - sga)).astype(BF16)
        dhg_ref[:, D_MODEL:] = (dm2 * pb * sgb * (1.0 - sgb)).astype(BF16)
        dpa = (dm2 * sga).astype(BF16)
        dpb = (dm2 * sgb).astype(BF16)
        dwoa_ref[...] += _dot_tn(y_a, dpa)
        dwob_ref[...] += _dot_tn(y_b, dpb)
        dy_a = _dot_nt(dpa, woa_ref[...])
        dy_b = _dot_nt(dpb, wob_ref[...])
        do_ref[...] = (dy_a * sa).astype(BF16)
        dhm_ref[:, 0:SGU_W] = (dy_a * o * dsa).astype(BF16)
        dsg = dy_b * sb
        dhm_ref[:, 3 * SGU_W:4 * SGU_W] = (dy_b * sgu * dsb).astype(BF16)
        dhm_ref[:, SGU_W:2 * SGU_W] = (dsg * mixed * dgu).astype(BF16)
        dmixed = dsg * gu
        dvn_rows = []
        dbs_sum = jnp.zeros((CHUNK, SGU_W), F32)
        for c in range(nch):
            dm_c = dmixed[c * CHUNK:(c + 1) * CHUNK, :]
            dbs_sum = dbs_sum + dm_c
            blocks = []
            for p in range(npair):
                dmb = dm_c[:, p * CHUNK:(p + 1) * CHUNK].astype(BF16)
                blk = vn[c * CHUNK:(c + 1) * CHUNK, p * CHUNK:(p + 1) * CHUNK]
                blocks.append(jnp.where(left, _dot_tn(ms[2 * p], dmb), _dot_tn(ms[2 * p + 1], dmb)))
                zero = jnp.zeros_like(dmb)
                dws_ref[2 * p] += jnp.where(tril, _dot_nt(jnp.where(left, dmb, zero), blk), 0.0)
                dws_ref[2 * p + 1] += jnp.where(tril, _dot_nt(jnp.where(left, zero, dmb), blk), 0.0)
            dvn_rows.append(jnp.concatenate(blocks, axis=1))
        dbacc_ref[...] += dbs_sum
        dvn = jnp.concatenate(dvn_rows, axis=0)
        dsgg_ref[...] += jnp.sum(dvn * vhat, axis=0, keepdims=True)
        dsgb_ref[...] += jnp.sum(dvn, axis=0, keepdims=True)
        dvh = dvn * sgg_ref[...]
        dgv_in = rstd_v * (dvh - jnp.mean(dvh, axis=-1, keepdims=True)
                           - vhat * jnp.mean(dvh * vhat, axis=-1, keepdims=True))
        dhm_ref[:, 2 * SGU_W:3 * SGU_W] = (dgv_in * dgv).astype(BF16)

        @pl.when(i == nsteps - 1)
        def _():
            grp = (lax.broadcasted_iota(jnp.int32, (SGU_W, CHUNK), 0) // V_DIM
                   == lax.broadcasted_iota(jnp.int32, (SGU_W, CHUNK), 1)).astype(BF16)
            hi, mid, lo = _split3(dbacc_ref[...])
            dbs_ref[...] = _dot(hi, grp) + _dot(mid, grp) + _dot(lo, grp)

    acc_shapes = [(D_MODEL, D_MODEL), (MLA_W, D_MODEL), (SGU_W, D_MODEL), (GROUPS, CHUNK, CHUNK), (CHUNK, CHUNK),
                  (1, D_MODEL), (1, D_MODEL), (1, SGU_W), (1, SGU_W), (1, 128)]
    return pl.pallas_call(
        body, name="mid", grid=(nsteps,),
        in_specs=[_row_spec(ts, D_MODEL), _row_spec(ts, D_MODEL), _row_spec(ts, MLA_W), _row_spec(ts, MID_W),
                  _row_spec(ts, GATE_W), _full_spec(woa.shape), _full_spec(wob.shape), _full_spec(wout.shape),
                  _full_spec(ln_g.shape), _full_spec(ln_b.shape), _full_spec(sg_g.shape), _full_spec(sg_b.shape),
                  _full_spec(w_s.shape), _full_spec(bsb.shape)],
        out_specs=[_row_spec(ts, D_MODEL), _row_spec(ts, GATE_W), _row_spec(ts, MID_W), _row_spec(ts, MLA_W)]
        + [_full_spec(sh) for sh in acc_shapes],
        out_shape=[jax.ShapeDtypeStruct((s, D_MODEL), F32), jax.ShapeDtypeStruct((s, GATE_W), BF16),
                   jax.ShapeDtypeStruct((s, MID_W), BF16), jax.ShapeDtypeStruct((s, MLA_W), BF16)]
        + [jax.ShapeDtypeStruct(sh, F32) for sh in acc_shapes],
        scratch_shapes=[pltpu.VMEM((CHUNK, SGU_W), F32)],
        compiler_params=_params(),
    )(x, tgt, o, hm, hg, woa, wob, wout, ln_g, ln_b, sg_g, sg_b, w_s, bsb)


def _lat_bwd(dq, dk, dv, hl, rc, rsl, rsh, g_q, g_kv, wuq, wk, wv):
    s = dq.shape[0]
    ts = ROW_TILE
    qk_w = HEADS * HEAD_PAD

    def body(dq_ref, dk_ref, dv_ref, hl_ref, rc_ref, rsl_ref, rsh_ref, gq_ref, gkv_ref, wuq_ref, wk_ref, wv_ref,
             dhl_ref, dwuq_ref, dwk_ref, dwv_ref, dgq_ref, dgkv_ref):
        i = pl.program_id(0)

        @pl.when(i == 0)
        def _():
            for r in (dwuq_ref, dwk_ref, dwv_ref, dgq_ref, dgkv_ref):
                r[...] = jnp.zeros_like(r)

        c, sl, sh = rc_ref[...], rsl_ref[...], rsh_ref[...]
        lane = lax.broadcasted_iota(jnp.int32, (ts, HEAD_PAD), 1)
        pe = (lane >= NOPE) & (lane < QK_DIM)
        dkpe = jnp.zeros((ts, HEAD_PAD), F32)
        dqu = []
        for hd in range(HEADS):
            lanes = slice(hd * HEAD_PAD, (hd + 1) * HEAD_PAD)
            dqu.append(_rope_t(dq_ref[:, lanes], c, sl, sh).astype(BF16))
            dkpe = dkpe + dk_ref[:, lanes]
        dqu = jnp.concatenate(dqu, axis=1)
        dkpe = _rope_t(jnp.where(pe, dkpe, 0.0), c, sl, sh)

        cq = hl_ref[:, :Q_RANK]
        rq = lax.rsqrt(jnp.mean(cq * cq, axis=-1, keepdims=True) + RMS_EPS)
        cqh = cq * rq
        cqn = (cqh * gq_ref[...]).astype(BF16)
        dwuq_ref[...] += _dot_tn(cqn, dqu)
        dcqn = _dot_nt(dqu, wuq_ref[...])
        dgq_ref[...] += jnp.sum(dcqn * cqh, axis=0, keepdims=True)
        dch = dcqn * gq_ref[...]
        dhl_ref[:, :Q_RANK] = (rq * (dch - cqh * jnp.mean(dch * cqh, axis=-1, keepdims=True))).astype(BF16)

        ckv = hl_ref[:, Q_RANK:Q_RANK + KV_RANK]
        rk = lax.rsqrt(jnp.mean(ckv * ckv, axis=-1, keepdims=True) + RMS_EPS)
        ckh = ckv * rk
        ckn = (ckh * gkv_ref[...]).astype(BF16)
        dkb = dk_ref[...].astype(BF16)
        dvb = dv_ref[...].astype(BF16)
        dwk_ref[...] += _dot_tn(ckn, dkb)
        dwv_ref[...] += _dot_tn(ckn, dvb)
        dckn = _dot_nt(dkb, wk_ref[...]) + _dot_nt(dvb, wv_ref[...])
        dgkv_ref[...] += jnp.sum(dckn * ckh, axis=0, keepdims=True)
        dkh = dckn * gkv_ref[...]
        dhl_ref[:, Q_RANK:Q_RANK + KV_RANK] = (rk * (dkh - ckh * jnp.mean(dkh * ckh, axis=-1, keepdims=True))).astype(BF16)
        dhl_ref[:, Q_RANK + KV_RANK:] = dkpe.astype(BF16)

    acc_shapes = [wuq.shape, wk.shape, wv.shape, g_q.shape, g_kv.shape]
    return pl.pallas_call(
        body, name="lat_bwd", grid=(s // ts,),
        in_specs=[_row_spec(ts, qk_w), _row_spec(ts, qk_w), _row_spec(ts, MLA_W), _row_spec(ts, LAT_W),
                  _row_spec(ts, HEAD_PAD), _row_spec(ts, HEAD_PAD), _row_spec(ts, HEAD_PAD),
                  _full_spec(g_q.shape), _full_spec(g_kv.shape), _full_spec(wuq.shape), _full_spec(wk.shape),
                  _full_spec(wv.shape)],
        out_specs=[_row_spec(ts, LAT_W)] + [_full_spec(sh) for sh in acc_shapes],
        out_shape=[jax.ShapeDtypeStruct((s, LAT_W), BF16)] + [jax.ShapeDtypeStruct(sh, F32) for sh in acc_shapes],
        compiler_params=_params(),
    )(dq, dk, dv, hl, rc, rsl, rsh, g_q, g_kv, wuq, wk, wv)


def _dx(dr, dhg, dhm, dhl, w_pad):
    s = dr.shape[0]
    ts = ROW_TILE

    def body(dr_ref, dhg_ref, dhm_ref, dhl_ref, w_ref, dx_ref):
        dx_ref[...] = (ALPHA * dr_ref[...]
                       + _dot_nt(dhg_ref[...], w_ref[:, :GATE_W])
                       + _dot_nt(dhm_ref[...], w_ref[:, GATE_W:GATE_W + MID_W])
                       + _dot_nt(dhl_ref[...], w_ref[:, GATE_W + MID_W:]))

    return pl.pallas_call(
        body, name="dx", grid=(s // ts,),
        in_specs=[_row_spec(ts, D_MODEL), _row_spec(ts, GATE_W), _row_spec(ts, MID_W), _row_spec(ts, LAT_W),
                  _full_spec(w_pad.shape)],
        out_specs=_row_spec(ts, D_MODEL),
        out_shape=jax.ShapeDtypeStruct((s, D_MODEL), F32),
        compiler_params=_params(),
    )(dr, dhg, dhm, dhl, w_pad)


def _dw_in(xt, dh, tn, name):
    s, width = dh.shape

    def body(xt_ref, dh_ref, dw_ref, db_ref):
        dhb = dh_ref[...]
        dw_ref[...] = _dot(xt_ref[...], dhb)
        db_ref[...] = jnp.sum(dhb.astype(F32), axis=0, keepdims=True)

    return pl.pallas_call(
        body, name=name, grid=(width // tn,),
        in_specs=[_full_spec(xt.shape), pl.BlockSpec((s, tn), lambda n: (0, n))],
        out_specs=[pl.BlockSpec((D_MODEL, tn), lambda n: (0, n)), pl.BlockSpec((1, tn), lambda n: (0, n))],
        out_shape=[jax.ShapeDtypeStruct((D_MODEL, width), F32), jax.ShapeDtypeStruct((1, width), F32)],
        compiler_params=_params(),
    )(xt, dh)


_C_Q, _C_KV, _K_PE, _Z_A, _U, _V, _Z_B, _G_A, _G_B = (
    (0, 384), (384, 512), (512, 544), (544, 1056), (1056, 1568), (1568, 2080), (2080, 2592), (2592, 3616),
    (3616, 4640))


def _pad_in_cols(w):
    cut = lambda ab: w[:, ab[0]:ab[1]]
    z = lambda n: jnp.zeros((w.shape[0], n), w.dtype)
    return jnp.concatenate([cut(_G_A), cut(_G_B), cut(_Z_A), cut(_U), cut(_V), cut(_Z_B), cut(_C_Q), cut(_C_KV),
                            z(NOPE), cut(_K_PE), z(HEAD_PAD - QK_DIM)], axis=1)


def _unpad_in_cols(wp):
    g, m, l = wp[:, :GATE_W], wp[:, GATE_W:GATE_W + MID_W], wp[:, GATE_W + MID_W:]
    kpe = l[:, Q_RANK + KV_RANK + NOPE:Q_RANK + KV_RANK + QK_DIM]
    return jnp.concatenate([l[:, :Q_RANK + KV_RANK], kpe, m, g], axis=1)


def _rope_tables(positions):
    half = ROPE // 2
    inv_freq = ROPE_THETA ** (-jnp.arange(0, ROPE, 2, dtype=F32) / ROPE)
    ang = positions.astype(F32)[:, None] * inv_freq
    cos, sin = jnp.cos(ang), jnp.sin(ang)
    n = positions.shape[0]
    one, zero = jnp.ones((n, NOPE), F32), jnp.zeros((n, half), F32)
    tail1, tail0 = jnp.ones((n, HEAD_PAD - QK_DIM), F32), jnp.zeros((n, HEAD_PAD - QK_DIM), F32)
    z64 = jnp.zeros((n, NOPE), F32)
    rc = jnp.concatenate([one, cos, cos, tail1], axis=1)
    rsl = jnp.concatenate([z64, -sin, zero, tail0], axis=1)
    rsh = jnp.concatenate([z64, zero, sin, tail0], axis=1)
    return rc, rsl, rsh


def _local_step(x, positions, tgt, w_in, b_in, g_q, w_uq, g_kv, w_ukv, w_oa, sg_g, sg_b, w_s, b_s, w_ob, w_out,
                ln_g, ln_b):
    rc, rsl, rsh = _rope_tables(positions)
    w_pad = _pad_in_cols(w_in)
    b_pad = _pad_in_cols(b_in[None, :])
    wuq = jnp.pad(w_uq, ((0, 0), (0, 0), (0, HEAD_PAD - QK_DIM))).reshape(Q_RANK, HEADS * HEAD_PAD).astype(BF16)
    wk = jnp.pad(w_ukv[:, :, :NOPE], ((0, 0), (0, 0), (0, HEAD_PAD - NOPE))).reshape(KV_RANK, HEADS * HEAD_PAD).astype(BF16)
    wv = w_ukv[:, :, NOPE:].reshape(KV_RANK, MLA_W).astype(BF16)
    bsb = jnp.repeat(b_s.T, V_DIM, axis=1)
    gq2, gkv2 = g_q[None, :], g_kv[None, :]

    hg, hm, hl, q, k, v, xt = _fwd_pre(x, w_pad, b_pad, gq2, wuq, gkv2, wk, wv, rc, rsl, rsh)
    o, lse = _attn_fwd(q, k, v)
    (dr, dhg, dhm, do, dwout, dwoa, dwob, dws, dbs, dlng, dlnb, dsgg, dsgb, loss) = _mid(
        x, tgt, o, hm, hg, w_oa, w_ob, w_out, ln_g[None, :], ln_b[None, :], sg_g[None, :], sg_b[None, :], w_s, bsb)
    dq, dk, dv = _attn_bwd(q, k, v, o, lse, do)
    dhl, dwuq, dwk, dwv, dgq, dgkv = _lat_bwd(dq, dk, dv, hl, rc, rsl, rsh, gq2, gkv2, wuq, wk, wv)
    dx = _dx(dr, dhg, dhm, dhl, w_pad)
    dwg, dbg = _dw_in(xt, dhg, 512, "dw_in_gates")
    dwm, dbm = _dw_in(xt, dhm, 512, "dw_in_mid")
    dwl, dbl = _dw_in(xt, dhl, LAT_W, "dw_in_lat")
    grads = {
        "w_in": _unpad_in_cols(jnp.concatenate([dwg, dwm, dwl], axis=1)),
        "b_in": _unpad_in_cols(jnp.concatenate([dbg, dbm, dbl], axis=1))[0],
        "g_q": dgq[0],
        "w_uq": dwuq.reshape(Q_RANK, HEADS, HEAD_PAD)[:, :, :QK_DIM],
        "g_kv": dgkv[0],
        "w_ukv": jnp.concatenate([dwk.reshape(KV_RANK, HEADS, HEAD_PAD)[:, :, :NOPE],
                                  dwv.reshape(KV_RANK, HEADS, V_DIM)], axis=2),
        "w_oa": dwoa, "sgu_ln_g": dsgg[0], "sgu_ln_b": dsgb[0], "w_s": dws, "b_s": dbs[:, :GROUPS].T,
        "w_ob": dwob, "w_out": dwout, "ln_g": dlng[0], "ln_b": dlnb[0],
    }
    return loss[0, 0], dx, grads


MESH = pl.DeviceIdType.MESH
N_CHIPS = 4
HBM_SPEC = pl.BlockSpec(memory_space=pl.ANY)
VMEM_SPEC = pl.BlockSpec(memory_space=pltpu.VMEM)

SHARD_ROWS = 1744
REP_ROWS = 80
SLAB_ROWS = SHARD_ROWS + REP_ROWS
HALF_ROWS = SLAB_ROWS // 2
REP_OFF = SHARD_ROWS - HALF_ROWS
ADD_TILE = HALF_ROWS // 3


def _place():
    x, y, c = lax.axis_index("x"), lax.axis_index("y"), lax.axis_index("c")
    others = [(1 - x, y), (x, 1 - y), (1 - x, 1 - y)]
    return x, y, c, others


def _gather_weights(shards):
    n = len(shards)

    def body(*refs):
        ins, outs, bufs = refs[:n], refs[n:2 * n], refs[2 * n:3 * n]
        send_sems, recv_sems, local_sems = refs[3 * n:]
        x, y, c, others = _place()
        me = 2 * x + y
        sibling = (x, y, 1 - c)
        for src, buf in zip(ins, bufs):
            buf[...] = src[...].astype(BF16)
        own = [pltpu.make_async_copy(bufs[w], outs[w].at[me], local_sems.at[w]) for w in range(n)]
        for cp in own:
            cp.start()

        def part(w, chip, half):
            hr = shards[w].shape[0] // 2
            return outs[w].at[chip, pl.ds(half * hr, hr), :]

        def sent(w, j):
            hr = shards[w].shape[0] // 2
            return pltpu.make_async_remote_copy(
                src_ref=bufs[w].at[pl.ds(c * hr, hr), :], dst_ref=part(w, me, c),
                send_sem=send_sems.at[w * 3 + j], recv_sem=recv_sems.at[w * 3 + j],
                device_id=(*others[j], c), device_id_type=MESH)

        def landed(w, j):
            px, py = others[j]
            return pltpu.make_async_remote_copy(
                src_ref=part(w, 2 * px + py, c), dst_ref=part(w, 2 * px + py, c),
                send_sem=send_sems.at[w * 3 + j], recv_sem=recv_sems.at[w * 3 + j],
                device_id=(px, py, c), device_id_type=MESH)

        def passed(w, j, half):
            px, py = others[j]
            k = n * 3 + w * 3 + j
            return pltpu.make_async_remote_copy(
                src_ref=part(w, 2 * px + py, half), dst_ref=part(w, 2 * px + py, half),
                send_sem=send_sems.at[k], recv_sem=recv_sems.at[k], device_id=sibling, device_id_type=MESH)

        first = [sent(w, j) for w in range(n) for j in range(3)]
        for cp in first:
            cp.start()
        fwd = []
        for w in range(n):
            for j in range(3):
                landed(w, j).wait_recv()
                cp = passed(w, j, c)
                cp.start()
                fwd.append(cp)
        for w in range(n):
            for j in range(3):
                passed(w, j, 1 - c).wait_recv()
        for cp in first + fwd:
            cp.wait_send()
        for cp in own:
            cp.wait()

    return pl.pallas_call(
        body, name="gather_weights",
        in_specs=[VMEM_SPEC] * n, out_specs=[HBM_SPEC] * n,
        out_shape=[jax.ShapeDtypeStruct((N_CHIPS,) + s.shape, BF16) for s in shards],
        scratch_shapes=[pltpu.VMEM(s.shape, BF16) for s in shards]
        + [pltpu.SemaphoreType.DMA((6 * n,)), pltpu.SemaphoreType.DMA((6 * n,)), pltpu.SemaphoreType.DMA((n,))],
        compiler_params=pltpu.CompilerParams(vmem_limit_bytes=VMEM_LIMIT),
    )(*shards)


def _swap_halves(p):
    def body(p_ref, r_ref, send_sem, recv_sem):
        x, y, c, _ = _place()
        cp = pltpu.make_async_remote_copy(
            src_ref=p_ref.at[:, pl.ds((1 - c) * HALF_ROWS, HALF_ROWS), :], dst_ref=r_ref,
            send_sem=send_sem, recv_sem=recv_sem, device_id=(x, y, 1 - c), device_id_type=MESH)
        cp.start()
        cp.wait()

    return pl.pallas_call(
        body, name="swap_halves", in_specs=[HBM_SPEC], out_specs=HBM_SPEC,
        out_shape=jax.ShapeDtypeStruct((N_CHIPS, HALF_ROWS, D_MODEL), BF16),
        scratch_shapes=[pltpu.SemaphoreType.DMA, pltpu.SemaphoreType.DMA],
    )(p)


def _pair_add(p, r, c):
    def body(c_ref, p_ref, r_ref, q_ref):
        q_ref[...] = (p_ref[...].astype(F32) + r_ref[...].astype(F32)).astype(BF16)

    nt = HALF_ROWS // ADD_TILE
    return pl.pallas_call(
        body, name="pair_add",
        grid_spec=pltpu.PrefetchScalarGridSpec(
            num_scalar_prefetch=1, grid=(N_CHIPS, nt),
            in_specs=[pl.BlockSpec((None, ADD_TILE, D_MODEL), lambda k, i, c_ref: (k, c_ref[0] * nt + i, 0)),
                      pl.BlockSpec((None, ADD_TILE, D_MODEL), lambda k, i, c_ref: (k, i, 0))],
            out_specs=pl.BlockSpec((None, ADD_TILE, D_MODEL), lambda k, i, c_ref: (k, i, 0))),
        out_shape=jax.ShapeDtypeStruct((N_CHIPS, HALF_ROWS, D_MODEL), BF16),
    )(c, p, r)


def _scatter_chips(q):
    def body(q_ref, r_ref, send_sems, recv_sems, local_sem):
        x, y, c, others = _place()
        me = 2 * x + y
        own = pltpu.make_async_copy(q_ref.at[me], r_ref.at[me], local_sem)
        own.start()
        cps = []
        for j, (px, py) in enumerate(others):
            cp = pltpu.make_async_remote_copy(
                src_ref=q_ref.at[2 * px + py], dst_ref=r_ref.at[me],
                send_sem=send_sems.at[j], recv_sem=recv_sems.at[j], device_id=(px, py, c), device_id_type=MESH)
            cp.start()
            cps.append(cp)
        for j, (px, py) in enumerate(others):
            pltpu.make_async_remote_copy(
                src_ref=q_ref.at[me], dst_ref=r_ref.at[2 * px + py],
                send_sem=send_sems.at[j], recv_sem=recv_sems.at[j], device_id=(px, py, c),
                device_id_type=MESH).wait_recv()
        for cp in cps:
            cp.wait_send()
        own.wait()

    return pl.pallas_call(
        body, name="scatter_chips", in_specs=[HBM_SPEC], out_specs=HBM_SPEC,
        out_shape=jax.ShapeDtypeStruct(q.shape, BF16),
        scratch_shapes=[pltpu.SemaphoreType.DMA((3,)), pltpu.SemaphoreType.DMA((3,)), pltpu.SemaphoreType.DMA],
    )(q)


def _sum_chips(r):
    def body(r_ref, g_ref):
        acc = r_ref[0].astype(F32)
        for k in range(1, N_CHIPS):
            acc = acc + r_ref[k].astype(F32)
        g_ref[...] = acc

    return pl.pallas_call(
        body, name="sum_chips", grid=(HALF_ROWS // ADD_TILE,),
        in_specs=[pl.BlockSpec((N_CHIPS, ADD_TILE, D_MODEL), lambda i: (0, i, 0))],
        out_specs=pl.BlockSpec((ADD_TILE, D_MODEL), lambda i: (i, 0)),
        out_shape=jax.ShapeDtypeStruct((HALF_ROWS, D_MODEL), F32),
    )(r)


def _share_reduced(gh):
    def body(gh_ref, g_ref, rep_ref, send_sems, recv_sems, local_sems):
        x, y, c, others = _place()
        me = 2 * x + y
        sibling = (x, y, 1 - c)
        own = pltpu.make_async_copy(gh_ref, g_ref.at[pl.ds(c * HALF_ROWS, HALF_ROWS), :], local_sems.at[0])
        own.start()
        half = pltpu.make_async_remote_copy(
            src_ref=gh_ref, dst_ref=g_ref.at[pl.ds(c * HALF_ROWS, HALF_ROWS), :],
            send_sem=send_sems.at[0], recv_sem=recv_sems.at[0], device_id=sibling, device_id_type=MESH)
        half.start()
        quarter = gh_ref.at[pl.ds(REP_OFF, REP_ROWS), :]

        def rep_copy(k, to):
            return pltpu.make_async_remote_copy(
                src_ref=quarter, dst_ref=rep_ref.at[me], send_sem=send_sems.at[k], recv_sem=recv_sems.at[k],
                device_id=to, device_id_type=MESH)

        @pl.when(c == 1)
        def _():
            mine = pltpu.make_async_copy(quarter, rep_ref.at[me], local_sems.at[1])
            mine.start()
            cps = [rep_copy(1, sibling)]
            for j, (px, py) in enumerate(others):
                cps.append(rep_copy(2 + j, (px, py, 1)))
                cps.append(rep_copy(5 + j, (px, py, 0)))
            for cp in cps:
                cp.start()
            for cp in cps:
                cp.wait_send()
            mine.wait()

        def rep_wait(k, chip):
            pltpu.make_async_remote_copy(
                src_ref=quarter, dst_ref=rep_ref.at[chip], send_sem=send_sems.at[k], recv_sem=recv_sems.at[k],
                device_id=sibling, device_id_type=MESH).wait_recv()

        @pl.when(c == 0)
        def _():
            rep_wait(1, me)
            for j, (px, py) in enumerate(others):
                rep_wait(5 + j, 2 * px + py)

        @pl.when(c == 1)
        def _():
            for j, (px, py) in enumerate(others):
                rep_wait(2 + j, 2 * px + py)

        half.wait()
        own.wait()

    return pl.pallas_call(
        body, name="share_reduced", in_specs=[HBM_SPEC], out_specs=[HBM_SPEC, HBM_SPEC],
        out_shape=[jax.ShapeDtypeStruct((SLAB_ROWS, D_MODEL), F32),
                   jax.ShapeDtypeStruct((N_CHIPS, REP_ROWS, D_MODEL), F32)],
        scratch_shapes=[pltpu.SemaphoreType.DMA((8,)), pltpu.SemaphoreType.DMA((8,)), pltpu.SemaphoreType.DMA((2,))],
    )(gh)


def _adamw(w, g, m, v):
    m2 = ADAM_B1 * m + (1.0 - ADAM_B1) * g
    v2 = ADAM_B2 * v + (1.0 - ADAM_B2) * (g * g)
    m_hat = m2 / (1.0 - ADAM_B1 ** ADAM_STEP)
    v_hat = v2 / (1.0 - ADAM_B2 ** ADAM_STEP)
    return -ADAM_LR * (m_hat / (jnp.sqrt(v_hat) + ADAM_EPS) + ADAM_WD * w), m2, v2


def _update_tiled(w, g, m, v, tile):
    def body(w_ref, g_ref, m_ref, v_ref, d_ref, m2_ref, v2_ref):
        d_ref[...], m2_ref[...], v2_ref[...] = _adamw(w_ref[...], g_ref[...], m_ref[...], v_ref[...])

    spec = pl.BlockSpec((tile, w.shape[1]), lambda i: (i, 0))
    return pl.pallas_call(
        body, name="update_w_in", grid=(w.shape[0] // tile,), in_specs=[spec] * 4, out_specs=[spec] * 3,
        out_shape=[jax.ShapeDtypeStruct(w.shape, F32)] * 3,
        compiler_params=_params(("parallel",)),
    )(w, g, m, v)


def _update_small(ws, gs, ms, vs):
    n = len(ws)

    def body(*refs):
        for k in range(n):
            w_ref, g_ref, m_ref, v_ref = refs[k], refs[n + k], refs[2 * n + k], refs[3 * n + k]
            d, m2, v2 = _adamw(w_ref[...], g_ref[...], m_ref[...], v_ref[...])
            refs[4 * n + k][...] = d
            refs[5 * n + k][...] = m2
            refs[6 * n + k][...] = v2

    shapes = [jax.ShapeDtypeStruct(w.shape, F32) for w in ws]
    outs = pl.pallas_call(
        body, name="update_small", in_specs=[VMEM_SPEC] * (4 * n), out_specs=[VMEM_SPEC] * (3 * n),
        out_shape=shapes * 3,
        compiler_params=pltpu.CompilerParams(vmem_limit_bytes=VMEM_LIMIT),
    )(*ws, *gs, *ms, *vs)
    return outs[:n], outs[n:2 * n], outs[2 * n:]


SHARDED = ("w_in", "w_uq", "w_oa", "w_ob", "w_out")
REPLICATED = ("b_in", "g_q", "g_kv", "w_ukv", "sgu_ln_g", "sgu_ln_b", "w_s", "b_s", "ln_g", "ln_b")
ORDER = ("w_in", "b_in", "g_q", "w_uq", "g_kv", "w_ukv", "w_oa", "sgu_ln_g", "sgu_ln_b", "w_s", "b_s", "w_ob", "w_out",
         "ln_g", "ln_b")


def _shard_of(name, g, k):
    if name == "w_in":
        return g[:, k * (IN_W // 4):(k + 1) * (IN_W // 4)]
    if name == "w_uq":
        return g[k * (Q_RANK // 4):(k + 1) * (Q_RANK // 4)]
    if name in ("w_oa", "w_ob"):
        return g[:, k * (D_MODEL // 4):(k + 1) * (D_MODEL // 4)]
    return g[k * (D_MODEL // 4):(k + 1) * (D_MODEL // 4)]


def _rows(a):
    return a.reshape(-1, D_MODEL)


def kernel(x, positions, w_in, b_in, g_q, w_uq, g_kv, w_ukv, w_oa, sgu_ln_g, sgu_ln_b, w_s, b_s, w_ob, w_out, ln_g, ln_b, loss_target, m_w_in, m_b_in, m_g_q, m_w_uq, m_g_kv, m_w_ukv, m_w_oa, m_sgu_ln_g, m_sgu_ln_b, m_w_s, m_b_s, m_w_ob, m_w_out, m_ln_g, m_ln_b, v_w_in, v_b_in, v_g_q, v_w_uq, v_g_kv, v_w_ukv, v_w_oa, v_sgu_ln_g, v_sgu_ln_b, v_w_s, v_b_s, v_w_ob, v_w_out, v_ln_g, v_ln_b):
    w = dict(w_in=w_in, b_in=b_in, g_q=g_q, w_uq=w_uq, g_kv=g_kv, w_ukv=w_ukv, w_oa=w_oa, sgu_ln_g=sgu_ln_g,
             sgu_ln_b=sgu_ln_b, w_s=w_s, b_s=b_s, w_ob=w_ob, w_out=w_out, ln_g=ln_g, ln_b=ln_b)
    m = dict(w_in=m_w_in, b_in=m_b_in, g_q=m_g_q, w_uq=m_w_uq, g_kv=m_g_kv, w_ukv=m_w_ukv, w_oa=m_w_oa,
             sgu_ln_g=m_sgu_ln_g, sgu_ln_b=m_sgu_ln_b, w_s=m_w_s, b_s=m_b_s, w_ob=m_w_ob, w_out=m_w_out, ln_g=m_ln_g,
             ln_b=m_ln_b)
    v = dict(w_in=v_w_in, b_in=v_b_in, g_q=v_g_q, w_uq=v_w_uq, g_kv=v_g_kv, w_ukv=v_w_ukv, w_oa=v_w_oa,
             sgu_ln_g=v_sgu_ln_g, sgu_ln_b=v_sgu_ln_b, w_s=v_w_s, b_s=v_b_s, w_ob=v_w_ob, w_out=v_w_out, ln_g=v_ln_g,
             ln_b=v_ln_b)
    w, m, v = ({n: a[0] for n, a in d.items()} for d in (w, m, v))
    c = lax.axis_index("c")

    g_in, g_uq, g_oa, g_ob, g_out = _gather_weights(
        [w["w_in"], w["w_uq"].reshape(Q_RANK // 4, HEADS * QK_DIM), w["w_oa"], w["w_ob"], w["w_out"]])
    full_in = jnp.transpose(g_in, (1, 0, 2)).reshape(D_MODEL, IN_W)
    full_uq = g_uq.reshape(Q_RANK, HEADS, QK_DIM)
    full_oa = jnp.transpose(g_oa, (1, 0, 2)).reshape(MLA_W, D_MODEL)
    full_ob = jnp.transpose(g_ob, (1, 0, 2)).reshape(SGU_W, D_MODEL)
    full_out = g_out.reshape(D_MODEL, D_MODEL)

    loss, dx, grads = _local_step(
        x[0], positions[0], loss_target[0], full_in, w["b_in"], w["g_q"], full_uq, w["g_kv"], w["w_ukv"], full_oa,
        w["sgu_ln_g"], w["sgu_ln_b"], w["w_s"], w["b_s"], full_ob, full_out, w["ln_g"], w["ln_b"])
    loss = lax.psum(loss, ("x", "y", "c"))

    rep = jnp.concatenate([grads[n].reshape(-1) for n in REPLICATED])
    rep = jnp.pad(rep, (0, N_CHIPS * REP_ROWS * D_MODEL - rep.shape[0])).reshape(N_CHIPS, REP_ROWS, D_MODEL)
    slabs = [jnp.concatenate([_rows(_shard_of(n, grads[n], k)) for n in SHARDED] + [rep[k]], axis=0)
             for k in range(N_CHIPS)]
    part = jnp.stack(slabs).astype(BF16)
    pair = _pair_add(part, _swap_halves(part), c.reshape(1).astype(jnp.int32))
    g_half = _sum_chips(_scatter_chips(pair))
    g_slab, g_rep = _share_reduced(g_half)

    red = {}
    off = 0
    for n in SHARDED:
        rows = w[n].size // D_MODEL
        red[n] = g_slab[off:off + rows].reshape(w[n].shape)
        off += rows
    flat = g_rep.reshape(-1)
    off = 0
    for n in REPLICATED:
        red[n] = flat[off:off + w[n].size].reshape(w[n].shape)
        off += w[n].size

    d_in, m_in, v_in = _update_tiled(w["w_in"], red["w_in"], m["w_in"], v["w_in"], 128)
    small = [n for n in ORDER if n != "w_in"]
    as2d = lambda a: a.reshape(-1, a.shape[-1])
    ds, ms, vs = _update_small([as2d(w[n]) for n in small], [as2d(red[n]) for n in small],
                               [as2d(m[n]) for n in small], [as2d(v[n]) for n in small])
    delta, new_m, new_v = {"w_in": d_in}, {"w_in": m_in}, {"w_in": v_in}
    for i, n in enumerate(small):
        delta[n], new_m[n], new_v[n] = (a[i].reshape(w[n].shape) for a in (ds, ms, vs))

    lead = lambda a: a[None]
    return (loss, dx[None], *[lead(red[n]) for n in ORDER], *[lead(delta[n]) for n in ORDER],
            *[lead(new_m[n]) for n in ORDER], *[lead(new_v[n]) for n in ORDER])
```

```python
import functools
import math

import jax
import jax.numpy as jnp
from jax import lax
from jax.experimental import pallas as pl
from jax.experimental.pallas import tpu as pltpu

F32 = jnp.float32
BF16 = jnp.bfloat16

D_MODEL = 1024
HEADS = 8
Q_RANK = 384
KV_RANK = 128
NOPE = 64
ROPE = 32
V_DIM = 64
QK_DIM = NOPE + ROPE
HEAD_PAD = 128
MLA_W = HEADS * V_DIM
SGU_W = 512
GROUPS = 8
CHUNK = 128
IN_W = 4640
RMS_EPS = 1e-6
LN_EPS = 1e-5
ALPHA = 2.0 ** 0.25
ROPE_THETA = 10000.0
SCALE = QK_DIM ** -0.5

GATE_W = 2 * D_MODEL
MID_W = 4 * SGU_W
LAT_W = Q_RANK + KV_RANK + HEAD_PAD
PAD_W = GATE_W + MID_W + LAT_W

ROW_TILE = 256
ATT_TQ = 256
ATT_TK = 256
ATT_BWD_TQ = 256
ATT_BWD_TK = 256
LOG2E = 1.4426950408889634
LN2 = 0.6931471805599453
Q_SCALE = SCALE * LOG2E
VMEM_LIMIT = 56 * 1024 * 1024

ADAM_LR = 0.001
ADAM_B1 = 0.9
ADAM_B2 = 0.999
ADAM_EPS = 1e-08
ADAM_WD = 0.01
ADAM_STEP = 10


def _dot(a, b):
    return jnp.dot(a, b, preferred_element_type=F32)


def _dot_nt(a, b):
    return lax.dot_general(a, b, (((1,), (1,)), ((), ())), preferred_element_type=F32)


def _dot_tn(a, b):
    return lax.dot_general(a, b, (((0,), (0,)), ((), ())), preferred_element_type=F32)


def _sigmoid(z):
    return 1.0 / (1.0 + jnp.exp(-z))


_GELU_C = math.sqrt(2.0 / math.pi)


def _gelu_and_grad(x):
    x2 = x * x
    t = jnp.tanh(_GELU_C * (x + 0.044715 * x * x2))
    g = 0.5 * x * (1.0 + t)
    dg = 0.5 * (1.0 + t) + 0.5 * x * (1.0 - t * t) * (_GELU_C * (1.0 + 3.0 * 0.044715 * x2))
    return g, dg


def _silu_and_grad(z):
    s = _sigmoid(z)
    return z * s, s * (1.0 + z * (1.0 - s))


def _rope(xb, c, sl, sh):
    return xb * c + pltpu.roll(xb, 112, 1) * sl + pltpu.roll(xb, 16, 1) * sh


def _rope_t(dy, c, sl, sh):
    return dy * c + pltpu.roll(dy * sl, 16, 1) + pltpu.roll(dy * sh, 112, 1)


def _params(sem=("arbitrary",)):
    return pltpu.CompilerParams(dimension_semantics=sem, vmem_limit_bytes=VMEM_LIMIT)


def _row_spec(tile, width):
    return pl.BlockSpec((tile, width), lambda i: (i, 0))


def _full_spec(shape):
    nd = len(shape)
    return pl.BlockSpec(shape, lambda i: (0,) * nd)


def _fwd_pre(x, w_pad, b_pad, g_q, wuq, g_kv, wk, wv, rc, rsl, rsh):
    s = x.shape[0]
    ts = ROW_TILE

    def body(x_ref, w_ref, b_ref, gq_ref, wuq_ref, gkv_ref, wk_ref, wv_ref, rc_ref, rsl_ref, rsh_ref,
             hg_ref, hm_ref, hl_ref, q_ref, k_ref, v_ref, xt_ref, qt_ref, kt_ref, vt_ref):
        xb = x_ref[...].astype(BF16)
        xt_ref[...] = xb.T
        h = _dot(xb, w_ref[...]) + b_ref[...]
        hg_ref[...] = h[:, :GATE_W]
        hm_ref[...] = h[:, GATE_W:GATE_W + MID_W]
        hl = h[:, GATE_W + MID_W:]
        hl_ref[...] = hl
        c, sl, sh = rc_ref[...], rsl_ref[...], rsh_ref[...]
        cq = hl[:, :Q_RANK]
        cqn = cq * lax.rsqrt(jnp.mean(cq * cq, axis=-1, keepdims=True) + RMS_EPS) * gq_ref[...]
        q = _dot(cqn.astype(BF16), wuq_ref[...])
        ckv = hl[:, Q_RANK:Q_RANK + KV_RANK]
        ckvn = (ckv * lax.rsqrt(jnp.mean(ckv * ckv, axis=-1, keepdims=True) + RMS_EPS) * gkv_ref[...]).astype(BF16)
        k = _dot(ckvn, wk_ref[...])
        vb = _dot(ckvn, wv_ref[...]).astype(BF16)
        v_ref[...] = vb
        vt_ref[...] = vb.T
        kpe = _rope(hl[:, Q_RANK + KV_RANK:], c, sl, sh)
        for hd in range(HEADS):
            lanes = slice(hd * HEAD_PAD, (hd + 1) * HEAD_PAD)
            qb = (_rope(q[:, lanes], c, sl, sh) * Q_SCALE).astype(BF16)
            kb = (k[:, lanes] + kpe).astype(BF16)
            q_ref[:, lanes] = qb
            k_ref[:, lanes] = kb
            qt_ref[lanes, :] = qb.T
            kt_ref[lanes, :] = kb.T

    qk_w = HEADS * HEAD_PAD
    col_spec = lambda rows: pl.BlockSpec((rows, ts), lambda i: (0, i))
    return pl.pallas_call(
        body, name="fwd_pre", grid=(s // ts,),
        in_specs=[_row_spec(ts, D_MODEL), _full_spec(w_pad.shape), _full_spec(b_pad.shape), _full_spec(g_q.shape),
                  _full_spec(wuq.shape), _full_spec(g_kv.shape), _full_spec(wk.shape), _full_spec(wv.shape),
                  _row_spec(ts, HEAD_PAD), _row_spec(ts, HEAD_PAD), _row_spec(ts, HEAD_PAD)],
        out_specs=[_row_spec(ts, GATE_W), _row_spec(ts, MID_W), _row_spec(ts, LAT_W), _row_spec(ts, qk_w),
                   _row_spec(ts, qk_w), _row_spec(ts, MLA_W), col_spec(D_MODEL), col_spec(qk_w), col_spec(qk_w),
                   col_spec(MLA_W)],
        out_shape=[jax.ShapeDtypeStruct((s, GATE_W), F32), jax.ShapeDtypeStruct((s, MID_W), F32),
                   jax.ShapeDtypeStruct((s, LAT_W), F32), jax.ShapeDtypeStruct((s, qk_w), BF16),
                   jax.ShapeDtypeStruct((s, qk_w), BF16), jax.ShapeDtypeStruct((s, MLA_W), BF16),
                   jax.ShapeDtypeStruct((D_MODEL, s), BF16), jax.ShapeDtypeStruct((qk_w, s), BF16),
                   jax.ShapeDtypeStruct((qk_w, s), BF16), jax.ShapeDtypeStruct((MLA_W, s), BF16)],
        compiler_params=_params(),
    )(x, w_pad, b_pad, g_q, wuq, g_kv, wk, wv, rc, rsl, rsh)


def _attn_fwd(qt, k, vt):
    s = k.shape[0]
    tq, tk = ATT_TQ, ATT_TK
    r = tq // tk
    pairs = HEADS // 2

    def body(qt_ref, k_ref, vt_ref, o_ref, lse_ref):
        i = pl.program_id(1)
        krow = lax.broadcasted_iota(jnp.int32, (tk, tq), 0)
        qcol = lax.broadcasted_iota(jnp.int32, (tk, tq), 1)
        qts = [qt_ref[hh * HEAD_PAD:(hh + 1) * HEAD_PAD, :] for hh in range(2)]

        def scores(j):
            koff = pl.multiple_of(j * tk, tk)
            return tuple(_dot(k_ref[pl.ds(koff, tk), hh * HEAD_PAD:(hh + 1) * HEAD_PAD], qts[hh]) for hh in range(2))

        def weighted(j, ps):
            koff = pl.multiple_of(j * tk, tk)
            return tuple(_dot(vt_ref[hh * V_DIM:(hh + 1) * V_DIM, pl.ds(koff, tk)], ps[hh]) for hh in range(2))

        def step(j, carry, diag, last):
            st, ps, stats = carry
            st_next = None if last else scores(j + 1)
            pvs = weighted(jnp.maximum(j - 1, 0), ps)
            new_ps, new_stats = [], []
            for hh in range(2):
                m, l, acc = stats[hh]
                s_ = st[hh]
                if diag is not None:
                    s_ = jnp.where(krow + diag * tk <= qcol, s_, -jnp.inf)
                m_new = jnp.maximum(m, jnp.max(s_, axis=0, keepdims=True))
                a = jnp.exp2(m - m_new)
                p = jnp.exp2(s_ - m_new)
                new_stats.append((m_new, a * l + jnp.sum(p, axis=0, keepdims=True), a * (acc + pvs[hh])))
                new_ps.append(p.astype(BF16))
            return st_next, tuple(new_ps), tuple(new_stats)

        one = (jnp.full((1, tq), -jnp.inf, F32), jnp.zeros((1, tq), F32), jnp.zeros((V_DIM, tq), F32))
        zero_p = jnp.zeros((tk, tq), BF16)
        nfull = i * r
        carry = lax.fori_loop(0, nfull, functools.partial(step, diag=None, last=False),
                              (scores(0), (zero_p, zero_p), (one, one)))
        for d in range(r):
            carry = step(nfull + d, carry, d, d == r - 1)
        _, ps, stats = carry
        pvs = weighted(nfull + r - 1, ps)
        ot = jnp.concatenate([(stats[hh][2] + pvs[hh]) / stats[hh][1] for hh in range(2)], axis=0)
        o_ref[...] = ot.T
        lse = [stats[hh][0] + jnp.log(stats[hh][1]) * LOG2E for hh in range(2)]
        lse_ref[...] = jnp.concatenate(lse + [jnp.zeros((6, tq), F32)], axis=0)

    return pl.pallas_call(
        body, name="attn_fwd", grid=(pairs, s // tq),
        in_specs=[pl.BlockSpec((2 * HEAD_PAD, tq), lambda p, i: (p, i)),
                  pl.BlockSpec((s, 2 * HEAD_PAD), lambda p, i: (0, p)),
                  pl.BlockSpec((2 * V_DIM, s), lambda p, i: (p, 0))],
        out_specs=[pl.BlockSpec((tq, 2 * V_DIM), lambda p, i: (i, p)),
                   pl.BlockSpec((None, 8, tq), lambda p, i: (p, 0, i))],
        out_shape=[jax.ShapeDtypeStruct((s, MLA_W), F32), jax.ShapeDtypeStruct((pairs, 8, s), F32)],
        compiler_params=_params(("arbitrary", "arbitrary")),
    )(qt, k, vt)


def _attn_bwd(q, qt, k, kt, v, do, dot, lse, delta):
    s = k.shape[0]
    tq, tk = ATT_BWD_TQ, ATT_BWD_TK
    r = tq // tk
    nq = s // tq
    nk = s // tk
    pairs = HEADS // 2

    def body(q_ref, qt_ref, k_ref, kt_ref, v_ref, do_ref, dot_ref, lse_ref, dl_ref, dqt_ref, dk_ref, dv_ref):
        j = pl.program_id(1)
        krow = lax.broadcasted_iota(jnp.int32, (tk, tq), 0)
        qcol = lax.broadcasted_iota(jnp.int32, (tk, tq), 1)
        lane = lax.broadcasted_iota(jnp.int32, (tk, 2 * V_DIM), 1)
        drow = lax.broadcasted_iota(jnp.int32, (2 * V_DIM, tq), 0)

        @pl.when(j == 0)
        def _():
            dqt_ref[...] = jnp.zeros_like(dqt_ref)

        koff = pl.multiple_of(j * tk, tk)
        vb = v_ref[pl.ds(koff, tk), :]
        kbs = [k_ref[pl.ds(koff, tk), hh * HEAD_PAD:(hh + 1) * HEAD_PAD] for hh in range(2)]
        ktbs = [kt_ref[hh * HEAD_PAD:(hh + 1) * HEAD_PAD, pl.ds(koff, tk)] for hh in range(2)]
        i0 = j // r

        def front(i):
            qoff = pl.multiple_of(i * tq, tq)
            dotb = dot_ref[:, pl.ds(qoff, tq)]
            out = []
            for hh in range(2):
                mine = (drow < V_DIM) if hh == 0 else (drow >= V_DIM)
                st = _dot(kbs[hh], qt_ref[hh * HEAD_PAD:(hh + 1) * HEAD_PAD, pl.ds(qoff, tq)])
                out.append((st, _dot(vb, jnp.where(mine, dotb, jnp.zeros_like(dotb)))))
            return tuple(out)

        def middle(i, tiles, diag):
            qoff = pl.multiple_of(i * tq, tq)
            out = []
            for hh in range(2):
                st, dpt = tiles[hh]
                if diag:
                    st = jnp.where(krow + (j - i0 * r) * tk <= qcol, st, -jnp.inf)
                p = jnp.exp2(st - lse_ref[hh:hh + 1, pl.ds(qoff, tq)])
                out.append((p.astype(BF16), (p * (dpt - dl_ref[hh:hh + 1, pl.ds(qoff, tq)])).astype(BF16)))
            return tuple(out)

        def back(i, pd, accs):
            qoff = pl.multiple_of(i * tq, tq)
            dob = do_ref[pl.ds(qoff, tq), :]
            out = []
            for hh in range(2):
                rows = slice(hh * HEAD_PAD, (hh + 1) * HEAD_PAD)
                p, dst = pd[hh]
                dk_acc, dv_acc = accs[hh]
                dv_acc = dv_acc + _dot(p, dob)
                dk_acc = dk_acc + _dot(dst, q_ref[pl.ds(qoff, tq), rows])
                dqt_ref[rows, pl.ds(qoff, tq)] += _dot(ktbs[hh], dst)
                out.append((dk_acc, dv_acc))
            return tuple(out)

        def step(i, accs, diag):
            return back(i, middle(i, front(i), diag), accs)

        zero_acc = (jnp.zeros((tk, HEAD_PAD), F32), jnp.zeros((tk, 2 * V_DIM), F32))
        accs = step(i0, (zero_acc, zero_acc), True)
        accs = lax.fori_loop(i0 + 1, nq, functools.partial(step, diag=False), accs)
        for hh in range(2):
            dk_ref[:, hh * HEAD_PAD:(hh + 1) * HEAD_PAD] = accs[hh][0] * LN2
        dv_ref[...] = jnp.where(lane < V_DIM, accs[0][1], accs[1][1])

        @pl.when(j == nk - 1)
        def _():
            dqt_ref[...] = dqt_ref[...] * SCALE

    pair_rows = lambda w: pl.BlockSpec((s, w), lambda p, j: (0, p))
    pair_cols = lambda w: pl.BlockSpec((w, s), lambda p, j: (p, 0))
    stats = pl.BlockSpec((None, 8, s), lambda p, j: (p, 0, 0))
    return pl.pallas_call(
        body, name="attn_bwd", grid=(pairs, nk),
        in_specs=[pair_rows(2 * HEAD_PAD), pair_cols(2 * HEAD_PAD), pair_rows(2 * HEAD_PAD), pair_cols(2 * HEAD_PAD),
                  pair_rows(2 * V_DIM), pair_rows(2 * V_DIM), pair_cols(2 * V_DIM), stats, stats],
        out_specs=[pair_cols(2 * HEAD_PAD),
                   pl.BlockSpec((tk, 2 * HEAD_PAD), lambda p, j: (j, p)),
                   pl.BlockSpec((tk, 2 * V_DIM), lambda p, j: (j, p))],
        out_shape=[jax.ShapeDtypeStruct((HEADS * HEAD_PAD, s), F32), jax.ShapeDtypeStruct((s, HEADS * HEAD_PAD), F32),
                   jax.ShapeDtypeStruct((s, MLA_W), F32)],
        compiler_params=_params(("arbitrary", "arbitrary")),
    )(q, qt, k, kt, v, do, dot, lse, delta)


def _split3(a):
    hi = a.astype(BF16)
    r1 = a - hi.astype(F32)
    mid = r1.astype(BF16)
    lo = (r1 - mid.astype(F32)).astype(BF16)
    return hi, mid, lo


def _mid(x, tgt, o, hm, hg, woa, wob, wout, ln_g, ln_b, sg_g, sg_b, w_s, bsb):
    s = x.shape[0]
    ts = ROW_TILE
    nsteps = s // ts
    nch = ts // CHUNK
    npair = GROUPS // 2

    def body(x_ref, t_ref, o_ref, hm_ref, hg_ref, woa_ref, wob_ref, wout_ref, lng_ref, lnb_ref, sgg_ref, sgb_ref,
             ws_ref, bsb_ref,
             dr_ref, dhg_ref, dhm_ref, do_ref, dot_ref, dl_ref,
             dwout_ref, dwoa_ref, dwob_ref, dws_ref, dbs_ref, dlng_ref, dlnb_ref, dsgg_ref, dsgb_ref, loss_ref,
             dbacc_ref):
        i = pl.program_id(0)

        @pl.when(i == 0)
        def _():
            for r in (dwout_ref, dwoa_ref, dwob_ref, dws_ref, dlng_ref, dlnb_ref, dsgg_ref, dsgb_ref, loss_ref,
                      dbacc_ref):
                r[...] = jnp.zeros_like(r)

        lane = lax.broadcasted_iota(jnp.int32, (CHUNK, CHUNK), 1)
        left = lane < V_DIM
        tril = lax.broadcasted_iota(jnp.int32, (CHUNK, CHUNK), 0) >= lane
        ms = [jnp.where(tril, ws_ref[g], 0.0).astype(BF16) for g in range(GROUPS)]

        z_a = hm_ref[:, 0:SGU_W]
        u = hm_ref[:, SGU_W:2 * SGU_W]
        v = hm_ref[:, 2 * SGU_W:3 * SGU_W]
        z_b = hm_ref[:, 3 * SGU_W:4 * SGU_W]
        o = o_ref[...]
        sa, dsa = _silu_and_grad(z_a)
        y_a = (o * sa).astype(BF16)
        gu, dgu = _gelu_and_grad(u)
        gv, dgv = _gelu_and_grad(v)
        mu = jnp.mean(gv, axis=-1, keepdims=True)
        vc = gv - mu
        rstd_v = lax.rsqrt(jnp.mean(vc * vc, axis=-1, keepdims=True) + LN_EPS)
        vhat = vc * rstd_v
        vn = (vhat * sgg_ref[...] + sgb_ref[...]).astype(BF16)
        rows = []
        for c in range(nch):
            blocks = []
            for p in range(npair):
                blk = vn[c * CHUNK:(c + 1) * CHUNK, p * CHUNK:(p + 1) * CHUNK]
                blocks.append(jnp.where(left, _dot(ms[2 * p], blk), _dot(ms[2 * p + 1], blk)))
            rows.append(jnp.concatenate(blocks, axis=1) + bsb_ref[...])
        mixed = jnp.concatenate(rows, axis=0)
        sgu = gu * mixed
        sb, dsb = _silu_and_grad(z_b)
        y_b = (sgu * sb).astype(BF16)
        pa = _dot(y_a, woa_ref[...])
        pb = _dot(y_b, wob_ref[...])
        sga = _sigmoid(hg_ref[:, :D_MODEL])
        sgb = _sigmoid(hg_ref[:, D_MODEL:])
        m2 = (sga * pa + sgb * pb).astype(BF16)
        r = ALPHA * x_ref[...] + _dot(m2, wout_ref[...])
        rmu = jnp.mean(r, axis=-1, keepdims=True)
        rc = r - rmu
        rstd = lax.rsqrt(jnp.mean(rc * rc, axis=-1, keepdims=True) + LN_EPS)
        xhat = rc * rstd
        y = xhat * lng_ref[...] + lnb_ref[...]
        err = y - t_ref[...]
        loss_ref[...] += jnp.full(loss_ref.shape, 0.5 / D_MODEL, F32) * jnp.sum(err * err)

        dy = err * (1.0 / D_MODEL)
        dlng_ref[...] += jnp.sum(dy * xhat, axis=0, keepdims=True)
        dlnb_ref[...] += jnp.sum(dy, axis=0, keepdims=True)
        dxh = dy * lng_ref[...]
        dr = rstd * (dxh - jnp.mean(dxh, axis=-1, keepdims=True) - xhat * jnp.mean(dxh * xhat, axis=-1, keepdims=True))
        dr_ref[...] = dr
        drb = dr.astype(BF16)
        dwout_ref[...] += _dot_tn(m2, drb)
        dm2 = _dot_nt(drb, wout_ref[...])
        dhg_ref[:, :D_MODEL] = (dm2 * pa * sga * (1.0 - sga)).astype(BF16)
        dhg_ref[:, D_MODEL:] = (dm2 * pb * sgb * (1.0 - sgb)).astype(BF16)
        dpa = (dm2 * sga).astype(BF16)
        dpb = (dm2 * sgb).astype(BF16)
        dwoa_ref[...] += _dot_tn(y_a, dpa)
        dwob_ref[...] += _dot_tn(y_b, dpb)
        dy_a = _dot_nt(dpa, woa_ref[...])
        dy_b = _dot_nt(dpb, wob_ref[...])
        dob = (dy_a * sa).astype(BF16)
        do_ref[...] = dob
        dot_ref[...] = dob.T
        head = (lax.broadcasted_iota(jnp.int32, (HEADS, MLA_W), 1) // V_DIM
                == lax.broadcasted_iota(jnp.int32, (HEADS, MLA_W), 0)).astype(BF16)
        dl_ref[...] = sum(_dot_nt(head, term) for term in _split3(dob.astype(F32) * o))
        dhm_ref[:, 0:SGU_W] = (dy_a * o * dsa).astype(BF16)
        dsg = dy_b * sb
        dhm_ref[:, 3 * SGU_W:4 * SGU_W] = (dy_b * sgu * dsb).astype(BF16)
        dhm_ref[:, SGU_W:2 * SGU_W] = (dsg * mixed * dgu).astype(BF16)
        dmixed = dsg * gu
        dvn_rows = []
        dbs_sum = jnp.zeros((CHUNK, SGU_W), F32)
        for c in range(nch):
            dm_c = dmixed[c * CHUNK:(c + 1) * CHUNK, :]
            dbs_sum = dbs_sum + dm_c
            blocks = []
            for p in range(npair):
                dmb = dm_c[:, p * CHUNK:(p + 1) * CHUNK].astype(BF16)
                blk = vn[c * CHUNK:(c + 1) * CHUNK, p * CHUNK:(p + 1) * CHUNK]
                blocks.append(jnp.where(left, _dot_tn(ms[2 * p], dmb), _dot_tn(ms[2 * p + 1], dmb)))
                zero = jnp.zeros_like(dmb)
                dws_ref[2 * p] += jnp.where(tril, _dot_nt(jnp.where(left, dmb, zero), blk), 0.0)
                dws_ref[2 * p + 1] += jnp.where(tril, _dot_nt(jnp.where(left, zero, dmb), blk), 0.0)
            dvn_rows.append(jnp.concatenate(blocks, axis=1))
        dbacc_ref[...] += dbs_sum
        dvn = jnp.concatenate(dvn_rows, axis=0)
        dsgg_ref[...] += jnp.sum(dvn * vhat, axis=0, keepdims=True)
        dsgb_ref[...] += jnp.sum(dvn, axis=0, keepdims=True)
        dvh = dvn * sgg_ref[...]
        dgv_in = rstd_v * (dvh - jnp.mean(dvh, axis=-1, keepdims=True)
                           - vhat * jnp.mean(dvh * vhat, axis=-1, keepdims=True))
        dhm_ref[:, 2 * SGU_W:3 * SGU_W] = (dgv_in * dgv).astype(BF16)

        @pl.when(i == nsteps - 1)
        def _():
            grp = (lax.broadcasted_iota(jnp.int32, (SGU_W, CHUNK), 0) // V_DIM
                   == lax.broadcasted_iota(jnp.int32, (SGU_W, CHUNK), 1)).astype(BF16)
            hi, mid, lo = _split3(dbacc_ref[...])
            dbs_ref[...] = _dot(hi, grp) + _dot(mid, grp) + _dot(lo, grp)

    acc_shapes = [(D_MODEL, D_MODEL), (MLA_W, D_MODEL), (SGU_W, D_MODEL), (GROUPS, CHUNK, CHUNK), (CHUNK, CHUNK),
                  (1, D_MODEL), (1, D_MODEL), (1, SGU_W), (1, SGU_W), (1, 128)]
    return pl.pallas_call(
        body, name="mid", grid=(nsteps,),
        in_specs=[_row_spec(ts, D_MODEL), _row_spec(ts, D_MODEL), _row_spec(ts, MLA_W), _row_spec(ts, MID_W),
                  _row_spec(ts, GATE_W), _full_spec(woa.shape), _full_spec(wob.shape), _full_spec(wout.shape),
                  _full_spec(ln_g.shape), _full_spec(ln_b.shape), _full_spec(sg_g.shape), _full_spec(sg_b.shape),
                  _full_spec(w_s.shape), _full_spec(bsb.shape)],
        out_specs=[_row_spec(ts, D_MODEL), _row_spec(ts, GATE_W), _row_spec(ts, MID_W), _row_spec(ts, MLA_W),
                   pl.BlockSpec((MLA_W, ts), lambda i: (0, i)), pl.BlockSpec((HEADS, ts), lambda i: (0, i))]
        + [_full_spec(sh) for sh in acc_shapes],
        out_shape=[jax.ShapeDtypeStruct((s, D_MODEL), F32), jax.ShapeDtypeStruct((s, GATE_W), BF16),
                   jax.ShapeDtypeStruct((s, MID_W), BF16), jax.ShapeDtypeStruct((s, MLA_W), BF16),
                   jax.ShapeDtypeStruct((MLA_W, s), BF16), jax.ShapeDtypeStruct((HEADS, s), F32)]
        + [jax.ShapeDtypeStruct(sh, F32) for sh in acc_shapes],
        scratch_shapes=[pltpu.VMEM((CHUNK, SGU_W), F32)],
        compiler_params=_params(),
    )(x, tgt, o, hm, hg, woa, wob, wout, ln_g, ln_b, sg_g, sg_b, w_s, bsb)


def _lat_bwd(dq, dk, dv, hl, rc, rsl, rsh, g_q, g_kv, wuq, wk, wv):
    s = dk.shape[0]
    ts = ROW_TILE
    qk_w = HEADS * HEAD_PAD

    def body(dq_ref, dk_ref, dv_ref, hl_ref, rc_ref, rsl_ref, rsh_ref, gq_ref, gkv_ref, wuq_ref, wk_ref, wv_ref,
             dhl_ref, dwuq_ref, dwk_ref, dwv_ref, dgq_ref, dgkv_ref):
        i = pl.program_id(0)

        @pl.when(i == 0)
        def _():
            for r in (dwuq_ref, dwk_ref, dwv_ref, dgq_ref, dgkv_ref):
                r[...] = jnp.zeros_like(r)

        c, sl, sh = rc_ref[...], rsl_ref[...], rsh_ref[...]
        lane = lax.broadcasted_iota(jnp.int32, (ts, HEAD_PAD), 1)
        pe = (lane >= NOPE) & (lane < QK_DIM)
        dkpe = jnp.zeros((ts, HEAD_PAD), F32)
        dqu = []
        for hd in range(HEADS):
            lanes = slice(hd * HEAD_PAD, (hd + 1) * HEAD_PAD)
            dqu.append(_rope_t(dq_ref[lanes, :].T, c, sl, sh).astype(BF16))
            dkpe = dkpe + dk_ref[:, lanes]
        dqu = jnp.concatenate(dqu, axis=1)
        dkpe = _rope_t(jnp.where(pe, dkpe, 0.0), c, sl, sh)

        cq = hl_ref[:, :Q_RANK]
        rq = lax.rsqrt(jnp.mean(cq * cq, axis=-1, keepdims=True) + RMS_EPS)
        cqh = cq * rq
        cqn = (cqh * gq_ref[...]).astype(BF16)
        dwuq_ref[...] += _dot_tn(cqn, dqu)
        dcqn = _dot_nt(dqu, wuq_ref[...])
        dgq_ref[...] += jnp.sum(dcqn * cqh, axis=0, keepdims=True)
        dch = dcqn * gq_ref[...]
        dhl_ref[:, :Q_RANK] = (rq * (dch - cqh * jnp.mean(dch * cqh, axis=-1, keepdims=True))).astype(BF16)

        ckv = hl_ref[:, Q_RANK:Q_RANK + KV_RANK]
        rk = lax.rsqrt(jnp.mean(ckv * ckv, axis=-1, keepdims=True) + RMS_EPS)
        ckh = ckv * rk
        ckn = (ckh * gkv_ref[...]).astype(BF16)
        dkb = dk_ref[...].astype(BF16)
        dvb = dv_ref[...].astype(BF16)
        dwk_ref[...] += _dot_tn(ckn, dkb)
        dwv_ref[...] += _dot_tn(ckn, dvb)
        dckn = _dot_nt(dkb, wk_ref[...]) + _dot_nt(dvb, wv_ref[...])
        dgkv_ref[...] += jnp.sum(dckn * ckh, axis=0, keepdims=True)
        dkh = dckn * gkv_ref[...]
        dhl_ref[:, Q_RANK:Q_RANK + KV_RANK] = (rk * (dkh - ckh * jnp.mean(dkh * ckh, axis=-1, keepdims=True))).astype(BF16)
        dhl_ref[:, Q_RANK + KV_RANK:] = dkpe.astype(BF16)

    acc_shapes = [wuq.shape, wk.shape, wv.shape, g_q.shape, g_kv.shape]
    return pl.pallas_call(
        body, name="lat_bwd", grid=(s // ts,),
        in_specs=[pl.BlockSpec((qk_w, ts), lambda i: (0, i)), _row_spec(ts, qk_w), _row_spec(ts, MLA_W),
                  _row_spec(ts, LAT_W), _row_spec(ts, HEAD_PAD), _row_spec(ts, HEAD_PAD), _row_spec(ts, HEAD_PAD),
                  _full_spec(g_q.shape), _full_spec(g_kv.shape), _full_spec(wuq.shape), _full_spec(wk.shape),
                  _full_spec(wv.shape)],
        out_specs=[_row_spec(ts, LAT_W)] + [_full_spec(sh) for sh in acc_shapes],
        out_shape=[jax.ShapeDtypeStruct((s, LAT_W), BF16)] + [jax.ShapeDtypeStruct(sh, F32) for sh in acc_shapes],
        compiler_params=_params(),
    )(dq, dk, dv, hl, rc, rsl, rsh, g_q, g_kv, wuq, wk, wv)


def _dx(dr, dhg, dhm, dhl, w_pad):
    s = dr.shape[0]
    ts = ROW_TILE

    def body(dr_ref, dhg_ref, dhm_ref, dhl_ref, w_ref, dx_ref):
        dx_ref[...] = (ALPHA * dr_ref[...]
                       + _dot_nt(dhg_ref[...], w_ref[:, :GATE_W])
                       + _dot_nt(dhm_ref[...], w_ref[:, GATE_W:GATE_W + MID_W])
                       + _dot_nt(dhl_ref[...], w_ref[:, GATE_W + MID_W:]))

    return pl.pallas_call(
        body, name="dx", grid=(s // ts,),
        in_specs=[_row_spec(ts, D_MODEL), _row_spec(ts, GATE_W), _row_spec(ts, MID_W), _row_spec(ts, LAT_W),
                  _full_spec(w_pad.shape)],
        out_specs=_row_spec(ts, D_MODEL),
        out_shape=jax.ShapeDtypeStruct((s, D_MODEL), F32),
        compiler_params=_params(),
    )(dr, dhg, dhm, dhl, w_pad)


def _dw_in(xt, dh, tn, name):
    s, width = dh.shape

    def body(xt_ref, dh_ref, dw_ref, db_ref):
        dhb = dh_ref[...]
        dw_ref[...] = _dot(xt_ref[...], dhb)
        db_ref[...] = jnp.sum(dhb.astype(F32), axis=0, keepdims=True)

    return pl.pallas_call(
        body, name=name, grid=(width // tn,),
        in_specs=[_full_spec(xt.shape), pl.BlockSpec((s, tn), lambda n: (0, n))],
        out_specs=[pl.BlockSpec((D_MODEL, tn), lambda n: (0, n)), pl.BlockSpec((1, tn), lambda n: (0, n))],
        out_shape=[jax.ShapeDtypeStruct((D_MODEL, width), F32), jax.ShapeDtypeStruct((1, width), F32)],
        compiler_params=_params(),
    )(xt, dh)


_C_Q, _C_KV, _K_PE, _Z_A, _U, _V, _Z_B, _G_A, _G_B = (
    (0, 384), (384, 512), (512, 544), (544, 1056), (1056, 1568), (1568, 2080), (2080, 2592), (2592, 3616),
    (3616, 4640))


def _pad_in_cols(w):
    cut = lambda ab: w[:, ab[0]:ab[1]]
    z = lambda n: jnp.zeros((w.shape[0], n), w.dtype)
    return jnp.concatenate([cut(_G_A), cut(_G_B), cut(_Z_A), cut(_U), cut(_V), cut(_Z_B), cut(_C_Q), cut(_C_KV),
                            z(NOPE), cut(_K_PE), z(HEAD_PAD - QK_DIM)], axis=1)


def _unpad_in_cols(wp):
    g, m, l = wp[:, :GATE_W], wp[:, GATE_W:GATE_W + MID_W], wp[:, GATE_W + MID_W:]
    kpe = l[:, Q_RANK + KV_RANK + NOPE:Q_RANK + KV_RANK + QK_DIM]
    return jnp.concatenate([l[:, :Q_RANK + KV_RANK], kpe, m, g], axis=1)


def _rope_tables(positions):
    half = ROPE // 2
    inv_freq = ROPE_THETA ** (-jnp.arange(0, ROPE, 2, dtype=F32) / ROPE)
    ang = positions.astype(F32)[:, None] * inv_freq
    cos, sin = jnp.cos(ang), jnp.sin(ang)
    n = positions.shape[0]
    one, zero = jnp.ones((n, NOPE), F32), jnp.zeros((n, half), F32)
    tail1, tail0 = jnp.ones((n, HEAD_PAD - QK_DIM), F32), jnp.zeros((n, HEAD_PAD - QK_DIM), F32)
    z64 = jnp.zeros((n, NOPE), F32)
    rc = jnp.concatenate([one, cos, cos, tail1], axis=1)
    rsl = jnp.concatenate([z64, -sin, zero, tail0], axis=1)
    rsh = jnp.concatenate([z64, zero, sin, tail0], axis=1)
    return rc, rsl, rsh


def _local_step(x, positions, tgt, w_in, b_in, g_q, w_uq, g_kv, w_ukv, w_oa, sg_g, sg_b, w_s, b_s, w_ob, w_out,
                ln_g, ln_b):
    rc, rsl, rsh = _rope_tables(positions)
    w_pad = _pad_in_cols(w_in)
    b_pad = _pad_in_cols(b_in[None, :])
    wuq = jnp.pad(w_uq, ((0, 0), (0, 0), (0, HEAD_PAD - QK_DIM))).reshape(Q_RANK, HEADS * HEAD_PAD).astype(BF16)
    wk = jnp.pad(w_ukv[:, :, :NOPE], ((0, 0), (0, 0), (0, HEAD_PAD - NOPE))).reshape(KV_RANK, HEADS * HEAD_PAD).astype(BF16)
    wv = w_ukv[:, :, NOPE:].reshape(KV_RANK, MLA_W).astype(BF16)
    bsb = jnp.repeat(b_s.T, V_DIM, axis=1)
    gq2, gkv2 = g_q[None, :], g_kv[None, :]

    hg, hm, hl, q, k, v, xt, qt, kt, vt = _fwd_pre(x, w_pad, b_pad, gq2, wuq, gkv2, wk, wv, rc, rsl, rsh)
    o, lse = _attn_fwd(qt, k, vt)
    (dr, dhg, dhm, do, dot, delta, dwout, dwoa, dwob, dws, dbs, dlng, dlnb, dsgg, dsgb, loss) = _mid(
        x, tgt, o, hm, hg, w_oa, w_ob, w_out, ln_g[None, :], ln_b[None, :], sg_g[None, :], sg_b[None, :], w_s, bsb)
    delta = jnp.pad(delta.reshape(HEADS // 2, 2, -1), ((0, 0), (0, 6), (0, 0)))
    dq, dk, dv = _attn_bwd(q, qt, k, kt, v, do, dot, lse, delta)
    dhl, dwuq, dwk, dwv, dgq, dgkv = _lat_bwd(dq, dk, dv, hl, rc, rsl, rsh, gq2, gkv2, wuq, wk, wv)
    dx = _dx(dr, dhg, dhm, dhl, w_pad)
    dwg, dbg = _dw_in(xt, dhg, 512, "dw_in_gates")
    dwm, dbm = _dw_in(xt, dhm, 512, "dw_in_mid")
    dwl, dbl = _dw_in(xt, dhl, LAT_W, "dw_in_lat")
    grads = {
        "w_in": _unpad_in_cols(jnp.concatenate([dwg, dwm, dwl], axis=1)),
        "b_in": _unpad_in_cols(jnp.concatenate([dbg, dbm, dbl], axis=1))[0],
        "g_q": dgq[0],
        "w_uq": dwuq.reshape(Q_RANK, HEADS, HEAD_PAD)[:, :, :QK_DIM],
        "g_kv": dgkv[0],
        "w_ukv": jnp.concatenate([dwk.reshape(KV_RANK, HEADS, HEAD_PAD)[:, :, :NOPE],
                                  dwv.reshape(KV_RANK, HEADS, V_DIM)], axis=2),
        "w_oa": dwoa, "sgu_ln_g": dsgg[0], "sgu_ln_b": dsgb[0], "w_s": dws, "b_s": dbs[:, :GROUPS].T,
        "w_ob": dwob, "w_out": dwout, "ln_g": dlng[0], "ln_b": dlnb[0],
    }
    return loss[0, 0], dx, grads


MESH = pl.DeviceIdType.MESH
N_CHIPS = 4
HBM_SPEC = pl.BlockSpec(memory_space=pl.ANY)
VMEM_SPEC = pl.BlockSpec(memory_space=pltpu.VMEM)

SHARD_ROWS = 1744
REP_ROWS = 80
SLAB_ROWS = SHARD_ROWS + REP_ROWS
HALF_ROWS = SLAB_ROWS // 2
REP_OFF = SHARD_ROWS - HALF_ROWS
ADD_TILE = HALF_ROWS // 3


def _place():
    x, y, c = lax.axis_index("x"), lax.axis_index("y"), lax.axis_index("c")
    others = [(1 - x, y), (x, 1 - y), (1 - x, 1 - y)]
    return x, y, c, others


def _gather_weights(shards):
    n = len(shards)

    def body(*refs):
        ins, outs, bufs = refs[:n], refs[n:2 * n], refs[2 * n:3 * n]
        send_sems, recv_sems, local_sems = refs[3 * n:]
        x, y, c, others = _place()
        me = 2 * x + y
        sibling = (x, y, 1 - c)
        for src, buf in zip(ins, bufs):
            buf[...] = src[...].astype(BF16)
        own = [pltpu.make_async_copy(bufs[w], outs[w].at[me], local_sems.at[w]) for w in range(n)]
        for cp in own:
            cp.start()

        def part(w, chip, half):
            hr = shards[w].shape[0] // 2
            return outs[w].at[chip, pl.ds(half * hr, hr), :]

        def sent(w, j):
            hr = shards[w].shape[0] // 2
            return pltpu.make_async_remote_copy(
                src_ref=bufs[w].at[pl.ds(c * hr, hr), :], dst_ref=part(w, me, c),
                send_sem=send_sems.at[w * 3 + j], recv_sem=recv_sems.at[w * 3 + j],
                device_id=(*others[j], c), device_id_type=MESH)

        def landed(w, j):
            px, py = others[j]
            return pltpu.make_async_remote_copy(
                src_ref=part(w, 2 * px + py, c), dst_ref=part(w, 2 * px + py, c),
                send_sem=send_sems.at[w * 3 + j], recv_sem=recv_sems.at[w * 3 + j],
                device_id=(px, py, c), device_id_type=MESH)

        def passed(w, j, half):
            px, py = others[j]
            k = n * 3 + w * 3 + j
            return pltpu.make_async_remote_copy(
                src_ref=part(w, 2 * px + py, half), dst_ref=part(w, 2 * px + py, half),
                send_sem=send_sems.at[k], recv_sem=recv_sems.at[k], device_id=sibling, device_id_type=MESH)

        first = [sent(w, j) for w in range(n) for j in range(3)]
        for cp in first:
            cp.start()
        fwd = []
        for w in range(n):
            for j in range(3):
                landed(w, j).wait_recv()
                cp = passed(w, j, c)
                cp.start()
                fwd.append(cp)
        for w in range(n):
            for j in range(3):
                passed(w, j, 1 - c).wait_recv()
        for cp in first + fwd:
            cp.wait_send()
        for cp in own:
            cp.wait()

    return pl.pallas_call(
        body, name="gather_weights",
        in_specs=[VMEM_SPEC] * n, out_specs=[HBM_SPEC] * n,
        out_shape=[jax.ShapeDtypeStruct((N_CHIPS,) + s.shape, BF16) for s in shards],
        scratch_shapes=[pltpu.VMEM(s.shape, BF16) for s in shards]
        + [pltpu.SemaphoreType.DMA((6 * n,)), pltpu.SemaphoreType.DMA((6 * n,)), pltpu.SemaphoreType.DMA((n,))],
        compiler_params=pltpu.CompilerParams(vmem_limit_bytes=VMEM_LIMIT),
    )(*shards)


def _swap_halves(p):
    def body(p_ref, r_ref, send_sem, recv_sem):
        x, y, c, _ = _place()
        cp = pltpu.make_async_remote_copy(
            src_ref=p_ref.at[:, pl.ds((1 - c) * HALF_ROWS, HALF_ROWS), :], dst_ref=r_ref,
            send_sem=send_sem, recv_sem=recv_sem, device_id=(x, y, 1 - c), device_id_type=MESH)
        cp.start()
        cp.wait()

    return pl.pallas_call(
        body, name="swap_halves", in_specs=[HBM_SPEC], out_specs=HBM_SPEC,
        out_shape=jax.ShapeDtypeStruct((N_CHIPS, HALF_ROWS, D_MODEL), BF16),
        scratch_shapes=[pltpu.SemaphoreType.DMA, pltpu.SemaphoreType.DMA],
    )(p)


def _pair_add(p, r, c):
    def body(c_ref, p_ref, r_ref, q_ref):
        q_ref[...] = (p_ref[...].astype(F32) + r_ref[...].astype(F32)).astype(BF16)

    nt = HALF_ROWS // ADD_TILE
    return pl.pallas_call(
        body, name="pair_add",
        grid_spec=pltpu.PrefetchScalarGridSpec(
            num_scalar_prefetch=1, grid=(N_CHIPS, nt),
            in_specs=[pl.BlockSpec((None, ADD_TILE, D_MODEL), lambda k, i, c_ref: (k, c_ref[0] * nt + i, 0)),
                      pl.BlockSpec((None, ADD_TILE, D_MODEL), lambda k, i, c_ref: (k, i, 0))],
            out_specs=pl.BlockSpec((None, ADD_TILE, D_MODEL), lambda k, i, c_ref: (k, i, 0))),
        out_shape=jax.ShapeDtypeStruct((N_CHIPS, HALF_ROWS, D_MODEL), BF16),
    )(c, p, r)


def _scatter_chips(q):
    def body(q_ref, r_ref, send_sems, recv_sems, local_sem):
        x, y, c, others = _place()
        me = 2 * x + y
        own = pltpu.make_async_copy(q_ref.at[me], r_ref.at[me], local_sem)
        own.start()
        cps = []
        for j, (px, py) in enumerate(others):
            cp = pltpu.make_async_remote_copy(
                src_ref=q_ref.at[2 * px + py], dst_ref=r_ref.at[me],
                send_sem=send_sems.at[j], recv_sem=recv_sems.at[j], device_id=(px, py, c), device_id_type=MESH)
            cp.start()
            cps.append(cp)
        for j, (px, py) in enumerate(others):
            pltpu.make_async_remote_copy(
                src_ref=q_ref.at[me], dst_ref=r_ref.at[2 * px + py],
                send_sem=send_sems.at[j], recv_sem=recv_sems.at[j], device_id=(px, py, c),
                device_id_type=MESH).wait_recv()
        for cp in cps:
            cp.wait_send()
        own.wait()

    return pl.pallas_call(
        body, name="scatter_chips", in_specs=[HBM_SPEC], out_specs=HBM_SPEC,
        out_shape=jax.ShapeDtypeStruct(q.shape, BF16),
        scratch_shapes=[pltpu.SemaphoreType.DMA((3,)), pltpu.SemaphoreType.DMA((3,)), pltpu.SemaphoreType.DMA],
    )(q)


def _sum_chips(r):
    def body(r_ref, g_ref):
        acc = r_ref[0].astype(F32)
        for k in range(1, N_CHIPS):
            acc = acc + r_ref[k].astype(F32)
        g_ref[...] = acc

    return pl.pallas_call(
        body, name="sum_chips", grid=(HALF_ROWS // ADD_TILE,),
        in_specs=[pl.BlockSpec((N_CHIPS, ADD_TILE, D_MODEL), lambda i: (0, i, 0))],
        out_specs=pl.BlockSpec((ADD_TILE, D_MODEL), lambda i: (i, 0)),
        out_shape=jax.ShapeDtypeStruct((HALF_ROWS, D_MODEL), F32),
    )(r)


def _share_reduced(gh):
    def body(gh_ref, g_ref, rep_ref, send_sems, recv_sems, local_sems):
        x, y, c, others = _place()
        me = 2 * x + y
        sibling = (x, y, 1 - c)
        own = pltpu.make_async_copy(gh_ref, g_ref.at[pl.ds(c * HALF_ROWS, HALF_ROWS), :], local_sems.at[0])
        own.start()
        half = pltpu.make_async_remote_copy(
            src_ref=gh_ref, dst_ref=g_ref.at[pl.ds(c * HALF_ROWS, HALF_ROWS), :],
            send_sem=send_sems.at[0], recv_sem=recv_sems.at[0], device_id=sibling, device_id_type=MESH)
        half.start()
        quarter = gh_ref.at[pl.ds(REP_OFF, REP_ROWS), :]

        def rep_copy(k, to):
            return pltpu.make_async_remote_copy(
                src_ref=quarter, dst_ref=rep_ref.at[me], send_sem=send_sems.at[k], recv_sem=recv_sems.at[k],
                device_id=to, device_id_type=MESH)

        @pl.when(c == 1)
        def _():
            mine = pltpu.make_async_copy(quarter, rep_ref.at[me], local_sems.at[1])
            mine.start()
            cps = [rep_copy(1, sibling)]
            for j, (px, py) in enumerate(others):
                cps.append(rep_copy(2 + j, (px, py, 1)))
                cps.append(rep_copy(5 + j, (px, py, 0)))
            for cp in cps:
                cp.start()
            for cp in cps:
                cp.wait_send()
            mine.wait()

        def rep_wait(k, chip):
            pltpu.make_async_remote_copy(
                src_ref=quarter, dst_ref=rep_ref.at[chip], send_sem=send_sems.at[k], recv_sem=recv_sems.at[k],
                device_id=sibling, device_id_type=MESH).wait_recv()

        @pl.when(c == 0)
        def _():
            rep_wait(1, me)
            for j, (px, py) in enumerate(others):
                rep_wait(5 + j, 2 * px + py)

        @pl.when(c == 1)
        def _():
            for j, (px, py) in enumerate(others):
                rep_wait(2 + j, 2 * px + py)

        half.wait()
        own.wait()

    return pl.pallas_call(
        body, name="share_reduced", in_specs=[HBM_SPEC], out_specs=[HBM_SPEC, HBM_SPEC],
        out_shape=[jax.ShapeDtypeStruct((SLAB_ROWS, D_MODEL), F32),
                   jax.ShapeDtypeStruct((N_CHIPS, REP_ROWS, D_MODEL), F32)],
        scratch_shapes=[pltpu.SemaphoreType.DMA((8,)), pltpu.SemaphoreType.DMA((8,)), pltpu.SemaphoreType.DMA((2,))],
    )(gh)


def _adamw(w, g, m, v):
    m2 = ADAM_B1 * m + (1.0 - ADAM_B1) * g
    v2 = ADAM_B2 * v + (1.0 - ADAM_B2) * (g * g)
    m_hat = m2 / (1.0 - ADAM_B1 ** ADAM_STEP)
    v_hat = v2 / (1.0 - ADAM_B2 ** ADAM_STEP)
    return -ADAM_LR * (m_hat / (jnp.sqrt(v_hat) + ADAM_EPS) + ADAM_WD * w), m2, v2


def _update_tiled(w, g, m, v, tile):
    def body(w_ref, g_ref, m_ref, v_ref, d_ref, m2_ref, v2_ref):
        d_ref[...], m2_ref[...], v2_ref[...] = _adamw(w_ref[...], g_ref[...], m_ref[...], v_ref[...])

    spec = pl.BlockSpec((tile, w.shape[1]), lambda i: (i, 0))
    return pl.pallas_call(
        body, name="update_w_in", grid=(w.shape[0] // tile,), in_specs=[spec] * 4, out_specs=[spec] * 3,
        out_shape=[jax.ShapeDtypeStruct(w.shape, F32)] * 3,
        compiler_params=_params(("parallel",)),
    )(w, g, m, v)


def _update_small(ws, gs, ms, vs):
    n = len(ws)

    def body(*refs):
        for k in range(n):
            w_ref, g_ref, m_ref, v_ref = refs[k], refs[n + k], refs[2 * n + k], refs[3 * n + k]
            d, m2, v2 = _adamw(w_ref[...], g_ref[...], m_ref[...], v_ref[...])
            refs[4 * n + k][...] = d
            refs[5 * n + k][...] = m2
            refs[6 * n + k][...] = v2

    shapes = [jax.ShapeDtypeStruct(w.shape, F32) for w in ws]
    outs = pl.pallas_call(
        body, name="update_small", in_specs=[VMEM_SPEC] * (4 * n), out_specs=[VMEM_SPEC] * (3 * n),
        out_shape=shapes * 3,
        compiler_params=pltpu.CompilerParams(vmem_limit_bytes=VMEM_LIMIT),
    )(*ws, *gs, *ms, *vs)
    return outs[:n], outs[n:2 * n], outs[2 * n:]


SHARDED = ("w_in", "w_uq", "w_oa", "w_ob", "w_out")
REPLICATED = ("b_in", "g_q", "g_kv", "w_ukv", "sgu_ln_g", "sgu_ln_b", "w_s", "b_s", "ln_g", "ln_b")
ORDER = ("w_in", "b_in", "g_q", "w_uq", "g_kv", "w_ukv", "w_oa", "sgu_ln_g", "sgu_ln_b", "w_s", "b_s", "w_ob", "w_out",
         "ln_g", "ln_b")


def _shard_of(name, g, k):
    if name == "w_in":
        return g[:, k * (IN_W // 4):(k + 1) * (IN_W // 4)]
    if name == "w_uq":
        return g[k * (Q_RANK // 4):(k + 1) * (Q_RANK // 4)]
    if name in ("w_oa", "w_ob"):
        return g[:, k * (D_MODEL // 4):(k + 1) * (D_MODEL // 4)]
    return g[k * (D_MODEL // 4):(k + 1) * (D_MODEL // 4)]


def _rows(a):
    return a.reshape(-1, D_MODEL)


def kernel(x, positions, w_in, b_in, g_q, w_uq, g_kv, w_ukv, w_oa, sgu_ln_g, sgu_ln_b, w_s, b_s, w_ob, w_out, ln_g, ln_b, loss_target, m_w_in, m_b_in, m_g_q, m_w_uq, m_g_kv, m_w_ukv, m_w_oa, m_sgu_ln_g, m_sgu_ln_b, m_w_s, m_b_s, m_w_ob, m_w_out, m_ln_g, m_ln_b, v_w_in, v_b_in, v_g_q, v_w_uq, v_g_kv, v_w_ukv, v_w_oa, v_sgu_ln_g, v_sgu_ln_b, v_w_s, v_b_s, v_w_ob, v_w_out, v_ln_g, v_ln_b):
    w = dict(w_in=w_in, b_in=b_in, g_q=g_q, w_uq=w_uq, g_kv=g_kv, w_ukv=w_ukv, w_oa=w_oa, sgu_ln_g=sgu_ln_g,
             sgu_ln_b=sgu_ln_b, w_s=w_s, b_s=b_s, w_ob=w_ob, w_out=w_out, ln_g=ln_g, ln_b=ln_b)
    m = dict(w_in=m_w_in, b_in=m_b_in, g_q=m_g_q, w_uq=m_w_uq, g_kv=m_g_kv, w_ukv=m_w_ukv, w_oa=m_w_oa,
             sgu_ln_g=m_sgu_ln_g, sgu_ln_b=m_sgu_ln_b, w_s=m_w_s, b_s=m_b_s, w_ob=m_w_ob, w_out=m_w_out, ln_g=m_ln_g,
             ln_b=m_ln_b)
    v = dict(w_in=v_w_in, b_in=v_b_in, g_q=v_g_q, w_uq=v_w_uq, g_kv=v_g_kv, w_ukv=v_w_ukv, w_oa=v_w_oa,
             sgu_ln_g=v_sgu_ln_g, sgu_ln_b=v_sgu_ln_b, w_s=v_w_s, b_s=v_b_s, w_ob=v_w_ob, w_out=v_w_out, ln_g=v_ln_g,
             ln_b=v_ln_b)
    w, m, v = ({n: a[0] for n, a in d.items()} for d in (w, m, v))
    c = lax.axis_index("c")

    g_in, g_uq, g_oa, g_ob, g_out = _gather_weights(
        [w["w_in"], w["w_uq"].reshape(Q_RANK // 4, HEADS * QK_DIM), w["w_oa"], w["w_ob"], w["w_out"]])
    full_in = jnp.transpose(g_in, (1, 0, 2)).reshape(D_MODEL, IN_W)
    full_uq = g_uq.reshape(Q_RANK, HEADS, QK_DIM)
    full_oa = jnp.transpose(g_oa, (1, 0, 2)).reshape(MLA_W, D_MODEL)
    full_ob = jnp.transpose(g_ob, (1, 0, 2)).reshape(SGU_W, D_MODEL)
    full_out = g_out.reshape(D_MODEL, D_MODEL)

    loss, dx, grads = _local_step(
        x[0], positions[0], loss_target[0], full_in, w["b_in"], w["g_q"], full_uq, w["g_kv"], w["w_ukv"], full_oa,
        w["sgu_ln_g"], w["sgu_ln_b"], w["w_s"], w["b_s"], full_ob, full_out, w["ln_g"], w["ln_b"])
    loss = lax.psum(loss, ("x", "y", "c"))

    rep = jnp.concatenate([grads[n].reshape(-1) for n in REPLICATED])
    rep = jnp.pad(rep, (0, N_CHIPS * REP_ROWS * D_MODEL - rep.shape[0])).reshape(N_CHIPS, REP_ROWS, D_MODEL)
    slabs = [jnp.concatenate([_rows(_shard_of(n, grads[n], k)) for n in SHARDED] + [rep[k]], axis=0)
             for k in range(N_CHIPS)]
    part = jnp.stack(slabs).astype(BF16)
    pair = _pair_add(part, _swap_halves(part), c.reshape(1).astype(jnp.int32))
    g_half = _sum_chips(_scatter_chips(pair))
    g_slab, g_rep = _share_reduced(g_half)

    red = {}
    off = 0
    for n in SHARDED:
        rows = w[n].size // D_MODEL
        red[n] = g_slab[off:off + rows].reshape(w[n].shape)
        off += rows
    flat = g_rep.reshape(-1)
    off = 0
    for n in REPLICATED:
        red[n] = flat[off:off + w[n].size].reshape(w[n].shape)
        off += w[n].size

    d_in, m_in, v_in = _update_tiled(w["w_in"], red["w_in"], m["w_in"], v["w_in"], 128)
    small = [n for n in ORDER if n != "w_in"]
    as2d = lambda a: a.reshape(-1, a.shape[-1])
    ds, ms, vs = _update_small([as2d(w[n]) for n in small], [as2d(red[n]) for n in small],
                               [as2d(m[n]) for n in small], [as2d(v[n]) for n in small])
    delta, new_m, new_v = {"w_in": d_in}, {"w_in": m_in}, {"w_in": v_in}
    for i, n in enumerate(small):
        delta[n], new_m[n], new_v[n] = (a[i].reshape(w[n].shape) for a in (ds, ms, vs))

    lead = lambda a: a[None]
    return (loss, dx[None], *[lead(red[n]) for n in ORDER], *[lead(delta[n]) for n in ORDER],
            *[lead(new_m[n]) for n in ORDER], *[lead(new_v[n]) for n in ORDER])
```

```python
import functools
import math

import jax
import jax.numpy as jnp
from jax import lax
from jax.experimental import pallas as pl
from jax.experimental.pallas import tpu as pltpu

F32 = jnp.float32
BF16 = jnp.bfloat16

D_MODEL = 1024
HEADS = 8
Q_RANK = 384
KV_RANK = 128
NOPE = 64
ROPE = 32
V_DIM = 64
QK_DIM = NOPE + ROPE
HEAD_PAD = 128
MLA_W = HEADS * V_DIM
SGU_W = 512
GROUPS = 8
CHUNK = 128
IN_W = 4640
RMS_EPS = 1e-6
LN_EPS = 1e-5
ALPHA = 2.0 ** 0.25
ROPE_THETA = 10000.0
SCALE = QK_DIM ** -0.5

GATE_W = 2 * D_MODEL
MID_W = 4 * SGU_W
LAT_W = Q_RANK + KV_RANK + HEAD_PAD
PAD_W = GATE_W + MID_W + LAT_W
N_SLABS = 4
SLAB_W = D_MODEL // N_SLABS

ROW_TILE = 256
ATT_TQ = 256
ATT_TK = 256
ATT_BWD_TQ = 256
ATT_BWD_TK = 256
LOG2E = 1.4426950408889634
LN2 = 0.6931471805599453
Q_SCALE = SCALE * LOG2E
VMEM_LIMIT = 56 * 1024 * 1024

ADAM_LR = 0.001
ADAM_B1 = 0.9
ADAM_B2 = 0.999
ADAM_EPS = 1e-08
ADAM_WD = 0.01
ADAM_STEP = 10


def _dot(a, b):
    return jnp.dot(a, b, preferred_element_type=F32)


def _dot_nt(a, b):
    return lax.dot_general(a, b, (((1,), (1,)), ((), ())), preferred_element_type=F32)


def _dot_tn(a, b):
    return lax.dot_general(a, b, (((0,), (0,)), ((), ())), preferred_element_type=F32)


def _sigmoid(z):
    return 1.0 / (1.0 + jnp.exp(-z))


_GELU_C = math.sqrt(2.0 / math.pi)


def _gelu_and_grad(x):
    x2 = x * x
    t = jnp.tanh(_GELU_C * (x + 0.044715 * x * x2))
    g = 0.5 * x * (1.0 + t)
    dg = 0.5 * (1.0 + t) + 0.5 * x * (1.0 - t * t) * (_GELU_C * (1.0 + 3.0 * 0.044715 * x2))
    return g, dg


def _silu_and_grad(z):
    s = _sigmoid(z)
    return z * s, s * (1.0 + z * (1.0 - s))


def _rope(xb, c, sl, sh):
    return xb * c + pltpu.roll(xb, 112, 1) * sl + pltpu.roll(xb, 16, 1) * sh


def _rope_t(dy, c, sl, sh):
    return dy * c + pltpu.roll(dy * sl, 16, 1) + pltpu.roll(dy * sh, 112, 1)


def _params(sem=("arbitrary",)):
    return pltpu.CompilerParams(dimension_semantics=sem, vmem_limit_bytes=VMEM_LIMIT)


def _row_spec(tile, width):
    return pl.BlockSpec((tile, width), lambda i: (i, 0))


def _full_spec(shape):
    nd = len(shape)
    return pl.BlockSpec(shape, lambda i: (0,) * nd)


def _fwd_pre(x, w_pad, b_pad, g_q, wuq, g_kv, wk, wv, rc, rsl, rsh):
    s = x.shape[0]
    ts = ROW_TILE

    def body(x_ref, w_ref, b_ref, gq_ref, wuq_ref, gkv_ref, wk_ref, wv_ref, rc_ref, rsl_ref, rsh_ref,
             hg_ref, hm_ref, hl_ref, q_ref, k_ref, v_ref, xt_ref, qt_ref, kt_ref, vt_ref):
        xb = x_ref[...].astype(BF16)
        xt_ref[...] = xb.T
        h = _dot(xb, w_ref[...]) + b_ref[...]
        hg_ref[...] = h[:, :GATE_W]
        hm_ref[...] = h[:, GATE_W:GATE_W + MID_W]
        hl = h[:, GATE_W + MID_W:]
        hl_ref[...] = hl
        c, sl, sh = rc_ref[...], rsl_ref[...], rsh_ref[...]
        cq = hl[:, :Q_RANK]
        cqn = cq * lax.rsqrt(jnp.mean(cq * cq, axis=-1, keepdims=True) + RMS_EPS) * gq_ref[...]
        q = _dot(cqn.astype(BF16), wuq_ref[...])
        ckv = hl[:, Q_RANK:Q_RANK + KV_RANK]
        ckvn = (ckv * lax.rsqrt(jnp.mean(ckv * ckv, axis=-1, keepdims=True) + RMS_EPS) * gkv_ref[...]).astype(BF16)
        k = _dot(ckvn, wk_ref[...])
        vb = _dot(ckvn, wv_ref[...]).astype(BF16)
        v_ref[...] = vb
        vt_ref[...] = vb.T
        kpe = _rope(hl[:, Q_RANK + KV_RANK:], c, sl, sh)
        for hd in range(HEADS):
            lanes = slice(hd * HEAD_PAD, (hd + 1) * HEAD_PAD)
            qb = (_rope(q[:, lanes], c, sl, sh) * Q_SCALE).astype(BF16)
            kb = (k[:, lanes] + kpe).astype(BF16)
            q_ref[:, lanes] = qb
            k_ref[:, lanes] = kb
            qt_ref[lanes, :] = qb.T
            kt_ref[lanes, :] = kb.T

    qk_w = HEADS * HEAD_PAD
    col_spec = lambda rows: pl.BlockSpec((rows, ts), lambda i: (0, i))
    return pl.pallas_call(
        body, name="fwd_pre", grid=(s // ts,),
        in_specs=[_row_spec(ts, D_MODEL), _full_spec(w_pad.shape), _full_spec(b_pad.shape), _full_spec(g_q.shape),
                  _full_spec(wuq.shape), _full_spec(g_kv.shape), _full_spec(wk.shape), _full_spec(wv.shape),
                  _row_spec(ts, HEAD_PAD), _row_spec(ts, HEAD_PAD), _row_spec(ts, HEAD_PAD)],
        out_specs=[_row_spec(ts, GATE_W), _row_spec(ts, MID_W), _row_spec(ts, LAT_W), _row_spec(ts, qk_w),
                   _row_spec(ts, qk_w), _row_spec(ts, MLA_W), col_spec(D_MODEL), col_spec(qk_w), col_spec(qk_w),
                   col_spec(MLA_W)],
        out_shape=[jax.ShapeDtypeStruct((s, GATE_W), F32), jax.ShapeDtypeStruct((s, MID_W), F32),
                   jax.ShapeDtypeStruct((s, LAT_W), F32), jax.ShapeDtypeStruct((s, qk_w), BF16),
                   jax.ShapeDtypeStruct((s, qk_w), BF16), jax.ShapeDtypeStruct((s, MLA_W), BF16),
                   jax.ShapeDtypeStruct((D_MODEL, s), BF16), jax.ShapeDtypeStruct((qk_w, s), BF16),
                   jax.ShapeDtypeStruct((qk_w, s), BF16), jax.ShapeDtypeStruct((MLA_W, s), BF16)],
        compiler_params=_params(),
    )(x, w_pad, b_pad, g_q, wuq, g_kv, wk, wv, rc, rsl, rsh)


def _attn_fwd(qt, k, vt):
    s = k.shape[0]
    tq, tk = ATT_TQ, ATT_TK
    r = tq // tk
    pairs = HEADS // 2

    def body(qt_ref, k_ref, vt_ref, o_ref, lse_ref):
        i = pl.program_id(1)
        krow = lax.broadcasted_iota(jnp.int32, (tk, tq), 0)
        qcol = lax.broadcasted_iota(jnp.int32, (tk, tq), 1)
        qts = [qt_ref[hh * HEAD_PAD:(hh + 1) * HEAD_PAD, :] for hh in range(2)]

        def scores(j):
            koff = pl.multiple_of(j * tk, tk)
            return tuple(_dot(k_ref[pl.ds(koff, tk), hh * HEAD_PAD:(hh + 1) * HEAD_PAD], qts[hh]) for hh in range(2))

        def weighted(j, ps):
            koff = pl.multiple_of(j * tk, tk)
            return tuple(_dot(vt_ref[hh * V_DIM:(hh + 1) * V_DIM, pl.ds(koff, tk)], ps[hh]) for hh in range(2))

        def step(j, carry, diag, last):
            st, ps, stats = carry
            st_next = None if last else scores(j + 1)
            pvs = weighted(jnp.maximum(j - 1, 0), ps)
            new_ps, new_stats = [], []
            for hh in range(2):
                m, l, acc = stats[hh]
                s_ = st[hh]
                if diag is not None:
                    s_ = jnp.where(krow + diag * tk <= qcol, s_, -jnp.inf)
                m_new = jnp.maximum(m, jnp.max(s_, axis=0, keepdims=True))
                a = jnp.exp2(m - m_new)
                p = jnp.exp2(s_ - m_new)
                new_stats.append((m_new, a * l + jnp.sum(p, axis=0, keepdims=True), a * (acc + pvs[hh])))
                new_ps.append(p.astype(BF16))
            return st_next, tuple(new_ps), tuple(new_stats)

        one = (jnp.full((1, tq), -jnp.inf, F32), jnp.zeros((1, tq), F32), jnp.zeros((V_DIM, tq), F32))
        zero_p = jnp.zeros((tk, tq), BF16)
        nfull = i * r
        carry = lax.fori_loop(0, nfull, functools.partial(step, diag=None, last=False),
                              (scores(0), (zero_p, zero_p), (one, one)))
        for d in range(r):
            carry = step(nfull + d, carry, d, d == r - 1)
        _, ps, stats = carry
        pvs = weighted(nfull + r - 1, ps)
        ot = jnp.concatenate([(stats[hh][2] + pvs[hh]) / stats[hh][1] for hh in range(2)], axis=0)
        o_ref[...] = ot.T
        lse = [stats[hh][0] + jnp.log(stats[hh][1]) * LOG2E for hh in range(2)]
        lse_ref[...] = jnp.concatenate(lse + [jnp.zeros((6, tq), F32)], axis=0)

    return pl.pallas_call(
        body, name="attn_fwd", grid=(pairs, s // tq),
        in_specs=[pl.BlockSpec((2 * HEAD_PAD, tq), lambda p, i: (p, i)),
                  pl.BlockSpec((s, 2 * HEAD_PAD), lambda p, i: (0, p)),
                  pl.BlockSpec((2 * V_DIM, s), lambda p, i: (p, 0))],
        out_specs=[pl.BlockSpec((tq, 2 * V_DIM), lambda p, i: (i, p)),
                   pl.BlockSpec((None, 8, tq), lambda p, i: (p, 0, i))],
        out_shape=[jax.ShapeDtypeStruct((s, MLA_W), F32), jax.ShapeDtypeStruct((pairs, 8, s), F32)],
        compiler_params=_params(("arbitrary", "arbitrary")),
    )(qt, k, vt)


def _attn_bwd(q, qt, k, kt, v, do, dot, lse, delta):
    s = k.shape[0]
    tq, tk = ATT_BWD_TQ, ATT_BWD_TK
    r = tq // tk
    nq = s // tq
    nk = s // tk
    pairs = HEADS // 2

    def body(q_ref, qt_ref, k_ref, kt_ref, v_ref, do_ref, dot_ref, lse_ref, dl_ref, dqt_ref, dk_ref, dv_ref):
        j = pl.program_id(1)
        krow = lax.broadcasted_iota(jnp.int32, (tk, tq), 0)
        qcol = lax.broadcasted_iota(jnp.int32, (tk, tq), 1)
        lane = lax.broadcasted_iota(jnp.int32, (tk, 2 * V_DIM), 1)
        drow = lax.broadcasted_iota(jnp.int32, (2 * V_DIM, tq), 0)

        @pl.when(j == 0)
        def _():
            dqt_ref[...] = jnp.zeros_like(dqt_ref)

        koff = pl.multiple_of(j * tk, tk)
        vb = v_ref[pl.ds(koff, tk), :]
        kbs = [k_ref[pl.ds(koff, tk), hh * HEAD_PAD:(hh + 1) * HEAD_PAD] for hh in range(2)]
        ktbs = [kt_ref[hh * HEAD_PAD:(hh + 1) * HEAD_PAD, pl.ds(koff, tk)] for hh in range(2)]
        i0 = j // r

        def front(i):
            qoff = pl.multiple_of(i * tq, tq)
            dotb = dot_ref[:, pl.ds(qoff, tq)]
            out = []
            for hh in range(2):
                mine = (drow < V_DIM) if hh == 0 else (drow >= V_DIM)
                st = _dot(kbs[hh], qt_ref[hh * HEAD_PAD:(hh + 1) * HEAD_PAD, pl.ds(qoff, tq)])
                out.append((st, _dot(vb, jnp.where(mine, dotb, jnp.zeros_like(dotb)))))
            return tuple(out)

        def middle(i, tiles, diag):
            qoff = pl.multiple_of(i * tq, tq)
            out = []
            for hh in range(2):
                st, dpt = tiles[hh]
                if diag:
                    st = jnp.where(krow + (j - i0 * r) * tk <= qcol, st, -jnp.inf)
                p = jnp.exp2(st - lse_ref[hh:hh + 1, pl.ds(qoff, tq)])
                out.append((p.astype(BF16), (p * (dpt - dl_ref[hh:hh + 1, pl.ds(qoff, tq)])).astype(BF16)))
            return tuple(out)

        def back(i, pd, accs):
            qoff = pl.multiple_of(i * tq, tq)
            dob = do_ref[pl.ds(qoff, tq), :]
            out = []
            for hh in range(2):
                rows = slice(hh * HEAD_PAD, (hh + 1) * HEAD_PAD)
                p, dst = pd[hh]
                dk_acc, dv_acc = accs[hh]
                dv_acc = dv_acc + _dot(p, dob)
                dk_acc = dk_acc + _dot(dst, q_ref[pl.ds(qoff, tq), rows])
                dqt_ref[rows, pl.ds(qoff, tq)] += _dot(ktbs[hh], dst)
                out.append((dk_acc, dv_acc))
            return tuple(out)

        def step(i, accs, diag):
            return back(i, middle(i, front(i), diag), accs)

        zero_acc = (jnp.zeros((tk, HEAD_PAD), F32), jnp.zeros((tk, 2 * V_DIM), F32))
        accs = step(i0, (zero_acc, zero_acc), True)
        accs = lax.fori_loop(i0 + 1, nq, functools.partial(step, diag=False), accs)
        for hh in range(2):
            dk_ref[:, hh * HEAD_PAD:(hh + 1) * HEAD_PAD] = accs[hh][0] * LN2
        dv_ref[...] = jnp.where(lane < V_DIM, accs[0][1], accs[1][1])

        @pl.when(j == nk - 1)
        def _():
            dqt_ref[...] = dqt_ref[...] * SCALE

    pair_rows = lambda w: pl.BlockSpec((s, w), lambda p, j: (0, p))
    pair_cols = lambda w: pl.BlockSpec((w, s), lambda p, j: (p, 0))
    stats = pl.BlockSpec((None, 8, s), lambda p, j: (p, 0, 0))
    return pl.pallas_call(
        body, name="attn_bwd", grid=(pairs, nk),
        in_specs=[pair_rows(2 * HEAD_PAD), pair_cols(2 * HEAD_PAD), pair_rows(2 * HEAD_PAD), pair_cols(2 * HEAD_PAD),
                  pair_rows(2 * V_DIM), pair_rows(2 * V_DIM), pair_cols(2 * V_DIM), stats, stats],
        out_specs=[pair_cols(2 * HEAD_PAD),
                   pl.BlockSpec((tk, 2 * HEAD_PAD), lambda p, j: (j, p)),
                   pl.BlockSpec((tk, 2 * V_DIM), lambda p, j: (j, p))],
        out_shape=[jax.ShapeDtypeStruct((HEADS * HEAD_PAD, s), F32), jax.ShapeDtypeStruct((s, HEADS * HEAD_PAD), F32),
                   jax.ShapeDtypeStruct((s, MLA_W), F32)],
        compiler_params=_params(("arbitrary", "arbitrary")),
    )(q, qt, k, kt, v, do, dot, lse, delta)


def _split3(a):
    hi = a.astype(BF16)
    r1 = a - hi.astype(F32)
    mid = r1.astype(BF16)
    lo = (r1 - mid.astype(F32)).astype(BF16)
    return hi, mid, lo


def _mid(x, tgt, o, hm, hg, woa, wob, wout, ln_g, ln_b, sg_g, sg_b, w_s, bsb):
    s = x.shape[0]
    ts = ROW_TILE
    nsteps = s // ts
    nch = ts // CHUNK
    npair = GROUPS // 2

    def body(x_ref, t_ref, o_ref, hm_ref, hg_ref, woa_ref, wob_ref, wout_ref, lng_ref, lnb_ref, sgg_ref, sgb_ref,
             ws_ref, bsb_ref,
             dr_ref, dhg_ref, dhm_ref, do_ref, dot_ref, dl_ref,
             dwout_ref, dwoa_ref, dwob_ref, dws_ref, dbs_ref, dlng_ref, dlnb_ref, dsgg_ref, dsgb_ref, loss_ref,
             dbacc_ref):
        i = pl.program_id(0)

        @pl.when(i == 0)
        def _():
            for r in (dwout_ref, dwoa_ref, dwob_ref, dws_ref, dlng_ref, dlnb_ref, dsgg_ref, dsgb_ref, loss_ref,
                      dbacc_ref):
                r[...] = jnp.zeros_like(r)

        lane = lax.broadcasted_iota(jnp.int32, (CHUNK, CHUNK), 1)
        left = lane < V_DIM
        tril = lax.broadcasted_iota(jnp.int32, (CHUNK, CHUNK), 0) >= lane
        ms = [jnp.where(tril, ws_ref[g], 0.0).astype(BF16) for g in range(GROUPS)]

        z_a = hm_ref[:, 0:SGU_W]
        u = hm_ref[:, SGU_W:2 * SGU_W]
        v = hm_ref[:, 2 * SGU_W:3 * SGU_W]
        z_b = hm_ref[:, 3 * SGU_W:4 * SGU_W]
        o = o_ref[...]
        sa, dsa = _silu_and_grad(z_a)
        y_a = (o * sa).astype(BF16)
        gu, dgu = _gelu_and_grad(u)
        gv, dgv = _gelu_and_grad(v)
        mu = jnp.mean(gv, axis=-1, keepdims=True)
        vc = gv - mu
        rstd_v = lax.rsqrt(jnp.mean(vc * vc, axis=-1, keepdims=True) + LN_EPS)
        vhat = vc * rstd_v
        vn = (vhat * sgg_ref[...] + sgb_ref[...]).astype(BF16)
        rows = []
        for c in range(nch):
            blocks = []
            for p in range(npair):
                blk = vn[c * CHUNK:(c + 1) * CHUNK, p * CHUNK:(p + 1) * CHUNK]
                blocks.append(jnp.where(left, _dot(ms[2 * p], blk), _dot(ms[2 * p + 1], blk)))
            rows.append(jnp.concatenate(blocks, axis=1) + bsb_ref[...])
        mixed = jnp.concatenate(rows, axis=0)
        sgu = gu * mixed
        sb, dsb = _silu_and_grad(z_b)
        y_b = (sgu * sb).astype(BF16)
        pa = jnp.concatenate([_dot(y_a, woa_ref[k]) for k in range(N_SLABS)], axis=1)
        pb = jnp.concatenate([_dot(y_b, wob_ref[k]) for k in range(N_SLABS)], axis=1)
        sga = _sigmoid(hg_ref[:, :D_MODEL])
        sgb = _sigmoid(hg_ref[:, D_MODEL:])
        m2 = (sga * pa + sgb * pb).astype(BF16)
        r = ALPHA * x_ref[...] + _dot(m2, wout_ref[...])
        rmu = jnp.mean(r, axis=-1, keepdims=True)
        rc = r - rmu
        rstd = lax.rsqrt(jnp.mean(rc * rc, axis=-1, keepdims=True) + LN_EPS)
        xhat = rc * rstd
        y = xhat * lng_ref[...] + lnb_ref[...]
        err = y - t_ref[...]
        loss_ref[...] += jnp.full(loss_ref.shape, 0.5 / D_MODEL, F32) * jnp.sum(err * err)

        dy = err * (1.0 / D_MODEL)
        dlng_ref[...] += jnp.sum(dy * xhat, axis=0, keepdims=True)
        dlnb_ref[...] += jnp.sum(dy, axis=0, keepdims=True)
        dxh = dy * lng_ref[...]
        dr = rstd * (dxh - jnp.mean(dxh, axis=-1, keepdims=True) - xhat * jnp.mean(dxh * xhat, axis=-1, keepdims=True))
        dr_ref[...] = dr
        drb = dr.astype(BF16)
        dwout_ref[...] += _dot_tn(m2, drb)
        dm2 = _dot_nt(drb, wout_ref[...])
        dhg_ref[:, :D_MODEL] = (dm2 * pa * sga * (1.0 - sga)).astype(BF16)
        dhg_ref[:, D_MODEL:] = (dm2 * pb * sgb * (1.0 - sgb)).astype(BF16)
        dpa = (dm2 * sga).astype(BF16)
        dpb = (dm2 * sgb).astype(BF16)
        dy_a = jnp.zeros((ts, MLA_W), F32)
        dy_b = jnp.zeros((ts, SGU_W), F32)
        for k in range(N_SLABS):
            cols = slice(k * SLAB_W, (k + 1) * SLAB_W)
            dwoa_ref[k] += _dot_tn(y_a, dpa[:, cols])
            dwob_ref[k] += _dot_tn(y_b, dpb[:, cols])
            dy_a = dy_a + _dot_nt(dpa[:, cols], woa_ref[k])
            dy_b = dy_b + _dot_nt(dpb[:, cols], wob_ref[k])
        dob = (dy_a * sa).astype(BF16)
        do_ref[...] = dob
        dot_ref[...] = dob.T
        head = (lax.broadcasted_iota(jnp.int32, (HEADS, MLA_W), 1) // V_DIM
                == lax.broadcasted_iota(jnp.int32, (HEADS, MLA_W), 0)).astype(BF16)
        dl_ref[...] = sum(_dot_nt(head, term) for term in _split3(dob.astype(F32) * o))
        dhm_ref[:, 0:SGU_W] = (dy_a * o * dsa).astype(BF16)
        dsg = dy_b * sb
        dhm_ref[:, 3 * SGU_W:4 * SGU_W] = (dy_b * sgu * dsb).astype(BF16)
        dhm_ref[:, SGU_W:2 * SGU_W] = (dsg * mixed * dgu).astype(BF16)
        dmixed = dsg * gu
        dvn_rows = []
        dbs_sum = jnp.zeros((CHUNK, SGU_W), F32)
        for c in range(nch):
            dm_c = dmixed[c * CHUNK:(c + 1) * CHUNK, :]
            dbs_sum = dbs_sum + dm_c
            blocks = []
            for p in range(npair):
                dmb = dm_c[:, p * CHUNK:(p + 1) * CHUNK].astype(BF16)
                blk = vn[c * CHUNK:(c + 1) * CHUNK, p * CHUNK:(p + 1) * CHUNK]
                blocks.append(jnp.where(left, _dot_tn(ms[2 * p], dmb), _dot_tn(ms[2 * p + 1], dmb)))
                zero = jnp.zeros_like(dmb)
                dws_ref[2 * p] += jnp.where(tril, _dot_nt(jnp.where(left, dmb, zero), blk), 0.0)
                dws_ref[2 * p + 1] += jnp.where(tril, _dot_nt(jnp.where(left, zero, dmb), blk), 0.0)
            dvn_rows.append(jnp.concatenate(blocks, axis=1))
        dbacc_ref[...] += dbs_sum
        dvn = jnp.concatenate(dvn_rows, axis=0)
        dsgg_ref[...] += jnp.sum(dvn * vhat, axis=0, keepdims=True)
        dsgb_ref[...] += jnp.sum(dvn, axis=0, keepdims=True)
        dvh = dvn * sgg_ref[...]
        dgv_in = rstd_v * (dvh - jnp.mean(dvh, axis=-1, keepdims=True)
                           - vhat * jnp.mean(dvh * vhat, axis=-1, keepdims=True))
        dhm_ref[:, 2 * SGU_W:3 * SGU_W] = (dgv_in * dgv).astype(BF16)

        @pl.when(i == nsteps - 1)
        def _():
            grp = (lax.broadcasted_iota(jnp.int32, (SGU_W, CHUNK), 0) // V_DIM
                   == lax.broadcasted_iota(jnp.int32, (SGU_W, CHUNK), 1)).astype(BF16)
            hi, mid, lo = _split3(dbacc_ref[...])
            dbs_ref[...] = _dot(hi, grp) + _dot(mid, grp) + _dot(lo, grp)

    acc_shapes = [(D_MODEL, D_MODEL), woa.shape, wob.shape, (GROUPS, CHUNK, CHUNK), (CHUNK, CHUNK),
                  (1, D_MODEL), (1, D_MODEL), (1, SGU_W), (1, SGU_W), (1, 128)]
    return pl.pallas_call(
        body, name="mid", grid=(nsteps,),
        in_specs=[_row_spec(ts, D_MODEL), _row_spec(ts, D_MODEL), _row_spec(ts, MLA_W), _row_spec(ts, MID_W),
                  _row_spec(ts, GATE_W), _full_spec(woa.shape), _full_spec(wob.shape), _full_spec(wout.shape),
                  _full_spec(ln_g.shape), _full_spec(ln_b.shape), _full_spec(sg_g.shape), _full_spec(sg_b.shape),
                  _full_spec(w_s.shape), _full_spec(bsb.shape)],
        out_specs=[_row_spec(ts, D_MODEL), _row_spec(ts, GATE_W), _row_spec(ts, MID_W), _row_spec(ts, MLA_W),
                   pl.BlockSpec((MLA_W, ts), lambda i: (0, i)), pl.BlockSpec((HEADS, ts), lambda i: (0, i))]
        + [_full_spec(sh) for sh in acc_shapes],
        out_shape=[jax.ShapeDtypeStruct((s, D_MODEL), F32), jax.ShapeDtypeStruct((s, GATE_W), BF16),
                   jax.ShapeDtypeStruct((s, MID_W), BF16), jax.ShapeDtypeStruct((s, MLA_W), BF16),
                   jax.ShapeDtypeStruct((MLA_W, s), BF16), jax.ShapeDtypeStruct((HEADS, s), F32)]
        + [jax.ShapeDtypeStruct(sh, F32) for sh in acc_shapes],
        scratch_shapes=[pltpu.VMEM((CHUNK, SGU_W), F32)],
        compiler_params=_params(),
    )(x, tgt, o, hm, hg, woa, wob, wout, ln_g, ln_b, sg_g, sg_b, w_s, bsb)


def _lat_bwd(dq, dk, dv, hl, rc, rsl, rsh, g_q, g_kv, wuq, wk, wv):
    s = dk.shape[0]
    ts = ROW_TILE
    qk_w = HEADS * HEAD_PAD

    def body(dq_ref, dk_ref, dv_ref, hl_ref, rc_ref, rsl_ref, rsh_ref, gq_ref, gkv_ref, wuq_ref, wk_ref, wv_ref,
             dhl_ref, dwuq_ref, dwk_ref, dwv_ref, dgq_ref, dgkv_ref):
        i = pl.program_id(0)

        @pl.when(i == 0)
        def _():
            for r in (dwuq_ref, dwk_ref, dwv_ref, dgq_ref, dgkv_ref):
                r[...] = jnp.zeros_like(r)

        c, sl, sh = rc_ref[...], rsl_ref[...], rsh_ref[...]
        lane = lax.broadcasted_iota(jnp.int32, (ts, HEAD_PAD), 1)
        pe = (lane >= NOPE) & (lane < QK_DIM)
        dkpe = jnp.zeros((ts, HEAD_PAD), F32)
        dqu = []
        for hd in range(HEADS):
            lanes = slice(hd * HEAD_PAD, (hd + 1) * HEAD_PAD)
            dqu.append(_rope_t(dq_ref[lanes, :].T, c, sl, sh).astype(BF16))
            dkpe = dkpe + dk_ref[:, lanes]
        dqu = jnp.concatenate(dqu, axis=1)
        dkpe = _rope_t(jnp.where(pe, dkpe, 0.0), c, sl, sh)

        cq = hl_ref[:, :Q_RANK]
        rq = lax.rsqrt(jnp.mean(cq * cq, axis=-1, keepdims=True) + RMS_EPS)
        cqh = cq * rq
        cqn = (cqh * gq_ref[...]).astype(BF16)
        dwuq_ref[...] += _dot_tn(cqn, dqu)
        dcqn = _dot_nt(dqu, wuq_ref[...])
        dgq_ref[...] += jnp.sum(dcqn * cqh, axis=0, keepdims=True)
        dch = dcqn * gq_ref[...]
        dhl_ref[:, :Q_RANK] = (rq * (dch - cqh * jnp.mean(dch * cqh, axis=-1, keepdims=True))).astype(BF16)

        ckv = hl_ref[:, Q_RANK:Q_RANK + KV_RANK]
        rk = lax.rsqrt(jnp.mean(ckv * ckv, axis=-1, keepdims=True) + RMS_EPS)
        ckh = ckv * rk
        ckn = (ckh * gkv_ref[...]).astype(BF16)
        dkb = dk_ref[...].astype(BF16)
        dvb = dv_ref[...].astype(BF16)
        dwk_ref[...] += _dot_tn(ckn, dkb)
        dwv_ref[...] += _dot_tn(ckn, dvb)
        dckn = _dot_nt(dkb, wk_ref[...]) + _dot_nt(dvb, wv_ref[...])
        dgkv_ref[...] += jnp.sum(dckn * ckh, axis=0, keepdims=True)
        dkh = dckn * gkv_ref[...]
        dhl_ref[:, Q_RANK:Q_RANK + KV_RANK] = (rk * (dkh - ckh * jnp.mean(dkh * ckh, axis=-1, keepdims=True))).astype(BF16)
        dhl_ref[:, Q_RANK + KV_RANK:] = dkpe.astype(BF16)

    acc_shapes = [wuq.shape, wk.shape, wv.shape, g_q.shape, g_kv.shape]
    return pl.pallas_call(
        body, name="lat_bwd", grid=(s // ts,),
        in_specs=[pl.BlockSpec((qk_w, ts), lambda i: (0, i)), _row_spec(ts, qk_w), _row_spec(ts, MLA_W),
                  _row_spec(ts, LAT_W), _row_spec(ts, HEAD_PAD), _row_spec(ts, HEAD_PAD), _row_spec(ts, HEAD_PAD),
                  _full_spec(g_q.shape), _full_spec(g_kv.shape), _full_spec(wuq.shape), _full_spec(wk.shape),
                  _full_spec(wv.shape)],
        out_specs=[_row_spec(ts, LAT_W)] + [_full_spec(sh) for sh in acc_shapes],
        out_shape=[jax.ShapeDtypeStruct((s, LAT_W), BF16)] + [jax.ShapeDtypeStruct(sh, F32) for sh in acc_shapes],
        compiler_params=_params(),
    )(dq, dk, dv, hl, rc, rsl, rsh, g_q, g_kv, wuq, wk, wv)


def _dx(dr, dhg, dhm, dhl, w_pad):
    s = dr.shape[0]
    ts = ROW_TILE

    def body(dr_ref, dhg_ref, dhm_ref, dhl_ref, w_ref, dx_ref):
        dx_ref[...] = (ALPHA * dr_ref[...]
                       + _dot_nt(dhg_ref[...], w_ref[:, :GATE_W])
                       + _dot_nt(dhm_ref[...], w_ref[:, GATE_W:GATE_W + MID_W])
                       + _dot_nt(dhl_ref[...], w_ref[:, GATE_W + MID_W:]))

    return pl.pallas_call(
        body, name="dx", grid=(s // ts,),
        in_specs=[_row_spec(ts, D_MODEL), _row_spec(ts, GATE_W), _row_spec(ts, MID_W), _row_spec(ts, LAT_W),
                  _full_spec(w_pad.shape)],
        out_specs=_row_spec(ts, D_MODEL),
        out_shape=jax.ShapeDtypeStruct((s, D_MODEL), F32),
        compiler_params=_params(),
    )(dr, dhg, dhm, dhl, w_pad)


def _dw_in(xt, dh, tn, name):
    s, width = dh.shape

    def body(xt_ref, dh_ref, dw_ref, db_ref):
        dhb = dh_ref[...]
        dw_ref[...] = _dot(xt_ref[...], dhb).astype(BF16)
        db_ref[...] = jnp.sum(dhb.astype(F32), axis=0, keepdims=True)

    return pl.pallas_call(
        body, name=name, grid=(width // tn,),
        in_specs=[_full_spec(xt.shape), pl.BlockSpec((s, tn), lambda n: (0, n))],
        out_specs=[pl.BlockSpec((D_MODEL, tn), lambda n: (0, n)), pl.BlockSpec((1, tn), lambda n: (0, n))],
        out_shape=[jax.ShapeDtypeStruct((D_MODEL, width), BF16), jax.ShapeDtypeStruct((1, width), F32)],
        compiler_params=_params(),
    )(xt, dh)


_C_Q, _C_KV, _K_PE, _Z_A, _U, _V, _Z_B, _G_A, _G_B = (
    (0, 384), (384, 512), (512, 544), (544, 1056), (1056, 1568), (1568, 2080), (2080, 2592), (2592, 3616),
    (3616, 4640))


def _pad_in_cols(w):
    cut = lambda ab: w[:, ab[0]:ab[1]]
    z = lambda n: jnp.zeros((w.shape[0], n), w.dtype)
    return jnp.concatenate([cut(_G_A), cut(_G_B), cut(_Z_A), cut(_U), cut(_V), cut(_Z_B), cut(_C_Q), cut(_C_KV),
                            z(NOPE), cut(_K_PE), z(HEAD_PAD - QK_DIM)], axis=1)


def _unpad_in_cols(g, m, l):
    kpe = l[:, Q_RANK + KV_RANK + NOPE:Q_RANK + KV_RANK + QK_DIM]
    return jnp.concatenate([l[:, :Q_RANK + KV_RANK], kpe, m, g], axis=1)


def _rope_tables(positions):
    half = ROPE // 2
    inv_freq = ROPE_THETA ** (-jnp.arange(0, ROPE, 2, dtype=F32) / ROPE)
    ang = positions.astype(F32)[:, None] * inv_freq
    cos, sin = jnp.cos(ang), jnp.sin(ang)
    n = positions.shape[0]
    one, zero = jnp.ones((n, NOPE), F32), jnp.zeros((n, half), F32)
    tail1, tail0 = jnp.ones((n, HEAD_PAD - QK_DIM), F32), jnp.zeros((n, HEAD_PAD - QK_DIM), F32)
    z64 = jnp.zeros((n, NOPE), F32)
    rc = jnp.concatenate([one, cos, cos, tail1], axis=1)
    rsl = jnp.concatenate([z64, -sin, zero, tail0], axis=1)
    rsh = jnp.concatenate([z64, zero, sin, tail0], axis=1)
    return rc, rsl, rsh


def _local_step(x, positions, tgt, w_in, b_in, g_q, w_uq, g_kv, w_ukv, w_oa, sg_g, sg_b, w_s, b_s, w_ob, w_out,
                ln_g, ln_b):
    rc, rsl, rsh = _rope_tables(positions)
    w_pad = _pad_in_cols(w_in)
    b_pad = _pad_in_cols(b_in[None, :])
    wuq = jnp.pad(w_uq, ((0, 0), (0, 0), (0, HEAD_PAD - QK_DIM))).reshape(Q_RANK, HEADS * HEAD_PAD).astype(BF16)
    wk = jnp.pad(w_ukv[:, :, :NOPE], ((0, 0), (0, 0), (0, HEAD_PAD - NOPE))).reshape(KV_RANK, HEADS * HEAD_PAD).astype(BF16)
    wv = w_ukv[:, :, NOPE:].reshape(KV_RANK, MLA_W).astype(BF16)
    bsb = jnp.repeat(b_s.T, V_DIM, axis=1)
    gq2, gkv2 = g_q[None, :], g_kv[None, :]

    hg, hm, hl, q, k, v, xt, qt, kt, vt = _fwd_pre(x, w_pad, b_pad, gq2, wuq, gkv2, wk, wv, rc, rsl, rsh)
    o, lse = _attn_fwd(qt, k, vt)
    (dr, dhg, dhm, do, dot, delta, dwout, dwoa, dwob, dws, dbs, dlng, dlnb, dsgg, dsgb, loss) = _mid(
        x, tgt, o, hm, hg, w_oa, w_ob, w_out, ln_g[None, :], ln_b[None, :], sg_g[None, :], sg_b[None, :], w_s, bsb)
    delta = jnp.pad(delta.reshape(HEADS // 2, 2, -1), ((0, 0), (0, 6), (0, 0)))
    dq, dk, dv = _attn_bwd(q, qt, k, kt, v, do, dot, lse, delta)
    dhl, dwuq, dwk, dwv, dgq, dgkv = _lat_bwd(dq, dk, dv, hl, rc, rsl, rsh, gq2, gkv2, wuq, wk, wv)
    dx = _dx(dr, dhg, dhm, dhl, w_pad)
    dwg, dbg = _dw_in(xt, dhg, 512, "dw_in_gates")
    dwm, dbm = _dw_in(xt, dhm, 512, "dw_in_mid")
    dwl, dbl = _dw_in(xt, dhl, LAT_W, "dw_in_lat")
    grads = {
        "w_in": _unpad_in_cols(dwg, dwm, dwl),
        "b_in": _unpad_in_cols(dbg, dbm, dbl)[0],
        "g_q": dgq[0],
        "w_uq": dwuq.reshape(Q_RANK, HEADS, HEAD_PAD)[:, :, :QK_DIM],
        "g_kv": dgkv[0],
        "w_ukv": jnp.concatenate([dwk.reshape(KV_RANK, HEADS, HEAD_PAD)[:, :, :NOPE],
                                  dwv.reshape(KV_RANK, HEADS, V_DIM)], axis=2),
        "w_oa": dwoa, "sgu_ln_g": dsgg[0], "sgu_ln_b": dsgb[0], "w_s": dws, "b_s": dbs[:, :GROUPS].T,
        "w_ob": dwob, "w_out": dwout, "ln_g": dlng[0], "ln_b": dlnb[0],
    }
    return loss, dx, grads


MESH = pl.DeviceIdType.MESH
N_CHIPS = 4
HBM_SPEC = pl.BlockSpec(memory_space=pl.ANY)
VMEM_SPEC = pl.BlockSpec(memory_space=pltpu.VMEM)

REP_ROWS = 96


def _place():
    x, y, c = lax.axis_index("x"), lax.axis_index("y"), lax.axis_index("c")
    others = [(1 - x, y), (x, 1 - y), (1 - x, 1 - y)]
    return x, y, c, others


def _gather_weights(shards):
    n = len(shards)

    def body(*refs):
        ins, outs, bufs = refs[:n], refs[n:2 * n], refs[2 * n:3 * n]
        send_sems, recv_sems, local_sems = refs[3 * n:]
        x, y, c, others = _place()
        me = 2 * x + y
        sibling = (x, y, 1 - c)
        for src, buf in zip(ins, bufs):
            buf[...] = src[...].astype(BF16)
        own = [pltpu.make_async_copy(bufs[w], outs[w].at[me], local_sems.at[w]) for w in range(n)]
        for cp in own:
            cp.start()

        def part(w, chip, half):
            hr = shards[w].shape[0] // 2
            return outs[w].at[chip, pl.ds(half * hr, hr), :]

        def sent(w, j):
            hr = shards[w].shape[0] // 2
            return pltpu.make_async_remote_copy(
                src_ref=bufs[w].at[pl.ds(c * hr, hr), :], dst_ref=part(w, me, c),
                send_sem=send_sems.at[w * 3 + j], recv_sem=recv_sems.at[w * 3 + j],
                device_id=(*others[j], c), device_id_type=MESH)

        def landed(w, j):
            px, py = others[j]
            return pltpu.make_async_remote_copy(
                src_ref=part(w, 2 * px + py, c), dst_ref=part(w, 2 * px + py, c),
                send_sem=send_sems.at[w * 3 + j], recv_sem=recv_sems.at[w * 3 + j],
                device_id=(px, py, c), device_id_type=MESH)

        def passed(w, j, half):
            px, py = others[j]
            k = n * 3 + w * 3 + j
            return pltpu.make_async_remote_copy(
                src_ref=part(w, 2 * px + py, half), dst_ref=part(w, 2 * px + py, half),
                send_sem=send_sems.at[k], recv_sem=recv_sems.at[k], device_id=sibling, device_id_type=MESH)

        first = [sent(w, j) for w in range(n) for j in range(3)]
        for cp in first:
            cp.start()
        fwd = []
        for w in range(n):
            for j in range(3):
                landed(w, j).wait_recv()
                cp = passed(w, j, c)
                cp.start()
                fwd.append(cp)
        for w in range(n):
            for j in range(3):
                passed(w, j, 1 - c).wait_recv()
        for cp in first + fwd:
            cp.wait_send()
        for cp in own:
            cp.wait()

    return pl.pallas_call(
        body, name="gather_weights",
        in_specs=[VMEM_SPEC] * n, out_specs=[HBM_SPEC] * n,
        out_shape=[jax.ShapeDtypeStruct((N_CHIPS,) + s.shape, BF16) for s in shards],
        scratch_shapes=[pltpu.VMEM(s.shape, BF16) for s in shards]
        + [pltpu.SemaphoreType.DMA((6 * n,)), pltpu.SemaphoreType.DMA((6 * n,)), pltpu.SemaphoreType.DMA((n,))],
        compiler_params=pltpu.CompilerParams(vmem_limit_bytes=VMEM_LIMIT),
    )(*shards)


N_DEV = 8
LOSS_TILE = (8, 128)


def _half(a):
    return a.shape[1] // 2


def _exchange_pairs(parts, loss):
    n = len(parts)

    def body(*refs):
        p_refs, loss_ref = refs[:n], refs[n]
        r_refs, all_loss_ref = refs[n + 1:2 * n + 1], refs[2 * n + 1]
        send_sems, recv_sems, loss_send, loss_recv, local_sem = refs[2 * n + 2:]
        x, y, c, _ = _place()
        sibling = (x, y, 1 - c)
        cps = []
        for w in range(n):
            h = _half(parts[w])
            cps.append(pltpu.make_async_remote_copy(
                src_ref=p_refs[w].at[:, pl.ds((1 - c) * h, h), :], dst_ref=r_refs[w],
                send_sem=send_sems.at[w], recv_sem=recv_sems.at[w], device_id=sibling, device_id_type=MESH))
        for cp in cps:
            cp.start()
        me = 4 * x + 2 * y + c
        own = pltpu.make_async_copy(loss_ref, all_loss_ref.at[me], local_sem)
        own.start()
        lcs = []
        for t in range(1, N_DEV):
            d = (me + t) % N_DEV
            lcs.append(pltpu.make_async_remote_copy(
                src_ref=loss_ref, dst_ref=all_loss_ref.at[me], send_sem=loss_send.at[t - 1],
                recv_sem=loss_recv.at[t - 1], device_id=(d // 4, (d // 2) % 2, d % 2), device_id_type=MESH))
        for cp in lcs:
            cp.start()
        for t in range(1, N_DEV):
            d = (me + N_DEV - t) % N_DEV
            pltpu.make_async_remote_copy(
                src_ref=loss_ref, dst_ref=all_loss_ref.at[d], send_sem=loss_send.at[t - 1],
                recv_sem=loss_recv.at[t - 1], device_id=(d // 4, (d // 2) % 2, d % 2), device_id_type=MESH).wait_recv()
        for cp in lcs:
            cp.wait_send()
        for cp in cps:
            cp.wait()
        own.wait()

    return pl.pallas_call(
        body, name="exchange_pairs", in_specs=[HBM_SPEC] * (n + 1), out_specs=[HBM_SPEC] * (n + 1),
        out_shape=[jax.ShapeDtypeStruct((N_CHIPS, _half(p), p.shape[2]), BF16) for p in parts]
        + [jax.ShapeDtypeStruct((N_DEV,) + LOSS_TILE, F32)],
        scratch_shapes=[pltpu.SemaphoreType.DMA((n,)), pltpu.SemaphoreType.DMA((n,)),
                        pltpu.SemaphoreType.DMA((N_DEV - 1,)), pltpu.SemaphoreType.DMA((N_DEV - 1,)),
                        pltpu.SemaphoreType.DMA],
    )(*parts, loss)


def _add_pair_tiled(p, r, c, tile):
    nt = _half(p) // tile
    cols = p.shape[2]

    def body(c_ref, p_ref, r_ref, q_ref):
        q_ref[...] = (p_ref[...].astype(F32) + r_ref[...].astype(F32)).astype(BF16)

    return pl.pallas_call(
        body, name="add_pair_w_in",
        grid_spec=pltpu.PrefetchScalarGridSpec(
            num_scalar_prefetch=1, grid=(N_CHIPS, nt),
            in_specs=[pl.BlockSpec((None, tile, cols), lambda k, i, c_ref: (k, c_ref[0] * nt + i, 0)),
                      pl.BlockSpec((None, tile, cols), lambda k, i, c_ref: (k, i, 0))],
            out_specs=pl.BlockSpec((None, tile, cols), lambda k, i, c_ref: (k, i, 0))),
        out_shape=jax.ShapeDtypeStruct(r.shape, BF16),
    )(c, p, r)


def _add_pair_small(ps, rs, c):
    n = len(ps)

    def body(c_ref, *refs):
        for w in range(n):
            h = _half(ps[w])
            mine = refs[w][:, pl.ds(pl.multiple_of(c_ref[0] * h, 16), h), :]
            refs[2 * n + w][...] = (mine.astype(F32) + refs[n + w][...].astype(F32)).astype(BF16)

    return pl.pallas_call(
        body, name="add_pair_small",
        in_specs=[pl.BlockSpec(memory_space=pltpu.SMEM)] + [VMEM_SPEC] * (2 * n), out_specs=[VMEM_SPEC] * n,
        out_shape=[jax.ShapeDtypeStruct(r.shape, BF16) for r in rs],
        compiler_params=pltpu.CompilerParams(vmem_limit_bytes=VMEM_LIMIT),
    )(c, *ps, *rs)


def _exchange_chips(qs):
    n = len(qs)

    def body(*refs):
        q_refs, r_refs = refs[:n], refs[n:2 * n]
        send_sems, recv_sems, local_sems = refs[2 * n:]
        x, y, c, others = _place()
        me = 2 * x + y
        own = [pltpu.make_async_copy(q_refs[w].at[me], r_refs[w].at[me], local_sems.at[w]) for w in range(n)]
        for cp in own:
            cp.start()
        cps = []
        for w in range(n):
            for j, (px, py) in enumerate(others):
                cps.append(pltpu.make_async_remote_copy(
                    src_ref=q_refs[w].at[2 * px + py], dst_ref=r_refs[w].at[me], send_sem=send_sems.at[3 * w + j],
                    recv_sem=recv_sems.at[3 * w + j], device_id=(px, py, c), device_id_type=MESH))
        for cp in cps:
            cp.start()
        for w in range(n):
            for j, (px, py) in enumerate(others):
                pltpu.make_async_remote_copy(
                    src_ref=q_refs[w].at[me], dst_ref=r_refs[w].at[2 * px + py], send_sem=send_sems.at[3 * w + j],
                    recv_sem=recv_sems.at[3 * w + j], device_id=(px, py, c), device_id_type=MESH).wait_recv()
        for cp in cps:
            cp.wait_send()
        for cp in own:
            cp.wait()

    return pl.pallas_call(
        body, name="exchange_chips", in_specs=[HBM_SPEC] * n, out_specs=[HBM_SPEC] * n,
        out_shape=[jax.ShapeDtypeStruct(q.shape, BF16) for q in qs],
        scratch_shapes=[pltpu.SemaphoreType.DMA((3 * n,)), pltpu.SemaphoreType.DMA((3 * n,)),
                        pltpu.SemaphoreType.DMA((n,))],
    )(*qs)


def _sum_chips_tiled(r, tile):
    rows, cols = r.shape[1:]

    def body(r_ref, g_ref):
        acc = r_ref[0].astype(F32)
        for k in range(1, N_CHIPS):
            acc = acc + r_ref[k].astype(F32)
        g_ref[...] = acc

    return pl.pallas_call(
        body, name="sum_chips_w_in", grid=(rows // tile,),
        in_specs=[pl.BlockSpec((N_CHIPS, tile, cols), lambda i: (0, i, 0))],
        out_specs=pl.BlockSpec((tile, cols), lambda i: (i, 0)),
        out_shape=jax.ShapeDtypeStruct((rows, cols), F32),
    )(r)


def _sum_chips_small(rs):
    n = len(rs)

    def body(*refs):
        for w in range(n):
            acc = refs[w][0].astype(F32)
            for k in range(1, N_CHIPS):
                acc = acc + refs[w][k].astype(F32)
            refs[n + w][...] = acc

    return pl.pallas_call(
        body, name="sum_chips_small", in_specs=[VMEM_SPEC] * n, out_specs=[VMEM_SPEC] * n,
        out_shape=[jax.ShapeDtypeStruct(r.shape[1:], F32) for r in rs],
        compiler_params=pltpu.CompilerParams(vmem_limit_bytes=VMEM_LIMIT),
    )(*rs)


def _share(halves, rep_half):
    n = len(halves)
    hr = rep_half.shape[0]

    def body(*refs):
        h_refs, rep_ref = refs[:n], refs[n]
        g_refs, all_ref = refs[n + 1:2 * n + 1], refs[2 * n + 1]
        send_sems, recv_sems, local_sems = refs[2 * n + 2:]
        x, y, c, others = _place()
        me = 2 * x + y
        sibling = (x, y, 1 - c)

        def rows_of(w):
            h = halves[w].shape[0]
            return g_refs[w].at[pl.ds(c * h, h), :]

        def quarter(chip, half):
            return all_ref.at[chip, pl.ds(half * hr, hr), :]

        own = [pltpu.make_async_copy(h_refs[w], rows_of(w), local_sems.at[w]) for w in range(n)]
        own.append(pltpu.make_async_copy(rep_ref, quarter(me, c), local_sems.at[n]))
        for cp in own:
            cp.start()
        cps = [pltpu.make_async_remote_copy(src_ref=h_refs[w], dst_ref=rows_of(w), send_sem=send_sems.at[w],
                                            recv_sem=recv_sems.at[w], device_id=sibling, device_id_type=MESH)
               for w in range(n)]
        cps.append(pltpu.make_async_remote_copy(src_ref=rep_ref, dst_ref=quarter(me, c), send_sem=send_sems.at[n],
                                                recv_sem=recv_sems.at[n], device_id=sibling, device_id_type=MESH))
        for j, (px, py) in enumerate(others):
            cps.append(pltpu.make_async_remote_copy(
                src_ref=rep_ref, dst_ref=quarter(me, c), send_sem=send_sems.at[n + 1 + j],
                recv_sem=recv_sems.at[n + 1 + j], device_id=(px, py, c), device_id_type=MESH))
        for cp in cps:
            cp.start()

        def passed(j, half):
            px, py = others[j]
            return pltpu.make_async_remote_copy(
                src_ref=quarter(2 * px + py, half), dst_ref=quarter(2 * px + py, half),
                send_sem=send_sems.at[n + 4 + j], recv_sem=recv_sems.at[n + 4 + j], device_id=sibling,
                device_id_type=MESH)

        fwd = []
        for j, (px, py) in enumerate(others):
            pltpu.make_async_remote_copy(
                src_ref=rep_ref, dst_ref=quarter(2 * px + py, c), send_sem=send_sems.at[n + 1 + j],
                recv_sem=recv_sems.at[n + 1 + j], device_id=(px, py, c), device_id_type=MESH).wait_recv()
            cp = passed(j, c)
            cp.start()
            fwd.append(cp)
        for j in range(3):
            passed(j, 1 - c).wait_recv()
        for w in range(n):
            h = halves[w].shape[0]
            pltpu.make_async_remote_copy(
                src_ref=h_refs[w], dst_ref=g_refs[w].at[pl.ds((1 - c) * h, h), :], send_sem=send_sems.at[w],
                recv_sem=recv_sems.at[w], device_id=sibling, device_id_type=MESH).wait_recv()
        pltpu.make_async_remote_copy(
            src_ref=rep_ref, dst_ref=quarter(me, 1 - c), send_sem=send_sems.at[n], recv_sem=recv_sems.at[n],
            device_id=sibling, device_id_type=MESH).wait_recv()
        for cp in cps + fwd:
            cp.wait_send()
        for cp in own:
            cp.wait()

    return pl.pallas_call(
        body, name="share", in_specs=[HBM_SPEC] * (n + 1), out_specs=[HBM_SPEC] * (n + 1),
        out_shape=[jax.ShapeDtypeStruct((2 * h.shape[0], h.shape[1]), F32) for h in halves]
        + [jax.ShapeDtypeStruct((N_CHIPS, 2 * hr, rep_half.shape[1]), F32)],
        scratch_shapes=[pltpu.SemaphoreType.DMA((n + 7,)), pltpu.SemaphoreType.DMA((n + 7,)),
                        pltpu.SemaphoreType.DMA((n + 1,))],
    )(*halves, rep_half)


def _adamw(w, g, m, v):
    m2 = ADAM_B1 * m + (1.0 - ADAM_B1) * g
    v2 = ADAM_B2 * v + (1.0 - ADAM_B2) * (g * g)
    m_hat = m2 / (1.0 - ADAM_B1 ** ADAM_STEP)
    v_hat = v2 / (1.0 - ADAM_B2 ** ADAM_STEP)
    return -ADAM_LR * (m_hat / (jnp.sqrt(v_hat) + ADAM_EPS) + ADAM_WD * w), m2, v2


def _update_tiled(w, g, m, v, tile):
    def body(w_ref, g_ref, m_ref, v_ref, d_ref, m2_ref, v2_ref):
        d_ref[...], m2_ref[...], v2_ref[...] = _adamw(w_ref[...], g_ref[...], m_ref[...], v_ref[...])

    spec = pl.BlockSpec((tile, w.shape[1]), lambda i: (i, 0))
    return pl.pallas_call(
        body, name="update_w_in", grid=(w.shape[0] // tile,), in_specs=[spec] * 4, out_specs=[spec] * 3,
        out_shape=[jax.ShapeDtypeStruct(w.shape, F32)] * 3,
        compiler_params=_params(("parallel",)),
    )(w, g, m, v)


def _update_small(ws, gs, ms, vs):
    n = len(ws)

    def body(*refs):
        for k in range(n):
            w_ref, g_ref, m_ref, v_ref = refs[k], refs[n + k], refs[2 * n + k], refs[3 * n + k]
            d, m2, v2 = _adamw(w_ref[...], g_ref[...], m_ref[...], v_ref[...])
            refs[4 * n + k][...] = d
            refs[5 * n + k][...] = m2
            refs[6 * n + k][...] = v2

    shapes = [jax.ShapeDtypeStruct(w.shape, F32) for w in ws]
    outs = pl.pallas_call(
        body, name="update_small", in_specs=[VMEM_SPEC] * (4 * n), out_specs=[VMEM_SPEC] * (3 * n),
        out_shape=shapes * 3,
        compiler_params=pltpu.CompilerParams(vmem_limit_bytes=VMEM_LIMIT),
    )(*ws, *gs, *ms, *vs)
    return outs[:n], outs[n:2 * n], outs[2 * n:]


SHARDED = ("w_in", "w_uq", "w_oa", "w_ob", "w_out")
REPLICATED = ("b_in", "g_q", "g_kv", "w_ukv", "sgu_ln_g", "sgu_ln_b", "w_s", "b_s", "ln_g", "ln_b")
ORDER = ("w_in", "b_in", "g_q", "w_uq", "g_kv", "w_ukv", "w_oa", "sgu_ln_g", "sgu_ln_b", "w_s", "b_s", "w_ob", "w_out",
         "ln_g", "ln_b")


def kernel(x, positions, w_in, b_in, g_q, w_uq, g_kv, w_ukv, w_oa, sgu_ln_g, sgu_ln_b, w_s, b_s, w_ob, w_out, ln_g, ln_b, loss_target, m_w_in, m_b_in, m_g_q, m_w_uq, m_g_kv, m_w_ukv, m_w_oa, m_sgu_ln_g, m_sgu_ln_b, m_w_s, m_b_s, m_w_ob, m_w_out, m_ln_g, m_ln_b, v_w_in, v_b_in, v_g_q, v_w_uq, v_g_kv, v_w_ukv, v_w_oa, v_sgu_ln_g, v_sgu_ln_b, v_w_s, v_b_s, v_w_ob, v_w_out, v_ln_g, v_ln_b):
    w = dict(w_in=w_in, b_in=b_in, g_q=g_q, w_uq=w_uq, g_kv=g_kv, w_ukv=w_ukv, w_oa=w_oa, sgu_ln_g=sgu_ln_g,
             sgu_ln_b=sgu_ln_b, w_s=w_s, b_s=b_s, w_ob=w_ob, w_out=w_out, ln_g=ln_g, ln_b=ln_b)
    m = dict(w_in=m_w_in, b_in=m_b_in, g_q=m_g_q, w_uq=m_w_uq, g_kv=m_g_kv, w_ukv=m_w_ukv, w_oa=m_w_oa,
             sgu_ln_g=m_sgu_ln_g, sgu_ln_b=m_sgu_ln_b, w_s=m_w_s, b_s=m_b_s, w_ob=m_w_ob, w_out=m_w_out, ln_g=m_ln_g,
             ln_b=m_ln_b)
    v = dict(w_in=v_w_in, b_in=v_b_in, g_q=v_g_q, w_uq=v_w_uq, g_kv=v_g_kv, w_ukv=v_w_ukv, w_oa=v_w_oa,
             sgu_ln_g=v_sgu_ln_g, sgu_ln_b=v_sgu_ln_b, w_s=v_w_s, b_s=v_b_s, w_ob=v_w_ob, w_out=v_w_out, ln_g=v_ln_g,
             ln_b=v_ln_b)
    w, m, v = ({n: a[0] for n, a in d.items()} for d in (w, m, v))
    c = lax.axis_index("c")

    g_in, g_uq, g_oa, g_ob, g_out = _gather_weights(
        [w["w_in"], w["w_uq"].reshape(Q_RANK // 4, HEADS * QK_DIM), w["w_oa"], w["w_ob"], w["w_out"]])
    full_in = jnp.transpose(g_in, (1, 0, 2)).reshape(D_MODEL, IN_W)
    full_uq = g_uq.reshape(Q_RANK, HEADS, QK_DIM)
    full_out = g_out.reshape(D_MODEL, D_MODEL)

    loss, dx, grads = _local_step(
        x[0], positions[0], loss_target[0], full_in, w["b_in"], w["g_q"], full_uq, w["g_kv"], w["w_ukv"], g_oa,
        w["sgu_ln_g"], w["sgu_ln_b"], w["w_s"], w["b_s"], g_ob, full_out, w["ln_g"], w["ln_b"])

    rep = jnp.concatenate([grads[n].reshape(-1) for n in REPLICATED])
    rep = jnp.pad(rep, (0, N_CHIPS * REP_ROWS * D_MODEL - rep.shape[0])).reshape(N_CHIPS, REP_ROWS, D_MODEL)
    parts = [jnp.transpose(grads["w_in"].reshape(D_MODEL, N_CHIPS, IN_W // N_CHIPS), (1, 0, 2)),
             grads["w_uq"].reshape(N_CHIPS, Q_RANK // N_CHIPS, HEADS * QK_DIM), grads["w_oa"], grads["w_ob"],
             grads["w_out"].reshape(N_CHIPS, SLAB_W, D_MODEL), rep]
    parts = [p.astype(BF16) for p in parts]
    c1 = c.reshape(1).astype(jnp.int32)
    *recv, all_loss = _exchange_pairs(parts, jnp.broadcast_to(loss, LOSS_TILE))
    pairs = [_add_pair_tiled(parts[0], recv[0], c1, 128), *_add_pair_small(parts[1:], recv[1:], c1)]
    landed = _exchange_chips(pairs)
    sums = [_sum_chips_tiled(landed[0], 128), *_sum_chips_small(landed[1:])]
    *shards, g_rep = _share(sums[:-1], sums[-1])
    loss = jnp.sum(all_loss[:, 0, 0])

    red = {n: s.reshape(w[n].shape) for n, s in zip(SHARDED, shards)}
    flat = g_rep.reshape(-1)
    off = 0
    for n in REPLICATED:
        red[n] = flat[off:off + w[n].size].reshape(w[n].shape)
        off += w[n].size

    d_in, m_in, v_in = _update_tiled(w["w_in"], red["w_in"], m["w_in"], v["w_in"], 128)
    small = [n for n in ORDER if n != "w_in"]
    as2d = lambda a: a.reshape(-1, a.shape[-1])
    ds, ms, vs = _update_small([as2d(w[n]) for n in small], [as2d(red[n]) for n in small],
                               [as2d(m[n]) for n in small], [as2d(v[n]) for n in small])
    delta, new_m, new_v = {"w_in": d_in}, {"w_in": m_in}, {"w_in": v_in}
    for i, n in enumerate(small):
        delta[n], new_m[n], new_v[n] = (a[i].reshape(w[n].shape) for a in (ds, ms, vs))

    lead = lambda a: a[None]
    return (loss, dx[None], *[lead(red[n]) for n in ORDER], *[lead(delta[n]) for n in ORDER],
            *[lead(new_m[n]) for n in ORDER], *[lead(new_v[n]) for n in ORDER])
```

```python
import functools
import math

import jax
import jax.numpy as jnp
from jax import lax
from jax.experimental import pallas as pl
from jax.experimental.pallas import tpu as pltpu

F32 = jnp.float32
BF16 = jnp.bfloat16

D_MODEL = 1024
HEADS = 8
Q_RANK = 384
KV_RANK = 128
NOPE = 64
ROPE = 32
V_DIM = 64
QK_DIM = NOPE + ROPE
HEAD_PAD = 128
MLA_W = HEADS * V_DIM
SGU_W = 512
GROUPS = 8
CHUNK = 128
IN_W = 4640
RMS_EPS = 1e-6
LN_EPS = 1e-5
ALPHA = 2.0 ** 0.25
ROPE_THETA = 10000.0
SCALE = QK_DIM ** -0.5

GATE_W = 2 * D_MODEL
MID_W = 4 * SGU_W
LAT_W = Q_RANK + KV_RANK + HEAD_PAD
PAD_W = GATE_W + MID_W + LAT_W
N_SLABS = 4
SLAB_W = D_MODEL // N_SLABS

ROW_TILE = 256
ATT_TQ = 256
ATT_TK = 256
ATT_BWD_TQ = 256
ATT_BWD_TK = 256
LOG2E = 1.4426950408889634
LN2 = 0.6931471805599453
Q_SCALE = SCALE * LOG2E
VMEM_LIMIT = 56 * 1024 * 1024

ADAM_LR = 0.001
ADAM_B1 = 0.9
ADAM_B2 = 0.999
ADAM_EPS = 1e-08
ADAM_WD = 0.01
ADAM_STEP = 10


def _dot(a, b):
    return jnp.dot(a, b, preferred_element_type=F32)


def _dot_nt(a, b):
    return lax.dot_general(a, b, (((1,), (1,)), ((), ())), preferred_element_type=F32)


def _dot_tn(a, b):
    return lax.dot_general(a, b, (((0,), (0,)), ((), ())), preferred_element_type=F32)


def _sigmoid(z):
    return 1.0 / (1.0 + jnp.exp(-z))


_GELU_C = math.sqrt(2.0 / math.pi)


def _gelu_and_grad(x):
    x2 = x * x
    t = jnp.tanh(_GELU_C * (x + 0.044715 * x * x2))
    g = 0.5 * x * (1.0 + t)
    dg = 0.5 * (1.0 + t) + 0.5 * x * (1.0 - t * t) * (_GELU_C * (1.0 + 3.0 * 0.044715 * x2))
    return g, dg


def _silu_and_grad(z):
    s = _sigmoid(z)
    return z * s, s * (1.0 + z * (1.0 - s))


def _rope(xb, c, sl, sh):
    return xb * c + pltpu.roll(xb, 112, 1) * sl + pltpu.roll(xb, 16, 1) * sh


def _rope_t(dy, c, sl, sh):
    return dy * c + pltpu.roll(dy * sl, 16, 1) + pltpu.roll(dy * sh, 112, 1)


def _params(sem=("arbitrary",)):
    return pltpu.CompilerParams(dimension_semantics=sem, vmem_limit_bytes=VMEM_LIMIT)


def _row_spec(tile, width):
    return pl.BlockSpec((tile, width), lambda i: (i, 0))


def _full_spec(shape):
    nd = len(shape)
    return pl.BlockSpec(shape, lambda i: (0,) * nd)


def _fwd_pre(x, w_pad, b_pad, g_q, wuq, g_kv, wk, wv, rc, rsl, rsh):
    s = x.shape[0]
    ts = ROW_TILE

    def body(x_ref, w_ref, b_ref, gq_ref, wuq_ref, gkv_ref, wk_ref, wv_ref, rc_ref, rsl_ref, rsh_ref,
             hg_ref, hm_ref, hl_ref, q_ref, k_ref, v_ref, xt_ref, qt_ref, kt_ref, vt_ref):
        xb = x_ref[...].astype(BF16)
        xt_ref[...] = xb.T
        h = _dot(xb, w_ref[...]) + b_ref[...]
        hg_ref[...] = h[:, :GATE_W]
        hm_ref[...] = h[:, GATE_W:GATE_W + MID_W]
        hl = h[:, GATE_W + MID_W:]
        hl_ref[...] = hl
        c, sl, sh = rc_ref[...], rsl_ref[...], rsh_ref[...]
        cq = hl[:, :Q_RANK]
        cqn = cq * lax.rsqrt(jnp.mean(cq * cq, axis=-1, keepdims=True) + RMS_EPS) * gq_ref[...]
        q = _dot(cqn.astype(BF16), wuq_ref[...])
        ckv = hl[:, Q_RANK:Q_RANK + KV_RANK]
        ckvn = (ckv * lax.rsqrt(jnp.mean(ckv * ckv, axis=-1, keepdims=True) + RMS_EPS) * gkv_ref[...]).astype(BF16)
        k = _dot(ckvn, wk_ref[...])
        vb = _dot(ckvn, wv_ref[...]).astype(BF16)
        v_ref[...] = vb
        vt_ref[...] = vb.T
        kpe = _rope(hl[:, Q_RANK + KV_RANK:], c, sl, sh)
        for hd in range(HEADS):
            lanes = slice(hd * HEAD_PAD, (hd + 1) * HEAD_PAD)
            qb = (_rope(q[:, lanes], c, sl, sh) * Q_SCALE).astype(BF16)
            kb = (k[:, lanes] + kpe).astype(BF16)
            q_ref[:, lanes] = qb
            k_ref[:, lanes] = kb
            qt_ref[lanes, :] = qb.T
            kt_ref[lanes, :] = kb.T

    qk_w = HEADS * HEAD_PAD
    col_spec = lambda rows: pl.BlockSpec((rows, ts), lambda i: (0, i))
    return pl.pallas_call(
        body, name="fwd_pre", grid=(s // ts,),
        in_specs=[_row_spec(ts, D_MODEL), _full_spec(w_pad.shape), _full_spec(b_pad.shape), _full_spec(g_q.shape),
                  _full_spec(wuq.shape), _full_spec(g_kv.shape), _full_spec(wk.shape), _full_spec(wv.shape),
                  _row_spec(ts, HEAD_PAD), _row_spec(ts, HEAD_PAD), _row_spec(ts, HEAD_PAD)],
        out_specs=[_row_spec(ts, GATE_W), _row_spec(ts, MID_W), _row_spec(ts, LAT_W), _row_spec(ts, qk_w),
                   _row_spec(ts, qk_w), _row_spec(ts, MLA_W), col_spec(D_MODEL), col_spec(qk_w), col_spec(qk_w),
                   col_spec(MLA_W)],
        out_shape=[jax.ShapeDtypeStruct((s, GATE_W), F32), jax.ShapeDtypeStruct((s, MID_W), F32),
                   jax.ShapeDtypeStruct((s, LAT_W), F32), jax.ShapeDtypeStruct((s, qk_w), BF16),
                   jax.ShapeDtypeStruct((s, qk_w), BF16), jax.ShapeDtypeStruct((s, MLA_W), BF16),
                   jax.ShapeDtypeStruct((D_MODEL, s), BF16), jax.ShapeDtypeStruct((qk_w, s), BF16),
                   jax.ShapeDtypeStruct((qk_w, s), BF16), jax.ShapeDtypeStruct((MLA_W, s), BF16)],
        compiler_params=_params(),
    )(x, w_pad, b_pad, g_q, wuq, g_kv, wk, wv, rc, rsl, rsh)


def _attn_fwd(qt, k, vt):
    s = k.shape[0]
    tq, tk = ATT_TQ, ATT_TK
    r = tq // tk
    pairs = HEADS // 2

    def body(qt_ref, k_ref, vt_ref, o_ref, lse_ref):
        i = pl.program_id(1)
        krow = lax.broadcasted_iota(jnp.int32, (tk, tq), 0)
        qcol = lax.broadcasted_iota(jnp.int32, (tk, tq), 1)
        qts = [qt_ref[hh * HEAD_PAD:(hh + 1) * HEAD_PAD, :] for hh in range(2)]

        def scores(j):
            koff = pl.multiple_of(j * tk, tk)
            return tuple(_dot(k_ref[pl.ds(koff, tk), hh * HEAD_PAD:(hh + 1) * HEAD_PAD], qts[hh]) for hh in range(2))

        def weighted(j, ps):
            koff = pl.multiple_of(j * tk, tk)
            return tuple(_dot(vt_ref[hh * V_DIM:(hh + 1) * V_DIM, pl.ds(koff, tk)], ps[hh]) for hh in range(2))

        def step(j, carry, diag, last):
            st, ps, stats = carry
            st_next = None if last else scores(j + 1)
            pvs = weighted(jnp.maximum(j - 1, 0), ps)
            new_ps, new_stats = [], []
            for hh in range(2):
                m, l, acc = stats[hh]
                s_ = st[hh]
                if diag is not None:
                    s_ = jnp.where(krow + diag * tk <= qcol, s_, -jnp.inf)
                m_new = jnp.maximum(m, jnp.max(s_, axis=0, keepdims=True))
                a = jnp.exp2(m - m_new)
                p = jnp.exp2(s_ - m_new)
                new_stats.append((m_new, a * l + jnp.sum(p, axis=0, keepdims=True), a * (acc + pvs[hh])))
                new_ps.append(p.astype(BF16))
            return st_next, tuple(new_ps), tuple(new_stats)

        one = (jnp.full((1, tq), -jnp.inf, F32), jnp.zeros((1, tq), F32), jnp.zeros((V_DIM, tq), F32))
        zero_p = jnp.zeros((tk, tq), BF16)
        nfull = i * r
        carry = lax.fori_loop(0, nfull, functools.partial(step, diag=None, last=False),
                              (scores(0), (zero_p, zero_p), (one, one)))
        for d in range(r):
            carry = step(nfull + d, carry, d, d == r - 1)
        _, ps, stats = carry
        pvs = weighted(nfull + r - 1, ps)
        ot = jnp.concatenate([(stats[hh][2] + pvs[hh]) / stats[hh][1] for hh in range(2)], axis=0)
        o_ref[...] = ot.T
        lse = [stats[hh][0] + jnp.log(stats[hh][1]) * LOG2E for hh in range(2)]
        lse_ref[...] = jnp.concatenate(lse + [jnp.zeros((6, tq), F32)], axis=0)

    return pl.pallas_call(
        body, name="attn_fwd", grid=(pairs, s // tq),
        in_specs=[pl.BlockSpec((2 * HEAD_PAD, tq), lambda p, i: (p, i)),
                  pl.BlockSpec((s, 2 * HEAD_PAD), lambda p, i: (0, p)),
                  pl.BlockSpec((2 * V_DIM, s), lambda p, i: (p, 0))],
        out_specs=[pl.BlockSpec((tq, 2 * V_DIM), lambda p, i: (i, p)),
                   pl.BlockSpec((None, 8, tq), lambda p, i: (p, 0, i))],
        out_shape=[jax.ShapeDtypeStruct((s, MLA_W), F32), jax.ShapeDtypeStruct((pairs, 8, s), F32)],
        compiler_params=_params(("arbitrary", "arbitrary")),
    )(qt, k, vt)


def _attn_bwd(q, qt, k, kt, v, do, dot, lse, delta):
    s = k.shape[0]
    tq, tk = ATT_BWD_TQ, ATT_BWD_TK
    r = tq // tk
    nq = s // tq
    nk = s // tk
    pairs = HEADS // 2

    def body(q_ref, qt_ref, k_ref, kt_ref, v_ref, do_ref, dot_ref, lse_ref, dl_ref, dqt_ref, dk_ref, dv_ref):
        j = pl.program_id(1)
        krow = lax.broadcasted_iota(jnp.int32, (tk, tq), 0)
        qcol = lax.broadcasted_iota(jnp.int32, (tk, tq), 1)
        lane = lax.broadcasted_iota(jnp.int32, (tk, 2 * V_DIM), 1)
        drow = lax.broadcasted_iota(jnp.int32, (2 * V_DIM, tq), 0)

        @pl.when(j == 0)
        def _():
            dqt_ref[...] = jnp.zeros_like(dqt_ref)

        koff = pl.multiple_of(j * tk, tk)
        vb = v_ref[pl.ds(koff, tk), :]
        kbs = [k_ref[pl.ds(koff, tk), hh * HEAD_PAD:(hh + 1) * HEAD_PAD] for hh in range(2)]
        ktbs = [kt_ref[hh * HEAD_PAD:(hh + 1) * HEAD_PAD, pl.ds(koff, tk)] for hh in range(2)]
        i0 = j // r

        def front(i):
            qoff = pl.multiple_of(i * tq, tq)
            dotb = dot_ref[:, pl.ds(qoff, tq)]
            out = []
            for hh in range(2):
                mine = (drow < V_DIM) if hh == 0 else (drow >= V_DIM)
                st = _dot(kbs[hh], qt_ref[hh * HEAD_PAD:(hh + 1) * HEAD_PAD, pl.ds(qoff, tq)])
                out.append((st, _dot(vb, jnp.where(mine, dotb, jnp.zeros_like(dotb)))))
            return tuple(out)

        def middle(i, tiles, diag):
            qoff = pl.multiple_of(i * tq, tq)
            out = []
            for hh in range(2):
                st, dpt = tiles[hh]
                if diag:
                    st = jnp.where(krow + (j - i0 * r) * tk <= qcol, st, -jnp.inf)
                p = jnp.exp2(st - lse_ref[hh:hh + 1, pl.ds(qoff, tq)])
                out.append((p.astype(BF16), (p * (dpt - dl_ref[hh:hh + 1, pl.ds(qoff, tq)])).astype(BF16)))
            return tuple(out)

        def back(i, pd, accs):
            qoff = pl.multiple_of(i * tq, tq)
            dob = do_ref[pl.ds(qoff, tq), :]
            out = []
            for hh in range(2):
                rows = slice(hh * HEAD_PAD, (hh + 1) * HEAD_PAD)
                p, dst = pd[hh]
                dk_acc, dv_acc = accs[hh]
                dv_acc = dv_acc + _dot(p, dob)
                dk_acc = dk_acc + _dot(dst, q_ref[pl.ds(qoff, tq), rows])
                dqt_ref[rows, pl.ds(qoff, tq)] += _dot(ktbs[hh], dst)
                out.append((dk_acc, dv_acc))
            return tuple(out)

        def step(i, accs, diag):
            return back(i, middle(i, front(i), diag), accs)

        zero_acc = (jnp.zeros((tk, HEAD_PAD), F32), jnp.zeros((tk, 2 * V_DIM), F32))
        accs = step(i0, (zero_acc, zero_acc), True)
        accs = lax.fori_loop(i0 + 1, nq, functools.partial(step, diag=False), accs)
        for hh in range(2):
            dk_ref[:, hh * HEAD_PAD:(hh + 1) * HEAD_PAD] = accs[hh][0] * LN2
        dv_ref[...] = jnp.where(lane < V_DIM, accs[0][1], accs[1][1])

        @pl.when(j == nk - 1)
        def _():
            dqt_ref[...] = dqt_ref[...] * SCALE

    pair_rows = lambda w: pl.BlockSpec((s, w), lambda p, j: (0, p))
    pair_cols = lambda w: pl.BlockSpec((w, s), lambda p, j: (p, 0))
    stats = pl.BlockSpec((None, 8, s), lambda p, j: (p, 0, 0))
    return pl.pallas_call(
        body, name="attn_bwd", grid=(pairs, nk),
        in_specs=[pair_rows(2 * HEAD_PAD), pair_cols(2 * HEAD_PAD), pair_rows(2 * HEAD_PAD), pair_cols(2 * HEAD_PAD),
                  pair_rows(2 * V_DIM), pair_rows(2 * V_DIM), pair_cols(2 * V_DIM), stats, stats],
        out_specs=[pair_cols(2 * HEAD_PAD),
                   pl.BlockSpec((tk, 2 * HEAD_PAD), lambda p, j: (j, p)),
                   pl.BlockSpec((tk, 2 * V_DIM), lambda p, j: (j, p))],
        out_shape=[jax.ShapeDtypeStruct((HEADS * HEAD_PAD, s), F32), jax.ShapeDtypeStruct((s, HEADS * HEAD_PAD), F32),
                   jax.ShapeDtypeStruct((s, MLA_W), F32)],
        compiler_params=_params(("arbitrary", "arbitrary")),
    )(q, qt, k, kt, v, do, dot, lse, delta)


def _split3(a):
    hi = a.astype(BF16)
    r1 = a - hi.astype(F32)
    mid = r1.astype(BF16)
    lo = (r1 - mid.astype(F32)).astype(BF16)
    return hi, mid, lo


def _mid(x, tgt, o, hm, hg, woa, wob, wout, ln_g, ln_b, sg_g, sg_b, w_s, bsb):
    s = x.shape[0]
    ts = ROW_TILE
    nsteps = s // ts
    nch = ts // CHUNK
    npair = GROUPS // 2

    def body(x_ref, t_ref, o_ref, hm_ref, hg_ref, woa_ref, wob_ref, wout_ref, lng_ref, lnb_ref, sgg_ref, sgb_ref,
             ws_ref, bsb_ref,
             dr_ref, dhg_ref, dhm_ref, do_ref, dot_ref, dl_ref,
             dwout_ref, dwoa_ref, dwob_ref, dws_ref, dbs_ref, dlng_ref, dlnb_ref, dsgg_ref, dsgb_ref, loss_ref,
             dbacc_ref):
        i = pl.program_id(0)

        @pl.when(i == 0)
        def _():
            for r in (dwout_ref, dwoa_ref, dwob_ref, dws_ref, dlng_ref, dlnb_ref, dsgg_ref, dsgb_ref, loss_ref,
                      dbacc_ref):
                r[...] = jnp.zeros_like(r)

        lane = lax.broadcasted_iota(jnp.int32, (CHUNK, CHUNK), 1)
        left = lane < V_DIM
        tril = lax.broadcasted_iota(jnp.int32, (CHUNK, CHUNK), 0) >= lane
        ms = [jnp.where(tril, ws_ref[g], 0.0).astype(BF16) for g in range(GROUPS)]

        z_a = hm_ref[:, 0:SGU_W]
        u = hm_ref[:, SGU_W:2 * SGU_W]
        v = hm_ref[:, 2 * SGU_W:3 * SGU_W]
        z_b = hm_ref[:, 3 * SGU_W:4 * SGU_W]
        o = o_ref[...]
        sa, dsa = _silu_and_grad(z_a)
        y_a = (o * sa).astype(BF16)
        gu, dgu = _gelu_and_grad(u)
        gv, dgv = _gelu_and_grad(v)
        mu = jnp.mean(gv, axis=-1, keepdims=True)
        vc = gv - mu
        rstd_v = lax.rsqrt(jnp.mean(vc * vc, axis=-1, keepdims=True) + LN_EPS)
        vhat = vc * rstd_v
        vn = (vhat * sgg_ref[...] + sgb_ref[...]).astype(BF16)
        rows = []
        for c in range(nch):
            blocks = []
            for p in range(npair):
                blk = vn[c * CHUNK:(c + 1) * CHUNK, p * CHUNK:(p + 1) * CHUNK]
                blocks.append(jnp.where(left, _dot(ms[2 * p], blk), _dot(ms[2 * p + 1], blk)))
            rows.append(jnp.concatenate(blocks, axis=1) + bsb_ref[...])
        mixed = jnp.concatenate(rows, axis=0)
        sgu = gu * mixed
        sb, dsb = _silu_and_grad(z_b)
        y_b = (sgu * sb).astype(BF16)
        pa = jnp.concatenate([_dot(y_a, woa_ref[k]) for k in range(N_SLABS)], axis=1)
        pb = jnp.concatenate([_dot(y_b, wob_ref[k]) for k in range(N_SLABS)], axis=1)
        sga = _sigmoid(hg_ref[:, :D_MODEL])
        sgb = _sigmoid(hg_ref[:, D_MODEL:])
        m2 = (sga * pa + sgb * pb).astype(BF16)
        r = ALPHA * x_ref[...] + _dot(m2, wout_ref[...])
        rmu = jnp.mean(r, axis=-1, keepdims=True)
        rc = r - rmu
        rstd = lax.rsqrt(jnp.mean(rc * rc, axis=-1, keepdims=True) + LN_EPS)
        xhat = rc * rstd
        y = xhat * lng_ref[...] + lnb_ref[...]
        err = y - t_ref[...]
        loss_ref[...] += jnp.full(loss_ref.shape, 0.5 / D_MODEL, F32) * jnp.sum(err * err)

        dy = err * (1.0 / D_MODEL)
        dlng_ref[...] += jnp.sum(dy * xhat, axis=0, keepdims=True)
        dlnb_ref[...] += jnp.sum(dy, axis=0, keepdims=True)
        dxh = dy * lng_ref[...]
        dr = rstd * (dxh - jnp.mean(dxh, axis=-1, keepdims=True) - xhat * jnp.mean(dxh * xhat, axis=-1, keepdims=True))
        dr_ref[...] = dr
        drb = dr.astype(BF16)
        dwout_ref[...] += _dot_tn(m2, drb)
        dm2 = _dot_nt(drb, wout_ref[...])
        dhg_ref[:, :D_MODEL] = (dm2 * pa * sga * (1.0 - sga)).astype(BF16)
        dhg_ref[:, D_MODEL:] = (dm2 * pb * sgb * (1.0 - sgb)).astype(BF16)
        dpa = (dm2 * sga).astype(BF16)
        dpb = (dm2 * sgb).astype(BF16)
        dy_a = jnp.zeros((ts, MLA_W), F32)
        dy_b = jnp.zeros((ts, SGU_W), F32)
        for k in range(N_SLABS):
            cols = slice(k * SLAB_W, (k + 1) * SLAB_W)
            dwoa_ref[k] += _dot_tn(y_a, dpa[:, cols])
            dwob_ref[k] += _dot_tn(y_b, dpb[:, cols])
            dy_a = dy_a + _dot_nt(dpa[:, cols], woa_ref[k])
            dy_b = dy_b + _dot_nt(dpb[:, cols], wob_ref[k])
        dob = (dy_a * sa).astype(BF16)
        do_ref[...] = dob
        dot_ref[...] = dob.T
        head = (lax.broadcasted_iota(jnp.int32, (HEADS, MLA_W), 1) // V_DIM
                == lax.broadcasted_iota(jnp.int32, (HEADS, MLA_W), 0)).astype(BF16)
        dl_ref[...] = sum(_dot_nt(head, term) for term in _split3(dob.astype(F32) * o))
        dhm_ref[:, 0:SGU_W] = (dy_a * o * dsa).astype(BF16)
        dsg = dy_b * sb
        dhm_ref[:, 3 * SGU_W:4 * SGU_W] = (dy_b * sgu * dsb).astype(BF16)
        dhm_ref[:, SGU_W:2 * SGU_W] = (dsg * mixed * dgu).astype(BF16)
        dmixed = dsg * gu
        dvn_rows = []
        dbs_sum = jnp.zeros((CHUNK, SGU_W), F32)
        for c in range(nch):
            dm_c = dmixed[c * CHUNK:(c + 1) * CHUNK, :]
            dbs_sum = dbs_sum + dm_c
            blocks = []
            for p in range(npair):
                dmb = dm_c[:, p * CHUNK:(p + 1) * CHUNK].astype(BF16)
                blk = vn[c * CHUNK:(c + 1) * CHUNK, p * CHUNK:(p + 1) * CHUNK]
                blocks.append(jnp.where(left, _dot_tn(ms[2 * p], dmb), _dot_tn(ms[2 * p + 1], dmb)))
                zero = jnp.zeros_like(dmb)
                dws_ref[2 * p] += jnp.where(tril, _dot_nt(jnp.where(left, dmb, zero), blk), 0.0)
                dws_ref[2 * p + 1] += jnp.where(tril, _dot_nt(jnp.where(left, zero, dmb), blk), 0.0)
            dvn_rows.append(jnp.concatenate(blocks, axis=1))
        dbacc_ref[...] += dbs_sum
        dvn = jnp.concatenate(dvn_rows, axis=0)
        dsgg_ref[...] += jnp.sum(dvn * vhat, axis=0, keepdims=True)
        dsgb_ref[...] += jnp.sum(dvn, axis=0, keepdims=True)
        dvh = dvn * sgg_ref[...]
        dgv_in = rstd_v * (dvh - jnp.mean(dvh, axis=-1, keepdims=True)
                           - vhat * jnp.mean(dvh * vhat, axis=-1, keepdims=True))
        dhm_ref[:, 2 * SGU_W:3 * SGU_W] = (dgv_in * dgv).astype(BF16)

        @pl.when(i == nsteps - 1)
        def _():
            grp = (lax.broadcasted_iota(jnp.int32, (SGU_W, CHUNK), 0) // V_DIM
                   == lax.broadcasted_iota(jnp.int32, (SGU_W, CHUNK), 1)).astype(BF16)
            hi, mid, lo = _split3(dbacc_ref[...])
            dbs_ref[...] = _dot(hi, grp) + _dot(mid, grp) + _dot(lo, grp)

    acc_shapes = [(D_MODEL, D_MODEL), woa.shape, wob.shape, (GROUPS, CHUNK, CHUNK), (CHUNK, CHUNK),
                  (1, D_MODEL), (1, D_MODEL), (1, SGU_W), (1, SGU_W), (1, 128)]
    return pl.pallas_call(
        body, name="mid", grid=(nsteps,),
        in_specs=[_row_spec(ts, D_MODEL), _row_spec(ts, D_MODEL), _row_spec(ts, MLA_W), _row_spec(ts, MID_W),
                  _row_spec(ts, GATE_W), _full_spec(woa.shape), _full_spec(wob.shape), _full_spec(wout.shape),
                  _full_spec(ln_g.shape), _full_spec(ln_b.shape), _full_spec(sg_g.shape), _full_spec(sg_b.shape),
                  _full_spec(w_s.shape), _full_spec(bsb.shape)],
        out_specs=[_row_spec(ts, D_MODEL), _row_spec(ts, GATE_W), _row_spec(ts, MID_W), _row_spec(ts, MLA_W),
                   pl.BlockSpec((MLA_W, ts), lambda i: (0, i)), pl.BlockSpec((HEADS, ts), lambda i: (0, i))]
        + [_full_spec(sh) for sh in acc_shapes],
        out_shape=[jax.ShapeDtypeStruct((s, D_MODEL), F32), jax.ShapeDtypeStruct((s, GATE_W), BF16),
                   jax.ShapeDtypeStruct((s, MID_W), BF16), jax.ShapeDtypeStruct((s, MLA_W), BF16),
                   jax.ShapeDtypeStruct((MLA_W, s), BF16), jax.ShapeDtypeStruct((HEADS, s), F32)]
        + [jax.ShapeDtypeStruct(sh, F32) for sh in acc_shapes],
        scratch_shapes=[pltpu.VMEM((CHUNK, SGU_W), F32)],
        compiler_params=_params(),
    )(x, tgt, o, hm, hg, woa, wob, wout, ln_g, ln_b, sg_g, sg_b, w_s, bsb)


def _lat_bwd(dq, dk, dv, hl, rc, rsl, rsh, g_q, g_kv, wuq, wk, wv):
    s = dk.shape[0]
    ts = ROW_TILE
    qk_w = HEADS * HEAD_PAD

    def body(dq_ref, dk_ref, dv_ref, hl_ref, rc_ref, rsl_ref, rsh_ref, gq_ref, gkv_ref, wuq_ref, wk_ref, wv_ref,
             dhl_ref, dwuq_ref, dwk_ref, dwv_ref, dgq_ref, dgkv_ref):
        i = pl.program_id(0)

        @pl.when(i == 0)
        def _():
            for r in (dwuq_ref, dwk_ref, dwv_ref, dgq_ref, dgkv_ref):
                r[...] = jnp.zeros_like(r)

        c, sl, sh = rc_ref[...], rsl_ref[...], rsh_ref[...]
        lane = lax.broadcasted_iota(jnp.int32, (ts, HEAD_PAD), 1)
        pe = (lane >= NOPE) & (lane < QK_DIM)
        dkpe = jnp.zeros((ts, HEAD_PAD), F32)
        dqu = []
        for hd in range(HEADS):
            lanes = slice(hd * HEAD_PAD, (hd + 1) * HEAD_PAD)
            dqu.append(_rope_t(dq_ref[lanes, :].T, c, sl, sh).astype(BF16))
            dkpe = dkpe + dk_ref[:, lanes]
        dqu = jnp.concatenate(dqu, axis=1)
        dkpe = _rope_t(jnp.where(pe, dkpe, 0.0), c, sl, sh)

        cq = hl_ref[:, :Q_RANK]
        rq = lax.rsqrt(jnp.mean(cq * cq, axis=-1, keepdims=True) + RMS_EPS)
        cqh = cq * rq
        cqn = (cqh * gq_ref[...]).astype(BF16)
        dwuq_ref[...] += _dot_tn(cqn, dqu)
        dcqn = _dot_nt(dqu, wuq_ref[...])
        dgq_ref[...] += jnp.sum(dcqn * cqh, axis=0, keepdims=True)
        dch = dcqn * gq_ref[...]
        dhl_ref[:, :Q_RANK] = (rq * (dch - cqh * jnp.mean(dch * cqh, axis=-1, keepdims=True))).astype(BF16)

        ckv = hl_ref[:, Q_RANK:Q_RANK + KV_RANK]
        rk = lax.rsqrt(jnp.mean(ckv * ckv, axis=-1, keepdims=True) + RMS_EPS)
        ckh = ckv * rk
        ckn = (ckh * gkv_ref[...]).astype(BF16)
        dkb = dk_ref[...].astype(BF16)
        dvb = dv_ref[...].astype(BF16)
        dwk_ref[...] += _dot_tn(ckn, dkb)
        dwv_ref[...] += _dot_tn(ckn, dvb)
        dckn = _dot_nt(dkb, wk_ref[...]) + _dot_nt(dvb, wv_ref[...])
        dgkv_ref[...] += jnp.sum(dckn * ckh, axis=0, keepdims=True)
        dkh = dckn * gkv_ref[...]
        dhl_ref[:, Q_RANK:Q_RANK + KV_RANK] = (rk * (dkh - ckh * jnp.mean(dkh * ckh, axis=-1, keepdims=True))).astype(BF16)
        dhl_ref[:, Q_RANK + KV_RANK:] = dkpe.astype(BF16)

    acc_shapes = [wuq.shape, wk.shape, wv.shape, g_q.shape, g_kv.shape]
    return pl.pallas_call(
        body, name="lat_bwd", grid=(s // ts,),
        in_specs=[pl.BlockSpec((qk_w, ts), lambda i: (0, i)), _row_spec(ts, qk_w), _row_spec(ts, MLA_W),
                  _row_spec(ts, LAT_W), _row_spec(ts, HEAD_PAD), _row_spec(ts, HEAD_PAD), _row_spec(ts, HEAD_PAD),
                  _full_spec(g_q.shape), _full_spec(g_kv.shape), _full_spec(wuq.shape), _full_spec(wk.shape),
                  _full_spec(wv.shape)],
        out_specs=[_row_spec(ts, LAT_W)] + [_full_spec(sh) for sh in acc_shapes],
        out_shape=[jax.ShapeDtypeStruct((s, LAT_W), BF16)] + [jax.ShapeDtypeStruct(sh, F32) for sh in acc_shapes],
        compiler_params=_params(),
    )(dq, dk, dv, hl, rc, rsl, rsh, g_q, g_kv, wuq, wk, wv)


def _dx(dr, dhg, dhm, dhl, w_pad):
    s = dr.shape[0]
    ts = ROW_TILE

    def body(dr_ref, dhg_ref, dhm_ref, dhl_ref, w_ref, dx_ref):
        dx_ref[...] = (ALPHA * dr_ref[...]
                       + _dot_nt(dhg_ref[...], w_ref[:, :GATE_W])
                       + _dot_nt(dhm_ref[...], w_ref[:, GATE_W:GATE_W + MID_W])
                       + _dot_nt(dhl_ref[...], w_ref[:, GATE_W + MID_W:]))

    return pl.pallas_call(
        body, name="dx", grid=(s // ts,),
        in_specs=[_row_spec(ts, D_MODEL), _row_spec(ts, GATE_W), _row_spec(ts, MID_W), _row_spec(ts, LAT_W),
                  _full_spec(w_pad.shape)],
        out_specs=_row_spec(ts, D_MODEL),
        out_shape=jax.ShapeDtypeStruct((s, D_MODEL), F32),
        compiler_params=_params(),
    )(dr, dhg, dhm, dhl, w_pad)


def _dw_in(xt, dh, tn, name):
    s, width = dh.shape

    def body(xt_ref, dh_ref, dw_ref, db_ref):
        dhb = dh_ref[...]
        dw_ref[...] = _dot(xt_ref[...], dhb).astype(BF16)
        db_ref[...] = jnp.sum(dhb.astype(F32), axis=0, keepdims=True)

    return pl.pallas_call(
        body, name=name, grid=(width // tn,),
        in_specs=[_full_spec(xt.shape), pl.BlockSpec((s, tn), lambda n: (0, n))],
        out_specs=[pl.BlockSpec((D_MODEL, tn), lambda n: (0, n)), pl.BlockSpec((1, tn), lambda n: (0, n))],
        out_shape=[jax.ShapeDtypeStruct((D_MODEL, width), BF16), jax.ShapeDtypeStruct((1, width), F32)],
        compiler_params=_params(),
    )(xt, dh)


_C_Q, _C_KV, _K_PE, _Z_A, _U, _V, _Z_B, _G_A, _G_B = (
    (0, 384), (384, 512), (512, 544), (544, 1056), (1056, 1568), (1568, 2080), (2080, 2592), (2592, 3616),
    (3616, 4640))


def _pad_in_cols(w):
    cut = lambda ab: w[:, ab[0]:ab[1]]
    z = lambda n: jnp.zeros((w.shape[0], n), w.dtype)
    return jnp.concatenate([cut(_G_A), cut(_G_B), cut(_Z_A), cut(_U), cut(_V), cut(_Z_B), cut(_C_Q), cut(_C_KV),
                            z(NOPE), cut(_K_PE), z(HEAD_PAD - QK_DIM)], axis=1)


def _unpad_in_cols(g, m, l):
    kpe = l[:, Q_RANK + KV_RANK + NOPE:Q_RANK + KV_RANK + QK_DIM]
    return jnp.concatenate([l[:, :Q_RANK + KV_RANK], kpe, m, g], axis=1)


def _rope_tables(positions):
    half = ROPE // 2
    inv_freq = ROPE_THETA ** (-jnp.arange(0, ROPE, 2, dtype=F32) / ROPE)
    ang = positions.astype(F32)[:, None] * inv_freq
    cos, sin = jnp.cos(ang), jnp.sin(ang)
    n = positions.shape[0]
    one, zero = jnp.ones((n, NOPE), F32), jnp.zeros((n, half), F32)
    tail1, tail0 = jnp.ones((n, HEAD_PAD - QK_DIM), F32), jnp.zeros((n, HEAD_PAD - QK_DIM), F32)
    z64 = jnp.zeros((n, NOPE), F32)
    rc = jnp.concatenate([one, cos, cos, tail1], axis=1)
    rsl = jnp.concatenate([z64, -sin, zero, tail0], axis=1)
    rsh = jnp.concatenate([z64, zero, sin, tail0], axis=1)
    return rc, rsl, rsh


def _local_step(x, positions, tgt, w_in, b_in, g_q, w_uq, g_kv, w_ukv, w_oa, sg_g, sg_b, w_s, b_s, w_ob, w_out,
                ln_g, ln_b):
    rc, rsl, rsh = _rope_tables(positions)
    w_pad = _pad_in_cols(w_in)
    b_pad = _pad_in_cols(b_in[None, :])
    wuq = jnp.pad(w_uq, ((0, 0), (0, 0), (0, HEAD_PAD - QK_DIM))).reshape(Q_RANK, HEADS * HEAD_PAD).astype(BF16)
    wk = jnp.pad(w_ukv[:, :, :NOPE], ((0, 0), (0, 0), (0, HEAD_PAD - NOPE))).reshape(KV_RANK, HEADS * HEAD_PAD).astype(BF16)
    wv = w_ukv[:, :, NOPE:].reshape(KV_RANK, MLA_W).astype(BF16)
    bsb = jnp.repeat(b_s.T, V_DIM, axis=1)
    gq2, gkv2 = g_q[None, :], g_kv[None, :]

    hg, hm, hl, q, k, v, xt, qt, kt, vt = _fwd_pre(x, w_pad, b_pad, gq2, wuq, gkv2, wk, wv, rc, rsl, rsh)
    o, lse = _attn_fwd(qt, k, vt)
    (dr, dhg, dhm, do, dot, delta, dwout, dwoa, dwob, dws, dbs, dlng, dlnb, dsgg, dsgb, loss) = _mid(
        x, tgt, o, hm, hg, w_oa, w_ob, w_out, ln_g[None, :], ln_b[None, :], sg_g[None, :], sg_b[None, :], w_s, bsb)
    delta = jnp.pad(delta.reshape(HEADS // 2, 2, -1), ((0, 0), (0, 6), (0, 0)))
    dq, dk, dv = _attn_bwd(q, qt, k, kt, v, do, dot, lse, delta)
    dhl, dwuq, dwk, dwv, dgq, dgkv = _lat_bwd(dq, dk, dv, hl, rc, rsl, rsh, gq2, gkv2, wuq, wk, wv)
    dx = _dx(dr, dhg, dhm, dhl, w_pad)
    dwg, dbg = _dw_in(xt, dhg, 512, "dw_in_gates")
    dwm, dbm = _dw_in(xt, dhm, 512, "dw_in_mid")
    dwl, dbl = _dw_in(xt, dhl, LAT_W, "dw_in_lat")
    grads = {
        "w_in": _unpad_in_cols(dwg, dwm, dwl),
        "b_in": _unpad_in_cols(dbg, dbm, dbl)[0],
        "g_q": dgq[0],
        "w_uq": dwuq.reshape(Q_RANK, HEADS, HEAD_PAD)[:, :, :QK_DIM],
        "g_kv": dgkv[0],
        "w_ukv": jnp.concatenate([dwk.reshape(KV_RANK, HEADS, HEAD_PAD)[:, :, :NOPE],
                                  dwv.reshape(KV_RANK, HEADS, V_DIM)], axis=2),
        "w_oa": dwoa, "sgu_ln_g": dsgg[0], "sgu_ln_b": dsgb[0], "w_s": dws, "b_s": dbs[:, :GROUPS].T,
        "w_ob": dwob, "w_out": dwout, "ln_g": dlng[0], "ln_b": dlnb[0],
    }
    return loss, dx, grads


MESH = pl.DeviceIdType.MESH
N_CHIPS = 4
HBM_SPEC = pl.BlockSpec(memory_space=pl.ANY)
VMEM_SPEC = pl.BlockSpec(memory_space=pltpu.VMEM)

REP_ROWS = 96


def _place():
    x, y, c = lax.axis_index("x"), lax.axis_index("y"), lax.axis_index("c")
    others = [(1 - x, y), (x, 1 - y), (1 - x, 1 - y)]
    return x, y, c, others


def _gather_weights(shards):
    n = len(shards)

    def body(*refs):
        ins, outs, bufs = refs[:n], refs[n:2 * n], refs[2 * n:3 * n]
        send_sems, recv_sems, local_sems = refs[3 * n:]
        x, y, c, others = _place()
        me = 2 * x + y
        sibling = (x, y, 1 - c)
        for src, buf in zip(ins, bufs):
            buf[...] = src[...].astype(BF16)
        own = [pltpu.make_async_copy(bufs[w], outs[w].at[me], local_sems.at[w]) for w in range(n)]
        for cp in own:
            cp.start()

        def part(w, chip, half):
            hr = shards[w].shape[0] // 2
            return outs[w].at[chip, pl.ds(half * hr, hr), :]

        def sent(w, j):
            hr = shards[w].shape[0] // 2
            return pltpu.make_async_remote_copy(
                src_ref=bufs[w].at[pl.ds(c * hr, hr), :], dst_ref=part(w, me, c),
                send_sem=send_sems.at[w * 3 + j], recv_sem=recv_sems.at[w * 3 + j],
                device_id=(*others[j], c), device_id_type=MESH)

        def landed(w, j):
            px, py = others[j]
            return pltpu.make_async_remote_copy(
                src_ref=part(w, 2 * px + py, c), dst_ref=part(w, 2 * px + py, c),
                send_sem=send_sems.at[w * 3 + j], recv_sem=recv_sems.at[w * 3 + j],
                device_id=(px, py, c), device_id_type=MESH)

        def passed(w, j, half):
            px, py = others[j]
            k = n * 3 + w * 3 + j
            return pltpu.make_async_remote_copy(
                src_ref=part(w, 2 * px + py, half), dst_ref=part(w, 2 * px + py, half),
                send_sem=send_sems.at[k], recv_sem=recv_sems.at[k], device_id=sibling, device_id_type=MESH)

        first = [sent(w, j) for w in range(n) for j in range(3)]
        for cp in first:
            cp.start()
        fwd = []
        for w in range(n):
            for j in range(3):
                landed(w, j).wait_recv()
                cp = passed(w, j, c)
                cp.start()
                fwd.append(cp)
        for w in range(n):
            for j in range(3):
                passed(w, j, 1 - c).wait_recv()
        for cp in first + fwd:
            cp.wait_send()
        for cp in own:
            cp.wait()

    return pl.pallas_call(
        body, name="gather_weights",
        in_specs=[VMEM_SPEC] * n, out_specs=[HBM_SPEC] * n,
        out_shape=[jax.ShapeDtypeStruct((N_CHIPS,) + s.shape, BF16) for s in shards],
        scratch_shapes=[pltpu.VMEM(s.shape, BF16) for s in shards]
        + [pltpu.SemaphoreType.DMA((6 * n,)), pltpu.SemaphoreType.DMA((6 * n,)), pltpu.SemaphoreType.DMA((n,))],
        compiler_params=pltpu.CompilerParams(vmem_limit_bytes=VMEM_LIMIT),
    )(*shards)


N_DEV = 8
LOSS_TILE = (8, 128)


def _half(a):
    return a.shape[1] // 2


def _exchange_pairs(parts, loss):
    n = len(parts)

    def body(*refs):
        p_refs, loss_ref = refs[:n], refs[n]
        r_refs, all_loss_ref = refs[n + 1:2 * n + 1], refs[2 * n + 1]
        send_sems, recv_sems, loss_send, loss_recv, local_sem = refs[2 * n + 2:]
        x, y, c, _ = _place()
        sibling = (x, y, 1 - c)
        cps = []
        for w in range(n):
            h = _half(parts[w])
            cps.append(pltpu.make_async_remote_copy(
                src_ref=p_refs[w].at[:, pl.ds((1 - c) * h, h), :], dst_ref=r_refs[w],
                send_sem=send_sems.at[w], recv_sem=recv_sems.at[w], device_id=sibling, device_id_type=MESH))
        for cp in cps:
            cp.start()
        me = 4 * x + 2 * y + c
        own = pltpu.make_async_copy(loss_ref, all_loss_ref.at[me], local_sem)
        own.start()
        lcs = []
        for t in range(1, N_DEV):
            d = (me + t) % N_DEV
            lcs.append(pltpu.make_async_remote_copy(
                src_ref=loss_ref, dst_ref=all_loss_ref.at[me], send_sem=loss_send.at[t - 1],
                recv_sem=loss_recv.at[t - 1], device_id=(d // 4, (d // 2) % 2, d % 2), device_id_type=MESH))
        for cp in lcs:
            cp.start()
        for t in range(1, N_DEV):
            d = (me + N_DEV - t) % N_DEV
            pltpu.make_async_remote_copy(
                src_ref=loss_ref, dst_ref=all_loss_ref.at[d], send_sem=loss_send.at[t - 1],
                recv_sem=loss_recv.at[t - 1], device_id=(d // 4, (d // 2) % 2, d % 2), device_id_type=MESH).wait_recv()
        for cp in lcs:
            cp.wait_send()
        for cp in cps:
            cp.wait()
        own.wait()

    return pl.pallas_call(
        body, name="exchange_pairs", in_specs=[HBM_SPEC] * (n + 1), out_specs=[HBM_SPEC] * (n + 1),
        out_shape=[jax.ShapeDtypeStruct((N_CHIPS, _half(p), p.shape[2]), BF16) for p in parts]
        + [jax.ShapeDtypeStruct((N_DEV,) + LOSS_TILE, F32)],
        scratch_shapes=[pltpu.SemaphoreType.DMA((n,)), pltpu.SemaphoreType.DMA((n,)),
                        pltpu.SemaphoreType.DMA((N_DEV - 1,)), pltpu.SemaphoreType.DMA((N_DEV - 1,)),
                        pltpu.SemaphoreType.DMA],
    )(*parts, loss)


def _add_pair_tiled(p, r, c, tile):
    nt = _half(p) // tile
    cols = p.shape[2]

    def body(c_ref, p_ref, r_ref, q_ref):
        q_ref[...] = (p_ref[...].astype(F32) + r_ref[...].astype(F32)).astype(BF16)

    return pl.pallas_call(
        body, name="add_pair_w_in",
        grid_spec=pltpu.PrefetchScalarGridSpec(
            num_scalar_prefetch=1, grid=(N_CHIPS, nt),
            in_specs=[pl.BlockSpec((None, tile, cols), lambda k, i, c_ref: (k, c_ref[0] * nt + i, 0)),
                      pl.BlockSpec((None, tile, cols), lambda k, i, c_ref: (k, i, 0))],
            out_specs=pl.BlockSpec((None, tile, cols), lambda k, i, c_ref: (k, i, 0))),
        out_shape=jax.ShapeDtypeStruct(r.shape, BF16),
    )(c, p, r)


def _add_pair_small(ps, rs, c):
    n = len(ps)

    def body(c_ref, *refs):
        for w in range(n):
            h = _half(ps[w])
            mine = refs[w][:, pl.ds(pl.multiple_of(c_ref[0] * h, 16), h), :]
            refs[2 * n + w][...] = (mine.astype(F32) + refs[n + w][...].astype(F32)).astype(BF16)

    return pl.pallas_call(
        body, name="add_pair_small",
        in_specs=[pl.BlockSpec(memory_space=pltpu.SMEM)] + [VMEM_SPEC] * (2 * n), out_specs=[VMEM_SPEC] * n,
        out_shape=[jax.ShapeDtypeStruct(r.shape, BF16) for r in rs],
        compiler_params=pltpu.CompilerParams(vmem_limit_bytes=VMEM_LIMIT),
    )(c, *ps, *rs)


def _exchange_chips(qs):
    n = len(qs)

    def body(*refs):
        q_refs, r_refs = refs[:n], refs[n:2 * n]
        send_sems, recv_sems = refs[2 * n:]
        x, y, c, others = _place()
        me = 2 * x + y
        cps = []
        for w in range(n):
            for j, (px, py) in enumerate(others):
                cps.append(pltpu.make_async_remote_copy(
                    src_ref=q_refs[w].at[2 * px + py], dst_ref=r_refs[w].at[me], send_sem=send_sems.at[3 * w + j],
                    recv_sem=recv_sems.at[3 * w + j], device_id=(px, py, c), device_id_type=MESH))
        for cp in cps:
            cp.start()
        for w in range(n):
            for j, (px, py) in enumerate(others):
                pltpu.make_async_remote_copy(
                    src_ref=q_refs[w].at[me], dst_ref=r_refs[w].at[2 * px + py], send_sem=send_sems.at[3 * w + j],
                    recv_sem=recv_sems.at[3 * w + j], device_id=(px, py, c), device_id_type=MESH).wait_recv()
        for cp in cps:
            cp.wait_send()

    return pl.pallas_call(
        body, name="exchange_chips", in_specs=[HBM_SPEC] * n, out_specs=[HBM_SPEC] * n,
        out_shape=[jax.ShapeDtypeStruct(q.shape, BF16) for q in qs],
        scratch_shapes=[pltpu.SemaphoreType.DMA((3 * n,)), pltpu.SemaphoreType.DMA((3 * n,))],
    )(*qs)


def _sum_chips_tiled(q, r, idx, tile):
    rows, cols = r.shape[1:]
    nt = rows // tile

    def body(idx_ref, q_ref, r0_ref, r1_ref, r2_ref, g_ref):
        g_ref[...] = (q_ref[...].astype(F32) + r0_ref[...].astype(F32) + r1_ref[...].astype(F32)
                      + r2_ref[...].astype(F32))

    def slab(t):
        return pl.BlockSpec((None, tile, cols), lambda i, idx_ref: (idx_ref[t], i, 0))

    return pl.pallas_call(
        body, name="sum_chips_w_in",
        grid_spec=pltpu.PrefetchScalarGridSpec(
            num_scalar_prefetch=1, grid=(nt,), in_specs=[slab(0), slab(1), slab(2), slab(3)],
            out_specs=pl.BlockSpec((tile, cols), lambda i, idx_ref: (idx_ref[4] * nt + i, 0))),
        out_shape=jax.ShapeDtypeStruct((2 * rows, cols), F32),
    )(idx, q, r, r, r)


def _sum_chips_small(qs, rs, idx):
    n = len(rs)

    def body(idx_ref, *refs):
        c = idx_ref[4]
        for w in range(n):
            q_ref, r_ref, g_ref = refs[w], refs[n + w], refs[2 * n + w]
            acc = q_ref[idx_ref[0]].astype(F32)
            for t in range(1, N_CHIPS):
                acc = acc + r_ref[idx_ref[t]].astype(F32)
            h = rs[w].shape[1]
            mine = pl.ds(pl.multiple_of(c * h, 8), h)
            g_ref[...] = jnp.zeros_like(g_ref)
            if w == n - 1:
                g_ref[idx_ref[0], mine, :] = acc
            else:
                g_ref[mine, :] = acc

    shapes = [jax.ShapeDtypeStruct((2 * r.shape[1], r.shape[2]), F32) for r in rs[:-1]]
    shapes.append(jax.ShapeDtypeStruct((N_CHIPS, 2 * rs[-1].shape[1], rs[-1].shape[2]), F32))
    return pl.pallas_call(
        body, name="sum_chips_small",
        in_specs=[pl.BlockSpec(memory_space=pltpu.SMEM)] + [VMEM_SPEC] * (2 * n), out_specs=[VMEM_SPEC] * n,
        out_shape=shapes, compiler_params=pltpu.CompilerParams(vmem_limit_bytes=VMEM_LIMIT),
    )(idx, *qs, *rs)


def _share(shards, rep):
    n = len(shards)
    hr = rep.shape[1] // 2

    def body(*refs):
        g_refs, all_ref = refs[n + 1:2 * n + 1], refs[2 * n + 1]
        send_sems, recv_sems = refs[2 * n + 2:]
        x, y, c, others = _place()
        me = 2 * x + y
        sibling = (x, y, 1 - c)

        def rows_of(w, half):
            h = shards[w].shape[0] // 2
            return g_refs[w].at[pl.ds(half * h, h), :]

        def quarter(chip, half):
            return all_ref.at[chip, pl.ds(half * hr, hr), :]

        cps = [pltpu.make_async_remote_copy(src_ref=rows_of(w, c), dst_ref=rows_of(w, c), send_sem=send_sems.at[w],
                                            recv_sem=recv_sems.at[w], device_id=sibling, device_id_type=MESH)
               for w in range(n)]
        cps.append(pltpu.make_async_remote_copy(src_ref=quarter(me, c), dst_ref=quarter(me, c),
                                                send_sem=send_sems.at[n], recv_sem=recv_sems.at[n],
                                                device_id=sibling, device_id_type=MESH))
        for j, (px, py) in enumerate(others):
            cps.append(pltpu.make_async_remote_copy(
                src_ref=quarter(me, c), dst_ref=quarter(me, c), send_sem=send_sems.at[n + 1 + j],
                recv_sem=recv_sems.at[n + 1 + j], device_id=(px, py, c), device_id_type=MESH))
        for cp in cps:
            cp.start()

        def passed(j, half):
            px, py = others[j]
            return pltpu.make_async_remote_copy(
                src_ref=quarter(2 * px + py, half), dst_ref=quarter(2 * px + py, half),
                send_sem=send_sems.at[n + 4 + j], recv_sem=recv_sems.at[n + 4 + j], device_id=sibling,
                device_id_type=MESH)

        fwd = []
        for j, (px, py) in enumerate(others):
            pltpu.make_async_remote_copy(
                src_ref=quarter(me, c), dst_ref=quarter(2 * px + py, c), send_sem=send_sems.at[n + 1 + j],
                recv_sem=recv_sems.at[n + 1 + j], device_id=(px, py, c), device_id_type=MESH).wait_recv()
            cp = passed(j, c)
            cp.start()
            fwd.append(cp)
        for j in range(3):
            passed(j, 1 - c).wait_recv()
        for w in range(n):
            pltpu.make_async_remote_copy(
                src_ref=rows_of(w, c), dst_ref=rows_of(w, 1 - c), send_sem=send_sems.at[w],
                recv_sem=recv_sems.at[w], device_id=sibling, device_id_type=MESH).wait_recv()
        pltpu.make_async_remote_copy(
            src_ref=quarter(me, c), dst_ref=quarter(me, 1 - c), send_sem=send_sems.at[n], recv_sem=recv_sems.at[n],
            device_id=sibling, device_id_type=MESH).wait_recv()
        for cp in cps + fwd:
            cp.wait_send()

    return pl.pallas_call(
        body, name="share", in_specs=[HBM_SPEC] * (n + 1), out_specs=[HBM_SPEC] * (n + 1),
        out_shape=[jax.ShapeDtypeStruct(a.shape, F32) for a in (*shards, rep)],
        input_output_aliases={i: i for i in range(n + 1)},
        scratch_shapes=[pltpu.SemaphoreType.DMA((n + 7,)), pltpu.SemaphoreType.DMA((n + 7,))],
    )(*shards, rep)


def _adamw(w, g, m, v):
    m2 = ADAM_B1 * m + (1.0 - ADAM_B1) * g
    v2 = ADAM_B2 * v + (1.0 - ADAM_B2) * (g * g)
    m_hat = m2 / (1.0 - ADAM_B1 ** ADAM_STEP)
    v_hat = v2 / (1.0 - ADAM_B2 ** ADAM_STEP)
    return -ADAM_LR * (m_hat / (jnp.sqrt(v_hat) + ADAM_EPS) + ADAM_WD * w), m2, v2


def _update_tiled(w, g, m, v, tile):
    def body(w_ref, g_ref, m_ref, v_ref, d_ref, m2_ref, v2_ref):
        d_ref[...], m2_ref[...], v2_ref[...] = _adamw(w_ref[...], g_ref[...], m_ref[...], v_ref[...])

    spec = pl.BlockSpec((tile, w.shape[1]), lambda i: (i, 0))
    return pl.pallas_call(
        body, name="update_w_in", grid=(w.shape[0] // tile,), in_specs=[spec] * 4, out_specs=[spec] * 3,
        out_shape=[jax.ShapeDtypeStruct(w.shape, F32)] * 3,
        compiler_params=_params(("parallel",)),
    )(w, g, m, v)


def _update_small(ws, gs, ms, vs):
    n = len(ws)

    def body(*refs):
        for k in range(n):
            w_ref, g_ref, m_ref, v_ref = refs[k], refs[n + k], refs[2 * n + k], refs[3 * n + k]
            d, m2, v2 = _adamw(w_ref[...], g_ref[...], m_ref[...], v_ref[...])
            refs[4 * n + k][...] = d
            refs[5 * n + k][...] = m2
            refs[6 * n + k][...] = v2

    shapes = [jax.ShapeDtypeStruct(w.shape, F32) for w in ws]
    outs = pl.pallas_call(
        body, name="update_small", in_specs=[VMEM_SPEC] * (4 * n), out_specs=[VMEM_SPEC] * (3 * n),
        out_shape=shapes * 3,
        compiler_params=pltpu.CompilerParams(vmem_limit_bytes=VMEM_LIMIT),
    )(*ws, *gs, *ms, *vs)
    return outs[:n], outs[n:2 * n], outs[2 * n:]


SHARDED = ("w_in", "w_uq", "w_oa", "w_ob", "w_out")
REPLICATED = ("b_in", "g_q", "g_kv", "w_ukv", "sgu_ln_g", "sgu_ln_b", "w_s", "b_s", "ln_g", "ln_b")
ORDER = ("w_in", "b_in", "g_q", "w_uq", "g_kv", "w_ukv", "w_oa", "sgu_ln_g", "sgu_ln_b", "w_s", "b_s", "w_ob", "w_out",
         "ln_g", "ln_b")


def kernel(x, positions, w_in, b_in, g_q, w_uq, g_kv, w_ukv, w_oa, sgu_ln_g, sgu_ln_b, w_s, b_s, w_ob, w_out, ln_g, ln_b, loss_target, m_w_in, m_b_in, m_g_q, m_w_uq, m_g_kv, m_w_ukv, m_w_oa, m_sgu_ln_g, m_sgu_ln_b, m_w_s, m_b_s, m_w_ob, m_w_out, m_ln_g, m_ln_b, v_w_in, v_b_in, v_g_q, v_w_uq, v_g_kv, v_w_ukv, v_w_oa, v_sgu_ln_g, v_sgu_ln_b, v_w_s, v_b_s, v_w_ob, v_w_out, v_ln_g, v_ln_b):
    w = dict(w_in=w_in, b_in=b_in, g_q=g_q, w_uq=w_uq, g_kv=g_kv, w_ukv=w_ukv, w_oa=w_oa, sgu_ln_g=sgu_ln_g,
             sgu_ln_b=sgu_ln_b, w_s=w_s, b_s=b_s, w_ob=w_ob, w_out=w_out, ln_g=ln_g, ln_b=ln_b)
    m = dict(w_in=m_w_in, b_in=m_b_in, g_q=m_g_q, w_uq=m_w_uq, g_kv=m_g_kv, w_ukv=m_w_ukv, w_oa=m_w_oa,
             sgu_ln_g=m_sgu_ln_g, sgu_ln_b=m_sgu_ln_b, w_s=m_w_s, b_s=m_b_s, w_ob=m_w_ob, w_out=m_w_out, ln_g=m_ln_g,
             ln_b=m_ln_b)
    v = dict(w_in=v_w_in, b_in=v_b_in, g_q=v_g_q, w_uq=v_w_uq, g_kv=v_g_kv, w_ukv=v_w_ukv, w_oa=v_w_oa,
             sgu_ln_g=v_sgu_ln_g, sgu_ln_b=v_sgu_ln_b, w_s=v_w_s, b_s=v_b_s, w_ob=v_w_ob, w_out=v_w_out, ln_g=v_ln_g,
             ln_b=v_ln_b)
    w, m, v = ({n: a[0] for n, a in d.items()} for d in (w, m, v))
    c = lax.axis_index("c")

    g_in, g_uq, g_oa, g_ob, g_out = _gather_weights(
        [w["w_in"], w["w_uq"].reshape(Q_RANK // 4, HEADS * QK_DIM), w["w_oa"], w["w_ob"], w["w_out"]])
    full_in = jnp.transpose(g_in, (1, 0, 2)).reshape(D_MODEL, IN_W)
    full_uq = g_uq.reshape(Q_RANK, HEADS, QK_DIM)
    full_out = g_out.reshape(D_MODEL, D_MODEL)

    loss, dx, grads = _local_step(
        x[0], positions[0], loss_target[0], full_in, w["b_in"], w["g_q"], full_uq, w["g_kv"], w["w_ukv"], g_oa,
        w["sgu_ln_g"], w["sgu_ln_b"], w["w_s"], w["b_s"], g_ob, full_out, w["ln_g"], w["ln_b"])

    rep = jnp.concatenate([grads[n].reshape(-1) for n in REPLICATED])
    rep = jnp.pad(rep, (0, N_CHIPS * REP_ROWS * D_MODEL - rep.shape[0])).reshape(N_CHIPS, REP_ROWS, D_MODEL)
    parts = [jnp.transpose(grads["w_in"].reshape(D_MODEL, N_CHIPS, IN_W // N_CHIPS), (1, 0, 2)),
             grads["w_uq"].reshape(N_CHIPS, Q_RANK // N_CHIPS, HEADS * QK_DIM), grads["w_oa"], grads["w_ob"],
             grads["w_out"].reshape(N_CHIPS, SLAB_W, D_MODEL), rep]
    parts = [p.astype(BF16) for p in parts]
    c1 = c.reshape(1).astype(jnp.int32)
    *recv, all_loss = _exchange_pairs(parts, jnp.broadcast_to(loss, LOSS_TILE))
    pairs = [_add_pair_tiled(parts[0], recv[0], c1, 128), *_add_pair_small(parts[1:], recv[1:], c1)]
    landed = _exchange_chips(pairs)
    xi, yi = lax.axis_index("x"), lax.axis_index("y")
    idx = jnp.stack([2 * xi + yi, 2 * (1 - xi) + yi, 2 * xi + (1 - yi), 2 * (1 - xi) + (1 - yi), c]).astype(jnp.int32)
    sums = [_sum_chips_tiled(pairs[0], landed[0], idx, 128), *_sum_chips_small(pairs[1:], landed[1:], idx)]
    *shards, g_rep = _share(sums[:-1], sums[-1])
    loss = jnp.sum(all_loss[:, 0, 0])

    red = {n: s.reshape(w[n].shape) for n, s in zip(SHARDED, shards)}
    flat = g_rep.reshape(-1)
    off = 0
    for n in REPLICATED:
        red[n] = flat[off:off + w[n].size].reshape(w[n].shape)
        off += w[n].size

    d_in, m_in, v_in = _update_tiled(w["w_in"], red["w_in"], m["w_in"], v["w_in"], 128)
    small = [n for n in ORDER if n != "w_in"]
    as2d = lambda a: a.reshape(-1, a.shape[-1])
    ds, ms, vs = _update_small([as2d(w[n]) for n in small], [as2d(red[n]) for n in small],
                               [as2d(m[n]) for n in small], [as2d(v[n]) for n in small])
    delta, new_m, new_v = {"w_in": d_in}, {"w_in": m_in}, {"w_in": v_in}
    for i, n in enumerate(small):
        delta[n], new_m[n], new_v[n] = (a[i].reshape(w[n].shape) for a in (ds, ms, vs))

    lead = lambda a: a[None]
    return (loss, dx[None], *[lead(red[n]) for n in ORDER], *[lead(delta[n]) for n in ORDER],
            *[lead(new_m[n]) for n in ORDER], *[lead(new_v[n]) for n in ORDER])
```

```python
import functools
import math

import jax
import jax.numpy as jnp
from jax import lax
from jax.experimental import pallas as pl
from jax.experimental.pallas import tpu as pltpu

F32 = jnp.float32
BF16 = jnp.bfloat16

D_MODEL = 1024
HEADS = 8
Q_RANK = 384
KV_RANK = 128
NOPE = 64
ROPE = 32
V_DIM = 64
QK_DIM = NOPE + ROPE
HEAD_PAD = 128
MLA_W = HEADS * V_DIM
SGU_W = 512
GROUPS = 8
CHUNK = 128
IN_W = 4640
RMS_EPS = 1e-6
LN_EPS = 1e-5
ALPHA = 2.0 ** 0.25
ROPE_THETA = 10000.0
SCALE = QK_DIM ** -0.5

GATE_W = 2 * D_MODEL
MID_W = 4 * SGU_W
LAT_W = Q_RANK + KV_RANK + HEAD_PAD
PAD_W = GATE_W + MID_W + LAT_W
LAT_COLS = Q_RANK + KV_RANK + ROPE
N_SLABS = 4
SLAB_W = D_MODEL // N_SLABS

ROW_TILE = 256
ATT_TQ = 256
ATT_TK = 256
ATT_BWD_TQ = 256
ATT_BWD_TK = 256
LOG2E = 1.4426950408889634
LN2 = 0.6931471805599453
Q_SCALE = SCALE * LOG2E
VMEM_LIMIT = 56 * 1024 * 1024

ADAM_LR = 0.001
ADAM_B1 = 0.9
ADAM_B2 = 0.999
ADAM_EPS = 1e-08
ADAM_WD = 0.01
ADAM_STEP = 10


def _dot(a, b):
    return jnp.dot(a, b, preferred_element_type=F32)


def _dot_nt(a, b):
    return lax.dot_general(a, b, (((1,), (1,)), ((), ())), preferred_element_type=F32)


def _dot_tn(a, b):
    return lax.dot_general(a, b, (((0,), (0,)), ((), ())), preferred_element_type=F32)


def _sigmoid(z):
    return 1.0 / (1.0 + jnp.exp(-z))


_GELU_C = math.sqrt(2.0 / math.pi)


def _gelu_and_grad(x):
    x2 = x * x
    t = jnp.tanh(_GELU_C * (x + 0.044715 * x * x2))
    g = 0.5 * x * (1.0 + t)
    dg = 0.5 * (1.0 + t) + 0.5 * x * (1.0 - t * t) * (_GELU_C * (1.0 + 3.0 * 0.044715 * x2))
    return g, dg


def _silu_and_grad(z):
    s = _sigmoid(z)
    return z * s, s * (1.0 + z * (1.0 - s))


def _rope(xb, c, sl, sh):
    return xb * c + pltpu.roll(xb, 112, 1) * sl + pltpu.roll(xb, 16, 1) * sh


def _rope_t(dy, c, sl, sh):
    return dy * c + pltpu.roll(dy * sl, 16, 1) + pltpu.roll(dy * sh, 112, 1)


def _params(sem=("arbitrary",)):
    return pltpu.CompilerParams(dimension_semantics=sem, vmem_limit_bytes=VMEM_LIMIT)


def _row_spec(tile, width):
    return pl.BlockSpec((tile, width), lambda i: (i, 0))


def _full_spec(shape):
    nd = len(shape)
    return pl.BlockSpec(shape, lambda i: (0,) * nd)


def _fwd_pre(x, w_pad, b_pad, g_q, wuq, g_kv, wk, wv, rc, rsl, rsh):
    s = x.shape[0]
    ts = ROW_TILE

    def body(x_ref, w_ref, b_ref, gq_ref, wuq_ref, gkv_ref, wk_ref, wv_ref, rc_ref, rsl_ref, rsh_ref,
             hg_ref, hm_ref, hl_ref, q_ref, k_ref, v_ref, xt_ref, qt_ref, kt_ref, vt_ref):
        xb = x_ref[...].astype(BF16)
        xt_ref[...] = xb.T
        h = _dot(xb, w_ref[...]) + b_ref[...]
        hg_ref[...] = h[:, :GATE_W]
        hm_ref[...] = h[:, GATE_W:GATE_W + MID_W]
        hl = h[:, GATE_W + MID_W:]
        hl_ref[...] = hl
        c, sl, sh = rc_ref[...], rsl_ref[...], rsh_ref[...]
        cq = hl[:, :Q_RANK]
        cqn = cq * lax.rsqrt(jnp.mean(cq * cq, axis=-1, keepdims=True) + RMS_EPS) * gq_ref[...]
        q = _dot(cqn.astype(BF16), wuq_ref[...])
        ckv = hl[:, Q_RANK:Q_RANK + KV_RANK]
        ckvn = (ckv * lax.rsqrt(jnp.mean(ckv * ckv, axis=-1, keepdims=True) + RMS_EPS) * gkv_ref[...]).astype(BF16)
        k = _dot(ckvn, wk_ref[...])
        vb = _dot(ckvn, wv_ref[...]).astype(BF16)
        v_ref[...] = vb
        vt_ref[...] = vb.T
        kpe = _rope(hl[:, Q_RANK + KV_RANK:], c, sl, sh)
        for hd in range(HEADS):
            lanes = slice(hd * HEAD_PAD, (hd + 1) * HEAD_PAD)
            qb = (_rope(q[:, lanes], c, sl, sh) * Q_SCALE).astype(BF16)
            kb = (k[:, lanes] + kpe).astype(BF16)
            q_ref[:, lanes] = qb
            k_ref[:, lanes] = kb
            qt_ref[lanes, :] = qb.T
            kt_ref[lanes, :] = kb.T

    qk_w = HEADS * HEAD_PAD
    col_spec = lambda rows: pl.BlockSpec((rows, ts), lambda i: (0, i))
    return pl.pallas_call(
        body, name="fwd_pre", grid=(s // ts,),
        in_specs=[_row_spec(ts, D_MODEL), _full_spec(w_pad.shape), _full_spec(b_pad.shape), _full_spec(g_q.shape),
                  _full_spec(wuq.shape), _full_spec(g_kv.shape), _full_spec(wk.shape), _full_spec(wv.shape),
                  _row_spec(ts, HEAD_PAD), _row_spec(ts, HEAD_PAD), _row_spec(ts, HEAD_PAD)],
        out_specs=[_row_spec(ts, GATE_W), _row_spec(ts, MID_W), _row_spec(ts, LAT_W), _row_spec(ts, qk_w),
                   _row_spec(ts, qk_w), _row_spec(ts, MLA_W), col_spec(D_MODEL), col_spec(qk_w), col_spec(qk_w),
                   col_spec(MLA_W)],
        out_shape=[jax.ShapeDtypeStruct((s, GATE_W), F32), jax.ShapeDtypeStruct((s, MID_W), F32),
                   jax.ShapeDtypeStruct((s, LAT_W), F32), jax.ShapeDtypeStruct((s, qk_w), BF16),
                   jax.ShapeDtypeStruct((s, qk_w), BF16), jax.ShapeDtypeStruct((s, MLA_W), BF16),
                   jax.ShapeDtypeStruct((D_MODEL, s), BF16), jax.ShapeDtypeStruct((qk_w, s), BF16),
                   jax.ShapeDtypeStruct((qk_w, s), BF16), jax.ShapeDtypeStruct((MLA_W, s), BF16)],
        compiler_params=_params(),
    )(x, w_pad, b_pad, g_q, wuq, g_kv, wk, wv, rc, rsl, rsh)


def _attn_fwd(qt, k, vt):
    s = k.shape[0]
    tq, tk = ATT_TQ, ATT_TK
    r = tq // tk
    pairs = HEADS // 2

    def body(qt_ref, k_ref, vt_ref, o_ref, lse_ref):
        i = pl.program_id(1)
        krow = lax.broadcasted_iota(jnp.int32, (tk, tq), 0)
        qcol = lax.broadcasted_iota(jnp.int32, (tk, tq), 1)
        qts = [qt_ref[hh * HEAD_PAD:(hh + 1) * HEAD_PAD, :] for hh in range(2)]

        def scores(j):
            koff = pl.multiple_of(j * tk, tk)
            return tuple(_dot(k_ref[pl.ds(koff, tk), hh * HEAD_PAD:(hh + 1) * HEAD_PAD], qts[hh]) for hh in range(2))

        def weighted(j, ps):
            koff = pl.multiple_of(j * tk, tk)
            return tuple(_dot(vt_ref[hh * V_DIM:(hh + 1) * V_DIM, pl.ds(koff, tk)], ps[hh]) for hh in range(2))

        def step(j, carry, diag, last):
            st, ps, stats = carry
            st_next = None if last else scores(j + 1)
            pvs = weighted(jnp.maximum(j - 1, 0), ps)
            new_ps, new_stats = [], []
            for hh in range(2):
                m, l, acc = stats[hh]
                s_ = st[hh]
                if diag is not None:
                    s_ = jnp.where(krow + diag * tk <= qcol, s_, -jnp.inf)
                m_new = jnp.maximum(m, jnp.max(s_, axis=0, keepdims=True))
                a = jnp.exp2(m - m_new)
                p = jnp.exp2(s_ - m_new)
                new_stats.append((m_new, a * l + jnp.sum(p, axis=0, keepdims=True), a * (acc + pvs[hh])))
                new_ps.append(p.astype(BF16))
            return st_next, tuple(new_ps), tuple(new_stats)

        one = (jnp.full((1, tq), -jnp.inf, F32), jnp.zeros((1, tq), F32), jnp.zeros((V_DIM, tq), F32))
        zero_p = jnp.zeros((tk, tq), BF16)
        nfull = i * r
        carry = lax.fori_loop(0, nfull, functools.partial(step, diag=None, last=False),
                              (scores(0), (zero_p, zero_p), (one, one)))
        for d in range(r):
            carry = step(nfull + d, carry, d, d == r - 1)
        _, ps, stats = carry
        pvs = weighted(nfull + r - 1, ps)
        ot = jnp.concatenate([(stats[hh][2] + pvs[hh]) / stats[hh][1] for hh in range(2)], axis=0)
        o_ref[...] = ot.T
        lse = [stats[hh][0] + jnp.log(stats[hh][1]) * LOG2E for hh in range(2)]
        lse_ref[...] = jnp.concatenate(lse + [jnp.zeros((6, tq), F32)], axis=0)

    return pl.pallas_call(
        body, name="attn_fwd", grid=(pairs, s // tq),
        in_specs=[pl.BlockSpec((2 * HEAD_PAD, tq), lambda p, i: (p, i)),
                  pl.BlockSpec((s, 2 * HEAD_PAD), lambda p, i: (0, p)),
                  pl.BlockSpec((2 * V_DIM, s), lambda p, i: (p, 0))],
        out_specs=[pl.BlockSpec((tq, 2 * V_DIM), lambda p, i: (i, p)),
                   pl.BlockSpec((None, 8, tq), lambda p, i: (p, 0, i))],
        out_shape=[jax.ShapeDtypeStruct((s, MLA_W), F32), jax.ShapeDtypeStruct((pairs, 8, s), F32)],
        compiler_params=_params(("arbitrary", "arbitrary")),
    )(qt, k, vt)


def _attn_bwd(q, qt, k, kt, v, do, dot, lse, delta):
    s = k.shape[0]
    tq, tk = ATT_BWD_TQ, ATT_BWD_TK
    r = tq // tk
    nq = s // tq
    nk = s // tk
    pairs = HEADS // 2

    def body(q_ref, qt_ref, k_ref, kt_ref, v_ref, do_ref, dot_ref, lse_ref, dl_ref, dqt_ref, dk_ref, dv_ref):
        j = pl.program_id(1)
        krow = lax.broadcasted_iota(jnp.int32, (tk, tq), 0)
        qcol = lax.broadcasted_iota(jnp.int32, (tk, tq), 1)
        lane = lax.broadcasted_iota(jnp.int32, (tk, 2 * V_DIM), 1)
        drow = lax.broadcasted_iota(jnp.int32, (2 * V_DIM, tq), 0)

        @pl.when(j == 0)
        def _():
            dqt_ref[...] = jnp.zeros_like(dqt_ref)

        koff = pl.multiple_of(j * tk, tk)
        vb = v_ref[pl.ds(koff, tk), :]
        kbs = [k_ref[pl.ds(koff, tk), hh * HEAD_PAD:(hh + 1) * HEAD_PAD] for hh in range(2)]
        ktbs = [kt_ref[hh * HEAD_PAD:(hh + 1) * HEAD_PAD, pl.ds(koff, tk)] for hh in range(2)]
        i0 = j // r

        def front(i):
            qoff = pl.multiple_of(i * tq, tq)
            dotb = dot_ref[:, pl.ds(qoff, tq)]
            out = []
            for hh in range(2):
                mine = (drow < V_DIM) if hh == 0 else (drow >= V_DIM)
                st = _dot(kbs[hh], qt_ref[hh * HEAD_PAD:(hh + 1) * HEAD_PAD, pl.ds(qoff, tq)])
                out.append((st, _dot(vb, jnp.where(mine, dotb, jnp.zeros_like(dotb)))))
            return tuple(out)

        def middle(i, tiles, diag):
            qoff = pl.multiple_of(i * tq, tq)
            out = []
            for hh in range(2):
                st, dpt = tiles[hh]
                if diag:
                    st = jnp.where(krow + (j - i0 * r) * tk <= qcol, st, -jnp.inf)
                p = jnp.exp2(st - lse_ref[hh:hh + 1, pl.ds(qoff, tq)])
                out.append((p.astype(BF16), (p * (dpt - dl_ref[hh:hh + 1, pl.ds(qoff, tq)])).astype(BF16)))
            return tuple(out)

        def back(i, pd, accs):
            qoff = pl.multiple_of(i * tq, tq)
            dob = do_ref[pl.ds(qoff, tq), :]
            out = []
            for hh in range(2):
                rows = slice(hh * HEAD_PAD, (hh + 1) * HEAD_PAD)
                p, dst = pd[hh]
                dk_acc, dv_acc = accs[hh]
                dv_acc = dv_acc + _dot(p, dob)
                dk_acc = dk_acc + _dot(dst, q_ref[pl.ds(qoff, tq), rows])
                dqt_ref[rows, pl.ds(qoff, tq)] += _dot(ktbs[hh], dst)
                out.append((dk_acc, dv_acc))
            return tuple(out)

        def step(i, accs, diag):
            return back(i, middle(i, front(i), diag), accs)

        zero_acc = (jnp.zeros((tk, HEAD_PAD), F32), jnp.zeros((tk, 2 * V_DIM), F32))
        accs = step(i0, (zero_acc, zero_acc), True)
        accs = lax.fori_loop(i0 + 1, nq, functools.partial(step, diag=False), accs)
        for hh in range(2):
            dk_ref[:, hh * HEAD_PAD:(hh + 1) * HEAD_PAD] = accs[hh][0] * LN2
        dv_ref[...] = jnp.where(lane < V_DIM, accs[0][1], accs[1][1])

        @pl.when(j == nk - 1)
        def _():
            dqt_ref[...] = dqt_ref[...] * SCALE

    pair_rows = lambda w: pl.BlockSpec((s, w), lambda p, j: (0, p))
    pair_cols = lambda w: pl.BlockSpec((w, s), lambda p, j: (p, 0))
    stats = pl.BlockSpec((None, 8, s), lambda p, j: (p, 0, 0))
    return pl.pallas_call(
        body, name="attn_bwd", grid=(pairs, nk),
        in_specs=[pair_rows(2 * HEAD_PAD), pair_cols(2 * HEAD_PAD), pair_rows(2 * HEAD_PAD), pair_cols(2 * HEAD_PAD),
                  pair_rows(2 * V_DIM), pair_rows(2 * V_DIM), pair_cols(2 * V_DIM), stats, stats],
        out_specs=[pair_cols(2 * HEAD_PAD),
                   pl.BlockSpec((tk, 2 * HEAD_PAD), lambda p, j: (j, p)),
                   pl.BlockSpec((tk, 2 * V_DIM), lambda p, j: (j, p))],
        out_shape=[jax.ShapeDtypeStruct((HEADS * HEAD_PAD, s), F32), jax.ShapeDtypeStruct((s, HEADS * HEAD_PAD), F32),
                   jax.ShapeDtypeStruct((s, MLA_W), F32)],
        compiler_params=_params(("arbitrary", "arbitrary")),
    )(q, qt, k, kt, v, do, dot, lse, delta)


def _split3(a):
    hi = a.astype(BF16)
    r1 = a - hi.astype(F32)
    mid = r1.astype(BF16)
    lo = (r1 - mid.astype(F32)).astype(BF16)
    return hi, mid, lo


def _mid(x, tgt, o, hm, hg, woa, wob, wout, ln_g, ln_b, sg_g, sg_b, w_s, bsb):
    s = x.shape[0]
    ts = ROW_TILE
    nsteps = s // ts
    nch = ts // CHUNK
    npair = GROUPS // 2

    def body(x_ref, t_ref, o_ref, hm_ref, hg_ref, woa_ref, wob_ref, wout_ref, lng_ref, lnb_ref, sgg_ref, sgb_ref,
             ws_ref, bsb_ref,
             dr_ref, dhg_ref, dhm_ref, do_ref, dot_ref, dl_ref,
             dwout_ref, dwoa_ref, dwob_ref, dws_ref, dbs_ref, dlng_ref, dlnb_ref, dsgg_ref, dsgb_ref, loss_ref,
             dbacc_ref):
        i = pl.program_id(0)

        @pl.when(i == 0)
        def _():
            for r in (dwout_ref, dwoa_ref, dwob_ref, dws_ref, dlng_ref, dlnb_ref, dsgg_ref, dsgb_ref, loss_ref,
                      dbacc_ref):
                r[...] = jnp.zeros_like(r)

        lane = lax.broadcasted_iota(jnp.int32, (CHUNK, CHUNK), 1)
        left = lane < V_DIM
        tril = lax.broadcasted_iota(jnp.int32, (CHUNK, CHUNK), 0) >= lane
        ms = [jnp.where(tril, ws_ref[g], 0.0).astype(BF16) for g in range(GROUPS)]

        z_a = hm_ref[:, 0:SGU_W]
        u = hm_ref[:, SGU_W:2 * SGU_W]
        v = hm_ref[:, 2 * SGU_W:3 * SGU_W]
        z_b = hm_ref[:, 3 * SGU_W:4 * SGU_W]
        o = o_ref[...]
        sa, dsa = _silu_and_grad(z_a)
        y_a = (o * sa).astype(BF16)
        gu, dgu = _gelu_and_grad(u)
        gv, dgv = _gelu_and_grad(v)
        mu = jnp.mean(gv, axis=-1, keepdims=True)
        vc = gv - mu
        rstd_v = lax.rsqrt(jnp.mean(vc * vc, axis=-1, keepdims=True) + LN_EPS)
        vhat = vc * rstd_v
        vn = (vhat * sgg_ref[...] + sgb_ref[...]).astype(BF16)
        rows = []
        for c in range(nch):
            blocks = []
            for p in range(npair):
                blk = vn[c * CHUNK:(c + 1) * CHUNK, p * CHUNK:(p + 1) * CHUNK]
                blocks.append(jnp.where(left, _dot(ms[2 * p], blk), _dot(ms[2 * p + 1], blk)))
            rows.append(jnp.concatenate(blocks, axis=1) + bsb_ref[...])
        mixed = jnp.concatenate(rows, axis=0)
        sgu = gu * mixed
        sb, dsb = _silu_and_grad(z_b)
        y_b = (sgu * sb).astype(BF16)
        pa = jnp.concatenate([_dot(y_a, woa_ref[k]) for k in range(N_SLABS)], axis=1)
        pb = jnp.concatenate([_dot(y_b, wob_ref[k]) for k in range(N_SLABS)], axis=1)
        sga = _sigmoid(hg_ref[:, :D_MODEL])
        sgb = _sigmoid(hg_ref[:, D_MODEL:])
        m2 = (sga * pa + sgb * pb).astype(BF16)
        r = ALPHA * x_ref[...] + _dot(m2, wout_ref[...])
        rmu = jnp.mean(r, axis=-1, keepdims=True)
        rc = r - rmu
        rstd = lax.rsqrt(jnp.mean(rc * rc, axis=-1, keepdims=True) + LN_EPS)
        xhat = rc * rstd
        y = xhat * lng_ref[...] + lnb_ref[...]
        err = y - t_ref[...]
        loss_ref[...] += jnp.full(loss_ref.shape, 0.5 / D_MODEL, F32) * jnp.sum(err * err)

        dy = err * (1.0 / D_MODEL)
        dlng_ref[...] += jnp.sum(dy * xhat, axis=0, keepdims=True)
        dlnb_ref[...] += jnp.sum(dy, axis=0, keepdims=True)
        dxh = dy * lng_ref[...]
        dr = rstd * (dxh - jnp.mean(dxh, axis=-1, keepdims=True) - xhat * jnp.mean(dxh * xhat, axis=-1, keepdims=True))
        dr_ref[...] = dr
        drb = dr.astype(BF16)
        dwout_ref[...] += _dot_tn(m2, drb)
        dm2 = _dot_nt(drb, wout_ref[...])
        dhg_ref[:, :D_MODEL] = (dm2 * pa * sga * (1.0 - sga)).astype(BF16)
        dhg_ref[:, D_MODEL:] = (dm2 * pb * sgb * (1.0 - sgb)).astype(BF16)
        dpa = (dm2 * sga).astype(BF16)
        dpb = (dm2 * sgb).astype(BF16)
        dy_a = jnp.zeros((ts, MLA_W), F32)
        dy_b = jnp.zeros((ts, SGU_W), F32)
        for k in range(N_SLABS):
            cols = slice(k * SLAB_W, (k + 1) * SLAB_W)
            dwoa_ref[k] += _dot_tn(y_a, dpa[:, cols])
            dwob_ref[k] += _dot_tn(y_b, dpb[:, cols])
            dy_a = dy_a + _dot_nt(dpa[:, cols], woa_ref[k])
            dy_b = dy_b + _dot_nt(dpb[:, cols], wob_ref[k])
        dob = (dy_a * sa).astype(BF16)
        do_ref[...] = dob
        dot_ref[...] = dob.T
        head = (lax.broadcasted_iota(jnp.int32, (HEADS, MLA_W), 1) // V_DIM
                == lax.broadcasted_iota(jnp.int32, (HEADS, MLA_W), 0)).astype(BF16)
        dl_ref[...] = sum(_dot_nt(head, term) for term in _split3(dob.astype(F32) * o))
        dhm_ref[:, 0:SGU_W] = (dy_a * o * dsa).astype(BF16)
        dsg = dy_b * sb
        dhm_ref[:, 3 * SGU_W:4 * SGU_W] = (dy_b * sgu * dsb).astype(BF16)
        dhm_ref[:, SGU_W:2 * SGU_W] = (dsg * mixed * dgu).astype(BF16)
        dmixed = dsg * gu
        dvn_rows = []
        dbs_sum = jnp.zeros((CHUNK, SGU_W), F32)
        for c in range(nch):
            dm_c = dmixed[c * CHUNK:(c + 1) * CHUNK, :]
            dbs_sum = dbs_sum + dm_c
            blocks = []
            for p in range(npair):
                dmb = dm_c[:, p * CHUNK:(p + 1) * CHUNK].astype(BF16)
                blk = vn[c * CHUNK:(c + 1) * CHUNK, p * CHUNK:(p + 1) * CHUNK]
                blocks.append(jnp.where(left, _dot_tn(ms[2 * p], dmb), _dot_tn(ms[2 * p + 1], dmb)))
                zero = jnp.zeros_like(dmb)
                dws_ref[2 * p] += jnp.where(tril, _dot_nt(jnp.where(left, dmb, zero), blk), 0.0)
                dws_ref[2 * p + 1] += jnp.where(tril, _dot_nt(jnp.where(left, zero, dmb), blk), 0.0)
            dvn_rows.append(jnp.concatenate(blocks, axis=1))
        dbacc_ref[...] += dbs_sum
        dvn = jnp.concatenate(dvn_rows, axis=0)
        dsgg_ref[...] += jnp.sum(dvn * vhat, axis=0, keepdims=True)
        dsgb_ref[...] += jnp.sum(dvn, axis=0, keepdims=True)
        dvh = dvn * sgg_ref[...]
        dgv_in = rstd_v * (dvh - jnp.mean(dvh, axis=-1, keepdims=True)
                           - vhat * jnp.mean(dvh * vhat, axis=-1, keepdims=True))
        dhm_ref[:, 2 * SGU_W:3 * SGU_W] = (dgv_in * dgv).astype(BF16)

        @pl.when(i == nsteps - 1)
        def _():
            grp = (lax.broadcasted_iota(jnp.int32, (SGU_W, CHUNK), 0) // V_DIM
                   == lax.broadcasted_iota(jnp.int32, (SGU_W, CHUNK), 1)).astype(BF16)
            hi, mid, lo = _split3(dbacc_ref[...])
            dbs_ref[...] = _dot(hi, grp) + _dot(mid, grp) + _dot(lo, grp)

    acc_shapes = [(D_MODEL, D_MODEL), woa.shape, wob.shape, (GROUPS, CHUNK, CHUNK), (CHUNK, CHUNK),
                  (1, D_MODEL), (1, D_MODEL), (1, SGU_W), (1, SGU_W), (1, 128)]
    return pl.pallas_call(
        body, name="mid", grid=(nsteps,),
        in_specs=[_row_spec(ts, D_MODEL), _row_spec(ts, D_MODEL), _row_spec(ts, MLA_W), _row_spec(ts, MID_W),
                  _row_spec(ts, GATE_W), _full_spec(woa.shape), _full_spec(wob.shape), _full_spec(wout.shape),
                  _full_spec(ln_g.shape), _full_spec(ln_b.shape), _full_spec(sg_g.shape), _full_spec(sg_b.shape),
                  _full_spec(w_s.shape), _full_spec(bsb.shape)],
        out_specs=[_row_spec(ts, D_MODEL), _row_spec(ts, GATE_W), _row_spec(ts, MID_W), _row_spec(ts, MLA_W),
                   pl.BlockSpec((MLA_W, ts), lambda i: (0, i)), pl.BlockSpec((HEADS, ts), lambda i: (0, i))]
        + [_full_spec(sh) for sh in acc_shapes],
        out_shape=[jax.ShapeDtypeStruct((s, D_MODEL), F32), jax.ShapeDtypeStruct((s, GATE_W), BF16),
                   jax.ShapeDtypeStruct((s, MID_W), BF16), jax.ShapeDtypeStruct((s, MLA_W), BF16),
                   jax.ShapeDtypeStruct((MLA_W, s), BF16), jax.ShapeDtypeStruct((HEADS, s), F32)]
        + [jax.ShapeDtypeStruct(sh, F32) for sh in acc_shapes],
        scratch_shapes=[pltpu.VMEM((CHUNK, SGU_W), F32)],
        compiler_params=_params(),
    )(x, tgt, o, hm, hg, woa, wob, wout, ln_g, ln_b, sg_g, sg_b, w_s, bsb)


def _lat_bwd(dq, dk, dv, hl, rc, rsl, rsh, g_q, g_kv, wuq, wk, wv):
    s = dk.shape[0]
    ts = ROW_TILE
    qk_w = HEADS * HEAD_PAD

    def body(dq_ref, dk_ref, dv_ref, hl_ref, rc_ref, rsl_ref, rsh_ref, gq_ref, gkv_ref, wuq_ref, wk_ref, wv_ref,
             dhl_ref, dwuq_ref, dwk_ref, dwv_ref, dgq_ref, dgkv_ref):
        i = pl.program_id(0)

        @pl.when(i == 0)
        def _():
            for r in (dwuq_ref, dwk_ref, dwv_ref, dgq_ref, dgkv_ref):
                r[...] = jnp.zeros_like(r)

        c, sl, sh = rc_ref[...], rsl_ref[...], rsh_ref[...]
        lane = lax.broadcasted_iota(jnp.int32, (ts, HEAD_PAD), 1)
        pe = (lane >= NOPE) & (lane < QK_DIM)
        dkpe = jnp.zeros((ts, HEAD_PAD), F32)
        dqu = []
        for hd in range(HEADS):
            lanes = slice(hd * HEAD_PAD, (hd + 1) * HEAD_PAD)
            dqu.append(_rope_t(dq_ref[lanes, :].T, c, sl, sh).astype(BF16))
            dkpe = dkpe + dk_ref[:, lanes]
        dqu = jnp.concatenate(dqu, axis=1)
        dkpe = _rope_t(jnp.where(pe, dkpe, 0.0), c, sl, sh)

        cq = hl_ref[:, :Q_RANK]
        rq = lax.rsqrt(jnp.mean(cq * cq, axis=-1, keepdims=True) + RMS_EPS)
        cqh = cq * rq
        cqn = (cqh * gq_ref[...]).astype(BF16)
        dwuq_ref[...] += _dot_tn(cqn, dqu)
        dcqn = _dot_nt(dqu, wuq_ref[...])
        dgq_ref[...] += jnp.sum(dcqn * cqh, axis=0, keepdims=True)
        dch = dcqn * gq_ref[...]
        dhl_ref[:, :Q_RANK] = (rq * (dch - cqh * jnp.mean(dch * cqh, axis=-1, keepdims=True))).astype(BF16)

        ckv = hl_ref[:, Q_RANK:Q_RANK + KV_RANK]
        rk = lax.rsqrt(jnp.mean(ckv * ckv, axis=-1, keepdims=True) + RMS_EPS)
        ckh = ckv * rk
        ckn = (ckh * gkv_ref[...]).astype(BF16)
        dkb = dk_ref[...].astype(BF16)
        dvb = dv_ref[...].astype(BF16)
        dwk_ref[...] += _dot_tn(ckn, dkb)
        dwv_ref[...] += _dot_tn(ckn, dvb)
        dckn = _dot_nt(dkb, wk_ref[...]) + _dot_nt(dvb, wv_ref[...])
        dgkv_ref[...] += jnp.sum(dckn * ckh, axis=0, keepdims=True)
        dkh = dckn * gkv_ref[...]
        dhl_ref[:, Q_RANK:Q_RANK + KV_RANK] = (rk * (dkh - ckh * jnp.mean(dkh * ckh, axis=-1, keepdims=True))).astype(BF16)
        dhl_ref[:, Q_RANK + KV_RANK:] = dkpe.astype(BF16)

    acc_shapes = [wuq.shape, wk.shape, wv.shape, g_q.shape, g_kv.shape]
    return pl.pallas_call(
        body, name="lat_bwd", grid=(s // ts,),
        in_specs=[pl.BlockSpec((qk_w, ts), lambda i: (0, i)), _row_spec(ts, qk_w), _row_spec(ts, MLA_W),
                  _row_spec(ts, LAT_W), _row_spec(ts, HEAD_PAD), _row_spec(ts, HEAD_PAD), _row_spec(ts, HEAD_PAD),
                  _full_spec(g_q.shape), _full_spec(g_kv.shape), _full_spec(wuq.shape), _full_spec(wk.shape),
                  _full_spec(wv.shape)],
        out_specs=[_row_spec(ts, LAT_W)] + [_full_spec(sh) for sh in acc_shapes],
        out_shape=[jax.ShapeDtypeStruct((s, LAT_W), BF16)] + [jax.ShapeDtypeStruct(sh, F32) for sh in acc_shapes],
        compiler_params=_params(),
    )(dq, dk, dv, hl, rc, rsl, rsh, g_q, g_kv, wuq, wk, wv)


def _dx(dr, dhg, dhm, dhl, w_pad):
    s = dr.shape[0]
    ts = ROW_TILE

    def body(dr_ref, dhg_ref, dhm_ref, dhl_ref, w_ref, dx_ref):
        dx_ref[...] = (ALPHA * dr_ref[...]
                       + _dot_nt(dhg_ref[...], w_ref[:, :GATE_W])
                       + _dot_nt(dhm_ref[...], w_ref[:, GATE_W:GATE_W + MID_W])
                       + _dot_nt(dhl_ref[...], w_ref[:, GATE_W + MID_W:]))

    return pl.pallas_call(
        body, name="dx", grid=(s // ts,),
        in_specs=[_row_spec(ts, D_MODEL), _row_spec(ts, GATE_W), _row_spec(ts, MID_W), _row_spec(ts, LAT_W),
                  _full_spec(w_pad.shape)],
        out_specs=_row_spec(ts, D_MODEL),
        out_shape=jax.ShapeDtypeStruct((s, D_MODEL), F32),
        compiler_params=_params(),
    )(dr, dhg, dhm, dhl, w_pad)


def _dw_in(xt, dh, tn, name):
    s, width = dh.shape

    def body(xt_ref, dh_ref, dw_ref, db_ref):
        dhb = dh_ref[...]
        dw_ref[...] = _dot(xt_ref[...], dhb).astype(BF16)
        db_ref[...] = jnp.sum(dhb.astype(F32), axis=0, keepdims=True)

    return pl.pallas_call(
        body, name=name, grid=(width // tn,),
        in_specs=[_full_spec(xt.shape), pl.BlockSpec((s, tn), lambda n: (0, n))],
        out_specs=[pl.BlockSpec((D_MODEL, tn), lambda n: (0, n)), pl.BlockSpec((1, tn), lambda n: (0, n))],
        out_shape=[jax.ShapeDtypeStruct((D_MODEL, width), BF16), jax.ShapeDtypeStruct((1, width), F32)],
        compiler_params=_params(),
    )(xt, dh)


_C_Q, _C_KV, _K_PE, _Z_A, _U, _V, _Z_B, _G_A, _G_B = (
    (0, 384), (384, 512), (512, 544), (544, 1056), (1056, 1568), (1568, 2080), (2080, 2592), (2592, 3616),
    (3616, 4640))


def _pad_in_cols(w):
    cut = lambda ab: w[:, ab[0]:ab[1]]
    z = lambda n: jnp.zeros((w.shape[0], n), w.dtype)
    return jnp.concatenate([cut(_G_A), cut(_G_B), cut(_Z_A), cut(_U), cut(_V), cut(_Z_B), cut(_C_Q), cut(_C_KV),
                            z(NOPE), cut(_K_PE), z(HEAD_PAD - QK_DIM)], axis=1)


def _unpad_in_cols(g, m, l):
    kpe = l[:, Q_RANK + KV_RANK + NOPE:Q_RANK + KV_RANK + QK_DIM]
    return jnp.concatenate([l[:, :Q_RANK + KV_RANK], kpe, m, g], axis=1)


def _rope_tables(positions):
    half = ROPE // 2
    inv_freq = ROPE_THETA ** (-jnp.arange(0, ROPE, 2, dtype=F32) / ROPE)
    ang = positions.astype(F32)[:, None] * inv_freq
    cos, sin = jnp.cos(ang), jnp.sin(ang)
    n = positions.shape[0]
    one, zero = jnp.ones((n, NOPE), F32), jnp.zeros((n, half), F32)
    tail1, tail0 = jnp.ones((n, HEAD_PAD - QK_DIM), F32), jnp.zeros((n, HEAD_PAD - QK_DIM), F32)
    z64 = jnp.zeros((n, NOPE), F32)
    rc = jnp.concatenate([one, cos, cos, tail1], axis=1)
    rsl = jnp.concatenate([z64, -sin, zero, tail0], axis=1)
    rsh = jnp.concatenate([z64, zero, sin, tail0], axis=1)
    return rc, rsl, rsh


def _local_head(x, positions, tgt, w_in, b_in, g_q, w_uq, g_kv, w_ukv, w_oa, sg_g, sg_b, w_s, b_s, w_ob, w_out,
                ln_g, ln_b):
    rc, rsl, rsh = _rope_tables(positions)
    w_pad = _pad_in_cols(w_in)
    b_pad = _pad_in_cols(b_in[None, :])
    wuq = jnp.pad(w_uq, ((0, 0), (0, 0), (0, HEAD_PAD - QK_DIM))).reshape(Q_RANK, HEADS * HEAD_PAD).astype(BF16)
    wk = jnp.pad(w_ukv[:, :, :NOPE], ((0, 0), (0, 0), (0, HEAD_PAD - NOPE))).reshape(KV_RANK, HEADS * HEAD_PAD).astype(BF16)
    wv = w_ukv[:, :, NOPE:].reshape(KV_RANK, MLA_W).astype(BF16)
    bsb = jnp.repeat(b_s.T, V_DIM, axis=1)
    gq2, gkv2 = g_q[None, :], g_kv[None, :]

    hg, hm, hl, q, k, v, xt, qt, kt, vt = _fwd_pre(x, w_pad, b_pad, gq2, wuq, gkv2, wk, wv, rc, rsl, rsh)
    o, lse = _attn_fwd(qt, k, vt)
    (dr, dhg, dhm, do, dot, delta, dwout, dwoa, dwob, dws, dbs, dlng, dlnb, dsgg, dsgb, loss) = _mid(
        x, tgt, o, hm, hg, w_oa, w_ob, w_out, ln_g[None, :], ln_b[None, :], sg_g[None, :], sg_b[None, :], w_s, bsb)
    delta = jnp.pad(delta.reshape(HEADS // 2, 2, -1), ((0, 0), (0, 6), (0, 0)))
    dwg, dbg = _dw_in(xt, dhg, 512, "dw_in_gates")
    dwm, dbm = _dw_in(xt, dhm, 512, "dw_in_mid")
    early = {
        "w_in": jnp.concatenate([jnp.zeros((D_MODEL, LAT_COLS), BF16), dwm, dwg], axis=1),
        "w_oa": dwoa, "sgu_ln_g": dsgg[0], "sgu_ln_b": dsgb[0], "w_s": dws, "b_s": dbs[:, :GROUPS].T,
        "w_ob": dwob, "w_out": dwout, "ln_g": dlng[0], "ln_b": dlnb[0],
    }
    state = dict(q=q, qt=qt, k=k, kt=kt, v=v, do=do, dot=dot, lse=lse, delta=delta, hl=hl, rc=rc, rsl=rsl, rsh=rsh,
                 gq2=gq2, gkv2=gkv2, wuq=wuq, wk=wk, wv=wv, dr=dr, dhg=dhg, dhm=dhm, w_pad=w_pad, xt=xt, dbg=dbg,
                 dbm=dbm)
    return loss, early, state


def _local_tail(st):
    dq, dk, dv = _attn_bwd(st["q"], st["qt"], st["k"], st["kt"], st["v"], st["do"], st["dot"], st["lse"], st["delta"])
    dhl, dwuq, dwk, dwv, dgq, dgkv = _lat_bwd(dq, dk, dv, st["hl"], st["rc"], st["rsl"], st["rsh"], st["gq2"],
                                              st["gkv2"], st["wuq"], st["wk"], st["wv"])
    dx = _dx(st["dr"], st["dhg"], st["dhm"], dhl, st["w_pad"])
    dwl, dbl = _dw_in(st["xt"], dhl, LAT_W, "dw_in_lat")
    kpe = slice(Q_RANK + KV_RANK + NOPE, Q_RANK + KV_RANK + QK_DIM)
    late = {
        "w_lat": jnp.concatenate([dwl[:, :Q_RANK + KV_RANK], dwl[:, kpe]], axis=1),
        "b_in": _unpad_in_cols(st["dbg"], st["dbm"], dbl)[0],
        "g_q": dgq[0],
        "w_uq": dwuq.reshape(Q_RANK, HEADS, HEAD_PAD)[:, :, :QK_DIM],
        "g_kv": dgkv[0],
        "w_ukv": jnp.concatenate([dwk.reshape(KV_RANK, HEADS, HEAD_PAD)[:, :, :NOPE],
                                  dwv.reshape(KV_RANK, HEADS, V_DIM)], axis=2),
    }
    return dx, late


def _local_step(*args):
    loss, early, st = _local_head(*args)
    dx, late = _local_tail(st)
    grads = {**early, **late}
    grads["w_in"] = jnp.concatenate([grads.pop("w_lat"), early["w_in"][:, LAT_COLS:]], axis=1)
    return loss, dx, grads


MESH = pl.DeviceIdType.MESH
N_CHIPS = 4
HBM_SPEC = pl.BlockSpec(memory_space=pl.ANY)
HBM_SPEC_STRICT = pl.BlockSpec(memory_space=pltpu.HBM)
VMEM_SPEC = pl.BlockSpec(memory_space=pltpu.VMEM)

REP_ROWS = 224


def _place():
    x, y, c = lax.axis_index("x"), lax.axis_index("y"), lax.axis_index("c")
    others = [(1 - x, y), (x, 1 - y), (1 - x, 1 - y)]
    return x, y, c, others


def _gather_weights(shards):
    n = len(shards)

    def body(*refs):
        ins, outs, bufs = refs[:n], refs[n:2 * n], refs[2 * n:3 * n]
        send_sems, recv_sems, local_sems = refs[3 * n:]
        x, y, c, others = _place()
        me = 2 * x + y
        sibling = (x, y, 1 - c)
        for src, buf in zip(ins, bufs):
            buf[...] = src[...].astype(BF16)
        own = [pltpu.make_async_copy(bufs[w], outs[w].at[me], local_sems.at[w]) for w in range(n)]
        for cp in own:
            cp.start()

        def part(w, chip, half):
            hr = shards[w].shape[0] // 2
            return outs[w].at[chip, pl.ds(half * hr, hr), :]

        def sent(w, j):
            hr = shards[w].shape[0] // 2
            return pltpu.make_async_remote_copy(
                src_ref=bufs[w].at[pl.ds(c * hr, hr), :], dst_ref=part(w, me, c),
                send_sem=send_sems.at[w * 3 + j], recv_sem=recv_sems.at[w * 3 + j],
                device_id=(*others[j], c), device_id_type=MESH)

        def landed(w, j):
            px, py = others[j]
            return pltpu.make_async_remote_copy(
                src_ref=part(w, 2 * px + py, c), dst_ref=part(w, 2 * px + py, c),
                send_sem=send_sems.at[w * 3 + j], recv_sem=recv_sems.at[w * 3 + j],
                device_id=(px, py, c), device_id_type=MESH)

        def passed(w, j, half):
            px, py = others[j]
            k = n * 3 + w * 3 + j
            return pltpu.make_async_remote_copy(
                src_ref=part(w, 2 * px + py, half), dst_ref=part(w, 2 * px + py, half),
                send_sem=send_sems.at[k], recv_sem=recv_sems.at[k], device_id=sibling, device_id_type=MESH)

        first = [sent(w, j) for w in range(n) for j in range(3)]
        for cp in first:
            cp.start()
        fwd = []
        for w in range(n):
            for j in range(3):
                landed(w, j).wait_recv()
                cp = passed(w, j, c)
                cp.start()
                fwd.append(cp)
        for w in range(n):
            for j in range(3):
                passed(w, j, 1 - c).wait_recv()
        for cp in first + fwd:
            cp.wait_send()
        for cp in own:
            cp.wait()

    return pl.pallas_call(
        body, name="gather_weights",
        in_specs=[VMEM_SPEC] * n, out_specs=[HBM_SPEC] * n,
        out_shape=[jax.ShapeDtypeStruct((N_CHIPS,) + s.shape, BF16) for s in shards],
        scratch_shapes=[pltpu.VMEM(s.shape, BF16) for s in shards]
        + [pltpu.SemaphoreType.DMA((6 * n,)), pltpu.SemaphoreType.DMA((6 * n,)), pltpu.SemaphoreType.DMA((n,))],
        compiler_params=pltpu.CompilerParams(vmem_limit_bytes=VMEM_LIMIT),
    )(*shards)


N_DEV = 8
LOSS_TILE = (8, 128)


def _half(a):
    return a.shape[1] // 2


def _exchange_pairs(parts, name):
    n = len(parts)

    def body(*refs):
        p_refs, r_refs = refs[:n], refs[n:2 * n]
        send_sems, recv_sems = refs[2 * n:]
        x, y, c, _ = _place()
        cps = []
        for w in range(n):
            h = _half(parts[w])
            cps.append(pltpu.make_async_remote_copy(
                src_ref=p_refs[w].at[:, pl.ds((1 - c) * h, h), :], dst_ref=r_refs[w],
                send_sem=send_sems.at[w], recv_sem=recv_sems.at[w], device_id=(x, y, 1 - c), device_id_type=MESH))
        for cp in cps:
            cp.start()
        for cp in cps:
            cp.wait()

    return pl.pallas_call(
        body, name=name, in_specs=[HBM_SPEC] * n, out_specs=[HBM_SPEC] * n,
        out_shape=[jax.ShapeDtypeStruct((N_CHIPS, _half(p), p.shape[2]), BF16) for p in parts],
        scratch_shapes=[pltpu.SemaphoreType.DMA((n,)), pltpu.SemaphoreType.DMA((n,))],
    )(*parts)


def _exchange_pairs_and_loss(parts, loss):
    n = len(parts)

    def body(*refs):
        p_refs, loss_ref = refs[:n], refs[n]
        r_refs, all_loss_ref = refs[n + 1:2 * n + 1], refs[2 * n + 1]
        send_sems, recv_sems, loss_send, loss_recv, local_sem = refs[2 * n + 2:]
        x, y, c, _ = _place()
        sibling = (x, y, 1 - c)
        cps = []
        for w in range(n):
            h = _half(parts[w])
            cps.append(pltpu.make_async_remote_copy(
                src_ref=p_refs[w].at[:, pl.ds((1 - c) * h, h), :], dst_ref=r_refs[w],
                send_sem=send_sems.at[w], recv_sem=recv_sems.at[w], device_id=sibling, device_id_type=MESH))
        for cp in cps:
            cp.start()
        me = 4 * x + 2 * y + c
        own = pltpu.make_async_copy(loss_ref, all_loss_ref.at[me], local_sem)
        own.start()
        lcs = []
        for t in range(1, N_DEV):
            d = (me + t) % N_DEV
            lcs.append(pltpu.make_async_remote_copy(
                src_ref=loss_ref, dst_ref=all_loss_ref.at[me], send_sem=loss_send.at[t - 1],
                recv_sem=loss_recv.at[t - 1], device_id=(d // 4, (d // 2) % 2, d % 2), device_id_type=MESH))
        for cp in lcs:
            cp.start()
        for t in range(1, N_DEV):
            d = (me + N_DEV - t) % N_DEV
            pltpu.make_async_remote_copy(
                src_ref=loss_ref, dst_ref=all_loss_ref.at[d], send_sem=loss_send.at[t - 1],
                recv_sem=loss_recv.at[t - 1], device_id=(d // 4, (d // 2) % 2, d % 2), device_id_type=MESH).wait_recv()
        for cp in lcs:
            cp.wait_send()
        for cp in cps:
            cp.wait()
        own.wait()

    return pl.pallas_call(
        body, name="exchange_pairs_and_loss", in_specs=[HBM_SPEC] * (n + 1), out_specs=[HBM_SPEC] * (n + 1),
        out_shape=[jax.ShapeDtypeStruct((N_CHIPS, _half(p), p.shape[2]), BF16) for p in parts]
        + [jax.ShapeDtypeStruct((N_DEV,) + LOSS_TILE, F32)],
        scratch_shapes=[pltpu.SemaphoreType.DMA((n,)), pltpu.SemaphoreType.DMA((n,)),
                        pltpu.SemaphoreType.DMA((N_DEV - 1,)), pltpu.SemaphoreType.DMA((N_DEV - 1,)),
                        pltpu.SemaphoreType.DMA],
    )(*parts, loss)


def _add_pair_tiled(p, r, c, tile):
    nt = _half(p) // tile
    cols = p.shape[2]

    def body(c_ref, p_ref, r_ref, q_ref):
        q_ref[...] = (p_ref[...].astype(F32) + r_ref[...].astype(F32)).astype(BF16)

    return pl.pallas_call(
        body, name="add_pair_w_in",
        grid_spec=pltpu.PrefetchScalarGridSpec(
            num_scalar_prefetch=1, grid=(N_CHIPS, nt),
            in_specs=[pl.BlockSpec((None, tile, cols), lambda k, i, c_ref: (k, c_ref[0] * nt + i, 0)),
                      pl.BlockSpec((None, tile, cols), lambda k, i, c_ref: (k, i, 0))],
            out_specs=pl.BlockSpec((None, tile, cols), lambda k, i, c_ref: (k, i, 0))),
        out_shape=jax.ShapeDtypeStruct(r.shape, BF16),
    )(c, p, r)


def _add_pair_small(ps, rs, c, name):
    n = len(ps)

    def body(c_ref, *refs):
        for w in range(n):
            h = _half(ps[w])
            mine = refs[w][:, pl.ds(pl.multiple_of(c_ref[0] * h, 16), h), :]
            refs[2 * n + w][...] = (mine.astype(F32) + refs[n + w][...].astype(F32)).astype(BF16)

    return pl.pallas_call(
        body, name=name,
        in_specs=[pl.BlockSpec(memory_space=pltpu.SMEM)] + [VMEM_SPEC] * (2 * n), out_specs=[VMEM_SPEC] * n,
        out_shape=[jax.ShapeDtypeStruct(r.shape, BF16) for r in rs],
        compiler_params=pltpu.CompilerParams(vmem_limit_bytes=VMEM_LIMIT),
    )(c, *ps, *rs)


def _exchange_chips(qs):
    n = len(qs)

    def body(*refs):
        q_refs, r_refs = refs[:n], refs[n:2 * n]
        send_sems, recv_sems = refs[2 * n:]
        x, y, c, others = _place()
        me = 2 * x + y
        cps = []
        for w in range(n):
            for j, (px, py) in enumerate(others):
                cps.append(pltpu.make_async_remote_copy(
                    src_ref=q_refs[w].at[2 * px + py], dst_ref=r_refs[w].at[me], send_sem=send_sems.at[3 * w + j],
                    recv_sem=recv_sems.at[3 * w + j], device_id=(px, py, c), device_id_type=MESH))
        for cp in cps:
            cp.start()
        for w in range(n):
            for j, (px, py) in enumerate(others):
                pltpu.make_async_remote_copy(
                    src_ref=q_refs[w].at[me], dst_ref=r_refs[w].at[2 * px + py], send_sem=send_sems.at[3 * w + j],
                    recv_sem=recv_sems.at[3 * w + j], device_id=(px, py, c), device_id_type=MESH).wait_recv()
        for cp in cps:
            cp.wait_send()

    return pl.pallas_call(
        body, name="exchange_chips", in_specs=[HBM_SPEC] * n, out_specs=[HBM_SPEC] * n,
        out_shape=[jax.ShapeDtypeStruct(q.shape, BF16) for q in qs],
        scratch_shapes=[pltpu.SemaphoreType.DMA((3 * n,)), pltpu.SemaphoreType.DMA((3 * n,))],
    )(*qs)


SEM_SPEC = pl.BlockSpec(memory_space=pltpu.SEMAPHORE)
SPLIT_EFFECT = pltpu.SideEffectType.DATAFLOW_SIDE_EFFECTING


def _chips_start(qs):
    n = len(qs)

    def body(*refs):
        q_refs, land_refs = refs[:n], refs[n:2 * n]
        send_sems, recv_sems, token = refs[2 * n], refs[2 * n + 1], refs[-1]
        x, y, c, others = _place()
        me = 2 * x + y
        for w in range(n):
            for j, (px, py) in enumerate(others):
                pltpu.make_async_remote_copy(
                    src_ref=q_refs[w].at[2 * px + py], dst_ref=land_refs[w].at[me], send_sem=send_sems.at[3 * w + j],
                    recv_sem=recv_sems.at[3 * w + j], device_id=(px, py, c), device_id_type=MESH).start()
        token[...] = jnp.zeros_like(token)

    hbm = [pltpu.HBM(q.shape, BF16) for q in qs]
    outs = pl.pallas_call(
        body, name="chips_start",
        out_shape=(pltpu.SemaphoreType.DMA((3 * n,)), pltpu.SemaphoreType.DMA((3 * n,)), *hbm, *hbm,
                   jax.ShapeDtypeStruct(LOSS_TILE, F32)),
        in_specs=[HBM_SPEC_STRICT] * (2 * n),
        out_specs=(SEM_SPEC, SEM_SPEC, *[HBM_SPEC_STRICT] * (2 * n), VMEM_SPEC),
        input_output_aliases={i: 2 + i for i in range(2 * n)},
        compiler_params=pltpu.CompilerParams(has_side_effects=SPLIT_EFFECT),
    )(*[pltpu.with_memory_space_constraint(q, pltpu.HBM) for q in qs],
      *[pltpu.with_memory_space_constraint(lax.empty(q.shape, BF16), pltpu.HBM) for q in qs])
    return outs[0], outs[1], outs[2:2 + n], outs[2 + n:2 + 2 * n], outs[-1]


def _chips_wait(send_sems, recv_sems, q_thru, land_thru, after):
    n = len(q_thru)

    def body(*refs):
        q_refs, land_refs = refs[:n], refs[n:2 * n]
        send_sems, recv_sems = refs[2 * n], refs[2 * n + 1]
        x, y, c, others = _place()
        me = 2 * x + y
        for w in range(n):
            for j, (px, py) in enumerate(others):
                cp = pltpu.make_async_remote_copy(
                    src_ref=q_refs[w].at[2 * px + py], dst_ref=land_refs[w].at[2 * px + py],
                    send_sem=send_sems.at[3 * w + j], recv_sem=recv_sems.at[3 * w + j], device_id=(px, py, c),
                    device_id_type=MESH)
                cp.wait_send()
                cp.wait_recv()

    outs = pl.pallas_call(
        body, name="chips_wait", out_shape=tuple(pltpu.HBM(a.shape, a.dtype) for a in (*q_thru, *land_thru)),
        in_specs=[HBM_SPEC_STRICT] * (2 * n) + [SEM_SPEC, SEM_SPEC, HBM_SPEC],
        out_specs=tuple([HBM_SPEC_STRICT] * (2 * n)), input_output_aliases={i: i for i in range(2 * n)},
        compiler_params=pltpu.CompilerParams(has_side_effects=SPLIT_EFFECT),
    )(*q_thru, *land_thru, send_sems, recv_sems, after)
    return list(outs[:n]), list(outs[n:])


def _sum_chips_tiled(q, r, idx, tile):
    rows, cols = r.shape[1:]
    nt = rows // tile

    def body(idx_ref, q_ref, r0_ref, r1_ref, r2_ref, g_ref):
        g_ref[...] = (q_ref[...].astype(F32) + r0_ref[...].astype(F32) + r1_ref[...].astype(F32)
                      + r2_ref[...].astype(F32))

    def slab(t):
        return pl.BlockSpec((None, tile, cols), lambda i, idx_ref: (idx_ref[t], i, 0))

    return pl.pallas_call(
        body, name="sum_chips_w_in",
        grid_spec=pltpu.PrefetchScalarGridSpec(
            num_scalar_prefetch=1, grid=(nt,), in_specs=[slab(0), slab(1), slab(2), slab(3)],
            out_specs=pl.BlockSpec((tile, cols), lambda i, idx_ref: (idx_ref[4] * nt + i, 0))),
        out_shape=jax.ShapeDtypeStruct((2 * rows, cols), F32),
    )(idx, q, r, r, r)


def _sum_chips_small(qs, rs, idx):
    n = len(rs)

    def body(idx_ref, *refs):
        c = idx_ref[4]
        for w in range(n):
            q_ref, r_ref, g_ref = refs[w], refs[n + w], refs[2 * n + w]
            acc = q_ref[idx_ref[0]].astype(F32)
            for t in range(1, N_CHIPS):
                acc = acc + r_ref[idx_ref[t]].astype(F32)
            h = rs[w].shape[1]
            mine = pl.ds(pl.multiple_of(c * h, 8), h)
            g_ref[...] = jnp.zeros_like(g_ref)
            if w == n - 1:
                g_ref[idx_ref[0], mine, :] = acc
            else:
                g_ref[mine, :] = acc

    shapes = [jax.ShapeDtypeStruct((2 * r.shape[1], r.shape[2]), F32) for r in rs[:-1]]
    shapes.append(jax.ShapeDtypeStruct((N_CHIPS, 2 * rs[-1].shape[1], rs[-1].shape[2]), F32))
    return pl.pallas_call(
        body, name="sum_chips_small",
        in_specs=[pl.BlockSpec(memory_space=pltpu.SMEM)] + [VMEM_SPEC] * (2 * n), out_specs=[VMEM_SPEC] * n,
        out_shape=shapes, compiler_params=pltpu.CompilerParams(vmem_limit_bytes=VMEM_LIMIT),
    )(idx, *qs, *rs)


def _share(shards, rep):
    n = len(shards)
    hr = rep.shape[1] // 2

    def body(*refs):
        g_refs, all_ref = refs[n + 1:2 * n + 1], refs[2 * n + 1]
        send_sems, recv_sems = refs[2 * n + 2:]
        x, y, c, others = _place()
        me = 2 * x + y
        sibling = (x, y, 1 - c)

        def rows_of(w, half):
            h = shards[w].shape[0] // 2
            return g_refs[w].at[pl.ds(half * h, h), :]

        def quarter(chip, half):
            return all_ref.at[chip, pl.ds(half * hr, hr), :]

        cps = [pltpu.make_async_remote_copy(src_ref=rows_of(w, c), dst_ref=rows_of(w, c), send_sem=send_sems.at[w],
                                            recv_sem=recv_sems.at[w], device_id=sibling, device_id_type=MESH)
               for w in range(n)]
        cps.append(pltpu.make_async_remote_copy(src_ref=quarter(me, c), dst_ref=quarter(me, c),
                                                send_sem=send_sems.at[n], recv_sem=recv_sems.at[n],
                                                device_id=sibling, device_id_type=MESH))
        for j, (px, py) in enumerate(others):
            cps.append(pltpu.make_async_remote_copy(
                src_ref=quarter(me, c), dst_ref=quarter(me, c), send_sem=send_sems.at[n + 1 + j],
                recv_sem=recv_sems.at[n + 1 + j], device_id=(px, py, c), device_id_type=MESH))
        for cp in cps:
            cp.start()

        def passed(j, half):
            px, py = others[j]
            return pltpu.make_async_remote_copy(
                src_ref=quarter(2 * px + py, half), dst_ref=quarter(2 * px + py, half),
                send_sem=send_sems.at[n + 4 + j], recv_sem=recv_sems.at[n + 4 + j], device_id=sibling,
                device_id_type=MESH)

        fwd = []
        for j, (px, py) in enumerate(others):
            pltpu.make_async_remote_copy(
                src_ref=quarter(me, c), dst_ref=quarter(2 * px + py, c), send_sem=send_sems.at[n + 1 + j],
                recv_sem=recv_sems.at[n + 1 + j], device_id=(px, py, c), device_id_type=MESH).wait_recv()
            cp = passed(j, c)
            cp.start()
            fwd.append(cp)
        for j in range(3):
            passed(j, 1 - c).wait_recv()
        for w in range(n):
            pltpu.make_async_remote_copy(
                src_ref=rows_of(w, c), dst_ref=rows_of(w, 1 - c), send_sem=send_sems.at[w],
                recv_sem=recv_sems.at[w], device_id=sibling, device_id_type=MESH).wait_recv()
        pltpu.make_async_remote_copy(
            src_ref=quarter(me, c), dst_ref=quarter(me, 1 - c), send_sem=send_sems.at[n], recv_sem=recv_sems.at[n],
            device_id=sibling, device_id_type=MESH).wait_recv()
        for cp in cps + fwd:
            cp.wait_send()

    return pl.pallas_call(
        body, name="share", in_specs=[HBM_SPEC] * (n + 1), out_specs=[HBM_SPEC] * (n + 1),
        out_shape=[jax.ShapeDtypeStruct(a.shape, F32) for a in (*shards, rep)],
        input_output_aliases={i: i for i in range(n + 1)},
        scratch_shapes=[pltpu.SemaphoreType.DMA((n + 7,)), pltpu.SemaphoreType.DMA((n + 7,))],
    )(*shards, rep)


def _adamw(w, g, m, v):
    m2 = ADAM_B1 * m + (1.0 - ADAM_B1) * g
    v2 = ADAM_B2 * v + (1.0 - ADAM_B2) * (g * g)
    m_hat = m2 / (1.0 - ADAM_B1 ** ADAM_STEP)
    v_hat = v2 / (1.0 - ADAM_B2 ** ADAM_STEP)
    return -ADAM_LR * (m_hat / (jnp.sqrt(v_hat) + ADAM_EPS) + ADAM_WD * w), m2, v2


def _update_tiled(w, g, m, v, tile):
    def body(w_ref, g_ref, m_ref, v_ref, d_ref, m2_ref, v2_ref):
        d_ref[...], m2_ref[...], v2_ref[...] = _adamw(w_ref[...], g_ref[...], m_ref[...], v_ref[...])

    spec = pl.BlockSpec((tile, w.shape[1]), lambda i: (i, 0))
    return pl.pallas_call(
        body, name="update_w_in", grid=(w.shape[0] // tile,), in_specs=[spec] * 4, out_specs=[spec] * 3,
        out_shape=[jax.ShapeDtypeStruct(w.shape, F32)] * 3,
        compiler_params=_params(("parallel",)),
    )(w, g, m, v)


def _update_small(ws, gs, ms, vs):
    n = len(ws)

    def body(*refs):
        for k in range(n):
            w_ref, g_ref, m_ref, v_ref = refs[k], refs[n + k], refs[2 * n + k], refs[3 * n + k]
            d, m2, v2 = _adamw(w_ref[...], g_ref[...], m_ref[...], v_ref[...])
            refs[4 * n + k][...] = d
            refs[5 * n + k][...] = m2
            refs[6 * n + k][...] = v2

    shapes = [jax.ShapeDtypeStruct(w.shape, F32) for w in ws]
    outs = pl.pallas_call(
        body, name="update_small", in_specs=[VMEM_SPEC] * (4 * n), out_specs=[VMEM_SPEC] * (3 * n),
        out_shape=shapes * 3,
        compiler_params=pltpu.CompilerParams(vmem_limit_bytes=VMEM_LIMIT),
    )(*ws, *gs, *ms, *vs)
    return outs[:n], outs[n:2 * n], outs[2 * n:]


SHARDED = ("w_in", "w_uq", "w_oa", "w_ob", "w_out")
REPLICATED = ("b_in", "g_q", "g_kv", "w_ukv", "sgu_ln_g", "sgu_ln_b", "w_s", "b_s", "ln_g", "ln_b")
ORDER = ("w_in", "b_in", "g_q", "w_uq", "g_kv", "w_ukv", "w_oa", "sgu_ln_g", "sgu_ln_b", "w_s", "b_s", "w_ob", "w_out",
         "ln_g", "ln_b")


def kernel(x, positions, w_in, b_in, g_q, w_uq, g_kv, w_ukv, w_oa, sgu_ln_g, sgu_ln_b, w_s, b_s, w_ob, w_out, ln_g, ln_b, loss_target, m_w_in, m_b_in, m_g_q, m_w_uq, m_g_kv, m_w_ukv, m_w_oa, m_sgu_ln_g, m_sgu_ln_b, m_w_s, m_b_s, m_w_ob, m_w_out, m_ln_g, m_ln_b, v_w_in, v_b_in, v_g_q, v_w_uq, v_g_kv, v_w_ukv, v_w_oa, v_sgu_ln_g, v_sgu_ln_b, v_w_s, v_b_s, v_w_ob, v_w_out, v_ln_g, v_ln_b):
    w = dict(w_in=w_in, b_in=b_in, g_q=g_q, w_uq=w_uq, g_kv=g_kv, w_ukv=w_ukv, w_oa=w_oa, sgu_ln_g=sgu_ln_g,
             sgu_ln_b=sgu_ln_b, w_s=w_s, b_s=b_s, w_ob=w_ob, w_out=w_out, ln_g=ln_g, ln_b=ln_b)
    m = dict(w_in=m_w_in, b_in=m_b_in, g_q=m_g_q, w_uq=m_w_uq, g_kv=m_g_kv, w_ukv=m_w_ukv, w_oa=m_w_oa,
             sgu_ln_g=m_sgu_ln_g, sgu_ln_b=m_sgu_ln_b, w_s=m_w_s, b_s=m_b_s, w_ob=m_w_ob, w_out=m_w_out, ln_g=m_ln_g,
             ln_b=m_ln_b)
    v = dict(w_in=v_w_in, b_in=v_b_in, g_q=v_g_q, w_uq=v_w_uq, g_kv=v_g_kv, w_ukv=v_w_ukv, w_oa=v_w_oa,
             sgu_ln_g=v_sgu_ln_g, sgu_ln_b=v_sgu_ln_b, w_s=v_w_s, b_s=v_b_s, w_ob=v_w_ob, w_out=v_w_out, ln_g=v_ln_g,
             ln_b=v_ln_b)
    w, m, v = ({n: a[0] for n, a in d.items()} for d in (w, m, v))
    c = lax.axis_index("c")

    g_in, g_uq, g_oa, g_ob, g_out = _gather_weights(
        [w["w_in"], w["w_uq"].reshape(Q_RANK // 4, HEADS * QK_DIM), w["w_oa"], w["w_ob"], w["w_out"]])
    full_in = jnp.transpose(g_in, (1, 0, 2)).reshape(D_MODEL, IN_W)
    full_uq = g_uq.reshape(Q_RANK, HEADS, QK_DIM)
    full_out = g_out.reshape(D_MODEL, D_MODEL)

    loss, early, st = _local_head(
        x[0], positions[0], loss_target[0], full_in, w["b_in"], w["g_q"], full_uq, w["g_kv"], w["w_ukv"], g_oa,
        w["sgu_ln_g"], w["sgu_ln_b"], w["w_s"], w["b_s"], g_ob, full_out, w["ln_g"], w["ln_b"])

    xi, yi = lax.axis_index("x"), lax.axis_index("y")
    c1 = c.reshape(1).astype(jnp.int32)
    idx = jnp.stack([2 * xi + yi, 2 * (1 - xi) + yi, 2 * xi + (1 - yi), 2 * (1 - xi) + (1 - yi), c]).astype(jnp.int32)
    parts1 = [jnp.transpose(early["w_in"].reshape(D_MODEL, N_CHIPS, IN_W // N_CHIPS), (1, 0, 2)),
              early["w_oa"].astype(BF16), early["w_ob"].astype(BF16),
              early["w_out"].reshape(N_CHIPS, SLAB_W, D_MODEL).astype(BF16)]
    *recv1, all_loss = _exchange_pairs_and_loss(parts1, jnp.broadcast_to(loss, LOSS_TILE))
    pairs1 = [_add_pair_tiled(parts1[0], recv1[0], c1, 128),
              *_add_pair_small(parts1[1:], recv1[1:], c1, "add_pair_early")]
    send_sems, recv_sems, q_thru, land_thru, token = _chips_start(pairs1)

    st["delta"] = st["delta"] + token[0, 0]
    dx, late = _local_tail(st)

    grads = {**early, **late}
    rep = jnp.concatenate([grads[n].reshape(-1) for n in REPLICATED] + [late["w_lat"].astype(F32).reshape(-1)])
    rep = jnp.pad(rep, (0, N_CHIPS * REP_ROWS * D_MODEL - rep.shape[0])).reshape(N_CHIPS, REP_ROWS, D_MODEL)
    parts2 = [late["w_uq"].reshape(N_CHIPS, Q_RANK // N_CHIPS, HEADS * QK_DIM).astype(BF16), rep.astype(BF16)]
    pairs2 = _add_pair_small(parts2, _exchange_pairs(parts2, "exchange_pairs_late"), c1, "add_pair_late")
    landed2 = _exchange_chips(pairs2)
    pairs1, landed1 = _chips_wait(send_sems, recv_sems, q_thru, land_thru, landed2[0])
    sums = [_sum_chips_tiled(pairs1[0], landed1[0], idx, 128),
            *_sum_chips_small([*pairs1[1:], *pairs2], [*landed1[1:], *landed2], idx)]
    *shards, g_rep = _share(sums[:-1], sums[-1])
    loss = jnp.sum(all_loss[:, 0, 0])

    red = {n: s.reshape(w[n].shape) for n, s in zip(("w_in", "w_oa", "w_ob", "w_out", "w_uq"), shards)}
    flat = g_rep.reshape(-1)
    off = 0
    for n in REPLICATED:
        red[n] = flat[off:off + w[n].size].reshape(w[n].shape)
        off += w[n].size
    lat = flat[off:off + D_MODEL * LAT_COLS].reshape(D_MODEL, LAT_COLS)
    red["w_in"] = jnp.where(2 * xi + yi == 0, jnp.concatenate([lat, red["w_in"][:, LAT_COLS:]], axis=1), red["w_in"])

    d_in, m_in, v_in = _update_tiled(w["w_in"], red["w_in"], m["w_in"], v["w_in"], 128)
    small = [n for n in ORDER if n != "w_in"]
    as2d = lambda a: a.reshape(-1, a.shape[-1])
    ds, ms, vs = _update_small([as2d(w[n]) for n in small], [as2d(red[n]) for n in small],
                               [as2d(m[n]) for n in small], [as2d(v[n]) for n in small])
    delta, new_m, new_v = {"w_in": d_in}, {"w_in": m_in}, {"w_in": v_in}
    for i, n in enumerate(small):
        delta[n], new_m[n], new_v[n] = (a[i].reshape(w[n].shape) for a in (ds, ms, vs))

    lead = lambda a: a[None]
    return (loss, dx[None], *[lead(red[n]) for n in ORDER], *[lead(delta[n]) for n in ORDER],
            *[lead(new_m[n]) for n in ORDER], *[lead(new_v[n]) for n in ORDER])
```

```python
import functools
import math

import jax
import jax.numpy as jnp
from jax import lax
from jax.experimental import pallas as pl
from jax.experimental.pallas import tpu as pltpu

F32 = jnp.float32
BF16 = jnp.bfloat16

D_MODEL = 1024
HEADS = 8
Q_RANK = 384
KV_RANK = 128
NOPE = 64
ROPE = 32
V_DIM = 64
QK_DIM = NOPE + ROPE
HEAD_PAD = 128
MLA_W = HEADS * V_DIM
SGU_W = 512
GROUPS = 8
CHUNK = 128
IN_W = 4640
RMS_EPS = 1e-6
LN_EPS = 1e-5
ALPHA = 2.0 ** 0.25
ROPE_THETA = 10000.0
SCALE = QK_DIM ** -0.5

GATE_W = 2 * D_MODEL
MID_W = 4 * SGU_W
LAT_W = Q_RANK + KV_RANK + HEAD_PAD
PAD_W = GATE_W + MID_W + LAT_W
LAT_COLS = Q_RANK + KV_RANK + ROPE
ROW_GATE = LAT_COLS + MID_W
LAT_ROWS_PAD = 576
N_SLABS = 4
SLAB_W = D_MODEL // N_SLABS

ROW_TILE = 256
ATT_TQ = 256
ATT_TK = 256
ATT_BWD_TQ = 256
ATT_BWD_TK = 256
LOG2E = 1.4426950408889634
LN2 = 0.6931471805599453
Q_SCALE = SCALE * LOG2E
VMEM_LIMIT = 56 * 1024 * 1024

ADAM_LR = 0.001
ADAM_B1 = 0.9
ADAM_B2 = 0.999
ADAM_EPS = 1e-08
ADAM_WD = 0.01
ADAM_STEP = 10


def _dot(a, b):
    return jnp.dot(a, b, preferred_element_type=F32)


def _dot_nt(a, b):
    return lax.dot_general(a, b, (((1,), (1,)), ((), ())), preferred_element_type=F32)


def _dot_tn(a, b):
    return lax.dot_general(a, b, (((0,), (0,)), ((), ())), preferred_element_type=F32)


def _sigmoid(z):
    return 1.0 / (1.0 + jnp.exp(-z))


_GELU_C = math.sqrt(2.0 / math.pi)


def _gelu_and_grad(x):
    x2 = x * x
    t = jnp.tanh(_GELU_C * (x + 0.044715 * x * x2))
    g = 0.5 * x * (1.0 + t)
    dg = 0.5 * (1.0 + t) + 0.5 * x * (1.0 - t * t) * (_GELU_C * (1.0 + 3.0 * 0.044715 * x2))
    return g, dg


def _silu_and_grad(z):
    s = _sigmoid(z)
    return z * s, s * (1.0 + z * (1.0 - s))


def _rope(xb, c, sl, sh):
    return xb * c + pltpu.roll(xb, 112, 1) * sl + pltpu.roll(xb, 16, 1) * sh


def _rope_t(dy, c, sl, sh):
    return dy * c + pltpu.roll(dy * sl, 16, 1) + pltpu.roll(dy * sh, 112, 1)


def _params(sem=("arbitrary",)):
    return pltpu.CompilerParams(dimension_semantics=sem, vmem_limit_bytes=VMEM_LIMIT)


def _row_spec(tile, width):
    return pl.BlockSpec((tile, width), lambda i: (i, 0))


def _full_spec(shape):
    nd = len(shape)
    return pl.BlockSpec(shape, lambda i: (0,) * nd)


def _kpe_rows(wt_ref):
    z = lambda n: jnp.zeros((n, D_MODEL), BF16)
    return jnp.concatenate([z(NOPE), wt_ref[Q_RANK + KV_RANK:LAT_COLS, :], z(HEAD_PAD - QK_DIM)], axis=0)


def _fwd_pre(x, wt, b_g, b_m, b_l, g_q, wuq, g_kv, wk, wv, rc, rsl, rsh):
    s = x.shape[0]
    ts = ROW_TILE

    def body(x_ref, wt_ref, bg_ref, bm_ref, bl_ref, gq_ref, wuq_ref, gkv_ref, wk_ref, wv_ref, rc_ref, rsl_ref,
             rsh_ref, hg_ref, hm_ref, hl_ref, q_ref, k_ref, v_ref, xb_ref, qt_ref, kt_ref, vt_ref):
        xb = x_ref[...].astype(BF16)
        xb_ref[...] = xb
        hg_ref[...] = _dot_nt(xb, wt_ref[ROW_GATE:IN_W, :]) + bg_ref[...]
        hm_ref[...] = _dot_nt(xb, wt_ref[LAT_COLS:ROW_GATE, :]) + bm_ref[...]
        hl = jnp.concatenate([_dot_nt(xb, wt_ref[0:Q_RANK + KV_RANK, :]), _dot_nt(xb, _kpe_rows(wt_ref))],
                             axis=1) + bl_ref[...]
        hl_ref[...] = hl
        c, sl, sh = rc_ref[...], rsl_ref[...], rsh_ref[...]
        cq = hl[:, :Q_RANK]
        cqn = cq * lax.rsqrt(jnp.mean(cq * cq, axis=-1, keepdims=True) + RMS_EPS) * gq_ref[...]
        q = _dot(cqn.astype(BF16), wuq_ref[...])
        ckv = hl[:, Q_RANK:Q_RANK + KV_RANK]
        ckvn = (ckv * lax.rsqrt(jnp.mean(ckv * ckv, axis=-1, keepdims=True) + RMS_EPS) * gkv_ref[...]).astype(BF16)
        k = _dot(ckvn, wk_ref[...])
        vb = _dot(ckvn, wv_ref[...]).astype(BF16)
        v_ref[...] = vb
        vt_ref[...] = vb.T
        kpe = _rope(hl[:, Q_RANK + KV_RANK:], c, sl, sh)
        for hd in range(HEADS):
            lanes = slice(hd * HEAD_PAD, (hd + 1) * HEAD_PAD)
            qb = (_rope(q[:, lanes], c, sl, sh) * Q_SCALE).astype(BF16)
            kb = (k[:, lanes] + kpe).astype(BF16)
            q_ref[:, lanes] = qb
            k_ref[:, lanes] = kb
            qt_ref[lanes, :] = qb.T
            kt_ref[lanes, :] = kb.T

    qk_w = HEADS * HEAD_PAD
    col_spec = lambda rows: pl.BlockSpec((rows, ts), lambda i: (0, i))
    return pl.pallas_call(
        body, name="fwd_pre", grid=(s // ts,),
        in_specs=[_row_spec(ts, D_MODEL), _full_spec(wt.shape), _full_spec(b_g.shape), _full_spec(b_m.shape),
                  _full_spec(b_l.shape), _full_spec(g_q.shape),
                  _full_spec(wuq.shape), _full_spec(g_kv.shape), _full_spec(wk.shape), _full_spec(wv.shape),
                  _row_spec(ts, HEAD_PAD), _row_spec(ts, HEAD_PAD), _row_spec(ts, HEAD_PAD)],
        out_specs=[_row_spec(ts, GATE_W), _row_spec(ts, MID_W), _row_spec(ts, LAT_W), _row_spec(ts, qk_w),
                   _row_spec(ts, qk_w), _row_spec(ts, MLA_W), _row_spec(ts, D_MODEL), col_spec(qk_w), col_spec(qk_w),
                   col_spec(MLA_W)],
        out_shape=[jax.ShapeDtypeStruct((s, GATE_W), F32), jax.ShapeDtypeStruct((s, MID_W), F32),
                   jax.ShapeDtypeStruct((s, LAT_W), F32), jax.ShapeDtypeStruct((s, qk_w), BF16),
                   jax.ShapeDtypeStruct((s, qk_w), BF16), jax.ShapeDtypeStruct((s, MLA_W), BF16),
                   jax.ShapeDtypeStruct((s, D_MODEL), BF16), jax.ShapeDtypeStruct((qk_w, s), BF16),
                   jax.ShapeDtypeStruct((qk_w, s), BF16), jax.ShapeDtypeStruct((MLA_W, s), BF16)],
        compiler_params=_params(),
    )(x, wt, b_g, b_m, b_l, g_q, wuq, g_kv, wk, wv, rc, rsl, rsh)


def _attn_fwd(qt, k, vt):
    s = k.shape[0]
    tq, tk = ATT_TQ, ATT_TK
    r = tq // tk
    pairs = HEADS // 2

    def body(qt_ref, k_ref, vt_ref, o_ref, lse_ref):
        i = pl.program_id(1)
        krow = lax.broadcasted_iota(jnp.int32, (tk, tq), 0)
        qcol = lax.broadcasted_iota(jnp.int32, (tk, tq), 1)
        qts = [qt_ref[hh * HEAD_PAD:(hh + 1) * HEAD_PAD, :] for hh in range(2)]

        def scores(j):
            koff = pl.multiple_of(j * tk, tk)
            return tuple(_dot(k_ref[pl.ds(koff, tk), hh * HEAD_PAD:(hh + 1) * HEAD_PAD], qts[hh]) for hh in range(2))

        def weighted(j, ps):
            koff = pl.multiple_of(j * tk, tk)
            return tuple(_dot(vt_ref[hh * V_DIM:(hh + 1) * V_DIM, pl.ds(koff, tk)], ps[hh]) for hh in range(2))

        def step(j, carry, diag, last):
            st, ps, stats = carry
            st_next = None if last else scores(j + 1)
            pvs = weighted(jnp.maximum(j - 1, 0), ps)
            new_ps, new_stats = [], []
            for hh in range(2):
                m, l, acc = stats[hh]
                s_ = st[hh]
                if diag is not None:
                    s_ = jnp.where(krow + diag * tk <= qcol, s_, -jnp.inf)
                m_new = jnp.maximum(m, jnp.max(s_, axis=0, keepdims=True))
                a = jnp.exp2(m - m_new)
                p = jnp.exp2(s_ - m_new)
                new_stats.append((m_new, a * l + jnp.sum(p, axis=0, keepdims=True), a * (acc + pvs[hh])))
                new_ps.append(p.astype(BF16))
            return st_next, tuple(new_ps), tuple(new_stats)

        one = (jnp.full((1, tq), -jnp.inf, F32), jnp.zeros((1, tq), F32), jnp.zeros((V_DIM, tq), F32))
        zero_p = jnp.zeros((tk, tq), BF16)
        nfull = i * r
        carry = lax.fori_loop(0, nfull, functools.partial(step, diag=None, last=False),
                              (scores(0), (zero_p, zero_p), (one, one)))
        for d in range(r):
            carry = step(nfull + d, carry, d, d == r - 1)
        _, ps, stats = carry
        pvs = weighted(nfull + r - 1, ps)
        ot = jnp.concatenate([(stats[hh][2] + pvs[hh]) / stats[hh][1] for hh in range(2)], axis=0)
        o_ref[...] = ot.T
        lse = [stats[hh][0] + jnp.log(stats[hh][1]) * LOG2E for hh in range(2)]
        lse_ref[...] = jnp.concatenate(lse + [jnp.zeros((6, tq), F32)], axis=0)

    return pl.pallas_call(
        body, name="attn_fwd", grid=(pairs, s // tq),
        in_specs=[pl.BlockSpec((2 * HEAD_PAD, tq), lambda p, i: (p, i)),
                  pl.BlockSpec((s, 2 * HEAD_PAD), lambda p, i: (0, p)),
                  pl.BlockSpec((2 * V_DIM, s), lambda p, i: (p, 0))],
        out_specs=[pl.BlockSpec((tq, 2 * V_DIM), lambda p, i: (i, p)),
                   pl.BlockSpec((None, 8, tq), lambda p, i: (p, 0, i))],
        out_shape=[jax.ShapeDtypeStruct((s, MLA_W), F32), jax.ShapeDtypeStruct((pairs, 8, s), F32)],
        compiler_params=_params(("arbitrary", "arbitrary")),
    )(qt, k, vt)


def _attn_bwd(q, qt, k, kt, v, do, dot, lse, delta):
    s = k.shape[0]
    tq, tk = ATT_BWD_TQ, ATT_BWD_TK
    r = tq // tk
    nq = s // tq
    nk = s // tk
    pairs = HEADS // 2

    def body(q_ref, qt_ref, k_ref, kt_ref, v_ref, do_ref, dot_ref, lse_ref, dl_ref, dqt_ref, dk_ref, dv_ref):
        j = pl.program_id(1)
        krow = lax.broadcasted_iota(jnp.int32, (tk, tq), 0)
        qcol = lax.broadcasted_iota(jnp.int32, (tk, tq), 1)
        lane = lax.broadcasted_iota(jnp.int32, (tk, 2 * V_DIM), 1)
        drow = lax.broadcasted_iota(jnp.int32, (2 * V_DIM, tq), 0)

        @pl.when(j == 0)
        def _():
            dqt_ref[...] = jnp.zeros_like(dqt_ref)

        koff = pl.multiple_of(j * tk, tk)
        vb = v_ref[pl.ds(koff, tk), :]
        kbs = [k_ref[pl.ds(koff, tk), hh * HEAD_PAD:(hh + 1) * HEAD_PAD] for hh in range(2)]
        ktbs = [kt_ref[hh * HEAD_PAD:(hh + 1) * HEAD_PAD, pl.ds(koff, tk)] for hh in range(2)]
        i0 = j // r

        def front(i):
            qoff = pl.multiple_of(i * tq, tq)
            dotb = dot_ref[:, pl.ds(qoff, tq)]
            out = []
            for hh in range(2):
                mine = (drow < V_DIM) if hh == 0 else (drow >= V_DIM)
                st = _dot(kbs[hh], qt_ref[hh * HEAD_PAD:(hh + 1) * HEAD_PAD, pl.ds(qoff, tq)])
                out.append((st, _dot(vb, jnp.where(mine, dotb, jnp.zeros_like(dotb)))))
            return tuple(out)

        def middle(i, tiles, diag):
            qoff = pl.multiple_of(i * tq, tq)
            out = []
            for hh in range(2):
                st, dpt = tiles[hh]
                if diag:
                    st = jnp.where(krow + (j - i0 * r) * tk <= qcol, st, -jnp.inf)
                p = jnp.exp2(st - lse_ref[hh:hh + 1, pl.ds(qoff, tq)])
                out.append((p.astype(BF16), (p * (dpt - dl_ref[hh:hh + 1, pl.ds(qoff, tq)])).astype(BF16)))
            return tuple(out)

        def back(i, pd, accs):
            qoff = pl.multiple_of(i * tq, tq)
            dob = do_ref[pl.ds(qoff, tq), :]
            out = []
            for hh in range(2):
                rows = slice(hh * HEAD_PAD, (hh + 1) * HEAD_PAD)
                p, dst = pd[hh]
                dk_acc, dv_acc = accs[hh]
                dv_acc = dv_acc + _dot(p, dob)
                dk_acc = dk_acc + _dot(dst, q_ref[pl.ds(qoff, tq), rows])
                dqt_ref[rows, pl.ds(qoff, tq)] += _dot(ktbs[hh], dst)
                out.append((dk_acc, dv_acc))
            return tuple(out)

        def step(i, accs, diag):
            return back(i, middle(i, front(i), diag), accs)

        zero_acc = (jnp.zeros((tk, HEAD_PAD), F32), jnp.zeros((tk, 2 * V_DIM), F32))
        accs = step(i0, (zero_acc, zero_acc), True)
        accs = lax.fori_loop(i0 + 1, nq, functools.partial(step, diag=False), accs)
        for hh in range(2):
            dk_ref[:, hh * HEAD_PAD:(hh + 1) * HEAD_PAD] = accs[hh][0] * LN2
        dv_ref[...] = jnp.where(lane < V_DIM, accs[0][1], accs[1][1])

        @pl.when(j == nk - 1)
        def _():
            dqt_ref[...] = dqt_ref[...] * SCALE

    pair_rows = lambda w: pl.BlockSpec((s, w), lambda p, j: (0, p))
    pair_cols = lambda w: pl.BlockSpec((w, s), lambda p, j: (p, 0))
    stats = pl.BlockSpec((None, 8, s), lambda p, j: (p, 0, 0))
    return pl.pallas_call(
        body, name="attn_bwd", grid=(pairs, nk),
        in_specs=[pair_rows(2 * HEAD_PAD), pair_cols(2 * HEAD_PAD), pair_rows(2 * HEAD_PAD), pair_cols(2 * HEAD_PAD),
                  pair_rows(2 * V_DIM), pair_rows(2 * V_DIM), pair_cols(2 * V_DIM), stats, stats],
        out_specs=[pair_cols(2 * HEAD_PAD),
                   pl.BlockSpec((tk, 2 * HEAD_PAD), lambda p, j: (j, p)),
                   pl.BlockSpec((tk, 2 * V_DIM), lambda p, j: (j, p))],
        out_shape=[jax.ShapeDtypeStruct((HEADS * HEAD_PAD, s), F32), jax.ShapeDtypeStruct((s, HEADS * HEAD_PAD), F32),
                   jax.ShapeDtypeStruct((s, MLA_W), F32)],
        compiler_params=_params(("arbitrary", "arbitrary")),
    )(q, qt, k, kt, v, do, dot, lse, delta)


def _split3(a):
    hi = a.astype(BF16)
    r1 = a - hi.astype(F32)
    mid = r1.astype(BF16)
    lo = (r1 - mid.astype(F32)).astype(BF16)
    return hi, mid, lo


def _mid(x, tgt, o, hm, hg, woa, wob, wout, ln_g, ln_b, sg_g, sg_b, w_s, bsb):
    s = x.shape[0]
    ts = ROW_TILE
    nsteps = s // ts
    nch = ts // CHUNK
    npair = GROUPS // 2

    def body(x_ref, t_ref, o_ref, hm_ref, hg_ref, woa_ref, wob_ref, wout_ref, lng_ref, lnb_ref, sgg_ref, sgb_ref,
             ws_ref, bsb_ref,
             dr_ref, dhg_ref, dhm_ref, do_ref, dot_ref, dl_ref, dhgt_ref, dhmt_ref,
             dwout_ref, dwoa_ref, dwob_ref, dws_ref, dbs_ref, dlng_ref, dlnb_ref, dsgg_ref, dsgb_ref, loss_ref,
             dbg_ref, dbm_ref, dbacc_ref):
        i = pl.program_id(0)

        @pl.when(i == 0)
        def _():
            for r in (dwout_ref, dwoa_ref, dwob_ref, dws_ref, dlng_ref, dlnb_ref, dsgg_ref, dsgb_ref, loss_ref,
                      dbg_ref, dbm_ref, dbacc_ref):
                r[...] = jnp.zeros_like(r)

        def emit(ref, tref, bref, lo, val):
            vb = val.astype(BF16)
            n = val.shape[1]
            ref[:, lo:lo + n] = vb
            tref[lo:lo + n, :] = vb.T
            bref[:, lo:lo + n] += jnp.sum(val, axis=0, keepdims=True)

        lane = lax.broadcasted_iota(jnp.int32, (CHUNK, CHUNK), 1)
        left = lane < V_DIM
        tril = lax.broadcasted_iota(jnp.int32, (CHUNK, CHUNK), 0) >= lane
        ms = [jnp.where(tril, ws_ref[g], 0.0).astype(BF16) for g in range(GROUPS)]

        z_a = hm_ref[:, 0:SGU_W]
        u = hm_ref[:, SGU_W:2 * SGU_W]
        v = hm_ref[:, 2 * SGU_W:3 * SGU_W]
        z_b = hm_ref[:, 3 * SGU_W:4 * SGU_W]
        o = o_ref[...]
        sa, dsa = _silu_and_grad(z_a)
        y_a = (o * sa).astype(BF16)
        gu, dgu = _gelu_and_grad(u)
        gv, dgv = _gelu_and_grad(v)
        mu = jnp.mean(gv, axis=-1, keepdims=True)
        vc = gv - mu
        rstd_v = lax.rsqrt(jnp.mean(vc * vc, axis=-1, keepdims=True) + LN_EPS)
        vhat = vc * rstd_v
        vn = (vhat * sgg_ref[...] + sgb_ref[...]).astype(BF16)
        rows = []
        for c in range(nch):
            blocks = []
            for p in range(npair):
                blk = vn[c * CHUNK:(c + 1) * CHUNK, p * CHUNK:(p + 1) * CHUNK]
                blocks.append(jnp.where(left, _dot(ms[2 * p], blk), _dot(ms[2 * p + 1], blk)))
            rows.append(jnp.concatenate(blocks, axis=1) + bsb_ref[...])
        mixed = jnp.concatenate(rows, axis=0)
        sgu = gu * mixed
        sb, dsb = _silu_and_grad(z_b)
        y_b = (sgu * sb).astype(BF16)
        pa = jnp.concatenate([_dot(y_a, woa_ref[k]) for k in range(N_SLABS)], axis=1)
        pb = jnp.concatenate([_dot(y_b, wob_ref[k]) for k in range(N_SLABS)], axis=1)
        sga = _sigmoid(hg_ref[:, :D_MODEL])
        sgb = _sigmoid(hg_ref[:, D_MODEL:])
        m2 = (sga * pa + sgb * pb).astype(BF16)
        r = ALPHA * x_ref[...] + _dot(m2, wout_ref[...])
        rmu = jnp.mean(r, axis=-1, keepdims=True)
        rc = r - rmu
        rstd = lax.rsqrt(jnp.mean(rc * rc, axis=-1, keepdims=True) + LN_EPS)
        xhat = rc * rstd
        y = xhat * lng_ref[...] + lnb_ref[...]
        err = y - t_ref[...]
        loss_ref[...] += jnp.full(loss_ref.shape, 0.5 / D_MODEL, F32) * jnp.sum(err * err)

        dy = err * (1.0 / D_MODEL)
        dlng_ref[...] += jnp.sum(dy * xhat, axis=0, keepdims=True)
        dlnb_ref[...] += jnp.sum(dy, axis=0, keepdims=True)
        dxh = dy * lng_ref[...]
        dr = rstd * (dxh - jnp.mean(dxh, axis=-1, keepdims=True) - xhat * jnp.mean(dxh * xhat, axis=-1, keepdims=True))
        dr_ref[...] = dr
        drb = dr.astype(BF16)
        dwout_ref[...] += _dot_tn(m2, drb)
        dm2 = _dot_nt(drb, wout_ref[...])
        emit(dhg_ref, dhgt_ref, dbg_ref, 0, dm2 * pa * sga * (1.0 - sga))
        emit(dhg_ref, dhgt_ref, dbg_ref, D_MODEL, dm2 * pb * sgb * (1.0 - sgb))
        dpa = (dm2 * sga).astype(BF16)
        dpb = (dm2 * sgb).astype(BF16)
        dy_a = jnp.zeros((ts, MLA_W), F32)
        dy_b = jnp.zeros((ts, SGU_W), F32)
        for k in range(N_SLABS):
            cols = slice(k * SLAB_W, (k + 1) * SLAB_W)
            dwoa_ref[k] += _dot_tn(y_a, dpa[:, cols])
            dwob_ref[k] += _dot_tn(y_b, dpb[:, cols])
            dy_a = dy_a + _dot_nt(dpa[:, cols], woa_ref[k])
            dy_b = dy_b + _dot_nt(dpb[:, cols], wob_ref[k])
        dob = (dy_a * sa).astype(BF16)
        do_ref[...] = dob
        dot_ref[...] = dob.T
        head = (lax.broadcasted_iota(jnp.int32, (HEADS, MLA_W), 1) // V_DIM
                == lax.broadcasted_iota(jnp.int32, (HEADS, MLA_W), 0)).astype(BF16)
        dl_ref[...] = sum(_dot_nt(head, term) for term in _split3(dob.astype(F32) * o))
        emit(dhm_ref, dhmt_ref, dbm_ref, 0, dy_a * o * dsa)
        dsg = dy_b * sb
        emit(dhm_ref, dhmt_ref, dbm_ref, 3 * SGU_W, dy_b * sgu * dsb)
        emit(dhm_ref, dhmt_ref, dbm_ref, SGU_W, dsg * mixed * dgu)
        dmixed = dsg * gu
        dvn_rows = []
        dbs_sum = jnp.zeros((CHUNK, SGU_W), F32)
        for c in range(nch):
            dm_c = dmixed[c * CHUNK:(c + 1) * CHUNK, :]
            dbs_sum = dbs_sum + dm_c
            blocks = []
            for p in range(npair):
                dmb = dm_c[:, p * CHUNK:(p + 1) * CHUNK].astype(BF16)
                blk = vn[c * CHUNK:(c + 1) * CHUNK, p * CHUNK:(p + 1) * CHUNK]
                blocks.append(jnp.where(left, _dot_tn(ms[2 * p], dmb), _dot_tn(ms[2 * p + 1], dmb)))
                zero = jnp.zeros_like(dmb)
                dws_ref[2 * p] += jnp.where(tril, _dot_nt(jnp.where(left, dmb, zero), blk), 0.0)
                dws_ref[2 * p + 1] += jnp.where(tril, _dot_nt(jnp.where(left, zero, dmb), blk), 0.0)
            dvn_rows.append(jnp.concatenate(blocks, axis=1))
        dbacc_ref[...] += dbs_sum
        dvn = jnp.concatenate(dvn_rows, axis=0)
        dsgg_ref[...] += jnp.sum(dvn * vhat, axis=0, keepdims=True)
        dsgb_ref[...] += jnp.sum(dvn, axis=0, keepdims=True)
        dvh = dvn * sgg_ref[...]
        dgv_in = rstd_v * (dvh - jnp.mean(dvh, axis=-1, keepdims=True)
                           - vhat * jnp.mean(dvh * vhat, axis=-1, keepdims=True))
        emit(dhm_ref, dhmt_ref, dbm_ref, 2 * SGU_W, dgv_in * dgv)

        @pl.when(i == nsteps - 1)
        def _():
            grp = (lax.broadcasted_iota(jnp.int32, (SGU_W, CHUNK), 0) // V_DIM
                   == lax.broadcasted_iota(jnp.int32, (SGU_W, CHUNK), 1)).astype(BF16)
            hi, mid, lo = _split3(dbacc_ref[...])
            dbs_ref[...] = _dot(hi, grp) + _dot(mid, grp) + _dot(lo, grp)

    acc_shapes = [(D_MODEL, D_MODEL), woa.shape, wob.shape, (GROUPS, CHUNK, CHUNK), (CHUNK, CHUNK),
                  (1, D_MODEL), (1, D_MODEL), (1, SGU_W), (1, SGU_W), (1, 128), (1, GATE_W), (1, MID_W)]
    col_spec = lambda rows: pl.BlockSpec((rows, ts), lambda i: (0, i))
    return pl.pallas_call(
        body, name="mid", grid=(nsteps,),
        in_specs=[_row_spec(ts, D_MODEL), _row_spec(ts, D_MODEL), _row_spec(ts, MLA_W), _row_spec(ts, MID_W),
                  _row_spec(ts, GATE_W), _full_spec(woa.shape), _full_spec(wob.shape), _full_spec(wout.shape),
                  _full_spec(ln_g.shape), _full_spec(ln_b.shape), _full_spec(sg_g.shape), _full_spec(sg_b.shape),
                  _full_spec(w_s.shape), _full_spec(bsb.shape)],
        out_specs=[_row_spec(ts, D_MODEL), _row_spec(ts, GATE_W), _row_spec(ts, MID_W), _row_spec(ts, MLA_W),
                   col_spec(MLA_W), col_spec(HEADS), col_spec(GATE_W), col_spec(MID_W)]
        + [_full_spec(sh) for sh in acc_shapes],
        out_shape=[jax.ShapeDtypeStruct((s, D_MODEL), F32), jax.ShapeDtypeStruct((s, GATE_W), BF16),
                   jax.ShapeDtypeStruct((s, MID_W), BF16), jax.ShapeDtypeStruct((s, MLA_W), BF16),
                   jax.ShapeDtypeStruct((MLA_W, s), BF16), jax.ShapeDtypeStruct((HEADS, s), F32),
                   jax.ShapeDtypeStruct((GATE_W, s), BF16), jax.ShapeDtypeStruct((MID_W, s), BF16)]
        + [jax.ShapeDtypeStruct(sh, F32) for sh in acc_shapes],
        scratch_shapes=[pltpu.VMEM((CHUNK, SGU_W), F32)],
        compiler_params=_params(),
    )(x, tgt, o, hm, hg, woa, wob, wout, ln_g, ln_b, sg_g, sg_b, w_s, bsb)


def _lat_bwd(dq, dk, dv, hl, rc, rsl, rsh, g_q, g_kv, wuq, wk, wv):
    s = dk.shape[0]
    ts = ROW_TILE
    qk_w = HEADS * HEAD_PAD

    def body(dq_ref, dk_ref, dv_ref, hl_ref, rc_ref, rsl_ref, rsh_ref, gq_ref, gkv_ref, wuq_ref, wk_ref, wv_ref,
             dhl_ref, dhlt_ref, dwuq_ref, dwk_ref, dwv_ref, dgq_ref, dgkv_ref, dbl_ref):
        i = pl.program_id(0)

        @pl.when(i == 0)
        def _():
            for r in (dwuq_ref, dwk_ref, dwv_ref, dgq_ref, dgkv_ref, dbl_ref):
                r[...] = jnp.zeros_like(r)

        def emit(lo, val):
            vb = val.astype(BF16)
            n = val.shape[1]
            dhl_ref[:, lo:lo + n] = vb
            dhlt_ref[lo:lo + n, :] = vb.T
            dbl_ref[:, lo:lo + n] += jnp.sum(val, axis=0, keepdims=True)

        c, sl, sh = rc_ref[...], rsl_ref[...], rsh_ref[...]
        lane = lax.broadcasted_iota(jnp.int32, (ts, HEAD_PAD), 1)
        pe = (lane >= NOPE) & (lane < QK_DIM)
        dkpe = jnp.zeros((ts, HEAD_PAD), F32)
        dqu = []
        for hd in range(HEADS):
            lanes = slice(hd * HEAD_PAD, (hd + 1) * HEAD_PAD)
            dqu.append(_rope_t(dq_ref[lanes, :].T, c, sl, sh).astype(BF16))
            dkpe = dkpe + dk_ref[:, lanes]
        dqu = jnp.concatenate(dqu, axis=1)
        dkpe = _rope_t(jnp.where(pe, dkpe, 0.0), c, sl, sh)

        cq = hl_ref[:, :Q_RANK]
        rq = lax.rsqrt(jnp.mean(cq * cq, axis=-1, keepdims=True) + RMS_EPS)
        cqh = cq * rq
        cqn = (cqh * gq_ref[...]).astype(BF16)
        dwuq_ref[...] += _dot_tn(cqn, dqu)
        dcqn = _dot_nt(dqu, wuq_ref[...])
        dgq_ref[...] += jnp.sum(dcqn * cqh, axis=0, keepdims=True)
        dch = dcqn * gq_ref[...]
        emit(0, rq * (dch - cqh * jnp.mean(dch * cqh, axis=-1, keepdims=True)))

        ckv = hl_ref[:, Q_RANK:Q_RANK + KV_RANK]
        rk = lax.rsqrt(jnp.mean(ckv * ckv, axis=-1, keepdims=True) + RMS_EPS)
        ckh = ckv * rk
        ckn = (ckh * gkv_ref[...]).astype(BF16)
        dkb = dk_ref[...].astype(BF16)
        dvb = dv_ref[...].astype(BF16)
        dwk_ref[...] += _dot_tn(ckn, dkb)
        dwv_ref[...] += _dot_tn(ckn, dvb)
        dckn = _dot_nt(dkb, wk_ref[...]) + _dot_nt(dvb, wv_ref[...])
        dgkv_ref[...] += jnp.sum(dckn * ckh, axis=0, keepdims=True)
        dkh = dckn * gkv_ref[...]
        emit(Q_RANK, rk * (dkh - ckh * jnp.mean(dkh * ckh, axis=-1, keepdims=True)))
        emit(Q_RANK + KV_RANK, dkpe)

    acc_shapes = [wuq.shape, wk.shape, wv.shape, g_q.shape, g_kv.shape, (1, LAT_W)]
    return pl.pallas_call(
        body, name="lat_bwd", grid=(s // ts,),
        in_specs=[pl.BlockSpec((qk_w, ts), lambda i: (0, i)), _row_spec(ts, qk_w), _row_spec(ts, MLA_W),
                  _row_spec(ts, LAT_W), _row_spec(ts, HEAD_PAD), _row_spec(ts, HEAD_PAD), _row_spec(ts, HEAD_PAD),
                  _full_spec(g_q.shape), _full_spec(g_kv.shape), _full_spec(wuq.shape), _full_spec(wk.shape),
                  _full_spec(wv.shape)],
        out_specs=[_row_spec(ts, LAT_W), pl.BlockSpec((LAT_W, ts), lambda i: (0, i))]
        + [_full_spec(sh) for sh in acc_shapes],
        out_shape=[jax.ShapeDtypeStruct((s, LAT_W), BF16), jax.ShapeDtypeStruct((LAT_W, s), BF16)]
        + [jax.ShapeDtypeStruct(sh, F32) for sh in acc_shapes],
        compiler_params=_params(),
    )(dq, dk, dv, hl, rc, rsl, rsh, g_q, g_kv, wuq, wk, wv)


def _dx(dr, dhg, dhm, dhl, wt):
    s = dr.shape[0]
    ts = ROW_TILE

    def body(dr_ref, dhg_ref, dhm_ref, dhl_ref, wt_ref, dx_ref):
        dx_ref[...] = (ALPHA * dr_ref[...]
                       + _dot(dhg_ref[...], wt_ref[ROW_GATE:IN_W, :])
                       + _dot(dhm_ref[...], wt_ref[LAT_COLS:ROW_GATE, :])
                       + _dot(dhl_ref[:, 0:Q_RANK + KV_RANK], wt_ref[0:Q_RANK + KV_RANK, :])
                       + _dot(dhl_ref[:, Q_RANK + KV_RANK:], _kpe_rows(wt_ref)))

    return pl.pallas_call(
        body, name="dx", grid=(s // ts,),
        in_specs=[_row_spec(ts, D_MODEL), _row_spec(ts, GATE_W), _row_spec(ts, MID_W), _row_spec(ts, LAT_W),
                  _full_spec(wt.shape)],
        out_specs=_row_spec(ts, D_MODEL),
        out_shape=jax.ShapeDtypeStruct((s, D_MODEL), F32),
        compiler_params=_params(),
    )(dr, dhg, dhm, dhl, wt)


def _dwt(dht, xb, tn, name):
    n, s = dht.shape

    def body(dht_ref, xb_ref, dw_ref):
        dw_ref[...] = _dot(dht_ref[...], xb_ref[...]).astype(BF16)

    return pl.pallas_call(
        body, name=name, grid=(n // tn,),
        in_specs=[pl.BlockSpec((tn, s), lambda i: (i, 0)), _full_spec(xb.shape)],
        out_specs=pl.BlockSpec((tn, D_MODEL), lambda i: (i, 0)),
        out_shape=jax.ShapeDtypeStruct((n, D_MODEL), BF16),
        compiler_params=_params(),
    )(dht, xb)


def _dwt_lat(dhlt, xb):
    n, s = dhlt.shape

    def body(dht_ref, xb_ref, dw_ref):
        dw = _dot(dht_ref[...], xb_ref[...]).astype(BF16)
        kpe = Q_RANK + KV_RANK + NOPE
        dw_ref[0:Q_RANK + KV_RANK, :] = dw[0:Q_RANK + KV_RANK]
        dw_ref[Q_RANK + KV_RANK:LAT_COLS, :] = dw[kpe:kpe + ROPE]
        dw_ref[LAT_COLS:, :] = jnp.zeros((LAT_ROWS_PAD - LAT_COLS, D_MODEL), BF16)

    return pl.pallas_call(
        body, name="dwt_lat", in_specs=[VMEM_SPEC, VMEM_SPEC], out_specs=VMEM_SPEC,
        out_shape=jax.ShapeDtypeStruct((LAT_ROWS_PAD, D_MODEL), BF16),
        compiler_params=pltpu.CompilerParams(vmem_limit_bytes=VMEM_LIMIT),
    )(dhlt, xb)


def _split_bias(b):
    z = lambda n: jnp.zeros((n,), b.dtype)
    lat = jnp.concatenate([b[:Q_RANK + KV_RANK], z(NOPE), b[Q_RANK + KV_RANK:LAT_COLS], z(HEAD_PAD - QK_DIM)])
    return b[None, ROW_GATE:], b[None, LAT_COLS:ROW_GATE], lat[None, :]


def _join_bias(g, m, l):
    kpe = Q_RANK + KV_RANK + NOPE
    return jnp.concatenate([l[0, :Q_RANK + KV_RANK], l[0, kpe:kpe + ROPE], m[0], g[0]])


def _rope_tables(positions):
    half = ROPE // 2
    inv_freq = ROPE_THETA ** (-jnp.arange(0, ROPE, 2, dtype=F32) / ROPE)
    ang = positions.astype(F32)[:, None] * inv_freq
    cos, sin = jnp.cos(ang), jnp.sin(ang)
    n = positions.shape[0]
    one, zero = jnp.ones((n, NOPE), F32), jnp.zeros((n, half), F32)
    tail1, tail0 = jnp.ones((n, HEAD_PAD - QK_DIM), F32), jnp.zeros((n, HEAD_PAD - QK_DIM), F32)
    z64 = jnp.zeros((n, NOPE), F32)
    rc = jnp.concatenate([one, cos, cos, tail1], axis=1)
    rsl = jnp.concatenate([z64, -sin, zero, tail0], axis=1)
    rsh = jnp.concatenate([z64, zero, sin, tail0], axis=1)
    return rc, rsl, rsh


def _local_head(x, positions, tgt, wt, b_in, g_q, w_uq, g_kv, w_ukv, w_oa, sg_g, sg_b, w_s, b_s, w_ob, w_out,
                ln_g, ln_b):
    rc, rsl, rsh = _rope_tables(positions)
    b_g, b_m, b_l = _split_bias(b_in)
    wuq = jnp.pad(w_uq, ((0, 0), (0, 0), (0, HEAD_PAD - QK_DIM))).reshape(Q_RANK, HEADS * HEAD_PAD).astype(BF16)
    wk = jnp.pad(w_ukv[:, :, :NOPE], ((0, 0), (0, 0), (0, HEAD_PAD - NOPE))).reshape(KV_RANK, HEADS * HEAD_PAD).astype(BF16)
    wv = w_ukv[:, :, NOPE:].reshape(KV_RANK, MLA_W).astype(BF16)
    bsb = jnp.repeat(b_s.T, V_DIM, axis=1)
    gq2, gkv2 = g_q[None, :], g_kv[None, :]

    hg, hm, hl, q, k, v, xb, qt, kt, vt = _fwd_pre(x, wt, b_g, b_m, b_l, gq2, wuq, gkv2, wk, wv, rc, rsl, rsh)
    o, lse = _attn_fwd(qt, k, vt)
    (dr, dhg, dhm, do, dot, delta, dhgt, dhmt, dwout, dwoa, dwob, dws, dbs, dlng, dlnb, dsgg, dsgb, loss, dbg,
     dbm) = _mid(x, tgt, o, hm, hg, w_oa, w_ob, w_out, ln_g[None, :], ln_b[None, :], sg_g[None, :], sg_b[None, :],
                 w_s, bsb)
    delta = jnp.pad(delta.reshape(HEADS // 2, 2, -1), ((0, 0), (0, 6), (0, 0)))
    early = {
        "w_in": jnp.concatenate([jnp.zeros((LAT_COLS, D_MODEL), BF16), _dwt(dhmt, xb, 512, "dwt_mid"),
                                 _dwt(dhgt, xb, 512, "dwt_gates")], axis=0),
        "w_oa": dwoa, "sgu_ln_g": dsgg[0], "sgu_ln_b": dsgb[0], "w_s": dws, "b_s": dbs[:, :GROUPS].T,
        "w_ob": dwob, "w_out": dwout, "ln_g": dlng[0], "ln_b": dlnb[0],
    }
    state = dict(q=q, qt=qt, k=k, kt=kt, v=v, do=do, dot=dot, lse=lse, delta=delta, hl=hl, rc=rc, rsl=rsl, rsh=rsh,
                 gq2=gq2, gkv2=gkv2, wuq=wuq, wk=wk, wv=wv, dr=dr, dhg=dhg, dhm=dhm, wt=wt, xb=xb, dbg=dbg, dbm=dbm)
    return loss, early, state


def _local_tail(st):
    dq, dk, dv = _attn_bwd(st["q"], st["qt"], st["k"], st["kt"], st["v"], st["do"], st["dot"], st["lse"], st["delta"])
    dhl, dhlt, dwuq, dwk, dwv, dgq, dgkv, dbl = _lat_bwd(dq, dk, dv, st["hl"], st["rc"], st["rsl"], st["rsh"],
                                                         st["gq2"], st["gkv2"], st["wuq"], st["wk"], st["wv"])
    dx = _dx(st["dr"], st["dhg"], st["dhm"], dhl, st["wt"])
    late = {
        "w_lat": _dwt_lat(dhlt, st["xb"]),
        "b_in": _join_bias(st["dbg"], st["dbm"], dbl),
        "g_q": dgq[0],
        "w_uq": dwuq.reshape(Q_RANK, HEADS, HEAD_PAD)[:, :, :QK_DIM],
        "g_kv": dgkv[0],
        "w_ukv": jnp.concatenate([dwk.reshape(KV_RANK, HEADS, HEAD_PAD)[:, :, :NOPE],
                                  dwv.reshape(KV_RANK, HEADS, V_DIM)], axis=2),
    }
    return dx, late


def _local_step(*args):
    loss, early, st = _local_head(*args)
    dx, late = _local_tail(st)
    grads = {**early, **late}
    grads["w_in"] = jnp.concatenate([grads.pop("w_lat")[:LAT_COLS], early["w_in"][LAT_COLS:]], axis=0)
    return loss, dx, grads


MESH = pl.DeviceIdType.MESH
N_CHIPS = 4
HBM_SPEC = pl.BlockSpec(memory_space=pl.ANY)
HBM_SPEC_STRICT = pl.BlockSpec(memory_space=pltpu.HBM)
VMEM_SPEC = pl.BlockSpec(memory_space=pltpu.VMEM)

REP_ROWS = 80


def _rows8(a):
    flat = a.reshape(-1)
    n = -(-flat.shape[0] // (8 * D_MODEL)) * 8 * D_MODEL
    return jnp.pad(flat, (0, n - flat.shape[0])).reshape(-1, D_MODEL)


def _place():
    x, y, c = lax.axis_index("x"), lax.axis_index("y"), lax.axis_index("c")
    others = [(1 - x, y), (x, 1 - y), (1 - x, 1 - y)]
    return x, y, c, others


def _gather_weights(shards):
    n = len(shards)

    def body(*refs):
        ins, outs, bufs = refs[:n], refs[n:2 * n], refs[2 * n:3 * n]
        send_sems, recv_sems, local_sems = refs[3 * n:]
        x, y, c, others = _place()
        me = 2 * x + y
        sibling = (x, y, 1 - c)
        for src, buf in zip(ins, bufs):
            buf[...] = src[...].astype(BF16)
        own = [pltpu.make_async_copy(bufs[w], outs[w].at[me], local_sems.at[w]) for w in range(n)]
        for cp in own:
            cp.start()

        def part(w, chip, half):
            hc = shards[w].shape[1] // 2
            return outs[w].at[chip, :, pl.ds(half * hc, hc)]

        def sent(w, j):
            hc = shards[w].shape[1] // 2
            return pltpu.make_async_remote_copy(
                src_ref=bufs[w].at[:, pl.ds(c * hc, hc)], dst_ref=part(w, me, c),
                send_sem=send_sems.at[w * 3 + j], recv_sem=recv_sems.at[w * 3 + j],
                device_id=(*others[j], c), device_id_type=MESH)

        def landed(w, j):
            px, py = others[j]
            return pltpu.make_async_remote_copy(
                src_ref=part(w, 2 * px + py, c), dst_ref=part(w, 2 * px + py, c),
                send_sem=send_sems.at[w * 3 + j], recv_sem=recv_sems.at[w * 3 + j],
                device_id=(px, py, c), device_id_type=MESH)

        def passed(w, j, half):
            px, py = others[j]
            k = n * 3 + w * 3 + j
            return pltpu.make_async_remote_copy(
                src_ref=part(w, 2 * px + py, half), dst_ref=part(w, 2 * px + py, half),
                send_sem=send_sems.at[k], recv_sem=recv_sems.at[k], device_id=sibling, device_id_type=MESH)

        first = [sent(w, j) for w in range(n) for j in range(3)]
        for cp in first:
            cp.start()
        fwd = []
        for w in range(n):
            for j in range(3):
                landed(w, j).wait_recv()
                cp = passed(w, j, c)
                cp.start()
                fwd.append(cp)
        for w in range(n):
            for j in range(3):
                passed(w, j, 1 - c).wait_recv()
        for cp in first + fwd:
            cp.wait_send()
        for cp in own:
            cp.wait()

    return pl.pallas_call(
        body, name="gather_weights",
        in_specs=[VMEM_SPEC] * n, out_specs=[HBM_SPEC] * n,
        out_shape=[jax.ShapeDtypeStruct((N_CHIPS,) + s.shape, BF16) for s in shards],
        scratch_shapes=[pltpu.VMEM(s.shape, BF16) for s in shards]
        + [pltpu.SemaphoreType.DMA((6 * n,)), pltpu.SemaphoreType.DMA((6 * n,)), pltpu.SemaphoreType.DMA((n,))],
        compiler_params=pltpu.CompilerParams(vmem_limit_bytes=VMEM_LIMIT),
    )(*shards)


N_DEV = 8
LOSS_TILE = (8, 128)


def _half(a):
    return a.shape[-1] // 2


def _exchange_pairs(parts, name):
    n = len(parts)

    def body(*refs):
        p_refs, r_refs = refs[:n], refs[n:2 * n]
        send_sems, recv_sems = refs[2 * n:]
        x, y, c, _ = _place()
        cps = []
        for w in range(n):
            h = _half(parts[w])
            cps.append(pltpu.make_async_remote_copy(
                src_ref=p_refs[w].at[:, :, pl.ds((1 - c) * h, h)], dst_ref=r_refs[w],
                send_sem=send_sems.at[w], recv_sem=recv_sems.at[w], device_id=(x, y, 1 - c), device_id_type=MESH))
        for cp in cps:
            cp.start()
        for cp in cps:
            cp.wait()

    return pl.pallas_call(
        body, name=name, in_specs=[HBM_SPEC] * n, out_specs=[HBM_SPEC] * n,
        out_shape=[jax.ShapeDtypeStruct((N_CHIPS, p.shape[1], _half(p)), BF16) for p in parts],
        scratch_shapes=[pltpu.SemaphoreType.DMA((n,)), pltpu.SemaphoreType.DMA((n,))],
    )(*parts)


def _exchange_pairs_and_loss(parts, loss):
    n = len(parts)

    def body(*refs):
        p_refs, loss_ref = refs[:n], refs[n]
        r_refs, all_loss_ref = refs[n + 1:2 * n + 1], refs[2 * n + 1]
        send_sems, recv_sems, loss_send, loss_recv, local_sem = refs[2 * n + 2:]
        x, y, c, _ = _place()
        sibling = (x, y, 1 - c)
        cps = []
        for w in range(n):
            h = _half(parts[w])
            cps.append(pltpu.make_async_remote_copy(
                src_ref=p_refs[w].at[:, :, pl.ds((1 - c) * h, h)], dst_ref=r_refs[w],
                send_sem=send_sems.at[w], recv_sem=recv_sems.at[w], device_id=sibling, device_id_type=MESH))
        for cp in cps:
            cp.start()
        me = 4 * x + 2 * y + c
        own = pltpu.make_async_copy(loss_ref, all_loss_ref.at[me], local_sem)
        own.start()
        lcs = []
        for t in range(1, N_DEV):
            d = (me + t) % N_DEV
            lcs.append(pltpu.make_async_remote_copy(
                src_ref=loss_ref, dst_ref=all_loss_ref.at[me], send_sem=loss_send.at[t - 1],
                recv_sem=loss_recv.at[t - 1], device_id=(d // 4, (d // 2) % 2, d % 2), device_id_type=MESH))
        for cp in lcs:
            cp.start()
        for t in range(1, N_DEV):
            d = (me + N_DEV - t) % N_DEV
            pltpu.make_async_remote_copy(
                src_ref=loss_ref, dst_ref=all_loss_ref.at[d], send_sem=loss_send.at[t - 1],
                recv_sem=loss_recv.at[t - 1], device_id=(d // 4, (d // 2) % 2, d % 2), device_id_type=MESH).wait_recv()
        for cp in lcs:
            cp.wait_send()
        for cp in cps:
            cp.wait()
        own.wait()

    return pl.pallas_call(
        body, name="exchange_pairs_and_loss", in_specs=[HBM_SPEC] * (n + 1), out_specs=[HBM_SPEC] * (n + 1),
        out_shape=[jax.ShapeDtypeStruct((N_CHIPS, p.shape[1], _half(p)), BF16) for p in parts]
        + [jax.ShapeDtypeStruct((N_DEV,) + LOSS_TILE, F32)],
        scratch_shapes=[pltpu.SemaphoreType.DMA((n,)), pltpu.SemaphoreType.DMA((n,)),
                        pltpu.SemaphoreType.DMA((N_DEV - 1,)), pltpu.SemaphoreType.DMA((N_DEV - 1,)),
                        pltpu.SemaphoreType.DMA],
    )(*parts, loss)


def _add_pair_tiled(p, r, c):
    rows, h = r.shape[1:]

    def body(c_ref, p_ref, r_ref, q_ref):
        q_ref[...] = (p_ref[...].astype(F32) + r_ref[...].astype(F32)).astype(BF16)

    return pl.pallas_call(
        body, name="add_pair_w_in",
        grid_spec=pltpu.PrefetchScalarGridSpec(
            num_scalar_prefetch=1, grid=(N_CHIPS,),
            in_specs=[pl.BlockSpec((None, rows, h), lambda k, c_ref: (k, 0, c_ref[0])),
                      pl.BlockSpec((None, rows, h), lambda k, c_ref: (k, 0, 0))],
            out_specs=pl.BlockSpec((None, rows, h), lambda k, c_ref: (k, 0, 0))),
        out_shape=jax.ShapeDtypeStruct(r.shape, BF16),
    )(c, p, r)


def _add_pair_small(ps, rs, c, name):
    n = len(ps)

    def body(c_ref, *refs):
        for w in range(n):
            h = _half(ps[w])
            mine = refs[w][:, :, pl.ds(pl.multiple_of(c_ref[0] * h, 128), h)]
            refs[2 * n + w][...] = (mine.astype(F32) + refs[n + w][...].astype(F32)).astype(BF16)

    return pl.pallas_call(
        body, name=name,
        in_specs=[pl.BlockSpec(memory_space=pltpu.SMEM)] + [VMEM_SPEC] * (2 * n), out_specs=[VMEM_SPEC] * n,
        out_shape=[jax.ShapeDtypeStruct(r.shape, BF16) for r in rs],
        compiler_params=pltpu.CompilerParams(vmem_limit_bytes=VMEM_LIMIT),
    )(c, *ps, *rs)


def _exchange_chips(qs):
    n = len(qs)

    def body(*refs):
        q_refs, r_refs = refs[:n], refs[n:2 * n]
        send_sems, recv_sems = refs[2 * n:]
        x, y, c, others = _place()
        me = 2 * x + y
        cps = []
        for w in range(n):
            for j, (px, py) in enumerate(others):
                cps.append(pltpu.make_async_remote_copy(
                    src_ref=q_refs[w].at[2 * px + py], dst_ref=r_refs[w].at[me], send_sem=send_sems.at[3 * w + j],
                    recv_sem=recv_sems.at[3 * w + j], device_id=(px, py, c), device_id_type=MESH))
        for cp in cps:
            cp.start()
        for w in range(n):
            for j, (px, py) in enumerate(others):
                pltpu.make_async_remote_copy(
                    src_ref=q_refs[w].at[me], dst_ref=r_refs[w].at[2 * px + py], send_sem=send_sems.at[3 * w + j],
                    recv_sem=recv_sems.at[3 * w + j], device_id=(px, py, c), device_id_type=MESH).wait_recv()
        for cp in cps:
            cp.wait_send()

    return pl.pallas_call(
        body, name="exchange_chips", in_specs=[HBM_SPEC] * n, out_specs=[HBM_SPEC] * n,
        out_shape=[jax.ShapeDtypeStruct(q.shape, BF16) for q in qs],
        scratch_shapes=[pltpu.SemaphoreType.DMA((3 * n,)), pltpu.SemaphoreType.DMA((3 * n,))],
    )(*qs)


SEM_SPEC = pl.BlockSpec(memory_space=pltpu.SEMAPHORE)
SPLIT_EFFECT = pltpu.SideEffectType.DATAFLOW_SIDE_EFFECTING


def _chips_start(qs):
    n = len(qs)

    def body(*refs):
        q_refs, land_refs = refs[:n], refs[n:2 * n]
        send_sems, recv_sems, token = refs[2 * n], refs[2 * n + 1], refs[-1]
        x, y, c, others = _place()
        me = 2 * x + y
        for w in range(n):
            for j, (px, py) in enumerate(others):
                pltpu.make_async_remote_copy(
                    src_ref=q_refs[w].at[2 * px + py], dst_ref=land_refs[w].at[me], send_sem=send_sems.at[3 * w + j],
                    recv_sem=recv_sems.at[3 * w + j], device_id=(px, py, c), device_id_type=MESH).start()
        token[...] = jnp.zeros_like(token)

    hbm = [pltpu.HBM(q.shape, BF16) for q in qs]
    outs = pl.pallas_call(
        body, name="chips_start",
        out_shape=(pltpu.SemaphoreType.DMA((3 * n,)), pltpu.SemaphoreType.DMA((3 * n,)), *hbm, *hbm,
                   jax.ShapeDtypeStruct(LOSS_TILE, F32)),
        in_specs=[HBM_SPEC_STRICT] * (2 * n),
        out_specs=(SEM_SPEC, SEM_SPEC, *[HBM_SPEC_STRICT] * (2 * n), VMEM_SPEC),
        input_output_aliases={i: 2 + i for i in range(2 * n)},
        compiler_params=pltpu.CompilerParams(has_side_effects=SPLIT_EFFECT),
    )(*[pltpu.with_memory_space_constraint(q, pltpu.HBM) for q in qs],
      *[pltpu.with_memory_space_constraint(lax.empty(q.shape, BF16), pltpu.HBM) for q in qs])
    return outs[0], outs[1], outs[2:2 + n], outs[2 + n:2 + 2 * n], outs[-1]


def _chips_wait(send_sems, recv_sems, q_thru, land_thru, after):
    n = len(q_thru)

    def body(*refs):
        q_refs, land_refs = refs[:n], refs[n:2 * n]
        send_sems, recv_sems = refs[2 * n], refs[2 * n + 1]
        x, y, c, others = _place()
        me = 2 * x + y
        for w in range(n):
            for j, (px, py) in enumerate(others):
                cp = pltpu.make_async_remote_copy(
                    src_ref=q_refs[w].at[2 * px + py], dst_ref=land_refs[w].at[2 * px + py],
                    send_sem=send_sems.at[3 * w + j], recv_sem=recv_sems.at[3 * w + j], device_id=(px, py, c),
                    device_id_type=MESH)
                cp.wait_send()
                cp.wait_recv()

    outs = pl.pallas_call(
        body, name="chips_wait", out_shape=tuple(pltpu.HBM(a.shape, a.dtype) for a in (*q_thru, *land_thru)),
        in_specs=[HBM_SPEC_STRICT] * (2 * n) + [SEM_SPEC, SEM_SPEC, HBM_SPEC],
        out_specs=tuple([HBM_SPEC_STRICT] * (2 * n)), input_output_aliases={i: i for i in range(2 * n)},
        compiler_params=pltpu.CompilerParams(has_side_effects=SPLIT_EFFECT),
    )(*q_thru, *land_thru, send_sems, recv_sems, after)
    return list(outs[:n]), list(outs[n:])


def _sum_chips_tiled(q, r, idx, tile):
    rows, h = r.shape[1:]
    nt = h // tile

    def body(idx_ref, q_ref, r0_ref, r1_ref, r2_ref, g_ref):
        g_ref[...] = (q_ref[...].astype(F32) + r0_ref[...].astype(F32) + r1_ref[...].astype(F32)
                      + r2_ref[...].astype(F32))

    def slab(t):
        return pl.BlockSpec((None, rows, tile), lambda i, idx_ref: (idx_ref[t], 0, i))

    return pl.pallas_call(
        body, name="sum_chips_w_in",
        grid_spec=pltpu.PrefetchScalarGridSpec(
            num_scalar_prefetch=1, grid=(nt,), in_specs=[slab(0), slab(1), slab(2), slab(3)],
            out_specs=pl.BlockSpec((rows, tile), lambda i, idx_ref: (0, idx_ref[4] * nt + i))),
        out_shape=jax.ShapeDtypeStruct((rows, 2 * h), F32),
    )(idx, q, r, r, r)


def _sum_chips_small(qs, rs, idx, n_all):
    n = len(rs)

    def body(idx_ref, *refs):
        c = idx_ref[4]
        for w in range(n):
            q_ref, r_ref, g_ref = refs[w], refs[n + w], refs[2 * n + w]
            acc = q_ref[idx_ref[0]].astype(F32)
            for t in range(1, N_CHIPS):
                acc = acc + r_ref[idx_ref[t]].astype(F32)
            h = rs[w].shape[2]
            mine = pl.ds(pl.multiple_of(c * h, 128), h)
            g_ref[...] = jnp.zeros_like(g_ref)
            if w >= n - n_all:
                g_ref[idx_ref[0], :, mine] = acc
            else:
                g_ref[:, mine] = acc

    shapes = [jax.ShapeDtypeStruct((r.shape[1], 2 * r.shape[2]), F32) for r in rs[:n - n_all]]
    shapes += [jax.ShapeDtypeStruct((N_CHIPS, r.shape[1], 2 * r.shape[2]), F32) for r in rs[n - n_all:]]
    return pl.pallas_call(
        body, name="sum_chips_small",
        in_specs=[pl.BlockSpec(memory_space=pltpu.SMEM)] + [VMEM_SPEC] * (2 * n), out_specs=[VMEM_SPEC] * n,
        out_shape=shapes, compiler_params=pltpu.CompilerParams(vmem_limit_bytes=VMEM_LIMIT),
    )(idx, *qs, *rs)


def _share(shards, alls):
    n, na = len(shards), len(alls)
    total = n + na

    def body(*refs):
        g_refs, a_refs = refs[total:total + n], refs[total + n:2 * total]
        send_sems, recv_sems = refs[2 * total:]
        x, y, c, others = _place()
        me = 2 * x + y
        sibling = (x, y, 1 - c)

        def cols_of(w, half):
            h = shards[w].shape[1] // 2
            return g_refs[w].at[:, pl.ds(half * h, h)]

        def slab(a, chip, half):
            h = alls[a].shape[2] // 2
            return a_refs[a].at[chip, :, pl.ds(half * h, h)]

        def copy(src, dst, k, to):
            return pltpu.make_async_remote_copy(src_ref=src, dst_ref=dst, send_sem=send_sems.at[k],
                                                recv_sem=recv_sems.at[k], device_id=to, device_id_type=MESH)

        cps = [copy(cols_of(w, c), cols_of(w, c), w, sibling) for w in range(n)]
        for a in range(na):
            base = n + 7 * a
            cps.append(copy(slab(a, me, c), slab(a, me, c), base, sibling))
            for j, (px, py) in enumerate(others):
                cps.append(copy(slab(a, me, c), slab(a, me, c), base + 1 + j, (px, py, c)))
        for cp in cps:
            cp.start()
        fwd = []
        for a in range(na):
            base = n + 7 * a
            for j, (px, py) in enumerate(others):
                chip = 2 * px + py
                copy(slab(a, me, c), slab(a, chip, c), base + 1 + j, (px, py, c)).wait_recv()
                cp = copy(slab(a, chip, c), slab(a, chip, c), base + 4 + j, sibling)
                cp.start()
                fwd.append(cp)
        for a in range(na):
            base = n + 7 * a
            for j, (px, py) in enumerate(others):
                chip = 2 * px + py
                copy(slab(a, chip, c), slab(a, chip, 1 - c), base + 4 + j, sibling).wait_recv()
            copy(slab(a, me, c), slab(a, me, 1 - c), base, sibling).wait_recv()
        for w in range(n):
            copy(cols_of(w, c), cols_of(w, 1 - c), w, sibling).wait_recv()
        for cp in cps + fwd:
            cp.wait_send()

    nsem = n + 7 * na
    return pl.pallas_call(
        body, name="share", in_specs=[HBM_SPEC] * total, out_specs=[HBM_SPEC] * total,
        out_shape=[jax.ShapeDtypeStruct(a.shape, F32) for a in (*shards, *alls)],
        input_output_aliases={i: i for i in range(total)},
        scratch_shapes=[pltpu.SemaphoreType.DMA((nsem,)), pltpu.SemaphoreType.DMA((nsem,))],
    )(*shards, *alls)


def _adamw(w, g, m, v):
    m2 = ADAM_B1 * m + (1.0 - ADAM_B1) * g
    v2 = ADAM_B2 * v + (1.0 - ADAM_B2) * (g * g)
    m_hat = m2 / (1.0 - ADAM_B1 ** ADAM_STEP)
    v_hat = v2 / (1.0 - ADAM_B2 ** ADAM_STEP)
    return -ADAM_LR * (m_hat / (jnp.sqrt(v_hat) + ADAM_EPS) + ADAM_WD * w), m2, v2


def _update_tiled(w, g, m, v, tile):
    def body(w_ref, g_ref, m_ref, v_ref, g2_ref, d_ref, m2_ref, v2_ref):
        g = g_ref[...]
        g2_ref[...] = g
        d_ref[...], m2_ref[...], v2_ref[...] = _adamw(w_ref[...], g, m_ref[...], v_ref[...])

    spec = pl.BlockSpec((tile, w.shape[1]), lambda i: (i, 0))
    return pl.pallas_call(
        body, name="update_w_in", grid=(w.shape[0] // tile,), in_specs=[spec] * 4, out_specs=[spec] * 4,
        out_shape=[jax.ShapeDtypeStruct(w.shape, F32)] * 4,
        compiler_params=_params(("parallel",)),
    )(w, g, m, v)


def _update_small(ws, gs, ms, vs):
    n = len(ws)

    def body(*refs):
        for k in range(n):
            w_ref, g_ref, m_ref, v_ref = refs[k], refs[n + k], refs[2 * n + k], refs[3 * n + k]
            d, m2, v2 = _adamw(w_ref[...], g_ref[...], m_ref[...], v_ref[...])
            refs[4 * n + k][...] = d
            refs[5 * n + k][...] = m2
            refs[6 * n + k][...] = v2

    shapes = [jax.ShapeDtypeStruct(w.shape, F32) for w in ws]
    outs = pl.pallas_call(
        body, name="update_small", in_specs=[VMEM_SPEC] * (4 * n), out_specs=[VMEM_SPEC] * (3 * n),
        out_shape=shapes * 3,
        compiler_params=pltpu.CompilerParams(vmem_limit_bytes=VMEM_LIMIT),
    )(*ws, *gs, *ms, *vs)
    return outs[:n], outs[n:2 * n], outs[2 * n:]


SHARDED = ("w_in", "w_uq", "w_oa", "w_ob", "w_out")
REPLICATED = ("b_in", "g_q", "g_kv", "w_ukv", "sgu_ln_g", "sgu_ln_b", "w_s", "b_s", "ln_g", "ln_b")
ORDER = ("w_in", "b_in", "g_q", "w_uq", "g_kv", "w_ukv", "w_oa", "sgu_ln_g", "sgu_ln_b", "w_s", "b_s", "w_ob", "w_out",
         "ln_g", "ln_b")


def kernel(x, positions, w_in, b_in, g_q, w_uq, g_kv, w_ukv, w_oa, sgu_ln_g, sgu_ln_b, w_s, b_s, w_ob, w_out, ln_g, ln_b, loss_target, m_w_in, m_b_in, m_g_q, m_w_uq, m_g_kv, m_w_ukv, m_w_oa, m_sgu_ln_g, m_sgu_ln_b, m_w_s, m_b_s, m_w_ob, m_w_out, m_ln_g, m_ln_b, v_w_in, v_b_in, v_g_q, v_w_uq, v_g_kv, v_w_ukv, v_w_oa, v_sgu_ln_g, v_sgu_ln_b, v_w_s, v_b_s, v_w_ob, v_w_out, v_ln_g, v_ln_b):
    w = dict(w_in=w_in, b_in=b_in, g_q=g_q, w_uq=w_uq, g_kv=g_kv, w_ukv=w_ukv, w_oa=w_oa, sgu_ln_g=sgu_ln_g,
             sgu_ln_b=sgu_ln_b, w_s=w_s, b_s=b_s, w_ob=w_ob, w_out=w_out, ln_g=ln_g, ln_b=ln_b)
    m = dict(w_in=m_w_in, b_in=m_b_in, g_q=m_g_q, w_uq=m_w_uq, g_kv=m_g_kv, w_ukv=m_w_ukv, w_oa=m_w_oa,
             sgu_ln_g=m_sgu_ln_g, sgu_ln_b=m_sgu_ln_b, w_s=m_w_s, b_s=m_b_s, w_ob=m_w_ob, w_out=m_w_out, ln_g=m_ln_g,
             ln_b=m_ln_b)
    v = dict(w_in=v_w_in, b_in=v_b_in, g_q=v_g_q, w_uq=v_w_uq, g_kv=v_g_kv, w_ukv=v_w_ukv, w_oa=v_w_oa,
             sgu_ln_g=v_sgu_ln_g, sgu_ln_b=v_sgu_ln_b, w_s=v_w_s, b_s=v_b_s, w_ob=v_w_ob, w_out=v_w_out, ln_g=v_ln_g,
             ln_b=v_ln_b)
    w, m, v = ({n: a[0] for n, a in d.items()} for d in (w, m, v))
    c = lax.axis_index("c")

    wt_shard, mt_shard, vt_shard = (jnp.transpose(d["w_in"]) for d in (w, m, v))
    g_in, g_uq, g_oa, g_ob, g_out = _gather_weights(
        [wt_shard, w["w_uq"].reshape(Q_RANK // 4, HEADS * QK_DIM), w["w_oa"], w["w_ob"], w["w_out"]])
    wt = g_in.reshape(IN_W, D_MODEL)
    full_uq = g_uq.reshape(Q_RANK, HEADS, QK_DIM)
    full_out = g_out.reshape(D_MODEL, D_MODEL)

    loss, early, st = _local_head(
        x[0], positions[0], loss_target[0], wt, w["b_in"], w["g_q"], full_uq, w["g_kv"], w["w_ukv"], g_oa,
        w["sgu_ln_g"], w["sgu_ln_b"], w["w_s"], w["b_s"], g_ob, full_out, w["ln_g"], w["ln_b"])

    xi, yi = lax.axis_index("x"), lax.axis_index("y")
    c1 = c.reshape(1).astype(jnp.int32)
    idx = jnp.stack([2 * xi + yi, 2 * (1 - xi) + yi, 2 * xi + (1 - yi), 2 * (1 - xi) + (1 - yi), c]).astype(jnp.int32)
    parts1 = [early["w_in"].reshape(N_CHIPS, IN_W // N_CHIPS, D_MODEL), early["w_oa"].astype(BF16),
              early["w_ob"].astype(BF16), early["w_out"].reshape(N_CHIPS, SLAB_W, D_MODEL).astype(BF16)]
    *recv1, all_loss = _exchange_pairs_and_loss(parts1, jnp.broadcast_to(loss, LOSS_TILE))
    pairs1 = [_add_pair_tiled(parts1[0], recv1[0], c1),
              *_add_pair_small(parts1[1:], recv1[1:], c1, "add_pair_early")]
    send_sems, recv_sems, q_thru, land_thru, token = _chips_start(pairs1)

    st["delta"] = st["delta"] + token[0, 0]
    dx, late = _local_tail(st)

    grads = {**early, **late}
    rep = jnp.concatenate([_rows8(grads[n]) for n in REPLICATED], axis=0)
    rep = jnp.pad(rep, ((0, N_CHIPS * REP_ROWS - rep.shape[0]), (0, 0))).reshape(N_CHIPS, REP_ROWS, D_MODEL)
    parts2 = [late["w_uq"].reshape(N_CHIPS, Q_RANK // N_CHIPS, HEADS * QK_DIM).astype(BF16), rep.astype(BF16),
              late["w_lat"].reshape(N_CHIPS, LAT_ROWS_PAD // N_CHIPS, D_MODEL)]
    pairs2 = _add_pair_small(parts2, _exchange_pairs(parts2, "exchange_pairs_late"), c1, "add_pair_late")
    landed2 = _exchange_chips(pairs2)
    pairs1, landed1 = _chips_wait(send_sems, recv_sems, q_thru, land_thru, landed2[0])
    sums = [_sum_chips_tiled(pairs1[0], landed1[0], idx, 128),
            *_sum_chips_small([*pairs1[1:], *pairs2], [*landed1[1:], *landed2], idx, 2)]
    *shards, g_rep, g_lat = _share(sums[:-2], sums[-2:])
    loss = jnp.sum(all_loss[:, 0, 0])

    red = {n: s.reshape(w[n].shape) for n, s in zip(("w_oa", "w_ob", "w_out", "w_uq"), shards[1:])}
    g_rep = g_rep.reshape(N_CHIPS * REP_ROWS, D_MODEL)
    off = 0
    for n in REPLICATED:
        rows = _rows8(w[n]).shape[0]
        red[n] = g_rep[off:off + rows].reshape(-1)[:w[n].size].reshape(w[n].shape)
        off += rows
    lat = g_lat.reshape(LAT_ROWS_PAD, D_MODEL)[:LAT_COLS]
    gt_shard = jnp.where(2 * xi + yi == 0, jnp.concatenate([lat, shards[0][LAT_COLS:]], axis=0), shards[0])

    gt, dt, mt, vt2 = _update_tiled(wt_shard, gt_shard, mt_shard, vt_shard, 232)
    red["w_in"] = jnp.transpose(gt)
    small = [n for n in ORDER if n != "w_in"]
    as2d = lambda a: a.reshape(-1, a.shape[-1])
    ds, ms, vs = _update_small([as2d(w[n]) for n in small], [as2d(red[n]) for n in small],
                               [as2d(m[n]) for n in small], [as2d(v[n]) for n in small])
    delta, new_m, new_v = {"w_in": jnp.transpose(dt)}, {"w_in": jnp.transpose(mt)}, {"w_in": jnp.transpose(vt2)}
    for i, n in enumerate(small):
        delta[n], new_m[n], new_v[n] = (a[i].reshape(w[n].shape) for a in (ds, ms, vs))

    lead = lambda a: a[None]
    return (loss, dx[None], *[lead(red[n]) for n in ORDER], *[lead(delta[n]) for n in ORDER],
            *[lead(new_m[n]) for n in ORDER], *[lead(new_v[n]) for n in ORDER])
```

```python
import functools
import math

import jax
import jax.numpy as jnp
from jax import lax
from jax.experimental import pallas as pl
from jax.experimental.pallas import tpu as pltpu

F32 = jnp.float32
BF16 = jnp.bfloat16

D_MODEL = 1024
HEADS = 8
Q_RANK = 384
KV_RANK = 128
NOPE = 64
ROPE = 32
V_DIM = 64
QK_DIM = NOPE + ROPE
HEAD_PAD = 128
MLA_W = HEADS * V_DIM
SGU_W = 512
GROUPS = 8
CHUNK = 128
IN_W = 4640
RMS_EPS = 1e-6
LN_EPS = 1e-5
ALPHA = 2.0 ** 0.25
ROPE_THETA = 10000.0
SCALE = QK_DIM ** -0.5

GATE_W = 2 * D_MODEL
MID_W = 4 * SGU_W
LAT_W = Q_RANK + KV_RANK + HEAD_PAD
PAD_W = GATE_W + MID_W + LAT_W
LAT_COLS = Q_RANK + KV_RANK + ROPE
ROW_GATE = LAT_COLS + MID_W
LAT_ROWS_PAD = 576
N_SLABS = 4
SLAB_W = D_MODEL // N_SLABS

ROW_TILE = 256
ATT_TQ = 256
ATT_TK = 256
ATT_BWD_TQ = 256
ATT_BWD_TK = 256
LOG2E = 1.4426950408889634
LN2 = 0.6931471805599453
Q_SCALE = SCALE * LOG2E
VMEM_LIMIT = 56 * 1024 * 1024

ADAM_LR = 0.001
ADAM_B1 = 0.9
ADAM_B2 = 0.999
ADAM_EPS = 1e-08
ADAM_WD = 0.01
ADAM_STEP = 10


def _dot(a, b):
    return jnp.dot(a, b, preferred_element_type=F32)


def _dot_nt(a, b):
    return lax.dot_general(a, b, (((1,), (1,)), ((), ())), preferred_element_type=F32)


def _dot_tn(a, b):
    return lax.dot_general(a, b, (((0,), (0,)), ((), ())), preferred_element_type=F32)


def _sigmoid(z):
    return 1.0 / (1.0 + jnp.exp(-z))


_GELU_C = math.sqrt(2.0 / math.pi)


def _gelu_and_grad(x):
    x2 = x * x
    t = jnp.tanh(_GELU_C * (x + 0.044715 * x * x2))
    g = 0.5 * x * (1.0 + t)
    dg = 0.5 * (1.0 + t) + 0.5 * x * (1.0 - t * t) * (_GELU_C * (1.0 + 3.0 * 0.044715 * x2))
    return g, dg


def _silu_and_grad(z):
    s = _sigmoid(z)
    return z * s, s * (1.0 + z * (1.0 - s))


def _rope(xb, c, sl, sh):
    return xb * c + pltpu.roll(xb, 112, 1) * sl + pltpu.roll(xb, 16, 1) * sh


def _rope_t(dy, c, sl, sh):
    return dy * c + pltpu.roll(dy * sl, 16, 1) + pltpu.roll(dy * sh, 112, 1)


def _params(sem=("arbitrary",)):
    return pltpu.CompilerParams(dimension_semantics=sem, vmem_limit_bytes=VMEM_LIMIT)


def _row_spec(tile, width):
    return pl.BlockSpec((tile, width), lambda i: (i, 0))


def _full_spec(shape):
    nd = len(shape)
    return pl.BlockSpec(shape, lambda i: (0,) * nd)


def _kpe_rows(wt_ref):
    z = lambda n: jnp.zeros((n, D_MODEL), BF16)
    return jnp.concatenate([z(NOPE), wt_ref[Q_RANK + KV_RANK:LAT_COLS, :], z(HEAD_PAD - QK_DIM)], axis=0)


def _fwd_pre(x, wt, b_g, b_m, b_l, g_q, wuq, g_kv, wk, wv, rc, rsl, rsh):
    s = x.shape[0]
    ts = ROW_TILE

    def body(x_ref, wt_ref, bg_ref, bm_ref, bl_ref, gq_ref, wuq_ref, gkv_ref, wk_ref, wv_ref, rc_ref, rsl_ref,
             rsh_ref, hg_ref, hm_ref, hl_ref, q_ref, k_ref, v_ref, xb_ref, qt_ref, kt_ref, vt_ref):
        xb = x_ref[...].astype(BF16)
        xb_ref[...] = xb
        hg_ref[...] = _dot_nt(xb, wt_ref[ROW_GATE:IN_W, :]) + bg_ref[...]
        hm_ref[...] = _dot_nt(xb, wt_ref[LAT_COLS:ROW_GATE, :]) + bm_ref[...]
        hl = jnp.concatenate([_dot_nt(xb, wt_ref[0:Q_RANK + KV_RANK, :]), _dot_nt(xb, _kpe_rows(wt_ref))],
                             axis=1) + bl_ref[...]
        hl_ref[...] = hl
        c, sl, sh = rc_ref[...], rsl_ref[...], rsh_ref[...]
        cq = hl[:, :Q_RANK]
        cqn = cq * lax.rsqrt(jnp.mean(cq * cq, axis=-1, keepdims=True) + RMS_EPS) * gq_ref[...]
        q = _dot(cqn.astype(BF16), wuq_ref[...])
        ckv = hl[:, Q_RANK:Q_RANK + KV_RANK]
        ckvn = (ckv * lax.rsqrt(jnp.mean(ckv * ckv, axis=-1, keepdims=True) + RMS_EPS) * gkv_ref[...]).astype(BF16)
        k = _dot(ckvn, wk_ref[...])
        vb = _dot(ckvn, wv_ref[...]).astype(BF16)
        v_ref[...] = vb
        vt_ref[...] = vb.T
        kpe = _rope(hl[:, Q_RANK + KV_RANK:], c, sl, sh)
        for hd in range(HEADS):
            lanes = slice(hd * HEAD_PAD, (hd + 1) * HEAD_PAD)
            qb = (_rope(q[:, lanes], c, sl, sh) * Q_SCALE).astype(BF16)
            kb = (k[:, lanes] + kpe).astype(BF16)
            q_ref[:, lanes] = qb
            k_ref[:, lanes] = kb
            qt_ref[lanes, :] = qb.T
            kt_ref[lanes, :] = kb.T

    qk_w = HEADS * HEAD_PAD
    col_spec = lambda rows: pl.BlockSpec((rows, ts), lambda i: (0, i))
    return pl.pallas_call(
        body, name="fwd_pre", grid=(s // ts,),
        in_specs=[_row_spec(ts, D_MODEL), _full_spec(wt.shape), _full_spec(b_g.shape), _full_spec(b_m.shape),
                  _full_spec(b_l.shape), _full_spec(g_q.shape),
                  _full_spec(wuq.shape), _full_spec(g_kv.shape), _full_spec(wk.shape), _full_spec(wv.shape),
                  _row_spec(ts, HEAD_PAD), _row_spec(ts, HEAD_PAD), _row_spec(ts, HEAD_PAD)],
        out_specs=[_row_spec(ts, GATE_W), _row_spec(ts, MID_W), _row_spec(ts, LAT_W), _row_spec(ts, qk_w),
                   _row_spec(ts, qk_w), _row_spec(ts, MLA_W), _row_spec(ts, D_MODEL), col_spec(qk_w), col_spec(qk_w),
                   col_spec(MLA_W)],
        out_shape=[jax.ShapeDtypeStruct((s, GATE_W), F32), jax.ShapeDtypeStruct((s, MID_W), F32),
                   jax.ShapeDtypeStruct((s, LAT_W), F32), jax.ShapeDtypeStruct((s, qk_w), BF16),
                   jax.ShapeDtypeStruct((s, qk_w), BF16), jax.ShapeDtypeStruct((s, MLA_W), BF16),
                   jax.ShapeDtypeStruct((s, D_MODEL), BF16), jax.ShapeDtypeStruct((qk_w, s), BF16),
                   jax.ShapeDtypeStruct((qk_w, s), BF16), jax.ShapeDtypeStruct((MLA_W, s), BF16)],
        compiler_params=_params(),
    )(x, wt, b_g, b_m, b_l, g_q, wuq, g_kv, wk, wv, rc, rsl, rsh)


def _attn_fwd(qt, k, vt):
    s = k.shape[0]
    tq, tk = ATT_TQ, ATT_TK
    r = tq // tk
    pairs = HEADS // 2

    def body(qt_ref, k_ref, vt_ref, o_ref, lse_ref):
        i = pl.program_id(1)
        krow = lax.broadcasted_iota(jnp.int32, (tk, tq), 0)
        qcol = lax.broadcasted_iota(jnp.int32, (tk, tq), 1)
        qts = [qt_ref[hh * HEAD_PAD:(hh + 1) * HEAD_PAD, :] for hh in range(2)]

        def scores(j):
            koff = pl.multiple_of(j * tk, tk)
            return tuple(_dot(k_ref[pl.ds(koff, tk), hh * HEAD_PAD:(hh + 1) * HEAD_PAD], qts[hh]) for hh in range(2))

        def weighted(j, ps):
            koff = pl.multiple_of(j * tk, tk)
            return tuple(_dot(vt_ref[hh * V_DIM:(hh + 1) * V_DIM, pl.ds(koff, tk)], ps[hh]) for hh in range(2))

        def step(j, carry, diag, last):
            st, ps, stats = carry
            st_next = None if last else scores(j + 1)
            pvs = weighted(jnp.maximum(j - 1, 0), ps)
            new_ps, new_stats = [], []
            for hh in range(2):
                m, l, acc = stats[hh]
                s_ = st[hh]
                if diag is not None:
                    s_ = jnp.where(krow + diag * tk <= qcol, s_, -jnp.inf)
                m_new = jnp.maximum(m, jnp.max(s_, axis=0, keepdims=True))
                a = jnp.exp2(m - m_new)
                p = jnp.exp2(s_ - m_new)
                new_stats.append((m_new, a * l + jnp.sum(p, axis=0, keepdims=True), a * (acc + pvs[hh])))
                new_ps.append(p.astype(BF16))
            return st_next, tuple(new_ps), tuple(new_stats)

        one = (jnp.full((1, tq), -jnp.inf, F32), jnp.zeros((1, tq), F32), jnp.zeros((V_DIM, tq), F32))
        zero_p = jnp.zeros((tk, tq), BF16)
        nfull = i * r
        carry = lax.fori_loop(0, nfull, functools.partial(step, diag=None, last=False),
                              (scores(0), (zero_p, zero_p), (one, one)))
        for d in range(r):
            carry = step(nfull + d, carry, d, d == r - 1)
        _, ps, stats = carry
        pvs = weighted(nfull + r - 1, ps)
        ot = jnp.concatenate([(stats[hh][2] + pvs[hh]) / stats[hh][1] for hh in range(2)], axis=0)
        o_ref[...] = ot.T
        lse = [stats[hh][0] + jnp.log(stats[hh][1]) * LOG2E for hh in range(2)]
        lse_ref[...] = jnp.concatenate(lse + [jnp.zeros((6, tq), F32)], axis=0)

    return pl.pallas_call(
        body, name="attn_fwd", grid=(pairs, s // tq),
        in_specs=[pl.BlockSpec((2 * HEAD_PAD, tq), lambda p, i: (p, i)),
                  pl.BlockSpec((s, 2 * HEAD_PAD), lambda p, i: (0, p)),
                  pl.BlockSpec((2 * V_DIM, s), lambda p, i: (p, 0))],
        out_specs=[pl.BlockSpec((tq, 2 * V_DIM), lambda p, i: (i, p)),
                   pl.BlockSpec((None, 8, tq), lambda p, i: (p, 0, i))],
        out_shape=[jax.ShapeDtypeStruct((s, MLA_W), F32), jax.ShapeDtypeStruct((pairs, 8, s), F32)],
        compiler_params=_params(("arbitrary", "arbitrary")),
    )(qt, k, vt)


def _attn_bwd(q, qt, k, kt, v, do, dot, lse, delta):
    s = k.shape[0]
    tq, tk = ATT_BWD_TQ, ATT_BWD_TK
    r = tq // tk
    nq = s // tq
    nk = s // tk
    pairs = HEADS // 2

    def body(q_ref, qt_ref, k_ref, kt_ref, v_ref, do_ref, dot_ref, lse_ref, dl_ref, dqt_ref, dk_ref, dv_ref):
        j = pl.program_id(1)
        krow = lax.broadcasted_iota(jnp.int32, (tk, tq), 0)
        qcol = lax.broadcasted_iota(jnp.int32, (tk, tq), 1)
        lane = lax.broadcasted_iota(jnp.int32, (tk, 2 * V_DIM), 1)
        drow = lax.broadcasted_iota(jnp.int32, (2 * V_DIM, tq), 0)

        @pl.when(j == 0)
        def _():
            dqt_ref[...] = jnp.zeros_like(dqt_ref)

        koff = pl.multiple_of(j * tk, tk)
        vb = v_ref[pl.ds(koff, tk), :]
        kbs = [k_ref[pl.ds(koff, tk), hh * HEAD_PAD:(hh + 1) * HEAD_PAD] for hh in range(2)]
        ktbs = [kt_ref[hh * HEAD_PAD:(hh + 1) * HEAD_PAD, pl.ds(koff, tk)] for hh in range(2)]
        i0 = j // r

        def front(i):
            qoff = pl.multiple_of(i * tq, tq)
            dotb = dot_ref[:, pl.ds(qoff, tq)]
            out = []
            for hh in range(2):
                mine = (drow < V_DIM) if hh == 0 else (drow >= V_DIM)
                st = _dot(kbs[hh], qt_ref[hh * HEAD_PAD:(hh + 1) * HEAD_PAD, pl.ds(qoff, tq)])
                out.append((st, _dot(vb, jnp.where(mine, dotb, jnp.zeros_like(dotb)))))
            return tuple(out)

        def middle(i, tiles, diag):
            qoff = pl.multiple_of(i * tq, tq)
            out = []
            for hh in range(2):
                st, dpt = tiles[hh]
                if diag:
                    st = jnp.where(krow + (j - i0 * r) * tk <= qcol, st, -jnp.inf)
                p = jnp.exp2(st - lse_ref[hh:hh + 1, pl.ds(qoff, tq)])
                out.append((p.astype(BF16), (p * (dpt - dl_ref[hh:hh + 1, pl.ds(qoff, tq)])).astype(BF16)))
            return tuple(out)

        def back(i, pd, accs):
            qoff = pl.multiple_of(i * tq, tq)
            dob = do_ref[pl.ds(qoff, tq), :]
            out = []
            for hh in range(2):
                rows = slice(hh * HEAD_PAD, (hh + 1) * HEAD_PAD)
                p, dst = pd[hh]
                dk_acc, dv_acc = accs[hh]
                dv_acc = dv_acc + _dot(p, dob)
                dk_acc = dk_acc + _dot(dst, q_ref[pl.ds(qoff, tq), rows])
                dqt_ref[rows, pl.ds(qoff, tq)] += _dot(ktbs[hh], dst)
                out.append((dk_acc, dv_acc))
            return tuple(out)

        def step(i, accs, diag):
            return back(i, middle(i, front(i), diag), accs)

        zero_acc = (jnp.zeros((tk, HEAD_PAD), F32), jnp.zeros((tk, 2 * V_DIM), F32))
        accs = step(i0, (zero_acc, zero_acc), True)
        accs = lax.fori_loop(i0 + 1, nq, functools.partial(step, diag=False), accs)
        for hh in range(2):
            dk_ref[:, hh * HEAD_PAD:(hh + 1) * HEAD_PAD] = accs[hh][0] * LN2
        dv_ref[...] = jnp.where(lane < V_DIM, accs[0][1], accs[1][1])

        @pl.when(j == nk - 1)
        def _():
            dqt_ref[...] = dqt_ref[...] * SCALE

    pair_rows = lambda w: pl.BlockSpec((s, w), lambda p, j: (0, p))
    pair_cols = lambda w: pl.BlockSpec((w, s), lambda p, j: (p, 0))
    stats = pl.BlockSpec((None, 8, s), lambda p, j: (p, 0, 0))
    return pl.pallas_call(
        body, name="attn_bwd", grid=(pairs, nk),
        in_specs=[pair_rows(2 * HEAD_PAD), pair_cols(2 * HEAD_PAD), pair_rows(2 * HEAD_PAD), pair_cols(2 * HEAD_PAD),
                  pair_rows(2 * V_DIM), pair_rows(2 * V_DIM), pair_cols(2 * V_DIM), stats, stats],
        out_specs=[pair_cols(2 * HEAD_PAD),
                   pl.BlockSpec((tk, 2 * HEAD_PAD), lambda p, j: (j, p)),
                   pl.BlockSpec((tk, 2 * V_DIM), lambda p, j: (j, p))],
        out_shape=[jax.ShapeDtypeStruct((HEADS * HEAD_PAD, s), F32), jax.ShapeDtypeStruct((s, HEADS * HEAD_PAD), F32),
                   jax.ShapeDtypeStruct((s, MLA_W), F32)],
        compiler_params=_params(("arbitrary", "arbitrary")),
    )(q, qt, k, kt, v, do, dot, lse, delta)


def _split3(a):
    hi = a.astype(BF16)
    r1 = a - hi.astype(F32)
    mid = r1.astype(BF16)
    lo = (r1 - mid.astype(F32)).astype(BF16)
    return hi, mid, lo


def _mid(x, tgt, o, hm, hg, woa, wob, wout, ln_g, ln_b, sg_g, sg_b, w_s, bsb):
    s = x.shape[0]
    ts = ROW_TILE
    nsteps = s // ts
    nch = ts // CHUNK
    npair = GROUPS // 2

    def body(x_ref, t_ref, o_ref, hm_ref, hg_ref, woa_ref, wob_ref, wout_ref, lng_ref, lnb_ref, sgg_ref, sgb_ref,
             ws_ref, bsb_ref,
             dr_ref, dhg_ref, dhm_ref, do_ref, dot_ref, dl_ref, dhgt_ref, dhmt_ref,
             dwout_ref, dwoa_ref, dwob_ref, dws_ref, dbs_ref, dlng_ref, dlnb_ref, dsgg_ref, dsgb_ref, loss_ref,
             dbg_ref, dbm_ref, dbacc_ref):
        i = pl.program_id(0)

        @pl.when(i == 0)
        def _():
            for r in (dwout_ref, dwoa_ref, dwob_ref, dws_ref, dlng_ref, dlnb_ref, dsgg_ref, dsgb_ref, loss_ref,
                      dbg_ref, dbm_ref, dbacc_ref):
                r[...] = jnp.zeros_like(r)

        def emit(ref, tref, bref, lo, val):
            vb = val.astype(BF16)
            n = val.shape[1]
            ref[:, lo:lo + n] = vb
            tref[lo:lo + n, :] = vb.T
            bref[:, lo:lo + n] += jnp.sum(val, axis=0, keepdims=True)

        lane = lax.broadcasted_iota(jnp.int32, (CHUNK, CHUNK), 1)
        left = lane < V_DIM
        tril = lax.broadcasted_iota(jnp.int32, (CHUNK, CHUNK), 0) >= lane
        ms = [jnp.where(tril, ws_ref[g], 0.0).astype(BF16) for g in range(GROUPS)]

        z_a = hm_ref[:, 0:SGU_W]
        u = hm_ref[:, SGU_W:2 * SGU_W]
        v = hm_ref[:, 2 * SGU_W:3 * SGU_W]
        z_b = hm_ref[:, 3 * SGU_W:4 * SGU_W]
        o = o_ref[...]
        sa, dsa = _silu_and_grad(z_a)
        y_a = (o * sa).astype(BF16)
        gu, dgu = _gelu_and_grad(u)
        gv, dgv = _gelu_and_grad(v)
        mu = jnp.mean(gv, axis=-1, keepdims=True)
        vc = gv - mu
        rstd_v = lax.rsqrt(jnp.mean(vc * vc, axis=-1, keepdims=True) + LN_EPS)
        vhat = vc * rstd_v
        vn = (vhat * sgg_ref[...] + sgb_ref[...]).astype(BF16)
        rows = []
        for c in range(nch):
            blocks = []
            for p in range(npair):
                blk = vn[c * CHUNK:(c + 1) * CHUNK, p * CHUNK:(p + 1) * CHUNK]
                blocks.append(jnp.where(left, _dot(ms[2 * p], blk), _dot(ms[2 * p + 1], blk)))
            rows.append(jnp.concatenate(blocks, axis=1) + bsb_ref[...])
        mixed = jnp.concatenate(rows, axis=0)
        sgu = gu * mixed
        sb, dsb = _silu_and_grad(z_b)
        y_b = (sgu * sb).astype(BF16)
        pa = jnp.concatenate([_dot(y_a, woa_ref[k]) for k in range(N_SLABS)], axis=1)
        pb = jnp.concatenate([_dot(y_b, wob_ref[k]) for k in range(N_SLABS)], axis=1)
        sga = _sigmoid(hg_ref[:, :D_MODEL])
        sgb = _sigmoid(hg_ref[:, D_MODEL:])
        m2 = (sga * pa + sgb * pb).astype(BF16)
        r = ALPHA * x_ref[...] + _dot(m2, wout_ref[...])
        rmu = jnp.mean(r, axis=-1, keepdims=True)
        rc = r - rmu
        rstd = lax.rsqrt(jnp.mean(rc * rc, axis=-1, keepdims=True) + LN_EPS)
        xhat = rc * rstd
        y = xhat * lng_ref[...] + lnb_ref[...]
        err = y - t_ref[...]
        loss_ref[...] += jnp.full(loss_ref.shape, 0.5 / D_MODEL, F32) * jnp.sum(err * err)

        dy = err * (1.0 / D_MODEL)
        dlng_ref[...] += jnp.sum(dy * xhat, axis=0, keepdims=True)
        dlnb_ref[...] += jnp.sum(dy, axis=0, keepdims=True)
        dxh = dy * lng_ref[...]
        dr = rstd * (dxh - jnp.mean(dxh, axis=-1, keepdims=True) - xhat * jnp.mean(dxh * xhat, axis=-1, keepdims=True))
        dr_ref[...] = dr
        drb = dr.astype(BF16)
        dwout_ref[...] += _dot_tn(m2, drb)
        dm2 = _dot_nt(drb, wout_ref[...])
        emit(dhg_ref, dhgt_ref, dbg_ref, 0, dm2 * pa * sga * (1.0 - sga))
        emit(dhg_ref, dhgt_ref, dbg_ref, D_MODEL, dm2 * pb * sgb * (1.0 - sgb))
        dpa = (dm2 * sga).astype(BF16)
        dpb = (dm2 * sgb).astype(BF16)
        dy_a = jnp.zeros((ts, MLA_W), F32)
        dy_b = jnp.zeros((ts, SGU_W), F32)
        for k in range(N_SLABS):
            cols = slice(k * SLAB_W, (k + 1) * SLAB_W)
            dwoa_ref[k] += _dot_tn(y_a, dpa[:, cols])
            dwob_ref[k] += _dot_tn(y_b, dpb[:, cols])
            dy_a = dy_a + _dot_nt(dpa[:, cols], woa_ref[k])
            dy_b = dy_b + _dot_nt(dpb[:, cols], wob_ref[k])
        dob = (dy_a * sa).astype(BF16)
        do_ref[...] = dob
        dot_ref[...] = dob.T
        head = (lax.broadcasted_iota(jnp.int32, (HEADS, MLA_W), 1) // V_DIM
                == lax.broadcasted_iota(jnp.int32, (HEADS, MLA_W), 0)).astype(BF16)
        dl_ref[...] = sum(_dot_nt(head, term) for term in _split3(dob.astype(F32) * o))
        emit(dhm_ref, dhmt_ref, dbm_ref, 0, dy_a * o * dsa)
        dsg = dy_b * sb
        emit(dhm_ref, dhmt_ref, dbm_ref, 3 * SGU_W, dy_b * sgu * dsb)
        emit(dhm_ref, dhmt_ref, dbm_ref, SGU_W, dsg * mixed * dgu)
        dmixed = dsg * gu
        dvn_rows = []
        dbs_sum = jnp.zeros((CHUNK, SGU_W), F32)
        for c in range(nch):
            dm_c = dmixed[c * CHUNK:(c + 1) * CHUNK, :]
            dbs_sum = dbs_sum + dm_c
            blocks = []
            for p in range(npair):
                dmb = dm_c[:, p * CHUNK:(p + 1) * CHUNK].astype(BF16)
                blk = vn[c * CHUNK:(c + 1) * CHUNK, p * CHUNK:(p + 1) * CHUNK]
                blocks.append(jnp.where(left, _dot_tn(ms[2 * p], dmb), _dot_tn(ms[2 * p + 1], dmb)))
                zero = jnp.zeros_like(dmb)
                dws_ref[2 * p] += jnp.where(tril, _dot_nt(jnp.where(left, dmb, zero), blk), 0.0)
                dws_ref[2 * p + 1] += jnp.where(tril, _dot_nt(jnp.where(left, zero, dmb), blk), 0.0)
            dvn_rows.append(jnp.concatenate(blocks, axis=1))
        dbacc_ref[...] += dbs_sum
        dvn = jnp.concatenate(dvn_rows, axis=0)
        dsgg_ref[...] += jnp.sum(dvn * vhat, axis=0, keepdims=True)
        dsgb_ref[...] += jnp.sum(dvn, axis=0, keepdims=True)
        dvh = dvn * sgg_ref[...]
        dgv_in = rstd_v * (dvh - jnp.mean(dvh, axis=-1, keepdims=True)
                           - vhat * jnp.mean(dvh * vhat, axis=-1, keepdims=True))
        emit(dhm_ref, dhmt_ref, dbm_ref, 2 * SGU_W, dgv_in * dgv)

        @pl.when(i == nsteps - 1)
        def _():
            grp = (lax.broadcasted_iota(jnp.int32, (SGU_W, CHUNK), 0) // V_DIM
                   == lax.broadcasted_iota(jnp.int32, (SGU_W, CHUNK), 1)).astype(BF16)
            hi, mid, lo = _split3(dbacc_ref[...])
            dbs_ref[...] = _dot(hi, grp) + _dot(mid, grp) + _dot(lo, grp)

    acc_shapes = [(D_MODEL, D_MODEL), woa.shape, wob.shape, (GROUPS, CHUNK, CHUNK), (CHUNK, CHUNK),
                  (1, D_MODEL), (1, D_MODEL), (1, SGU_W), (1, SGU_W), (1, 128), (1, GATE_W), (1, MID_W)]
    col_spec = lambda rows: pl.BlockSpec((rows, ts), lambda i: (0, i))
    return pl.pallas_call(
        body, name="mid", grid=(nsteps,),
        in_specs=[_row_spec(ts, D_MODEL), _row_spec(ts, D_MODEL), _row_spec(ts, MLA_W), _row_spec(ts, MID_W),
                  _row_spec(ts, GATE_W), _full_spec(woa.shape), _full_spec(wob.shape), _full_spec(wout.shape),
                  _full_spec(ln_g.shape), _full_spec(ln_b.shape), _full_spec(sg_g.shape), _full_spec(sg_b.shape),
                  _full_spec(w_s.shape), _full_spec(bsb.shape)],
        out_specs=[_row_spec(ts, D_MODEL), _row_spec(ts, GATE_W), _row_spec(ts, MID_W), _row_spec(ts, MLA_W),
                   col_spec(MLA_W), col_spec(HEADS), col_spec(GATE_W), col_spec(MID_W)]
        + [_full_spec(sh) for sh in acc_shapes],
        out_shape=[jax.ShapeDtypeStruct((s, D_MODEL), F32), jax.ShapeDtypeStruct((s, GATE_W), BF16),
                   jax.ShapeDtypeStruct((s, MID_W), BF16), jax.ShapeDtypeStruct((s, MLA_W), BF16),
                   jax.ShapeDtypeStruct((MLA_W, s), BF16), jax.ShapeDtypeStruct((HEADS, s), F32),
                   jax.ShapeDtypeStruct((GATE_W, s), BF16), jax.ShapeDtypeStruct((MID_W, s), BF16)]
        + [jax.ShapeDtypeStruct(sh, F32) for sh in acc_shapes],
        scratch_shapes=[pltpu.VMEM((CHUNK, SGU_W), F32)],
        compiler_params=_params(),
    )(x, tgt, o, hm, hg, woa, wob, wout, ln_g, ln_b, sg_g, sg_b, w_s, bsb)


def _lat_bwd(dq, dk, dv, hl, rc, rsl, rsh, g_q, g_kv, wuq, wk, wv):
    s = dk.shape[0]
    ts = ROW_TILE
    qk_w = HEADS * HEAD_PAD

    def body(dq_ref, dk_ref, dv_ref, hl_ref, rc_ref, rsl_ref, rsh_ref, gq_ref, gkv_ref, wuq_ref, wk_ref, wv_ref,
             dhl_ref, dhlt_ref, dwuq_ref, dwk_ref, dwv_ref, dgq_ref, dgkv_ref, dbl_ref):
        i = pl.program_id(0)

        @pl.when(i == 0)
        def _():
            for r in (dwuq_ref, dwk_ref, dwv_ref, dgq_ref, dgkv_ref, dbl_ref):
                r[...] = jnp.zeros_like(r)

        def emit(lo, val):
            vb = val.astype(BF16)
            n = val.shape[1]
            dhl_ref[:, lo:lo + n] = vb
            dhlt_ref[lo:lo + n, :] = vb.T
            dbl_ref[:, lo:lo + n] += jnp.sum(val, axis=0, keepdims=True)

        c, sl, sh = rc_ref[...], rsl_ref[...], rsh_ref[...]
        lane = lax.broadcasted_iota(jnp.int32, (ts, HEAD_PAD), 1)
        pe = (lane >= NOPE) & (lane < QK_DIM)
        dkpe = jnp.zeros((ts, HEAD_PAD), F32)
        dqu = []
        for hd in range(HEADS):
            lanes = slice(hd * HEAD_PAD, (hd + 1) * HEAD_PAD)
            dqu.append(_rope_t(dq_ref[lanes, :].T, c, sl, sh).astype(BF16))
            dkpe = dkpe + dk_ref[:, lanes]
        dqu = jnp.concatenate(dqu, axis=1)
        dkpe = _rope_t(jnp.where(pe, dkpe, 0.0), c, sl, sh)

        cq = hl_ref[:, :Q_RANK]
        rq = lax.rsqrt(jnp.mean(cq * cq, axis=-1, keepdims=True) + RMS_EPS)
        cqh = cq * rq
        cqn = (cqh * gq_ref[...]).astype(BF16)
        dwuq_ref[...] += _dot_tn(cqn, dqu)
        dcqn = _dot_nt(dqu, wuq_ref[...])
        dgq_ref[...] += jnp.sum(dcqn * cqh, axis=0, keepdims=True)
        dch = dcqn * gq_ref[...]
        emit(0, rq * (dch - cqh * jnp.mean(dch * cqh, axis=-1, keepdims=True)))

        ckv = hl_ref[:, Q_RANK:Q_RANK + KV_RANK]
        rk = lax.rsqrt(jnp.mean(ckv * ckv, axis=-1, keepdims=True) + RMS_EPS)
        ckh = ckv * rk
        ckn = (ckh * gkv_ref[...]).astype(BF16)
        dkb = dk_ref[...].astype(BF16)
        dvb = dv_ref[...].astype(BF16)
        dwk_ref[...] += _dot_tn(ckn, dkb)
        dwv_ref[...] += _dot_tn(ckn, dvb)
        dckn = _dot_nt(dkb, wk_ref[...]) + _dot_nt(dvb, wv_ref[...])
        dgkv_ref[...] += jnp.sum(dckn * ckh, axis=0, keepdims=True)
        dkh = dckn * gkv_ref[...]
        emit(Q_RANK, rk * (dkh - ckh * jnp.mean(dkh * ckh, axis=-1, keepdims=True)))
        emit(Q_RANK + KV_RANK, dkpe)

    acc_shapes = [wuq.shape, wk.shape, wv.shape, g_q.shape, g_kv.shape, (1, LAT_W)]
    return pl.pallas_call(
        body, name="lat_bwd", grid=(s // ts,),
        in_specs=[pl.BlockSpec((qk_w, ts), lambda i: (0, i)), _row_spec(ts, qk_w), _row_spec(ts, MLA_W),
                  _row_spec(ts, LAT_W), _row_spec(ts, HEAD_PAD), _row_spec(ts, HEAD_PAD), _row_spec(ts, HEAD_PAD),
                  _full_spec(g_q.shape), _full_spec(g_kv.shape), _full_spec(wuq.shape), _full_spec(wk.shape),
                  _full_spec(wv.shape)],
        out_specs=[_row_spec(ts, LAT_W), pl.BlockSpec((LAT_W, ts), lambda i: (0, i))]
        + [_full_spec(sh) for sh in acc_shapes],
        out_shape=[jax.ShapeDtypeStruct((s, LAT_W), BF16), jax.ShapeDtypeStruct((LAT_W, s), BF16)]
        + [jax.ShapeDtypeStruct(sh, F32) for sh in acc_shapes],
        compiler_params=_params(),
    )(dq, dk, dv, hl, rc, rsl, rsh, g_q, g_kv, wuq, wk, wv)


def _dx(dr, dhg, dhm, dhl, wt, after):
    s = dr.shape[0]
    ts = ROW_TILE

    def body(dr_ref, dhg_ref, dhm_ref, dhl_ref, wt_ref, after_ref, dx_ref):
        dx_ref[...] = (ALPHA * dr_ref[...]
                       + _dot(dhg_ref[...], wt_ref[ROW_GATE:IN_W, :])
                       + _dot(dhm_ref[...], wt_ref[LAT_COLS:ROW_GATE, :])
                       + _dot(dhl_ref[:, 0:Q_RANK + KV_RANK], wt_ref[0:Q_RANK + KV_RANK, :])
                       + _dot(dhl_ref[:, Q_RANK + KV_RANK:], _kpe_rows(wt_ref)))

    return pl.pallas_call(
        body, name="dx", grid=(s // ts,),
        in_specs=[_row_spec(ts, D_MODEL), _row_spec(ts, GATE_W), _row_spec(ts, MID_W), _row_spec(ts, LAT_W),
                  _full_spec(wt.shape), pl.BlockSpec(memory_space=pl.ANY)],
        out_specs=_row_spec(ts, D_MODEL),
        out_shape=jax.ShapeDtypeStruct((s, D_MODEL), F32),
        compiler_params=_params(),
    )(dr, dhg, dhm, dhl, wt, after)


def _dwt(dht, xb, tn, name):
    n, s = dht.shape

    def body(dht_ref, xb_ref, dw_ref):
        dw_ref[...] = _dot(dht_ref[...], xb_ref[...]).astype(BF16)

    return pl.pallas_call(
        body, name=name, grid=(n // tn,),
        in_specs=[pl.BlockSpec((tn, s), lambda i: (i, 0)), _full_spec(xb.shape)],
        out_specs=pl.BlockSpec((tn, D_MODEL), lambda i: (i, 0)),
        out_shape=jax.ShapeDtypeStruct((n, D_MODEL), BF16),
        compiler_params=_params(),
    )(dht, xb)


def _dwt_lat(dhlt, xb):
    n, s = dhlt.shape

    def body(dht_ref, xb_ref, dw_ref):
        dw = _dot(dht_ref[...], xb_ref[...]).astype(BF16)
        kpe = Q_RANK + KV_RANK + NOPE
        dw_ref[0:Q_RANK + KV_RANK, :] = dw[0:Q_RANK + KV_RANK]
        dw_ref[Q_RANK + KV_RANK:LAT_COLS, :] = dw[kpe:kpe + ROPE]
        dw_ref[LAT_COLS:, :] = jnp.zeros((LAT_ROWS_PAD - LAT_COLS, D_MODEL), BF16)

    return pl.pallas_call(
        body, name="dwt_lat", in_specs=[VMEM_SPEC, VMEM_SPEC], out_specs=VMEM_SPEC,
        out_shape=jax.ShapeDtypeStruct((LAT_ROWS_PAD, D_MODEL), BF16),
        compiler_params=pltpu.CompilerParams(vmem_limit_bytes=VMEM_LIMIT),
    )(dhlt, xb)


def _split_bias(b):
    z = lambda n: jnp.zeros((n,), b.dtype)
    lat = jnp.concatenate([b[:Q_RANK + KV_RANK], z(NOPE), b[Q_RANK + KV_RANK:LAT_COLS], z(HEAD_PAD - QK_DIM)])
    return b[None, ROW_GATE:], b[None, LAT_COLS:ROW_GATE], lat[None, :]


def _join_bias(g, m, l):
    kpe = Q_RANK + KV_RANK + NOPE
    return jnp.concatenate([l[0, :Q_RANK + KV_RANK], l[0, kpe:kpe + ROPE], m[0], g[0]])


def _rope_tables(positions):
    half = ROPE // 2
    inv_freq = ROPE_THETA ** (-jnp.arange(0, ROPE, 2, dtype=F32) / ROPE)
    ang = positions.astype(F32)[:, None] * inv_freq
    cos, sin = jnp.cos(ang), jnp.sin(ang)
    n = positions.shape[0]
    one, zero = jnp.ones((n, NOPE), F32), jnp.zeros((n, half), F32)
    tail1, tail0 = jnp.ones((n, HEAD_PAD - QK_DIM), F32), jnp.zeros((n, HEAD_PAD - QK_DIM), F32)
    z64 = jnp.zeros((n, NOPE), F32)
    rc = jnp.concatenate([one, cos, cos, tail1], axis=1)
    rsl = jnp.concatenate([z64, -sin, zero, tail0], axis=1)
    rsh = jnp.concatenate([z64, zero, sin, tail0], axis=1)
    return rc, rsl, rsh


def _local_head(x, positions, tgt, wt, b_in, g_q, w_uq, g_kv, w_ukv, w_oa, sg_g, sg_b, w_s, b_s, w_ob, w_out,
                ln_g, ln_b):
    rc, rsl, rsh = _rope_tables(positions)
    b_g, b_m, b_l = _split_bias(b_in)
    wuq = jnp.pad(w_uq, ((0, 0), (0, 0), (0, HEAD_PAD - QK_DIM))).reshape(Q_RANK, HEADS * HEAD_PAD).astype(BF16)
    wk = jnp.pad(w_ukv[:, :, :NOPE], ((0, 0), (0, 0), (0, HEAD_PAD - NOPE))).reshape(KV_RANK, HEADS * HEAD_PAD).astype(BF16)
    wv = w_ukv[:, :, NOPE:].reshape(KV_RANK, MLA_W).astype(BF16)
    bsb = jnp.repeat(b_s.T, V_DIM, axis=1)
    gq2, gkv2 = g_q[None, :], g_kv[None, :]

    hg, hm, hl, q, k, v, xb, qt, kt, vt = _fwd_pre(x, wt, b_g, b_m, b_l, gq2, wuq, gkv2, wk, wv, rc, rsl, rsh)
    o, lse = _attn_fwd(qt, k, vt)
    (dr, dhg, dhm, do, dot, delta, dhgt, dhmt, dwout, dwoa, dwob, dws, dbs, dlng, dlnb, dsgg, dsgb, loss, dbg,
     dbm) = _mid(x, tgt, o, hm, hg, w_oa, w_ob, w_out, ln_g[None, :], ln_b[None, :], sg_g[None, :], sg_b[None, :],
                 w_s, bsb)
    delta = jnp.pad(delta.reshape(HEADS // 2, 2, -1), ((0, 0), (0, 6), (0, 0)))
    early = {
        "w_in": jnp.concatenate([jnp.zeros((LAT_COLS, D_MODEL), BF16), _dwt(dhmt, xb, 512, "dwt_mid"),
                                 _dwt(dhgt, xb, 512, "dwt_gates")], axis=0),
        "w_oa": dwoa, "sgu_ln_g": dsgg[0], "sgu_ln_b": dsgb[0], "w_s": dws, "b_s": dbs[:, :GROUPS].T,
        "w_ob": dwob, "w_out": dwout, "ln_g": dlng[0], "ln_b": dlnb[0],
    }
    state = dict(q=q, qt=qt, k=k, kt=kt, v=v, do=do, dot=dot, lse=lse, delta=delta, hl=hl, rc=rc, rsl=rsl, rsh=rsh,
                 gq2=gq2, gkv2=gkv2, wuq=wuq, wk=wk, wv=wv, dr=dr, dhg=dhg, dhm=dhm, wt=wt, xb=xb, dbg=dbg, dbm=dbm)
    return loss, early, state


def _local_tail(st):
    dq, dk, dv = _attn_bwd(st["q"], st["qt"], st["k"], st["kt"], st["v"], st["do"], st["dot"], st["lse"], st["delta"])
    dhl, dhlt, dwuq, dwk, dwv, dgq, dgkv, dbl = _lat_bwd(dq, dk, dv, st["hl"], st["rc"], st["rsl"], st["rsh"],
                                                         st["gq2"], st["gkv2"], st["wuq"], st["wk"], st["wv"])
    late = {
        "w_lat": _dwt_lat(dhlt, st["xb"]),
        "b_in": _join_bias(st["dbg"], st["dbm"], dbl),
        "g_q": dgq[0],
        "w_uq": dwuq.reshape(Q_RANK, HEADS, HEAD_PAD)[:, :, :QK_DIM],
        "g_kv": dgkv[0],
        "w_ukv": jnp.concatenate([dwk.reshape(KV_RANK, HEADS, HEAD_PAD)[:, :, :NOPE],
                                  dwv.reshape(KV_RANK, HEADS, V_DIM)], axis=2),
    }
    return dhl, late


def _local_step(*args):
    loss, early, st = _local_head(*args)
    dhl, late = _local_tail(st)
    dx = _dx(st["dr"], st["dhg"], st["dhm"], dhl, st["wt"], dhl)
    grads = {**early, **late}
    grads["w_in"] = jnp.concatenate([grads.pop("w_lat")[:LAT_COLS], early["w_in"][LAT_COLS:]], axis=0)
    return loss, dx, grads


MESH = pl.DeviceIdType.MESH
N_CHIPS = 4
HBM_SPEC = pl.BlockSpec(memory_space=pl.ANY)
HBM_SPEC_STRICT = pl.BlockSpec(memory_space=pltpu.HBM)
VMEM_SPEC = pl.BlockSpec(memory_space=pltpu.VMEM)

REP_ROWS = 80


def _rows8(a):
    flat = a.reshape(-1)
    n = -(-flat.shape[0] // (8 * D_MODEL)) * 8 * D_MODEL
    return jnp.pad(flat, (0, n - flat.shape[0])).reshape(-1, D_MODEL)


def _place():
    x, y, c = lax.axis_index("x"), lax.axis_index("y"), lax.axis_index("c")
    others = [(1 - x, y), (x, 1 - y), (1 - x, 1 - y)]
    return x, y, c, others


def _gather_weights(shards):
    n = len(shards)

    def body(*refs):
        ins, outs, bufs = refs[:n], refs[n:2 * n], refs[2 * n:3 * n]
        send_sems, recv_sems, local_sems = refs[3 * n:]
        x, y, c, others = _place()
        me = 2 * x + y
        sibling = (x, y, 1 - c)
        for src, buf in zip(ins, bufs):
            buf[...] = src[...].astype(BF16)
        own = [pltpu.make_async_copy(bufs[w], outs[w].at[me], local_sems.at[w]) for w in range(n)]
        for cp in own:
            cp.start()

        def part(w, chip, half):
            hc = shards[w].shape[1] // 2
            return outs[w].at[chip, :, pl.ds(half * hc, hc)]

        def sent(w, j):
            hc = shards[w].shape[1] // 2
            return pltpu.make_async_remote_copy(
                src_ref=bufs[w].at[:, pl.ds(c * hc, hc)], dst_ref=part(w, me, c),
                send_sem=send_sems.at[w * 3 + j], recv_sem=recv_sems.at[w * 3 + j],
                device_id=(*others[j], c), device_id_type=MESH)

        def landed(w, j):
            px, py = others[j]
            return pltpu.make_async_remote_copy(
                src_ref=part(w, 2 * px + py, c), dst_ref=part(w, 2 * px + py, c),
                send_sem=send_sems.at[w * 3 + j], recv_sem=recv_sems.at[w * 3 + j],
                device_id=(px, py, c), device_id_type=MESH)

        def passed(w, j, half):
            px, py = others[j]
            k = n * 3 + w * 3 + j
            return pltpu.make_async_remote_copy(
                src_ref=part(w, 2 * px + py, half), dst_ref=part(w, 2 * px + py, half),
                send_sem=send_sems.at[k], recv_sem=recv_sems.at[k], device_id=sibling, device_id_type=MESH)

        first = [sent(w, j) for w in range(n) for j in range(3)]
        for cp in first:
            cp.start()
        fwd = []
        for w in range(n):
            for j in range(3):
                landed(w, j).wait_recv()
                cp = passed(w, j, c)
                cp.start()
                fwd.append(cp)
        for w in range(n):
            for j in range(3):
                passed(w, j, 1 - c).wait_recv()
        for cp in first + fwd:
            cp.wait_send()
        for cp in own:
            cp.wait()

    return pl.pallas_call(
        body, name="gather_weights",
        in_specs=[VMEM_SPEC] * n, out_specs=[HBM_SPEC] * n,
        out_shape=[jax.ShapeDtypeStruct((N_CHIPS,) + s.shape, BF16) for s in shards],
        scratch_shapes=[pltpu.VMEM(s.shape, BF16) for s in shards]
        + [pltpu.SemaphoreType.DMA((6 * n,)), pltpu.SemaphoreType.DMA((6 * n,)), pltpu.SemaphoreType.DMA((n,))],
        compiler_params=pltpu.CompilerParams(vmem_limit_bytes=VMEM_LIMIT),
    )(*shards)


N_DEV = 8
LOSS_TILE = (8, 128)


def _half(a):
    return a.shape[-1] // 2


def _exchange_pairs(parts, name):
    n = len(parts)

    def body(*refs):
        p_refs, r_refs = refs[:n], refs[n:2 * n]
        send_sems, recv_sems = refs[2 * n:]
        x, y, c, _ = _place()
        cps = []
        for w in range(n):
            h = _half(parts[w])
            cps.append(pltpu.make_async_remote_copy(
                src_ref=p_refs[w].at[:, :, pl.ds((1 - c) * h, h)], dst_ref=r_refs[w],
                send_sem=send_sems.at[w], recv_sem=recv_sems.at[w], device_id=(x, y, 1 - c), device_id_type=MESH))
        for cp in cps:
            cp.start()
        for cp in cps:
            cp.wait()

    return pl.pallas_call(
        body, name=name, in_specs=[HBM_SPEC] * n, out_specs=[HBM_SPEC] * n,
        out_shape=[jax.ShapeDtypeStruct((N_CHIPS, p.shape[1], _half(p)), BF16) for p in parts],
        scratch_shapes=[pltpu.SemaphoreType.DMA((n,)), pltpu.SemaphoreType.DMA((n,))],
    )(*parts)


def _exchange_pairs_and_loss(parts, loss):
    n = len(parts)

    def body(*refs):
        p_refs, loss_ref = refs[:n], refs[n]
        r_refs, all_loss_ref = refs[n + 1:2 * n + 1], refs[2 * n + 1]
        send_sems, recv_sems, loss_send, loss_recv, local_sem = refs[2 * n + 2:]
        x, y, c, _ = _place()
        sibling = (x, y, 1 - c)
        cps = []
        for w in range(n):
            h = _half(parts[w])
            cps.append(pltpu.make_async_remote_copy(
                src_ref=p_refs[w].at[:, :, pl.ds((1 - c) * h, h)], dst_ref=r_refs[w],
                send_sem=send_sems.at[w], recv_sem=recv_sems.at[w], device_id=sibling, device_id_type=MESH))
        for cp in cps:
            cp.start()
        me = 4 * x + 2 * y + c
        own = pltpu.make_async_copy(loss_ref, all_loss_ref.at[me], local_sem)
        own.start()
        lcs = []
        for t in range(1, N_DEV):
            d = (me + t) % N_DEV
            lcs.append(pltpu.make_async_remote_copy(
                src_ref=loss_ref, dst_ref=all_loss_ref.at[me], send_sem=loss_send.at[t - 1],
                recv_sem=loss_recv.at[t - 1], device_id=(d // 4, (d // 2) % 2, d % 2), device_id_type=MESH))
        for cp in lcs:
            cp.start()
        for t in range(1, N_DEV):
            d = (me + N_DEV - t) % N_DEV
            pltpu.make_async_remote_copy(
                src_ref=loss_ref, dst_ref=all_loss_ref.at[d], send_sem=loss_send.at[t - 1],
                recv_sem=loss_recv.at[t - 1], device_id=(d // 4, (d // 2) % 2, d % 2), device_id_type=MESH).wait_recv()
        for cp in lcs:
            cp.wait_send()
        for cp in cps:
            cp.wait()
        own.wait()

    return pl.pallas_call(
        body, name="exchange_pairs_and_loss", in_specs=[HBM_SPEC] * (n + 1), out_specs=[HBM_SPEC] * (n + 1),
        out_shape=[jax.ShapeDtypeStruct((N_CHIPS, p.shape[1], _half(p)), BF16) for p in parts]
        + [jax.ShapeDtypeStruct((N_DEV,) + LOSS_TILE, F32)],
        scratch_shapes=[pltpu.SemaphoreType.DMA((n,)), pltpu.SemaphoreType.DMA((n,)),
                        pltpu.SemaphoreType.DMA((N_DEV - 1,)), pltpu.SemaphoreType.DMA((N_DEV - 1,)),
                        pltpu.SemaphoreType.DMA],
    )(*parts, loss)


def _add_pair_tiled(p, r, c):
    rows, h = r.shape[1:]

    def body(c_ref, p_ref, r_ref, q_ref):
        q_ref[...] = (p_ref[...].astype(F32) + r_ref[...].astype(F32)).astype(BF16)

    return pl.pallas_call(
        body, name="add_pair_w_in",
        grid_spec=pltpu.PrefetchScalarGridSpec(
            num_scalar_prefetch=1, grid=(N_CHIPS,),
            in_specs=[pl.BlockSpec((None, rows, h), lambda k, c_ref: (k, 0, c_ref[0])),
                      pl.BlockSpec((None, rows, h), lambda k, c_ref: (k, 0, 0))],
            out_specs=pl.BlockSpec((None, rows, h), lambda k, c_ref: (k, 0, 0))),
        out_shape=jax.ShapeDtypeStruct(r.shape, BF16),
    )(c, p, r)


def _add_pair_small(ps, rs, c, name):
    n = len(ps)

    def body(c_ref, *refs):
        for w in range(n):
            h = _half(ps[w])
            mine = refs[w][:, :, pl.ds(pl.multiple_of(c_ref[0] * h, 128), h)]
            refs[2 * n + w][...] = (mine.astype(F32) + refs[n + w][...].astype(F32)).astype(BF16)

    return pl.pallas_call(
        body, name=name,
        in_specs=[pl.BlockSpec(memory_space=pltpu.SMEM)] + [VMEM_SPEC] * (2 * n), out_specs=[VMEM_SPEC] * n,
        out_shape=[jax.ShapeDtypeStruct(r.shape, BF16) for r in rs],
        compiler_params=pltpu.CompilerParams(vmem_limit_bytes=VMEM_LIMIT),
    )(c, *ps, *rs)


def _exchange_chips(qs):
    n = len(qs)

    def body(*refs):
        q_refs, r_refs = refs[:n], refs[n:2 * n]
        send_sems, recv_sems = refs[2 * n:]
        x, y, c, others = _place()
        me = 2 * x + y
        cps = []
        for w in range(n):
            for j, (px, py) in enumerate(others):
                cps.append(pltpu.make_async_remote_copy(
                    src_ref=q_refs[w].at[2 * px + py], dst_ref=r_refs[w].at[me], send_sem=send_sems.at[3 * w + j],
                    recv_sem=recv_sems.at[3 * w + j], device_id=(px, py, c), device_id_type=MESH))
        for cp in cps:
            cp.start()
        for w in range(n):
            for j, (px, py) in enumerate(others):
                pltpu.make_async_remote_copy(
                    src_ref=q_refs[w].at[me], dst_ref=r_refs[w].at[2 * px + py], send_sem=send_sems.at[3 * w + j],
                    recv_sem=recv_sems.at[3 * w + j], device_id=(px, py, c), device_id_type=MESH).wait_recv()
        for cp in cps:
            cp.wait_send()

    return pl.pallas_call(
        body, name="exchange_chips", in_specs=[HBM_SPEC] * n, out_specs=[HBM_SPEC] * n,
        out_shape=[jax.ShapeDtypeStruct(q.shape, BF16) for q in qs],
        scratch_shapes=[pltpu.SemaphoreType.DMA((3 * n,)), pltpu.SemaphoreType.DMA((3 * n,))],
    )(*qs)


SEM_SPEC = pl.BlockSpec(memory_space=pltpu.SEMAPHORE)
SPLIT_EFFECT = pltpu.SideEffectType.DATAFLOW_SIDE_EFFECTING


def _chips_start(qs, name):
    n = len(qs)

    def body(*refs):
        q_refs, land_refs = refs[:n], refs[n:2 * n]
        send_sems, recv_sems, token = refs[2 * n], refs[2 * n + 1], refs[-1]
        x, y, c, others = _place()
        me = 2 * x + y
        for w in range(n):
            for j, (px, py) in enumerate(others):
                pltpu.make_async_remote_copy(
                    src_ref=q_refs[w].at[2 * px + py], dst_ref=land_refs[w].at[me], send_sem=send_sems.at[3 * w + j],
                    recv_sem=recv_sems.at[3 * w + j], device_id=(px, py, c), device_id_type=MESH).start()
        token[...] = jnp.zeros_like(token)

    hbm = [pltpu.HBM(q.shape, BF16) for q in qs]
    outs = pl.pallas_call(
        body, name=name,
        out_shape=(pltpu.SemaphoreType.DMA((3 * n,)), pltpu.SemaphoreType.DMA((3 * n,)), *hbm, *hbm,
                   jax.ShapeDtypeStruct(LOSS_TILE, F32)),
        in_specs=[HBM_SPEC_STRICT] * (2 * n),
        out_specs=(SEM_SPEC, SEM_SPEC, *[HBM_SPEC_STRICT] * (2 * n), VMEM_SPEC),
        input_output_aliases={i: 2 + i for i in range(2 * n)},
        compiler_params=pltpu.CompilerParams(has_side_effects=SPLIT_EFFECT),
    )(*[pltpu.with_memory_space_constraint(q, pltpu.HBM) for q in qs],
      *[pltpu.with_memory_space_constraint(lax.empty(q.shape, BF16), pltpu.HBM) for q in qs])
    return outs[0], outs[1], outs[2:2 + n], outs[2 + n:2 + 2 * n], outs[-1]


def _chips_wait(send_sems, recv_sems, q_thru, land_thru, after, name):
    n = len(q_thru)

    def body(*refs):
        q_refs, land_refs = refs[:n], refs[n:2 * n]
        send_sems, recv_sems = refs[2 * n], refs[2 * n + 1]
        x, y, c, others = _place()
        me = 2 * x + y
        for w in range(n):
            for j, (px, py) in enumerate(others):
                cp = pltpu.make_async_remote_copy(
                    src_ref=q_refs[w].at[2 * px + py], dst_ref=land_refs[w].at[2 * px + py],
                    send_sem=send_sems.at[3 * w + j], recv_sem=recv_sems.at[3 * w + j], device_id=(px, py, c),
                    device_id_type=MESH)
                cp.wait_send()
                cp.wait_recv()

    outs = pl.pallas_call(
        body, name=name, out_shape=tuple(pltpu.HBM(a.shape, a.dtype) for a in (*q_thru, *land_thru)),
        in_specs=[HBM_SPEC_STRICT] * (2 * n) + [SEM_SPEC, SEM_SPEC, HBM_SPEC],
        out_specs=tuple([HBM_SPEC_STRICT] * (2 * n)), input_output_aliases={i: i for i in range(2 * n)},
        compiler_params=pltpu.CompilerParams(has_side_effects=SPLIT_EFFECT),
    )(*q_thru, *land_thru, send_sems, recv_sems, after)
    return list(outs[:n]), list(outs[n:])


def _sum_chips_tiled(q, r, idx, tile):
    rows, h = r.shape[1:]
    nt = h // tile

    def body(idx_ref, q_ref, r0_ref, r1_ref, r2_ref, g_ref):
        g_ref[...] = (q_ref[...].astype(F32) + r0_ref[...].astype(F32) + r1_ref[...].astype(F32)
                      + r2_ref[...].astype(F32))

    def slab(t):
        return pl.BlockSpec((None, rows, tile), lambda i, idx_ref: (idx_ref[t], 0, i))

    return pl.pallas_call(
        body, name="sum_chips_w_in",
        grid_spec=pltpu.PrefetchScalarGridSpec(
            num_scalar_prefetch=1, grid=(nt,), in_specs=[slab(0), slab(1), slab(2), slab(3)],
            out_specs=pl.BlockSpec((rows, tile), lambda i, idx_ref: (0, idx_ref[4] * nt + i))),
        out_shape=jax.ShapeDtypeStruct((rows, 2 * h), F32),
    )(idx, q, r, r, r)


def _sum_chips_small(qs, rs, idx, n_all):
    n = len(rs)

    def body(idx_ref, *refs):
        c = idx_ref[4]
        for w in range(n):
            q_ref, r_ref, g_ref = refs[w], refs[n + w], refs[2 * n + w]
            acc = q_ref[idx_ref[0]].astype(F32)
            for t in range(1, N_CHIPS):
                acc = acc + r_ref[idx_ref[t]].astype(F32)
            h = rs[w].shape[2]
            mine = pl.ds(pl.multiple_of(c * h, 128), h)
            g_ref[...] = jnp.zeros_like(g_ref)
            if w >= n - n_all:
                g_ref[idx_ref[0], :, mine] = acc
            else:
                g_ref[:, mine] = acc

    shapes = [jax.ShapeDtypeStruct((r.shape[1], 2 * r.shape[2]), F32) for r in rs[:n - n_all]]
    shapes += [jax.ShapeDtypeStruct((N_CHIPS, r.shape[1], 2 * r.shape[2]), F32) for r in rs[n - n_all:]]
    return pl.pallas_call(
        body, name="sum_chips_small",
        in_specs=[pl.BlockSpec(memory_space=pltpu.SMEM)] + [VMEM_SPEC] * (2 * n), out_specs=[VMEM_SPEC] * n,
        out_shape=shapes, compiler_params=pltpu.CompilerParams(vmem_limit_bytes=VMEM_LIMIT),
    )(idx, *qs, *rs)


def _share(shards, alls):
    n, na = len(shards), len(alls)
    total = n + na

    def body(*refs):
        g_refs, a_refs = refs[total:total + n], refs[total + n:2 * total]
        send_sems, recv_sems = refs[2 * total:]
        x, y, c, others = _place()
        me = 2 * x + y
        sibling = (x, y, 1 - c)

        def cols_of(w, half):
            h = shards[w].shape[1] // 2
            return g_refs[w].at[:, pl.ds(half * h, h)]

        def slab(a, chip, half):
            h = alls[a].shape[2] // 2
            return a_refs[a].at[chip, :, pl.ds(half * h, h)]

        def copy(src, dst, k, to):
            return pltpu.make_async_remote_copy(src_ref=src, dst_ref=dst, send_sem=send_sems.at[k],
                                                recv_sem=recv_sems.at[k], device_id=to, device_id_type=MESH)

        cps = [copy(cols_of(w, c), cols_of(w, c), w, sibling) for w in range(n)]
        for a in range(na):
            base = n + 7 * a
            cps.append(copy(slab(a, me, c), slab(a, me, c), base, sibling))
            for j, (px, py) in enumerate(others):
                cps.append(copy(slab(a, me, c), slab(a, me, c), base + 1 + j, (px, py, c)))
        for cp in cps:
            cp.start()
        fwd = []
        for a in range(na):
            base = n + 7 * a
            for j, (px, py) in enumerate(others):
                chip = 2 * px + py
                copy(slab(a, me, c), slab(a, chip, c), base + 1 + j, (px, py, c)).wait_recv()
                cp = copy(slab(a, chip, c), slab(a, chip, c), base + 4 + j, sibling)
                cp.start()
                fwd.append(cp)
        for a in range(na):
            base = n + 7 * a
            for j, (px, py) in enumerate(others):
                chip = 2 * px + py
                copy(slab(a, chip, c), slab(a, chip, 1 - c), base + 4 + j, sibling).wait_recv()
            copy(slab(a, me, c), slab(a, me, 1 - c), base, sibling).wait_recv()
        for w in range(n):
            copy(cols_of(w, c), cols_of(w, 1 - c), w, sibling).wait_recv()
        for cp in cps + fwd:
            cp.wait_send()

    nsem = n + 7 * na
    return pl.pallas_call(
        body, name="share", in_specs=[HBM_SPEC] * total, out_specs=[HBM_SPEC] * total,
        out_shape=[jax.ShapeDtypeStruct(a.shape, F32) for a in (*shards, *alls)],
        input_output_aliases={i: i for i in range(total)},
        scratch_shapes=[pltpu.SemaphoreType.DMA((nsem,)), pltpu.SemaphoreType.DMA((nsem,))],
    )(*shards, *alls)


def _adamw(w, g, m, v):
    m2 = ADAM_B1 * m + (1.0 - ADAM_B1) * g
    v2 = ADAM_B2 * v + (1.0 - ADAM_B2) * (g * g)
    m_hat = m2 / (1.0 - ADAM_B1 ** ADAM_STEP)
    v_hat = v2 / (1.0 - ADAM_B2 ** ADAM_STEP)
    return -ADAM_LR * (m_hat / (jnp.sqrt(v_hat) + ADAM_EPS) + ADAM_WD * w), m2, v2


def _update_tiled(w, g, m, v, tile):
    def body(w_ref, g_ref, m_ref, v_ref, g2_ref, d_ref, m2_ref, v2_ref):
        g = g_ref[...]
        g2_ref[...] = g
        d_ref[...], m2_ref[...], v2_ref[...] = _adamw(w_ref[...], g, m_ref[...], v_ref[...])

    spec = pl.BlockSpec((tile, w.shape[1]), lambda i: (i, 0))
    return pl.pallas_call(
        body, name="update_w_in", grid=(w.shape[0] // tile,), in_specs=[spec] * 4, out_specs=[spec] * 4,
        out_shape=[jax.ShapeDtypeStruct(w.shape, F32)] * 4,
        compiler_params=_params(("parallel",)),
    )(w, g, m, v)


def _update_small(ws, gs, ms, vs):
    n = len(ws)

    def body(*refs):
        for k in range(n):
            w_ref, g_ref, m_ref, v_ref = refs[k], refs[n + k], refs[2 * n + k], refs[3 * n + k]
            d, m2, v2 = _adamw(w_ref[...], g_ref[...], m_ref[...], v_ref[...])
            refs[4 * n + k][...] = d
            refs[5 * n + k][...] = m2
            refs[6 * n + k][...] = v2

    shapes = [jax.ShapeDtypeStruct(w.shape, F32) for w in ws]
    outs = pl.pallas_call(
        body, name="update_small", in_specs=[VMEM_SPEC] * (4 * n), out_specs=[VMEM_SPEC] * (3 * n),
        out_shape=shapes * 3,
        compiler_params=pltpu.CompilerParams(vmem_limit_bytes=VMEM_LIMIT),
    )(*ws, *gs, *ms, *vs)
    return outs[:n], outs[n:2 * n], outs[2 * n:]


SHARDED = ("w_in", "w_uq", "w_oa", "w_ob", "w_out")
REPLICATED = ("b_in", "g_q", "g_kv", "w_ukv", "sgu_ln_g", "sgu_ln_b", "w_s", "b_s", "ln_g", "ln_b")
ORDER = ("w_in", "b_in", "g_q", "w_uq", "g_kv", "w_ukv", "w_oa", "sgu_ln_g", "sgu_ln_b", "w_s", "b_s", "w_ob", "w_out",
         "ln_g", "ln_b")


def kernel(x, positions, w_in, b_in, g_q, w_uq, g_kv, w_ukv, w_oa, sgu_ln_g, sgu_ln_b, w_s, b_s, w_ob, w_out, ln_g, ln_b, loss_target, m_w_in, m_b_in, m_g_q, m_w_uq, m_g_kv, m_w_ukv, m_w_oa, m_sgu_ln_g, m_sgu_ln_b, m_w_s, m_b_s, m_w_ob, m_w_out, m_ln_g, m_ln_b, v_w_in, v_b_in, v_g_q, v_w_uq, v_g_kv, v_w_ukv, v_w_oa, v_sgu_ln_g, v_sgu_ln_b, v_w_s, v_b_s, v_w_ob, v_w_out, v_ln_g, v_ln_b):
    w = dict(w_in=w_in, b_in=b_in, g_q=g_q, w_uq=w_uq, g_kv=g_kv, w_ukv=w_ukv, w_oa=w_oa, sgu_ln_g=sgu_ln_g,
             sgu_ln_b=sgu_ln_b, w_s=w_s, b_s=b_s, w_ob=w_ob, w_out=w_out, ln_g=ln_g, ln_b=ln_b)
    m = dict(w_in=m_w_in, b_in=m_b_in, g_q=m_g_q, w_uq=m_w_uq, g_kv=m_g_kv, w_ukv=m_w_ukv, w_oa=m_w_oa,
             sgu_ln_g=m_sgu_ln_g, sgu_ln_b=m_sgu_ln_b, w_s=m_w_s, b_s=m_b_s, w_ob=m_w_ob, w_out=m_w_out, ln_g=m_ln_g,
             ln_b=m_ln_b)
    v = dict(w_in=v_w_in, b_in=v_b_in, g_q=v_g_q, w_uq=v_w_uq, g_kv=v_g_kv, w_ukv=v_w_ukv, w_oa=v_w_oa,
             sgu_ln_g=v_sgu_ln_g, sgu_ln_b=v_sgu_ln_b, w_s=v_w_s, b_s=v_b_s, w_ob=v_w_ob, w_out=v_w_out, ln_g=v_ln_g,
             ln_b=v_ln_b)
    w, m, v = ({n: a[0] for n, a in d.items()} for d in (w, m, v))
    c = lax.axis_index("c")

    wt_shard, mt_shard, vt_shard = (jnp.transpose(d["w_in"]) for d in (w, m, v))
    g_in, g_uq, g_oa, g_ob, g_out = _gather_weights(
        [wt_shard, w["w_uq"].reshape(Q_RANK // 4, HEADS * QK_DIM), w["w_oa"], w["w_ob"], w["w_out"]])
    wt = g_in.reshape(IN_W, D_MODEL)
    full_uq = g_uq.reshape(Q_RANK, HEADS, QK_DIM)
    full_out = g_out.reshape(D_MODEL, D_MODEL)

    loss, early, st = _local_head(
        x[0], positions[0], loss_target[0], wt, w["b_in"], w["g_q"], full_uq, w["g_kv"], w["w_ukv"], g_oa,
        w["sgu_ln_g"], w["sgu_ln_b"], w["w_s"], w["b_s"], g_ob, full_out, w["ln_g"], w["ln_b"])

    xi, yi = lax.axis_index("x"), lax.axis_index("y")
    c1 = c.reshape(1).astype(jnp.int32)
    idx = jnp.stack([2 * xi + yi, 2 * (1 - xi) + yi, 2 * xi + (1 - yi), 2 * (1 - xi) + (1 - yi), c]).astype(jnp.int32)
    parts1 = [early["w_in"].reshape(N_CHIPS, IN_W // N_CHIPS, D_MODEL), early["w_oa"].astype(BF16),
              early["w_ob"].astype(BF16), early["w_out"].reshape(N_CHIPS, SLAB_W, D_MODEL).astype(BF16)]
    *recv1, all_loss = _exchange_pairs_and_loss(parts1, jnp.broadcast_to(loss, LOSS_TILE))
    pairs1 = [_add_pair_tiled(parts1[0], recv1[0], c1),
              *_add_pair_small(parts1[1:], recv1[1:], c1, "add_pair_early")]
    sems1 = _chips_start(pairs1, "chips_start_early")

    st["delta"] = st["delta"] + sems1[4][0, 0]
    dhl, late = _local_tail(st)

    grads = {**early, **late}
    rep = jnp.concatenate([_rows8(grads[n]) for n in REPLICATED], axis=0)
    rep = jnp.pad(rep, ((0, N_CHIPS * REP_ROWS - rep.shape[0]), (0, 0))).reshape(N_CHIPS, REP_ROWS, D_MODEL)
    parts2 = [late["w_uq"].reshape(N_CHIPS, Q_RANK // N_CHIPS, HEADS * QK_DIM).astype(BF16), rep.astype(BF16),
              late["w_lat"].reshape(N_CHIPS, LAT_ROWS_PAD // N_CHIPS, D_MODEL)]
    pairs2 = _add_pair_small(parts2, _exchange_pairs(parts2, "exchange_pairs_late"), c1, "add_pair_late")
    sems2 = _chips_start(pairs2, "chips_start_late")
    dx = _dx(st["dr"], st["dhg"], st["dhm"], dhl, st["wt"], sems2[4])
    pairs2, landed2 = _chips_wait(*sems2[:4], dx, "chips_wait_late")
    pairs1, landed1 = _chips_wait(*sems1[:4], landed2[0], "chips_wait_early")
    sums = [_sum_chips_tiled(pairs1[0], landed1[0], idx, 128),
            *_sum_chips_small([*pairs1[1:], *pairs2], [*landed1[1:], *landed2], idx, 2)]
    *shards, g_rep, g_lat = _share(sums[:-2], sums[-2:])
    loss = jnp.sum(all_loss[:, 0, 0])

    red = {n: s.reshape(w[n].shape) for n, s in zip(("w_oa", "w_ob", "w_out", "w_uq"), shards[1:])}
    g_rep = g_rep.reshape(N_CHIPS * REP_ROWS, D_MODEL)
    off = 0
    for n in REPLICATED:
        rows = _rows8(w[n]).shape[0]
        red[n] = g_rep[off:off + rows].reshape(-1)[:w[n].size].reshape(w[n].shape)
        off += rows
    lat = g_lat.reshape(LAT_ROWS_PAD, D_MODEL)[:LAT_COLS]
    gt_shard = jnp.where(2 * xi + yi == 0, jnp.concatenate([lat, shards[0][LAT_COLS:]], axis=0), shards[0])

    gt, dt, mt, vt2 = _update_tiled(wt_shard, gt_shard, mt_shard, vt_shard, 232)
    red["w_in"] = jnp.transpose(gt)
    small = [n for n in ORDER if n != "w_in"]
    as2d = lambda a: a.reshape(-1, a.shape[-1])
    ds, ms, vs = _update_small([as2d(w[n]) for n in small], [as2d(red[n]) for n in small],
                               [as2d(m[n]) for n in small], [as2d(v[n]) for n in small])
    delta, new_m, new_v = {"w_in": jnp.transpose(dt)}, {"w_in": jnp.transpose(mt)}, {"w_in": jnp.transpose(vt2)}
    for i, n in enumerate(small):
        delta[n], new_m[n], new_v[n] = (a[i].reshape(w[n].shape) for a in (ds, ms, vs))

    lead = lambda a: a[None]
    return (loss, dx[None], *[lead(red[n]) for n in ORDER], *[lead(delta[n]) for n in ORDER],
            *[lead(new_m[n]) for n in ORDER], *[lead(new_v[n]) for n in ORDER])
```

```python
import functools
import math

import jax
import jax.numpy as jnp
from jax import lax
from jax.experimental import pallas as pl
from jax.experimental.pallas import tpu as pltpu

F32 = jnp.float32
BF16 = jnp.bfloat16

D_MODEL = 1024
HEADS = 8
Q_RANK = 384
KV_RANK = 128
NOPE = 64
ROPE = 32
V_DIM = 64
QK_DIM = NOPE + ROPE
HEAD_PAD = 128
MLA_W = HEADS * V_DIM
SGU_W = 512
GROUPS = 8
CHUNK = 128
IN_W = 4640
RMS_EPS = 1e-6
LN_EPS = 1e-5
ALPHA = 2.0 ** 0.25
ROPE_THETA = 10000.0
SCALE = QK_DIM ** -0.5

GATE_W = 2 * D_MODEL
MID_W = 4 * SGU_W
LAT_W = Q_RANK + KV_RANK + HEAD_PAD
PAD_W = GATE_W + MID_W + LAT_W
LAT_COLS = Q_RANK + KV_RANK + ROPE
ROW_GATE = LAT_COLS + MID_W
LAT_ROWS_PAD = 704
N_SLABS = 4
SLAB_W = D_MODEL // N_SLABS

ROW_TILE = 256
ATT_TQ = 256
ATT_TK = 256
ATT_BWD_TQ = 256
ATT_BWD_TK = 256
LOG2E = 1.4426950408889634
LN2 = 0.6931471805599453
Q_SCALE = SCALE * LOG2E
VMEM_LIMIT = 56 * 1024 * 1024

ADAM_LR = 0.001
ADAM_B1 = 0.9
ADAM_B2 = 0.999
ADAM_EPS = 1e-08
ADAM_WD = 0.01
ADAM_STEP = 10


def _dot(a, b):
    return jnp.dot(a, b, preferred_element_type=F32)


def _dot_nt(a, b):
    return lax.dot_general(a, b, (((1,), (1,)), ((), ())), preferred_element_type=F32)


def _dot_tn(a, b):
    return lax.dot_general(a, b, (((0,), (0,)), ((), ())), preferred_element_type=F32)


def _sigmoid(z):
    return 0.5 * jnp.tanh(0.5 * z) + 0.5


_GELU_C = math.sqrt(2.0 / math.pi)


def _gelu_and_grad(x):
    x2 = x * x
    t = jnp.tanh(_GELU_C * (x + 0.044715 * x * x2))
    g = 0.5 * x * (1.0 + t)
    dg = 0.5 * (1.0 + t) + 0.5 * x * (1.0 - t * t) * (_GELU_C * (1.0 + 3.0 * 0.044715 * x2))
    return g, dg


def _silu_and_grad(z):
    s = _sigmoid(z)
    return z * s, s * (1.0 + z * (1.0 - s))


def _rope(xb, c, sl, sh):
    return xb * c + pltpu.roll(xb, 112, 1) * sl + pltpu.roll(xb, 16, 1) * sh


def _rope_t(dy, c, sl, sh):
    return dy * c + pltpu.roll(dy * sl, 16, 1) + pltpu.roll(dy * sh, 112, 1)


def _params(sem=("arbitrary",)):
    return pltpu.CompilerParams(dimension_semantics=sem, vmem_limit_bytes=VMEM_LIMIT)


def _row_spec(tile, width):
    return pl.BlockSpec((tile, width), lambda i: (i, 0))


def _full_spec(shape):
    nd = len(shape)
    return pl.BlockSpec(shape, lambda i: (0,) * nd)


def _kpe_rows(wt_ref):
    z = lambda n: jnp.zeros((n, D_MODEL), BF16)
    return jnp.concatenate([z(NOPE), wt_ref[Q_RANK + KV_RANK:LAT_COLS, :], z(HEAD_PAD - QK_DIM)], axis=0)


def _fwd_pre(x, wt, b_g, b_m, b_l, g_q, wuq, g_kv, wk, wv, rc, rsl, rsh):
    s = x.shape[0]
    ts = ROW_TILE

    def body(x_ref, wt_ref, bg_ref, bm_ref, bl_ref, gq_ref, wuq_ref, gkv_ref, wk_ref, wv_ref, rc_ref, rsl_ref,
             rsh_ref, hg_ref, hm_ref, hl_ref, q_ref, k_ref, v_ref, xb_ref, qt_ref, kt_ref, vt_ref):
        xb = x_ref[...].astype(BF16)
        xb_ref[...] = xb
        hg_ref[...] = _dot_nt(xb, wt_ref[ROW_GATE:IN_W, :]) + bg_ref[...]
        hm_ref[...] = _dot_nt(xb, wt_ref[LAT_COLS:ROW_GATE, :]) + bm_ref[...]
        hl = jnp.concatenate([_dot_nt(xb, wt_ref[0:Q_RANK + KV_RANK, :]), _dot_nt(xb, _kpe_rows(wt_ref))],
                             axis=1) + bl_ref[...]
        hl_ref[...] = hl
        c, sl, sh = rc_ref[...], rsl_ref[...], rsh_ref[...]
        cq = hl[:, :Q_RANK]
        cqn = cq * lax.rsqrt(jnp.mean(cq * cq, axis=-1, keepdims=True) + RMS_EPS) * gq_ref[...]
        q = _dot(cqn.astype(BF16), wuq_ref[...])
        ckv = hl[:, Q_RANK:Q_RANK + KV_RANK]
        ckvn = (ckv * lax.rsqrt(jnp.mean(ckv * ckv, axis=-1, keepdims=True) + RMS_EPS) * gkv_ref[...]).astype(BF16)
        k = _dot(ckvn, wk_ref[...])
        vb = _dot(ckvn, wv_ref[...]).astype(BF16)
        v_ref[...] = vb
        vt_ref[...] = vb.T
        kpe = _rope(hl[:, Q_RANK + KV_RANK:], c, sl, sh)
        for hd in range(HEADS):
            lanes = slice(hd * HEAD_PAD, (hd + 1) * HEAD_PAD)
            qb = (_rope(q[:, lanes], c, sl, sh) * Q_SCALE).astype(BF16)
            kb = (k[:, lanes] + kpe).astype(BF16)
            q_ref[:, lanes] = qb
            k_ref[:, lanes] = kb
            qt_ref[lanes, :] = qb.T
            kt_ref[lanes, :] = kb.T

    qk_w = HEADS * HEAD_PAD
    col_spec = lambda rows: pl.BlockSpec((rows, ts), lambda i: (0, i))
    return pl.pallas_call(
        body, name="fwd_pre", grid=(s // ts,),
        in_specs=[_row_spec(ts, D_MODEL), _full_spec(wt.shape), _full_spec(b_g.shape), _full_spec(b_m.shape),
                  _full_spec(b_l.shape), _full_spec(g_q.shape),
                  _full_spec(wuq.shape), _full_spec(g_kv.shape), _full_spec(wk.shape), _full_spec(wv.shape),
                  _row_spec(ts, HEAD_PAD), _row_spec(ts, HEAD_PAD), _row_spec(ts, HEAD_PAD)],
        out_specs=[_row_spec(ts, GATE_W), _row_spec(ts, MID_W), _row_spec(ts, LAT_W), _row_spec(ts, qk_w),
                   _row_spec(ts, qk_w), _row_spec(ts, MLA_W), _row_spec(ts, D_MODEL), col_spec(qk_w), col_spec(qk_w),
                   col_spec(MLA_W)],
        out_shape=[jax.ShapeDtypeStruct((s, GATE_W), F32), jax.ShapeDtypeStruct((s, MID_W), F32),
                   jax.ShapeDtypeStruct((s, LAT_W), F32), jax.ShapeDtypeStruct((s, qk_w), BF16),
                   jax.ShapeDtypeStruct((s, qk_w), BF16), jax.ShapeDtypeStruct((s, MLA_W), BF16),
                   jax.ShapeDtypeStruct((s, D_MODEL), BF16), jax.ShapeDtypeStruct((qk_w, s), BF16),
                   jax.ShapeDtypeStruct((qk_w, s), BF16), jax.ShapeDtypeStruct((MLA_W, s), BF16)],
        compiler_params=_params(),
    )(x, wt, b_g, b_m, b_l, g_q, wuq, g_kv, wk, wv, rc, rsl, rsh)


def _attn_fwd(qt, k, vt):
    s = k.shape[0]
    tq, tk = ATT_TQ, ATT_TK
    r = tq // tk
    pairs = HEADS // 2

    def body(qt_ref, k_ref, vt_ref, o_ref, lse_ref):
        i = pl.program_id(1)
        krow = lax.broadcasted_iota(jnp.int32, (tk, tq), 0)
        qcol = lax.broadcasted_iota(jnp.int32, (tk, tq), 1)
        qts = [qt_ref[hh * HEAD_PAD:(hh + 1) * HEAD_PAD, :] for hh in range(2)]

        def scores(j):
            koff = pl.multiple_of(j * tk, tk)
            return tuple(_dot(k_ref[pl.ds(koff, tk), hh * HEAD_PAD:(hh + 1) * HEAD_PAD], qts[hh]) for hh in range(2))

        def weighted(j, ps):
            koff = pl.multiple_of(j * tk, tk)
            return tuple(_dot(vt_ref[hh * V_DIM:(hh + 1) * V_DIM, pl.ds(koff, tk)], ps[hh]) for hh in range(2))

        def step(j, carry, diag, last):
            st, ps, stats = carry
            st_next = None if last else scores(j + 1)
            pvs = weighted(jnp.maximum(j - 1, 0), ps)
            new_ps, new_stats = [], []
            for hh in range(2):
                m, l, acc = stats[hh]
                s_ = st[hh]
                if diag is not None:
                    s_ = jnp.where(krow + diag * tk <= qcol, s_, -jnp.inf)
                m_new = jnp.maximum(m, jnp.max(s_, axis=0, keepdims=True))
                a = jnp.exp2(m - m_new)
                p = jnp.exp2(s_ - m_new)
                new_stats.append((m_new, a * l + jnp.sum(p, axis=0, keepdims=True), a * (acc + pvs[hh])))
                new_ps.append(p.astype(BF16))
            return st_next, tuple(new_ps), tuple(new_stats)

        one = (jnp.full((1, tq), -jnp.inf, F32), jnp.zeros((1, tq), F32), jnp.zeros((V_DIM, tq), F32))
        zero_p = jnp.zeros((tk, tq), BF16)
        nfull = i * r
        carry = lax.fori_loop(0, nfull, functools.partial(step, diag=None, last=False),
                              (scores(0), (zero_p, zero_p), (one, one)))
        for d in range(r):
            carry = step(nfull + d, carry, d, d == r - 1)
        _, ps, stats = carry
        pvs = weighted(nfull + r - 1, ps)
        ot = jnp.concatenate([(stats[hh][2] + pvs[hh]) / stats[hh][1] for hh in range(2)], axis=0)
        o_ref[...] = ot.T
        lse = [stats[hh][0] + jnp.log(stats[hh][1]) * LOG2E for hh in range(2)]
        lse_ref[...] = jnp.concatenate(lse + [jnp.zeros((6, tq), F32)], axis=0)

    return pl.pallas_call(
        body, name="attn_fwd", grid=(pairs, s // tq),
        in_specs=[pl.BlockSpec((2 * HEAD_PAD, tq), lambda p, i: (p, i)),
                  pl.BlockSpec((s, 2 * HEAD_PAD), lambda p, i: (0, p)),
                  pl.BlockSpec((2 * V_DIM, s), lambda p, i: (p, 0))],
        out_specs=[pl.BlockSpec((tq, 2 * V_DIM), lambda p, i: (i, p)),
                   pl.BlockSpec((None, 8, tq), lambda p, i: (p, 0, i))],
        out_shape=[jax.ShapeDtypeStruct((s, MLA_W), F32), jax.ShapeDtypeStruct((pairs, 8, s), F32)],
        compiler_params=_params(("arbitrary", "arbitrary")),
    )(qt, k, vt)


def _attn_bwd(q, qt, k, kt, v, do, dot, lse, delta):
    s = k.shape[0]
    tq, tk = ATT_BWD_TQ, ATT_BWD_TK
    r = tq // tk
    nq = s // tq
    nk = s // tk
    pairs = HEADS // 2

    def body(q_ref, qt_ref, k_ref, kt_ref, v_ref, do_ref, dot_ref, lse_ref, dl_ref, dqt_ref, dk_ref, dv_ref):
        j = pl.program_id(1)
        krow = lax.broadcasted_iota(jnp.int32, (tk, tq), 0)
        qcol = lax.broadcasted_iota(jnp.int32, (tk, tq), 1)
        lane = lax.broadcasted_iota(jnp.int32, (tk, 2 * V_DIM), 1)
        drow = lax.broadcasted_iota(jnp.int32, (2 * V_DIM, tq), 0)

        @pl.when(j == 0)
        def _():
            dqt_ref[...] = jnp.zeros_like(dqt_ref)

        koff = pl.multiple_of(j * tk, tk)
        vb = v_ref[pl.ds(koff, tk), :]
        kbs = [k_ref[pl.ds(koff, tk), hh * HEAD_PAD:(hh + 1) * HEAD_PAD] for hh in range(2)]
        ktbs = [kt_ref[hh * HEAD_PAD:(hh + 1) * HEAD_PAD, pl.ds(koff, tk)] for hh in range(2)]
        i0 = j // r

        def front(i):
            qoff = pl.multiple_of(i * tq, tq)
            dotb = dot_ref[:, pl.ds(qoff, tq)]
            out = []
            for hh in range(2):
                mine = (drow < V_DIM) if hh == 0 else (drow >= V_DIM)
                st = _dot(kbs[hh], qt_ref[hh * HEAD_PAD:(hh + 1) * HEAD_PAD, pl.ds(qoff, tq)])
                out.append((st, _dot(vb, jnp.where(mine, dotb, jnp.zeros_like(dotb)))))
            return tuple(out)

        def middle(i, tiles, diag):
            qoff = pl.multiple_of(i * tq, tq)
            out = []
            for hh in range(2):
                st, dpt = tiles[hh]
                if diag:
                    st = jnp.where(krow + (j - i0 * r) * tk <= qcol, st, -jnp.inf)
                p = jnp.exp2(st - lse_ref[hh:hh + 1, pl.ds(qoff, tq)])
                out.append((p.astype(BF16), (p * (dpt - dl_ref[hh:hh + 1, pl.ds(qoff, tq)])).astype(BF16)))
            return tuple(out)

        def back(i, pd, accs):
            qoff = pl.multiple_of(i * tq, tq)
            dob = do_ref[pl.ds(qoff, tq), :]
            out = []
            for hh in range(2):
                rows = slice(hh * HEAD_PAD, (hh + 1) * HEAD_PAD)
                p, dst = pd[hh]
                dk_acc, dv_acc = accs[hh]
                dv_acc = dv_acc + _dot(p, dob)
                dk_acc = dk_acc + _dot(dst, q_ref[pl.ds(qoff, tq), rows])
                dqt_ref[rows, pl.ds(qoff, tq)] += _dot(ktbs[hh], dst)
                out.append((dk_acc, dv_acc))
            return tuple(out)

        def step(i, accs, diag):
            return back(i, middle(i, front(i), diag), accs)

        zero_acc = (jnp.zeros((tk, HEAD_PAD), F32), jnp.zeros((tk, 2 * V_DIM), F32))
        accs = step(i0, (zero_acc, zero_acc), True)
        accs = lax.fori_loop(i0 + 1, nq, functools.partial(step, diag=False), accs)
        for hh in range(2):
            dk_ref[:, hh * HEAD_PAD:(hh + 1) * HEAD_PAD] = accs[hh][0] * LN2
        dv_ref[...] = jnp.where(lane < V_DIM, accs[0][1], accs[1][1])

        @pl.when(j == nk - 1)
        def _():
            dqt_ref[...] = dqt_ref[...] * SCALE

    pair_rows = lambda w: pl.BlockSpec((s, w), lambda p, j: (0, p))
    pair_cols = lambda w: pl.BlockSpec((w, s), lambda p, j: (p, 0))
    stats = pl.BlockSpec((None, 8, s), lambda p, j: (p, 0, 0))
    return pl.pallas_call(
        body, name="attn_bwd", grid=(pairs, nk),
        in_specs=[pair_rows(2 * HEAD_PAD), pair_cols(2 * HEAD_PAD), pair_rows(2 * HEAD_PAD), pair_cols(2 * HEAD_PAD),
                  pair_rows(2 * V_DIM), pair_rows(2 * V_DIM), pair_cols(2 * V_DIM), stats, stats],
        out_specs=[pair_cols(2 * HEAD_PAD),
                   pl.BlockSpec((tk, 2 * HEAD_PAD), lambda p, j: (j, p)),
                   pl.BlockSpec((tk, 2 * V_DIM), lambda p, j: (j, p))],
        out_shape=[jax.ShapeDtypeStruct((HEADS * HEAD_PAD, s), F32), jax.ShapeDtypeStruct((s, HEADS * HEAD_PAD), F32),
                   jax.ShapeDtypeStruct((s, MLA_W), F32)],
        compiler_params=_params(("arbitrary", "arbitrary")),
    )(q, qt, k, kt, v, do, dot, lse, delta)


def _split3(a):
    hi = a.astype(BF16)
    r1 = a - hi.astype(F32)
    mid = r1.astype(BF16)
    lo = (r1 - mid.astype(F32)).astype(BF16)
    return hi, mid, lo


def _mid(x, tgt, o, hm, hg, woa, wob, wout, ln_g, ln_b, sg_g, sg_b, w_s, bsb):
    s = x.shape[0]
    ts = ROW_TILE
    nsteps = s // ts
    nch = ts // CHUNK
    npair = GROUPS // 2

    def body(x_ref, t_ref, o_ref, hm_ref, hg_ref, woa_ref, wob_ref, wout_ref, lng_ref, lnb_ref, sgg_ref, sgb_ref,
             ws_ref, bsb_ref,
             dr_ref, dhg_ref, dhm_ref, do_ref, dot_ref, dl_ref, dhgt_ref, dhmt_ref,
             dwout_ref, dwoa_ref, dwob_ref, dws_ref, dbs_ref, dlng_ref, dlnb_ref, dsgg_ref, dsgb_ref, loss_ref,
             dbg_ref, dbm_ref, dbacc_ref):
        i = pl.program_id(0)

        @pl.when(i == 0)
        def _():
            for r in (dwout_ref, dwoa_ref, dwob_ref, dws_ref, dlng_ref, dlnb_ref, dsgg_ref, dsgb_ref, loss_ref,
                      dbg_ref, dbm_ref, dbacc_ref):
                r[...] = jnp.zeros_like(r)

        def emit(ref, tref, bref, lo, val):
            vb = val.astype(BF16)
            n = val.shape[1]
            ref[:, lo:lo + n] = vb
            tref[lo:lo + n, :] = vb.T
            bref[:, lo:lo + n] += jnp.sum(val, axis=0, keepdims=True)

        lane = lax.broadcasted_iota(jnp.int32, (CHUNK, CHUNK), 1)
        left = lane < V_DIM
        tril = lax.broadcasted_iota(jnp.int32, (CHUNK, CHUNK), 0) >= lane
        ms = [jnp.where(tril, ws_ref[g], 0.0).astype(BF16) for g in range(GROUPS)]

        z_a = hm_ref[:, 0:SGU_W]
        u = hm_ref[:, SGU_W:2 * SGU_W]
        v = hm_ref[:, 2 * SGU_W:3 * SGU_W]
        z_b = hm_ref[:, 3 * SGU_W:4 * SGU_W]
        o = o_ref[...]
        sa, dsa = _silu_and_grad(z_a)
        y_a = (o * sa).astype(BF16)
        gu, dgu = _gelu_and_grad(u)
        gv, dgv = _gelu_and_grad(v)
        mu = jnp.mean(gv, axis=-1, keepdims=True)
        vc = gv - mu
        rstd_v = lax.rsqrt(jnp.mean(vc * vc, axis=-1, keepdims=True) + LN_EPS)
        vhat = vc * rstd_v
        vn = (vhat * sgg_ref[...] + sgb_ref[...]).astype(BF16)
        rows = []
        for c in range(nch):
            blocks = []
            for p in range(npair):
                blk = vn[c * CHUNK:(c + 1) * CHUNK, p * CHUNK:(p + 1) * CHUNK]
                blocks.append(jnp.where(left, _dot(ms[2 * p], blk), _dot(ms[2 * p + 1], blk)))
            rows.append(jnp.concatenate(blocks, axis=1) + bsb_ref[...])
        mixed = jnp.concatenate(rows, axis=0)
        sgu = gu * mixed
        sb, dsb = _silu_and_grad(z_b)
        y_b = (sgu * sb).astype(BF16)
        pa = jnp.concatenate([_dot(y_a, woa_ref[k]) for k in range(N_SLABS)], axis=1)
        pb = jnp.concatenate([_dot(y_b, wob_ref[k]) for k in range(N_SLABS)], axis=1)
        sga = _sigmoid(hg_ref[:, :D_MODEL])
        sgb = _sigmoid(hg_ref[:, D_MODEL:])
        m2 = (sga * pa + sgb * pb).astype(BF16)
        r = ALPHA * x_ref[...] + _dot(m2, wout_ref[...])
        rmu = jnp.mean(r, axis=-1, keepdims=True)
        rc = r - rmu
        rstd = lax.rsqrt(jnp.mean(rc * rc, axis=-1, keepdims=True) + LN_EPS)
        xhat = rc * rstd
        y = xhat * lng_ref[...] + lnb_ref[...]
        err = y - t_ref[...]
        loss_ref[...] += jnp.full(loss_ref.shape, 0.5 / D_MODEL, F32) * jnp.sum(err * err)

        dy = err * (1.0 / D_MODEL)
        dlng_ref[...] += jnp.sum(dy * xhat, axis=0, keepdims=True)
        dlnb_ref[...] += jnp.sum(dy, axis=0, keepdims=True)
        dxh = dy * lng_ref[...]
        dr = rstd * (dxh - jnp.mean(dxh, axis=-1, keepdims=True) - xhat * jnp.mean(dxh * xhat, axis=-1, keepdims=True))
        dr_ref[...] = dr
        drb = dr.astype(BF16)
        dwout_ref[...] += _dot_tn(m2, drb)
        dm2 = _dot_nt(drb, wout_ref[...])
        emit(dhg_ref, dhgt_ref, dbg_ref, 0, dm2 * pa * sga * (1.0 - sga))
        emit(dhg_ref, dhgt_ref, dbg_ref, D_MODEL, dm2 * pb * sgb * (1.0 - sgb))
        dpa = (dm2 * sga).astype(BF16)
        dpb = (dm2 * sgb).astype(BF16)
        dy_a = jnp.zeros((ts, MLA_W), F32)
        dy_b = jnp.zeros((ts, SGU_W), F32)
        for k in range(N_SLABS):
            cols = slice(k * SLAB_W, (k + 1) * SLAB_W)
            dwoa_ref[k] += _dot_tn(y_a, dpa[:, cols])
            dwob_ref[k] += _dot_tn(y_b, dpb[:, cols])
            dy_a = dy_a + _dot_nt(dpa[:, cols], woa_ref[k])
            dy_b = dy_b + _dot_nt(dpb[:, cols], wob_ref[k])
        dob = (dy_a * sa).astype(BF16)
        do_ref[...] = dob
        dot_ref[...] = dob.T
        head = (lax.broadcasted_iota(jnp.int32, (HEADS, MLA_W), 1) // V_DIM
                == lax.broadcasted_iota(jnp.int32, (HEADS, MLA_W), 0)).astype(BF16)
        dl_ref[...] = sum(_dot_nt(head, term) for term in _split3(dob.astype(F32) * o))
        emit(dhm_ref, dhmt_ref, dbm_ref, 0, dy_a * o * dsa)
        dsg = dy_b * sb
        emit(dhm_ref, dhmt_ref, dbm_ref, 3 * SGU_W, dy_b * sgu * dsb)
        emit(dhm_ref, dhmt_ref, dbm_ref, SGU_W, dsg * mixed * dgu)
        dmixed = dsg * gu
        dvn_rows = []
        dbs_sum = jnp.zeros((CHUNK, SGU_W), F32)
        for c in range(nch):
            dm_c = dmixed[c * CHUNK:(c + 1) * CHUNK, :]
            dbs_sum = dbs_sum + dm_c
            blocks = []
            for p in range(npair):
                dmb = dm_c[:, p * CHUNK:(p + 1) * CHUNK].astype(BF16)
                blk = vn[c * CHUNK:(c + 1) * CHUNK, p * CHUNK:(p + 1) * CHUNK]
                blocks.append(jnp.where(left, _dot_tn(ms[2 * p], dmb), _dot_tn(ms[2 * p + 1], dmb)))
                zero = jnp.zeros_like(dmb)
                dws_ref[2 * p] += jnp.where(tril, _dot_nt(jnp.where(left, dmb, zero), blk), 0.0)
                dws_ref[2 * p + 1] += jnp.where(tril, _dot_nt(jnp.where(left, zero, dmb), blk), 0.0)
            dvn_rows.append(jnp.concatenate(blocks, axis=1))
        dbacc_ref[...] += dbs_sum
        dvn = jnp.concatenate(dvn_rows, axis=0)
        dsgg_ref[...] += jnp.sum(dvn * vhat, axis=0, keepdims=True)
        dsgb_ref[...] += jnp.sum(dvn, axis=0, keepdims=True)
        dvh = dvn * sgg_ref[...]
        dgv_in = rstd_v * (dvh - jnp.mean(dvh, axis=-1, keepdims=True)
                           - vhat * jnp.mean(dvh * vhat, axis=-1, keepdims=True))
        emit(dhm_ref, dhmt_ref, dbm_ref, 2 * SGU_W, dgv_in * dgv)

        @pl.when(i == nsteps - 1)
        def _():
            grp = (lax.broadcasted_iota(jnp.int32, (SGU_W, CHUNK), 0) // V_DIM
                   == lax.broadcasted_iota(jnp.int32, (SGU_W, CHUNK), 1)).astype(BF16)
            hi, mid, lo = _split3(dbacc_ref[...])
            dbs_ref[...] = _dot(hi, grp) + _dot(mid, grp) + _dot(lo, grp)

    acc_shapes = [(D_MODEL, D_MODEL), woa.shape, wob.shape, (GROUPS, CHUNK, CHUNK), (CHUNK, CHUNK),
                  (1, D_MODEL), (1, D_MODEL), (1, SGU_W), (1, SGU_W), (1, 128), (1, GATE_W), (1, MID_W)]
    col_spec = lambda rows: pl.BlockSpec((rows, ts), lambda i: (0, i))
    return pl.pallas_call(
        body, name="mid", grid=(nsteps,),
        in_specs=[_row_spec(ts, D_MODEL), _row_spec(ts, D_MODEL), _row_spec(ts, MLA_W), _row_spec(ts, MID_W),
                  _row_spec(ts, GATE_W), _full_spec(woa.shape), _full_spec(wob.shape), _full_spec(wout.shape),
                  _full_spec(ln_g.shape), _full_spec(ln_b.shape), _full_spec(sg_g.shape), _full_spec(sg_b.shape),
                  _full_spec(w_s.shape), _full_spec(bsb.shape)],
        out_specs=[_row_spec(ts, D_MODEL), _row_spec(ts, GATE_W), _row_spec(ts, MID_W), _row_spec(ts, MLA_W),
                   col_spec(MLA_W), col_spec(HEADS), col_spec(GATE_W), col_spec(MID_W)]
        + [_full_spec(sh) for sh in acc_shapes],
        out_shape=[jax.ShapeDtypeStruct((s, D_MODEL), F32), jax.ShapeDtypeStruct((s, GATE_W), BF16),
                   jax.ShapeDtypeStruct((s, MID_W), BF16), jax.ShapeDtypeStruct((s, MLA_W), BF16),
                   jax.ShapeDtypeStruct((MLA_W, s), BF16), jax.ShapeDtypeStruct((HEADS, s), F32),
                   jax.ShapeDtypeStruct((GATE_W, s), BF16), jax.ShapeDtypeStruct((MID_W, s), BF16)]
        + [jax.ShapeDtypeStruct(sh, F32) for sh in acc_shapes],
        scratch_shapes=[pltpu.VMEM((CHUNK, SGU_W), F32)],
        compiler_params=_params(),
    )(x, tgt, o, hm, hg, woa, wob, wout, ln_g, ln_b, sg_g, sg_b, w_s, bsb)


def _lat_bwd(dq, dk, dv, hl, rc, rsl, rsh, g_q, g_kv, wuq, wk, wv):
    s = dk.shape[0]
    ts = ROW_TILE
    qk_w = HEADS * HEAD_PAD

    def body(dq_ref, dk_ref, dv_ref, hl_ref, rc_ref, rsl_ref, rsh_ref, gq_ref, gkv_ref, wuq_ref, wk_ref, wv_ref,
             dhl_ref, dhlt_ref, dwuq_ref, dwk_ref, dwv_ref, dgq_ref, dgkv_ref, dbl_ref):
        i = pl.program_id(0)

        @pl.when(i == 0)
        def _():
            for r in (dwuq_ref, dwk_ref, dwv_ref, dgq_ref, dgkv_ref, dbl_ref):
                r[...] = jnp.zeros_like(r)

        def emit(lo, val):
            vb = val.astype(BF16)
            n = val.shape[1]
            dhl_ref[:, lo:lo + n] = vb
            dhlt_ref[lo:lo + n, :] = vb.T
            dbl_ref[:, lo:lo + n] += jnp.sum(val, axis=0, keepdims=True)

        c, sl, sh = rc_ref[...], rsl_ref[...], rsh_ref[...]
        lane = lax.broadcasted_iota(jnp.int32, (ts, HEAD_PAD), 1)
        pe = (lane >= NOPE) & (lane < QK_DIM)
        dkpe = jnp.zeros((ts, HEAD_PAD), F32)
        dqu = []
        for hd in range(HEADS):
            lanes = slice(hd * HEAD_PAD, (hd + 1) * HEAD_PAD)
            dqu.append(_rope_t(dq_ref[lanes, :].T, c, sl, sh).astype(BF16))
            dkpe = dkpe + dk_ref[:, lanes]
        dqu = jnp.concatenate(dqu, axis=1)
        dkpe = _rope_t(jnp.where(pe, dkpe, 0.0), c, sl, sh)

        cq = hl_ref[:, :Q_RANK]
        rq = lax.rsqrt(jnp.mean(cq * cq, axis=-1, keepdims=True) + RMS_EPS)
        cqh = cq * rq
        cqn = (cqh * gq_ref[...]).astype(BF16)
        dwuq_ref[...] += _dot_tn(cqn, dqu)
        dcqn = _dot_nt(dqu, wuq_ref[...])
        dgq_ref[...] += jnp.sum(dcqn * cqh, axis=0, keepdims=True)
        dch = dcqn * gq_ref[...]
        emit(0, rq * (dch - cqh * jnp.mean(dch * cqh, axis=-1, keepdims=True)))

        ckv = hl_ref[:, Q_RANK:Q_RANK + KV_RANK]
        rk = lax.rsqrt(jnp.mean(ckv * ckv, axis=-1, keepdims=True) + RMS_EPS)
        ckh = ckv * rk
        ckn = (ckh * gkv_ref[...]).astype(BF16)
        dkb = dk_ref[...].astype(BF16)
        dvb = dv_ref[...].astype(BF16)
        dwk_ref[...] += _dot_tn(ckn, dkb)
        dwv_ref[...] += _dot_tn(ckn, dvb)
        dckn = _dot_nt(dkb, wk_ref[...]) + _dot_nt(dvb, wv_ref[...])
        dgkv_ref[...] += jnp.sum(dckn * ckh, axis=0, keepdims=True)
        dkh = dckn * gkv_ref[...]
        emit(Q_RANK, rk * (dkh - ckh * jnp.mean(dkh * ckh, axis=-1, keepdims=True)))
        emit(Q_RANK + KV_RANK, dkpe)

    acc_shapes = [wuq.shape, wk.shape, wv.shape, g_q.shape, g_kv.shape, (1, LAT_W)]
    return pl.pallas_call(
        body, name="lat_bwd", grid=(s // ts,),
        in_specs=[pl.BlockSpec((qk_w, ts), lambda i: (0, i)), _row_spec(ts, qk_w), _row_spec(ts, MLA_W),
                  _row_spec(ts, LAT_W), _row_spec(ts, HEAD_PAD), _row_spec(ts, HEAD_PAD), _row_spec(ts, HEAD_PAD),
                  _full_spec(g_q.shape), _full_spec(g_kv.shape), _full_spec(wuq.shape), _full_spec(wk.shape),
                  _full_spec(wv.shape)],
        out_specs=[_row_spec(ts, LAT_W), pl.BlockSpec((LAT_W, ts), lambda i: (0, i))]
        + [_full_spec(sh) for sh in acc_shapes],
        out_shape=[jax.ShapeDtypeStruct((s, LAT_W), BF16), jax.ShapeDtypeStruct((LAT_W, s), BF16)]
        + [jax.ShapeDtypeStruct(sh, F32) for sh in acc_shapes],
        compiler_params=_params(),
    )(dq, dk, dv, hl, rc, rsl, rsh, g_q, g_kv, wuq, wk, wv)


def _dx(dr, dhg, dhm, dhl, wt, after):
    s = dr.shape[0]
    ts = ROW_TILE

    def body(dr_ref, dhg_ref, dhm_ref, dhl_ref, wt_ref, after_ref, dx_ref):
        dx_ref[...] = (ALPHA * dr_ref[...]
                       + _dot(dhg_ref[...], wt_ref[ROW_GATE:IN_W, :])
                       + _dot(dhm_ref[...], wt_ref[LAT_COLS:ROW_GATE, :])
                       + _dot(dhl_ref[:, 0:Q_RANK + KV_RANK], wt_ref[0:Q_RANK + KV_RANK, :])
                       + _dot(dhl_ref[:, Q_RANK + KV_RANK:], _kpe_rows(wt_ref)))

    return pl.pallas_call(
        body, name="dx", grid=(s // ts,),
        in_specs=[_row_spec(ts, D_MODEL), _row_spec(ts, GATE_W), _row_spec(ts, MID_W), _row_spec(ts, LAT_W),
                  _full_spec(wt.shape), pl.BlockSpec(memory_space=pl.ANY)],
        out_specs=_row_spec(ts, D_MODEL),
        out_shape=jax.ShapeDtypeStruct((s, D_MODEL), F32),
        compiler_params=_params(),
    )(dr, dhg, dhm, dhl, wt, after)


def _dwt_early(dhmt, dhgt, xb):
    tn = 512
    nm, ng = MID_W // tn, GATE_W // tn
    s = dhmt.shape[1]

    def body(dm_ref, dg_ref, xb_ref, dw_ref):
        i = pl.program_id(0)

        @pl.when(i < nm)
        def _():
            dw_ref[...] = _dot(dm_ref[...], xb_ref[...]).astype(BF16)

        @pl.when(i >= nm)
        def _():
            dw_ref[...] = _dot(dg_ref[...], xb_ref[...]).astype(BF16)

    rows = pl.pallas_call(
        body, name="dwt_early", grid=(nm + ng,),
        in_specs=[pl.BlockSpec((tn, s), lambda i: (jnp.minimum(i, nm - 1), 0)),
                  pl.BlockSpec((tn, s), lambda i: (jnp.maximum(i - nm, 0), 0)), _full_spec(xb.shape)],
        out_specs=pl.BlockSpec((pl.Element(tn), pl.Element(D_MODEL)), lambda i: (pl.multiple_of(LAT_COLS + i * tn, 32), 0)),
        out_shape=jax.ShapeDtypeStruct((IN_W, D_MODEL), BF16),
        compiler_params=_params(),
    )(dhmt, dhgt, xb)

    def zero(buf_ref, out_ref):
        out_ref[...] = jnp.zeros_like(out_ref)

    return pl.pallas_call(
        zero, name="dwt_zero_lat", grid=(1,), in_specs=[pl.BlockSpec(memory_space=pl.ANY)],
        out_specs=pl.BlockSpec((LAT_COLS, D_MODEL), lambda i: (0, 0)),
        out_shape=jax.ShapeDtypeStruct((IN_W, D_MODEL), BF16), input_output_aliases={0: 0},
    )(rows)


def _dwt_lat(dhlt, xb):
    n, s = dhlt.shape

    def body(dht_ref, xb_ref, dw_ref):
        dw = _dot(dht_ref[...], xb_ref[...]).astype(BF16)
        kpe = Q_RANK + KV_RANK + NOPE
        dw_ref[0:Q_RANK + KV_RANK, :] = dw[0:Q_RANK + KV_RANK]
        dw_ref[Q_RANK + KV_RANK:LAT_COLS, :] = dw[kpe:kpe + ROPE]
        dw_ref[LAT_COLS:, :] = jnp.zeros((LAT_ROWS_PAD - LAT_COLS, D_MODEL), BF16)

    return pl.pallas_call(
        body, name="dwt_lat", in_specs=[VMEM_SPEC, VMEM_SPEC], out_specs=VMEM_SPEC,
        out_shape=jax.ShapeDtypeStruct((LAT_ROWS_PAD, D_MODEL), BF16),
        compiler_params=pltpu.CompilerParams(vmem_limit_bytes=VMEM_LIMIT),
    )(dhlt, xb)


def _split_bias(b):
    z = lambda n: jnp.zeros((n,), b.dtype)
    lat = jnp.concatenate([b[:Q_RANK + KV_RANK], z(NOPE), b[Q_RANK + KV_RANK:LAT_COLS], z(HEAD_PAD - QK_DIM)])
    return b[None, ROW_GATE:], b[None, LAT_COLS:ROW_GATE], lat[None, :]


def _join_bias(g, m, l):
    kpe = Q_RANK + KV_RANK + NOPE
    return jnp.concatenate([l[0, :Q_RANK + KV_RANK], l[0, kpe:kpe + ROPE], m[0], g[0]])


def _rope_tables(positions):
    half = ROPE // 2
    inv_freq = ROPE_THETA ** (-jnp.arange(0, ROPE, 2, dtype=F32) / ROPE)
    ang = positions.astype(F32)[:, None] * inv_freq
    cos, sin = jnp.cos(ang), jnp.sin(ang)
    n = positions.shape[0]
    one, zero = jnp.ones((n, NOPE), F32), jnp.zeros((n, half), F32)
    tail1, tail0 = jnp.ones((n, HEAD_PAD - QK_DIM), F32), jnp.zeros((n, HEAD_PAD - QK_DIM), F32)
    z64 = jnp.zeros((n, NOPE), F32)
    rc = jnp.concatenate([one, cos, cos, tail1], axis=1)
    rsl = jnp.concatenate([z64, -sin, zero, tail0], axis=1)
    rsh = jnp.concatenate([z64, zero, sin, tail0], axis=1)
    return rc, rsl, rsh


def _local_head(x, positions, tgt, wt, b_in, g_q, w_uq, g_kv, w_ukv, w_oa, sg_g, sg_b, w_s, b_s, w_ob, w_out,
                ln_g, ln_b):
    rc, rsl, rsh = _rope_tables(positions)
    b_g, b_m, b_l = _split_bias(b_in)
    wuq = jnp.pad(w_uq, ((0, 0), (0, 0), (0, HEAD_PAD - QK_DIM))).reshape(Q_RANK, HEADS * HEAD_PAD).astype(BF16)
    wk = jnp.pad(w_ukv[:, :, :NOPE], ((0, 0), (0, 0), (0, HEAD_PAD - NOPE))).reshape(KV_RANK, HEADS * HEAD_PAD).astype(BF16)
    wv = w_ukv[:, :, NOPE:].reshape(KV_RANK, MLA_W).astype(BF16)
    bsb = jnp.repeat(b_s.T, V_DIM, axis=1)
    gq2, gkv2 = g_q[None, :], g_kv[None, :]

    hg, hm, hl, q, k, v, xb, qt, kt, vt = _fwd_pre(x, wt, b_g, b_m, b_l, gq2, wuq, gkv2, wk, wv, rc, rsl, rsh)
    o, lse = _attn_fwd(qt, k, vt)
    (dr, dhg, dhm, do, dot, delta, dhgt, dhmt, dwout, dwoa, dwob, dws, dbs, dlng, dlnb, dsgg, dsgb, loss, dbg,
     dbm) = _mid(x, tgt, o, hm, hg, w_oa, w_ob, w_out, ln_g[None, :], ln_b[None, :], sg_g[None, :], sg_b[None, :],
                 w_s, bsb)
    delta = jnp.pad(delta.reshape(HEADS // 2, 2, -1), ((0, 0), (0, 6), (0, 0)))
    early = {
        "w_in": _dwt_early(dhmt, dhgt, xb),
        "w_oa": dwoa, "sgu_ln_g": dsgg[0], "sgu_ln_b": dsgb[0], "w_s": dws, "b_s": dbs[:, :GROUPS].T,
        "w_ob": dwob, "w_out": dwout, "ln_g": dlng[0], "ln_b": dlnb[0],
    }
    state = dict(q=q, qt=qt, k=k, kt=kt, v=v, do=do, dot=dot, lse=lse, delta=delta, hl=hl, rc=rc, rsl=rsl, rsh=rsh,
                 gq2=gq2, gkv2=gkv2, wuq=wuq, wk=wk, wv=wv, dr=dr, dhg=dhg, dhm=dhm, wt=wt, xb=xb, dbg=dbg, dbm=dbm)
    return loss, early, state


def _local_tail(st):
    dq, dk, dv = _attn_bwd(st["q"], st["qt"], st["k"], st["kt"], st["v"], st["do"], st["dot"], st["lse"], st["delta"])
    dhl, dhlt, dwuq, dwk, dwv, dgq, dgkv, dbl = _lat_bwd(dq, dk, dv, st["hl"], st["rc"], st["rsl"], st["rsh"],
                                                         st["gq2"], st["gkv2"], st["wuq"], st["wk"], st["wv"])
    late = {
        "w_lat": _dwt_lat(dhlt, st["xb"]),
        "b_in": _join_bias(st["dbg"], st["dbm"], dbl),
        "g_q": dgq[0],
        "w_uq": dwuq.reshape(Q_RANK, HEADS, HEAD_PAD)[:, :, :QK_DIM],
        "g_kv": dgkv[0],
        "w_ukv": jnp.concatenate([dwk.reshape(KV_RANK, HEADS, HEAD_PAD)[:, :, :NOPE],
                                  dwv.reshape(KV_RANK, HEADS, V_DIM)], axis=2),
    }
    return dhl, late


def _local_step(*args):
    loss, early, st = _local_head(*args)
    dhl, late = _local_tail(st)
    dx = _dx(st["dr"], st["dhg"], st["dhm"], dhl, st["wt"], dhl)
    grads = {**early, **late}
    grads["w_in"] = jnp.concatenate([grads.pop("w_lat")[:LAT_COLS], early["w_in"][LAT_COLS:]], axis=0)
    return loss, dx, grads


MESH = pl.DeviceIdType.MESH
N_CHIPS = 4
HBM_SPEC = pl.BlockSpec(memory_space=pl.ANY)
HBM_SPEC_STRICT = pl.BlockSpec(memory_space=pltpu.HBM)
VMEM_SPEC = pl.BlockSpec(memory_space=pltpu.VMEM)

REP_ROWS = 80


def _rows8(a):
    flat = a.reshape(-1)
    n = -(-flat.shape[0] // (8 * D_MODEL)) * 8 * D_MODEL
    return jnp.pad(flat, (0, n - flat.shape[0])).reshape(-1, D_MODEL)


def _place():
    x, y, c = lax.axis_index("x"), lax.axis_index("y"), lax.axis_index("c")
    others = [(1 - x, y), (x, 1 - y), (1 - x, 1 - y)]
    return x, y, c, others


def _gather_weights(shards):
    n = len(shards)

    def body(*refs):
        ins, outs, bufs = refs[:n], refs[n:2 * n], refs[2 * n:3 * n]
        send_sems, recv_sems, local_sems = refs[3 * n:]
        x, y, c, others = _place()
        me = 2 * x + y
        sibling = (x, y, 1 - c)
        for src, buf in zip(ins, bufs):
            buf[...] = src[...].astype(BF16)
        own = [pltpu.make_async_copy(bufs[w], outs[w].at[me], local_sems.at[w]) for w in range(n)]
        for cp in own:
            cp.start()

        def part(w, chip, half):
            hc = shards[w].shape[1] // 2
            return outs[w].at[chip, :, pl.ds(half * hc, hc)]

        def sent(w, j):
            hc = shards[w].shape[1] // 2
            return pltpu.make_async_remote_copy(
                src_ref=bufs[w].at[:, pl.ds(c * hc, hc)], dst_ref=part(w, me, c),
                send_sem=send_sems.at[w * 3 + j], recv_sem=recv_sems.at[w * 3 + j],
                device_id=(*others[j], c), device_id_type=MESH)

        def landed(w, j):
            px, py = others[j]
            return pltpu.make_async_remote_copy(
                src_ref=part(w, 2 * px + py, c), dst_ref=part(w, 2 * px + py, c),
                send_sem=send_sems.at[w * 3 + j], recv_sem=recv_sems.at[w * 3 + j],
                device_id=(px, py, c), device_id_type=MESH)

        def passed(w, j, half):
            px, py = others[j]
            k = n * 3 + w * 3 + j
            return pltpu.make_async_remote_copy(
                src_ref=part(w, 2 * px + py, half), dst_ref=part(w, 2 * px + py, half),
                send_sem=send_sems.at[k], recv_sem=recv_sems.at[k], device_id=sibling, device_id_type=MESH)

        first = [sent(w, j) for w in range(n) for j in range(3)]
        for cp in first:
            cp.start()
        fwd = []
        for w in range(n):
            for j in range(3):
                landed(w, j).wait_recv()
                cp = passed(w, j, c)
                cp.start()
                fwd.append(cp)
        for w in range(n):
            for j in range(3):
                passed(w, j, 1 - c).wait_recv()
        for cp in first + fwd:
            cp.wait_send()
        for cp in own:
            cp.wait()

    return pl.pallas_call(
        body, name="gather_weights",
        in_specs=[VMEM_SPEC] * n, out_specs=[HBM_SPEC] * n,
        out_shape=[jax.ShapeDtypeStruct((N_CHIPS,) + s.shape, BF16) for s in shards],
        scratch_shapes=[pltpu.VMEM(s.shape, BF16) for s in shards]
        + [pltpu.SemaphoreType.DMA((6 * n,)), pltpu.SemaphoreType.DMA((6 * n,)), pltpu.SemaphoreType.DMA((n,))],
        compiler_params=pltpu.CompilerParams(vmem_limit_bytes=VMEM_LIMIT),
    )(*shards)


N_DEV = 8
LOSS_TILE = (8, 128)


def _half(a):
    return a.shape[-1] // 2


def _exchange_pairs(parts, name):
    n = len(parts)

    def body(*refs):
        p_refs, r_refs = refs[:n], refs[n:2 * n]
        send_sems, recv_sems = refs[2 * n:]
        x, y, c, _ = _place()
        cps = []
        for w in range(n):
            h = _half(parts[w])
            cps.append(pltpu.make_async_remote_copy(
                src_ref=p_refs[w].at[:, :, pl.ds((1 - c) * h, h)], dst_ref=r_refs[w],
                send_sem=send_sems.at[w], recv_sem=recv_sems.at[w], device_id=(x, y, 1 - c), device_id_type=MESH))
        for cp in cps:
            cp.start()
        for cp in cps:
            cp.wait()

    return pl.pallas_call(
        body, name=name, in_specs=[HBM_SPEC] * n, out_specs=[HBM_SPEC] * n,
        out_shape=[jax.ShapeDtypeStruct((N_CHIPS, p.shape[1], _half(p)), BF16) for p in parts],
        scratch_shapes=[pltpu.SemaphoreType.DMA((n,)), pltpu.SemaphoreType.DMA((n,))],
    )(*parts)


def _exchange_pairs_and_loss(parts, loss):
    n = len(parts)

    def body(*refs):
        p_refs, loss_ref = refs[:n], refs[n]
        r_refs, all_loss_ref = refs[n + 1:2 * n + 1], refs[2 * n + 1]
        send_sems, recv_sems, loss_send, loss_recv, local_sem = refs[2 * n + 2:]
        x, y, c, _ = _place()
        sibling = (x, y, 1 - c)
        cps = []
        for w in range(n):
            h = _half(parts[w])
            cps.append(pltpu.make_async_remote_copy(
                src_ref=p_refs[w].at[:, :, pl.ds((1 - c) * h, h)], dst_ref=r_refs[w],
                send_sem=send_sems.at[w], recv_sem=recv_sems.at[w], device_id=sibling, device_id_type=MESH))
        for cp in cps:
            cp.start()
        me = 4 * x + 2 * y + c
        own = pltpu.make_async_copy(loss_ref, all_loss_ref.at[me], local_sem)
        own.start()
        lcs = []
        for t in range(1, N_DEV):
            d = (me + t) % N_DEV
            lcs.append(pltpu.make_async_remote_copy(
                src_ref=loss_ref, dst_ref=all_loss_ref.at[me], send_sem=loss_send.at[t - 1],
                recv_sem=loss_recv.at[t - 1], device_id=(d // 4, (d // 2) % 2, d % 2), device_id_type=MESH))
        for cp in lcs:
            cp.start()
        for t in range(1, N_DEV):
            d = (me + N_DEV - t) % N_DEV
            pltpu.make_async_remote_copy(
                src_ref=loss_ref, dst_ref=all_loss_ref.at[d], send_sem=loss_send.at[t - 1],
                recv_sem=loss_recv.at[t - 1], device_id=(d // 4, (d // 2) % 2, d % 2), device_id_type=MESH).wait_recv()
        for cp in lcs:
            cp.wait_send()
        for cp in cps:
            cp.wait()
        own.wait()

    return pl.pallas_call(
        body, name="exchange_pairs_and_loss", in_specs=[HBM_SPEC] * (n + 1), out_specs=[HBM_SPEC] * (n + 1),
        out_shape=[jax.ShapeDtypeStruct((N_CHIPS, p.shape[1], _half(p)), BF16) for p in parts]
        + [jax.ShapeDtypeStruct((N_DEV,) + LOSS_TILE, F32)],
        scratch_shapes=[pltpu.SemaphoreType.DMA((n,)), pltpu.SemaphoreType.DMA((n,)),
                        pltpu.SemaphoreType.DMA((N_DEV - 1,)), pltpu.SemaphoreType.DMA((N_DEV - 1,)),
                        pltpu.SemaphoreType.DMA],
    )(*parts, loss)


def _add_pair_tiled(p, r, c):
    rows, h = r.shape[1:]

    def body(c_ref, p_ref, r_ref, q_ref):
        q_ref[...] = (p_ref[...].astype(F32) + r_ref[...].astype(F32)).astype(BF16)

    return pl.pallas_call(
        body, name="add_pair_w_in",
        grid_spec=pltpu.PrefetchScalarGridSpec(
            num_scalar_prefetch=1, grid=(N_CHIPS,),
            in_specs=[pl.BlockSpec((None, rows, h), lambda k, c_ref: (k, 0, c_ref[0])),
                      pl.BlockSpec((None, rows, h), lambda k, c_ref: (k, 0, 0))],
            out_specs=pl.BlockSpec((None, rows, h), lambda k, c_ref: (k, 0, 0))),
        out_shape=jax.ShapeDtypeStruct(r.shape, BF16),
    )(c, p, r)


def _add_pair_small(ps, rs, c, name):
    n = len(ps)

    def body(c_ref, *refs):
        for w in range(n):
            h = _half(ps[w])
            mine = refs[w][:, :, pl.ds(pl.multiple_of(c_ref[0] * h, 128), h)]
            refs[2 * n + w][...] = (mine.astype(F32) + refs[n + w][...].astype(F32)).astype(BF16)

    return pl.pallas_call(
        body, name=name,
        in_specs=[pl.BlockSpec(memory_space=pltpu.SMEM)] + [VMEM_SPEC] * (2 * n), out_specs=[VMEM_SPEC] * n,
        out_shape=[jax.ShapeDtypeStruct(r.shape, BF16) for r in rs],
        compiler_params=pltpu.CompilerParams(vmem_limit_bytes=VMEM_LIMIT),
    )(c, *ps, *rs)


def _exchange_chips(qs):
    n = len(qs)

    def body(*refs):
        q_refs, r_refs = refs[:n], refs[n:2 * n]
        send_sems, recv_sems = refs[2 * n:]
        x, y, c, others = _place()
        me = 2 * x + y
        cps = []
        for w in range(n):
            for j, (px, py) in enumerate(others):
                cps.append(pltpu.make_async_remote_copy(
                    src_ref=q_refs[w].at[2 * px + py], dst_ref=r_refs[w].at[me], send_sem=send_sems.at[3 * w + j],
                    recv_sem=recv_sems.at[3 * w + j], device_id=(px, py, c), device_id_type=MESH))
        for cp in cps:
            cp.start()
        for w in range(n):
            for j, (px, py) in enumerate(others):
                pltpu.make_async_remote_copy(
                    src_ref=q_refs[w].at[me], dst_ref=r_refs[w].at[2 * px + py], send_sem=send_sems.at[3 * w + j],
                    recv_sem=recv_sems.at[3 * w + j], device_id=(px, py, c), device_id_type=MESH).wait_recv()
        for cp in cps:
            cp.wait_send()

    return pl.pallas_call(
        body, name="exchange_chips", in_specs=[HBM_SPEC] * n, out_specs=[HBM_SPEC] * n,
        out_shape=[jax.ShapeDtypeStruct(q.shape, BF16) for q in qs],
        scratch_shapes=[pltpu.SemaphoreType.DMA((3 * n,)), pltpu.SemaphoreType.DMA((3 * n,))],
    )(*qs)


SEM_SPEC = pl.BlockSpec(memory_space=pltpu.SEMAPHORE)
SPLIT_EFFECT = pltpu.SideEffectType.DATAFLOW_SIDE_EFFECTING


def _chips_start(qs, name):
    n = len(qs)

    def body(*refs):
        q_refs, land_refs = refs[:n], refs[n:2 * n]
        send_sems, recv_sems, token = refs[2 * n], refs[2 * n + 1], refs[-1]
        x, y, c, others = _place()
        me = 2 * x + y
        for w in range(n):
            for j, (px, py) in enumerate(others):
                pltpu.make_async_remote_copy(
                    src_ref=q_refs[w].at[2 * px + py], dst_ref=land_refs[w].at[me], send_sem=send_sems.at[3 * w + j],
                    recv_sem=recv_sems.at[3 * w + j], device_id=(px, py, c), device_id_type=MESH).start()
        token[...] = jnp.zeros_like(token)

    hbm = [pltpu.HBM(q.shape, BF16) for q in qs]
    outs = pl.pallas_call(
        body, name=name,
        out_shape=(pltpu.SemaphoreType.DMA((3 * n,)), pltpu.SemaphoreType.DMA((3 * n,)), *hbm, *hbm,
                   jax.ShapeDtypeStruct(LOSS_TILE, F32)),
        in_specs=[HBM_SPEC_STRICT] * (2 * n),
        out_specs=(SEM_SPEC, SEM_SPEC, *[HBM_SPEC_STRICT] * (2 * n), VMEM_SPEC),
        input_output_aliases={i: 2 + i for i in range(2 * n)},
        compiler_params=pltpu.CompilerParams(has_side_effects=SPLIT_EFFECT),
    )(*[pltpu.with_memory_space_constraint(q, pltpu.HBM) for q in qs],
      *[pltpu.with_memory_space_constraint(lax.empty(q.shape, BF16), pltpu.HBM) for q in qs])
    return outs[0], outs[1], outs[2:2 + n], outs[2 + n:2 + 2 * n], outs[-1]


def _chips_wait(send_sems, recv_sems, q_thru, land_thru, after, name):
    n = len(q_thru)

    def body(*refs):
        q_refs, land_refs = refs[:n], refs[n:2 * n]
        send_sems, recv_sems = refs[2 * n], refs[2 * n + 1]
        x, y, c, others = _place()
        me = 2 * x + y
        for w in range(n):
            for j, (px, py) in enumerate(others):
                cp = pltpu.make_async_remote_copy(
                    src_ref=q_refs[w].at[2 * px + py], dst_ref=land_refs[w].at[2 * px + py],
                    send_sem=send_sems.at[3 * w + j], recv_sem=recv_sems.at[3 * w + j], device_id=(px, py, c),
                    device_id_type=MESH)
                cp.wait_send()
                cp.wait_recv()

    outs = pl.pallas_call(
        body, name=name, out_shape=tuple(pltpu.HBM(a.shape, a.dtype) for a in (*q_thru, *land_thru)),
        in_specs=[HBM_SPEC_STRICT] * (2 * n) + [SEM_SPEC, SEM_SPEC, HBM_SPEC],
        out_specs=tuple([HBM_SPEC_STRICT] * (2 * n)), input_output_aliases={i: i for i in range(2 * n)},
        compiler_params=pltpu.CompilerParams(has_side_effects=SPLIT_EFFECT),
    )(*q_thru, *land_thru, send_sems, recv_sems, after)
    return list(outs[:n]), list(outs[n:])


def _sum_chips_tiled(q, r, idx, tile):
    rows, h = r.shape[1:]
    nt = h // tile

    def body(idx_ref, q_ref, r0_ref, r1_ref, r2_ref, g_ref):
        g_ref[...] = (q_ref[...].astype(F32) + r0_ref[...].astype(F32) + r1_ref[...].astype(F32)
                      + r2_ref[...].astype(F32))

    def slab(t):
        return pl.BlockSpec((None, rows, tile), lambda i, idx_ref: (idx_ref[t], 0, i))

    return pl.pallas_call(
        body, name="sum_chips_w_in",
        grid_spec=pltpu.PrefetchScalarGridSpec(
            num_scalar_prefetch=1, grid=(nt,), in_specs=[slab(0), slab(1), slab(2), slab(3)],
            out_specs=pl.BlockSpec((rows, tile), lambda i, idx_ref: (0, idx_ref[4] * nt + i))),
        out_shape=jax.ShapeDtypeStruct((rows, 2 * h), F32),
    )(idx, q, r, r, r)


def _sum_chips_small(qs, rs, idx, n_all):
    n = len(rs)

    def body(idx_ref, *refs):
        c = idx_ref[4]
        for w in range(n):
            q_ref, r_ref, g_ref = refs[w], refs[n + w], refs[2 * n + w]
            acc = q_ref[idx_ref[0]].astype(F32)
            for t in range(1, N_CHIPS):
                acc = acc + r_ref[idx_ref[t]].astype(F32)
            h = rs[w].shape[2]
            mine = pl.ds(pl.multiple_of(c * h, 128), h)
            g_ref[...] = jnp.zeros_like(g_ref)
            if w >= n - n_all:
                g_ref[idx_ref[0], :, mine] = acc
            else:
                g_ref[:, mine] = acc

    shapes = [jax.ShapeDtypeStruct((r.shape[1], 2 * r.shape[2]), F32) for r in rs[:n - n_all]]
    shapes += [jax.ShapeDtypeStruct((N_CHIPS, r.shape[1], 2 * r.shape[2]), F32) for r in rs[n - n_all:]]
    return pl.pallas_call(
        body, name="sum_chips_small",
        in_specs=[pl.BlockSpec(memory_space=pltpu.SMEM)] + [VMEM_SPEC] * (2 * n), out_specs=[VMEM_SPEC] * n,
        out_shape=shapes, compiler_params=pltpu.CompilerParams(vmem_limit_bytes=VMEM_LIMIT),
    )(idx, *qs, *rs)


def _share(shards, alls):
    n, na = len(shards), len(alls)
    total = n + na

    def body(*refs):
        g_refs, a_refs = refs[total:total + n], refs[total + n:2 * total]
        send_sems, recv_sems = refs[2 * total:]
        x, y, c, others = _place()
        me = 2 * x + y
        sibling = (x, y, 1 - c)

        def cols_of(w, half):
            h = shards[w].shape[1] // 2
            return g_refs[w].at[:, pl.ds(half * h, h)]

        def slab(a, chip, half):
            h = alls[a].shape[2] // 2
            return a_refs[a].at[chip, :, pl.ds(half * h, h)]

        def copy(src, dst, k, to):
            return pltpu.make_async_remote_copy(src_ref=src, dst_ref=dst, send_sem=send_sems.at[k],
                                                recv_sem=recv_sems.at[k], device_id=to, device_id_type=MESH)

        cps = [copy(cols_of(w, c), cols_of(w, c), w, sibling) for w in range(n)]
        for a in range(na):
            base = n + 7 * a
            cps.append(copy(slab(a, me, c), slab(a, me, c), base, sibling))
            for j, (px, py) in enumerate(others):
                cps.append(copy(slab(a, me, c), slab(a, me, c), base + 1 + j, (px, py, c)))
        for cp in cps:
            cp.start()
        fwd = []
        for a in range(na):
            base = n + 7 * a
            for j, (px, py) in enumerate(others):
                chip = 2 * px + py
                copy(slab(a, me, c), slab(a, chip, c), base + 1 + j, (px, py, c)).wait_recv()
                cp = copy(slab(a, chip, c), slab(a, chip, c), base + 4 + j, sibling)
                cp.start()
                fwd.append(cp)
        for a in range(na):
            base = n + 7 * a
            for j, (px, py) in enumerate(others):
                chip = 2 * px + py
                copy(slab(a, chip, c), slab(a, chip, 1 - c), base + 4 + j, sibling).wait_recv()
            copy(slab(a, me, c), slab(a, me, 1 - c), base, sibling).wait_recv()
        for w in range(n):
            copy(cols_of(w, c), cols_of(w, 1 - c), w, sibling).wait_recv()
        for cp in cps + fwd:
            cp.wait_send()

    nsem = n + 7 * na
    return pl.pallas_call(
        body, name="share", in_specs=[HBM_SPEC] * total, out_specs=[HBM_SPEC] * total,
        out_shape=[jax.ShapeDtypeStruct(a.shape, F32) for a in (*shards, *alls)],
        input_output_aliases={i: i for i in range(total)},
        scratch_shapes=[pltpu.SemaphoreType.DMA((nsem,)), pltpu.SemaphoreType.DMA((nsem,))],
    )(*shards, *alls)


def _adamw(w, g, m, v):
    m2 = ADAM_B1 * m + (1.0 - ADAM_B1) * g
    v2 = ADAM_B2 * v + (1.0 - ADAM_B2) * (g * g)
    m_hat = m2 / (1.0 - ADAM_B1 ** ADAM_STEP)
    v_hat = v2 / (1.0 - ADAM_B2 ** ADAM_STEP)
    return -ADAM_LR * (m_hat / (jnp.sqrt(v_hat) + ADAM_EPS) + ADAM_WD * w), m2, v2


def _update_w_in(wt, gt, mt, vt, lat, owner, tile):
    nlat = lat.shape[0] // tile

    def body(owner_ref, w_ref, g_ref, m_ref, v_ref, lat_ref, g2_ref, d_ref, m2_ref, v2_ref):
        row = pl.program_id(0) * tile + lax.broadcasted_iota(jnp.int32, (tile, 1), 0)
        g = jnp.where((row < LAT_COLS) & (owner_ref[0] == 1), lat_ref[...], g_ref[...])
        g2_ref[...] = g
        d_ref[...], m2_ref[...], v2_ref[...] = _adamw(w_ref[...], g, m_ref[...], v_ref[...])

    spec = pl.BlockSpec((tile, wt.shape[1]), lambda i, o: (i, 0))
    return pl.pallas_call(
        body, name="update_w_in",
        grid_spec=pltpu.PrefetchScalarGridSpec(
            num_scalar_prefetch=1, grid=(wt.shape[0] // tile,),
            in_specs=[spec] * 4 + [pl.BlockSpec((tile, wt.shape[1]), lambda i, o: (jnp.minimum(i, nlat - 1), 0))],
            out_specs=[spec] * 4),
        out_shape=[jax.ShapeDtypeStruct(wt.shape, F32)] * 4,
        compiler_params=_params(("parallel",)),
    )(owner, wt, gt, mt, vt, lat)


def _update_small(ws, gs, ms, vs):
    n = len(ws)

    def body(*refs):
        for k in range(n):
            w_ref, g_ref, m_ref, v_ref = refs[k], refs[n + k], refs[2 * n + k], refs[3 * n + k]
            d, m2, v2 = _adamw(w_ref[...], g_ref[...], m_ref[...], v_ref[...])
            refs[4 * n + k][...] = d
            refs[5 * n + k][...] = m2
            refs[6 * n + k][...] = v2

    shapes = [jax.ShapeDtypeStruct(w.shape, F32) for w in ws]
    outs = pl.pallas_call(
        body, name="update_small", in_specs=[VMEM_SPEC] * (4 * n), out_specs=[VMEM_SPEC] * (3 * n),
        out_shape=shapes * 3,
        compiler_params=pltpu.CompilerParams(vmem_limit_bytes=VMEM_LIMIT),
    )(*ws, *gs, *ms, *vs)
    return outs[:n], outs[n:2 * n], outs[2 * n:]


SHARDED = ("w_in", "w_uq", "w_oa", "w_ob", "w_out")
REPLICATED = ("b_in", "g_q", "g_kv", "w_ukv", "sgu_ln_g", "sgu_ln_b", "w_s", "b_s", "ln_g", "ln_b")
ORDER = ("w_in", "b_in", "g_q", "w_uq", "g_kv", "w_ukv", "w_oa", "sgu_ln_g", "sgu_ln_b", "w_s", "b_s", "w_ob", "w_out",
         "ln_g", "ln_b")


def kernel(x, positions, w_in, b_in, g_q, w_uq, g_kv, w_ukv, w_oa, sgu_ln_g, sgu_ln_b, w_s, b_s, w_ob, w_out, ln_g, ln_b, loss_target, m_w_in, m_b_in, m_g_q, m_w_uq, m_g_kv, m_w_ukv, m_w_oa, m_sgu_ln_g, m_sgu_ln_b, m_w_s, m_b_s, m_w_ob, m_w_out, m_ln_g, m_ln_b, v_w_in, v_b_in, v_g_q, v_w_uq, v_g_kv, v_w_ukv, v_w_oa, v_sgu_ln_g, v_sgu_ln_b, v_w_s, v_b_s, v_w_ob, v_w_out, v_ln_g, v_ln_b):
    w = dict(w_in=w_in, b_in=b_in, g_q=g_q, w_uq=w_uq, g_kv=g_kv, w_ukv=w_ukv, w_oa=w_oa, sgu_ln_g=sgu_ln_g,
             sgu_ln_b=sgu_ln_b, w_s=w_s, b_s=b_s, w_ob=w_ob, w_out=w_out, ln_g=ln_g, ln_b=ln_b)
    m = dict(w_in=m_w_in, b_in=m_b_in, g_q=m_g_q, w_uq=m_w_uq, g_kv=m_g_kv, w_ukv=m_w_ukv, w_oa=m_w_oa,
             sgu_ln_g=m_sgu_ln_g, sgu_ln_b=m_sgu_ln_b, w_s=m_w_s, b_s=m_b_s, w_ob=m_w_ob, w_out=m_w_out, ln_g=m_ln_g,
             ln_b=m_ln_b)
    v = dict(w_in=v_w_in, b_in=v_b_in, g_q=v_g_q, w_uq=v_w_uq, g_kv=v_g_kv, w_ukv=v_w_ukv, w_oa=v_w_oa,
             sgu_ln_g=v_sgu_ln_g, sgu_ln_b=v_sgu_ln_b, w_s=v_w_s, b_s=v_b_s, w_ob=v_w_ob, w_out=v_w_out, ln_g=v_ln_g,
             ln_b=v_ln_b)
    w, m, v = ({n: a[0] for n, a in d.items()} for d in (w, m, v))
    c = lax.axis_index("c")

    wt_shard, mt_shard, vt_shard = (jnp.transpose(d["w_in"]) for d in (w, m, v))
    g_in, g_uq, g_oa, g_ob, g_out = _gather_weights(
        [wt_shard, w["w_uq"].reshape(Q_RANK // 4, HEADS * QK_DIM), w["w_oa"], w["w_ob"], w["w_out"]])
    wt = g_in.reshape(IN_W, D_MODEL)
    full_uq = g_uq.reshape(Q_RANK, HEADS, QK_DIM)
    full_out = g_out.reshape(D_MODEL, D_MODEL)

    loss, early, st = _local_head(
        x[0], positions[0], loss_target[0], wt, w["b_in"], w["g_q"], full_uq, w["g_kv"], w["w_ukv"], g_oa,
        w["sgu_ln_g"], w["sgu_ln_b"], w["w_s"], w["b_s"], g_ob, full_out, w["ln_g"], w["ln_b"])

    xi, yi = lax.axis_index("x"), lax.axis_index("y")
    c1 = c.reshape(1).astype(jnp.int32)
    idx = jnp.stack([2 * xi + yi, 2 * (1 - xi) + yi, 2 * xi + (1 - yi), 2 * (1 - xi) + (1 - yi), c]).astype(jnp.int32)
    parts1 = [early["w_in"].reshape(N_CHIPS, IN_W // N_CHIPS, D_MODEL), early["w_oa"].astype(BF16),
              early["w_ob"].astype(BF16), early["w_out"].reshape(N_CHIPS, SLAB_W, D_MODEL).astype(BF16)]
    *recv1, all_loss = _exchange_pairs_and_loss(parts1, jnp.broadcast_to(loss, LOSS_TILE))
    pairs1 = [_add_pair_tiled(parts1[0], recv1[0], c1),
              *_add_pair_small(parts1[1:], recv1[1:], c1, "add_pair_early")]
    sems1 = _chips_start(pairs1, "chips_start_early")

    st["delta"] = st["delta"] + sems1[4][0, 0]
    dhl, late = _local_tail(st)

    grads = {**early, **late}
    rep = jnp.concatenate([_rows8(grads[n]) for n in REPLICATED], axis=0)
    rep = jnp.pad(rep, ((0, N_CHIPS * REP_ROWS - rep.shape[0]), (0, 0))).reshape(N_CHIPS, REP_ROWS, D_MODEL)
    parts2 = [late["w_uq"].reshape(N_CHIPS, Q_RANK // N_CHIPS, HEADS * QK_DIM).astype(BF16), rep.astype(BF16),
              late["w_lat"].reshape(N_CHIPS, LAT_ROWS_PAD // N_CHIPS, D_MODEL)]
    pairs2 = _add_pair_small(parts2, _exchange_pairs(parts2, "exchange_pairs_late"), c1, "add_pair_late")
    sems2 = _chips_start(pairs2, "chips_start_late")
    dx = _dx(st["dr"], st["dhg"], st["dhm"], dhl, st["wt"], sems2[4])
    pairs2, landed2 = _chips_wait(*sems2[:4], dx, "chips_wait_late")
    pairs1, landed1 = _chips_wait(*sems1[:4], landed2[0], "chips_wait_early")
    sums = [_sum_chips_tiled(pairs1[0], landed1[0], idx, 128),
            *_sum_chips_small([*pairs1[1:], *pairs2], [*landed1[1:], *landed2], idx, 2)]
    *shards, g_rep, g_lat = _share(sums[:-2], sums[-2:])
    loss = jnp.sum(all_loss[:, 0, 0])

    red = {n: s.reshape(w[n].shape) for n, s in zip(("w_oa", "w_ob", "w_out", "w_uq"), shards[1:])}
    g_rep = g_rep.reshape(N_CHIPS * REP_ROWS, D_MODEL)
    off = 0
    for n in REPLICATED:
        rows = _rows8(w[n]).shape[0]
        red[n] = g_rep[off:off + rows].reshape(-1)[:w[n].size].reshape(w[n].shape)
        off += rows
    owner = (2 * xi + yi == 0).astype(jnp.int32).reshape(1)
    gt, dt, mt, vt2 = _update_w_in(wt_shard, shards[0], mt_shard, vt_shard, g_lat.reshape(LAT_ROWS_PAD, D_MODEL),
                                   owner, 232)
    red["w_in"] = jnp.transpose(gt)
    small = [n for n in ORDER if n != "w_in"]
    as2d = lambda a: a.reshape(-1, a.shape[-1])
    ds, ms, vs = _update_small([as2d(w[n]) for n in small], [as2d(red[n]) for n in small],
                               [as2d(m[n]) for n in small], [as2d(v[n]) for n in small])
    delta, new_m, new_v = {"w_in": jnp.transpose(dt)}, {"w_in": jnp.transpose(mt)}, {"w_in": jnp.transpose(vt2)}
    for i, n in enumerate(small):
        delta[n], new_m[n], new_v[n] = (a[i].reshape(w[n].shape) for a in (ds, ms, vs))

    lead = lambda a: a[None]
    return (loss, dx[None], *[lead(red[n]) for n in ORDER], *[lead(delta[n]) for n in ORDER],
            *[lead(new_m[n]) for n in ORDER], *[lead(new_v[n]) for n in ORDER])
```

```python
import functools
import math

import jax
import jax.numpy as jnp
from jax import lax
from jax.experimental import pallas as pl
from jax.experimental.pallas import tpu as pltpu

F32 = jnp.float32
BF16 = jnp.bfloat16

D_MODEL = 1024
HEADS = 8
Q_RANK = 384
KV_RANK = 128
NOPE = 64
ROPE = 32
V_DIM = 64
QK_DIM = NOPE + ROPE
HEAD_PAD = 128
MLA_W = HEADS * V_DIM
SGU_W = 512
GROUPS = 8
CHUNK = 128
IN_W = 4640
RMS_EPS = 1e-6
LN_EPS = 1e-5
ALPHA = 2.0 ** 0.25
ROPE_THETA = 10000.0
SCALE = QK_DIM ** -0.5

GATE_W = 2 * D_MODEL
MID_W = 4 * SGU_W
LAT_W = Q_RANK + KV_RANK + HEAD_PAD
PAD_W = GATE_W + MID_W + LAT_W
LAT_COLS = Q_RANK + KV_RANK + ROPE
ROW_GATE = LAT_COLS + MID_W
LAT_ROWS_PAD = 704
N_SLABS = 4
SLAB_W = D_MODEL // N_SLABS

ROW_TILE = 256
ATT_TQ = 256
ATT_TK = 256
ATT_BWD_TQ = 256
ATT_BWD_TK = 256
LOG2E = 1.4426950408889634
LN2 = 0.6931471805599453
Q_SCALE = SCALE * LOG2E
VMEM_LIMIT = 56 * 1024 * 1024

ADAM_LR = 0.001
ADAM_B1 = 0.9
ADAM_B2 = 0.999
ADAM_EPS = 1e-08
ADAM_WD = 0.01
ADAM_STEP = 10


def _dot(a, b):
    return jnp.dot(a, b, preferred_element_type=F32)


def _dot_nt(a, b):
    return lax.dot_general(a, b, (((1,), (1,)), ((), ())), preferred_element_type=F32)


def _dot_tn(a, b):
    return lax.dot_general(a, b, (((0,), (0,)), ((), ())), preferred_element_type=F32)


def _sigmoid(z):
    return 0.5 * jnp.tanh(0.5 * z) + 0.5


_GELU_C = math.sqrt(2.0 / math.pi)


def _gelu_and_grad(x):
    x2 = x * x
    t = jnp.tanh(_GELU_C * (x + 0.044715 * x * x2))
    g = 0.5 * x * (1.0 + t)
    dg = 0.5 * (1.0 + t) + 0.5 * x * (1.0 - t * t) * (_GELU_C * (1.0 + 3.0 * 0.044715 * x2))
    return g, dg


def _silu_and_grad(z):
    s = _sigmoid(z)
    return z * s, s * (1.0 + z * (1.0 - s))


def _rope(xb, c, sl, sh):
    return xb * c + pltpu.roll(xb, 112, 1) * sl + pltpu.roll(xb, 16, 1) * sh


def _rope_t(dy, c, sl, sh):
    return dy * c + pltpu.roll(dy * sl, 16, 1) + pltpu.roll(dy * sh, 112, 1)


def _params(sem=("arbitrary",)):
    return pltpu.CompilerParams(dimension_semantics=sem, vmem_limit_bytes=VMEM_LIMIT)


def _row_spec(tile, width):
    return pl.BlockSpec((tile, width), lambda i: (i, 0))


def _full_spec(shape):
    nd = len(shape)
    return pl.BlockSpec(shape, lambda i: (0,) * nd)


def _kpe_rows(wt_ref):
    z = lambda n: jnp.zeros((n, D_MODEL), BF16)
    return jnp.concatenate([z(NOPE), wt_ref[Q_RANK + KV_RANK:LAT_COLS, :], z(HEAD_PAD - QK_DIM)], axis=0)


def _fwd_rest(xb, wt, b_g, b_m):
    s = xb.shape[0]
    ts = ROW_TILE

    def body(xb_ref, wt_ref, bg_ref, bm_ref, hg_ref, hm_ref):
        xb_ = xb_ref[...]
        hg_ref[...] = _dot_nt(xb_, wt_ref[ROW_GATE:IN_W, :]) + bg_ref[...]
        hm_ref[...] = _dot_nt(xb_, wt_ref[LAT_COLS:ROW_GATE, :]) + bm_ref[...]

    return pl.pallas_call(
        body, name="fwd_rest", grid=(s // ts,),
        in_specs=[_row_spec(ts, D_MODEL), _full_spec(wt.shape), _full_spec(b_g.shape), _full_spec(b_m.shape)],
        out_specs=[_row_spec(ts, GATE_W), _row_spec(ts, MID_W)],
        out_shape=[jax.ShapeDtypeStruct((s, GATE_W), F32), jax.ShapeDtypeStruct((s, MID_W), F32)],
        compiler_params=_params(),
    )(xb, wt, b_g, b_m)


def _fwd_lat(x, wlat, b_l, g_q, wuq, g_kv, wk, wv, rc, rsl, rsh, after):
    s = x.shape[0]
    ts = ROW_TILE

    def body(x_ref, wt_ref, bl_ref, gq_ref, wuq_ref, gkv_ref, wk_ref, wv_ref, rc_ref, rsl_ref,
             rsh_ref, after_ref, hl_ref, q_ref, k_ref, v_ref, xb_ref, qt_ref, kt_ref, vt_ref):
        xb = x_ref[...].astype(BF16)
        xb_ref[...] = xb
        hl = jnp.concatenate([_dot_nt(xb, wt_ref[0:Q_RANK + KV_RANK, :]), _dot_nt(xb, _kpe_rows(wt_ref))],
                             axis=1) + bl_ref[...]
        hl_ref[...] = hl
        c, sl, sh = rc_ref[...], rsl_ref[...], rsh_ref[...]
        cq = hl[:, :Q_RANK]
        cqn = cq * lax.rsqrt(jnp.mean(cq * cq, axis=-1, keepdims=True) + RMS_EPS) * gq_ref[...]
        q = _dot(cqn.astype(BF16), wuq_ref[...])
        ckv = hl[:, Q_RANK:Q_RANK + KV_RANK]
        ckvn = (ckv * lax.rsqrt(jnp.mean(ckv * ckv, axis=-1, keepdims=True) + RMS_EPS) * gkv_ref[...]).astype(BF16)
        k = _dot(ckvn, wk_ref[...])
        vb = _dot(ckvn, wv_ref[...]).astype(BF16)
        v_ref[...] = vb
        vt_ref[...] = vb.T
        kpe = _rope(hl[:, Q_RANK + KV_RANK:], c, sl, sh)
        for hd in range(HEADS):
            lanes = slice(hd * HEAD_PAD, (hd + 1) * HEAD_PAD)
            qb = (_rope(q[:, lanes], c, sl, sh) * Q_SCALE).astype(BF16)
            kb = (k[:, lanes] + kpe).astype(BF16)
            q_ref[:, lanes] = qb
            k_ref[:, lanes] = kb
            qt_ref[lanes, :] = qb.T
            kt_ref[lanes, :] = kb.T

    qk_w = HEADS * HEAD_PAD
    col_spec = lambda rows: pl.BlockSpec((rows, ts), lambda i: (0, i))
    return pl.pallas_call(
        body, name="fwd_lat", grid=(s // ts,),
        in_specs=[_row_spec(ts, D_MODEL), pl.BlockSpec((None,) + wlat.shape[1:], lambda i: (0, 0, 0)),
                  _full_spec(b_l.shape), _full_spec(g_q.shape),
                  _full_spec(wuq.shape), _full_spec(g_kv.shape), _full_spec(wk.shape), _full_spec(wv.shape),
                  _row_spec(ts, HEAD_PAD), _row_spec(ts, HEAD_PAD), _row_spec(ts, HEAD_PAD),
                  pl.BlockSpec(memory_space=pl.ANY)],
        out_specs=[_row_spec(ts, LAT_W), _row_spec(ts, qk_w),
                   _row_spec(ts, qk_w), _row_spec(ts, MLA_W), _row_spec(ts, D_MODEL), col_spec(qk_w), col_spec(qk_w),
                   col_spec(MLA_W)],
        out_shape=[jax.ShapeDtypeStruct((s, LAT_W), F32), jax.ShapeDtypeStruct((s, qk_w), BF16),
                   jax.ShapeDtypeStruct((s, qk_w), BF16), jax.ShapeDtypeStruct((s, MLA_W), BF16),
                   jax.ShapeDtypeStruct((s, D_MODEL), BF16), jax.ShapeDtypeStruct((qk_w, s), BF16),
                   jax.ShapeDtypeStruct((qk_w, s), BF16), jax.ShapeDtypeStruct((MLA_W, s), BF16)],
        compiler_params=_params(),
    )(x, wlat, b_l, g_q, wuq, g_kv, wk, wv, rc, rsl, rsh, after)


def _attn_fwd(qt, k, vt):
    s = k.shape[0]
    tq, tk = ATT_TQ, ATT_TK
    r = tq // tk
    pairs = HEADS // 2

    def body(qt_ref, k_ref, vt_ref, o_ref, lse_ref):
        i = pl.program_id(1)
        krow = lax.broadcasted_iota(jnp.int32, (tk, tq), 0)
        qcol = lax.broadcasted_iota(jnp.int32, (tk, tq), 1)
        qts = [qt_ref[hh * HEAD_PAD:(hh + 1) * HEAD_PAD, :] for hh in range(2)]

        def scores(j):
            koff = pl.multiple_of(j * tk, tk)
            return tuple(_dot(k_ref[pl.ds(koff, tk), hh * HEAD_PAD:(hh + 1) * HEAD_PAD], qts[hh]) for hh in range(2))

        def weighted(j, ps):
            koff = pl.multiple_of(j * tk, tk)
            return tuple(_dot(vt_ref[hh * V_DIM:(hh + 1) * V_DIM, pl.ds(koff, tk)], ps[hh]) for hh in range(2))

        def step(j, carry, diag, last):
            st, ps, stats = carry
            st_next = None if last else scores(j + 1)
            pvs = weighted(jnp.maximum(j - 1, 0), ps)
            new_ps, new_stats = [], []
            for hh in range(2):
                m, l, acc = stats[hh]
                s_ = st[hh]
                if diag is not None:
                    s_ = jnp.where(krow + diag * tk <= qcol, s_, -jnp.inf)
                m_new = jnp.maximum(m, jnp.max(s_, axis=0, keepdims=True))
                a = jnp.exp2(m - m_new)
                p = jnp.exp2(s_ - m_new)
                new_stats.append((m_new, a * l + jnp.sum(p, axis=0, keepdims=True), a * (acc + pvs[hh])))
                new_ps.append(p.astype(BF16))
            return st_next, tuple(new_ps), tuple(new_stats)

        one = (jnp.full((1, tq), -jnp.inf, F32), jnp.zeros((1, tq), F32), jnp.zeros((V_DIM, tq), F32))
        zero_p = jnp.zeros((tk, tq), BF16)
        nfull = i * r
        carry = lax.fori_loop(0, nfull, functools.partial(step, diag=None, last=False),
                              (scores(0), (zero_p, zero_p), (one, one)))
        for d in range(r):
            carry = step(nfull + d, carry, d, d == r - 1)
        _, ps, stats = carry
        pvs = weighted(nfull + r - 1, ps)
        ot = jnp.concatenate([(stats[hh][2] + pvs[hh]) / stats[hh][1] for hh in range(2)], axis=0)
        o_ref[...] = ot.T
        lse = [stats[hh][0] + jnp.log(stats[hh][1]) * LOG2E for hh in range(2)]
        lse_ref[...] = jnp.concatenate(lse + [jnp.zeros((6, tq), F32)], axis=0)

    return pl.pallas_call(
        body, name="attn_fwd", grid=(pairs, s // tq),
        in_specs=[pl.BlockSpec((2 * HEAD_PAD, tq), lambda p, i: (p, i)),
                  pl.BlockSpec((s, 2 * HEAD_PAD), lambda p, i: (0, p)),
                  pl.BlockSpec((2 * V_DIM, s), lambda p, i: (p, 0))],
        out_specs=[pl.BlockSpec((tq, 2 * V_DIM), lambda p, i: (i, p)),
                   pl.BlockSpec((None, 8, tq), lambda p, i: (p, 0, i))],
        out_shape=[jax.ShapeDtypeStruct((s, MLA_W), F32), jax.ShapeDtypeStruct((pairs, 8, s), F32)],
        compiler_params=_params(("arbitrary", "arbitrary")),
    )(qt, k, vt)


def _attn_bwd(q, qt, k, kt, v, do, dot, lse, delta):
    s = k.shape[0]
    tq, tk = ATT_BWD_TQ, ATT_BWD_TK
    r = tq // tk
    nq = s // tq
    nk = s // tk
    pairs = HEADS // 2

    def body(q_ref, qt_ref, k_ref, kt_ref, v_ref, do_ref, dot_ref, lse_ref, dl_ref, dqt_ref, dk_ref, dv_ref):
        j = pl.program_id(1)
        krow = lax.broadcasted_iota(jnp.int32, (tk, tq), 0)
        qcol = lax.broadcasted_iota(jnp.int32, (tk, tq), 1)
        lane = lax.broadcasted_iota(jnp.int32, (tk, 2 * V_DIM), 1)
        drow = lax.broadcasted_iota(jnp.int32, (2 * V_DIM, tq), 0)

        @pl.when(j == 0)
        def _():
            dqt_ref[...] = jnp.zeros_like(dqt_ref)

        koff = pl.multiple_of(j * tk, tk)
        vb = v_ref[pl.ds(koff, tk), :]
        kbs = [k_ref[pl.ds(koff, tk), hh * HEAD_PAD:(hh + 1) * HEAD_PAD] for hh in range(2)]
        ktbs = [kt_ref[hh * HEAD_PAD:(hh + 1) * HEAD_PAD, pl.ds(koff, tk)] for hh in range(2)]
        i0 = j // r

        def front(i):
            qoff = pl.multiple_of(i * tq, tq)
            dotb = dot_ref[:, pl.ds(qoff, tq)]
            out = []
            for hh in range(2):
                mine = (drow < V_DIM) if hh == 0 else (drow >= V_DIM)
                st = _dot(kbs[hh], qt_ref[hh * HEAD_PAD:(hh + 1) * HEAD_PAD, pl.ds(qoff, tq)])
                out.append((st, _dot(vb, jnp.where(mine, dotb, jnp.zeros_like(dotb)))))
            return tuple(out)

        def middle(i, tiles, diag):
            qoff = pl.multiple_of(i * tq, tq)
            out = []
            for hh in range(2):
                st, dpt = tiles[hh]
                if diag:
                    st = jnp.where(krow + (j - i0 * r) * tk <= qcol, st, -jnp.inf)
                p = jnp.exp2(st - lse_ref[hh:hh + 1, pl.ds(qoff, tq)])
                out.append((p.astype(BF16), (p * (dpt - dl_ref[hh:hh + 1, pl.ds(qoff, tq)])).astype(BF16)))
            return tuple(out)

        def back(i, pd, accs):
            qoff = pl.multiple_of(i * tq, tq)
            dob = do_ref[pl.ds(qoff, tq), :]
            out = []
            for hh in range(2):
                rows = slice(hh * HEAD_PAD, (hh + 1) * HEAD_PAD)
                p, dst = pd[hh]
                dk_acc, dv_acc = accs[hh]
                dv_acc = dv_acc + _dot(p, dob)
                dk_acc = dk_acc + _dot(dst, q_ref[pl.ds(qoff, tq), rows])
                dqt_ref[rows, pl.ds(qoff, tq)] += _dot(ktbs[hh], dst)
                out.append((dk_acc, dv_acc))
            return tuple(out)

        def step(i, accs, diag):
            return back(i, middle(i, front(i), diag), accs)

        zero_acc = (jnp.zeros((tk, HEAD_PAD), F32), jnp.zeros((tk, 2 * V_DIM), F32))
        accs = step(i0, (zero_acc, zero_acc), True)
        accs = lax.fori_loop(i0 + 1, nq, functools.partial(step, diag=False), accs)
        for hh in range(2):
            dk_ref[:, hh * HEAD_PAD:(hh + 1) * HEAD_PAD] = accs[hh][0] * LN2
        dv_ref[...] = jnp.where(lane < V_DIM, accs[0][1], accs[1][1])

        @pl.when(j == nk - 1)
        def _():
            dqt_ref[...] = dqt_ref[...] * SCALE

    pair_rows = lambda w: pl.BlockSpec((s, w), lambda p, j: (0, p))
    pair_cols = lambda w: pl.BlockSpec((w, s), lambda p, j: (p, 0))
    stats = pl.BlockSpec((None, 8, s), lambda p, j: (p, 0, 0))
    return pl.pallas_call(
        body, name="attn_bwd", grid=(pairs, nk),
        in_specs=[pair_rows(2 * HEAD_PAD), pair_cols(2 * HEAD_PAD), pair_rows(2 * HEAD_PAD), pair_cols(2 * HEAD_PAD),
                  pair_rows(2 * V_DIM), pair_rows(2 * V_DIM), pair_cols(2 * V_DIM), stats, stats],
        out_specs=[pair_cols(2 * HEAD_PAD),
                   pl.BlockSpec((tk, 2 * HEAD_PAD), lambda p, j: (j, p)),
                   pl.BlockSpec((tk, 2 * V_DIM), lambda p, j: (j, p))],
        out_shape=[jax.ShapeDtypeStruct((HEADS * HEAD_PAD, s), F32), jax.ShapeDtypeStruct((s, HEADS * HEAD_PAD), F32),
                   jax.ShapeDtypeStruct((s, MLA_W), F32)],
        compiler_params=_params(("arbitrary", "arbitrary")),
    )(q, qt, k, kt, v, do, dot, lse, delta)


def _split3(a):
    hi = a.astype(BF16)
    r1 = a - hi.astype(F32)
    mid = r1.astype(BF16)
    lo = (r1 - mid.astype(F32)).astype(BF16)
    return hi, mid, lo


def _mid(x, tgt, o, hm, hg, woa, wob, wout, ln_g, ln_b, sg_g, sg_b, w_s, bsb):
    s = x.shape[0]
    ts = ROW_TILE
    nsteps = s // ts
    nch = ts // CHUNK
    npair = GROUPS // 2

    def body(x_ref, t_ref, o_ref, hm_ref, hg_ref, woa_ref, wob_ref, wout_ref, lng_ref, lnb_ref, sgg_ref, sgb_ref,
             ws_ref, bsb_ref,
             dr_ref, dhg_ref, dhm_ref, do_ref, dot_ref, dl_ref, dhgt_ref, dhmt_ref,
             dwout_ref, dwoa_ref, dwob_ref, dws_ref, dbs_ref, dlng_ref, dlnb_ref, dsgg_ref, dsgb_ref, loss_ref,
             dbg_ref, dbm_ref, dbacc_ref):
        i = pl.program_id(0)

        @pl.when(i == 0)
        def _():
            for r in (dwout_ref, dwoa_ref, dwob_ref, dws_ref, dlng_ref, dlnb_ref, dsgg_ref, dsgb_ref, loss_ref,
                      dbg_ref, dbm_ref, dbacc_ref):
                r[...] = jnp.zeros_like(r)

        def emit(ref, tref, bref, lo, val):
            vb = val.astype(BF16)
            n = val.shape[1]
            ref[:, lo:lo + n] = vb
            tref[lo:lo + n, :] = vb.T
            bref[:, lo:lo + n] += jnp.sum(val, axis=0, keepdims=True)

        lane = lax.broadcasted_iota(jnp.int32, (CHUNK, CHUNK), 1)
        left = lane < V_DIM
        tril = lax.broadcasted_iota(jnp.int32, (CHUNK, CHUNK), 0) >= lane
        ms = [jnp.where(tril, ws_ref[g], 0.0).astype(BF16) for g in range(GROUPS)]

        z_a = hm_ref[:, 0:SGU_W]
        u = hm_ref[:, SGU_W:2 * SGU_W]
        v = hm_ref[:, 2 * SGU_W:3 * SGU_W]
        z_b = hm_ref[:, 3 * SGU_W:4 * SGU_W]
        o = o_ref[...]
        sa, dsa = _silu_and_grad(z_a)
        y_a = (o * sa).astype(BF16)
        gu, dgu = _gelu_and_grad(u)
        gv, dgv = _gelu_and_grad(v)
        mu = jnp.mean(gv, axis=-1, keepdims=True)
        vc = gv - mu
        rstd_v = lax.rsqrt(jnp.mean(vc * vc, axis=-1, keepdims=True) + LN_EPS)
        vhat = vc * rstd_v
        vn = (vhat * sgg_ref[...] + sgb_ref[...]).astype(BF16)
        rows = []
        for c in range(nch):
            blocks = []
            for p in range(npair):
                blk = vn[c * CHUNK:(c + 1) * CHUNK, p * CHUNK:(p + 1) * CHUNK]
                blocks.append(jnp.where(left, _dot(ms[2 * p], blk), _dot(ms[2 * p + 1], blk)))
            rows.append(jnp.concatenate(blocks, axis=1) + bsb_ref[...])
        mixed = jnp.concatenate(rows, axis=0)
        sgu = gu * mixed
        sb, dsb = _silu_and_grad(z_b)
        y_b = (sgu * sb).astype(BF16)
        pa = jnp.concatenate([_dot(y_a, woa_ref[k]) for k in range(N_SLABS)], axis=1)
        pb = jnp.concatenate([_dot(y_b, wob_ref[k]) for k in range(N_SLABS)], axis=1)
        sga = _sigmoid(hg_ref[:, :D_MODEL])
        sgb = _sigmoid(hg_ref[:, D_MODEL:])
        m2 = (sga * pa + sgb * pb).astype(BF16)
        r = ALPHA * x_ref[...] + _dot(m2, wout_ref[...])
        rmu = jnp.mean(r, axis=-1, keepdims=True)
        rc = r - rmu
        rstd = lax.rsqrt(jnp.mean(rc * rc, axis=-1, keepdims=True) + LN_EPS)
        xhat = rc * rstd
        y = xhat * lng_ref[...] + lnb_ref[...]
        err = y - t_ref[...]
        loss_ref[...] += jnp.full(loss_ref.shape, 0.5 / D_MODEL, F32) * jnp.sum(err * err)

        dy = err * (1.0 / D_MODEL)
        dlng_ref[...] += jnp.sum(dy * xhat, axis=0, keepdims=True)
        dlnb_ref[...] += jnp.sum(dy, axis=0, keepdims=True)
        dxh = dy * lng_ref[...]
        dr = rstd * (dxh - jnp.mean(dxh, axis=-1, keepdims=True) - xhat * jnp.mean(dxh * xhat, axis=-1, keepdims=True))
        dr_ref[...] = dr
        drb = dr.astype(BF16)
        dwout_ref[...] += _dot_tn(m2, drb)
        dm2 = _dot_nt(drb, wout_ref[...])
        emit(dhg_ref, dhgt_ref, dbg_ref, 0, dm2 * pa * sga * (1.0 - sga))
        emit(dhg_ref, dhgt_ref, dbg_ref, D_MODEL, dm2 * pb * sgb * (1.0 - sgb))
        dpa = (dm2 * sga).astype(BF16)
        dpb = (dm2 * sgb).astype(BF16)
        dy_a = jnp.zeros((ts, MLA_W), F32)
        dy_b = jnp.zeros((ts, SGU_W), F32)
        for k in range(N_SLABS):
            cols = slice(k * SLAB_W, (k + 1) * SLAB_W)
            dwoa_ref[k] += _dot_tn(y_a, dpa[:, cols])
            dwob_ref[k] += _dot_tn(y_b, dpb[:, cols])
            dy_a = dy_a + _dot_nt(dpa[:, cols], woa_ref[k])
            dy_b = dy_b + _dot_nt(dpb[:, cols], wob_ref[k])
        dob = (dy_a * sa).astype(BF16)
        do_ref[...] = dob
        dot_ref[...] = dob.T
        head = (lax.broadcasted_iota(jnp.int32, (HEADS, MLA_W), 1) // V_DIM
                == lax.broadcasted_iota(jnp.int32, (HEADS, MLA_W), 0)).astype(BF16)
        dl_ref[...] = sum(_dot_nt(head, term) for term in _split3(dob.astype(F32) * o))
        emit(dhm_ref, dhmt_ref, dbm_ref, 0, dy_a * o * dsa)
        dsg = dy_b * sb
        emit(dhm_ref, dhmt_ref, dbm_ref, 3 * SGU_W, dy_b * sgu * dsb)
        emit(dhm_ref, dhmt_ref, dbm_ref, SGU_W, dsg * mixed * dgu)
        dmixed = dsg * gu
        dvn_rows = []
        dbs_sum = jnp.zeros((CHUNK, SGU_W), F32)
        for c in range(nch):
            dm_c = dmixed[c * CHUNK:(c + 1) * CHUNK, :]
            dbs_sum = dbs_sum + dm_c
            blocks = []
            for p in range(npair):
                dmb = dm_c[:, p * CHUNK:(p + 1) * CHUNK].astype(BF16)
                blk = vn[c * CHUNK:(c + 1) * CHUNK, p * CHUNK:(p + 1) * CHUNK]
                blocks.append(jnp.where(left, _dot_tn(ms[2 * p], dmb), _dot_tn(ms[2 * p + 1], dmb)))
                zero = jnp.zeros_like(dmb)
                dws_ref[2 * p] += jnp.where(tril, _dot_nt(jnp.where(left, dmb, zero), blk), 0.0)
                dws_ref[2 * p + 1] += jnp.where(tril, _dot_nt(jnp.where(left, zero, dmb), blk), 0.0)
            dvn_rows.append(jnp.concatenate(blocks, axis=1))
        dbacc_ref[...] += dbs_sum
        dvn = jnp.concatenate(dvn_rows, axis=0)
        dsgg_ref[...] += jnp.sum(dvn * vhat, axis=0, keepdims=True)
        dsgb_ref[...] += jnp.sum(dvn, axis=0, keepdims=True)
        dvh = dvn * sgg_ref[...]
        dgv_in = rstd_v * (dvh - jnp.mean(dvh, axis=-1, keepdims=True)
                           - vhat * jnp.mean(dvh * vhat, axis=-1, keepdims=True))
        emit(dhm_ref, dhmt_ref, dbm_ref, 2 * SGU_W, dgv_in * dgv)

        @pl.when(i == nsteps - 1)
        def _():
            grp = (lax.broadcasted_iota(jnp.int32, (SGU_W, CHUNK), 0) // V_DIM
                   == lax.broadcasted_iota(jnp.int32, (SGU_W, CHUNK), 1)).astype(BF16)
            hi, mid, lo = _split3(dbacc_ref[...])
            dbs_ref[...] = _dot(hi, grp) + _dot(mid, grp) + _dot(lo, grp)

    acc_shapes = [(D_MODEL, D_MODEL), woa.shape, wob.shape, (GROUPS, CHUNK, CHUNK), (CHUNK, CHUNK),
                  (1, D_MODEL), (1, D_MODEL), (1, SGU_W), (1, SGU_W), (1, 128), (1, GATE_W), (1, MID_W)]
    col_spec = lambda rows: pl.BlockSpec((rows, ts), lambda i: (0, i))
    return pl.pallas_call(
        body, name="mid", grid=(nsteps,),
        in_specs=[_row_spec(ts, D_MODEL), _row_spec(ts, D_MODEL), _row_spec(ts, MLA_W), _row_spec(ts, MID_W),
                  _row_spec(ts, GATE_W), _full_spec(woa.shape), _full_spec(wob.shape), _full_spec(wout.shape),
                  _full_spec(ln_g.shape), _full_spec(ln_b.shape), _full_spec(sg_g.shape), _full_spec(sg_b.shape),
                  _full_spec(w_s.shape), _full_spec(bsb.shape)],
        out_specs=[_row_spec(ts, D_MODEL), _row_spec(ts, GATE_W), _row_spec(ts, MID_W), _row_spec(ts, MLA_W),
                   col_spec(MLA_W), col_spec(HEADS), col_spec(GATE_W), col_spec(MID_W)]
        + [_full_spec(sh) for sh in acc_shapes],
        out_shape=[jax.ShapeDtypeStruct((s, D_MODEL), F32), jax.ShapeDtypeStruct((s, GATE_W), BF16),
                   jax.ShapeDtypeStruct((s, MID_W), BF16), jax.ShapeDtypeStruct((s, MLA_W), BF16),
                   jax.ShapeDtypeStruct((MLA_W, s), BF16), jax.ShapeDtypeStruct((HEADS, s), F32),
                   jax.ShapeDtypeStruct((GATE_W, s), BF16), jax.ShapeDtypeStruct((MID_W, s), BF16)]
        + [jax.ShapeDtypeStruct(sh, F32) for sh in acc_shapes],
        scratch_shapes=[pltpu.VMEM((CHUNK, SGU_W), F32)],
        compiler_params=_params(),
    )(x, tgt, o, hm, hg, woa, wob, wout, ln_g, ln_b, sg_g, sg_b, w_s, bsb)


def _lat_bwd(dq, dk, dv, hl, rc, rsl, rsh, g_q, g_kv, wuq, wk, wv):
    s = dk.shape[0]
    ts = ROW_TILE
    qk_w = HEADS * HEAD_PAD

    def body(dq_ref, dk_ref, dv_ref, hl_ref, rc_ref, rsl_ref, rsh_ref, gq_ref, gkv_ref, wuq_ref, wk_ref, wv_ref,
             dhl_ref, dhlt_ref, dwuq_ref, dwk_ref, dwv_ref, dgq_ref, dgkv_ref, dbl_ref):
        i = pl.program_id(0)

        @pl.when(i == 0)
        def _():
            for r in (dwuq_ref, dwk_ref, dwv_ref, dgq_ref, dgkv_ref, dbl_ref):
                r[...] = jnp.zeros_like(r)

        def emit(lo, val):
            vb = val.astype(BF16)
            n = val.shape[1]
            dhl_ref[:, lo:lo + n] = vb
            dhlt_ref[lo:lo + n, :] = vb.T
            dbl_ref[:, lo:lo + n] += jnp.sum(val, axis=0, keepdims=True)

        c, sl, sh = rc_ref[...], rsl_ref[...], rsh_ref[...]
        lane = lax.broadcasted_iota(jnp.int32, (ts, HEAD_PAD), 1)
        pe = (lane >= NOPE) & (lane < QK_DIM)
        dkpe = jnp.zeros((ts, HEAD_PAD), F32)
        dqu = []
        for hd in range(HEADS):
            lanes = slice(hd * HEAD_PAD, (hd + 1) * HEAD_PAD)
            dqu.append(_rope_t(dq_ref[lanes, :].T, c, sl, sh).astype(BF16))
            dkpe = dkpe + dk_ref[:, lanes]
        dqu = jnp.concatenate(dqu, axis=1)
        dkpe = _rope_t(jnp.where(pe, dkpe, 0.0), c, sl, sh)

        cq = hl_ref[:, :Q_RANK]
        rq = lax.rsqrt(jnp.mean(cq * cq, axis=-1, keepdims=True) + RMS_EPS)
        cqh = cq * rq
        cqn = (cqh * gq_ref[...]).astype(BF16)
        dwuq_ref[...] += _dot_tn(cqn, dqu)
        dcqn = _dot_nt(dqu, wuq_ref[...])
        dgq_ref[...] += jnp.sum(dcqn * cqh, axis=0, keepdims=True)
        dch = dcqn * gq_ref[...]
        emit(0, rq * (dch - cqh * jnp.mean(dch * cqh, axis=-1, keepdims=True)))

        ckv = hl_ref[:, Q_RANK:Q_RANK + KV_RANK]
        rk = lax.rsqrt(jnp.mean(ckv * ckv, axis=-1, keepdims=True) + RMS_EPS)
        ckh = ckv * rk
        ckn = (ckh * gkv_ref[...]).astype(BF16)
        dkb = dk_ref[...].astype(BF16)
        dvb = dv_ref[...].astype(BF16)
        dwk_ref[...] += _dot_tn(ckn, dkb)
        dwv_ref[...] += _dot_tn(ckn, dvb)
        dckn = _dot_nt(dkb, wk_ref[...]) + _dot_nt(dvb, wv_ref[...])
        dgkv_ref[...] += jnp.sum(dckn * ckh, axis=0, keepdims=True)
        dkh = dckn * gkv_ref[...]
        emit(Q_RANK, rk * (dkh - ckh * jnp.mean(dkh * ckh, axis=-1, keepdims=True)))
        emit(Q_RANK + KV_RANK, dkpe)

    acc_shapes = [wuq.shape, wk.shape, wv.shape, g_q.shape, g_kv.shape, (1, LAT_W)]
    return pl.pallas_call(
        body, name="lat_bwd", grid=(s // ts,),
        in_specs=[pl.BlockSpec((qk_w, ts), lambda i: (0, i)), _row_spec(ts, qk_w), _row_spec(ts, MLA_W),
                  _row_spec(ts, LAT_W), _row_spec(ts, HEAD_PAD), _row_spec(ts, HEAD_PAD), _row_spec(ts, HEAD_PAD),
                  _full_spec(g_q.shape), _full_spec(g_kv.shape), _full_spec(wuq.shape), _full_spec(wk.shape),
                  _full_spec(wv.shape)],
        out_specs=[_row_spec(ts, LAT_W), pl.BlockSpec((LAT_W, ts), lambda i: (0, i))]
        + [_full_spec(sh) for sh in acc_shapes],
        out_shape=[jax.ShapeDtypeStruct((s, LAT_W), BF16), jax.ShapeDtypeStruct((LAT_W, s), BF16)]
        + [jax.ShapeDtypeStruct(sh, F32) for sh in acc_shapes],
        compiler_params=_params(),
    )(dq, dk, dv, hl, rc, rsl, rsh, g_q, g_kv, wuq, wk, wv)


def _dx(dr, dhg, dhm, dhl, wt, after):
    s = dr.shape[0]
    ts = ROW_TILE

    def body(dr_ref, dhg_ref, dhm_ref, dhl_ref, wt_ref, after_ref, dx_ref):
        dx_ref[...] = (ALPHA * dr_ref[...]
                       + _dot(dhg_ref[...], wt_ref[ROW_GATE:IN_W, :])
                       + _dot(dhm_ref[...], wt_ref[LAT_COLS:ROW_GATE, :])
                       + _dot(dhl_ref[:, 0:Q_RANK + KV_RANK], wt_ref[0:Q_RANK + KV_RANK, :])
                       + _dot(dhl_ref[:, Q_RANK + KV_RANK:], _kpe_rows(wt_ref)))

    return pl.pallas_call(
        body, name="dx", grid=(s // ts,),
        in_specs=[_row_spec(ts, D_MODEL), _row_spec(ts, GATE_W), _row_spec(ts, MID_W), _row_spec(ts, LAT_W),
                  _full_spec(wt.shape), pl.BlockSpec(memory_space=pl.ANY)],
        out_specs=_row_spec(ts, D_MODEL),
        out_shape=jax.ShapeDtypeStruct((s, D_MODEL), F32),
        compiler_params=_params(),
    )(dr, dhg, dhm, dhl, wt, after)


def _dwt_early(dhmt, dhgt, xb):
    tn = 512
    nm, ng = MID_W // tn, GATE_W // tn
    s = dhmt.shape[1]

    def body(dm_ref, dg_ref, xb_ref, dw_ref):
        i = pl.program_id(0)

        @pl.when(i < nm)
        def _():
            dw_ref[...] = _dot(dm_ref[...], xb_ref[...]).astype(BF16)

        @pl.when(i >= nm)
        def _():
            dw_ref[...] = _dot(dg_ref[...], xb_ref[...]).astype(BF16)

    rows = pl.pallas_call(
        body, name="dwt_early", grid=(nm + ng,),
        in_specs=[pl.BlockSpec((tn, s), lambda i: (jnp.minimum(i, nm - 1), 0)),
                  pl.BlockSpec((tn, s), lambda i: (jnp.maximum(i - nm, 0), 0)), _full_spec(xb.shape)],
        out_specs=pl.BlockSpec((pl.Element(tn), pl.Element(D_MODEL)), lambda i: (pl.multiple_of(LAT_COLS + i * tn, 32), 0)),
        out_shape=jax.ShapeDtypeStruct((IN_W, D_MODEL), BF16),
        compiler_params=_params(),
    )(dhmt, dhgt, xb)

    def zero(buf_ref, out_ref):
        out_ref[...] = jnp.zeros_like(out_ref)

    return pl.pallas_call(
        zero, name="dwt_zero_lat", grid=(1,), in_specs=[pl.BlockSpec(memory_space=pl.ANY)],
        out_specs=pl.BlockSpec((LAT_COLS, D_MODEL), lambda i: (0, 0)),
        out_shape=jax.ShapeDtypeStruct((IN_W, D_MODEL), BF16), input_output_aliases={0: 0},
    )(rows)


def _dwt_lat(dhlt, xb):
    n, s = dhlt.shape

    def body(dht_ref, xb_ref, dw_ref):
        dw = _dot(dht_ref[...], xb_ref[...]).astype(BF16)
        kpe = Q_RANK + KV_RANK + NOPE
        dw_ref[0:Q_RANK + KV_RANK, :] = dw[0:Q_RANK + KV_RANK]
        dw_ref[Q_RANK + KV_RANK:LAT_COLS, :] = dw[kpe:kpe + ROPE]
        dw_ref[LAT_COLS:, :] = jnp.zeros((LAT_ROWS_PAD - LAT_COLS, D_MODEL), BF16)

    return pl.pallas_call(
        body, name="dwt_lat", in_specs=[VMEM_SPEC, VMEM_SPEC], out_specs=VMEM_SPEC,
        out_shape=jax.ShapeDtypeStruct((LAT_ROWS_PAD, D_MODEL), BF16),
        compiler_params=pltpu.CompilerParams(vmem_limit_bytes=VMEM_LIMIT),
    )(dhlt, xb)


def _split_bias(b):
    z = lambda n: jnp.zeros((n,), b.dtype)
    lat = jnp.concatenate([b[:Q_RANK + KV_RANK], z(NOPE), b[Q_RANK + KV_RANK:LAT_COLS], z(HEAD_PAD - QK_DIM)])
    return b[None, ROW_GATE:], b[None, LAT_COLS:ROW_GATE], lat[None, :]


def _join_bias(g, m, l):
    kpe = Q_RANK + KV_RANK + NOPE
    return jnp.concatenate([l[0, :Q_RANK + KV_RANK], l[0, kpe:kpe + ROPE], m[0], g[0]])


def _rope_tables(positions):
    half = ROPE // 2
    inv_freq = ROPE_THETA ** (-jnp.arange(0, ROPE, 2, dtype=F32) / ROPE)
    ang = positions.astype(F32)[:, None] * inv_freq
    cos, sin = jnp.cos(ang), jnp.sin(ang)
    n = positions.shape[0]
    one, zero = jnp.ones((n, NOPE), F32), jnp.zeros((n, half), F32)
    tail1, tail0 = jnp.ones((n, HEAD_PAD - QK_DIM), F32), jnp.zeros((n, HEAD_PAD - QK_DIM), F32)
    z64 = jnp.zeros((n, NOPE), F32)
    rc = jnp.concatenate([one, cos, cos, tail1], axis=1)
    rsl = jnp.concatenate([z64, -sin, zero, tail0], axis=1)
    rsh = jnp.concatenate([z64, zero, sin, tail0], axis=1)
    return rc, rsl, rsh


def _local_attention(x, positions, wlat, b_in, g_q, w_uq, g_kv, w_ukv, after):
    rc, rsl, rsh = _rope_tables(positions)
    b_g, b_m, b_l = _split_bias(b_in)
    wuq = jnp.pad(w_uq, ((0, 0), (0, 0), (0, HEAD_PAD - QK_DIM))).reshape(Q_RANK, HEADS * HEAD_PAD).astype(BF16)
    wk = jnp.pad(w_ukv[:, :, :NOPE], ((0, 0), (0, 0), (0, HEAD_PAD - NOPE))).reshape(KV_RANK, HEADS * HEAD_PAD).astype(BF16)
    wv = w_ukv[:, :, NOPE:].reshape(KV_RANK, MLA_W).astype(BF16)
    gq2, gkv2 = g_q[None, :], g_kv[None, :]
    hl, q, k, v, xb, qt, kt, vt = _fwd_lat(x, wlat, b_l, gq2, wuq, gkv2, wk, wv, rc, rsl, rsh, after)
    o, lse = _attn_fwd(qt, k, vt)
    return dict(q=q, qt=qt, k=k, kt=kt, v=v, o=o, lse=lse, hl=hl, rc=rc, rsl=rsl, rsh=rsh, gq2=gq2, gkv2=gkv2,
                wuq=wuq, wk=wk, wv=wv, xb=xb, b_g=b_g, b_m=b_m)


def _local_head(st, x, tgt, wt, w_oa, sg_g, sg_b, w_s, b_s, w_ob, w_out, ln_g, ln_b):
    q, qt, k, kt, v, o, lse, hl, xb = (st[n] for n in ("q", "qt", "k", "kt", "v", "o", "lse", "hl", "xb"))
    rc, rsl, rsh, gq2, gkv2, wuq, wk, wv = (st[n] for n in ("rc", "rsl", "rsh", "gq2", "gkv2", "wuq", "wk", "wv"))
    bsb = jnp.repeat(b_s.T, V_DIM, axis=1)
    hg, hm = _fwd_rest(xb, wt, st["b_g"], st["b_m"])
    (dr, dhg, dhm, do, dot, delta, dhgt, dhmt, dwout, dwoa, dwob, dws, dbs, dlng, dlnb, dsgg, dsgb, loss, dbg,
     dbm) = _mid(x, tgt, o, hm, hg, w_oa, w_ob, w_out, ln_g[None, :], ln_b[None, :], sg_g[None, :], sg_b[None, :],
                 w_s, bsb)
    delta = jnp.pad(delta.reshape(HEADS // 2, 2, -1), ((0, 0), (0, 6), (0, 0)))
    early = {
        "w_in": _dwt_early(dhmt, dhgt, xb),
        "w_oa": dwoa, "sgu_ln_g": dsgg[0], "sgu_ln_b": dsgb[0], "w_s": dws, "b_s": dbs[:, :GROUPS].T,
        "w_ob": dwob, "w_out": dwout, "ln_g": dlng[0], "ln_b": dlnb[0],
    }
    state = dict(q=q, qt=qt, k=k, kt=kt, v=v, do=do, dot=dot, lse=lse, delta=delta, hl=hl, rc=rc, rsl=rsl, rsh=rsh,
                 gq2=gq2, gkv2=gkv2, wuq=wuq, wk=wk, wv=wv, dr=dr, dhg=dhg, dhm=dhm, wt=wt, xb=xb, dbg=dbg, dbm=dbm)
    return loss, early, state


def _local_tail(st):
    dq, dk, dv = _attn_bwd(st["q"], st["qt"], st["k"], st["kt"], st["v"], st["do"], st["dot"], st["lse"], st["delta"])
    dhl, dhlt, dwuq, dwk, dwv, dgq, dgkv, dbl = _lat_bwd(dq, dk, dv, st["hl"], st["rc"], st["rsl"], st["rsh"],
                                                         st["gq2"], st["gkv2"], st["wuq"], st["wk"], st["wv"])
    late = {
        "w_lat": _dwt_lat(dhlt, st["xb"]),
        "b_in": _join_bias(st["dbg"], st["dbm"], dbl),
        "g_q": dgq[0],
        "w_uq": dwuq.reshape(Q_RANK, HEADS, HEAD_PAD)[:, :, :QK_DIM],
        "g_kv": dgkv[0],
        "w_ukv": jnp.concatenate([dwk.reshape(KV_RANK, HEADS, HEAD_PAD)[:, :, :NOPE],
                                  dwv.reshape(KV_RANK, HEADS, V_DIM)], axis=2),
    }
    return dhl, late


def _local_step(x, positions, tgt, wt, b_in, g_q, w_uq, g_kv, w_ukv, w_oa, sg_g, sg_b, w_s, b_s, w_ob, w_out, ln_g,
                ln_b):
    wlat = jnp.broadcast_to(wt[None, :LAT_COLS], (N_SLABS, LAT_COLS, D_MODEL))
    st = _local_attention(x, positions, wlat, b_in, g_q, w_uq, g_kv, w_ukv, b_in)
    loss, early, st = _local_head(st, x, tgt, wt, w_oa, sg_g, sg_b, w_s, b_s, w_ob, w_out, ln_g, ln_b)
    dhl, late = _local_tail(st)
    dx = _dx(st["dr"], st["dhg"], st["dhm"], dhl, st["wt"], dhl)
    grads = {**early, **late}
    grads["w_in"] = jnp.concatenate([grads.pop("w_lat")[:LAT_COLS], early["w_in"][LAT_COLS:]], axis=0)
    return loss, dx, grads


MESH = pl.DeviceIdType.MESH
N_CHIPS = 4
HBM_SPEC = pl.BlockSpec(memory_space=pl.ANY)
HBM_SPEC_STRICT = pl.BlockSpec(memory_space=pltpu.HBM)
VMEM_SPEC = pl.BlockSpec(memory_space=pltpu.VMEM)

REP_ROWS = 80


def _rows8(a):
    flat = a.reshape(-1)
    n = -(-flat.shape[0] // (8 * D_MODEL)) * 8 * D_MODEL
    return jnp.pad(flat, (0, n - flat.shape[0])).reshape(-1, D_MODEL)


def _place():
    x, y, c = lax.axis_index("x"), lax.axis_index("y"), lax.axis_index("c")
    others = [(1 - x, y), (x, 1 - y), (1 - x, 1 - y)]
    return x, y, c, others


def _gather_weights(shards):
    n = len(shards)

    def body(*refs):
        ins, outs, bufs = refs[:n], refs[n:2 * n], refs[2 * n:3 * n]
        send_sems, recv_sems, local_sems = refs[3 * n:]
        x, y, c, others = _place()
        me = 2 * x + y
        sibling = (x, y, 1 - c)
        for src, buf in zip(ins, bufs):
            buf[...] = src[...].astype(BF16)
        own = [pltpu.make_async_copy(bufs[w], outs[w].at[me], local_sems.at[w]) for w in range(n)]
        for cp in own:
            cp.start()

        def part(w, chip, half):
            hc = shards[w].shape[1] // 2
            return outs[w].at[chip, :, pl.ds(half * hc, hc)]

        def sent(w, j):
            hc = shards[w].shape[1] // 2
            return pltpu.make_async_remote_copy(
                src_ref=bufs[w].at[:, pl.ds(c * hc, hc)], dst_ref=part(w, me, c),
                send_sem=send_sems.at[w * 3 + j], recv_sem=recv_sems.at[w * 3 + j],
                device_id=(*others[j], c), device_id_type=MESH)

        def landed(w, j):
            px, py = others[j]
            return pltpu.make_async_remote_copy(
                src_ref=part(w, 2 * px + py, c), dst_ref=part(w, 2 * px + py, c),
                send_sem=send_sems.at[w * 3 + j], recv_sem=recv_sems.at[w * 3 + j],
                device_id=(px, py, c), device_id_type=MESH)

        def passed(w, j, half):
            px, py = others[j]
            k = n * 3 + w * 3 + j
            return pltpu.make_async_remote_copy(
                src_ref=part(w, 2 * px + py, half), dst_ref=part(w, 2 * px + py, half),
                send_sem=send_sems.at[k], recv_sem=recv_sems.at[k], device_id=sibling, device_id_type=MESH)

        first = [sent(w, j) for w in range(n) for j in range(3)]
        for cp in first:
            cp.start()
        fwd = []
        for w in range(n):
            for j in range(3):
                landed(w, j).wait_recv()
                cp = passed(w, j, c)
                cp.start()
                fwd.append(cp)
        for w in range(n):
            for j in range(3):
                passed(w, j, 1 - c).wait_recv()
        for cp in first + fwd:
            cp.wait_send()
        for cp in own:
            cp.wait()

    return pl.pallas_call(
        body, name="gather_weights",
        in_specs=[VMEM_SPEC] * n, out_specs=[HBM_SPEC] * n,
        out_shape=[jax.ShapeDtypeStruct((N_CHIPS,) + s.shape, BF16) for s in shards],
        scratch_shapes=[pltpu.VMEM(s.shape, BF16) for s in shards]
        + [pltpu.SemaphoreType.DMA((6 * n,)), pltpu.SemaphoreType.DMA((6 * n,)), pltpu.SemaphoreType.DMA((n,))],
        compiler_params=pltpu.CompilerParams(vmem_limit_bytes=VMEM_LIMIT),
    )(*shards)


N_DEV = 8
LOSS_TILE = (8, 128)


def _cast_own(shards, me):
    n = len(shards)

    def body(me_ref, *refs):
        for w in range(n):
            refs[n + w][...] = refs[w][...].astype(BF16)

    return pl.pallas_call(
        body, name="cast_own",
        grid_spec=pltpu.PrefetchScalarGridSpec(
            num_scalar_prefetch=1, grid=(1,),
            in_specs=[pl.BlockSpec(s.shape, lambda i, me_ref: (0, 0)) for s in shards],
            out_specs=[pl.BlockSpec((None,) + s.shape, lambda i, me_ref: (me_ref[0], 0, 0)) for s in shards]),
        out_shape=[jax.ShapeDtypeStruct((N_CHIPS,) + s.shape, BF16) for s in shards],
        compiler_params=pltpu.CompilerParams(vmem_limit_bytes=VMEM_LIMIT),
    )(me, *shards)


def _gather_start(bufs, after):
    n = len(bufs)

    def body(*refs):
        b_refs = refs[:n]
        send_sems, recv_sems, token = refs[n + 1], refs[n + 2], refs[-1]
        x, y, c, others = _place()
        me = 2 * x + y
        for w in range(n):
            hc = _half(bufs[w])
            mine = b_refs[w].at[me, :, pl.ds(c * hc, hc)]
            for j, (px, py) in enumerate(others):
                pltpu.make_async_remote_copy(
                    src_ref=mine, dst_ref=mine, send_sem=send_sems.at[3 * w + j], recv_sem=recv_sems.at[3 * w + j],
                    device_id=(px, py, c), device_id_type=MESH).start()
        token[...] = jnp.zeros_like(token)

    hbm = [pltpu.HBM(b.shape, BF16) for b in bufs]
    outs = pl.pallas_call(
        body, name="gather_start",
        out_shape=(pltpu.SemaphoreType.DMA((3 * n,)), pltpu.SemaphoreType.DMA((3 * n,)), *hbm,
                   jax.ShapeDtypeStruct(LOSS_TILE, F32)),
        in_specs=[HBM_SPEC_STRICT] * n + [HBM_SPEC],
        out_specs=(SEM_SPEC, SEM_SPEC, *[HBM_SPEC_STRICT] * n, VMEM_SPEC),
        input_output_aliases={i: 2 + i for i in range(n)},
        compiler_params=pltpu.CompilerParams(has_side_effects=SPLIT_EFFECT),
    )(*[pltpu.with_memory_space_constraint(b, pltpu.HBM) for b in bufs], after)
    return outs[0], outs[1], list(outs[2:2 + n]), outs[-1]


def _gather_wait(send_sems, recv_sems, bufs, after):
    n = len(bufs)

    def body(*refs):
        b_refs = refs[:n]
        send_sems, recv_sems = refs[n], refs[n + 1]
        x, y, c, others = _place()
        me = 2 * x + y
        for w in range(n):
            hc = _half(bufs[w])
            for j, (px, py) in enumerate(others):
                cp = pltpu.make_async_remote_copy(
                    src_ref=b_refs[w].at[me, :, pl.ds(c * hc, hc)],
                    dst_ref=b_refs[w].at[2 * px + py, :, pl.ds(c * hc, hc)],
                    send_sem=send_sems.at[3 * w + j], recv_sem=recv_sems.at[3 * w + j], device_id=(px, py, c),
                    device_id_type=MESH)
                cp.wait_send()
                cp.wait_recv()

    outs = pl.pallas_call(
        body, name="gather_wait", out_shape=tuple(pltpu.HBM(b.shape, b.dtype) for b in bufs),
        in_specs=[HBM_SPEC_STRICT] * n + [SEM_SPEC, SEM_SPEC, HBM_SPEC],
        out_specs=tuple([HBM_SPEC_STRICT] * n), input_output_aliases={i: i for i in range(n)},
        compiler_params=pltpu.CompilerParams(has_side_effects=SPLIT_EFFECT),
    )(*bufs, send_sems, recv_sems, after)
    return list(outs)


def _gather_finish(bufs):
    n = len(bufs)

    def body(*refs):
        b_refs = refs[n:2 * n]
        send_sems, recv_sems = refs[2 * n:]
        x, y, c, others = _place()
        cps = []
        for w in range(n):
            hc = _half(bufs[w])
            for j, (px, py) in enumerate(others):
                part = b_refs[w].at[2 * px + py, :, pl.ds(c * hc, hc)]
                cps.append(pltpu.make_async_remote_copy(
                    src_ref=part, dst_ref=part, send_sem=send_sems.at[3 * w + j], recv_sem=recv_sems.at[3 * w + j],
                    device_id=(x, y, 1 - c), device_id_type=MESH))
        for cp in cps:
            cp.start()
        for w in range(n):
            hc = _half(bufs[w])
            for j, (px, py) in enumerate(others):
                theirs = b_refs[w].at[2 * px + py, :, pl.ds((1 - c) * hc, hc)]
                pltpu.make_async_remote_copy(
                    src_ref=theirs, dst_ref=theirs, send_sem=send_sems.at[3 * w + j], recv_sem=recv_sems.at[3 * w + j],
                    device_id=(x, y, 1 - c), device_id_type=MESH).wait_recv()
        for cp in cps:
            cp.wait_send()

    return pl.pallas_call(
        body, name="gather_finish", in_specs=[HBM_SPEC] * n, out_specs=[HBM_SPEC] * n,
        out_shape=[jax.ShapeDtypeStruct(b.shape, b.dtype) for b in bufs],
        input_output_aliases={i: i for i in range(n)},
        scratch_shapes=[pltpu.SemaphoreType.DMA((3 * n,)), pltpu.SemaphoreType.DMA((3 * n,))],
    )(*bufs)


def _half(a):
    return a.shape[-1] // 2


def _exchange_pairs(parts, name):
    n = len(parts)

    def body(*refs):
        p_refs, r_refs = refs[:n], refs[n:2 * n]
        send_sems, recv_sems = refs[2 * n:]
        x, y, c, _ = _place()
        cps = []
        for w in range(n):
            h = _half(parts[w])
            cps.append(pltpu.make_async_remote_copy(
                src_ref=p_refs[w].at[:, :, pl.ds((1 - c) * h, h)], dst_ref=r_refs[w],
                send_sem=send_sems.at[w], recv_sem=recv_sems.at[w], device_id=(x, y, 1 - c), device_id_type=MESH))
        for cp in cps:
            cp.start()
        for cp in cps:
            cp.wait()

    return pl.pallas_call(
        body, name=name, in_specs=[HBM_SPEC] * n, out_specs=[HBM_SPEC] * n,
        out_shape=[jax.ShapeDtypeStruct((N_CHIPS, p.shape[1], _half(p)), BF16) for p in parts],
        scratch_shapes=[pltpu.SemaphoreType.DMA((n,)), pltpu.SemaphoreType.DMA((n,))],
    )(*parts)


def _exchange_pairs_and_loss(parts, loss):
    n = len(parts)

    def body(*refs):
        p_refs, loss_ref = refs[:n], refs[n]
        r_refs, all_loss_ref = refs[n + 1:2 * n + 1], refs[2 * n + 1]
        send_sems, recv_sems, loss_send, loss_recv, local_sem = refs[2 * n + 2:]
        x, y, c, _ = _place()
        sibling = (x, y, 1 - c)
        cps = []
        for w in range(n):
            h = _half(parts[w])
            cps.append(pltpu.make_async_remote_copy(
                src_ref=p_refs[w].at[:, :, pl.ds((1 - c) * h, h)], dst_ref=r_refs[w],
                send_sem=send_sems.at[w], recv_sem=recv_sems.at[w], device_id=sibling, device_id_type=MESH))
        for cp in cps:
            cp.start()
        me = 4 * x + 2 * y + c
        own = pltpu.make_async_copy(loss_ref, all_loss_ref.at[me], local_sem)
        own.start()
        lcs = []
        for t in range(1, N_DEV):
            d = (me + t) % N_DEV
            lcs.append(pltpu.make_async_remote_copy(
                src_ref=loss_ref, dst_ref=all_loss_ref.at[me], send_sem=loss_send.at[t - 1],
                recv_sem=loss_recv.at[t - 1], device_id=(d // 4, (d // 2) % 2, d % 2), device_id_type=MESH))
        for cp in lcs:
            cp.start()
        for t in range(1, N_DEV):
            d = (me + N_DEV - t) % N_DEV
            pltpu.make_async_remote_copy(
                src_ref=loss_ref, dst_ref=all_loss_ref.at[d], send_sem=loss_send.at[t - 1],
                recv_sem=loss_recv.at[t - 1], device_id=(d // 4, (d // 2) % 2, d % 2), device_id_type=MESH).wait_recv()
        for cp in lcs:
            cp.wait_send()
        for cp in cps:
            cp.wait()
        own.wait()

    return pl.pallas_call(
        body, name="exchange_pairs_and_loss", in_specs=[HBM_SPEC] * (n + 1), out_specs=[HBM_SPEC] * (n + 1),
        out_shape=[jax.ShapeDtypeStruct((N_CHIPS, p.shape[1], _half(p)), BF16) for p in parts]
        + [jax.ShapeDtypeStruct((N_DEV,) + LOSS_TILE, F32)],
        scratch_shapes=[pltpu.SemaphoreType.DMA((n,)), pltpu.SemaphoreType.DMA((n,)),
                        pltpu.SemaphoreType.DMA((N_DEV - 1,)), pltpu.SemaphoreType.DMA((N_DEV - 1,)),
                        pltpu.SemaphoreType.DMA],
    )(*parts, loss)


def _add_pair_tiled(p, r, c):
    rows, h = r.shape[1:]

    def body(c_ref, p_ref, r_ref, q_ref):
        q_ref[...] = (p_ref[...].astype(F32) + r_ref[...].astype(F32)).astype(BF16)

    return pl.pallas_call(
        body, name="add_pair_w_in",
        grid_spec=pltpu.PrefetchScalarGridSpec(
            num_scalar_prefetch=1, grid=(N_CHIPS,),
            in_specs=[pl.BlockSpec((None, rows, h), lambda k, c_ref: (k, 0, c_ref[0])),
                      pl.BlockSpec((None, rows, h), lambda k, c_ref: (k, 0, 0))],
            out_specs=pl.BlockSpec((None, rows, h), lambda k, c_ref: (k, 0, 0))),
        out_shape=jax.ShapeDtypeStruct(r.shape, BF16),
    )(c, p, r)


def _add_pair_small(ps, rs, c, name):
    n = len(ps)

    def body(c_ref, *refs):
        for w in range(n):
            h = _half(ps[w])
            mine = refs[w][:, :, pl.ds(pl.multiple_of(c_ref[0] * h, 128), h)]
            refs[2 * n + w][...] = (mine.astype(F32) + refs[n + w][...].astype(F32)).astype(BF16)

    return pl.pallas_call(
        body, name=name,
        in_specs=[pl.BlockSpec(memory_space=pltpu.SMEM)] + [VMEM_SPEC] * (2 * n), out_specs=[VMEM_SPEC] * n,
        out_shape=[jax.ShapeDtypeStruct(r.shape, BF16) for r in rs],
        compiler_params=pltpu.CompilerParams(vmem_limit_bytes=VMEM_LIMIT),
    )(c, *ps, *rs)


def _exchange_chips(qs):
    n = len(qs)

    def body(*refs):
        q_refs, r_refs = refs[:n], refs[n:2 * n]
        send_sems, recv_sems = refs[2 * n:]
        x, y, c, others = _place()
        me = 2 * x + y
        cps = []
        for w in range(n):
            for j, (px, py) in enumerate(others):
                cps.append(pltpu.make_async_remote_copy(
                    src_ref=q_refs[w].at[2 * px + py], dst_ref=r_refs[w].at[me], send_sem=send_sems.at[3 * w + j],
                    recv_sem=recv_sems.at[3 * w + j], device_id=(px, py, c), device_id_type=MESH))
        for cp in cps:
            cp.start()
        for w in range(n):
            for j, (px, py) in enumerate(others):
                pltpu.make_async_remote_copy(
                    src_ref=q_refs[w].at[me], dst_ref=r_refs[w].at[2 * px + py], send_sem=send_sems.at[3 * w + j],
                    recv_sem=recv_sems.at[3 * w + j], device_id=(px, py, c), device_id_type=MESH).wait_recv()
        for cp in cps:
            cp.wait_send()

    return pl.pallas_call(
        body, name="exchange_chips", in_specs=[HBM_SPEC] * n, out_specs=[HBM_SPEC] * n,
        out_shape=[jax.ShapeDtypeStruct(q.shape, BF16) for q in qs],
        scratch_shapes=[pltpu.SemaphoreType.DMA((3 * n,)), pltpu.SemaphoreType.DMA((3 * n,))],
    )(*qs)


SEM_SPEC = pl.BlockSpec(memory_space=pltpu.SEMAPHORE)
SPLIT_EFFECT = pltpu.SideEffectType.DATAFLOW_SIDE_EFFECTING


def _chips_start(qs, name):
    n = len(qs)

    def body(*refs):
        q_refs, land_refs = refs[:n], refs[n:2 * n]
        send_sems, recv_sems, token = refs[2 * n], refs[2 * n + 1], refs[-1]
        x, y, c, others = _place()
        me = 2 * x + y
        for w in range(n):
            for j, (px, py) in enumerate(others):
                pltpu.make_async_remote_copy(
                    src_ref=q_refs[w].at[2 * px + py], dst_ref=land_refs[w].at[me], send_sem=send_sems.at[3 * w + j],
                    recv_sem=recv_sems.at[3 * w + j], device_id=(px, py, c), device_id_type=MESH).start()
        token[...] = jnp.zeros_like(token)

    hbm = [pltpu.HBM(q.shape, BF16) for q in qs]
    outs = pl.pallas_call(
        body, name=name,
        out_shape=(pltpu.SemaphoreType.DMA((3 * n,)), pltpu.SemaphoreType.DMA((3 * n,)), *hbm, *hbm,
                   jax.ShapeDtypeStruct(LOSS_TILE, F32)),
        in_specs=[HBM_SPEC_STRICT] * (2 * n),
        out_specs=(SEM_SPEC, SEM_SPEC, *[HBM_SPEC_STRICT] * (2 * n), VMEM_SPEC),
        input_output_aliases={i: 2 + i for i in range(2 * n)},
        compiler_params=pltpu.CompilerParams(has_side_effects=SPLIT_EFFECT),
    )(*[pltpu.with_memory_space_constraint(q, pltpu.HBM) for q in qs],
      *[pltpu.with_memory_space_constraint(lax.empty(q.shape, BF16), pltpu.HBM) for q in qs])
    return outs[0], outs[1], outs[2:2 + n], outs[2 + n:2 + 2 * n], outs[-1]


def _chips_wait(send_sems, recv_sems, q_thru, land_thru, after, name):
    n = len(q_thru)

    def body(*refs):
        q_refs, land_refs = refs[:n], refs[n:2 * n]
        send_sems, recv_sems = refs[2 * n], refs[2 * n + 1]
        x, y, c, others = _place()
        me = 2 * x + y
        for w in range(n):
            for j, (px, py) in enumerate(others):
                cp = pltpu.make_async_remote_copy(
                    src_ref=q_refs[w].at[2 * px + py], dst_ref=land_refs[w].at[2 * px + py],
                    send_sem=send_sems.at[3 * w + j], recv_sem=recv_sems.at[3 * w + j], device_id=(px, py, c),
                    device_id_type=MESH)
                cp.wait_send()
                cp.wait_recv()

    outs = pl.pallas_call(
        body, name=name, out_shape=tuple(pltpu.HBM(a.shape, a.dtype) for a in (*q_thru, *land_thru)),
        in_specs=[HBM_SPEC_STRICT] * (2 * n) + [SEM_SPEC, SEM_SPEC, HBM_SPEC],
        out_specs=tuple([HBM_SPEC_STRICT] * (2 * n)), input_output_aliases={i: i for i in range(2 * n)},
        compiler_params=pltpu.CompilerParams(has_side_effects=SPLIT_EFFECT),
    )(*q_thru, *land_thru, send_sems, recv_sems, after)
    return list(outs[:n]), list(outs[n:])


def _sum_chips_tiled(q, r, idx, tile):
    rows, h = r.shape[1:]
    nt = h // tile

    def body(idx_ref, q_ref, r0_ref, r1_ref, r2_ref, g_ref):
        g_ref[...] = (q_ref[...].astype(F32) + r0_ref[...].astype(F32) + r1_ref[...].astype(F32)
                      + r2_ref[...].astype(F32))

    def slab(t):
        return pl.BlockSpec((None, rows, tile), lambda i, idx_ref: (idx_ref[t], 0, i))

    return pl.pallas_call(
        body, name="sum_chips_w_in",
        grid_spec=pltpu.PrefetchScalarGridSpec(
            num_scalar_prefetch=1, grid=(nt,), in_specs=[slab(0), slab(1), slab(2), slab(3)],
            out_specs=pl.BlockSpec((rows, tile), lambda i, idx_ref: (0, idx_ref[4] * nt + i))),
        out_shape=jax.ShapeDtypeStruct((rows, 2 * h), F32),
    )(idx, q, r, r, r)


def _sum_chips_small(qs, rs, idx, n_all):
    n = len(rs)

    def body(idx_ref, *refs):
        c = idx_ref[4]
        for w in range(n):
            q_ref, r_ref, g_ref = refs[w], refs[n + w], refs[2 * n + w]
            acc = q_ref[idx_ref[0]].astype(F32)
            for t in range(1, N_CHIPS):
                acc = acc + r_ref[idx_ref[t]].astype(F32)
            h = rs[w].shape[2]
            mine = pl.ds(pl.multiple_of(c * h, 128), h)
            g_ref[...] = jnp.zeros_like(g_ref)
            if w >= n - n_all:
                g_ref[idx_ref[0], :, mine] = acc
            else:
                g_ref[:, mine] = acc

    shapes = [jax.ShapeDtypeStruct((r.shape[1], 2 * r.shape[2]), F32) for r in rs[:n - n_all]]
    shapes += [jax.ShapeDtypeStruct((N_CHIPS, r.shape[1], 2 * r.shape[2]), F32) for r in rs[n - n_all:]]
    return pl.pallas_call(
        body, name="sum_chips_small",
        in_specs=[pl.BlockSpec(memory_space=pltpu.SMEM)] + [VMEM_SPEC] * (2 * n), out_specs=[VMEM_SPEC] * n,
        out_shape=shapes, compiler_params=pltpu.CompilerParams(vmem_limit_bytes=VMEM_LIMIT),
    )(idx, *qs, *rs)


def _share(shards, alls):
    n, na = len(shards), len(alls)
    total = n + na

    def body(*refs):
        g_refs, a_refs = refs[total:total + n], refs[total + n:2 * total]
        send_sems, recv_sems = refs[2 * total:]
        x, y, c, others = _place()
        me = 2 * x + y
        sibling = (x, y, 1 - c)

        def cols_of(w, half):
            h = shards[w].shape[1] // 2
            return g_refs[w].at[:, pl.ds(half * h, h)]

        def slab(a, chip, half):
            h = alls[a].shape[2] // 2
            return a_refs[a].at[chip, :, pl.ds(half * h, h)]

        def copy(src, dst, k, to):
            return pltpu.make_async_remote_copy(src_ref=src, dst_ref=dst, send_sem=send_sems.at[k],
                                                recv_sem=recv_sems.at[k], device_id=to, device_id_type=MESH)

        cps = [copy(cols_of(w, c), cols_of(w, c), w, sibling) for w in range(n)]
        for a in range(na):
            base = n + 7 * a
            cps.append(copy(slab(a, me, c), slab(a, me, c), base, sibling))
            for j, (px, py) in enumerate(others):
                cps.append(copy(slab(a, me, c), slab(a, me, c), base + 1 + j, (px, py, c)))
        for cp in cps:
            cp.start()
        fwd = []
        for a in range(na):
            base = n + 7 * a
            for j, (px, py) in enumerate(others):
                chip = 2 * px + py
                copy(slab(a, me, c), slab(a, chip, c), base + 1 + j, (px, py, c)).wait_recv()
                cp = copy(slab(a, chip, c), slab(a, chip, c), base + 4 + j, sibling)
                cp.start()
                fwd.append(cp)
        for a in range(na):
            base = n + 7 * a
            for j, (px, py) in enumerate(others):
                chip = 2 * px + py
                copy(slab(a, chip, c), slab(a, chip, 1 - c), base + 4 + j, sibling).wait_recv()
            copy(slab(a, me, c), slab(a, me, 1 - c), base, sibling).wait_recv()
        for w in range(n):
            copy(cols_of(w, c), cols_of(w, 1 - c), w, sibling).wait_recv()
        for cp in cps + fwd:
            cp.wait_send()

    nsem = n + 7 * na
    return pl.pallas_call(
        body, name="share", in_specs=[HBM_SPEC] * total, out_specs=[HBM_SPEC] * total,
        out_shape=[jax.ShapeDtypeStruct(a.shape, F32) for a in (*shards, *alls)],
        input_output_aliases={i: i for i in range(total)},
        scratch_shapes=[pltpu.SemaphoreType.DMA((nsem,)), pltpu.SemaphoreType.DMA((nsem,))],
    )(*shards, *alls)


def _adamw(w, g, m, v):
    m2 = ADAM_B1 * m + (1.0 - ADAM_B1) * g
    v2 = ADAM_B2 * v + (1.0 - ADAM_B2) * (g * g)
    m_hat = m2 / (1.0 - ADAM_B1 ** ADAM_STEP)
    v_hat = v2 / (1.0 - ADAM_B2 ** ADAM_STEP)
    return -ADAM_LR * (m_hat / (jnp.sqrt(v_hat) + ADAM_EPS) + ADAM_WD * w), m2, v2


def _update_w_in(wt, gt, mt, vt, lat, owner, tile):
    nlat = lat.shape[0] // tile

    def body(owner_ref, w_ref, g_ref, m_ref, v_ref, lat_ref, g2_ref, d_ref, m2_ref, v2_ref):
        row = pl.program_id(0) * tile + lax.broadcasted_iota(jnp.int32, (tile, 1), 0)
        g = jnp.where((row < LAT_COLS) & (owner_ref[0] == 1), lat_ref[...], g_ref[...])
        g2_ref[...] = g
        d_ref[...], m2_ref[...], v2_ref[...] = _adamw(w_ref[...], g, m_ref[...], v_ref[...])

    spec = pl.BlockSpec((tile, wt.shape[1]), lambda i, o: (i, 0))
    return pl.pallas_call(
        body, name="update_w_in",
        grid_spec=pltpu.PrefetchScalarGridSpec(
            num_scalar_prefetch=1, grid=(wt.shape[0] // tile,),
            in_specs=[spec] * 4 + [pl.BlockSpec((tile, wt.shape[1]), lambda i, o: (jnp.minimum(i, nlat - 1), 0))],
            out_specs=[spec] * 4),
        out_shape=[jax.ShapeDtypeStruct(wt.shape, F32)] * 4,
        compiler_params=_params(("parallel",)),
    )(owner, wt, gt, mt, vt, lat)


def _update_small(ws, gs, ms, vs):
    n = len(ws)

    def body(*refs):
        for k in range(n):
            w_ref, g_ref, m_ref, v_ref = refs[k], refs[n + k], refs[2 * n + k], refs[3 * n + k]
            d, m2, v2 = _adamw(w_ref[...], g_ref[...], m_ref[...], v_ref[...])
            refs[4 * n + k][...] = d
            refs[5 * n + k][...] = m2
            refs[6 * n + k][...] = v2

    shapes = [jax.ShapeDtypeStruct(w.shape, F32) for w in ws]
    outs = pl.pallas_call(
        body, name="update_small", in_specs=[VMEM_SPEC] * (4 * n), out_specs=[VMEM_SPEC] * (3 * n),
        out_shape=shapes * 3,
        compiler_params=pltpu.CompilerParams(vmem_limit_bytes=VMEM_LIMIT),
    )(*ws, *gs, *ms, *vs)
    return outs[:n], outs[n:2 * n], outs[2 * n:]


SHARDED = ("w_in", "w_uq", "w_oa", "w_ob", "w_out")
REPLICATED = ("b_in", "g_q", "g_kv", "w_ukv", "sgu_ln_g", "sgu_ln_b", "w_s", "b_s", "ln_g", "ln_b")
ORDER = ("w_in", "b_in", "g_q", "w_uq", "g_kv", "w_ukv", "w_oa", "sgu_ln_g", "sgu_ln_b", "w_s", "b_s", "w_ob", "w_out",
         "ln_g", "ln_b")


def kernel(x, positions, w_in, b_in, g_q, w_uq, g_kv, w_ukv, w_oa, sgu_ln_g, sgu_ln_b, w_s, b_s, w_ob, w_out, ln_g, ln_b, loss_target, m_w_in, m_b_in, m_g_q, m_w_uq, m_g_kv, m_w_ukv, m_w_oa, m_sgu_ln_g, m_sgu_ln_b, m_w_s, m_b_s, m_w_ob, m_w_out, m_ln_g, m_ln_b, v_w_in, v_b_in, v_g_q, v_w_uq, v_g_kv, v_w_ukv, v_w_oa, v_sgu_ln_g, v_sgu_ln_b, v_w_s, v_b_s, v_w_ob, v_w_out, v_ln_g, v_ln_b):
    w = dict(w_in=w_in, b_in=b_in, g_q=g_q, w_uq=w_uq, g_kv=g_kv, w_ukv=w_ukv, w_oa=w_oa, sgu_ln_g=sgu_ln_g,
             sgu_ln_b=sgu_ln_b, w_s=w_s, b_s=b_s, w_ob=w_ob, w_out=w_out, ln_g=ln_g, ln_b=ln_b)
    m = dict(w_in=m_w_in, b_in=m_b_in, g_q=m_g_q, w_uq=m_w_uq, g_kv=m_g_kv, w_ukv=m_w_ukv, w_oa=m_w_oa,
             sgu_ln_g=m_sgu_ln_g, sgu_ln_b=m_sgu_ln_b, w_s=m_w_s, b_s=m_b_s, w_ob=m_w_ob, w_out=m_w_out, ln_g=m_ln_g,
             ln_b=m_ln_b)
    v = dict(w_in=v_w_in, b_in=v_b_in, g_q=v_g_q, w_uq=v_w_uq, g_kv=v_g_kv, w_ukv=v_w_ukv, w_oa=v_w_oa,
             sgu_ln_g=v_sgu_ln_g, sgu_ln_b=v_sgu_ln_b, w_s=v_w_s, b_s=v_b_s, w_ob=v_w_ob, w_out=v_w_out, ln_g=v_ln_g,
             ln_b=v_ln_b)
    w, m, v = ({n: a[0] for n, a in d.items()} for d in (w, m, v))
    c = lax.axis_index("c")

    wt_shard, mt_shard, vt_shard = (jnp.transpose(d["w_in"]) for d in (w, m, v))
    xi, yi = lax.axis_index("x"), lax.axis_index("y")
    me1 = (2 * xi + yi).reshape(1).astype(jnp.int32)
    g_lat, g_uq = _gather_weights([wt_shard[:LAT_COLS], w["w_uq"].reshape(Q_RANK // 4, HEADS * QK_DIM)])
    bufs = _cast_own([wt_shard, w["w_oa"], w["w_ob"], w["w_out"]], me1)
    send0, recv0, bufs, token0 = _gather_start(bufs, g_lat)
    st = _local_attention(x[0], positions[0], g_lat, w["b_in"], w["g_q"], g_uq.reshape(Q_RANK, HEADS, QK_DIM),
                          w["g_kv"], w["w_ukv"], token0)
    g_in, g_oa, g_ob, g_out = _gather_finish(_gather_wait(send0, recv0, bufs, st["o"]))
    wt = g_in.reshape(IN_W, D_MODEL)

    loss, early, st = _local_head(
        st, x[0], loss_target[0], wt, g_oa, w["sgu_ln_g"], w["sgu_ln_b"], w["w_s"], w["b_s"], g_ob,
        g_out.reshape(D_MODEL, D_MODEL), w["ln_g"], w["ln_b"])

    xi, yi = lax.axis_index("x"), lax.axis_index("y")
    c1 = c.reshape(1).astype(jnp.int32)
    idx = jnp.stack([2 * xi + yi, 2 * (1 - xi) + yi, 2 * xi + (1 - yi), 2 * (1 - xi) + (1 - yi), c]).astype(jnp.int32)
    parts1 = [early["w_in"].reshape(N_CHIPS, IN_W // N_CHIPS, D_MODEL), early["w_oa"].astype(BF16),
              early["w_ob"].astype(BF16), early["w_out"].reshape(N_CHIPS, SLAB_W, D_MODEL).astype(BF16)]
    *recv1, all_loss = _exchange_pairs_and_loss(parts1, jnp.broadcast_to(loss, LOSS_TILE))
    pairs1 = [_add_pair_tiled(parts1[0], recv1[0], c1),
              *_add_pair_small(parts1[1:], recv1[1:], c1, "add_pair_early")]
    sems1 = _chips_start(pairs1, "chips_start_early")

    st["delta"] = st["delta"] + sems1[4][0, 0]
    dhl, late = _local_tail(st)

    grads = {**early, **late}
    rep = jnp.concatenate([_rows8(grads[n]) for n in REPLICATED], axis=0)
    rep = jnp.pad(rep, ((0, N_CHIPS * REP_ROWS - rep.shape[0]), (0, 0))).reshape(N_CHIPS, REP_ROWS, D_MODEL)
    parts2 = [late["w_uq"].reshape(N_CHIPS, Q_RANK // N_CHIPS, HEADS * QK_DIM).astype(BF16), rep.astype(BF16),
              late["w_lat"].reshape(N_CHIPS, LAT_ROWS_PAD // N_CHIPS, D_MODEL)]
    pairs2 = _add_pair_small(parts2, _exchange_pairs(parts2, "exchange_pairs_late"), c1, "add_pair_late")
    sems2 = _chips_start(pairs2, "chips_start_late")
    dx = _dx(st["dr"], st["dhg"], st["dhm"], dhl, st["wt"], sems2[4])
    pairs2, landed2 = _chips_wait(*sems2[:4], dx, "chips_wait_late")
    pairs1, landed1 = _chips_wait(*sems1[:4], landed2[0], "chips_wait_early")
    sums = [_sum_chips_tiled(pairs1[0], landed1[0], idx, 128),
            *_sum_chips_small([*pairs1[1:], *pairs2], [*landed1[1:], *landed2], idx, 2)]
    *shards, g_rep, g_lat = _share(sums[:-2], sums[-2:])
    loss = jnp.sum(all_loss[:, 0, 0])

    red = {n: s.reshape(w[n].shape) for n, s in zip(("w_oa", "w_ob", "w_out", "w_uq"), shards[1:])}
    g_rep = g_rep.reshape(N_CHIPS * REP_ROWS, D_MODEL)
    off = 0
    for n in REPLICATED:
        rows = _rows8(w[n]).shape[0]
        red[n] = g_rep[off:off + rows].reshape(-1)[:w[n].size].reshape(w[n].shape)
        off += rows
    owner = (2 * xi + yi == 0).astype(jnp.int32).reshape(1)
    gt, dt, mt, vt2 = _update_w_in(wt_shard, shards[0], mt_shard, vt_shard, g_lat.reshape(LAT_ROWS_PAD, D_MODEL),
                                   owner, 232)
    red["w_in"] = jnp.transpose(gt)
    small = [n for n in ORDER if n != "w_in"]
    as2d = lambda a: a.reshape(-1, a.shape[-1])
    ds, ms, vs = _update_small([as2d(w[n]) for n in small], [as2d(red[n]) for n in small],
                               [as2d(m[n]) for n in small], [as2d(v[n]) for n in small])
    delta, new_m, new_v = {"w_in": jnp.transpose(dt)}, {"w_in": jnp.transpose(mt)}, {"w_in": jnp.transpose(vt2)}
    for i, n in enumerate(small):
        delta[n], new_m[n], new_v[n] = (a[i].reshape(w[n].shape) for a in (ds, ms, vs))

    lead = lambda a: a[None]
    return (loss, dx[None], *[lead(red[n]) for n in ORDER], *[lead(delta[n]) for n in ORDER],
            *[lead(new_m[n]) for n in ORDER], *[lead(new_v[n]) for n in ORDER])
```

```python
import functools
import math

import jax
import jax.numpy as jnp
from jax import lax
from jax.experimental import pallas as pl
from jax.experimental.pallas import tpu as pltpu

F32 = jnp.float32
BF16 = jnp.bfloat16

D_MODEL = 1024
HEADS = 8
Q_RANK = 384
KV_RANK = 128
NOPE = 64
ROPE = 32
V_DIM = 64
QK_DIM = NOPE + ROPE
HEAD_PAD = 128
MLA_W = HEADS * V_DIM
SGU_W = 512
GROUPS = 8
CHUNK = 128
IN_W = 4640
RMS_EPS = 1e-6
LN_EPS = 1e-5
ALPHA = 2.0 ** 0.25
ROPE_THETA = 10000.0
SCALE = QK_DIM ** -0.5

GATE_W = 2 * D_MODEL
MID_W = 4 * SGU_W
LAT_W = Q_RANK + KV_RANK + HEAD_PAD
PAD_W = GATE_W + MID_W + LAT_W
LAT_COLS = Q_RANK + KV_RANK + ROPE
ROW_GATE = LAT_COLS + MID_W
LAT_ROWS_PAD = 704
N_SLABS = 4
SLAB_W = D_MODEL // N_SLABS

ROW_TILE = 256
ATT_TQ = 256
ATT_TK = 256
ATT_BWD_TQ = 256
ATT_BWD_TK = 256
LOG2E = 1.4426950408889634
LN2 = 0.6931471805599453
Q_SCALE = SCALE * LOG2E
VMEM_LIMIT = 56 * 1024 * 1024

ADAM_LR = 0.001
ADAM_B1 = 0.9
ADAM_B2 = 0.999
ADAM_EPS = 1e-08
ADAM_WD = 0.01
ADAM_STEP = 10


def _dot(a, b):
    return jnp.dot(a, b, preferred_element_type=F32)


def _dot_nt(a, b):
    return lax.dot_general(a, b, (((1,), (1,)), ((), ())), preferred_element_type=F32)


def _dot_tn(a, b):
    return lax.dot_general(a, b, (((0,), (0,)), ((), ())), preferred_element_type=F32)


def _sigmoid(z):
    return 0.5 * jnp.tanh(0.5 * z) + 0.5


_GELU_C = math.sqrt(2.0 / math.pi)


def _gelu_and_grad(x):
    x2 = x * x
    t = jnp.tanh(_GELU_C * (x + 0.044715 * x * x2))
    g = 0.5 * x * (1.0 + t)
    dg = 0.5 * (1.0 + t) + 0.5 * x * (1.0 - t * t) * (_GELU_C * (1.0 + 3.0 * 0.044715 * x2))
    return g, dg


def _silu_and_grad(z):
    s = _sigmoid(z)
    return z * s, s * (1.0 + z * (1.0 - s))


def _rope(xb, c, sl, sh):
    return xb * c + pltpu.roll(xb, 112, 1) * sl + pltpu.roll(xb, 16, 1) * sh


def _rope_t(dy, c, sl, sh):
    return dy * c + pltpu.roll(dy * sl, 16, 1) + pltpu.roll(dy * sh, 112, 1)


def _params(sem=("arbitrary",)):
    return pltpu.CompilerParams(dimension_semantics=sem, vmem_limit_bytes=VMEM_LIMIT)


def _row_spec(tile, width):
    return pl.BlockSpec((tile, width), lambda i: (i, 0))


def _full_spec(shape):
    nd = len(shape)
    return pl.BlockSpec(shape, lambda i: (0,) * nd)


def _kpe_rows(wt_ref):
    z = lambda n: jnp.zeros((n, D_MODEL), BF16)
    return jnp.concatenate([z(NOPE), wt_ref[Q_RANK + KV_RANK:LAT_COLS, :], z(HEAD_PAD - QK_DIM)], axis=0)


def _fwd_rest(xb, wt, b_g, b_m):
    s = xb.shape[0]
    ts = ROW_TILE

    def body(xb_ref, wt_ref, bg_ref, bm_ref, hg_ref, hm_ref):
        xb_ = xb_ref[...]
        hg_ref[...] = _dot_nt(xb_, wt_ref[ROW_GATE:IN_W, :]) + bg_ref[...]
        hm_ref[...] = _dot_nt(xb_, wt_ref[LAT_COLS:ROW_GATE, :]) + bm_ref[...]

    return pl.pallas_call(
        body, name="fwd_rest", grid=(s // ts,),
        in_specs=[_row_spec(ts, D_MODEL), _full_spec(wt.shape), _full_spec(b_g.shape), _full_spec(b_m.shape)],
        out_specs=[_row_spec(ts, GATE_W), _row_spec(ts, MID_W)],
        out_shape=[jax.ShapeDtypeStruct((s, GATE_W), F32), jax.ShapeDtypeStruct((s, MID_W), F32)],
        compiler_params=_params(),
    )(xb, wt, b_g, b_m)


def _fwd_lat(x, wlat, b_l, g_q, wuq, g_kv, wk, wv, rc, rsl, rsh, after):
    s = x.shape[0]
    ts = ROW_TILE

    def body(x_ref, wt_ref, bl_ref, gq_ref, wuq_ref, gkv_ref, wk_ref, wv_ref, rc_ref, rsl_ref,
             rsh_ref, after_ref, hl_ref, q_ref, k_ref, v_ref, xb_ref, qt_ref, kt_ref, vt_ref):
        xb = x_ref[...].astype(BF16)
        xb_ref[...] = xb
        hl = jnp.concatenate([_dot_nt(xb, wt_ref[0:Q_RANK + KV_RANK, :]), _dot_nt(xb, _kpe_rows(wt_ref))],
                             axis=1) + bl_ref[...]
        hl_ref[...] = hl
        c, sl, sh = rc_ref[...], rsl_ref[...], rsh_ref[...]
        cq = hl[:, :Q_RANK]
        cqn = cq * lax.rsqrt(jnp.mean(cq * cq, axis=-1, keepdims=True) + RMS_EPS) * gq_ref[...]
        q = _dot(cqn.astype(BF16), wuq_ref[...])
        ckv = hl[:, Q_RANK:Q_RANK + KV_RANK]
        ckvn = (ckv * lax.rsqrt(jnp.mean(ckv * ckv, axis=-1, keepdims=True) + RMS_EPS) * gkv_ref[...]).astype(BF16)
        k = _dot(ckvn, wk_ref[...])
        vb = _dot(ckvn, wv_ref[...]).astype(BF16)
        v_ref[...] = vb
        vt_ref[...] = vb.T
        kpe = _rope(hl[:, Q_RANK + KV_RANK:], c, sl, sh)
        for hd in range(HEADS):
            lanes = slice(hd * HEAD_PAD, (hd + 1) * HEAD_PAD)
            qb = (_rope(q[:, lanes], c, sl, sh) * Q_SCALE).astype(BF16)
            kb = (k[:, lanes] + kpe).astype(BF16)
            q_ref[:, lanes] = qb
            k_ref[:, lanes] = kb
            qt_ref[lanes, :] = qb.T
            kt_ref[lanes, :] = kb.T

    qk_w = HEADS * HEAD_PAD
    col_spec = lambda rows: pl.BlockSpec((rows, ts), lambda i: (0, i))
    return pl.pallas_call(
        body, name="fwd_lat", grid=(s // ts,),
        in_specs=[_row_spec(ts, D_MODEL), _full_spec(wlat.shape),
                  _full_spec(b_l.shape), _full_spec(g_q.shape),
                  _full_spec(wuq.shape), _full_spec(g_kv.shape), _full_spec(wk.shape), _full_spec(wv.shape),
                  _row_spec(ts, HEAD_PAD), _row_spec(ts, HEAD_PAD), _row_spec(ts, HEAD_PAD),
                  pl.BlockSpec(memory_space=pl.ANY)],
        out_specs=[_row_spec(ts, LAT_W), _row_spec(ts, qk_w),
                   _row_spec(ts, qk_w), _row_spec(ts, MLA_W), _row_spec(ts, D_MODEL), col_spec(qk_w), col_spec(qk_w),
                   col_spec(MLA_W)],
        out_shape=[jax.ShapeDtypeStruct((s, LAT_W), F32), jax.ShapeDtypeStruct((s, qk_w), BF16),
                   jax.ShapeDtypeStruct((s, qk_w), BF16), jax.ShapeDtypeStruct((s, MLA_W), BF16),
                   jax.ShapeDtypeStruct((s, D_MODEL), BF16), jax.ShapeDtypeStruct((qk_w, s), BF16),
                   jax.ShapeDtypeStruct((qk_w, s), BF16), jax.ShapeDtypeStruct((MLA_W, s), BF16)],
        compiler_params=_params(),
    )(x, wlat, b_l, g_q, wuq, g_kv, wk, wv, rc, rsl, rsh, after)


def _attn_fwd(qt, k, vt):
    s = k.shape[0]
    tq, tk = ATT_TQ, ATT_TK
    r = tq // tk
    pairs = HEADS // 2

    def body(qt_ref, k_ref, vt_ref, o_ref, lse_ref):
        i = pl.program_id(1)
        krow = lax.broadcasted_iota(jnp.int32, (tk, tq), 0)
        qcol = lax.broadcasted_iota(jnp.int32, (tk, tq), 1)
        qts = [qt_ref[hh * HEAD_PAD:(hh + 1) * HEAD_PAD, :] for hh in range(2)]

        def scores(j):
            koff = pl.multiple_of(j * tk, tk)
            return tuple(_dot(k_ref[pl.ds(koff, tk), hh * HEAD_PAD:(hh + 1) * HEAD_PAD], qts[hh]) for hh in range(2))

        def weighted(j, ps):
            koff = pl.multiple_of(j * tk, tk)
            return tuple(_dot(vt_ref[hh * V_DIM:(hh + 1) * V_DIM, pl.ds(koff, tk)], ps[hh]) for hh in range(2))

        def step(j, carry, diag, last):
            st, ps, stats = carry
            st_next = None if last else scores(j + 1)
            pvs = weighted(jnp.maximum(j - 1, 0), ps)
            new_ps, new_stats = [], []
            for hh in range(2):
                m, l, acc = stats[hh]
                s_ = st[hh]
                if diag is not None:
                    s_ = jnp.where(krow + diag * tk <= qcol, s_, -jnp.inf)
                m_new = jnp.maximum(m, jnp.max(s_, axis=0, keepdims=True))
                a = jnp.exp2(m - m_new)
                p = jnp.exp2(s_ - m_new)
                new_stats.append((m_new, a * l + jnp.sum(p, axis=0, keepdims=True), a * (acc + pvs[hh])))
                new_ps.append(p.astype(BF16))
            return st_next, tuple(new_ps), tuple(new_stats)

        one = (jnp.full((1, tq), -jnp.inf, F32), jnp.zeros((1, tq), F32), jnp.zeros((V_DIM, tq), F32))
        zero_p = jnp.zeros((tk, tq), BF16)
        nfull = i * r
        carry = lax.fori_loop(0, nfull, functools.partial(step, diag=None, last=False),
                              (scores(0), (zero_p, zero_p), (one, one)))
        for d in range(r):
            carry = step(nfull + d, carry, d, d == r - 1)
        _, ps, stats = carry
        pvs = weighted(nfull + r - 1, ps)
        ot = jnp.concatenate([(stats[hh][2] + pvs[hh]) / stats[hh][1] for hh in range(2)], axis=0)
        o_ref[...] = ot.T
        lse = [stats[hh][0] + jnp.log(stats[hh][1]) * LOG2E for hh in range(2)]
        lse_ref[...] = jnp.concatenate(lse + [jnp.zeros((6, tq), F32)], axis=0)

    return pl.pallas_call(
        body, name="attn_fwd", grid=(pairs, s // tq),
        in_specs=[pl.BlockSpec((2 * HEAD_PAD, tq), lambda p, i: (p, i)),
                  pl.BlockSpec((s, 2 * HEAD_PAD), lambda p, i: (0, p)),
                  pl.BlockSpec((2 * V_DIM, s), lambda p, i: (p, 0))],
        out_specs=[pl.BlockSpec((tq, 2 * V_DIM), lambda p, i: (i, p)),
                   pl.BlockSpec((None, 8, tq), lambda p, i: (p, 0, i))],
        out_shape=[jax.ShapeDtypeStruct((s, MLA_W), F32), jax.ShapeDtypeStruct((pairs, 8, s), F32)],
        compiler_params=_params(("arbitrary", "arbitrary")),
    )(qt, k, vt)


def _attn_bwd(q, qt, k, kt, v, do, dot, lse, delta):
    s = k.shape[0]
    tq, tk = ATT_BWD_TQ, ATT_BWD_TK
    r = tq // tk
    nq = s // tq
    nk = s // tk
    pairs = HEADS // 2

    def body(q_ref, qt_ref, k_ref, kt_ref, v_ref, do_ref, dot_ref, lse_ref, dl_ref, dqt_ref, dk_ref, dv_ref):
        j = pl.program_id(1)
        krow = lax.broadcasted_iota(jnp.int32, (tk, tq), 0)
        qcol = lax.broadcasted_iota(jnp.int32, (tk, tq), 1)
        lane = lax.broadcasted_iota(jnp.int32, (tk, 2 * V_DIM), 1)
        drow = lax.broadcasted_iota(jnp.int32, (2 * V_DIM, tq), 0)

        @pl.when(j == 0)
        def _():
            dqt_ref[...] = jnp.zeros_like(dqt_ref)

        koff = pl.multiple_of(j * tk, tk)
        vb = v_ref[pl.ds(koff, tk), :]
        kbs = [k_ref[pl.ds(koff, tk), hh * HEAD_PAD:(hh + 1) * HEAD_PAD] for hh in range(2)]
        ktbs = [kt_ref[hh * HEAD_PAD:(hh + 1) * HEAD_PAD, pl.ds(koff, tk)] for hh in range(2)]
        i0 = j // r

        def front(i):
            qoff = pl.multiple_of(i * tq, tq)
            dotb = dot_ref[:, pl.ds(qoff, tq)]
            out = []
            for hh in range(2):
                mine = (drow < V_DIM) if hh == 0 else (drow >= V_DIM)
                st = _dot(kbs[hh], qt_ref[hh * HEAD_PAD:(hh + 1) * HEAD_PAD, pl.ds(qoff, tq)])
                out.append((st, _dot(vb, jnp.where(mine, dotb, jnp.zeros_like(dotb)))))
            return tuple(out)

        def middle(i, tiles, diag):
            qoff = pl.multiple_of(i * tq, tq)
            out = []
            for hh in range(2):
                st, dpt = tiles[hh]
                if diag:
                    st = jnp.where(krow + (j - i0 * r) * tk <= qcol, st, -jnp.inf)
                p = jnp.exp2(st - lse_ref[hh:hh + 1, pl.ds(qoff, tq)])
                out.append((p.astype(BF16), (p * (dpt - dl_ref[hh:hh + 1, pl.ds(qoff, tq)])).astype(BF16)))
            return tuple(out)

        def back(i, pd, accs):
            qoff = pl.multiple_of(i * tq, tq)
            dob = do_ref[pl.ds(qoff, tq), :]
            out = []
            for hh in range(2):
                rows = slice(hh * HEAD_PAD, (hh + 1) * HEAD_PAD)
                p, dst = pd[hh]
                dk_acc, dv_acc = accs[hh]
                dv_acc = dv_acc + _dot(p, dob)
                dk_acc = dk_acc + _dot(dst, q_ref[pl.ds(qoff, tq), rows])
                dqt_ref[rows, pl.ds(qoff, tq)] += _dot(ktbs[hh], dst)
                out.append((dk_acc, dv_acc))
            return tuple(out)

        def step(i, accs, diag):
            return back(i, middle(i, front(i), diag), accs)

        zero_acc = (jnp.zeros((tk, HEAD_PAD), F32), jnp.zeros((tk, 2 * V_DIM), F32))
        accs = step(i0, (zero_acc, zero_acc), True)
        accs = lax.fori_loop(i0 + 1, nq, functools.partial(step, diag=False), accs)
        for hh in range(2):
            dk_ref[:, hh * HEAD_PAD:(hh + 1) * HEAD_PAD] = accs[hh][0] * LN2
        dv_ref[...] = jnp.where(lane < V_DIM, accs[0][1], accs[1][1])

        @pl.when(j == nk - 1)
        def _():
            dqt_ref[...] = dqt_ref[...] * SCALE

    pair_rows = lambda w: pl.BlockSpec((s, w), lambda p, j: (0, p))
    pair_cols = lambda w: pl.BlockSpec((w, s), lambda p, j: (p, 0))
    stats = pl.BlockSpec((None, 8, s), lambda p, j: (p, 0, 0))
    return pl.pallas_call(
        body, name="attn_bwd", grid=(pairs, nk),
        in_specs=[pair_rows(2 * HEAD_PAD), pair_cols(2 * HEAD_PAD), pair_rows(2 * HEAD_PAD), pair_cols(2 * HEAD_PAD),
                  pair_rows(2 * V_DIM), pair_rows(2 * V_DIM), pair_cols(2 * V_DIM), stats, stats],
        out_specs=[pair_cols(2 * HEAD_PAD),
                   pl.BlockSpec((tk, 2 * HEAD_PAD), lambda p, j: (j, p)),
                   pl.BlockSpec((tk, 2 * V_DIM), lambda p, j: (j, p))],
        out_shape=[jax.ShapeDtypeStruct((HEADS * HEAD_PAD, s), F32), jax.ShapeDtypeStruct((s, HEADS * HEAD_PAD), F32),
                   jax.ShapeDtypeStruct((s, MLA_W), F32)],
        compiler_params=_params(("arbitrary", "arbitrary")),
    )(q, qt, k, kt, v, do, dot, lse, delta)


def _split3(a):
    hi = a.astype(BF16)
    r1 = a - hi.astype(F32)
    mid = r1.astype(BF16)
    lo = (r1 - mid.astype(F32)).astype(BF16)
    return hi, mid, lo


def _mid(x, tgt, o, hm, hg, woa, wob, wout, ln_g, ln_b, sg_g, sg_b, w_s, bsb):
    s = x.shape[0]
    ts = ROW_TILE
    nsteps = s // ts
    nch = ts // CHUNK
    npair = GROUPS // 2

    def body(x_ref, t_ref, o_ref, hm_ref, hg_ref, woa_ref, wob_ref, wout_ref, lng_ref, lnb_ref, sgg_ref, sgb_ref,
             ws_ref, bsb_ref,
             dr_ref, dhg_ref, dhm_ref, do_ref, dot_ref, dl_ref, dhgt_ref, dhmt_ref,
             dwout_ref, dwoa_ref, dwob_ref, dws_ref, dbs_ref, dlng_ref, dlnb_ref, dsgg_ref, dsgb_ref, loss_ref,
             dbg_ref, dbm_ref, dbacc_ref):
        i = pl.program_id(0)

        @pl.when(i == 0)
        def _():
            for r in (dwout_ref, dwoa_ref, dwob_ref, dws_ref, dlng_ref, dlnb_ref, dsgg_ref, dsgb_ref, loss_ref,
                      dbg_ref, dbm_ref, dbacc_ref):
                r[...] = jnp.zeros_like(r)

        def emit(ref, tref, bref, lo, val):
            vb = val.astype(BF16)
            n = val.shape[1]
            ref[:, lo:lo + n] = vb
            tref[lo:lo + n, :] = vb.T
            bref[:, lo:lo + n] += jnp.sum(val, axis=0, keepdims=True)

        lane = lax.broadcasted_iota(jnp.int32, (CHUNK, CHUNK), 1)
        left = lane < V_DIM
        tril = lax.broadcasted_iota(jnp.int32, (CHUNK, CHUNK), 0) >= lane
        ms = [jnp.where(tril, ws_ref[g], 0.0).astype(BF16) for g in range(GROUPS)]

        z_a = hm_ref[:, 0:SGU_W]
        u = hm_ref[:, SGU_W:2 * SGU_W]
        v = hm_ref[:, 2 * SGU_W:3 * SGU_W]
        z_b = hm_ref[:, 3 * SGU_W:4 * SGU_W]
        o = o_ref[...]
        sa, dsa = _silu_and_grad(z_a)
        y_a = (o * sa).astype(BF16)
        gu, dgu = _gelu_and_grad(u)
        gv, dgv = _gelu_and_grad(v)
        mu = jnp.mean(gv, axis=-1, keepdims=True)
        vc = gv - mu
        rstd_v = lax.rsqrt(jnp.mean(vc * vc, axis=-1, keepdims=True) + LN_EPS)
        vhat = vc * rstd_v
        vn = (vhat * sgg_ref[...] + sgb_ref[...]).astype(BF16)
        rows = []
        for c in range(nch):
            blocks = []
            for p in range(npair):
                blk = vn[c * CHUNK:(c + 1) * CHUNK, p * CHUNK:(p + 1) * CHUNK]
                blocks.append(jnp.where(left, _dot(ms[2 * p], blk), _dot(ms[2 * p + 1], blk)))
            rows.append(jnp.concatenate(blocks, axis=1) + bsb_ref[...])
        mixed = jnp.concatenate(rows, axis=0)
        sgu = gu * mixed
        sb, dsb = _silu_and_grad(z_b)
        y_b = (sgu * sb).astype(BF16)
        pa = jnp.concatenate([_dot(y_a, woa_ref[k]) for k in range(N_SLABS)], axis=1)
        pb = jnp.concatenate([_dot(y_b, wob_ref[k]) for k in range(N_SLABS)], axis=1)
        sga = _sigmoid(hg_ref[:, :D_MODEL])
        sgb = _sigmoid(hg_ref[:, D_MODEL:])
        m2 = (sga * pa + sgb * pb).astype(BF16)
        r = ALPHA * x_ref[...] + _dot(m2, wout_ref[...])
        rmu = jnp.mean(r, axis=-1, keepdims=True)
        rc = r - rmu
        rstd = lax.rsqrt(jnp.mean(rc * rc, axis=-1, keepdims=True) + LN_EPS)
        xhat = rc * rstd
        y = xhat * lng_ref[...] + lnb_ref[...]
        err = y - t_ref[...]
        loss_ref[...] += jnp.full(loss_ref.shape, 0.5 / D_MODEL, F32) * jnp.sum(err * err)

        dy = err * (1.0 / D_MODEL)
        dlng_ref[...] += jnp.sum(dy * xhat, axis=0, keepdims=True)
        dlnb_ref[...] += jnp.sum(dy, axis=0, keepdims=True)
        dxh = dy * lng_ref[...]
        dr = rstd * (dxh - jnp.mean(dxh, axis=-1, keepdims=True) - xhat * jnp.mean(dxh * xhat, axis=-1, keepdims=True))
        dr_ref[...] = dr
        drb = dr.astype(BF16)
        dwout_ref[...] += _dot_tn(m2, drb)
        dm2 = _dot_nt(drb, wout_ref[...])
        emit(dhg_ref, dhgt_ref, dbg_ref, 0, dm2 * pa * sga * (1.0 - sga))
        emit(dhg_ref, dhgt_ref, dbg_ref, D_MODEL, dm2 * pb * sgb * (1.0 - sgb))
        dpa = (dm2 * sga).astype(BF16)
        dpb = (dm2 * sgb).astype(BF16)
        dy_a = jnp.zeros((ts, MLA_W), F32)
        dy_b = jnp.zeros((ts, SGU_W), F32)
        for k in range(N_SLABS):
            cols = slice(k * SLAB_W, (k + 1) * SLAB_W)
            dwoa_ref[k] += _dot_tn(y_a, dpa[:, cols])
            dwob_ref[k] += _dot_tn(y_b, dpb[:, cols])
            dy_a = dy_a + _dot_nt(dpa[:, cols], woa_ref[k])
            dy_b = dy_b + _dot_nt(dpb[:, cols], wob_ref[k])
        dob = (dy_a * sa).astype(BF16)
        do_ref[...] = dob
        dot_ref[...] = dob.T
        head = (lax.broadcasted_iota(jnp.int32, (HEADS, MLA_W), 1) // V_DIM
                == lax.broadcasted_iota(jnp.int32, (HEADS, MLA_W), 0)).astype(BF16)
        dl_ref[...] = sum(_dot_nt(head, term) for term in _split3(dob.astype(F32) * o))
        emit(dhm_ref, dhmt_ref, dbm_ref, 0, dy_a * o * dsa)
        dsg = dy_b * sb
        emit(dhm_ref, dhmt_ref, dbm_ref, 3 * SGU_W, dy_b * sgu * dsb)
        emit(dhm_ref, dhmt_ref, dbm_ref, SGU_W, dsg * mixed * dgu)
        dmixed = dsg * gu
        dvn_rows = []
        dbs_sum = jnp.zeros((CHUNK, SGU_W), F32)
        for c in range(nch):
            dm_c = dmixed[c * CHUNK:(c + 1) * CHUNK, :]
            dbs_sum = dbs_sum + dm_c
            blocks = []
            for p in range(npair):
                dmb = dm_c[:, p * CHUNK:(p + 1) * CHUNK].astype(BF16)
                blk = vn[c * CHUNK:(c + 1) * CHUNK, p * CHUNK:(p + 1) * CHUNK]
                blocks.append(jnp.where(left, _dot_tn(ms[2 * p], dmb), _dot_tn(ms[2 * p + 1], dmb)))
                zero = jnp.zeros_like(dmb)
                dws_ref[2 * p] += jnp.where(tril, _dot_nt(jnp.where(left, dmb, zero), blk), 0.0)
                dws_ref[2 * p + 1] += jnp.where(tril, _dot_nt(jnp.where(left, zero, dmb), blk), 0.0)
            dvn_rows.append(jnp.concatenate(blocks, axis=1))
        dbacc_ref[...] += dbs_sum
        dvn = jnp.concatenate(dvn_rows, axis=0)
        dsgg_ref[...] += jnp.sum(dvn * vhat, axis=0, keepdims=True)
        dsgb_ref[...] += jnp.sum(dvn, axis=0, keepdims=True)
        dvh = dvn * sgg_ref[...]
        dgv_in = rstd_v * (dvh - jnp.mean(dvh, axis=-1, keepdims=True)
                           - vhat * jnp.mean(dvh * vhat, axis=-1, keepdims=True))
        emit(dhm_ref, dhmt_ref, dbm_ref, 2 * SGU_W, dgv_in * dgv)

        @pl.when(i == nsteps - 1)
        def _():
            grp = (lax.broadcasted_iota(jnp.int32, (SGU_W, CHUNK), 0) // V_DIM
                   == lax.broadcasted_iota(jnp.int32, (SGU_W, CHUNK), 1)).astype(BF16)
            hi, mid, lo = _split3(dbacc_ref[...])
            dbs_ref[...] = _dot(hi, grp) + _dot(mid, grp) + _dot(lo, grp)

    acc_shapes = [(D_MODEL, D_MODEL), woa.shape, wob.shape, (GROUPS, CHUNK, CHUNK), (CHUNK, CHUNK),
                  (1, D_MODEL), (1, D_MODEL), (1, SGU_W), (1, SGU_W), (1, 128), (1, GATE_W), (1, MID_W)]
    col_spec = lambda rows: pl.BlockSpec((rows, ts), lambda i: (0, i))
    return pl.pallas_call(
        body, name="mid", grid=(nsteps,),
        in_specs=[_row_spec(ts, D_MODEL), _row_spec(ts, D_MODEL), _row_spec(ts, MLA_W), _row_spec(ts, MID_W),
                  _row_spec(ts, GATE_W), _full_spec(woa.shape), _full_spec(wob.shape), _full_spec(wout.shape),
                  _full_spec(ln_g.shape), _full_spec(ln_b.shape), _full_spec(sg_g.shape), _full_spec(sg_b.shape),
                  _full_spec(w_s.shape), _full_spec(bsb.shape)],
        out_specs=[_row_spec(ts, D_MODEL), _row_spec(ts, GATE_W), _row_spec(ts, MID_W), _row_spec(ts, MLA_W),
                   col_spec(MLA_W), col_spec(HEADS), col_spec(GATE_W), col_spec(MID_W)]
        + [_full_spec(sh) for sh in acc_shapes],
        out_shape=[jax.ShapeDtypeStruct((s, D_MODEL), F32), jax.ShapeDtypeStruct((s, GATE_W), BF16),
                   jax.ShapeDtypeStruct((s, MID_W), BF16), jax.ShapeDtypeStruct((s, MLA_W), BF16),
                   jax.ShapeDtypeStruct((MLA_W, s), BF16), jax.ShapeDtypeStruct((HEADS, s), F32),
                   jax.ShapeDtypeStruct((GATE_W, s), BF16), jax.ShapeDtypeStruct((MID_W, s), BF16)]
        + [jax.ShapeDtypeStruct(sh, F32) for sh in acc_shapes],
        scratch_shapes=[pltpu.VMEM((CHUNK, SGU_W), F32)],
        compiler_params=_params(),
    )(x, tgt, o, hm, hg, woa, wob, wout, ln_g, ln_b, sg_g, sg_b, w_s, bsb)


def _lat_bwd(dq, dk, dv, hl, rc, rsl, rsh, g_q, g_kv, wuq, wk, wv):
    s = dk.shape[0]
    ts = ROW_TILE
    qk_w = HEADS * HEAD_PAD

    def body(dq_ref, dk_ref, dv_ref, hl_ref, rc_ref, rsl_ref, rsh_ref, gq_ref, gkv_ref, wuq_ref, wk_ref, wv_ref,
             dhl_ref, dhlt_ref, dwuq_ref, dwk_ref, dwv_ref, dgq_ref, dgkv_ref, dbl_ref):
        i = pl.program_id(0)

        @pl.when(i == 0)
        def _():
            for r in (dwuq_ref, dwk_ref, dwv_ref, dgq_ref, dgkv_ref, dbl_ref):
                r[...] = jnp.zeros_like(r)

        def emit(lo, val):
            vb = val.astype(BF16)
            n = val.shape[1]
            dhl_ref[:, lo:lo + n] = vb
            dhlt_ref[lo:lo + n, :] = vb.T
            dbl_ref[:, lo:lo + n] += jnp.sum(val, axis=0, keepdims=True)

        c, sl, sh = rc_ref[...], rsl_ref[...], rsh_ref[...]
        lane = lax.broadcasted_iota(jnp.int32, (ts, HEAD_PAD), 1)
        pe = (lane >= NOPE) & (lane < QK_DIM)
        dkpe = jnp.zeros((ts, HEAD_PAD), F32)
        dqu = []
        for hd in range(HEADS):
            lanes = slice(hd * HEAD_PAD, (hd + 1) * HEAD_PAD)
            dqu.append(_rope_t(dq_ref[lanes, :].T, c, sl, sh).astype(BF16))
            dkpe = dkpe + dk_ref[:, lanes]
        dqu = jnp.concatenate(dqu, axis=1)
        dkpe = _rope_t(jnp.where(pe, dkpe, 0.0), c, sl, sh)

        cq = hl_ref[:, :Q_RANK]
        rq = lax.rsqrt(jnp.mean(cq * cq, axis=-1, keepdims=True) + RMS_EPS)
        cqh = cq * rq
        cqn = (cqh * gq_ref[...]).astype(BF16)
        dwuq_ref[...] += _dot_tn(cqn, dqu)
        dcqn = _dot_nt(dqu, wuq_ref[...])
        dgq_ref[...] += jnp.sum(dcqn * cqh, axis=0, keepdims=True)
        dch = dcqn * gq_ref[...]
        emit(0, rq * (dch - cqh * jnp.mean(dch * cqh, axis=-1, keepdims=True)))

        ckv = hl_ref[:, Q_RANK:Q_RANK + KV_RANK]
        rk = lax.rsqrt(jnp.mean(ckv * ckv, axis=-1, keepdims=True) + RMS_EPS)
        ckh = ckv * rk
        ckn = (ckh * gkv_ref[...]).astype(BF16)
        dkb = dk_ref[...].astype(BF16)
        dvb = dv_ref[...].astype(BF16)
        dwk_ref[...] += _dot_tn(ckn, dkb)
        dwv_ref[...] += _dot_tn(ckn, dvb)
        dckn = _dot_nt(dkb, wk_ref[...]) + _dot_nt(dvb, wv_ref[...])
        dgkv_ref[...] += jnp.sum(dckn * ckh, axis=0, keepdims=True)
        dkh = dckn * gkv_ref[...]
        emit(Q_RANK, rk * (dkh - ckh * jnp.mean(dkh * ckh, axis=-1, keepdims=True)))
        emit(Q_RANK + KV_RANK, dkpe)

    acc_shapes = [wuq.shape, wk.shape, wv.shape, g_q.shape, g_kv.shape, (1, LAT_W)]
    return pl.pallas_call(
        body, name="lat_bwd", grid=(s // ts,),
        in_specs=[pl.BlockSpec((qk_w, ts), lambda i: (0, i)), _row_spec(ts, qk_w), _row_spec(ts, MLA_W),
                  _row_spec(ts, LAT_W), _row_spec(ts, HEAD_PAD), _row_spec(ts, HEAD_PAD), _row_spec(ts, HEAD_PAD),
                  _full_spec(g_q.shape), _full_spec(g_kv.shape), _full_spec(wuq.shape), _full_spec(wk.shape),
                  _full_spec(wv.shape)],
        out_specs=[_row_spec(ts, LAT_W), pl.BlockSpec((LAT_W, ts), lambda i: (0, i))]
        + [_full_spec(sh) for sh in acc_shapes],
        out_shape=[jax.ShapeDtypeStruct((s, LAT_W), BF16), jax.ShapeDtypeStruct((LAT_W, s), BF16)]
        + [jax.ShapeDtypeStruct(sh, F32) for sh in acc_shapes],
        compiler_params=_params(),
    )(dq, dk, dv, hl, rc, rsl, rsh, g_q, g_kv, wuq, wk, wv)


def _dx(dr, dhg, dhm, dhl, wt, after):
    s = dr.shape[0]
    ts = ROW_TILE

    def body(dr_ref, dhg_ref, dhm_ref, dhl_ref, wt_ref, after_ref, dx_ref):
        dx_ref[...] = (ALPHA * dr_ref[...]
                       + _dot(dhg_ref[...], wt_ref[ROW_GATE:IN_W, :])
                       + _dot(dhm_ref[...], wt_ref[LAT_COLS:ROW_GATE, :])
                       + _dot(dhl_ref[:, 0:Q_RANK + KV_RANK], wt_ref[0:Q_RANK + KV_RANK, :])
                       + _dot(dhl_ref[:, Q_RANK + KV_RANK:], _kpe_rows(wt_ref)))

    return pl.pallas_call(
        body, name="dx", grid=(s // ts,),
        in_specs=[_row_spec(ts, D_MODEL), _row_spec(ts, GATE_W), _row_spec(ts, MID_W), _row_spec(ts, LAT_W),
                  _full_spec(wt.shape), pl.BlockSpec(memory_space=pl.ANY)],
        out_specs=_row_spec(ts, D_MODEL),
        out_shape=jax.ShapeDtypeStruct((s, D_MODEL), F32),
        compiler_params=_params(),
    )(dr, dhg, dhm, dhl, wt, after)


def _dwt_early(dhmt, dhgt, xb):
    tn = 512
    nm, ng = MID_W // tn, GATE_W // tn
    s = dhmt.shape[1]

    def body(dm_ref, dg_ref, xb_ref, dw_ref):
        i = pl.program_id(0)

        @pl.when(i < nm)
        def _():
            dw_ref[...] = _dot(dm_ref[...], xb_ref[...]).astype(BF16)

        @pl.when(i >= nm)
        def _():
            dw_ref[...] = _dot(dg_ref[...], xb_ref[...]).astype(BF16)

    rows = pl.pallas_call(
        body, name="dwt_early", grid=(nm + ng,),
        in_specs=[pl.BlockSpec((tn, s), lambda i: (jnp.minimum(i, nm - 1), 0)),
                  pl.BlockSpec((tn, s), lambda i: (jnp.maximum(i - nm, 0), 0)), _full_spec(xb.shape)],
        out_specs=pl.BlockSpec((pl.Element(tn), pl.Element(D_MODEL)), lambda i: (pl.multiple_of(LAT_COLS + i * tn, 32), 0)),
        out_shape=jax.ShapeDtypeStruct((IN_W, D_MODEL), BF16),
        compiler_params=_params(),
    )(dhmt, dhgt, xb)

    def zero(buf_ref, out_ref):
        out_ref[...] = jnp.zeros_like(out_ref)

    return pl.pallas_call(
        zero, name="dwt_zero_lat", grid=(1,), in_specs=[pl.BlockSpec(memory_space=pl.ANY)],
        out_specs=pl.BlockSpec((LAT_COLS, D_MODEL), lambda i: (0, 0)),
        out_shape=jax.ShapeDtypeStruct((IN_W, D_MODEL), BF16), input_output_aliases={0: 0},
    )(rows)


def _dwt_lat(dhlt, xb):
    n, s = dhlt.shape

    def body(dht_ref, xb_ref, dw_ref):
        dw = _dot(dht_ref[...], xb_ref[...]).astype(BF16)
        kpe = Q_RANK + KV_RANK + NOPE
        dw_ref[0:Q_RANK + KV_RANK, :] = dw[0:Q_RANK + KV_RANK]
        dw_ref[Q_RANK + KV_RANK:LAT_COLS, :] = dw[kpe:kpe + ROPE]
        dw_ref[LAT_COLS:, :] = jnp.zeros((LAT_ROWS_PAD - LAT_COLS, D_MODEL), BF16)

    return pl.pallas_call(
        body, name="dwt_lat", in_specs=[VMEM_SPEC, VMEM_SPEC], out_specs=VMEM_SPEC,
        out_shape=jax.ShapeDtypeStruct((LAT_ROWS_PAD, D_MODEL), BF16),
        compiler_params=pltpu.CompilerParams(vmem_limit_bytes=VMEM_LIMIT),
    )(dhlt, xb)


def _split_bias(b):
    z = lambda n: jnp.zeros((n,), b.dtype)
    lat = jnp.concatenate([b[:Q_RANK + KV_RANK], z(NOPE), b[Q_RANK + KV_RANK:LAT_COLS], z(HEAD_PAD - QK_DIM)])
    return b[None, ROW_GATE:], b[None, LAT_COLS:ROW_GATE], lat[None, :]


def _join_bias(g, m, l):
    kpe = Q_RANK + KV_RANK + NOPE
    return jnp.concatenate([l[0, :Q_RANK + KV_RANK], l[0, kpe:kpe + ROPE], m[0], g[0]])


def _rope_tables(positions):
    half = ROPE // 2
    inv_freq = ROPE_THETA ** (-jnp.arange(0, ROPE, 2, dtype=F32) / ROPE)
    ang = positions.astype(F32)[:, None] * inv_freq
    cos, sin = jnp.cos(ang), jnp.sin(ang)
    n = positions.shape[0]
    one, zero = jnp.ones((n, NOPE), F32), jnp.zeros((n, half), F32)
    tail1, tail0 = jnp.ones((n, HEAD_PAD - QK_DIM), F32), jnp.zeros((n, HEAD_PAD - QK_DIM), F32)
    z64 = jnp.zeros((n, NOPE), F32)
    rc = jnp.concatenate([one, cos, cos, tail1], axis=1)
    rsl = jnp.concatenate([z64, -sin, zero, tail0], axis=1)
    rsh = jnp.concatenate([z64, zero, sin, tail0], axis=1)
    return rc, rsl, rsh


def _local_attention(x, positions, wlat, b_in, g_q, w_uq, g_kv, w_ukv, after):
    rc, rsl, rsh = _rope_tables(positions)
    b_g, b_m, b_l = _split_bias(b_in)
    wuq = jnp.pad(w_uq, ((0, 0), (0, 0), (0, HEAD_PAD - QK_DIM))).reshape(Q_RANK, HEADS * HEAD_PAD).astype(BF16)
    wk = jnp.pad(w_ukv[:, :, :NOPE], ((0, 0), (0, 0), (0, HEAD_PAD - NOPE))).reshape(KV_RANK, HEADS * HEAD_PAD).astype(BF16)
    wv = w_ukv[:, :, NOPE:].reshape(KV_RANK, MLA_W).astype(BF16)
    gq2, gkv2 = g_q[None, :], g_kv[None, :]
    hl, q, k, v, xb, qt, kt, vt = _fwd_lat(x, wlat, b_l, gq2, wuq, gkv2, wk, wv, rc, rsl, rsh, after)
    o, lse = _attn_fwd(qt, k, vt)
    return dict(q=q, qt=qt, k=k, kt=kt, v=v, o=o, lse=lse, hl=hl, rc=rc, rsl=rsl, rsh=rsh, gq2=gq2, gkv2=gkv2,
                wuq=wuq, wk=wk, wv=wv, xb=xb, b_g=b_g, b_m=b_m)


def _local_head(st, x, tgt, wt, w_oa, sg_g, sg_b, w_s, b_s, w_ob, w_out, ln_g, ln_b):
    q, qt, k, kt, v, o, lse, hl, xb = (st[n] for n in ("q", "qt", "k", "kt", "v", "o", "lse", "hl", "xb"))
    rc, rsl, rsh, gq2, gkv2, wuq, wk, wv = (st[n] for n in ("rc", "rsl", "rsh", "gq2", "gkv2", "wuq", "wk", "wv"))
    bsb = jnp.repeat(b_s.T, V_DIM, axis=1)
    hg, hm = _fwd_rest(xb, wt, st["b_g"], st["b_m"])
    (dr, dhg, dhm, do, dot, delta, dhgt, dhmt, dwout, dwoa, dwob, dws, dbs, dlng, dlnb, dsgg, dsgb, loss, dbg,
     dbm) = _mid(x, tgt, o, hm, hg, w_oa, w_ob, w_out, ln_g[None, :], ln_b[None, :], sg_g[None, :], sg_b[None, :],
                 w_s, bsb)
    delta = jnp.pad(delta.reshape(HEADS // 2, 2, -1), ((0, 0), (0, 6), (0, 0)))
    early = {
        "w_in": _dwt_early(dhmt, dhgt, xb),
        "w_oa": dwoa, "sgu_ln_g": dsgg[0], "sgu_ln_b": dsgb[0], "w_s": dws, "b_s": dbs[:, :GROUPS].T,
        "w_ob": dwob, "w_out": dwout, "ln_g": dlng[0], "ln_b": dlnb[0],
    }
    state = dict(q=q, qt=qt, k=k, kt=kt, v=v, do=do, dot=dot, lse=lse, delta=delta, hl=hl, rc=rc, rsl=rsl, rsh=rsh,
                 gq2=gq2, gkv2=gkv2, wuq=wuq, wk=wk, wv=wv, dr=dr, dhg=dhg, dhm=dhm, wt=wt, xb=xb, dbg=dbg, dbm=dbm)
    return loss, early, state


def _local_tail(st):
    dq, dk, dv = _attn_bwd(st["q"], st["qt"], st["k"], st["kt"], st["v"], st["do"], st["dot"], st["lse"], st["delta"])
    dhl, dhlt, dwuq, dwk, dwv, dgq, dgkv, dbl = _lat_bwd(dq, dk, dv, st["hl"], st["rc"], st["rsl"], st["rsh"],
                                                         st["gq2"], st["gkv2"], st["wuq"], st["wk"], st["wv"])
    late = {
        "w_lat": _dwt_lat(dhlt, st["xb"]),
        "b_in": _join_bias(st["dbg"], st["dbm"], dbl),
        "g_q": dgq[0],
        "w_uq": dwuq.reshape(Q_RANK, HEADS, HEAD_PAD)[:, :, :QK_DIM],
        "g_kv": dgkv[0],
        "w_ukv": jnp.concatenate([dwk.reshape(KV_RANK, HEADS, HEAD_PAD)[:, :, :NOPE],
                                  dwv.reshape(KV_RANK, HEADS, V_DIM)], axis=2),
    }
    return dhl, late


def _local_step(x, positions, tgt, wt, b_in, g_q, w_uq, g_kv, w_ukv, w_oa, sg_g, sg_b, w_s, b_s, w_ob, w_out, ln_g,
                ln_b):
    st = _local_attention(x, positions, wt[:LAT_COLS], b_in, g_q, w_uq, g_kv, w_ukv, b_in)
    loss, early, st = _local_head(st, x, tgt, wt, w_oa, sg_g, sg_b, w_s, b_s, w_ob, w_out, ln_g, ln_b)
    dhl, late = _local_tail(st)
    dx = _dx(st["dr"], st["dhg"], st["dhm"], dhl, st["wt"], dhl)
    grads = {**early, **late}
    grads["w_in"] = jnp.concatenate([grads.pop("w_lat")[:LAT_COLS], early["w_in"][LAT_COLS:]], axis=0)
    return loss, dx, grads


MESH = pl.DeviceIdType.MESH
N_CHIPS = 4
HBM_SPEC = pl.BlockSpec(memory_space=pl.ANY)
HBM_SPEC_STRICT = pl.BlockSpec(memory_space=pltpu.HBM)
VMEM_SPEC = pl.BlockSpec(memory_space=pltpu.VMEM)

REP_ROWS = 80


def _rows8(a):
    flat = a.reshape(-1)
    n = -(-flat.shape[0] // (8 * D_MODEL)) * 8 * D_MODEL
    return jnp.pad(flat, (0, n - flat.shape[0])).reshape(-1, D_MODEL)


def _place():
    x, y, c = lax.axis_index("x"), lax.axis_index("y"), lax.axis_index("c")
    others = [(1 - x, y), (x, 1 - y), (1 - x, 1 - y)]
    return x, y, c, others


def _gather_weights(shards):
    n = len(shards)

    def body(*refs):
        ins, outs, bufs = refs[:n], refs[n:2 * n], refs[2 * n:3 * n]
        send_sems, recv_sems, local_sems = refs[3 * n:]
        x, y, c, others = _place()
        me = 2 * x + y
        sibling = (x, y, 1 - c)
        for src, buf in zip(ins, bufs):
            buf[...] = src[...].astype(BF16)
        own = [pltpu.make_async_copy(bufs[w], outs[w].at[me], local_sems.at[w]) for w in range(n)]
        for cp in own:
            cp.start()

        def part(w, chip, half):
            hc = shards[w].shape[1] // 2
            return outs[w].at[chip, :, pl.ds(half * hc, hc)]

        def sent(w, j):
            hc = shards[w].shape[1] // 2
            return pltpu.make_async_remote_copy(
                src_ref=bufs[w].at[:, pl.ds(c * hc, hc)], dst_ref=part(w, me, c),
                send_sem=send_sems.at[w * 3 + j], recv_sem=recv_sems.at[w * 3 + j],
                device_id=(*others[j], c), device_id_type=MESH)

        def landed(w, j):
            px, py = others[j]
            return pltpu.make_async_remote_copy(
                src_ref=part(w, 2 * px + py, c), dst_ref=part(w, 2 * px + py, c),
                send_sem=send_sems.at[w * 3 + j], recv_sem=recv_sems.at[w * 3 + j],
                device_id=(px, py, c), device_id_type=MESH)

        def passed(w, j, half):
            px, py = others[j]
            k = n * 3 + w * 3 + j
            return pltpu.make_async_remote_copy(
                src_ref=part(w, 2 * px + py, half), dst_ref=part(w, 2 * px + py, half),
                send_sem=send_sems.at[k], recv_sem=recv_sems.at[k], device_id=sibling, device_id_type=MESH)

        first = [sent(w, j) for w in range(n) for j in range(3)]
        for cp in first:
            cp.start()
        fwd = []
        for w in range(n):
            for j in range(3):
                landed(w, j).wait_recv()
                cp = passed(w, j, c)
                cp.start()
                fwd.append(cp)
        for w in range(n):
            for j in range(3):
                passed(w, j, 1 - c).wait_recv()
        for cp in first + fwd:
            cp.wait_send()
        for cp in own:
            cp.wait()

    return pl.pallas_call(
        body, name="gather_weights",
        in_specs=[VMEM_SPEC] * n, out_specs=[HBM_SPEC] * n,
        out_shape=[jax.ShapeDtypeStruct((N_CHIPS,) + s.shape, BF16) for s in shards],
        scratch_shapes=[pltpu.VMEM(s.shape, BF16) for s in shards]
        + [pltpu.SemaphoreType.DMA((6 * n,)), pltpu.SemaphoreType.DMA((6 * n,)), pltpu.SemaphoreType.DMA((n,))],
        compiler_params=pltpu.CompilerParams(vmem_limit_bytes=VMEM_LIMIT),
    )(*shards)


N_DEV = 8
LOSS_TILE = (8, 128)


def _gather_first(lat, uq):
    hl, hu = lat.shape[1] // 2, uq.shape[1] // 2

    def body(lat_ref, uq_ref, wlat_ref, guq_ref, lat_buf, uq_buf, send_sems, recv_sems, local_sems):
        x, y, c, others = _place()
        me = 2 * x + y
        sibling = (x, y, 1 - c)
        lat_buf[...] = lat_ref[...].astype(BF16)
        uq_buf[...] = uq_ref[...].astype(BF16)

        def copy(src, dst, k, to):
            return pltpu.make_async_remote_copy(src_ref=src, dst_ref=dst, send_sem=send_sems.at[k],
                                                recv_sem=recv_sems.at[k], device_id=to, device_id_type=MESH)

        def uq_part(chip, half):
            return guq_ref.at[chip, :, pl.ds(half * hu, hu)]

        def lat_part(half):
            return wlat_ref.at[:, pl.ds(half * hl, hl)]

        own = pltpu.make_async_copy(uq_buf, guq_ref.at[me], local_sems.at[0])
        own.start()
        first = [copy(uq_buf.at[:, pl.ds(c * hu, hu)], uq_part(me, c), j, (*others[j], c)) for j in range(3)]
        for cp in first:
            cp.start()

        @pl.when(me == 0)
        def _():
            mine = pltpu.make_async_copy(lat_buf, wlat_ref, local_sems.at[1])
            mine.start()
            cps = [copy(lat_buf.at[:, pl.ds(c * hl, hl)], lat_part(c), 6 + j, (*others[j], c)) for j in range(3)]
            for cp in cps:
                cp.start()
            for cp in cps:
                cp.wait_send()
            mine.wait()

        @pl.when(me != 0)
        def _():
            j0 = x + 2 * y - 1
            copy(lat_part(c), lat_part(c), 6 + j0, (0, 0, c)).wait_recv()
            fwd = copy(lat_part(c), lat_part(c), 9, sibling)
            fwd.start()
            copy(lat_part(1 - c), lat_part(1 - c), 9, sibling).wait_recv()
            fwd.wait_send()

        fwd = []
        for j, (px, py) in enumerate(others):
            chip = 2 * px + py
            copy(uq_part(chip, c), uq_part(chip, c), j, (px, py, c)).wait_recv()
            cp = copy(uq_part(chip, c), uq_part(chip, c), 3 + j, sibling)
            cp.start()
            fwd.append(cp)
        for j, (px, py) in enumerate(others):
            chip = 2 * px + py
            copy(uq_part(chip, 1 - c), uq_part(chip, 1 - c), 3 + j, sibling).wait_recv()
        for cp in first + fwd:
            cp.wait_send()
        own.wait()

    return pl.pallas_call(
        body, name="gather_first", in_specs=[VMEM_SPEC, VMEM_SPEC], out_specs=[HBM_SPEC, HBM_SPEC],
        out_shape=[jax.ShapeDtypeStruct(lat.shape, BF16), jax.ShapeDtypeStruct((N_CHIPS,) + uq.shape, BF16)],
        scratch_shapes=[pltpu.VMEM(lat.shape, BF16), pltpu.VMEM(uq.shape, BF16), pltpu.SemaphoreType.DMA((10,)),
                        pltpu.SemaphoreType.DMA((10,)), pltpu.SemaphoreType.DMA((2,))],
        compiler_params=pltpu.CompilerParams(vmem_limit_bytes=VMEM_LIMIT),
    )(lat, uq)


def _cast_own(shards, me):
    n = len(shards)

    def body(me_ref, *refs):
        for w in range(n):
            refs[n + w][...] = refs[w][...].astype(BF16)

    return pl.pallas_call(
        body, name="cast_own",
        grid_spec=pltpu.PrefetchScalarGridSpec(
            num_scalar_prefetch=1, grid=(1,),
            in_specs=[pl.BlockSpec(s.shape, lambda i, me_ref: (0, 0)) for s in shards],
            out_specs=[pl.BlockSpec((None,) + s.shape, lambda i, me_ref: (me_ref[0], 0, 0)) for s in shards]),
        out_shape=[jax.ShapeDtypeStruct((N_CHIPS,) + s.shape, BF16) for s in shards],
        compiler_params=pltpu.CompilerParams(vmem_limit_bytes=VMEM_LIMIT),
    )(me, *shards)


def _gather_start(bufs, after):
    n = len(bufs)

    def body(*refs):
        b_refs = refs[:n]
        send_sems, recv_sems, token = refs[n + 1], refs[n + 2], refs[-1]
        x, y, c, others = _place()
        me = 2 * x + y
        for w in range(n):
            hc = _half(bufs[w])
            mine = b_refs[w].at[me, :, pl.ds(c * hc, hc)]
            for j, (px, py) in enumerate(others):
                pltpu.make_async_remote_copy(
                    src_ref=mine, dst_ref=mine, send_sem=send_sems.at[3 * w + j], recv_sem=recv_sems.at[3 * w + j],
                    device_id=(px, py, c), device_id_type=MESH).start()
        token[...] = jnp.zeros_like(token)

    hbm = [pltpu.HBM(b.shape, BF16) for b in bufs]
    outs = pl.pallas_call(
        body, name="gather_start",
        out_shape=(pltpu.SemaphoreType.DMA((3 * n,)), pltpu.SemaphoreType.DMA((3 * n,)), *hbm,
                   jax.ShapeDtypeStruct(LOSS_TILE, F32)),
        in_specs=[HBM_SPEC_STRICT] * n + [HBM_SPEC],
        out_specs=(SEM_SPEC, SEM_SPEC, *[HBM_SPEC_STRICT] * n, VMEM_SPEC),
        input_output_aliases={i: 2 + i for i in range(n)},
        compiler_params=pltpu.CompilerParams(has_side_effects=SPLIT_EFFECT),
    )(*[pltpu.with_memory_space_constraint(b, pltpu.HBM) for b in bufs], after)
    return outs[0], outs[1], list(outs[2:2 + n]), outs[-1]


def _gather_wait(send_sems, recv_sems, bufs, after):
    n = len(bufs)

    def body(*refs):
        b_refs = refs[:n]
        send_sems, recv_sems = refs[n], refs[n + 1]
        x, y, c, others = _place()
        me = 2 * x + y
        for w in range(n):
            hc = _half(bufs[w])
            for j, (px, py) in enumerate(others):
                cp = pltpu.make_async_remote_copy(
                    src_ref=b_refs[w].at[me, :, pl.ds(c * hc, hc)],
                    dst_ref=b_refs[w].at[2 * px + py, :, pl.ds(c * hc, hc)],
                    send_sem=send_sems.at[3 * w + j], recv_sem=recv_sems.at[3 * w + j], device_id=(px, py, c),
                    device_id_type=MESH)
                cp.wait_send()
                cp.wait_recv()

    outs = pl.pallas_call(
        body, name="gather_wait", out_shape=tuple(pltpu.HBM(b.shape, b.dtype) for b in bufs),
        in_specs=[HBM_SPEC_STRICT] * n + [SEM_SPEC, SEM_SPEC, HBM_SPEC],
        out_specs=tuple([HBM_SPEC_STRICT] * n), input_output_aliases={i: i for i in range(n)},
        compiler_params=pltpu.CompilerParams(has_side_effects=SPLIT_EFFECT),
    )(*bufs, send_sems, recv_sems, after)
    return list(outs)


def _gather_finish(bufs):
    n = len(bufs)

    def body(*refs):
        b_refs = refs[n:2 * n]
        send_sems, recv_sems = refs[2 * n:]
        x, y, c, others = _place()
        cps = []
        for w in range(n):
            hc = _half(bufs[w])
            for j, (px, py) in enumerate(others):
                part = b_refs[w].at[2 * px + py, :, pl.ds(c * hc, hc)]
                cps.append(pltpu.make_async_remote_copy(
                    src_ref=part, dst_ref=part, send_sem=send_sems.at[3 * w + j], recv_sem=recv_sems.at[3 * w + j],
                    device_id=(x, y, 1 - c), device_id_type=MESH))
        for cp in cps:
            cp.start()
        for w in range(n):
            hc = _half(bufs[w])
            for j, (px, py) in enumerate(others):
                theirs = b_refs[w].at[2 * px + py, :, pl.ds((1 - c) * hc, hc)]
                pltpu.make_async_remote_copy(
                    src_ref=theirs, dst_ref=theirs, send_sem=send_sems.at[3 * w + j], recv_sem=recv_sems.at[3 * w + j],
                    device_id=(x, y, 1 - c), device_id_type=MESH).wait_recv()
        for cp in cps:
            cp.wait_send()

    return pl.pallas_call(
        body, name="gather_finish", in_specs=[HBM_SPEC] * n, out_specs=[HBM_SPEC] * n,
        out_shape=[jax.ShapeDtypeStruct(b.shape, b.dtype) for b in bufs],
        input_output_aliases={i: i for i in range(n)},
        scratch_shapes=[pltpu.SemaphoreType.DMA((3 * n,)), pltpu.SemaphoreType.DMA((3 * n,))],
    )(*bufs)


def _half(a):
    return a.shape[-1] // 2


def _exchange_pairs(parts, name):
    n = len(parts)

    def body(*refs):
        p_refs, r_refs = refs[:n], refs[n:2 * n]
        send_sems, recv_sems = refs[2 * n:]
        x, y, c, _ = _place()
        cps = []
        for w in range(n):
            h = _half(parts[w])
            cps.append(pltpu.make_async_remote_copy(
                src_ref=p_refs[w].at[:, :, pl.ds((1 - c) * h, h)], dst_ref=r_refs[w],
                send_sem=send_sems.at[w], recv_sem=recv_sems.at[w], device_id=(x, y, 1 - c), device_id_type=MESH))
        for cp in cps:
            cp.start()
        for cp in cps:
            cp.wait()

    return pl.pallas_call(
        body, name=name, in_specs=[HBM_SPEC] * n, out_specs=[HBM_SPEC] * n,
        out_shape=[jax.ShapeDtypeStruct((N_CHIPS, p.shape[1], _half(p)), BF16) for p in parts],
        scratch_shapes=[pltpu.SemaphoreType.DMA((n,)), pltpu.SemaphoreType.DMA((n,))],
    )(*parts)


def _exchange_pairs_and_loss(parts, loss):
    n = len(parts)

    def body(*refs):
        p_refs, loss_ref = refs[:n], refs[n]
        r_refs, all_loss_ref = refs[n + 1:2 * n + 1], refs[2 * n + 1]
        send_sems, recv_sems, loss_send, loss_recv, local_sem = refs[2 * n + 2:]
        x, y, c, _ = _place()
        sibling = (x, y, 1 - c)
        cps = []
        for w in range(n):
            h = _half(parts[w])
            cps.append(pltpu.make_async_remote_copy(
                src_ref=p_refs[w].at[:, :, pl.ds((1 - c) * h, h)], dst_ref=r_refs[w],
                send_sem=send_sems.at[w], recv_sem=recv_sems.at[w], device_id=sibling, device_id_type=MESH))
        for cp in cps:
            cp.start()
        me = 4 * x + 2 * y + c
        own = pltpu.make_async_copy(loss_ref, all_loss_ref.at[me], local_sem)
        own.start()
        lcs = []
        for t in range(1, N_DEV):
            d = (me + t) % N_DEV
            lcs.append(pltpu.make_async_remote_copy(
                src_ref=loss_ref, dst_ref=all_loss_ref.at[me], send_sem=loss_send.at[t - 1],
                recv_sem=loss_recv.at[t - 1], device_id=(d // 4, (d // 2) % 2, d % 2), device_id_type=MESH))
        for cp in lcs:
            cp.start()
        for t in range(1, N_DEV):
            d = (me + N_DEV - t) % N_DEV
            pltpu.make_async_remote_copy(
                src_ref=loss_ref, dst_ref=all_loss_ref.at[d], send_sem=loss_send.at[t - 1],
                recv_sem=loss_recv.at[t - 1], device_id=(d // 4, (d // 2) % 2, d % 2), device_id_type=MESH).wait_recv()
        for cp in lcs:
            cp.wait_send()
        for cp in cps:
            cp.wait()
        own.wait()

    return pl.pallas_call(
        body, name="exchange_pairs_and_loss", in_specs=[HBM_SPEC] * (n + 1), out_specs=[HBM_SPEC] * (n + 1),
        out_shape=[jax.ShapeDtypeStruct((N_CHIPS, p.shape[1], _half(p)), BF16) for p in parts]
        + [jax.ShapeDtypeStruct((N_DEV,) + LOSS_TILE, F32)],
        scratch_shapes=[pltpu.SemaphoreType.DMA((n,)), pltpu.SemaphoreType.DMA((n,)),
                        pltpu.SemaphoreType.DMA((N_DEV - 1,)), pltpu.SemaphoreType.DMA((N_DEV - 1,)),
                        pltpu.SemaphoreType.DMA],
    )(*parts, loss)


def _add_pair_tiled(p, r, c):
    rows, h = r.shape[1:]

    def body(c_ref, p_ref, r_ref, q_ref):
        q_ref[...] = (p_ref[...].astype(F32) + r_ref[...].astype(F32)).astype(BF16)

    return pl.pallas_call(
        body, name="add_pair_w_in",
        grid_spec=pltpu.PrefetchScalarGridSpec(
            num_scalar_prefetch=1, grid=(N_CHIPS,),
            in_specs=[pl.BlockSpec((None, rows, h), lambda k, c_ref: (k, 0, c_ref[0])),
                      pl.BlockSpec((None, rows, h), lambda k, c_ref: (k, 0, 0))],
            out_specs=pl.BlockSpec((None, rows, h), lambda k, c_ref: (k, 0, 0))),
        out_shape=jax.ShapeDtypeStruct(r.shape, BF16),
    )(c, p, r)


def _add_pair_small(ps, rs, c, name):
    n = len(ps)

    def body(c_ref, *refs):
        for w in range(n):
            h = _half(ps[w])
            mine = refs[w][:, :, pl.ds(pl.multiple_of(c_ref[0] * h, 128), h)]
            refs[2 * n + w][...] = (mine.astype(F32) + refs[n + w][...].astype(F32)).astype(BF16)

    return pl.pallas_call(
        body, name=name,
        in_specs=[pl.BlockSpec(memory_space=pltpu.SMEM)] + [VMEM_SPEC] * (2 * n), out_specs=[VMEM_SPEC] * n,
        out_shape=[jax.ShapeDtypeStruct(r.shape, BF16) for r in rs],
        compiler_params=pltpu.CompilerParams(vmem_limit_bytes=VMEM_LIMIT),
    )(c, *ps, *rs)


def _exchange_chips(qs):
    n = len(qs)

    def body(*refs):
        q_refs, r_refs = refs[:n], refs[n:2 * n]
        send_sems, recv_sems = refs[2 * n:]
        x, y, c, others = _place()
        me = 2 * x + y
        cps = []
        for w in range(n):
            for j, (px, py) in enumerate(others):
                cps.append(pltpu.make_async_remote_copy(
                    src_ref=q_refs[w].at[2 * px + py], dst_ref=r_refs[w].at[me], send_sem=send_sems.at[3 * w + j],
                    recv_sem=recv_sems.at[3 * w + j], device_id=(px, py, c), device_id_type=MESH))
        for cp in cps:
            cp.start()
        for w in range(n):
            for j, (px, py) in enumerate(others):
                pltpu.make_async_remote_copy(
                    src_ref=q_refs[w].at[me], dst_ref=r_refs[w].at[2 * px + py], send_sem=send_sems.at[3 * w + j],
                    recv_sem=recv_sems.at[3 * w + j], device_id=(px, py, c), device_id_type=MESH).wait_recv()
        for cp in cps:
            cp.wait_send()

    return pl.pallas_call(
        body, name="exchange_chips", in_specs=[HBM_SPEC] * n, out_specs=[HBM_SPEC] * n,
        out_shape=[jax.ShapeDtypeStruct(q.shape, BF16) for q in qs],
        scratch_shapes=[pltpu.SemaphoreType.DMA((3 * n,)), pltpu.SemaphoreType.DMA((3 * n,))],
    )(*qs)


SEM_SPEC = pl.BlockSpec(memory_space=pltpu.SEMAPHORE)
SPLIT_EFFECT = pltpu.SideEffectType.DATAFLOW_SIDE_EFFECTING


def _chips_start(qs, name):
    n = len(qs)

    def body(*refs):
        q_refs, land_refs = refs[:n], refs[n:2 * n]
        send_sems, recv_sems, token = refs[2 * n], refs[2 * n + 1], refs[-1]
        x, y, c, others = _place()
        me = 2 * x + y
        for w in range(n):
            for j, (px, py) in enumerate(others):
                pltpu.make_async_remote_copy(
                    src_ref=q_refs[w].at[2 * px + py], dst_ref=land_refs[w].at[me], send_sem=send_sems.at[3 * w + j],
                    recv_sem=recv_sems.at[3 * w + j], device_id=(px, py, c), device_id_type=MESH).start()
        token[...] = jnp.zeros_like(token)

    hbm = [pltpu.HBM(q.shape, BF16) for q in qs]
    outs = pl.pallas_call(
        body, name=name,
        out_shape=(pltpu.SemaphoreType.DMA((3 * n,)), pltpu.SemaphoreType.DMA((3 * n,)), *hbm, *hbm,
                   jax.ShapeDtypeStruct(LOSS_TILE, F32)),
        in_specs=[HBM_SPEC_STRICT] * (2 * n),
        out_specs=(SEM_SPEC, SEM_SPEC, *[HBM_SPEC_STRICT] * (2 * n), VMEM_SPEC),
        input_output_aliases={i: 2 + i for i in range(2 * n)},
        compiler_params=pltpu.CompilerParams(has_side_effects=SPLIT_EFFECT),
    )(*[pltpu.with_memory_space_constraint(q, pltpu.HBM) for q in qs],
      *[pltpu.with_memory_space_constraint(lax.empty(q.shape, BF16), pltpu.HBM) for q in qs])
    return outs[0], outs[1], outs[2:2 + n], outs[2 + n:2 + 2 * n], outs[-1]


def _chips_wait(send_sems, recv_sems, q_thru, land_thru, after, name):
    n = len(q_thru)

    def body(*refs):
        q_refs, land_refs = refs[:n], refs[n:2 * n]
        send_sems, recv_sems = refs[2 * n], refs[2 * n + 1]
        x, y, c, others = _place()
        me = 2 * x + y
        for w in range(n):
            for j, (px, py) in enumerate(others):
                cp = pltpu.make_async_remote_copy(
                    src_ref=q_refs[w].at[2 * px + py], dst_ref=land_refs[w].at[2 * px + py],
                    send_sem=send_sems.at[3 * w + j], recv_sem=recv_sems.at[3 * w + j], device_id=(px, py, c),
                    device_id_type=MESH)
                cp.wait_send()
                cp.wait_recv()

    outs = pl.pallas_call(
        body, name=name, out_shape=tuple(pltpu.HBM(a.shape, a.dtype) for a in (*q_thru, *land_thru)),
        in_specs=[HBM_SPEC_STRICT] * (2 * n) + [SEM_SPEC, SEM_SPEC, HBM_SPEC],
        out_specs=tuple([HBM_SPEC_STRICT] * (2 * n)), input_output_aliases={i: i for i in range(2 * n)},
        compiler_params=pltpu.CompilerParams(has_side_effects=SPLIT_EFFECT),
    )(*q_thru, *land_thru, send_sems, recv_sems, after)
    return list(outs[:n]), list(outs[n:])


def _sum_chips_tiled(q, r, idx, tile):
    rows, h = r.shape[1:]
    nt = h // tile

    def body(idx_ref, q_ref, r0_ref, r1_ref, r2_ref, g_ref):
        g_ref[...] = (q_ref[...].astype(F32) + r0_ref[...].astype(F32) + r1_ref[...].astype(F32)
                      + r2_ref[...].astype(F32))

    def slab(t):
        return pl.BlockSpec((None, rows, tile), lambda i, idx_ref: (idx_ref[t], 0, i))

    return pl.pallas_call(
        body, name="sum_chips_w_in",
        grid_spec=pltpu.PrefetchScalarGridSpec(
            num_scalar_prefetch=1, grid=(nt,), in_specs=[slab(0), slab(1), slab(2), slab(3)],
            out_specs=pl.BlockSpec((rows, tile), lambda i, idx_ref: (0, idx_ref[4] * nt + i))),
        out_shape=jax.ShapeDtypeStruct((rows, 2 * h), F32),
    )(idx, q, r, r, r)


def _sum_chips_small(qs, rs, idx, all_dtypes):
    n = len(rs)
    n_all = len(all_dtypes)

    def body(idx_ref, *refs):
        c = idx_ref[4]
        for w in range(n):
            q_ref, r_ref, g_ref = refs[w], refs[n + w], refs[2 * n + w]
            acc = q_ref[idx_ref[0]].astype(F32)
            for t in range(1, N_CHIPS):
                acc = acc + r_ref[idx_ref[t]].astype(F32)
            h = rs[w].shape[2]
            mine = pl.ds(pl.multiple_of(c * h, 128), h)
            g_ref[...] = jnp.zeros_like(g_ref)
            if w >= n - n_all:
                g_ref[idx_ref[0], :, mine] = acc.astype(g_ref.dtype)
            else:
                g_ref[:, mine] = acc

    shapes = [jax.ShapeDtypeStruct((r.shape[1], 2 * r.shape[2]), F32) for r in rs[:n - n_all]]
    shapes += [jax.ShapeDtypeStruct((N_CHIPS, r.shape[1], 2 * r.shape[2]), dt)
               for r, dt in zip(rs[n - n_all:], all_dtypes)]
    return pl.pallas_call(
        body, name="sum_chips_small",
        in_specs=[pl.BlockSpec(memory_space=pltpu.SMEM)] + [VMEM_SPEC] * (2 * n), out_specs=[VMEM_SPEC] * n,
        out_shape=shapes, compiler_params=pltpu.CompilerParams(vmem_limit_bytes=VMEM_LIMIT),
    )(idx, *qs, *rs)


def _share(shards, alls):
    n, na = len(shards), len(alls)
    total = n + na

    def body(*refs):
        g_refs, a_refs = refs[total:total + n], refs[total + n:2 * total]
        send_sems, recv_sems = refs[2 * total:]
        x, y, c, others = _place()
        me = 2 * x + y
        sibling = (x, y, 1 - c)

        def cols_of(w, half):
            h = shards[w].shape[1] // 2
            return g_refs[w].at[:, pl.ds(half * h, h)]

        def slab(a, chip, half):
            h = alls[a].shape[2] // 2
            return a_refs[a].at[chip, :, pl.ds(half * h, h)]

        def copy(src, dst, k, to):
            return pltpu.make_async_remote_copy(src_ref=src, dst_ref=dst, send_sem=send_sems.at[k],
                                                recv_sem=recv_sems.at[k], device_id=to, device_id_type=MESH)

        cps = [copy(cols_of(w, c), cols_of(w, c), w, sibling) for w in range(n)]
        for a in range(na):
            base = n + 7 * a
            cps.append(copy(slab(a, me, c), slab(a, me, c), base, sibling))
            for j, (px, py) in enumerate(others):
                cps.append(copy(slab(a, me, c), slab(a, me, c), base + 1 + j, (px, py, c)))
        for cp in cps:
            cp.start()
        fwd = []
        for a in range(na):
            base = n + 7 * a
            for j, (px, py) in enumerate(others):
                chip = 2 * px + py
                copy(slab(a, me, c), slab(a, chip, c), base + 1 + j, (px, py, c)).wait_recv()
                cp = copy(slab(a, chip, c), slab(a, chip, c), base + 4 + j, sibling)
                cp.start()
                fwd.append(cp)
        for a in range(na):
            base = n + 7 * a
            for j, (px, py) in enumerate(others):
                chip = 2 * px + py
                copy(slab(a, chip, c), slab(a, chip, 1 - c), base + 4 + j, sibling).wait_recv()
            copy(slab(a, me, c), slab(a, me, 1 - c), base, sibling).wait_recv()
        for w in range(n):
            copy(cols_of(w, c), cols_of(w, 1 - c), w, sibling).wait_recv()
        for cp in cps + fwd:
            cp.wait_send()

    nsem = n + 7 * na
    return pl.pallas_call(
        body, name="share", in_specs=[HBM_SPEC] * total, out_specs=[HBM_SPEC] * total,
        out_shape=[jax.ShapeDtypeStruct(a.shape, a.dtype) for a in (*shards, *alls)],
        input_output_aliases={i: i for i in range(total)},
        scratch_shapes=[pltpu.SemaphoreType.DMA((nsem,)), pltpu.SemaphoreType.DMA((nsem,))],
    )(*shards, *alls)


def _adamw(w, g, m, v):
    m2 = ADAM_B1 * m + (1.0 - ADAM_B1) * g
    v2 = ADAM_B2 * v + (1.0 - ADAM_B2) * (g * g)
    m_hat = m2 / (1.0 - ADAM_B1 ** ADAM_STEP)
    v_hat = v2 / (1.0 - ADAM_B2 ** ADAM_STEP)
    return -ADAM_LR * (m_hat / (jnp.sqrt(v_hat) + ADAM_EPS) + ADAM_WD * w), m2, v2


def _update_w_in(wt, gt, mt, vt, lat, owner, tile):
    nlat = lat.shape[0] // tile

    def body(owner_ref, w_ref, g_ref, m_ref, v_ref, lat_ref, g2_ref, d_ref, m2_ref, v2_ref):
        row = pl.program_id(0) * tile + lax.broadcasted_iota(jnp.int32, (tile, 1), 0)
        g = jnp.where((row < LAT_COLS) & (owner_ref[0] == 1), lat_ref[...].astype(F32), g_ref[...])
        g2_ref[...] = g
        d_ref[...], m2_ref[...], v2_ref[...] = _adamw(w_ref[...], g, m_ref[...], v_ref[...])

    spec = pl.BlockSpec((tile, wt.shape[1]), lambda i, o: (i, 0))
    return pl.pallas_call(
        body, name="update_w_in",
        grid_spec=pltpu.PrefetchScalarGridSpec(
            num_scalar_prefetch=1, grid=(wt.shape[0] // tile,),
            in_specs=[spec] * 4 + [pl.BlockSpec((tile, wt.shape[1]), lambda i, o: (jnp.minimum(i, nlat - 1), 0))],
            out_specs=[spec] * 4),
        out_shape=[jax.ShapeDtypeStruct(wt.shape, F32)] * 4,
        compiler_params=_params(("parallel",)),
    )(owner, wt, gt, mt, vt, lat)


def _update_small(ws, gs, ms, vs):
    n = len(ws)

    def body(*refs):
        for k in range(n):
            w_ref, g_ref, m_ref, v_ref = refs[k], refs[n + k], refs[2 * n + k], refs[3 * n + k]
            d, m2, v2 = _adamw(w_ref[...], g_ref[...], m_ref[...], v_ref[...])
            refs[4 * n + k][...] = d
            refs[5 * n + k][...] = m2
            refs[6 * n + k][...] = v2

    shapes = [jax.ShapeDtypeStruct(w.shape, F32) for w in ws]
    outs = pl.pallas_call(
        body, name="update_small", in_specs=[VMEM_SPEC] * (4 * n), out_specs=[VMEM_SPEC] * (3 * n),
        out_shape=shapes * 3,
        compiler_params=pltpu.CompilerParams(vmem_limit_bytes=VMEM_LIMIT),
    )(*ws, *gs, *ms, *vs)
    return outs[:n], outs[n:2 * n], outs[2 * n:]


SHARDED = ("w_in", "w_uq", "w_oa", "w_ob", "w_out")
REPLICATED = ("b_in", "g_q", "g_kv", "w_ukv", "sgu_ln_g", "sgu_ln_b", "w_s", "b_s", "ln_g", "ln_b")
ORDER = ("w_in", "b_in", "g_q", "w_uq", "g_kv", "w_ukv", "w_oa", "sgu_ln_g", "sgu_ln_b", "w_s", "b_s", "w_ob", "w_out",
         "ln_g", "ln_b")


def kernel(x, positions, w_in, b_in, g_q, w_uq, g_kv, w_ukv, w_oa, sgu_ln_g, sgu_ln_b, w_s, b_s, w_ob, w_out, ln_g, ln_b, loss_target, m_w_in, m_b_in, m_g_q, m_w_uq, m_g_kv, m_w_ukv, m_w_oa, m_sgu_ln_g, m_sgu_ln_b, m_w_s, m_b_s, m_w_ob, m_w_out, m_ln_g, m_ln_b, v_w_in, v_b_in, v_g_q, v_w_uq, v_g_kv, v_w_ukv, v_w_oa, v_sgu_ln_g, v_sgu_ln_b, v_w_s, v_b_s, v_w_ob, v_w_out, v_ln_g, v_ln_b):
    w = dict(w_in=w_in, b_in=b_in, g_q=g_q, w_uq=w_uq, g_kv=g_kv, w_ukv=w_ukv, w_oa=w_oa, sgu_ln_g=sgu_ln_g,
             sgu_ln_b=sgu_ln_b, w_s=w_s, b_s=b_s, w_ob=w_ob, w_out=w_out, ln_g=ln_g, ln_b=ln_b)
    m = dict(w_in=m_w_in, b_in=m_b_in, g_q=m_g_q, w_uq=m_w_uq, g_kv=m_g_kv, w_ukv=m_w_ukv, w_oa=m_w_oa,
             sgu_ln_g=m_sgu_ln_g, sgu_ln_b=m_sgu_ln_b, w_s=m_w_s, b_s=m_b_s, w_ob=m_w_ob, w_out=m_w_out, ln_g=m_ln_g,
             ln_b=m_ln_b)
    v = dict(w_in=v_w_in, b_in=v_b_in, g_q=v_g_q, w_uq=v_w_uq, g_kv=v_g_kv, w_ukv=v_w_ukv, w_oa=v_w_oa,
             sgu_ln_g=v_sgu_ln_g, sgu_ln_b=v_sgu_ln_b, w_s=v_w_s, b_s=v_b_s, w_ob=v_w_ob, w_out=v_w_out, ln_g=v_ln_g,
             ln_b=v_ln_b)
    w, m, v = ({n: a[0] for n, a in d.items()} for d in (w, m, v))
    c = lax.axis_index("c")

    wt_shard, mt_shard, vt_shard = (jnp.transpose(d["w_in"]) for d in (w, m, v))
    xi, yi = lax.axis_index("x"), lax.axis_index("y")
    me1 = (2 * xi + yi).reshape(1).astype(jnp.int32)
    g_lat, g_uq = _gather_first(wt_shard[:LAT_COLS], w["w_uq"].reshape(Q_RANK // 4, HEADS * QK_DIM))
    bufs = _cast_own([wt_shard, w["w_oa"], w["w_ob"], w["w_out"]], me1)
    send0, recv0, bufs, token0 = _gather_start(bufs, g_lat)
    st = _local_attention(x[0], positions[0], g_lat, w["b_in"], w["g_q"], g_uq.reshape(Q_RANK, HEADS, QK_DIM),
                          w["g_kv"], w["w_ukv"], token0)
    g_in, g_oa, g_ob, g_out = _gather_finish(_gather_wait(send0, recv0, bufs, st["o"]))
    wt = g_in.reshape(IN_W, D_MODEL)

    loss, early, st = _local_head(
        st, x[0], loss_target[0], wt, g_oa, w["sgu_ln_g"], w["sgu_ln_b"], w["w_s"], w["b_s"], g_ob,
        g_out.reshape(D_MODEL, D_MODEL), w["ln_g"], w["ln_b"])

    xi, yi = lax.axis_index("x"), lax.axis_index("y")
    c1 = c.reshape(1).astype(jnp.int32)
    idx = jnp.stack([2 * xi + yi, 2 * (1 - xi) + yi, 2 * xi + (1 - yi), 2 * (1 - xi) + (1 - yi), c]).astype(jnp.int32)
    parts1 = [early["w_in"].reshape(N_CHIPS, IN_W // N_CHIPS, D_MODEL), early["w_oa"].astype(BF16),
              early["w_ob"].astype(BF16), early["w_out"].reshape(N_CHIPS, SLAB_W, D_MODEL).astype(BF16)]
    *recv1, all_loss = _exchange_pairs_and_loss(parts1, jnp.broadcast_to(loss, LOSS_TILE))
    pairs1 = [_add_pair_tiled(parts1[0], recv1[0], c1),
              *_add_pair_small(parts1[1:], recv1[1:], c1, "add_pair_early")]
    sems1 = _chips_start(pairs1, "chips_start_early")

    st["delta"] = st["delta"] + sems1[4][0, 0]
    dhl, late = _local_tail(st)

    grads = {**early, **late}
    rep = jnp.concatenate([_rows8(grads[n]) for n in REPLICATED], axis=0)
    rep = jnp.pad(rep, ((0, N_CHIPS * REP_ROWS - rep.shape[0]), (0, 0))).reshape(N_CHIPS, REP_ROWS, D_MODEL)
    parts2 = [late["w_uq"].reshape(N_CHIPS, Q_RANK // N_CHIPS, HEADS * QK_DIM).astype(BF16), rep.astype(BF16),
              late["w_lat"].reshape(N_CHIPS, LAT_ROWS_PAD // N_CHIPS, D_MODEL)]
    pairs2 = _add_pair_small(parts2, _exchange_pairs(parts2, "exchange_pairs_late"), c1, "add_pair_late")
    sems2 = _chips_start(pairs2, "chips_start_late")
    dx = _dx(st["dr"], st["dhg"], st["dhm"], dhl, st["wt"], sems2[4])
    pairs2, landed2 = _chips_wait(*sems2[:4], dx, "chips_wait_late")
    pairs1, landed1 = _chips_wait(*sems1[:4], landed2[0], "chips_wait_early")
    sums = [_sum_chips_tiled(pairs1[0], landed1[0], idx, 128),
            *_sum_chips_small([*pairs1[1:], *pairs2], [*landed1[1:], *landed2], idx, (F32, BF16))]
    *shards, g_rep, g_lat = _share(sums[:-2], sums[-2:])
    loss = jnp.sum(all_loss[:, 0, 0])

    red = {n: s.reshape(w[n].shape) for n, s in zip(("w_oa", "w_ob", "w_out", "w_uq"), shards[1:])}
    g_rep = g_rep.reshape(N_CHIPS * REP_ROWS, D_MODEL)
    off = 0
    for n in REPLICATED:
        rows = _rows8(w[n]).shape[0]
        red[n] = g_rep[off:off + rows].reshape(-1)[:w[n].size].reshape(w[n].shape)
        off += rows
    owner = (2 * xi + yi == 0).astype(jnp.int32).reshape(1)
    gt, dt, mt, vt2 = _update_w_in(wt_shard, shards[0], mt_shard, vt_shard,
                                   g_lat.reshape(LAT_ROWS_PAD, D_MODEL).astype(F32), owner, 232)
    red["w_in"] = jnp.transpose(gt)
    small = [n for n in ORDER if n != "w_in"]
    as2d = lambda a: a.reshape(-1, a.shape[-1])
    ds, ms, vs = _update_small([as2d(w[n]) for n in small], [as2d(red[n]) for n in small],
                               [as2d(m[n]) for n in small], [as2d(v[n]) for n in small])
    delta, new_m, new_v = {"w_in": jnp.transpose(dt)}, {"w_in": jnp.transpose(mt)}, {"w_in": jnp.transpose(vt2)}
    for i, n in enumerate(small):
        delta[n], new_m[n], new_v[n] = (a[i].reshape(w[n].shape) for a in (ds, ms, vs))

    lead = lambda a: a[None]
    return (loss, dx[None], *[lead(red[n]) for n in ORDER], *[lead(delta[n]) for n in ORDER],
            *[lead(new_m[n]) for n in ORDER], *[lead(new_v[n]) for n in ORDER])
```

```python
import functools
import math

import jax
import jax.numpy as jnp
from jax import lax
from jax.experimental import pallas as pl
from jax.experimental.pallas import tpu as pltpu

F32 = jnp.float32
BF16 = jnp.bfloat16

D_MODEL = 1024
HEADS = 8
Q_RANK = 384
KV_RANK = 128
NOPE = 64
ROPE = 32
V_DIM = 64
QK_DIM = NOPE + ROPE
HEAD_PAD = 128
MLA_W = HEADS * V_DIM
SGU_W = 512
GROUPS = 8
CHUNK = 128
IN_W = 4640
RMS_EPS = 1e-6
LN_EPS = 1e-5
ALPHA = 2.0 ** 0.25
ROPE_THETA = 10000.0
SCALE = QK_DIM ** -0.5

GATE_W = 2 * D_MODEL
MID_W = 4 * SGU_W
LAT_W = Q_RANK + KV_RANK + HEAD_PAD
PAD_W = GATE_W + MID_W + LAT_W
LAT_COLS = Q_RANK + KV_RANK + ROPE
ROW_GATE = LAT_COLS + MID_W
LAT_ROWS_PAD = 704
N_SLABS = 4
SLAB_W = D_MODEL // N_SLABS

ROW_TILE = 256
MATMUL_ROW_TILE = 512
ATT_TQ = 256
ATT_TK = 256
ATT_BWD_TQ = 256
ATT_BWD_TK = 256
LOG2E = 1.4426950408889634
LN2 = 0.6931471805599453
Q_SCALE = SCALE * LOG2E
VMEM_LIMIT = 56 * 1024 * 1024

ADAM_LR = 0.001
ADAM_B1 = 0.9
ADAM_B2 = 0.999
ADAM_EPS = 1e-08
ADAM_WD = 0.01
ADAM_STEP = 10


def _dot(a, b):
    return jnp.dot(a, b, preferred_element_type=F32)


def _dot_nt(a, b):
    return lax.dot_general(a, b, (((1,), (1,)), ((), ())), preferred_element_type=F32)


def _dot_tn(a, b):
    return lax.dot_general(a, b, (((0,), (0,)), ((), ())), preferred_element_type=F32)


def _sigmoid(z):
    return 0.5 * jnp.tanh(0.5 * z) + 0.5


_GELU_C = math.sqrt(2.0 / math.pi)


def _gelu_and_grad(x):
    x2 = x * x
    t = jnp.tanh(_GELU_C * (x + 0.044715 * x * x2))
    g = 0.5 * x * (1.0 + t)
    dg = 0.5 * (1.0 + t) + 0.5 * x * (1.0 - t * t) * (_GELU_C * (1.0 + 3.0 * 0.044715 * x2))
    return g, dg


def _silu_and_grad(z):
    s = _sigmoid(z)
    return z * s, s * (1.0 + z * (1.0 - s))


def _rope(xb, c, sl, sh):
    return xb * c + pltpu.roll(xb, 112, 1) * sl + pltpu.roll(xb, 16, 1) * sh


def _rope_t(dy, c, sl, sh):
    return dy * c + pltpu.roll(dy * sl, 16, 1) + pltpu.roll(dy * sh, 112, 1)


def _params(sem=("arbitrary",)):
    return pltpu.CompilerParams(dimension_semantics=sem, vmem_limit_bytes=VMEM_LIMIT)


def _row_spec(tile, width):
    return pl.BlockSpec((tile, width), lambda i: (i, 0))


def _full_spec(shape):
    nd = len(shape)
    return pl.BlockSpec(shape, lambda i: (0,) * nd)


def _kpe_rows(wt_ref):
    z = lambda n: jnp.zeros((n, D_MODEL), BF16)
    return jnp.concatenate([z(NOPE), wt_ref[Q_RANK + KV_RANK:LAT_COLS, :], z(HEAD_PAD - QK_DIM)], axis=0)


def _fwd_rest(xb, wt, b_g, b_m):
    s = xb.shape[0]
    ts = MATMUL_ROW_TILE

    def body(xb_ref, wt_ref, bg_ref, bm_ref, hg_ref, hm_ref):
        xb_ = xb_ref[...]
        hg_ref[...] = _dot_nt(xb_, wt_ref[ROW_GATE:IN_W, :]) + bg_ref[...]
        hm_ref[...] = _dot_nt(xb_, wt_ref[LAT_COLS:ROW_GATE, :]) + bm_ref[...]

    return pl.pallas_call(
        body, name="fwd_rest", grid=(s // ts,),
        in_specs=[_row_spec(ts, D_MODEL), _full_spec(wt.shape), _full_spec(b_g.shape), _full_spec(b_m.shape)],
        out_specs=[_row_spec(ts, GATE_W), _row_spec(ts, MID_W)],
        out_shape=[jax.ShapeDtypeStruct((s, GATE_W), F32), jax.ShapeDtypeStruct((s, MID_W), F32)],
        compiler_params=_params(),
    )(xb, wt, b_g, b_m)


def _fwd_lat(x, wlat, b_l, g_q, wuq, g_kv, wk, wv, rc, rsl, rsh, after):
    s = x.shape[0]
    ts = ROW_TILE

    def body(x_ref, wt_ref, bl_ref, gq_ref, wuq_ref, gkv_ref, wk_ref, wv_ref, rc_ref, rsl_ref,
             rsh_ref, after_ref, hl_ref, q_ref, k_ref, v_ref, xb_ref, qt_ref, kt_ref, vt_ref):
        xb = x_ref[...].astype(BF16)
        xb_ref[...] = xb
        hl = jnp.concatenate([_dot_nt(xb, wt_ref[0:Q_RANK + KV_RANK, :]), _dot_nt(xb, _kpe_rows(wt_ref))],
                             axis=1) + bl_ref[...]
        hl_ref[...] = hl
        c, sl, sh = rc_ref[...], rsl_ref[...], rsh_ref[...]
        cq = hl[:, :Q_RANK]
        cqn = cq * lax.rsqrt(jnp.mean(cq * cq, axis=-1, keepdims=True) + RMS_EPS) * gq_ref[...]
        q = _dot(cqn.astype(BF16), wuq_ref[...])
        ckv = hl[:, Q_RANK:Q_RANK + KV_RANK]
        ckvn = (ckv * lax.rsqrt(jnp.mean(ckv * ckv, axis=-1, keepdims=True) + RMS_EPS) * gkv_ref[...]).astype(BF16)
        k = _dot(ckvn, wk_ref[...])
        vb = _dot(ckvn, wv_ref[...]).astype(BF16)
        v_ref[...] = vb
        vt_ref[...] = vb.T
        kpe = _rope(hl[:, Q_RANK + KV_RANK:], c, sl, sh)
        for hd in range(HEADS):
            lanes = slice(hd * HEAD_PAD, (hd + 1) * HEAD_PAD)
            qb = (_rope(q[:, lanes], c, sl, sh) * Q_SCALE).astype(BF16)
            kb = (k[:, lanes] + kpe).astype(BF16)
            q_ref[:, lanes] = qb
            k_ref[:, lanes] = kb
            qt_ref[lanes, :] = qb.T
            kt_ref[lanes, :] = kb.T

    qk_w = HEADS * HEAD_PAD
    col_spec = lambda rows: pl.BlockSpec((rows, ts), lambda i: (0, i))
    return pl.pallas_call(
        body, name="fwd_lat", grid=(s // ts,),
        in_specs=[_row_spec(ts, D_MODEL), _full_spec(wlat.shape),
                  _full_spec(b_l.shape), _full_spec(g_q.shape),
                  _full_spec(wuq.shape), _full_spec(g_kv.shape), _full_spec(wk.shape), _full_spec(wv.shape),
                  _row_spec(ts, HEAD_PAD), _row_spec(ts, HEAD_PAD), _row_spec(ts, HEAD_PAD),
                  pl.BlockSpec(memory_space=pl.ANY)],
        out_specs=[_row_spec(ts, LAT_W), _row_spec(ts, qk_w),
                   _row_spec(ts, qk_w), _row_spec(ts, MLA_W), _row_spec(ts, D_MODEL), col_spec(qk_w), col_spec(qk_w),
                   col_spec(MLA_W)],
        out_shape=[jax.ShapeDtypeStruct((s, LAT_W), F32), jax.ShapeDtypeStruct((s, qk_w), BF16),
                   jax.ShapeDtypeStruct((s, qk_w), BF16), jax.ShapeDtypeStruct((s, MLA_W), BF16),
                   jax.ShapeDtypeStruct((s, D_MODEL), BF16), jax.ShapeDtypeStruct((qk_w, s), BF16),
                   jax.ShapeDtypeStruct((qk_w, s), BF16), jax.ShapeDtypeStruct((MLA_W, s), BF16)],
        compiler_params=_params(),
    )(x, wlat, b_l, g_q, wuq, g_kv, wk, wv, rc, rsl, rsh, after)


def _attn_fwd(qt, k, vt):
    s = k.shape[0]
    tq, tk = ATT_TQ, ATT_TK
    r = tq // tk
    pairs = HEADS // 2

    def body(qt_ref, k_ref, vt_ref, o_ref, lse_ref):
        i = pl.program_id(1)
        krow = lax.broadcasted_iota(jnp.int32, (tk, tq), 0)
        qcol = lax.broadcasted_iota(jnp.int32, (tk, tq), 1)
        qts = [qt_ref[hh * HEAD_PAD:(hh + 1) * HEAD_PAD, :] for hh in range(2)]

        def scores(j):
            koff = pl.multiple_of(j * tk, tk)
            return tuple(_dot(k_ref[pl.ds(koff, tk), hh * HEAD_PAD:(hh + 1) * HEAD_PAD], qts[hh]) for hh in range(2))

        def weighted(j, ps):
            koff = pl.multiple_of(j * tk, tk)
            return tuple(_dot(vt_ref[hh * V_DIM:(hh + 1) * V_DIM, pl.ds(koff, tk)], ps[hh]) for hh in range(2))

        def step(j, carry, diag, last):
            st, ps, stats = carry
            st_next = None if last else scores(j + 1)
            pvs = weighted(jnp.maximum(j - 1, 0), ps)
            new_ps, new_stats = [], []
            for hh in range(2):
                m, l, acc = stats[hh]
                s_ = st[hh]
                if diag is not None:
                    s_ = jnp.where(krow + diag * tk <= qcol, s_, -jnp.inf)
                m_new = jnp.maximum(m, jnp.max(s_, axis=0, keepdims=True))
                a = jnp.exp2(m - m_new)
                p = jnp.exp2(s_ - m_new)
                new_stats.append((m_new, a * l + jnp.sum(p, axis=0, keepdims=True), a * (acc + pvs[hh])))
                new_ps.append(p.astype(BF16))
            return st_next, tuple(new_ps), tuple(new_stats)

        one = (jnp.full((1, tq), -jnp.inf, F32), jnp.zeros((1, tq), F32), jnp.zeros((V_DIM, tq), F32))
        zero_p = jnp.zeros((tk, tq), BF16)
        nfull = i * r
        carry = lax.fori_loop(0, nfull, functools.partial(step, diag=None, last=False),
                              (scores(0), (zero_p, zero_p), (one, one)))
        for d in range(r):
            carry = step(nfull + d, carry, d, d == r - 1)
        _, ps, stats = carry
        pvs = weighted(nfull + r - 1, ps)
        ot = jnp.concatenate([(stats[hh][2] + pvs[hh]) / stats[hh][1] for hh in range(2)], axis=0)
        o_ref[...] = ot.T
        lse = [stats[hh][0] + jnp.log(stats[hh][1]) * LOG2E for hh in range(2)]
        lse_ref[...] = jnp.concatenate(lse + [jnp.zeros((6, tq), F32)], axis=0)

    return pl.pallas_call(
        body, name="attn_fwd", grid=(pairs, s // tq),
        in_specs=[pl.BlockSpec((2 * HEAD_PAD, tq), lambda p, i: (p, i)),
                  pl.BlockSpec((s, 2 * HEAD_PAD), lambda p, i: (0, p)),
                  pl.BlockSpec((2 * V_DIM, s), lambda p, i: (p, 0))],
        out_specs=[pl.BlockSpec((tq, 2 * V_DIM), lambda p, i: (i, p)),
                   pl.BlockSpec((None, 8, tq), lambda p, i: (p, 0, i))],
        out_shape=[jax.ShapeDtypeStruct((s, MLA_W), F32), jax.ShapeDtypeStruct((pairs, 8, s), F32)],
        compiler_params=_params(("arbitrary", "arbitrary")),
    )(qt, k, vt)


def _attn_bwd(q, qt, k, kt, v, do, dot, lse, delta):
    s = k.shape[0]
    tq, tk = ATT_BWD_TQ, ATT_BWD_TK
    r = tq // tk
    nq = s // tq
    nk = s // tk
    pairs = HEADS // 2

    def body(q_ref, qt_ref, k_ref, kt_ref, v_ref, do_ref, dot_ref, lse_ref, dl_ref, dqt_ref, dk_ref, dv_ref):
        j = pl.program_id(1)
        krow = lax.broadcasted_iota(jnp.int32, (tk, tq), 0)
        qcol = lax.broadcasted_iota(jnp.int32, (tk, tq), 1)
        lane = lax.broadcasted_iota(jnp.int32, (tk, 2 * V_DIM), 1)
        drow = lax.broadcasted_iota(jnp.int32, (2 * V_DIM, tq), 0)

        @pl.when(j == 0)
        def _():
            dqt_ref[...] = jnp.zeros_like(dqt_ref)

        koff = pl.multiple_of(j * tk, tk)
        vb = v_ref[pl.ds(koff, tk), :]
        kbs = [k_ref[pl.ds(koff, tk), hh * HEAD_PAD:(hh + 1) * HEAD_PAD] for hh in range(2)]
        ktbs = [kt_ref[hh * HEAD_PAD:(hh + 1) * HEAD_PAD, pl.ds(koff, tk)] for hh in range(2)]
        i0 = j // r

        def front(i):
            qoff = pl.multiple_of(i * tq, tq)
            dotb = dot_ref[:, pl.ds(qoff, tq)]
            out = []
            for hh in range(2):
                mine = (drow < V_DIM) if hh == 0 else (drow >= V_DIM)
                st = _dot(kbs[hh], qt_ref[hh * HEAD_PAD:(hh + 1) * HEAD_PAD, pl.ds(qoff, tq)])
                out.append((st, _dot(vb, jnp.where(mine, dotb, jnp.zeros_like(dotb)))))
            return tuple(out)

        def middle(i, tiles, diag):
            qoff = pl.multiple_of(i * tq, tq)
            out = []
            for hh in range(2):
                st, dpt = tiles[hh]
                if diag:
                    st = jnp.where(krow + (j - i0 * r) * tk <= qcol, st, -jnp.inf)
                p = jnp.exp2(st - lse_ref[hh:hh + 1, pl.ds(qoff, tq)])
                out.append((p.astype(BF16), (p * (dpt - dl_ref[hh:hh + 1, pl.ds(qoff, tq)])).astype(BF16)))
            return tuple(out)

        def back(i, pd, accs):
            qoff = pl.multiple_of(i * tq, tq)
            dob = do_ref[pl.ds(qoff, tq), :]
            out = []
            for hh in range(2):
                rows = slice(hh * HEAD_PAD, (hh + 1) * HEAD_PAD)
                p, dst = pd[hh]
                dk_acc, dv_acc = accs[hh]
                dv_acc = dv_acc + _dot(p, dob)
                dk_acc = dk_acc + _dot(dst, q_ref[pl.ds(qoff, tq), rows])
                dqt_ref[rows, pl.ds(qoff, tq)] += _dot(ktbs[hh], dst)
                out.append((dk_acc, dv_acc))
            return tuple(out)

        def step(i, accs, diag):
            return back(i, middle(i, front(i), diag), accs)

        zero_acc = (jnp.zeros((tk, HEAD_PAD), F32), jnp.zeros((tk, 2 * V_DIM), F32))
        accs = step(i0, (zero_acc, zero_acc), True)
        accs = lax.fori_loop(i0 + 1, nq, functools.partial(step, diag=False), accs)
        for hh in range(2):
            dk_ref[:, hh * HEAD_PAD:(hh + 1) * HEAD_PAD] = accs[hh][0] * LN2
        dv_ref[...] = jnp.where(lane < V_DIM, accs[0][1], accs[1][1])

        @pl.when(j == nk - 1)
        def _():
            dqt_ref[...] = dqt_ref[...] * SCALE

    pair_rows = lambda w: pl.BlockSpec((s, w), lambda p, j: (0, p))
    pair_cols = lambda w: pl.BlockSpec((w, s), lambda p, j: (p, 0))
    stats = pl.BlockSpec((None, 8, s), lambda p, j: (p, 0, 0))
    return pl.pallas_call(
        body, name="attn_bwd", grid=(pairs, nk),
        in_specs=[pair_rows(2 * HEAD_PAD), pair_cols(2 * HEAD_PAD), pair_rows(2 * HEAD_PAD), pair_cols(2 * HEAD_PAD),
                  pair_rows(2 * V_DIM), pair_rows(2 * V_DIM), pair_cols(2 * V_DIM), stats, stats],
        out_specs=[pair_cols(2 * HEAD_PAD),
                   pl.BlockSpec((tk, 2 * HEAD_PAD), lambda p, j: (j, p)),
                   pl.BlockSpec((tk, 2 * V_DIM), lambda p, j: (j, p))],
        out_shape=[jax.ShapeDtypeStruct((HEADS * HEAD_PAD, s), F32), jax.ShapeDtypeStruct((s, HEADS * HEAD_PAD), F32),
                   jax.ShapeDtypeStruct((s, MLA_W), F32)],
        compiler_params=_params(("arbitrary", "arbitrary")),
    )(q, qt, k, kt, v, do, dot, lse, delta)


def _split3(a):
    hi = a.astype(BF16)
    r1 = a - hi.astype(F32)
    mid = r1.astype(BF16)
    lo = (r1 - mid.astype(F32)).astype(BF16)
    return hi, mid, lo


def _mid(x, tgt, o, hm, hg, woa, wob, wout, ln_g, ln_b, sg_g, sg_b, w_s, bsb):
    s = x.shape[0]
    ts = ROW_TILE
    nsteps = s // ts
    nch = ts // CHUNK
    npair = GROUPS // 2

    def body(x_ref, t_ref, o_ref, hm_ref, hg_ref, woa_ref, wob_ref, wout_ref, lng_ref, lnb_ref, sgg_ref, sgb_ref,
             ws_ref, bsb_ref,
             dr_ref, dhg_ref, dhm_ref, do_ref, dot_ref, dl_ref, dhgt_ref, dhmt_ref,
             dwout_ref, dwoa_ref, dwob_ref, dws_ref, dbs_ref, dlng_ref, dlnb_ref, dsgg_ref, dsgb_ref, loss_ref,
             dbg_ref, dbm_ref, dbacc_ref):
        i = pl.program_id(0)

        @pl.when(i == 0)
        def _():
            for r in (dwout_ref, dwoa_ref, dwob_ref, dws_ref, dlng_ref, dlnb_ref, dsgg_ref, dsgb_ref, loss_ref,
                      dbg_ref, dbm_ref, dbacc_ref):
                r[...] = jnp.zeros_like(r)

        def emit(ref, tref, bref, lo, val):
            vb = val.astype(BF16)
            n = val.shape[1]
            ref[:, lo:lo + n] = vb
            tref[lo:lo + n, :] = vb.T
            bref[:, lo:lo + n] += jnp.sum(val, axis=0, keepdims=True)

        lane = lax.broadcasted_iota(jnp.int32, (CHUNK, CHUNK), 1)
        left = lane < V_DIM
        tril = lax.broadcasted_iota(jnp.int32, (CHUNK, CHUNK), 0) >= lane
        ms = [jnp.where(tril, ws_ref[g], 0.0).astype(BF16) for g in range(GROUPS)]

        z_a = hm_ref[:, 0:SGU_W]
        u = hm_ref[:, SGU_W:2 * SGU_W]
        v = hm_ref[:, 2 * SGU_W:3 * SGU_W]
        z_b = hm_ref[:, 3 * SGU_W:4 * SGU_W]
        o = o_ref[...]
        sa, dsa = _silu_and_grad(z_a)
        y_a = (o * sa).astype(BF16)
        gu, dgu = _gelu_and_grad(u)
        gv, dgv = _gelu_and_grad(v)
        mu = jnp.mean(gv, axis=-1, keepdims=True)
        vc = gv - mu
        rstd_v = lax.rsqrt(jnp.mean(vc * vc, axis=-1, keepdims=True) + LN_EPS)
        vhat = vc * rstd_v
        vn = (vhat * sgg_ref[...] + sgb_ref[...]).astype(BF16)
        rows = []
        for c in range(nch):
            blocks = []
            for p in range(npair):
                blk = vn[c * CHUNK:(c + 1) * CHUNK, p * CHUNK:(p + 1) * CHUNK]
                blocks.append(jnp.where(left, _dot(ms[2 * p], blk), _dot(ms[2 * p + 1], blk)))
            rows.append(jnp.concatenate(blocks, axis=1) + bsb_ref[...])
        mixed = jnp.concatenate(rows, axis=0)
        sgu = gu * mixed
        sb, dsb = _silu_and_grad(z_b)
        y_b = (sgu * sb).astype(BF16)
        pa = jnp.concatenate([_dot(y_a, woa_ref[k]) for k in range(N_SLABS)], axis=1)
        pb = jnp.concatenate([_dot(y_b, wob_ref[k]) for k in range(N_SLABS)], axis=1)
        sga = _sigmoid(hg_ref[:, :D_MODEL])
        sgb = _sigmoid(hg_ref[:, D_MODEL:])
        m2 = (sga * pa + sgb * pb).astype(BF16)
        r = ALPHA * x_ref[...] + _dot(m2, wout_ref[...])
        rmu = jnp.mean(r, axis=-1, keepdims=True)
        rc = r - rmu
        rstd = lax.rsqrt(jnp.mean(rc * rc, axis=-1, keepdims=True) + LN_EPS)
        xhat = rc * rstd
        y = xhat * lng_ref[...] + lnb_ref[...]
        err = y - t_ref[...]
        loss_ref[...] += jnp.full(loss_ref.shape, 0.5 / D_MODEL, F32) * jnp.sum(err * err)

        dy = err * (1.0 / D_MODEL)
        dlng_ref[...] += jnp.sum(dy * xhat, axis=0, keepdims=True)
        dlnb_ref[...] += jnp.sum(dy, axis=0, keepdims=True)
        dxh = dy * lng_ref[...]
        dr = rstd * (dxh - jnp.mean(dxh, axis=-1, keepdims=True) - xhat * jnp.mean(dxh * xhat, axis=-1, keepdims=True))
        dr_ref[...] = dr
        drb = dr.astype(BF16)
        dwout_ref[...] += _dot_tn(m2, drb)
        dm2 = _dot_nt(drb, wout_ref[...])
        emit(dhg_ref, dhgt_ref, dbg_ref, 0, dm2 * pa * sga * (1.0 - sga))
        emit(dhg_ref, dhgt_ref, dbg_ref, D_MODEL, dm2 * pb * sgb * (1.0 - sgb))
        dpa = (dm2 * sga).astype(BF16)
        dpb = (dm2 * sgb).astype(BF16)
        dy_a = jnp.zeros((ts, MLA_W), F32)
        dy_b = jnp.zeros((ts, SGU_W), F32)
        for k in range(N_SLABS):
            cols = slice(k * SLAB_W, (k + 1) * SLAB_W)
            dwoa_ref[k] += _dot_tn(y_a, dpa[:, cols])
            dwob_ref[k] += _dot_tn(y_b, dpb[:, cols])
            dy_a = dy_a + _dot_nt(dpa[:, cols], woa_ref[k])
            dy_b = dy_b + _dot_nt(dpb[:, cols], wob_ref[k])
        dob = (dy_a * sa).astype(BF16)
        do_ref[...] = dob
        dot_ref[...] = dob.T
        head = (lax.broadcasted_iota(jnp.int32, (HEADS, MLA_W), 1) // V_DIM
                == lax.broadcasted_iota(jnp.int32, (HEADS, MLA_W), 0)).astype(BF16)
        dl_ref[...] = sum(_dot_nt(head, term) for term in _split3(dob.astype(F32) * o))
        emit(dhm_ref, dhmt_ref, dbm_ref, 0, dy_a * o * dsa)
        dsg = dy_b * sb
        emit(dhm_ref, dhmt_ref, dbm_ref, 3 * SGU_W, dy_b * sgu * dsb)
        emit(dhm_ref, dhmt_ref, dbm_ref, SGU_W, dsg * mixed * dgu)
        dmixed = dsg * gu
        dvn_rows = []
        dbs_sum = jnp.zeros((CHUNK, SGU_W), F32)
        for c in range(nch):
            dm_c = dmixed[c * CHUNK:(c + 1) * CHUNK, :]
            dbs_sum = dbs_sum + dm_c
            blocks = []
            for p in range(npair):
                dmb = dm_c[:, p * CHUNK:(p + 1) * CHUNK].astype(BF16)
                blk = vn[c * CHUNK:(c + 1) * CHUNK, p * CHUNK:(p + 1) * CHUNK]
                blocks.append(jnp.where(left, _dot_tn(ms[2 * p], dmb), _dot_tn(ms[2 * p + 1], dmb)))
                zero = jnp.zeros_like(dmb)
                dws_ref[2 * p] += jnp.where(tril, _dot_nt(jnp.where(left, dmb, zero), blk), 0.0)
                dws_ref[2 * p + 1] += jnp.where(tril, _dot_nt(jnp.where(left, zero, dmb), blk), 0.0)
            dvn_rows.append(jnp.concatenate(blocks, axis=1))
        dbacc_ref[...] += dbs_sum
        dvn = jnp.concatenate(dvn_rows, axis=0)
        dsgg_ref[...] += jnp.sum(dvn * vhat, axis=0, keepdims=True)
        dsgb_ref[...] += jnp.sum(dvn, axis=0, keepdims=True)
        dvh = dvn * sgg_ref[...]
        dgv_in = rstd_v * (dvh - jnp.mean(dvh, axis=-1, keepdims=True)
                           - vhat * jnp.mean(dvh * vhat, axis=-1, keepdims=True))
        emit(dhm_ref, dhmt_ref, dbm_ref, 2 * SGU_W, dgv_in * dgv)

        @pl.when(i == nsteps - 1)
        def _():
            grp = (lax.broadcasted_iota(jnp.int32, (SGU_W, CHUNK), 0) // V_DIM
                   == lax.broadcasted_iota(jnp.int32, (SGU_W, CHUNK), 1)).astype(BF16)
            hi, mid, lo = _split3(dbacc_ref[...])
            dbs_ref[...] = _dot(hi, grp) + _dot(mid, grp) + _dot(lo, grp)

    acc_shapes = [(D_MODEL, D_MODEL), woa.shape, wob.shape, (GROUPS, CHUNK, CHUNK), (CHUNK, CHUNK),
                  (1, D_MODEL), (1, D_MODEL), (1, SGU_W), (1, SGU_W), (1, 128), (1, GATE_W), (1, MID_W)]
    col_spec = lambda rows: pl.BlockSpec((rows, ts), lambda i: (0, i))
    return pl.pallas_call(
        body, name="mid", grid=(nsteps,),
        in_specs=[_row_spec(ts, D_MODEL), _row_spec(ts, D_MODEL), _row_spec(ts, MLA_W), _row_spec(ts, MID_W),
                  _row_spec(ts, GATE_W), _full_spec(woa.shape), _full_spec(wob.shape), _full_spec(wout.shape),
                  _full_spec(ln_g.shape), _full_spec(ln_b.shape), _full_spec(sg_g.shape), _full_spec(sg_b.shape),
                  _full_spec(w_s.shape), _full_spec(bsb.shape)],
        out_specs=[_row_spec(ts, D_MODEL), _row_spec(ts, GATE_W), _row_spec(ts, MID_W), _row_spec(ts, MLA_W),
                   col_spec(MLA_W), col_spec(HEADS), col_spec(GATE_W), col_spec(MID_W)]
        + [_full_spec(sh) for sh in acc_shapes],
        out_shape=[jax.ShapeDtypeStruct((s, D_MODEL), F32), jax.ShapeDtypeStruct((s, GATE_W), BF16),
                   jax.ShapeDtypeStruct((s, MID_W), BF16), jax.ShapeDtypeStruct((s, MLA_W), BF16),
                   jax.ShapeDtypeStruct((MLA_W, s), BF16), jax.ShapeDtypeStruct((HEADS, s), F32),
                   jax.ShapeDtypeStruct((GATE_W, s), BF16), jax.ShapeDtypeStruct((MID_W, s), BF16)]
        + [jax.ShapeDtypeStruct(sh, F32) for sh in acc_shapes],
        scratch_shapes=[pltpu.VMEM((CHUNK, SGU_W), F32)],
        compiler_params=_params(),
    )(x, tgt, o, hm, hg, woa, wob, wout, ln_g, ln_b, sg_g, sg_b, w_s, bsb)


def _lat_bwd(dq, dk, dv, hl, rc, rsl, rsh, g_q, g_kv, wuq, wk, wv, after):
    s = dk.shape[0]
    ts = ROW_TILE
    qk_w = HEADS * HEAD_PAD

    def body(dq_ref, dk_ref, dv_ref, hl_ref, rc_ref, rsl_ref, rsh_ref, gq_ref, gkv_ref, wuq_ref, wk_ref, wv_ref,
             after_ref, dhl_ref, dhlt_ref, dwuq_ref, dwk_ref, dwv_ref, dgq_ref, dgkv_ref, dbl_ref):
        i = pl.program_id(0)

        @pl.when(i == 0)
        def _():
            for r in (dwuq_ref, dwk_ref, dwv_ref, dgq_ref, dgkv_ref, dbl_ref):
                r[...] = jnp.zeros_like(r)

        def emit(lo, val):
            vb = val.astype(BF16)
            n = val.shape[1]
            dhl_ref[:, lo:lo + n] = vb
            dhlt_ref[lo:lo + n, :] = vb.T
            dbl_ref[:, lo:lo + n] += jnp.sum(val, axis=0, keepdims=True)

        c, sl, sh = rc_ref[...], rsl_ref[...], rsh_ref[...]
        lane = lax.broadcasted_iota(jnp.int32, (ts, HEAD_PAD), 1)
        pe = (lane >= NOPE) & (lane < QK_DIM)
        dkpe = jnp.zeros((ts, HEAD_PAD), F32)
        dqu = []
        for hd in range(HEADS):
            lanes = slice(hd * HEAD_PAD, (hd + 1) * HEAD_PAD)
            dqu.append(_rope_t(dq_ref[lanes, :].T, c, sl, sh).astype(BF16))
            dkpe = dkpe + dk_ref[:, lanes]
        dqu = jnp.concatenate(dqu, axis=1)
        dkpe = _rope_t(jnp.where(pe, dkpe, 0.0), c, sl, sh)

        cq = hl_ref[:, :Q_RANK]
        rq = lax.rsqrt(jnp.mean(cq * cq, axis=-1, keepdims=True) + RMS_EPS)
        cqh = cq * rq
        cqn = (cqh * gq_ref[...]).astype(BF16)
        dwuq_ref[...] += _dot_tn(cqn, dqu)
        dcqn = _dot_nt(dqu, wuq_ref[...])
        dgq_ref[...] += jnp.sum(dcqn * cqh, axis=0, keepdims=True)
        dch = dcqn * gq_ref[...]
        emit(0, rq * (dch - cqh * jnp.mean(dch * cqh, axis=-1, keepdims=True)))

        ckv = hl_ref[:, Q_RANK:Q_RANK + KV_RANK]
        rk = lax.rsqrt(jnp.mean(ckv * ckv, axis=-1, keepdims=True) + RMS_EPS)
        ckh = ckv * rk
        ckn = (ckh * gkv_ref[...]).astype(BF16)
        dkb = dk_ref[...].astype(BF16)
        dvb = dv_ref[...].astype(BF16)
        dwk_ref[...] += _dot_tn(ckn, dkb)
        dwv_ref[...] += _dot_tn(ckn, dvb)
        dckn = _dot_nt(dkb, wk_ref[...]) + _dot_nt(dvb, wv_ref[...])
        dgkv_ref[...] += jnp.sum(dckn * ckh, axis=0, keepdims=True)
        dkh = dckn * gkv_ref[...]
        emit(Q_RANK, rk * (dkh - ckh * jnp.mean(dkh * ckh, axis=-1, keepdims=True)))
        emit(Q_RANK + KV_RANK, dkpe)

    acc_shapes = [wuq.shape, wk.shape, wv.shape, g_q.shape, g_kv.shape, (1, LAT_W)]
    return pl.pallas_call(
        body, name="lat_bwd", grid=(s // ts,),
        in_specs=[pl.BlockSpec((qk_w, ts), lambda i: (0, i)), _row_spec(ts, qk_w), _row_spec(ts, MLA_W),
                  _row_spec(ts, LAT_W), _row_spec(ts, HEAD_PAD), _row_spec(ts, HEAD_PAD), _row_spec(ts, HEAD_PAD),
                  _full_spec(g_q.shape), _full_spec(g_kv.shape), _full_spec(wuq.shape), _full_spec(wk.shape),
                  _full_spec(wv.shape), pl.BlockSpec(memory_space=pl.ANY)],
        out_specs=[_row_spec(ts, LAT_W), pl.BlockSpec((LAT_W, ts), lambda i: (0, i))]
        + [_full_spec(sh) for sh in acc_shapes],
        out_shape=[jax.ShapeDtypeStruct((s, LAT_W), BF16), jax.ShapeDtypeStruct((LAT_W, s), BF16)]
        + [jax.ShapeDtypeStruct(sh, F32) for sh in acc_shapes],
        compiler_params=_params(),
    )(dq, dk, dv, hl, rc, rsl, rsh, g_q, g_kv, wuq, wk, wv, after)


def _dx(dr, dhg, dhm, dhl, wt, after):
    s = dr.shape[0]
    ts = MATMUL_ROW_TILE

    def body(dr_ref, dhg_ref, dhm_ref, dhl_ref, wt_ref, after_ref, dx_ref):
        dx_ref[...] = (ALPHA * dr_ref[...]
                       + _dot(dhg_ref[...], wt_ref[ROW_GATE:IN_W, :])
                       + _dot(dhm_ref[...], wt_ref[LAT_COLS:ROW_GATE, :])
                       + _dot(dhl_ref[:, 0:Q_RANK + KV_RANK], wt_ref[0:Q_RANK + KV_RANK, :])
                       + _dot(dhl_ref[:, Q_RANK + KV_RANK:], _kpe_rows(wt_ref)))

    return pl.pallas_call(
        body, name="dx", grid=(s // ts,),
        in_specs=[_row_spec(ts, D_MODEL), _row_spec(ts, GATE_W), _row_spec(ts, MID_W), _row_spec(ts, LAT_W),
                  _full_spec(wt.shape), pl.BlockSpec(memory_space=pl.ANY)],
        out_specs=_row_spec(ts, D_MODEL),
        out_shape=jax.ShapeDtypeStruct((s, D_MODEL), F32),
        compiler_params=_params(),
    )(dr, dhg, dhm, dhl, wt, after)


def _dwt_early(dhmt, dhgt, xb):
    tn = 512
    nm, ng = MID_W // tn, GATE_W // tn
    s = dhmt.shape[1]

    def body(dm_ref, dg_ref, xb_ref, dw_ref):
        i = pl.program_id(0)

        @pl.when(i < nm)
        def _():
            dw_ref[...] = _dot(dm_ref[...], xb_ref[...]).astype(BF16)

        @pl.when(i >= nm)
        def _():
            dw_ref[...] = _dot(dg_ref[...], xb_ref[...]).astype(BF16)

    rows = pl.pallas_call(
        body, name="dwt_early", grid=(nm + ng,),
        in_specs=[pl.BlockSpec((tn, s), lambda i: (jnp.minimum(i, nm - 1), 0)),
                  pl.BlockSpec((tn, s), lambda i: (jnp.maximum(i - nm, 0), 0)), _full_spec(xb.shape)],
        out_specs=pl.BlockSpec((pl.Element(tn), pl.Element(D_MODEL)), lambda i: (pl.multiple_of(LAT_COLS + i * tn, 32), 0)),
        out_shape=jax.ShapeDtypeStruct((IN_W, D_MODEL), BF16),
        compiler_params=_params(),
    )(dhmt, dhgt, xb)

    def zero(buf_ref, out_ref):
        out_ref[...] = jnp.zeros_like(out_ref)

    return pl.pallas_call(
        zero, name="dwt_zero_lat", grid=(1,), in_specs=[pl.BlockSpec(memory_space=pl.ANY)],
        out_specs=pl.BlockSpec((LAT_COLS, D_MODEL), lambda i: (0, 0)),
        out_shape=jax.ShapeDtypeStruct((IN_W, D_MODEL), BF16), input_output_aliases={0: 0},
    )(rows)


def _dwt_lat(dhlt, xb):
    n, s = dhlt.shape

    def body(dht_ref, xb_ref, dw_ref):
        dw = _dot(dht_ref[...], xb_ref[...]).astype(BF16)
        kpe = Q_RANK + KV_RANK + NOPE
        dw_ref[0:Q_RANK + KV_RANK, :] = dw[0:Q_RANK + KV_RANK]
        dw_ref[Q_RANK + KV_RANK:LAT_COLS, :] = dw[kpe:kpe + ROPE]
        dw_ref[LAT_COLS:, :] = jnp.zeros((LAT_ROWS_PAD - LAT_COLS, D_MODEL), BF16)

    return pl.pallas_call(
        body, name="dwt_lat", in_specs=[VMEM_SPEC, VMEM_SPEC], out_specs=VMEM_SPEC,
        out_shape=jax.ShapeDtypeStruct((LAT_ROWS_PAD, D_MODEL), BF16),
        compiler_params=pltpu.CompilerParams(vmem_limit_bytes=VMEM_LIMIT),
    )(dhlt, xb)


def _split_bias(b):
    z = lambda n: jnp.zeros((n,), b.dtype)
    lat = jnp.concatenate([b[:Q_RANK + KV_RANK], z(NOPE), b[Q_RANK + KV_RANK:LAT_COLS], z(HEAD_PAD - QK_DIM)])
    return b[None, ROW_GATE:], b[None, LAT_COLS:ROW_GATE], lat[None, :]


def _join_bias(g, m, l):
    kpe = Q_RANK + KV_RANK + NOPE
    return jnp.concatenate([l[0, :Q_RANK + KV_RANK], l[0, kpe:kpe + ROPE], m[0], g[0]])


def _rope_tables(positions):
    half = ROPE // 2
    inv_freq = ROPE_THETA ** (-jnp.arange(0, ROPE, 2, dtype=F32) / ROPE)
    ang = positions.astype(F32)[:, None] * inv_freq
    cos, sin = jnp.cos(ang), jnp.sin(ang)
    n = positions.shape[0]
    one, zero = jnp.ones((n, NOPE), F32), jnp.zeros((n, half), F32)
    tail1, tail0 = jnp.ones((n, HEAD_PAD - QK_DIM), F32), jnp.zeros((n, HEAD_PAD - QK_DIM), F32)
    z64 = jnp.zeros((n, NOPE), F32)
    rc = jnp.concatenate([one, cos, cos, tail1], axis=1)
    rsl = jnp.concatenate([z64, -sin, zero, tail0], axis=1)
    rsh = jnp.concatenate([z64, zero, sin, tail0], axis=1)
    return rc, rsl, rsh


def _local_attention(x, positions, wlat, b_in, g_q, w_uq, g_kv, w_ukv, after):
    rc, rsl, rsh = _rope_tables(positions)
    b_g, b_m, b_l = _split_bias(b_in)
    wuq = jnp.pad(w_uq, ((0, 0), (0, 0), (0, HEAD_PAD - QK_DIM))).reshape(Q_RANK, HEADS * HEAD_PAD).astype(BF16)
    wk = jnp.pad(w_ukv[:, :, :NOPE], ((0, 0), (0, 0), (0, HEAD_PAD - NOPE))).reshape(KV_RANK, HEADS * HEAD_PAD).astype(BF16)
    wv = w_ukv[:, :, NOPE:].reshape(KV_RANK, MLA_W).astype(BF16)
    gq2, gkv2 = g_q[None, :], g_kv[None, :]
    hl, q, k, v, xb, qt, kt, vt = _fwd_lat(x, wlat, b_l, gq2, wuq, gkv2, wk, wv, rc, rsl, rsh, after)
    o, lse = _attn_fwd(qt, k, vt)
    return dict(q=q, qt=qt, k=k, kt=kt, v=v, o=o, lse=lse, hl=hl, rc=rc, rsl=rsl, rsh=rsh, gq2=gq2, gkv2=gkv2,
                wuq=wuq, wk=wk, wv=wv, xb=xb, b_g=b_g, b_m=b_m)


def _local_head(st, x, tgt, wt, w_oa, sg_g, sg_b, w_s, b_s, w_ob, w_out, ln_g, ln_b):
    q, qt, k, kt, v, o, lse, hl, xb = (st[n] for n in ("q", "qt", "k", "kt", "v", "o", "lse", "hl", "xb"))
    rc, rsl, rsh, gq2, gkv2, wuq, wk, wv = (st[n] for n in ("rc", "rsl", "rsh", "gq2", "gkv2", "wuq", "wk", "wv"))
    bsb = jnp.repeat(b_s.T, V_DIM, axis=1)
    hg, hm = _fwd_rest(xb, wt, st["b_g"], st["b_m"])
    (dr, dhg, dhm, do, dot, delta, dhgt, dhmt, dwout, dwoa, dwob, dws, dbs, dlng, dlnb, dsgg, dsgb, loss, dbg,
     dbm) = _mid(x, tgt, o, hm, hg, w_oa, w_ob, w_out, ln_g[None, :], ln_b[None, :], sg_g[None, :], sg_b[None, :],
                 w_s, bsb)
    delta = jnp.pad(delta.reshape(HEADS // 2, 2, -1), ((0, 0), (0, 6), (0, 0)))
    early = {
        "w_in": _dwt_early(dhmt, dhgt, xb),
        "w_oa": dwoa, "sgu_ln_g": dsgg[0], "sgu_ln_b": dsgb[0], "w_s": dws, "b_s": dbs[:, :GROUPS].T,
        "w_ob": dwob, "w_out": dwout, "ln_g": dlng[0], "ln_b": dlnb[0],
    }
    state = dict(q=q, qt=qt, k=k, kt=kt, v=v, do=do, dot=dot, lse=lse, delta=delta, hl=hl, rc=rc, rsl=rsl, rsh=rsh,
                 gq2=gq2, gkv2=gkv2, wuq=wuq, wk=wk, wv=wv, dr=dr, dhg=dhg, dhm=dhm, wt=wt, xb=xb, dbg=dbg, dbm=dbm)
    return loss, early, state


def _local_attn_bwd(st):
    return _attn_bwd(st["q"], st["qt"], st["k"], st["kt"], st["v"], st["do"], st["dot"], st["lse"], st["delta"])


def _local_tail(st, dq, dk, dv, after):
    dhl, dhlt, dwuq, dwk, dwv, dgq, dgkv, dbl = _lat_bwd(dq, dk, dv, st["hl"], st["rc"], st["rsl"], st["rsh"],
                                                         st["gq2"], st["gkv2"], st["wuq"], st["wk"], st["wv"], after)
    late = {
        "w_lat": _dwt_lat(dhlt, st["xb"]),
        "b_in": _join_bias(st["dbg"], st["dbm"], dbl),
        "g_q": dgq[0],
        "w_uq": dwuq.reshape(Q_RANK, HEADS, HEAD_PAD)[:, :, :QK_DIM],
        "g_kv": dgkv[0],
        "w_ukv": jnp.concatenate([dwk.reshape(KV_RANK, HEADS, HEAD_PAD)[:, :, :NOPE],
                                  dwv.reshape(KV_RANK, HEADS, V_DIM)], axis=2),
    }
    return dhl, late


def _local_step(x, positions, tgt, wt, b_in, g_q, w_uq, g_kv, w_ukv, w_oa, sg_g, sg_b, w_s, b_s, w_ob, w_out, ln_g,
                ln_b):
    st = _local_attention(x, positions, wt[:LAT_COLS], b_in, g_q, w_uq, g_kv, w_ukv, b_in)
    loss, early, st = _local_head(st, x, tgt, wt, w_oa, sg_g, sg_b, w_s, b_s, w_ob, w_out, ln_g, ln_b)
    dq, dk, dv = _local_attn_bwd(st)
    dhl, late = _local_tail(st, dq, dk, dv, dv)
    dx = _dx(st["dr"], st["dhg"], st["dhm"], dhl, st["wt"], dhl)
    grads = {**early, **late}
    grads["w_in"] = jnp.concatenate([grads.pop("w_lat")[:LAT_COLS], early["w_in"][LAT_COLS:]], axis=0)
    return loss, dx, grads


MESH = pl.DeviceIdType.MESH
N_CHIPS = 4
HBM_SPEC = pl.BlockSpec(memory_space=pl.ANY)
HBM_SPEC_STRICT = pl.BlockSpec(memory_space=pltpu.HBM)
VMEM_SPEC = pl.BlockSpec(memory_space=pltpu.VMEM)

REP_ROWS = 80


def _rows8(a):
    flat = a.reshape(-1)
    n = -(-flat.shape[0] // (8 * D_MODEL)) * 8 * D_MODEL
    return jnp.pad(flat, (0, n - flat.shape[0])).reshape(-1, D_MODEL)


def _place():
    x, y, c = lax.axis_index("x"), lax.axis_index("y"), lax.axis_index("c")
    others = [(1 - x, y), (x, 1 - y), (1 - x, 1 - y)]
    return x, y, c, others


def _gather_weights(shards):
    n = len(shards)

    def body(*refs):
        ins, outs, bufs = refs[:n], refs[n:2 * n], refs[2 * n:3 * n]
        send_sems, recv_sems, local_sems = refs[3 * n:]
        x, y, c, others = _place()
        me = 2 * x + y
        sibling = (x, y, 1 - c)
        for src, buf in zip(ins, bufs):
            buf[...] = src[...].astype(BF16)
        own = [pltpu.make_async_copy(bufs[w], outs[w].at[me], local_sems.at[w]) for w in range(n)]
        for cp in own:
            cp.start()

        def part(w, chip, half):
            hc = shards[w].shape[1] // 2
            return outs[w].at[chip, :, pl.ds(half * hc, hc)]

        def sent(w, j):
            hc = shards[w].shape[1] // 2
            return pltpu.make_async_remote_copy(
                src_ref=bufs[w].at[:, pl.ds(c * hc, hc)], dst_ref=part(w, me, c),
                send_sem=send_sems.at[w * 3 + j], recv_sem=recv_sems.at[w * 3 + j],
                device_id=(*others[j], c), device_id_type=MESH)

        def landed(w, j):
            px, py = others[j]
            return pltpu.make_async_remote_copy(
                src_ref=part(w, 2 * px + py, c), dst_ref=part(w, 2 * px + py, c),
                send_sem=send_sems.at[w * 3 + j], recv_sem=recv_sems.at[w * 3 + j],
                device_id=(px, py, c), device_id_type=MESH)

        def passed(w, j, half):
            px, py = others[j]
            k = n * 3 + w * 3 + j
            return pltpu.make_async_remote_copy(
                src_ref=part(w, 2 * px + py, half), dst_ref=part(w, 2 * px + py, half),
                send_sem=send_sems.at[k], recv_sem=recv_sems.at[k], device_id=sibling, device_id_type=MESH)

        first = [sent(w, j) for w in range(n) for j in range(3)]
        for cp in first:
            cp.start()
        fwd = []
        for w in range(n):
            for j in range(3):
                landed(w, j).wait_recv()
                cp = passed(w, j, c)
                cp.start()
                fwd.append(cp)
        for w in range(n):
            for j in range(3):
                passed(w, j, 1 - c).wait_recv()
        for cp in first + fwd:
            cp.wait_send()
        for cp in own:
            cp.wait()

    return pl.pallas_call(
        body, name="gather_weights",
        in_specs=[VMEM_SPEC] * n, out_specs=[HBM_SPEC] * n,
        out_shape=[jax.ShapeDtypeStruct((N_CHIPS,) + s.shape, BF16) for s in shards],
        scratch_shapes=[pltpu.VMEM(s.shape, BF16) for s in shards]
        + [pltpu.SemaphoreType.DMA((6 * n,)), pltpu.SemaphoreType.DMA((6 * n,)), pltpu.SemaphoreType.DMA((n,))],
        compiler_params=pltpu.CompilerParams(vmem_limit_bytes=VMEM_LIMIT),
    )(*shards)


N_DEV = 8
LOSS_TILE = (8, 128)


def _gather_first(lat, uq):
    hl, hu = lat.shape[1] // 2, uq.shape[1] // 2

    def body(lat_ref, uq_ref, wlat_ref, guq_ref, lat_buf, uq_buf, send_sems, recv_sems, local_sems):
        x, y, c, others = _place()
        me = 2 * x + y
        sibling = (x, y, 1 - c)
        lat_buf[...] = lat_ref[...].astype(BF16)
        uq_buf[...] = uq_ref[...].astype(BF16)

        def copy(src, dst, k, to):
            return pltpu.make_async_remote_copy(src_ref=src, dst_ref=dst, send_sem=send_sems.at[k],
                                                recv_sem=recv_sems.at[k], device_id=to, device_id_type=MESH)

        def uq_part(chip, half):
            return guq_ref.at[chip, :, pl.ds(half * hu, hu)]

        def lat_part(half):
            return wlat_ref.at[:, pl.ds(half * hl, hl)]

        own = pltpu.make_async_copy(uq_buf, guq_ref.at[me], local_sems.at[0])
        own.start()
        first = [copy(uq_buf.at[:, pl.ds(c * hu, hu)], uq_part(me, c), j, (*others[j], c)) for j in range(3)]
        for cp in first:
            cp.start()

        @pl.when(me == 0)
        def _():
            mine = pltpu.make_async_copy(lat_buf, wlat_ref, local_sems.at[1])
            mine.start()
            cps = [copy(lat_buf.at[:, pl.ds(c * hl, hl)], lat_part(c), 6 + j, (*others[j], c)) for j in range(3)]
            for cp in cps:
                cp.start()
            for cp in cps:
                cp.wait_send()
            mine.wait()

        @pl.when(me != 0)
        def _():
            j0 = x + 2 * y - 1
            copy(lat_part(c), lat_part(c), 6 + j0, (0, 0, c)).wait_recv()
            fwd = copy(lat_part(c), lat_part(c), 9, sibling)
            fwd.start()
            copy(lat_part(1 - c), lat_part(1 - c), 9, sibling).wait_recv()
            fwd.wait_send()

        fwd = []
        for j, (px, py) in enumerate(others):
            chip = 2 * px + py
            copy(uq_part(chip, c), uq_part(chip, c), j, (px, py, c)).wait_recv()
            cp = copy(uq_part(chip, c), uq_part(chip, c), 3 + j, sibling)
            cp.start()
            fwd.append(cp)
        for j, (px, py) in enumerate(others):
            chip = 2 * px + py
            copy(uq_part(chip, 1 - c), uq_part(chip, 1 - c), 3 + j, sibling).wait_recv()
        for cp in first + fwd:
            cp.wait_send()
        own.wait()

    return pl.pallas_call(
        body, name="gather_first", in_specs=[VMEM_SPEC, VMEM_SPEC], out_specs=[HBM_SPEC, HBM_SPEC],
        out_shape=[jax.ShapeDtypeStruct(lat.shape, BF16), jax.ShapeDtypeStruct((N_CHIPS,) + uq.shape, BF16)],
        scratch_shapes=[pltpu.VMEM(lat.shape, BF16), pltpu.VMEM(uq.shape, BF16), pltpu.SemaphoreType.DMA((10,)),
                        pltpu.SemaphoreType.DMA((10,)), pltpu.SemaphoreType.DMA((2,))],
        compiler_params=pltpu.CompilerParams(vmem_limit_bytes=VMEM_LIMIT),
    )(lat, uq)


def _cast_own(shards, me):
    n = len(shards)

    def body(me_ref, *refs):
        for w in range(n):
            refs[n + w][...] = refs[w][...].astype(BF16)

    return pl.pallas_call(
        body, name="cast_own",
        grid_spec=pltpu.PrefetchScalarGridSpec(
            num_scalar_prefetch=1, grid=(1,),
            in_specs=[pl.BlockSpec(s.shape, lambda i, me_ref: (0, 0)) for s in shards],
            out_specs=[pl.BlockSpec((None,) + s.shape, lambda i, me_ref: (me_ref[0], 0, 0)) for s in shards]),
        out_shape=[jax.ShapeDtypeStruct((N_CHIPS,) + s.shape, BF16) for s in shards],
        compiler_params=pltpu.CompilerParams(vmem_limit_bytes=VMEM_LIMIT),
    )(me, *shards)


def _gather_start(bufs, after):
    n = len(bufs)

    def body(*refs):
        b_refs = refs[:n]
        send_sems, recv_sems, token = refs[n + 1], refs[n + 2], refs[-1]
        x, y, c, others = _place()
        me = 2 * x + y
        for w in range(n):
            hc = _half(bufs[w])
            mine = b_refs[w].at[me, :, pl.ds(c * hc, hc)]
            for j, (px, py) in enumerate(others):
                pltpu.make_async_remote_copy(
                    src_ref=mine, dst_ref=mine, send_sem=send_sems.at[3 * w + j], recv_sem=recv_sems.at[3 * w + j],
                    device_id=(px, py, c), device_id_type=MESH).start()
        token[...] = jnp.zeros_like(token)

    hbm = [pltpu.HBM(b.shape, BF16) for b in bufs]
    outs = pl.pallas_call(
        body, name="gather_start",
        out_shape=(pltpu.SemaphoreType.DMA((3 * n,)), pltpu.SemaphoreType.DMA((3 * n,)), *hbm,
                   jax.ShapeDtypeStruct(LOSS_TILE, F32)),
        in_specs=[HBM_SPEC_STRICT] * n + [HBM_SPEC],
        out_specs=(SEM_SPEC, SEM_SPEC, *[HBM_SPEC_STRICT] * n, VMEM_SPEC),
        input_output_aliases={i: 2 + i for i in range(n)},
        compiler_params=pltpu.CompilerParams(has_side_effects=SPLIT_EFFECT),
    )(*[pltpu.with_memory_space_constraint(b, pltpu.HBM) for b in bufs], after)
    return outs[0], outs[1], list(outs[2:2 + n]), outs[-1]


def _gather_wait(send_sems, recv_sems, bufs, after):
    n = len(bufs)

    def body(*refs):
        b_refs = refs[:n]
        send_sems, recv_sems = refs[n], refs[n + 1]
        x, y, c, others = _place()
        me = 2 * x + y
        for w in range(n):
            hc = _half(bufs[w])
            for j, (px, py) in enumerate(others):
                cp = pltpu.make_async_remote_copy(
                    src_ref=b_refs[w].at[me, :, pl.ds(c * hc, hc)],
                    dst_ref=b_refs[w].at[2 * px + py, :, pl.ds(c * hc, hc)],
                    send_sem=send_sems.at[3 * w + j], recv_sem=recv_sems.at[3 * w + j], device_id=(px, py, c),
                    device_id_type=MESH)
                cp.wait_send()
                cp.wait_recv()

    outs = pl.pallas_call(
        body, name="gather_wait", out_shape=tuple(pltpu.HBM(b.shape, b.dtype) for b in bufs),
        in_specs=[HBM_SPEC_STRICT] * n + [SEM_SPEC, SEM_SPEC, HBM_SPEC],
        out_specs=tuple([HBM_SPEC_STRICT] * n), input_output_aliases={i: i for i in range(n)},
        compiler_params=pltpu.CompilerParams(has_side_effects=SPLIT_EFFECT),
    )(*bufs, send_sems, recv_sems, after)
    return list(outs)


def _gather_finish(bufs):
    n = len(bufs)

    def body(*refs):
        b_refs = refs[n:2 * n]
        send_sems, recv_sems = refs[2 * n:]
        x, y, c, others = _place()
        cps = []
        for w in range(n):
            hc = _half(bufs[w])
            for j, (px, py) in enumerate(others):
                part = b_refs[w].at[2 * px + py, :, pl.ds(c * hc, hc)]
                cps.append(pltpu.make_async_remote_copy(
                    src_ref=part, dst_ref=part, send_sem=send_sems.at[3 * w + j], recv_sem=recv_sems.at[3 * w + j],
                    device_id=(x, y, 1 - c), device_id_type=MESH))
        for cp in cps:
            cp.start()
        for w in range(n):
            hc = _half(bufs[w])
            for j, (px, py) in enumerate(others):
                theirs = b_refs[w].at[2 * px + py, :, pl.ds((1 - c) * hc, hc)]
                pltpu.make_async_remote_copy(
                    src_ref=theirs, dst_ref=theirs, send_sem=send_sems.at[3 * w + j], recv_sem=recv_sems.at[3 * w + j],
                    device_id=(x, y, 1 - c), device_id_type=MESH).wait_recv()
        for cp in cps:
            cp.wait_send()

    return pl.pallas_call(
        body, name="gather_finish", in_specs=[HBM_SPEC] * n, out_specs=[HBM_SPEC] * n,
        out_shape=[jax.ShapeDtypeStruct(b.shape, b.dtype) for b in bufs],
        input_output_aliases={i: i for i in range(n)},
        scratch_shapes=[pltpu.SemaphoreType.DMA((3 * n,)), pltpu.SemaphoreType.DMA((3 * n,))],
    )(*bufs)


def _half(a):
    return a.shape[-1] // 2


def _exchange_pairs(parts, name):
    n = len(parts)

    def body(*refs):
        p_refs, r_refs = refs[:n], refs[n:2 * n]
        send_sems, recv_sems = refs[2 * n:]
        x, y, c, _ = _place()
        cps = []
        for w in range(n):
            h = _half(parts[w])
            cps.append(pltpu.make_async_remote_copy(
                src_ref=p_refs[w].at[:, :, pl.ds((1 - c) * h, h)], dst_ref=r_refs[w],
                send_sem=send_sems.at[w], recv_sem=recv_sems.at[w], device_id=(x, y, 1 - c), device_id_type=MESH))
        for cp in cps:
            cp.start()
        for cp in cps:
            cp.wait()

    return pl.pallas_call(
        body, name=name, in_specs=[HBM_SPEC] * n, out_specs=[HBM_SPEC] * n,
        out_shape=[jax.ShapeDtypeStruct((N_CHIPS, p.shape[1], _half(p)), BF16) for p in parts],
        scratch_shapes=[pltpu.SemaphoreType.DMA((n,)), pltpu.SemaphoreType.DMA((n,))],
    )(*parts)


def _pairs_start(parts, all_loss):
    n = len(parts)

    def body(*refs):
        p_refs, r_refs, loss_ref = refs[:n], refs[n:2 * n], refs[2 * n]
        send_sems, recv_sems, token = refs[2 * n + 1], refs[2 * n + 2], refs[-1]
        x, y, c, _ = _place()
        for w in range(n):
            h = _half(parts[w])
            pltpu.make_async_remote_copy(
                src_ref=p_refs[w].at[:, :, pl.ds((1 - c) * h, h)], dst_ref=r_refs[w], send_sem=send_sems.at[w],
                recv_sem=recv_sems.at[w], device_id=(x, y, 1 - c), device_id_type=MESH).start()
        me = 4 * x + 2 * y + c
        for t in range(1, N_DEV):
            d = (me + t) % N_DEV
            pltpu.make_async_remote_copy(
                src_ref=loss_ref.at[me], dst_ref=loss_ref.at[me], send_sem=send_sems.at[n + t - 1],
                recv_sem=recv_sems.at[n + t - 1], device_id=(d // 4, (d // 2) % 2, d % 2), device_id_type=MESH).start()
        token[...] = jnp.zeros_like(token)

    lands = [pltpu.HBM((N_CHIPS, p.shape[1], _half(p)), BF16) for p in parts]
    nsem = n + N_DEV - 1
    outs = pl.pallas_call(
        body, name="pairs_start",
        out_shape=(pltpu.SemaphoreType.DMA((nsem,)), pltpu.SemaphoreType.DMA((nsem,)),
                   *[pltpu.HBM(p.shape, p.dtype) for p in parts], *lands, pltpu.HBM(all_loss.shape, F32),
                   jax.ShapeDtypeStruct(LOSS_TILE, F32)),
        in_specs=[HBM_SPEC_STRICT] * (2 * n + 1),
        out_specs=(SEM_SPEC, SEM_SPEC, *[HBM_SPEC_STRICT] * (2 * n + 1), VMEM_SPEC),
        input_output_aliases={i: 2 + i for i in range(2 * n + 1)},
        compiler_params=pltpu.CompilerParams(has_side_effects=SPLIT_EFFECT),
    )(*[pltpu.with_memory_space_constraint(p, pltpu.HBM) for p in parts],
      *[pltpu.with_memory_space_constraint(lax.empty(l.shape, BF16), pltpu.HBM) for l in lands],
      pltpu.with_memory_space_constraint(all_loss, pltpu.HBM))
    return outs[0], outs[1], list(outs[2:2 + n]), list(outs[2 + n:2 + 2 * n]), outs[2 + 2 * n], outs[-1]


def _pairs_wait(send_sems, recv_sems, parts, lands, all_loss, after):
    n = len(parts)

    def body(*refs):
        p_refs, r_refs, loss_ref = refs[:n], refs[n:2 * n], refs[2 * n]
        send_sems, recv_sems = refs[2 * n + 1], refs[2 * n + 2]
        x, y, c, _ = _place()
        for w in range(n):
            h = _half(parts[w])
            cp = pltpu.make_async_remote_copy(
                src_ref=p_refs[w].at[:, :, pl.ds((1 - c) * h, h)], dst_ref=r_refs[w], send_sem=send_sems.at[w],
                recv_sem=recv_sems.at[w], device_id=(x, y, 1 - c), device_id_type=MESH)
            cp.wait_send()
            cp.wait_recv()
        me = 4 * x + 2 * y + c
        for t in range(1, N_DEV):
            d = (me + N_DEV - t) % N_DEV
            cp = pltpu.make_async_remote_copy(
                src_ref=loss_ref.at[me], dst_ref=loss_ref.at[d], send_sem=send_sems.at[n + t - 1],
                recv_sem=recv_sems.at[n + t - 1], device_id=(d // 4, (d // 2) % 2, d % 2), device_id_type=MESH)
            cp.wait_send()
            cp.wait_recv()

    bufs = (*parts, *lands, all_loss)
    outs = pl.pallas_call(
        body, name="pairs_wait", out_shape=tuple(pltpu.HBM(a.shape, a.dtype) for a in bufs),
        in_specs=[HBM_SPEC_STRICT] * len(bufs) + [SEM_SPEC, SEM_SPEC, HBM_SPEC],
        out_specs=tuple([HBM_SPEC_STRICT] * len(bufs)), input_output_aliases={i: i for i in range(len(bufs))},
        compiler_params=pltpu.CompilerParams(has_side_effects=SPLIT_EFFECT),
    )(*bufs, send_sems, recv_sems, after)
    return list(outs[:n]), list(outs[n:2 * n]), outs[2 * n]


def _add_pair_tiled(p, r, c):
    rows, h = r.shape[1:]

    def body(c_ref, p_ref, r_ref, q_ref):
        q_ref[...] = (p_ref[...].astype(F32) + r_ref[...].astype(F32)).astype(BF16)

    return pl.pallas_call(
        body, name="add_pair_w_in",
        grid_spec=pltpu.PrefetchScalarGridSpec(
            num_scalar_prefetch=1, grid=(N_CHIPS,),
            in_specs=[pl.BlockSpec((None, rows, h), lambda k, c_ref: (k, 0, c_ref[0])),
                      pl.BlockSpec((None, rows, h), lambda k, c_ref: (k, 0, 0))],
            out_specs=pl.BlockSpec((None, rows, h), lambda k, c_ref: (k, 0, 0))),
        out_shape=jax.ShapeDtypeStruct(r.shape, BF16),
    )(c, p, r)


def _add_pair_small(ps, rs, c, name):
    n = len(ps)

    def body(c_ref, *refs):
        for w in range(n):
            h = _half(ps[w])
            mine = refs[w][:, :, pl.ds(pl.multiple_of(c_ref[0] * h, 128), h)]
            refs[2 * n + w][...] = (mine.astype(F32) + refs[n + w][...].astype(F32)).astype(BF16)

    return pl.pallas_call(
        body, name=name,
        in_specs=[pl.BlockSpec(memory_space=pltpu.SMEM)] + [VMEM_SPEC] * (2 * n), out_specs=[VMEM_SPEC] * n,
        out_shape=[jax.ShapeDtypeStruct(r.shape, BF16) for r in rs],
        compiler_params=pltpu.CompilerParams(vmem_limit_bytes=VMEM_LIMIT),
    )(c, *ps, *rs)


def _exchange_chips(qs):
    n = len(qs)

    def body(*refs):
        q_refs, r_refs = refs[:n], refs[n:2 * n]
        send_sems, recv_sems = refs[2 * n:]
        x, y, c, others = _place()
        me = 2 * x + y
        cps = []
        for w in range(n):
            for j, (px, py) in enumerate(others):
                cps.append(pltpu.make_async_remote_copy(
                    src_ref=q_refs[w].at[2 * px + py], dst_ref=r_refs[w].at[me], send_sem=send_sems.at[3 * w + j],
                    recv_sem=recv_sems.at[3 * w + j], device_id=(px, py, c), device_id_type=MESH))
        for cp in cps:
            cp.start()
        for w in range(n):
            for j, (px, py) in enumerate(others):
                pltpu.make_async_remote_copy(
                    src_ref=q_refs[w].at[me], dst_ref=r_refs[w].at[2 * px + py], send_sem=send_sems.at[3 * w + j],
                    recv_sem=recv_sems.at[3 * w + j], device_id=(px, py, c), device_id_type=MESH).wait_recv()
        for cp in cps:
            cp.wait_send()

    return pl.pallas_call(
        body, name="exchange_chips", in_specs=[HBM_SPEC] * n, out_specs=[HBM_SPEC] * n,
        out_shape=[jax.ShapeDtypeStruct(q.shape, BF16) for q in qs],
        scratch_shapes=[pltpu.SemaphoreType.DMA((3 * n,)), pltpu.SemaphoreType.DMA((3 * n,))],
    )(*qs)


SEM_SPEC = pl.BlockSpec(memory_space=pltpu.SEMAPHORE)
SPLIT_EFFECT = pltpu.SideEffectType.DATAFLOW_SIDE_EFFECTING


def _chips_start(qs, name):
    n = len(qs)

    def body(*refs):
        q_refs, land_refs = refs[:n], refs[n:2 * n]
        send_sems, recv_sems, token = refs[2 * n], refs[2 * n + 1], refs[-1]
        x, y, c, others = _place()
        me = 2 * x + y
        for w in range(n):
            for j, (px, py) in enumerate(others):
                pltpu.make_async_remote_copy(
                    src_ref=q_refs[w].at[2 * px + py], dst_ref=land_refs[w].at[me], send_sem=send_sems.at[3 * w + j],
                    recv_sem=recv_sems.at[3 * w + j], device_id=(px, py, c), device_id_type=MESH).start()
        token[...] = jnp.zeros_like(token)

    hbm = [pltpu.HBM(q.shape, BF16) for q in qs]
    outs = pl.pallas_call(
        body, name=name,
        out_shape=(pltpu.SemaphoreType.DMA((3 * n,)), pltpu.SemaphoreType.DMA((3 * n,)), *hbm, *hbm,
                   jax.ShapeDtypeStruct(LOSS_TILE, F32)),
        in_specs=[HBM_SPEC_STRICT] * (2 * n),
        out_specs=(SEM_SPEC, SEM_SPEC, *[HBM_SPEC_STRICT] * (2 * n), VMEM_SPEC),
        input_output_aliases={i: 2 + i for i in range(2 * n)},
        compiler_params=pltpu.CompilerParams(has_side_effects=SPLIT_EFFECT),
    )(*[pltpu.with_memory_space_constraint(q, pltpu.HBM) for q in qs],
      *[pltpu.with_memory_space_constraint(lax.empty(q.shape, BF16), pltpu.HBM) for q in qs])
    return outs[0], outs[1], outs[2:2 + n], outs[2 + n:2 + 2 * n], outs[-1]


def _chips_wait(send_sems, recv_sems, q_thru, land_thru, after, name):
    n = len(q_thru)

    def body(*refs):
        q_refs, land_refs = refs[:n], refs[n:2 * n]
        send_sems, recv_sems = refs[2 * n], refs[2 * n + 1]
        x, y, c, others = _place()
        me = 2 * x + y
        for w in range(n):
            for j, (px, py) in enumerate(others):
                cp = pltpu.make_async_remote_copy(
                    src_ref=q_refs[w].at[2 * px + py], dst_ref=land_refs[w].at[2 * px + py],
                    send_sem=send_sems.at[3 * w + j], recv_sem=recv_sems.at[3 * w + j], device_id=(px, py, c),
                    device_id_type=MESH)
                cp.wait_send()
                cp.wait_recv()

    outs = pl.pallas_call(
        body, name=name, out_shape=tuple(pltpu.HBM(a.shape, a.dtype) for a in (*q_thru, *land_thru)),
        in_specs=[HBM_SPEC_STRICT] * (2 * n) + [SEM_SPEC, SEM_SPEC, HBM_SPEC],
        out_specs=tuple([HBM_SPEC_STRICT] * (2 * n)), input_output_aliases={i: i for i in range(2 * n)},
        compiler_params=pltpu.CompilerParams(has_side_effects=SPLIT_EFFECT),
    )(*q_thru, *land_thru, send_sems, recv_sems, after)
    return list(outs[:n]), list(outs[n:])


def _sum_chips_tiled(q, r, idx, tile):
    rows, h = r.shape[1:]
    nt = h // tile

    def body(idx_ref, q_ref, r0_ref, r1_ref, r2_ref, g_ref):
        g_ref[...] = (q_ref[...].astype(F32) + r0_ref[...].astype(F32) + r1_ref[...].astype(F32)
                      + r2_ref[...].astype(F32))

    def slab(t):
        return pl.BlockSpec((None, rows, tile), lambda i, idx_ref: (idx_ref[t], 0, i))

    return pl.pallas_call(
        body, name="sum_chips_w_in",
        grid_spec=pltpu.PrefetchScalarGridSpec(
            num_scalar_prefetch=1, grid=(nt,), in_specs=[slab(0), slab(1), slab(2), slab(3)],
            out_specs=pl.BlockSpec((rows, tile), lambda i, idx_ref: (0, idx_ref[4] * nt + i))),
        out_shape=jax.ShapeDtypeStruct((rows, 2 * h), F32),
    )(idx, q, r, r, r)


def _sum_chips_small(qs, rs, idx, all_dtypes):
    n = len(rs)
    n_all = len(all_dtypes)

    def body(idx_ref, *refs):
        c = idx_ref[4]
        for w in range(n):
            q_ref, r_ref, g_ref = refs[w], refs[n + w], refs[2 * n + w]
            acc = q_ref[idx_ref[0]].astype(F32)
            for t in range(1, N_CHIPS):
                acc = acc + r_ref[idx_ref[t]].astype(F32)
            h = rs[w].shape[2]
            mine = pl.ds(pl.multiple_of(c * h, 128), h)
            g_ref[...] = jnp.zeros_like(g_ref)
            if w >= n - n_all:
                g_ref[idx_ref[0], :, mine] = acc.astype(g_ref.dtype)
            else:
                g_ref[:, mine] = acc

    shapes = [jax.ShapeDtypeStruct((r.shape[1], 2 * r.shape[2]), F32) for r in rs[:n - n_all]]
    shapes += [jax.ShapeDtypeStruct((N_CHIPS, r.shape[1], 2 * r.shape[2]), dt)
               for r, dt in zip(rs[n - n_all:], all_dtypes)]
    return pl.pallas_call(
        body, name="sum_chips_small",
        in_specs=[pl.BlockSpec(memory_space=pltpu.SMEM)] + [VMEM_SPEC] * (2 * n), out_specs=[VMEM_SPEC] * n,
        out_shape=shapes, compiler_params=pltpu.CompilerParams(vmem_limit_bytes=VMEM_LIMIT),
    )(idx, *qs, *rs)


def _share(shards, alls):
    n, na = len(shards), len(alls)
    total = n + na

    def body(*refs):
        g_refs, a_refs = refs[total:total + n], refs[total + n:2 * total]
        send_sems, recv_sems = refs[2 * total:]
        x, y, c, others = _place()
        me = 2 * x + y
        sibling = (x, y, 1 - c)

        def cols_of(w, half):
            h = shards[w].shape[1] // 2
            return g_refs[w].at[:, pl.ds(half * h, h)]

        def slab(a, chip, half):
            h = alls[a].shape[2] // 2
            return a_refs[a].at[chip, :, pl.ds(half * h, h)]

        def copy(src, dst, k, to):
            return pltpu.make_async_remote_copy(src_ref=src, dst_ref=dst, send_sem=send_sems.at[k],
                                                recv_sem=recv_sems.at[k], device_id=to, device_id_type=MESH)

        cps = [copy(cols_of(w, c), cols_of(w, c), w, sibling) for w in range(n)]
        for a in range(na):
            base = n + 7 * a
            cps.append(copy(slab(a, me, c), slab(a, me, c), base, sibling))
            for j, (px, py) in enumerate(others):
                cps.append(copy(slab(a, me, c), slab(a, me, c), base + 1 + j, (px, py, c)))
        for cp in cps:
            cp.start()
        fwd = []
        for a in range(na):
            base = n + 7 * a
            for j, (px, py) in enumerate(others):
                chip = 2 * px + py
                copy(slab(a, me, c), slab(a, chip, c), base + 1 + j, (px, py, c)).wait_recv()
                cp = copy(slab(a, chip, c), slab(a, chip, c), base + 4 + j, sibling)
                cp.start()
                fwd.append(cp)
        for a in range(na):
            base = n + 7 * a
            for j, (px, py) in enumerate(others):
                chip = 2 * px + py
                copy(slab(a, chip, c), slab(a, chip, 1 - c), base + 4 + j, sibling).wait_recv()
            copy(slab(a, me, c), slab(a, me, 1 - c), base, sibling).wait_recv()
        for w in range(n):
            copy(cols_of(w, c), cols_of(w, 1 - c), w, sibling).wait_recv()
        for cp in cps + fwd:
            cp.wait_send()

    nsem = n + 7 * na
    return pl.pallas_call(
        body, name="share", in_specs=[HBM_SPEC] * total, out_specs=[HBM_SPEC] * total,
        out_shape=[jax.ShapeDtypeStruct(a.shape, a.dtype) for a in (*shards, *alls)],
        input_output_aliases={i: i for i in range(total)},
        scratch_shapes=[pltpu.SemaphoreType.DMA((nsem,)), pltpu.SemaphoreType.DMA((nsem,))],
    )(*shards, *alls)


def _adamw(w, g, m, v):
    m2 = ADAM_B1 * m + (1.0 - ADAM_B1) * g
    v2 = ADAM_B2 * v + (1.0 - ADAM_B2) * (g * g)
    m_hat = m2 / (1.0 - ADAM_B1 ** ADAM_STEP)
    v_hat = v2 / (1.0 - ADAM_B2 ** ADAM_STEP)
    return -ADAM_LR * (m_hat / (jnp.sqrt(v_hat) + ADAM_EPS) + ADAM_WD * w), m2, v2


def _update_w_in(wt, gt, mt, vt, lat, owner, tile):
    nlat = lat.shape[0] // tile

    def body(owner_ref, w_ref, g_ref, m_ref, v_ref, lat_ref, g2_ref, d_ref, m2_ref, v2_ref):
        row = pl.program_id(0) * tile + lax.broadcasted_iota(jnp.int32, (tile, 1), 0)
        g = jnp.where((row < LAT_COLS) & (owner_ref[0] == 1), lat_ref[...].astype(F32), g_ref[...])
        g2_ref[...] = g
        d_ref[...], m2_ref[...], v2_ref[...] = _adamw(w_ref[...], g, m_ref[...], v_ref[...])

    spec = pl.BlockSpec((tile, wt.shape[1]), lambda i, o: (i, 0))
    return pl.pallas_call(
        body, name="update_w_in",
        grid_spec=pltpu.PrefetchScalarGridSpec(
            num_scalar_prefetch=1, grid=(wt.shape[0] // tile,),
            in_specs=[spec] * 4 + [pl.BlockSpec((tile, wt.shape[1]), lambda i, o: (jnp.minimum(i, nlat - 1), 0))],
            out_specs=[spec] * 4),
        out_shape=[jax.ShapeDtypeStruct(wt.shape, F32)] * 4,
        compiler_params=_params(("parallel",)),
    )(owner, wt, gt, mt, vt, lat)


def _update_small(ws, gs, ms, vs):
    n = len(ws)

    def body(*refs):
        for k in range(n):
            w_ref, g_ref, m_ref, v_ref = refs[k], refs[n + k], refs[2 * n + k], refs[3 * n + k]
            d, m2, v2 = _adamw(w_ref[...], g_ref[...], m_ref[...], v_ref[...])
            refs[4 * n + k][...] = d
            refs[5 * n + k][...] = m2
            refs[6 * n + k][...] = v2

    shapes = [jax.ShapeDtypeStruct(w.shape, F32) for w in ws]
    outs = pl.pallas_call(
        body, name="update_small", in_specs=[VMEM_SPEC] * (4 * n), out_specs=[VMEM_SPEC] * (3 * n),
        out_shape=shapes * 3,
        compiler_params=pltpu.CompilerParams(vmem_limit_bytes=VMEM_LIMIT),
    )(*ws, *gs, *ms, *vs)
    return outs[:n], outs[n:2 * n], outs[2 * n:]


SHARDED = ("w_in", "w_uq", "w_oa", "w_ob", "w_out")
REPLICATED = ("b_in", "g_q", "g_kv", "w_ukv", "sgu_ln_g", "sgu_ln_b", "w_s", "b_s", "ln_g", "ln_b")
ORDER = ("w_in", "b_in", "g_q", "w_uq", "g_kv", "w_ukv", "w_oa", "sgu_ln_g", "sgu_ln_b", "w_s", "b_s", "w_ob", "w_out",
         "ln_g", "ln_b")


def kernel(x, positions, w_in, b_in, g_q, w_uq, g_kv, w_ukv, w_oa, sgu_ln_g, sgu_ln_b, w_s, b_s, w_ob, w_out, ln_g, ln_b, loss_target, m_w_in, m_b_in, m_g_q, m_w_uq, m_g_kv, m_w_ukv, m_w_oa, m_sgu_ln_g, m_sgu_ln_b, m_w_s, m_b_s, m_w_ob, m_w_out, m_ln_g, m_ln_b, v_w_in, v_b_in, v_g_q, v_w_uq, v_g_kv, v_w_ukv, v_w_oa, v_sgu_ln_g, v_sgu_ln_b, v_w_s, v_b_s, v_w_ob, v_w_out, v_ln_g, v_ln_b):
    w = dict(w_in=w_in, b_in=b_in, g_q=g_q, w_uq=w_uq, g_kv=g_kv, w_ukv=w_ukv, w_oa=w_oa, sgu_ln_g=sgu_ln_g,
             sgu_ln_b=sgu_ln_b, w_s=w_s, b_s=b_s, w_ob=w_ob, w_out=w_out, ln_g=ln_g, ln_b=ln_b)
    m = dict(w_in=m_w_in, b_in=m_b_in, g_q=m_g_q, w_uq=m_w_uq, g_kv=m_g_kv, w_ukv=m_w_ukv, w_oa=m_w_oa,
             sgu_ln_g=m_sgu_ln_g, sgu_ln_b=m_sgu_ln_b, w_s=m_w_s, b_s=m_b_s, w_ob=m_w_ob, w_out=m_w_out, ln_g=m_ln_g,
             ln_b=m_ln_b)
    v = dict(w_in=v_w_in, b_in=v_b_in, g_q=v_g_q, w_uq=v_w_uq, g_kv=v_g_kv, w_ukv=v_w_ukv, w_oa=v_w_oa,
             sgu_ln_g=v_sgu_ln_g, sgu_ln_b=v_sgu_ln_b, w_s=v_w_s, b_s=v_b_s, w_ob=v_w_ob, w_out=v_w_out, ln_g=v_ln_g,
             ln_b=v_ln_b)
    w, m, v = ({n: a[0] for n, a in d.items()} for d in (w, m, v))
    c = lax.axis_index("c")

    wt_shard, mt_shard, vt_shard = (jnp.transpose(d["w_in"]) for d in (w, m, v))
    xi, yi = lax.axis_index("x"), lax.axis_index("y")
    me1 = (2 * xi + yi).reshape(1).astype(jnp.int32)
    g_lat, g_uq = _gather_first(wt_shard[:LAT_COLS], w["w_uq"].reshape(Q_RANK // 4, HEADS * QK_DIM))
    bufs = _cast_own([wt_shard, w["w_oa"], w["w_ob"], w["w_out"]], me1)
    send0, recv0, bufs, token0 = _gather_start(bufs, g_lat)
    st = _local_attention(x[0], positions[0], g_lat, w["b_in"], w["g_q"], g_uq.reshape(Q_RANK, HEADS, QK_DIM),
                          w["g_kv"], w["w_ukv"], token0)
    g_in, g_oa, g_ob, g_out = _gather_finish(_gather_wait(send0, recv0, bufs, st["o"]))
    wt = g_in.reshape(IN_W, D_MODEL)

    loss, early, st = _local_head(
        st, x[0], loss_target[0], wt, g_oa, w["sgu_ln_g"], w["sgu_ln_b"], w["w_s"], w["b_s"], g_ob,
        g_out.reshape(D_MODEL, D_MODEL), w["ln_g"], w["ln_b"])

    c1 = c.reshape(1).astype(jnp.int32)
    idx = jnp.stack([2 * xi + yi, 2 * (1 - xi) + yi, 2 * xi + (1 - yi), 2 * (1 - xi) + (1 - yi), c]).astype(jnp.int32)
    parts1 = [early["w_in"].reshape(N_CHIPS, IN_W // N_CHIPS, D_MODEL), early["w_oa"].astype(BF16),
              early["w_ob"].astype(BF16), early["w_out"].reshape(N_CHIPS, SLAB_W, D_MODEL).astype(BF16)]
    my_loss = lax.dynamic_update_slice(jnp.zeros((N_DEV,) + LOSS_TILE, F32), jnp.broadcast_to(loss, (1,) + LOSS_TILE),
                                       (4 * xi + 2 * yi + c, 0, 0))
    sems0 = _pairs_start(parts1, my_loss)
    st["delta"] = st["delta"] + sems0[5][0, 0]
    dq, dk, dv = _local_attn_bwd(st)
    parts1, recv1, all_loss = _pairs_wait(*sems0[:5], dk)
    pairs1 = [_add_pair_tiled(parts1[0], recv1[0], c1),
              *_add_pair_small(parts1[1:], recv1[1:], c1, "add_pair_early")]
    sems1 = _chips_start(pairs1, "chips_start_early")
    dhl, late = _local_tail(st, dq, dk, dv, sems1[4])

    grads = {**early, **late}
    rep = jnp.concatenate([_rows8(grads[n]) for n in REPLICATED], axis=0)
    rep = jnp.pad(rep, ((0, N_CHIPS * REP_ROWS - rep.shape[0]), (0, 0))).reshape(N_CHIPS, REP_ROWS, D_MODEL)
    parts2 = [late["w_uq"].reshape(N_CHIPS, Q_RANK // N_CHIPS, HEADS * QK_DIM).astype(BF16), rep.astype(BF16),
              late["w_lat"].reshape(N_CHIPS, LAT_ROWS_PAD // N_CHIPS, D_MODEL)]
    pairs2 = _add_pair_small(parts2, _exchange_pairs(parts2, "exchange_pairs_late"), c1, "add_pair_late")
    sems2 = _chips_start(pairs2, "chips_start_late")
    dx = _dx(st["dr"], st["dhg"], st["dhm"], dhl, st["wt"], sems2[4])
    pairs2, landed2 = _chips_wait(*sems2[:4], dx, "chips_wait_late")
    pairs1, landed1 = _chips_wait(*sems1[:4], landed2[0], "chips_wait_early")
    sums = [_sum_chips_tiled(pairs1[0], landed1[0], idx, 128),
            *_sum_chips_small([*pairs1[1:], *pairs2], [*landed1[1:], *landed2], idx, (F32, BF16))]
    *shards, g_rep, g_lat = _share(sums[:-2], sums[-2:])
    loss = jnp.sum(all_loss[:, 0, 0])

    red = {n: s.reshape(w[n].shape) for n, s in zip(("w_oa", "w_ob", "w_out", "w_uq"), shards[1:])}
    g_rep = g_rep.reshape(N_CHIPS * REP_ROWS, D_MODEL)
    off = 0
    for n in REPLICATED:
        rows = _rows8(w[n]).shape[0]
        red[n] = g_rep[off:off + rows].reshape(-1)[:w[n].size].reshape(w[n].shape)
        off += rows
    owner = (2 * xi + yi == 0).astype(jnp.int32).reshape(1)
    gt, dt, mt, vt2 = _update_w_in(wt_shard, shards[0], mt_shard, vt_shard,
                                   g_lat.reshape(LAT_ROWS_PAD, D_MODEL).astype(F32), owner, 232)
    red["w_in"] = jnp.transpose(gt)
    small = [n for n in ORDER if n != "w_in"]
    as2d = lambda a: a.reshape(-1, a.shape[-1])
    ds, ms, vs = _update_small([as2d(w[n]) for n in small], [as2d(red[n]) for n in small],
                               [as2d(m[n]) for n in small], [as2d(v[n]) for n in small])
    delta, new_m, new_v = {"w_in": jnp.transpose(dt)}, {"w_in": jnp.transpose(mt)}, {"w_in": jnp.transpose(vt2)}
    for i, n in enumerate(small):
        delta[n], new_m[n], new_v[n] = (a[i].reshape(w[n].shape) for a in (ds, ms, vs))

    lead = lambda a: a[None]
    return (loss, dx[None], *[lead(red[n]) for n in ORDER], *[lead(delta[n]) for n in ORDER],
            *[lead(new_m[n]) for n in ORDER], *[lead(new_v[n]) for n in ORDER])
```

```python
import functools
import math

import jax
import jax.numpy as jnp
from jax import lax
from jax.experimental import pallas as pl
from jax.experimental.pallas import tpu as pltpu

F32 = jnp.float32
BF16 = jnp.bfloat16

D_MODEL = 1024
HEADS = 8
Q_RANK = 384
KV_RANK = 128
NOPE = 64
ROPE = 32
V_DIM = 64
QK_DIM = NOPE + ROPE
HEAD_PAD = 128
MLA_W = HEADS * V_DIM
SGU_W = 512
GROUPS = 8
CHUNK = 128
IN_W = 4640
RMS_EPS = 1e-6
LN_EPS = 1e-5
ALPHA = 2.0 ** 0.25
ROPE_THETA = 10000.0
SCALE = QK_DIM ** -0.5

GATE_W = 2 * D_MODEL
MID_W = 4 * SGU_W
LAT_W = Q_RANK + KV_RANK + HEAD_PAD
PAD_W = GATE_W + MID_W + LAT_W
LAT_COLS = Q_RANK + KV_RANK + ROPE
ROW_GATE = LAT_COLS + MID_W
LAT_ROWS_PAD = 704
N_SLABS = 4
SLAB_W = D_MODEL // N_SLABS

ROW_TILE = 256
MATMUL_ROW_TILE = 512
ATT_TQ = 256
ATT_TK = 256
ATT_BWD_TQ = 256
ATT_BWD_TK = 256
LOG2E = 1.4426950408889634
LN2 = 0.6931471805599453
Q_SCALE = SCALE * LOG2E
VMEM_LIMIT = 56 * 1024 * 1024

ADAM_LR = 0.001
ADAM_B1 = 0.9
ADAM_B2 = 0.999
ADAM_EPS = 1e-08
ADAM_WD = 0.01
ADAM_STEP = 10


def _dot(a, b):
    return jnp.dot(a, b, preferred_element_type=F32)


def _dot_nt(a, b):
    return lax.dot_general(a, b, (((1,), (1,)), ((), ())), preferred_element_type=F32)


def _dot_tn(a, b):
    return lax.dot_general(a, b, (((0,), (0,)), ((), ())), preferred_element_type=F32)


def _sigmoid(z):
    return 0.5 * jnp.tanh(0.5 * z) + 0.5


_GELU_C = math.sqrt(2.0 / math.pi)


def _gelu_and_grad(x):
    x2 = x * x
    t = jnp.tanh(_GELU_C * (x + 0.044715 * x * x2))
    g = 0.5 * x * (1.0 + t)
    dg = 0.5 * (1.0 + t) + 0.5 * x * (1.0 - t * t) * (_GELU_C * (1.0 + 3.0 * 0.044715 * x2))
    return g, dg


def _silu_and_grad(z):
    s = _sigmoid(z)
    return z * s, s * (1.0 + z * (1.0 - s))


def _rope(xb, c, sl, sh):
    return xb * c + pltpu.roll(xb, 112, 1) * sl + pltpu.roll(xb, 16, 1) * sh


def _rope_t(dy, c, sl, sh):
    return dy * c + pltpu.roll(dy * sl, 16, 1) + pltpu.roll(dy * sh, 112, 1)


def _params(sem=("arbitrary",)):
    return pltpu.CompilerParams(dimension_semantics=sem, vmem_limit_bytes=VMEM_LIMIT)


def _row_spec(tile, width):
    return pl.BlockSpec((tile, width), lambda i: (i, 0))


def _full_spec(shape):
    nd = len(shape)
    return pl.BlockSpec(shape, lambda i: (0,) * nd)


def _kpe_rows(wt_ref):
    z = lambda n: jnp.zeros((n, D_MODEL), BF16)
    return jnp.concatenate([z(NOPE), wt_ref[Q_RANK + KV_RANK:LAT_COLS, :], z(HEAD_PAD - QK_DIM)], axis=0)


def _fwd_rest(xb, wt, b_g, b_m):
    s = xb.shape[0]
    ts = MATMUL_ROW_TILE

    def body(xb_ref, wt_ref, bg_ref, bm_ref, hg_ref, hm_ref):
        xb_ = xb_ref[...]
        hg_ref[...] = _dot_nt(xb_, wt_ref[ROW_GATE:IN_W, :]) + bg_ref[...]
        hm_ref[...] = _dot_nt(xb_, wt_ref[LAT_COLS:ROW_GATE, :]) + bm_ref[...]

    return pl.pallas_call(
        body, name="fwd_rest", grid=(s // ts,),
        in_specs=[_row_spec(ts, D_MODEL), _full_spec(wt.shape), _full_spec(b_g.shape), _full_spec(b_m.shape)],
        out_specs=[_row_spec(ts, GATE_W), _row_spec(ts, MID_W)],
        out_shape=[jax.ShapeDtypeStruct((s, GATE_W), F32), jax.ShapeDtypeStruct((s, MID_W), F32)],
        compiler_params=_params(),
    )(xb, wt, b_g, b_m)


def _fwd_lat(x, wlat, b_l, g_q, wuq, g_kv, wk, wv, rc, rsl, rsh, after):
    s = x.shape[0]
    ts = ROW_TILE

    def body(x_ref, wt_ref, bl_ref, gq_ref, wuq_ref, gkv_ref, wk_ref, wv_ref, rc_ref, rsl_ref,
             rsh_ref, after_ref, hl_ref, q_ref, k_ref, v_ref, xb_ref, qt_ref, kt_ref, vt_ref):
        xb = x_ref[...].astype(BF16)
        xb_ref[...] = xb
        hl = jnp.concatenate([_dot_nt(xb, wt_ref[0:Q_RANK + KV_RANK, :]), _dot_nt(xb, _kpe_rows(wt_ref))],
                             axis=1) + bl_ref[...]
        hl_ref[...] = hl
        c, sl, sh = rc_ref[...], rsl_ref[...], rsh_ref[...]
        cq = hl[:, :Q_RANK]
        cqn = cq * lax.rsqrt(jnp.mean(cq * cq, axis=-1, keepdims=True) + RMS_EPS) * gq_ref[...]
        q = _dot(cqn.astype(BF16), wuq_ref[...])
        ckv = hl[:, Q_RANK:Q_RANK + KV_RANK]
        ckvn = (ckv * lax.rsqrt(jnp.mean(ckv * ckv, axis=-1, keepdims=True) + RMS_EPS) * gkv_ref[...]).astype(BF16)
        k = _dot(ckvn, wk_ref[...])
        vb = _dot(ckvn, wv_ref[...]).astype(BF16)
        v_ref[...] = vb
        vt_ref[...] = vb.T
        kpe = _rope(hl[:, Q_RANK + KV_RANK:], c, sl, sh)
        for hd in range(HEADS):
            lanes = slice(hd * HEAD_PAD, (hd + 1) * HEAD_PAD)
            qb = (_rope(q[:, lanes], c, sl, sh) * Q_SCALE).astype(BF16)
            kb = (k[:, lanes] + kpe).astype(BF16)
            q_ref[:, lanes] = qb
            k_ref[:, lanes] = kb
            qt_ref[lanes, :] = qb.T
            kt_ref[lanes, :] = kb.T

    qk_w = HEADS * HEAD_PAD
    col_spec = lambda rows: pl.BlockSpec((rows, ts), lambda i: (0, i))
    return pl.pallas_call(
        body, name="fwd_lat", grid=(s // ts,),
        in_specs=[_row_spec(ts, D_MODEL), _full_spec(wlat.shape),
                  _full_spec(b_l.shape), _full_spec(g_q.shape),
                  _full_spec(wuq.shape), _full_spec(g_kv.shape), _full_spec(wk.shape), _full_spec(wv.shape),
                  _row_spec(ts, HEAD_PAD), _row_spec(ts, HEAD_PAD), _row_spec(ts, HEAD_PAD),
                  pl.BlockSpec(memory_space=pl.ANY)],
        out_specs=[_row_spec(ts, LAT_W), _row_spec(ts, qk_w),
                   _row_spec(ts, qk_w), _row_spec(ts, MLA_W), _row_spec(ts, D_MODEL), col_spec(qk_w), col_spec(qk_w),
                   col_spec(MLA_W)],
        out_shape=[jax.ShapeDtypeStruct((s, LAT_W), F32), jax.ShapeDtypeStruct((s, qk_w), BF16),
                   jax.ShapeDtypeStruct((s, qk_w), BF16), jax.ShapeDtypeStruct((s, MLA_W), BF16),
                   jax.ShapeDtypeStruct((s, D_MODEL), BF16), jax.ShapeDtypeStruct((qk_w, s), BF16),
                   jax.ShapeDtypeStruct((qk_w, s), BF16), jax.ShapeDtypeStruct((MLA_W, s), BF16)],
        compiler_params=_params(),
    )(x, wlat, b_l, g_q, wuq, g_kv, wk, wv, rc, rsl, rsh, after)


def _attn_fwd(qt, k, vt):
    s = k.shape[0]
    tq, tk = ATT_TQ, ATT_TK
    r = tq // tk
    pairs = HEADS // 2

    def body(qt_ref, k_ref, vt_ref, o_ref, lse_ref):
        i = pl.program_id(1)
        krow = lax.broadcasted_iota(jnp.int32, (tk, tq), 0)
        qcol = lax.broadcasted_iota(jnp.int32, (tk, tq), 1)
        qts = [qt_ref[hh * HEAD_PAD:(hh + 1) * HEAD_PAD, :] for hh in range(2)]

        def scores(j):
            koff = pl.multiple_of(j * tk, tk)
            return tuple(_dot(k_ref[pl.ds(koff, tk), hh * HEAD_PAD:(hh + 1) * HEAD_PAD], qts[hh]) for hh in range(2))

        def weighted(j, ps):
            koff = pl.multiple_of(j * tk, tk)
            return tuple(_dot(vt_ref[hh * V_DIM:(hh + 1) * V_DIM, pl.ds(koff, tk)], ps[hh]) for hh in range(2))

        def step(j, carry, diag, last):
            st, ps, stats = carry
            st_next = None if last else scores(j + 1)
            pvs = weighted(jnp.maximum(j - 1, 0), ps)
            new_ps, new_stats = [], []
            for hh in range(2):
                m, l, acc = stats[hh]
                s_ = st[hh]
                if diag is not None:
                    s_ = jnp.where(krow + diag * tk <= qcol, s_, -jnp.inf)
                m_new = jnp.maximum(m, jnp.max(s_, axis=0, keepdims=True))
                a = jnp.exp2(m - m_new)
                p = jnp.exp2(s_ - m_new)
                new_stats.append((m_new, a * l + jnp.sum(p, axis=0, keepdims=True), a * (acc + pvs[hh])))
                new_ps.append(p.astype(BF16))
            return st_next, tuple(new_ps), tuple(new_stats)

        one = (jnp.full((1, tq), -jnp.inf, F32), jnp.zeros((1, tq), F32), jnp.zeros((V_DIM, tq), F32))
        zero_p = jnp.zeros((tk, tq), BF16)
        nfull = i * r
        carry = lax.fori_loop(0, nfull, functools.partial(step, diag=None, last=False),
                              (scores(0), (zero_p, zero_p), (one, one)))
        for d in range(r):
            carry = step(nfull + d, carry, d, d == r - 1)
        _, ps, stats = carry
        pvs = weighted(nfull + r - 1, ps)
        ot = jnp.concatenate([(stats[hh][2] + pvs[hh]) / stats[hh][1] for hh in range(2)], axis=0)
        o_ref[...] = ot.T
        lse = [stats[hh][0] + jnp.log(stats[hh][1]) * LOG2E for hh in range(2)]
        lse_ref[...] = jnp.concatenate(lse + [jnp.zeros((6, tq), F32)], axis=0)

    return pl.pallas_call(
        body, name="attn_fwd", grid=(pairs, s // tq),
        in_specs=[pl.BlockSpec((2 * HEAD_PAD, tq), lambda p, i: (p, i)),
                  pl.BlockSpec((s, 2 * HEAD_PAD), lambda p, i: (0, p)),
                  pl.BlockSpec((2 * V_DIM, s), lambda p, i: (p, 0))],
        out_specs=[pl.BlockSpec((tq, 2 * V_DIM), lambda p, i: (i, p)),
                   pl.BlockSpec((None, 8, tq), lambda p, i: (p, 0, i))],
        out_shape=[jax.ShapeDtypeStruct((s, MLA_W), F32), jax.ShapeDtypeStruct((pairs, 8, s), F32)],
        compiler_params=_params(("arbitrary", "arbitrary")),
    )(qt, k, vt)


def _attn_bwd(q, qt, k, kt, v, do, dot, lse, delta):
    s = k.shape[0]
    tq, tk = ATT_BWD_TQ, ATT_BWD_TK
    r = tq // tk
    nq = s // tq
    nk = s // tk
    pairs = HEADS // 2

    def body(q_ref, qt_ref, k_ref, kt_ref, v_ref, do_ref, dot_ref, lse_ref, dl_ref, dqt_ref, dk_ref, dv_ref):
        j = pl.program_id(1)
        krow = lax.broadcasted_iota(jnp.int32, (tk, tq), 0)
        qcol = lax.broadcasted_iota(jnp.int32, (tk, tq), 1)
        lane = lax.broadcasted_iota(jnp.int32, (tk, 2 * V_DIM), 1)
        drow = lax.broadcasted_iota(jnp.int32, (2 * V_DIM, tq), 0)

        @pl.when(j == 0)
        def _():
            dqt_ref[...] = jnp.zeros_like(dqt_ref)

        koff = pl.multiple_of(j * tk, tk)
        vb = v_ref[pl.ds(koff, tk), :]
        kbs = [k_ref[pl.ds(koff, tk), hh * HEAD_PAD:(hh + 1) * HEAD_PAD] for hh in range(2)]
        ktbs = [kt_ref[hh * HEAD_PAD:(hh + 1) * HEAD_PAD, pl.ds(koff, tk)] for hh in range(2)]
        i0 = j // r

        def front(i):
            qoff = pl.multiple_of(i * tq, tq)
            dotb = dot_ref[:, pl.ds(qoff, tq)]
            out = []
            for hh in range(2):
                mine = (drow < V_DIM) if hh == 0 else (drow >= V_DIM)
                st = _dot(kbs[hh], qt_ref[hh * HEAD_PAD:(hh + 1) * HEAD_PAD, pl.ds(qoff, tq)])
                out.append((st, _dot(vb, jnp.where(mine, dotb, jnp.zeros_like(dotb)))))
            return tuple(out)

        def middle(i, tiles, diag):
            qoff = pl.multiple_of(i * tq, tq)
            out = []
            for hh in range(2):
                st, dpt = tiles[hh]
                if diag:
                    st = jnp.where(krow + (j - i0 * r) * tk <= qcol, st, -jnp.inf)
                p = jnp.exp2(st - lse_ref[hh:hh + 1, pl.ds(qoff, tq)])
                out.append((p.astype(BF16), (p * (dpt - dl_ref[hh:hh + 1, pl.ds(qoff, tq)])).astype(BF16)))
            return tuple(out)

        def back(i, pd, accs):
            qoff = pl.multiple_of(i * tq, tq)
            dob = do_ref[pl.ds(qoff, tq), :]
            out = []
            for hh in range(2):
                rows = slice(hh * HEAD_PAD, (hh + 1) * HEAD_PAD)
                p, dst = pd[hh]
                dk_acc, dv_acc = accs[hh]
                dv_acc = dv_acc + _dot(p, dob)
                dk_acc = dk_acc + _dot(dst, q_ref[pl.ds(qoff, tq), rows])
                dqt_ref[rows, pl.ds(qoff, tq)] += _dot(ktbs[hh], dst)
                out.append((dk_acc, dv_acc))
            return tuple(out)

        def step(i, accs, diag):
            return back(i, middle(i, front(i), diag), accs)

        zero_acc = (jnp.zeros((tk, HEAD_PAD), F32), jnp.zeros((tk, 2 * V_DIM), F32))
        accs = step(i0, (zero_acc, zero_acc), True)
        accs = lax.fori_loop(i0 + 1, nq, functools.partial(step, diag=False), accs)
        for hh in range(2):
            dk_ref[:, hh * HEAD_PAD:(hh + 1) * HEAD_PAD] = accs[hh][0] * LN2
        dv_ref[...] = jnp.where(lane < V_DIM, accs[0][1], accs[1][1])

        @pl.when(j == nk - 1)
        def _():
            dqt_ref[...] = dqt_ref[...] * SCALE

    pair_rows = lambda w: pl.BlockSpec((s, w), lambda p, j: (0, p))
    pair_cols = lambda w: pl.BlockSpec((w, s), lambda p, j: (p, 0))
    stats = pl.BlockSpec((None, 8, s), lambda p, j: (p, 0, 0))
    return pl.pallas_call(
        body, name="attn_bwd", grid=(pairs, nk),
        in_specs=[pair_rows(2 * HEAD_PAD), pair_cols(2 * HEAD_PAD), pair_rows(2 * HEAD_PAD), pair_cols(2 * HEAD_PAD),
                  pair_rows(2 * V_DIM), pair_rows(2 * V_DIM), pair_cols(2 * V_DIM), stats, stats],
        out_specs=[pair_cols(2 * HEAD_PAD),
                   pl.BlockSpec((tk, 2 * HEAD_PAD), lambda p, j: (j, p)),
                   pl.BlockSpec((tk, 2 * V_DIM), lambda p, j: (j, p))],
        out_shape=[jax.ShapeDtypeStruct((HEADS * HEAD_PAD, s), F32), jax.ShapeDtypeStruct((s, HEADS * HEAD_PAD), F32),
                   jax.ShapeDtypeStruct((s, MLA_W), F32)],
        compiler_params=_params(("arbitrary", "arbitrary")),
    )(q, qt, k, kt, v, do, dot, lse, delta)


def _split3(a):
    hi = a.astype(BF16)
    r1 = a - hi.astype(F32)
    mid = r1.astype(BF16)
    lo = (r1 - mid.astype(F32)).astype(BF16)
    return hi, mid, lo


def _mid(x, tgt, o, hm, hg, woa, wob, wout, ln_g, ln_b, sg_g, sg_b, w_s, bsb):
    s = x.shape[0]
    ts = ROW_TILE
    nsteps = s // ts
    nch = ts // CHUNK
    npair = GROUPS // 2

    def body(x_ref, t_ref, o_ref, hm_ref, hg_ref, woa_ref, wob_ref, wout_ref, lng_ref, lnb_ref, sgg_ref, sgb_ref,
             ws_ref, bsb_ref,
             dr_ref, dhg_ref, dhm_ref, do_ref, dot_ref, dl_ref, dhgt_ref, dhmt_ref,
             dwout_ref, dwoa_ref, dwob_ref, dws_ref, dbs_ref, dlng_ref, dlnb_ref, dsgg_ref, dsgb_ref, loss_ref,
             dbg_ref, dbm_ref, dbacc_ref):
        i = pl.program_id(0)

        @pl.when(i == 0)
        def _():
            for r in (dwout_ref, dwoa_ref, dwob_ref, dws_ref, dlng_ref, dlnb_ref, dsgg_ref, dsgb_ref, loss_ref,
                      dbg_ref, dbm_ref, dbacc_ref):
                r[...] = jnp.zeros_like(r)

        def emit(ref, tref, bref, lo, val):
            vb = val.astype(BF16)
            n = val.shape[1]
            ref[:, lo:lo + n] = vb
            tref[lo:lo + n, :] = vb.T
            bref[:, lo:lo + n] += jnp.sum(val, axis=0, keepdims=True)

        lane = lax.broadcasted_iota(jnp.int32, (CHUNK, CHUNK), 1)
        left = lane < V_DIM
        tril = lax.broadcasted_iota(jnp.int32, (CHUNK, CHUNK), 0) >= lane
        ms = [jnp.where(tril, ws_ref[g], 0.0).astype(BF16) for g in range(GROUPS)]

        z_a = hm_ref[:, 0:SGU_W]
        u = hm_ref[:, SGU_W:2 * SGU_W]
        v = hm_ref[:, 2 * SGU_W:3 * SGU_W]
        z_b = hm_ref[:, 3 * SGU_W:4 * SGU_W]
        o = o_ref[...]
        sa, dsa = _silu_and_grad(z_a)
        y_a = (o * sa).astype(BF16)
        gu, dgu = _gelu_and_grad(u)
        gv, dgv = _gelu_and_grad(v)
        mu = jnp.mean(gv, axis=-1, keepdims=True)
        vc = gv - mu
        rstd_v = lax.rsqrt(jnp.mean(vc * vc, axis=-1, keepdims=True) + LN_EPS)
        vhat = vc * rstd_v
        vn = (vhat * sgg_ref[...] + sgb_ref[...]).astype(BF16)
        rows = []
        for c in range(nch):
            blocks = []
            for p in range(npair):
                blk = vn[c * CHUNK:(c + 1) * CHUNK, p * CHUNK:(p + 1) * CHUNK]
                blocks.append(jnp.where(left, _dot(ms[2 * p], blk), _dot(ms[2 * p + 1], blk)))
            rows.append(jnp.concatenate(blocks, axis=1) + bsb_ref[...])
        mixed = jnp.concatenate(rows, axis=0)
        sgu = gu * mixed
        sb, dsb = _silu_and_grad(z_b)
        y_b = (sgu * sb).astype(BF16)
        pa = jnp.concatenate([_dot(y_a, woa_ref[k]) for k in range(N_SLABS)], axis=1)
        pb = jnp.concatenate([_dot(y_b, wob_ref[k]) for k in range(N_SLABS)], axis=1)
        sga = _sigmoid(hg_ref[:, :D_MODEL])
        sgb = _sigmoid(hg_ref[:, D_MODEL:])
        m2 = (sga * pa + sgb * pb).astype(BF16)
        r = ALPHA * x_ref[...] + _dot(m2, wout_ref[...])
        rmu = jnp.mean(r, axis=-1, keepdims=True)
        rc = r - rmu
        rstd = lax.rsqrt(jnp.mean(rc * rc, axis=-1, keepdims=True) + LN_EPS)
        xhat = rc * rstd
        y = xhat * lng_ref[...] + lnb_ref[...]
        err = y - t_ref[...]
        loss_ref[...] += jnp.full(loss_ref.shape, 0.5 / D_MODEL, F32) * jnp.sum(err * err)

        dy = err * (1.0 / D_MODEL)
        dlng_ref[...] += jnp.sum(dy * xhat, axis=0, keepdims=True)
        dlnb_ref[...] += jnp.sum(dy, axis=0, keepdims=True)
        dxh = dy * lng_ref[...]
        dr = rstd * (dxh - jnp.mean(dxh, axis=-1, keepdims=True) - xhat * jnp.mean(dxh * xhat, axis=-1, keepdims=True))
        dr_ref[...] = dr
        drb = dr.astype(BF16)
        dwout_ref[...] += _dot_tn(m2, drb)
        dm2 = _dot_nt(drb, wout_ref[...])
        emit(dhg_ref, dhgt_ref, dbg_ref, 0, dm2 * pa * sga * (1.0 - sga))
        emit(dhg_ref, dhgt_ref, dbg_ref, D_MODEL, dm2 * pb * sgb * (1.0 - sgb))
        dpa = (dm2 * sga).astype(BF16)
        dpb = (dm2 * sgb).astype(BF16)
        dy_a = jnp.zeros((ts, MLA_W), F32)
        dy_b = jnp.zeros((ts, SGU_W), F32)
        for k in range(N_SLABS):
            cols = slice(k * SLAB_W, (k + 1) * SLAB_W)
            dwoa_ref[k] += _dot_tn(y_a, dpa[:, cols])
            dwob_ref[k] += _dot_tn(y_b, dpb[:, cols])
            dy_a = dy_a + _dot_nt(dpa[:, cols], woa_ref[k])
            dy_b = dy_b + _dot_nt(dpb[:, cols], wob_ref[k])
        dob = (dy_a * sa).astype(BF16)
        do_ref[...] = dob
        dot_ref[...] = dob.T
        head = (lax.broadcasted_iota(jnp.int32, (HEADS, MLA_W), 1) // V_DIM
                == lax.broadcasted_iota(jnp.int32, (HEADS, MLA_W), 0)).astype(BF16)
        dl_ref[...] = sum(_dot_nt(head, term) for term in _split3(dob.astype(F32) * o))
        emit(dhm_ref, dhmt_ref, dbm_ref, 0, dy_a * o * dsa)
        dsg = dy_b * sb
        emit(dhm_ref, dhmt_ref, dbm_ref, 3 * SGU_W, dy_b * sgu * dsb)
        emit(dhm_ref, dhmt_ref, dbm_ref, SGU_W, dsg * mixed * dgu)
        dmixed = dsg * gu
        dvn_rows = []
        dbs_sum = jnp.zeros((CHUNK, SGU_W), F32)
        for c in range(nch):
            dm_c = dmixed[c * CHUNK:(c + 1) * CHUNK, :]
            dbs_sum = dbs_sum + dm_c
            blocks = []
            for p in range(npair):
                dmb = dm_c[:, p * CHUNK:(p + 1) * CHUNK].astype(BF16)
                blk = vn[c * CHUNK:(c + 1) * CHUNK, p * CHUNK:(p + 1) * CHUNK]
                blocks.append(jnp.where(left, _dot_tn(ms[2 * p], dmb), _dot_tn(ms[2 * p + 1], dmb)))
                zero = jnp.zeros_like(dmb)
                dws_ref[2 * p] += jnp.where(tril, _dot_nt(jnp.where(left, dmb, zero), blk), 0.0)
                dws_ref[2 * p + 1] += jnp.where(tril, _dot_nt(jnp.where(left, zero, dmb), blk), 0.0)
            dvn_rows.append(jnp.concatenate(blocks, axis=1))
        dbacc_ref[...] += dbs_sum
        dvn = jnp.concatenate(dvn_rows, axis=0)
        dsgg_ref[...] += jnp.sum(dvn * vhat, axis=0, keepdims=True)
        dsgb_ref[...] += jnp.sum(dvn, axis=0, keepdims=True)
        dvh = dvn * sgg_ref[...]
        dgv_in = rstd_v * (dvh - jnp.mean(dvh, axis=-1, keepdims=True)
                           - vhat * jnp.mean(dvh * vhat, axis=-1, keepdims=True))
        emit(dhm_ref, dhmt_ref, dbm_ref, 2 * SGU_W, dgv_in * dgv)

        @pl.when(i == nsteps - 1)
        def _():
            grp = (lax.broadcasted_iota(jnp.int32, (SGU_W, CHUNK), 0) // V_DIM
                   == lax.broadcasted_iota(jnp.int32, (SGU_W, CHUNK), 1)).astype(BF16)
            hi, mid, lo = _split3(dbacc_ref[...])
            dbs_ref[...] = _dot(hi, grp) + _dot(mid, grp) + _dot(lo, grp)

    acc_shapes = [(D_MODEL, D_MODEL), woa.shape, wob.shape, (GROUPS, CHUNK, CHUNK), (CHUNK, CHUNK),
                  (1, D_MODEL), (1, D_MODEL), (1, SGU_W), (1, SGU_W), (1, 128), (1, GATE_W), (1, MID_W)]
    col_spec = lambda rows: pl.BlockSpec((rows, ts), lambda i: (0, i))
    return pl.pallas_call(
        body, name="mid", grid=(nsteps,),
        in_specs=[_row_spec(ts, D_MODEL), _row_spec(ts, D_MODEL), _row_spec(ts, MLA_W), _row_spec(ts, MID_W),
                  _row_spec(ts, GATE_W), _full_spec(woa.shape), _full_spec(wob.shape), _full_spec(wout.shape),
                  _full_spec(ln_g.shape), _full_spec(ln_b.shape), _full_spec(sg_g.shape), _full_spec(sg_b.shape),
                  _full_spec(w_s.shape), _full_spec(bsb.shape)],
        out_specs=[_row_spec(ts, D_MODEL), _row_spec(ts, GATE_W), _row_spec(ts, MID_W), _row_spec(ts, MLA_W),
                   col_spec(MLA_W), col_spec(HEADS), col_spec(GATE_W), col_spec(MID_W)]
        + [_full_spec(sh) for sh in acc_shapes],
        out_shape=[jax.ShapeDtypeStruct((s, D_MODEL), F32), jax.ShapeDtypeStruct((s, GATE_W), BF16),
                   jax.ShapeDtypeStruct((s, MID_W), BF16), jax.ShapeDtypeStruct((s, MLA_W), BF16),
                   jax.ShapeDtypeStruct((MLA_W, s), BF16), jax.ShapeDtypeStruct((HEADS, s), F32),
                   jax.ShapeDtypeStruct((GATE_W, s), BF16), jax.ShapeDtypeStruct((MID_W, s), BF16)]
        + [jax.ShapeDtypeStruct(sh, F32) for sh in acc_shapes],
        scratch_shapes=[pltpu.VMEM((CHUNK, SGU_W), F32)],
        compiler_params=_params(),
    )(x, tgt, o, hm, hg, woa, wob, wout, ln_g, ln_b, sg_g, sg_b, w_s, bsb)


def _lat_bwd(dq, dk, dv, hl, rc, rsl, rsh, g_q, g_kv, wuq, wk, wv, after):
    s = dk.shape[0]
    ts = ROW_TILE
    qk_w = HEADS * HEAD_PAD

    def body(dq_ref, dk_ref, dv_ref, hl_ref, rc_ref, rsl_ref, rsh_ref, gq_ref, gkv_ref, wuq_ref, wk_ref, wv_ref,
             after_ref, dhl_ref, dhlt_ref, dwuq_ref, dwk_ref, dwv_ref, dgq_ref, dgkv_ref, dbl_ref):
        i = pl.program_id(0)

        @pl.when(i == 0)
        def _():
            for r in (dwuq_ref, dwk_ref, dwv_ref, dgq_ref, dgkv_ref, dbl_ref):
                r[...] = jnp.zeros_like(r)

        def emit(lo, val):
            vb = val.astype(BF16)
            n = val.shape[1]
            dhl_ref[:, lo:lo + n] = vb
            dhlt_ref[lo:lo + n, :] = vb.T
            dbl_ref[:, lo:lo + n] += jnp.sum(val, axis=0, keepdims=True)

        c, sl, sh = rc_ref[...], rsl_ref[...], rsh_ref[...]
        lane = lax.broadcasted_iota(jnp.int32, (ts, HEAD_PAD), 1)
        pe = (lane >= NOPE) & (lane < QK_DIM)
        dkpe = jnp.zeros((ts, HEAD_PAD), F32)
        dqu = []
        for hd in range(HEADS):
            lanes = slice(hd * HEAD_PAD, (hd + 1) * HEAD_PAD)
            dqu.append(_rope_t(dq_ref[lanes, :].T, c, sl, sh).astype(BF16))
            dkpe = dkpe + dk_ref[:, lanes]
        dqu = jnp.concatenate(dqu, axis=1)
        dkpe = _rope_t(jnp.where(pe, dkpe, 0.0), c, sl, sh)

        cq = hl_ref[:, :Q_RANK]
        rq = lax.rsqrt(jnp.mean(cq * cq, axis=-1, keepdims=True) + RMS_EPS)
        cqh = cq * rq
        cqn = (cqh * gq_ref[...]).astype(BF16)
        dwuq_ref[...] += _dot_tn(cqn, dqu)
        dcqn = _dot_nt(dqu, wuq_ref[...])
        dgq_ref[...] += jnp.sum(dcqn * cqh, axis=0, keepdims=True)
        dch = dcqn * gq_ref[...]
        emit(0, rq * (dch - cqh * jnp.mean(dch * cqh, axis=-1, keepdims=True)))

        ckv = hl_ref[:, Q_RANK:Q_RANK + KV_RANK]
        rk = lax.rsqrt(jnp.mean(ckv * ckv, axis=-1, keepdims=True) + RMS_EPS)
        ckh = ckv * rk
        ckn = (ckh * gkv_ref[...]).astype(BF16)
        dkb = dk_ref[...].astype(BF16)
        dvb = dv_ref[...].astype(BF16)
        dwk_ref[...] += _dot_tn(ckn, dkb)
        dwv_ref[...] += _dot_tn(ckn, dvb)
        dckn = _dot_nt(dkb, wk_ref[...]) + _dot_nt(dvb, wv_ref[...])
        dgkv_ref[...] += jnp.sum(dckn * ckh, axis=0, keepdims=True)
        dkh = dckn * gkv_ref[...]
        emit(Q_RANK, rk * (dkh - ckh * jnp.mean(dkh * ckh, axis=-1, keepdims=True)))
        emit(Q_RANK + KV_RANK, dkpe)

    acc_shapes = [wuq.shape, wk.shape, wv.shape, g_q.shape, g_kv.shape, (1, LAT_W)]
    return pl.pallas_call(
        body, name="lat_bwd", grid=(s // ts,),
        in_specs=[pl.BlockSpec((qk_w, ts), lambda i: (0, i)), _row_spec(ts, qk_w), _row_spec(ts, MLA_W),
                  _row_spec(ts, LAT_W), _row_spec(ts, HEAD_PAD), _row_spec(ts, HEAD_PAD), _row_spec(ts, HEAD_PAD),
                  _full_spec(g_q.shape), _full_spec(g_kv.shape), _full_spec(wuq.shape), _full_spec(wk.shape),
                  _full_spec(wv.shape), pl.BlockSpec(memory_space=pl.ANY)],
        out_specs=[_row_spec(ts, LAT_W), pl.BlockSpec((LAT_W, ts), lambda i: (0, i))]
        + [_full_spec(sh) for sh in acc_shapes],
        out_shape=[jax.ShapeDtypeStruct((s, LAT_W), BF16), jax.ShapeDtypeStruct((LAT_W, s), BF16)]
        + [jax.ShapeDtypeStruct(sh, F32) for sh in acc_shapes],
        compiler_params=_params(),
    )(dq, dk, dv, hl, rc, rsl, rsh, g_q, g_kv, wuq, wk, wv, after)


def _dx(dr, dhg, dhm, dhl, wt, after):
    s = dr.shape[0]
    ts = MATMUL_ROW_TILE

    def body(dr_ref, dhg_ref, dhm_ref, dhl_ref, wt_ref, after_ref, dx_ref):
        dx_ref[...] = (ALPHA * dr_ref[...]
                       + _dot(dhg_ref[...], wt_ref[ROW_GATE:IN_W, :])
                       + _dot(dhm_ref[...], wt_ref[LAT_COLS:ROW_GATE, :])
                       + _dot(dhl_ref[:, 0:Q_RANK + KV_RANK], wt_ref[0:Q_RANK + KV_RANK, :])
                       + _dot(dhl_ref[:, Q_RANK + KV_RANK:], _kpe_rows(wt_ref)))

    return pl.pallas_call(
        body, name="dx", grid=(s // ts,),
        in_specs=[_row_spec(ts, D_MODEL), _row_spec(ts, GATE_W), _row_spec(ts, MID_W), _row_spec(ts, LAT_W),
                  _full_spec(wt.shape), pl.BlockSpec(memory_space=pl.ANY)],
        out_specs=_row_spec(ts, D_MODEL),
        out_shape=jax.ShapeDtypeStruct((s, D_MODEL), F32),
        compiler_params=_params(),
    )(dr, dhg, dhm, dhl, wt, after)


def _dwt_early(dhmt, dhgt, xb, col, after, name):
    tn = 512
    nm, ng = MID_W // tn, GATE_W // tn
    s = dhmt.shape[1]
    hc = D_MODEL // 2

    def body(col_ref, dm_ref, dg_ref, xb_ref, after_ref, dw_ref):
        i = pl.program_id(0)

        @pl.when(i < nm)
        def _():
            dw_ref[...] = _dot(dm_ref[...], xb_ref[...]).astype(BF16)

        @pl.when(i >= nm)
        def _():
            dw_ref[...] = _dot(dg_ref[...], xb_ref[...]).astype(BF16)

    rows = pl.pallas_call(
        body, name=name,
        grid_spec=pltpu.PrefetchScalarGridSpec(
            num_scalar_prefetch=1, grid=(nm + ng,),
            in_specs=[pl.BlockSpec((tn, s), lambda i, col_ref: (jnp.minimum(i, nm - 1), 0)),
                      pl.BlockSpec((tn, s), lambda i, col_ref: (jnp.maximum(i - nm, 0), 0)),
                      pl.BlockSpec((s, hc), lambda i, col_ref: (0, col_ref[0])),
                      pl.BlockSpec(memory_space=pl.ANY)],
            out_specs=pl.BlockSpec((pl.Element(tn), pl.Element(hc)),
                                   lambda i, col_ref: (pl.multiple_of(LAT_COLS + i * tn, 32), 0))),
        out_shape=jax.ShapeDtypeStruct((IN_W, hc), BF16),
        compiler_params=_params(),
    )(col, dhmt, dhgt, xb, after)

    def zero(buf_ref, out_ref):
        out_ref[...] = jnp.zeros_like(out_ref)

    return pl.pallas_call(
        zero, name=name + "_zero_lat", grid=(1,), in_specs=[pl.BlockSpec(memory_space=pl.ANY)],
        out_specs=pl.BlockSpec((LAT_COLS, hc), lambda i: (0, 0)),
        out_shape=jax.ShapeDtypeStruct((IN_W, hc), BF16), input_output_aliases={0: 0},
    )(rows)


def _dwt_lat(dhlt, xb):
    n, s = dhlt.shape

    def body(dht_ref, xb_ref, dw_ref):
        dw = _dot(dht_ref[...], xb_ref[...]).astype(BF16)
        kpe = Q_RANK + KV_RANK + NOPE
        dw_ref[0:Q_RANK + KV_RANK, :] = dw[0:Q_RANK + KV_RANK]
        dw_ref[Q_RANK + KV_RANK:LAT_COLS, :] = dw[kpe:kpe + ROPE]
        dw_ref[LAT_COLS:, :] = jnp.zeros((LAT_ROWS_PAD - LAT_COLS, D_MODEL), BF16)

    return pl.pallas_call(
        body, name="dwt_lat", in_specs=[VMEM_SPEC, VMEM_SPEC], out_specs=VMEM_SPEC,
        out_shape=jax.ShapeDtypeStruct((LAT_ROWS_PAD, D_MODEL), BF16),
        compiler_params=pltpu.CompilerParams(vmem_limit_bytes=VMEM_LIMIT),
    )(dhlt, xb)


def _split_bias(b):
    z = lambda n: jnp.zeros((n,), b.dtype)
    lat = jnp.concatenate([b[:Q_RANK + KV_RANK], z(NOPE), b[Q_RANK + KV_RANK:LAT_COLS], z(HEAD_PAD - QK_DIM)])
    return b[None, ROW_GATE:], b[None, LAT_COLS:ROW_GATE], lat[None, :]


def _join_bias(g, m, l):
    kpe = Q_RANK + KV_RANK + NOPE
    return jnp.concatenate([l[0, :Q_RANK + KV_RANK], l[0, kpe:kpe + ROPE], m[0], g[0]])


def _rope_tables(positions):
    half = ROPE // 2
    inv_freq = ROPE_THETA ** (-jnp.arange(0, ROPE, 2, dtype=F32) / ROPE)
    ang = positions.astype(F32)[:, None] * inv_freq
    cos, sin = jnp.cos(ang), jnp.sin(ang)
    n = positions.shape[0]
    one, zero = jnp.ones((n, NOPE), F32), jnp.zeros((n, half), F32)
    tail1, tail0 = jnp.ones((n, HEAD_PAD - QK_DIM), F32), jnp.zeros((n, HEAD_PAD - QK_DIM), F32)
    z64 = jnp.zeros((n, NOPE), F32)
    rc = jnp.concatenate([one, cos, cos, tail1], axis=1)
    rsl = jnp.concatenate([z64, -sin, zero, tail0], axis=1)
    rsh = jnp.concatenate([z64, zero, sin, tail0], axis=1)
    return rc, rsl, rsh


def _local_attention(x, positions, wlat, b_in, g_q, w_uq, g_kv, w_ukv, after):
    rc, rsl, rsh = _rope_tables(positions)
    b_g, b_m, b_l = _split_bias(b_in)
    wuq = jnp.pad(w_uq, ((0, 0), (0, 0), (0, HEAD_PAD - QK_DIM))).reshape(Q_RANK, HEADS * HEAD_PAD).astype(BF16)
    wk = jnp.pad(w_ukv[:, :, :NOPE], ((0, 0), (0, 0), (0, HEAD_PAD - NOPE))).reshape(KV_RANK, HEADS * HEAD_PAD).astype(BF16)
    wv = w_ukv[:, :, NOPE:].reshape(KV_RANK, MLA_W).astype(BF16)
    gq2, gkv2 = g_q[None, :], g_kv[None, :]
    hl, q, k, v, xb, qt, kt, vt = _fwd_lat(x, wlat, b_l, gq2, wuq, gkv2, wk, wv, rc, rsl, rsh, after)
    o, lse = _attn_fwd(qt, k, vt)
    return dict(q=q, qt=qt, k=k, kt=kt, v=v, o=o, lse=lse, hl=hl, rc=rc, rsl=rsl, rsh=rsh, gq2=gq2, gkv2=gkv2,
                wuq=wuq, wk=wk, wv=wv, xb=xb, b_g=b_g, b_m=b_m)


def _local_head(st, x, tgt, wt, w_oa, sg_g, sg_b, w_s, b_s, w_ob, w_out, ln_g, ln_b):
    q, qt, k, kt, v, o, lse, hl, xb = (st[n] for n in ("q", "qt", "k", "kt", "v", "o", "lse", "hl", "xb"))
    rc, rsl, rsh, gq2, gkv2, wuq, wk, wv = (st[n] for n in ("rc", "rsl", "rsh", "gq2", "gkv2", "wuq", "wk", "wv"))
    bsb = jnp.repeat(b_s.T, V_DIM, axis=1)
    hg, hm = _fwd_rest(xb, wt, st["b_g"], st["b_m"])
    (dr, dhg, dhm, do, dot, delta, dhgt, dhmt, dwout, dwoa, dwob, dws, dbs, dlng, dlnb, dsgg, dsgb, loss, dbg,
     dbm) = _mid(x, tgt, o, hm, hg, w_oa, w_ob, w_out, ln_g[None, :], ln_b[None, :], sg_g[None, :], sg_b[None, :],
                 w_s, bsb)
    delta = jnp.pad(delta.reshape(HEADS // 2, 2, -1), ((0, 0), (0, 6), (0, 0)))
    early = {
        "w_oa": dwoa, "sgu_ln_g": dsgg[0], "sgu_ln_b": dsgb[0], "w_s": dws, "b_s": dbs[:, :GROUPS].T,
        "w_ob": dwob, "w_out": dwout, "ln_g": dlng[0], "ln_b": dlnb[0],
    }
    state = dict(q=q, qt=qt, k=k, kt=kt, v=v, do=do, dot=dot, lse=lse, delta=delta, hl=hl, rc=rc, rsl=rsl, rsh=rsh,
                 gq2=gq2, gkv2=gkv2, wuq=wuq, wk=wk, wv=wv, dr=dr, dhg=dhg, dhm=dhm, wt=wt, xb=xb, dbg=dbg, dbm=dbm,
                 dhgt=dhgt, dhmt=dhmt)
    return loss, early, state


def _local_attn_bwd(st):
    return _attn_bwd(st["q"], st["qt"], st["k"], st["kt"], st["v"], st["do"], st["dot"], st["lse"], st["delta"])


def _local_tail(st, dq, dk, dv, after):
    dhl, dhlt, dwuq, dwk, dwv, dgq, dgkv, dbl = _lat_bwd(dq, dk, dv, st["hl"], st["rc"], st["rsl"], st["rsh"],
                                                         st["gq2"], st["gkv2"], st["wuq"], st["wk"], st["wv"], after)
    late = {
        "w_lat": _dwt_lat(dhlt, st["xb"]),
        "b_in": _join_bias(st["dbg"], st["dbm"], dbl),
        "g_q": dgq[0],
        "w_uq": dwuq.reshape(Q_RANK, HEADS, HEAD_PAD)[:, :, :QK_DIM],
        "g_kv": dgkv[0],
        "w_ukv": jnp.concatenate([dwk.reshape(KV_RANK, HEADS, HEAD_PAD)[:, :, :NOPE],
                                  dwv.reshape(KV_RANK, HEADS, V_DIM)], axis=2),
    }
    return dhl, late


def _local_step(x, positions, tgt, wt, b_in, g_q, w_uq, g_kv, w_ukv, w_oa, sg_g, sg_b, w_s, b_s, w_ob, w_out, ln_g,
                ln_b):
    st = _local_attention(x, positions, wt[:LAT_COLS], b_in, g_q, w_uq, g_kv, w_ukv, b_in)
    loss, early, st = _local_head(st, x, tgt, wt, w_oa, sg_g, sg_b, w_s, b_s, w_ob, w_out, ln_g, ln_b)
    dq, dk, dv = _local_attn_bwd(st)
    dhl, late = _local_tail(st, dq, dk, dv, dv)
    dx = _dx(st["dr"], st["dhg"], st["dhm"], dhl, st["wt"], dhl)
    grads = {**early, **late}
    halves = [_dwt_early(st["dhmt"], st["dhgt"], st["xb"], jnp.full((1,), h, jnp.int32), dhl, "dwt_half%d" % h)
              for h in range(2)]
    grads["w_in"] = jnp.concatenate([grads.pop("w_lat")[:LAT_COLS], jnp.concatenate(halves, axis=1)[LAT_COLS:]],
                                    axis=0)
    return loss, dx, grads


MESH = pl.DeviceIdType.MESH
N_CHIPS = 4
HBM_SPEC = pl.BlockSpec(memory_space=pl.ANY)
HBM_SPEC_STRICT = pl.BlockSpec(memory_space=pltpu.HBM)
VMEM_SPEC = pl.BlockSpec(memory_space=pltpu.VMEM)

REP_ROWS = 80


def _rows8(a):
    flat = a.reshape(-1)
    n = -(-flat.shape[0] // (8 * D_MODEL)) * 8 * D_MODEL
    return jnp.pad(flat, (0, n - flat.shape[0])).reshape(-1, D_MODEL)


def _place():
    x, y, c = lax.axis_index("x"), lax.axis_index("y"), lax.axis_index("c")
    others = [(1 - x, y), (x, 1 - y), (1 - x, 1 - y)]
    return x, y, c, others


def _gather_weights(shards):
    n = len(shards)

    def body(*refs):
        ins, outs, bufs = refs[:n], refs[n:2 * n], refs[2 * n:3 * n]
        send_sems, recv_sems, local_sems = refs[3 * n:]
        x, y, c, others = _place()
        me = 2 * x + y
        sibling = (x, y, 1 - c)
        for src, buf in zip(ins, bufs):
            buf[...] = src[...].astype(BF16)
        own = [pltpu.make_async_copy(bufs[w], outs[w].at[me], local_sems.at[w]) for w in range(n)]
        for cp in own:
            cp.start()

        def part(w, chip, half):
            hc = shards[w].shape[1] // 2
            return outs[w].at[chip, :, pl.ds(half * hc, hc)]

        def sent(w, j):
            hc = shards[w].shape[1] // 2
            return pltpu.make_async_remote_copy(
                src_ref=bufs[w].at[:, pl.ds(c * hc, hc)], dst_ref=part(w, me, c),
                send_sem=send_sems.at[w * 3 + j], recv_sem=recv_sems.at[w * 3 + j],
                device_id=(*others[j], c), device_id_type=MESH)

        def landed(w, j):
            px, py = others[j]
            return pltpu.make_async_remote_copy(
                src_ref=part(w, 2 * px + py, c), dst_ref=part(w, 2 * px + py, c),
                send_sem=send_sems.at[w * 3 + j], recv_sem=recv_sems.at[w * 3 + j],
                device_id=(px, py, c), device_id_type=MESH)

        def passed(w, j, half):
            px, py = others[j]
            k = n * 3 + w * 3 + j
            return pltpu.make_async_remote_copy(
                src_ref=part(w, 2 * px + py, half), dst_ref=part(w, 2 * px + py, half),
                send_sem=send_sems.at[k], recv_sem=recv_sems.at[k], device_id=sibling, device_id_type=MESH)

        first = [sent(w, j) for w in range(n) for j in range(3)]
        for cp in first:
            cp.start()
        fwd = []
        for w in range(n):
            for j in range(3):
                landed(w, j).wait_recv()
                cp = passed(w, j, c)
                cp.start()
                fwd.append(cp)
        for w in range(n):
            for j in range(3):
                passed(w, j, 1 - c).wait_recv()
        for cp in first + fwd:
            cp.wait_send()
        for cp in own:
            cp.wait()

    return pl.pallas_call(
        body, name="gather_weights",
        in_specs=[VMEM_SPEC] * n, out_specs=[HBM_SPEC] * n,
        out_shape=[jax.ShapeDtypeStruct((N_CHIPS,) + s.shape, BF16) for s in shards],
        scratch_shapes=[pltpu.VMEM(s.shape, BF16) for s in shards]
        + [pltpu.SemaphoreType.DMA((6 * n,)), pltpu.SemaphoreType.DMA((6 * n,)), pltpu.SemaphoreType.DMA((n,))],
        compiler_params=pltpu.CompilerParams(vmem_limit_bytes=VMEM_LIMIT),
    )(*shards)


N_DEV = 8
LOSS_TILE = (8, 128)


def _gather_first(lat, uq):
    hl, hu = lat.shape[1] // 2, uq.shape[1] // 2

    def body(lat_ref, uq_ref, wlat_ref, guq_ref, lat_buf, uq_buf, send_sems, recv_sems, local_sems):
        x, y, c, others = _place()
        me = 2 * x + y
        sibling = (x, y, 1 - c)
        lat_buf[...] = lat_ref[...].astype(BF16)
        uq_buf[...] = uq_ref[...].astype(BF16)

        def copy(src, dst, k, to):
            return pltpu.make_async_remote_copy(src_ref=src, dst_ref=dst, send_sem=send_sems.at[k],
                                                recv_sem=recv_sems.at[k], device_id=to, device_id_type=MESH)

        def uq_part(chip, half):
            return guq_ref.at[chip, :, pl.ds(half * hu, hu)]

        def lat_part(half):
            return wlat_ref.at[:, pl.ds(half * hl, hl)]

        own = pltpu.make_async_copy(uq_buf, guq_ref.at[me], local_sems.at[0])
        own.start()
        first = [copy(uq_buf.at[:, pl.ds(c * hu, hu)], uq_part(me, c), j, (*others[j], c)) for j in range(3)]
        for cp in first:
            cp.start()

        @pl.when(me == 0)
        def _():
            mine = pltpu.make_async_copy(lat_buf, wlat_ref, local_sems.at[1])
            mine.start()
            cps = [copy(lat_buf.at[:, pl.ds(c * hl, hl)], lat_part(c), 6 + j, (*others[j], c)) for j in range(3)]
            for cp in cps:
                cp.start()
            for cp in cps:
                cp.wait_send()
            mine.wait()

        @pl.when(me != 0)
        def _():
            j0 = x + 2 * y - 1
            copy(lat_part(c), lat_part(c), 6 + j0, (0, 0, c)).wait_recv()
            fwd = copy(lat_part(c), lat_part(c), 9, sibling)
            fwd.start()
            copy(lat_part(1 - c), lat_part(1 - c), 9, sibling).wait_recv()
            fwd.wait_send()

        fwd = []
        for j, (px, py) in enumerate(others):
            chip = 2 * px + py
            copy(uq_part(chip, c), uq_part(chip, c), j, (px, py, c)).wait_recv()
            cp = copy(uq_part(chip, c), uq_part(chip, c), 3 + j, sibling)
            cp.start()
            fwd.append(cp)
        for j, (px, py) in enumerate(others):
            chip = 2 * px + py
            copy(uq_part(chip, 1 - c), uq_part(chip, 1 - c), 3 + j, sibling).wait_recv()
        for cp in first + fwd:
            cp.wait_send()
        own.wait()

    return pl.pallas_call(
        body, name="gather_first", in_specs=[VMEM_SPEC, VMEM_SPEC], out_specs=[HBM_SPEC, HBM_SPEC],
        out_shape=[jax.ShapeDtypeStruct(lat.shape, BF16), jax.ShapeDtypeStruct((N_CHIPS,) + uq.shape, BF16)],
        scratch_shapes=[pltpu.VMEM(lat.shape, BF16), pltpu.VMEM(uq.shape, BF16), pltpu.SemaphoreType.DMA((10,)),
                        pltpu.SemaphoreType.DMA((10,)), pltpu.SemaphoreType.DMA((2,))],
        compiler_params=pltpu.CompilerParams(vmem_limit_bytes=VMEM_LIMIT),
    )(lat, uq)


def _cast_own(shards, me):
    n = len(shards)

    def body(me_ref, *refs):
        for w in range(n):
            refs[n + w][...] = refs[w][...].astype(BF16)

    return pl.pallas_call(
        body, name="cast_own",
        grid_spec=pltpu.PrefetchScalarGridSpec(
            num_scalar_prefetch=1, grid=(1,),
            in_specs=[pl.BlockSpec(s.shape, lambda i, me_ref: (0, 0)) for s in shards],
            out_specs=[pl.BlockSpec((None,) + s.shape, lambda i, me_ref: (me_ref[0], 0, 0)) for s in shards]),
        out_shape=[jax.ShapeDtypeStruct((N_CHIPS,) + s.shape, BF16) for s in shards],
        compiler_params=pltpu.CompilerParams(vmem_limit_bytes=VMEM_LIMIT),
    )(me, *shards)


def _gather_start(bufs, after):
    n = len(bufs)

    def body(*refs):
        b_refs = refs[:n]
        send_sems, recv_sems, token = refs[n + 1], refs[n + 2], refs[-1]
        x, y, c, others = _place()
        me = 2 * x + y
        for w in range(n):
            hc = _half(bufs[w])
            mine = b_refs[w].at[me, :, pl.ds(c * hc, hc)]
            for j, (px, py) in enumerate(others):
                pltpu.make_async_remote_copy(
                    src_ref=mine, dst_ref=mine, send_sem=send_sems.at[3 * w + j], recv_sem=recv_sems.at[3 * w + j],
                    device_id=(px, py, c), device_id_type=MESH).start()
        token[...] = jnp.zeros_like(token)

    hbm = [pltpu.HBM(b.shape, BF16) for b in bufs]
    outs = pl.pallas_call(
        body, name="gather_start",
        out_shape=(pltpu.SemaphoreType.DMA((3 * n,)), pltpu.SemaphoreType.DMA((3 * n,)), *hbm,
                   jax.ShapeDtypeStruct(LOSS_TILE, F32)),
        in_specs=[HBM_SPEC_STRICT] * n + [HBM_SPEC],
        out_specs=(SEM_SPEC, SEM_SPEC, *[HBM_SPEC_STRICT] * n, VMEM_SPEC),
        input_output_aliases={i: 2 + i for i in range(n)},
        compiler_params=pltpu.CompilerParams(has_side_effects=SPLIT_EFFECT),
    )(*[pltpu.with_memory_space_constraint(b, pltpu.HBM) for b in bufs], after)
    return outs[0], outs[1], list(outs[2:2 + n]), outs[-1]


def _gather_wait(send_sems, recv_sems, bufs, after):
    n = len(bufs)

    def body(*refs):
        b_refs = refs[:n]
        send_sems, recv_sems = refs[n], refs[n + 1]
        x, y, c, others = _place()
        me = 2 * x + y
        for w in range(n):
            hc = _half(bufs[w])
            for j, (px, py) in enumerate(others):
                cp = pltpu.make_async_remote_copy(
                    src_ref=b_refs[w].at[me, :, pl.ds(c * hc, hc)],
                    dst_ref=b_refs[w].at[2 * px + py, :, pl.ds(c * hc, hc)],
                    send_sem=send_sems.at[3 * w + j], recv_sem=recv_sems.at[3 * w + j], device_id=(px, py, c),
                    device_id_type=MESH)
                cp.wait_send()
                cp.wait_recv()

    outs = pl.pallas_call(
        body, name="gather_wait", out_shape=tuple(pltpu.HBM(b.shape, b.dtype) for b in bufs),
        in_specs=[HBM_SPEC_STRICT] * n + [SEM_SPEC, SEM_SPEC, HBM_SPEC],
        out_specs=tuple([HBM_SPEC_STRICT] * n), input_output_aliases={i: i for i in range(n)},
        compiler_params=pltpu.CompilerParams(has_side_effects=SPLIT_EFFECT),
    )(*bufs, send_sems, recv_sems, after)
    return list(outs)


def _gather_finish(bufs):
    n = len(bufs)

    def body(*refs):
        b_refs = refs[n:2 * n]
        send_sems, recv_sems = refs[2 * n:]
        x, y, c, others = _place()
        cps = []
        for w in range(n):
            hc = _half(bufs[w])
            for j, (px, py) in enumerate(others):
                part = b_refs[w].at[2 * px + py, :, pl.ds(c * hc, hc)]
                cps.append(pltpu.make_async_remote_copy(
                    src_ref=part, dst_ref=part, send_sem=send_sems.at[3 * w + j], recv_sem=recv_sems.at[3 * w + j],
                    device_id=(x, y, 1 - c), device_id_type=MESH))
        for cp in cps:
            cp.start()
        for w in range(n):
            hc = _half(bufs[w])
            for j, (px, py) in enumerate(others):
                theirs = b_refs[w].at[2 * px + py, :, pl.ds((1 - c) * hc, hc)]
                pltpu.make_async_remote_copy(
                    src_ref=theirs, dst_ref=theirs, send_sem=send_sems.at[3 * w + j], recv_sem=recv_sems.at[3 * w + j],
                    device_id=(x, y, 1 - c), device_id_type=MESH).wait_recv()
        for cp in cps:
            cp.wait_send()

    return pl.pallas_call(
        body, name="gather_finish", in_specs=[HBM_SPEC] * n, out_specs=[HBM_SPEC] * n,
        out_shape=[jax.ShapeDtypeStruct(b.shape, b.dtype) for b in bufs],
        input_output_aliases={i: i for i in range(n)},
        scratch_shapes=[pltpu.SemaphoreType.DMA((3 * n,)), pltpu.SemaphoreType.DMA((3 * n,))],
    )(*bufs)


def _half(a):
    return a.shape[-1] // 2


def _exchange_pairs(parts, name):
    n = len(parts)

    def body(*refs):
        p_refs, r_refs = refs[:n], refs[n:2 * n]
        send_sems, recv_sems = refs[2 * n:]
        x, y, c, _ = _place()
        cps = []
        for w in range(n):
            h = _half(parts[w])
            cps.append(pltpu.make_async_remote_copy(
                src_ref=p_refs[w].at[:, :, pl.ds((1 - c) * h, h)], dst_ref=r_refs[w],
                send_sem=send_sems.at[w], recv_sem=recv_sems.at[w], device_id=(x, y, 1 - c), device_id_type=MESH))
        for cp in cps:
            cp.start()
        for cp in cps:
            cp.wait()

    return pl.pallas_call(
        body, name=name, in_specs=[HBM_SPEC] * n, out_specs=[HBM_SPEC] * n,
        out_shape=[jax.ShapeDtypeStruct((N_CHIPS, p.shape[1], _half(p)), BF16) for p in parts],
        scratch_shapes=[pltpu.SemaphoreType.DMA((n,)), pltpu.SemaphoreType.DMA((n,))],
    )(*parts)


def _sibling_part(ref, w, n_whole, shape, c):
    if w < n_whole:
        return ref
    h = shape[-1] // 2
    return ref.at[:, :, pl.ds((1 - c) * h, h)]


def _pairs_start(parts, all_loss, n_whole):
    n = len(parts)

    def body(*refs):
        p_refs, r_refs, loss_ref = refs[:n], refs[n:2 * n], refs[2 * n]
        send_sems, recv_sems, token = refs[2 * n + 1], refs[2 * n + 2], refs[-1]
        x, y, c, _ = _place()
        for w in range(n):
            h = _half(parts[w])
            pltpu.make_async_remote_copy(
                src_ref=_sibling_part(p_refs[w], w, n_whole, parts[w].shape, c), dst_ref=r_refs[w],
                send_sem=send_sems.at[w], recv_sem=recv_sems.at[w], device_id=(x, y, 1 - c),
                device_id_type=MESH).start()
        me = 4 * x + 2 * y + c
        for t in range(1, N_DEV):
            d = (me + t) % N_DEV
            pltpu.make_async_remote_copy(
                src_ref=loss_ref.at[me], dst_ref=loss_ref.at[me], send_sem=send_sems.at[n + t - 1],
                recv_sem=recv_sems.at[n + t - 1], device_id=(d // 4, (d // 2) % 2, d % 2), device_id_type=MESH).start()
        token[...] = jnp.zeros_like(token)

    lands = [pltpu.HBM(p.shape if w < n_whole else (N_CHIPS, p.shape[1], _half(p)), BF16)
             for w, p in enumerate(parts)]
    nsem = n + N_DEV - 1
    outs = pl.pallas_call(
        body, name="pairs_start",
        out_shape=(pltpu.SemaphoreType.DMA((nsem,)), pltpu.SemaphoreType.DMA((nsem,)),
                   *[pltpu.HBM(p.shape, p.dtype) for p in parts], *lands, pltpu.HBM(all_loss.shape, F32),
                   jax.ShapeDtypeStruct(LOSS_TILE, F32)),
        in_specs=[HBM_SPEC_STRICT] * (2 * n + 1),
        out_specs=(SEM_SPEC, SEM_SPEC, *[HBM_SPEC_STRICT] * (2 * n + 1), VMEM_SPEC),
        input_output_aliases={i: 2 + i for i in range(2 * n + 1)},
        compiler_params=pltpu.CompilerParams(has_side_effects=SPLIT_EFFECT),
    )(*[pltpu.with_memory_space_constraint(p, pltpu.HBM) for p in parts],
      *[pltpu.with_memory_space_constraint(lax.empty(l.shape, BF16), pltpu.HBM) for l in lands],
      pltpu.with_memory_space_constraint(all_loss, pltpu.HBM))
    return outs[0], outs[1], list(outs[2:2 + n]), list(outs[2 + n:2 + 2 * n]), outs[2 + 2 * n], outs[-1]


def _pairs_wait(send_sems, recv_sems, parts, lands, all_loss, after, n_whole):
    n = len(parts)

    def body(*refs):
        p_refs, r_refs, loss_ref = refs[:n], refs[n:2 * n], refs[2 * n]
        send_sems, recv_sems = refs[2 * n + 1], refs[2 * n + 2]
        x, y, c, _ = _place()
        for w in range(n):
            h = _half(parts[w])
            cp = pltpu.make_async_remote_copy(
                src_ref=_sibling_part(p_refs[w], w, n_whole, parts[w].shape, c), dst_ref=r_refs[w],
                send_sem=send_sems.at[w],
                recv_sem=recv_sems.at[w], device_id=(x, y, 1 - c), device_id_type=MESH)
            cp.wait_send()
            cp.wait_recv()
        me = 4 * x + 2 * y + c
        for t in range(1, N_DEV):
            d = (me + N_DEV - t) % N_DEV
            cp = pltpu.make_async_remote_copy(
                src_ref=loss_ref.at[me], dst_ref=loss_ref.at[d], send_sem=send_sems.at[n + t - 1],
                recv_sem=recv_sems.at[n + t - 1], device_id=(d // 4, (d // 2) % 2, d % 2), device_id_type=MESH)
            cp.wait_send()
            cp.wait_recv()

    bufs = (*parts, *lands, all_loss)
    outs = pl.pallas_call(
        body, name="pairs_wait", out_shape=tuple(pltpu.HBM(a.shape, a.dtype) for a in bufs),
        in_specs=[HBM_SPEC_STRICT] * len(bufs) + [SEM_SPEC, SEM_SPEC, HBM_SPEC],
        out_specs=tuple([HBM_SPEC_STRICT] * len(bufs)), input_output_aliases={i: i for i in range(len(bufs))},
        compiler_params=pltpu.CompilerParams(has_side_effects=SPLIT_EFFECT),
    )(*bufs, send_sems, recv_sems, after)
    return list(outs[:n]), list(outs[n:2 * n]), outs[2 * n]


def _add_pair_tiled(p, r):
    rows, h = r.shape[1:]

    def body(p_ref, r_ref, q_ref):
        q_ref[...] = (p_ref[...].astype(F32) + r_ref[...].astype(F32)).astype(BF16)

    spec = pl.BlockSpec((None, rows, h), lambda k: (k, 0, 0))
    return pl.pallas_call(
        body, name="add_pair_w_in", grid=(N_CHIPS,), in_specs=[spec, spec], out_specs=spec,
        out_shape=jax.ShapeDtypeStruct(r.shape, BF16),
    )(p, r)


def _add_pair_small(ps, rs, c, name):
    n = len(ps)

    def body(c_ref, *refs):
        for w in range(n):
            h = _half(ps[w])
            mine = refs[w][:, :, pl.ds(pl.multiple_of(c_ref[0] * h, 128), h)]
            refs[2 * n + w][...] = (mine.astype(F32) + refs[n + w][...].astype(F32)).astype(BF16)

    return pl.pallas_call(
        body, name=name,
        in_specs=[pl.BlockSpec(memory_space=pltpu.SMEM)] + [VMEM_SPEC] * (2 * n), out_specs=[VMEM_SPEC] * n,
        out_shape=[jax.ShapeDtypeStruct(r.shape, BF16) for r in rs],
        compiler_params=pltpu.CompilerParams(vmem_limit_bytes=VMEM_LIMIT),
    )(c, *ps, *rs)


def _exchange_chips(qs):
    n = len(qs)

    def body(*refs):
        q_refs, r_refs = refs[:n], refs[n:2 * n]
        send_sems, recv_sems = refs[2 * n:]
        x, y, c, others = _place()
        me = 2 * x + y
        cps = []
        for w in range(n):
            for j, (px, py) in enumerate(others):
                cps.append(pltpu.make_async_remote_copy(
                    src_ref=q_refs[w].at[2 * px + py], dst_ref=r_refs[w].at[me], send_sem=send_sems.at[3 * w + j],
                    recv_sem=recv_sems.at[3 * w + j], device_id=(px, py, c), device_id_type=MESH))
        for cp in cps:
            cp.start()
        for w in range(n):
            for j, (px, py) in enumerate(others):
                pltpu.make_async_remote_copy(
                    src_ref=q_refs[w].at[me], dst_ref=r_refs[w].at[2 * px + py], send_sem=send_sems.at[3 * w + j],
                    recv_sem=recv_sems.at[3 * w + j], device_id=(px, py, c), device_id_type=MESH).wait_recv()
        for cp in cps:
            cp.wait_send()

    return pl.pallas_call(
        body, name="exchange_chips", in_specs=[HBM_SPEC] * n, out_specs=[HBM_SPEC] * n,
        out_shape=[jax.ShapeDtypeStruct(q.shape, BF16) for q in qs],
        scratch_shapes=[pltpu.SemaphoreType.DMA((3 * n,)), pltpu.SemaphoreType.DMA((3 * n,))],
    )(*qs)


SEM_SPEC = pl.BlockSpec(memory_space=pltpu.SEMAPHORE)
SPLIT_EFFECT = pltpu.SideEffectType.DATAFLOW_SIDE_EFFECTING


def _chips_start(qs, name):
    n = len(qs)

    def body(*refs):
        q_refs, land_refs = refs[:n], refs[n:2 * n]
        send_sems, recv_sems, token = refs[2 * n], refs[2 * n + 1], refs[-1]
        x, y, c, others = _place()
        me = 2 * x + y
        for w in range(n):
            for j, (px, py) in enumerate(others):
                pltpu.make_async_remote_copy(
                    src_ref=q_refs[w].at[2 * px + py], dst_ref=land_refs[w].at[me], send_sem=send_sems.at[3 * w + j],
                    recv_sem=recv_sems.at[3 * w + j], device_id=(px, py, c), device_id_type=MESH).start()
        token[...] = jnp.zeros_like(token)

    hbm = [pltpu.HBM(q.shape, BF16) for q in qs]
    outs = pl.pallas_call(
        body, name=name,
        out_shape=(pltpu.SemaphoreType.DMA((3 * n,)), pltpu.SemaphoreType.DMA((3 * n,)), *hbm, *hbm,
                   jax.ShapeDtypeStruct(LOSS_TILE, F32)),
        in_specs=[HBM_SPEC_STRICT] * (2 * n),
        out_specs=(SEM_SPEC, SEM_SPEC, *[HBM_SPEC_STRICT] * (2 * n), VMEM_SPEC),
        input_output_aliases={i: 2 + i for i in range(2 * n)},
        compiler_params=pltpu.CompilerParams(has_side_effects=SPLIT_EFFECT),
    )(*[pltpu.with_memory_space_constraint(q, pltpu.HBM) for q in qs],
      *[pltpu.with_memory_space_constraint(lax.empty(q.shape, BF16), pltpu.HBM) for q in qs])
    return outs[0], outs[1], outs[2:2 + n], outs[2 + n:2 + 2 * n], outs[-1]


def _chips_wait(send_sems, recv_sems, q_thru, land_thru, after, name):
    n = len(q_thru)

    def body(*refs):
        q_refs, land_refs = refs[:n], refs[n:2 * n]
        send_sems, recv_sems = refs[2 * n], refs[2 * n + 1]
        x, y, c, others = _place()
        me = 2 * x + y
        for w in range(n):
            for j, (px, py) in enumerate(others):
                cp = pltpu.make_async_remote_copy(
                    src_ref=q_refs[w].at[2 * px + py], dst_ref=land_refs[w].at[2 * px + py],
                    send_sem=send_sems.at[3 * w + j], recv_sem=recv_sems.at[3 * w + j], device_id=(px, py, c),
                    device_id_type=MESH)
                cp.wait_send()
                cp.wait_recv()

    outs = pl.pallas_call(
        body, name=name, out_shape=tuple(pltpu.HBM(a.shape, a.dtype) for a in (*q_thru, *land_thru)),
        in_specs=[HBM_SPEC_STRICT] * (2 * n) + [SEM_SPEC, SEM_SPEC, HBM_SPEC],
        out_specs=tuple([HBM_SPEC_STRICT] * (2 * n)), input_output_aliases={i: i for i in range(2 * n)},
        compiler_params=pltpu.CompilerParams(has_side_effects=SPLIT_EFFECT),
    )(*q_thru, *land_thru, send_sems, recv_sems, after)
    return list(outs[:n]), list(outs[n:])


def _sum_chips_tiled(q, r, idx, tile):
    rows, h = r.shape[1:]
    nt = h // tile

    def body(idx_ref, q_ref, r0_ref, r1_ref, r2_ref, g_ref):
        g_ref[...] = (q_ref[...].astype(F32) + r0_ref[...].astype(F32) + r1_ref[...].astype(F32)
                      + r2_ref[...].astype(F32))

    def slab(t):
        return pl.BlockSpec((None, rows, tile), lambda i, idx_ref: (idx_ref[t], 0, i))

    return pl.pallas_call(
        body, name="sum_chips_w_in",
        grid_spec=pltpu.PrefetchScalarGridSpec(
            num_scalar_prefetch=1, grid=(nt,), in_specs=[slab(0), slab(1), slab(2), slab(3)],
            out_specs=pl.BlockSpec((rows, tile), lambda i, idx_ref: (0, idx_ref[4] * nt + i))),
        out_shape=jax.ShapeDtypeStruct((rows, 2 * h), F32),
    )(idx, q, r, r, r)


def _sum_chips_small(qs, rs, idx, all_dtypes):
    n = len(rs)
    n_all = len(all_dtypes)

    def body(idx_ref, *refs):
        c = idx_ref[4]
        for w in range(n):
            q_ref, r_ref, g_ref = refs[w], refs[n + w], refs[2 * n + w]
            acc = q_ref[idx_ref[0]].astype(F32)
            for t in range(1, N_CHIPS):
                acc = acc + r_ref[idx_ref[t]].astype(F32)
            h = rs[w].shape[2]
            mine = pl.ds(pl.multiple_of(c * h, 128), h)
            g_ref[...] = jnp.zeros_like(g_ref)
            if w >= n - n_all:
                g_ref[idx_ref[0], :, mine] = acc.astype(g_ref.dtype)
            else:
                g_ref[:, mine] = acc

    shapes = [jax.ShapeDtypeStruct((r.shape[1], 2 * r.shape[2]), F32) for r in rs[:n - n_all]]
    shapes += [jax.ShapeDtypeStruct((N_CHIPS, r.shape[1], 2 * r.shape[2]), dt)
               for r, dt in zip(rs[n - n_all:], all_dtypes)]
    return pl.pallas_call(
        body, name="sum_chips_small",
        in_specs=[pl.BlockSpec(memory_space=pltpu.SMEM)] + [VMEM_SPEC] * (2 * n), out_specs=[VMEM_SPEC] * n,
        out_shape=shapes, compiler_params=pltpu.CompilerParams(vmem_limit_bytes=VMEM_LIMIT),
    )(idx, *qs, *rs)


def _share(shards, alls):
    n, na = len(shards), len(alls)
    total = n + na

    def body(*refs):
        g_refs, a_refs = refs[total:total + n], refs[total + n:2 * total]
        send_sems, recv_sems = refs[2 * total:]
        x, y, c, others = _place()
        me = 2 * x + y
        sibling = (x, y, 1 - c)

        def cols_of(w, half):
            h = shards[w].shape[1] // 2
            return g_refs[w].at[:, pl.ds(half * h, h)]

        def slab(a, chip, half):
            h = alls[a].shape[2] // 2
            return a_refs[a].at[chip, :, pl.ds(half * h, h)]

        def copy(src, dst, k, to):
            return pltpu.make_async_remote_copy(src_ref=src, dst_ref=dst, send_sem=send_sems.at[k],
                                                recv_sem=recv_sems.at[k], device_id=to, device_id_type=MESH)

        cps = [copy(cols_of(w, c), cols_of(w, c), w, sibling) for w in range(n)]
        for a in range(na):
            base = n + 7 * a
            cps.append(copy(slab(a, me, c), slab(a, me, c), base, sibling))
            for j, (px, py) in enumerate(others):
                cps.append(copy(slab(a, me, c), slab(a, me, c), base + 1 + j, (px, py, c)))
        for cp in cps:
            cp.start()
        fwd = []
        for a in range(na):
            base = n + 7 * a
            for j, (px, py) in enumerate(others):
                chip = 2 * px + py
                copy(slab(a, me, c), slab(a, chip, c), base + 1 + j, (px, py, c)).wait_recv()
                cp = copy(slab(a, chip, c), slab(a, chip, c), base + 4 + j, sibling)
                cp.start()
                fwd.append(cp)
        for a in range(na):
            base = n + 7 * a
            for j, (px, py) in enumerate(others):
                chip = 2 * px + py
                copy(slab(a, chip, c), slab(a, chip, 1 - c), base + 4 + j, sibling).wait_recv()
            copy(slab(a, me, c), slab(a, me, 1 - c), base, sibling).wait_recv()
        for w in range(n):
            copy(cols_of(w, c), cols_of(w, 1 - c), w, sibling).wait_recv()
        for cp in cps + fwd:
            cp.wait_send()

    nsem = n + 7 * na
    return pl.pallas_call(
        body, name="share", in_specs=[HBM_SPEC] * total, out_specs=[HBM_SPEC] * total,
        out_shape=[jax.ShapeDtypeStruct(a.shape, a.dtype) for a in (*shards, *alls)],
        input_output_aliases={i: i for i in range(total)},
        scratch_shapes=[pltpu.SemaphoreType.DMA((nsem,)), pltpu.SemaphoreType.DMA((nsem,))],
    )(*shards, *alls)


def _adamw(w, g, m, v):
    m2 = ADAM_B1 * m + (1.0 - ADAM_B1) * g
    v2 = ADAM_B2 * v + (1.0 - ADAM_B2) * (g * g)
    m_hat = m2 / (1.0 - ADAM_B1 ** ADAM_STEP)
    v_hat = v2 / (1.0 - ADAM_B2 ** ADAM_STEP)
    return -ADAM_LR * (m_hat / (jnp.sqrt(v_hat) + ADAM_EPS) + ADAM_WD * w), m2, v2


def _update_w_in(wt, gt, mt, vt, lat, owner, tile):
    nlat = lat.shape[0] // tile

    def body(owner_ref, w_ref, g_ref, m_ref, v_ref, lat_ref, g2_ref, d_ref, m2_ref, v2_ref):
        row = pl.program_id(0) * tile + lax.broadcasted_iota(jnp.int32, (tile, 1), 0)
        g = jnp.where((row < LAT_COLS) & (owner_ref[0] == 1), lat_ref[...].astype(F32), g_ref[...])
        g2_ref[...] = g
        d_ref[...], m2_ref[...], v2_ref[...] = _adamw(w_ref[...], g, m_ref[...], v_ref[...])

    spec = pl.BlockSpec((tile, wt.shape[1]), lambda i, o: (i, 0))
    return pl.pallas_call(
        body, name="update_w_in",
        grid_spec=pltpu.PrefetchScalarGridSpec(
            num_scalar_prefetch=1, grid=(wt.shape[0] // tile,),
            in_specs=[spec] * 4 + [pl.BlockSpec((tile, wt.shape[1]), lambda i, o: (jnp.minimum(i, nlat - 1), 0))],
            out_specs=[spec] * 4),
        out_shape=[jax.ShapeDtypeStruct(wt.shape, F32)] * 4,
        compiler_params=_params(("parallel",)),
    )(owner, wt, gt, mt, vt, lat)


def _update_small(ws, gs, ms, vs):
    n = len(ws)

    def body(*refs):
        for k in range(n):
            w_ref, g_ref, m_ref, v_ref = refs[k], refs[n + k], refs[2 * n + k], refs[3 * n + k]
            d, m2, v2 = _adamw(w_ref[...], g_ref[...], m_ref[...], v_ref[...])
            refs[4 * n + k][...] = d
            refs[5 * n + k][...] = m2
            refs[6 * n + k][...] = v2

    shapes = [jax.ShapeDtypeStruct(w.shape, F32) for w in ws]
    outs = pl.pallas_call(
        body, name="update_small", in_specs=[VMEM_SPEC] * (4 * n), out_specs=[VMEM_SPEC] * (3 * n),
        out_shape=shapes * 3,
        compiler_params=pltpu.CompilerParams(vmem_limit_bytes=VMEM_LIMIT),
    )(*ws, *gs, *ms, *vs)
    return outs[:n], outs[n:2 * n], outs[2 * n:]


SHARDED = ("w_in", "w_uq", "w_oa", "w_ob", "w_out")
REPLICATED = ("b_in", "g_q", "g_kv", "w_ukv", "sgu_ln_g", "sgu_ln_b", "w_s", "b_s", "ln_g", "ln_b")
ORDER = ("w_in", "b_in", "g_q", "w_uq", "g_kv", "w_ukv", "w_oa", "sgu_ln_g", "sgu_ln_b", "w_s", "b_s", "w_ob", "w_out",
         "ln_g", "ln_b")


def kernel(x, positions, w_in, b_in, g_q, w_uq, g_kv, w_ukv, w_oa, sgu_ln_g, sgu_ln_b, w_s, b_s, w_ob, w_out, ln_g, ln_b, loss_target, m_w_in, m_b_in, m_g_q, m_w_uq, m_g_kv, m_w_ukv, m_w_oa, m_sgu_ln_g, m_sgu_ln_b, m_w_s, m_b_s, m_w_ob, m_w_out, m_ln_g, m_ln_b, v_w_in, v_b_in, v_g_q, v_w_uq, v_g_kv, v_w_ukv, v_w_oa, v_sgu_ln_g, v_sgu_ln_b, v_w_s, v_b_s, v_w_ob, v_w_out, v_ln_g, v_ln_b):
    w = dict(w_in=w_in, b_in=b_in, g_q=g_q, w_uq=w_uq, g_kv=g_kv, w_ukv=w_ukv, w_oa=w_oa, sgu_ln_g=sgu_ln_g,
             sgu_ln_b=sgu_ln_b, w_s=w_s, b_s=b_s, w_ob=w_ob, w_out=w_out, ln_g=ln_g, ln_b=ln_b)
    m = dict(w_in=m_w_in, b_in=m_b_in, g_q=m_g_q, w_uq=m_w_uq, g_kv=m_g_kv, w_ukv=m_w_ukv, w_oa=m_w_oa,
             sgu_ln_g=m_sgu_ln_g, sgu_ln_b=m_sgu_ln_b, w_s=m_w_s, b_s=m_b_s, w_ob=m_w_ob, w_out=m_w_out, ln_g=m_ln_g,
             ln_b=m_ln_b)
    v = dict(w_in=v_w_in, b_in=v_b_in, g_q=v_g_q, w_uq=v_w_uq, g_kv=v_g_kv, w_ukv=v_w_ukv, w_oa=v_w_oa,
             sgu_ln_g=v_sgu_ln_g, sgu_ln_b=v_sgu_ln_b, w_s=v_w_s, b_s=v_b_s, w_ob=v_w_ob, w_out=v_w_out, ln_g=v_ln_g,
             ln_b=v_ln_b)
    w, m, v = ({n: a[0] for n, a in d.items()} for d in (w, m, v))
    c = lax.axis_index("c")

    wt_shard, mt_shard, vt_shard = (jnp.transpose(d["w_in"]) for d in (w, m, v))
    xi, yi = lax.axis_index("x"), lax.axis_index("y")
    me1 = (2 * xi + yi).reshape(1).astype(jnp.int32)
    g_lat, g_uq = _gather_first(wt_shard[:LAT_COLS], w["w_uq"].reshape(Q_RANK // 4, HEADS * QK_DIM))
    bufs = _cast_own([wt_shard, w["w_oa"], w["w_ob"], w["w_out"]], me1)
    send0, recv0, bufs, token0 = _gather_start(bufs, g_lat)
    st = _local_attention(x[0], positions[0], g_lat, w["b_in"], w["g_q"], g_uq.reshape(Q_RANK, HEADS, QK_DIM),
                          w["g_kv"], w["w_ukv"], token0)
    g_in, g_oa, g_ob, g_out = _gather_finish(_gather_wait(send0, recv0, bufs, st["o"]))
    wt = g_in.reshape(IN_W, D_MODEL)

    loss, early, st = _local_head(
        st, x[0], loss_target[0], wt, g_oa, w["sgu_ln_g"], w["sgu_ln_b"], w["w_s"], w["b_s"], g_ob,
        g_out.reshape(D_MODEL, D_MODEL), w["ln_g"], w["ln_b"])

    c1 = c.reshape(1).astype(jnp.int32)
    idx = jnp.stack([2 * xi + yi, 2 * (1 - xi) + yi, 2 * xi + (1 - yi), 2 * (1 - xi) + (1 - yi), c]).astype(jnp.int32)
    slabs = lambda a: a.reshape(N_CHIPS, IN_W // N_CHIPS, D_MODEL // 2)
    theirs = slabs(_dwt_early(st["dhmt"], st["dhgt"], st["xb"], 1 - c1, c1, "dwt_theirs"))
    parts1 = [theirs, early["w_oa"].astype(BF16), early["w_ob"].astype(BF16),
              early["w_out"].reshape(N_CHIPS, SLAB_W, D_MODEL).astype(BF16)]
    my_loss = lax.dynamic_update_slice(jnp.zeros((N_DEV,) + LOSS_TILE, F32), jnp.broadcast_to(loss, (1,) + LOSS_TILE),
                                       (4 * xi + 2 * yi + c, 0, 0))
    sems0 = _pairs_start(parts1, my_loss, 1)
    mine = slabs(_dwt_early(st["dhmt"], st["dhgt"], st["xb"], c1, sems0[5], "dwt_mine"))
    parts1, recv1, all_loss = _pairs_wait(*sems0[:5], mine, 1)
    pairs1 = [_add_pair_tiled(mine, recv1[0]), *_add_pair_small(parts1[1:], recv1[1:], c1, "add_pair_early")]
    sems1 = _chips_start(pairs1, "chips_start_early")
    st["delta"] = st["delta"] + sems1[4][0, 0]
    dq, dk, dv = _local_attn_bwd(st)
    dhl, late = _local_tail(st, dq, dk, dv, dv)

    grads = {**early, **late}
    rep = jnp.concatenate([_rows8(grads[n]) for n in REPLICATED], axis=0)
    rep = jnp.pad(rep, ((0, N_CHIPS * REP_ROWS - rep.shape[0]), (0, 0))).reshape(N_CHIPS, REP_ROWS, D_MODEL)
    parts2 = [late["w_uq"].reshape(N_CHIPS, Q_RANK // N_CHIPS, HEADS * QK_DIM).astype(BF16), rep.astype(BF16),
              late["w_lat"].reshape(N_CHIPS, LAT_ROWS_PAD // N_CHIPS, D_MODEL)]
    pairs2 = _add_pair_small(parts2, _exchange_pairs(parts2, "exchange_pairs_late"), c1, "add_pair_late")
    sems2 = _chips_start(pairs2, "chips_start_late")
    dx = _dx(st["dr"], st["dhg"], st["dhm"], dhl, st["wt"], sems2[4])
    pairs2, landed2 = _chips_wait(*sems2[:4], dx, "chips_wait_late")
    pairs1, landed1 = _chips_wait(*sems1[:4], landed2[0], "chips_wait_early")
    sums = [_sum_chips_tiled(pairs1[0], landed1[0], idx, 128),
            *_sum_chips_small([*pairs1[1:], *pairs2], [*landed1[1:], *landed2], idx, (F32, BF16))]
    *shards, g_rep, g_lat = _share(sums[:-2], sums[-2:])
    loss = jnp.sum(all_loss[:, 0, 0])

    red = {n: s.reshape(w[n].shape) for n, s in zip(("w_oa", "w_ob", "w_out", "w_uq"), shards[1:])}
    g_rep = g_rep.reshape(N_CHIPS * REP_ROWS, D_MODEL)
    off = 0
    for n in REPLICATED:
        rows = _rows8(w[n]).shape[0]
        red[n] = g_rep[off:off + rows].reshape(-1)[:w[n].size].reshape(w[n].shape)
        off += rows
    owner = (2 * xi + yi == 0).astype(jnp.int32).reshape(1)
    gt, dt, mt, vt2 = _update_w_in(wt_shard, shards[0], mt_shard, vt_shard,
                                   g_lat.reshape(LAT_ROWS_PAD, D_MODEL).astype(F32), owner, 232)
    red["w_in"] = jnp.transpose(gt)
    small = [n for n in ORDER if n != "w_in"]
    as2d = lambda a: a.reshape(-1, a.shape[-1])
    ds, ms, vs = _update_small([as2d(w[n]) for n in small], [as2d(red[n]) for n in small],
                               [as2d(m[n]) for n in small], [as2d(v[n]) for n in small])
    delta, new_m, new_v = {"w_in": jnp.transpose(dt)}, {"w_in": jnp.transpose(mt)}, {"w_in": jnp.transpose(vt2)}
    for i, n in enumerate(small):
        delta[n], new_m[n], new_v[n] = (a[i].reshape(w[n].shape) for a in (ds, ms, vs))

    lead = lambda a: a[None]
    return (loss, dx[None], *[lead(red[n]) for n in ORDER], *[lead(delta[n]) for n in ORDER],
            *[lead(new_m[n]) for n in ORDER], *[lead(new_v[n]) for n in ORDER])
```

```python
import functools
import math

import jax
import jax.numpy as jnp
from jax import lax
from jax.experimental import pallas as pl
from jax.experimental.pallas import tpu as pltpu

F32 = jnp.float32
BF16 = jnp.bfloat16

D_MODEL = 1024
HEADS = 8
Q_RANK = 384
KV_RANK = 128
NOPE = 64
ROPE = 32
V_DIM = 64
QK_DIM = NOPE + ROPE
HEAD_PAD = 128
MLA_W = HEADS * V_DIM
SGU_W = 512
GROUPS = 8
CHUNK = 128
IN_W = 4640
RMS_EPS = 1e-6
LN_EPS = 1e-5
ALPHA = 2.0 ** 0.25
ROPE_THETA = 10000.0
SCALE = QK_DIM ** -0.5

GATE_W = 2 * D_MODEL
MID_W = 4 * SGU_W
LAT_W = Q_RANK + KV_RANK + HEAD_PAD
PAD_W = GATE_W + MID_W + LAT_W
LAT_COLS = Q_RANK + KV_RANK + ROPE
ROW_GATE = LAT_COLS + MID_W
LAT_ROWS_PAD = 704
N_SLABS = 4
SLAB_W = D_MODEL // N_SLABS

ROW_TILE = 256
MATMUL_ROW_TILE = 512
ATT_TQ = 512
ATT_TK = 256
ATT_BWD_TQ = 512
ATT_BWD_TK = 256
LOG2E = 1.4426950408889634
LN2 = 0.6931471805599453
Q_SCALE = SCALE * LOG2E
VMEM_LIMIT = 56 * 1024 * 1024

ADAM_LR = 0.001
ADAM_B1 = 0.9
ADAM_B2 = 0.999
ADAM_EPS = 1e-08
ADAM_WD = 0.01
ADAM_STEP = 10


def _dot(a, b):
    return jnp.dot(a, b, preferred_element_type=F32)


def _dot_nt(a, b):
    return lax.dot_general(a, b, (((1,), (1,)), ((), ())), preferred_element_type=F32)


def _dot_tn(a, b):
    return lax.dot_general(a, b, (((0,), (0,)), ((), ())), preferred_element_type=F32)


def _sigmoid(z):
    return 0.5 * jnp.tanh(0.5 * z) + 0.5


_GELU_C = math.sqrt(2.0 / math.pi)


def _gelu_and_grad(x):
    x2 = x * x
    t = jnp.tanh(_GELU_C * (x + 0.044715 * x * x2))
    g = 0.5 * x * (1.0 + t)
    dg = 0.5 * (1.0 + t) + 0.5 * x * (1.0 - t * t) * (_GELU_C * (1.0 + 3.0 * 0.044715 * x2))
    return g, dg


def _silu_and_grad(z):
    s = _sigmoid(z)
    return z * s, s * (1.0 + z * (1.0 - s))


def _rope(xb, c, sl, sh):
    return xb * c + pltpu.roll(xb, 112, 1) * sl + pltpu.roll(xb, 16, 1) * sh


def _rope_t(dy, c, sl, sh):
    return dy * c + pltpu.roll(dy * sl, 16, 1) + pltpu.roll(dy * sh, 112, 1)


def _params(sem=("arbitrary",)):
    return pltpu.CompilerParams(dimension_semantics=sem, vmem_limit_bytes=VMEM_LIMIT)


def _row_spec(tile, width):
    return pl.BlockSpec((tile, width), lambda i: (i, 0))


def _full_spec(shape):
    nd = len(shape)
    return pl.BlockSpec(shape, lambda i: (0,) * nd)


def _kpe_rows(wt_ref):
    z = lambda n: jnp.zeros((n, D_MODEL), BF16)
    return jnp.concatenate([z(NOPE), wt_ref[Q_RANK + KV_RANK:LAT_COLS, :], z(HEAD_PAD - QK_DIM)], axis=0)


def _fwd_rest(xb, wt, b_g, b_m):
    s = xb.shape[0]
    ts = MATMUL_ROW_TILE

    def body(xb_ref, wt_ref, bg_ref, bm_ref, hg_ref, hm_ref):
        xb_ = xb_ref[...]
        hg_ref[...] = _dot_nt(xb_, wt_ref[ROW_GATE:IN_W, :]) + bg_ref[...]
        hm_ref[...] = _dot_nt(xb_, wt_ref[LAT_COLS:ROW_GATE, :]) + bm_ref[...]

    return pl.pallas_call(
        body, name="fwd_rest", grid=(s // ts,),
        in_specs=[_row_spec(ts, D_MODEL), _full_spec(wt.shape), _full_spec(b_g.shape), _full_spec(b_m.shape)],
        out_specs=[_row_spec(ts, GATE_W), _row_spec(ts, MID_W)],
        out_shape=[jax.ShapeDtypeStruct((s, GATE_W), F32), jax.ShapeDtypeStruct((s, MID_W), F32)],
        compiler_params=_params(),
    )(xb, wt, b_g, b_m)


def _fwd_lat(x, wlat, b_l, g_q, wuq, g_kv, wk, wv, rc, rsl, rsh, after):
    s = x.shape[0]
    ts = ROW_TILE

    def body(x_ref, wt_ref, bl_ref, gq_ref, wuq_ref, gkv_ref, wk_ref, wv_ref, rc_ref, rsl_ref,
             rsh_ref, after_ref, hl_ref, q_ref, k_ref, v_ref, xb_ref, qt_ref, kt_ref, vt_ref):
        xb = x_ref[...].astype(BF16)
        xb_ref[...] = xb
        hl = jnp.concatenate([_dot_nt(xb, wt_ref[0:Q_RANK + KV_RANK, :]), _dot_nt(xb, _kpe_rows(wt_ref))],
                             axis=1) + bl_ref[...]
        hl_ref[...] = hl
        c, sl, sh = rc_ref[...], rsl_ref[...], rsh_ref[...]
        cq = hl[:, :Q_RANK]
        cqn = cq * lax.rsqrt(jnp.mean(cq * cq, axis=-1, keepdims=True) + RMS_EPS) * gq_ref[...]
        q = _dot(cqn.astype(BF16), wuq_ref[...])
        ckv = hl[:, Q_RANK:Q_RANK + KV_RANK]
        ckvn = (ckv * lax.rsqrt(jnp.mean(ckv * ckv, axis=-1, keepdims=True) + RMS_EPS) * gkv_ref[...]).astype(BF16)
        k = _dot(ckvn, wk_ref[...])
        vb = _dot(ckvn, wv_ref[...]).astype(BF16)
        v_ref[...] = vb
        vt_ref[...] = vb.T
        kpe = _rope(hl[:, Q_RANK + KV_RANK:], c, sl, sh)
        for hd in range(HEADS):
            lanes = slice(hd * HEAD_PAD, (hd + 1) * HEAD_PAD)
            qb = (_rope(q[:, lanes], c, sl, sh) * Q_SCALE).astype(BF16)
            kb = (k[:, lanes] + kpe).astype(BF16)
            q_ref[:, lanes] = qb
            k_ref[:, lanes] = kb
            qt_ref[lanes, :] = qb.T
            kt_ref[lanes, :] = kb.T

    qk_w = HEADS * HEAD_PAD
    col_spec = lambda rows: pl.BlockSpec((rows, ts), lambda i: (0, i))
    return pl.pallas_call(
        body, name="fwd_lat", grid=(s // ts,),
        in_specs=[_row_spec(ts, D_MODEL), _full_spec(wlat.shape),
                  _full_spec(b_l.shape), _full_spec(g_q.shape),
                  _full_spec(wuq.shape), _full_spec(g_kv.shape), _full_spec(wk.shape), _full_spec(wv.shape),
                  _row_spec(ts, HEAD_PAD), _row_spec(ts, HEAD_PAD), _row_spec(ts, HEAD_PAD),
                  pl.BlockSpec(memory_space=pl.ANY)],
        out_specs=[_row_spec(ts, LAT_W), _row_spec(ts, qk_w),
                   _row_spec(ts, qk_w), _row_spec(ts, MLA_W), _row_spec(ts, D_MODEL), col_spec(qk_w), col_spec(qk_w),
                   col_spec(MLA_W)],
        out_shape=[jax.ShapeDtypeStruct((s, LAT_W), F32), jax.ShapeDtypeStruct((s, qk_w), BF16),
                   jax.ShapeDtypeStruct((s, qk_w), BF16), jax.ShapeDtypeStruct((s, MLA_W), BF16),
                   jax.ShapeDtypeStruct((s, D_MODEL), BF16), jax.ShapeDtypeStruct((qk_w, s), BF16),
                   jax.ShapeDtypeStruct((qk_w, s), BF16), jax.ShapeDtypeStruct((MLA_W, s), BF16)],
        compiler_params=_params(),
    )(x, wlat, b_l, g_q, wuq, g_kv, wk, wv, rc, rsl, rsh, after)


def _attn_fwd(qt, k, vt):
    s = k.shape[0]
    tq, tk = ATT_TQ, ATT_TK
    r = tq // tk
    pairs = HEADS // 2

    def body(qt_ref, k_ref, vt_ref, o_ref, lse_ref):
        i = pl.program_id(1)
        krow = lax.broadcasted_iota(jnp.int32, (tk, tq), 0)
        qcol = lax.broadcasted_iota(jnp.int32, (tk, tq), 1)
        qts = [qt_ref[hh * HEAD_PAD:(hh + 1) * HEAD_PAD, :] for hh in range(2)]

        def scores(j):
            koff = pl.multiple_of(j * tk, tk)
            return tuple(_dot(k_ref[pl.ds(koff, tk), hh * HEAD_PAD:(hh + 1) * HEAD_PAD], qts[hh]) for hh in range(2))

        def weighted(j, ps):
            koff = pl.multiple_of(j * tk, tk)
            return tuple(_dot(vt_ref[hh * V_DIM:(hh + 1) * V_DIM, pl.ds(koff, tk)], ps[hh]) for hh in range(2))

        def step(j, carry, diag, last):
            st, ps, stats = carry
            st_next = None if last else scores(j + 1)
            pvs = weighted(jnp.maximum(j - 1, 0), ps)
            new_ps, new_stats = [], []
            for hh in range(2):
                m, l, acc = stats[hh]
                s_ = st[hh]
                if diag is not None:
                    s_ = jnp.where(krow + diag * tk <= qcol, s_, -jnp.inf)
                m_new = jnp.maximum(m, jnp.max(s_, axis=0, keepdims=True))
                a = jnp.exp2(m - m_new)
                p = jnp.exp2(s_ - m_new)
                new_stats.append((m_new, a * l + jnp.sum(p, axis=0, keepdims=True), a * (acc + pvs[hh])))
                new_ps.append(p.astype(BF16))
            return st_next, tuple(new_ps), tuple(new_stats)

        one = (jnp.full((1, tq), -jnp.inf, F32), jnp.zeros((1, tq), F32), jnp.zeros((V_DIM, tq), F32))
        zero_p = jnp.zeros((tk, tq), BF16)
        nfull = i * r
        carry = lax.fori_loop(0, nfull, functools.partial(step, diag=None, last=False),
                              (scores(0), (zero_p, zero_p), (one, one)))
        for d in range(r):
            carry = step(nfull + d, carry, d, d == r - 1)
        _, ps, stats = carry
        pvs = weighted(nfull + r - 1, ps)
        ot = jnp.concatenate([(stats[hh][2] + pvs[hh]) / stats[hh][1] for hh in range(2)], axis=0)
        o_ref[...] = ot.T
        lse = [stats[hh][0] + jnp.log(stats[hh][1]) * LOG2E for hh in range(2)]
        lse_ref[...] = jnp.concatenate(lse + [jnp.zeros((6, tq), F32)], axis=0)

    return pl.pallas_call(
        body, name="attn_fwd", grid=(pairs, s // tq),
        in_specs=[pl.BlockSpec((2 * HEAD_PAD, tq), lambda p, i: (p, i)),
                  pl.BlockSpec((s, 2 * HEAD_PAD), lambda p, i: (0, p)),
                  pl.BlockSpec((2 * V_DIM, s), lambda p, i: (p, 0))],
        out_specs=[pl.BlockSpec((tq, 2 * V_DIM), lambda p, i: (i, p)),
                   pl.BlockSpec((None, 8, tq), lambda p, i: (p, 0, i))],
        out_shape=[jax.ShapeDtypeStruct((s, MLA_W), F32), jax.ShapeDtypeStruct((pairs, 8, s), F32)],
        compiler_params=_params(("arbitrary", "arbitrary")),
    )(qt, k, vt)


def _attn_bwd(q, qt, k, kt, v, do, dot, lse, delta):
    s = k.shape[0]
    tq, tk = ATT_BWD_TQ, ATT_BWD_TK
    r = tq // tk
    nq = s // tq
    nk = s // tk
    pairs = HEADS // 2

    def body(q_ref, qt_ref, k_ref, kt_ref, v_ref, do_ref, dot_ref, lse_ref, dl_ref, dqt_ref, dk_ref, dv_ref):
        j = pl.program_id(1)
        krow = lax.broadcasted_iota(jnp.int32, (tk, tq), 0)
        qcol = lax.broadcasted_iota(jnp.int32, (tk, tq), 1)
        lane = lax.broadcasted_iota(jnp.int32, (tk, 2 * V_DIM), 1)
        drow = lax.broadcasted_iota(jnp.int32, (2 * V_DIM, tq), 0)

        @pl.when(j == 0)
        def _():
            dqt_ref[...] = jnp.zeros_like(dqt_ref)

        koff = pl.multiple_of(j * tk, tk)
        vb = v_ref[pl.ds(koff, tk), :]
        kbs = [k_ref[pl.ds(koff, tk), hh * HEAD_PAD:(hh + 1) * HEAD_PAD] for hh in range(2)]
        ktbs = [kt_ref[hh * HEAD_PAD:(hh + 1) * HEAD_PAD, pl.ds(koff, tk)] for hh in range(2)]
        i0 = j // r

        def front(i):
            qoff = pl.multiple_of(i * tq, tq)
            dotb = dot_ref[:, pl.ds(qoff, tq)]
            out = []
            for hh in range(2):
                mine = (drow < V_DIM) if hh == 0 else (drow >= V_DIM)
                st = _dot(kbs[hh], qt_ref[hh * HEAD_PAD:(hh + 1) * HEAD_PAD, pl.ds(qoff, tq)])
                out.append((st, _dot(vb, jnp.where(mine, dotb, jnp.zeros_like(dotb)))))
            return tuple(out)

        def middle(i, tiles, diag):
            qoff = pl.multiple_of(i * tq, tq)
            out = []
            for hh in range(2):
                st, dpt = tiles[hh]
                if diag:
                    st = jnp.where(krow + (j - i0 * r) * tk <= qcol, st, -jnp.inf)
                p = jnp.exp2(st - lse_ref[hh:hh + 1, pl.ds(qoff, tq)])
                out.append((p.astype(BF16), (p * (dpt - dl_ref[hh:hh + 1, pl.ds(qoff, tq)])).astype(BF16)))
            return tuple(out)

        def back(i, pd, accs):
            qoff = pl.multiple_of(i * tq, tq)
            dob = do_ref[pl.ds(qoff, tq), :]
            out = []
            for hh in range(2):
                rows = slice(hh * HEAD_PAD, (hh + 1) * HEAD_PAD)
                p, dst = pd[hh]
                dk_acc, dv_acc = accs[hh]
                dv_acc = dv_acc + _dot(p, dob)
                dk_acc = dk_acc + _dot(dst, q_ref[pl.ds(qoff, tq), rows])
                dqt_ref[rows, pl.ds(qoff, tq)] += _dot(ktbs[hh], dst)
                out.append((dk_acc, dv_acc))
            return tuple(out)

        def step(i, accs, diag):
            return back(i, middle(i, front(i), diag), accs)

        zero_acc = (jnp.zeros((tk, HEAD_PAD), F32), jnp.zeros((tk, 2 * V_DIM), F32))
        accs = step(i0, (zero_acc, zero_acc), True)
        accs = lax.fori_loop(i0 + 1, nq, functools.partial(step, diag=False), accs)
        for hh in range(2):
            dk_ref[:, hh * HEAD_PAD:(hh + 1) * HEAD_PAD] = accs[hh][0] * LN2
        dv_ref[...] = jnp.where(lane < V_DIM, accs[0][1], accs[1][1])

        @pl.when(j == nk - 1)
        def _():
            dqt_ref[...] = dqt_ref[...] * SCALE

    pair_rows = lambda w: pl.BlockSpec((s, w), lambda p, j: (0, p))
    pair_cols = lambda w: pl.BlockSpec((w, s), lambda p, j: (p, 0))
    stats = pl.BlockSpec((None, 8, s), lambda p, j: (p, 0, 0))
    return pl.pallas_call(
        body, name="attn_bwd", grid=(pairs, nk),
        in_specs=[pair_rows(2 * HEAD_PAD), pair_cols(2 * HEAD_PAD), pair_rows(2 * HEAD_PAD), pair_cols(2 * HEAD_PAD),
                  pair_rows(2 * V_DIM), pair_rows(2 * V_DIM), pair_cols(2 * V_DIM), stats, stats],
        out_specs=[pair_cols(2 * HEAD_PAD),
                   pl.BlockSpec((tk, 2 * HEAD_PAD), lambda p, j: (j, p)),
                   pl.BlockSpec((tk, 2 * V_DIM), lambda p, j: (j, p))],
        out_shape=[jax.ShapeDtypeStruct((HEADS * HEAD_PAD, s), F32), jax.ShapeDtypeStruct((s, HEADS * HEAD_PAD), F32),
                   jax.ShapeDtypeStruct((s, MLA_W), F32)],
        compiler_params=_params(("arbitrary", "arbitrary")),
    )(q, qt, k, kt, v, do, dot, lse, delta)


def _split3(a):
    hi = a.astype(BF16)
    r1 = a - hi.astype(F32)
    mid = r1.astype(BF16)
    lo = (r1 - mid.astype(F32)).astype(BF16)
    return hi, mid, lo


def _mid(x, tgt, o, hm, hg, woa, wob, wout, ln_g, ln_b, sg_g, sg_b, w_s, bsb):
    s = x.shape[0]
    ts = ROW_TILE
    nsteps = s // ts
    nch = ts // CHUNK
    npair = GROUPS // 2

    def body(x_ref, t_ref, o_ref, hm_ref, hg_ref, woa_ref, wob_ref, wout_ref, lng_ref, lnb_ref, sgg_ref, sgb_ref,
             ws_ref, bsb_ref,
             dr_ref, dhg_ref, dhm_ref, do_ref, dot_ref, dl_ref, dhgt_ref, dhmt_ref,
             dwout_ref, dwoa_ref, dwob_ref, dws_ref, dbs_ref, dlng_ref, dlnb_ref, dsgg_ref, dsgb_ref, loss_ref,
             dbg_ref, dbm_ref, dbacc_ref):
        i = pl.program_id(0)

        @pl.when(i == 0)
        def _():
            for r in (dwout_ref, dwoa_ref, dwob_ref, dws_ref, dlng_ref, dlnb_ref, dsgg_ref, dsgb_ref, loss_ref,
                      dbg_ref, dbm_ref, dbacc_ref):
                r[...] = jnp.zeros_like(r)

        def emit(ref, tref, bref, lo, val):
            vb = val.astype(BF16)
            n = val.shape[1]
            ref[:, lo:lo + n] = vb
            tref[lo:lo + n, :] = vb.T
            bref[:, lo:lo + n] += jnp.sum(val, axis=0, keepdims=True)

        lane = lax.broadcasted_iota(jnp.int32, (CHUNK, CHUNK), 1)
        left = lane < V_DIM
        tril = lax.broadcasted_iota(jnp.int32, (CHUNK, CHUNK), 0) >= lane
        ms = [jnp.where(tril, ws_ref[g], 0.0).astype(BF16) for g in range(GROUPS)]

        z_a = hm_ref[:, 0:SGU_W]
        u = hm_ref[:, SGU_W:2 * SGU_W]
        v = hm_ref[:, 2 * SGU_W:3 * SGU_W]
        z_b = hm_ref[:, 3 * SGU_W:4 * SGU_W]
        o = o_ref[...]
        sa, dsa = _silu_and_grad(z_a)
        y_a = (o * sa).astype(BF16)
        gu, dgu = _gelu_and_grad(u)
        gv, dgv = _gelu_and_grad(v)
        mu = jnp.mean(gv, axis=-1, keepdims=True)
        vc = gv - mu
        rstd_v = lax.rsqrt(jnp.mean(vc * vc, axis=-1, keepdims=True) + LN_EPS)
        vhat = vc * rstd_v
        vn = (vhat * sgg_ref[...] + sgb_ref[...]).astype(BF16)
        rows = []
        for c in range(nch):
            blocks = []
            for p in range(npair):
                blk = vn[c * CHUNK:(c + 1) * CHUNK, p * CHUNK:(p + 1) * CHUNK]
                blocks.append(jnp.where(left, _dot(ms[2 * p], blk), _dot(ms[2 * p + 1], blk)))
            rows.append(jnp.concatenate(blocks, axis=1) + bsb_ref[...])
        mixed = jnp.concatenate(rows, axis=0)
        sgu = gu * mixed
        sb, dsb = _silu_and_grad(z_b)
        y_b = (sgu * sb).astype(BF16)
        pa = jnp.concatenate([_dot(y_a, woa_ref[k]) for k in range(N_SLABS)], axis=1)
        pb = jnp.concatenate([_dot(y_b, wob_ref[k]) for k in range(N_SLABS)], axis=1)
        sga = _sigmoid(hg_ref[:, :D_MODEL])
        sgb = _sigmoid(hg_ref[:, D_MODEL:])
        m2 = (sga * pa + sgb * pb).astype(BF16)
        r = ALPHA * x_ref[...] + _dot(m2, wout_ref[...])
        rmu = jnp.mean(r, axis=-1, keepdims=True)
        rc = r - rmu
        rstd = lax.rsqrt(jnp.mean(rc * rc, axis=-1, keepdims=True) + LN_EPS)
        xhat = rc * rstd
        y = xhat * lng_ref[...] + lnb_ref[...]
        err = y - t_ref[...]
        loss_ref[...] += jnp.full(loss_ref.shape, 0.5 / D_MODEL, F32) * jnp.sum(err * err)

        dy = err * (1.0 / D_MODEL)
        dlng_ref[...] += jnp.sum(dy * xhat, axis=0, keepdims=True)
        dlnb_ref[...] += jnp.sum(dy, axis=0, keepdims=True)
        dxh = dy * lng_ref[...]
        dr = rstd * (dxh - jnp.mean(dxh, axis=-1, keepdims=True) - xhat * jnp.mean(dxh * xhat, axis=-1, keepdims=True))
        dr_ref[...] = dr
        drb = dr.astype(BF16)
        dwout_ref[...] += _dot_tn(m2, drb)
        dm2 = _dot_nt(drb, wout_ref[...])
        emit(dhg_ref, dhgt_ref, dbg_ref, 0, dm2 * pa * sga * (1.0 - sga))
        emit(dhg_ref, dhgt_ref, dbg_ref, D_MODEL, dm2 * pb * sgb * (1.0 - sgb))
        dpa = (dm2 * sga).astype(BF16)
        dpb = (dm2 * sgb).astype(BF16)
        dy_a = jnp.zeros((ts, MLA_W), F32)
        dy_b = jnp.zeros((ts, SGU_W), F32)
        y_at, y_bt = y_a.T, y_b.T
        for k in range(N_SLABS):
            cols = slice(k * SLAB_W, (k + 1) * SLAB_W)
            dwoa_ref[k] += _dot(y_at, dpa[:, cols])
            dwob_ref[k] += _dot(y_bt, dpb[:, cols])
            dy_a = dy_a + _dot_nt(dpa[:, cols], woa_ref[k])
            dy_b = dy_b + _dot_nt(dpb[:, cols], wob_ref[k])
        dob = (dy_a * sa).astype(BF16)
        do_ref[...] = dob
        dot_ref[...] = dob.T
        head = (lax.broadcasted_iota(jnp.int32, (HEADS, MLA_W), 1) // V_DIM
                == lax.broadcasted_iota(jnp.int32, (HEADS, MLA_W), 0)).astype(BF16)
        dl_ref[...] = sum(_dot_nt(head, term) for term in _split3(dob.astype(F32) * o))
        emit(dhm_ref, dhmt_ref, dbm_ref, 0, dy_a * o * dsa)
        dsg = dy_b * sb
        emit(dhm_ref, dhmt_ref, dbm_ref, 3 * SGU_W, dy_b * sgu * dsb)
        emit(dhm_ref, dhmt_ref, dbm_ref, SGU_W, dsg * mixed * dgu)
        dmixed = dsg * gu
        dvn_rows = []
        dbs_sum = jnp.zeros((CHUNK, SGU_W), F32)
        for c in range(nch):
            dm_c = dmixed[c * CHUNK:(c + 1) * CHUNK, :]
            dbs_sum = dbs_sum + dm_c
            blocks = []
            for p in range(npair):
                dmb = dm_c[:, p * CHUNK:(p + 1) * CHUNK].astype(BF16)
                blk = vn[c * CHUNK:(c + 1) * CHUNK, p * CHUNK:(p + 1) * CHUNK]
                blocks.append(jnp.where(left, _dot_tn(ms[2 * p], dmb), _dot_tn(ms[2 * p + 1], dmb)))
                zero = jnp.zeros_like(dmb)
                dws_ref[2 * p] += jnp.where(tril, _dot_nt(jnp.where(left, dmb, zero), blk), 0.0)
                dws_ref[2 * p + 1] += jnp.where(tril, _dot_nt(jnp.where(left, zero, dmb), blk), 0.0)
            dvn_rows.append(jnp.concatenate(blocks, axis=1))
        dbacc_ref[...] += dbs_sum
        dvn = jnp.concatenate(dvn_rows, axis=0)
        dsgg_ref[...] += jnp.sum(dvn * vhat, axis=0, keepdims=True)
        dsgb_ref[...] += jnp.sum(dvn, axis=0, keepdims=True)
        dvh = dvn * sgg_ref[...]
        dgv_in = rstd_v * (dvh - jnp.mean(dvh, axis=-1, keepdims=True)
                           - vhat * jnp.mean(dvh * vhat, axis=-1, keepdims=True))
        emit(dhm_ref, dhmt_ref, dbm_ref, 2 * SGU_W, dgv_in * dgv)

        @pl.when(i == nsteps - 1)
        def _():
            grp = (lax.broadcasted_iota(jnp.int32, (SGU_W, CHUNK), 0) // V_DIM
                   == lax.broadcasted_iota(jnp.int32, (SGU_W, CHUNK), 1)).astype(BF16)
            hi, mid, lo = _split3(dbacc_ref[...])
            dbs_ref[...] = _dot(hi, grp) + _dot(mid, grp) + _dot(lo, grp)

    acc_shapes = [(D_MODEL, D_MODEL), woa.shape, wob.shape, (GROUPS, CHUNK, CHUNK), (CHUNK, CHUNK),
                  (1, D_MODEL), (1, D_MODEL), (1, SGU_W), (1, SGU_W), (1, 128), (1, GATE_W), (1, MID_W)]
    col_spec = lambda rows: pl.BlockSpec((rows, ts), lambda i: (0, i))
    return pl.pallas_call(
        body, name="mid", grid=(nsteps,),
        in_specs=[_row_spec(ts, D_MODEL), _row_spec(ts, D_MODEL), _row_spec(ts, MLA_W), _row_spec(ts, MID_W),
                  _row_spec(ts, GATE_W), _full_spec(woa.shape), _full_spec(wob.shape), _full_spec(wout.shape),
                  _full_spec(ln_g.shape), _full_spec(ln_b.shape), _full_spec(sg_g.shape), _full_spec(sg_b.shape),
                  _full_spec(w_s.shape), _full_spec(bsb.shape)],
        out_specs=[_row_spec(ts, D_MODEL), _row_spec(ts, GATE_W), _row_spec(ts, MID_W), _row_spec(ts, MLA_W),
                   col_spec(MLA_W), col_spec(HEADS), col_spec(GATE_W), col_spec(MID_W)]
        + [_full_spec(sh) for sh in acc_shapes],
        out_shape=[jax.ShapeDtypeStruct((s, D_MODEL), F32), jax.ShapeDtypeStruct((s, GATE_W), BF16),
                   jax.ShapeDtypeStruct((s, MID_W), BF16), jax.ShapeDtypeStruct((s, MLA_W), BF16),
                   jax.ShapeDtypeStruct((MLA_W, s), BF16), jax.ShapeDtypeStruct((HEADS, s), F32),
                   jax.ShapeDtypeStruct((GATE_W, s), BF16), jax.ShapeDtypeStruct((MID_W, s), BF16)]
        + [jax.ShapeDtypeStruct(sh, F32) for sh in acc_shapes],
        scratch_shapes=[pltpu.VMEM((CHUNK, SGU_W), F32)],
        compiler_params=_params(),
    )(x, tgt, o, hm, hg, woa, wob, wout, ln_g, ln_b, sg_g, sg_b, w_s, bsb)


def _lat_bwd(dq, dk, dv, hl, rc, rsl, rsh, g_q, g_kv, wuq, wk, wv, after):
    s = dk.shape[0]
    ts = ROW_TILE
    qk_w = HEADS * HEAD_PAD

    def body(dq_ref, dk_ref, dv_ref, hl_ref, rc_ref, rsl_ref, rsh_ref, gq_ref, gkv_ref, wuq_ref, wk_ref, wv_ref,
             after_ref, dhl_ref, dhlt_ref, dwuq_ref, dwk_ref, dwv_ref, dgq_ref, dgkv_ref, dbl_ref):
        i = pl.program_id(0)

        @pl.when(i == 0)
        def _():
            for r in (dwuq_ref, dwk_ref, dwv_ref, dgq_ref, dgkv_ref, dbl_ref):
                r[...] = jnp.zeros_like(r)

        def emit(lo, val):
            vb = val.astype(BF16)
            n = val.shape[1]
            dhl_ref[:, lo:lo + n] = vb
            dhlt_ref[lo:lo + n, :] = vb.T
            dbl_ref[:, lo:lo + n] += jnp.sum(val, axis=0, keepdims=True)

        c, sl, sh = rc_ref[...], rsl_ref[...], rsh_ref[...]
        lane = lax.broadcasted_iota(jnp.int32, (ts, HEAD_PAD), 1)
        pe = (lane >= NOPE) & (lane < QK_DIM)
        dkpe = jnp.zeros((ts, HEAD_PAD), F32)
        dqu = []
        for hd in range(HEADS):
            lanes = slice(hd * HEAD_PAD, (hd + 1) * HEAD_PAD)
            dqu.append(_rope_t(dq_ref[lanes, :].T, c, sl, sh).astype(BF16))
            dkpe = dkpe + dk_ref[:, lanes]
        dqu = jnp.concatenate(dqu, axis=1)
        dkpe = _rope_t(jnp.where(pe, dkpe, 0.0), c, sl, sh)

        cq = hl_ref[:, :Q_RANK]
        rq = lax.rsqrt(jnp.mean(cq * cq, axis=-1, keepdims=True) + RMS_EPS)
        cqh = cq * rq
        cqn = (cqh * gq_ref[...]).astype(BF16)
        dwuq_ref[...] += _dot_tn(cqn, dqu)
        dcqn = _dot_nt(dqu, wuq_ref[...])
        dgq_ref[...] += jnp.sum(dcqn * cqh, axis=0, keepdims=True)
        dch = dcqn * gq_ref[...]
        emit(0, rq * (dch - cqh * jnp.mean(dch * cqh, axis=-1, keepdims=True)))

        ckv = hl_ref[:, Q_RANK:Q_RANK + KV_RANK]
        rk = lax.rsqrt(jnp.mean(ckv * ckv, axis=-1, keepdims=True) + RMS_EPS)
        ckh = ckv * rk
        ckn = (ckh * gkv_ref[...]).astype(BF16)
        dkb = dk_ref[...].astype(BF16)
        dvb = dv_ref[...].astype(BF16)
        dwk_ref[...] += _dot_tn(ckn, dkb)
        dwv_ref[...] += _dot_tn(ckn, dvb)
        dckn = _dot_nt(dkb, wk_ref[...]) + _dot_nt(dvb, wv_ref[...])
        dgkv_ref[...] += jnp.sum(dckn * ckh, axis=0, keepdims=True)
        dkh = dckn * gkv_ref[...]
        emit(Q_RANK, rk * (dkh - ckh * jnp.mean(dkh * ckh, axis=-1, keepdims=True)))
        emit(Q_RANK + KV_RANK, dkpe)

    acc_shapes = [wuq.shape, wk.shape, wv.shape, g_q.shape, g_kv.shape, (1, LAT_W)]
    return pl.pallas_call(
        body, name="lat_bwd", grid=(s // ts,),
        in_specs=[pl.BlockSpec((qk_w, ts), lambda i: (0, i)), _row_spec(ts, qk_w), _row_spec(ts, MLA_W),
                  _row_spec(ts, LAT_W), _row_spec(ts, HEAD_PAD), _row_spec(ts, HEAD_PAD), _row_spec(ts, HEAD_PAD),
                  _full_spec(g_q.shape), _full_spec(g_kv.shape), _full_spec(wuq.shape), _full_spec(wk.shape),
                  _full_spec(wv.shape), pl.BlockSpec(memory_space=pl.ANY)],
        out_specs=[_row_spec(ts, LAT_W), pl.BlockSpec((LAT_W, ts), lambda i: (0, i))]
        + [_full_spec(sh) for sh in acc_shapes],
        out_shape=[jax.ShapeDtypeStruct((s, LAT_W), BF16), jax.ShapeDtypeStruct((LAT_W, s), BF16)]
        + [jax.ShapeDtypeStruct(sh, F32) for sh in acc_shapes],
        compiler_params=_params(),
    )(dq, dk, dv, hl, rc, rsl, rsh, g_q, g_kv, wuq, wk, wv, after)


def _dx(dr, dhg, dhm, dhl, wt, after):
    s = dr.shape[0]
    ts = MATMUL_ROW_TILE

    def body(dr_ref, dhg_ref, dhm_ref, dhl_ref, wt_ref, after_ref, dx_ref):
        dx_ref[...] = (ALPHA * dr_ref[...]
                       + _dot(dhg_ref[...], wt_ref[ROW_GATE:IN_W, :])
                       + _dot(dhm_ref[...], wt_ref[LAT_COLS:ROW_GATE, :])
                       + _dot(dhl_ref[:, 0:Q_RANK + KV_RANK], wt_ref[0:Q_RANK + KV_RANK, :])
                       + _dot(dhl_ref[:, Q_RANK + KV_RANK:], _kpe_rows(wt_ref)))

    return pl.pallas_call(
        body, name="dx", grid=(s // ts,),
        in_specs=[_row_spec(ts, D_MODEL), _row_spec(ts, GATE_W), _row_spec(ts, MID_W), _row_spec(ts, LAT_W),
                  _full_spec(wt.shape), pl.BlockSpec(memory_space=pl.ANY)],
        out_specs=_row_spec(ts, D_MODEL),
        out_shape=jax.ShapeDtypeStruct((s, D_MODEL), F32),
        compiler_params=_params(),
    )(dr, dhg, dhm, dhl, wt, after)


def _dwt_early(dhmt, dhgt, xb, col, after, name):
    tn = 512
    nm, ng = MID_W // tn, GATE_W // tn
    s = dhmt.shape[1]
    hc = D_MODEL // 2

    def body(col_ref, dm_ref, dg_ref, xb_ref, after_ref, dw_ref):
        i = pl.program_id(0)

        @pl.when(i < nm)
        def _():
            dw_ref[...] = _dot(dm_ref[...], xb_ref[...]).astype(BF16)

        @pl.when(i >= nm)
        def _():
            dw_ref[...] = _dot(dg_ref[...], xb_ref[...]).astype(BF16)

    rows = pl.pallas_call(
        body, name=name,
        grid_spec=pltpu.PrefetchScalarGridSpec(
            num_scalar_prefetch=1, grid=(nm + ng,),
            in_specs=[pl.BlockSpec((tn, s), lambda i, col_ref: (jnp.minimum(i, nm - 1), 0)),
                      pl.BlockSpec((tn, s), lambda i, col_ref: (jnp.maximum(i - nm, 0), 0)),
                      pl.BlockSpec((s, hc), lambda i, col_ref: (0, col_ref[0])),
                      pl.BlockSpec(memory_space=pl.ANY)],
            out_specs=pl.BlockSpec((pl.Element(tn), pl.Element(hc)),
                                   lambda i, col_ref: (pl.multiple_of(LAT_COLS + i * tn, 32), 0))),
        out_shape=jax.ShapeDtypeStruct((IN_W, hc), BF16),
        compiler_params=_params(),
    )(col, dhmt, dhgt, xb, after)

    def zero(buf_ref, out_ref):
        out_ref[...] = jnp.zeros_like(out_ref)

    return pl.pallas_call(
        zero, name=name + "_zero_lat", grid=(1,), in_specs=[pl.BlockSpec(memory_space=pl.ANY)],
        out_specs=pl.BlockSpec((LAT_COLS, hc), lambda i: (0, 0)),
        out_shape=jax.ShapeDtypeStruct((IN_W, hc), BF16), input_output_aliases={0: 0},
    )(rows)


def _dwt_lat(dhlt, xb):
    n, s = dhlt.shape

    def body(dht_ref, xb_ref, dw_ref):
        dw = _dot(dht_ref[...], xb_ref[...]).astype(BF16)
        kpe = Q_RANK + KV_RANK + NOPE
        dw_ref[0:Q_RANK + KV_RANK, :] = dw[0:Q_RANK + KV_RANK]
        dw_ref[Q_RANK + KV_RANK:LAT_COLS, :] = dw[kpe:kpe + ROPE]
        dw_ref[LAT_COLS:, :] = jnp.zeros((LAT_ROWS_PAD - LAT_COLS, D_MODEL), BF16)

    return pl.pallas_call(
        body, name="dwt_lat", in_specs=[VMEM_SPEC, VMEM_SPEC], out_specs=VMEM_SPEC,
        out_shape=jax.ShapeDtypeStruct((LAT_ROWS_PAD, D_MODEL), BF16),
        compiler_params=pltpu.CompilerParams(vmem_limit_bytes=VMEM_LIMIT),
    )(dhlt, xb)


def _split_bias(b):
    z = lambda n: jnp.zeros((n,), b.dtype)
    lat = jnp.concatenate([b[:Q_RANK + KV_RANK], z(NOPE), b[Q_RANK + KV_RANK:LAT_COLS], z(HEAD_PAD - QK_DIM)])
    return b[None, ROW_GATE:], b[None, LAT_COLS:ROW_GATE], lat[None, :]


def _join_bias(g, m, l):
    kpe = Q_RANK + KV_RANK + NOPE
    return jnp.concatenate([l[0, :Q_RANK + KV_RANK], l[0, kpe:kpe + ROPE], m[0], g[0]])


def _rope_tables(positions):
    half = ROPE // 2
    inv_freq = ROPE_THETA ** (-jnp.arange(0, ROPE, 2, dtype=F32) / ROPE)
    ang = positions.astype(F32)[:, None] * inv_freq
    cos, sin = jnp.cos(ang), jnp.sin(ang)
    n = positions.shape[0]
    one, zero = jnp.ones((n, NOPE), F32), jnp.zeros((n, half), F32)
    tail1, tail0 = jnp.ones((n, HEAD_PAD - QK_DIM), F32), jnp.zeros((n, HEAD_PAD - QK_DIM), F32)
    z64 = jnp.zeros((n, NOPE), F32)
    rc = jnp.concatenate([one, cos, cos, tail1], axis=1)
    rsl = jnp.concatenate([z64, -sin, zero, tail0], axis=1)
    rsh = jnp.concatenate([z64, zero, sin, tail0], axis=1)
    return rc, rsl, rsh


def _local_attention(x, positions, wlat, b_in, g_q, w_uq, g_kv, w_ukv, after):
    rc, rsl, rsh = _rope_tables(positions)
    b_g, b_m, b_l = _split_bias(b_in)
    wuq = jnp.pad(w_uq, ((0, 0), (0, 0), (0, HEAD_PAD - QK_DIM))).reshape(Q_RANK, HEADS * HEAD_PAD).astype(BF16)
    wk = jnp.pad(w_ukv[:, :, :NOPE], ((0, 0), (0, 0), (0, HEAD_PAD - NOPE))).reshape(KV_RANK, HEADS * HEAD_PAD).astype(BF16)
    wv = w_ukv[:, :, NOPE:].reshape(KV_RANK, MLA_W).astype(BF16)
    gq2, gkv2 = g_q[None, :], g_kv[None, :]
    hl, q, k, v, xb, qt, kt, vt = _fwd_lat(x, wlat, b_l, gq2, wuq, gkv2, wk, wv, rc, rsl, rsh, after)
    o, lse = _attn_fwd(qt, k, vt)
    return dict(q=q, qt=qt, k=k, kt=kt, v=v, o=o, lse=lse, hl=hl, rc=rc, rsl=rsl, rsh=rsh, gq2=gq2, gkv2=gkv2,
                wuq=wuq, wk=wk, wv=wv, xb=xb, b_g=b_g, b_m=b_m)


def _local_head(st, x, tgt, wt, w_oa, sg_g, sg_b, w_s, b_s, w_ob, w_out, ln_g, ln_b):
    q, qt, k, kt, v, o, lse, hl, xb = (st[n] for n in ("q", "qt", "k", "kt", "v", "o", "lse", "hl", "xb"))
    rc, rsl, rsh, gq2, gkv2, wuq, wk, wv = (st[n] for n in ("rc", "rsl", "rsh", "gq2", "gkv2", "wuq", "wk", "wv"))
    bsb = jnp.repeat(b_s.T, V_DIM, axis=1)
    hg, hm = _fwd_rest(xb, wt, st["b_g"], st["b_m"])
    (dr, dhg, dhm, do, dot, delta, dhgt, dhmt, dwout, dwoa, dwob, dws, dbs, dlng, dlnb, dsgg, dsgb, loss, dbg,
     dbm) = _mid(x, tgt, o, hm, hg, w_oa, w_ob, w_out, ln_g[None, :], ln_b[None, :], sg_g[None, :], sg_b[None, :],
                 w_s, bsb)
    delta = jnp.pad(delta.reshape(HEADS // 2, 2, -1), ((0, 0), (0, 6), (0, 0)))
    early = {
        "w_oa": dwoa, "sgu_ln_g": dsgg[0], "sgu_ln_b": dsgb[0], "w_s": dws, "b_s": dbs[:, :GROUPS].T,
        "w_ob": dwob, "w_out": dwout, "ln_g": dlng[0], "ln_b": dlnb[0],
    }
    state = dict(q=q, qt=qt, k=k, kt=kt, v=v, do=do, dot=dot, lse=lse, delta=delta, hl=hl, rc=rc, rsl=rsl, rsh=rsh,
                 gq2=gq2, gkv2=gkv2, wuq=wuq, wk=wk, wv=wv, dr=dr, dhg=dhg, dhm=dhm, wt=wt, xb=xb, dbg=dbg, dbm=dbm,
                 dhgt=dhgt, dhmt=dhmt)
    return loss, early, state


def _local_attn_bwd(st):
    return _attn_bwd(st["q"], st["qt"], st["k"], st["kt"], st["v"], st["do"], st["dot"], st["lse"], st["delta"])


def _local_tail(st, dq, dk, dv, after):
    dhl, dhlt, dwuq, dwk, dwv, dgq, dgkv, dbl = _lat_bwd(dq, dk, dv, st["hl"], st["rc"], st["rsl"], st["rsh"],
                                                         st["gq2"], st["gkv2"], st["wuq"], st["wk"], st["wv"], after)
    late = {
        "w_lat": _dwt_lat(dhlt, st["xb"]),
        "b_in": _join_bias(st["dbg"], st["dbm"], dbl),
        "g_q": dgq[0],
        "w_uq": dwuq.reshape(Q_RANK, HEADS, HEAD_PAD)[:, :, :QK_DIM],
        "g_kv": dgkv[0],
        "w_ukv": jnp.concatenate([dwk.reshape(KV_RANK, HEADS, HEAD_PAD)[:, :, :NOPE],
                                  dwv.reshape(KV_RANK, HEADS, V_DIM)], axis=2),
    }
    return dhl, late


def _local_step(x, positions, tgt, wt, b_in, g_q, w_uq, g_kv, w_ukv, w_oa, sg_g, sg_b, w_s, b_s, w_ob, w_out, ln_g,
                ln_b):
    st = _local_attention(x, positions, wt[:LAT_COLS], b_in, g_q, w_uq, g_kv, w_ukv, b_in)
    loss, early, st = _local_head(st, x, tgt, wt, w_oa, sg_g, sg_b, w_s, b_s, w_ob, w_out, ln_g, ln_b)
    dq, dk, dv = _local_attn_bwd(st)
    dhl, late = _local_tail(st, dq, dk, dv, dv)
    dx = _dx(st["dr"], st["dhg"], st["dhm"], dhl, st["wt"], dhl)
    grads = {**early, **late}
    halves = [_dwt_early(st["dhmt"], st["dhgt"], st["xb"], jnp.full((1,), h, jnp.int32), dhl, "dwt_half%d" % h)
              for h in range(2)]
    grads["w_in"] = jnp.concatenate([grads.pop("w_lat")[:LAT_COLS], jnp.concatenate(halves, axis=1)[LAT_COLS:]],
                                    axis=0)
    return loss, dx, grads


MESH = pl.DeviceIdType.MESH
N_CHIPS = 4
HBM_SPEC = pl.BlockSpec(memory_space=pl.ANY)
HBM_SPEC_STRICT = pl.BlockSpec(memory_space=pltpu.HBM)
VMEM_SPEC = pl.BlockSpec(memory_space=pltpu.VMEM)

REP_ROWS = 80


def _rows8(a):
    flat = a.reshape(-1)
    n = -(-flat.shape[0] // (8 * D_MODEL)) * 8 * D_MODEL
    return jnp.pad(flat, (0, n - flat.shape[0])).reshape(-1, D_MODEL)


def _place():
    x, y, c = lax.axis_index("x"), lax.axis_index("y"), lax.axis_index("c")
    others = [(1 - x, y), (x, 1 - y), (1 - x, 1 - y)]
    return x, y, c, others


def _gather_weights(shards):
    n = len(shards)

    def body(*refs):
        ins, outs, bufs = refs[:n], refs[n:2 * n], refs[2 * n:3 * n]
        send_sems, recv_sems, local_sems = refs[3 * n:]
        x, y, c, others = _place()
        me = 2 * x + y
        sibling = (x, y, 1 - c)
        for src, buf in zip(ins, bufs):
            buf[...] = src[...].astype(BF16)
        own = [pltpu.make_async_copy(bufs[w], outs[w].at[me], local_sems.at[w]) for w in range(n)]
        for cp in own:
            cp.start()

        def part(w, chip, half):
            hc = shards[w].shape[1] // 2
            return outs[w].at[chip, :, pl.ds(half * hc, hc)]

        def sent(w, j):
            hc = shards[w].shape[1] // 2
            return pltpu.make_async_remote_copy(
                src_ref=bufs[w].at[:, pl.ds(c * hc, hc)], dst_ref=part(w, me, c),
                send_sem=send_sems.at[w * 3 + j], recv_sem=recv_sems.at[w * 3 + j],
                device_id=(*others[j], c), device_id_type=MESH)

        def landed(w, j):
            px, py = others[j]
            return pltpu.make_async_remote_copy(
                src_ref=part(w, 2 * px + py, c), dst_ref=part(w, 2 * px + py, c),
                send_sem=send_sems.at[w * 3 + j], recv_sem=recv_sems.at[w * 3 + j],
                device_id=(px, py, c), device_id_type=MESH)

        def passed(w, j, half):
            px, py = others[j]
            k = n * 3 + w * 3 + j
            return pltpu.make_async_remote_copy(
                src_ref=part(w, 2 * px + py, half), dst_ref=part(w, 2 * px + py, half),
                send_sem=send_sems.at[k], recv_sem=recv_sems.at[k], device_id=sibling, device_id_type=MESH)

        first = [sent(w, j) for w in range(n) for j in range(3)]
        for cp in first:
            cp.start()
        fwd = []
        for w in range(n):
            for j in range(3):
                landed(w, j).wait_recv()
                cp = passed(w, j, c)
                cp.start()
                fwd.append(cp)
        for w in range(n):
            for j in range(3):
                passed(w, j, 1 - c).wait_recv()
        for cp in first + fwd:
            cp.wait_send()
        for cp in own:
            cp.wait()

    return pl.pallas_call(
        body, name="gather_weights",
        in_specs=[VMEM_SPEC] * n, out_specs=[HBM_SPEC] * n,
        out_shape=[jax.ShapeDtypeStruct((N_CHIPS,) + s.shape, BF16) for s in shards],
        scratch_shapes=[pltpu.VMEM(s.shape, BF16) for s in shards]
        + [pltpu.SemaphoreType.DMA((6 * n,)), pltpu.SemaphoreType.DMA((6 * n,)), pltpu.SemaphoreType.DMA((n,))],
        compiler_params=pltpu.CompilerParams(vmem_limit_bytes=VMEM_LIMIT),
    )(*shards)


N_DEV = 8
LOSS_TILE = (8, 128)


def _gather_first(lat, uq):
    hl, hu = lat.shape[1] // 2, uq.shape[1] // 2

    def body(lat_ref, uq_ref, wlat_ref, guq_ref, lat_buf, uq_buf, send_sems, recv_sems, local_sems):
        x, y, c, others = _place()
        me = 2 * x + y
        sibling = (x, y, 1 - c)
        lat_buf[...] = lat_ref[...].astype(BF16)
        uq_buf[...] = uq_ref[...].astype(BF16)

        def copy(src, dst, k, to):
            return pltpu.make_async_remote_copy(src_ref=src, dst_ref=dst, send_sem=send_sems.at[k],
                                                recv_sem=recv_sems.at[k], device_id=to, device_id_type=MESH)

        def uq_part(chip, half):
            return guq_ref.at[chip, :, pl.ds(half * hu, hu)]

        def lat_part(half):
            return wlat_ref.at[:, pl.ds(half * hl, hl)]

        own = pltpu.make_async_copy(uq_buf, guq_ref.at[me], local_sems.at[0])
        own.start()
        first = [copy(uq_buf.at[:, pl.ds(c * hu, hu)], uq_part(me, c), j, (*others[j], c)) for j in range(3)]
        for cp in first:
            cp.start()

        @pl.when(me == 0)
        def _():
            mine = pltpu.make_async_copy(lat_buf, wlat_ref, local_sems.at[1])
            mine.start()
            cps = [copy(lat_buf.at[:, pl.ds(c * hl, hl)], lat_part(c), 6 + j, (*others[j], c)) for j in range(3)]
            for cp in cps:
                cp.start()
            for cp in cps:
                cp.wait_send()
            mine.wait()

        @pl.when(me != 0)
        def _():
            j0 = x + 2 * y - 1
            copy(lat_part(c), lat_part(c), 6 + j0, (0, 0, c)).wait_recv()
            fwd = copy(lat_part(c), lat_part(c), 9, sibling)
            fwd.start()
            copy(lat_part(1 - c), lat_part(1 - c), 9, sibling).wait_recv()
            fwd.wait_send()

        fwd = []
        for j, (px, py) in enumerate(others):
            chip = 2 * px + py
            copy(uq_part(chip, c), uq_part(chip, c), j, (px, py, c)).wait_recv()
            cp = copy(uq_part(chip, c), uq_part(chip, c), 3 + j, sibling)
            cp.start()
            fwd.append(cp)
        for j, (px, py) in enumerate(others):
            chip = 2 * px + py
            copy(uq_part(chip, 1 - c), uq_part(chip, 1 - c), 3 + j, sibling).wait_recv()
        for cp in first + fwd:
            cp.wait_send()
        own.wait()

    return pl.pallas_call(
        body, name="gather_first", in_specs=[VMEM_SPEC, VMEM_SPEC], out_specs=[HBM_SPEC, HBM_SPEC],
        out_shape=[jax.ShapeDtypeStruct(lat.shape, BF16), jax.ShapeDtypeStruct((N_CHIPS,) + uq.shape, BF16)],
        scratch_shapes=[pltpu.VMEM(lat.shape, BF16), pltpu.VMEM(uq.shape, BF16), pltpu.SemaphoreType.DMA((10,)),
                        pltpu.SemaphoreType.DMA((10,)), pltpu.SemaphoreType.DMA((2,))],
        compiler_params=pltpu.CompilerParams(vmem_limit_bytes=VMEM_LIMIT),
    )(lat, uq)


def _cast_own(shards, me):
    n = len(shards)

    def body(me_ref, *refs):
        for w in range(n):
            refs[n + w][...] = refs[w][...].astype(BF16)

    return pl.pallas_call(
        body, name="cast_own",
        grid_spec=pltpu.PrefetchScalarGridSpec(
            num_scalar_prefetch=1, grid=(1,),
            in_specs=[pl.BlockSpec(s.shape, lambda i, me_ref: (0, 0)) for s in shards],
            out_specs=[pl.BlockSpec((None,) + s.shape, lambda i, me_ref: (me_ref[0], 0, 0)) for s in shards]),
        out_shape=[jax.ShapeDtypeStruct((N_CHIPS,) + s.shape, BF16) for s in shards],
        compiler_params=pltpu.CompilerParams(vmem_limit_bytes=VMEM_LIMIT),
    )(me, *shards)


def _gather_start(bufs, after):
    n = len(bufs)

    def body(*refs):
        b_refs = refs[:n]
        send_sems, recv_sems, token = refs[n + 1], refs[n + 2], refs[-1]
        x, y, c, others = _place()
        me = 2 * x + y
        for w in range(n):
            hc = _half(bufs[w])
            mine = b_refs[w].at[me, :, pl.ds(c * hc, hc)]
            for j, (px, py) in enumerate(others):
                pltpu.make_async_remote_copy(
                    src_ref=mine, dst_ref=mine, send_sem=send_sems.at[3 * w + j], recv_sem=recv_sems.at[3 * w + j],
                    device_id=(px, py, c), device_id_type=MESH).start()
        token[...] = jnp.zeros_like(token)

    hbm = [pltpu.HBM(b.shape, BF16) for b in bufs]
    outs = pl.pallas_call(
        body, name="gather_start",
        out_shape=(pltpu.SemaphoreType.DMA((3 * n,)), pltpu.SemaphoreType.DMA((3 * n,)), *hbm,
                   jax.ShapeDtypeStruct(LOSS_TILE, F32)),
        in_specs=[HBM_SPEC_STRICT] * n + [HBM_SPEC],
        out_specs=(SEM_SPEC, SEM_SPEC, *[HBM_SPEC_STRICT] * n, VMEM_SPEC),
        input_output_aliases={i: 2 + i for i in range(n)},
        compiler_params=pltpu.CompilerParams(has_side_effects=SPLIT_EFFECT),
    )(*[pltpu.with_memory_space_constraint(b, pltpu.HBM) for b in bufs], after)
    return outs[0], outs[1], list(outs[2:2 + n]), outs[-1]


def _gather_wait(send_sems, recv_sems, bufs, after):
    n = len(bufs)

    def body(*refs):
        b_refs = refs[:n]
        send_sems, recv_sems = refs[n], refs[n + 1]
        x, y, c, others = _place()
        me = 2 * x + y
        for w in range(n):
            hc = _half(bufs[w])
            for j, (px, py) in enumerate(others):
                cp = pltpu.make_async_remote_copy(
                    src_ref=b_refs[w].at[me, :, pl.ds(c * hc, hc)],
                    dst_ref=b_refs[w].at[2 * px + py, :, pl.ds(c * hc, hc)],
                    send_sem=send_sems.at[3 * w + j], recv_sem=recv_sems.at[3 * w + j], device_id=(px, py, c),
                    device_id_type=MESH)
                cp.wait_send()
                cp.wait_recv()

    outs = pl.pallas_call(
        body, name="gather_wait", out_shape=tuple(pltpu.HBM(b.shape, b.dtype) for b in bufs),
        in_specs=[HBM_SPEC_STRICT] * n + [SEM_SPEC, SEM_SPEC, HBM_SPEC],
        out_specs=tuple([HBM_SPEC_STRICT] * n), input_output_aliases={i: i for i in range(n)},
        compiler_params=pltpu.CompilerParams(has_side_effects=SPLIT_EFFECT),
    )(*bufs, send_sems, recv_sems, after)
    return list(outs)


def _gather_finish(bufs):
    n = len(bufs)

    def body(*refs):
        b_refs = refs[n:2 * n]
        send_sems, recv_sems = refs[2 * n:]
        x, y, c, others = _place()
        cps = []
        for w in range(n):
            hc = _half(bufs[w])
            for j, (px, py) in enumerate(others):
                part = b_refs[w].at[2 * px + py, :, pl.ds(c * hc, hc)]
                cps.append(pltpu.make_async_remote_copy(
                    src_ref=part, dst_ref=part, send_sem=send_sems.at[3 * w + j], recv_sem=recv_sems.at[3 * w + j],
                    device_id=(x, y, 1 - c), device_id_type=MESH))
        for cp in cps:
            cp.start()
        for w in range(n):
            hc = _half(bufs[w])
            for j, (px, py) in enumerate(others):
                theirs = b_refs[w].at[2 * px + py, :, pl.ds((1 - c) * hc, hc)]
                pltpu.make_async_remote_copy(
                    src_ref=theirs, dst_ref=theirs, send_sem=send_sems.at[3 * w + j], recv_sem=recv_sems.at[3 * w + j],
                    device_id=(x, y, 1 - c), device_id_type=MESH).wait_recv()
        for cp in cps:
            cp.wait_send()

    return pl.pallas_call(
        body, name="gather_finish", in_specs=[HBM_SPEC] * n, out_specs=[HBM_SPEC] * n,
        out_shape=[jax.ShapeDtypeStruct(b.shape, b.dtype) for b in bufs],
        input_output_aliases={i: i for i in range(n)},
        scratch_shapes=[pltpu.SemaphoreType.DMA((3 * n,)), pltpu.SemaphoreType.DMA((3 * n,))],
    )(*bufs)


def _half(a):
    return a.shape[-1] // 2


def _exchange_pairs(parts, name):
    n = len(parts)

    def body(*refs):
        p_refs, r_refs = refs[:n], refs[n:2 * n]
        send_sems, recv_sems = refs[2 * n:]
        x, y, c, _ = _place()
        cps = []
        for w in range(n):
            h = _half(parts[w])
            cps.append(pltpu.make_async_remote_copy(
                src_ref=p_refs[w].at[:, :, pl.ds((1 - c) * h, h)], dst_ref=r_refs[w],
                send_sem=send_sems.at[w], recv_sem=recv_sems.at[w], device_id=(x, y, 1 - c), device_id_type=MESH))
        for cp in cps:
            cp.start()
        for cp in cps:
            cp.wait()

    return pl.pallas_call(
        body, name=name, in_specs=[HBM_SPEC] * n, out_specs=[HBM_SPEC] * n,
        out_shape=[jax.ShapeDtypeStruct((N_CHIPS, p.shape[1], _half(p)), BF16) for p in parts],
        scratch_shapes=[pltpu.SemaphoreType.DMA((n,)), pltpu.SemaphoreType.DMA((n,))],
    )(*parts)


def _sibling_part(ref, w, n_whole, shape, c):
    if w < n_whole:
        return ref
    h = shape[-1] // 2
    return ref.at[:, :, pl.ds((1 - c) * h, h)]


def _pairs_start(parts, all_loss, n_whole):
    n = len(parts)

    def body(*refs):
        p_refs, r_refs, loss_ref = refs[:n], refs[n:2 * n], refs[2 * n]
        send_sems, recv_sems, token = refs[2 * n + 1], refs[2 * n + 2], refs[-1]
        x, y, c, _ = _place()
        for w in range(n):
            h = _half(parts[w])
            pltpu.make_async_remote_copy(
                src_ref=_sibling_part(p_refs[w], w, n_whole, parts[w].shape, c), dst_ref=r_refs[w],
                send_sem=send_sems.at[w], recv_sem=recv_sems.at[w], device_id=(x, y, 1 - c),
                device_id_type=MESH).start()
        me = 4 * x + 2 * y + c
        for t in range(1, N_DEV):
            d = (me + t) % N_DEV
            pltpu.make_async_remote_copy(
                src_ref=loss_ref.at[me], dst_ref=loss_ref.at[me], send_sem=send_sems.at[n + t - 1],
                recv_sem=recv_sems.at[n + t - 1], device_id=(d // 4, (d // 2) % 2, d % 2), device_id_type=MESH).start()
        token[...] = jnp.zeros_like(token)

    lands = [pltpu.HBM(p.shape if w < n_whole else (N_CHIPS, p.shape[1], _half(p)), BF16)
             for w, p in enumerate(parts)]
    nsem = n + N_DEV - 1
    outs = pl.pallas_call(
        body, name="pairs_start",
        out_shape=(pltpu.SemaphoreType.DMA((nsem,)), pltpu.SemaphoreType.DMA((nsem,)),
                   *[pltpu.HBM(p.shape, p.dtype) for p in parts], *lands, pltpu.HBM(all_loss.shape, F32),
                   jax.ShapeDtypeStruct(LOSS_TILE, F32)),
        in_specs=[HBM_SPEC_STRICT] * (2 * n + 1),
        out_specs=(SEM_SPEC, SEM_SPEC, *[HBM_SPEC_STRICT] * (2 * n + 1), VMEM_SPEC),
        input_output_aliases={i: 2 + i for i in range(2 * n + 1)},
        compiler_params=pltpu.CompilerParams(has_side_effects=SPLIT_EFFECT),
    )(*[pltpu.with_memory_space_constraint(p, pltpu.HBM) for p in parts],
      *[pltpu.with_memory_space_constraint(lax.empty(l.shape, BF16), pltpu.HBM) for l in lands],
      pltpu.with_memory_space_constraint(all_loss, pltpu.HBM))
    return outs[0], outs[1], list(outs[2:2 + n]), list(outs[2 + n:2 + 2 * n]), outs[2 + 2 * n], outs[-1]


def _pairs_wait(send_sems, recv_sems, parts, lands, all_loss, after, n_whole):
    n = len(parts)

    def body(*refs):
        p_refs, r_refs, loss_ref = refs[:n], refs[n:2 * n], refs[2 * n]
        send_sems, recv_sems = refs[2 * n + 1], refs[2 * n + 2]
        x, y, c, _ = _place()
        for w in range(n):
            h = _half(parts[w])
            cp = pltpu.make_async_remote_copy(
                src_ref=_sibling_part(p_refs[w], w, n_whole, parts[w].shape, c), dst_ref=r_refs[w],
                send_sem=send_sems.at[w],
                recv_sem=recv_sems.at[w], device_id=(x, y, 1 - c), device_id_type=MESH)
            cp.wait_send()
            cp.wait_recv()
        me = 4 * x + 2 * y + c
        for t in range(1, N_DEV):
            d = (me + N_DEV - t) % N_DEV
            cp = pltpu.make_async_remote_copy(
                src_ref=loss_ref.at[me], dst_ref=loss_ref.at[d], send_sem=send_sems.at[n + t - 1],
                recv_sem=recv_sems.at[n + t - 1], device_id=(d // 4, (d // 2) % 2, d % 2), device_id_type=MESH)
            cp.wait_send()
            cp.wait_recv()

    bufs = (*parts, *lands, all_loss)
    outs = pl.pallas_call(
        body, name="pairs_wait", out_shape=tuple(pltpu.HBM(a.shape, a.dtype) for a in bufs),
        in_specs=[HBM_SPEC_STRICT] * len(bufs) + [SEM_SPEC, SEM_SPEC, HBM_SPEC],
        out_specs=tuple([HBM_SPEC_STRICT] * len(bufs)), input_output_aliases={i: i for i in range(len(bufs))},
        compiler_params=pltpu.CompilerParams(has_side_effects=SPLIT_EFFECT),
    )(*bufs, send_sems, recv_sems, after)
    return list(outs[:n]), list(outs[n:2 * n]), outs[2 * n]


def _add_pair_tiled(p, r):
    rows, h = r.shape[1:]

    def body(p_ref, r_ref, q_ref):
        q_ref[...] = (p_ref[...].astype(F32) + r_ref[...].astype(F32)).astype(BF16)

    spec = pl.BlockSpec((None, rows, h), lambda k: (k, 0, 0))
    return pl.pallas_call(
        body, name="add_pair_w_in", grid=(N_CHIPS,), in_specs=[spec, spec], out_specs=spec,
        out_shape=jax.ShapeDtypeStruct(r.shape, BF16),
    )(p, r)


def _add_pair_small(ps, rs, c, name):
    n = len(ps)

    def body(c_ref, *refs):
        for w in range(n):
            h = _half(ps[w])
            mine = refs[w][:, :, pl.ds(pl.multiple_of(c_ref[0] * h, 128), h)]
            refs[2 * n + w][...] = (mine.astype(F32) + refs[n + w][...].astype(F32)).astype(BF16)

    return pl.pallas_call(
        body, name=name,
        in_specs=[pl.BlockSpec(memory_space=pltpu.SMEM)] + [VMEM_SPEC] * (2 * n), out_specs=[VMEM_SPEC] * n,
        out_shape=[jax.ShapeDtypeStruct(r.shape, BF16) for r in rs],
        compiler_params=pltpu.CompilerParams(vmem_limit_bytes=VMEM_LIMIT),
    )(c, *ps, *rs)


def _exchange_chips(qs):
    n = len(qs)

    def body(*refs):
        q_refs, r_refs = refs[:n], refs[n:2 * n]
        send_sems, recv_sems = refs[2 * n:]
        x, y, c, others = _place()
        me = 2 * x + y
        cps = []
        for w in range(n):
            for j, (px, py) in enumerate(others):
                cps.append(pltpu.make_async_remote_copy(
                    src_ref=q_refs[w].at[2 * px + py], dst_ref=r_refs[w].at[me], send_sem=send_sems.at[3 * w + j],
                    recv_sem=recv_sems.at[3 * w + j], device_id=(px, py, c), device_id_type=MESH))
        for cp in cps:
            cp.start()
        for w in range(n):
            for j, (px, py) in enumerate(others):
                pltpu.make_async_remote_copy(
                    src_ref=q_refs[w].at[me], dst_ref=r_refs[w].at[2 * px + py], send_sem=send_sems.at[3 * w + j],
                    recv_sem=recv_sems.at[3 * w + j], device_id=(px, py, c), device_id_type=MESH).wait_recv()
        for cp in cps:
            cp.wait_send()

    return pl.pallas_call(
        body, name="exchange_chips", in_specs=[HBM_SPEC] * n, out_specs=[HBM_SPEC] * n,
        out_shape=[jax.ShapeDtypeStruct(q.shape, BF16) for q in qs],
        scratch_shapes=[pltpu.SemaphoreType.DMA((3 * n,)), pltpu.SemaphoreType.DMA((3 * n,))],
    )(*qs)


SEM_SPEC = pl.BlockSpec(memory_space=pltpu.SEMAPHORE)
SPLIT_EFFECT = pltpu.SideEffectType.DATAFLOW_SIDE_EFFECTING


def _chips_start(qs, name):
    n = len(qs)

    def body(*refs):
        q_refs, land_refs = refs[:n], refs[n:2 * n]
        send_sems, recv_sems, token = refs[2 * n], refs[2 * n + 1], refs[-1]
        x, y, c, others = _place()
        me = 2 * x + y
        for w in range(n):
            for j, (px, py) in enumerate(others):
                pltpu.make_async_remote_copy(
                    src_ref=q_refs[w].at[2 * px + py], dst_ref=land_refs[w].at[me], send_sem=send_sems.at[3 * w + j],
                    recv_sem=recv_sems.at[3 * w + j], device_id=(px, py, c), device_id_type=MESH).start()
        token[...] = jnp.zeros_like(token)

    hbm = [pltpu.HBM(q.shape, BF16) for q in qs]
    outs = pl.pallas_call(
        body, name=name,
        out_shape=(pltpu.SemaphoreType.DMA((3 * n,)), pltpu.SemaphoreType.DMA((3 * n,)), *hbm, *hbm,
                   jax.ShapeDtypeStruct(LOSS_TILE, F32)),
        in_specs=[HBM_SPEC_STRICT] * (2 * n),
        out_specs=(SEM_SPEC, SEM_SPEC, *[HBM_SPEC_STRICT] * (2 * n), VMEM_SPEC),
        input_output_aliases={i: 2 + i for i in range(2 * n)},
        compiler_params=pltpu.CompilerParams(has_side_effects=SPLIT_EFFECT),
    )(*[pltpu.with_memory_space_constraint(q, pltpu.HBM) for q in qs],
      *[pltpu.with_memory_space_constraint(lax.empty(q.shape, BF16), pltpu.HBM) for q in qs])
    return outs[0], outs[1], outs[2:2 + n], outs[2 + n:2 + 2 * n], outs[-1]


def _chips_wait(send_sems, recv_sems, q_thru, land_thru, after, name):
    n = len(q_thru)

    def body(*refs):
        q_refs, land_refs = refs[:n], refs[n:2 * n]
        send_sems, recv_sems = refs[2 * n], refs[2 * n + 1]
        x, y, c, others = _place()
        me = 2 * x + y
        for w in range(n):
            for j, (px, py) in enumerate(others):
                cp = pltpu.make_async_remote_copy(
                    src_ref=q_refs[w].at[2 * px + py], dst_ref=land_refs[w].at[2 * px + py],
                    send_sem=send_sems.at[3 * w + j], recv_sem=recv_sems.at[3 * w + j], device_id=(px, py, c),
                    device_id_type=MESH)
                cp.wait_send()
                cp.wait_recv()

    outs = pl.pallas_call(
        body, name=name, out_shape=tuple(pltpu.HBM(a.shape, a.dtype) for a in (*q_thru, *land_thru)),
        in_specs=[HBM_SPEC_STRICT] * (2 * n) + [SEM_SPEC, SEM_SPEC, HBM_SPEC],
        out_specs=tuple([HBM_SPEC_STRICT] * (2 * n)), input_output_aliases={i: i for i in range(2 * n)},
        compiler_params=pltpu.CompilerParams(has_side_effects=SPLIT_EFFECT),
    )(*q_thru, *land_thru, send_sems, recv_sems, after)
    return list(outs[:n]), list(outs[n:])


def _sum_chips_tiled(q, r, idx, tile):
    rows, h = r.shape[1:]
    nt = h // tile

    def body(idx_ref, q_ref, r0_ref, r1_ref, r2_ref, g_ref):
        g_ref[...] = (q_ref[...].astype(F32) + r0_ref[...].astype(F32) + r1_ref[...].astype(F32)
                      + r2_ref[...].astype(F32))

    def slab(t):
        return pl.BlockSpec((None, rows, tile), lambda i, idx_ref: (idx_ref[t], 0, i))

    return pl.pallas_call(
        body, name="sum_chips_w_in",
        grid_spec=pltpu.PrefetchScalarGridSpec(
            num_scalar_prefetch=1, grid=(nt,), in_specs=[slab(0), slab(1), slab(2), slab(3)],
            out_specs=pl.BlockSpec((rows, tile), lambda i, idx_ref: (0, idx_ref[4] * nt + i))),
        out_shape=jax.ShapeDtypeStruct((rows, 2 * h), F32),
    )(idx, q, r, r, r)


def _sum_chips_small(qs, rs, idx, all_dtypes):
    n = len(rs)
    n_all = len(all_dtypes)

    def body(idx_ref, *refs):
        c = idx_ref[4]
        for w in range(n):
            q_ref, r_ref, g_ref = refs[w], refs[n + w], refs[2 * n + w]
            acc = q_ref[idx_ref[0]].astype(F32)
            for t in range(1, N_CHIPS):
                acc = acc + r_ref[idx_ref[t]].astype(F32)
            h = rs[w].shape[2]
            mine = pl.ds(pl.multiple_of(c * h, 128), h)
            g_ref[...] = jnp.zeros_like(g_ref)
            if w >= n - n_all:
                g_ref[idx_ref[0], :, mine] = acc.astype(g_ref.dtype)
            else:
                g_ref[:, mine] = acc

    shapes = [jax.ShapeDtypeStruct((r.shape[1], 2 * r.shape[2]), F32) for r in rs[:n - n_all]]
    shapes += [jax.ShapeDtypeStruct((N_CHIPS, r.shape[1], 2 * r.shape[2]), dt)
               for r, dt in zip(rs[n - n_all:], all_dtypes)]
    return pl.pallas_call(
        body, name="sum_chips_small",
        in_specs=[pl.BlockSpec(memory_space=pltpu.SMEM)] + [VMEM_SPEC] * (2 * n), out_specs=[VMEM_SPEC] * n,
        out_shape=shapes, compiler_params=pltpu.CompilerParams(vmem_limit_bytes=VMEM_LIMIT),
    )(idx, *qs, *rs)


def _share(shards, alls):
    n, na = len(shards), len(alls)
    total = n + na

    def body(*refs):
        g_refs, a_refs = refs[total:total + n], refs[total + n:2 * total]
        send_sems, recv_sems = refs[2 * total:]
        x, y, c, others = _place()
        me = 2 * x + y
        sibling = (x, y, 1 - c)

        def cols_of(w, half):
            h = shards[w].shape[1] // 2
            return g_refs[w].at[:, pl.ds(half * h, h)]

        def slab(a, chip, half):
            h = alls[a].shape[2] // 2
            return a_refs[a].at[chip, :, pl.ds(half * h, h)]

        def copy(src, dst, k, to):
            return pltpu.make_async_remote_copy(src_ref=src, dst_ref=dst, send_sem=send_sems.at[k],
                                                recv_sem=recv_sems.at[k], device_id=to, device_id_type=MESH)

        cps = [copy(cols_of(w, c), cols_of(w, c), w, sibling) for w in range(n)]
        for a in range(na):
            base = n + 7 * a
            cps.append(copy(slab(a, me, c), slab(a, me, c), base, sibling))
            for j, (px, py) in enumerate(others):
                cps.append(copy(slab(a, me, c), slab(a, me, c), base + 1 + j, (px, py, c)))
        for cp in cps:
            cp.start()
        fwd = []
        for a in range(na):
            base = n + 7 * a
            for j, (px, py) in enumerate(others):
                chip = 2 * px + py
                copy(slab(a, me, c), slab(a, chip, c), base + 1 + j, (px, py, c)).wait_recv()
                cp = copy(slab(a, chip, c), slab(a, chip, c), base + 4 + j, sibling)
                cp.start()
                fwd.append(cp)
        for a in range(na):
            base = n + 7 * a
            for j, (px, py) in enumerate(others):
                chip = 2 * px + py
                copy(slab(a, chip, c), slab(a, chip, 1 - c), base + 4 + j, sibling).wait_recv()
            copy(slab(a, me, c), slab(a, me, 1 - c), base, sibling).wait_recv()
        for w in range(n):
            copy(cols_of(w, c), cols_of(w, 1 - c), w, sibling).wait_recv()
        for cp in cps + fwd:
            cp.wait_send()

    nsem = n + 7 * na
    return pl.pallas_call(
        body, name="share", in_specs=[HBM_SPEC] * total, out_specs=[HBM_SPEC] * total,
        out_shape=[jax.ShapeDtypeStruct(a.shape, a.dtype) for a in (*shards, *alls)],
        input_output_aliases={i: i for i in range(total)},
        scratch_shapes=[pltpu.SemaphoreType.DMA((nsem,)), pltpu.SemaphoreType.DMA((nsem,))],
    )(*shards, *alls)


def _adamw(w, g, m, v):
    m2 = ADAM_B1 * m + (1.0 - ADAM_B1) * g
    v2 = ADAM_B2 * v + (1.0 - ADAM_B2) * (g * g)
    m_hat = m2 / (1.0 - ADAM_B1 ** ADAM_STEP)
    v_hat = v2 / (1.0 - ADAM_B2 ** ADAM_STEP)
    return -ADAM_LR * (m_hat / (jnp.sqrt(v_hat) + ADAM_EPS) + ADAM_WD * w), m2, v2


def _update_w_in(wt, gt, mt, vt, lat, owner, tile):
    nlat = lat.shape[0] // tile

    def body(owner_ref, w_ref, g_ref, m_ref, v_ref, lat_ref, g2_ref, d_ref, m2_ref, v2_ref):
        row = pl.program_id(0) * tile + lax.broadcasted_iota(jnp.int32, (tile, 1), 0)
        g = jnp.where((row < LAT_COLS) & (owner_ref[0] == 1), lat_ref[...].astype(F32), g_ref[...])
        g2_ref[...] = g
        d_ref[...], m2_ref[...], v2_ref[...] = _adamw(w_ref[...], g, m_ref[...], v_ref[...])

    spec = pl.BlockSpec((tile, wt.shape[1]), lambda i, o: (i, 0))
    return pl.pallas_call(
        body, name="update_w_in",
        grid_spec=pltpu.PrefetchScalarGridSpec(
            num_scalar_prefetch=1, grid=(wt.shape[0] // tile,),
            in_specs=[spec] * 4 + [pl.BlockSpec((tile, wt.shape[1]), lambda i, o: (jnp.minimum(i, nlat - 1), 0))],
            out_specs=[spec] * 4),
        out_shape=[jax.ShapeDtypeStruct(wt.shape, F32)] * 4,
        compiler_params=_params(("parallel",)),
    )(owner, wt, gt, mt, vt, lat)


def _update_small(ws, gs, ms, vs):
    n = len(ws)

    def body(*refs):
        for k in range(n):
            w_ref, g_ref, m_ref, v_ref = refs[k], refs[n + k], refs[2 * n + k], refs[3 * n + k]
            d, m2, v2 = _adamw(w_ref[...], g_ref[...], m_ref[...], v_ref[...])
            refs[4 * n + k][...] = d
            refs[5 * n + k][...] = m2
            refs[6 * n + k][...] = v2

    shapes = [jax.ShapeDtypeStruct(w.shape, F32) for w in ws]
    outs = pl.pallas_call(
        body, name="update_small", in_specs=[VMEM_SPEC] * (4 * n), out_specs=[VMEM_SPEC] * (3 * n),
        out_shape=shapes * 3,
        compiler_params=pltpu.CompilerParams(vmem_limit_bytes=VMEM_LIMIT),
    )(*ws, *gs, *ms, *vs)
    return outs[:n], outs[n:2 * n], outs[2 * n:]


SHARDED = ("w_in", "w_uq", "w_oa", "w_ob", "w_out")
REPLICATED = ("b_in", "g_q", "g_kv", "w_ukv", "sgu_ln_g", "sgu_ln_b", "w_s", "b_s", "ln_g", "ln_b")
ORDER = ("w_in", "b_in", "g_q", "w_uq", "g_kv", "w_ukv", "w_oa", "sgu_ln_g", "sgu_ln_b", "w_s", "b_s", "w_ob", "w_out",
         "ln_g", "ln_b")


def kernel(x, positions, w_in, b_in, g_q, w_uq, g_kv, w_ukv, w_oa, sgu_ln_g, sgu_ln_b, w_s, b_s, w_ob, w_out, ln_g, ln_b, loss_target, m_w_in, m_b_in, m_g_q, m_w_uq, m_g_kv, m_w_ukv, m_w_oa, m_sgu_ln_g, m_sgu_ln_b, m_w_s, m_b_s, m_w_ob, m_w_out, m_ln_g, m_ln_b, v_w_in, v_b_in, v_g_q, v_w_uq, v_g_kv, v_w_ukv, v_w_oa, v_sgu_ln_g, v_sgu_ln_b, v_w_s, v_b_s, v_w_ob, v_w_out, v_ln_g, v_ln_b):
    w = dict(w_in=w_in, b_in=b_in, g_q=g_q, w_uq=w_uq, g_kv=g_kv, w_ukv=w_ukv, w_oa=w_oa, sgu_ln_g=sgu_ln_g,
             sgu_ln_b=sgu_ln_b, w_s=w_s, b_s=b_s, w_ob=w_ob, w_out=w_out, ln_g=ln_g, ln_b=ln_b)
    m = dict(w_in=m_w_in, b_in=m_b_in, g_q=m_g_q, w_uq=m_w_uq, g_kv=m_g_kv, w_ukv=m_w_ukv, w_oa=m_w_oa,
             sgu_ln_g=m_sgu_ln_g, sgu_ln_b=m_sgu_ln_b, w_s=m_w_s, b_s=m_b_s, w_ob=m_w_ob, w_out=m_w_out, ln_g=m_ln_g,
             ln_b=m_ln_b)
    v = dict(w_in=v_w_in, b_in=v_b_in, g_q=v_g_q, w_uq=v_w_uq, g_kv=v_g_kv, w_ukv=v_w_ukv, w_oa=v_w_oa,
             sgu_ln_g=v_sgu_ln_g, sgu_ln_b=v_sgu_ln_b, w_s=v_w_s, b_s=v_b_s, w_ob=v_w_ob, w_out=v_w_out, ln_g=v_ln_g,
             ln_b=v_ln_b)
    w, m, v = ({n: a[0] for n, a in d.items()} for d in (w, m, v))
    c = lax.axis_index("c")

    wt_shard, mt_shard, vt_shard = (jnp.transpose(d["w_in"]) for d in (w, m, v))
    xi, yi = lax.axis_index("x"), lax.axis_index("y")
    me1 = (2 * xi + yi).reshape(1).astype(jnp.int32)
    g_lat, g_uq = _gather_first(wt_shard[:LAT_COLS], w["w_uq"].reshape(Q_RANK // 4, HEADS * QK_DIM))
    bufs = _cast_own([wt_shard, w["w_oa"], w["w_ob"], w["w_out"]], me1)
    send0, recv0, bufs, token0 = _gather_start(bufs, g_lat)
    st = _local_attention(x[0], positions[0], g_lat, w["b_in"], w["g_q"], g_uq.reshape(Q_RANK, HEADS, QK_DIM),
                          w["g_kv"], w["w_ukv"], token0)
    g_in, g_oa, g_ob, g_out = _gather_finish(_gather_wait(send0, recv0, bufs, st["o"]))
    wt = g_in.reshape(IN_W, D_MODEL)

    loss, early, st = _local_head(
        st, x[0], loss_target[0], wt, g_oa, w["sgu_ln_g"], w["sgu_ln_b"], w["w_s"], w["b_s"], g_ob,
        g_out.reshape(D_MODEL, D_MODEL), w["ln_g"], w["ln_b"])

    c1 = c.reshape(1).astype(jnp.int32)
    idx = jnp.stack([2 * xi + yi, 2 * (1 - xi) + yi, 2 * xi + (1 - yi), 2 * (1 - xi) + (1 - yi), c]).astype(jnp.int32)
    slabs = lambda a: a.reshape(N_CHIPS, IN_W // N_CHIPS, D_MODEL // 2)
    theirs = slabs(_dwt_early(st["dhmt"], st["dhgt"], st["xb"], 1 - c1, c1, "dwt_theirs"))
    parts1 = [theirs, early["w_oa"].astype(BF16), early["w_ob"].astype(BF16),
              early["w_out"].reshape(N_CHIPS, SLAB_W, D_MODEL).astype(BF16)]
    my_loss = lax.dynamic_update_slice(jnp.zeros((N_DEV,) + LOSS_TILE, F32), jnp.broadcast_to(loss, (1,) + LOSS_TILE),
                                       (4 * xi + 2 * yi + c, 0, 0))
    sems0 = _pairs_start(parts1, my_loss, 1)
    mine = slabs(_dwt_early(st["dhmt"], st["dhgt"], st["xb"], c1, sems0[5], "dwt_mine"))
    parts1, recv1, all_loss = _pairs_wait(*sems0[:5], mine, 1)
    pairs1 = [_add_pair_tiled(mine, recv1[0]), *_add_pair_small(parts1[1:], recv1[1:], c1, "add_pair_early")]
    sems1 = _chips_start(pairs1, "chips_start_early")
    st["delta"] = st["delta"] + sems1[4][0, 0]
    dq, dk, dv = _local_attn_bwd(st)
    dhl, late = _local_tail(st, dq, dk, dv, dv)

    grads = {**early, **late}
    rep = jnp.concatenate([_rows8(grads[n]) for n in REPLICATED], axis=0)
    rep = jnp.pad(rep, ((0, N_CHIPS * REP_ROWS - rep.shape[0]), (0, 0))).reshape(N_CHIPS, REP_ROWS, D_MODEL)
    parts2 = [late["w_uq"].reshape(N_CHIPS, Q_RANK // N_CHIPS, HEADS * QK_DIM).astype(BF16), rep.astype(BF16),
              late["w_lat"].reshape(N_CHIPS, LAT_ROWS_PAD // N_CHIPS, D_MODEL)]
    pairs2 = _add_pair_small(parts2, _exchange_pairs(parts2, "exchange_pairs_late"), c1, "add_pair_late")
    sems2 = _chips_start(pairs2, "chips_start_late")
    dx = _dx(st["dr"], st["dhg"], st["dhm"], dhl, st["wt"], sems2[4])
    pairs2, landed2 = _chips_wait(*sems2[:4], dx, "chips_wait_late")
    pairs1, landed1 = _chips_wait(*sems1[:4], landed2[0], "chips_wait_early")
    sums = [_sum_chips_tiled(pairs1[0], landed1[0], idx, 128),
            *_sum_chips_small([*pairs1[1:], *pairs2], [*landed1[1:], *landed2], idx, (F32, BF16))]
    *shards, g_rep, g_lat = _share(sums[:-2], sums[-2:])
    loss = jnp.sum(all_loss[:, 0, 0])

    red = {n: s.reshape(w[n].shape) for n, s in zip(("w_oa", "w_ob", "w_out", "w_uq"), shards[1:])}
    g_rep = g_rep.reshape(N_CHIPS * REP_ROWS, D_MODEL)
    off = 0
    for n in REPLICATED:
        rows = _rows8(w[n]).shape[0]
        red[n] = g_rep[off:off + rows].reshape(-1)[:w[n].size].reshape(w[n].shape)
        off += rows
    owner = (2 * xi + yi == 0).astype(jnp.int32).reshape(1)
    gt, dt, mt, vt2 = _update_w_in(wt_shard, shards[0], mt_shard, vt_shard,
                                   g_lat.reshape(LAT_ROWS_PAD, D_MODEL).astype(F32), owner, 232)
    red["w_in"] = jnp.transpose(gt)
    small = [n for n in ORDER if n != "w_in"]
    as2d = lambda a: a.reshape(-1, a.shape[-1])
    ds, ms, vs = _update_small([as2d(w[n]) for n in small], [as2d(red[n]) for n in small],
                               [as2d(m[n]) for n in small], [as2d(v[n]) for n in small])
    delta, new_m, new_v = {"w_in": jnp.transpose(dt)}, {"w_in": jnp.transpose(mt)}, {"w_in": jnp.transpose(vt2)}
    for i, n in enumerate(small):
        delta[n], new_m[n], new_v[n] = (a[i].reshape(w[n].shape) for a in (ds, ms, vs))

    lead = lambda a: a[None]
    return (loss, dx[None], *[lead(red[n]) for n in ORDER], *[lead(delta[n]) for n in ORDER],
            *[lead(new_m[n]) for n in ORDER], *[lead(new_v[n]) for n in ORDER])
```

```python
import functools
import math

import jax
import jax.numpy as jnp
from jax import lax
from jax.experimental import pallas as pl
from jax.experimental.pallas import tpu as pltpu

F32 = jnp.float32
BF16 = jnp.bfloat16

D_MODEL = 1024
HEADS = 8
Q_RANK = 384
KV_RANK = 128
NOPE = 64
ROPE = 32
V_DIM = 64
QK_DIM = NOPE + ROPE
HEAD_PAD = 128
MLA_W = HEADS * V_DIM
SGU_W = 512
GROUPS = 8
CHUNK = 128
IN_W = 4640
RMS_EPS = 1e-6
LN_EPS = 1e-5
ALPHA = 2.0 ** 0.25
ROPE_THETA = 10000.0
SCALE = QK_DIM ** -0.5

GATE_W = 2 * D_MODEL
MID_W = 4 * SGU_W
LAT_W = Q_RANK + KV_RANK + HEAD_PAD
PAD_W = GATE_W + MID_W + LAT_W
LAT_COLS = Q_RANK + KV_RANK + ROPE
ROW_GATE = LAT_COLS + MID_W
LAT_ROWS_PAD = 704
N_SLABS = 4
SLAB_W = D_MODEL // N_SLABS

ROW_TILE = 256
MATMUL_ROW_TILE = 512
ATT_TQ = 1024
ATT_TK = 256
ATT_BWD_TQ = 512
ATT_BWD_TK = 256
LOG2E = 1.4426950408889634
LN2 = 0.6931471805599453
Q_SCALE = SCALE * LOG2E
VMEM_LIMIT = 56 * 1024 * 1024

ADAM_LR = 0.001
ADAM_B1 = 0.9
ADAM_B2 = 0.999
ADAM_EPS = 1e-08
ADAM_WD = 0.01
ADAM_STEP = 10


def _dot(a, b):
    return jnp.dot(a, b, preferred_element_type=F32)


def _dot_nt(a, b):
    return lax.dot_general(a, b, (((1,), (1,)), ((), ())), preferred_element_type=F32)


def _dot_tn(a, b):
    return lax.dot_general(a, b, (((0,), (0,)), ((), ())), preferred_element_type=F32)


def _sigmoid(z):
    return 0.5 * jnp.tanh(0.5 * z) + 0.5


_GELU_C = math.sqrt(2.0 / math.pi)


def _gelu_and_grad(x):
    x2 = x * x
    t = jnp.tanh(_GELU_C * (x + 0.044715 * x * x2))
    g = 0.5 * x * (1.0 + t)
    dg = 0.5 * (1.0 + t) + 0.5 * x * (1.0 - t * t) * (_GELU_C * (1.0 + 3.0 * 0.044715 * x2))
    return g, dg


def _silu_and_grad(z):
    s = _sigmoid(z)
    return z * s, s * (1.0 + z * (1.0 - s))


def _rope(xb, c, sl, sh):
    return xb * c + pltpu.roll(xb, 112, 1) * sl + pltpu.roll(xb, 16, 1) * sh


def _rope_t(dy, c, sl, sh):
    return dy * c + pltpu.roll(dy * sl, 16, 1) + pltpu.roll(dy * sh, 112, 1)


def _params(sem=("arbitrary",)):
    return pltpu.CompilerParams(dimension_semantics=sem, vmem_limit_bytes=VMEM_LIMIT)


def _row_spec(tile, width):
    return pl.BlockSpec((tile, width), lambda i: (i, 0))


def _full_spec(shape):
    nd = len(shape)
    return pl.BlockSpec(shape, lambda i: (0,) * nd)


def _kpe_rows(wt_ref):
    z = lambda n: jnp.zeros((n, D_MODEL), BF16)
    return jnp.concatenate([z(NOPE), wt_ref[Q_RANK + KV_RANK:LAT_COLS, :], z(HEAD_PAD - QK_DIM)], axis=0)


def _fwd_rest(xb, wt, b_g, b_m):
    s = xb.shape[0]
    ts = MATMUL_ROW_TILE

    def body(xb_ref, wt_ref, bg_ref, bm_ref, hg_ref, hm_ref):
        xb_ = xb_ref[...]
        hg_ref[...] = _dot_nt(xb_, wt_ref[ROW_GATE:IN_W, :]) + bg_ref[...]
        hm_ref[...] = _dot_nt(xb_, wt_ref[LAT_COLS:ROW_GATE, :]) + bm_ref[...]

    return pl.pallas_call(
        body, name="fwd_rest", grid=(s // ts,),
        in_specs=[_row_spec(ts, D_MODEL), _full_spec(wt.shape), _full_spec(b_g.shape), _full_spec(b_m.shape)],
        out_specs=[_row_spec(ts, GATE_W), _row_spec(ts, MID_W)],
        out_shape=[jax.ShapeDtypeStruct((s, GATE_W), F32), jax.ShapeDtypeStruct((s, MID_W), F32)],
        compiler_params=_params(),
    )(xb, wt, b_g, b_m)


def _fwd_lat(x, wlat, b_l, g_q, wuq, g_kv, wk, wv, rc, rsl, rsh, after):
    s = x.shape[0]
    ts = ROW_TILE

    def body(x_ref, wt_ref, bl_ref, gq_ref, wuq_ref, gkv_ref, wk_ref, wv_ref, rc_ref, rsl_ref,
             rsh_ref, after_ref, hl_ref, q_ref, k_ref, v_ref, xb_ref, qt_ref, kt_ref, vt_ref):
        xb = x_ref[...].astype(BF16)
        xb_ref[...] = xb
        hl = jnp.concatenate([_dot_nt(xb, wt_ref[0:Q_RANK + KV_RANK, :]), _dot_nt(xb, _kpe_rows(wt_ref))],
                             axis=1) + bl_ref[...]
        hl_ref[...] = hl
        c, sl, sh = rc_ref[...], rsl_ref[...], rsh_ref[...]
        cq = hl[:, :Q_RANK]
        cqn = cq * lax.rsqrt(jnp.mean(cq * cq, axis=-1, keepdims=True) + RMS_EPS) * gq_ref[...]
        q = _dot(cqn.astype(BF16), wuq_ref[...])
        ckv = hl[:, Q_RANK:Q_RANK + KV_RANK]
        ckvn = (ckv * lax.rsqrt(jnp.mean(ckv * ckv, axis=-1, keepdims=True) + RMS_EPS) * gkv_ref[...]).astype(BF16)
        k = _dot(ckvn, wk_ref[...])
        vb = _dot(ckvn, wv_ref[...]).astype(BF16)
        v_ref[...] = vb
        vt_ref[...] = vb.T
        kpe = _rope(hl[:, Q_RANK + KV_RANK:], c, sl, sh)
        for hd in range(HEADS):
            lanes = slice(hd * HEAD_PAD, (hd + 1) * HEAD_PAD)
            qb = (_rope(q[:, lanes], c, sl, sh) * Q_SCALE).astype(BF16)
            kb = (k[:, lanes] + kpe).astype(BF16)
            q_ref[:, lanes] = qb
            k_ref[:, lanes] = kb
            qt_ref[lanes, :] = qb.T
            kt_ref[lanes, :] = kb.T

    qk_w = HEADS * HEAD_PAD
    col_spec = lambda rows: pl.BlockSpec((rows, ts), lambda i: (0, i))
    return pl.pallas_call(
        body, name="fwd_lat", grid=(s // ts,),
        in_specs=[_row_spec(ts, D_MODEL), _full_spec(wlat.shape),
                  _full_spec(b_l.shape), _full_spec(g_q.shape),
                  _full_spec(wuq.shape), _full_spec(g_kv.shape), _full_spec(wk.shape), _full_spec(wv.shape),
                  _row_spec(ts, HEAD_PAD), _row_spec(ts, HEAD_PAD), _row_spec(ts, HEAD_PAD),
                  pl.BlockSpec(memory_space=pl.ANY)],
        out_specs=[_row_spec(ts, LAT_W), _row_spec(ts, qk_w),
                   _row_spec(ts, qk_w), _row_spec(ts, MLA_W), _row_spec(ts, D_MODEL), col_spec(qk_w), col_spec(qk_w),
                   col_spec(MLA_W)],
        out_shape=[jax.ShapeDtypeStruct((s, LAT_W), F32), jax.ShapeDtypeStruct((s, qk_w), BF16),
                   jax.ShapeDtypeStruct((s, qk_w), BF16), jax.ShapeDtypeStruct((s, MLA_W), BF16),
                   jax.ShapeDtypeStruct((s, D_MODEL), BF16), jax.ShapeDtypeStruct((qk_w, s), BF16),
                   jax.ShapeDtypeStruct((qk_w, s), BF16), jax.ShapeDtypeStruct((MLA_W, s), BF16)],
        compiler_params=_params(),
    )(x, wlat, b_l, g_q, wuq, g_kv, wk, wv, rc, rsl, rsh, after)


def _attn_fwd(qt, k, vt):
    s = k.shape[0]
    tq, tk = ATT_TQ, ATT_TK
    r = tq // tk
    pairs = HEADS // 2

    def body(qt_ref, k_ref, vt_ref, o_ref, lse_ref):
        i = pl.program_id(1)
        krow = lax.broadcasted_iota(jnp.int32, (tk, tq), 0)
        qcol = lax.broadcasted_iota(jnp.int32, (tk, tq), 1)
        qts = [qt_ref[hh * HEAD_PAD:(hh + 1) * HEAD_PAD, :] for hh in range(2)]

        def scores(j):
            koff = pl.multiple_of(j * tk, tk)
            return tuple(_dot(k_ref[pl.ds(koff, tk), hh * HEAD_PAD:(hh + 1) * HEAD_PAD], qts[hh]) for hh in range(2))

        def weighted(j, ps):
            koff = pl.multiple_of(j * tk, tk)
            return tuple(_dot(vt_ref[hh * V_DIM:(hh + 1) * V_DIM, pl.ds(koff, tk)], ps[hh]) for hh in range(2))

        def step(j, carry, diag, last):
            st, ps, stats = carry
            st_next = None if last else scores(j + 1)
            pvs = weighted(jnp.maximum(j - 1, 0), ps)
            new_ps, new_stats = [], []
            for hh in range(2):
                m, l, acc = stats[hh]
                s_ = st[hh]
                if diag is not None:
                    s_ = jnp.where(krow + diag * tk <= qcol, s_, -jnp.inf)
                m_new = jnp.maximum(m, jnp.max(s_, axis=0, keepdims=True))
                a = jnp.exp2(m - m_new)
                p = jnp.exp2(s_ - m_new)
                new_stats.append((m_new, a * l + jnp.sum(p, axis=0, keepdims=True), a * (acc + pvs[hh])))
                new_ps.append(p.astype(BF16))
            return st_next, tuple(new_ps), tuple(new_stats)

        one = (jnp.full((1, tq), -jnp.inf, F32), jnp.zeros((1, tq), F32), jnp.zeros((V_DIM, tq), F32))
        zero_p = jnp.zeros((tk, tq), BF16)
        nfull = i * r
        carry = lax.fori_loop(0, nfull, functools.partial(step, diag=None, last=False),
                              (scores(0), (zero_p, zero_p), (one, one)))
        for d in range(r):
            carry = step(nfull + d, carry, d, d == r - 1)
        _, ps, stats = carry
        pvs = weighted(nfull + r - 1, ps)
        ot = jnp.concatenate([(stats[hh][2] + pvs[hh]) / stats[hh][1] for hh in range(2)], axis=0)
        o_ref[...] = ot.T
        lse = [stats[hh][0] + jnp.log(stats[hh][1]) * LOG2E for hh in range(2)]
        lse_ref[...] = jnp.concatenate(lse + [jnp.zeros((6, tq), F32)], axis=0)

    return pl.pallas_call(
        body, name="attn_fwd", grid=(pairs, s // tq),
        in_specs=[pl.BlockSpec((2 * HEAD_PAD, tq), lambda p, i: (p, i)),
                  pl.BlockSpec((s, 2 * HEAD_PAD), lambda p, i: (0, p)),
                  pl.BlockSpec((2 * V_DIM, s), lambda p, i: (p, 0))],
        out_specs=[pl.BlockSpec((tq, 2 * V_DIM), lambda p, i: (i, p)),
                   pl.BlockSpec((None, 8, tq), lambda p, i: (p, 0, i))],
        out_shape=[jax.ShapeDtypeStruct((s, MLA_W), F32), jax.ShapeDtypeStruct((pairs, 8, s), F32)],
        compiler_params=_params(("arbitrary", "arbitrary")),
    )(qt, k, vt)


def _attn_bwd(q, qt, k, kt, v, do, dot, lse, delta):
    s = k.shape[0]
    tq, tk = ATT_BWD_TQ, ATT_BWD_TK
    r = tq // tk
    nq = s // tq
    nk = s // tk
    pairs = HEADS // 2

    def body(q_ref, qt_ref, k_ref, kt_ref, v_ref, do_ref, dot_ref, lse_ref, dl_ref, dqt_ref, dk_ref, dv_ref):
        j = pl.program_id(1)
        krow = lax.broadcasted_iota(jnp.int32, (tk, tq), 0)
        qcol = lax.broadcasted_iota(jnp.int32, (tk, tq), 1)
        lane = lax.broadcasted_iota(jnp.int32, (tk, 2 * V_DIM), 1)
        drow = lax.broadcasted_iota(jnp.int32, (2 * V_DIM, tq), 0)

        @pl.when(j == 0)
        def _():
            dqt_ref[...] = jnp.zeros_like(dqt_ref)

        koff = pl.multiple_of(j * tk, tk)
        vb = v_ref[pl.ds(koff, tk), :]
        kbs = [k_ref[pl.ds(koff, tk), hh * HEAD_PAD:(hh + 1) * HEAD_PAD] for hh in range(2)]
        ktbs = [kt_ref[hh * HEAD_PAD:(hh + 1) * HEAD_PAD, pl.ds(koff, tk)] for hh in range(2)]
        i0 = j // r

        def front(i):
            qoff = pl.multiple_of(i * tq, tq)
            dotb = dot_ref[:, pl.ds(qoff, tq)]
            out = []
            for hh in range(2):
                mine = (drow < V_DIM) if hh == 0 else (drow >= V_DIM)
                st = _dot(kbs[hh], qt_ref[hh * HEAD_PAD:(hh + 1) * HEAD_PAD, pl.ds(qoff, tq)])
                out.append((st, _dot(vb, jnp.where(mine, dotb, jnp.zeros_like(dotb)))))
            return tuple(out)

        def middle(i, tiles, diag):
            qoff = pl.multiple_of(i * tq, tq)
            out = []
            for hh in range(2):
                st, dpt = tiles[hh]
                if diag:
                    st = jnp.where(krow + (j - i0 * r) * tk <= qcol, st, -jnp.inf)
                p = jnp.exp2(st - lse_ref[hh:hh + 1, pl.ds(qoff, tq)])
                out.append((p.astype(BF16), (p * (dpt - dl_ref[hh:hh + 1, pl.ds(qoff, tq)])).astype(BF16)))
            return tuple(out)

        def back(i, pd, accs):
            qoff = pl.multiple_of(i * tq, tq)
            dob = do_ref[pl.ds(qoff, tq), :]
            out = []
            for hh in range(2):
                rows = slice(hh * HEAD_PAD, (hh + 1) * HEAD_PAD)
                p, dst = pd[hh]
                dk_acc, dv_acc = accs[hh]
                dv_acc = dv_acc + _dot(p, dob)
                dk_acc = dk_acc + _dot(dst, q_ref[pl.ds(qoff, tq), rows])
                dqt_ref[rows, pl.ds(qoff, tq)] += _dot(ktbs[hh], dst)
                out.append((dk_acc, dv_acc))
            return tuple(out)

        def step(i, accs, diag):
            return back(i, middle(i, front(i), diag), accs)

        zero_acc = (jnp.zeros((tk, HEAD_PAD), F32), jnp.zeros((tk, 2 * V_DIM), F32))
        accs = step(i0, (zero_acc, zero_acc), True)
        accs = lax.fori_loop(i0 + 1, nq, functools.partial(step, diag=False), accs)
        for hh in range(2):
            dk_ref[:, hh * HEAD_PAD:(hh + 1) * HEAD_PAD] = accs[hh][0] * LN2
        dv_ref[...] = jnp.where(lane < V_DIM, accs[0][1], accs[1][1])

        @pl.when(j == nk - 1)
        def _():
            dqt_ref[...] = dqt_ref[...] * SCALE

    pair_rows = lambda w: pl.BlockSpec((s, w), lambda p, j: (0, p))
    pair_cols = lambda w: pl.BlockSpec((w, s), lambda p, j: (p, 0))
    stats = pl.BlockSpec((None, 8, s), lambda p, j: (p, 0, 0))
    return pl.pallas_call(
        body, name="attn_bwd", grid=(pairs, nk),
        in_specs=[pair_rows(2 * HEAD_PAD), pair_cols(2 * HEAD_PAD), pair_rows(2 * HEAD_PAD), pair_cols(2 * HEAD_PAD),
                  pair_rows(2 * V_DIM), pair_rows(2 * V_DIM), pair_cols(2 * V_DIM), stats, stats],
        out_specs=[pair_cols(2 * HEAD_PAD),
                   pl.BlockSpec((tk, 2 * HEAD_PAD), lambda p, j: (j, p)),
                   pl.BlockSpec((tk, 2 * V_DIM), lambda p, j: (j, p))],
        out_shape=[jax.ShapeDtypeStruct((HEADS * HEAD_PAD, s), F32), jax.ShapeDtypeStruct((s, HEADS * HEAD_PAD), F32),
                   jax.ShapeDtypeStruct((s, MLA_W), F32)],
        compiler_params=_params(("arbitrary", "arbitrary")),
    )(q, qt, k, kt, v, do, dot, lse, delta)


def _split3(a):
    hi = a.astype(BF16)
    r1 = a - hi.astype(F32)
    mid = r1.astype(BF16)
    lo = (r1 - mid.astype(F32)).astype(BF16)
    return hi, mid, lo


def _mid(x, tgt, o, hm, hg, woa, wob, wout, ln_g, ln_b, sg_g, sg_b, w_s, bsb):
    s = x.shape[0]
    ts = ROW_TILE
    nsteps = s // ts
    nch = ts // CHUNK
    npair = GROUPS // 2

    def body(x_ref, t_ref, o_ref, hm_ref, hg_ref, woa_ref, wob_ref, wout_ref, lng_ref, lnb_ref, sgg_ref, sgb_ref,
             ws_ref, bsb_ref,
             dr_ref, dhg_ref, dhm_ref, do_ref, dot_ref, dl_ref, dhgt_ref, dhmt_ref,
             dwout_ref, dwoa_ref, dwob_ref, dws_ref, dbs_ref, dlng_ref, dlnb_ref, dsgg_ref, dsgb_ref, loss_ref,
             dbg_ref, dbm_ref, dbacc_ref):
        i = pl.program_id(0)

        @pl.when(i == 0)
        def _():
            for r in (dwout_ref, dwoa_ref, dwob_ref, dws_ref, dlng_ref, dlnb_ref, dsgg_ref, dsgb_ref, loss_ref,
                      dbg_ref, dbm_ref, dbacc_ref):
                r[...] = jnp.zeros_like(r)

        def emit(ref, tref, bref, lo, val):
            vb = val.astype(BF16)
            n = val.shape[1]
            ref[:, lo:lo + n] = vb
            tref[lo:lo + n, :] = vb.T
            bref[:, lo:lo + n] += jnp.sum(val, axis=0, keepdims=True)

        lane = lax.broadcasted_iota(jnp.int32, (CHUNK, CHUNK), 1)
        left = lane < V_DIM
        tril = lax.broadcasted_iota(jnp.int32, (CHUNK, CHUNK), 0) >= lane
        ms = [jnp.where(tril, ws_ref[g], 0.0).astype(BF16) for g in range(GROUPS)]

        z_a = hm_ref[:, 0:SGU_W]
        u = hm_ref[:, SGU_W:2 * SGU_W]
        v = hm_ref[:, 2 * SGU_W:3 * SGU_W]
        z_b = hm_ref[:, 3 * SGU_W:4 * SGU_W]
        o = o_ref[...]
        sa, dsa = _silu_and_grad(z_a)
        y_a = (o * sa).astype(BF16)
        gu, dgu = _gelu_and_grad(u)
        gv, dgv = _gelu_and_grad(v)
        mu = jnp.mean(gv, axis=-1, keepdims=True)
        vc = gv - mu
        rstd_v = lax.rsqrt(jnp.mean(vc * vc, axis=-1, keepdims=True) + LN_EPS)
        vhat = vc * rstd_v
        vn = (vhat * sgg_ref[...] + sgb_ref[...]).astype(BF16)
        rows = []
        for c in range(nch):
            blocks = []
            for p in range(npair):
                blk = vn[c * CHUNK:(c + 1) * CHUNK, p * CHUNK:(p + 1) * CHUNK]
                blocks.append(jnp.where(left, _dot(ms[2 * p], blk), _dot(ms[2 * p + 1], blk)))
            rows.append(jnp.concatenate(blocks, axis=1) + bsb_ref[...])
        mixed = jnp.concatenate(rows, axis=0)
        sgu = gu * mixed
        sb, dsb = _silu_and_grad(z_b)
        y_b = (sgu * sb).astype(BF16)
        pa = jnp.concatenate([_dot(y_a, woa_ref[k]) for k in range(N_SLABS)], axis=1)
        pb = jnp.concatenate([_dot(y_b, wob_ref[k]) for k in range(N_SLABS)], axis=1)
        sga = _sigmoid(hg_ref[:, :D_MODEL])
        sgb = _sigmoid(hg_ref[:, D_MODEL:])
        m2 = (sga * pa + sgb * pb).astype(BF16)
        r = ALPHA * x_ref[...] + _dot(m2, wout_ref[...])
        rmu = jnp.mean(r, axis=-1, keepdims=True)
        rc = r - rmu
        rstd = lax.rsqrt(jnp.mean(rc * rc, axis=-1, keepdims=True) + LN_EPS)
        xhat = rc * rstd
        y = xhat * lng_ref[...] + lnb_ref[...]
        err = y - t_ref[...]
        loss_ref[...] += jnp.full(loss_ref.shape, 0.5 / D_MODEL, F32) * jnp.sum(err * err)

        dy = err * (1.0 / D_MODEL)
        dlng_ref[...] += jnp.sum(dy * xhat, axis=0, keepdims=True)
        dlnb_ref[...] += jnp.sum(dy, axis=0, keepdims=True)
        dxh = dy * lng_ref[...]
        dr = rstd * (dxh - jnp.mean(dxh, axis=-1, keepdims=True) - xhat * jnp.mean(dxh * xhat, axis=-1, keepdims=True))
        dr_ref[...] = dr
        drb = dr.astype(BF16)
        dwout_ref[...] += _dot_tn(m2, drb)
        dm2 = _dot_nt(drb, wout_ref[...])
        emit(dhg_ref, dhgt_ref, dbg_ref, 0, dm2 * pa * sga * (1.0 - sga))
        emit(dhg_ref, dhgt_ref, dbg_ref, D_MODEL, dm2 * pb * sgb * (1.0 - sgb))
        dpa = (dm2 * sga).astype(BF16)
        dpb = (dm2 * sgb).astype(BF16)
        dy_a = jnp.zeros((ts, MLA_W), F32)
        dy_b = jnp.zeros((ts, SGU_W), F32)
        y_at, y_bt = y_a.T, y_b.T
        for k in range(N_SLABS):
            cols = slice(k * SLAB_W, (k + 1) * SLAB_W)
            dwoa_ref[k] += _dot(y_at, dpa[:, cols])
            dwob_ref[k] += _dot(y_bt, dpb[:, cols])
            dy_a = dy_a + _dot_nt(dpa[:, cols], woa_ref[k])
            dy_b = dy_b + _dot_nt(dpb[:, cols], wob_ref[k])
        dob = (dy_a * sa).astype(BF16)
        do_ref[...] = dob
        dot_ref[...] = dob.T
        head = (lax.broadcasted_iota(jnp.int32, (HEADS, MLA_W), 1) // V_DIM
                == lax.broadcasted_iota(jnp.int32, (HEADS, MLA_W), 0)).astype(BF16)
        dl_ref[...] = sum(_dot_nt(head, term) for term in _split3(dob.astype(F32) * o))
        emit(dhm_ref, dhmt_ref, dbm_ref, 0, dy_a * o * dsa)
        dsg = dy_b * sb
        emit(dhm_ref, dhmt_ref, dbm_ref, 3 * SGU_W, dy_b * sgu * dsb)
        emit(dhm_ref, dhmt_ref, dbm_ref, SGU_W, dsg * mixed * dgu)
        dmixed = dsg * gu
        dvn_rows = []
        dbs_sum = jnp.zeros((CHUNK, SGU_W), F32)
        for c in range(nch):
            dm_c = dmixed[c * CHUNK:(c + 1) * CHUNK, :]
            dbs_sum = dbs_sum + dm_c
            blocks = []
            for p in range(npair):
                dmb = dm_c[:, p * CHUNK:(p + 1) * CHUNK].astype(BF16)
                blk = vn[c * CHUNK:(c + 1) * CHUNK, p * CHUNK:(p + 1) * CHUNK]
                blocks.append(jnp.where(left, _dot_tn(ms[2 * p], dmb), _dot_tn(ms[2 * p + 1], dmb)))
                zero = jnp.zeros_like(dmb)
                dws_ref[2 * p] += jnp.where(tril, _dot_nt(jnp.where(left, dmb, zero), blk), 0.0)
                dws_ref[2 * p + 1] += jnp.where(tril, _dot_nt(jnp.where(left, zero, dmb), blk), 0.0)
            dvn_rows.append(jnp.concatenate(blocks, axis=1))
        dbacc_ref[...] += dbs_sum
        dvn = jnp.concatenate(dvn_rows, axis=0)
        dsgg_ref[...] += jnp.sum(dvn * vhat, axis=0, keepdims=True)
        dsgb_ref[...] += jnp.sum(dvn, axis=0, keepdims=True)
        dvh = dvn * sgg_ref[...]
        dgv_in = rstd_v * (dvh - jnp.mean(dvh, axis=-1, keepdims=True)
                           - vhat * jnp.mean(dvh * vhat, axis=-1, keepdims=True))
        emit(dhm_ref, dhmt_ref, dbm_ref, 2 * SGU_W, dgv_in * dgv)

        @pl.when(i == nsteps - 1)
        def _():
            grp = (lax.broadcasted_iota(jnp.int32, (SGU_W, CHUNK), 0) // V_DIM
                   == lax.broadcasted_iota(jnp.int32, (SGU_W, CHUNK), 1)).astype(BF16)
            hi, mid, lo = _split3(dbacc_ref[...])
            dbs_ref[...] = _dot(hi, grp) + _dot(mid, grp) + _dot(lo, grp)

    acc_shapes = [(D_MODEL, D_MODEL), woa.shape, wob.shape, (GROUPS, CHUNK, CHUNK), (CHUNK, CHUNK),
                  (1, D_MODEL), (1, D_MODEL), (1, SGU_W), (1, SGU_W), (1, 128), (1, GATE_W), (1, MID_W)]
    col_spec = lambda rows: pl.BlockSpec((rows, ts), lambda i: (0, i))
    return pl.pallas_call(
        body, name="mid", grid=(nsteps,),
        in_specs=[_row_spec(ts, D_MODEL), _row_spec(ts, D_MODEL), _row_spec(ts, MLA_W), _row_spec(ts, MID_W),
                  _row_spec(ts, GATE_W), _full_spec(woa.shape), _full_spec(wob.shape), _full_spec(wout.shape),
                  _full_spec(ln_g.shape), _full_spec(ln_b.shape), _full_spec(sg_g.shape), _full_spec(sg_b.shape),
                  _full_spec(w_s.shape), _full_spec(bsb.shape)],
        out_specs=[_row_spec(ts, D_MODEL), _row_spec(ts, GATE_W), _row_spec(ts, MID_W), _row_spec(ts, MLA_W),
                   col_spec(MLA_W), col_spec(HEADS), col_spec(GATE_W), col_spec(MID_W)]
        + [_full_spec(sh) for sh in acc_shapes],
        out_shape=[jax.ShapeDtypeStruct((s, D_MODEL), F32), jax.ShapeDtypeStruct((s, GATE_W), BF16),
                   jax.ShapeDtypeStruct((s, MID_W), BF16), jax.ShapeDtypeStruct((s, MLA_W), BF16),
                   jax.ShapeDtypeStruct((MLA_W, s), BF16), jax.ShapeDtypeStruct((HEADS, s), F32),
                   jax.ShapeDtypeStruct((GATE_W, s), BF16), jax.ShapeDtypeStruct((MID_W, s), BF16)]
        + [jax.ShapeDtypeStruct(sh, F32) for sh in acc_shapes],
        scratch_shapes=[pltpu.VMEM((CHUNK, SGU_W), F32)],
        compiler_params=_params(),
    )(x, tgt, o, hm, hg, woa, wob, wout, ln_g, ln_b, sg_g, sg_b, w_s, bsb)


def _lat_bwd(dq, dk, dv, hl, rc, rsl, rsh, g_q, g_kv, wuq, wk, wv, after):
    s = dk.shape[0]
    ts = ROW_TILE
    qk_w = HEADS * HEAD_PAD

    def body(dq_ref, dk_ref, dv_ref, hl_ref, rc_ref, rsl_ref, rsh_ref, gq_ref, gkv_ref, wuq_ref, wk_ref, wv_ref,
             after_ref, dhl_ref, dhlt_ref, dwuq_ref, dwk_ref, dwv_ref, dgq_ref, dgkv_ref, dbl_ref):
        i = pl.program_id(0)

        @pl.when(i == 0)
        def _():
            for r in (dwuq_ref, dwk_ref, dwv_ref, dgq_ref, dgkv_ref, dbl_ref):
                r[...] = jnp.zeros_like(r)

        def emit(lo, val):
            vb = val.astype(BF16)
            n = val.shape[1]
            dhl_ref[:, lo:lo + n] = vb
            dhlt_ref[lo:lo + n, :] = vb.T
            dbl_ref[:, lo:lo + n] += jnp.sum(val, axis=0, keepdims=True)

        c, sl, sh = rc_ref[...], rsl_ref[...], rsh_ref[...]
        lane = lax.broadcasted_iota(jnp.int32, (ts, HEAD_PAD), 1)
        pe = (lane >= NOPE) & (lane < QK_DIM)
        dkpe = jnp.zeros((ts, HEAD_PAD), F32)
        dqu = []
        for hd in range(HEADS):
            lanes = slice(hd * HEAD_PAD, (hd + 1) * HEAD_PAD)
            dqu.append(_rope_t(dq_ref[lanes, :].T, c, sl, sh).astype(BF16))
            dkpe = dkpe + dk_ref[:, lanes]
        dqu = jnp.concatenate(dqu, axis=1)
        dkpe = _rope_t(jnp.where(pe, dkpe, 0.0), c, sl, sh)

        cq = hl_ref[:, :Q_RANK]
        rq = lax.rsqrt(jnp.mean(cq * cq, axis=-1, keepdims=True) + RMS_EPS)
        cqh = cq * rq
        cqn = (cqh * gq_ref[...]).astype(BF16)
        dwuq_ref[...] += _dot_tn(cqn, dqu)
        dcqn = _dot_nt(dqu, wuq_ref[...])
        dgq_ref[...] += jnp.sum(dcqn * cqh, axis=0, keepdims=True)
        dch = dcqn * gq_ref[...]
        emit(0, rq * (dch - cqh * jnp.mean(dch * cqh, axis=-1, keepdims=True)))

        ckv = hl_ref[:, Q_RANK:Q_RANK + KV_RANK]
        rk = lax.rsqrt(jnp.mean(ckv * ckv, axis=-1, keepdims=True) + RMS_EPS)
        ckh = ckv * rk
        ckn = (ckh * gkv_ref[...]).astype(BF16)
        dkb = dk_ref[...].astype(BF16)
        dvb = dv_ref[...].astype(BF16)
        dwk_ref[...] += _dot_tn(ckn, dkb)
        dwv_ref[...] += _dot_tn(ckn, dvb)
        dckn = _dot_nt(dkb, wk_ref[...]) + _dot_nt(dvb, wv_ref[...])
        dgkv_ref[...] += jnp.sum(dckn * ckh, axis=0, keepdims=True)
        dkh = dckn * gkv_ref[...]
        emit(Q_RANK, rk * (dkh - ckh * jnp.mean(dkh * ckh, axis=-1, keepdims=True)))
        emit(Q_RANK + KV_RANK, dkpe)

    acc_shapes = [wuq.shape, wk.shape, wv.shape, g_q.shape, g_kv.shape, (1, LAT_W)]
    return pl.pallas_call(
        body, name="lat_bwd", grid=(s // ts,),
        in_specs=[pl.BlockSpec((qk_w, ts), lambda i: (0, i)), _row_spec(ts, qk_w), _row_spec(ts, MLA_W),
                  _row_spec(ts, LAT_W), _row_spec(ts, HEAD_PAD), _row_spec(ts, HEAD_PAD), _row_spec(ts, HEAD_PAD),
                  _full_spec(g_q.shape), _full_spec(g_kv.shape), _full_spec(wuq.shape), _full_spec(wk.shape),
                  _full_spec(wv.shape), pl.BlockSpec(memory_space=pl.ANY)],
        out_specs=[_row_spec(ts, LAT_W), pl.BlockSpec((LAT_W, ts), lambda i: (0, i))]
        + [_full_spec(sh) for sh in acc_shapes],
        out_shape=[jax.ShapeDtypeStruct((s, LAT_W), BF16), jax.ShapeDtypeStruct((LAT_W, s), BF16)]
        + [jax.ShapeDtypeStruct(sh, F32) for sh in acc_shapes],
        compiler_params=_params(),
    )(dq, dk, dv, hl, rc, rsl, rsh, g_q, g_kv, wuq, wk, wv, after)


def _dx(dr, dhg, dhm, dhl, wt, after):
    s = dr.shape[0]
    ts = MATMUL_ROW_TILE

    def body(dr_ref, dhg_ref, dhm_ref, dhl_ref, wt_ref, after_ref, dx_ref):
        dx_ref[...] = (ALPHA * dr_ref[...]
                       + _dot(dhg_ref[...], wt_ref[ROW_GATE:IN_W, :])
                       + _dot(dhm_ref[...], wt_ref[LAT_COLS:ROW_GATE, :])
                       + _dot(dhl_ref[:, 0:Q_RANK + KV_RANK], wt_ref[0:Q_RANK + KV_RANK, :])
                       + _dot(dhl_ref[:, Q_RANK + KV_RANK:], _kpe_rows(wt_ref)))

    return pl.pallas_call(
        body, name="dx", grid=(s // ts,),
        in_specs=[_row_spec(ts, D_MODEL), _row_spec(ts, GATE_W), _row_spec(ts, MID_W), _row_spec(ts, LAT_W),
                  _full_spec(wt.shape), pl.BlockSpec(memory_space=pl.ANY)],
        out_specs=_row_spec(ts, D_MODEL),
        out_shape=jax.ShapeDtypeStruct((s, D_MODEL), F32),
        compiler_params=_params(),
    )(dr, dhg, dhm, dhl, wt, after)


def _dwt_early(dhmt, dhgt, xb, col, after, name):
    tn = 512
    nm, ng = MID_W // tn, GATE_W // tn
    s = dhmt.shape[1]
    hc = D_MODEL // 2

    def body(col_ref, dm_ref, dg_ref, xb_ref, after_ref, dw_ref):
        i = pl.program_id(0)

        @pl.when(i < nm)
        def _():
            dw_ref[...] = _dot(dm_ref[...], xb_ref[...]).astype(BF16)

        @pl.when(i >= nm)
        def _():
            dw_ref[...] = _dot(dg_ref[...], xb_ref[...]).astype(BF16)

    rows = pl.pallas_call(
        body, name=name,
        grid_spec=pltpu.PrefetchScalarGridSpec(
            num_scalar_prefetch=1, grid=(nm + ng,),
            in_specs=[pl.BlockSpec((tn, s), lambda i, col_ref: (jnp.minimum(i, nm - 1), 0)),
                      pl.BlockSpec((tn, s), lambda i, col_ref: (jnp.maximum(i - nm, 0), 0)),
                      pl.BlockSpec((s, hc), lambda i, col_ref: (0, col_ref[0])),
                      pl.BlockSpec(memory_space=pl.ANY)],
            out_specs=pl.BlockSpec((pl.Element(tn), pl.Element(hc)),
                                   lambda i, col_ref: (pl.multiple_of(LAT_COLS + i * tn, 32), 0))),
        out_shape=jax.ShapeDtypeStruct((IN_W, hc), BF16),
        compiler_params=_params(),
    )(col, dhmt, dhgt, xb, after)

    def zero(buf_ref, out_ref):
        out_ref[...] = jnp.zeros_like(out_ref)

    return pl.pallas_call(
        zero, name=name + "_zero_lat", grid=(1,), in_specs=[pl.BlockSpec(memory_space=pl.ANY)],
        out_specs=pl.BlockSpec((LAT_COLS, hc), lambda i: (0, 0)),
        out_shape=jax.ShapeDtypeStruct((IN_W, hc), BF16), input_output_aliases={0: 0},
    )(rows)


def _dwt_lat(dhlt, xb):
    n, s = dhlt.shape

    def body(dht_ref, xb_ref, dw_ref):
        dw = _dot(dht_ref[...], xb_ref[...]).astype(BF16)
        kpe = Q_RANK + KV_RANK + NOPE
        dw_ref[0:Q_RANK + KV_RANK, :] = dw[0:Q_RANK + KV_RANK]
        dw_ref[Q_RANK + KV_RANK:LAT_COLS, :] = dw[kpe:kpe + ROPE]
        dw_ref[LAT_COLS:, :] = jnp.zeros((LAT_ROWS_PAD - LAT_COLS, D_MODEL), BF16)

    return pl.pallas_call(
        body, name="dwt_lat", in_specs=[VMEM_SPEC, VMEM_SPEC], out_specs=VMEM_SPEC,
        out_shape=jax.ShapeDtypeStruct((LAT_ROWS_PAD, D_MODEL), BF16),
        compiler_params=pltpu.CompilerParams(vmem_limit_bytes=VMEM_LIMIT),
    )(dhlt, xb)


def _split_bias(b):
    z = lambda n: jnp.zeros((n,), b.dtype)
    lat = jnp.concatenate([b[:Q_RANK + KV_RANK], z(NOPE), b[Q_RANK + KV_RANK:LAT_COLS], z(HEAD_PAD - QK_DIM)])
    return b[None, ROW_GATE:], b[None, LAT_COLS:ROW_GATE], lat[None, :]


def _join_bias(g, m, l):
    kpe = Q_RANK + KV_RANK + NOPE
    return jnp.concatenate([l[0, :Q_RANK + KV_RANK], l[0, kpe:kpe + ROPE], m[0], g[0]])


def _rope_tables(positions):
    half = ROPE // 2
    inv_freq = ROPE_THETA ** (-jnp.arange(0, ROPE, 2, dtype=F32) / ROPE)
    ang = positions.astype(F32)[:, None] * inv_freq
    cos, sin = jnp.cos(ang), jnp.sin(ang)
    n = positions.shape[0]
    one, zero = jnp.ones((n, NOPE), F32), jnp.zeros((n, half), F32)
    tail1, tail0 = jnp.ones((n, HEAD_PAD - QK_DIM), F32), jnp.zeros((n, HEAD_PAD - QK_DIM), F32)
    z64 = jnp.zeros((n, NOPE), F32)
    rc = jnp.concatenate([one, cos, cos, tail1], axis=1)
    rsl = jnp.concatenate([z64, -sin, zero, tail0], axis=1)
    rsh = jnp.concatenate([z64, zero, sin, tail0], axis=1)
    return rc, rsl, rsh


def _local_attention(x, positions, wlat, b_in, g_q, w_uq, g_kv, w_ukv, after):
    rc, rsl, rsh = _rope_tables(positions)
    b_g, b_m, b_l = _split_bias(b_in)
    wuq = jnp.pad(w_uq, ((0, 0), (0, 0), (0, HEAD_PAD - QK_DIM))).reshape(Q_RANK, HEADS * HEAD_PAD).astype(BF16)
    wk = jnp.pad(w_ukv[:, :, :NOPE], ((0, 0), (0, 0), (0, HEAD_PAD - NOPE))).reshape(KV_RANK, HEADS * HEAD_PAD).astype(BF16)
    wv = w_ukv[:, :, NOPE:].reshape(KV_RANK, MLA_W).astype(BF16)
    gq2, gkv2 = g_q[None, :], g_kv[None, :]
    hl, q, k, v, xb, qt, kt, vt = _fwd_lat(x, wlat, b_l, gq2, wuq, gkv2, wk, wv, rc, rsl, rsh, after)
    o, lse = _attn_fwd(qt, k, vt)
    return dict(q=q, qt=qt, k=k, kt=kt, v=v, o=o, lse=lse, hl=hl, rc=rc, rsl=rsl, rsh=rsh, gq2=gq2, gkv2=gkv2,
                wuq=wuq, wk=wk, wv=wv, xb=xb, b_g=b_g, b_m=b_m)


def _local_head(st, x, tgt, wt, w_oa, sg_g, sg_b, w_s, b_s, w_ob, w_out, ln_g, ln_b):
    q, qt, k, kt, v, o, lse, hl, xb = (st[n] for n in ("q", "qt", "k", "kt", "v", "o", "lse", "hl", "xb"))
    rc, rsl, rsh, gq2, gkv2, wuq, wk, wv = (st[n] for n in ("rc", "rsl", "rsh", "gq2", "gkv2", "wuq", "wk", "wv"))
    bsb = jnp.repeat(b_s.T, V_DIM, axis=1)
    hg, hm = _fwd_rest(xb, wt, st["b_g"], st["b_m"])
    (dr, dhg, dhm, do, dot, delta, dhgt, dhmt, dwout, dwoa, dwob, dws, dbs, dlng, dlnb, dsgg, dsgb, loss, dbg,
     dbm) = _mid(x, tgt, o, hm, hg, w_oa, w_ob, w_out, ln_g[None, :], ln_b[None, :], sg_g[None, :], sg_b[None, :],
                 w_s, bsb)
    delta = jnp.pad(delta.reshape(HEADS // 2, 2, -1), ((0, 0), (0, 6), (0, 0)))
    early = {
        "w_oa": dwoa, "sgu_ln_g": dsgg[0], "sgu_ln_b": dsgb[0], "w_s": dws, "b_s": dbs[:, :GROUPS].T,
        "w_ob": dwob, "w_out": dwout, "ln_g": dlng[0], "ln_b": dlnb[0],
    }
    state = dict(q=q, qt=qt, k=k, kt=kt, v=v, do=do, dot=dot, lse=lse, delta=delta, hl=hl, rc=rc, rsl=rsl, rsh=rsh,
                 gq2=gq2, gkv2=gkv2, wuq=wuq, wk=wk, wv=wv, dr=dr, dhg=dhg, dhm=dhm, wt=wt, xb=xb, dbg=dbg, dbm=dbm,
                 dhgt=dhgt, dhmt=dhmt)
    return loss, early, state


def _local_attn_bwd(st):
    return _attn_bwd(st["q"], st["qt"], st["k"], st["kt"], st["v"], st["do"], st["dot"], st["lse"], st["delta"])


def _local_tail(st, dq, dk, dv, after):
    dhl, dhlt, dwuq, dwk, dwv, dgq, dgkv, dbl = _lat_bwd(dq, dk, dv, st["hl"], st["rc"], st["rsl"], st["rsh"],
                                                         st["gq2"], st["gkv2"], st["wuq"], st["wk"], st["wv"], after)
    late = {
        "w_lat": _dwt_lat(dhlt, st["xb"]),
        "b_in": _join_bias(st["dbg"], st["dbm"], dbl),
        "g_q": dgq[0],
        "w_uq": dwuq.reshape(Q_RANK, HEADS, HEAD_PAD)[:, :, :QK_DIM],
        "g_kv": dgkv[0],
        "w_ukv": jnp.concatenate([dwk.reshape(KV_RANK, HEADS, HEAD_PAD)[:, :, :NOPE],
                                  dwv.reshape(KV_RANK, HEADS, V_DIM)], axis=2),
    }
    return dhl, late


def _local_step(x, positions, tgt, wt, b_in, g_q, w_uq, g_kv, w_ukv, w_oa, sg_g, sg_b, w_s, b_s, w_ob, w_out, ln_g,
                ln_b):
    st = _local_attention(x, positions, wt[:LAT_COLS], b_in, g_q, w_uq, g_kv, w_ukv, b_in)
    loss, early, st = _local_head(st, x, tgt, wt, w_oa, sg_g, sg_b, w_s, b_s, w_ob, w_out, ln_g, ln_b)
    dq, dk, dv = _local_attn_bwd(st)
    dhl, late = _local_tail(st, dq, dk, dv, dv)
    dx = _dx(st["dr"], st["dhg"], st["dhm"], dhl, st["wt"], dhl)
    grads = {**early, **late}
    halves = [_dwt_early(st["dhmt"], st["dhgt"], st["xb"], jnp.full((1,), h, jnp.int32), dhl, "dwt_half%d" % h)
              for h in range(2)]
    grads["w_in"] = jnp.concatenate([grads.pop("w_lat")[:LAT_COLS], jnp.concatenate(halves, axis=1)[LAT_COLS:]],
                                    axis=0)
    return loss, dx, grads


MESH = pl.DeviceIdType.MESH
N_CHIPS = 4
HBM_SPEC = pl.BlockSpec(memory_space=pl.ANY)
HBM_SPEC_STRICT = pl.BlockSpec(memory_space=pltpu.HBM)
VMEM_SPEC = pl.BlockSpec(memory_space=pltpu.VMEM)

REP_ROWS = 80


def _rows8(a):
    flat = a.reshape(-1)
    n = -(-flat.shape[0] // (8 * D_MODEL)) * 8 * D_MODEL
    return jnp.pad(flat, (0, n - flat.shape[0])).reshape(-1, D_MODEL)


def _place():
    x, y, c = lax.axis_index("x"), lax.axis_index("y"), lax.axis_index("c")
    others = [(1 - x, y), (x, 1 - y), (1 - x, 1 - y)]
    return x, y, c, others


def _gather_weights(shards):
    n = len(shards)

    def body(*refs):
        ins, outs, bufs = refs[:n], refs[n:2 * n], refs[2 * n:3 * n]
        send_sems, recv_sems, local_sems = refs[3 * n:]
        x, y, c, others = _place()
        me = 2 * x + y
        sibling = (x, y, 1 - c)
        for src, buf in zip(ins, bufs):
            buf[...] = src[...].astype(BF16)
        own = [pltpu.make_async_copy(bufs[w], outs[w].at[me], local_sems.at[w]) for w in range(n)]
        for cp in own:
            cp.start()

        def part(w, chip, half):
            hc = shards[w].shape[1] // 2
            return outs[w].at[chip, :, pl.ds(half * hc, hc)]

        def sent(w, j):
            hc = shards[w].shape[1] // 2
            return pltpu.make_async_remote_copy(
                src_ref=bufs[w].at[:, pl.ds(c * hc, hc)], dst_ref=part(w, me, c),
                send_sem=send_sems.at[w * 3 + j], recv_sem=recv_sems.at[w * 3 + j],
                device_id=(*others[j], c), device_id_type=MESH)

        def landed(w, j):
            px, py = others[j]
            return pltpu.make_async_remote_copy(
                src_ref=part(w, 2 * px + py, c), dst_ref=part(w, 2 * px + py, c),
                send_sem=send_sems.at[w * 3 + j], recv_sem=recv_sems.at[w * 3 + j],
                device_id=(px, py, c), device_id_type=MESH)

        def passed(w, j, half):
            px, py = others[j]
            k = n * 3 + w * 3 + j
            return pltpu.make_async_remote_copy(
                src_ref=part(w, 2 * px + py, half), dst_ref=part(w, 2 * px + py, half),
                send_sem=send_sems.at[k], recv_sem=recv_sems.at[k], device_id=sibling, device_id_type=MESH)

        first = [sent(w, j) for w in range(n) for j in range(3)]
        for cp in first:
            cp.start()
        fwd = []
        for w in range(n):
            for j in range(3):
                landed(w, j).wait_recv()
                cp = passed(w, j, c)
                cp.start()
                fwd.append(cp)
        for w in range(n):
            for j in range(3):
                passed(w, j, 1 - c).wait_recv()
        for cp in first + fwd:
            cp.wait_send()
        for cp in own:
            cp.wait()

    return pl.pallas_call(
        body, name="gather_weights",
        in_specs=[VMEM_SPEC] * n, out_specs=[HBM_SPEC] * n,
        out_shape=[jax.ShapeDtypeStruct((N_CHIPS,) + s.shape, BF16) for s in shards],
        scratch_shapes=[pltpu.VMEM(s.shape, BF16) for s in shards]
        + [pltpu.SemaphoreType.DMA((6 * n,)), pltpu.SemaphoreType.DMA((6 * n,)), pltpu.SemaphoreType.DMA((n,))],
        compiler_params=pltpu.CompilerParams(vmem_limit_bytes=VMEM_LIMIT),
    )(*shards)


N_DEV = 8
LOSS_TILE = (8, 128)


def _gather_first(lat, uq):
    hl, hu = lat.shape[1] // 2, uq.shape[1] // 2

    def body(lat_ref, uq_ref, wlat_ref, guq_ref, lat_buf, uq_buf, send_sems, recv_sems, local_sems):
        x, y, c, others = _place()
        me = 2 * x + y
        sibling = (x, y, 1 - c)
        lat_buf[...] = lat_ref[...].astype(BF16)
        uq_buf[...] = uq_ref[...].astype(BF16)

        def copy(src, dst, k, to):
            return pltpu.make_async_remote_copy(src_ref=src, dst_ref=dst, send_sem=send_sems.at[k],
                                                recv_sem=recv_sems.at[k], device_id=to, device_id_type=MESH)

        def uq_part(chip, half):
            return guq_ref.at[chip, :, pl.ds(half * hu, hu)]

        def lat_part(half):
            return wlat_ref.at[:, pl.ds(half * hl, hl)]

        own = pltpu.make_async_copy(uq_buf, guq_ref.at[me], local_sems.at[0])
        own.start()
        first = [copy(uq_buf.at[:, pl.ds(c * hu, hu)], uq_part(me, c), j, (*others[j], c)) for j in range(3)]
        for cp in first:
            cp.start()

        @pl.when(me == 0)
        def _():
            mine = pltpu.make_async_copy(lat_buf, wlat_ref, local_sems.at[1])
            mine.start()
            cps = [copy(lat_buf.at[:, pl.ds(c * hl, hl)], lat_part(c), 6 + j, (*others[j], c)) for j in range(3)]
            for cp in cps:
                cp.start()
            for cp in cps:
                cp.wait_send()
            mine.wait()

        @pl.when(me != 0)
        def _():
            j0 = x + 2 * y - 1
            copy(lat_part(c), lat_part(c), 6 + j0, (0, 0, c)).wait_recv()
            fwd = copy(lat_part(c), lat_part(c), 9, sibling)
            fwd.start()
            copy(lat_part(1 - c), lat_part(1 - c), 9, sibling).wait_recv()
            fwd.wait_send()

        fwd = []
        for j, (px, py) in enumerate(others):
            chip = 2 * px + py
            copy(uq_part(chip, c), uq_part(chip, c), j, (px, py, c)).wait_recv()
            cp = copy(uq_part(chip, c), uq_part(chip, c), 3 + j, sibling)
            cp.start()
            fwd.append(cp)
        for j, (px, py) in enumerate(others):
            chip = 2 * px + py
            copy(uq_part(chip, 1 - c), uq_part(chip, 1 - c), 3 + j, sibling).wait_recv()
        for cp in first + fwd:
            cp.wait_send()
        own.wait()

    return pl.pallas_call(
        body, name="gather_first", in_specs=[VMEM_SPEC, VMEM_SPEC], out_specs=[HBM_SPEC, HBM_SPEC],
        out_shape=[jax.ShapeDtypeStruct(lat.shape, BF16), jax.ShapeDtypeStruct((N_CHIPS,) + uq.shape, BF16)],
        scratch_shapes=[pltpu.VMEM(lat.shape, BF16), pltpu.VMEM(uq.shape, BF16), pltpu.SemaphoreType.DMA((10,)),
                        pltpu.SemaphoreType.DMA((10,)), pltpu.SemaphoreType.DMA((2,))],
        compiler_params=pltpu.CompilerParams(vmem_limit_bytes=VMEM_LIMIT),
    )(lat, uq)


def _cast_own(shards, me):
    n = len(shards)

    def body(me_ref, *refs):
        for w in range(n):
            refs[n + w][...] = refs[w][...].astype(BF16)

    return pl.pallas_call(
        body, name="cast_own",
        grid_spec=pltpu.PrefetchScalarGridSpec(
            num_scalar_prefetch=1, grid=(1,),
            in_specs=[pl.BlockSpec(s.shape, lambda i, me_ref: (0, 0)) for s in shards],
            out_specs=[pl.BlockSpec((None,) + s.shape, lambda i, me_ref: (me_ref[0], 0, 0)) for s in shards]),
        out_shape=[jax.ShapeDtypeStruct((N_CHIPS,) + s.shape, BF16) for s in shards],
        compiler_params=pltpu.CompilerParams(vmem_limit_bytes=VMEM_LIMIT),
    )(me, *shards)


def _gather_start(bufs, after):
    n = len(bufs)

    def body(*refs):
        b_refs = refs[:n]
        send_sems, recv_sems, token = refs[n + 1], refs[n + 2], refs[-1]
        x, y, c, others = _place()
        me = 2 * x + y
        for w in range(n):
            hc = _half(bufs[w])
            mine = b_refs[w].at[me, :, pl.ds(c * hc, hc)]
            for j, (px, py) in enumerate(others):
                pltpu.make_async_remote_copy(
                    src_ref=mine, dst_ref=mine, send_sem=send_sems.at[3 * w + j], recv_sem=recv_sems.at[3 * w + j],
                    device_id=(px, py, c), device_id_type=MESH).start()
        token[...] = jnp.zeros_like(token)

    hbm = [pltpu.HBM(b.shape, BF16) for b in bufs]
    outs = pl.pallas_call(
        body, name="gather_start",
        out_shape=(pltpu.SemaphoreType.DMA((3 * n,)), pltpu.SemaphoreType.DMA((3 * n,)), *hbm,
                   jax.ShapeDtypeStruct(LOSS_TILE, F32)),
        in_specs=[HBM_SPEC_STRICT] * n + [HBM_SPEC],
        out_specs=(SEM_SPEC, SEM_SPEC, *[HBM_SPEC_STRICT] * n, VMEM_SPEC),
        input_output_aliases={i: 2 + i for i in range(n)},
        compiler_params=pltpu.CompilerParams(has_side_effects=SPLIT_EFFECT),
    )(*[pltpu.with_memory_space_constraint(b, pltpu.HBM) for b in bufs], after)
    return outs[0], outs[1], list(outs[2:2 + n]), outs[-1]


def _gather_wait(send_sems, recv_sems, bufs, after):
    n = len(bufs)

    def body(*refs):
        b_refs = refs[:n]
        send_sems, recv_sems = refs[n], refs[n + 1]
        x, y, c, others = _place()
        me = 2 * x + y
        for w in range(n):
            hc = _half(bufs[w])
            for j, (px, py) in enumerate(others):
                cp = pltpu.make_async_remote_copy(
                    src_ref=b_refs[w].at[me, :, pl.ds(c * hc, hc)],
                    dst_ref=b_refs[w].at[2 * px + py, :, pl.ds(c * hc, hc)],
                    send_sem=send_sems.at[3 * w + j], recv_sem=recv_sems.at[3 * w + j], device_id=(px, py, c),
                    device_id_type=MESH)
                cp.wait_send()
                cp.wait_recv()

    outs = pl.pallas_call(
        body, name="gather_wait", out_shape=tuple(pltpu.HBM(b.shape, b.dtype) for b in bufs),
        in_specs=[HBM_SPEC_STRICT] * n + [SEM_SPEC, SEM_SPEC, HBM_SPEC],
        out_specs=tuple([HBM_SPEC_STRICT] * n), input_output_aliases={i: i for i in range(n)},
        compiler_params=pltpu.CompilerParams(has_side_effects=SPLIT_EFFECT),
    )(*bufs, send_sems, recv_sems, after)
    return list(outs)


def _gather_finish(bufs):
    n = len(bufs)

    def body(*refs):
        b_refs = refs[n:2 * n]
        send_sems, recv_sems = refs[2 * n:]
        x, y, c, others = _place()
        cps = []
        for w in range(n):
            hc = _half(bufs[w])
            for j, (px, py) in enumerate(others):
                part = b_refs[w].at[2 * px + py, :, pl.ds(c * hc, hc)]
                cps.append(pltpu.make_async_remote_copy(
                    src_ref=part, dst_ref=part, send_sem=send_sems.at[3 * w + j], recv_sem=recv_sems.at[3 * w + j],
                    device_id=(x, y, 1 - c), device_id_type=MESH))
        for cp in cps:
            cp.start()
        for w in range(n):
            hc = _half(bufs[w])
            for j, (px, py) in enumerate(others):
                theirs = b_refs[w].at[2 * px + py, :, pl.ds((1 - c) * hc, hc)]
                pltpu.make_async_remote_copy(
                    src_ref=theirs, dst_ref=theirs, send_sem=send_sems.at[3 * w + j], recv_sem=recv_sems.at[3 * w + j],
                    device_id=(x, y, 1 - c), device_id_type=MESH).wait_recv()
        for cp in cps:
            cp.wait_send()

    return pl.pallas_call(
        body, name="gather_finish", in_specs=[HBM_SPEC] * n, out_specs=[HBM_SPEC] * n,
        out_shape=[jax.ShapeDtypeStruct(b.shape, b.dtype) for b in bufs],
        input_output_aliases={i: i for i in range(n)},
        scratch_shapes=[pltpu.SemaphoreType.DMA((3 * n,)), pltpu.SemaphoreType.DMA((3 * n,))],
    )(*bufs)


def _half(a):
    return a.shape[-1] // 2


def _exchange_pairs(parts, name):
    n = len(parts)

    def body(*refs):
        p_refs, r_refs = refs[:n], refs[n:2 * n]
        send_sems, recv_sems = refs[2 * n:]
        x, y, c, _ = _place()
        cps = []
        for w in range(n):
            h = _half(parts[w])
            cps.append(pltpu.make_async_remote_copy(
                src_ref=p_refs[w].at[:, :, pl.ds((1 - c) * h, h)], dst_ref=r_refs[w],
                send_sem=send_sems.at[w], recv_sem=recv_sems.at[w], device_id=(x, y, 1 - c), device_id_type=MESH))
        for cp in cps:
            cp.start()
        for cp in cps:
            cp.wait()

    return pl.pallas_call(
        body, name=name, in_specs=[HBM_SPEC] * n, out_specs=[HBM_SPEC] * n,
        out_shape=[jax.ShapeDtypeStruct((N_CHIPS, p.shape[1], _half(p)), BF16) for p in parts],
        scratch_shapes=[pltpu.SemaphoreType.DMA((n,)), pltpu.SemaphoreType.DMA((n,))],
    )(*parts)


def _sibling_part(ref, w, n_whole, shape, c):
    if w < n_whole:
        return ref
    h = shape[-1] // 2
    return ref.at[:, :, pl.ds((1 - c) * h, h)]


def _pairs_start(parts, all_loss, n_whole):
    n = len(parts)

    def body(*refs):
        p_refs, r_refs, loss_ref = refs[:n], refs[n:2 * n], refs[2 * n]
        send_sems, recv_sems, token = refs[2 * n + 1], refs[2 * n + 2], refs[-1]
        x, y, c, _ = _place()
        for w in range(n):
            h = _half(parts[w])
            pltpu.make_async_remote_copy(
                src_ref=_sibling_part(p_refs[w], w, n_whole, parts[w].shape, c), dst_ref=r_refs[w],
                send_sem=send_sems.at[w], recv_sem=recv_sems.at[w], device_id=(x, y, 1 - c),
                device_id_type=MESH).start()
        me = 4 * x + 2 * y + c
        for t in range(1, N_DEV):
            d = (me + t) % N_DEV
            pltpu.make_async_remote_copy(
                src_ref=loss_ref.at[me], dst_ref=loss_ref.at[me], send_sem=send_sems.at[n + t - 1],
                recv_sem=recv_sems.at[n + t - 1], device_id=(d // 4, (d // 2) % 2, d % 2), device_id_type=MESH).start()
        token[...] = jnp.zeros_like(token)

    lands = [pltpu.HBM(p.shape if w < n_whole else (N_CHIPS, p.shape[1], _half(p)), BF16)
             for w, p in enumerate(parts)]
    nsem = n + N_DEV - 1
    outs = pl.pallas_call(
        body, name="pairs_start",
        out_shape=(pltpu.SemaphoreType.DMA((nsem,)), pltpu.SemaphoreType.DMA((nsem,)),
                   *[pltpu.HBM(p.shape, p.dtype) for p in parts], *lands, pltpu.HBM(all_loss.shape, F32),
                   jax.ShapeDtypeStruct(LOSS_TILE, F32)),
        in_specs=[HBM_SPEC_STRICT] * (2 * n + 1),
        out_specs=(SEM_SPEC, SEM_SPEC, *[HBM_SPEC_STRICT] * (2 * n + 1), VMEM_SPEC),
        input_output_aliases={i: 2 + i for i in range(2 * n + 1)},
        compiler_params=pltpu.CompilerParams(has_side_effects=SPLIT_EFFECT),
    )(*[pltpu.with_memory_space_constraint(p, pltpu.HBM) for p in parts],
      *[pltpu.with_memory_space_constraint(lax.empty(l.shape, BF16), pltpu.HBM) for l in lands],
      pltpu.with_memory_space_constraint(all_loss, pltpu.HBM))
    return outs[0], outs[1], list(outs[2:2 + n]), list(outs[2 + n:2 + 2 * n]), outs[2 + 2 * n], outs[-1]


def _pairs_wait(send_sems, recv_sems, parts, lands, all_loss, after, n_whole):
    n = len(parts)

    def body(*refs):
        p_refs, r_refs, loss_ref = refs[:n], refs[n:2 * n], refs[2 * n]
        send_sems, recv_sems = refs[2 * n + 1], refs[2 * n + 2]
        x, y, c, _ = _place()
        for w in range(n):
            h = _half(parts[w])
            cp = pltpu.make_async_remote_copy(
                src_ref=_sibling_part(p_refs[w], w, n_whole, parts[w].shape, c), dst_ref=r_refs[w],
                send_sem=send_sems.at[w],
                recv_sem=recv_sems.at[w], device_id=(x, y, 1 - c), device_id_type=MESH)
            cp.wait_send()
            cp.wait_recv()
        me = 4 * x + 2 * y + c
        for t in range(1, N_DEV):
            d = (me + N_DEV - t) % N_DEV
            cp = pltpu.make_async_remote_copy(
                src_ref=loss_ref.at[me], dst_ref=loss_ref.at[d], send_sem=send_sems.at[n + t - 1],
                recv_sem=recv_sems.at[n + t - 1], device_id=(d // 4, (d // 2) % 2, d % 2), device_id_type=MESH)
            cp.wait_send()
            cp.wait_recv()

    bufs = (*parts, *lands, all_loss)
    outs = pl.pallas_call(
        body, name="pairs_wait", out_shape=tuple(pltpu.HBM(a.shape, a.dtype) for a in bufs),
        in_specs=[HBM_SPEC_STRICT] * len(bufs) + [SEM_SPEC, SEM_SPEC, HBM_SPEC],
        out_specs=tuple([HBM_SPEC_STRICT] * len(bufs)), input_output_aliases={i: i for i in range(len(bufs))},
        compiler_params=pltpu.CompilerParams(has_side_effects=SPLIT_EFFECT),
    )(*bufs, send_sems, recv_sems, after)
    return list(outs[:n]), list(outs[n:2 * n]), outs[2 * n]


def _add_pair_tiled(p, r):
    rows, h = r.shape[1:]

    def body(p_ref, r_ref, q_ref):
        q_ref[...] = (p_ref[...].astype(F32) + r_ref[...].astype(F32)).astype(BF16)

    spec = pl.BlockSpec((None, rows, h), lambda k: (k, 0, 0))
    return pl.pallas_call(
        body, name="add_pair_w_in", grid=(N_CHIPS,), in_specs=[spec, spec], out_specs=spec,
        out_shape=jax.ShapeDtypeStruct(r.shape, BF16),
    )(p, r)


def _add_pair_small(ps, rs, c, name):
    n = len(ps)

    def body(c_ref, *refs):
        for w in range(n):
            h = _half(ps[w])
            mine = refs[w][:, :, pl.ds(pl.multiple_of(c_ref[0] * h, 128), h)]
            refs[2 * n + w][...] = (mine.astype(F32) + refs[n + w][...].astype(F32)).astype(BF16)

    return pl.pallas_call(
        body, name=name,
        in_specs=[pl.BlockSpec(memory_space=pltpu.SMEM)] + [VMEM_SPEC] * (2 * n), out_specs=[VMEM_SPEC] * n,
        out_shape=[jax.ShapeDtypeStruct(r.shape, BF16) for r in rs],
        compiler_params=pltpu.CompilerParams(vmem_limit_bytes=VMEM_LIMIT),
    )(c, *ps, *rs)


def _exchange_chips(qs):
    n = len(qs)

    def body(*refs):
        q_refs, r_refs = refs[:n], refs[n:2 * n]
        send_sems, recv_sems = refs[2 * n:]
        x, y, c, others = _place()
        me = 2 * x + y
        cps = []
        for w in range(n):
            for j, (px, py) in enumerate(others):
                cps.append(pltpu.make_async_remote_copy(
                    src_ref=q_refs[w].at[2 * px + py], dst_ref=r_refs[w].at[me], send_sem=send_sems.at[3 * w + j],
                    recv_sem=recv_sems.at[3 * w + j], device_id=(px, py, c), device_id_type=MESH))
        for cp in cps:
            cp.start()
        for w in range(n):
            for j, (px, py) in enumerate(others):
                pltpu.make_async_remote_copy(
                    src_ref=q_refs[w].at[me], dst_ref=r_refs[w].at[2 * px + py], send_sem=send_sems.at[3 * w + j],
                    recv_sem=recv_sems.at[3 * w + j], device_id=(px, py, c), device_id_type=MESH).wait_recv()
        for cp in cps:
            cp.wait_send()

    return pl.pallas_call(
        body, name="exchange_chips", in_specs=[HBM_SPEC] * n, out_specs=[HBM_SPEC] * n,
        out_shape=[jax.ShapeDtypeStruct(q.shape, BF16) for q in qs],
        scratch_shapes=[pltpu.SemaphoreType.DMA((3 * n,)), pltpu.SemaphoreType.DMA((3 * n,))],
    )(*qs)


SEM_SPEC = pl.BlockSpec(memory_space=pltpu.SEMAPHORE)
SPLIT_EFFECT = pltpu.SideEffectType.DATAFLOW_SIDE_EFFECTING


def _chips_start(qs, name):
    n = len(qs)

    def body(*refs):
        q_refs, land_refs = refs[:n], refs[n:2 * n]
        send_sems, recv_sems, token = refs[2 * n], refs[2 * n + 1], refs[-1]
        x, y, c, others = _place()
        me = 2 * x + y
        for w in range(n):
            for j, (px, py) in enumerate(others):
                pltpu.make_async_remote_copy(
                    src_ref=q_refs[w].at[2 * px + py], dst_ref=land_refs[w].at[me], send_sem=send_sems.at[3 * w + j],
                    recv_sem=recv_sems.at[3 * w + j], device_id=(px, py, c), device_id_type=MESH).start()
        token[...] = jnp.zeros_like(token)

    hbm = [pltpu.HBM(q.shape, BF16) for q in qs]
    outs = pl.pallas_call(
        body, name=name,
        out_shape=(pltpu.SemaphoreType.DMA((3 * n,)), pltpu.SemaphoreType.DMA((3 * n,)), *hbm, *hbm,
                   jax.ShapeDtypeStruct(LOSS_TILE, F32)),
        in_specs=[HBM_SPEC_STRICT] * (2 * n),
        out_specs=(SEM_SPEC, SEM_SPEC, *[HBM_SPEC_STRICT] * (2 * n), VMEM_SPEC),
        input_output_aliases={i: 2 + i for i in range(2 * n)},
        compiler_params=pltpu.CompilerParams(has_side_effects=SPLIT_EFFECT),
    )(*[pltpu.with_memory_space_constraint(q, pltpu.HBM) for q in qs],
      *[pltpu.with_memory_space_constraint(lax.empty(q.shape, BF16), pltpu.HBM) for q in qs])
    return outs[0], outs[1], outs[2:2 + n], outs[2 + n:2 + 2 * n], outs[-1]


def _chips_wait(send_sems, recv_sems, q_thru, land_thru, after, name):
    n = len(q_thru)

    def body(*refs):
        q_refs, land_refs = refs[:n], refs[n:2 * n]
        send_sems, recv_sems = refs[2 * n], refs[2 * n + 1]
        x, y, c, others = _place()
        me = 2 * x + y
        for w in range(n):
            for j, (px, py) in enumerate(others):
                cp = pltpu.make_async_remote_copy(
                    src_ref=q_refs[w].at[2 * px + py], dst_ref=land_refs[w].at[2 * px + py],
                    send_sem=send_sems.at[3 * w + j], recv_sem=recv_sems.at[3 * w + j], device_id=(px, py, c),
                    device_id_type=MESH)
                cp.wait_send()
                cp.wait_recv()

    outs = pl.pallas_call(
        body, name=name, out_shape=tuple(pltpu.HBM(a.shape, a.dtype) for a in (*q_thru, *land_thru)),
        in_specs=[HBM_SPEC_STRICT] * (2 * n) + [SEM_SPEC, SEM_SPEC, HBM_SPEC],
        out_specs=tuple([HBM_SPEC_STRICT] * (2 * n)), input_output_aliases={i: i for i in range(2 * n)},
        compiler_params=pltpu.CompilerParams(has_side_effects=SPLIT_EFFECT),
    )(*q_thru, *land_thru, send_sems, recv_sems, after)
    return list(outs[:n]), list(outs[n:])


def _sum_chips_tiled(q, r, idx, tile):
    rows, h = r.shape[1:]
    nt = h // tile

    def body(idx_ref, q_ref, r0_ref, r1_ref, r2_ref, g_ref):
        g_ref[...] = (q_ref[...].astype(F32) + r0_ref[...].astype(F32) + r1_ref[...].astype(F32)
                      + r2_ref[...].astype(F32))

    def slab(t):
        return pl.BlockSpec((None, rows, tile), lambda i, idx_ref: (idx_ref[t], 0, i))

    return pl.pallas_call(
        body, name="sum_chips_w_in",
        grid_spec=pltpu.PrefetchScalarGridSpec(
            num_scalar_prefetch=1, grid=(nt,), in_specs=[slab(0), slab(1), slab(2), slab(3)],
            out_specs=pl.BlockSpec((rows, tile), lambda i, idx_ref: (0, idx_ref[4] * nt + i))),
        out_shape=jax.ShapeDtypeStruct((rows, 2 * h), F32),
    )(idx, q, r, r, r)


def _sum_chips_small(qs, rs, idx, all_dtypes):
    n = len(rs)
    n_all = len(all_dtypes)

    def body(idx_ref, *refs):
        c = idx_ref[4]
        for w in range(n):
            q_ref, r_ref, g_ref = refs[w], refs[n + w], refs[2 * n + w]
            acc = q_ref[idx_ref[0]].astype(F32)
            for t in range(1, N_CHIPS):
                acc = acc + r_ref[idx_ref[t]].astype(F32)
            h = rs[w].shape[2]
            mine = pl.ds(pl.multiple_of(c * h, 128), h)
            g_ref[...] = jnp.zeros_like(g_ref)
            if w >= n - n_all:
                g_ref[idx_ref[0], :, mine] = acc.astype(g_ref.dtype)
            else:
                g_ref[:, mine] = acc

    shapes = [jax.ShapeDtypeStruct((r.shape[1], 2 * r.shape[2]), F32) for r in rs[:n - n_all]]
    shapes += [jax.ShapeDtypeStruct((N_CHIPS, r.shape[1], 2 * r.shape[2]), dt)
               for r, dt in zip(rs[n - n_all:], all_dtypes)]
    return pl.pallas_call(
        body, name="sum_chips_small",
        in_specs=[pl.BlockSpec(memory_space=pltpu.SMEM)] + [VMEM_SPEC] * (2 * n), out_specs=[VMEM_SPEC] * n,
        out_shape=shapes, compiler_params=pltpu.CompilerParams(vmem_limit_bytes=VMEM_LIMIT),
    )(idx, *qs, *rs)


def _share(shards, alls):
    n, na = len(shards), len(alls)
    total = n + na

    def body(*refs):
        g_refs, a_refs = refs[total:total + n], refs[total + n:2 * total]
        send_sems, recv_sems = refs[2 * total:]
        x, y, c, others = _place()
        me = 2 * x + y
        sibling = (x, y, 1 - c)

        def cols_of(w, half):
            h = shards[w].shape[1] // 2
            return g_refs[w].at[:, pl.ds(half * h, h)]

        def slab(a, chip, half):
            h = alls[a].shape[2] // 2
            return a_refs[a].at[chip, :, pl.ds(half * h, h)]

        def copy(src, dst, k, to):
            return pltpu.make_async_remote_copy(src_ref=src, dst_ref=dst, send_sem=send_sems.at[k],
                                                recv_sem=recv_sems.at[k], device_id=to, device_id_type=MESH)

        cps = [copy(cols_of(w, c), cols_of(w, c), w, sibling) for w in range(n)]
        for a in range(na):
            base = n + 7 * a
            cps.append(copy(slab(a, me, c), slab(a, me, c), base, sibling))
            for j, (px, py) in enumerate(others):
                cps.append(copy(slab(a, me, c), slab(a, me, c), base + 1 + j, (px, py, c)))
        for cp in cps:
            cp.start()
        fwd = []
        for a in range(na):
            base = n + 7 * a
            for j, (px, py) in enumerate(others):
                chip = 2 * px + py
                copy(slab(a, me, c), slab(a, chip, c), base + 1 + j, (px, py, c)).wait_recv()
                cp = copy(slab(a, chip, c), slab(a, chip, c), base + 4 + j, sibling)
                cp.start()
                fwd.append(cp)
        for a in range(na):
            base = n + 7 * a
            for j, (px, py) in enumerate(others):
                chip = 2 * px + py
                copy(slab(a, chip, c), slab(a, chip, 1 - c), base + 4 + j, sibling).wait_recv()
            copy(slab(a, me, c), slab(a, me, 1 - c), base, sibling).wait_recv()
        for w in range(n):
            copy(cols_of(w, c), cols_of(w, 1 - c), w, sibling).wait_recv()
        for cp in cps + fwd:
            cp.wait_send()

    nsem = n + 7 * na
    return pl.pallas_call(
        body, name="share", in_specs=[HBM_SPEC] * total, out_specs=[HBM_SPEC] * total,
        out_shape=[jax.ShapeDtypeStruct(a.shape, a.dtype) for a in (*shards, *alls)],
        input_output_aliases={i: i for i in range(total)},
        scratch_shapes=[pltpu.SemaphoreType.DMA((nsem,)), pltpu.SemaphoreType.DMA((nsem,))],
    )(*shards, *alls)


def _adamw(w, g, m, v):
    m2 = ADAM_B1 * m + (1.0 - ADAM_B1) * g
    v2 = ADAM_B2 * v + (1.0 - ADAM_B2) * (g * g)
    m_hat = m2 / (1.0 - ADAM_B1 ** ADAM_STEP)
    v_hat = v2 / (1.0 - ADAM_B2 ** ADAM_STEP)
    return -ADAM_LR * (m_hat / (jnp.sqrt(v_hat) + ADAM_EPS) + ADAM_WD * w), m2, v2


def _update_w_in(wt, gt, mt, vt, lat, owner, tile):
    nlat = lat.shape[0] // tile

    def body(owner_ref, w_ref, g_ref, m_ref, v_ref, lat_ref, g2_ref, d_ref, m2_ref, v2_ref):
        row = pl.program_id(0) * tile + lax.broadcasted_iota(jnp.int32, (tile, 1), 0)
        g = jnp.where((row < LAT_COLS) & (owner_ref[0] == 1), lat_ref[...].astype(F32), g_ref[...])
        g2_ref[...] = g
        d_ref[...], m2_ref[...], v2_ref[...] = _adamw(w_ref[...], g, m_ref[...], v_ref[...])

    spec = pl.BlockSpec((tile, wt.shape[1]), lambda i, o: (i, 0))
    return pl.pallas_call(
        body, name="update_w_in",
        grid_spec=pltpu.PrefetchScalarGridSpec(
            num_scalar_prefetch=1, grid=(wt.shape[0] // tile,),
            in_specs=[spec] * 4 + [pl.BlockSpec((tile, wt.shape[1]), lambda i, o: (jnp.minimum(i, nlat - 1), 0))],
            out_specs=[spec] * 4),
        out_shape=[jax.ShapeDtypeStruct(wt.shape, F32)] * 4,
        compiler_params=_params(("parallel",)),
    )(owner, wt, gt, mt, vt, lat)


def _update_small(ws, gs, ms, vs):
    n = len(ws)

    def body(*refs):
        for k in range(n):
            w_ref, g_ref, m_ref, v_ref = refs[k], refs[n + k], refs[2 * n + k], refs[3 * n + k]
            d, m2, v2 = _adamw(w_ref[...], g_ref[...], m_ref[...], v_ref[...])
            refs[4 * n + k][...] = d
            refs[5 * n + k][...] = m2
            refs[6 * n + k][...] = v2

    shapes = [jax.ShapeDtypeStruct(w.shape, F32) for w in ws]
    outs = pl.pallas_call(
        body, name="update_small", in_specs=[VMEM_SPEC] * (4 * n), out_specs=[VMEM_SPEC] * (3 * n),
        out_shape=shapes * 3,
        compiler_params=pltpu.CompilerParams(vmem_limit_bytes=VMEM_LIMIT),
    )(*ws, *gs, *ms, *vs)
    return outs[:n], outs[n:2 * n], outs[2 * n:]


SHARDED = ("w_in", "w_uq", "w_oa", "w_ob", "w_out")
REPLICATED = ("b_in", "g_q", "g_kv", "w_ukv", "sgu_ln_g", "sgu_ln_b", "w_s", "b_s", "ln_g", "ln_b")
ORDER = ("w_in", "b_in", "g_q", "w_uq", "g_kv", "w_ukv", "w_oa", "sgu_ln_g", "sgu_ln_b", "w_s", "b_s", "w_ob", "w_out",
         "ln_g", "ln_b")


def kernel(x, positions, w_in, b_in, g_q, w_uq, g_kv, w_ukv, w_oa, sgu_ln_g, sgu_ln_b, w_s, b_s, w_ob, w_out, ln_g, ln_b, loss_target, m_w_in, m_b_in, m_g_q, m_w_uq, m_g_kv, m_w_ukv, m_w_oa, m_sgu_ln_g, m_sgu_ln_b, m_w_s, m_b_s, m_w_ob, m_w_out, m_ln_g, m_ln_b, v_w_in, v_b_in, v_g_q, v_w_uq, v_g_kv, v_w_ukv, v_w_oa, v_sgu_ln_g, v_sgu_ln_b, v_w_s, v_b_s, v_w_ob, v_w_out, v_ln_g, v_ln_b):
    w = dict(w_in=w_in, b_in=b_in, g_q=g_q, w_uq=w_uq, g_kv=g_kv, w_ukv=w_ukv, w_oa=w_oa, sgu_ln_g=sgu_ln_g,
             sgu_ln_b=sgu_ln_b, w_s=w_s, b_s=b_s, w_ob=w_ob, w_out=w_out, ln_g=ln_g, ln_b=ln_b)
    m = dict(w_in=m_w_in, b_in=m_b_in, g_q=m_g_q, w_uq=m_w_uq, g_kv=m_g_kv, w_ukv=m_w_ukv, w_oa=m_w_oa,
             sgu_ln_g=m_sgu_ln_g, sgu_ln_b=m_sgu_ln_b, w_s=m_w_s, b_s=m_b_s, w_ob=m_w_ob, w_out=m_w_out, ln_g=m_ln_g,
             ln_b=m_ln_b)
    v = dict(w_in=v_w_in, b_in=v_b_in, g_q=v_g_q, w_uq=v_w_uq, g_kv=v_g_kv, w_ukv=v_w_ukv, w_oa=v_w_oa,
             sgu_ln_g=v_sgu_ln_g, sgu_ln_b=v_sgu_ln_b, w_s=v_w_s, b_s=v_b_s, w_ob=v_w_ob, w_out=v_w_out, ln_g=v_ln_g,
             ln_b=v_ln_b)
    w, m, v = ({n: a[0] for n, a in d.items()} for d in (w, m, v))
    c = lax.axis_index("c")

    wt_shard, mt_shard, vt_shard = (jnp.transpose(d["w_in"]) for d in (w, m, v))
    xi, yi = lax.axis_index("x"), lax.axis_index("y")
    me1 = (2 * xi + yi).reshape(1).astype(jnp.int32)
    g_lat, g_uq = _gather_first(wt_shard[:LAT_COLS], w["w_uq"].reshape(Q_RANK // 4, HEADS * QK_DIM))
    bufs = _cast_own([wt_shard, w["w_oa"], w["w_ob"], w["w_out"]], me1)
    send0, recv0, bufs, token0 = _gather_start(bufs, g_lat)
    st = _local_attention(x[0], positions[0], g_lat, w["b_in"], w["g_q"], g_uq.reshape(Q_RANK, HEADS, QK_DIM),
                          w["g_kv"], w["w_ukv"], token0)
    g_in, g_oa, g_ob, g_out = _gather_finish(_gather_wait(send0, recv0, bufs, st["o"]))
    wt = g_in.reshape(IN_W, D_MODEL)

    loss, early, st = _local_head(
        st, x[0], loss_target[0], wt, g_oa, w["sgu_ln_g"], w["sgu_ln_b"], w["w_s"], w["b_s"], g_ob,
        g_out.reshape(D_MODEL, D_MODEL), w["ln_g"], w["ln_b"])

    c1 = c.reshape(1).astype(jnp.int32)
    idx = jnp.stack([2 * xi + yi, 2 * (1 - xi) + yi, 2 * xi + (1 - yi), 2 * (1 - xi) + (1 - yi), c]).astype(jnp.int32)
    slabs = lambda a: a.reshape(N_CHIPS, IN_W // N_CHIPS, D_MODEL // 2)
    theirs = slabs(_dwt_early(st["dhmt"], st["dhgt"], st["xb"], 1 - c1, c1, "dwt_theirs"))
    parts1 = [theirs, early["w_oa"].astype(BF16), early["w_ob"].astype(BF16),
              early["w_out"].reshape(N_CHIPS, SLAB_W, D_MODEL).astype(BF16)]
    my_loss = lax.dynamic_update_slice(jnp.zeros((N_DEV,) + LOSS_TILE, F32), jnp.broadcast_to(loss, (1,) + LOSS_TILE),
                                       (4 * xi + 2 * yi + c, 0, 0))
    sems0 = _pairs_start(parts1, my_loss, 1)
    mine = slabs(_dwt_early(st["dhmt"], st["dhgt"], st["xb"], c1, sems0[5], "dwt_mine"))
    parts1, recv1, all_loss = _pairs_wait(*sems0[:5], mine, 1)
    pairs1 = [_add_pair_tiled(mine, recv1[0]), *_add_pair_small(parts1[1:], recv1[1:], c1, "add_pair_early")]
    sems1 = _chips_start(pairs1, "chips_start_early")
    st["delta"] = st["delta"] + sems1[4][0, 0]
    dq, dk, dv = _local_attn_bwd(st)
    dhl, late = _local_tail(st, dq, dk, dv, dv)

    grads = {**early, **late}
    rep = jnp.concatenate([_rows8(grads[n]) for n in REPLICATED], axis=0)
    rep = jnp.pad(rep, ((0, N_CHIPS * REP_ROWS - rep.shape[0]), (0, 0))).reshape(N_CHIPS, REP_ROWS, D_MODEL)
    parts2 = [late["w_uq"].reshape(N_CHIPS, Q_RANK // N_CHIPS, HEADS * QK_DIM).astype(BF16), rep.astype(BF16),
              late["w_lat"].reshape(N_CHIPS, LAT_ROWS_PAD // N_CHIPS, D_MODEL)]
    pairs2 = _add_pair_small(parts2, _exchange_pairs(parts2, "exchange_pairs_late"), c1, "add_pair_late")
    sems2 = _chips_start(pairs2, "chips_start_late")
    dx = _dx(st["dr"], st["dhg"], st["dhm"], dhl, st["wt"], sems2[4])
    pairs2, landed2 = _chips_wait(*sems2[:4], dx, "chips_wait_late")
    pairs1, landed1 = _chips_wait(*sems1[:4], landed2[0], "chips_wait_early")
    sums = [_sum_chips_tiled(pairs1[0], landed1[0], idx, 128),
            *_sum_chips_small([*pairs1[1:], *pairs2], [*landed1[1:], *landed2], idx, (F32, BF16))]
    *shards, g_rep, g_lat = _share(sums[:-2], sums[-2:])
    loss = jnp.sum(all_loss[:, 0, 0])

    red = {n: s.reshape(w[n].shape) for n, s in zip(("w_oa", "w_ob", "w_out", "w_uq"), shards[1:])}
    g_rep = g_rep.reshape(N_CHIPS * REP_ROWS, D_MODEL)
    off = 0
    for n in REPLICATED:
        rows = _rows8(w[n]).shape[0]
        red[n] = g_rep[off:off + rows].reshape(-1)[:w[n].size].reshape(w[n].shape)
        off += rows
    owner = (2 * xi + yi == 0).astype(jnp.int32).reshape(1)
    gt, dt, mt, vt2 = _update_w_in(wt_shard, shards[0], mt_shard, vt_shard,
                                   g_lat.reshape(LAT_ROWS_PAD, D_MODEL).astype(F32), owner, 232)
    red["w_in"] = jnp.transpose(gt)
    small = [n for n in ORDER if n != "w_in"]
    as2d = lambda a: a.reshape(-1, a.shape[-1])
    ds, ms, vs = _update_small([as2d(w[n]) for n in small], [as2d(red[n]) for n in small],
                               [as2d(m[n]) for n in small], [as2d(v[n]) for n in small])
    delta, new_m, new_v = {"w_in": jnp.transpose(dt)}, {"w_in": jnp.transpose(mt)}, {"w_in": jnp.transpose(vt2)}
    for i, n in enumerate(small):
        delta[n], new_m[n], new_v[n] = (a[i].reshape(w[n].shape) for a in (ds, ms, vs))

    lead = lambda a: a[None]
    return (loss, dx[None], *[lead(red[n]) for n in ORDER], *[lead(delta[n]) for n in ORDER],
            *[lead(new_m[n]) for n in ORDER], *[lead(new_v[n]) for n in ORDER])
```

```python
import functools
import math

import jax
import jax.numpy as jnp
from jax import lax
from jax.experimental import pallas as pl
from jax.experimental.pallas import tpu as pltpu

F32 = jnp.float32
BF16 = jnp.bfloat16

D_MODEL = 1024
HEADS = 8
Q_RANK = 384
KV_RANK = 128
NOPE = 64
ROPE = 32
V_DIM = 64
QK_DIM = NOPE + ROPE
HEAD_PAD = 128
MLA_W = HEADS * V_DIM
SGU_W = 512
GROUPS = 8
CHUNK = 128
IN_W = 4640
RMS_EPS = 1e-6
LN_EPS = 1e-5
ALPHA = 2.0 ** 0.25
ROPE_THETA = 10000.0
SCALE = QK_DIM ** -0.5

GATE_W = 2 * D_MODEL
MID_W = 4 * SGU_W
LAT_W = Q_RANK + KV_RANK + HEAD_PAD
PAD_W = GATE_W + MID_W + LAT_W
LAT_COLS = Q_RANK + KV_RANK + ROPE
ROW_GATE = LAT_COLS + MID_W
LAT_ROWS_PAD = 704
N_SLABS = 4
SLAB_W = D_MODEL // N_SLABS

ROW_TILE = 256
MATMUL_ROW_TILE = 512
ATT_TQ = 2048
ATT_TK = 256
ATT_BWD_TQ = 512
ATT_BWD_TK = 256
LOG2E = 1.4426950408889634
LN2 = 0.6931471805599453
Q_SCALE = SCALE * LOG2E
VMEM_LIMIT = 56 * 1024 * 1024

ADAM_LR = 0.001
ADAM_B1 = 0.9
ADAM_B2 = 0.999
ADAM_EPS = 1e-08
ADAM_WD = 0.01
ADAM_STEP = 10


def _dot(a, b):
    return jnp.dot(a, b, preferred_element_type=F32)


def _dot_nt(a, b):
    return lax.dot_general(a, b, (((1,), (1,)), ((), ())), preferred_element_type=F32)


def _dot_tn(a, b):
    return lax.dot_general(a, b, (((0,), (0,)), ((), ())), preferred_element_type=F32)


def _sigmoid(z):
    return 0.5 * jnp.tanh(0.5 * z) + 0.5


_GELU_C = math.sqrt(2.0 / math.pi)


def _gelu_and_grad(x):
    x2 = x * x
    t = jnp.tanh(_GELU_C * (x + 0.044715 * x * x2))
    g = 0.5 * x * (1.0 + t)
    dg = 0.5 * (1.0 + t) + 0.5 * x * (1.0 - t * t) * (_GELU_C * (1.0 + 3.0 * 0.044715 * x2))
    return g, dg


def _silu_and_grad(z):
    s = _sigmoid(z)
    return z * s, s * (1.0 + z * (1.0 - s))


def _rope(xb, c, sl, sh):
    return xb * c + pltpu.roll(xb, 112, 1) * sl + pltpu.roll(xb, 16, 1) * sh


def _rope_t(dy, c, sl, sh):
    return dy * c + pltpu.roll(dy * sl, 16, 1) + pltpu.roll(dy * sh, 112, 1)


def _params(sem=("arbitrary",)):
    return pltpu.CompilerParams(dimension_semantics=sem, vmem_limit_bytes=VMEM_LIMIT)


def _row_spec(tile, width):
    return pl.BlockSpec((tile, width), lambda i: (i, 0))


def _full_spec(shape):
    nd = len(shape)
    return pl.BlockSpec(shape, lambda i: (0,) * nd)


def _kpe_rows(wt_ref):
    z = lambda n: jnp.zeros((n, D_MODEL), BF16)
    return jnp.concatenate([z(NOPE), wt_ref[Q_RANK + KV_RANK:LAT_COLS, :], z(HEAD_PAD - QK_DIM)], axis=0)


def _fwd_rest(xb, wt, b_g, b_m):
    s = xb.shape[0]
    ts = MATMUL_ROW_TILE

    def body(xb_ref, wt_ref, bg_ref, bm_ref, hg_ref, hm_ref):
        xb_ = xb_ref[...]
        hg_ref[...] = _dot_nt(xb_, wt_ref[ROW_GATE:IN_W, :]) + bg_ref[...]
        hm_ref[...] = _dot_nt(xb_, wt_ref[LAT_COLS:ROW_GATE, :]) + bm_ref[...]

    return pl.pallas_call(
        body, name="fwd_rest", grid=(s // ts,),
        in_specs=[_row_spec(ts, D_MODEL), _full_spec(wt.shape), _full_spec(b_g.shape), _full_spec(b_m.shape)],
        out_specs=[_row_spec(ts, GATE_W), _row_spec(ts, MID_W)],
        out_shape=[jax.ShapeDtypeStruct((s, GATE_W), F32), jax.ShapeDtypeStruct((s, MID_W), F32)],
        compiler_params=_params(),
    )(xb, wt, b_g, b_m)


def _fwd_lat(x, wlat, b_l, g_q, wuq, g_kv, wk, wv, rc, rsl, rsh, after):
    s = x.shape[0]
    ts = ROW_TILE

    def body(x_ref, wt_ref, bl_ref, gq_ref, wuq_ref, gkv_ref, wk_ref, wv_ref, rc_ref, rsl_ref,
             rsh_ref, after_ref, hl_ref, q_ref, k_ref, v_ref, xb_ref, qt_ref, kt_ref, vt_ref):
        xb = x_ref[...].astype(BF16)
        xb_ref[...] = xb
        hl = jnp.concatenate([_dot_nt(xb, wt_ref[0:Q_RANK + KV_RANK, :]), _dot_nt(xb, _kpe_rows(wt_ref))],
                             axis=1) + bl_ref[...]
        hl_ref[...] = hl
        c, sl, sh = rc_ref[...], rsl_ref[...], rsh_ref[...]
        cq = hl[:, :Q_RANK]
        cqn = cq * lax.rsqrt(jnp.mean(cq * cq, axis=-1, keepdims=True) + RMS_EPS) * gq_ref[...]
        q = _dot(cqn.astype(BF16), wuq_ref[...])
        ckv = hl[:, Q_RANK:Q_RANK + KV_RANK]
        ckvn = (ckv * lax.rsqrt(jnp.mean(ckv * ckv, axis=-1, keepdims=True) + RMS_EPS) * gkv_ref[...]).astype(BF16)
        k = _dot(ckvn, wk_ref[...])
        vb = _dot(ckvn, wv_ref[...]).astype(BF16)
        v_ref[...] = vb
        vt_ref[...] = vb.T
        kpe = _rope(hl[:, Q_RANK + KV_RANK:], c, sl, sh)
        for hd in range(HEADS):
            lanes = slice(hd * HEAD_PAD, (hd + 1) * HEAD_PAD)
            qb = (_rope(q[:, lanes], c, sl, sh) * Q_SCALE).astype(BF16)
            kb = (k[:, lanes] + kpe).astype(BF16)
            q_ref[:, lanes] = qb
            k_ref[:, lanes] = kb
            qt_ref[lanes, :] = qb.T
            kt_ref[lanes, :] = kb.T

    qk_w = HEADS * HEAD_PAD
    col_spec = lambda rows: pl.BlockSpec((rows, ts), lambda i: (0, i))
    return pl.pallas_call(
        body, name="fwd_lat", grid=(s // ts,),
        in_specs=[_row_spec(ts, D_MODEL), _full_spec(wlat.shape),
                  _full_spec(b_l.shape), _full_spec(g_q.shape),
                  _full_spec(wuq.shape), _full_spec(g_kv.shape), _full_spec(wk.shape), _full_spec(wv.shape),
                  _row_spec(ts, HEAD_PAD), _row_spec(ts, HEAD_PAD), _row_spec(ts, HEAD_PAD),
                  pl.BlockSpec(memory_space=pl.ANY)],
        out_specs=[_row_spec(ts, LAT_W), _row_spec(ts, qk_w),
                   _row_spec(ts, qk_w), _row_spec(ts, MLA_W), _row_spec(ts, D_MODEL), col_spec(qk_w), col_spec(qk_w),
                   col_spec(MLA_W)],
        out_shape=[jax.ShapeDtypeStruct((s, LAT_W), F32), jax.ShapeDtypeStruct((s, qk_w), BF16),
                   jax.ShapeDtypeStruct((s, qk_w), BF16), jax.ShapeDtypeStruct((s, MLA_W), BF16),
                   jax.ShapeDtypeStruct((s, D_MODEL), BF16), jax.ShapeDtypeStruct((qk_w, s), BF16),
                   jax.ShapeDtypeStruct((qk_w, s), BF16), jax.ShapeDtypeStruct((MLA_W, s), BF16)],
        compiler_params=_params(),
    )(x, wlat, b_l, g_q, wuq, g_kv, wk, wv, rc, rsl, rsh, after)


def _attn_fwd(qt, k, vt):
    s = k.shape[0]
    tq, tk = ATT_TQ, ATT_TK
    r = tq // tk
    pairs = HEADS // 2

    def body(qt_ref, k_ref, vt_ref, o_ref, lse_ref):
        i = pl.program_id(1)
        qts = [qt_ref[hh * HEAD_PAD:(hh + 1) * HEAD_PAD, :] for hh in range(2)]

        def scores(j, lo):
            koff = pl.multiple_of(j * tk, tk)
            return tuple(_dot(k_ref[pl.ds(koff, tk), hh * HEAD_PAD:(hh + 1) * HEAD_PAD], qts[hh][:, lo:])
                         for hh in range(2))

        def weighted(j, ps):
            koff = pl.multiple_of(j * tk, tk)
            return tuple(_dot(vt_ref[hh * V_DIM:(hh + 1) * V_DIM, pl.ds(koff, tk)], ps[hh]) for hh in range(2))

        def from_lane(full, lo, part):
            return part if lo == 0 else jnp.concatenate([full[:, :lo], part], axis=1)

        def step(j, carry, diag, last):
            st, ps, stats = carry
            lo = 0 if diag is None else diag * tk
            lo_prev = 0 if not diag else (diag - 1) * tk
            st_next = None if last else scores(j + 1, 0 if diag is None else lo + tk)
            pvs = weighted(jnp.maximum(j - 1, 0), ps)
            new_ps, new_stats = [], []
            for hh in range(2):
                m, l, acc = stats[hh]
                s_ = st[hh]
                if diag is not None:
                    krow = lax.broadcasted_iota(jnp.int32, s_.shape, 0)
                    qcol = lax.broadcasted_iota(jnp.int32, s_.shape, 1)
                    s_ = jnp.where(krow <= qcol, s_, -jnp.inf)
                acc = from_lane(acc, lo_prev, acc[:, lo_prev:] + pvs[hh])
                m_old = m[:, lo:]
                m_new = jnp.maximum(m_old, jnp.max(s_, axis=0, keepdims=True))
                a = jnp.exp2(m_old - m_new)
                p = jnp.exp2(s_ - m_new)
                new_stats.append((from_lane(m, lo, m_new),
                                  from_lane(l, lo, a * l[:, lo:] + jnp.sum(p, axis=0, keepdims=True)),
                                  from_lane(acc, lo, a * acc[:, lo:])))
                new_ps.append(p.astype(BF16))
            return st_next, tuple(new_ps), tuple(new_stats)

        one = (jnp.full((1, tq), -jnp.inf, F32), jnp.zeros((1, tq), F32), jnp.zeros((V_DIM, tq), F32))
        zero_p = jnp.zeros((tk, tq), BF16)
        nfull = i * r
        carry = lax.fori_loop(0, nfull, functools.partial(step, diag=None, last=False),
                              (scores(0, 0), (zero_p, zero_p), (one, one)))
        for d in range(r):
            carry = step(nfull + d, carry, d, d == r - 1)
        _, ps, stats = carry
        pvs = weighted(nfull + r - 1, ps)
        lo = (r - 1) * tk
        ot = jnp.concatenate([from_lane(stats[hh][2], lo, stats[hh][2][:, lo:] + pvs[hh]) / stats[hh][1]
                              for hh in range(2)], axis=0)
        o_ref[...] = ot.T
        lse = [stats[hh][0] + jnp.log(stats[hh][1]) * LOG2E for hh in range(2)]
        lse_ref[...] = jnp.concatenate(lse + [jnp.zeros((6, tq), F32)], axis=0)

    return pl.pallas_call(
        body, name="attn_fwd", grid=(pairs, s // tq),
        in_specs=[pl.BlockSpec((2 * HEAD_PAD, tq), lambda p, i: (p, i)),
                  pl.BlockSpec((s, 2 * HEAD_PAD), lambda p, i: (0, p)),
                  pl.BlockSpec((2 * V_DIM, s), lambda p, i: (p, 0))],
        out_specs=[pl.BlockSpec((tq, 2 * V_DIM), lambda p, i: (i, p)),
                   pl.BlockSpec((None, 8, tq), lambda p, i: (p, 0, i))],
        out_shape=[jax.ShapeDtypeStruct((s, MLA_W), F32), jax.ShapeDtypeStruct((pairs, 8, s), F32)],
        compiler_params=_params(("arbitrary", "arbitrary")),
    )(qt, k, vt)


def _attn_bwd(q, qt, k, kt, v, do, dot, lse, delta):
    s = k.shape[0]
    tq, tk = ATT_BWD_TQ, ATT_BWD_TK
    r = tq // tk
    nq = s // tq
    nk = s // tk
    pairs = HEADS // 2

    def body(q_ref, qt_ref, k_ref, kt_ref, v_ref, do_ref, dot_ref, lse_ref, dl_ref, dqt_ref, dk_ref, dv_ref):
        j = pl.program_id(1)
        krow = lax.broadcasted_iota(jnp.int32, (tk, tq), 0)
        qcol = lax.broadcasted_iota(jnp.int32, (tk, tq), 1)
        lane = lax.broadcasted_iota(jnp.int32, (tk, 2 * V_DIM), 1)
        drow = lax.broadcasted_iota(jnp.int32, (2 * V_DIM, tq), 0)

        @pl.when(j == 0)
        def _():
            dqt_ref[...] = jnp.zeros_like(dqt_ref)

        koff = pl.multiple_of(j * tk, tk)
        vb = v_ref[pl.ds(koff, tk), :]
        kbs = [k_ref[pl.ds(koff, tk), hh * HEAD_PAD:(hh + 1) * HEAD_PAD] for hh in range(2)]
        ktbs = [kt_ref[hh * HEAD_PAD:(hh + 1) * HEAD_PAD, pl.ds(koff, tk)] for hh in range(2)]
        i0 = j // r

        def front(i):
            qoff = pl.multiple_of(i * tq, tq)
            dotb = dot_ref[:, pl.ds(qoff, tq)]
            out = []
            for hh in range(2):
                mine = (drow < V_DIM) if hh == 0 else (drow >= V_DIM)
                st = _dot(kbs[hh], qt_ref[hh * HEAD_PAD:(hh + 1) * HEAD_PAD, pl.ds(qoff, tq)])
                out.append((st, _dot(vb, jnp.where(mine, dotb, jnp.zeros_like(dotb)))))
            return tuple(out)

        def middle(i, tiles, diag):
            qoff = pl.multiple_of(i * tq, tq)
            out = []
            for hh in range(2):
                st, dpt = tiles[hh]
                if diag:
                    st = jnp.where(krow + (j - i0 * r) * tk <= qcol, st, -jnp.inf)
                p = jnp.exp2(st - lse_ref[hh:hh + 1, pl.ds(qoff, tq)])
                out.append((p.astype(BF16), (p * (dpt - dl_ref[hh:hh + 1, pl.ds(qoff, tq)])).astype(BF16)))
            return tuple(out)

        def back(i, pd, accs):
            qoff = pl.multiple_of(i * tq, tq)
            dob = do_ref[pl.ds(qoff, tq), :]
            out = []
            for hh in range(2):
                rows = slice(hh * HEAD_PAD, (hh + 1) * HEAD_PAD)
                p, dst = pd[hh]
                dk_acc, dv_acc = accs[hh]
                dv_acc = dv_acc + _dot(p, dob)
                dk_acc = dk_acc + _dot(dst, q_ref[pl.ds(qoff, tq), rows])
                dqt_ref[rows, pl.ds(qoff, tq)] += _dot(ktbs[hh], dst)
                out.append((dk_acc, dv_acc))
            return tuple(out)

        def step(i, accs, diag):
            return back(i, middle(i, front(i), diag), accs)

        zero_acc = (jnp.zeros((tk, HEAD_PAD), F32), jnp.zeros((tk, 2 * V_DIM), F32))
        accs = step(i0, (zero_acc, zero_acc), True)
        accs = lax.fori_loop(i0 + 1, nq, functools.partial(step, diag=False), accs)
        for hh in range(2):
            dk_ref[:, hh * HEAD_PAD:(hh + 1) * HEAD_PAD] = accs[hh][0] * LN2
        dv_ref[...] = jnp.where(lane < V_DIM, accs[0][1], accs[1][1])

        @pl.when(j == nk - 1)
        def _():
            dqt_ref[...] = dqt_ref[...] * SCALE

    pair_rows = lambda w: pl.BlockSpec((s, w), lambda p, j: (0, p))
    pair_cols = lambda w: pl.BlockSpec((w, s), lambda p, j: (p, 0))
    stats = pl.BlockSpec((None, 8, s), lambda p, j: (p, 0, 0))
    return pl.pallas_call(
        body, name="attn_bwd", grid=(pairs, nk),
        in_specs=[pair_rows(2 * HEAD_PAD), pair_cols(2 * HEAD_PAD), pair_rows(2 * HEAD_PAD), pair_cols(2 * HEAD_PAD),
                  pair_rows(2 * V_DIM), pair_rows(2 * V_DIM), pair_cols(2 * V_DIM), stats, stats],
        out_specs=[pair_cols(2 * HEAD_PAD),
                   pl.BlockSpec((tk, 2 * HEAD_PAD), lambda p, j: (j, p)),
                   pl.BlockSpec((tk, 2 * V_DIM), lambda p, j: (j, p))],
        out_shape=[jax.ShapeDtypeStruct((HEADS * HEAD_PAD, s), F32), jax.ShapeDtypeStruct((s, HEADS * HEAD_PAD), F32),
                   jax.ShapeDtypeStruct((s, MLA_W), F32)],
        compiler_params=_params(("arbitrary", "arbitrary")),
    )(q, qt, k, kt, v, do, dot, lse, delta)


def _split3(a):
    hi = a.astype(BF16)
    r1 = a - hi.astype(F32)
    mid = r1.astype(BF16)
    lo = (r1 - mid.astype(F32)).astype(BF16)
    return hi, mid, lo


def _mid(x, tgt, o, hm, hg, woa, wob, wout, ln_g, ln_b, sg_g, sg_b, w_s, bsb):
    s = x.shape[0]
    ts = ROW_TILE
    nsteps = s // ts
    nch = ts // CHUNK
    npair = GROUPS // 2

    def body(x_ref, t_ref, o_ref, hm_ref, hg_ref, woa_ref, wob_ref, wout_ref, lng_ref, lnb_ref, sgg_ref, sgb_ref,
             ws_ref, bsb_ref,
             dr_ref, dhg_ref, dhm_ref, do_ref, dot_ref, dl_ref, dhgt_ref, dhmt_ref,
             dwout_ref, dwoa_ref, dwob_ref, dws_ref, dbs_ref, dlng_ref, dlnb_ref, dsgg_ref, dsgb_ref, loss_ref,
             dbg_ref, dbm_ref, dbacc_ref):
        i = pl.program_id(0)

        @pl.when(i == 0)
        def _():
            for r in (dwout_ref, dwoa_ref, dwob_ref, dws_ref, dlng_ref, dlnb_ref, dsgg_ref, dsgb_ref, loss_ref,
                      dbg_ref, dbm_ref, dbacc_ref):
                r[...] = jnp.zeros_like(r)

        def emit(ref, tref, bref, lo, val):
            vb = val.astype(BF16)
            n = val.shape[1]
            ref[:, lo:lo + n] = vb
            tref[lo:lo + n, :] = vb.T
            bref[:, lo:lo + n] += jnp.sum(val, axis=0, keepdims=True)

        lane = lax.broadcasted_iota(jnp.int32, (CHUNK, CHUNK), 1)
        left = lane < V_DIM
        tril = lax.broadcasted_iota(jnp.int32, (CHUNK, CHUNK), 0) >= lane
        ms = [jnp.where(tril, ws_ref[g], 0.0).astype(BF16) for g in range(GROUPS)]

        z_a = hm_ref[:, 0:SGU_W]
        u = hm_ref[:, SGU_W:2 * SGU_W]
        v = hm_ref[:, 2 * SGU_W:3 * SGU_W]
        z_b = hm_ref[:, 3 * SGU_W:4 * SGU_W]
        o = o_ref[...]
        sa, dsa = _silu_and_grad(z_a)
        y_a = (o * sa).astype(BF16)
        gu, dgu = _gelu_and_grad(u)
        gv, dgv = _gelu_and_grad(v)
        mu = jnp.mean(gv, axis=-1, keepdims=True)
        vc = gv - mu
        rstd_v = lax.rsqrt(jnp.mean(vc * vc, axis=-1, keepdims=True) + LN_EPS)
        vhat = vc * rstd_v
        vn = (vhat * sgg_ref[...] + sgb_ref[...]).astype(BF16)
        rows = []
        for c in range(nch):
            blocks = []
            for p in range(npair):
                blk = vn[c * CHUNK:(c + 1) * CHUNK, p * CHUNK:(p + 1) * CHUNK]
                blocks.append(jnp.where(left, _dot(ms[2 * p], blk), _dot(ms[2 * p + 1], blk)))
            rows.append(jnp.concatenate(blocks, axis=1) + bsb_ref[...])
        mixed = jnp.concatenate(rows, axis=0)
        sgu = gu * mixed
        sb, dsb = _silu_and_grad(z_b)
        y_b = (sgu * sb).astype(BF16)
        pa = jnp.concatenate([_dot(y_a, woa_ref[k]) for k in range(N_SLABS)], axis=1)
        pb = jnp.concatenate([_dot(y_b, wob_ref[k]) for k in range(N_SLABS)], axis=1)
        sga = _sigmoid(hg_ref[:, :D_MODEL])
        sgb = _sigmoid(hg_ref[:, D_MODEL:])
        m2 = (sga * pa + sgb * pb).astype(BF16)
        r = ALPHA * x_ref[...] + _dot(m2, wout_ref[...])
        rmu = jnp.mean(r, axis=-1, keepdims=True)
        rc = r - rmu
        rstd = lax.rsqrt(jnp.mean(rc * rc, axis=-1, keepdims=True) + LN_EPS)
        xhat = rc * rstd
        y = xhat * lng_ref[...] + lnb_ref[...]
        err = y - t_ref[...]
        loss_ref[...] += jnp.full(loss_ref.shape, 0.5 / D_MODEL, F32) * jnp.sum(err * err)

        dy = err * (1.0 / D_MODEL)
        dlng_ref[...] += jnp.sum(dy * xhat, axis=0, keepdims=True)
        dlnb_ref[...] += jnp.sum(dy, axis=0, keepdims=True)
        dxh = dy * lng_ref[...]
        dr = rstd * (dxh - jnp.mean(dxh, axis=-1, keepdims=True) - xhat * jnp.mean(dxh * xhat, axis=-1, keepdims=True))
        dr_ref[...] = dr
        drb = dr.astype(BF16)
        dwout_ref[...] += _dot_tn(m2, drb)
        dm2 = _dot_nt(drb, wout_ref[...])
        emit(dhg_ref, dhgt_ref, dbg_ref, 0, dm2 * pa * sga * (1.0 - sga))
        emit(dhg_ref, dhgt_ref, dbg_ref, D_MODEL, dm2 * pb * sgb * (1.0 - sgb))
        dpa = (dm2 * sga).astype(BF16)
        dpb = (dm2 * sgb).astype(BF16)
        dy_a = jnp.zeros((ts, MLA_W), F32)
        dy_b = jnp.zeros((ts, SGU_W), F32)
        y_at, y_bt = y_a.T, y_b.T
        for k in range(N_SLABS):
            cols = slice(k * SLAB_W, (k + 1) * SLAB_W)
            dwoa_ref[k] += _dot(y_at, dpa[:, cols])
            dwob_ref[k] += _dot(y_bt, dpb[:, cols])
            dy_a = dy_a + _dot_nt(dpa[:, cols], woa_ref[k])
            dy_b = dy_b + _dot_nt(dpb[:, cols], wob_ref[k])
        dob = (dy_a * sa).astype(BF16)
        do_ref[...] = dob
        dot_ref[...] = dob.T
        head = (lax.broadcasted_iota(jnp.int32, (HEADS, MLA_W), 1) // V_DIM
                == lax.broadcasted_iota(jnp.int32, (HEADS, MLA_W), 0)).astype(BF16)
        dl_ref[...] = sum(_dot_nt(head, term) for term in _split3(dob.astype(F32) * o))
        emit(dhm_ref, dhmt_ref, dbm_ref, 0, dy_a * o * dsa)
        dsg = dy_b * sb
        emit(dhm_ref, dhmt_ref, dbm_ref, 3 * SGU_W, dy_b * sgu * dsb)
        emit(dhm_ref, dhmt_ref, dbm_ref, SGU_W, dsg * mixed * dgu)
        dmixed = dsg * gu
        dvn_rows = []
        dbs_sum = jnp.zeros((CHUNK, SGU_W), F32)
        for c in range(nch):
            dm_c = dmixed[c * CHUNK:(c + 1) * CHUNK, :]
            dbs_sum = dbs_sum + dm_c
            blocks = []
            for p in range(npair):
                dmb = dm_c[:, p * CHUNK:(p + 1) * CHUNK].astype(BF16)
                blk = vn[c * CHUNK:(c + 1) * CHUNK, p * CHUNK:(p + 1) * CHUNK]
                blocks.append(jnp.where(left, _dot_tn(ms[2 * p], dmb), _dot_tn(ms[2 * p + 1], dmb)))
                zero = jnp.zeros_like(dmb)
                dws_ref[2 * p] += jnp.where(tril, _dot_nt(jnp.where(left, dmb, zero), blk), 0.0)
                dws_ref[2 * p + 1] += jnp.where(tril, _dot_nt(jnp.where(left, zero, dmb), blk), 0.0)
            dvn_rows.append(jnp.concatenate(blocks, axis=1))
        dbacc_ref[...] += dbs_sum
        dvn = jnp.concatenate(dvn_rows, axis=0)
        dsgg_ref[...] += jnp.sum(dvn * vhat, axis=0, keepdims=True)
        dsgb_ref[...] += jnp.sum(dvn, axis=0, keepdims=True)
        dvh = dvn * sgg_ref[...]
        dgv_in = rstd_v * (dvh - jnp.mean(dvh, axis=-1, keepdims=True)
                           - vhat * jnp.mean(dvh * vhat, axis=-1, keepdims=True))
        emit(dhm_ref, dhmt_ref, dbm_ref, 2 * SGU_W, dgv_in * dgv)

        @pl.when(i == nsteps - 1)
        def _():
            grp = (lax.broadcasted_iota(jnp.int32, (SGU_W, CHUNK), 0) // V_DIM
                   == lax.broadcasted_iota(jnp.int32, (SGU_W, CHUNK), 1)).astype(BF16)
            hi, mid, lo = _split3(dbacc_ref[...])
            dbs_ref[...] = _dot(hi, grp) + _dot(mid, grp) + _dot(lo, grp)

    acc_shapes = [(D_MODEL, D_MODEL), woa.shape, wob.shape, (GROUPS, CHUNK, CHUNK), (CHUNK, CHUNK),
                  (1, D_MODEL), (1, D_MODEL), (1, SGU_W), (1, SGU_W), (1, 128), (1, GATE_W), (1, MID_W)]
    col_spec = lambda rows: pl.BlockSpec((rows, ts), lambda i: (0, i))
    return pl.pallas_call(
        body, name="mid", grid=(nsteps,),
        in_specs=[_row_spec(ts, D_MODEL), _row_spec(ts, D_MODEL), _row_spec(ts, MLA_W), _row_spec(ts, MID_W),
                  _row_spec(ts, GATE_W), _full_spec(woa.shape), _full_spec(wob.shape), _full_spec(wout.shape),
                  _full_spec(ln_g.shape), _full_spec(ln_b.shape), _full_spec(sg_g.shape), _full_spec(sg_b.shape),
                  _full_spec(w_s.shape), _full_spec(bsb.shape)],
        out_specs=[_row_spec(ts, D_MODEL), _row_spec(ts, GATE_W), _row_spec(ts, MID_W), _row_spec(ts, MLA_W),
                   col_spec(MLA_W), col_spec(HEADS), col_spec(GATE_W), col_spec(MID_W)]
        + [_full_spec(sh) for sh in acc_shapes],
        out_shape=[jax.ShapeDtypeStruct((s, D_MODEL), F32), jax.ShapeDtypeStruct((s, GATE_W), BF16),
                   jax.ShapeDtypeStruct((s, MID_W), BF16), jax.ShapeDtypeStruct((s, MLA_W), BF16),
                   jax.ShapeDtypeStruct((MLA_W, s), BF16), jax.ShapeDtypeStruct((HEADS, s), F32),
                   jax.ShapeDtypeStruct((GATE_W, s), BF16), jax.ShapeDtypeStruct((MID_W, s), BF16)]
        + [jax.ShapeDtypeStruct(sh, F32) for sh in acc_shapes],
        scratch_shapes=[pltpu.VMEM((CHUNK, SGU_W), F32)],
        compiler_params=_params(),
    )(x, tgt, o, hm, hg, woa, wob, wout, ln_g, ln_b, sg_g, sg_b, w_s, bsb)


def _lat_bwd(dq, dk, dv, hl, rc, rsl, rsh, g_q, g_kv, wuq, wk, wv, after):
    s = dk.shape[0]
    ts = ROW_TILE
    qk_w = HEADS * HEAD_PAD

    def body(dq_ref, dk_ref, dv_ref, hl_ref, rc_ref, rsl_ref, rsh_ref, gq_ref, gkv_ref, wuq_ref, wk_ref, wv_ref,
             after_ref, dhl_ref, dhlt_ref, dwuq_ref, dwk_ref, dwv_ref, dgq_ref, dgkv_ref, dbl_ref):
        i = pl.program_id(0)

        @pl.when(i == 0)
        def _():
            for r in (dwuq_ref, dwk_ref, dwv_ref, dgq_ref, dgkv_ref, dbl_ref):
                r[...] = jnp.zeros_like(r)

        def emit(lo, val):
            vb = val.astype(BF16)
            n = val.shape[1]
            dhl_ref[:, lo:lo + n] = vb
            dhlt_ref[lo:lo + n, :] = vb.T
            dbl_ref[:, lo:lo + n] += jnp.sum(val, axis=0, keepdims=True)

        c, sl, sh = rc_ref[...], rsl_ref[...], rsh_ref[...]
        lane = lax.broadcasted_iota(jnp.int32, (ts, HEAD_PAD), 1)
        pe = (lane >= NOPE) & (lane < QK_DIM)
        dkpe = jnp.zeros((ts, HEAD_PAD), F32)
        dqu = []
        for hd in range(HEADS):
            lanes = slice(hd * HEAD_PAD, (hd + 1) * HEAD_PAD)
            dqu.append(_rope_t(dq_ref[lanes, :].T, c, sl, sh).astype(BF16))
            dkpe = dkpe + dk_ref[:, lanes]
        dqu = jnp.concatenate(dqu, axis=1)
        dkpe = _rope_t(jnp.where(pe, dkpe, 0.0), c, sl, sh)

        cq = hl_ref[:, :Q_RANK]
        rq = lax.rsqrt(jnp.mean(cq * cq, axis=-1, keepdims=True) + RMS_EPS)
        cqh = cq * rq
        cqn = (cqh * gq_ref[...]).astype(BF16)
        dwuq_ref[...] += _dot_tn(cqn, dqu)
        dcqn = _dot_nt(dqu, wuq_ref[...])
        dgq_ref[...] += jnp.sum(dcqn * cqh, axis=0, keepdims=True)
        dch = dcqn * gq_ref[...]
        emit(0, rq * (dch - cqh * jnp.mean(dch * cqh, axis=-1, keepdims=True)))

        ckv = hl_ref[:, Q_RANK:Q_RANK + KV_RANK]
        rk = lax.rsqrt(jnp.mean(ckv * ckv, axis=-1, keepdims=True) + RMS_EPS)
        ckh = ckv * rk
        ckn = (ckh * gkv_ref[...]).astype(BF16)
        dkb = dk_ref[...].astype(BF16)
        dvb = dv_ref[...].astype(BF16)
        dwk_ref[...] += _dot_tn(ckn, dkb)
        dwv_ref[...] += _dot_tn(ckn, dvb)
        dckn = _dot_nt(dkb, wk_ref[...]) + _dot_nt(dvb, wv_ref[...])
        dgkv_ref[...] += jnp.sum(dckn * ckh, axis=0, keepdims=True)
        dkh = dckn * gkv_ref[...]
        emit(Q_RANK, rk * (dkh - ckh * jnp.mean(dkh * ckh, axis=-1, keepdims=True)))
        emit(Q_RANK + KV_RANK, dkpe)

    acc_shapes = [wuq.shape, wk.shape, wv.shape, g_q.shape, g_kv.shape, (1, LAT_W)]
    return pl.pallas_call(
        body, name="lat_bwd", grid=(s // ts,),
        in_specs=[pl.BlockSpec((qk_w, ts), lambda i: (0, i)), _row_spec(ts, qk_w), _row_spec(ts, MLA_W),
                  _row_spec(ts, LAT_W), _row_spec(ts, HEAD_PAD), _row_spec(ts, HEAD_PAD), _row_spec(ts, HEAD_PAD),
                  _full_spec(g_q.shape), _full_spec(g_kv.shape), _full_spec(wuq.shape), _full_spec(wk.shape),
                  _full_spec(wv.shape), pl.BlockSpec(memory_space=pl.ANY)],
        out_specs=[_row_spec(ts, LAT_W), pl.BlockSpec((LAT_W, ts), lambda i: (0, i))]
        + [_full_spec(sh) for sh in acc_shapes],
        out_shape=[jax.ShapeDtypeStruct((s, LAT_W), BF16), jax.ShapeDtypeStruct((LAT_W, s), BF16)]
        + [jax.ShapeDtypeStruct(sh, F32) for sh in acc_shapes],
        compiler_params=_params(),
    )(dq, dk, dv, hl, rc, rsl, rsh, g_q, g_kv, wuq, wk, wv, after)


def _dx(dr, dhg, dhm, dhl, wt, after):
    s = dr.shape[0]
    ts = MATMUL_ROW_TILE

    def body(dr_ref, dhg_ref, dhm_ref, dhl_ref, wt_ref, after_ref, dx_ref):
        dx_ref[...] = (ALPHA * dr_ref[...]
                       + _dot(dhg_ref[...], wt_ref[ROW_GATE:IN_W, :])
                       + _dot(dhm_ref[...], wt_ref[LAT_COLS:ROW_GATE, :])
                       + _dot(dhl_ref[:, 0:Q_RANK + KV_RANK], wt_ref[0:Q_RANK + KV_RANK, :])
                       + _dot(dhl_ref[:, Q_RANK + KV_RANK:], _kpe_rows(wt_ref)))

    return pl.pallas_call(
        body, name="dx", grid=(s // ts,),
        in_specs=[_row_spec(ts, D_MODEL), _row_spec(ts, GATE_W), _row_spec(ts, MID_W), _row_spec(ts, LAT_W),
                  _full_spec(wt.shape), pl.BlockSpec(memory_space=pl.ANY)],
        out_specs=_row_spec(ts, D_MODEL),
        out_shape=jax.ShapeDtypeStruct((s, D_MODEL), F32),
        compiler_params=_params(),
    )(dr, dhg, dhm, dhl, wt, after)


def _dwt_early(dhmt, dhgt, xb, col, after, name):
    tn = 512
    nm, ng = MID_W // tn, GATE_W // tn
    s = dhmt.shape[1]
    hc = D_MODEL // 2

    def body(col_ref, dm_ref, dg_ref, xb_ref, after_ref, dw_ref):
        i = pl.program_id(0)

        @pl.when(i < nm)
        def _():
            dw_ref[...] = _dot(dm_ref[...], xb_ref[...]).astype(BF16)

        @pl.when(i >= nm)
        def _():
            dw_ref[...] = _dot(dg_ref[...], xb_ref[...]).astype(BF16)

    rows = pl.pallas_call(
        body, name=name,
        grid_spec=pltpu.PrefetchScalarGridSpec(
            num_scalar_prefetch=1, grid=(nm + ng,),
            in_specs=[pl.BlockSpec((tn, s), lambda i, col_ref: (jnp.minimum(i, nm - 1), 0)),
                      pl.BlockSpec((tn, s), lambda i, col_ref: (jnp.maximum(i - nm, 0), 0)),
                      pl.BlockSpec((s, hc), lambda i, col_ref: (0, col_ref[0])),
                      pl.BlockSpec(memory_space=pl.ANY)],
            out_specs=pl.BlockSpec((pl.Element(tn), pl.Element(hc)),
                                   lambda i, col_ref: (pl.multiple_of(LAT_COLS + i * tn, 32), 0))),
        out_shape=jax.ShapeDtypeStruct((IN_W, hc), BF16),
        compiler_params=_params(),
    )(col, dhmt, dhgt, xb, after)

    def zero(buf_ref, out_ref):
        out_ref[...] = jnp.zeros_like(out_ref)

    return pl.pallas_call(
        zero, name=name + "_zero_lat", grid=(1,), in_specs=[pl.BlockSpec(memory_space=pl.ANY)],
        out_specs=pl.BlockSpec((LAT_COLS, hc), lambda i: (0, 0)),
        out_shape=jax.ShapeDtypeStruct((IN_W, hc), BF16), input_output_aliases={0: 0},
    )(rows)


def _dwt_lat(dhlt, xb):
    n, s = dhlt.shape

    def body(dht_ref, xb_ref, dw_ref):
        dw = _dot(dht_ref[...], xb_ref[...]).astype(BF16)
        kpe = Q_RANK + KV_RANK + NOPE
        dw_ref[0:Q_RANK + KV_RANK, :] = dw[0:Q_RANK + KV_RANK]
        dw_ref[Q_RANK + KV_RANK:LAT_COLS, :] = dw[kpe:kpe + ROPE]
        dw_ref[LAT_COLS:, :] = jnp.zeros((LAT_ROWS_PAD - LAT_COLS, D_MODEL), BF16)

    return pl.pallas_call(
        body, name="dwt_lat", in_specs=[VMEM_SPEC, VMEM_SPEC], out_specs=VMEM_SPEC,
        out_shape=jax.ShapeDtypeStruct((LAT_ROWS_PAD, D_MODEL), BF16),
        compiler_params=pltpu.CompilerParams(vmem_limit_bytes=VMEM_LIMIT),
    )(dhlt, xb)


def _split_bias(b):
    z = lambda n: jnp.zeros((n,), b.dtype)
    lat = jnp.concatenate([b[:Q_RANK + KV_RANK], z(NOPE), b[Q_RANK + KV_RANK:LAT_COLS], z(HEAD_PAD - QK_DIM)])
    return b[None, ROW_GATE:], b[None, LAT_COLS:ROW_GATE], lat[None, :]


def _join_bias(g, m, l):
    kpe = Q_RANK + KV_RANK + NOPE
    return jnp.concatenate([l[0, :Q_RANK + KV_RANK], l[0, kpe:kpe + ROPE], m[0], g[0]])


def _rope_tables(positions):
    half = ROPE // 2
    inv_freq = ROPE_THETA ** (-jnp.arange(0, ROPE, 2, dtype=F32) / ROPE)
    ang = positions.astype(F32)[:, None] * inv_freq
    cos, sin = jnp.cos(ang), jnp.sin(ang)
    n = positions.shape[0]
    one, zero = jnp.ones((n, NOPE), F32), jnp.zeros((n, half), F32)
    tail1, tail0 = jnp.ones((n, HEAD_PAD - QK_DIM), F32), jnp.zeros((n, HEAD_PAD - QK_DIM), F32)
    z64 = jnp.zeros((n, NOPE), F32)
    rc = jnp.concatenate([one, cos, cos, tail1], axis=1)
    rsl = jnp.concatenate([z64, -sin, zero, tail0], axis=1)
    rsh = jnp.concatenate([z64, zero, sin, tail0], axis=1)
    return rc, rsl, rsh


def _local_attention(x, positions, wlat, b_in, g_q, w_uq, g_kv, w_ukv, after):
    rc, rsl, rsh = _rope_tables(positions)
    b_g, b_m, b_l = _split_bias(b_in)
    wuq = jnp.pad(w_uq, ((0, 0), (0, 0), (0, HEAD_PAD - QK_DIM))).reshape(Q_RANK, HEADS * HEAD_PAD).astype(BF16)
    wk = jnp.pad(w_ukv[:, :, :NOPE], ((0, 0), (0, 0), (0, HEAD_PAD - NOPE))).reshape(KV_RANK, HEADS * HEAD_PAD).astype(BF16)
    wv = w_ukv[:, :, NOPE:].reshape(KV_RANK, MLA_W).astype(BF16)
    gq2, gkv2 = g_q[None, :], g_kv[None, :]
    hl, q, k, v, xb, qt, kt, vt = _fwd_lat(x, wlat, b_l, gq2, wuq, gkv2, wk, wv, rc, rsl, rsh, after)
    o, lse = _attn_fwd(qt, k, vt)
    return dict(q=q, qt=qt, k=k, kt=kt, v=v, o=o, lse=lse, hl=hl, rc=rc, rsl=rsl, rsh=rsh, gq2=gq2, gkv2=gkv2,
                wuq=wuq, wk=wk, wv=wv, xb=xb, b_g=b_g, b_m=b_m)


def _local_head(st, x, tgt, wt, w_oa, sg_g, sg_b, w_s, b_s, w_ob, w_out, ln_g, ln_b):
    q, qt, k, kt, v, o, lse, hl, xb = (st[n] for n in ("q", "qt", "k", "kt", "v", "o", "lse", "hl", "xb"))
    rc, rsl, rsh, gq2, gkv2, wuq, wk, wv = (st[n] for n in ("rc", "rsl", "rsh", "gq2", "gkv2", "wuq", "wk", "wv"))
    bsb = jnp.repeat(b_s.T, V_DIM, axis=1)
    hg, hm = _fwd_rest(xb, wt, st["b_g"], st["b_m"])
    (dr, dhg, dhm, do, dot, delta, dhgt, dhmt, dwout, dwoa, dwob, dws, dbs, dlng, dlnb, dsgg, dsgb, loss, dbg,
     dbm) = _mid(x, tgt, o, hm, hg, w_oa, w_ob, w_out, ln_g[None, :], ln_b[None, :], sg_g[None, :], sg_b[None, :],
                 w_s, bsb)
    delta = jnp.pad(delta.reshape(HEADS // 2, 2, -1), ((0, 0), (0, 6), (0, 0)))
    early = {
        "w_oa": dwoa, "sgu_ln_g": dsgg[0], "sgu_ln_b": dsgb[0], "w_s": dws, "b_s": dbs[:, :GROUPS].T,
        "w_ob": dwob, "w_out": dwout, "ln_g": dlng[0], "ln_b": dlnb[0],
    }
    state = dict(q=q, qt=qt, k=k, kt=kt, v=v, do=do, dot=dot, lse=lse, delta=delta, hl=hl, rc=rc, rsl=rsl, rsh=rsh,
                 gq2=gq2, gkv2=gkv2, wuq=wuq, wk=wk, wv=wv, dr=dr, dhg=dhg, dhm=dhm, wt=wt, xb=xb, dbg=dbg, dbm=dbm,
                 dhgt=dhgt, dhmt=dhmt)
    return loss, early, state


def _local_attn_bwd(st):
    return _attn_bwd(st["q"], st["qt"], st["k"], st["kt"], st["v"], st["do"], st["dot"], st["lse"], st["delta"])


def _local_tail(st, dq, dk, dv, after):
    dhl, dhlt, dwuq, dwk, dwv, dgq, dgkv, dbl = _lat_bwd(dq, dk, dv, st["hl"], st["rc"], st["rsl"], st["rsh"],
                                                         st["gq2"], st["gkv2"], st["wuq"], st["wk"], st["wv"], after)
    late = {
        "w_lat": _dwt_lat(dhlt, st["xb"]),
        "b_in": _join_bias(st["dbg"], st["dbm"], dbl),
        "g_q": dgq[0],
        "w_uq": dwuq.reshape(Q_RANK, HEADS, HEAD_PAD)[:, :, :QK_DIM],
        "g_kv": dgkv[0],
        "w_ukv": jnp.concatenate([dwk.reshape(KV_RANK, HEADS, HEAD_PAD)[:, :, :NOPE],
                                  dwv.reshape(KV_RANK, HEADS, V_DIM)], axis=2),
    }
    return dhl, late


def _local_step(x, positions, tgt, wt, b_in, g_q, w_uq, g_kv, w_ukv, w_oa, sg_g, sg_b, w_s, b_s, w_ob, w_out, ln_g,
                ln_b):
    st = _local_attention(x, positions, wt[:LAT_COLS], b_in, g_q, w_uq, g_kv, w_ukv, b_in)
    loss, early, st = _local_head(st, x, tgt, wt, w_oa, sg_g, sg_b, w_s, b_s, w_ob, w_out, ln_g, ln_b)
    dq, dk, dv = _local_attn_bwd(st)
    dhl, late = _local_tail(st, dq, dk, dv, dv)
    dx = _dx(st["dr"], st["dhg"], st["dhm"], dhl, st["wt"], dhl)
    grads = {**early, **late}
    halves = [_dwt_early(st["dhmt"], st["dhgt"], st["xb"], jnp.full((1,), h, jnp.int32), dhl, "dwt_half%d" % h)
              for h in range(2)]
    grads["w_in"] = jnp.concatenate([grads.pop("w_lat")[:LAT_COLS], jnp.concatenate(halves, axis=1)[LAT_COLS:]],
                                    axis=0)
    return loss, dx, grads


MESH = pl.DeviceIdType.MESH
N_CHIPS = 4
HBM_SPEC = pl.BlockSpec(memory_space=pl.ANY)
HBM_SPEC_STRICT = pl.BlockSpec(memory_space=pltpu.HBM)
VMEM_SPEC = pl.BlockSpec(memory_space=pltpu.VMEM)

REP_ROWS = 80


def _rows8(a):
    flat = a.reshape(-1)
    n = -(-flat.shape[0] // (8 * D_MODEL)) * 8 * D_MODEL
    return jnp.pad(flat, (0, n - flat.shape[0])).reshape(-1, D_MODEL)


def _place():
    x, y, c = lax.axis_index("x"), lax.axis_index("y"), lax.axis_index("c")
    others = [(1 - x, y), (x, 1 - y), (1 - x, 1 - y)]
    return x, y, c, others


def _gather_weights(shards):
    n = len(shards)

    def body(*refs):
        ins, outs, bufs = refs[:n], refs[n:2 * n], refs[2 * n:3 * n]
        send_sems, recv_sems, local_sems = refs[3 * n:]
        x, y, c, others = _place()
        me = 2 * x + y
        sibling = (x, y, 1 - c)
        for src, buf in zip(ins, bufs):
            buf[...] = src[...].astype(BF16)
        own = [pltpu.make_async_copy(bufs[w], outs[w].at[me], local_sems.at[w]) for w in range(n)]
        for cp in own:
            cp.start()

        def part(w, chip, half):
            hc = shards[w].shape[1] // 2
            return outs[w].at[chip, :, pl.ds(half * hc, hc)]

        def sent(w, j):
            hc = shards[w].shape[1] // 2
            return pltpu.make_async_remote_copy(
                src_ref=bufs[w].at[:, pl.ds(c * hc, hc)], dst_ref=part(w, me, c),
                send_sem=send_sems.at[w * 3 + j], recv_sem=recv_sems.at[w * 3 + j],
                device_id=(*others[j], c), device_id_type=MESH)

        def landed(w, j):
            px, py = others[j]
            return pltpu.make_async_remote_copy(
                src_ref=part(w, 2 * px + py, c), dst_ref=part(w, 2 * px + py, c),
                send_sem=send_sems.at[w * 3 + j], recv_sem=recv_sems.at[w * 3 + j],
                device_id=(px, py, c), device_id_type=MESH)

        def passed(w, j, half):
            px, py = others[j]
            k = n * 3 + w * 3 + j
            return pltpu.make_async_remote_copy(
                src_ref=part(w, 2 * px + py, half), dst_ref=part(w, 2 * px + py, half),
                send_sem=send_sems.at[k], recv_sem=recv_sems.at[k], device_id=sibling, device_id_type=MESH)

        first = [sent(w, j) for w in range(n) for j in range(3)]
        for cp in first:
            cp.start()
        fwd = []
        for w in range(n):
            for j in range(3):
                landed(w, j).wait_recv()
                cp = passed(w, j, c)
                cp.start()
                fwd.append(cp)
        for w in range(n):
            for j in range(3):
                passed(w, j, 1 - c).wait_recv()
        for cp in first + fwd:
            cp.wait_send()
        for cp in own:
            cp.wait()

    return pl.pallas_call(
        body, name="gather_weights",
        in_specs=[VMEM_SPEC] * n, out_specs=[HBM_SPEC] * n,
        out_shape=[jax.ShapeDtypeStruct((N_CHIPS,) + s.shape, BF16) for s in shards],
        scratch_shapes=[pltpu.VMEM(s.shape, BF16) for s in shards]
        + [pltpu.SemaphoreType.DMA((6 * n,)), pltpu.SemaphoreType.DMA((6 * n,)), pltpu.SemaphoreType.DMA((n,))],
        compiler_params=pltpu.CompilerParams(vmem_limit_bytes=VMEM_LIMIT),
    )(*shards)


N_DEV = 8
LOSS_TILE = (8, 128)


def _gather_first(lat, uq):
    hl, hu = lat.shape[1] // 2, uq.shape[1] // 2

    def body(lat_ref, uq_ref, wlat_ref, guq_ref, lat_buf, uq_buf, send_sems, recv_sems, local_sems):
        x, y, c, others = _place()
        me = 2 * x + y
        sibling = (x, y, 1 - c)
        lat_buf[...] = lat_ref[...].astype(BF16)
        uq_buf[...] = uq_ref[...].astype(BF16)

        def copy(src, dst, k, to):
            return pltpu.make_async_remote_copy(src_ref=src, dst_ref=dst, send_sem=send_sems.at[k],
                                                recv_sem=recv_sems.at[k], device_id=to, device_id_type=MESH)

        def uq_part(chip, half):
            return guq_ref.at[chip, :, pl.ds(half * hu, hu)]

        def lat_part(half):
            return wlat_ref.at[:, pl.ds(half * hl, hl)]

        own = pltpu.make_async_copy(uq_buf, guq_ref.at[me], local_sems.at[0])
        own.start()
        first = [copy(uq_buf.at[:, pl.ds(c * hu, hu)], uq_part(me, c), j, (*others[j], c)) for j in range(3)]
        for cp in first:
            cp.start()

        @pl.when(me == 0)
        def _():
            mine = pltpu.make_async_copy(lat_buf, wlat_ref, local_sems.at[1])
            mine.start()
            cps = [copy(lat_buf.at[:, pl.ds(c * hl, hl)], lat_part(c), 6 + j, (*others[j], c)) for j in range(3)]
            for cp in cps:
                cp.start()
            for cp in cps:
                cp.wait_send()
            mine.wait()

        @pl.when(me != 0)
        def _():
            j0 = x + 2 * y - 1
            copy(lat_part(c), lat_part(c), 6 + j0, (0, 0, c)).wait_recv()
            fwd = copy(lat_part(c), lat_part(c), 9, sibling)
            fwd.start()
            copy(lat_part(1 - c), lat_part(1 - c), 9, sibling).wait_recv()
            fwd.wait_send()

        fwd = []
        for j, (px, py) in enumerate(others):
            chip = 2 * px + py
            copy(uq_part(chip, c), uq_part(chip, c), j, (px, py, c)).wait_recv()
            cp = copy(uq_part(chip, c), uq_part(chip, c), 3 + j, sibling)
            cp.start()
            fwd.append(cp)
        for j, (px, py) in enumerate(others):
            chip = 2 * px + py
            copy(uq_part(chip, 1 - c), uq_part(chip, 1 - c), 3 + j, sibling).wait_recv()
        for cp in first + fwd:
            cp.wait_send()
        own.wait()

    return pl.pallas_call(
        body, name="gather_first", in_specs=[VMEM_SPEC, VMEM_SPEC], out_specs=[HBM_SPEC, HBM_SPEC],
        out_shape=[jax.ShapeDtypeStruct(lat.shape, BF16), jax.ShapeDtypeStruct((N_CHIPS,) + uq.shape, BF16)],
        scratch_shapes=[pltpu.VMEM(lat.shape, BF16), pltpu.VMEM(uq.shape, BF16), pltpu.SemaphoreType.DMA((10,)),
                        pltpu.SemaphoreType.DMA((10,)), pltpu.SemaphoreType.DMA((2,))],
        compiler_params=pltpu.CompilerParams(vmem_limit_bytes=VMEM_LIMIT),
    )(lat, uq)


def _cast_own(shards, me):
    n = len(shards)

    def body(me_ref, *refs):
        for w in range(n):
            refs[n + w][...] = refs[w][...].astype(BF16)

    return pl.pallas_call(
        body, name="cast_own",
        grid_spec=pltpu.PrefetchScalarGridSpec(
            num_scalar_prefetch=1, grid=(1,),
            in_specs=[pl.BlockSpec(s.shape, lambda i, me_ref: (0, 0)) for s in shards],
            out_specs=[pl.BlockSpec((None,) + s.shape, lambda i, me_ref: (me_ref[0], 0, 0)) for s in shards]),
        out_shape=[jax.ShapeDtypeStruct((N_CHIPS,) + s.shape, BF16) for s in shards],
        compiler_params=pltpu.CompilerParams(vmem_limit_bytes=VMEM_LIMIT),
    )(me, *shards)


def _gather_start(bufs, after):
    n = len(bufs)

    def body(*refs):
        b_refs = refs[:n]
        send_sems, recv_sems, token = refs[n + 1], refs[n + 2], refs[-1]
        x, y, c, others = _place()
        me = 2 * x + y
        for w in range(n):
            hc = _half(bufs[w])
            mine = b_refs[w].at[me, :, pl.ds(c * hc, hc)]
            for j, (px, py) in enumerate(others):
                pltpu.make_async_remote_copy(
                    src_ref=mine, dst_ref=mine, send_sem=send_sems.at[3 * w + j], recv_sem=recv_sems.at[3 * w + j],
                    device_id=(px, py, c), device_id_type=MESH).start()
        token[...] = jnp.zeros_like(token)

    hbm = [pltpu.HBM(b.shape, BF16) for b in bufs]
    outs = pl.pallas_call(
        body, name="gather_start",
        out_shape=(pltpu.SemaphoreType.DMA((3 * n,)), pltpu.SemaphoreType.DMA((3 * n,)), *hbm,
                   jax.ShapeDtypeStruct(LOSS_TILE, F32)),
        in_specs=[HBM_SPEC_STRICT] * n + [HBM_SPEC],
        out_specs=(SEM_SPEC, SEM_SPEC, *[HBM_SPEC_STRICT] * n, VMEM_SPEC),
        input_output_aliases={i: 2 + i for i in range(n)},
        compiler_params=pltpu.CompilerParams(has_side_effects=SPLIT_EFFECT),
    )(*[pltpu.with_memory_space_constraint(b, pltpu.HBM) for b in bufs], after)
    return outs[0], outs[1], list(outs[2:2 + n]), outs[-1]


def _gather_wait(send_sems, recv_sems, bufs, after):
    n = len(bufs)

    def body(*refs):
        b_refs = refs[:n]
        send_sems, recv_sems = refs[n], refs[n + 1]
        x, y, c, others = _place()
        me = 2 * x + y
        for w in range(n):
            hc = _half(bufs[w])
            for j, (px, py) in enumerate(others):
                cp = pltpu.make_async_remote_copy(
                    src_ref=b_refs[w].at[me, :, pl.ds(c * hc, hc)],
                    dst_ref=b_refs[w].at[2 * px + py, :, pl.ds(c * hc, hc)],
                    send_sem=send_sems.at[3 * w + j], recv_sem=recv_sems.at[3 * w + j], device_id=(px, py, c),
                    device_id_type=MESH)
                cp.wait_send()
                cp.wait_recv()

    outs = pl.pallas_call(
        body, name="gather_wait", out_shape=tuple(pltpu.HBM(b.shape, b.dtype) for b in bufs),
        in_specs=[HBM_SPEC_STRICT] * n + [SEM_SPEC, SEM_SPEC, HBM_SPEC],
        out_specs=tuple([HBM_SPEC_STRICT] * n), input_output_aliases={i: i for i in range(n)},
        compiler_params=pltpu.CompilerParams(has_side_effects=SPLIT_EFFECT),
    )(*bufs, send_sems, recv_sems, after)
    return list(outs)


def _gather_finish(bufs):
    n = len(bufs)

    def body(*refs):
        b_refs = refs[n:2 * n]
        send_sems, recv_sems = refs[2 * n:]
        x, y, c, others = _place()
        cps = []
        for w in range(n):
            hc = _half(bufs[w])
            for j, (px, py) in enumerate(others):
                part = b_refs[w].at[2 * px + py, :, pl.ds(c * hc, hc)]
                cps.append(pltpu.make_async_remote_copy(
                    src_ref=part, dst_ref=part, send_sem=send_sems.at[3 * w + j], recv_sem=recv_sems.at[3 * w + j],
                    device_id=(x, y, 1 - c), device_id_type=MESH))
        for cp in cps:
            cp.start()
        for w in range(n):
            hc = _half(bufs[w])
            for j, (px, py) in enumerate(others):
                theirs = b_refs[w].at[2 * px + py, :, pl.ds((1 - c) * hc, hc)]
                pltpu.make_async_remote_copy(
                    src_ref=theirs, dst_ref=theirs, send_sem=send_sems.at[3 * w + j], recv_sem=recv_sems.at[3 * w + j],
                    device_id=(x, y, 1 - c), device_id_type=MESH).wait_recv()
        for cp in cps:
            cp.wait_send()

    return pl.pallas_call(
        body, name="gather_finish", in_specs=[HBM_SPEC] * n, out_specs=[HBM_SPEC] * n,
        out_shape=[jax.ShapeDtypeStruct(b.shape, b.dtype) for b in bufs],
        input_output_aliases={i: i for i in range(n)},
        scratch_shapes=[pltpu.SemaphoreType.DMA((3 * n,)), pltpu.SemaphoreType.DMA((3 * n,))],
    )(*bufs)


def _half(a):
    return a.shape[-1] // 2


def _exchange_pairs(parts, name):
    n = len(parts)

    def body(*refs):
        p_refs, r_refs = refs[:n], refs[n:2 * n]
        send_sems, recv_sems = refs[2 * n:]
        x, y, c, _ = _place()
        cps = []
        for w in range(n):
            h = _half(parts[w])
            cps.append(pltpu.make_async_remote_copy(
                src_ref=p_refs[w].at[:, :, pl.ds((1 - c) * h, h)], dst_ref=r_refs[w],
                send_sem=send_sems.at[w], recv_sem=recv_sems.at[w], device_id=(x, y, 1 - c), device_id_type=MESH))
        for cp in cps:
            cp.start()
        for cp in cps:
            cp.wait()

    return pl.pallas_call(
        body, name=name, in_specs=[HBM_SPEC] * n, out_specs=[HBM_SPEC] * n,
        out_shape=[jax.ShapeDtypeStruct((N_CHIPS, p.shape[1], _half(p)), BF16) for p in parts],
        scratch_shapes=[pltpu.SemaphoreType.DMA((n,)), pltpu.SemaphoreType.DMA((n,))],
    )(*parts)


def _sibling_part(ref, w, n_whole, shape, c):
    if w < n_whole:
        return ref
    h = shape[-1] // 2
    return ref.at[:, :, pl.ds((1 - c) * h, h)]


def _pairs_start(parts, all_loss, n_whole):
    n = len(parts)

    def body(*refs):
        p_refs, r_refs, loss_ref = refs[:n], refs[n:2 * n], refs[2 * n]
        send_sems, recv_sems, token = refs[2 * n + 1], refs[2 * n + 2], refs[-1]
        x, y, c, _ = _place()
        for w in range(n):
            h = _half(parts[w])
            pltpu.make_async_remote_copy(
                src_ref=_sibling_part(p_refs[w], w, n_whole, parts[w].shape, c), dst_ref=r_refs[w],
                send_sem=send_sems.at[w], recv_sem=recv_sems.at[w], device_id=(x, y, 1 - c),
                device_id_type=MESH).start()
        me = 4 * x + 2 * y + c
        for t in range(1, N_DEV):
            d = (me + t) % N_DEV
            pltpu.make_async_remote_copy(
                src_ref=loss_ref.at[me], dst_ref=loss_ref.at[me], send_sem=send_sems.at[n + t - 1],
                recv_sem=recv_sems.at[n + t - 1], device_id=(d // 4, (d // 2) % 2, d % 2), device_id_type=MESH).start()
        token[...] = jnp.zeros_like(token)

    lands = [pltpu.HBM(p.shape if w < n_whole else (N_CHIPS, p.shape[1], _half(p)), BF16)
             for w, p in enumerate(parts)]
    nsem = n + N_DEV - 1
    outs = pl.pallas_call(
        body, name="pairs_start",
        out_shape=(pltpu.SemaphoreType.DMA((nsem,)), pltpu.SemaphoreType.DMA((nsem,)),
                   *[pltpu.HBM(p.shape, p.dtype) for p in parts], *lands, pltpu.HBM(all_loss.shape, F32),
                   jax.ShapeDtypeStruct(LOSS_TILE, F32)),
        in_specs=[HBM_SPEC_STRICT] * (2 * n + 1),
        out_specs=(SEM_SPEC, SEM_SPEC, *[HBM_SPEC_STRICT] * (2 * n + 1), VMEM_SPEC),
        input_output_aliases={i: 2 + i for i in range(2 * n + 1)},
        compiler_params=pltpu.CompilerParams(has_side_effects=SPLIT_EFFECT),
    )(*[pltpu.with_memory_space_constraint(p, pltpu.HBM) for p in parts],
      *[pltpu.with_memory_space_constraint(lax.empty(l.shape, BF16), pltpu.HBM) for l in lands],
      pltpu.with_memory_space_constraint(all_loss, pltpu.HBM))
    return outs[0], outs[1], list(outs[2:2 + n]), list(outs[2 + n:2 + 2 * n]), outs[2 + 2 * n], outs[-1]


def _pairs_wait(send_sems, recv_sems, parts, lands, all_loss, after, n_whole):
    n = len(parts)

    def body(*refs):
        p_refs, r_refs, loss_ref = refs[:n], refs[n:2 * n], refs[2 * n]
        send_sems, recv_sems = refs[2 * n + 1], refs[2 * n + 2]
        x, y, c, _ = _place()
        for w in range(n):
            h = _half(parts[w])
            cp = pltpu.make_async_remote_copy(
                src_ref=_sibling_part(p_refs[w], w, n_whole, parts[w].shape, c), dst_ref=r_refs[w],
                send_sem=send_sems.at[w],
                recv_sem=recv_sems.at[w], device_id=(x, y, 1 - c), device_id_type=MESH)
            cp.wait_send()
            cp.wait_recv()
        me = 4 * x + 2 * y + c
        for t in range(1, N_DEV):
            d = (me + N_DEV - t) % N_DEV
            cp = pltpu.make_async_remote_copy(
                src_ref=loss_ref.at[me], dst_ref=loss_ref.at[d], send_sem=send_sems.at[n + t - 1],
                recv_sem=recv_sems.at[n + t - 1], device_id=(d // 4, (d // 2) % 2, d % 2), device_id_type=MESH)
            cp.wait_send()
            cp.wait_recv()

    bufs = (*parts, *lands, all_loss)
    outs = pl.pallas_call(
        body, name="pairs_wait", out_shape=tuple(pltpu.HBM(a.shape, a.dtype) for a in bufs),
        in_specs=[HBM_SPEC_STRICT] * len(bufs) + [SEM_SPEC, SEM_SPEC, HBM_SPEC],
        out_specs=tuple([HBM_SPEC_STRICT] * len(bufs)), input_output_aliases={i: i for i in range(len(bufs))},
        compiler_params=pltpu.CompilerParams(has_side_effects=SPLIT_EFFECT),
    )(*bufs, send_sems, recv_sems, after)
    return list(outs[:n]), list(outs[n:2 * n]), outs[2 * n]


def _add_pair_tiled(p, r):
    rows, h = r.shape[1:]

    def body(p_ref, r_ref, q_ref):
        q_ref[...] = (p_ref[...].astype(F32) + r_ref[...].astype(F32)).astype(BF16)

    spec = pl.BlockSpec((None, rows, h), lambda k: (k, 0, 0))
    return pl.pallas_call(
        body, name="add_pair_w_in", grid=(N_CHIPS,), in_specs=[spec, spec], out_specs=spec,
        out_shape=jax.ShapeDtypeStruct(r.shape, BF16),
    )(p, r)


def _add_pair_small(ps, rs, c, name):
    n = len(ps)

    def body(c_ref, *refs):
        for w in range(n):
            h = _half(ps[w])
            mine = refs[w][:, :, pl.ds(pl.multiple_of(c_ref[0] * h, 128), h)]
            refs[2 * n + w][...] = (mine.astype(F32) + refs[n + w][...].astype(F32)).astype(BF16)

    return pl.pallas_call(
        body, name=name,
        in_specs=[pl.BlockSpec(memory_space=pltpu.SMEM)] + [VMEM_SPEC] * (2 * n), out_specs=[VMEM_SPEC] * n,
        out_shape=[jax.ShapeDtypeStruct(r.shape, BF16) for r in rs],
        compiler_params=pltpu.CompilerParams(vmem_limit_bytes=VMEM_LIMIT),
    )(c, *ps, *rs)


def _exchange_chips(qs):
    n = len(qs)

    def body(*refs):
        q_refs, r_refs = refs[:n], refs[n:2 * n]
        send_sems, recv_sems = refs[2 * n:]
        x, y, c, others = _place()
        me = 2 * x + y
        cps = []
        for w in range(n):
            for j, (px, py) in enumerate(others):
                cps.append(pltpu.make_async_remote_copy(
                    src_ref=q_refs[w].at[2 * px + py], dst_ref=r_refs[w].at[me], send_sem=send_sems.at[3 * w + j],
                    recv_sem=recv_sems.at[3 * w + j], device_id=(px, py, c), device_id_type=MESH))
        for cp in cps:
            cp.start()
        for w in range(n):
            for j, (px, py) in enumerate(others):
                pltpu.make_async_remote_copy(
                    src_ref=q_refs[w].at[me], dst_ref=r_refs[w].at[2 * px + py], send_sem=send_sems.at[3 * w + j],
                    recv_sem=recv_sems.at[3 * w + j], device_id=(px, py, c), device_id_type=MESH).wait_recv()
        for cp in cps:
            cp.wait_send()

    return pl.pallas_call(
        body, name="exchange_chips", in_specs=[HBM_SPEC] * n, out_specs=[HBM_SPEC] * n,
        out_shape=[jax.ShapeDtypeStruct(q.shape, BF16) for q in qs],
        scratch_shapes=[pltpu.SemaphoreType.DMA((3 * n,)), pltpu.SemaphoreType.DMA((3 * n,))],
    )(*qs)


SEM_SPEC = pl.BlockSpec(memory_space=pltpu.SEMAPHORE)
SPLIT_EFFECT = pltpu.SideEffectType.DATAFLOW_SIDE_EFFECTING


def _chips_start(qs, name):
    n = len(qs)

    def body(*refs):
        q_refs, land_refs = refs[:n], refs[n:2 * n]
        send_sems, recv_sems, token = refs[2 * n], refs[2 * n + 1], refs[-1]
        x, y, c, others = _place()
        me = 2 * x + y
        for w in range(n):
            for j, (px, py) in enumerate(others):
                pltpu.make_async_remote_copy(
                    src_ref=q_refs[w].at[2 * px + py], dst_ref=land_refs[w].at[me], send_sem=send_sems.at[3 * w + j],
                    recv_sem=recv_sems.at[3 * w + j], device_id=(px, py, c), device_id_type=MESH).start()
        token[...] = jnp.zeros_like(token)

    hbm = [pltpu.HBM(q.shape, BF16) for q in qs]
    outs = pl.pallas_call(
        body, name=name,
        out_shape=(pltpu.SemaphoreType.DMA((3 * n,)), pltpu.SemaphoreType.DMA((3 * n,)), *hbm, *hbm,
                   jax.ShapeDtypeStruct(LOSS_TILE, F32)),
        in_specs=[HBM_SPEC_STRICT] * (2 * n),
        out_specs=(SEM_SPEC, SEM_SPEC, *[HBM_SPEC_STRICT] * (2 * n), VMEM_SPEC),
        input_output_aliases={i: 2 + i for i in range(2 * n)},
        compiler_params=pltpu.CompilerParams(has_side_effects=SPLIT_EFFECT),
    )(*[pltpu.with_memory_space_constraint(q, pltpu.HBM) for q in qs],
      *[pltpu.with_memory_space_constraint(lax.empty(q.shape, BF16), pltpu.HBM) for q in qs])
    return outs[0], outs[1], outs[2:2 + n], outs[2 + n:2 + 2 * n], outs[-1]


def _chips_wait(send_sems, recv_sems, q_thru, land_thru, after, name):
    n = len(q_thru)

    def body(*refs):
        q_refs, land_refs = refs[:n], refs[n:2 * n]
        send_sems, recv_sems = refs[2 * n], refs[2 * n + 1]
        x, y, c, others = _place()
        me = 2 * x + y
        for w in range(n):
            for j, (px, py) in enumerate(others):
                cp = pltpu.make_async_remote_copy(
                    src_ref=q_refs[w].at[2 * px + py], dst_ref=land_refs[w].at[2 * px + py],
                    send_sem=send_sems.at[3 * w + j], recv_sem=recv_sems.at[3 * w + j], device_id=(px, py, c),
                    device_id_type=MESH)
                cp.wait_send()
                cp.wait_recv()

    outs = pl.pallas_call(
        body, name=name, out_shape=tuple(pltpu.HBM(a.shape, a.dtype) for a in (*q_thru, *land_thru)),
        in_specs=[HBM_SPEC_STRICT] * (2 * n) + [SEM_SPEC, SEM_SPEC, HBM_SPEC],
        out_specs=tuple([HBM_SPEC_STRICT] * (2 * n)), input_output_aliases={i: i for i in range(2 * n)},
        compiler_params=pltpu.CompilerParams(has_side_effects=SPLIT_EFFECT),
    )(*q_thru, *land_thru, send_sems, recv_sems, after)
    return list(outs[:n]), list(outs[n:])


def _sum_chips_tiled(q, r, idx, tile):
    rows, h = r.shape[1:]
    nt = h // tile

    def body(idx_ref, q_ref, r0_ref, r1_ref, r2_ref, g_ref):
        g_ref[...] = (q_ref[...].astype(F32) + r0_ref[...].astype(F32) + r1_ref[...].astype(F32)
                      + r2_ref[...].astype(F32))

    def slab(t):
        return pl.BlockSpec((None, rows, tile), lambda i, idx_ref: (idx_ref[t], 0, i))

    return pl.pallas_call(
        body, name="sum_chips_w_in",
        grid_spec=pltpu.PrefetchScalarGridSpec(
            num_scalar_prefetch=1, grid=(nt,), in_specs=[slab(0), slab(1), slab(2), slab(3)],
            out_specs=pl.BlockSpec((rows, tile), lambda i, idx_ref: (0, idx_ref[4] * nt + i))),
        out_shape=jax.ShapeDtypeStruct((rows, 2 * h), F32),
    )(idx, q, r, r, r)


def _sum_chips_small(qs, rs, idx, all_dtypes):
    n = len(rs)
    n_all = len(all_dtypes)

    def body(idx_ref, *refs):
        c = idx_ref[4]
        for w in range(n):
            q_ref, r_ref, g_ref = refs[w], refs[n + w], refs[2 * n + w]
            acc = q_ref[idx_ref[0]].astype(F32)
            for t in range(1, N_CHIPS):
                acc = acc + r_ref[idx_ref[t]].astype(F32)
            h = rs[w].shape[2]
            mine = pl.ds(pl.multiple_of(c * h, 128), h)
            g_ref[...] = jnp.zeros_like(g_ref)
            if w >= n - n_all:
                g_ref[idx_ref[0], :, mine] = acc.astype(g_ref.dtype)
            else:
                g_ref[:, mine] = acc

    shapes = [jax.ShapeDtypeStruct((r.shape[1], 2 * r.shape[2]), F32) for r in rs[:n - n_all]]
    shapes += [jax.ShapeDtypeStruct((N_CHIPS, r.shape[1], 2 * r.shape[2]), dt)
               for r, dt in zip(rs[n - n_all:], all_dtypes)]
    return pl.pallas_call(
        body, name="sum_chips_small",
        in_specs=[pl.BlockSpec(memory_space=pltpu.SMEM)] + [VMEM_SPEC] * (2 * n), out_specs=[VMEM_SPEC] * n,
        out_shape=shapes, compiler_params=pltpu.CompilerParams(vmem_limit_bytes=VMEM_LIMIT),
    )(idx, *qs, *rs)


def _share(shards, alls):
    n, na = len(shards), len(alls)
    total = n + na

    def body(*refs):
        g_refs, a_refs = refs[total:total + n], refs[total + n:2 * total]
        send_sems, recv_sems = refs[2 * total:]
        x, y, c, others = _place()
        me = 2 * x + y
        sibling = (x, y, 1 - c)

        def cols_of(w, half):
            h = shards[w].shape[1] // 2
            return g_refs[w].at[:, pl.ds(half * h, h)]

        def slab(a, chip, half):
            h = alls[a].shape[2] // 2
            return a_refs[a].at[chip, :, pl.ds(half * h, h)]

        def copy(src, dst, k, to):
            return pltpu.make_async_remote_copy(src_ref=src, dst_ref=dst, send_sem=send_sems.at[k],
                                                recv_sem=recv_sems.at[k], device_id=to, device_id_type=MESH)

        cps = [copy(cols_of(w, c), cols_of(w, c), w, sibling) for w in range(n)]
        for a in range(na):
            base = n + 7 * a
            cps.append(copy(slab(a, me, c), slab(a, me, c), base, sibling))
            for j, (px, py) in enumerate(others):
                cps.append(copy(slab(a, me, c), slab(a, me, c), base + 1 + j, (px, py, c)))
        for cp in cps:
            cp.start()
        fwd = []
        for a in range(na):
            base = n + 7 * a
            for j, (px, py) in enumerate(others):
                chip = 2 * px + py
                copy(slab(a, me, c), slab(a, chip, c), base + 1 + j, (px, py, c)).wait_recv()
                cp = copy(slab(a, chip, c), slab(a, chip, c), base + 4 + j, sibling)
                cp.start()
                fwd.append(cp)
        for a in range(na):
            base = n + 7 * a
            for j, (px, py) in enumerate(others):
                chip = 2 * px + py
                copy(slab(a, chip, c), slab(a, chip, 1 - c), base + 4 + j, sibling).wait_recv()
            copy(slab(a, me, c), slab(a, me, 1 - c), base, sibling).wait_recv()
        for w in range(n):
            copy(cols_of(w, c), cols_of(w, 1 - c), w, sibling).wait_recv()
        for cp in cps + fwd:
            cp.wait_send()

    nsem = n + 7 * na
    return pl.pallas_call(
        body, name="share", in_specs=[HBM_SPEC] * total, out_specs=[HBM_SPEC] * total,
        out_shape=[jax.ShapeDtypeStruct(a.shape, a.dtype) for a in (*shards, *alls)],
        input_output_aliases={i: i for i in range(total)},
        scratch_shapes=[pltpu.SemaphoreType.DMA((nsem,)), pltpu.SemaphoreType.DMA((nsem,))],
    )(*shards, *alls)


def _adamw(w, g, m, v):
    m2 = ADAM_B1 * m + (1.0 - ADAM_B1) * g
    v2 = ADAM_B2 * v + (1.0 - ADAM_B2) * (g * g)
    m_hat = m2 / (1.0 - ADAM_B1 ** ADAM_STEP)
    v_hat = v2 / (1.0 - ADAM_B2 ** ADAM_STEP)
    return -ADAM_LR * (m_hat / (jnp.sqrt(v_hat) + ADAM_EPS) + ADAM_WD * w), m2, v2


def _update_w_in(wt, gt, mt, vt, lat, owner, tile):
    nlat = lat.shape[0] // tile

    def body(owner_ref, w_ref, g_ref, m_ref, v_ref, lat_ref, g2_ref, d_ref, m2_ref, v2_ref):
        row = pl.program_id(0) * tile + lax.broadcasted_iota(jnp.int32, (tile, 1), 0)
        g = jnp.where((row < LAT_COLS) & (owner_ref[0] == 1), lat_ref[...].astype(F32), g_ref[...])
        g2_ref[...] = g
        d_ref[...], m2_ref[...], v2_ref[...] = _adamw(w_ref[...], g, m_ref[...], v_ref[...])

    spec = pl.BlockSpec((tile, wt.shape[1]), lambda i, o: (i, 0))
    return pl.pallas_call(
        body, name="update_w_in",
        grid_spec=pltpu.PrefetchScalarGridSpec(
            num_scalar_prefetch=1, grid=(wt.shape[0] // tile,),
            in_specs=[spec] * 4 + [pl.BlockSpec((tile, wt.shape[1]), lambda i, o: (jnp.minimum(i, nlat - 1), 0))],
            out_specs=[spec] * 4),
        out_shape=[jax.ShapeDtypeStruct(wt.shape, F32)] * 4,
        compiler_params=_params(("parallel",)),
    )(owner, wt, gt, mt, vt, lat)


def _update_small(ws, gs, ms, vs):
    n = len(ws)

    def body(*refs):
        for k in range(n):
            w_ref, g_ref, m_ref, v_ref = refs[k], refs[n + k], refs[2 * n + k], refs[3 * n + k]
            d, m2, v2 = _adamw(w_ref[...], g_ref[...], m_ref[...], v_ref[...])
            refs[4 * n + k][...] = d
            refs[5 * n + k][...] = m2
            refs[6 * n + k][...] = v2

    shapes = [jax.ShapeDtypeStruct(w.shape, F32) for w in ws]
    outs = pl.pallas_call(
        body, name="update_small", in_specs=[VMEM_SPEC] * (4 * n), out_specs=[VMEM_SPEC] * (3 * n),
        out_shape=shapes * 3,
        compiler_params=pltpu.CompilerParams(vmem_limit_bytes=VMEM_LIMIT),
    )(*ws, *gs, *ms, *vs)
    return outs[:n], outs[n:2 * n], outs[2 * n:]


SHARDED = ("w_in", "w_uq", "w_oa", "w_ob", "w_out")
REPLICATED = ("b_in", "g_q", "g_kv", "w_ukv", "sgu_ln_g", "sgu_ln_b", "w_s", "b_s", "ln_g", "ln_b")
ORDER = ("w_in", "b_in", "g_q", "w_uq", "g_kv", "w_ukv", "w_oa", "sgu_ln_g", "sgu_ln_b", "w_s", "b_s", "w_ob", "w_out",
         "ln_g", "ln_b")


def kernel(x, positions, w_in, b_in, g_q, w_uq, g_kv, w_ukv, w_oa, sgu_ln_g, sgu_ln_b, w_s, b_s, w_ob, w_out, ln_g, ln_b, loss_target, m_w_in, m_b_in, m_g_q, m_w_uq, m_g_kv, m_w_ukv, m_w_oa, m_sgu_ln_g, m_sgu_ln_b, m_w_s, m_b_s, m_w_ob, m_w_out, m_ln_g, m_ln_b, v_w_in, v_b_in, v_g_q, v_w_uq, v_g_kv, v_w_ukv, v_w_oa, v_sgu_ln_g, v_sgu_ln_b, v_w_s, v_b_s, v_w_ob, v_w_out, v_ln_g, v_ln_b):
    w = dict(w_in=w_in, b_in=b_in, g_q=g_q, w_uq=w_uq, g_kv=g_kv, w_ukv=w_ukv, w_oa=w_oa, sgu_ln_g=sgu_ln_g,
             sgu_ln_b=sgu_ln_b, w_s=w_s, b_s=b_s, w_ob=w_ob, w_out=w_out, ln_g=ln_g, ln_b=ln_b)
    m = dict(w_in=m_w_in, b_in=m_b_in, g_q=m_g_q, w_uq=m_w_uq, g_kv=m_g_kv, w_ukv=m_w_ukv, w_oa=m_w_oa,
             sgu_ln_g=m_sgu_ln_g, sgu_ln_b=m_sgu_ln_b, w_s=m_w_s, b_s=m_b_s, w_ob=m_w_ob, w_out=m_w_out, ln_g=m_ln_g,
             ln_b=m_ln_b)
    v = dict(w_in=v_w_in, b_in=v_b_in, g_q=v_g_q, w_uq=v_w_uq, g_kv=v_g_kv, w_ukv=v_w_ukv, w_oa=v_w_oa,
             sgu_ln_g=v_sgu_ln_g, sgu_ln_b=v_sgu_ln_b, w_s=v_w_s, b_s=v_b_s, w_ob=v_w_ob, w_out=v_w_out, ln_g=v_ln_g,
             ln_b=v_ln_b)
    w, m, v = ({n: a[0] for n, a in d.items()} for d in (w, m, v))
    c = lax.axis_index("c")

    wt_shard, mt_shard, vt_shard = (jnp.transpose(d["w_in"]) for d in (w, m, v))
    xi, yi = lax.axis_index("x"), lax.axis_index("y")
    me1 = (2 * xi + yi).reshape(1).astype(jnp.int32)
    g_lat, g_uq = _gather_first(wt_shard[:LAT_COLS], w["w_uq"].reshape(Q_RANK // 4, HEADS * QK_DIM))
    bufs = _cast_own([wt_shard, w["w_oa"], w["w_ob"], w["w_out"]], me1)
    send0, recv0, bufs, token0 = _gather_start(bufs, g_lat)
    st = _local_attention(x[0], positions[0], g_lat, w["b_in"], w["g_q"], g_uq.reshape(Q_RANK, HEADS, QK_DIM),
                          w["g_kv"], w["w_ukv"], token0)
    g_in, g_oa, g_ob, g_out = _gather_finish(_gather_wait(send0, recv0, bufs, st["o"]))
    wt = g_in.reshape(IN_W, D_MODEL)

    loss, early, st = _local_head(
        st, x[0], loss_target[0], wt, g_oa, w["sgu_ln_g"], w["sgu_ln_b"], w["w_s"], w["b_s"], g_ob,
        g_out.reshape(D_MODEL, D_MODEL), w["ln_g"], w["ln_b"])

    c1 = c.reshape(1).astype(jnp.int32)
    idx = jnp.stack([2 * xi + yi, 2 * (1 - xi) + yi, 2 * xi + (1 - yi), 2 * (1 - xi) + (1 - yi), c]).astype(jnp.int32)
    slabs = lambda a: a.reshape(N_CHIPS, IN_W // N_CHIPS, D_MODEL // 2)
    theirs = slabs(_dwt_early(st["dhmt"], st["dhgt"], st["xb"], 1 - c1, c1, "dwt_theirs"))
    parts1 = [theirs, early["w_oa"].astype(BF16), early["w_ob"].astype(BF16),
              early["w_out"].reshape(N_CHIPS, SLAB_W, D_MODEL).astype(BF16)]
    my_loss = lax.dynamic_update_slice(jnp.zeros((N_DEV,) + LOSS_TILE, F32), jnp.broadcast_to(loss, (1,) + LOSS_TILE),
                                       (4 * xi + 2 * yi + c, 0, 0))
    sems0 = _pairs_start(parts1, my_loss, 1)
    mine = slabs(_dwt_early(st["dhmt"], st["dhgt"], st["xb"], c1, sems0[5], "dwt_mine"))
    parts1, recv1, all_loss = _pairs_wait(*sems0[:5], mine, 1)
    pairs1 = [_add_pair_tiled(mine, recv1[0]), *_add_pair_small(parts1[1:], recv1[1:], c1, "add_pair_early")]
    sems1 = _chips_start(pairs1, "chips_start_early")
    st["delta"] = st["delta"] + sems1[4][0, 0]
    dq, dk, dv = _local_attn_bwd(st)
    dhl, late = _local_tail(st, dq, dk, dv, dv)

    grads = {**early, **late}
    rep = jnp.concatenate([_rows8(grads[n]) for n in REPLICATED], axis=0)
    rep = jnp.pad(rep, ((0, N_CHIPS * REP_ROWS - rep.shape[0]), (0, 0))).reshape(N_CHIPS, REP_ROWS, D_MODEL)
    parts2 = [late["w_uq"].reshape(N_CHIPS, Q_RANK // N_CHIPS, HEADS * QK_DIM).astype(BF16), rep.astype(BF16),
              late["w_lat"].reshape(N_CHIPS, LAT_ROWS_PAD // N_CHIPS, D_MODEL)]
    pairs2 = _add_pair_small(parts2, _exchange_pairs(parts2, "exchange_pairs_late"), c1, "add_pair_late")
    sems2 = _chips_start(pairs2, "chips_start_late")
    dx = _dx(st["dr"], st["dhg"], st["dhm"], dhl, st["wt"], sems2[4])
    pairs2, landed2 = _chips_wait(*sems2[:4], dx, "chips_wait_late")
    pairs1, landed1 = _chips_wait(*sems1[:4], landed2[0], "chips_wait_early")
    sums = [_sum_chips_tiled(pairs1[0], landed1[0], idx, 128),
            *_sum_chips_small([*pairs1[1:], *pairs2], [*landed1[1:], *landed2], idx, (F32, BF16))]
    *shards, g_rep, g_lat = _share(sums[:-2], sums[-2:])
    loss = jnp.sum(all_loss[:, 0, 0])

    red = {n: s.reshape(w[n].shape) for n, s in zip(("w_oa", "w_ob", "w_out", "w_uq"), shards[1:])}
    g_rep = g_rep.reshape(N_CHIPS * REP_ROWS, D_MODEL)
    off = 0
    for n in REPLICATED:
        rows = _rows8(w[n]).shape[0]
        red[n] = g_rep[off:off + rows].reshape(-1)[:w[n].size].reshape(w[n].shape)
        off += rows
    owner = (2 * xi + yi == 0).astype(jnp.int32).reshape(1)
    gt, dt, mt, vt2 = _update_w_in(wt_shard, shards[0], mt_shard, vt_shard,
                                   g_lat.reshape(LAT_ROWS_PAD, D_MODEL).astype(F32), owner, 232)
    red["w_in"] = jnp.transpose(gt)
    small = [n for n in ORDER if n != "w_in"]
    as2d = lambda a: a.reshape(-1, a.shape[-1])
    ds, ms, vs = _update_small([as2d(w[n]) for n in small], [as2d(red[n]) for n in small],
                               [as2d(m[n]) for n in small], [as2d(v[n]) for n in small])
    delta, new_m, new_v = {"w_in": jnp.transpose(dt)}, {"w_in": jnp.transpose(mt)}, {"w_in": jnp.transpose(vt2)}
    for i, n in enumerate(small):
        delta[n], new_m[n], new_v[n] = (a[i].reshape(w[n].shape) for a in (ds, ms, vs))

    lead = lambda a: a[None]
    return (loss, dx[None], *[lead(red[n]) for n in ORDER], *[lead(delta[n]) for n in ORDER],
            *[lead(new_m[n]) for n in ORDER], *[lead(new_v[n]) for n in ORDER])
```

```python
import functools
import math

import jax
import jax.numpy as jnp
from jax import lax
from jax.experimental import pallas as pl
from jax.experimental.pallas import tpu as pltpu

F32 = jnp.float32
BF16 = jnp.bfloat16

D_MODEL = 1024
HEADS = 8
Q_RANK = 384
KV_RANK = 128
NOPE = 64
ROPE = 32
V_DIM = 64
QK_DIM = NOPE + ROPE
HEAD_PAD = 128
MLA_W = HEADS * V_DIM
SGU_W = 512
GROUPS = 8
CHUNK = 128
IN_W = 4640
RMS_EPS = 1e-6
LN_EPS = 1e-5
ALPHA = 2.0 ** 0.25
ROPE_THETA = 10000.0
SCALE = QK_DIM ** -0.5

GATE_W = 2 * D_MODEL
MID_W = 4 * SGU_W
LAT_W = Q_RANK + KV_RANK + HEAD_PAD
PAD_W = GATE_W + MID_W + LAT_W
LAT_COLS = Q_RANK + KV_RANK + ROPE
ROW_GATE = LAT_COLS + MID_W
LAT_ROWS_PAD = 704
N_SLABS = 4
SLAB_W = D_MODEL // N_SLABS

ROW_TILE = 256
MATMUL_ROW_TILE = 512
ATT_TQ = 2048
ATT_TK = 256
ATT_BWD_TQ = 512
ATT_BWD_TK = 256
LOG2E = 1.4426950408889634
LN2 = 0.6931471805599453
Q_SCALE = SCALE * LOG2E
VMEM_LIMIT = 56 * 1024 * 1024

ADAM_LR = 0.001
ADAM_B1 = 0.9
ADAM_B2 = 0.999
ADAM_EPS = 1e-08
ADAM_WD = 0.01
ADAM_STEP = 10


def _dot(a, b):
    return jnp.dot(a, b, preferred_element_type=F32)


def _dot_nt(a, b):
    return lax.dot_general(a, b, (((1,), (1,)), ((), ())), preferred_element_type=F32)


def _dot_tn(a, b):
    return lax.dot_general(a, b, (((0,), (0,)), ((), ())), preferred_element_type=F32)


def _sigmoid(z):
    return 0.5 * jnp.tanh(0.5 * z) + 0.5


_GELU_C = math.sqrt(2.0 / math.pi)


def _gelu_and_grad(x):
    x2 = x * x
    t = jnp.tanh(_GELU_C * (x + 0.044715 * x * x2))
    g = 0.5 * x * (1.0 + t)
    dg = 0.5 * (1.0 + t) + 0.5 * x * (1.0 - t * t) * (_GELU_C * (1.0 + 3.0 * 0.044715 * x2))
    return g, dg


def _silu_and_grad(z):
    s = _sigmoid(z)
    return z * s, s * (1.0 + z * (1.0 - s))


def _rope(xb, c, sl, sh):
    return xb * c + pltpu.roll(xb, 112, 1) * sl + pltpu.roll(xb, 16, 1) * sh


def _rope_t(dy, c, sl, sh):
    return dy * c + pltpu.roll(dy * sl, 16, 1) + pltpu.roll(dy * sh, 112, 1)


def _params(sem=("arbitrary",)):
    return pltpu.CompilerParams(dimension_semantics=sem, vmem_limit_bytes=VMEM_LIMIT)


def _row_spec(tile, width):
    return pl.BlockSpec((tile, width), lambda i: (i, 0))


def _full_spec(shape):
    nd = len(shape)
    return pl.BlockSpec(shape, lambda i: (0,) * nd)


def _kpe_rows(wt_ref):
    z = lambda n: jnp.zeros((n, D_MODEL), BF16)
    return jnp.concatenate([z(NOPE), wt_ref[Q_RANK + KV_RANK:LAT_COLS, :], z(HEAD_PAD - QK_DIM)], axis=0)


def _fwd_rest(xb, wt, b_g, b_m):
    s = xb.shape[0]
    ts = MATMUL_ROW_TILE

    def body(xb_ref, wt_ref, bg_ref, bm_ref, hg_ref, hm_ref):
        xb_ = xb_ref[...]
        hg_ref[...] = _dot_nt(xb_, wt_ref[ROW_GATE:IN_W, :]) + bg_ref[...]
        hm_ref[...] = _dot_nt(xb_, wt_ref[LAT_COLS:ROW_GATE, :]) + bm_ref[...]

    return pl.pallas_call(
        body, name="fwd_rest", grid=(s // ts,),
        in_specs=[_row_spec(ts, D_MODEL), _full_spec(wt.shape), _full_spec(b_g.shape), _full_spec(b_m.shape)],
        out_specs=[_row_spec(ts, GATE_W), _row_spec(ts, MID_W)],
        out_shape=[jax.ShapeDtypeStruct((s, GATE_W), F32), jax.ShapeDtypeStruct((s, MID_W), F32)],
        compiler_params=_params(),
    )(xb, wt, b_g, b_m)


def _fwd_lat(x, wlat, b_l, g_q, wuq, g_kv, wk, wv, rc, rsl, rsh, after):
    s = x.shape[0]
    ts = ROW_TILE

    def body(x_ref, wt_ref, bl_ref, gq_ref, wuq_ref, gkv_ref, wk_ref, wv_ref, rc_ref, rsl_ref,
             rsh_ref, after_ref, hl_ref, q_ref, k_ref, v_ref, xb_ref, qt_ref, kt_ref, vt_ref):
        xb = x_ref[...].astype(BF16)
        xb_ref[...] = xb
        hl = jnp.concatenate([_dot_nt(xb, wt_ref[0:Q_RANK + KV_RANK, :]), _dot_nt(xb, _kpe_rows(wt_ref))],
                             axis=1) + bl_ref[...]
        hl_ref[...] = hl
        c, sl, sh = rc_ref[...], rsl_ref[...], rsh_ref[...]
        cq = hl[:, :Q_RANK]
        cqn = cq * lax.rsqrt(jnp.mean(cq * cq, axis=-1, keepdims=True) + RMS_EPS) * gq_ref[...]
        q = _dot(cqn.astype(BF16), wuq_ref[...])
        ckv = hl[:, Q_RANK:Q_RANK + KV_RANK]
        ckvn = (ckv * lax.rsqrt(jnp.mean(ckv * ckv, axis=-1, keepdims=True) + RMS_EPS) * gkv_ref[...]).astype(BF16)
        k = _dot(ckvn, wk_ref[...])
        vb = _dot(ckvn, wv_ref[...]).astype(BF16)
        v_ref[...] = vb
        vt_ref[...] = vb.T
        kpe = _rope(hl[:, Q_RANK + KV_RANK:], c, sl, sh)
        for hd in range(HEADS):
            lanes = slice(hd * HEAD_PAD, (hd + 1) * HEAD_PAD)
            qb = (_rope(q[:, lanes], c, sl, sh) * Q_SCALE).astype(BF16)
            kb = (k[:, lanes] + kpe).astype(BF16)
            q_ref[:, lanes] = qb
            k_ref[:, lanes] = kb
            qt_ref[lanes, :] = qb.T
            kt_ref[lanes, :] = kb.T

    qk_w = HEADS * HEAD_PAD
    col_spec = lambda rows: pl.BlockSpec((rows, ts), lambda i: (0, i))
    return pl.pallas_call(
        body, name="fwd_lat", grid=(s // ts,),
        in_specs=[_row_spec(ts, D_MODEL), _full_spec(wlat.shape),
                  _full_spec(b_l.shape), _full_spec(g_q.shape),
                  _full_spec(wuq.shape), _full_spec(g_kv.shape), _full_spec(wk.shape), _full_spec(wv.shape),
                  _row_spec(ts, HEAD_PAD), _row_spec(ts, HEAD_PAD), _row_spec(ts, HEAD_PAD),
                  pl.BlockSpec(memory_space=pl.ANY)],
        out_specs=[_row_spec(ts, LAT_W), _row_spec(ts, qk_w),
                   _row_spec(ts, qk_w), _row_spec(ts, MLA_W), _row_spec(ts, D_MODEL), col_spec(qk_w), col_spec(qk_w),
                   col_spec(MLA_W)],
        out_shape=[jax.ShapeDtypeStruct((s, LAT_W), F32), jax.ShapeDtypeStruct((s, qk_w), BF16),
                   jax.ShapeDtypeStruct((s, qk_w), BF16), jax.ShapeDtypeStruct((s, MLA_W), BF16),
                   jax.ShapeDtypeStruct((s, D_MODEL), BF16), jax.ShapeDtypeStruct((qk_w, s), BF16),
                   jax.ShapeDtypeStruct((qk_w, s), BF16), jax.ShapeDtypeStruct((MLA_W, s), BF16)],
        compiler_params=_params(),
    )(x, wlat, b_l, g_q, wuq, g_kv, wk, wv, rc, rsl, rsh, after)


def _attn_fwd(qt, k, vt):
    s = k.shape[0]
    tq, tk = ATT_TQ, ATT_TK
    r = tq // tk
    pairs = HEADS // 2

    def body(qt_ref, k_ref, vt_ref, o_ref, lse_ref):
        i = pl.program_id(1)
        qts = [qt_ref[hh * HEAD_PAD:(hh + 1) * HEAD_PAD, :] for hh in range(2)]

        def scores(j, lo):
            koff = pl.multiple_of(j * tk, tk)
            return tuple(_dot(k_ref[pl.ds(koff, tk), hh * HEAD_PAD:(hh + 1) * HEAD_PAD], qts[hh][:, lo:])
                         for hh in range(2))

        def weighted(j, ps):
            koff = pl.multiple_of(j * tk, tk)
            return tuple(_dot(vt_ref[hh * V_DIM:(hh + 1) * V_DIM, pl.ds(koff, tk)], ps[hh]) for hh in range(2))

        def from_lane(full, lo, part):
            return part if lo == 0 else jnp.concatenate([full[:, :lo], part], axis=1)

        def step(j, carry, diag, last):
            st, ps, stats = carry
            lo = 0 if diag is None else diag * tk
            lo_prev = 0 if not diag else (diag - 1) * tk
            st_next = None if last else scores(j + 1, 0 if diag is None else lo + tk)
            pvs = weighted(jnp.maximum(j - 1, 0), ps)
            new_ps, new_stats = [], []
            for hh in range(2):
                m, l, acc = stats[hh]
                s_ = st[hh]
                if diag is not None:
                    krow = lax.broadcasted_iota(jnp.int32, s_.shape, 0)
                    qcol = lax.broadcasted_iota(jnp.int32, s_.shape, 1)
                    s_ = jnp.where(krow <= qcol, s_, -jnp.inf)
                acc = from_lane(acc, lo_prev, acc[:, lo_prev:] + pvs[hh])
                m_old = m[:, lo:]
                m_new = jnp.maximum(m_old, jnp.max(s_, axis=0, keepdims=True))
                a = jnp.exp2(m_old - m_new)
                p = jnp.exp2(s_ - m_new)
                new_stats.append((from_lane(m, lo, m_new),
                                  from_lane(l, lo, a * l[:, lo:] + jnp.sum(p, axis=0, keepdims=True)),
                                  from_lane(acc, lo, a * acc[:, lo:])))
                new_ps.append(p.astype(BF16))
            return st_next, tuple(new_ps), tuple(new_stats)

        one = (jnp.full((1, tq), -jnp.inf, F32), jnp.zeros((1, tq), F32), jnp.zeros((V_DIM, tq), F32))
        zero_p = jnp.zeros((tk, tq), BF16)
        nfull = i * r
        carry = lax.fori_loop(0, nfull, functools.partial(step, diag=None, last=False),
                              (scores(0, 0), (zero_p, zero_p), (one, one)))
        for d in range(r):
            carry = step(nfull + d, carry, d, d == r - 1)
        _, ps, stats = carry
        pvs = weighted(nfull + r - 1, ps)
        lo = (r - 1) * tk
        ot = jnp.concatenate([from_lane(stats[hh][2], lo, stats[hh][2][:, lo:] + pvs[hh]) / stats[hh][1]
                              for hh in range(2)], axis=0)
        o_ref[...] = ot.T
        lse = [stats[hh][0] + jnp.log(stats[hh][1]) * LOG2E for hh in range(2)]
        lse_ref[...] = jnp.concatenate(lse + [jnp.zeros((6, tq), F32)], axis=0)

    return pl.pallas_call(
        body, name="attn_fwd", grid=(pairs, s // tq),
        in_specs=[pl.BlockSpec((2 * HEAD_PAD, tq), lambda p, i: (p, i)),
                  pl.BlockSpec((s, 2 * HEAD_PAD), lambda p, i: (0, p)),
                  pl.BlockSpec((2 * V_DIM, s), lambda p, i: (p, 0))],
        out_specs=[pl.BlockSpec((tq, 2 * V_DIM), lambda p, i: (i, p)),
                   pl.BlockSpec((None, 8, tq), lambda p, i: (p, 0, i))],
        out_shape=[jax.ShapeDtypeStruct((s, MLA_W), F32), jax.ShapeDtypeStruct((pairs, 8, s), F32)],
        compiler_params=_params(("arbitrary", "arbitrary")),
    )(qt, k, vt)


def _attn_bwd(q, qt, k, kt, v, do, dot, lse, delta):
    s = k.shape[0]
    tk = ATT_BWD_TK
    nk = s // tk
    pairs = HEADS // 2

    def body(q_ref, qt_ref, k_ref, kt_ref, v_ref, do_ref, dot_ref, lse_ref, dl_ref, dqt_ref, dk_ref, dv_ref):
        krow = lax.broadcasted_iota(jnp.int32, (tk, tk), 0)
        qcol = lax.broadcasted_iota(jnp.int32, (tk, tk), 1)
        lane = lax.broadcasted_iota(jnp.int32, (tk, 2 * V_DIM), 1)
        drow = lax.broadcasted_iota(jnp.int32, (2 * V_DIM, s), 0)
        dotb = dot_ref[...]
        dots = [jnp.where((drow < V_DIM) if hh == 0 else (drow >= V_DIM), dotb, jnp.zeros_like(dotb))
                for hh in range(2)]
        for j in range(nk):
            lo = j * tk
            vb = v_ref[lo:lo + tk, :]
            dob = do_ref[lo:, :]
            dvs = []
            for hh in range(2):
                rows = slice(hh * HEAD_PAD, (hh + 1) * HEAD_PAD)
                st = _dot(k_ref[lo:lo + tk, rows], qt_ref[rows, lo:])
                diag = jnp.where(krow <= qcol, st[:, :tk], -jnp.inf)
                st = diag if j == nk - 1 else jnp.concatenate([diag, st[:, tk:]], axis=1)
                p = jnp.exp2(st - lse_ref[hh:hh + 1, lo:])
                dpt = _dot(vb, dots[hh][:, lo:])
                dst = (p * (dpt - dl_ref[hh:hh + 1, lo:])).astype(BF16)
                dvs.append(_dot(p.astype(BF16), dob))
                dk_ref[lo:lo + tk, rows] = _dot(dst, q_ref[lo:, rows]) * LN2
                dqt = _dot(kt_ref[rows, lo:lo + tk], dst)
                if j == 0:
                    dqt_ref[rows, :] = dqt
                else:
                    dqt_ref[rows, lo:] += dqt
            dv_ref[lo:lo + tk, :] = jnp.where(lane < V_DIM, dvs[0], dvs[1])
        dqt_ref[...] = dqt_ref[...] * SCALE

    pair_rows = lambda w: pl.BlockSpec((s, w), lambda p: (0, p))
    pair_cols = lambda w: pl.BlockSpec((w, s), lambda p: (p, 0))
    stats = pl.BlockSpec((None, 8, s), lambda p: (p, 0, 0))
    return pl.pallas_call(
        body, name="attn_bwd", grid=(pairs,),
        in_specs=[pair_rows(2 * HEAD_PAD), pair_cols(2 * HEAD_PAD), pair_rows(2 * HEAD_PAD), pair_cols(2 * HEAD_PAD),
                  pair_rows(2 * V_DIM), pair_rows(2 * V_DIM), pair_cols(2 * V_DIM), stats, stats],
        out_specs=[pair_cols(2 * HEAD_PAD), pair_rows(2 * HEAD_PAD), pair_rows(2 * V_DIM)],
        out_shape=[jax.ShapeDtypeStruct((HEADS * HEAD_PAD, s), F32), jax.ShapeDtypeStruct((s, HEADS * HEAD_PAD), F32),
                   jax.ShapeDtypeStruct((s, MLA_W), F32)],
        compiler_params=_params(("arbitrary",)),
    )(q, qt, k, kt, v, do, dot, lse, delta)


def _attn_bwd_dynamic(q, qt, k, kt, v, do, dot, lse, delta):
    s = k.shape[0]
    tq, tk = ATT_BWD_TQ, ATT_BWD_TK
    r = tq // tk
    nq = s // tq
    nk = s // tk
    pairs = HEADS // 2

    def body(q_ref, qt_ref, k_ref, kt_ref, v_ref, do_ref, dot_ref, lse_ref, dl_ref, dqt_ref, dk_ref, dv_ref):
        j = pl.program_id(1)
        krow = lax.broadcasted_iota(jnp.int32, (tk, tq), 0)
        qcol = lax.broadcasted_iota(jnp.int32, (tk, tq), 1)
        lane = lax.broadcasted_iota(jnp.int32, (tk, 2 * V_DIM), 1)
        drow = lax.broadcasted_iota(jnp.int32, (2 * V_DIM, tq), 0)

        @pl.when(j == 0)
        def _():
            dqt_ref[...] = jnp.zeros_like(dqt_ref)

        koff = pl.multiple_of(j * tk, tk)
        vb = v_ref[pl.ds(koff, tk), :]
        kbs = [k_ref[pl.ds(koff, tk), hh * HEAD_PAD:(hh + 1) * HEAD_PAD] for hh in range(2)]
        ktbs = [kt_ref[hh * HEAD_PAD:(hh + 1) * HEAD_PAD, pl.ds(koff, tk)] for hh in range(2)]
        i0 = j // r

        def front(i):
            qoff = pl.multiple_of(i * tq, tq)
            dotb = dot_ref[:, pl.ds(qoff, tq)]
            out = []
            for hh in range(2):
                mine = (drow < V_DIM) if hh == 0 else (drow >= V_DIM)
                st = _dot(kbs[hh], qt_ref[hh * HEAD_PAD:(hh + 1) * HEAD_PAD, pl.ds(qoff, tq)])
                out.append((st, _dot(vb, jnp.where(mine, dotb, jnp.zeros_like(dotb)))))
            return tuple(out)

        def middle(i, tiles, diag):
            qoff = pl.multiple_of(i * tq, tq)
            out = []
            for hh in range(2):
                st, dpt = tiles[hh]
                if diag:
                    st = jnp.where(krow + (j - i0 * r) * tk <= qcol, st, -jnp.inf)
                p = jnp.exp2(st - lse_ref[hh:hh + 1, pl.ds(qoff, tq)])
                out.append((p.astype(BF16), (p * (dpt - dl_ref[hh:hh + 1, pl.ds(qoff, tq)])).astype(BF16)))
            return tuple(out)

        def back(i, pd, accs):
            qoff = pl.multiple_of(i * tq, tq)
            dob = do_ref[pl.ds(qoff, tq), :]
            out = []
            for hh in range(2):
                rows = slice(hh * HEAD_PAD, (hh + 1) * HEAD_PAD)
                p, dst = pd[hh]
                dk_acc, dv_acc = accs[hh]
                dv_acc = dv_acc + _dot(p, dob)
                dk_acc = dk_acc + _dot(dst, q_ref[pl.ds(qoff, tq), rows])
                dqt_ref[rows, pl.ds(qoff, tq)] += _dot(ktbs[hh], dst)
                out.append((dk_acc, dv_acc))
            return tuple(out)

        def step(i, accs, diag):
            return back(i, middle(i, front(i), diag), accs)

        zero_acc = (jnp.zeros((tk, HEAD_PAD), F32), jnp.zeros((tk, 2 * V_DIM), F32))
        accs = step(i0, (zero_acc, zero_acc), True)
        accs = lax.fori_loop(i0 + 1, nq, functools.partial(step, diag=False), accs)
        for hh in range(2):
            dk_ref[:, hh * HEAD_PAD:(hh + 1) * HEAD_PAD] = accs[hh][0] * LN2
        dv_ref[...] = jnp.where(lane < V_DIM, accs[0][1], accs[1][1])

        @pl.when(j == nk - 1)
        def _():
            dqt_ref[...] = dqt_ref[...] * SCALE

    pair_rows = lambda w: pl.BlockSpec((s, w), lambda p, j: (0, p))
    pair_cols = lambda w: pl.BlockSpec((w, s), lambda p, j: (p, 0))
    stats = pl.BlockSpec((None, 8, s), lambda p, j: (p, 0, 0))
    return pl.pallas_call(
        body, name="attn_bwd", grid=(pairs, nk),
        in_specs=[pair_rows(2 * HEAD_PAD), pair_cols(2 * HEAD_PAD), pair_rows(2 * HEAD_PAD), pair_cols(2 * HEAD_PAD),
                  pair_rows(2 * V_DIM), pair_rows(2 * V_DIM), pair_cols(2 * V_DIM), stats, stats],
        out_specs=[pair_cols(2 * HEAD_PAD),
                   pl.BlockSpec((tk, 2 * HEAD_PAD), lambda p, j: (j, p)),
                   pl.BlockSpec((tk, 2 * V_DIM), lambda p, j: (j, p))],
        out_shape=[jax.ShapeDtypeStruct((HEADS * HEAD_PAD, s), F32), jax.ShapeDtypeStruct((s, HEADS * HEAD_PAD), F32),
                   jax.ShapeDtypeStruct((s, MLA_W), F32)],
        compiler_params=_params(("arbitrary", "arbitrary")),
    )(q, qt, k, kt, v, do, dot, lse, delta)


def _split3(a):
    hi = a.astype(BF16)
    r1 = a - hi.astype(F32)
    mid = r1.astype(BF16)
    lo = (r1 - mid.astype(F32)).astype(BF16)
    return hi, mid, lo


def _mid(x, tgt, o, hm, hg, woa, wob, wout, ln_g, ln_b, sg_g, sg_b, w_s, bsb):
    s = x.shape[0]
    ts = ROW_TILE
    nsteps = s // ts
    nch = ts // CHUNK
    npair = GROUPS // 2

    def body(x_ref, t_ref, o_ref, hm_ref, hg_ref, woa_ref, wob_ref, wout_ref, lng_ref, lnb_ref, sgg_ref, sgb_ref,
             ws_ref, bsb_ref,
             dr_ref, dhg_ref, dhm_ref, do_ref, dot_ref, dl_ref, dhgt_ref, dhmt_ref,
             dwout_ref, dwoa_ref, dwob_ref, dws_ref, dbs_ref, dlng_ref, dlnb_ref, dsgg_ref, dsgb_ref, loss_ref,
             dbg_ref, dbm_ref, dbacc_ref):
        i = pl.program_id(0)

        @pl.when(i == 0)
        def _():
            for r in (dwout_ref, dwoa_ref, dwob_ref, dws_ref, dlng_ref, dlnb_ref, dsgg_ref, dsgb_ref, loss_ref,
                      dbg_ref, dbm_ref, dbacc_ref):
                r[...] = jnp.zeros_like(r)

        def emit(ref, tref, bref, lo, val):
            vb = val.astype(BF16)
            n = val.shape[1]
            ref[:, lo:lo + n] = vb
            tref[lo:lo + n, :] = vb.T
            bref[:, lo:lo + n] += jnp.sum(val, axis=0, keepdims=True)

        lane = lax.broadcasted_iota(jnp.int32, (CHUNK, CHUNK), 1)
        left = lane < V_DIM
        tril = lax.broadcasted_iota(jnp.int32, (CHUNK, CHUNK), 0) >= lane
        ms = [jnp.where(tril, ws_ref[g], 0.0).astype(BF16) for g in range(GROUPS)]

        z_a = hm_ref[:, 0:SGU_W]
        u = hm_ref[:, SGU_W:2 * SGU_W]
        v = hm_ref[:, 2 * SGU_W:3 * SGU_W]
        z_b = hm_ref[:, 3 * SGU_W:4 * SGU_W]
        o = o_ref[...]
        sa, dsa = _silu_and_grad(z_a)
        y_a = (o * sa).astype(BF16)
        gu, dgu = _gelu_and_grad(u)
        gv, dgv = _gelu_and_grad(v)
        mu = jnp.mean(gv, axis=-1, keepdims=True)
        vc = gv - mu
        rstd_v = lax.rsqrt(jnp.mean(vc * vc, axis=-1, keepdims=True) + LN_EPS)
        vhat = vc * rstd_v
        vn = (vhat * sgg_ref[...] + sgb_ref[...]).astype(BF16)
        rows = []
        for c in range(nch):
            blocks = []
            for p in range(npair):
                blk = vn[c * CHUNK:(c + 1) * CHUNK, p * CHUNK:(p + 1) * CHUNK]
                blocks.append(jnp.where(left, _dot(ms[2 * p], blk), _dot(ms[2 * p + 1], blk)))
            rows.append(jnp.concatenate(blocks, axis=1) + bsb_ref[...])
        mixed = jnp.concatenate(rows, axis=0)
        sgu = gu * mixed
        sb, dsb = _silu_and_grad(z_b)
        y_b = (sgu * sb).astype(BF16)
        pa = jnp.concatenate([_dot(y_a, woa_ref[k]) for k in range(N_SLABS)], axis=1)
        pb = jnp.concatenate([_dot(y_b, wob_ref[k]) for k in range(N_SLABS)], axis=1)
        sga = _sigmoid(hg_ref[:, :D_MODEL])
        sgb = _sigmoid(hg_ref[:, D_MODEL:])
        m2 = (sga * pa + sgb * pb).astype(BF16)
        r = ALPHA * x_ref[...] + _dot(m2, wout_ref[...])
        rmu = jnp.mean(r, axis=-1, keepdims=True)
        rc = r - rmu
        rstd = lax.rsqrt(jnp.mean(rc * rc, axis=-1, keepdims=True) + LN_EPS)
        xhat = rc * rstd
        y = xhat * lng_ref[...] + lnb_ref[...]
        err = y - t_ref[...]
        loss_ref[...] += jnp.full(loss_ref.shape, 0.5 / D_MODEL, F32) * jnp.sum(err * err)

        dy = err * (1.0 / D_MODEL)
        dlng_ref[...] += jnp.sum(dy * xhat, axis=0, keepdims=True)
        dlnb_ref[...] += jnp.sum(dy, axis=0, keepdims=True)
        dxh = dy * lng_ref[...]
        dr = rstd * (dxh - jnp.mean(dxh, axis=-1, keepdims=True) - xhat * jnp.mean(dxh * xhat, axis=-1, keepdims=True))
        dr_ref[...] = dr
        drb = dr.astype(BF16)
        dwout_ref[...] += _dot_tn(m2, drb)
        dm2 = _dot_nt(drb, wout_ref[...])
        emit(dhg_ref, dhgt_ref, dbg_ref, 0, dm2 * pa * sga * (1.0 - sga))
        emit(dhg_ref, dhgt_ref, dbg_ref, D_MODEL, dm2 * pb * sgb * (1.0 - sgb))
        dpa = (dm2 * sga).astype(BF16)
        dpb = (dm2 * sgb).astype(BF16)
        dy_a = jnp.zeros((ts, MLA_W), F32)
        dy_b = jnp.zeros((ts, SGU_W), F32)
        y_at, y_bt = y_a.T, y_b.T
        for k in range(N_SLABS):
            cols = slice(k * SLAB_W, (k + 1) * SLAB_W)
            dwoa_ref[k] += _dot(y_at, dpa[:, cols])
            dwob_ref[k] += _dot(y_bt, dpb[:, cols])
            dy_a = dy_a + _dot_nt(dpa[:, cols], woa_ref[k])
            dy_b = dy_b + _dot_nt(dpb[:, cols], wob_ref[k])
        dob = (dy_a * sa).astype(BF16)
        do_ref[...] = dob
        dot_ref[...] = dob.T
        head = (lax.broadcasted_iota(jnp.int32, (HEADS, MLA_W), 1) // V_DIM
                == lax.broadcasted_iota(jnp.int32, (HEADS, MLA_W), 0)).astype(BF16)
        dl_ref[...] = sum(_dot_nt(head, term) for term in _split3(dob.astype(F32) * o))
        emit(dhm_ref, dhmt_ref, dbm_ref, 0, dy_a * o * dsa)
        dsg = dy_b * sb
        emit(dhm_ref, dhmt_ref, dbm_ref, 3 * SGU_W, dy_b * sgu * dsb)
        emit(dhm_ref, dhmt_ref, dbm_ref, SGU_W, dsg * mixed * dgu)
        dmixed = dsg * gu
        dvn_rows = []
        dbs_sum = jnp.zeros((CHUNK, SGU_W), F32)
        for c in range(nch):
            dm_c = dmixed[c * CHUNK:(c + 1) * CHUNK, :]
            dbs_sum = dbs_sum + dm_c
            blocks = []
            for p in range(npair):
                dmb = dm_c[:, p * CHUNK:(p + 1) * CHUNK].astype(BF16)
                blk = vn[c * CHUNK:(c + 1) * CHUNK, p * CHUNK:(p + 1) * CHUNK]
                blocks.append(jnp.where(left, _dot_tn(ms[2 * p], dmb), _dot_tn(ms[2 * p + 1], dmb)))
                zero = jnp.zeros_like(dmb)
                dws_ref[2 * p] += jnp.where(tril, _dot_nt(jnp.where(left, dmb, zero), blk), 0.0)
                dws_ref[2 * p + 1] += jnp.where(tril, _dot_nt(jnp.where(left, zero, dmb), blk), 0.0)
            dvn_rows.append(jnp.concatenate(blocks, axis=1))
        dbacc_ref[...] += dbs_sum
        dvn = jnp.concatenate(dvn_rows, axis=0)
        dsgg_ref[...] += jnp.sum(dvn * vhat, axis=0, keepdims=True)
        dsgb_ref[...] += jnp.sum(dvn, axis=0, keepdims=True)
        dvh = dvn * sgg_ref[...]
        dgv_in = rstd_v * (dvh - jnp.mean(dvh, axis=-1, keepdims=True)
                           - vhat * jnp.mean(dvh * vhat, axis=-1, keepdims=True))
        emit(dhm_ref, dhmt_ref, dbm_ref, 2 * SGU_W, dgv_in * dgv)

        @pl.when(i == nsteps - 1)
        def _():
            grp = (lax.broadcasted_iota(jnp.int32, (SGU_W, CHUNK), 0) // V_DIM
                   == lax.broadcasted_iota(jnp.int32, (SGU_W, CHUNK), 1)).astype(BF16)
            hi, mid, lo = _split3(dbacc_ref[...])
            dbs_ref[...] = _dot(hi, grp) + _dot(mid, grp) + _dot(lo, grp)

    acc_shapes = [(D_MODEL, D_MODEL), woa.shape, wob.shape, (GROUPS, CHUNK, CHUNK), (CHUNK, CHUNK),
                  (1, D_MODEL), (1, D_MODEL), (1, SGU_W), (1, SGU_W), (1, 128), (1, GATE_W), (1, MID_W)]
    col_spec = lambda rows: pl.BlockSpec((rows, ts), lambda i: (0, i))
    return pl.pallas_call(
        body, name="mid", grid=(nsteps,),
        in_specs=[_row_spec(ts, D_MODEL), _row_spec(ts, D_MODEL), _row_spec(ts, MLA_W), _row_spec(ts, MID_W),
                  _row_spec(ts, GATE_W), _full_spec(woa.shape), _full_spec(wob.shape), _full_spec(wout.shape),
                  _full_spec(ln_g.shape), _full_spec(ln_b.shape), _full_spec(sg_g.shape), _full_spec(sg_b.shape),
                  _full_spec(w_s.shape), _full_spec(bsb.shape)],
        out_specs=[_row_spec(ts, D_MODEL), _row_spec(ts, GATE_W), _row_spec(ts, MID_W), _row_spec(ts, MLA_W),
                   col_spec(MLA_W), col_spec(HEADS), col_spec(GATE_W), col_spec(MID_W)]
        + [_full_spec(sh) for sh in acc_shapes],
        out_shape=[jax.ShapeDtypeStruct((s, D_MODEL), F32), jax.ShapeDtypeStruct((s, GATE_W), BF16),
                   jax.ShapeDtypeStruct((s, MID_W), BF16), jax.ShapeDtypeStruct((s, MLA_W), BF16),
                   jax.ShapeDtypeStruct((MLA_W, s), BF16), jax.ShapeDtypeStruct((HEADS, s), F32),
                   jax.ShapeDtypeStruct((GATE_W, s), BF16), jax.ShapeDtypeStruct((MID_W, s), BF16)]
        + [jax.ShapeDtypeStruct(sh, F32) for sh in acc_shapes],
        scratch_shapes=[pltpu.VMEM((CHUNK, SGU_W), F32)],
        compiler_params=_params(),
    )(x, tgt, o, hm, hg, woa, wob, wout, ln_g, ln_b, sg_g, sg_b, w_s, bsb)


def _lat_bwd(dq, dk, dv, hl, rc, rsl, rsh, g_q, g_kv, wuq, wk, wv, after):
    s = dk.shape[0]
    ts = ROW_TILE
    qk_w = HEADS * HEAD_PAD

    def body(dq_ref, dk_ref, dv_ref, hl_ref, rc_ref, rsl_ref, rsh_ref, gq_ref, gkv_ref, wuq_ref, wk_ref, wv_ref,
             after_ref, dhl_ref, dhlt_ref, dwuq_ref, dwk_ref, dwv_ref, dgq_ref, dgkv_ref, dbl_ref):
        i = pl.program_id(0)

        @pl.when(i == 0)
        def _():
            for r in (dwuq_ref, dwk_ref, dwv_ref, dgq_ref, dgkv_ref, dbl_ref):
                r[...] = jnp.zeros_like(r)

        def emit(lo, val):
            vb = val.astype(BF16)
            n = val.shape[1]
            dhl_ref[:, lo:lo + n] = vb
            dhlt_ref[lo:lo + n, :] = vb.T
            dbl_ref[:, lo:lo + n] += jnp.sum(val, axis=0, keepdims=True)

        c, sl, sh = rc_ref[...], rsl_ref[...], rsh_ref[...]
        lane = lax.broadcasted_iota(jnp.int32, (ts, HEAD_PAD), 1)
        pe = (lane >= NOPE) & (lane < QK_DIM)
        dkpe = jnp.zeros((ts, HEAD_PAD), F32)
        dqu = []
        for hd in range(HEADS):
            lanes = slice(hd * HEAD_PAD, (hd + 1) * HEAD_PAD)
            dqu.append(_rope_t(dq_ref[lanes, :].T, c, sl, sh).astype(BF16))
            dkpe = dkpe + dk_ref[:, lanes]
        dqu = jnp.concatenate(dqu, axis=1)
        dkpe = _rope_t(jnp.where(pe, dkpe, 0.0), c, sl, sh)

        cq = hl_ref[:, :Q_RANK]
        rq = lax.rsqrt(jnp.mean(cq * cq, axis=-1, keepdims=True) + RMS_EPS)
        cqh = cq * rq
        cqn = (cqh * gq_ref[...]).astype(BF16)
        dwuq_ref[...] += _dot_tn(cqn, dqu)
        dcqn = _dot_nt(dqu, wuq_ref[...])
        dgq_ref[...] += jnp.sum(dcqn * cqh, axis=0, keepdims=True)
        dch = dcqn * gq_ref[...]
        emit(0, rq * (dch - cqh * jnp.mean(dch * cqh, axis=-1, keepdims=True)))

        ckv = hl_ref[:, Q_RANK:Q_RANK + KV_RANK]
        rk = lax.rsqrt(jnp.mean(ckv * ckv, axis=-1, keepdims=True) + RMS_EPS)
        ckh = ckv * rk
        ckn = (ckh * gkv_ref[...]).astype(BF16)
        dkb = dk_ref[...].astype(BF16)
        dvb = dv_ref[...].astype(BF16)
        dwk_ref[...] += _dot_tn(ckn, dkb)
        dwv_ref[...] += _dot_tn(ckn, dvb)
        dckn = _dot_nt(dkb, wk_ref[...]) + _dot_nt(dvb, wv_ref[...])
        dgkv_ref[...] += jnp.sum(dckn * ckh, axis=0, keepdims=True)
        dkh = dckn * gkv_ref[...]
        emit(Q_RANK, rk * (dkh - ckh * jnp.mean(dkh * ckh, axis=-1, keepdims=True)))
        emit(Q_RANK + KV_RANK, dkpe)

    acc_shapes = [wuq.shape, wk.shape, wv.shape, g_q.shape, g_kv.shape, (1, LAT_W)]
    return pl.pallas_call(
        body, name="lat_bwd", grid=(s // ts,),
        in_specs=[pl.BlockSpec((qk_w, ts), lambda i: (0, i)), _row_spec(ts, qk_w), _row_spec(ts, MLA_W),
                  _row_spec(ts, LAT_W), _row_spec(ts, HEAD_PAD), _row_spec(ts, HEAD_PAD), _row_spec(ts, HEAD_PAD),
                  _full_spec(g_q.shape), _full_spec(g_kv.shape), _full_spec(wuq.shape), _full_spec(wk.shape),
                  _full_spec(wv.shape), pl.BlockSpec(memory_space=pl.ANY)],
        out_specs=[_row_spec(ts, LAT_W), pl.BlockSpec((LAT_W, ts), lambda i: (0, i))]
        + [_full_spec(sh) for sh in acc_shapes],
        out_shape=[jax.ShapeDtypeStruct((s, LAT_W), BF16), jax.ShapeDtypeStruct((LAT_W, s), BF16)]
        + [jax.ShapeDtypeStruct(sh, F32) for sh in acc_shapes],
        compiler_params=_params(),
    )(dq, dk, dv, hl, rc, rsl, rsh, g_q, g_kv, wuq, wk, wv, after)


def _dx(dr, dhg, dhm, dhl, wt, after):
    s = dr.shape[0]
    ts = MATMUL_ROW_TILE

    def body(dr_ref, dhg_ref, dhm_ref, dhl_ref, wt_ref, after_ref, dx_ref):
        dx_ref[...] = (ALPHA * dr_ref[...]
                       + _dot(dhg_ref[...], wt_ref[ROW_GATE:IN_W, :])
                       + _dot(dhm_ref[...], wt_ref[LAT_COLS:ROW_GATE, :])
                       + _dot(dhl_ref[:, 0:Q_RANK + KV_RANK], wt_ref[0:Q_RANK + KV_RANK, :])
                       + _dot(dhl_ref[:, Q_RANK + KV_RANK:], _kpe_rows(wt_ref)))

    return pl.pallas_call(
        body, name="dx", grid=(s // ts,),
        in_specs=[_row_spec(ts, D_MODEL), _row_spec(ts, GATE_W), _row_spec(ts, MID_W), _row_spec(ts, LAT_W),
                  _full_spec(wt.shape), pl.BlockSpec(memory_space=pl.ANY)],
        out_specs=_row_spec(ts, D_MODEL),
        out_shape=jax.ShapeDtypeStruct((s, D_MODEL), F32),
        compiler_params=_params(),
    )(dr, dhg, dhm, dhl, wt, after)


def _dwt_early(dhmt, dhgt, xb, col, after, name):
    tn = 512
    nm, ng = MID_W // tn, GATE_W // tn
    s = dhmt.shape[1]
    hc = D_MODEL // 2

    def body(col_ref, dm_ref, dg_ref, xb_ref, after_ref, dw_ref):
        i = pl.program_id(0)

        @pl.when(i < nm)
        def _():
            dw_ref[...] = _dot(dm_ref[...], xb_ref[...]).astype(BF16)

        @pl.when(i >= nm)
        def _():
            dw_ref[...] = _dot(dg_ref[...], xb_ref[...]).astype(BF16)

    rows = pl.pallas_call(
        body, name=name,
        grid_spec=pltpu.PrefetchScalarGridSpec(
            num_scalar_prefetch=1, grid=(nm + ng,),
            in_specs=[pl.BlockSpec((tn, s), lambda i, col_ref: (jnp.minimum(i, nm - 1), 0)),
                      pl.BlockSpec((tn, s), lambda i, col_ref: (jnp.maximum(i - nm, 0), 0)),
                      pl.BlockSpec((s, hc), lambda i, col_ref: (0, col_ref[0])),
                      pl.BlockSpec(memory_space=pl.ANY)],
            out_specs=pl.BlockSpec((pl.Element(tn), pl.Element(hc)),
                                   lambda i, col_ref: (pl.multiple_of(LAT_COLS + i * tn, 32), 0))),
        out_shape=jax.ShapeDtypeStruct((IN_W, hc), BF16),
        compiler_params=_params(),
    )(col, dhmt, dhgt, xb, after)

    def zero(buf_ref, out_ref):
        out_ref[...] = jnp.zeros_like(out_ref)

    return pl.pallas_call(
        zero, name=name + "_zero_lat", grid=(1,), in_specs=[pl.BlockSpec(memory_space=pl.ANY)],
        out_specs=pl.BlockSpec((LAT_COLS, hc), lambda i: (0, 0)),
        out_shape=jax.ShapeDtypeStruct((IN_W, hc), BF16), input_output_aliases={0: 0},
    )(rows)


def _dwt_lat(dhlt, xb):
    n, s = dhlt.shape

    def body(dht_ref, xb_ref, dw_ref):
        dw = _dot(dht_ref[...], xb_ref[...]).astype(BF16)
        kpe = Q_RANK + KV_RANK + NOPE
        dw_ref[0:Q_RANK + KV_RANK, :] = dw[0:Q_RANK + KV_RANK]
        dw_ref[Q_RANK + KV_RANK:LAT_COLS, :] = dw[kpe:kpe + ROPE]
        dw_ref[LAT_COLS:, :] = jnp.zeros((LAT_ROWS_PAD - LAT_COLS, D_MODEL), BF16)

    return pl.pallas_call(
        body, name="dwt_lat", in_specs=[VMEM_SPEC, VMEM_SPEC], out_specs=VMEM_SPEC,
        out_shape=jax.ShapeDtypeStruct((LAT_ROWS_PAD, D_MODEL), BF16),
        compiler_params=pltpu.CompilerParams(vmem_limit_bytes=VMEM_LIMIT),
    )(dhlt, xb)


def _split_bias(b):
    z = lambda n: jnp.zeros((n,), b.dtype)
    lat = jnp.concatenate([b[:Q_RANK + KV_RANK], z(NOPE), b[Q_RANK + KV_RANK:LAT_COLS], z(HEAD_PAD - QK_DIM)])
    return b[None, ROW_GATE:], b[None, LAT_COLS:ROW_GATE], lat[None, :]


def _join_bias(g, m, l):
    kpe = Q_RANK + KV_RANK + NOPE
    return jnp.concatenate([l[0, :Q_RANK + KV_RANK], l[0, kpe:kpe + ROPE], m[0], g[0]])


def _rope_tables(positions):
    half = ROPE // 2
    inv_freq = ROPE_THETA ** (-jnp.arange(0, ROPE, 2, dtype=F32) / ROPE)
    ang = positions.astype(F32)[:, None] * inv_freq
    cos, sin = jnp.cos(ang), jnp.sin(ang)
    n = positions.shape[0]
    one, zero = jnp.ones((n, NOPE), F32), jnp.zeros((n, half), F32)
    tail1, tail0 = jnp.ones((n, HEAD_PAD - QK_DIM), F32), jnp.zeros((n, HEAD_PAD - QK_DIM), F32)
    z64 = jnp.zeros((n, NOPE), F32)
    rc = jnp.concatenate([one, cos, cos, tail1], axis=1)
    rsl = jnp.concatenate([z64, -sin, zero, tail0], axis=1)
    rsh = jnp.concatenate([z64, zero, sin, tail0], axis=1)
    return rc, rsl, rsh


def _local_attention(x, positions, wlat, b_in, g_q, w_uq, g_kv, w_ukv, after):
    rc, rsl, rsh = _rope_tables(positions)
    b_g, b_m, b_l = _split_bias(b_in)
    wuq = jnp.pad(w_uq, ((0, 0), (0, 0), (0, HEAD_PAD - QK_DIM))).reshape(Q_RANK, HEADS * HEAD_PAD).astype(BF16)
    wk = jnp.pad(w_ukv[:, :, :NOPE], ((0, 0), (0, 0), (0, HEAD_PAD - NOPE))).reshape(KV_RANK, HEADS * HEAD_PAD).astype(BF16)
    wv = w_ukv[:, :, NOPE:].reshape(KV_RANK, MLA_W).astype(BF16)
    gq2, gkv2 = g_q[None, :], g_kv[None, :]
    hl, q, k, v, xb, qt, kt, vt = _fwd_lat(x, wlat, b_l, gq2, wuq, gkv2, wk, wv, rc, rsl, rsh, after)
    o, lse = _attn_fwd(qt, k, vt)
    return dict(q=q, qt=qt, k=k, kt=kt, v=v, o=o, lse=lse, hl=hl, rc=rc, rsl=rsl, rsh=rsh, gq2=gq2, gkv2=gkv2,
                wuq=wuq, wk=wk, wv=wv, xb=xb, b_g=b_g, b_m=b_m)


def _local_head(st, x, tgt, wt, w_oa, sg_g, sg_b, w_s, b_s, w_ob, w_out, ln_g, ln_b):
    q, qt, k, kt, v, o, lse, hl, xb = (st[n] for n in ("q", "qt", "k", "kt", "v", "o", "lse", "hl", "xb"))
    rc, rsl, rsh, gq2, gkv2, wuq, wk, wv = (st[n] for n in ("rc", "rsl", "rsh", "gq2", "gkv2", "wuq", "wk", "wv"))
    bsb = jnp.repeat(b_s.T, V_DIM, axis=1)
    hg, hm = _fwd_rest(xb, wt, st["b_g"], st["b_m"])
    (dr, dhg, dhm, do, dot, delta, dhgt, dhmt, dwout, dwoa, dwob, dws, dbs, dlng, dlnb, dsgg, dsgb, loss, dbg,
     dbm) = _mid(x, tgt, o, hm, hg, w_oa, w_ob, w_out, ln_g[None, :], ln_b[None, :], sg_g[None, :], sg_b[None, :],
                 w_s, bsb)
    delta = jnp.pad(delta.reshape(HEADS // 2, 2, -1), ((0, 0), (0, 6), (0, 0)))
    early = {
        "w_oa": dwoa, "sgu_ln_g": dsgg[0], "sgu_ln_b": dsgb[0], "w_s": dws, "b_s": dbs[:, :GROUPS].T,
        "w_ob": dwob, "w_out": dwout, "ln_g": dlng[0], "ln_b": dlnb[0],
    }
    state = dict(q=q, qt=qt, k=k, kt=kt, v=v, do=do, dot=dot, lse=lse, delta=delta, hl=hl, rc=rc, rsl=rsl, rsh=rsh,
                 gq2=gq2, gkv2=gkv2, wuq=wuq, wk=wk, wv=wv, dr=dr, dhg=dhg, dhm=dhm, wt=wt, xb=xb, dbg=dbg, dbm=dbm,
                 dhgt=dhgt, dhmt=dhmt)
    return loss, early, state


def _local_attn_bwd(st):
    return _attn_bwd(st["q"], st["qt"], st["k"], st["kt"], st["v"], st["do"], st["dot"], st["lse"], st["delta"])


def _local_tail(st, dq, dk, dv, after):
    dhl, dhlt, dwuq, dwk, dwv, dgq, dgkv, dbl = _lat_bwd(dq, dk, dv, st["hl"], st["rc"], st["rsl"], st["rsh"],
                                                         st["gq2"], st["gkv2"], st["wuq"], st["wk"], st["wv"], after)
    late = {
        "w_lat": _dwt_lat(dhlt, st["xb"]),
        "b_in": _join_bias(st["dbg"], st["dbm"], dbl),
        "g_q": dgq[0],
        "w_uq": dwuq.reshape(Q_RANK, HEADS, HEAD_PAD)[:, :, :QK_DIM],
        "g_kv": dgkv[0],
        "w_ukv": jnp.concatenate([dwk.reshape(KV_RANK, HEADS, HEAD_PAD)[:, :, :NOPE],
                                  dwv.reshape(KV_RANK, HEADS, V_DIM)], axis=2),
    }
    return dhl, late


def _local_step(x, positions, tgt, wt, b_in, g_q, w_uq, g_kv, w_ukv, w_oa, sg_g, sg_b, w_s, b_s, w_ob, w_out, ln_g,
                ln_b):
    st = _local_attention(x, positions, wt[:LAT_COLS], b_in, g_q, w_uq, g_kv, w_ukv, b_in)
    loss, early, st = _local_head(st, x, tgt, wt, w_oa, sg_g, sg_b, w_s, b_s, w_ob, w_out, ln_g, ln_b)
    dq, dk, dv = _local_attn_bwd(st)
    dhl, late = _local_tail(st, dq, dk, dv, dv)
    dx = _dx(st["dr"], st["dhg"], st["dhm"], dhl, st["wt"], dhl)
    grads = {**early, **late}
    halves = [_dwt_early(st["dhmt"], st["dhgt"], st["xb"], jnp.full((1,), h, jnp.int32), dhl, "dwt_half%d" % h)
              for h in range(2)]
    grads["w_in"] = jnp.concatenate([grads.pop("w_lat")[:LAT_COLS], jnp.concatenate(halves, axis=1)[LAT_COLS:]],
                                    axis=0)
    return loss, dx, grads


MESH = pl.DeviceIdType.MESH
N_CHIPS = 4
HBM_SPEC = pl.BlockSpec(memory_space=pl.ANY)
HBM_SPEC_STRICT = pl.BlockSpec(memory_space=pltpu.HBM)
VMEM_SPEC = pl.BlockSpec(memory_space=pltpu.VMEM)

REP_ROWS = 80


def _rows8(a):
    flat = a.reshape(-1)
    n = -(-flat.shape[0] // (8 * D_MODEL)) * 8 * D_MODEL
    return jnp.pad(flat, (0, n - flat.shape[0])).reshape(-1, D_MODEL)


def _place():
    x, y, c = lax.axis_index("x"), lax.axis_index("y"), lax.axis_index("c")
    others = [(1 - x, y), (x, 1 - y), (1 - x, 1 - y)]
    return x, y, c, others


def _gather_weights(shards):
    n = len(shards)

    def body(*refs):
        ins, outs, bufs = refs[:n], refs[n:2 * n], refs[2 * n:3 * n]
        send_sems, recv_sems, local_sems = refs[3 * n:]
        x, y, c, others = _place()
        me = 2 * x + y
        sibling = (x, y, 1 - c)
        for src, buf in zip(ins, bufs):
            buf[...] = src[...].astype(BF16)
        own = [pltpu.make_async_copy(bufs[w], outs[w].at[me], local_sems.at[w]) for w in range(n)]
        for cp in own:
            cp.start()

        def part(w, chip, half):
            hc = shards[w].shape[1] // 2
            return outs[w].at[chip, :, pl.ds(half * hc, hc)]

        def sent(w, j):
            hc = shards[w].shape[1] // 2
            return pltpu.make_async_remote_copy(
                src_ref=bufs[w].at[:, pl.ds(c * hc, hc)], dst_ref=part(w, me, c),
                send_sem=send_sems.at[w * 3 + j], recv_sem=recv_sems.at[w * 3 + j],
                device_id=(*others[j], c), device_id_type=MESH)

        def landed(w, j):
            px, py = others[j]
            return pltpu.make_async_remote_copy(
                src_ref=part(w, 2 * px + py, c), dst_ref=part(w, 2 * px + py, c),
                send_sem=send_sems.at[w * 3 + j], recv_sem=recv_sems.at[w * 3 + j],
                device_id=(px, py, c), device_id_type=MESH)

        def passed(w, j, half):
            px, py = others[j]
            k = n * 3 + w * 3 + j
            return pltpu.make_async_remote_copy(
                src_ref=part(w, 2 * px + py, half), dst_ref=part(w, 2 * px + py, half),
                send_sem=send_sems.at[k], recv_sem=recv_sems.at[k], device_id=sibling, device_id_type=MESH)

        first = [sent(w, j) for w in range(n) for j in range(3)]
        for cp in first:
            cp.start()
        fwd = []
        for w in range(n):
            for j in range(3):
                landed(w, j).wait_recv()
                cp = passed(w, j, c)
                cp.start()
                fwd.append(cp)
        for w in range(n):
            for j in range(3):
                passed(w, j, 1 - c).wait_recv()
        for cp in first + fwd:
            cp.wait_send()
        for cp in own:
            cp.wait()

    return pl.pallas_call(
        body, name="gather_weights",
        in_specs=[VMEM_SPEC] * n, out_specs=[HBM_SPEC] * n,
        out_shape=[jax.ShapeDtypeStruct((N_CHIPS,) + s.shape, BF16) for s in shards],
        scratch_shapes=[pltpu.VMEM(s.shape, BF16) for s in shards]
        + [pltpu.SemaphoreType.DMA((6 * n,)), pltpu.SemaphoreType.DMA((6 * n,)), pltpu.SemaphoreType.DMA((n,))],
        compiler_params=pltpu.CompilerParams(vmem_limit_bytes=VMEM_LIMIT),
    )(*shards)


N_DEV = 8
LOSS_TILE = (8, 128)


def _gather_first(lat, uq):
    hl, hu = lat.shape[1] // 2, uq.shape[1] // 2

    def body(lat_ref, uq_ref, wlat_ref, guq_ref, lat_buf, uq_buf, send_sems, recv_sems, local_sems):
        x, y, c, others = _place()
        me = 2 * x + y
        sibling = (x, y, 1 - c)
        lat_buf[...] = lat_ref[...].astype(BF16)
        uq_buf[...] = uq_ref[...].astype(BF16)

        def copy(src, dst, k, to):
            return pltpu.make_async_remote_copy(src_ref=src, dst_ref=dst, send_sem=send_sems.at[k],
                                                recv_sem=recv_sems.at[k], device_id=to, device_id_type=MESH)

        def uq_part(chip, half):
            return guq_ref.at[chip, :, pl.ds(half * hu, hu)]

        def lat_part(half):
            return wlat_ref.at[:, pl.ds(half * hl, hl)]

        own = pltpu.make_async_copy(uq_buf, guq_ref.at[me], local_sems.at[0])
        own.start()
        first = [copy(uq_buf.at[:, pl.ds(c * hu, hu)], uq_part(me, c), j, (*others[j], c)) for j in range(3)]
        for cp in first:
            cp.start()

        @pl.when(me == 0)
        def _():
            mine = pltpu.make_async_copy(lat_buf, wlat_ref, local_sems.at[1])
            mine.start()
            cps = [copy(lat_buf.at[:, pl.ds(c * hl, hl)], lat_part(c), 6 + j, (*others[j], c)) for j in range(3)]
            for cp in cps:
                cp.start()
            for cp in cps:
                cp.wait_send()
            mine.wait()

        @pl.when(me != 0)
        def _():
            j0 = x + 2 * y - 1
            copy(lat_part(c), lat_part(c), 6 + j0, (0, 0, c)).wait_recv()
            fwd = copy(lat_part(c), lat_part(c), 9, sibling)
            fwd.start()
            copy(lat_part(1 - c), lat_part(1 - c), 9, sibling).wait_recv()
            fwd.wait_send()

        fwd = []
        for j, (px, py) in enumerate(others):
            chip = 2 * px + py
            copy(uq_part(chip, c), uq_part(chip, c), j, (px, py, c)).wait_recv()
            cp = copy(uq_part(chip, c), uq_part(chip, c), 3 + j, sibling)
            cp.start()
            fwd.append(cp)
        for j, (px, py) in enumerate(others):
            chip = 2 * px + py
            copy(uq_part(chip, 1 - c), uq_part(chip, 1 - c), 3 + j, sibling).wait_recv()
        for cp in first + fwd:
            cp.wait_send()
        own.wait()

    return pl.pallas_call(
        body, name="gather_first", in_specs=[VMEM_SPEC, VMEM_SPEC], out_specs=[HBM_SPEC, HBM_SPEC],
        out_shape=[jax.ShapeDtypeStruct(lat.shape, BF16), jax.ShapeDtypeStruct((N_CHIPS,) + uq.shape, BF16)],
        scratch_shapes=[pltpu.VMEM(lat.shape, BF16), pltpu.VMEM(uq.shape, BF16), pltpu.SemaphoreType.DMA((10,)),
                        pltpu.SemaphoreType.DMA((10,)), pltpu.SemaphoreType.DMA((2,))],
        compiler_params=pltpu.CompilerParams(vmem_limit_bytes=VMEM_LIMIT),
    )(lat, uq)


def _cast_own(shards, me):
    n = len(shards)

    def body(me_ref, *refs):
        for w in range(n):
            refs[n + w][...] = refs[w][...].astype(BF16)

    return pl.pallas_call(
        body, name="cast_own",
        grid_spec=pltpu.PrefetchScalarGridSpec(
            num_scalar_prefetch=1, grid=(1,),
            in_specs=[pl.BlockSpec(s.shape, lambda i, me_ref: (0, 0)) for s in shards],
            out_specs=[pl.BlockSpec((None,) + s.shape, lambda i, me_ref: (me_ref[0], 0, 0)) for s in shards]),
        out_shape=[jax.ShapeDtypeStruct((N_CHIPS,) + s.shape, BF16) for s in shards],
        compiler_params=pltpu.CompilerParams(vmem_limit_bytes=VMEM_LIMIT),
    )(me, *shards)


def _gather_start(bufs, after):
    n = len(bufs)

    def body(*refs):
        b_refs = refs[:n]
        send_sems, recv_sems, token = refs[n + 1], refs[n + 2], refs[-1]
        x, y, c, others = _place()
        me = 2 * x + y
        for w in range(n):
            hc = _half(bufs[w])
            mine = b_refs[w].at[me, :, pl.ds(c * hc, hc)]
            for j, (px, py) in enumerate(others):
                pltpu.make_async_remote_copy(
                    src_ref=mine, dst_ref=mine, send_sem=send_sems.at[3 * w + j], recv_sem=recv_sems.at[3 * w + j],
                    device_id=(px, py, c), device_id_type=MESH).start()
        token[...] = jnp.zeros_like(token)

    hbm = [pltpu.HBM(b.shape, BF16) for b in bufs]
    outs = pl.pallas_call(
        body, name="gather_start",
        out_shape=(pltpu.SemaphoreType.DMA((3 * n,)), pltpu.SemaphoreType.DMA((3 * n,)), *hbm,
                   jax.ShapeDtypeStruct(LOSS_TILE, F32)),
        in_specs=[HBM_SPEC_STRICT] * n + [HBM_SPEC],
        out_specs=(SEM_SPEC, SEM_SPEC, *[HBM_SPEC_STRICT] * n, VMEM_SPEC),
        input_output_aliases={i: 2 + i for i in range(n)},
        compiler_params=pltpu.CompilerParams(has_side_effects=SPLIT_EFFECT),
    )(*[pltpu.with_memory_space_constraint(b, pltpu.HBM) for b in bufs], after)
    return outs[0], outs[1], list(outs[2:2 + n]), outs[-1]


def _gather_wait(send_sems, recv_sems, bufs, after):
    n = len(bufs)

    def body(*refs):
        b_refs = refs[:n]
        send_sems, recv_sems = refs[n], refs[n + 1]
        x, y, c, others = _place()
        me = 2 * x + y
        for w in range(n):
            hc = _half(bufs[w])
            for j, (px, py) in enumerate(others):
                cp = pltpu.make_async_remote_copy(
                    src_ref=b_refs[w].at[me, :, pl.ds(c * hc, hc)],
                    dst_ref=b_refs[w].at[2 * px + py, :, pl.ds(c * hc, hc)],
                    send_sem=send_sems.at[3 * w + j], recv_sem=recv_sems.at[3 * w + j], device_id=(px, py, c),
                    device_id_type=MESH)
                cp.wait_send()
                cp.wait_recv()

    outs = pl.pallas_call(
        body, name="gather_wait", out_shape=tuple(pltpu.HBM(b.shape, b.dtype) for b in bufs),
        in_specs=[HBM_SPEC_STRICT] * n + [SEM_SPEC, SEM_SPEC, HBM_SPEC],
        out_specs=tuple([HBM_SPEC_STRICT] * n), input_output_aliases={i: i for i in range(n)},
        compiler_params=pltpu.CompilerParams(has_side_effects=SPLIT_EFFECT),
    )(*bufs, send_sems, recv_sems, after)
    return list(outs)


def _gather_finish(bufs):
    n = len(bufs)

    def body(*refs):
        b_refs = refs[n:2 * n]
        send_sems, recv_sems = refs[2 * n:]
        x, y, c, others = _place()
        cps = []
        for w in range(n):
            hc = _half(bufs[w])
            for j, (px, py) in enumerate(others):
                part = b_refs[w].at[2 * px + py, :, pl.ds(c * hc, hc)]
                cps.append(pltpu.make_async_remote_copy(
                    src_ref=part, dst_ref=part, send_sem=send_sems.at[3 * w + j], recv_sem=recv_sems.at[3 * w + j],
                    device_id=(x, y, 1 - c), device_id_type=MESH))
        for cp in cps:
            cp.start()
        for w in range(n):
            hc = _half(bufs[w])
            for j, (px, py) in enumerate(others):
                theirs = b_refs[w].at[2 * px + py, :, pl.ds((1 - c) * hc, hc)]
                pltpu.make_async_remote_copy(
                    src_ref=theirs, dst_ref=theirs, send_sem=send_sems.at[3 * w + j], recv_sem=recv_sems.at[3 * w + j],
                    device_id=(x, y, 1 - c), device_id_type=MESH).wait_recv()
        for cp in cps:
            cp.wait_send()

    return pl.pallas_call(
        body, name="gather_finish", in_specs=[HBM_SPEC] * n, out_specs=[HBM_SPEC] * n,
        out_shape=[jax.ShapeDtypeStruct(b.shape, b.dtype) for b in bufs],
        input_output_aliases={i: i for i in range(n)},
        scratch_shapes=[pltpu.SemaphoreType.DMA((3 * n,)), pltpu.SemaphoreType.DMA((3 * n,))],
    )(*bufs)


def _half(a):
    return a.shape[-1] // 2


def _exchange_pairs(parts, name):
    n = len(parts)

    def body(*refs):
        p_refs, r_refs = refs[:n], refs[n:2 * n]
        send_sems, recv_sems = refs[2 * n:]
        x, y, c, _ = _place()
        cps = []
        for w in range(n):
            h = _half(parts[w])
            cps.append(pltpu.make_async_remote_copy(
                src_ref=p_refs[w].at[:, :, pl.ds((1 - c) * h, h)], dst_ref=r_refs[w],
                send_sem=send_sems.at[w], recv_sem=recv_sems.at[w], device_id=(x, y, 1 - c), device_id_type=MESH))
        for cp in cps:
            cp.start()
        for cp in cps:
            cp.wait()

    return pl.pallas_call(
        body, name=name, in_specs=[HBM_SPEC] * n, out_specs=[HBM_SPEC] * n,
        out_shape=[jax.ShapeDtypeStruct((N_CHIPS, p.shape[1], _half(p)), BF16) for p in parts],
        scratch_shapes=[pltpu.SemaphoreType.DMA((n,)), pltpu.SemaphoreType.DMA((n,))],
    )(*parts)


def _sibling_part(ref, w, n_whole, shape, c):
    if w < n_whole:
        return ref
    h = shape[-1] // 2
    return ref.at[:, :, pl.ds((1 - c) * h, h)]


def _pairs_start(parts, all_loss, n_whole):
    n = len(parts)

    def body(*refs):
        p_refs, r_refs, loss_ref = refs[:n], refs[n:2 * n], refs[2 * n]
        send_sems, recv_sems, token = refs[2 * n + 1], refs[2 * n + 2], refs[-1]
        x, y, c, _ = _place()
        for w in range(n):
            h = _half(parts[w])
            pltpu.make_async_remote_copy(
                src_ref=_sibling_part(p_refs[w], w, n_whole, parts[w].shape, c), dst_ref=r_refs[w],
                send_sem=send_sems.at[w], recv_sem=recv_sems.at[w], device_id=(x, y, 1 - c),
                device_id_type=MESH).start()
        me = 4 * x + 2 * y + c
        for t in range(1, N_DEV):
            d = (me + t) % N_DEV
            pltpu.make_async_remote_copy(
                src_ref=loss_ref.at[me], dst_ref=loss_ref.at[me], send_sem=send_sems.at[n + t - 1],
                recv_sem=recv_sems.at[n + t - 1], device_id=(d // 4, (d // 2) % 2, d % 2), device_id_type=MESH).start()
        token[...] = jnp.zeros_like(token)

    lands = [pltpu.HBM(p.shape if w < n_whole else (N_CHIPS, p.shape[1], _half(p)), BF16)
             for w, p in enumerate(parts)]
    nsem = n + N_DEV - 1
    outs = pl.pallas_call(
        body, name="pairs_start",
        out_shape=(pltpu.SemaphoreType.DMA((nsem,)), pltpu.SemaphoreType.DMA((nsem,)),
                   *[pltpu.HBM(p.shape, p.dtype) for p in parts], *lands, pltpu.HBM(all_loss.shape, F32),
                   jax.ShapeDtypeStruct(LOSS_TILE, F32)),
        in_specs=[HBM_SPEC_STRICT] * (2 * n + 1),
        out_specs=(SEM_SPEC, SEM_SPEC, *[HBM_SPEC_STRICT] * (2 * n + 1), VMEM_SPEC),
        input_output_aliases={i: 2 + i for i in range(2 * n + 1)},
        compiler_params=pltpu.CompilerParams(has_side_effects=SPLIT_EFFECT),
    )(*[pltpu.with_memory_space_constraint(p, pltpu.HBM) for p in parts],
      *[pltpu.with_memory_space_constraint(lax.empty(l.shape, BF16), pltpu.HBM) for l in lands],
      pltpu.with_memory_space_constraint(all_loss, pltpu.HBM))
    return outs[0], outs[1], list(outs[2:2 + n]), list(outs[2 + n:2 + 2 * n]), outs[2 + 2 * n], outs[-1]


def _pairs_wait(send_sems, recv_sems, parts, lands, all_loss, after, n_whole):
    n = len(parts)

    def body(*refs):
        p_refs, r_refs, loss_ref = refs[:n], refs[n:2 * n], refs[2 * n]
        send_sems, recv_sems = refs[2 * n + 1], refs[2 * n + 2]
        x, y, c, _ = _place()
        for w in range(n):
            h = _half(parts[w])
            cp = pltpu.make_async_remote_copy(
                src_ref=_sibling_part(p_refs[w], w, n_whole, parts[w].shape, c), dst_ref=r_refs[w],
                send_sem=send_sems.at[w],
                recv_sem=recv_sems.at[w], device_id=(x, y, 1 - c), device_id_type=MESH)
            cp.wait_send()
            cp.wait_recv()
        me = 4 * x + 2 * y + c
        for t in range(1, N_DEV):
            d = (me + N_DEV - t) % N_DEV
            cp = pltpu.make_async_remote_copy(
                src_ref=loss_ref.at[me], dst_ref=loss_ref.at[d], send_sem=send_sems.at[n + t - 1],
                recv_sem=recv_sems.at[n + t - 1], device_id=(d // 4, (d // 2) % 2, d % 2), device_id_type=MESH)
            cp.wait_send()
            cp.wait_recv()

    bufs = (*parts, *lands, all_loss)
    outs = pl.pallas_call(
        body, name="pairs_wait", out_shape=tuple(pltpu.HBM(a.shape, a.dtype) for a in bufs),
        in_specs=[HBM_SPEC_STRICT] * len(bufs) + [SEM_SPEC, SEM_SPEC, HBM_SPEC],
        out_specs=tuple([HBM_SPEC_STRICT] * len(bufs)), input_output_aliases={i: i for i in range(len(bufs))},
        compiler_params=pltpu.CompilerParams(has_side_effects=SPLIT_EFFECT),
    )(*bufs, send_sems, recv_sems, after)
    return list(outs[:n]), list(outs[n:2 * n]), outs[2 * n]


def _add_pair_tiled(p, r):
    rows, h = r.shape[1:]

    def body(p_ref, r_ref, q_ref):
        q_ref[...] = (p_ref[...].astype(F32) + r_ref[...].astype(F32)).astype(BF16)

    spec = pl.BlockSpec((None, rows, h), lambda k: (k, 0, 0))
    return pl.pallas_call(
        body, name="add_pair_w_in", grid=(N_CHIPS,), in_specs=[spec, spec], out_specs=spec,
        out_shape=jax.ShapeDtypeStruct(r.shape, BF16),
    )(p, r)


def _add_pair_small(ps, rs, c, name):
    n = len(ps)

    def body(c_ref, *refs):
        for w in range(n):
            h = _half(ps[w])
            mine = refs[w][:, :, pl.ds(pl.multiple_of(c_ref[0] * h, 128), h)]
            refs[2 * n + w][...] = (mine.astype(F32) + refs[n + w][...].astype(F32)).astype(BF16)

    return pl.pallas_call(
        body, name=name,
        in_specs=[pl.BlockSpec(memory_space=pltpu.SMEM)] + [VMEM_SPEC] * (2 * n), out_specs=[VMEM_SPEC] * n,
        out_shape=[jax.ShapeDtypeStruct(r.shape, BF16) for r in rs],
        compiler_params=pltpu.CompilerParams(vmem_limit_bytes=VMEM_LIMIT),
    )(c, *ps, *rs)


def _exchange_chips(qs):
    n = len(qs)

    def body(*refs):
        q_refs, r_refs = refs[:n], refs[n:2 * n]
        send_sems, recv_sems = refs[2 * n:]
        x, y, c, others = _place()
        me = 2 * x + y
        cps = []
        for w in range(n):
            for j, (px, py) in enumerate(others):
                cps.append(pltpu.make_async_remote_copy(
                    src_ref=q_refs[w].at[2 * px + py], dst_ref=r_refs[w].at[me], send_sem=send_sems.at[3 * w + j],
                    recv_sem=recv_sems.at[3 * w + j], device_id=(px, py, c), device_id_type=MESH))
        for cp in cps:
            cp.start()
        for w in range(n):
            for j, (px, py) in enumerate(others):
                pltpu.make_async_remote_copy(
                    src_ref=q_refs[w].at[me], dst_ref=r_refs[w].at[2 * px + py], send_sem=send_sems.at[3 * w + j],
                    recv_sem=recv_sems.at[3 * w + j], device_id=(px, py, c), device_id_type=MESH).wait_recv()
        for cp in cps:
            cp.wait_send()

    return pl.pallas_call(
        body, name="exchange_chips", in_specs=[HBM_SPEC] * n, out_specs=[HBM_SPEC] * n,
        out_shape=[jax.ShapeDtypeStruct(q.shape, BF16) for q in qs],
        scratch_shapes=[pltpu.SemaphoreType.DMA((3 * n,)), pltpu.SemaphoreType.DMA((3 * n,))],
    )(*qs)


SEM_SPEC = pl.BlockSpec(memory_space=pltpu.SEMAPHORE)
SPLIT_EFFECT = pltpu.SideEffectType.DATAFLOW_SIDE_EFFECTING


def _chips_start(qs, name):
    n = len(qs)

    def body(*refs):
        q_refs, land_refs = refs[:n], refs[n:2 * n]
        send_sems, recv_sems, token = refs[2 * n], refs[2 * n + 1], refs[-1]
        x, y, c, others = _place()
        me = 2 * x + y
        for w in range(n):
            for j, (px, py) in enumerate(others):
                pltpu.make_async_remote_copy(
                    src_ref=q_refs[w].at[2 * px + py], dst_ref=land_refs[w].at[me], send_sem=send_sems.at[3 * w + j],
                    recv_sem=recv_sems.at[3 * w + j], device_id=(px, py, c), device_id_type=MESH).start()
        token[...] = jnp.zeros_like(token)

    hbm = [pltpu.HBM(q.shape, BF16) for q in qs]
    outs = pl.pallas_call(
        body, name=name,
        out_shape=(pltpu.SemaphoreType.DMA((3 * n,)), pltpu.SemaphoreType.DMA((3 * n,)), *hbm, *hbm,
                   jax.ShapeDtypeStruct(LOSS_TILE, F32)),
        in_specs=[HBM_SPEC_STRICT] * (2 * n),
        out_specs=(SEM_SPEC, SEM_SPEC, *[HBM_SPEC_STRICT] * (2 * n), VMEM_SPEC),
        input_output_aliases={i: 2 + i for i in range(2 * n)},
        compiler_params=pltpu.CompilerParams(has_side_effects=SPLIT_EFFECT),
    )(*[pltpu.with_memory_space_constraint(q, pltpu.HBM) for q in qs],
      *[pltpu.with_memory_space_constraint(lax.empty(q.shape, BF16), pltpu.HBM) for q in qs])
    return outs[0], outs[1], outs[2:2 + n], outs[2 + n:2 + 2 * n], outs[-1]


def _chips_wait(send_sems, recv_sems, q_thru, land_thru, after, name):
    n = len(q_thru)

    def body(*refs):
        q_refs, land_refs = refs[:n], refs[n:2 * n]
        send_sems, recv_sems = refs[2 * n], refs[2 * n + 1]
        x, y, c, others = _place()
        me = 2 * x + y
        for w in range(n):
            for j, (px, py) in enumerate(others):
                cp = pltpu.make_async_remote_copy(
                    src_ref=q_refs[w].at[2 * px + py], dst_ref=land_refs[w].at[2 * px + py],
                    send_sem=send_sems.at[3 * w + j], recv_sem=recv_sems.at[3 * w + j], device_id=(px, py, c),
                    device_id_type=MESH)
                cp.wait_send()
                cp.wait_recv()

    outs = pl.pallas_call(
        body, name=name, out_shape=tuple(pltpu.HBM(a.shape, a.dtype) for a in (*q_thru, *land_thru)),
        in_specs=[HBM_SPEC_STRICT] * (2 * n) + [SEM_SPEC, SEM_SPEC, HBM_SPEC],
        out_specs=tuple([HBM_SPEC_STRICT] * (2 * n)), input_output_aliases={i: i for i in range(2 * n)},
        compiler_params=pltpu.CompilerParams(has_side_effects=SPLIT_EFFECT),
    )(*q_thru, *land_thru, send_sems, recv_sems, after)
    return list(outs[:n]), list(outs[n:])


def _sum_chips_tiled(q, r, idx, tile):
    rows, h = r.shape[1:]
    nt = h // tile

    def body(idx_ref, q_ref, r0_ref, r1_ref, r2_ref, g_ref):
        g_ref[...] = (q_ref[...].astype(F32) + r0_ref[...].astype(F32) + r1_ref[...].astype(F32)
                      + r2_ref[...].astype(F32))

    def slab(t):
        return pl.BlockSpec((None, rows, tile), lambda i, idx_ref: (idx_ref[t], 0, i))

    return pl.pallas_call(
        body, name="sum_chips_w_in",
        grid_spec=pltpu.PrefetchScalarGridSpec(
            num_scalar_prefetch=1, grid=(nt,), in_specs=[slab(0), slab(1), slab(2), slab(3)],
            out_specs=pl.BlockSpec((rows, tile), lambda i, idx_ref: (0, idx_ref[4] * nt + i))),
        out_shape=jax.ShapeDtypeStruct((rows, 2 * h), F32),
    )(idx, q, r, r, r)


def _sum_chips_small(qs, rs, idx, all_dtypes):
    n = len(rs)
    n_all = len(all_dtypes)

    def body(idx_ref, *refs):
        c = idx_ref[4]
        for w in range(n):
            q_ref, r_ref, g_ref = refs[w], refs[n + w], refs[2 * n + w]
            acc = q_ref[idx_ref[0]].astype(F32)
            for t in range(1, N_CHIPS):
                acc = acc + r_ref[idx_ref[t]].astype(F32)
            h = rs[w].shape[2]
            mine = pl.ds(pl.multiple_of(c * h, 128), h)
            g_ref[...] = jnp.zeros_like(g_ref)
            if w >= n - n_all:
                g_ref[idx_ref[0], :, mine] = acc.astype(g_ref.dtype)
            else:
                g_ref[:, mine] = acc

    shapes = [jax.ShapeDtypeStruct((r.shape[1], 2 * r.shape[2]), F32) for r in rs[:n - n_all]]
    shapes += [jax.ShapeDtypeStruct((N_CHIPS, r.shape[1], 2 * r.shape[2]), dt)
               for r, dt in zip(rs[n - n_all:], all_dtypes)]
    return pl.pallas_call(
        body, name="sum_chips_small",
        in_specs=[pl.BlockSpec(memory_space=pltpu.SMEM)] + [VMEM_SPEC] * (2 * n), out_specs=[VMEM_SPEC] * n,
        out_shape=shapes, compiler_params=pltpu.CompilerParams(vmem_limit_bytes=VMEM_LIMIT),
    )(idx, *qs, *rs)


def _share(shards, alls):
    n, na = len(shards), len(alls)
    total = n + na

    def body(*refs):
        g_refs, a_refs = refs[total:total + n], refs[total + n:2 * total]
        send_sems, recv_sems = refs[2 * total:]
        x, y, c, others = _place()
        me = 2 * x + y
        sibling = (x, y, 1 - c)

        def cols_of(w, half):
            h = shards[w].shape[1] // 2
            return g_refs[w].at[:, pl.ds(half * h, h)]

        def slab(a, chip, half):
            h = alls[a].shape[2] // 2
            return a_refs[a].at[chip, :, pl.ds(half * h, h)]

        def copy(src, dst, k, to):
            return pltpu.make_async_remote_copy(src_ref=src, dst_ref=dst, send_sem=send_sems.at[k],
                                                recv_sem=recv_sems.at[k], device_id=to, device_id_type=MESH)

        cps = [copy(cols_of(w, c), cols_of(w, c), w, sibling) for w in range(n)]
        for a in range(na):
            base = n + 7 * a
            cps.append(copy(slab(a, me, c), slab(a, me, c), base, sibling))
            for j, (px, py) in enumerate(others):
                cps.append(copy(slab(a, me, c), slab(a, me, c), base + 1 + j, (px, py, c)))
        for cp in cps:
            cp.start()
        fwd = []
        for a in range(na):
            base = n + 7 * a
            for j, (px, py) in enumerate(others):
                chip = 2 * px + py
                copy(slab(a, me, c), slab(a, chip, c), base + 1 + j, (px, py, c)).wait_recv()
                cp = copy(slab(a, chip, c), slab(a, chip, c), base + 4 + j, sibling)
                cp.start()
                fwd.append(cp)
        for a in range(na):
            base = n + 7 * a
            for j, (px, py) in enumerate(others):
                chip = 2 * px + py
                copy(slab(a, chip, c), slab(a, chip, 1 - c), base + 4 + j, sibling).wait_recv()
            copy(slab(a, me, c), slab(a, me, 1 - c), base, sibling).wait_recv()
        for w in range(n):
            copy(cols_of(w, c), cols_of(w, 1 - c), w, sibling).wait_recv()
        for cp in cps + fwd:
            cp.wait_send()

    nsem = n + 7 * na
    return pl.pallas_call(
        body, name="share", in_specs=[HBM_SPEC] * total, out_specs=[HBM_SPEC] * total,
        out_shape=[jax.ShapeDtypeStruct(a.shape, a.dtype) for a in (*shards, *alls)],
        input_output_aliases={i: i for i in range(total)},
        scratch_shapes=[pltpu.SemaphoreType.DMA((nsem,)), pltpu.SemaphoreType.DMA((nsem,))],
    )(*shards, *alls)


def _adamw(w, g, m, v):
    m2 = ADAM_B1 * m + (1.0 - ADAM_B1) * g
    v2 = ADAM_B2 * v + (1.0 - ADAM_B2) * (g * g)
    m_hat = m2 / (1.0 - ADAM_B1 ** ADAM_STEP)
    v_hat = v2 / (1.0 - ADAM_B2 ** ADAM_STEP)
    return -ADAM_LR * (m_hat / (jnp.sqrt(v_hat) + ADAM_EPS) + ADAM_WD * w), m2, v2


def _update_w_in(wt, gt, mt, vt, lat, owner, tile):
    nlat = lat.shape[0] // tile

    def body(owner_ref, w_ref, g_ref, m_ref, v_ref, lat_ref, g2_ref, d_ref, m2_ref, v2_ref):
        row = pl.program_id(0) * tile + lax.broadcasted_iota(jnp.int32, (tile, 1), 0)
        g = jnp.where((row < LAT_COLS) & (owner_ref[0] == 1), lat_ref[...].astype(F32), g_ref[...])
        g2_ref[...] = g
        d_ref[...], m2_ref[...], v2_ref[...] = _adamw(w_ref[...], g, m_ref[...], v_ref[...])

    spec = pl.BlockSpec((tile, wt.shape[1]), lambda i, o: (i, 0))
    return pl.pallas_call(
        body, name="update_w_in",
        grid_spec=pltpu.PrefetchScalarGridSpec(
            num_scalar_prefetch=1, grid=(wt.shape[0] // tile,),
            in_specs=[spec] * 4 + [pl.BlockSpec((tile, wt.shape[1]), lambda i, o: (jnp.minimum(i, nlat - 1), 0))],
            out_specs=[spec] * 4),
        out_shape=[jax.ShapeDtypeStruct(wt.shape, F32)] * 4,
        compiler_params=_params(("parallel",)),
    )(owner, wt, gt, mt, vt, lat)


def _update_small(ws, gs, ms, vs):
    n = len(ws)

    def body(*refs):
        for k in range(n):
            w_ref, g_ref, m_ref, v_ref = refs[k], refs[n + k], refs[2 * n + k], refs[3 * n + k]
            d, m2, v2 = _adamw(w_ref[...], g_ref[...], m_ref[...], v_ref[...])
            refs[4 * n + k][...] = d
            refs[5 * n + k][...] = m2
            refs[6 * n + k][...] = v2

    shapes = [jax.ShapeDtypeStruct(w.shape, F32) for w in ws]
    outs = pl.pallas_call(
        body, name="update_small", in_specs=[VMEM_SPEC] * (4 * n), out_specs=[VMEM_SPEC] * (3 * n),
        out_shape=shapes * 3,
        compiler_params=pltpu.CompilerParams(vmem_limit_bytes=VMEM_LIMIT),
    )(*ws, *gs, *ms, *vs)
    return outs[:n], outs[n:2 * n], outs[2 * n:]


SHARDED = ("w_in", "w_uq", "w_oa", "w_ob", "w_out")
REPLICATED = ("b_in", "g_q", "g_kv", "w_ukv", "sgu_ln_g", "sgu_ln_b", "w_s", "b_s", "ln_g", "ln_b")
ORDER = ("w_in", "b_in", "g_q", "w_uq", "g_kv", "w_ukv", "w_oa", "sgu_ln_g", "sgu_ln_b", "w_s", "b_s", "w_ob", "w_out",
         "ln_g", "ln_b")


def kernel(x, positions, w_in, b_in, g_q, w_uq, g_kv, w_ukv, w_oa, sgu_ln_g, sgu_ln_b, w_s, b_s, w_ob, w_out, ln_g, ln_b, loss_target, m_w_in, m_b_in, m_g_q, m_w_uq, m_g_kv, m_w_ukv, m_w_oa, m_sgu_ln_g, m_sgu_ln_b, m_w_s, m_b_s, m_w_ob, m_w_out, m_ln_g, m_ln_b, v_w_in, v_b_in, v_g_q, v_w_uq, v_g_kv, v_w_ukv, v_w_oa, v_sgu_ln_g, v_sgu_ln_b, v_w_s, v_b_s, v_w_ob, v_w_out, v_ln_g, v_ln_b):
    w = dict(w_in=w_in, b_in=b_in, g_q=g_q, w_uq=w_uq, g_kv=g_kv, w_ukv=w_ukv, w_oa=w_oa, sgu_ln_g=sgu_ln_g,
             sgu_ln_b=sgu_ln_b, w_s=w_s, b_s=b_s, w_ob=w_ob, w_out=w_out, ln_g=ln_g, ln_b=ln_b)
    m = dict(w_in=m_w_in, b_in=m_b_in, g_q=m_g_q, w_uq=m_w_uq, g_kv=m_g_kv, w_ukv=m_w_ukv, w_oa=m_w_oa,
             sgu_ln_g=m_sgu_ln_g, sgu_ln_b=m_sgu_ln_b, w_s=m_w_s, b_s=m_b_s, w_ob=m_w_ob, w_out=m_w_out, ln_g=m_ln_g,
             ln_b=m_ln_b)
    v = dict(w_in=v_w_in, b_in=v_b_in, g_q=v_g_q, w_uq=v_w_uq, g_kv=v_g_kv, w_ukv=v_w_ukv, w_oa=v_w_oa,
             sgu_ln_g=v_sgu_ln_g, sgu_ln_b=v_sgu_ln_b, w_s=v_w_s, b_s=v_b_s, w_ob=v_w_ob, w_out=v_w_out, ln_g=v_ln_g,
             ln_b=v_ln_b)
    w, m, v = ({n: a[0] for n, a in d.items()} for d in (w, m, v))
    c = lax.axis_index("c")

    wt_shard, mt_shard, vt_shard = (jnp.transpose(d["w_in"]) for d in (w, m, v))
    xi, yi = lax.axis_index("x"), lax.axis_index("y")
    me1 = (2 * xi + yi).reshape(1).astype(jnp.int32)
    g_lat, g_uq = _gather_first(wt_shard[:LAT_COLS], w["w_uq"].reshape(Q_RANK // 4, HEADS * QK_DIM))
    bufs = _cast_own([wt_shard, w["w_oa"], w["w_ob"], w["w_out"]], me1)
    send0, recv0, bufs, token0 = _gather_start(bufs, g_lat)
    st = _local_attention(x[0], positions[0], g_lat, w["b_in"], w["g_q"], g_uq.reshape(Q_RANK, HEADS, QK_DIM),
                          w["g_kv"], w["w_ukv"], token0)
    g_in, g_oa, g_ob, g_out = _gather_finish(_gather_wait(send0, recv0, bufs, st["o"]))
    wt = g_in.reshape(IN_W, D_MODEL)

    loss, early, st = _local_head(
        st, x[0], loss_target[0], wt, g_oa, w["sgu_ln_g"], w["sgu_ln_b"], w["w_s"], w["b_s"], g_ob,
        g_out.reshape(D_MODEL, D_MODEL), w["ln_g"], w["ln_b"])

    c1 = c.reshape(1).astype(jnp.int32)
    idx = jnp.stack([2 * xi + yi, 2 * (1 - xi) + yi, 2 * xi + (1 - yi), 2 * (1 - xi) + (1 - yi), c]).astype(jnp.int32)
    slabs = lambda a: a.reshape(N_CHIPS, IN_W // N_CHIPS, D_MODEL // 2)
    theirs = slabs(_dwt_early(st["dhmt"], st["dhgt"], st["xb"], 1 - c1, c1, "dwt_theirs"))
    parts1 = [theirs, early["w_oa"].astype(BF16), early["w_ob"].astype(BF16),
              early["w_out"].reshape(N_CHIPS, SLAB_W, D_MODEL).astype(BF16)]
    my_loss = lax.dynamic_update_slice(jnp.zeros((N_DEV,) + LOSS_TILE, F32), jnp.broadcast_to(loss, (1,) + LOSS_TILE),
                                       (4 * xi + 2 * yi + c, 0, 0))
    sems0 = _pairs_start(parts1, my_loss, 1)
    mine = slabs(_dwt_early(st["dhmt"], st["dhgt"], st["xb"], c1, sems0[5], "dwt_mine"))
    parts1, recv1, all_loss = _pairs_wait(*sems0[:5], mine, 1)
    pairs1 = [_add_pair_tiled(mine, recv1[0]), *_add_pair_small(parts1[1:], recv1[1:], c1, "add_pair_early")]
    sems1 = _chips_start(pairs1, "chips_start_early")
    st["delta"] = st["delta"] + sems1[4][0, 0]
    dq, dk, dv = _local_attn_bwd(st)
    dhl, late = _local_tail(st, dq, dk, dv, dv)

    grads = {**early, **late}
    rep = jnp.concatenate([_rows8(grads[n]) for n in REPLICATED], axis=0)
    rep = jnp.pad(rep, ((0, N_CHIPS * REP_ROWS - rep.shape[0]), (0, 0))).reshape(N_CHIPS, REP_ROWS, D_MODEL)
    parts2 = [late["w_uq"].reshape(N_CHIPS, Q_RANK // N_CHIPS, HEADS * QK_DIM).astype(BF16), rep.astype(BF16),
              late["w_lat"].reshape(N_CHIPS, LAT_ROWS_PAD // N_CHIPS, D_MODEL)]
    pairs2 = _add_pair_small(parts2, _exchange_pairs(parts2, "exchange_pairs_late"), c1, "add_pair_late")
    sems2 = _chips_start(pairs2, "chips_start_late")
    dx = _dx(st["dr"], st["dhg"], st["dhm"], dhl, st["wt"], sems2[4])
    pairs2, landed2 = _chips_wait(*sems2[:4], dx, "chips_wait_late")
    pairs1, landed1 = _chips_wait(*sems1[:4], landed2[0], "chips_wait_early")
    sums = [_sum_chips_tiled(pairs1[0], landed1[0], idx, 128),
            *_sum_chips_small([*pairs1[1:], *pairs2], [*landed1[1:], *landed2], idx, (F32, BF16))]
    *shards, g_rep, g_lat = _share(sums[:-2], sums[-2:])
    loss = jnp.sum(all_loss[:, 0, 0])

    red = {n: s.reshape(w[n].shape) for n, s in zip(("w_oa", "w_ob", "w_out", "w_uq"), shards[1:])}
    g_rep = g_rep.reshape(N_CHIPS * REP_ROWS, D_MODEL)
    off = 0
    for n in REPLICATED:
        rows = _rows8(w[n]).shape[0]
        red[n] = g_rep[off:off + rows].reshape(-1)[:w[n].size].reshape(w[n].shape)
        off += rows
    owner = (2 * xi + yi == 0).astype(jnp.int32).reshape(1)
    gt, dt, mt, vt2 = _update_w_in(wt_shard, shards[0], mt_shard, vt_shard,
                                   g_lat.reshape(LAT_ROWS_PAD, D_MODEL).astype(F32), owner, 232)
    red["w_in"] = jnp.transpose(gt)
    small = [n for n in ORDER if n != "w_in"]
    as2d = lambda a: a.reshape(-1, a.shape[-1])
    ds, ms, vs = _update_small([as2d(w[n]) for n in small], [as2d(red[n]) for n in small],
                               [as2d(m[n]) for n in small], [as2d(v[n]) for n in small])
    delta, new_m, new_v = {"w_in": jnp.transpose(dt)}, {"w_in": jnp.transpose(mt)}, {"w_in": jnp.transpose(vt2)}
    for i, n in enumerate(small):
        delta[n], new_m[n], new_v[n] = (a[i].reshape(w[n].shape) for a in (ds, ms, vs))

    lead = lambda a: a[None]
    return (loss, dx[None], *[lead(red[n]) for n in ORDER], *[lead(delta[n]) for n in ORDER],
            *[lead(new_m[n]) for n in ORDER], *[lead(new_v[n]) for n in ORDER])
```

```python
import functools
import math

import jax
import jax.numpy as jnp
from jax import lax
from jax.experimental import pallas as pl
from jax.experimental.pallas import tpu as pltpu

F32 = jnp.float32
BF16 = jnp.bfloat16

D_MODEL = 1024
HEADS = 8
Q_RANK = 384
KV_RANK = 128
NOPE = 64
ROPE = 32
V_DIM = 64
QK_DIM = NOPE + ROPE
HEAD_PAD = 128
MLA_W = HEADS * V_DIM
SGU_W = 512
GROUPS = 8
CHUNK = 128
IN_W = 4640
RMS_EPS = 1e-6
LN_EPS = 1e-5
ALPHA = 2.0 ** 0.25
ROPE_THETA = 10000.0
SCALE = QK_DIM ** -0.5

GATE_W = 2 * D_MODEL
MID_W = 4 * SGU_W
LAT_W = Q_RANK + KV_RANK + HEAD_PAD
PAD_W = GATE_W + MID_W + LAT_W
LAT_COLS = Q_RANK + KV_RANK + ROPE
ROW_GATE = LAT_COLS + MID_W
LAT_ROWS_PAD = 704
N_SLABS = 4
SLAB_W = D_MODEL // N_SLABS

ROW_TILE = 256
MATMUL_ROW_TILE = 512
ATT_TQ = 2048
ATT_TK = 256
ATT_BWD_TQ = 512
ATT_BWD_TK = 256
LOG2E = 1.4426950408889634
LN2 = 0.6931471805599453
Q_SCALE = SCALE * LOG2E
VMEM_LIMIT = 56 * 1024 * 1024

ADAM_LR = 0.001
ADAM_B1 = 0.9
ADAM_B2 = 0.999
ADAM_EPS = 1e-08
ADAM_WD = 0.01
ADAM_STEP = 10


def _dot(a, b):
    return jnp.dot(a, b, preferred_element_type=F32)


def _dot_nt(a, b):
    return lax.dot_general(a, b, (((1,), (1,)), ((), ())), preferred_element_type=F32)


def _dot_tn(a, b):
    return lax.dot_general(a, b, (((0,), (0,)), ((), ())), preferred_element_type=F32)


def _sigmoid(z):
    return 0.5 * jnp.tanh(0.5 * z) + 0.5


_GELU_C = math.sqrt(2.0 / math.pi)


def _gelu_and_grad(x):
    x2 = x * x
    t = jnp.tanh(_GELU_C * (x + 0.044715 * x * x2))
    g = 0.5 * x * (1.0 + t)
    dg = 0.5 * (1.0 + t) + 0.5 * x * (1.0 - t * t) * (_GELU_C * (1.0 + 3.0 * 0.044715 * x2))
    return g, dg


def _silu_and_grad(z):
    s = _sigmoid(z)
    return z * s, s * (1.0 + z * (1.0 - s))


def _rope(xb, c, sl, sh):
    return xb * c + pltpu.roll(xb, 112, 1) * sl + pltpu.roll(xb, 16, 1) * sh


def _rope_t(dy, c, sl, sh):
    return dy * c + pltpu.roll(dy * sl, 16, 1) + pltpu.roll(dy * sh, 112, 1)


def _params(sem=("arbitrary",)):
    return pltpu.CompilerParams(dimension_semantics=sem, vmem_limit_bytes=VMEM_LIMIT)


def _row_spec(tile, width):
    return pl.BlockSpec((tile, width), lambda i: (i, 0))


def _full_spec(shape):
    nd = len(shape)
    return pl.BlockSpec(shape, lambda i: (0,) * nd)


def _kpe_rows(wt_ref):
    z = lambda n: jnp.zeros((n, D_MODEL), BF16)
    return jnp.concatenate([z(NOPE), wt_ref[Q_RANK + KV_RANK:LAT_COLS, :], z(HEAD_PAD - QK_DIM)], axis=0)


def _fwd_rest(xb, wt, b_g, b_m):
    s = xb.shape[0]
    ts = MATMUL_ROW_TILE

    def body(xb_ref, wt_ref, bg_ref, bm_ref, hg_ref, hm_ref):
        xb_ = xb_ref[...]
        hg_ref[...] = _dot_nt(xb_, wt_ref[ROW_GATE:IN_W, :]) + bg_ref[...]
        hm_ref[...] = _dot_nt(xb_, wt_ref[LAT_COLS:ROW_GATE, :]) + bm_ref[...]

    return pl.pallas_call(
        body, name="fwd_rest", grid=(s // ts,),
        in_specs=[_row_spec(ts, D_MODEL), _full_spec(wt.shape), _full_spec(b_g.shape), _full_spec(b_m.shape)],
        out_specs=[_row_spec(ts, GATE_W), _row_spec(ts, MID_W)],
        out_shape=[jax.ShapeDtypeStruct((s, GATE_W), F32), jax.ShapeDtypeStruct((s, MID_W), F32)],
        compiler_params=_params(),
    )(xb, wt, b_g, b_m)


def _fwd_lat(x, wlat, b_l, g_q, wuq, g_kv, wk, wv, rc, rsl, rsh, after):
    s = x.shape[0]
    ts = ROW_TILE

    def body(x_ref, wt_ref, bl_ref, gq_ref, wuq_ref, gkv_ref, wk_ref, wv_ref, rc_ref, rsl_ref,
             rsh_ref, after_ref, hl_ref, q_ref, k_ref, v_ref, xb_ref, qt_ref, kt_ref, vt_ref):
        xb = x_ref[...].astype(BF16)
        xb_ref[...] = xb
        hl = jnp.concatenate([_dot_nt(xb, wt_ref[0:Q_RANK + KV_RANK, :]), _dot_nt(xb, _kpe_rows(wt_ref))],
                             axis=1) + bl_ref[...]
        hl_ref[...] = hl
        c, sl, sh = rc_ref[...], rsl_ref[...], rsh_ref[...]
        cq = hl[:, :Q_RANK]
        cqn = cq * lax.rsqrt(jnp.mean(cq * cq, axis=-1, keepdims=True) + RMS_EPS) * gq_ref[...]
        q = _dot(cqn.astype(BF16), wuq_ref[...])
        ckv = hl[:, Q_RANK:Q_RANK + KV_RANK]
        ckvn = (ckv * lax.rsqrt(jnp.mean(ckv * ckv, axis=-1, keepdims=True) + RMS_EPS) * gkv_ref[...]).astype(BF16)
        k = _dot(ckvn, wk_ref[...])
        vb = _dot(ckvn, wv_ref[...]).astype(BF16)
        v_ref[...] = vb
        vt_ref[...] = vb.T
        kpe = _rope(hl[:, Q_RANK + KV_RANK:], c, sl, sh)
        for hd in range(HEADS):
            lanes = slice(hd * HEAD_PAD, (hd + 1) * HEAD_PAD)
            qb = (_rope(q[:, lanes], c, sl, sh) * Q_SCALE).astype(BF16)
            kb = (k[:, lanes] + kpe).astype(BF16)
            q_ref[:, lanes] = qb
            k_ref[:, lanes] = kb
            qt_ref[lanes, :] = qb.T
            kt_ref[lanes, :] = kb.T

    qk_w = HEADS * HEAD_PAD
    col_spec = lambda rows: pl.BlockSpec((rows, ts), lambda i: (0, i))
    return pl.pallas_call(
        body, name="fwd_lat", grid=(s // ts,),
        in_specs=[_row_spec(ts, D_MODEL), _full_spec(wlat.shape),
                  _full_spec(b_l.shape), _full_spec(g_q.shape),
                  _full_spec(wuq.shape), _full_spec(g_kv.shape), _full_spec(wk.shape), _full_spec(wv.shape),
                  _row_spec(ts, HEAD_PAD), _row_spec(ts, HEAD_PAD), _row_spec(ts, HEAD_PAD),
                  pl.BlockSpec(memory_space=pl.ANY)],
        out_specs=[_row_spec(ts, LAT_W), _row_spec(ts, qk_w),
                   _row_spec(ts, qk_w), _row_spec(ts, MLA_W), _row_spec(ts, D_MODEL), col_spec(qk_w), col_spec(qk_w),
                   col_spec(MLA_W)],
        out_shape=[jax.ShapeDtypeStruct((s, LAT_W), F32), jax.ShapeDtypeStruct((s, qk_w), BF16),
                   jax.ShapeDtypeStruct((s, qk_w), BF16), jax.ShapeDtypeStruct((s, MLA_W), BF16),
                   jax.ShapeDtypeStruct((s, D_MODEL), BF16), jax.ShapeDtypeStruct((qk_w, s), BF16),
                   jax.ShapeDtypeStruct((qk_w, s), BF16), jax.ShapeDtypeStruct((MLA_W, s), BF16)],
        compiler_params=_params(),
    )(x, wlat, b_l, g_q, wuq, g_kv, wk, wv, rc, rsl, rsh, after)


def _attn_fwd(qt, k, vt):
    s = k.shape[0]
    tq, tk = ATT_TQ, ATT_TK
    r = tq // tk
    pairs = HEADS // 2

    def body(qt_ref, k_ref, vt_ref, o_ref, lse_ref):
        i = pl.program_id(1)
        qts = [qt_ref[hh * HEAD_PAD:(hh + 1) * HEAD_PAD, :] for hh in range(2)]

        def scores(j, lo):
            koff = pl.multiple_of(j * tk, tk)
            return tuple(_dot(k_ref[pl.ds(koff, tk), hh * HEAD_PAD:(hh + 1) * HEAD_PAD], qts[hh][:, lo:])
                         for hh in range(2))

        def weighted(j, ps):
            koff = pl.multiple_of(j * tk, tk)
            return tuple(_dot(vt_ref[hh * V_DIM:(hh + 1) * V_DIM, pl.ds(koff, tk)], ps[hh]) for hh in range(2))

        def from_lane(full, lo, part):
            return part if lo == 0 else jnp.concatenate([full[:, :lo], part], axis=1)

        def step(j, carry, diag, last):
            st, ps, stats = carry
            lo = 0 if diag is None else diag * tk
            lo_prev = 0 if not diag else (diag - 1) * tk
            st_next = None if last else scores(j + 1, 0 if diag is None else lo + tk)
            pvs = weighted(jnp.maximum(j - 1, 0), ps)
            new_ps, new_stats = [], []
            for hh in range(2):
                m, l, acc = stats[hh]
                s_ = st[hh]
                if diag is not None:
                    krow = lax.broadcasted_iota(jnp.int32, s_.shape, 0)
                    qcol = lax.broadcasted_iota(jnp.int32, s_.shape, 1)
                    s_ = jnp.where(krow <= qcol, s_, -jnp.inf)
                acc = from_lane(acc, lo_prev, acc[:, lo_prev:] + pvs[hh])
                m_old = m[:, lo:]
                m_new = jnp.maximum(m_old, jnp.max(s_, axis=0, keepdims=True))
                a = jnp.exp2(m_old - m_new)
                p = jnp.exp2(s_ - m_new)
                new_stats.append((from_lane(m, lo, m_new),
                                  from_lane(l, lo, a * l[:, lo:] + jnp.sum(p, axis=0, keepdims=True)),
                                  from_lane(acc, lo, a * acc[:, lo:])))
                new_ps.append(p.astype(BF16))
            return st_next, tuple(new_ps), tuple(new_stats)

        one = (jnp.full((1, tq), -jnp.inf, F32), jnp.zeros((1, tq), F32), jnp.zeros((V_DIM, tq), F32))
        zero_p = jnp.zeros((tk, tq), BF16)
        nfull = i * r
        carry = lax.fori_loop(0, nfull, functools.partial(step, diag=None, last=False),
                              (scores(0, 0), (zero_p, zero_p), (one, one)))
        for d in range(r):
            carry = step(nfull + d, carry, d, d == r - 1)
        _, ps, stats = carry
        pvs = weighted(nfull + r - 1, ps)
        lo = (r - 1) * tk
        ot = jnp.concatenate([from_lane(stats[hh][2], lo, stats[hh][2][:, lo:] + pvs[hh]) / stats[hh][1]
                              for hh in range(2)], axis=0)
        o_ref[...] = ot.T
        lse = [stats[hh][0] + jnp.log(stats[hh][1]) * LOG2E for hh in range(2)]
        lse_ref[...] = jnp.concatenate(lse + [jnp.zeros((6, tq), F32)], axis=0)

    return pl.pallas_call(
        body, name="attn_fwd", grid=(pairs, s // tq),
        in_specs=[pl.BlockSpec((2 * HEAD_PAD, tq), lambda p, i: (p, i)),
                  pl.BlockSpec((s, 2 * HEAD_PAD), lambda p, i: (0, p)),
                  pl.BlockSpec((2 * V_DIM, s), lambda p, i: (p, 0))],
        out_specs=[pl.BlockSpec((tq, 2 * V_DIM), lambda p, i: (i, p)),
                   pl.BlockSpec((None, 8, tq), lambda p, i: (p, 0, i))],
        out_shape=[jax.ShapeDtypeStruct((s, MLA_W), F32), jax.ShapeDtypeStruct((pairs, 8, s), F32)],
        compiler_params=_params(("arbitrary", "arbitrary")),
    )(qt, k, vt)


def _attn_bwd(q, qt, k, kt, v, do, dot, lse, delta):
    s = k.shape[0]
    tk = ATT_BWD_TK
    nk = s // tk
    pairs = HEADS // 2

    def body(q_ref, qt_ref, k_ref, kt_ref, v_ref, do_ref, dot_ref, lse_ref, dl_ref, dqt_ref, dk_ref, dv_ref):
        krow = lax.broadcasted_iota(jnp.int32, (tk, tk), 0)
        qcol = lax.broadcasted_iota(jnp.int32, (tk, tk), 1)
        lane = lax.broadcasted_iota(jnp.int32, (tk, 2 * V_DIM), 1)
        drow = lax.broadcasted_iota(jnp.int32, (2 * V_DIM, s), 0)
        dotb = dot_ref[...]
        dots = [jnp.where((drow < V_DIM) if hh == 0 else (drow >= V_DIM), dotb, jnp.zeros_like(dotb))
                for hh in range(2)]
        for j in range(nk):
            lo = j * tk
            vb = v_ref[lo:lo + tk, :]
            dob = do_ref[lo:, :]
            dvs = []
            for hh in range(2):
                rows = slice(hh * HEAD_PAD, (hh + 1) * HEAD_PAD)
                st = _dot(k_ref[lo:lo + tk, rows], qt_ref[rows, lo:])
                diag = jnp.where(krow <= qcol, st[:, :tk], -jnp.inf)
                st = diag if j == nk - 1 else jnp.concatenate([diag, st[:, tk:]], axis=1)
                p = jnp.exp2(st - lse_ref[hh:hh + 1, lo:])
                dpt = _dot(vb, dots[hh][:, lo:])
                dst = (p * (dpt - dl_ref[hh:hh + 1, lo:])).astype(BF16)
                dvs.append(_dot(p.astype(BF16), dob))
                dk_ref[lo:lo + tk, rows] = _dot(dst, q_ref[lo:, rows]) * LN2
                dqt = _dot(kt_ref[rows, lo:lo + tk], dst)
                if j == 0:
                    dqt_ref[rows, :] = dqt
                else:
                    dqt_ref[rows, lo:] += dqt
            dv_ref[lo:lo + tk, :] = jnp.where(lane < V_DIM, dvs[0], dvs[1])
        dqt_ref[...] = dqt_ref[...] * SCALE

    pair_rows = lambda w: pl.BlockSpec((s, w), lambda p: (0, p))
    pair_cols = lambda w: pl.BlockSpec((w, s), lambda p: (p, 0))
    stats = pl.BlockSpec((None, 8, s), lambda p: (p, 0, 0))
    return pl.pallas_call(
        body, name="attn_bwd", grid=(pairs,),
        in_specs=[pair_rows(2 * HEAD_PAD), pair_cols(2 * HEAD_PAD), pair_rows(2 * HEAD_PAD), pair_cols(2 * HEAD_PAD),
                  pair_rows(2 * V_DIM), pair_rows(2 * V_DIM), pair_cols(2 * V_DIM), stats, stats],
        out_specs=[pair_cols(2 * HEAD_PAD), pair_rows(2 * HEAD_PAD), pair_rows(2 * V_DIM)],
        out_shape=[jax.ShapeDtypeStruct((HEADS * HEAD_PAD, s), F32), jax.ShapeDtypeStruct((s, HEADS * HEAD_PAD), F32),
                   jax.ShapeDtypeStruct((s, MLA_W), F32)],
        compiler_params=_params(("arbitrary",)),
    )(q, qt, k, kt, v, do, dot, lse, delta)


def _attn_bwd_dynamic(q, qt, k, kt, v, do, dot, lse, delta):
    s = k.shape[0]
    tq, tk = ATT_BWD_TQ, ATT_BWD_TK
    r = tq // tk
    nq = s // tq
    nk = s // tk
    pairs = HEADS // 2

    def body(q_ref, qt_ref, k_ref, kt_ref, v_ref, do_ref, dot_ref, lse_ref, dl_ref, dqt_ref, dk_ref, dv_ref):
        j = pl.program_id(1)
        krow = lax.broadcasted_iota(jnp.int32, (tk, tq), 0)
        qcol = lax.broadcasted_iota(jnp.int32, (tk, tq), 1)
        lane = lax.broadcasted_iota(jnp.int32, (tk, 2 * V_DIM), 1)
        drow = lax.broadcasted_iota(jnp.int32, (2 * V_DIM, tq), 0)

        @pl.when(j == 0)
        def _():
            dqt_ref[...] = jnp.zeros_like(dqt_ref)

        koff = pl.multiple_of(j * tk, tk)
        vb = v_ref[pl.ds(koff, tk), :]
        kbs = [k_ref[pl.ds(koff, tk), hh * HEAD_PAD:(hh + 1) * HEAD_PAD] for hh in range(2)]
        ktbs = [kt_ref[hh * HEAD_PAD:(hh + 1) * HEAD_PAD, pl.ds(koff, tk)] for hh in range(2)]
        i0 = j // r

        def front(i):
            qoff = pl.multiple_of(i * tq, tq)
            dotb = dot_ref[:, pl.ds(qoff, tq)]
            out = []
            for hh in range(2):
                mine = (drow < V_DIM) if hh == 0 else (drow >= V_DIM)
                st = _dot(kbs[hh], qt_ref[hh * HEAD_PAD:(hh + 1) * HEAD_PAD, pl.ds(qoff, tq)])
                out.append((st, _dot(vb, jnp.where(mine, dotb, jnp.zeros_like(dotb)))))
            return tuple(out)

        def middle(i, tiles, diag):
            qoff = pl.multiple_of(i * tq, tq)
            out = []
            for hh in range(2):
                st, dpt = tiles[hh]
                if diag:
                    st = jnp.where(krow + (j - i0 * r) * tk <= qcol, st, -jnp.inf)
                p = jnp.exp2(st - lse_ref[hh:hh + 1, pl.ds(qoff, tq)])
                out.append((p.astype(BF16), (p * (dpt - dl_ref[hh:hh + 1, pl.ds(qoff, tq)])).astype(BF16)))
            return tuple(out)

        def back(i, pd, accs):
            qoff = pl.multiple_of(i * tq, tq)
            dob = do_ref[pl.ds(qoff, tq), :]
            out = []
            for hh in range(2):
                rows = slice(hh * HEAD_PAD, (hh + 1) * HEAD_PAD)
                p, dst = pd[hh]
                dk_acc, dv_acc = accs[hh]
                dv_acc = dv_acc + _dot(p, dob)
                dk_acc = dk_acc + _dot(dst, q_ref[pl.ds(qoff, tq), rows])
                dqt_ref[rows, pl.ds(qoff, tq)] += _dot(ktbs[hh], dst)
                out.append((dk_acc, dv_acc))
            return tuple(out)

        def step(i, accs, diag):
            return back(i, middle(i, front(i), diag), accs)

        zero_acc = (jnp.zeros((tk, HEAD_PAD), F32), jnp.zeros((tk, 2 * V_DIM), F32))
        accs = step(i0, (zero_acc, zero_acc), True)
        accs = lax.fori_loop(i0 + 1, nq, functools.partial(step, diag=False), accs)
        for hh in range(2):
            dk_ref[:, hh * HEAD_PAD:(hh + 1) * HEAD_PAD] = accs[hh][0] * LN2
        dv_ref[...] = jnp.where(lane < V_DIM, accs[0][1], accs[1][1])

        @pl.when(j == nk - 1)
        def _():
            dqt_ref[...] = dqt_ref[...] * SCALE

    pair_rows = lambda w: pl.BlockSpec((s, w), lambda p, j: (0, p))
    pair_cols = lambda w: pl.BlockSpec((w, s), lambda p, j: (p, 0))
    stats = pl.BlockSpec((None, 8, s), lambda p, j: (p, 0, 0))
    return pl.pallas_call(
        body, name="attn_bwd", grid=(pairs, nk),
        in_specs=[pair_rows(2 * HEAD_PAD), pair_cols(2 * HEAD_PAD), pair_rows(2 * HEAD_PAD), pair_cols(2 * HEAD_PAD),
                  pair_rows(2 * V_DIM), pair_rows(2 * V_DIM), pair_cols(2 * V_DIM), stats, stats],
        out_specs=[pair_cols(2 * HEAD_PAD),
                   pl.BlockSpec((tk, 2 * HEAD_PAD), lambda p, j: (j, p)),
                   pl.BlockSpec((tk, 2 * V_DIM), lambda p, j: (j, p))],
        out_shape=[jax.ShapeDtypeStruct((HEADS * HEAD_PAD, s), F32), jax.ShapeDtypeStruct((s, HEADS * HEAD_PAD), F32),
                   jax.ShapeDtypeStruct((s, MLA_W), F32)],
        compiler_params=_params(("arbitrary", "arbitrary")),
    )(q, qt, k, kt, v, do, dot, lse, delta)


def _split3(a):
    hi = a.astype(BF16)
    r1 = a - hi.astype(F32)
    mid = r1.astype(BF16)
    lo = (r1 - mid.astype(F32)).astype(BF16)
    return hi, mid, lo


def _mid(x, tgt, o, hm, hg, woa, wob, wout, ln_g, ln_b, sg_g, sg_b, w_s, bsb):
    s = x.shape[0]
    ts = ROW_TILE
    nsteps = s // ts
    nch = ts // CHUNK
    npair = GROUPS // 2

    def body(x_ref, t_ref, o_ref, hm_ref, hg_ref, woa_ref, wob_ref, wout_ref, lng_ref, lnb_ref, sgg_ref, sgb_ref,
             ws_ref, bsb_ref,
             dr_ref, dhg_ref, dhm_ref, do_ref, dot_ref, dl_ref, dhgt_ref, dhmt_ref,
             dwout_ref, dwoa_ref, dwob_ref, dws_ref, dbs_ref, dlng_ref, dlnb_ref, dsgg_ref, dsgb_ref, loss_ref,
             dbg_ref, dbm_ref, dbacc_ref):
        i = pl.program_id(0)

        @pl.when(i == 0)
        def _():
            for r in (dwout_ref, dwoa_ref, dwob_ref, dws_ref, dlng_ref, dlnb_ref, dsgg_ref, dsgb_ref, loss_ref,
                      dbg_ref, dbm_ref, dbacc_ref):
                r[...] = jnp.zeros_like(r)

        def emit(ref, tref, bref, lo, val):
            vb = val.astype(BF16)
            n = val.shape[1]
            ref[:, lo:lo + n] = vb
            tref[lo:lo + n, :] = vb.T
            bref[:, lo:lo + n] += jnp.sum(val, axis=0, keepdims=True)

        lane = lax.broadcasted_iota(jnp.int32, (CHUNK, CHUNK), 1)
        left = lane < V_DIM
        tril = lax.broadcasted_iota(jnp.int32, (CHUNK, CHUNK), 0) >= lane
        ms = [jnp.where(tril, ws_ref[g], 0.0).astype(BF16) for g in range(GROUPS)]

        z_a = hm_ref[:, 0:SGU_W]
        u = hm_ref[:, SGU_W:2 * SGU_W]
        v = hm_ref[:, 2 * SGU_W:3 * SGU_W]
        z_b = hm_ref[:, 3 * SGU_W:4 * SGU_W]
        o = o_ref[...]
        sa, dsa = _silu_and_grad(z_a)
        y_a = (o * sa).astype(BF16)
        gu, dgu = _gelu_and_grad(u)
        gv, dgv = _gelu_and_grad(v)
        mu = jnp.mean(gv, axis=-1, keepdims=True)
        vc = gv - mu
        rstd_v = lax.rsqrt(jnp.mean(vc * vc, axis=-1, keepdims=True) + LN_EPS)
        vhat = vc * rstd_v
        vn = (vhat * sgg_ref[...] + sgb_ref[...]).astype(BF16)
        rows = []
        for c in range(nch):
            blocks = []
            for p in range(npair):
                blk = vn[c * CHUNK:(c + 1) * CHUNK, p * CHUNK:(p + 1) * CHUNK]
                blocks.append(jnp.where(left, _dot(ms[2 * p], blk), _dot(ms[2 * p + 1], blk)))
            rows.append(jnp.concatenate(blocks, axis=1) + bsb_ref[...])
        mixed = jnp.concatenate(rows, axis=0)
        sgu = gu * mixed
        sb, dsb = _silu_and_grad(z_b)
        y_b = (sgu * sb).astype(BF16)
        pa = jnp.concatenate([_dot(y_a, woa_ref[k]) for k in range(N_SLABS)], axis=1)
        pb = jnp.concatenate([_dot(y_b, wob_ref[k]) for k in range(N_SLABS)], axis=1)
        sga = _sigmoid(hg_ref[:, :D_MODEL])
        sgb = _sigmoid(hg_ref[:, D_MODEL:])
        m2 = (sga * pa + sgb * pb).astype(BF16)
        r = ALPHA * x_ref[...] + _dot(m2, wout_ref[...])
        rmu = jnp.mean(r, axis=-1, keepdims=True)
        rc = r - rmu
        rstd = lax.rsqrt(jnp.mean(rc * rc, axis=-1, keepdims=True) + LN_EPS)
        xhat = rc * rstd
        y = xhat * lng_ref[...] + lnb_ref[...]
        err = y - t_ref[...]
        loss_ref[...] += jnp.full(loss_ref.shape, 0.5 / D_MODEL, F32) * jnp.sum(err * err)

        dy = err * (1.0 / D_MODEL)
        dlng_ref[...] += jnp.sum(dy * xhat, axis=0, keepdims=True)
        dlnb_ref[...] += jnp.sum(dy, axis=0, keepdims=True)
        dxh = dy * lng_ref[...]
        dr = rstd * (dxh - jnp.mean(dxh, axis=-1, keepdims=True) - xhat * jnp.mean(dxh * xhat, axis=-1, keepdims=True))
        dr_ref[...] = dr
        drb = dr.astype(BF16)
        dwout_ref[...] += _dot_tn(m2, drb)
        dm2 = _dot_nt(drb, wout_ref[...])
        emit(dhg_ref, dhgt_ref, dbg_ref, 0, dm2 * pa * sga * (1.0 - sga))
        emit(dhg_ref, dhgt_ref, dbg_ref, D_MODEL, dm2 * pb * sgb * (1.0 - sgb))
        dpa = (dm2 * sga).astype(BF16)
        dpb = (dm2 * sgb).astype(BF16)
        dy_a = jnp.zeros((ts, MLA_W), F32)
        dy_b = jnp.zeros((ts, SGU_W), F32)
        y_at, y_bt = y_a.T, y_b.T
        for k in range(N_SLABS):
            cols = slice(k * SLAB_W, (k + 1) * SLAB_W)
            dwoa_ref[k] += _dot(y_at, dpa[:, cols])
            dwob_ref[k] += _dot(y_bt, dpb[:, cols])
            dy_a = dy_a + _dot_nt(dpa[:, cols], woa_ref[k])
            dy_b = dy_b + _dot_nt(dpb[:, cols], wob_ref[k])
        dob = (dy_a * sa).astype(BF16)
        do_ref[...] = dob
        dot_ref[...] = dob.T
        head = (lax.broadcasted_iota(jnp.int32, (HEADS, MLA_W), 1) // V_DIM
                == lax.broadcasted_iota(jnp.int32, (HEADS, MLA_W), 0)).astype(BF16)
        dl_ref[...] = sum(_dot_nt(head, term) for term in _split3(dob.astype(F32) * o))
        emit(dhm_ref, dhmt_ref, dbm_ref, 0, dy_a * o * dsa)
        dsg = dy_b * sb
        emit(dhm_ref, dhmt_ref, dbm_ref, 3 * SGU_W, dy_b * sgu * dsb)
        emit(dhm_ref, dhmt_ref, dbm_ref, SGU_W, dsg * mixed * dgu)
        dmixed = dsg * gu
        dvn_rows = []
        dbs_sum = jnp.zeros((CHUNK, SGU_W), F32)
        for c in range(nch):
            dm_c = dmixed[c * CHUNK:(c + 1) * CHUNK, :]
            dbs_sum = dbs_sum + dm_c
            blocks = []
            for p in range(npair):
                dmb = dm_c[:, p * CHUNK:(p + 1) * CHUNK].astype(BF16)
                blk = vn[c * CHUNK:(c + 1) * CHUNK, p * CHUNK:(p + 1) * CHUNK]
                blocks.append(jnp.where(left, _dot_tn(ms[2 * p], dmb), _dot_tn(ms[2 * p + 1], dmb)))
                zero = jnp.zeros_like(dmb)
                dws_ref[2 * p] += jnp.where(tril, _dot_nt(jnp.where(left, dmb, zero), blk), 0.0)
                dws_ref[2 * p + 1] += jnp.where(tril, _dot_nt(jnp.where(left, zero, dmb), blk), 0.0)
            dvn_rows.append(jnp.concatenate(blocks, axis=1))
        dbacc_ref[...] += dbs_sum
        dvn = jnp.concatenate(dvn_rows, axis=0)
        dsgg_ref[...] += jnp.sum(dvn * vhat, axis=0, keepdims=True)
        dsgb_ref[...] += jnp.sum(dvn, axis=0, keepdims=True)
        dvh = dvn * sgg_ref[...]
        dgv_in = rstd_v * (dvh - jnp.mean(dvh, axis=-1, keepdims=True)
                           - vhat * jnp.mean(dvh * vhat, axis=-1, keepdims=True))
        emit(dhm_ref, dhmt_ref, dbm_ref, 2 * SGU_W, dgv_in * dgv)

        @pl.when(i == nsteps - 1)
        def _():
            grp = (lax.broadcasted_iota(jnp.int32, (SGU_W, CHUNK), 0) // V_DIM
                   == lax.broadcasted_iota(jnp.int32, (SGU_W, CHUNK), 1)).astype(BF16)
            hi, mid, lo = _split3(dbacc_ref[...])
            dbs_ref[...] = _dot(hi, grp) + _dot(mid, grp) + _dot(lo, grp)

    acc_shapes = [(D_MODEL, D_MODEL), woa.shape, wob.shape, (GROUPS, CHUNK, CHUNK), (CHUNK, CHUNK),
                  (1, D_MODEL), (1, D_MODEL), (1, SGU_W), (1, SGU_W), (1, 128), (1, GATE_W), (1, MID_W)]
    col_spec = lambda rows: pl.BlockSpec((rows, ts), lambda i: (0, i))
    return pl.pallas_call(
        body, name="mid", grid=(nsteps,),
        in_specs=[_row_spec(ts, D_MODEL), _row_spec(ts, D_MODEL), _row_spec(ts, MLA_W), _row_spec(ts, MID_W),
                  _row_spec(ts, GATE_W), _full_spec(woa.shape), _full_spec(wob.shape), _full_spec(wout.shape),
                  _full_spec(ln_g.shape), _full_spec(ln_b.shape), _full_spec(sg_g.shape), _full_spec(sg_b.shape),
                  _full_spec(w_s.shape), _full_spec(bsb.shape)],
        out_specs=[_row_spec(ts, D_MODEL), _row_spec(ts, GATE_W), _row_spec(ts, MID_W), _row_spec(ts, MLA_W),
                   col_spec(MLA_W), col_spec(HEADS), col_spec(GATE_W), col_spec(MID_W)]
        + [_full_spec(sh) for sh in acc_shapes],
        out_shape=[jax.ShapeDtypeStruct((s, D_MODEL), F32), jax.ShapeDtypeStruct((s, GATE_W), BF16),
                   jax.ShapeDtypeStruct((s, MID_W), BF16), jax.ShapeDtypeStruct((s, MLA_W), BF16),
                   jax.ShapeDtypeStruct((MLA_W, s), BF16), jax.ShapeDtypeStruct((HEADS, s), F32),
                   jax.ShapeDtypeStruct((GATE_W, s), BF16), jax.ShapeDtypeStruct((MID_W, s), BF16)]
        + [jax.ShapeDtypeStruct(sh, F32) for sh in acc_shapes],
        scratch_shapes=[pltpu.VMEM((CHUNK, SGU_W), F32)],
        compiler_params=_params(),
    )(x, tgt, o, hm, hg, woa, wob, wout, ln_g, ln_b, sg_g, sg_b, w_s, bsb)


def _lat_bwd(dq, dk, dv, hl, rc, rsl, rsh, g_q, g_kv, wuq, wk, wv, after):
    s = dk.shape[0]
    ts = ROW_TILE
    qk_w = HEADS * HEAD_PAD

    def body(dq_ref, dk_ref, dv_ref, hl_ref, rc_ref, rsl_ref, rsh_ref, gq_ref, gkv_ref, wuq_ref, wk_ref, wv_ref,
             after_ref, dhl_ref, dhlt_ref, dwuq_ref, dwk_ref, dwv_ref, dgq_ref, dgkv_ref, dbl_ref):
        i = pl.program_id(0)

        @pl.when(i == 0)
        def _():
            for r in (dwuq_ref, dwk_ref, dwv_ref, dgq_ref, dgkv_ref, dbl_ref):
                r[...] = jnp.zeros_like(r)

        def emit(lo, val):
            vb = val.astype(BF16)
            n = val.shape[1]
            dhl_ref[:, lo:lo + n] = vb
            dhlt_ref[lo:lo + n, :] = vb.T
            dbl_ref[:, lo:lo + n] += jnp.sum(val, axis=0, keepdims=True)

        c, sl, sh = rc_ref[...], rsl_ref[...], rsh_ref[...]
        lane = lax.broadcasted_iota(jnp.int32, (ts, HEAD_PAD), 1)
        pe = (lane >= NOPE) & (lane < QK_DIM)
        dkpe = jnp.zeros((ts, HEAD_PAD), F32)
        dqu = []
        for hd in range(HEADS):
            lanes = slice(hd * HEAD_PAD, (hd + 1) * HEAD_PAD)
            dqu.append(_rope_t(dq_ref[lanes, :].T, c, sl, sh).astype(BF16))
            dkpe = dkpe + dk_ref[:, lanes]
        dqu = jnp.concatenate(dqu, axis=1)
        dkpe = _rope_t(jnp.where(pe, dkpe, 0.0), c, sl, sh)

        cq = hl_ref[:, :Q_RANK]
        rq = lax.rsqrt(jnp.mean(cq * cq, axis=-1, keepdims=True) + RMS_EPS)
        cqh = cq * rq
        cqn = (cqh * gq_ref[...]).astype(BF16)
        dwuq_ref[...] += _dot_tn(cqn, dqu)
        dcqn = _dot_nt(dqu, wuq_ref[...])
        dgq_ref[...] += jnp.sum(dcqn * cqh, axis=0, keepdims=True)
        dch = dcqn * gq_ref[...]
        emit(0, rq * (dch - cqh * jnp.mean(dch * cqh, axis=-1, keepdims=True)))

        ckv = hl_ref[:, Q_RANK:Q_RANK + KV_RANK]
        rk = lax.rsqrt(jnp.mean(ckv * ckv, axis=-1, keepdims=True) + RMS_EPS)
        ckh = ckv * rk
        ckn = (ckh * gkv_ref[...]).astype(BF16)
        dkb = dk_ref[...].astype(BF16)
        dvb = dv_ref[...].astype(BF16)
        dwk_ref[...] += _dot_tn(ckn, dkb)
        dwv_ref[...] += _dot_tn(ckn, dvb)
        dckn = _dot_nt(dkb, wk_ref[...]) + _dot_nt(dvb, wv_ref[...])
        dgkv_ref[...] += jnp.sum(dckn * ckh, axis=0, keepdims=True)
        dkh = dckn * gkv_ref[...]
        emit(Q_RANK, rk * (dkh - ckh * jnp.mean(dkh * ckh, axis=-1, keepdims=True)))
        emit(Q_RANK + KV_RANK, dkpe)

    acc_shapes = [wuq.shape, wk.shape, wv.shape, g_q.shape, g_kv.shape, (1, LAT_W)]
    return pl.pallas_call(
        body, name="lat_bwd", grid=(s // ts,),
        in_specs=[pl.BlockSpec((qk_w, ts), lambda i: (0, i)), _row_spec(ts, qk_w), _row_spec(ts, MLA_W),
                  _row_spec(ts, LAT_W), _row_spec(ts, HEAD_PAD), _row_spec(ts, HEAD_PAD), _row_spec(ts, HEAD_PAD),
                  _full_spec(g_q.shape), _full_spec(g_kv.shape), _full_spec(wuq.shape), _full_spec(wk.shape),
                  _full_spec(wv.shape), pl.BlockSpec(memory_space=pl.ANY)],
        out_specs=[_row_spec(ts, LAT_W), pl.BlockSpec((LAT_W, ts), lambda i: (0, i))]
        + [_full_spec(sh) for sh in acc_shapes],
        out_shape=[jax.ShapeDtypeStruct((s, LAT_W), BF16), jax.ShapeDtypeStruct((LAT_W, s), BF16)]
        + [jax.ShapeDtypeStruct(sh, F32) for sh in acc_shapes],
        compiler_params=_params(),
    )(dq, dk, dv, hl, rc, rsl, rsh, g_q, g_kv, wuq, wk, wv, after)


def _dx(dr, dhg, dhm, dhl, wt, after):
    s = dr.shape[0]
    ts = MATMUL_ROW_TILE

    def body(dr_ref, dhg_ref, dhm_ref, dhl_ref, wt_ref, after_ref, dx_ref):
        dx_ref[...] = (ALPHA * dr_ref[...]
                       + _dot(dhg_ref[...], wt_ref[ROW_GATE:IN_W, :])
                       + _dot(dhm_ref[...], wt_ref[LAT_COLS:ROW_GATE, :])
                       + _dot(dhl_ref[:, 0:Q_RANK + KV_RANK], wt_ref[0:Q_RANK + KV_RANK, :])
                       + _dot(dhl_ref[:, Q_RANK + KV_RANK:], _kpe_rows(wt_ref)))

    return pl.pallas_call(
        body, name="dx", grid=(s // ts,),
        in_specs=[_row_spec(ts, D_MODEL), _row_spec(ts, GATE_W), _row_spec(ts, MID_W), _row_spec(ts, LAT_W),
                  _full_spec(wt.shape), pl.BlockSpec(memory_space=pl.ANY)],
        out_specs=_row_spec(ts, D_MODEL),
        out_shape=jax.ShapeDtypeStruct((s, D_MODEL), F32),
        compiler_params=_params(),
    )(dr, dhg, dhm, dhl, wt, after)


def _dwt_early(dhmt, dhgt, xb, col, after, name):
    tn = 512
    nm, ng = MID_W // tn, GATE_W // tn
    s = dhmt.shape[1]
    hc = D_MODEL // 2

    def body(col_ref, dm_ref, dg_ref, xb_ref, after_ref, dw_ref):
        i = pl.program_id(0)

        @pl.when(i < nm)
        def _():
            dw_ref[...] = _dot(dm_ref[...], xb_ref[...]).astype(BF16)

        @pl.when(i >= nm)
        def _():
            dw_ref[...] = _dot(dg_ref[...], xb_ref[...]).astype(BF16)

    rows = pl.pallas_call(
        body, name=name,
        grid_spec=pltpu.PrefetchScalarGridSpec(
            num_scalar_prefetch=1, grid=(nm + ng,),
            in_specs=[pl.BlockSpec((tn, s), lambda i, col_ref: (jnp.minimum(i, nm - 1), 0)),
                      pl.BlockSpec((tn, s), lambda i, col_ref: (jnp.maximum(i - nm, 0), 0)),
                      pl.BlockSpec((s, hc), lambda i, col_ref: (0, col_ref[0])),
                      pl.BlockSpec(memory_space=pl.ANY)],
            out_specs=pl.BlockSpec((pl.Element(tn), pl.Element(hc)),
                                   lambda i, col_ref: (pl.multiple_of(LAT_COLS + i * tn, 32), 0))),
        out_shape=jax.ShapeDtypeStruct((IN_W, hc), BF16),
        compiler_params=_params(),
    )(col, dhmt, dhgt, xb, after)

    def zero(buf_ref, out_ref):
        out_ref[...] = jnp.zeros_like(out_ref)

    return pl.pallas_call(
        zero, name=name + "_zero_lat", grid=(1,), in_specs=[pl.BlockSpec(memory_space=pl.ANY)],
        out_specs=pl.BlockSpec((LAT_COLS, hc), lambda i: (0, 0)),
        out_shape=jax.ShapeDtypeStruct((IN_W, hc), BF16), input_output_aliases={0: 0},
    )(rows)


def _dwt_lat(dhlt, xb):
    n, s = dhlt.shape

    def body(dht_ref, xb_ref, dw_ref):
        dw = _dot(dht_ref[...], xb_ref[...]).astype(BF16)
        kpe = Q_RANK + KV_RANK + NOPE
        dw_ref[0:Q_RANK + KV_RANK, :] = dw[0:Q_RANK + KV_RANK]
        dw_ref[Q_RANK + KV_RANK:LAT_COLS, :] = dw[kpe:kpe + ROPE]
        dw_ref[LAT_COLS:, :] = jnp.zeros((LAT_ROWS_PAD - LAT_COLS, D_MODEL), BF16)

    return pl.pallas_call(
        body, name="dwt_lat", in_specs=[VMEM_SPEC, VMEM_SPEC], out_specs=VMEM_SPEC,
        out_shape=jax.ShapeDtypeStruct((LAT_ROWS_PAD, D_MODEL), BF16),
        compiler_params=pltpu.CompilerParams(vmem_limit_bytes=VMEM_LIMIT),
    )(dhlt, xb)


def _split_bias(b):
    z = lambda n: jnp.zeros((n,), b.dtype)
    lat = jnp.concatenate([b[:Q_RANK + KV_RANK], z(NOPE), b[Q_RANK + KV_RANK:LAT_COLS], z(HEAD_PAD - QK_DIM)])
    return b[None, ROW_GATE:], b[None, LAT_COLS:ROW_GATE], lat[None, :]


def _join_bias(g, m, l):
    kpe = Q_RANK + KV_RANK + NOPE
    return jnp.concatenate([l[0, :Q_RANK + KV_RANK], l[0, kpe:kpe + ROPE], m[0], g[0]])


def _rope_tables(positions):
    half = ROPE // 2
    inv_freq = ROPE_THETA ** (-jnp.arange(0, ROPE, 2, dtype=F32) / ROPE)
    ang = positions.astype(F32)[:, None] * inv_freq
    cos, sin = jnp.cos(ang), jnp.sin(ang)
    n = positions.shape[0]
    one, zero = jnp.ones((n, NOPE), F32), jnp.zeros((n, half), F32)
    tail1, tail0 = jnp.ones((n, HEAD_PAD - QK_DIM), F32), jnp.zeros((n, HEAD_PAD - QK_DIM), F32)
    z64 = jnp.zeros((n, NOPE), F32)
    rc = jnp.concatenate([one, cos, cos, tail1], axis=1)
    rsl = jnp.concatenate([z64, -sin, zero, tail0], axis=1)
    rsh = jnp.concatenate([z64, zero, sin, tail0], axis=1)
    return rc, rsl, rsh


def _local_attention(x, positions, wlat, b_in, g_q, w_uq, g_kv, w_ukv, after):
    rc, rsl, rsh = _rope_tables(positions)
    b_g, b_m, b_l = _split_bias(b_in)
    wuq = jnp.pad(w_uq, ((0, 0), (0, 0), (0, HEAD_PAD - QK_DIM))).reshape(Q_RANK, HEADS * HEAD_PAD).astype(BF16)
    wk = jnp.pad(w_ukv[:, :, :NOPE], ((0, 0), (0, 0), (0, HEAD_PAD - NOPE))).reshape(KV_RANK, HEADS * HEAD_PAD).astype(BF16)
    wv = w_ukv[:, :, NOPE:].reshape(KV_RANK, MLA_W).astype(BF16)
    gq2, gkv2 = g_q[None, :], g_kv[None, :]
    hl, q, k, v, xb, qt, kt, vt = _fwd_lat(x, wlat, b_l, gq2, wuq, gkv2, wk, wv, rc, rsl, rsh, after)
    o, lse = _attn_fwd(qt, k, vt)
    return dict(q=q, qt=qt, k=k, kt=kt, v=v, o=o, lse=lse, hl=hl, rc=rc, rsl=rsl, rsh=rsh, gq2=gq2, gkv2=gkv2,
                wuq=wuq, wk=wk, wv=wv, xb=xb, b_g=b_g, b_m=b_m)


def _local_head(st, x, tgt, wt, w_oa, sg_g, sg_b, w_s, b_s, w_ob, w_out, ln_g, ln_b):
    q, qt, k, kt, v, o, lse, hl, xb = (st[n] for n in ("q", "qt", "k", "kt", "v", "o", "lse", "hl", "xb"))
    rc, rsl, rsh, gq2, gkv2, wuq, wk, wv = (st[n] for n in ("rc", "rsl", "rsh", "gq2", "gkv2", "wuq", "wk", "wv"))
    bsb = jnp.repeat(b_s.T, V_DIM, axis=1)
    hg, hm = _fwd_rest(xb, wt, st["b_g"], st["b_m"])
    (dr, dhg, dhm, do, dot, delta, dhgt, dhmt, dwout, dwoa, dwob, dws, dbs, dlng, dlnb, dsgg, dsgb, loss, dbg,
     dbm) = _mid(x, tgt, o, hm, hg, w_oa, w_ob, w_out, ln_g[None, :], ln_b[None, :], sg_g[None, :], sg_b[None, :],
                 w_s, bsb)
    delta = jnp.pad(delta.reshape(HEADS // 2, 2, -1), ((0, 0), (0, 6), (0, 0)))
    early = {
        "w_oa": dwoa, "sgu_ln_g": dsgg[0], "sgu_ln_b": dsgb[0], "w_s": dws, "b_s": dbs[:, :GROUPS].T,
        "w_ob": dwob, "w_out": dwout, "ln_g": dlng[0], "ln_b": dlnb[0],
    }
    state = dict(q=q, qt=qt, k=k, kt=kt, v=v, do=do, dot=dot, lse=lse, delta=delta, hl=hl, rc=rc, rsl=rsl, rsh=rsh,
                 gq2=gq2, gkv2=gkv2, wuq=wuq, wk=wk, wv=wv, dr=dr, dhg=dhg, dhm=dhm, wt=wt, xb=xb, dbg=dbg, dbm=dbm,
                 dhgt=dhgt, dhmt=dhmt)
    return loss, early, state


def _local_attn_bwd(st):
    return _attn_bwd(st["q"], st["qt"], st["k"], st["kt"], st["v"], st["do"], st["dot"], st["lse"], st["delta"])


def _local_tail(st, dq, dk, dv, after):
    dhl, dhlt, dwuq, dwk, dwv, dgq, dgkv, dbl = _lat_bwd(dq, dk, dv, st["hl"], st["rc"], st["rsl"], st["rsh"],
                                                         st["gq2"], st["gkv2"], st["wuq"], st["wk"], st["wv"], after)
    late = {
        "w_lat": _dwt_lat(dhlt, st["xb"]),
        "b_in": _join_bias(st["dbg"], st["dbm"], dbl),
        "g_q": dgq[0],
        "w_uq": dwuq.reshape(Q_RANK, HEADS, HEAD_PAD)[:, :, :QK_DIM],
        "g_kv": dgkv[0],
        "w_ukv": jnp.concatenate([dwk.reshape(KV_RANK, HEADS, HEAD_PAD)[:, :, :NOPE],
                                  dwv.reshape(KV_RANK, HEADS, V_DIM)], axis=2),
    }
    return dhl, late


def _local_step(x, positions, tgt, wt, b_in, g_q, w_uq, g_kv, w_ukv, w_oa, sg_g, sg_b, w_s, b_s, w_ob, w_out, ln_g,
                ln_b):
    st = _local_attention(x, positions, wt[:LAT_COLS], b_in, g_q, w_uq, g_kv, w_ukv, b_in)
    loss, early, st = _local_head(st, x, tgt, wt, w_oa, sg_g, sg_b, w_s, b_s, w_ob, w_out, ln_g, ln_b)
    dq, dk, dv = _local_attn_bwd(st)
    dhl, late = _local_tail(st, dq, dk, dv, dv)
    dx = _dx(st["dr"], st["dhg"], st["dhm"], dhl, st["wt"], dhl)
    grads = {**early, **late}
    halves = [_dwt_early(st["dhmt"], st["dhgt"], st["xb"], jnp.full((1,), h, jnp.int32), dhl, "dwt_half%d" % h)
              for h in range(2)]
    grads["w_in"] = jnp.concatenate([grads.pop("w_lat")[:LAT_COLS], jnp.concatenate(halves, axis=1)[LAT_COLS:]],
                                    axis=0)
    return loss, dx, grads


MESH = pl.DeviceIdType.MESH
N_CHIPS = 4
HBM_SPEC = pl.BlockSpec(memory_space=pl.ANY)
HBM_SPEC_STRICT = pl.BlockSpec(memory_space=pltpu.HBM)
VMEM_SPEC = pl.BlockSpec(memory_space=pltpu.VMEM)

REP_ROWS = 80


def _rows8(a):
    flat = a.reshape(-1)
    n = -(-flat.shape[0] // (8 * D_MODEL)) * 8 * D_MODEL
    return jnp.pad(flat, (0, n - flat.shape[0])).reshape(-1, D_MODEL)


def _place():
    x, y, c = lax.axis_index("x"), lax.axis_index("y"), lax.axis_index("c")
    others = [(1 - x, y), (x, 1 - y), (1 - x, 1 - y)]
    return x, y, c, others


def _gather_weights(shards):
    n = len(shards)

    def body(*refs):
        ins, outs, bufs = refs[:n], refs[n:2 * n], refs[2 * n:3 * n]
        send_sems, recv_sems, local_sems = refs[3 * n:]
        x, y, c, others = _place()
        me = 2 * x + y
        sibling = (x, y, 1 - c)
        for src, buf in zip(ins, bufs):
            buf[...] = src[...].astype(BF16)
        own = [pltpu.make_async_copy(bufs[w], outs[w].at[me], local_sems.at[w]) for w in range(n)]
        for cp in own:
            cp.start()

        def part(w, chip, half):
            hc = shards[w].shape[1] // 2
            return outs[w].at[chip, :, pl.ds(half * hc, hc)]

        def sent(w, j):
            hc = shards[w].shape[1] // 2
            return pltpu.make_async_remote_copy(
                src_ref=bufs[w].at[:, pl.ds(c * hc, hc)], dst_ref=part(w, me, c),
                send_sem=send_sems.at[w * 3 + j], recv_sem=recv_sems.at[w * 3 + j],
                device_id=(*others[j], c), device_id_type=MESH)

        def landed(w, j):
            px, py = others[j]
            return pltpu.make_async_remote_copy(
                src_ref=part(w, 2 * px + py, c), dst_ref=part(w, 2 * px + py, c),
                send_sem=send_sems.at[w * 3 + j], recv_sem=recv_sems.at[w * 3 + j],
                device_id=(px, py, c), device_id_type=MESH)

        def passed(w, j, half):
            px, py = others[j]
            k = n * 3 + w * 3 + j
            return pltpu.make_async_remote_copy(
                src_ref=part(w, 2 * px + py, half), dst_ref=part(w, 2 * px + py, half),
                send_sem=send_sems.at[k], recv_sem=recv_sems.at[k], device_id=sibling, device_id_type=MESH)

        first = [sent(w, j) for w in range(n) for j in range(3)]
        for cp in first:
            cp.start()
        fwd = []
        for w in range(n):
            for j in range(3):
                landed(w, j).wait_recv()
                cp = passed(w, j, c)
                cp.start()
                fwd.append(cp)
        for w in range(n):
            for j in range(3):
                passed(w, j, 1 - c).wait_recv()
        for cp in first + fwd:
            cp.wait_send()
        for cp in own:
            cp.wait()

    return pl.pallas_call(
        body, name="gather_weights",
        in_specs=[VMEM_SPEC] * n, out_specs=[HBM_SPEC] * n,
        out_shape=[jax.ShapeDtypeStruct((N_CHIPS,) + s.shape, BF16) for s in shards],
        scratch_shapes=[pltpu.VMEM(s.shape, BF16) for s in shards]
        + [pltpu.SemaphoreType.DMA((6 * n,)), pltpu.SemaphoreType.DMA((6 * n,)), pltpu.SemaphoreType.DMA((n,))],
        compiler_params=pltpu.CompilerParams(vmem_limit_bytes=VMEM_LIMIT),
    )(*shards)


N_DEV = 8
LOSS_TILE = (8, 128)


def _gather_first(lat, uq):
    hl, hu = lat.shape[1] // 2, uq.shape[1] // 2

    def body(lat_ref, uq_ref, wlat_ref, guq_ref, lat_buf, uq_buf, send_sems, recv_sems, local_sems):
        x, y, c, others = _place()
        me = 2 * x + y
        sibling = (x, y, 1 - c)
        lat_buf[...] = lat_ref[...].astype(BF16)
        uq_buf[...] = uq_ref[...].astype(BF16)

        def copy(src, dst, k, to):
            return pltpu.make_async_remote_copy(src_ref=src, dst_ref=dst, send_sem=send_sems.at[k],
                                                recv_sem=recv_sems.at[k], device_id=to, device_id_type=MESH)

        def uq_part(chip, half):
            return guq_ref.at[chip, :, pl.ds(half * hu, hu)]

        def lat_part(half):
            return wlat_ref.at[:, pl.ds(half * hl, hl)]

        own = pltpu.make_async_copy(uq_buf, guq_ref.at[me], local_sems.at[0])
        own.start()
        first = [copy(uq_buf.at[:, pl.ds(c * hu, hu)], uq_part(me, c), j, (*others[j], c)) for j in range(3)]
        for cp in first:
            cp.start()

        @pl.when(me == 0)
        def _():
            mine = pltpu.make_async_copy(lat_buf, wlat_ref, local_sems.at[1])
            mine.start()
            cps = [copy(lat_buf.at[:, pl.ds(c * hl, hl)], lat_part(c), 6 + j, (*others[j], c)) for j in range(3)]
            for cp in cps:
                cp.start()
            for cp in cps:
                cp.wait_send()
            mine.wait()

        @pl.when(me != 0)
        def _():
            j0 = x + 2 * y - 1
            copy(lat_part(c), lat_part(c), 6 + j0, (0, 0, c)).wait_recv()
            fwd = copy(lat_part(c), lat_part(c), 9, sibling)
            fwd.start()
            copy(lat_part(1 - c), lat_part(1 - c), 9, sibling).wait_recv()
            fwd.wait_send()

        fwd = []
        for j, (px, py) in enumerate(others):
            chip = 2 * px + py
            copy(uq_part(chip, c), uq_part(chip, c), j, (px, py, c)).wait_recv()
            cp = copy(uq_part(chip, c), uq_part(chip, c), 3 + j, sibling)
            cp.start()
            fwd.append(cp)
        for j, (px, py) in enumerate(others):
            chip = 2 * px + py
            copy(uq_part(chip, 1 - c), uq_part(chip, 1 - c), 3 + j, sibling).wait_recv()
        for cp in first + fwd:
            cp.wait_send()
        own.wait()

    return pl.pallas_call(
        body, name="gather_first", in_specs=[VMEM_SPEC, VMEM_SPEC], out_specs=[HBM_SPEC, HBM_SPEC],
        out_shape=[jax.ShapeDtypeStruct(lat.shape, BF16), jax.ShapeDtypeStruct((N_CHIPS,) + uq.shape, BF16)],
        scratch_shapes=[pltpu.VMEM(lat.shape, BF16), pltpu.VMEM(uq.shape, BF16), pltpu.SemaphoreType.DMA((10,)),
                        pltpu.SemaphoreType.DMA((10,)), pltpu.SemaphoreType.DMA((2,))],
        compiler_params=pltpu.CompilerParams(vmem_limit_bytes=VMEM_LIMIT),
    )(lat, uq)


def _cast_own(shards, me, after):
    n = len(shards)

    def body(me_ref, *refs):
        for w in range(n):
            refs[n + 1 + w][...] = refs[w][...].astype(BF16)

    return pl.pallas_call(
        body, name="cast_own",
        grid_spec=pltpu.PrefetchScalarGridSpec(
            num_scalar_prefetch=1, grid=(1,),
            in_specs=[pl.BlockSpec(s.shape, lambda i, me_ref: (0, 0)) for s in shards]
            + [pl.BlockSpec(memory_space=pl.ANY)],
            out_specs=[pl.BlockSpec((None,) + s.shape, lambda i, me_ref: (me_ref[0], 0, 0)) for s in shards]),
        out_shape=[jax.ShapeDtypeStruct((N_CHIPS,) + s.shape, BF16) for s in shards],
        compiler_params=pltpu.CompilerParams(vmem_limit_bytes=VMEM_LIMIT),
    )(me, *shards, after)


def _cast_first(lat, uq, me):
    def body(me_ref, lat_ref, uq_ref, wlat_ref, guq_ref):
        wlat_ref[...] = lat_ref[...].astype(BF16)
        guq_ref[...] = uq_ref[...].astype(BF16)

    return pl.pallas_call(
        body, name="cast_first",
        grid_spec=pltpu.PrefetchScalarGridSpec(
            num_scalar_prefetch=1, grid=(1,),
            in_specs=[pl.BlockSpec(lat.shape, lambda i, me_ref: (0, 0)),
                      pl.BlockSpec(uq.shape, lambda i, me_ref: (0, 0))],
            out_specs=[pl.BlockSpec(lat.shape, lambda i, me_ref: (0, 0)),
                       pl.BlockSpec((None,) + uq.shape, lambda i, me_ref: (me_ref[0], 0, 0))]),
        out_shape=[jax.ShapeDtypeStruct(lat.shape, BF16), jax.ShapeDtypeStruct((N_CHIPS,) + uq.shape, BF16)],
    )(me, lat, uq)


def _first_copies(wlat_ref, guq_ref, send_sems, recv_sems, shapes):
    x, y, c, others = _place()
    me = 2 * x + y
    hl, hu = shapes[0][1] // 2, shapes[1][2] // 2
    lat_half = wlat_ref.at[:, pl.ds(c * hl, hl)]

    def copy(src, dst, k, to):
        return pltpu.make_async_remote_copy(src_ref=src, dst_ref=dst, send_sem=send_sems.at[k],
                                            recv_sem=recv_sems.at[k], device_id=to, device_id_type=MESH)

    def uq_half(chip):
        return guq_ref.at[chip, :, pl.ds(c * hu, hu)]

    lat_out = [copy(lat_half, lat_half, j, (*others[j], c)) for j in range(3)]
    uq_out = [copy(uq_half(me), uq_half(me), 3 + j, (*others[j], c)) for j in range(3)]
    j0 = jnp.maximum(x + 2 * y - 1, 0)
    lat_in = copy(lat_half, lat_half, j0, (0, 0, c))
    uq_in = [copy(uq_half(me), uq_half(2 * px + py), 3 + j, (px, py, c)) for j, (px, py) in enumerate(others)]
    return me, lat_out, uq_out, lat_in, uq_in


def _first_start(wlat, guq):
    shapes = (wlat.shape, guq.shape)

    def body(wlat_ref, guq_ref, send_sems, recv_sems, wlat_thru, guq_thru, token):
        me, lat_out, uq_out, _, _ = _first_copies(wlat_ref, guq_ref, send_sems, recv_sems, shapes)

        @pl.when(me == 0)
        def _():
            for cp in lat_out:
                cp.start()

        for cp in uq_out:
            cp.start()
        token[...] = jnp.zeros_like(token)

    outs = pl.pallas_call(
        body, name="first_start",
        out_shape=(pltpu.SemaphoreType.DMA((6,)), pltpu.SemaphoreType.DMA((6,)), pltpu.HBM(wlat.shape, BF16),
                   pltpu.HBM(guq.shape, BF16), jax.ShapeDtypeStruct(LOSS_TILE, F32)),
        in_specs=[HBM_SPEC_STRICT] * 2, out_specs=(SEM_SPEC, SEM_SPEC, HBM_SPEC_STRICT, HBM_SPEC_STRICT, VMEM_SPEC),
        input_output_aliases={0: 2, 1: 3},
        compiler_params=pltpu.CompilerParams(has_side_effects=SPLIT_EFFECT),
    )(pltpu.with_memory_space_constraint(wlat, pltpu.HBM), pltpu.with_memory_space_constraint(guq, pltpu.HBM))
    return outs


def _first_wait(send_sems, recv_sems, wlat, guq, after):
    shapes = (wlat.shape, guq.shape)

    def body(wlat_ref, guq_ref, send_sems, recv_sems, after_ref, wlat_out, guq_out):
        me, lat_out, uq_out, lat_in, uq_in = _first_copies(wlat_ref, guq_ref, send_sems, recv_sems, shapes)

        @pl.when(me == 0)
        def _():
            for cp in lat_out:
                cp.wait_send()

        @pl.when(me != 0)
        def _():
            lat_in.wait_recv()

        for cp in uq_out:
            cp.wait_send()
        for cp in uq_in:
            cp.wait_recv()

    return pl.pallas_call(
        body, name="first_wait", out_shape=(pltpu.HBM(wlat.shape, BF16), pltpu.HBM(guq.shape, BF16)),
        in_specs=[HBM_SPEC_STRICT, HBM_SPEC_STRICT, SEM_SPEC, SEM_SPEC, HBM_SPEC],
        out_specs=(HBM_SPEC_STRICT, HBM_SPEC_STRICT), input_output_aliases={0: 0, 1: 1},
        compiler_params=pltpu.CompilerParams(has_side_effects=SPLIT_EFFECT),
    )(wlat, guq, send_sems, recv_sems, after)


def _first_forward(wlat, guq):
    hl, hu = wlat.shape[1] // 2, guq.shape[2] // 2

    def body(wlat_in, guq_in, wlat_ref, guq_ref, send_sems, recv_sems):
        x, y, c, others = _place()
        me = 2 * x + y
        sibling = (x, y, 1 - c)

        def copy(part, k):
            return pltpu.make_async_remote_copy(src_ref=part, dst_ref=part, send_sem=send_sems.at[k],
                                                recv_sem=recv_sems.at[k], device_id=sibling, device_id_type=MESH)

        def uq_part(j, half):
            px, py = others[j]
            return guq_ref.at[2 * px + py, :, pl.ds(half * hu, hu)]

        cps = [copy(uq_part(j, c), j) for j in range(3)]
        for cp in cps:
            cp.start()

        @pl.when(me != 0)
        def _():
            mine = copy(wlat_ref.at[:, pl.ds(c * hl, hl)], 3)
            mine.start()
            copy(wlat_ref.at[:, pl.ds((1 - c) * hl, hl)], 3).wait_recv()
            mine.wait_send()

        for j in range(3):
            copy(uq_part(j, 1 - c), j).wait_recv()
        for cp in cps:
            cp.wait_send()

    return pl.pallas_call(
        body, name="first_forward", in_specs=[HBM_SPEC, HBM_SPEC], out_specs=[HBM_SPEC, HBM_SPEC],
        out_shape=[jax.ShapeDtypeStruct(wlat.shape, BF16), jax.ShapeDtypeStruct(guq.shape, BF16)],
        input_output_aliases={0: 0, 1: 1},
        scratch_shapes=[pltpu.SemaphoreType.DMA((4,)), pltpu.SemaphoreType.DMA((4,))],
    )(wlat, guq)


def _gather_start(bufs, after):
    n = len(bufs)

    def body(*refs):
        b_refs = refs[:n]
        send_sems, recv_sems, token = refs[n + 1], refs[n + 2], refs[-1]
        x, y, c, others = _place()
        me = 2 * x + y
        for w in range(n):
            hc = _half(bufs[w])
            mine = b_refs[w].at[me, :, pl.ds(c * hc, hc)]
            for j, (px, py) in enumerate(others):
                pltpu.make_async_remote_copy(
                    src_ref=mine, dst_ref=mine, send_sem=send_sems.at[3 * w + j], recv_sem=recv_sems.at[3 * w + j],
                    device_id=(px, py, c), device_id_type=MESH).start()
        token[...] = jnp.zeros_like(token)

    hbm = [pltpu.HBM(b.shape, BF16) for b in bufs]
    outs = pl.pallas_call(
        body, name="gather_start",
        out_shape=(pltpu.SemaphoreType.DMA((3 * n,)), pltpu.SemaphoreType.DMA((3 * n,)), *hbm,
                   jax.ShapeDtypeStruct(LOSS_TILE, F32)),
        in_specs=[HBM_SPEC_STRICT] * n + [HBM_SPEC],
        out_specs=(SEM_SPEC, SEM_SPEC, *[HBM_SPEC_STRICT] * n, VMEM_SPEC),
        input_output_aliases={i: 2 + i for i in range(n)},
        compiler_params=pltpu.CompilerParams(has_side_effects=SPLIT_EFFECT),
    )(*[pltpu.with_memory_space_constraint(b, pltpu.HBM) for b in bufs], after)
    return outs[0], outs[1], list(outs[2:2 + n]), outs[-1]


def _gather_wait(send_sems, recv_sems, bufs, after):
    n = len(bufs)

    def body(*refs):
        b_refs = refs[:n]
        send_sems, recv_sems = refs[n], refs[n + 1]
        x, y, c, others = _place()
        me = 2 * x + y
        for w in range(n):
            hc = _half(bufs[w])
            for j, (px, py) in enumerate(others):
                cp = pltpu.make_async_remote_copy(
                    src_ref=b_refs[w].at[me, :, pl.ds(c * hc, hc)],
                    dst_ref=b_refs[w].at[2 * px + py, :, pl.ds(c * hc, hc)],
                    send_sem=send_sems.at[3 * w + j], recv_sem=recv_sems.at[3 * w + j], device_id=(px, py, c),
                    device_id_type=MESH)
                cp.wait_send()
                cp.wait_recv()

    outs = pl.pallas_call(
        body, name="gather_wait", out_shape=tuple(pltpu.HBM(b.shape, b.dtype) for b in bufs),
        in_specs=[HBM_SPEC_STRICT] * n + [SEM_SPEC, SEM_SPEC, HBM_SPEC],
        out_specs=tuple([HBM_SPEC_STRICT] * n), input_output_aliases={i: i for i in range(n)},
        compiler_params=pltpu.CompilerParams(has_side_effects=SPLIT_EFFECT),
    )(*bufs, send_sems, recv_sems, after)
    return list(outs)


def _gather_finish(bufs):
    n = len(bufs)

    def body(*refs):
        b_refs = refs[n:2 * n]
        send_sems, recv_sems = refs[2 * n:]
        x, y, c, others = _place()
        cps = []
        for w in range(n):
            hc = _half(bufs[w])
            for j, (px, py) in enumerate(others):
                part = b_refs[w].at[2 * px + py, :, pl.ds(c * hc, hc)]
                cps.append(pltpu.make_async_remote_copy(
                    src_ref=part, dst_ref=part, send_sem=send_sems.at[3 * w + j], recv_sem=recv_sems.at[3 * w + j],
                    device_id=(x, y, 1 - c), device_id_type=MESH))
        for cp in cps:
            cp.start()
        for w in range(n):
            hc = _half(bufs[w])
            for j, (px, py) in enumerate(others):
                theirs = b_refs[w].at[2 * px + py, :, pl.ds((1 - c) * hc, hc)]
                pltpu.make_async_remote_copy(
                    src_ref=theirs, dst_ref=theirs, send_sem=send_sems.at[3 * w + j], recv_sem=recv_sems.at[3 * w + j],
                    device_id=(x, y, 1 - c), device_id_type=MESH).wait_recv()
        for cp in cps:
            cp.wait_send()

    return pl.pallas_call(
        body, name="gather_finish", in_specs=[HBM_SPEC] * n, out_specs=[HBM_SPEC] * n,
        out_shape=[jax.ShapeDtypeStruct(b.shape, b.dtype) for b in bufs],
        input_output_aliases={i: i for i in range(n)},
        scratch_shapes=[pltpu.SemaphoreType.DMA((3 * n,)), pltpu.SemaphoreType.DMA((3 * n,))],
    )(*bufs)


def _half(a):
    return a.shape[-1] // 2


def _exchange_pairs(parts, name):
    n = len(parts)

    def body(*refs):
        p_refs, r_refs = refs[:n], refs[n:2 * n]
        send_sems, recv_sems = refs[2 * n:]
        x, y, c, _ = _place()
        cps = []
        for w in range(n):
            h = _half(parts[w])
            cps.append(pltpu.make_async_remote_copy(
                src_ref=p_refs[w].at[:, :, pl.ds((1 - c) * h, h)], dst_ref=r_refs[w],
                send_sem=send_sems.at[w], recv_sem=recv_sems.at[w], device_id=(x, y, 1 - c), device_id_type=MESH))
        for cp in cps:
            cp.start()
        for cp in cps:
            cp.wait()

    return pl.pallas_call(
        body, name=name, in_specs=[HBM_SPEC] * n, out_specs=[HBM_SPEC] * n,
        out_shape=[jax.ShapeDtypeStruct((N_CHIPS, p.shape[1], _half(p)), BF16) for p in parts],
        scratch_shapes=[pltpu.SemaphoreType.DMA((n,)), pltpu.SemaphoreType.DMA((n,))],
    )(*parts)


def _sibling_part(ref, w, n_whole, shape, c):
    if w < n_whole:
        return ref
    h = shape[-1] // 2
    return ref.at[:, :, pl.ds((1 - c) * h, h)]


def _pairs_start(parts, all_loss, n_whole):
    n = len(parts)

    def body(*refs):
        p_refs, r_refs, loss_ref = refs[:n], refs[n:2 * n], refs[2 * n]
        send_sems, recv_sems, token = refs[2 * n + 1], refs[2 * n + 2], refs[-1]
        x, y, c, _ = _place()
        for w in range(n):
            h = _half(parts[w])
            pltpu.make_async_remote_copy(
                src_ref=_sibling_part(p_refs[w], w, n_whole, parts[w].shape, c), dst_ref=r_refs[w],
                send_sem=send_sems.at[w], recv_sem=recv_sems.at[w], device_id=(x, y, 1 - c),
                device_id_type=MESH).start()
        me = 4 * x + 2 * y + c
        for t in range(1, N_DEV):
            d = (me + t) % N_DEV
            pltpu.make_async_remote_copy(
                src_ref=loss_ref.at[me], dst_ref=loss_ref.at[me], send_sem=send_sems.at[n + t - 1],
                recv_sem=recv_sems.at[n + t - 1], device_id=(d // 4, (d // 2) % 2, d % 2), device_id_type=MESH).start()
        token[...] = jnp.zeros_like(token)

    lands = [pltpu.HBM(p.shape if w < n_whole else (N_CHIPS, p.shape[1], _half(p)), BF16)
             for w, p in enumerate(parts)]
    nsem = n + N_DEV - 1
    outs = pl.pallas_call(
        body, name="pairs_start",
        out_shape=(pltpu.SemaphoreType.DMA((nsem,)), pltpu.SemaphoreType.DMA((nsem,)),
                   *[pltpu.HBM(p.shape, p.dtype) for p in parts], *lands, pltpu.HBM(all_loss.shape, F32),
                   jax.ShapeDtypeStruct(LOSS_TILE, F32)),
        in_specs=[HBM_SPEC_STRICT] * (2 * n + 1),
        out_specs=(SEM_SPEC, SEM_SPEC, *[HBM_SPEC_STRICT] * (2 * n + 1), VMEM_SPEC),
        input_output_aliases={i: 2 + i for i in range(2 * n + 1)},
        compiler_params=pltpu.CompilerParams(has_side_effects=SPLIT_EFFECT),
    )(*[pltpu.with_memory_space_constraint(p, pltpu.HBM) for p in parts],
      *[pltpu.with_memory_space_constraint(lax.empty(l.shape, BF16), pltpu.HBM) for l in lands],
      pltpu.with_memory_space_constraint(all_loss, pltpu.HBM))
    return outs[0], outs[1], list(outs[2:2 + n]), list(outs[2 + n:2 + 2 * n]), outs[2 + 2 * n], outs[-1]


def _pairs_wait(send_sems, recv_sems, parts, lands, all_loss, after, n_whole):
    n = len(parts)

    def body(*refs):
        p_refs, r_refs, loss_ref = refs[:n], refs[n:2 * n], refs[2 * n]
        send_sems, recv_sems = refs[2 * n + 1], refs[2 * n + 2]
        x, y, c, _ = _place()
        for w in range(n):
            h = _half(parts[w])
            cp = pltpu.make_async_remote_copy(
                src_ref=_sibling_part(p_refs[w], w, n_whole, parts[w].shape, c), dst_ref=r_refs[w],
                send_sem=send_sems.at[w],
                recv_sem=recv_sems.at[w], device_id=(x, y, 1 - c), device_id_type=MESH)
            cp.wait_send()
            cp.wait_recv()
        me = 4 * x + 2 * y + c
        for t in range(1, N_DEV):
            d = (me + N_DEV - t) % N_DEV
            cp = pltpu.make_async_remote_copy(
                src_ref=loss_ref.at[me], dst_ref=loss_ref.at[d], send_sem=send_sems.at[n + t - 1],
                recv_sem=recv_sems.at[n + t - 1], device_id=(d // 4, (d // 2) % 2, d % 2), device_id_type=MESH)
            cp.wait_send()
            cp.wait_recv()

    bufs = (*parts, *lands, all_loss)
    outs = pl.pallas_call(
        body, name="pairs_wait", out_shape=tuple(pltpu.HBM(a.shape, a.dtype) for a in bufs),
        in_specs=[HBM_SPEC_STRICT] * len(bufs) + [SEM_SPEC, SEM_SPEC, HBM_SPEC],
        out_specs=tuple([HBM_SPEC_STRICT] * len(bufs)), input_output_aliases={i: i for i in range(len(bufs))},
        compiler_params=pltpu.CompilerParams(has_side_effects=SPLIT_EFFECT),
    )(*bufs, send_sems, recv_sems, after)
    return list(outs[:n]), list(outs[n:2 * n]), outs[2 * n]


def _add_pair_tiled(p, r):
    rows, h = r.shape[1:]

    def body(p_ref, r_ref, q_ref):
        q_ref[...] = (p_ref[...].astype(F32) + r_ref[...].astype(F32)).astype(BF16)

    spec = pl.BlockSpec((None, rows, h), lambda k: (k, 0, 0))
    return pl.pallas_call(
        body, name="add_pair_w_in", grid=(N_CHIPS,), in_specs=[spec, spec], out_specs=spec,
        out_shape=jax.ShapeDtypeStruct(r.shape, BF16),
    )(p, r)


def _add_pair_small(ps, rs, c, name):
    n = len(ps)

    def body(c_ref, *refs):
        for w in range(n):
            h = _half(ps[w])
            mine = refs[w][:, :, pl.ds(pl.multiple_of(c_ref[0] * h, 128), h)]
            refs[2 * n + w][...] = (mine.astype(F32) + refs[n + w][...].astype(F32)).astype(BF16)

    return pl.pallas_call(
        body, name=name,
        in_specs=[pl.BlockSpec(memory_space=pltpu.SMEM)] + [VMEM_SPEC] * (2 * n), out_specs=[VMEM_SPEC] * n,
        out_shape=[jax.ShapeDtypeStruct(r.shape, BF16) for r in rs],
        compiler_params=pltpu.CompilerParams(vmem_limit_bytes=VMEM_LIMIT),
    )(c, *ps, *rs)


def _exchange_chips(qs):
    n = len(qs)

    def body(*refs):
        q_refs, r_refs = refs[:n], refs[n:2 * n]
        send_sems, recv_sems = refs[2 * n:]
        x, y, c, others = _place()
        me = 2 * x + y
        cps = []
        for w in range(n):
            for j, (px, py) in enumerate(others):
                cps.append(pltpu.make_async_remote_copy(
                    src_ref=q_refs[w].at[2 * px + py], dst_ref=r_refs[w].at[me], send_sem=send_sems.at[3 * w + j],
                    recv_sem=recv_sems.at[3 * w + j], device_id=(px, py, c), device_id_type=MESH))
        for cp in cps:
            cp.start()
        for w in range(n):
            for j, (px, py) in enumerate(others):
                pltpu.make_async_remote_copy(
                    src_ref=q_refs[w].at[me], dst_ref=r_refs[w].at[2 * px + py], send_sem=send_sems.at[3 * w + j],
                    recv_sem=recv_sems.at[3 * w + j], device_id=(px, py, c), device_id_type=MESH).wait_recv()
        for cp in cps:
            cp.wait_send()

    return pl.pallas_call(
        body, name="exchange_chips", in_specs=[HBM_SPEC] * n, out_specs=[HBM_SPEC] * n,
        out_shape=[jax.ShapeDtypeStruct(q.shape, BF16) for q in qs],
        scratch_shapes=[pltpu.SemaphoreType.DMA((3 * n,)), pltpu.SemaphoreType.DMA((3 * n,))],
    )(*qs)


SEM_SPEC = pl.BlockSpec(memory_space=pltpu.SEMAPHORE)
SPLIT_EFFECT = pltpu.SideEffectType.DATAFLOW_SIDE_EFFECTING


def _chips_start(qs, name):
    n = len(qs)

    def body(*refs):
        q_refs, land_refs = refs[:n], refs[n:2 * n]
        send_sems, recv_sems, token = refs[2 * n], refs[2 * n + 1], refs[-1]
        x, y, c, others = _place()
        me = 2 * x + y
        for w in range(n):
            for j, (px, py) in enumerate(others):
                pltpu.make_async_remote_copy(
                    src_ref=q_refs[w].at[2 * px + py], dst_ref=land_refs[w].at[me], send_sem=send_sems.at[3 * w + j],
                    recv_sem=recv_sems.at[3 * w + j], device_id=(px, py, c), device_id_type=MESH).start()
        token[...] = jnp.zeros_like(token)

    hbm = [pltpu.HBM(q.shape, BF16) for q in qs]
    outs = pl.pallas_call(
        body, name=name,
        out_shape=(pltpu.SemaphoreType.DMA((3 * n,)), pltpu.SemaphoreType.DMA((3 * n,)), *hbm, *hbm,
                   jax.ShapeDtypeStruct(LOSS_TILE, F32)),
        in_specs=[HBM_SPEC_STRICT] * (2 * n),
        out_specs=(SEM_SPEC, SEM_SPEC, *[HBM_SPEC_STRICT] * (2 * n), VMEM_SPEC),
        input_output_aliases={i: 2 + i for i in range(2 * n)},
        compiler_params=pltpu.CompilerParams(has_side_effects=SPLIT_EFFECT),
    )(*[pltpu.with_memory_space_constraint(q, pltpu.HBM) for q in qs],
      *[pltpu.with_memory_space_constraint(lax.empty(q.shape, BF16), pltpu.HBM) for q in qs])
    return outs[0], outs[1], outs[2:2 + n], outs[2 + n:2 + 2 * n], outs[-1]


def _chips_wait(send_sems, recv_sems, q_thru, land_thru, after, name):
    n = len(q_thru)

    def body(*refs):
        q_refs, land_refs = refs[:n], refs[n:2 * n]
        send_sems, recv_sems = refs[2 * n], refs[2 * n + 1]
        x, y, c, others = _place()
        me = 2 * x + y
        for w in range(n):
            for j, (px, py) in enumerate(others):
                cp = pltpu.make_async_remote_copy(
                    src_ref=q_refs[w].at[2 * px + py], dst_ref=land_refs[w].at[2 * px + py],
                    send_sem=send_sems.at[3 * w + j], recv_sem=recv_sems.at[3 * w + j], device_id=(px, py, c),
                    device_id_type=MESH)
                cp.wait_send()
                cp.wait_recv()

    outs = pl.pallas_call(
        body, name=name, out_shape=tuple(pltpu.HBM(a.shape, a.dtype) for a in (*q_thru, *land_thru)),
        in_specs=[HBM_SPEC_STRICT] * (2 * n) + [SEM_SPEC, SEM_SPEC, HBM_SPEC],
        out_specs=tuple([HBM_SPEC_STRICT] * (2 * n)), input_output_aliases={i: i for i in range(2 * n)},
        compiler_params=pltpu.CompilerParams(has_side_effects=SPLIT_EFFECT),
    )(*q_thru, *land_thru, send_sems, recv_sems, after)
    return list(outs[:n]), list(outs[n:])


def _sum_chips_tiled(q, r, idx, tile):
    rows, h = r.shape[1:]
    nt = h // tile

    def body(idx_ref, q_ref, r0_ref, r1_ref, r2_ref, g_ref):
        g_ref[...] = (q_ref[...].astype(F32) + r0_ref[...].astype(F32) + r1_ref[...].astype(F32)
                      + r2_ref[...].astype(F32))

    def slab(t):
        return pl.BlockSpec((None, rows, tile), lambda i, idx_ref: (idx_ref[t], 0, i))

    return pl.pallas_call(
        body, name="sum_chips_w_in",
        grid_spec=pltpu.PrefetchScalarGridSpec(
            num_scalar_prefetch=1, grid=(nt,), in_specs=[slab(0), slab(1), slab(2), slab(3)],
            out_specs=pl.BlockSpec((rows, tile), lambda i, idx_ref: (0, idx_ref[4] * nt + i))),
        out_shape=jax.ShapeDtypeStruct((rows, 2 * h), F32),
    )(idx, q, r, r, r)


def _sum_chips_small(qs, rs, idx, all_dtypes):
    n = len(rs)
    n_all = len(all_dtypes)

    def body(idx_ref, *refs):
        c = idx_ref[4]
        for w in range(n):
            q_ref, r_ref, g_ref = refs[w], refs[n + w], refs[2 * n + w]
            acc = q_ref[idx_ref[0]].astype(F32)
            for t in range(1, N_CHIPS):
                acc = acc + r_ref[idx_ref[t]].astype(F32)
            h = rs[w].shape[2]
            mine = pl.ds(pl.multiple_of(c * h, 128), h)
            g_ref[...] = jnp.zeros_like(g_ref)
            if w >= n - n_all:
                g_ref[idx_ref[0], :, mine] = acc.astype(g_ref.dtype)
            else:
                g_ref[:, mine] = acc

    shapes = [jax.ShapeDtypeStruct((r.shape[1], 2 * r.shape[2]), F32) for r in rs[:n - n_all]]
    shapes += [jax.ShapeDtypeStruct((N_CHIPS, r.shape[1], 2 * r.shape[2]), dt)
               for r, dt in zip(rs[n - n_all:], all_dtypes)]
    return pl.pallas_call(
        body, name="sum_chips_small",
        in_specs=[pl.BlockSpec(memory_space=pltpu.SMEM)] + [VMEM_SPEC] * (2 * n), out_specs=[VMEM_SPEC] * n,
        out_shape=shapes, compiler_params=pltpu.CompilerParams(vmem_limit_bytes=VMEM_LIMIT),
    )(idx, *qs, *rs)


def _share(shards, alls):
    n, na = len(shards), len(alls)
    total = n + na

    def body(*refs):
        g_refs, a_refs = refs[total:total + n], refs[total + n:2 * total]
        send_sems, recv_sems = refs[2 * total:]
        x, y, c, others = _place()
        me = 2 * x + y
        sibling = (x, y, 1 - c)

        def cols_of(w, half):
            h = shards[w].shape[1] // 2
            return g_refs[w].at[:, pl.ds(half * h, h)]

        def slab(a, chip, half):
            h = alls[a].shape[2] // 2
            return a_refs[a].at[chip, :, pl.ds(half * h, h)]

        def copy(src, dst, k, to):
            return pltpu.make_async_remote_copy(src_ref=src, dst_ref=dst, send_sem=send_sems.at[k],
                                                recv_sem=recv_sems.at[k], device_id=to, device_id_type=MESH)

        cps = [copy(cols_of(w, c), cols_of(w, c), w, sibling) for w in range(n)]
        for a in range(na):
            base = n + 7 * a
            cps.append(copy(slab(a, me, c), slab(a, me, c), base, sibling))
            for j, (px, py) in enumerate(others):
                cps.append(copy(slab(a, me, c), slab(a, me, c), base + 1 + j, (px, py, c)))
        for cp in cps:
            cp.start()
        fwd = []
        for a in range(na):
            base = n + 7 * a
            for j, (px, py) in enumerate(others):
                chip = 2 * px + py
                copy(slab(a, me, c), slab(a, chip, c), base + 1 + j, (px, py, c)).wait_recv()
                cp = copy(slab(a, chip, c), slab(a, chip, c), base + 4 + j, sibling)
                cp.start()
                fwd.append(cp)
        for a in range(na):
            base = n + 7 * a
            for j, (px, py) in enumerate(others):
                chip = 2 * px + py
                copy(slab(a, chip, c), slab(a, chip, 1 - c), base + 4 + j, sibling).wait_recv()
            copy(slab(a, me, c), slab(a, me, 1 - c), base, sibling).wait_recv()
        for w in range(n):
            copy(cols_of(w, c), cols_of(w, 1 - c), w, sibling).wait_recv()
        for cp in cps + fwd:
            cp.wait_send()

    nsem = n + 7 * na
    return pl.pallas_call(
        body, name="share", in_specs=[HBM_SPEC] * total, out_specs=[HBM_SPEC] * total,
        out_shape=[jax.ShapeDtypeStruct(a.shape, a.dtype) for a in (*shards, *alls)],
        input_output_aliases={i: i for i in range(total)},
        scratch_shapes=[pltpu.SemaphoreType.DMA((nsem,)), pltpu.SemaphoreType.DMA((nsem,))],
    )(*shards, *alls)


def _adamw(w, g, m, v):
    m2 = ADAM_B1 * m + (1.0 - ADAM_B1) * g
    v2 = ADAM_B2 * v + (1.0 - ADAM_B2) * (g * g)
    m_hat = m2 / (1.0 - ADAM_B1 ** ADAM_STEP)
    v_hat = v2 / (1.0 - ADAM_B2 ** ADAM_STEP)
    return -ADAM_LR * (m_hat / (jnp.sqrt(v_hat) + ADAM_EPS) + ADAM_WD * w), m2, v2


def _update_w_in(wt, gt, mt, vt, lat, owner, tile):
    nlat = lat.shape[0] // tile

    def body(owner_ref, w_ref, g_ref, m_ref, v_ref, lat_ref, g2_ref, d_ref, m2_ref, v2_ref):
        row = pl.program_id(0) * tile + lax.broadcasted_iota(jnp.int32, (tile, 1), 0)
        g = jnp.where((row < LAT_COLS) & (owner_ref[0] == 1), lat_ref[...].astype(F32), g_ref[...])
        g2_ref[...] = g
        d_ref[...], m2_ref[...], v2_ref[...] = _adamw(w_ref[...], g, m_ref[...], v_ref[...])

    spec = pl.BlockSpec((tile, wt.shape[1]), lambda i, o: (i, 0))
    return pl.pallas_call(
        body, name="update_w_in",
        grid_spec=pltpu.PrefetchScalarGridSpec(
            num_scalar_prefetch=1, grid=(wt.shape[0] // tile,),
            in_specs=[spec] * 4 + [pl.BlockSpec((tile, wt.shape[1]), lambda i, o: (jnp.minimum(i, nlat - 1), 0))],
            out_specs=[spec] * 4),
        out_shape=[jax.ShapeDtypeStruct(wt.shape, F32)] * 4,
        compiler_params=_params(("parallel",)),
    )(owner, wt, gt, mt, vt, lat)


def _update_small(ws, gs, ms, vs):
    n = len(ws)

    def body(*refs):
        for k in range(n):
            w_ref, g_ref, m_ref, v_ref = refs[k], refs[n + k], refs[2 * n + k], refs[3 * n + k]
            d, m2, v2 = _adamw(w_ref[...], g_ref[...], m_ref[...], v_ref[...])
            refs[4 * n + k][...] = d
            refs[5 * n + k][...] = m2
            refs[6 * n + k][...] = v2

    shapes = [jax.ShapeDtypeStruct(w.shape, F32) for w in ws]
    outs = pl.pallas_call(
        body, name="update_small", in_specs=[VMEM_SPEC] * (4 * n), out_specs=[VMEM_SPEC] * (3 * n),
        out_shape=shapes * 3,
        compiler_params=pltpu.CompilerParams(vmem_limit_bytes=VMEM_LIMIT),
    )(*ws, *gs, *ms, *vs)
    return outs[:n], outs[n:2 * n], outs[2 * n:]


SHARDED = ("w_in", "w_uq", "w_oa", "w_ob", "w_out")
REPLICATED = ("b_in", "g_q", "g_kv", "w_ukv", "sgu_ln_g", "sgu_ln_b", "w_s", "b_s", "ln_g", "ln_b")
ORDER = ("w_in", "b_in", "g_q", "w_uq", "g_kv", "w_ukv", "w_oa", "sgu_ln_g", "sgu_ln_b", "w_s", "b_s", "w_ob", "w_out",
         "ln_g", "ln_b")


def kernel(x, positions, w_in, b_in, g_q, w_uq, g_kv, w_ukv, w_oa, sgu_ln_g, sgu_ln_b, w_s, b_s, w_ob, w_out, ln_g, ln_b, loss_target, m_w_in, m_b_in, m_g_q, m_w_uq, m_g_kv, m_w_ukv, m_w_oa, m_sgu_ln_g, m_sgu_ln_b, m_w_s, m_b_s, m_w_ob, m_w_out, m_ln_g, m_ln_b, v_w_in, v_b_in, v_g_q, v_w_uq, v_g_kv, v_w_ukv, v_w_oa, v_sgu_ln_g, v_sgu_ln_b, v_w_s, v_b_s, v_w_ob, v_w_out, v_ln_g, v_ln_b):
    w = dict(w_in=w_in, b_in=b_in, g_q=g_q, w_uq=w_uq, g_kv=g_kv, w_ukv=w_ukv, w_oa=w_oa, sgu_ln_g=sgu_ln_g,
             sgu_ln_b=sgu_ln_b, w_s=w_s, b_s=b_s, w_ob=w_ob, w_out=w_out, ln_g=ln_g, ln_b=ln_b)
    m = dict(w_in=m_w_in, b_in=m_b_in, g_q=m_g_q, w_uq=m_w_uq, g_kv=m_g_kv, w_ukv=m_w_ukv, w_oa=m_w_oa,
             sgu_ln_g=m_sgu_ln_g, sgu_ln_b=m_sgu_ln_b, w_s=m_w_s, b_s=m_b_s, w_ob=m_w_ob, w_out=m_w_out, ln_g=m_ln_g,
             ln_b=m_ln_b)
    v = dict(w_in=v_w_in, b_in=v_b_in, g_q=v_g_q, w_uq=v_w_uq, g_kv=v_g_kv, w_ukv=v_w_ukv, w_oa=v_w_oa,
             sgu_ln_g=v_sgu_ln_g, sgu_ln_b=v_sgu_ln_b, w_s=v_w_s, b_s=v_b_s, w_ob=v_w_ob, w_out=v_w_out, ln_g=v_ln_g,
             ln_b=v_ln_b)
    w, m, v = ({n: a[0] for n, a in d.items()} for d in (w, m, v))
    c = lax.axis_index("c")

    wt_shard, mt_shard, vt_shard = (jnp.transpose(d["w_in"]) for d in (w, m, v))
    xi, yi = lax.axis_index("x"), lax.axis_index("y")
    me1 = (2 * xi + yi).reshape(1).astype(jnp.int32)
    first = _first_start(*_cast_first(wt_shard[:LAT_COLS], w["w_uq"].reshape(Q_RANK // 4, HEADS * QK_DIM), me1))
    bufs = _cast_own([wt_shard, w["w_oa"], w["w_ob"], w["w_out"]], me1, first[4])
    send0, recv0, bufs, token0 = _gather_start(bufs, first[4])
    g_lat, g_uq = _first_forward(*_first_wait(*first[:4], token0))
    st = _local_attention(x[0], positions[0], g_lat, w["b_in"], w["g_q"], g_uq.reshape(Q_RANK, HEADS, QK_DIM),
                          w["g_kv"], w["w_ukv"], token0)
    g_in, g_oa, g_ob, g_out = _gather_finish(_gather_wait(send0, recv0, bufs, st["o"]))
    wt = g_in.reshape(IN_W, D_MODEL)

    loss, early, st = _local_head(
        st, x[0], loss_target[0], wt, g_oa, w["sgu_ln_g"], w["sgu_ln_b"], w["w_s"], w["b_s"], g_ob,
        g_out.reshape(D_MODEL, D_MODEL), w["ln_g"], w["ln_b"])

    c1 = c.reshape(1).astype(jnp.int32)
    idx = jnp.stack([2 * xi + yi, 2 * (1 - xi) + yi, 2 * xi + (1 - yi), 2 * (1 - xi) + (1 - yi), c]).astype(jnp.int32)
    slabs = lambda a: a.reshape(N_CHIPS, IN_W // N_CHIPS, D_MODEL // 2)
    theirs = slabs(_dwt_early(st["dhmt"], st["dhgt"], st["xb"], 1 - c1, c1, "dwt_theirs"))
    parts1 = [theirs, early["w_oa"].astype(BF16), early["w_ob"].astype(BF16),
              early["w_out"].reshape(N_CHIPS, SLAB_W, D_MODEL).astype(BF16)]
    my_loss = lax.dynamic_update_slice(jnp.zeros((N_DEV,) + LOSS_TILE, F32), jnp.broadcast_to(loss, (1,) + LOSS_TILE),
                                       (4 * xi + 2 * yi + c, 0, 0))
    sems0 = _pairs_start(parts1, my_loss, 1)
    mine = slabs(_dwt_early(st["dhmt"], st["dhgt"], st["xb"], c1, sems0[5], "dwt_mine"))
    parts1, recv1, all_loss = _pairs_wait(*sems0[:5], mine, 1)
    pairs1 = [_add_pair_tiled(mine, recv1[0]), *_add_pair_small(parts1[1:], recv1[1:], c1, "add_pair_early")]
    sems1 = _chips_start(pairs1, "chips_start_early")
    st["delta"] = st["delta"] + sems1[4][0, 0]
    dq, dk, dv = _local_attn_bwd(st)
    dhl, late = _local_tail(st, dq, dk, dv, dv)

    grads = {**early, **late}
    rep = jnp.concatenate([_rows8(grads[n]) for n in REPLICATED], axis=0)
    rep = jnp.pad(rep, ((0, N_CHIPS * REP_ROWS - rep.shape[0]), (0, 0))).reshape(N_CHIPS, REP_ROWS, D_MODEL)
    parts2 = [late["w_uq"].reshape(N_CHIPS, Q_RANK // N_CHIPS, HEADS * QK_DIM).astype(BF16), rep.astype(BF16),
              late["w_lat"].reshape(N_CHIPS, LAT_ROWS_PAD // N_CHIPS, D_MODEL)]
    pairs2 = _add_pair_small(parts2, _exchange_pairs(parts2, "exchange_pairs_late"), c1, "add_pair_late")
    sems2 = _chips_start(pairs2, "chips_start_late")
    dx = _dx(st["dr"], st["dhg"], st["dhm"], dhl, st["wt"], sems2[4])
    pairs2, landed2 = _chips_wait(*sems2[:4], dx, "chips_wait_late")
    pairs1, landed1 = _chips_wait(*sems1[:4], landed2[0], "chips_wait_early")
    sums = [_sum_chips_tiled(pairs1[0], landed1[0], idx, 128),
            *_sum_chips_small([*pairs1[1:], *pairs2], [*landed1[1:], *landed2], idx, (F32, BF16))]
    *shards, g_rep, g_lat = _share(sums[:-2], sums[-2:])
    loss = jnp.sum(all_loss[:, 0, 0])

    red = {n: s.reshape(w[n].shape) for n, s in zip(("w_oa", "w_ob", "w_out", "w_uq"), shards[1:])}
    g_rep = g_rep.reshape(N_CHIPS * REP_ROWS, D_MODEL)
    off = 0
    for n in REPLICATED:
        rows = _rows8(w[n]).shape[0]
        red[n] = g_rep[off:off + rows].reshape(-1)[:w[n].size].reshape(w[n].shape)
        off += rows
    owner = (2 * xi + yi == 0).astype(jnp.int32).reshape(1)
    gt, dt, mt, vt2 = _update_w_in(wt_shard, shards[0], mt_shard, vt_shard,
                                   g_lat.reshape(LAT_ROWS_PAD, D_MODEL).astype(F32), owner, 232)
    red["w_in"] = jnp.transpose(gt)
    small = [n for n in ORDER if n != "w_in"]
    as2d = lambda a: a.reshape(-1, a.shape[-1])
    ds, ms, vs = _update_small([as2d(w[n]) for n in small], [as2d(red[n]) for n in small],
                               [as2d(m[n]) for n in small], [as2d(v[n]) for n in small])
    delta, new_m, new_v = {"w_in": jnp.transpose(dt)}, {"w_in": jnp.transpose(mt)}, {"w_in": jnp.transpose(vt2)}
    for i, n in enumerate(small):
        delta[n], new_m[n], new_v[n] = (a[i].reshape(w[n].shape) for a in (ds, ms, vs))

    lead = lambda a: a[None]
    return (loss, dx[None], *[lead(red[n]) for n in ORDER], *[lead(delta[n]) for n in ORDER],
            *[lead(new_m[n]) for n in ORDER], *[lead(new_v[n]) for n in ORDER])
```

```python
import functools
import math

import jax
import jax.numpy as jnp
from jax import lax
from jax.experimental import pallas as pl
from jax.experimental.pallas import tpu as pltpu

F32 = jnp.float32
BF16 = jnp.bfloat16

D_MODEL = 1024
HEADS = 8
Q_RANK = 384
KV_RANK = 128
NOPE = 64
ROPE = 32
V_DIM = 64
QK_DIM = NOPE + ROPE
HEAD_PAD = 128
MLA_W = HEADS * V_DIM
SGU_W = 512
GROUPS = 8
CHUNK = 128
IN_W = 4640
RMS_EPS = 1e-6
LN_EPS = 1e-5
ALPHA = 2.0 ** 0.25
ROPE_THETA = 10000.0
SCALE = QK_DIM ** -0.5

GATE_W = 2 * D_MODEL
MID_W = 4 * SGU_W
LAT_W = Q_RANK + KV_RANK + HEAD_PAD
PAD_W = GATE_W + MID_W + LAT_W
LAT_COLS = Q_RANK + KV_RANK + ROPE
ROW_GATE = LAT_COLS + MID_W
LAT_ROWS_PAD = 704
N_SLABS = 4
SLAB_W = D_MODEL // N_SLABS

ROW_TILE = 256
MATMUL_ROW_TILE = 512
ATT_TQ = 2048
ATT_TK = 256
ATT_BWD_TQ = 512
ATT_BWD_TK = 256
LOG2E = 1.4426950408889634
LN2 = 0.6931471805599453
Q_SCALE = SCALE * LOG2E
VMEM_LIMIT = 56 * 1024 * 1024

ADAM_LR = 0.001
ADAM_B1 = 0.9
ADAM_B2 = 0.999
ADAM_EPS = 1e-08
ADAM_WD = 0.01
ADAM_STEP = 10


def _dot(a, b):
    return jnp.dot(a, b, preferred_element_type=F32)


def _dot_nt(a, b):
    return lax.dot_general(a, b, (((1,), (1,)), ((), ())), preferred_element_type=F32)


def _dot_tn(a, b):
    return lax.dot_general(a, b, (((0,), (0,)), ((), ())), preferred_element_type=F32)


def _sigmoid(z):
    return 0.5 * jnp.tanh(0.5 * z) + 0.5


_GELU_C = math.sqrt(2.0 / math.pi)


def _gelu_and_grad(x):
    x2 = x * x
    t = jnp.tanh(_GELU_C * (x + 0.044715 * x * x2))
    g = 0.5 * x * (1.0 + t)
    dg = 0.5 * (1.0 + t) + 0.5 * x * (1.0 - t * t) * (_GELU_C * (1.0 + 3.0 * 0.044715 * x2))
    return g, dg


def _silu_and_grad(z):
    s = _sigmoid(z)
    return z * s, s * (1.0 + z * (1.0 - s))


def _rope(xb, c, sl, sh):
    return xb * c + pltpu.roll(xb, 112, 1) * sl + pltpu.roll(xb, 16, 1) * sh


def _rope_t(dy, c, sl, sh):
    return dy * c + pltpu.roll(dy * sl, 16, 1) + pltpu.roll(dy * sh, 112, 1)


def _params(sem=("arbitrary",)):
    return pltpu.CompilerParams(dimension_semantics=sem, vmem_limit_bytes=VMEM_LIMIT)


def _row_spec(tile, width):
    return pl.BlockSpec((tile, width), lambda i: (i, 0))


def _full_spec(shape):
    nd = len(shape)
    return pl.BlockSpec(shape, lambda i: (0,) * nd)


def _kpe_rows(wt_ref):
    z = lambda n: jnp.zeros((n, D_MODEL), BF16)
    return jnp.concatenate([z(NOPE), wt_ref[Q_RANK + KV_RANK:LAT_COLS, :], z(HEAD_PAD - QK_DIM)], axis=0)


def _fwd_rest(xb, wt, b_g, b_m):
    s = xb.shape[0]
    ts = MATMUL_ROW_TILE

    def body(xb_ref, wt_ref, bg_ref, bm_ref, hg_ref, hm_ref):
        xb_ = xb_ref[...]
        hg_ref[...] = _dot_nt(xb_, wt_ref[ROW_GATE:IN_W, :]) + bg_ref[...]
        hm_ref[...] = _dot_nt(xb_, wt_ref[LAT_COLS:ROW_GATE, :]) + bm_ref[...]

    return pl.pallas_call(
        body, name="fwd_rest", grid=(s // ts,),
        in_specs=[_row_spec(ts, D_MODEL), _full_spec(wt.shape), _full_spec(b_g.shape), _full_spec(b_m.shape)],
        out_specs=[_row_spec(ts, GATE_W), _row_spec(ts, MID_W)],
        out_shape=[jax.ShapeDtypeStruct((s, GATE_W), F32), jax.ShapeDtypeStruct((s, MID_W), F32)],
        compiler_params=_params(),
    )(xb, wt, b_g, b_m)


def _fwd_lat(x, wlat, b_l, g_q, wuq, g_kv, wk, wv, rc, rsl, rsh, after):
    s = x.shape[0]
    ts = ROW_TILE

    def body(x_ref, wt_ref, bl_ref, gq_ref, wuq_ref, gkv_ref, wk_ref, wv_ref, rc_ref, rsl_ref,
             rsh_ref, after_ref, hl_ref, q_ref, k_ref, v_ref, xb_ref, qt_ref, kt_ref, vt_ref):
        xb = x_ref[...].astype(BF16)
        xb_ref[...] = xb
        hl = jnp.concatenate([_dot_nt(xb, wt_ref[0:Q_RANK + KV_RANK, :]), _dot_nt(xb, _kpe_rows(wt_ref))],
                             axis=1) + bl_ref[...]
        hl_ref[...] = hl
        c, sl, sh = rc_ref[...], rsl_ref[...], rsh_ref[...]
        cq = hl[:, :Q_RANK]
        cqn = cq * lax.rsqrt(jnp.mean(cq * cq, axis=-1, keepdims=True) + RMS_EPS) * gq_ref[...]
        q = _dot(cqn.astype(BF16), wuq_ref[...])
        ckv = hl[:, Q_RANK:Q_RANK + KV_RANK]
        ckvn = (ckv * lax.rsqrt(jnp.mean(ckv * ckv, axis=-1, keepdims=True) + RMS_EPS) * gkv_ref[...]).astype(BF16)
        k = _dot(ckvn, wk_ref[...])
        vb = _dot(ckvn, wv_ref[...]).astype(BF16)
        v_ref[...] = vb
        vt_ref[...] = vb.T
        kpe = _rope(hl[:, Q_RANK + KV_RANK:], c, sl, sh)
        for hd in range(HEADS):
            lanes = slice(hd * HEAD_PAD, (hd + 1) * HEAD_PAD)
            qb = (_rope(q[:, lanes], c, sl, sh) * Q_SCALE).astype(BF16)
            kb = (k[:, lanes] + kpe).astype(BF16)
            q_ref[:, lanes] = qb
            k_ref[:, lanes] = kb
            qt_ref[lanes, :] = qb.T
            kt_ref[lanes, :] = kb.T

    qk_w = HEADS * HEAD_PAD
    col_spec = lambda rows: pl.BlockSpec((rows, ts), lambda i: (0, i))
    return pl.pallas_call(
        body, name="fwd_lat", grid=(s // ts,),
        in_specs=[_row_spec(ts, D_MODEL), _full_spec(wlat.shape),
                  _full_spec(b_l.shape), _full_spec(g_q.shape),
                  _full_spec(wuq.shape), _full_spec(g_kv.shape), _full_spec(wk.shape), _full_spec(wv.shape),
                  _row_spec(ts, HEAD_PAD), _row_spec(ts, HEAD_PAD), _row_spec(ts, HEAD_PAD),
                  pl.BlockSpec(memory_space=pl.ANY)],
        out_specs=[_row_spec(ts, LAT_W), _row_spec(ts, qk_w),
                   _row_spec(ts, qk_w), _row_spec(ts, MLA_W), _row_spec(ts, D_MODEL), col_spec(qk_w), col_spec(qk_w),
                   col_spec(MLA_W)],
        out_shape=[jax.ShapeDtypeStruct((s, LAT_W), F32), jax.ShapeDtypeStruct((s, qk_w), BF16),
                   jax.ShapeDtypeStruct((s, qk_w), BF16), jax.ShapeDtypeStruct((s, MLA_W), BF16),
                   jax.ShapeDtypeStruct((s, D_MODEL), BF16), jax.ShapeDtypeStruct((qk_w, s), BF16),
                   jax.ShapeDtypeStruct((qk_w, s), BF16), jax.ShapeDtypeStruct((MLA_W, s), BF16)],
        compiler_params=_params(),
    )(x, wlat, b_l, g_q, wuq, g_kv, wk, wv, rc, rsl, rsh, after)


def _attn_fwd(qt, k, vt):
    s = k.shape[0]
    tq, tk = ATT_TQ, ATT_TK
    r = tq // tk
    pairs = HEADS // 2

    def body(qt_ref, k_ref, vt_ref, o_ref, lse_ref):
        i = pl.program_id(1)
        qts = [qt_ref[hh * HEAD_PAD:(hh + 1) * HEAD_PAD, :] for hh in range(2)]

        def scores(j, lo):
            koff = pl.multiple_of(j * tk, tk)
            return tuple(_dot(k_ref[pl.ds(koff, tk), hh * HEAD_PAD:(hh + 1) * HEAD_PAD], qts[hh][:, lo:])
                         for hh in range(2))

        def weighted(j, ps):
            koff = pl.multiple_of(j * tk, tk)
            return tuple(_dot(vt_ref[hh * V_DIM:(hh + 1) * V_DIM, pl.ds(koff, tk)], ps[hh]) for hh in range(2))

        def from_lane(full, lo, part):
            return part if lo == 0 else jnp.concatenate([full[:, :lo], part], axis=1)

        def step(j, carry, diag, last):
            st, ps, stats = carry
            lo = 0 if diag is None else diag * tk
            lo_prev = 0 if not diag else (diag - 1) * tk
            st_next = None if last else scores(j + 1, 0 if diag is None else lo + tk)
            pvs = weighted(jnp.maximum(j - 1, 0), ps)
            new_ps, new_stats = [], []
            for hh in range(2):
                m, l, acc = stats[hh]
                s_ = st[hh]
                if diag is not None:
                    krow = lax.broadcasted_iota(jnp.int32, s_.shape, 0)
                    qcol = lax.broadcasted_iota(jnp.int32, s_.shape, 1)
                    s_ = jnp.where(krow <= qcol, s_, -jnp.inf)
                acc = from_lane(acc, lo_prev, acc[:, lo_prev:] + pvs[hh])
                m_old = m[:, lo:]
                m_new = jnp.maximum(m_old, jnp.max(s_, axis=0, keepdims=True))
                a = jnp.exp2(m_old - m_new)
                p = jnp.exp2(s_ - m_new)
                new_stats.append((from_lane(m, lo, m_new),
                                  from_lane(l, lo, a * l[:, lo:] + jnp.sum(p, axis=0, keepdims=True)),
                                  from_lane(acc, lo, a * acc[:, lo:])))
                new_ps.append(p.astype(BF16))
            return st_next, tuple(new_ps), tuple(new_stats)

        one = (jnp.full((1, tq), -jnp.inf, F32), jnp.zeros((1, tq), F32), jnp.zeros((V_DIM, tq), F32))
        zero_p = jnp.zeros((tk, tq), BF16)
        nfull = i * r
        carry = lax.fori_loop(0, nfull, functools.partial(step, diag=None, last=False),
                              (scores(0, 0), (zero_p, zero_p), (one, one)))
        for d in range(r):
            carry = step(nfull + d, carry, d, d == r - 1)
        _, ps, stats = carry
        pvs = weighted(nfull + r - 1, ps)
        lo = (r - 1) * tk
        ot = jnp.concatenate([from_lane(stats[hh][2], lo, stats[hh][2][:, lo:] + pvs[hh]) / stats[hh][1]
                              for hh in range(2)], axis=0)
        o_ref[...] = ot.T
        lse = [stats[hh][0] + jnp.log(stats[hh][1]) * LOG2E for hh in range(2)]
        lse_ref[...] = jnp.concatenate(lse + [jnp.zeros((6, tq), F32)], axis=0)

    return pl.pallas_call(
        body, name="attn_fwd", grid=(pairs, s // tq),
        in_specs=[pl.BlockSpec((2 * HEAD_PAD, tq), lambda p, i: (p, i)),
                  pl.BlockSpec((s, 2 * HEAD_PAD), lambda p, i: (0, p)),
                  pl.BlockSpec((2 * V_DIM, s), lambda p, i: (p, 0))],
        out_specs=[pl.BlockSpec((tq, 2 * V_DIM), lambda p, i: (i, p)),
                   pl.BlockSpec((None, 8, tq), lambda p, i: (p, 0, i))],
        out_shape=[jax.ShapeDtypeStruct((s, MLA_W), F32), jax.ShapeDtypeStruct((pairs, 8, s), F32)],
        compiler_params=_params(("arbitrary", "arbitrary")),
    )(qt, k, vt)


def _attn_bwd(q, qt, k, kt, v, do, dot, lse, delta):
    s = k.shape[0]
    tk = ATT_BWD_TK
    nk = s // tk
    pairs = HEADS // 2

    def body(q_ref, qt_ref, k_ref, kt_ref, v_ref, do_ref, dot_ref, lse_ref, dl_ref, dqt_ref, dk_ref, dv_ref):
        krow = lax.broadcasted_iota(jnp.int32, (tk, tk), 0)
        qcol = lax.broadcasted_iota(jnp.int32, (tk, tk), 1)
        lane = lax.broadcasted_iota(jnp.int32, (tk, 2 * V_DIM), 1)
        drow = lax.broadcasted_iota(jnp.int32, (2 * V_DIM, s), 0)
        dotb = dot_ref[...]
        dots = [jnp.where((drow < V_DIM) if hh == 0 else (drow >= V_DIM), dotb, jnp.zeros_like(dotb))
                for hh in range(2)]
        for j in range(nk):
            lo = j * tk
            vb = v_ref[lo:lo + tk, :]
            dob = do_ref[lo:, :]
            dvs = []
            for hh in range(2):
                rows = slice(hh * HEAD_PAD, (hh + 1) * HEAD_PAD)
                st = _dot(k_ref[lo:lo + tk, rows], qt_ref[rows, lo:])
                diag = jnp.where(krow <= qcol, st[:, :tk], -jnp.inf)
                st = diag if j == nk - 1 else jnp.concatenate([diag, st[:, tk:]], axis=1)
                p = jnp.exp2(st - lse_ref[hh:hh + 1, lo:])
                dpt = _dot(vb, dots[hh][:, lo:])
                dst = (p * (dpt - dl_ref[hh:hh + 1, lo:])).astype(BF16)
                dvs.append(_dot(p.astype(BF16), dob))
                dk_ref[lo:lo + tk, rows] = _dot(dst, q_ref[lo:, rows]) * LN2
                dqt = _dot(kt_ref[rows, lo:lo + tk], dst)
                if j == 0:
                    dqt_ref[rows, :] = dqt
                else:
                    dqt_ref[rows, lo:] += dqt
            dv_ref[lo:lo + tk, :] = jnp.where(lane < V_DIM, dvs[0], dvs[1])
        dqt_ref[...] = dqt_ref[...] * SCALE

    pair_rows = lambda w: pl.BlockSpec((s, w), lambda p: (0, p))
    pair_cols = lambda w: pl.BlockSpec((w, s), lambda p: (p, 0))
    stats = pl.BlockSpec((None, 8, s), lambda p: (p, 0, 0))
    return pl.pallas_call(
        body, name="attn_bwd", grid=(pairs,),
        in_specs=[pair_rows(2 * HEAD_PAD), pair_cols(2 * HEAD_PAD), pair_rows(2 * HEAD_PAD), pair_cols(2 * HEAD_PAD),
                  pair_rows(2 * V_DIM), pair_rows(2 * V_DIM), pair_cols(2 * V_DIM), stats, stats],
        out_specs=[pair_cols(2 * HEAD_PAD), pair_rows(2 * HEAD_PAD), pair_rows(2 * V_DIM)],
        out_shape=[jax.ShapeDtypeStruct((HEADS * HEAD_PAD, s), F32), jax.ShapeDtypeStruct((s, HEADS * HEAD_PAD), F32),
                   jax.ShapeDtypeStruct((s, MLA_W), F32)],
        compiler_params=_params(("arbitrary",)),
    )(q, qt, k, kt, v, do, dot, lse, delta)


def _attn_bwd_dynamic(q, qt, k, kt, v, do, dot, lse, delta):
    s = k.shape[0]
    tq, tk = ATT_BWD_TQ, ATT_BWD_TK
    r = tq // tk
    nq = s // tq
    nk = s // tk
    pairs = HEADS // 2

    def body(q_ref, qt_ref, k_ref, kt_ref, v_ref, do_ref, dot_ref, lse_ref, dl_ref, dqt_ref, dk_ref, dv_ref):
        j = pl.program_id(1)
        krow = lax.broadcasted_iota(jnp.int32, (tk, tq), 0)
        qcol = lax.broadcasted_iota(jnp.int32, (tk, tq), 1)
        lane = lax.broadcasted_iota(jnp.int32, (tk, 2 * V_DIM), 1)
        drow = lax.broadcasted_iota(jnp.int32, (2 * V_DIM, tq), 0)

        @pl.when(j == 0)
        def _():
            dqt_ref[...] = jnp.zeros_like(dqt_ref)

        koff = pl.multiple_of(j * tk, tk)
        vb = v_ref[pl.ds(koff, tk), :]
        kbs = [k_ref[pl.ds(koff, tk), hh * HEAD_PAD:(hh + 1) * HEAD_PAD] for hh in range(2)]
        ktbs = [kt_ref[hh * HEAD_PAD:(hh + 1) * HEAD_PAD, pl.ds(koff, tk)] for hh in range(2)]
        i0 = j // r

        def front(i):
            qoff = pl.multiple_of(i * tq, tq)
            dotb = dot_ref[:, pl.ds(qoff, tq)]
            out = []
            for hh in range(2):
                mine = (drow < V_DIM) if hh == 0 else (drow >= V_DIM)
                st = _dot(kbs[hh], qt_ref[hh * HEAD_PAD:(hh + 1) * HEAD_PAD, pl.ds(qoff, tq)])
                out.append((st, _dot(vb, jnp.where(mine, dotb, jnp.zeros_like(dotb)))))
            return tuple(out)

        def middle(i, tiles, diag):
            qoff = pl.multiple_of(i * tq, tq)
            out = []
            for hh in range(2):
                st, dpt = tiles[hh]
                if diag:
                    st = jnp.where(krow + (j - i0 * r) * tk <= qcol, st, -jnp.inf)
                p = jnp.exp2(st - lse_ref[hh:hh + 1, pl.ds(qoff, tq)])
                out.append((p.astype(BF16), (p * (dpt - dl_ref[hh:hh + 1, pl.ds(qoff, tq)])).astype(BF16)))
            return tuple(out)

        def back(i, pd, accs):
            qoff = pl.multiple_of(i * tq, tq)
            dob = do_ref[pl.ds(qoff, tq), :]
            out = []
            for hh in range(2):
                rows = slice(hh * HEAD_PAD, (hh + 1) * HEAD_PAD)
                p, dst = pd[hh]
                dk_acc, dv_acc = accs[hh]
                dv_acc = dv_acc + _dot(p, dob)
                dk_acc = dk_acc + _dot(dst, q_ref[pl.ds(qoff, tq), rows])
                dqt_ref[rows, pl.ds(qoff, tq)] += _dot(ktbs[hh], dst)
                out.append((dk_acc, dv_acc))
            return tuple(out)

        def step(i, accs, diag):
            return back(i, middle(i, front(i), diag), accs)

        zero_acc = (jnp.zeros((tk, HEAD_PAD), F32), jnp.zeros((tk, 2 * V_DIM), F32))
        accs = step(i0, (zero_acc, zero_acc), True)
        accs = lax.fori_loop(i0 + 1, nq, functools.partial(step, diag=False), accs)
        for hh in range(2):
            dk_ref[:, hh * HEAD_PAD:(hh + 1) * HEAD_PAD] = accs[hh][0] * LN2
        dv_ref[...] = jnp.where(lane < V_DIM, accs[0][1], accs[1][1])

        @pl.when(j == nk - 1)
        def _():
            dqt_ref[...] = dqt_ref[...] * SCALE

    pair_rows = lambda w: pl.BlockSpec((s, w), lambda p, j: (0, p))
    pair_cols = lambda w: pl.BlockSpec((w, s), lambda p, j: (p, 0))
    stats = pl.BlockSpec((None, 8, s), lambda p, j: (p, 0, 0))
    return pl.pallas_call(
        body, name="attn_bwd", grid=(pairs, nk),
        in_specs=[pair_rows(2 * HEAD_PAD), pair_cols(2 * HEAD_PAD), pair_rows(2 * HEAD_PAD), pair_cols(2 * HEAD_PAD),
                  pair_rows(2 * V_DIM), pair_rows(2 * V_DIM), pair_cols(2 * V_DIM), stats, stats],
        out_specs=[pair_cols(2 * HEAD_PAD),
                   pl.BlockSpec((tk, 2 * HEAD_PAD), lambda p, j: (j, p)),
                   pl.BlockSpec((tk, 2 * V_DIM), lambda p, j: (j, p))],
        out_shape=[jax.ShapeDtypeStruct((HEADS * HEAD_PAD, s), F32), jax.ShapeDtypeStruct((s, HEADS * HEAD_PAD), F32),
                   jax.ShapeDtypeStruct((s, MLA_W), F32)],
        compiler_params=_params(("arbitrary", "arbitrary")),
    )(q, qt, k, kt, v, do, dot, lse, delta)


def _split3(a):
    hi = a.astype(BF16)
    r1 = a - hi.astype(F32)
    mid = r1.astype(BF16)
    lo = (r1 - mid.astype(F32)).astype(BF16)
    return hi, mid, lo


def _mid(x, tgt, o, hm, hg, woa, wob, wout, ln_g, ln_b, sg_g, sg_b, w_s, bsb):
    s = x.shape[0]
    ts = ROW_TILE
    nsteps = s // ts
    nch = ts // CHUNK
    npair = GROUPS // 2

    def body(x_ref, t_ref, o_ref, hm_ref, hg_ref, woa_ref, wob_ref, wout_ref, lng_ref, lnb_ref, sgg_ref, sgb_ref,
             ws_ref, bsb_ref,
             dr_ref, dhg_ref, dhm_ref, do_ref, dot_ref, dl_ref, dhgt_ref, dhmt_ref,
             dwout_ref, dwoa_ref, dwob_ref, dws_ref, dbs_ref, dlng_ref, dlnb_ref, dsgg_ref, dsgb_ref, loss_ref,
             dbg_ref, dbm_ref, dbacc_ref):
        i = pl.program_id(0)

        @pl.when(i == 0)
        def _():
            for r in (dwout_ref, dwoa_ref, dwob_ref, dws_ref, dlng_ref, dlnb_ref, dsgg_ref, dsgb_ref, loss_ref,
                      dbg_ref, dbm_ref, dbacc_ref):
                r[...] = jnp.zeros_like(r)

        def emit(ref, tref, bref, lo, val):
            vb = val.astype(BF16)
            n = val.shape[1]
            ref[:, lo:lo + n] = vb
            tref[lo:lo + n, :] = vb.T
            bref[:, lo:lo + n] += jnp.sum(val, axis=0, keepdims=True)

        lane = lax.broadcasted_iota(jnp.int32, (CHUNK, CHUNK), 1)
        left = lane < V_DIM
        tril = lax.broadcasted_iota(jnp.int32, (CHUNK, CHUNK), 0) >= lane
        ms = [jnp.where(tril, ws_ref[g], 0.0).astype(BF16) for g in range(GROUPS)]

        z_a = hm_ref[:, 0:SGU_W]
        u = hm_ref[:, SGU_W:2 * SGU_W]
        v = hm_ref[:, 2 * SGU_W:3 * SGU_W]
        z_b = hm_ref[:, 3 * SGU_W:4 * SGU_W]
        o = o_ref[...]
        sa, dsa = _silu_and_grad(z_a)
        y_a = (o * sa).astype(BF16)
        gu, dgu = _gelu_and_grad(u)
        gv, dgv = _gelu_and_grad(v)
        mu = jnp.mean(gv, axis=-1, keepdims=True)
        vc = gv - mu
        rstd_v = lax.rsqrt(jnp.mean(vc * vc, axis=-1, keepdims=True) + LN_EPS)
        vhat = vc * rstd_v
        vn = (vhat * sgg_ref[...] + sgb_ref[...]).astype(BF16)
        rows = []
        for c in range(nch):
            blocks = []
            for p in range(npair):
                blk = vn[c * CHUNK:(c + 1) * CHUNK, p * CHUNK:(p + 1) * CHUNK]
                blocks.append(jnp.where(left, _dot(ms[2 * p], blk), _dot(ms[2 * p + 1], blk)))
            rows.append(jnp.concatenate(blocks, axis=1) + bsb_ref[...])
        mixed = jnp.concatenate(rows, axis=0)
        sgu = gu * mixed
        sb, dsb = _silu_and_grad(z_b)
        y_b = (sgu * sb).astype(BF16)
        pa = jnp.concatenate([_dot(y_a, woa_ref[k]) for k in range(N_SLABS)], axis=1)
        pb = jnp.concatenate([_dot(y_b, wob_ref[k]) for k in range(N_SLABS)], axis=1)
        sga = _sigmoid(hg_ref[:, :D_MODEL])
        sgb = _sigmoid(hg_ref[:, D_MODEL:])
        m2 = (sga * pa + sgb * pb).astype(BF16)
        r = ALPHA * x_ref[...] + _dot(m2, wout_ref[...])
        rmu = jnp.mean(r, axis=-1, keepdims=True)
        rc = r - rmu
        rstd = lax.rsqrt(jnp.mean(rc * rc, axis=-1, keepdims=True) + LN_EPS)
        xhat = rc * rstd
        y = xhat * lng_ref[...] + lnb_ref[...]
        err = y - t_ref[...]
        loss_ref[...] += jnp.full(loss_ref.shape, 0.5 / D_MODEL, F32) * jnp.sum(err * err)

        dy = err * (1.0 / D_MODEL)
        dlng_ref[...] += jnp.sum(dy * xhat, axis=0, keepdims=True)
        dlnb_ref[...] += jnp.sum(dy, axis=0, keepdims=True)
        dxh = dy * lng_ref[...]
        dr = rstd * (dxh - jnp.mean(dxh, axis=-1, keepdims=True) - xhat * jnp.mean(dxh * xhat, axis=-1, keepdims=True))
        dr_ref[...] = dr
        drb = dr.astype(BF16)
        dwout_ref[...] += _dot_tn(m2, drb)
        dm2 = _dot_nt(drb, wout_ref[...])
        emit(dhg_ref, dhgt_ref, dbg_ref, 0, dm2 * pa * sga * (1.0 - sga))
        emit(dhg_ref, dhgt_ref, dbg_ref, D_MODEL, dm2 * pb * sgb * (1.0 - sgb))
        dpa = (dm2 * sga).astype(BF16)
        dpb = (dm2 * sgb).astype(BF16)
        dy_a = jnp.zeros((ts, MLA_W), F32)
        dy_b = jnp.zeros((ts, SGU_W), F32)
        y_at, y_bt = y_a.T, y_b.T
        for k in range(N_SLABS):
            cols = slice(k * SLAB_W, (k + 1) * SLAB_W)
            dwoa_ref[k] += _dot(y_at, dpa[:, cols])
            dwob_ref[k] += _dot(y_bt, dpb[:, cols])
            dy_a = dy_a + _dot_nt(dpa[:, cols], woa_ref[k])
            dy_b = dy_b + _dot_nt(dpb[:, cols], wob_ref[k])
        dob = (dy_a * sa).astype(BF16)
        do_ref[...] = dob
        dot_ref[...] = dob.T
        head = (lax.broadcasted_iota(jnp.int32, (HEADS, MLA_W), 1) // V_DIM
                == lax.broadcasted_iota(jnp.int32, (HEADS, MLA_W), 0)).astype(BF16)
        dl_ref[...] = sum(_dot_nt(head, term) for term in _split3(dob.astype(F32) * o))
        emit(dhm_ref, dhmt_ref, dbm_ref, 0, dy_a * o * dsa)
        dsg = dy_b * sb
        emit(dhm_ref, dhmt_ref, dbm_ref, 3 * SGU_W, dy_b * sgu * dsb)
        emit(dhm_ref, dhmt_ref, dbm_ref, SGU_W, dsg * mixed * dgu)
        dmixed = dsg * gu
        dvn_rows = []
        dbs_sum = jnp.zeros((CHUNK, SGU_W), F32)
        for c in range(nch):
            dm_c = dmixed[c * CHUNK:(c + 1) * CHUNK, :]
            dbs_sum = dbs_sum + dm_c
            blocks = []
            for p in range(npair):
                dmb = dm_c[:, p * CHUNK:(p + 1) * CHUNK].astype(BF16)
                blk = vn[c * CHUNK:(c + 1) * CHUNK, p * CHUNK:(p + 1) * CHUNK]
                blocks.append(jnp.where(left, _dot_tn(ms[2 * p], dmb), _dot_tn(ms[2 * p + 1], dmb)))
                zero = jnp.zeros_like(dmb)
                dws_ref[2 * p] += jnp.where(tril, _dot_nt(jnp.where(left, dmb, zero), blk), 0.0)
                dws_ref[2 * p + 1] += jnp.where(tril, _dot_nt(jnp.where(left, zero, dmb), blk), 0.0)
            dvn_rows.append(jnp.concatenate(blocks, axis=1))
        dbacc_ref[...] += dbs_sum
        dvn = jnp.concatenate(dvn_rows, axis=0)
        dsgg_ref[...] += jnp.sum(dvn * vhat, axis=0, keepdims=True)
        dsgb_ref[...] += jnp.sum(dvn, axis=0, keepdims=True)
        dvh = dvn * sgg_ref[...]
        dgv_in = rstd_v * (dvh - jnp.mean(dvh, axis=-1, keepdims=True)
                           - vhat * jnp.mean(dvh * vhat, axis=-1, keepdims=True))
        emit(dhm_ref, dhmt_ref, dbm_ref, 2 * SGU_W, dgv_in * dgv)

        @pl.when(i == nsteps - 1)
        def _():
            grp = (lax.broadcasted_iota(jnp.int32, (SGU_W, CHUNK), 0) // V_DIM
                   == lax.broadcasted_iota(jnp.int32, (SGU_W, CHUNK), 1)).astype(BF16)
            hi, mid, lo = _split3(dbacc_ref[...])
            dbs_ref[...] = _dot(hi, grp) + _dot(mid, grp) + _dot(lo, grp)

    acc_shapes = [(D_MODEL, D_MODEL), woa.shape, wob.shape, (GROUPS, CHUNK, CHUNK), (CHUNK, CHUNK),
                  (1, D_MODEL), (1, D_MODEL), (1, SGU_W), (1, SGU_W), (1, 128), (1, GATE_W), (1, MID_W)]
    col_spec = lambda rows: pl.BlockSpec((rows, ts), lambda i: (0, i))
    return pl.pallas_call(
        body, name="mid", grid=(nsteps,),
        in_specs=[_row_spec(ts, D_MODEL), _row_spec(ts, D_MODEL), _row_spec(ts, MLA_W), _row_spec(ts, MID_W),
                  _row_spec(ts, GATE_W), _full_spec(woa.shape), _full_spec(wob.shape), _full_spec(wout.shape),
                  _full_spec(ln_g.shape), _full_spec(ln_b.shape), _full_spec(sg_g.shape), _full_spec(sg_b.shape),
                  _full_spec(w_s.shape), _full_spec(bsb.shape)],
        out_specs=[_row_spec(ts, D_MODEL), _row_spec(ts, GATE_W), _row_spec(ts, MID_W), _row_spec(ts, MLA_W),
                   col_spec(MLA_W), col_spec(HEADS), col_spec(GATE_W), col_spec(MID_W)]
        + [_full_spec(sh) for sh in acc_shapes],
        out_shape=[jax.ShapeDtypeStruct((s, D_MODEL), F32), jax.ShapeDtypeStruct((s, GATE_W), BF16),
                   jax.ShapeDtypeStruct((s, MID_W), BF16), jax.ShapeDtypeStruct((s, MLA_W), BF16),
                   jax.ShapeDtypeStruct((MLA_W, s), BF16), jax.ShapeDtypeStruct((HEADS, s), F32),
                   jax.ShapeDtypeStruct((GATE_W, s), BF16), jax.ShapeDtypeStruct((MID_W, s), BF16)]
        + [jax.ShapeDtypeStruct(sh, F32) for sh in acc_shapes],
        scratch_shapes=[pltpu.VMEM((CHUNK, SGU_W), F32)],
        compiler_params=_params(),
    )(x, tgt, o, hm, hg, woa, wob, wout, ln_g, ln_b, sg_g, sg_b, w_s, bsb)


def _lat_bwd(dq, dk, dv, hl, rc, rsl, rsh, g_q, g_kv, wuq, wk, wv, after):
    s = dk.shape[0]
    ts = ROW_TILE
    qk_w = HEADS * HEAD_PAD

    def body(dq_ref, dk_ref, dv_ref, hl_ref, rc_ref, rsl_ref, rsh_ref, gq_ref, gkv_ref, wuq_ref, wk_ref, wv_ref,
             after_ref, dhl_ref, dhlt_ref, dwuq_ref, dwk_ref, dwv_ref, dgq_ref, dgkv_ref, dbl_ref):
        i = pl.program_id(0)

        @pl.when(i == 0)
        def _():
            for r in (dwuq_ref, dwk_ref, dwv_ref, dgq_ref, dgkv_ref, dbl_ref):
                r[...] = jnp.zeros_like(r)

        def emit(lo, val):
            vb = val.astype(BF16)
            n = val.shape[1]
            dhl_ref[:, lo:lo + n] = vb
            dhlt_ref[lo:lo + n, :] = vb.T
            dbl_ref[:, lo:lo + n] += jnp.sum(val, axis=0, keepdims=True)

        c, sl, sh = rc_ref[...], rsl_ref[...], rsh_ref[...]
        lane = lax.broadcasted_iota(jnp.int32, (ts, HEAD_PAD), 1)
        pe = (lane >= NOPE) & (lane < QK_DIM)
        dkpe = jnp.zeros((ts, HEAD_PAD), F32)
        dqu = []
        for hd in range(HEADS):
            lanes = slice(hd * HEAD_PAD, (hd + 1) * HEAD_PAD)
            dqu.append(_rope_t(dq_ref[lanes, :].T, c, sl, sh).astype(BF16))
            dkpe = dkpe + dk_ref[:, lanes]
        dqu = jnp.concatenate(dqu, axis=1)
        dkpe = _rope_t(jnp.where(pe, dkpe, 0.0), c, sl, sh)

        cq = hl_ref[:, :Q_RANK]
        rq = lax.rsqrt(jnp.mean(cq * cq, axis=-1, keepdims=True) + RMS_EPS)
        cqh = cq * rq
        cqn = (cqh * gq_ref[...]).astype(BF16)
        dwuq_ref[...] += _dot_tn(cqn, dqu)
        dcqn = _dot_nt(dqu, wuq_ref[...])
        dgq_ref[...] += jnp.sum(dcqn * cqh, axis=0, keepdims=True)
        dch = dcqn * gq_ref[...]
        emit(0, rq * (dch - cqh * jnp.mean(dch * cqh, axis=-1, keepdims=True)))

        ckv = hl_ref[:, Q_RANK:Q_RANK + KV_RANK]
        rk = lax.rsqrt(jnp.mean(ckv * ckv, axis=-1, keepdims=True) + RMS_EPS)
        ckh = ckv * rk
        ckn = (ckh * gkv_ref[...]).astype(BF16)
        dkb = dk_ref[...].astype(BF16)
        dvb = dv_ref[...].astype(BF16)
        dwk_ref[...] += _dot_tn(ckn, dkb)
        dwv_ref[...] += _dot_tn(ckn, dvb)
        dckn = _dot_nt(dkb, wk_ref[...]) + _dot_nt(dvb, wv_ref[...])
        dgkv_ref[...] += jnp.sum(dckn * ckh, axis=0, keepdims=True)
        dkh = dckn * gkv_ref[...]
        emit(Q_RANK, rk * (dkh - ckh * jnp.mean(dkh * ckh, axis=-1, keepdims=True)))
        emit(Q_RANK + KV_RANK, dkpe)

    acc_shapes = [wuq.shape, wk.shape, wv.shape, g_q.shape, g_kv.shape, (1, LAT_W)]
    return pl.pallas_call(
        body, name="lat_bwd", grid=(s // ts,),
        in_specs=[pl.BlockSpec((qk_w, ts), lambda i: (0, i)), _row_spec(ts, qk_w), _row_spec(ts, MLA_W),
                  _row_spec(ts, LAT_W), _row_spec(ts, HEAD_PAD), _row_spec(ts, HEAD_PAD), _row_spec(ts, HEAD_PAD),
                  _full_spec(g_q.shape), _full_spec(g_kv.shape), _full_spec(wuq.shape), _full_spec(wk.shape),
                  _full_spec(wv.shape), pl.BlockSpec(memory_space=pl.ANY)],
        out_specs=[_row_spec(ts, LAT_W), pl.BlockSpec((LAT_W, ts), lambda i: (0, i))]
        + [_full_spec(sh) for sh in acc_shapes],
        out_shape=[jax.ShapeDtypeStruct((s, LAT_W), BF16), jax.ShapeDtypeStruct((LAT_W, s), BF16)]
        + [jax.ShapeDtypeStruct(sh, F32) for sh in acc_shapes],
        compiler_params=_params(),
    )(dq, dk, dv, hl, rc, rsl, rsh, g_q, g_kv, wuq, wk, wv, after)


def _dx(dr, dhg, dhm, dhl, wt, after):
    s = dr.shape[0]
    ts = MATMUL_ROW_TILE

    def body(dr_ref, dhg_ref, dhm_ref, dhl_ref, wt_ref, after_ref, dx_ref):
        dx_ref[...] = (ALPHA * dr_ref[...]
                       + _dot(dhg_ref[...], wt_ref[ROW_GATE:IN_W, :])
                       + _dot(dhm_ref[...], wt_ref[LAT_COLS:ROW_GATE, :])
                       + _dot(dhl_ref[:, 0:Q_RANK + KV_RANK], wt_ref[0:Q_RANK + KV_RANK, :])
                       + _dot(dhl_ref[:, Q_RANK + KV_RANK:], _kpe_rows(wt_ref)))

    return pl.pallas_call(
        body, name="dx", grid=(s // ts,),
        in_specs=[_row_spec(ts, D_MODEL), _row_spec(ts, GATE_W), _row_spec(ts, MID_W), _row_spec(ts, LAT_W),
                  _full_spec(wt.shape), pl.BlockSpec(memory_space=pl.ANY)],
        out_specs=_row_spec(ts, D_MODEL),
        out_shape=jax.ShapeDtypeStruct((s, D_MODEL), F32),
        compiler_params=_params(),
    )(dr, dhg, dhm, dhl, wt, after)


def _dwt_early(dhmt, dhgt, xb, col, after, name):
    tn = 512
    nm, ng = MID_W // tn, GATE_W // tn
    s = dhmt.shape[1]
    hc = D_MODEL // 2

    ks = s // 2

    def body(col_ref, dma_ref, dmb_ref, dga_ref, dgb_ref, xba_ref, xbb_ref, after_ref, dw_ref):
        i = pl.program_id(0)

        @pl.when(i < nm)
        def _():
            dw_ref[...] = (_dot(dma_ref[...], xba_ref[...]) + _dot(dmb_ref[...], xbb_ref[...])).astype(BF16)

        @pl.when(i >= nm)
        def _():
            dw_ref[...] = (_dot(dga_ref[...], xba_ref[...]) + _dot(dgb_ref[...], xbb_ref[...])).astype(BF16)

    def dh_spec(first, part):
        if first:
            return pl.BlockSpec((tn, ks), lambda i, col_ref: (jnp.minimum(i, nm - 1), part))
        return pl.BlockSpec((tn, ks), lambda i, col_ref: (jnp.maximum(i - nm, 0), part))

    rows = pl.pallas_call(
        body, name=name,
        grid_spec=pltpu.PrefetchScalarGridSpec(
            num_scalar_prefetch=1, grid=(nm + ng,),
            in_specs=[dh_spec(True, 0), dh_spec(True, 1), dh_spec(False, 0), dh_spec(False, 1),
                      pl.BlockSpec((ks, hc), lambda i, col_ref: (0, col_ref[0])),
                      pl.BlockSpec((ks, hc), lambda i, col_ref: (1, col_ref[0])),
                      pl.BlockSpec(memory_space=pl.ANY)],
            out_specs=pl.BlockSpec((pl.Element(tn), pl.Element(hc)),
                                   lambda i, col_ref: (pl.multiple_of(LAT_COLS + i * tn, 32), 0))),
        out_shape=jax.ShapeDtypeStruct((IN_W, hc), BF16),
        compiler_params=_params(),
    )(col, dhmt, dhmt, dhgt, dhgt, xb, xb, after)

    def zero(buf_ref, out_ref):
        out_ref[...] = jnp.zeros_like(out_ref)

    return pl.pallas_call(
        zero, name=name + "_zero_lat", grid=(1,), in_specs=[pl.BlockSpec(memory_space=pl.ANY)],
        out_specs=pl.BlockSpec((LAT_COLS, hc), lambda i: (0, 0)),
        out_shape=jax.ShapeDtypeStruct((IN_W, hc), BF16), input_output_aliases={0: 0},
    )(rows)


def _dwt_lat(dhlt, xb):
    n, s = dhlt.shape

    def body(dht_ref, xb_ref, dw_ref):
        dw = _dot(dht_ref[...], xb_ref[...]).astype(BF16)
        kpe = Q_RANK + KV_RANK + NOPE
        dw_ref[0:Q_RANK + KV_RANK, :] = dw[0:Q_RANK + KV_RANK]
        dw_ref[Q_RANK + KV_RANK:LAT_COLS, :] = dw[kpe:kpe + ROPE]
        dw_ref[LAT_COLS:, :] = jnp.zeros((LAT_ROWS_PAD - LAT_COLS, D_MODEL), BF16)

    return pl.pallas_call(
        body, name="dwt_lat", in_specs=[VMEM_SPEC, VMEM_SPEC], out_specs=VMEM_SPEC,
        out_shape=jax.ShapeDtypeStruct((LAT_ROWS_PAD, D_MODEL), BF16),
        compiler_params=pltpu.CompilerParams(vmem_limit_bytes=VMEM_LIMIT),
    )(dhlt, xb)


def _split_bias(b):
    z = lambda n: jnp.zeros((n,), b.dtype)
    lat = jnp.concatenate([b[:Q_RANK + KV_RANK], z(NOPE), b[Q_RANK + KV_RANK:LAT_COLS], z(HEAD_PAD - QK_DIM)])
    return b[None, ROW_GATE:], b[None, LAT_COLS:ROW_GATE], lat[None, :]


def _join_bias(g, m, l):
    kpe = Q_RANK + KV_RANK + NOPE
    return jnp.concatenate([l[0, :Q_RANK + KV_RANK], l[0, kpe:kpe + ROPE], m[0], g[0]])


def _rope_tables(positions):
    half = ROPE // 2
    inv_freq = ROPE_THETA ** (-jnp.arange(0, ROPE, 2, dtype=F32) / ROPE)
    ang = positions.astype(F32)[:, None] * inv_freq
    cos, sin = jnp.cos(ang), jnp.sin(ang)
    n = positions.shape[0]
    one, zero = jnp.ones((n, NOPE), F32), jnp.zeros((n, half), F32)
    tail1, tail0 = jnp.ones((n, HEAD_PAD - QK_DIM), F32), jnp.zeros((n, HEAD_PAD - QK_DIM), F32)
    z64 = jnp.zeros((n, NOPE), F32)
    rc = jnp.concatenate([one, cos, cos, tail1], axis=1)
    rsl = jnp.concatenate([z64, -sin, zero, tail0], axis=1)
    rsh = jnp.concatenate([z64, zero, sin, tail0], axis=1)
    return rc, rsl, rsh


def _local_attention(x, positions, wlat, b_in, g_q, w_uq, g_kv, w_ukv, after):
    rc, rsl, rsh = _rope_tables(positions)
    b_g, b_m, b_l = _split_bias(b_in)
    wuq = jnp.pad(w_uq, ((0, 0), (0, 0), (0, HEAD_PAD - QK_DIM))).reshape(Q_RANK, HEADS * HEAD_PAD).astype(BF16)
    wk = jnp.pad(w_ukv[:, :, :NOPE], ((0, 0), (0, 0), (0, HEAD_PAD - NOPE))).reshape(KV_RANK, HEADS * HEAD_PAD).astype(BF16)
    wv = w_ukv[:, :, NOPE:].reshape(KV_RANK, MLA_W).astype(BF16)
    gq2, gkv2 = g_q[None, :], g_kv[None, :]
    hl, q, k, v, xb, qt, kt, vt = _fwd_lat(x, wlat, b_l, gq2, wuq, gkv2, wk, wv, rc, rsl, rsh, after)
    o, lse = _attn_fwd(qt, k, vt)
    return dict(q=q, qt=qt, k=k, kt=kt, v=v, o=o, lse=lse, hl=hl, rc=rc, rsl=rsl, rsh=rsh, gq2=gq2, gkv2=gkv2,
                wuq=wuq, wk=wk, wv=wv, xb=xb, b_g=b_g, b_m=b_m)


def _local_head(st, x, tgt, wt, w_oa, sg_g, sg_b, w_s, b_s, w_ob, w_out, ln_g, ln_b):
    q, qt, k, kt, v, o, lse, hl, xb = (st[n] for n in ("q", "qt", "k", "kt", "v", "o", "lse", "hl", "xb"))
    rc, rsl, rsh, gq2, gkv2, wuq, wk, wv = (st[n] for n in ("rc", "rsl", "rsh", "gq2", "gkv2", "wuq", "wk", "wv"))
    bsb = jnp.repeat(b_s.T, V_DIM, axis=1)
    hg, hm = _fwd_rest(xb, wt, st["b_g"], st["b_m"])
    (dr, dhg, dhm, do, dot, delta, dhgt, dhmt, dwout, dwoa, dwob, dws, dbs, dlng, dlnb, dsgg, dsgb, loss, dbg,
     dbm) = _mid(x, tgt, o, hm, hg, w_oa, w_ob, w_out, ln_g[None, :], ln_b[None, :], sg_g[None, :], sg_b[None, :],
                 w_s, bsb)
    delta = jnp.pad(delta.reshape(HEADS // 2, 2, -1), ((0, 0), (0, 6), (0, 0)))
    early = {
        "w_oa": dwoa, "sgu_ln_g": dsgg[0], "sgu_ln_b": dsgb[0], "w_s": dws, "b_s": dbs[:, :GROUPS].T,
        "w_ob": dwob, "w_out": dwout, "ln_g": dlng[0], "ln_b": dlnb[0],
    }
    state = dict(q=q, qt=qt, k=k, kt=kt, v=v, do=do, dot=dot, lse=lse, delta=delta, hl=hl, rc=rc, rsl=rsl, rsh=rsh,
                 gq2=gq2, gkv2=gkv2, wuq=wuq, wk=wk, wv=wv, dr=dr, dhg=dhg, dhm=dhm, wt=wt, xb=xb, dbg=dbg, dbm=dbm,
                 dhgt=dhgt, dhmt=dhmt)
    return loss, early, state


def _local_attn_bwd(st):
    return _attn_bwd(st["q"], st["qt"], st["k"], st["kt"], st["v"], st["do"], st["dot"], st["lse"], st["delta"])


def _local_tail(st, dq, dk, dv, after):
    dhl, dhlt, dwuq, dwk, dwv, dgq, dgkv, dbl = _lat_bwd(dq, dk, dv, st["hl"], st["rc"], st["rsl"], st["rsh"],
                                                         st["gq2"], st["gkv2"], st["wuq"], st["wk"], st["wv"], after)
    late = {
        "w_lat": _dwt_lat(dhlt, st["xb"]),
        "b_in": _join_bias(st["dbg"], st["dbm"], dbl),
        "g_q": dgq[0],
        "w_uq": dwuq.reshape(Q_RANK, HEADS, HEAD_PAD)[:, :, :QK_DIM],
        "g_kv": dgkv[0],
        "w_ukv": jnp.concatenate([dwk.reshape(KV_RANK, HEADS, HEAD_PAD)[:, :, :NOPE],
                                  dwv.reshape(KV_RANK, HEADS, V_DIM)], axis=2),
    }
    return dhl, late


def _local_step(x, positions, tgt, wt, b_in, g_q, w_uq, g_kv, w_ukv, w_oa, sg_g, sg_b, w_s, b_s, w_ob, w_out, ln_g,
                ln_b):
    st = _local_attention(x, positions, wt[:LAT_COLS], b_in, g_q, w_uq, g_kv, w_ukv, b_in)
    loss, early, st = _local_head(st, x, tgt, wt, w_oa, sg_g, sg_b, w_s, b_s, w_ob, w_out, ln_g, ln_b)
    dq, dk, dv = _local_attn_bwd(st)
    dhl, late = _local_tail(st, dq, dk, dv, dv)
    dx = _dx(st["dr"], st["dhg"], st["dhm"], dhl, st["wt"], dhl)
    grads = {**early, **late}
    halves = [_dwt_early(st["dhmt"], st["dhgt"], st["xb"], jnp.full((1,), h, jnp.int32), dhl, "dwt_half%d" % h)
              for h in range(2)]
    grads["w_in"] = jnp.concatenate([grads.pop("w_lat")[:LAT_COLS], jnp.concatenate(halves, axis=1)[LAT_COLS:]],
                                    axis=0)
    return loss, dx, grads


MESH = pl.DeviceIdType.MESH
N_CHIPS = 4
HBM_SPEC = pl.BlockSpec(memory_space=pl.ANY)
HBM_SPEC_STRICT = pl.BlockSpec(memory_space=pltpu.HBM)
VMEM_SPEC = pl.BlockSpec(memory_space=pltpu.VMEM)

REP_ROWS = 80


def _rows8(a):
    flat = a.reshape(-1)
    n = -(-flat.shape[0] // (8 * D_MODEL)) * 8 * D_MODEL
    return jnp.pad(flat, (0, n - flat.shape[0])).reshape(-1, D_MODEL)


def _place():
    x, y, c = lax.axis_index("x"), lax.axis_index("y"), lax.axis_index("c")
    others = [(1 - x, y), (x, 1 - y), (1 - x, 1 - y)]
    return x, y, c, others


def _gather_weights(shards):
    n = len(shards)

    def body(*refs):
        ins, outs, bufs = refs[:n], refs[n:2 * n], refs[2 * n:3 * n]
        send_sems, recv_sems, local_sems = refs[3 * n:]
        x, y, c, others = _place()
        me = 2 * x + y
        sibling = (x, y, 1 - c)
        for src, buf in zip(ins, bufs):
            buf[...] = src[...].astype(BF16)
        own = [pltpu.make_async_copy(bufs[w], outs[w].at[me], local_sems.at[w]) for w in range(n)]
        for cp in own:
            cp.start()

        def part(w, chip, half):
            hc = shards[w].shape[1] // 2
            return outs[w].at[chip, :, pl.ds(half * hc, hc)]

        def sent(w, j):
            hc = shards[w].shape[1] // 2
            return pltpu.make_async_remote_copy(
                src_ref=bufs[w].at[:, pl.ds(c * hc, hc)], dst_ref=part(w, me, c),
                send_sem=send_sems.at[w * 3 + j], recv_sem=recv_sems.at[w * 3 + j],
                device_id=(*others[j], c), device_id_type=MESH)

        def landed(w, j):
            px, py = others[j]
            return pltpu.make_async_remote_copy(
                src_ref=part(w, 2 * px + py, c), dst_ref=part(w, 2 * px + py, c),
                send_sem=send_sems.at[w * 3 + j], recv_sem=recv_sems.at[w * 3 + j],
                device_id=(px, py, c), device_id_type=MESH)

        def passed(w, j, half):
            px, py = others[j]
            k = n * 3 + w * 3 + j
            return pltpu.make_async_remote_copy(
                src_ref=part(w, 2 * px + py, half), dst_ref=part(w, 2 * px + py, half),
                send_sem=send_sems.at[k], recv_sem=recv_sems.at[k], device_id=sibling, device_id_type=MESH)

        first = [sent(w, j) for w in range(n) for j in range(3)]
        for cp in first:
            cp.start()
        fwd = []
        for w in range(n):
            for j in range(3):
                landed(w, j).wait_recv()
                cp = passed(w, j, c)
                cp.start()
                fwd.append(cp)
        for w in range(n):
            for j in range(3):
                passed(w, j, 1 - c).wait_recv()
        for cp in first + fwd:
            cp.wait_send()
        for cp in own:
            cp.wait()

    return pl.pallas_call(
        body, name="gather_weights",
        in_specs=[VMEM_SPEC] * n, out_specs=[HBM_SPEC] * n,
        out_shape=[jax.ShapeDtypeStruct((N_CHIPS,) + s.shape, BF16) for s in shards],
        scratch_shapes=[pltpu.VMEM(s.shape, BF16) for s in shards]
        + [pltpu.SemaphoreType.DMA((6 * n,)), pltpu.SemaphoreType.DMA((6 * n,)), pltpu.SemaphoreType.DMA((n,))],
        compiler_params=pltpu.CompilerParams(vmem_limit_bytes=VMEM_LIMIT),
    )(*shards)


N_DEV = 8
LOSS_TILE = (8, 128)


def _gather_first(lat, uq):
    hl, hu = lat.shape[1] // 2, uq.shape[1] // 2

    def body(lat_ref, uq_ref, wlat_ref, guq_ref, lat_buf, uq_buf, send_sems, recv_sems, local_sems):
        x, y, c, others = _place()
        me = 2 * x + y
        sibling = (x, y, 1 - c)
        lat_buf[...] = lat_ref[...].astype(BF16)
        uq_buf[...] = uq_ref[...].astype(BF16)

        def copy(src, dst, k, to):
            return pltpu.make_async_remote_copy(src_ref=src, dst_ref=dst, send_sem=send_sems.at[k],
                                                recv_sem=recv_sems.at[k], device_id=to, device_id_type=MESH)

        def uq_part(chip, half):
            return guq_ref.at[chip, :, pl.ds(half * hu, hu)]

        def lat_part(half):
            return wlat_ref.at[:, pl.ds(half * hl, hl)]

        own = pltpu.make_async_copy(uq_buf, guq_ref.at[me], local_sems.at[0])
        own.start()
        first = [copy(uq_buf.at[:, pl.ds(c * hu, hu)], uq_part(me, c), j, (*others[j], c)) for j in range(3)]
        for cp in first:
            cp.start()

        @pl.when(me == 0)
        def _():
            mine = pltpu.make_async_copy(lat_buf, wlat_ref, local_sems.at[1])
            mine.start()
            cps = [copy(lat_buf.at[:, pl.ds(c * hl, hl)], lat_part(c), 6 + j, (*others[j], c)) for j in range(3)]
            for cp in cps:
                cp.start()
            for cp in cps:
                cp.wait_send()
            mine.wait()

        @pl.when(me != 0)
        def _():
            j0 = x + 2 * y - 1
            copy(lat_part(c), lat_part(c), 6 + j0, (0, 0, c)).wait_recv()
            fwd = copy(lat_part(c), lat_part(c), 9, sibling)
            fwd.start()
            copy(lat_part(1 - c), lat_part(1 - c), 9, sibling).wait_recv()
            fwd.wait_send()

        fwd = []
        for j, (px, py) in enumerate(others):
            chip = 2 * px + py
            copy(uq_part(chip, c), uq_part(chip, c), j, (px, py, c)).wait_recv()
            cp = copy(uq_part(chip, c), uq_part(chip, c), 3 + j, sibling)
            cp.start()
            fwd.append(cp)
        for j, (px, py) in enumerate(others):
            chip = 2 * px + py
            copy(uq_part(chip, 1 - c), uq_part(chip, 1 - c), 3 + j, sibling).wait_recv()
        for cp in first + fwd:
            cp.wait_send()
        own.wait()

    return pl.pallas_call(
        body, name="gather_first", in_specs=[VMEM_SPEC, VMEM_SPEC], out_specs=[HBM_SPEC, HBM_SPEC],
        out_shape=[jax.ShapeDtypeStruct(lat.shape, BF16), jax.ShapeDtypeStruct((N_CHIPS,) + uq.shape, BF16)],
        scratch_shapes=[pltpu.VMEM(lat.shape, BF16), pltpu.VMEM(uq.shape, BF16), pltpu.SemaphoreType.DMA((10,)),
                        pltpu.SemaphoreType.DMA((10,)), pltpu.SemaphoreType.DMA((2,))],
        compiler_params=pltpu.CompilerParams(vmem_limit_bytes=VMEM_LIMIT),
    )(lat, uq)


def _cast_own(shards, me, after):
    n = len(shards)

    def body(me_ref, *refs):
        for w in range(n):
            refs[n + 1 + w][...] = refs[w][...].astype(BF16)

    return pl.pallas_call(
        body, name="cast_own",
        grid_spec=pltpu.PrefetchScalarGridSpec(
            num_scalar_prefetch=1, grid=(1,),
            in_specs=[pl.BlockSpec(s.shape, lambda i, me_ref: (0, 0)) for s in shards]
            + [pl.BlockSpec(memory_space=pl.ANY)],
            out_specs=[pl.BlockSpec((None,) + s.shape, lambda i, me_ref: (me_ref[0], 0, 0)) for s in shards]),
        out_shape=[jax.ShapeDtypeStruct((N_CHIPS,) + s.shape, BF16) for s in shards],
        compiler_params=pltpu.CompilerParams(vmem_limit_bytes=VMEM_LIMIT),
    )(me, *shards, after)


def _cast_first(lat, uq, me):
    def body(me_ref, lat_ref, uq_ref, wlat_ref, guq_ref):
        wlat_ref[...] = lat_ref[...].astype(BF16)
        guq_ref[...] = uq_ref[...].astype(BF16)

    return pl.pallas_call(
        body, name="cast_first",
        grid_spec=pltpu.PrefetchScalarGridSpec(
            num_scalar_prefetch=1, grid=(1,),
            in_specs=[pl.BlockSpec(lat.shape, lambda i, me_ref: (0, 0)),
                      pl.BlockSpec(uq.shape, lambda i, me_ref: (0, 0))],
            out_specs=[pl.BlockSpec(lat.shape, lambda i, me_ref: (0, 0)),
                       pl.BlockSpec((None,) + uq.shape, lambda i, me_ref: (me_ref[0], 0, 0))]),
        out_shape=[jax.ShapeDtypeStruct(lat.shape, BF16), jax.ShapeDtypeStruct((N_CHIPS,) + uq.shape, BF16)],
    )(me, lat, uq)


def _first_copies(wlat_ref, guq_ref, send_sems, recv_sems, shapes):
    x, y, c, others = _place()
    me = 2 * x + y
    hl, hu = shapes[0][1] // 2, shapes[1][2] // 2
    lat_half = wlat_ref.at[:, pl.ds(c * hl, hl)]

    def copy(src, dst, k, to):
        return pltpu.make_async_remote_copy(src_ref=src, dst_ref=dst, send_sem=send_sems.at[k],
                                            recv_sem=recv_sems.at[k], device_id=to, device_id_type=MESH)

    def uq_half(chip):
        return guq_ref.at[chip, :, pl.ds(c * hu, hu)]

    lat_out = [copy(lat_half, lat_half, j, (*others[j], c)) for j in range(3)]
    uq_out = [copy(uq_half(me), uq_half(me), 3 + j, (*others[j], c)) for j in range(3)]
    j0 = jnp.maximum(x + 2 * y - 1, 0)
    lat_in = copy(lat_half, lat_half, j0, (0, 0, c))
    uq_in = [copy(uq_half(me), uq_half(2 * px + py), 3 + j, (px, py, c)) for j, (px, py) in enumerate(others)]
    return me, lat_out, uq_out, lat_in, uq_in


def _first_start(wlat, guq):
    shapes = (wlat.shape, guq.shape)

    def body(wlat_ref, guq_ref, send_sems, recv_sems, wlat_thru, guq_thru, token):
        me, lat_out, uq_out, _, _ = _first_copies(wlat_ref, guq_ref, send_sems, recv_sems, shapes)

        @pl.when(me == 0)
        def _():
            for cp in lat_out:
                cp.start()

        for cp in uq_out:
            cp.start()
        token[...] = jnp.zeros_like(token)

    outs = pl.pallas_call(
        body, name="first_start",
        out_shape=(pltpu.SemaphoreType.DMA((6,)), pltpu.SemaphoreType.DMA((6,)), pltpu.HBM(wlat.shape, BF16),
                   pltpu.HBM(guq.shape, BF16), jax.ShapeDtypeStruct(LOSS_TILE, F32)),
        in_specs=[HBM_SPEC_STRICT] * 2, out_specs=(SEM_SPEC, SEM_SPEC, HBM_SPEC_STRICT, HBM_SPEC_STRICT, VMEM_SPEC),
        input_output_aliases={0: 2, 1: 3},
        compiler_params=pltpu.CompilerParams(has_side_effects=SPLIT_EFFECT),
    )(pltpu.with_memory_space_constraint(wlat, pltpu.HBM), pltpu.with_memory_space_constraint(guq, pltpu.HBM))
    return outs


def _first_wait(send_sems, recv_sems, wlat, guq, after):
    shapes = (wlat.shape, guq.shape)

    def body(wlat_ref, guq_ref, send_sems, recv_sems, after_ref, wlat_out, guq_out):
        me, lat_out, uq_out, lat_in, uq_in = _first_copies(wlat_ref, guq_ref, send_sems, recv_sems, shapes)

        @pl.when(me == 0)
        def _():
            for cp in lat_out:
                cp.wait_send()

        @pl.when(me != 0)
        def _():
            lat_in.wait_recv()

        for cp in uq_out:
            cp.wait_send()
        for cp in uq_in:
            cp.wait_recv()

    return pl.pallas_call(
        body, name="first_wait", out_shape=(pltpu.HBM(wlat.shape, BF16), pltpu.HBM(guq.shape, BF16)),
        in_specs=[HBM_SPEC_STRICT, HBM_SPEC_STRICT, SEM_SPEC, SEM_SPEC, HBM_SPEC],
        out_specs=(HBM_SPEC_STRICT, HBM_SPEC_STRICT), input_output_aliases={0: 0, 1: 1},
        compiler_params=pltpu.CompilerParams(has_side_effects=SPLIT_EFFECT),
    )(wlat, guq, send_sems, recv_sems, after)


def _first_forward(wlat, guq):
    hl, hu = wlat.shape[1] // 2, guq.shape[2] // 2

    def body(wlat_in, guq_in, wlat_ref, guq_ref, send_sems, recv_sems):
        x, y, c, others = _place()
        me = 2 * x + y
        sibling = (x, y, 1 - c)

        def copy(part, k):
            return pltpu.make_async_remote_copy(src_ref=part, dst_ref=part, send_sem=send_sems.at[k],
                                                recv_sem=recv_sems.at[k], device_id=sibling, device_id_type=MESH)

        def uq_part(j, half):
            px, py = others[j]
            return guq_ref.at[2 * px + py, :, pl.ds(half * hu, hu)]

        cps = [copy(uq_part(j, c), j) for j in range(3)]
        for cp in cps:
            cp.start()

        @pl.when(me != 0)
        def _():
            mine = copy(wlat_ref.at[:, pl.ds(c * hl, hl)], 3)
            mine.start()
            copy(wlat_ref.at[:, pl.ds((1 - c) * hl, hl)], 3).wait_recv()
            mine.wait_send()

        for j in range(3):
            copy(uq_part(j, 1 - c), j).wait_recv()
        for cp in cps:
            cp.wait_send()

    return pl.pallas_call(
        body, name="first_forward", in_specs=[HBM_SPEC, HBM_SPEC], out_specs=[HBM_SPEC, HBM_SPEC],
        out_shape=[jax.ShapeDtypeStruct(wlat.shape, BF16), jax.ShapeDtypeStruct(guq.shape, BF16)],
        input_output_aliases={0: 0, 1: 1},
        scratch_shapes=[pltpu.SemaphoreType.DMA((4,)), pltpu.SemaphoreType.DMA((4,))],
    )(wlat, guq)


def _gather_start(bufs, after):
    n = len(bufs)

    def body(*refs):
        b_refs = refs[:n]
        send_sems, recv_sems, token = refs[n + 1], refs[n + 2], refs[-1]
        x, y, c, others = _place()
        me = 2 * x + y
        for w in range(n):
            hc = _half(bufs[w])
            mine = b_refs[w].at[me, :, pl.ds(c * hc, hc)]
            for j, (px, py) in enumerate(others):
                pltpu.make_async_remote_copy(
                    src_ref=mine, dst_ref=mine, send_sem=send_sems.at[3 * w + j], recv_sem=recv_sems.at[3 * w + j],
                    device_id=(px, py, c), device_id_type=MESH).start()
        token[...] = jnp.zeros_like(token)

    hbm = [pltpu.HBM(b.shape, BF16) for b in bufs]
    outs = pl.pallas_call(
        body, name="gather_start",
        out_shape=(pltpu.SemaphoreType.DMA((3 * n,)), pltpu.SemaphoreType.DMA((3 * n,)), *hbm,
                   jax.ShapeDtypeStruct(LOSS_TILE, F32)),
        in_specs=[HBM_SPEC_STRICT] * n + [HBM_SPEC],
        out_specs=(SEM_SPEC, SEM_SPEC, *[HBM_SPEC_STRICT] * n, VMEM_SPEC),
        input_output_aliases={i: 2 + i for i in range(n)},
        compiler_params=pltpu.CompilerParams(has_side_effects=SPLIT_EFFECT),
    )(*[pltpu.with_memory_space_constraint(b, pltpu.HBM) for b in bufs], after)
    return outs[0], outs[1], list(outs[2:2 + n]), outs[-1]


def _gather_wait(send_sems, recv_sems, bufs, after):
    n = len(bufs)

    def body(*refs):
        b_refs = refs[:n]
        send_sems, recv_sems = refs[n], refs[n + 1]
        x, y, c, others = _place()
        me = 2 * x + y
        for w in range(n):
            hc = _half(bufs[w])
            for j, (px, py) in enumerate(others):
                cp = pltpu.make_async_remote_copy(
                    src_ref=b_refs[w].at[me, :, pl.ds(c * hc, hc)],
                    dst_ref=b_refs[w].at[2 * px + py, :, pl.ds(c * hc, hc)],
                    send_sem=send_sems.at[3 * w + j], recv_sem=recv_sems.at[3 * w + j], device_id=(px, py, c),
                    device_id_type=MESH)
                cp.wait_send()
                cp.wait_recv()

    outs = pl.pallas_call(
        body, name="gather_wait", out_shape=tuple(pltpu.HBM(b.shape, b.dtype) for b in bufs),
        in_specs=[HBM_SPEC_STRICT] * n + [SEM_SPEC, SEM_SPEC, HBM_SPEC],
        out_specs=tuple([HBM_SPEC_STRICT] * n), input_output_aliases={i: i for i in range(n)},
        compiler_params=pltpu.CompilerParams(has_side_effects=SPLIT_EFFECT),
    )(*bufs, send_sems, recv_sems, after)
    return list(outs)


def _gather_finish(bufs):
    n = len(bufs)

    def body(*refs):
        b_refs = refs[n:2 * n]
        send_sems, recv_sems = refs[2 * n:]
        x, y, c, others = _place()
        cps = []
        for w in range(n):
            hc = _half(bufs[w])
            for j, (px, py) in enumerate(others):
                part = b_refs[w].at[2 * px + py, :, pl.ds(c * hc, hc)]
                cps.append(pltpu.make_async_remote_copy(
                    src_ref=part, dst_ref=part, send_sem=send_sems.at[3 * w + j], recv_sem=recv_sems.at[3 * w + j],
                    device_id=(x, y, 1 - c), device_id_type=MESH))
        for cp in cps:
            cp.start()
        for w in range(n):
            hc = _half(bufs[w])
            for j, (px, py) in enumerate(others):
                theirs = b_refs[w].at[2 * px + py, :, pl.ds((1 - c) * hc, hc)]
                pltpu.make_async_remote_copy(
                    src_ref=theirs, dst_ref=theirs, send_sem=send_sems.at[3 * w + j], recv_sem=recv_sems.at[3 * w + j],
                    device_id=(x, y, 1 - c), device_id_type=MESH).wait_recv()
        for cp in cps:
            cp.wait_send()

    return pl.pallas_call(
        body, name="gather_finish", in_specs=[HBM_SPEC] * n, out_specs=[HBM_SPEC] * n,
        out_shape=[jax.ShapeDtypeStruct(b.shape, b.dtype) for b in bufs],
        input_output_aliases={i: i for i in range(n)},
        scratch_shapes=[pltpu.SemaphoreType.DMA((3 * n,)), pltpu.SemaphoreType.DMA((3 * n,))],
    )(*bufs)


def _half(a):
    return a.shape[-1] // 2


def _exchange_pairs(parts, name):
    n = len(parts)

    def body(*refs):
        p_refs, r_refs = refs[:n], refs[n:2 * n]
        send_sems, recv_sems = refs[2 * n:]
        x, y, c, _ = _place()
        cps = []
        for w in range(n):
            h = _half(parts[w])
            cps.append(pltpu.make_async_remote_copy(
                src_ref=p_refs[w].at[:, :, pl.ds((1 - c) * h, h)], dst_ref=r_refs[w],
                send_sem=send_sems.at[w], recv_sem=recv_sems.at[w], device_id=(x, y, 1 - c), device_id_type=MESH))
        for cp in cps:
            cp.start()
        for cp in cps:
            cp.wait()

    return pl.pallas_call(
        body, name=name, in_specs=[HBM_SPEC] * n, out_specs=[HBM_SPEC] * n,
        out_shape=[jax.ShapeDtypeStruct((N_CHIPS, p.shape[1], _half(p)), BF16) for p in parts],
        scratch_shapes=[pltpu.SemaphoreType.DMA((n,)), pltpu.SemaphoreType.DMA((n,))],
    )(*parts)


def _sibling_part(ref, w, n_whole, shape, c):
    if w < n_whole:
        return ref
    h = shape[-1] // 2
    return ref.at[:, :, pl.ds((1 - c) * h, h)]


def _pairs_start(parts, all_loss, n_whole):
    n = len(parts)

    def body(*refs):
        p_refs, r_refs, loss_ref = refs[:n], refs[n:2 * n], refs[2 * n]
        send_sems, recv_sems, token = refs[2 * n + 1], refs[2 * n + 2], refs[-1]
        x, y, c, _ = _place()
        for w in range(n):
            h = _half(parts[w])
            pltpu.make_async_remote_copy(
                src_ref=_sibling_part(p_refs[w], w, n_whole, parts[w].shape, c), dst_ref=r_refs[w],
                send_sem=send_sems.at[w], recv_sem=recv_sems.at[w], device_id=(x, y, 1 - c),
                device_id_type=MESH).start()
        me = 4 * x + 2 * y + c
        for t in range(1, N_DEV):
            d = (me + t) % N_DEV
            pltpu.make_async_remote_copy(
                src_ref=loss_ref.at[me], dst_ref=loss_ref.at[me], send_sem=send_sems.at[n + t - 1],
                recv_sem=recv_sems.at[n + t - 1], device_id=(d // 4, (d // 2) % 2, d % 2), device_id_type=MESH).start()
        token[...] = jnp.zeros_like(token)

    lands = [pltpu.HBM(p.shape if w < n_whole else (N_CHIPS, p.shape[1], _half(p)), BF16)
             for w, p in enumerate(parts)]
    nsem = n + N_DEV - 1
    outs = pl.pallas_call(
        body, name="pairs_start",
        out_shape=(pltpu.SemaphoreType.DMA((nsem,)), pltpu.SemaphoreType.DMA((nsem,)),
                   *[pltpu.HBM(p.shape, p.dtype) for p in parts], *lands, pltpu.HBM(all_loss.shape, F32),
                   jax.ShapeDtypeStruct(LOSS_TILE, F32)),
        in_specs=[HBM_SPEC_STRICT] * (2 * n + 1),
        out_specs=(SEM_SPEC, SEM_SPEC, *[HBM_SPEC_STRICT] * (2 * n + 1), VMEM_SPEC),
        input_output_aliases={i: 2 + i for i in range(2 * n + 1)},
        compiler_params=pltpu.CompilerParams(has_side_effects=SPLIT_EFFECT),
    )(*[pltpu.with_memory_space_constraint(p, pltpu.HBM) for p in parts],
      *[pltpu.with_memory_space_constraint(lax.empty(l.shape, BF16), pltpu.HBM) for l in lands],
      pltpu.with_memory_space_constraint(all_loss, pltpu.HBM))
    return outs[0], outs[1], list(outs[2:2 + n]), list(outs[2 + n:2 + 2 * n]), outs[2 + 2 * n], outs[-1]


def _pairs_wait(send_sems, recv_sems, parts, lands, all_loss, after, n_whole):
    n = len(parts)

    def body(*refs):
        p_refs, r_refs, loss_ref = refs[:n], refs[n:2 * n], refs[2 * n]
        send_sems, recv_sems = refs[2 * n + 1], refs[2 * n + 2]
        x, y, c, _ = _place()
        for w in range(n):
            h = _half(parts[w])
            cp = pltpu.make_async_remote_copy(
                src_ref=_sibling_part(p_refs[w], w, n_whole, parts[w].shape, c), dst_ref=r_refs[w],
                send_sem=send_sems.at[w],
                recv_sem=recv_sems.at[w], device_id=(x, y, 1 - c), device_id_type=MESH)
            cp.wait_send()
            cp.wait_recv()
        me = 4 * x + 2 * y + c
        for t in range(1, N_DEV):
            d = (me + N_DEV - t) % N_DEV
            cp = pltpu.make_async_remote_copy(
                src_ref=loss_ref.at[me], dst_ref=loss_ref.at[d], send_sem=send_sems.at[n + t - 1],
                recv_sem=recv_sems.at[n + t - 1], device_id=(d // 4, (d // 2) % 2, d % 2), device_id_type=MESH)
            cp.wait_send()
            cp.wait_recv()

    bufs = (*parts, *lands, all_loss)
    outs = pl.pallas_call(
        body, name="pairs_wait", out_shape=tuple(pltpu.HBM(a.shape, a.dtype) for a in bufs),
        in_specs=[HBM_SPEC_STRICT] * len(bufs) + [SEM_SPEC, SEM_SPEC, HBM_SPEC],
        out_specs=tuple([HBM_SPEC_STRICT] * len(bufs)), input_output_aliases={i: i for i in range(len(bufs))},
        compiler_params=pltpu.CompilerParams(has_side_effects=SPLIT_EFFECT),
    )(*bufs, send_sems, recv_sems, after)
    return list(outs[:n]), list(outs[n:2 * n]), outs[2 * n]


def _add_pair_tiled(p, r):
    rows, h = r.shape[1:]

    def body(p_ref, r_ref, q_ref):
        q_ref[...] = (p_ref[...].astype(F32) + r_ref[...].astype(F32)).astype(BF16)

    spec = pl.BlockSpec((None, rows, h), lambda k: (k, 0, 0))
    return pl.pallas_call(
        body, name="add_pair_w_in", grid=(N_CHIPS,), in_specs=[spec, spec], out_specs=spec,
        out_shape=jax.ShapeDtypeStruct(r.shape, BF16),
    )(p, r)


def _add_pair_small(ps, rs, c, name):
    n = len(ps)

    def body(c_ref, *refs):
        for w in range(n):
            h = _half(ps[w])
            mine = refs[w][:, :, pl.ds(pl.multiple_of(c_ref[0] * h, 128), h)]
            refs[2 * n + w][...] = (mine.astype(F32) + refs[n + w][...].astype(F32)).astype(BF16)

    return pl.pallas_call(
        body, name=name,
        in_specs=[pl.BlockSpec(memory_space=pltpu.SMEM)] + [VMEM_SPEC] * (2 * n), out_specs=[VMEM_SPEC] * n,
        out_shape=[jax.ShapeDtypeStruct(r.shape, BF16) for r in rs],
        compiler_params=pltpu.CompilerParams(vmem_limit_bytes=VMEM_LIMIT),
    )(c, *ps, *rs)


def _exchange_chips(qs):
    n = len(qs)

    def body(*refs):
        q_refs, r_refs = refs[:n], refs[n:2 * n]
        send_sems, recv_sems = refs[2 * n:]
        x, y, c, others = _place()
        me = 2 * x + y
        cps = []
        for w in range(n):
            for j, (px, py) in enumerate(others):
                cps.append(pltpu.make_async_remote_copy(
                    src_ref=q_refs[w].at[2 * px + py], dst_ref=r_refs[w].at[me], send_sem=send_sems.at[3 * w + j],
                    recv_sem=recv_sems.at[3 * w + j], device_id=(px, py, c), device_id_type=MESH))
        for cp in cps:
            cp.start()
        for w in range(n):
            for j, (px, py) in enumerate(others):
                pltpu.make_async_remote_copy(
                    src_ref=q_refs[w].at[me], dst_ref=r_refs[w].at[2 * px + py], send_sem=send_sems.at[3 * w + j],
                    recv_sem=recv_sems.at[3 * w + j], device_id=(px, py, c), device_id_type=MESH).wait_recv()
        for cp in cps:
            cp.wait_send()

    return pl.pallas_call(
        body, name="exchange_chips", in_specs=[HBM_SPEC] * n, out_specs=[HBM_SPEC] * n,
        out_shape=[jax.ShapeDtypeStruct(q.shape, BF16) for q in qs],
        scratch_shapes=[pltpu.SemaphoreType.DMA((3 * n,)), pltpu.SemaphoreType.DMA((3 * n,))],
    )(*qs)


SEM_SPEC = pl.BlockSpec(memory_space=pltpu.SEMAPHORE)
SPLIT_EFFECT = pltpu.SideEffectType.DATAFLOW_SIDE_EFFECTING


def _chips_start(qs, name):
    n = len(qs)

    def body(*refs):
        q_refs, land_refs = refs[:n], refs[n:2 * n]
        send_sems, recv_sems, token = refs[2 * n], refs[2 * n + 1], refs[-1]
        x, y, c, others = _place()
        me = 2 * x + y
        for w in range(n):
            for j, (px, py) in enumerate(others):
                pltpu.make_async_remote_copy(
                    src_ref=q_refs[w].at[2 * px + py], dst_ref=land_refs[w].at[me], send_sem=send_sems.at[3 * w + j],
                    recv_sem=recv_sems.at[3 * w + j], device_id=(px, py, c), device_id_type=MESH).start()
        token[...] = jnp.zeros_like(token)

    hbm = [pltpu.HBM(q.shape, BF16) for q in qs]
    outs = pl.pallas_call(
        body, name=name,
        out_shape=(pltpu.SemaphoreType.DMA((3 * n,)), pltpu.SemaphoreType.DMA((3 * n,)), *hbm, *hbm,
                   jax.ShapeDtypeStruct(LOSS_TILE, F32)),
        in_specs=[HBM_SPEC_STRICT] * (2 * n),
        out_specs=(SEM_SPEC, SEM_SPEC, *[HBM_SPEC_STRICT] * (2 * n), VMEM_SPEC),
        input_output_aliases={i: 2 + i for i in range(2 * n)},
        compiler_params=pltpu.CompilerParams(has_side_effects=SPLIT_EFFECT),
    )(*[pltpu.with_memory_space_constraint(q, pltpu.HBM) for q in qs],
      *[pltpu.with_memory_space_constraint(lax.empty(q.shape, BF16), pltpu.HBM) for q in qs])
    return outs[0], outs[1], outs[2:2 + n], outs[2 + n:2 + 2 * n], outs[-1]


def _chips_wait(send_sems, recv_sems, q_thru, land_thru, after, name):
    n = len(q_thru)

    def body(*refs):
        q_refs, land_refs = refs[:n], refs[n:2 * n]
        send_sems, recv_sems = refs[2 * n], refs[2 * n + 1]
        x, y, c, others = _place()
        me = 2 * x + y
        for w in range(n):
            for j, (px, py) in enumerate(others):
                cp = pltpu.make_async_remote_copy(
                    src_ref=q_refs[w].at[2 * px + py], dst_ref=land_refs[w].at[2 * px + py],
                    send_sem=send_sems.at[3 * w + j], recv_sem=recv_sems.at[3 * w + j], device_id=(px, py, c),
                    device_id_type=MESH)
                cp.wait_send()
                cp.wait_recv()

    outs = pl.pallas_call(
        body, name=name, out_shape=tuple(pltpu.HBM(a.shape, a.dtype) for a in (*q_thru, *land_thru)),
        in_specs=[HBM_SPEC_STRICT] * (2 * n) + [SEM_SPEC, SEM_SPEC, HBM_SPEC],
        out_specs=tuple([HBM_SPEC_STRICT] * (2 * n)), input_output_aliases={i: i for i in range(2 * n)},
        compiler_params=pltpu.CompilerParams(has_side_effects=SPLIT_EFFECT),
    )(*q_thru, *land_thru, send_sems, recv_sems, after)
    return list(outs[:n]), list(outs[n:])


def _sum_chips_tiled(q, r, idx, tile):
    rows, h = r.shape[1:]
    nt = h // tile

    def body(idx_ref, q_ref, r0_ref, r1_ref, r2_ref, g_ref):
        g_ref[...] = (q_ref[...].astype(F32) + r0_ref[...].astype(F32) + r1_ref[...].astype(F32)
                      + r2_ref[...].astype(F32))

    def slab(t):
        return pl.BlockSpec((None, rows, tile), lambda i, idx_ref: (idx_ref[t], 0, i))

    return pl.pallas_call(
        body, name="sum_chips_w_in",
        grid_spec=pltpu.PrefetchScalarGridSpec(
            num_scalar_prefetch=1, grid=(nt,), in_specs=[slab(0), slab(1), slab(2), slab(3)],
            out_specs=pl.BlockSpec((rows, tile), lambda i, idx_ref: (0, idx_ref[4] * nt + i))),
        out_shape=jax.ShapeDtypeStruct((rows, 2 * h), F32),
    )(idx, q, r, r, r)


def _sum_chips_small(qs, rs, idx, all_dtypes):
    n = len(rs)
    n_all = len(all_dtypes)

    def body(idx_ref, *refs):
        c = idx_ref[4]
        for w in range(n):
            q_ref, r_ref, g_ref = refs[w], refs[n + w], refs[2 * n + w]
            acc = q_ref[idx_ref[0]].astype(F32)
            for t in range(1, N_CHIPS):
                acc = acc + r_ref[idx_ref[t]].astype(F32)
            h = rs[w].shape[2]
            mine = pl.ds(pl.multiple_of(c * h, 128), h)
            g_ref[...] = jnp.zeros_like(g_ref)
            if w >= n - n_all:
                g_ref[idx_ref[0], :, mine] = acc.astype(g_ref.dtype)
            else:
                g_ref[:, mine] = acc

    shapes = [jax.ShapeDtypeStruct((r.shape[1], 2 * r.shape[2]), F32) for r in rs[:n - n_all]]
    shapes += [jax.ShapeDtypeStruct((N_CHIPS, r.shape[1], 2 * r.shape[2]), dt)
               for r, dt in zip(rs[n - n_all:], all_dtypes)]
    return pl.pallas_call(
        body, name="sum_chips_small",
        in_specs=[pl.BlockSpec(memory_space=pltpu.SMEM)] + [VMEM_SPEC] * (2 * n), out_specs=[VMEM_SPEC] * n,
        out_shape=shapes, compiler_params=pltpu.CompilerParams(vmem_limit_bytes=VMEM_LIMIT),
    )(idx, *qs, *rs)


def _share(shards, alls):
    n, na = len(shards), len(alls)
    total = n + na

    def body(*refs):
        g_refs, a_refs = refs[total:total + n], refs[total + n:2 * total]
        send_sems, recv_sems = refs[2 * total:]
        x, y, c, others = _place()
        me = 2 * x + y
        sibling = (x, y, 1 - c)

        def cols_of(w, half):
            h = shards[w].shape[1] // 2
            return g_refs[w].at[:, pl.ds(half * h, h)]

        def slab(a, chip, half):
            h = alls[a].shape[2] // 2
            return a_refs[a].at[chip, :, pl.ds(half * h, h)]

        def copy(src, dst, k, to):
            return pltpu.make_async_remote_copy(src_ref=src, dst_ref=dst, send_sem=send_sems.at[k],
                                                recv_sem=recv_sems.at[k], device_id=to, device_id_type=MESH)

        cps = [copy(cols_of(w, c), cols_of(w, c), w, sibling) for w in range(n)]
        for a in range(na):
            base = n + 7 * a
            cps.append(copy(slab(a, me, c), slab(a, me, c), base, sibling))
            for j, (px, py) in enumerate(others):
                cps.append(copy(slab(a, me, c), slab(a, me, c), base + 1 + j, (px, py, c)))
        for cp in cps:
            cp.start()
        fwd = []
        for a in range(na):
            base = n + 7 * a
            for j, (px, py) in enumerate(others):
                chip = 2 * px + py
                copy(slab(a, me, c), slab(a, chip, c), base + 1 + j, (px, py, c)).wait_recv()
                cp = copy(slab(a, chip, c), slab(a, chip, c), base + 4 + j, sibling)
                cp.start()
                fwd.append(cp)
        for a in range(na):
            base = n + 7 * a
            for j, (px, py) in enumerate(others):
                chip = 2 * px + py
                copy(slab(a, chip, c), slab(a, chip, 1 - c), base + 4 + j, sibling).wait_recv()
            copy(slab(a, me, c), slab(a, me, 1 - c), base, sibling).wait_recv()
        for w in range(n):
            copy(cols_of(w, c), cols_of(w, 1 - c), w, sibling).wait_recv()
        for cp in cps + fwd:
            cp.wait_send()

    nsem = n + 7 * na
    return pl.pallas_call(
        body, name="share", in_specs=[HBM_SPEC] * total, out_specs=[HBM_SPEC] * total,
        out_shape=[jax.ShapeDtypeStruct(a.shape, a.dtype) for a in (*shards, *alls)],
        input_output_aliases={i: i for i in range(total)},
        scratch_shapes=[pltpu.SemaphoreType.DMA((nsem,)), pltpu.SemaphoreType.DMA((nsem,))],
    )(*shards, *alls)


def _adamw(w, g, m, v):
    m2 = ADAM_B1 * m + (1.0 - ADAM_B1) * g
    v2 = ADAM_B2 * v + (1.0 - ADAM_B2) * (g * g)
    m_hat = m2 / (1.0 - ADAM_B1 ** ADAM_STEP)
    v_hat = v2 / (1.0 - ADAM_B2 ** ADAM_STEP)
    return -ADAM_LR * (m_hat / (jnp.sqrt(v_hat) + ADAM_EPS) + ADAM_WD * w), m2, v2


def _update_w_in(wt, gt, mt, vt, lat, owner, tile):
    nlat = lat.shape[0] // tile

    def body(owner_ref, w_ref, g_ref, m_ref, v_ref, lat_ref, g2_ref, d_ref, m2_ref, v2_ref):
        row = pl.program_id(0) * tile + lax.broadcasted_iota(jnp.int32, (tile, 1), 0)
        g = jnp.where((row < LAT_COLS) & (owner_ref[0] == 1), lat_ref[...].astype(F32), g_ref[...])
        g2_ref[...] = g
        d_ref[...], m2_ref[...], v2_ref[...] = _adamw(w_ref[...], g, m_ref[...], v_ref[...])

    spec = pl.BlockSpec((tile, wt.shape[1]), lambda i, o: (i, 0))
    return pl.pallas_call(
        body, name="update_w_in",
        grid_spec=pltpu.PrefetchScalarGridSpec(
            num_scalar_prefetch=1, grid=(wt.shape[0] // tile,),
            in_specs=[spec] * 4 + [pl.BlockSpec((tile, wt.shape[1]), lambda i, o: (jnp.minimum(i, nlat - 1), 0))],
            out_specs=[spec] * 4),
        out_shape=[jax.ShapeDtypeStruct(wt.shape, F32)] * 4,
        compiler_params=_params(("parallel",)),
    )(owner, wt, gt, mt, vt, lat)


def _update_small(ws, gs, ms, vs):
    n = len(ws)

    def body(*refs):
        for k in range(n):
            w_ref, g_ref, m_ref, v_ref = refs[k], refs[n + k], refs[2 * n + k], refs[3 * n + k]
            d, m2, v2 = _adamw(w_ref[...], g_ref[...], m_ref[...], v_ref[...])
            refs[4 * n + k][...] = d
            refs[5 * n + k][...] = m2
            refs[6 * n + k][...] = v2

    shapes = [jax.ShapeDtypeStruct(w.shape, F32) for w in ws]
    outs = pl.pallas_call(
        body, name="update_small", in_specs=[VMEM_SPEC] * (4 * n), out_specs=[VMEM_SPEC] * (3 * n),
        out_shape=shapes * 3,
        compiler_params=pltpu.CompilerParams(vmem_limit_bytes=VMEM_LIMIT),
    )(*ws, *gs, *ms, *vs)
    return outs[:n], outs[n:2 * n], outs[2 * n:]


SHARDED = ("w_in", "w_uq", "w_oa", "w_ob", "w_out")
REPLICATED = ("b_in", "g_q", "g_kv", "w_ukv", "sgu_ln_g", "sgu_ln_b", "w_s", "b_s", "ln_g", "ln_b")
ORDER = ("w_in", "b_in", "g_q", "w_uq", "g_kv", "w_ukv", "w_oa", "sgu_ln_g", "sgu_ln_b", "w_s", "b_s", "w_ob", "w_out",
         "ln_g", "ln_b")


def kernel(x, positions, w_in, b_in, g_q, w_uq, g_kv, w_ukv, w_oa, sgu_ln_g, sgu_ln_b, w_s, b_s, w_ob, w_out, ln_g, ln_b, loss_target, m_w_in, m_b_in, m_g_q, m_w_uq, m_g_kv, m_w_ukv, m_w_oa, m_sgu_ln_g, m_sgu_ln_b, m_w_s, m_b_s, m_w_ob, m_w_out, m_ln_g, m_ln_b, v_w_in, v_b_in, v_g_q, v_w_uq, v_g_kv, v_w_ukv, v_w_oa, v_sgu_ln_g, v_sgu_ln_b, v_w_s, v_b_s, v_w_ob, v_w_out, v_ln_g, v_ln_b):
    w = dict(w_in=w_in, b_in=b_in, g_q=g_q, w_uq=w_uq, g_kv=g_kv, w_ukv=w_ukv, w_oa=w_oa, sgu_ln_g=sgu_ln_g,
             sgu_ln_b=sgu_ln_b, w_s=w_s, b_s=b_s, w_ob=w_ob, w_out=w_out, ln_g=ln_g, ln_b=ln_b)
    m = dict(w_in=m_w_in, b_in=m_b_in, g_q=m_g_q, w_uq=m_w_uq, g_kv=m_g_kv, w_ukv=m_w_ukv, w_oa=m_w_oa,
             sgu_ln_g=m_sgu_ln_g, sgu_ln_b=m_sgu_ln_b, w_s=m_w_s, b_s=m_b_s, w_ob=m_w_ob, w_out=m_w_out, ln_g=m_ln_g,
             ln_b=m_ln_b)
    v = dict(w_in=v_w_in, b_in=v_b_in, g_q=v_g_q, w_uq=v_w_uq, g_kv=v_g_kv, w_ukv=v_w_ukv, w_oa=v_w_oa,
             sgu_ln_g=v_sgu_ln_g, sgu_ln_b=v_sgu_ln_b, w_s=v_w_s, b_s=v_b_s, w_ob=v_w_ob, w_out=v_w_out, ln_g=v_ln_g,
             ln_b=v_ln_b)
    w, m, v = ({n: a[0] for n, a in d.items()} for d in (w, m, v))
    c = lax.axis_index("c")

    wt_shard, mt_shard, vt_shard = (jnp.transpose(d["w_in"]) for d in (w, m, v))
    xi, yi = lax.axis_index("x"), lax.axis_index("y")
    me1 = (2 * xi + yi).reshape(1).astype(jnp.int32)
    first = _first_start(*_cast_first(wt_shard[:LAT_COLS], w["w_uq"].reshape(Q_RANK // 4, HEADS * QK_DIM), me1))
    bufs = _cast_own([wt_shard, w["w_oa"], w["w_ob"], w["w_out"]], me1, first[4])
    send0, recv0, bufs, token0 = _gather_start(bufs, first[4])
    g_lat, g_uq = _first_forward(*_first_wait(*first[:4], token0))
    st = _local_attention(x[0], positions[0], g_lat, w["b_in"], w["g_q"], g_uq.reshape(Q_RANK, HEADS, QK_DIM),
                          w["g_kv"], w["w_ukv"], token0)
    g_in, g_oa, g_ob, g_out = _gather_finish(_gather_wait(send0, recv0, bufs, st["o"]))
    wt = g_in.reshape(IN_W, D_MODEL)

    loss, early, st = _local_head(
        st, x[0], loss_target[0], wt, g_oa, w["sgu_ln_g"], w["sgu_ln_b"], w["w_s"], w["b_s"], g_ob,
        g_out.reshape(D_MODEL, D_MODEL), w["ln_g"], w["ln_b"])

    c1 = c.reshape(1).astype(jnp.int32)
    idx = jnp.stack([2 * xi + yi, 2 * (1 - xi) + yi, 2 * xi + (1 - yi), 2 * (1 - xi) + (1 - yi), c]).astype(jnp.int32)
    slabs = lambda a: a.reshape(N_CHIPS, IN_W // N_CHIPS, D_MODEL // 2)
    theirs = slabs(_dwt_early(st["dhmt"], st["dhgt"], st["xb"], 1 - c1, c1, "dwt_theirs"))
    parts1 = [theirs, early["w_oa"].astype(BF16), early["w_ob"].astype(BF16),
              early["w_out"].reshape(N_CHIPS, SLAB_W, D_MODEL).astype(BF16)]
    my_loss = lax.dynamic_update_slice(jnp.zeros((N_DEV,) + LOSS_TILE, F32), jnp.broadcast_to(loss, (1,) + LOSS_TILE),
                                       (4 * xi + 2 * yi + c, 0, 0))
    sems0 = _pairs_start(parts1, my_loss, 1)
    mine = slabs(_dwt_early(st["dhmt"], st["dhgt"], st["xb"], c1, sems0[5], "dwt_mine"))
    parts1, recv1, all_loss = _pairs_wait(*sems0[:5], mine, 1)
    pairs1 = [_add_pair_tiled(mine, recv1[0]), *_add_pair_small(parts1[1:], recv1[1:], c1, "add_pair_early")]
    sems1 = _chips_start(pairs1, "chips_start_early")
    st["delta"] = st["delta"] + sems1[4][0, 0]
    dq, dk, dv = _local_attn_bwd(st)
    dhl, late = _local_tail(st, dq, dk, dv, dv)

    grads = {**early, **late}
    rep = jnp.concatenate([_rows8(grads[n]) for n in REPLICATED], axis=0)
    rep = jnp.pad(rep, ((0, N_CHIPS * REP_ROWS - rep.shape[0]), (0, 0))).reshape(N_CHIPS, REP_ROWS, D_MODEL)
    parts2 = [late["w_uq"].reshape(N_CHIPS, Q_RANK // N_CHIPS, HEADS * QK_DIM).astype(BF16), rep.astype(BF16),
              late["w_lat"].reshape(N_CHIPS, LAT_ROWS_PAD // N_CHIPS, D_MODEL)]
    pairs2 = _add_pair_small(parts2, _exchange_pairs(parts2, "exchange_pairs_late"), c1, "add_pair_late")
    sems2 = _chips_start(pairs2, "chips_start_late")
    dx = _dx(st["dr"], st["dhg"], st["dhm"], dhl, st["wt"], sems2[4])
    pairs2, landed2 = _chips_wait(*sems2[:4], dx, "chips_wait_late")
    pairs1, landed1 = _chips_wait(*sems1[:4], landed2[0], "chips_wait_early")
    sums = [_sum_chips_tiled(pairs1[0], landed1[0], idx, 128),
            *_sum_chips_small([*pairs1[1:], *pairs2], [*landed1[1:], *landed2], idx, (F32, BF16))]
    *shards, g_rep, g_lat = _share(sums[:-2], sums[-2:])
    loss = jnp.sum(all_loss[:, 0, 0])

    red = {n: s.reshape(w[n].shape) for n, s in zip(("w_oa", "w_ob", "w_out", "w_uq"), shards[1:])}
    g_rep = g_rep.reshape(N_CHIPS * REP_ROWS, D_MODEL)
    off = 0
    for n in REPLICATED:
        rows = _rows8(w[n]).shape[0]
        red[n] = g_rep[off:off + rows].reshape(-1)[:w[n].size].reshape(w[n].shape)
        off += rows
    owner = (2 * xi + yi == 0).astype(jnp.int32).reshape(1)
    gt, dt, mt, vt2 = _update_w_in(wt_shard, shards[0], mt_shard, vt_shard,
                                   g_lat.reshape(LAT_ROWS_PAD, D_MODEL).astype(F32), owner, 232)
    red["w_in"] = jnp.transpose(gt)
    small = [n for n in ORDER if n != "w_in"]
    as2d = lambda a: a.reshape(-1, a.shape[-1])
    ds, ms, vs = _update_small([as2d(w[n]) for n in small], [as2d(red[n]) for n in small],
                               [as2d(m[n]) for n in small], [as2d(v[n]) for n in small])
    delta, new_m, new_v = {"w_in": jnp.transpose(dt)}, {"w_in": jnp.transpose(mt)}, {"w_in": jnp.transpose(vt2)}
    for i, n in enumerate(small):
        delta[n], new_m[n], new_v[n] = (a[i].reshape(w[n].shape) for a in (ds, ms, vs))

    lead = lambda a: a[None]
    return (loss, dx[None], *[lead(red[n]) for n in ORDER], *[lead(delta[n]) for n in ORDER],
            *[lead(new_m[n]) for n in ORDER], *[lead(new_v[n]) for n in ORDER])
```

```python
import functools
import math

import jax
import jax.numpy as jnp
from jax import lax
from jax.experimental import pallas as pl
from jax.experimental.pallas import tpu as pltpu

F32 = jnp.float32
BF16 = jnp.bfloat16

D_MODEL = 1024
HEADS = 8
Q_RANK = 384
KV_RANK = 128
NOPE = 64
ROPE = 32
V_DIM = 64
QK_DIM = NOPE + ROPE
HEAD_PAD = 128
MLA_W = HEADS * V_DIM
SGU_W = 512
GROUPS = 8
CHUNK = 128
IN_W = 4640
RMS_EPS = 1e-6
LN_EPS = 1e-5
ALPHA = 2.0 ** 0.25
ROPE_THETA = 10000.0
SCALE = QK_DIM ** -0.5

GATE_W = 2 * D_MODEL
MID_W = 4 * SGU_W
LAT_W = Q_RANK + KV_RANK + HEAD_PAD
PAD_W = GATE_W + MID_W + LAT_W
LAT_COLS = Q_RANK + KV_RANK + ROPE
ROW_GATE = LAT_COLS + MID_W
LAT_ROWS_PAD = 704
N_SLABS = 4
SLAB_W = D_MODEL // N_SLABS

ROW_TILE = 256
MATMUL_ROW_TILE = 512
ATT_TQ = 2048
ATT_TK = 256
ATT_BWD_TQ = 512
ATT_BWD_TK = 256
LOG2E = 1.4426950408889634
LN2 = 0.6931471805599453
Q_SCALE = SCALE * LOG2E
VMEM_LIMIT = 56 * 1024 * 1024

ADAM_LR = 0.001
ADAM_B1 = 0.9
ADAM_B2 = 0.999
ADAM_EPS = 1e-08
ADAM_WD = 0.01
ADAM_STEP = 10


def _dot(a, b):
    return jnp.dot(a, b, preferred_element_type=F32)


def _dot_nt(a, b):
    return lax.dot_general(a, b, (((1,), (1,)), ((), ())), preferred_element_type=F32)


def _dot_tn(a, b):
    return lax.dot_general(a, b, (((0,), (0,)), ((), ())), preferred_element_type=F32)


def _sigmoid(z):
    return 0.5 * jnp.tanh(0.5 * z) + 0.5


_GELU_C = math.sqrt(2.0 / math.pi)


def _gelu_and_grad(x):
    x2 = x * x
    t = jnp.tanh(_GELU_C * (x + 0.044715 * x * x2))
    g = 0.5 * x * (1.0 + t)
    dg = 0.5 * (1.0 + t) + 0.5 * x * (1.0 - t * t) * (_GELU_C * (1.0 + 3.0 * 0.044715 * x2))
    return g, dg


def _silu_and_grad(z):
    s = _sigmoid(z)
    return z * s, s * (1.0 + z * (1.0 - s))


def _rope(xb, c, sl, sh):
    return xb * c + pltpu.roll(xb, 112, 1) * sl + pltpu.roll(xb, 16, 1) * sh


def _rope_t(dy, c, sl, sh):
    return dy * c + pltpu.roll(dy * sl, 16, 1) + pltpu.roll(dy * sh, 112, 1)


def _params(sem=("arbitrary",)):
    return pltpu.CompilerParams(dimension_semantics=sem, vmem_limit_bytes=VMEM_LIMIT)


def _row_spec(tile, width):
    return pl.BlockSpec((tile, width), lambda i: (i, 0))


def _full_spec(shape):
    nd = len(shape)
    return pl.BlockSpec(shape, lambda i: (0,) * nd)


def _kpe_rows(wt_ref):
    z = lambda n: jnp.zeros((n, D_MODEL), BF16)
    return jnp.concatenate([z(NOPE), wt_ref[Q_RANK + KV_RANK:LAT_COLS, :], z(HEAD_PAD - QK_DIM)], axis=0)


def _fwd_rest(xb, wt, b_g, b_m):
    s = xb.shape[0]
    ts = MATMUL_ROW_TILE

    def body(xb_ref, wt_ref, bg_ref, bm_ref, hg_ref, hm_ref):
        xb_ = xb_ref[...]
        hg_ref[...] = _dot_nt(xb_, wt_ref[ROW_GATE:IN_W, :]) + bg_ref[...]
        hm_ref[...] = _dot_nt(xb_, wt_ref[LAT_COLS:ROW_GATE, :]) + bm_ref[...]

    return pl.pallas_call(
        body, name="fwd_rest", grid=(s // ts,),
        in_specs=[_row_spec(ts, D_MODEL), _full_spec(wt.shape), _full_spec(b_g.shape), _full_spec(b_m.shape)],
        out_specs=[_row_spec(ts, GATE_W), _row_spec(ts, MID_W)],
        out_shape=[jax.ShapeDtypeStruct((s, GATE_W), F32), jax.ShapeDtypeStruct((s, MID_W), F32)],
        compiler_params=_params(),
    )(xb, wt, b_g, b_m)


def _fwd_lat(x, wlat, b_l, g_q, wuq, g_kv, wk, wv, rc, rsl, rsh, after):
    s = x.shape[0]
    ts = ROW_TILE

    def body(x_ref, wt_ref, bl_ref, gq_ref, wuq_ref, gkv_ref, wk_ref, wv_ref, rc_ref, rsl_ref,
             rsh_ref, after_ref, hl_ref, q_ref, k_ref, v_ref, xb_ref, qt_ref, kt_ref, vt_ref, xb2_ref):
        xb = x_ref[...].astype(BF16)
        xb_ref[...] = xb
        xb2_ref[0] = xb[:, :D_MODEL // 2]
        xb2_ref[1] = xb[:, D_MODEL // 2:]
        hl = jnp.concatenate([_dot_nt(xb, wt_ref[0:Q_RANK + KV_RANK, :]), _dot_nt(xb, _kpe_rows(wt_ref))],
                             axis=1) + bl_ref[...]
        hl_ref[...] = hl
        c, sl, sh = rc_ref[...], rsl_ref[...], rsh_ref[...]
        cq = hl[:, :Q_RANK]
        cqn = cq * lax.rsqrt(jnp.mean(cq * cq, axis=-1, keepdims=True) + RMS_EPS) * gq_ref[...]
        q = _dot(cqn.astype(BF16), wuq_ref[...])
        ckv = hl[:, Q_RANK:Q_RANK + KV_RANK]
        ckvn = (ckv * lax.rsqrt(jnp.mean(ckv * ckv, axis=-1, keepdims=True) + RMS_EPS) * gkv_ref[...]).astype(BF16)
        k = _dot(ckvn, wk_ref[...])
        vb = _dot(ckvn, wv_ref[...]).astype(BF16)
        v_ref[...] = vb
        vt_ref[...] = vb.T
        kpe = _rope(hl[:, Q_RANK + KV_RANK:], c, sl, sh)
        for hd in range(HEADS):
            lanes = slice(hd * HEAD_PAD, (hd + 1) * HEAD_PAD)
            qb = (_rope(q[:, lanes], c, sl, sh) * Q_SCALE).astype(BF16)
            kb = (k[:, lanes] + kpe).astype(BF16)
            q_ref[:, lanes] = qb
            k_ref[:, lanes] = kb
            qt_ref[lanes, :] = qb.T
            kt_ref[lanes, :] = kb.T

    qk_w = HEADS * HEAD_PAD
    col_spec = lambda rows: pl.BlockSpec((rows, ts), lambda i: (0, i))
    return pl.pallas_call(
        body, name="fwd_lat", grid=(s // ts,),
        in_specs=[_row_spec(ts, D_MODEL), _full_spec(wlat.shape),
                  _full_spec(b_l.shape), _full_spec(g_q.shape),
                  _full_spec(wuq.shape), _full_spec(g_kv.shape), _full_spec(wk.shape), _full_spec(wv.shape),
                  _row_spec(ts, HEAD_PAD), _row_spec(ts, HEAD_PAD), _row_spec(ts, HEAD_PAD),
                  pl.BlockSpec(memory_space=pl.ANY)],
        out_specs=[_row_spec(ts, LAT_W), _row_spec(ts, qk_w),
                   _row_spec(ts, qk_w), _row_spec(ts, MLA_W), _row_spec(ts, D_MODEL), col_spec(qk_w), col_spec(qk_w),
                   col_spec(MLA_W), pl.BlockSpec((2, ts, D_MODEL // 2), lambda i: (0, i, 0))],
        out_shape=[jax.ShapeDtypeStruct((s, LAT_W), F32), jax.ShapeDtypeStruct((s, qk_w), BF16),
                   jax.ShapeDtypeStruct((s, qk_w), BF16), jax.ShapeDtypeStruct((s, MLA_W), BF16),
                   jax.ShapeDtypeStruct((s, D_MODEL), BF16), jax.ShapeDtypeStruct((qk_w, s), BF16),
                   jax.ShapeDtypeStruct((qk_w, s), BF16), jax.ShapeDtypeStruct((MLA_W, s), BF16),
                   jax.ShapeDtypeStruct((2, s, D_MODEL // 2), BF16)],
        compiler_params=_params(),
    )(x, wlat, b_l, g_q, wuq, g_kv, wk, wv, rc, rsl, rsh, after)


def _attn_fwd(qt, k, vt):
    s = k.shape[0]
    tq, tk = ATT_TQ, ATT_TK
    r = tq // tk
    pairs = HEADS // 2

    def body(qt_ref, k_ref, vt_ref, o_ref, lse_ref):
        i = pl.program_id(1)
        qts = [qt_ref[hh * HEAD_PAD:(hh + 1) * HEAD_PAD, :] for hh in range(2)]

        def scores(j, lo):
            koff = pl.multiple_of(j * tk, tk)
            return tuple(_dot(k_ref[pl.ds(koff, tk), hh * HEAD_PAD:(hh + 1) * HEAD_PAD], qts[hh][:, lo:])
                         for hh in range(2))

        def weighted(j, ps):
            koff = pl.multiple_of(j * tk, tk)
            return tuple(_dot(vt_ref[hh * V_DIM:(hh + 1) * V_DIM, pl.ds(koff, tk)], ps[hh]) for hh in range(2))

        def from_lane(full, lo, part):
            return part if lo == 0 else jnp.concatenate([full[:, :lo], part], axis=1)

        def step(j, carry, diag, last):
            st, ps, stats = carry
            lo = 0 if diag is None else diag * tk
            lo_prev = 0 if not diag else (diag - 1) * tk
            st_next = None if last else scores(j + 1, 0 if diag is None else lo + tk)
            pvs = weighted(jnp.maximum(j - 1, 0), ps)
            new_ps, new_stats = [], []
            for hh in range(2):
                m, l, acc = stats[hh]
                s_ = st[hh]
                if diag is not None:
                    krow = lax.broadcasted_iota(jnp.int32, s_.shape, 0)
                    qcol = lax.broadcasted_iota(jnp.int32, s_.shape, 1)
                    s_ = jnp.where(krow <= qcol, s_, -jnp.inf)
                acc = from_lane(acc, lo_prev, acc[:, lo_prev:] + pvs[hh])
                m_old = m[:, lo:]
                m_new = jnp.maximum(m_old, jnp.max(s_, axis=0, keepdims=True))
                a = jnp.exp2(m_old - m_new)
                p = jnp.exp2(s_ - m_new)
                new_stats.append((from_lane(m, lo, m_new),
                                  from_lane(l, lo, a * l[:, lo:] + jnp.sum(p, axis=0, keepdims=True)),
                                  from_lane(acc, lo, a * acc[:, lo:])))
                new_ps.append(p.astype(BF16))
            return st_next, tuple(new_ps), tuple(new_stats)

        one = (jnp.full((1, tq), -jnp.inf, F32), jnp.zeros((1, tq), F32), jnp.zeros((V_DIM, tq), F32))
        zero_p = jnp.zeros((tk, tq), BF16)
        nfull = i * r
        carry = lax.fori_loop(0, nfull, functools.partial(step, diag=None, last=False),
                              (scores(0, 0), (zero_p, zero_p), (one, one)))
        for d in range(r):
            carry = step(nfull + d, carry, d, d == r - 1)
        _, ps, stats = carry
        pvs = weighted(nfull + r - 1, ps)
        lo = (r - 1) * tk
        ot = jnp.concatenate([from_lane(stats[hh][2], lo, stats[hh][2][:, lo:] + pvs[hh]) / stats[hh][1]
                              for hh in range(2)], axis=0)
        o_ref[...] = ot.T
        lse = [stats[hh][0] + jnp.log(stats[hh][1]) * LOG2E for hh in range(2)]
        lse_ref[...] = jnp.concatenate(lse + [jnp.zeros((6, tq), F32)], axis=0)

    return pl.pallas_call(
        body, name="attn_fwd", grid=(pairs, s // tq),
        in_specs=[pl.BlockSpec((2 * HEAD_PAD, tq), lambda p, i: (p, i)),
                  pl.BlockSpec((s, 2 * HEAD_PAD), lambda p, i: (0, p)),
                  pl.BlockSpec((2 * V_DIM, s), lambda p, i: (p, 0))],
        out_specs=[pl.BlockSpec((tq, 2 * V_DIM), lambda p, i: (i, p)),
                   pl.BlockSpec((None, 8, tq), lambda p, i: (p, 0, i))],
        out_shape=[jax.ShapeDtypeStruct((s, MLA_W), F32), jax.ShapeDtypeStruct((pairs, 8, s), F32)],
        compiler_params=_params(("arbitrary", "arbitrary")),
    )(qt, k, vt)


def _attn_bwd(q, qt, k, kt, v, do, dot, lse, delta):
    s = k.shape[0]
    tk = ATT_BWD_TK
    nk = s // tk
    pairs = HEADS // 2

    def body(q_ref, qt_ref, k_ref, kt_ref, v_ref, do_ref, dot_ref, lse_ref, dl_ref, dqt_ref, dk_ref, dv_ref):
        krow = lax.broadcasted_iota(jnp.int32, (tk, tk), 0)
        qcol = lax.broadcasted_iota(jnp.int32, (tk, tk), 1)
        lane = lax.broadcasted_iota(jnp.int32, (tk, 2 * V_DIM), 1)
        drow = lax.broadcasted_iota(jnp.int32, (2 * V_DIM, s), 0)
        dotb = dot_ref[...]
        dots = [jnp.where((drow < V_DIM) if hh == 0 else (drow >= V_DIM), dotb, jnp.zeros_like(dotb))
                for hh in range(2)]
        for j in range(nk):
            lo = j * tk
            vb = v_ref[lo:lo + tk, :]
            dob = do_ref[lo:, :]
            dvs = []
            for hh in range(2):
                rows = slice(hh * HEAD_PAD, (hh + 1) * HEAD_PAD)
                st = _dot(k_ref[lo:lo + tk, rows], qt_ref[rows, lo:])
                diag = jnp.where(krow <= qcol, st[:, :tk], -jnp.inf)
                st = diag if j == nk - 1 else jnp.concatenate([diag, st[:, tk:]], axis=1)
                p = jnp.exp2(st - lse_ref[hh:hh + 1, lo:])
                dpt = _dot(vb, dots[hh][:, lo:])
                dst = (p * (dpt - dl_ref[hh:hh + 1, lo:])).astype(BF16)
                dvs.append(_dot(p.astype(BF16), dob))
                dk_ref[lo:lo + tk, rows] = _dot(dst, q_ref[lo:, rows]) * LN2
                dqt = _dot(kt_ref[rows, lo:lo + tk], dst)
                if j == 0:
                    dqt_ref[rows, :] = dqt
                else:
                    dqt_ref[rows, lo:] += dqt
            dv_ref[lo:lo + tk, :] = jnp.where(lane < V_DIM, dvs[0], dvs[1])
        dqt_ref[...] = dqt_ref[...] * SCALE

    pair_rows = lambda w: pl.BlockSpec((s, w), lambda p: (0, p))
    pair_cols = lambda w: pl.BlockSpec((w, s), lambda p: (p, 0))
    stats = pl.BlockSpec((None, 8, s), lambda p: (p, 0, 0))
    return pl.pallas_call(
        body, name="attn_bwd", grid=(pairs,),
        in_specs=[pair_rows(2 * HEAD_PAD), pair_cols(2 * HEAD_PAD), pair_rows(2 * HEAD_PAD), pair_cols(2 * HEAD_PAD),
                  pair_rows(2 * V_DIM), pair_rows(2 * V_DIM), pair_cols(2 * V_DIM), stats, stats],
        out_specs=[pair_cols(2 * HEAD_PAD), pair_rows(2 * HEAD_PAD), pair_rows(2 * V_DIM)],
        out_shape=[jax.ShapeDtypeStruct((HEADS * HEAD_PAD, s), F32), jax.ShapeDtypeStruct((s, HEADS * HEAD_PAD), F32),
                   jax.ShapeDtypeStruct((s, MLA_W), F32)],
        compiler_params=_params(("arbitrary",)),
    )(q, qt, k, kt, v, do, dot, lse, delta)


def _attn_bwd_dynamic(q, qt, k, kt, v, do, dot, lse, delta):
    s = k.shape[0]
    tq, tk = ATT_BWD_TQ, ATT_BWD_TK
    r = tq // tk
    nq = s // tq
    nk = s // tk
    pairs = HEADS // 2

    def body(q_ref, qt_ref, k_ref, kt_ref, v_ref, do_ref, dot_ref, lse_ref, dl_ref, dqt_ref, dk_ref, dv_ref):
        j = pl.program_id(1)
        krow = lax.broadcasted_iota(jnp.int32, (tk, tq), 0)
        qcol = lax.broadcasted_iota(jnp.int32, (tk, tq), 1)
        lane = lax.broadcasted_iota(jnp.int32, (tk, 2 * V_DIM), 1)
        drow = lax.broadcasted_iota(jnp.int32, (2 * V_DIM, tq), 0)

        @pl.when(j == 0)
        def _():
            dqt_ref[...] = jnp.zeros_like(dqt_ref)

        koff = pl.multiple_of(j * tk, tk)
        vb = v_ref[pl.ds(koff, tk), :]
        kbs = [k_ref[pl.ds(koff, tk), hh * HEAD_PAD:(hh + 1) * HEAD_PAD] for hh in range(2)]
        ktbs = [kt_ref[hh * HEAD_PAD:(hh + 1) * HEAD_PAD, pl.ds(koff, tk)] for hh in range(2)]
        i0 = j // r

        def front(i):
            qoff = pl.multiple_of(i * tq, tq)
            dotb = dot_ref[:, pl.ds(qoff, tq)]
            out = []
            for hh in range(2):
                mine = (drow < V_DIM) if hh == 0 else (drow >= V_DIM)
                st = _dot(kbs[hh], qt_ref[hh * HEAD_PAD:(hh + 1) * HEAD_PAD, pl.ds(qoff, tq)])
                out.append((st, _dot(vb, jnp.where(mine, dotb, jnp.zeros_like(dotb)))))
            return tuple(out)

        def middle(i, tiles, diag):
            qoff = pl.multiple_of(i * tq, tq)
            out = []
            for hh in range(2):
                st, dpt = tiles[hh]
                if diag:
                    st = jnp.where(krow + (j - i0 * r) * tk <= qcol, st, -jnp.inf)
                p = jnp.exp2(st - lse_ref[hh:hh + 1, pl.ds(qoff, tq)])
                out.append((p.astype(BF16), (p * (dpt - dl_ref[hh:hh + 1, pl.ds(qoff, tq)])).astype(BF16)))
            return tuple(out)

        def back(i, pd, accs):
            qoff = pl.multiple_of(i * tq, tq)
            dob = do_ref[pl.ds(qoff, tq), :]
            out = []
            for hh in range(2):
                rows = slice(hh * HEAD_PAD, (hh + 1) * HEAD_PAD)
                p, dst = pd[hh]
                dk_acc, dv_acc = accs[hh]
                dv_acc = dv_acc + _dot(p, dob)
                dk_acc = dk_acc + _dot(dst, q_ref[pl.ds(qoff, tq), rows])
                dqt_ref[rows, pl.ds(qoff, tq)] += _dot(ktbs[hh], dst)
                out.append((dk_acc, dv_acc))
            return tuple(out)

        def step(i, accs, diag):
            return back(i, middle(i, front(i), diag), accs)

        zero_acc = (jnp.zeros((tk, HEAD_PAD), F32), jnp.zeros((tk, 2 * V_DIM), F32))
        accs = step(i0, (zero_acc, zero_acc), True)
        accs = lax.fori_loop(i0 + 1, nq, functools.partial(step, diag=False), accs)
        for hh in range(2):
            dk_ref[:, hh * HEAD_PAD:(hh + 1) * HEAD_PAD] = accs[hh][0] * LN2
        dv_ref[...] = jnp.where(lane < V_DIM, accs[0][1], accs[1][1])

        @pl.when(j == nk - 1)
        def _():
            dqt_ref[...] = dqt_ref[...] * SCALE

    pair_rows = lambda w: pl.BlockSpec((s, w), lambda p, j: (0, p))
    pair_cols = lambda w: pl.BlockSpec((w, s), lambda p, j: (p, 0))
    stats = pl.BlockSpec((None, 8, s), lambda p, j: (p, 0, 0))
    return pl.pallas_call(
        body, name="attn_bwd", grid=(pairs, nk),
        in_specs=[pair_rows(2 * HEAD_PAD), pair_cols(2 * HEAD_PAD), pair_rows(2 * HEAD_PAD), pair_cols(2 * HEAD_PAD),
                  pair_rows(2 * V_DIM), pair_rows(2 * V_DIM), pair_cols(2 * V_DIM), stats, stats],
        out_specs=[pair_cols(2 * HEAD_PAD),
                   pl.BlockSpec((tk, 2 * HEAD_PAD), lambda p, j: (j, p)),
                   pl.BlockSpec((tk, 2 * V_DIM), lambda p, j: (j, p))],
        out_shape=[jax.ShapeDtypeStruct((HEADS * HEAD_PAD, s), F32), jax.ShapeDtypeStruct((s, HEADS * HEAD_PAD), F32),
                   jax.ShapeDtypeStruct((s, MLA_W), F32)],
        compiler_params=_params(("arbitrary", "arbitrary")),
    )(q, qt, k, kt, v, do, dot, lse, delta)


def _split3(a):
    hi = a.astype(BF16)
    r1 = a - hi.astype(F32)
    mid = r1.astype(BF16)
    lo = (r1 - mid.astype(F32)).astype(BF16)
    return hi, mid, lo


def _mid(x, tgt, o, hm, hg, woa, wob, wout, ln_g, ln_b, sg_g, sg_b, w_s, bsb):
    s = x.shape[0]
    ts = ROW_TILE
    nsteps = s // ts
    nch = ts // CHUNK
    npair = GROUPS // 2

    def body(x_ref, t_ref, o_ref, hm_ref, hg_ref, woa_ref, wob_ref, wout_ref, lng_ref, lnb_ref, sgg_ref, sgb_ref,
             ws_ref, bsb_ref,
             dr_ref, dhg_ref, dhm_ref, do_ref, dot_ref, dl_ref, dhgt_ref, dhmt_ref,
             dwout_ref, dwoa_ref, dwob_ref, dws_ref, dbs_ref, dlng_ref, dlnb_ref, dsgg_ref, dsgb_ref, loss_ref,
             dbg_ref, dbm_ref, dbacc_ref):
        i = pl.program_id(0)

        @pl.when(i == 0)
        def _():
            for r in (dwout_ref, dwoa_ref, dwob_ref, dws_ref, dlng_ref, dlnb_ref, dsgg_ref, dsgb_ref, loss_ref,
                      dbg_ref, dbm_ref, dbacc_ref):
                r[...] = jnp.zeros_like(r)

        def emit(ref, tref, bref, lo, val):
            vb = val.astype(BF16)
            n = val.shape[1]
            ref[:, lo:lo + n] = vb
            tref[lo:lo + n, :] = vb.T
            bref[:, lo:lo + n] += jnp.sum(val, axis=0, keepdims=True)

        lane = lax.broadcasted_iota(jnp.int32, (CHUNK, CHUNK), 1)
        left = lane < V_DIM
        tril = lax.broadcasted_iota(jnp.int32, (CHUNK, CHUNK), 0) >= lane
        ms = [jnp.where(tril, ws_ref[g], 0.0).astype(BF16) for g in range(GROUPS)]

        z_a = hm_ref[:, 0:SGU_W]
        u = hm_ref[:, SGU_W:2 * SGU_W]
        v = hm_ref[:, 2 * SGU_W:3 * SGU_W]
        z_b = hm_ref[:, 3 * SGU_W:4 * SGU_W]
        o = o_ref[...]
        sa, dsa = _silu_and_grad(z_a)
        y_a = (o * sa).astype(BF16)
        gu, dgu = _gelu_and_grad(u)
        gv, dgv = _gelu_and_grad(v)
        mu = jnp.mean(gv, axis=-1, keepdims=True)
        vc = gv - mu
        rstd_v = lax.rsqrt(jnp.mean(vc * vc, axis=-1, keepdims=True) + LN_EPS)
        vhat = vc * rstd_v
        vn = (vhat * sgg_ref[...] + sgb_ref[...]).astype(BF16)
        rows = []
        for c in range(nch):
            blocks = []
            for p in range(npair):
                blk = vn[c * CHUNK:(c + 1) * CHUNK, p * CHUNK:(p + 1) * CHUNK]
                blocks.append(jnp.where(left, _dot(ms[2 * p], blk), _dot(ms[2 * p + 1], blk)))
            rows.append(jnp.concatenate(blocks, axis=1) + bsb_ref[...])
        mixed = jnp.concatenate(rows, axis=0)
        sgu = gu * mixed
        sb, dsb = _silu_and_grad(z_b)
        y_b = (sgu * sb).astype(BF16)
        pa = jnp.concatenate([_dot(y_a, woa_ref[k]) for k in range(N_SLABS)], axis=1)
        pb = jnp.concatenate([_dot(y_b, wob_ref[k]) for k in range(N_SLABS)], axis=1)
        sga = _sigmoid(hg_ref[:, :D_MODEL])
        sgb = _sigmoid(hg_ref[:, D_MODEL:])
        m2 = (sga * pa + sgb * pb).astype(BF16)
        r = ALPHA * x_ref[...] + _dot(m2, wout_ref[...])
        rmu = jnp.mean(r, axis=-1, keepdims=True)
        rc = r - rmu
        rstd = lax.rsqrt(jnp.mean(rc * rc, axis=-1, keepdims=True) + LN_EPS)
        xhat = rc * rstd
        y = xhat * lng_ref[...] + lnb_ref[...]
        err = y - t_ref[...]
        loss_ref[...] += jnp.full(loss_ref.shape, 0.5 / D_MODEL, F32) * jnp.sum(err * err)

        dy = err * (1.0 / D_MODEL)
        dlng_ref[...] += jnp.sum(dy * xhat, axis=0, keepdims=True)
        dlnb_ref[...] += jnp.sum(dy, axis=0, keepdims=True)
        dxh = dy * lng_ref[...]
        dr = rstd * (dxh - jnp.mean(dxh, axis=-1, keepdims=True) - xhat * jnp.mean(dxh * xhat, axis=-1, keepdims=True))
        dr_ref[...] = dr
        drb = dr.astype(BF16)
        dwout_ref[...] += _dot_tn(m2, drb)
        dm2 = _dot_nt(drb, wout_ref[...])
        emit(dhg_ref, dhgt_ref, dbg_ref, 0, dm2 * pa * sga * (1.0 - sga))
        emit(dhg_ref, dhgt_ref, dbg_ref, D_MODEL, dm2 * pb * sgb * (1.0 - sgb))
        dpa = (dm2 * sga).astype(BF16)
        dpb = (dm2 * sgb).astype(BF16)
        dy_a = jnp.zeros((ts, MLA_W), F32)
        dy_b = jnp.zeros((ts, SGU_W), F32)
        y_at, y_bt = y_a.T, y_b.T
        for k in range(N_SLABS):
            cols = slice(k * SLAB_W, (k + 1) * SLAB_W)
            dwoa_ref[k] += _dot(y_at, dpa[:, cols])
            dwob_ref[k] += _dot(y_bt, dpb[:, cols])
            dy_a = dy_a + _dot_nt(dpa[:, cols], woa_ref[k])
            dy_b = dy_b + _dot_nt(dpb[:, cols], wob_ref[k])
        dob = (dy_a * sa).astype(BF16)
        do_ref[...] = dob
        dot_ref[...] = dob.T
        head = (lax.broadcasted_iota(jnp.int32, (HEADS, MLA_W), 1) // V_DIM
                == lax.broadcasted_iota(jnp.int32, (HEADS, MLA_W), 0)).astype(BF16)
        dl_ref[...] = sum(_dot_nt(head, term) for term in _split3(dob.astype(F32) * o))
        emit(dhm_ref, dhmt_ref, dbm_ref, 0, dy_a * o * dsa)
        dsg = dy_b * sb
        emit(dhm_ref, dhmt_ref, dbm_ref, 3 * SGU_W, dy_b * sgu * dsb)
        emit(dhm_ref, dhmt_ref, dbm_ref, SGU_W, dsg * mixed * dgu)
        dmixed = dsg * gu
        dvn_rows = []
        dbs_sum = jnp.zeros((CHUNK, SGU_W), F32)
        for c in range(nch):
            dm_c = dmixed[c * CHUNK:(c + 1) * CHUNK, :]
            dbs_sum = dbs_sum + dm_c
            blocks = []
            for p in range(npair):
                dmb = dm_c[:, p * CHUNK:(p + 1) * CHUNK].astype(BF16)
                blk = vn[c * CHUNK:(c + 1) * CHUNK, p * CHUNK:(p + 1) * CHUNK]
                blocks.append(jnp.where(left, _dot_tn(ms[2 * p], dmb), _dot_tn(ms[2 * p + 1], dmb)))
                zero = jnp.zeros_like(dmb)
                dws_ref[2 * p] += jnp.where(tril, _dot_nt(jnp.where(left, dmb, zero), blk), 0.0)
                dws_ref[2 * p + 1] += jnp.where(tril, _dot_nt(jnp.where(left, zero, dmb), blk), 0.0)
            dvn_rows.append(jnp.concatenate(blocks, axis=1))
        dbacc_ref[...] += dbs_sum
        dvn = jnp.concatenate(dvn_rows, axis=0)
        dsgg_ref[...] += jnp.sum(dvn * vhat, axis=0, keepdims=True)
        dsgb_ref[...] += jnp.sum(dvn, axis=0, keepdims=True)
        dvh = dvn * sgg_ref[...]
        dgv_in = rstd_v * (dvh - jnp.mean(dvh, axis=-1, keepdims=True)
                           - vhat * jnp.mean(dvh * vhat, axis=-1, keepdims=True))
        emit(dhm_ref, dhmt_ref, dbm_ref, 2 * SGU_W, dgv_in * dgv)

        @pl.when(i == nsteps - 1)
        def _():
            grp = (lax.broadcasted_iota(jnp.int32, (SGU_W, CHUNK), 0) // V_DIM
                   == lax.broadcasted_iota(jnp.int32, (SGU_W, CHUNK), 1)).astype(BF16)
            hi, mid, lo = _split3(dbacc_ref[...])
            dbs_ref[...] = _dot(hi, grp) + _dot(mid, grp) + _dot(lo, grp)

    acc_shapes = [(D_MODEL, D_MODEL), woa.shape, wob.shape, (GROUPS, CHUNK, CHUNK), (CHUNK, CHUNK),
                  (1, D_MODEL), (1, D_MODEL), (1, SGU_W), (1, SGU_W), (1, 128), (1, GATE_W), (1, MID_W)]
    col_spec = lambda rows: pl.BlockSpec((rows, ts), lambda i: (0, i))
    return pl.pallas_call(
        body, name="mid", grid=(nsteps,),
        in_specs=[_row_spec(ts, D_MODEL), _row_spec(ts, D_MODEL), _row_spec(ts, MLA_W), _row_spec(ts, MID_W),
                  _row_spec(ts, GATE_W), _full_spec(woa.shape), _full_spec(wob.shape), _full_spec(wout.shape),
                  _full_spec(ln_g.shape), _full_spec(ln_b.shape), _full_spec(sg_g.shape), _full_spec(sg_b.shape),
                  _full_spec(w_s.shape), _full_spec(bsb.shape)],
        out_specs=[_row_spec(ts, D_MODEL), _row_spec(ts, GATE_W), _row_spec(ts, MID_W), _row_spec(ts, MLA_W),
                   col_spec(MLA_W), col_spec(HEADS), col_spec(GATE_W), col_spec(MID_W)]
        + [_full_spec(sh) for sh in acc_shapes],
        out_shape=[jax.ShapeDtypeStruct((s, D_MODEL), F32), jax.ShapeDtypeStruct((s, GATE_W), BF16),
                   jax.ShapeDtypeStruct((s, MID_W), BF16), jax.ShapeDtypeStruct((s, MLA_W), BF16),
                   jax.ShapeDtypeStruct((MLA_W, s), BF16), jax.ShapeDtypeStruct((HEADS, s), F32),
                   jax.ShapeDtypeStruct((GATE_W, s), BF16), jax.ShapeDtypeStruct((MID_W, s), BF16)]
        + [jax.ShapeDtypeStruct(sh, F32) for sh in acc_shapes],
        scratch_shapes=[pltpu.VMEM((CHUNK, SGU_W), F32)],
        compiler_params=_params(),
    )(x, tgt, o, hm, hg, woa, wob, wout, ln_g, ln_b, sg_g, sg_b, w_s, bsb)


def _lat_bwd(dq, dk, dv, hl, rc, rsl, rsh, g_q, g_kv, wuq, wk, wv, after):
    s = dk.shape[0]
    ts = ROW_TILE
    qk_w = HEADS * HEAD_PAD

    def body(dq_ref, dk_ref, dv_ref, hl_ref, rc_ref, rsl_ref, rsh_ref, gq_ref, gkv_ref, wuq_ref, wk_ref, wv_ref,
             after_ref, dhl_ref, dhlt_ref, dwuq_ref, dwk_ref, dwv_ref, dgq_ref, dgkv_ref, dbl_ref):
        i = pl.program_id(0)

        @pl.when(i == 0)
        def _():
            for r in (dwuq_ref, dwk_ref, dwv_ref, dgq_ref, dgkv_ref, dbl_ref):
                r[...] = jnp.zeros_like(r)

        def emit(lo, val):
            vb = val.astype(BF16)
            n = val.shape[1]
            dhl_ref[:, lo:lo + n] = vb
            dhlt_ref[lo:lo + n, :] = vb.T
            dbl_ref[:, lo:lo + n] += jnp.sum(val, axis=0, keepdims=True)

        c, sl, sh = rc_ref[...], rsl_ref[...], rsh_ref[...]
        lane = lax.broadcasted_iota(jnp.int32, (ts, HEAD_PAD), 1)
        pe = (lane >= NOPE) & (lane < QK_DIM)
        dkpe = jnp.zeros((ts, HEAD_PAD), F32)
        dqu = []
        for hd in range(HEADS):
            lanes = slice(hd * HEAD_PAD, (hd + 1) * HEAD_PAD)
            dqu.append(_rope_t(dq_ref[lanes, :].T, c, sl, sh).astype(BF16))
            dkpe = dkpe + dk_ref[:, lanes]
        dqu = jnp.concatenate(dqu, axis=1)
        dkpe = _rope_t(jnp.where(pe, dkpe, 0.0), c, sl, sh)

        cq = hl_ref[:, :Q_RANK]
        rq = lax.rsqrt(jnp.mean(cq * cq, axis=-1, keepdims=True) + RMS_EPS)
        cqh = cq * rq
        cqn = (cqh * gq_ref[...]).astype(BF16)
        dwuq_ref[...] += _dot_tn(cqn, dqu)
        dcqn = _dot_nt(dqu, wuq_ref[...])
        dgq_ref[...] += jnp.sum(dcqn * cqh, axis=0, keepdims=True)
        dch = dcqn * gq_ref[...]
        emit(0, rq * (dch - cqh * jnp.mean(dch * cqh, axis=-1, keepdims=True)))

        ckv = hl_ref[:, Q_RANK:Q_RANK + KV_RANK]
        rk = lax.rsqrt(jnp.mean(ckv * ckv, axis=-1, keepdims=True) + RMS_EPS)
        ckh = ckv * rk
        ckn = (ckh * gkv_ref[...]).astype(BF16)
        dkb = dk_ref[...].astype(BF16)
        dvb = dv_ref[...].astype(BF16)
        dwk_ref[...] += _dot_tn(ckn, dkb)
        dwv_ref[...] += _dot_tn(ckn, dvb)
        dckn = _dot_nt(dkb, wk_ref[...]) + _dot_nt(dvb, wv_ref[...])
        dgkv_ref[...] += jnp.sum(dckn * ckh, axis=0, keepdims=True)
        dkh = dckn * gkv_ref[...]
        emit(Q_RANK, rk * (dkh - ckh * jnp.mean(dkh * ckh, axis=-1, keepdims=True)))
        emit(Q_RANK + KV_RANK, dkpe)

    acc_shapes = [wuq.shape, wk.shape, wv.shape, g_q.shape, g_kv.shape, (1, LAT_W)]
    return pl.pallas_call(
        body, name="lat_bwd", grid=(s // ts,),
        in_specs=[pl.BlockSpec((qk_w, ts), lambda i: (0, i)), _row_spec(ts, qk_w), _row_spec(ts, MLA_W),
                  _row_spec(ts, LAT_W), _row_spec(ts, HEAD_PAD), _row_spec(ts, HEAD_PAD), _row_spec(ts, HEAD_PAD),
                  _full_spec(g_q.shape), _full_spec(g_kv.shape), _full_spec(wuq.shape), _full_spec(wk.shape),
                  _full_spec(wv.shape), pl.BlockSpec(memory_space=pl.ANY)],
        out_specs=[_row_spec(ts, LAT_W), pl.BlockSpec((LAT_W, ts), lambda i: (0, i))]
        + [_full_spec(sh) for sh in acc_shapes],
        out_shape=[jax.ShapeDtypeStruct((s, LAT_W), BF16), jax.ShapeDtypeStruct((LAT_W, s), BF16)]
        + [jax.ShapeDtypeStruct(sh, F32) for sh in acc_shapes],
        compiler_params=_params(),
    )(dq, dk, dv, hl, rc, rsl, rsh, g_q, g_kv, wuq, wk, wv, after)


def _dx(dr, dhg, dhm, dhl, wt, after):
    s = dr.shape[0]
    ts = MATMUL_ROW_TILE

    def body(dr_ref, dhg_ref, dhm_ref, dhl_ref, wt_ref, after_ref, dx_ref):
        dx_ref[...] = (ALPHA * dr_ref[...]
                       + _dot(dhg_ref[...], wt_ref[ROW_GATE:IN_W, :])
                       + _dot(dhm_ref[...], wt_ref[LAT_COLS:ROW_GATE, :])
                       + _dot(dhl_ref[:, 0:Q_RANK + KV_RANK], wt_ref[0:Q_RANK + KV_RANK, :])
                       + _dot(dhl_ref[:, Q_RANK + KV_RANK:], _kpe_rows(wt_ref)))

    return pl.pallas_call(
        body, name="dx", grid=(s // ts,),
        in_specs=[_row_spec(ts, D_MODEL), _row_spec(ts, GATE_W), _row_spec(ts, MID_W), _row_spec(ts, LAT_W),
                  _full_spec(wt.shape), pl.BlockSpec(memory_space=pl.ANY)],
        out_specs=_row_spec(ts, D_MODEL),
        out_shape=jax.ShapeDtypeStruct((s, D_MODEL), F32),
        compiler_params=_params(),
    )(dr, dhg, dhm, dhl, wt, after)


def _dwt_early(dhmt, dhgt, xb, col, after, name):
    tn = 512
    nm, ng = MID_W // tn, GATE_W // tn
    s = dhmt.shape[1]
    hc = D_MODEL // 2

    ks = s // 2

    def body(col_ref, dma_ref, dmb_ref, dga_ref, dgb_ref, xba_ref, xbb_ref, after_ref, dw_ref):
        i = pl.program_id(0)

        @pl.when(i < nm)
        def _():
            dw_ref[...] = (_dot(dma_ref[...], xba_ref[...]) + _dot(dmb_ref[...], xbb_ref[...])).astype(BF16)

        @pl.when(i >= nm)
        def _():
            dw_ref[...] = (_dot(dga_ref[...], xba_ref[...]) + _dot(dgb_ref[...], xbb_ref[...])).astype(BF16)

    def dh_spec(first, part):
        if first:
            return pl.BlockSpec((tn, ks), lambda i, col_ref: (jnp.minimum(i, nm - 1), part))
        return pl.BlockSpec((tn, ks), lambda i, col_ref: (jnp.maximum(i - nm, 0), part))

    rows = pl.pallas_call(
        body, name=name,
        grid_spec=pltpu.PrefetchScalarGridSpec(
            num_scalar_prefetch=1, grid=(nm + ng,),
            in_specs=[dh_spec(True, 0), dh_spec(True, 1), dh_spec(False, 0), dh_spec(False, 1),
                      pl.BlockSpec((None, ks, hc), lambda i, col_ref: (col_ref[0], 0, 0)),
                      pl.BlockSpec((None, ks, hc), lambda i, col_ref: (col_ref[0], 1, 0)),
                      pl.BlockSpec(memory_space=pl.ANY)],
            out_specs=pl.BlockSpec((pl.Element(tn), pl.Element(hc)),
                                   lambda i, col_ref: (pl.multiple_of(LAT_COLS + i * tn, 32), 0))),
        out_shape=jax.ShapeDtypeStruct((IN_W, hc), BF16),
        compiler_params=_params(),
    )(col, dhmt, dhmt, dhgt, dhgt, xb, xb, after)

    def zero(buf_ref, out_ref):
        out_ref[...] = jnp.zeros_like(out_ref)

    return pl.pallas_call(
        zero, name=name + "_zero_lat", grid=(1,), in_specs=[pl.BlockSpec(memory_space=pl.ANY)],
        out_specs=pl.BlockSpec((LAT_COLS, hc), lambda i: (0, 0)),
        out_shape=jax.ShapeDtypeStruct((IN_W, hc), BF16), input_output_aliases={0: 0},
    )(rows)


def _dwt_lat(dhlt, xb):
    n, s = dhlt.shape

    def body(dht_ref, xb_ref, dw_ref):
        dw = _dot(dht_ref[...], xb_ref[...]).astype(BF16)
        kpe = Q_RANK + KV_RANK + NOPE
        dw_ref[0:Q_RANK + KV_RANK, :] = dw[0:Q_RANK + KV_RANK]
        dw_ref[Q_RANK + KV_RANK:LAT_COLS, :] = dw[kpe:kpe + ROPE]
        dw_ref[LAT_COLS:, :] = jnp.zeros((LAT_ROWS_PAD - LAT_COLS, D_MODEL), BF16)

    return pl.pallas_call(
        body, name="dwt_lat", in_specs=[VMEM_SPEC, VMEM_SPEC], out_specs=VMEM_SPEC,
        out_shape=jax.ShapeDtypeStruct((LAT_ROWS_PAD, D_MODEL), BF16),
        compiler_params=pltpu.CompilerParams(vmem_limit_bytes=VMEM_LIMIT),
    )(dhlt, xb)


def _split_bias(b):
    z = lambda n: jnp.zeros((n,), b.dtype)
    lat = jnp.concatenate([b[:Q_RANK + KV_RANK], z(NOPE), b[Q_RANK + KV_RANK:LAT_COLS], z(HEAD_PAD - QK_DIM)])
    return b[None, ROW_GATE:], b[None, LAT_COLS:ROW_GATE], lat[None, :]


def _join_bias(g, m, l):
    kpe = Q_RANK + KV_RANK + NOPE
    return jnp.concatenate([l[0, :Q_RANK + KV_RANK], l[0, kpe:kpe + ROPE], m[0], g[0]])


def _rope_tables(positions):
    half = ROPE // 2
    inv_freq = ROPE_THETA ** (-jnp.arange(0, ROPE, 2, dtype=F32) / ROPE)
    ang = positions.astype(F32)[:, None] * inv_freq
    cos, sin = jnp.cos(ang), jnp.sin(ang)
    n = positions.shape[0]
    one, zero = jnp.ones((n, NOPE), F32), jnp.zeros((n, half), F32)
    tail1, tail0 = jnp.ones((n, HEAD_PAD - QK_DIM), F32), jnp.zeros((n, HEAD_PAD - QK_DIM), F32)
    z64 = jnp.zeros((n, NOPE), F32)
    rc = jnp.concatenate([one, cos, cos, tail1], axis=1)
    rsl = jnp.concatenate([z64, -sin, zero, tail0], axis=1)
    rsh = jnp.concatenate([z64, zero, sin, tail0], axis=1)
    return rc, rsl, rsh


def _local_attention(x, positions, wlat, b_in, g_q, w_uq, g_kv, w_ukv, after):
    rc, rsl, rsh = _rope_tables(positions)
    b_g, b_m, b_l = _split_bias(b_in)
    wuq = jnp.pad(w_uq, ((0, 0), (0, 0), (0, HEAD_PAD - QK_DIM))).reshape(Q_RANK, HEADS * HEAD_PAD).astype(BF16)
    wk = jnp.pad(w_ukv[:, :, :NOPE], ((0, 0), (0, 0), (0, HEAD_PAD - NOPE))).reshape(KV_RANK, HEADS * HEAD_PAD).astype(BF16)
    wv = w_ukv[:, :, NOPE:].reshape(KV_RANK, MLA_W).astype(BF16)
    gq2, gkv2 = g_q[None, :], g_kv[None, :]
    hl, q, k, v, xb, qt, kt, vt, xb2 = _fwd_lat(x, wlat, b_l, gq2, wuq, gkv2, wk, wv, rc, rsl, rsh, after)
    o, lse = _attn_fwd(qt, k, vt)
    return dict(q=q, qt=qt, k=k, kt=kt, v=v, o=o, lse=lse, hl=hl, rc=rc, rsl=rsl, rsh=rsh, gq2=gq2, gkv2=gkv2,
                wuq=wuq, wk=wk, wv=wv, xb=xb, xb2=xb2, b_g=b_g, b_m=b_m)


def _local_head(st, x, tgt, wt, w_oa, sg_g, sg_b, w_s, b_s, w_ob, w_out, ln_g, ln_b):
    q, qt, k, kt, v, o, lse, hl, xb = (st[n] for n in ("q", "qt", "k", "kt", "v", "o", "lse", "hl", "xb"))
    rc, rsl, rsh, gq2, gkv2, wuq, wk, wv = (st[n] for n in ("rc", "rsl", "rsh", "gq2", "gkv2", "wuq", "wk", "wv"))
    bsb = jnp.repeat(b_s.T, V_DIM, axis=1)
    hg, hm = _fwd_rest(xb, wt, st["b_g"], st["b_m"])
    (dr, dhg, dhm, do, dot, delta, dhgt, dhmt, dwout, dwoa, dwob, dws, dbs, dlng, dlnb, dsgg, dsgb, loss, dbg,
     dbm) = _mid(x, tgt, o, hm, hg, w_oa, w_ob, w_out, ln_g[None, :], ln_b[None, :], sg_g[None, :], sg_b[None, :],
                 w_s, bsb)
    delta = jnp.pad(delta.reshape(HEADS // 2, 2, -1), ((0, 0), (0, 6), (0, 0)))
    early = {
        "w_oa": dwoa, "sgu_ln_g": dsgg[0], "sgu_ln_b": dsgb[0], "w_s": dws, "b_s": dbs[:, :GROUPS].T,
        "w_ob": dwob, "w_out": dwout, "ln_g": dlng[0], "ln_b": dlnb[0],
    }
    state = dict(q=q, qt=qt, k=k, kt=kt, v=v, do=do, dot=dot, lse=lse, delta=delta, hl=hl, rc=rc, rsl=rsl, rsh=rsh,
                 gq2=gq2, gkv2=gkv2, wuq=wuq, wk=wk, wv=wv, dr=dr, dhg=dhg, dhm=dhm, wt=wt, xb=xb, dbg=dbg, dbm=dbm,
                 dhgt=dhgt, dhmt=dhmt, xb2=st["xb2"])
    return loss, early, state


def _local_attn_bwd(st):
    return _attn_bwd(st["q"], st["qt"], st["k"], st["kt"], st["v"], st["do"], st["dot"], st["lse"], st["delta"])


def _local_tail(st, dq, dk, dv, after):
    dhl, dhlt, dwuq, dwk, dwv, dgq, dgkv, dbl = _lat_bwd(dq, dk, dv, st["hl"], st["rc"], st["rsl"], st["rsh"],
                                                         st["gq2"], st["gkv2"], st["wuq"], st["wk"], st["wv"], after)
    late = {
        "w_lat": _dwt_lat(dhlt, st["xb"]),
        "b_in": _join_bias(st["dbg"], st["dbm"], dbl),
        "g_q": dgq[0],
        "w_uq": dwuq.reshape(Q_RANK, HEADS, HEAD_PAD)[:, :, :QK_DIM],
        "g_kv": dgkv[0],
        "w_ukv": jnp.concatenate([dwk.reshape(KV_RANK, HEADS, HEAD_PAD)[:, :, :NOPE],
                                  dwv.reshape(KV_RANK, HEADS, V_DIM)], axis=2),
    }
    return dhl, late


def _local_step(x, positions, tgt, wt, b_in, g_q, w_uq, g_kv, w_ukv, w_oa, sg_g, sg_b, w_s, b_s, w_ob, w_out, ln_g,
                ln_b):
    st = _local_attention(x, positions, wt[:LAT_COLS], b_in, g_q, w_uq, g_kv, w_ukv, b_in)
    loss, early, st = _local_head(st, x, tgt, wt, w_oa, sg_g, sg_b, w_s, b_s, w_ob, w_out, ln_g, ln_b)
    dq, dk, dv = _local_attn_bwd(st)
    dhl, late = _local_tail(st, dq, dk, dv, dv)
    dx = _dx(st["dr"], st["dhg"], st["dhm"], dhl, st["wt"], dhl)
    grads = {**early, **late}
    halves = [_dwt_early(st["dhmt"], st["dhgt"], st["xb2"], jnp.full((1,), h, jnp.int32), dhl, "dwt_half%d" % h)
              for h in range(2)]
    grads["w_in"] = jnp.concatenate([grads.pop("w_lat")[:LAT_COLS], jnp.concatenate(halves, axis=1)[LAT_COLS:]],
                                    axis=0)
    return loss, dx, grads


MESH = pl.DeviceIdType.MESH
N_CHIPS = 4
HBM_SPEC = pl.BlockSpec(memory_space=pl.ANY)
HBM_SPEC_STRICT = pl.BlockSpec(memory_space=pltpu.HBM)
VMEM_SPEC = pl.BlockSpec(memory_space=pltpu.VMEM)

REP_ROWS = 80


def _rows8(a):
    flat = a.reshape(-1)
    n = -(-flat.shape[0] // (8 * D_MODEL)) * 8 * D_MODEL
    return jnp.pad(flat, (0, n - flat.shape[0])).reshape(-1, D_MODEL)


def _place():
    x, y, c = lax.axis_index("x"), lax.axis_index("y"), lax.axis_index("c")
    others = [(1 - x, y), (x, 1 - y), (1 - x, 1 - y)]
    return x, y, c, others


def _gather_weights(shards):
    n = len(shards)

    def body(*refs):
        ins, outs, bufs = refs[:n], refs[n:2 * n], refs[2 * n:3 * n]
        send_sems, recv_sems, local_sems = refs[3 * n:]
        x, y, c, others = _place()
        me = 2 * x + y
        sibling = (x, y, 1 - c)
        for src, buf in zip(ins, bufs):
            buf[...] = src[...].astype(BF16)
        own = [pltpu.make_async_copy(bufs[w], outs[w].at[me], local_sems.at[w]) for w in range(n)]
        for cp in own:
            cp.start()

        def part(w, chip, half):
            hc = shards[w].shape[1] // 2
            return outs[w].at[chip, :, pl.ds(half * hc, hc)]

        def sent(w, j):
            hc = shards[w].shape[1] // 2
            return pltpu.make_async_remote_copy(
                src_ref=bufs[w].at[:, pl.ds(c * hc, hc)], dst_ref=part(w, me, c),
                send_sem=send_sems.at[w * 3 + j], recv_sem=recv_sems.at[w * 3 + j],
                device_id=(*others[j], c), device_id_type=MESH)

        def landed(w, j):
            px, py = others[j]
            return pltpu.make_async_remote_copy(
                src_ref=part(w, 2 * px + py, c), dst_ref=part(w, 2 * px + py, c),
                send_sem=send_sems.at[w * 3 + j], recv_sem=recv_sems.at[w * 3 + j],
                device_id=(px, py, c), device_id_type=MESH)

        def passed(w, j, half):
            px, py = others[j]
            k = n * 3 + w * 3 + j
            return pltpu.make_async_remote_copy(
                src_ref=part(w, 2 * px + py, half), dst_ref=part(w, 2 * px + py, half),
                send_sem=send_sems.at[k], recv_sem=recv_sems.at[k], device_id=sibling, device_id_type=MESH)

        first = [sent(w, j) for w in range(n) for j in range(3)]
        for cp in first:
            cp.start()
        fwd = []
        for w in range(n):
            for j in range(3):
                landed(w, j).wait_recv()
                cp = passed(w, j, c)
                cp.start()
                fwd.append(cp)
        for w in range(n):
            for j in range(3):
                passed(w, j, 1 - c).wait_recv()
        for cp in first + fwd:
            cp.wait_send()
        for cp in own:
            cp.wait()

    return pl.pallas_call(
        body, name="gather_weights",
        in_specs=[VMEM_SPEC] * n, out_specs=[HBM_SPEC] * n,
        out_shape=[jax.ShapeDtypeStruct((N_CHIPS,) + s.shape, BF16) for s in shards],
        scratch_shapes=[pltpu.VMEM(s.shape, BF16) for s in shards]
        + [pltpu.SemaphoreType.DMA((6 * n,)), pltpu.SemaphoreType.DMA((6 * n,)), pltpu.SemaphoreType.DMA((n,))],
        compiler_params=pltpu.CompilerParams(vmem_limit_bytes=VMEM_LIMIT),
    )(*shards)


N_DEV = 8
LOSS_TILE = (8, 128)


def _gather_first(lat, uq):
    hl, hu = lat.shape[1] // 2, uq.shape[1] // 2

    def body(lat_ref, uq_ref, wlat_ref, guq_ref, lat_buf, uq_buf, send_sems, recv_sems, local_sems):
        x, y, c, others = _place()
        me = 2 * x + y
        sibling = (x, y, 1 - c)
        lat_buf[...] = lat_ref[...].astype(BF16)
        uq_buf[...] = uq_ref[...].astype(BF16)

        def copy(src, dst, k, to):
            return pltpu.make_async_remote_copy(src_ref=src, dst_ref=dst, send_sem=send_sems.at[k],
                                                recv_sem=recv_sems.at[k], device_id=to, device_id_type=MESH)

        def uq_part(chip, half):
            return guq_ref.at[chip, :, pl.ds(half * hu, hu)]

        def lat_part(half):
            return wlat_ref.at[:, pl.ds(half * hl, hl)]

        own = pltpu.make_async_copy(uq_buf, guq_ref.at[me], local_sems.at[0])
        own.start()
        first = [copy(uq_buf.at[:, pl.ds(c * hu, hu)], uq_part(me, c), j, (*others[j], c)) for j in range(3)]
        for cp in first:
            cp.start()

        @pl.when(me == 0)
        def _():
            mine = pltpu.make_async_copy(lat_buf, wlat_ref, local_sems.at[1])
            mine.start()
            cps = [copy(lat_buf.at[:, pl.ds(c * hl, hl)], lat_part(c), 6 + j, (*others[j], c)) for j in range(3)]
            for cp in cps:
                cp.start()
            for cp in cps:
                cp.wait_send()
            mine.wait()

        @pl.when(me != 0)
        def _():
            j0 = x + 2 * y - 1
            copy(lat_part(c), lat_part(c), 6 + j0, (0, 0, c)).wait_recv()
            fwd = copy(lat_part(c), lat_part(c), 9, sibling)
            fwd.start()
            copy(lat_part(1 - c), lat_part(1 - c), 9, sibling).wait_recv()
            fwd.wait_send()

        fwd = []
        for j, (px, py) in enumerate(others):
            chip = 2 * px + py
            copy(uq_part(chip, c), uq_part(chip, c), j, (px, py, c)).wait_recv()
            cp = copy(uq_part(chip, c), uq_part(chip, c), 3 + j, sibling)
            cp.start()
            fwd.append(cp)
        for j, (px, py) in enumerate(others):
            chip = 2 * px + py
            copy(uq_part(chip, 1 - c), uq_part(chip, 1 - c), 3 + j, sibling).wait_recv()
        for cp in first + fwd:
            cp.wait_send()
        own.wait()

    return pl.pallas_call(
        body, name="gather_first", in_specs=[VMEM_SPEC, VMEM_SPEC], out_specs=[HBM_SPEC, HBM_SPEC],
        out_shape=[jax.ShapeDtypeStruct(lat.shape, BF16), jax.ShapeDtypeStruct((N_CHIPS,) + uq.shape, BF16)],
        scratch_shapes=[pltpu.VMEM(lat.shape, BF16), pltpu.VMEM(uq.shape, BF16), pltpu.SemaphoreType.DMA((10,)),
                        pltpu.SemaphoreType.DMA((10,)), pltpu.SemaphoreType.DMA((2,))],
        compiler_params=pltpu.CompilerParams(vmem_limit_bytes=VMEM_LIMIT),
    )(lat, uq)


def _cast_own(shards, me, after):
    n = len(shards)

    def body(me_ref, *refs):
        for w in range(n):
            refs[n + 1 + w][...] = refs[w][...].astype(BF16)

    return pl.pallas_call(
        body, name="cast_own",
        grid_spec=pltpu.PrefetchScalarGridSpec(
            num_scalar_prefetch=1, grid=(1,),
            in_specs=[pl.BlockSpec(s.shape, lambda i, me_ref: (0, 0)) for s in shards]
            + [pl.BlockSpec(memory_space=pl.ANY)],
            out_specs=[pl.BlockSpec((None,) + s.shape, lambda i, me_ref: (me_ref[0], 0, 0)) for s in shards]),
        out_shape=[jax.ShapeDtypeStruct((N_CHIPS,) + s.shape, BF16) for s in shards],
        compiler_params=pltpu.CompilerParams(vmem_limit_bytes=VMEM_LIMIT),
    )(me, *shards, after)


def _cast_first(lat, uq, me):
    def body(me_ref, lat_ref, uq_ref, wlat_ref, guq_ref):
        wlat_ref[...] = lat_ref[...].astype(BF16)
        guq_ref[...] = uq_ref[...].astype(BF16)

    return pl.pallas_call(
        body, name="cast_first",
        grid_spec=pltpu.PrefetchScalarGridSpec(
            num_scalar_prefetch=1, grid=(1,),
            in_specs=[pl.BlockSpec(lat.shape, lambda i, me_ref: (0, 0)),
                      pl.BlockSpec(uq.shape, lambda i, me_ref: (0, 0))],
            out_specs=[pl.BlockSpec(lat.shape, lambda i, me_ref: (0, 0)),
                       pl.BlockSpec((None,) + uq.shape, lambda i, me_ref: (me_ref[0], 0, 0))]),
        out_shape=[jax.ShapeDtypeStruct(lat.shape, BF16), jax.ShapeDtypeStruct((N_CHIPS,) + uq.shape, BF16)],
    )(me, lat, uq)


def _first_copies(wlat_ref, guq_ref, send_sems, recv_sems, shapes):
    x, y, c, others = _place()
    me = 2 * x + y
    hl, hu = shapes[0][1] // 2, shapes[1][2] // 2
    lat_half = wlat_ref.at[:, pl.ds(c * hl, hl)]

    def copy(src, dst, k, to):
        return pltpu.make_async_remote_copy(src_ref=src, dst_ref=dst, send_sem=send_sems.at[k],
                                            recv_sem=recv_sems.at[k], device_id=to, device_id_type=MESH)

    def uq_half(chip):
        return guq_ref.at[chip, :, pl.ds(c * hu, hu)]

    lat_out = [copy(lat_half, lat_half, j, (*others[j], c)) for j in range(3)]
    uq_out = [copy(uq_half(me), uq_half(me), 3 + j, (*others[j], c)) for j in range(3)]
    j0 = jnp.maximum(x + 2 * y - 1, 0)
    lat_in = copy(lat_half, lat_half, j0, (0, 0, c))
    uq_in = [copy(uq_half(me), uq_half(2 * px + py), 3 + j, (px, py, c)) for j, (px, py) in enumerate(others)]
    return me, lat_out, uq_out, lat_in, uq_in


def _first_start(wlat, guq):
    shapes = (wlat.shape, guq.shape)

    def body(wlat_ref, guq_ref, send_sems, recv_sems, wlat_thru, guq_thru, token):
        me, lat_out, uq_out, _, _ = _first_copies(wlat_ref, guq_ref, send_sems, recv_sems, shapes)

        @pl.when(me == 0)
        def _():
            for cp in lat_out:
                cp.start()

        for cp in uq_out:
            cp.start()
        token[...] = jnp.zeros_like(token)

    outs = pl.pallas_call(
        body, name="first_start",
        out_shape=(pltpu.SemaphoreType.DMA((6,)), pltpu.SemaphoreType.DMA((6,)), pltpu.HBM(wlat.shape, BF16),
                   pltpu.HBM(guq.shape, BF16), jax.ShapeDtypeStruct(LOSS_TILE, F32)),
        in_specs=[HBM_SPEC_STRICT] * 2, out_specs=(SEM_SPEC, SEM_SPEC, HBM_SPEC_STRICT, HBM_SPEC_STRICT, VMEM_SPEC),
        input_output_aliases={0: 2, 1: 3},
        compiler_params=pltpu.CompilerParams(has_side_effects=SPLIT_EFFECT),
    )(pltpu.with_memory_space_constraint(wlat, pltpu.HBM), pltpu.with_memory_space_constraint(guq, pltpu.HBM))
    return outs


def _first_wait(send_sems, recv_sems, wlat, guq, after):
    shapes = (wlat.shape, guq.shape)

    def body(wlat_ref, guq_ref, send_sems, recv_sems, after_ref, wlat_out, guq_out):
        me, lat_out, uq_out, lat_in, uq_in = _first_copies(wlat_ref, guq_ref, send_sems, recv_sems, shapes)

        @pl.when(me == 0)
        def _():
            for cp in lat_out:
                cp.wait_send()

        @pl.when(me != 0)
        def _():
            lat_in.wait_recv()

        for cp in uq_out:
            cp.wait_send()
        for cp in uq_in:
            cp.wait_recv()

    return pl.pallas_call(
        body, name="first_wait", out_shape=(pltpu.HBM(wlat.shape, BF16), pltpu.HBM(guq.shape, BF16)),
        in_specs=[HBM_SPEC_STRICT, HBM_SPEC_STRICT, SEM_SPEC, SEM_SPEC, HBM_SPEC],
        out_specs=(HBM_SPEC_STRICT, HBM_SPEC_STRICT), input_output_aliases={0: 0, 1: 1},
        compiler_params=pltpu.CompilerParams(has_side_effects=SPLIT_EFFECT),
    )(wlat, guq, send_sems, recv_sems, after)


def _first_forward(wlat, guq):
    hl, hu = wlat.shape[1] // 2, guq.shape[2] // 2

    def body(wlat_in, guq_in, wlat_ref, guq_ref, send_sems, recv_sems):
        x, y, c, others = _place()
        me = 2 * x + y
        sibling = (x, y, 1 - c)

        def copy(part, k):
            return pltpu.make_async_remote_copy(src_ref=part, dst_ref=part, send_sem=send_sems.at[k],
                                                recv_sem=recv_sems.at[k], device_id=sibling, device_id_type=MESH)

        def uq_part(j, half):
            px, py = others[j]
            return guq_ref.at[2 * px + py, :, pl.ds(half * hu, hu)]

        cps = [copy(uq_part(j, c), j) for j in range(3)]
        for cp in cps:
            cp.start()

        @pl.when(me != 0)
        def _():
            mine = copy(wlat_ref.at[:, pl.ds(c * hl, hl)], 3)
            mine.start()
            copy(wlat_ref.at[:, pl.ds((1 - c) * hl, hl)], 3).wait_recv()
            mine.wait_send()

        for j in range(3):
            copy(uq_part(j, 1 - c), j).wait_recv()
        for cp in cps:
            cp.wait_send()

    return pl.pallas_call(
        body, name="first_forward", in_specs=[HBM_SPEC, HBM_SPEC], out_specs=[HBM_SPEC, HBM_SPEC],
        out_shape=[jax.ShapeDtypeStruct(wlat.shape, BF16), jax.ShapeDtypeStruct(guq.shape, BF16)],
        input_output_aliases={0: 0, 1: 1},
        scratch_shapes=[pltpu.SemaphoreType.DMA((4,)), pltpu.SemaphoreType.DMA((4,))],
    )(wlat, guq)


def _gather_start(bufs, after):
    n = len(bufs)

    def body(*refs):
        b_refs = refs[:n]
        send_sems, recv_sems, token = refs[n + 1], refs[n + 2], refs[-1]
        x, y, c, others = _place()
        me = 2 * x + y
        for w in range(n):
            hc = _half(bufs[w])
            mine = b_refs[w].at[me, :, pl.ds(c * hc, hc)]
            for j, (px, py) in enumerate(others):
                pltpu.make_async_remote_copy(
                    src_ref=mine, dst_ref=mine, send_sem=send_sems.at[3 * w + j], recv_sem=recv_sems.at[3 * w + j],
                    device_id=(px, py, c), device_id_type=MESH).start()
        token[...] = jnp.zeros_like(token)

    hbm = [pltpu.HBM(b.shape, BF16) for b in bufs]
    outs = pl.pallas_call(
        body, name="gather_start",
        out_shape=(pltpu.SemaphoreType.DMA((3 * n,)), pltpu.SemaphoreType.DMA((3 * n,)), *hbm,
                   jax.ShapeDtypeStruct(LOSS_TILE, F32)),
        in_specs=[HBM_SPEC_STRICT] * n + [HBM_SPEC],
        out_specs=(SEM_SPEC, SEM_SPEC, *[HBM_SPEC_STRICT] * n, VMEM_SPEC),
        input_output_aliases={i: 2 + i for i in range(n)},
        compiler_params=pltpu.CompilerParams(has_side_effects=SPLIT_EFFECT),
    )(*[pltpu.with_memory_space_constraint(b, pltpu.HBM) for b in bufs], after)
    return outs[0], outs[1], list(outs[2:2 + n]), outs[-1]


def _gather_wait(send_sems, recv_sems, bufs, after):
    n = len(bufs)

    def body(*refs):
        b_refs = refs[:n]
        send_sems, recv_sems = refs[n], refs[n + 1]
        x, y, c, others = _place()
        me = 2 * x + y
        for w in range(n):
            hc = _half(bufs[w])
            for j, (px, py) in enumerate(others):
                cp = pltpu.make_async_remote_copy(
                    src_ref=b_refs[w].at[me, :, pl.ds(c * hc, hc)],
                    dst_ref=b_refs[w].at[2 * px + py, :, pl.ds(c * hc, hc)],
                    send_sem=send_sems.at[3 * w + j], recv_sem=recv_sems.at[3 * w + j], device_id=(px, py, c),
                    device_id_type=MESH)
                cp.wait_send()
                cp.wait_recv()

    outs = pl.pallas_call(
        body, name="gather_wait", out_shape=tuple(pltpu.HBM(b.shape, b.dtype) for b in bufs),
        in_specs=[HBM_SPEC_STRICT] * n + [SEM_SPEC, SEM_SPEC, HBM_SPEC],
        out_specs=tuple([HBM_SPEC_STRICT] * n), input_output_aliases={i: i for i in range(n)},
        compiler_params=pltpu.CompilerParams(has_side_effects=SPLIT_EFFECT),
    )(*bufs, send_sems, recv_sems, after)
    return list(outs)


def _gather_finish(bufs):
    n = len(bufs)

    def body(*refs):
        b_refs = refs[n:2 * n]
        send_sems, recv_sems = refs[2 * n:]
        x, y, c, others = _place()
        cps = []
        for w in range(n):
            hc = _half(bufs[w])
            for j, (px, py) in enumerate(others):
                part = b_refs[w].at[2 * px + py, :, pl.ds(c * hc, hc)]
                cps.append(pltpu.make_async_remote_copy(
                    src_ref=part, dst_ref=part, send_sem=send_sems.at[3 * w + j], recv_sem=recv_sems.at[3 * w + j],
                    device_id=(x, y, 1 - c), device_id_type=MESH))
        for cp in cps:
            cp.start()
        for w in range(n):
            hc = _half(bufs[w])
            for j, (px, py) in enumerate(others):
                theirs = b_refs[w].at[2 * px + py, :, pl.ds((1 - c) * hc, hc)]
                pltpu.make_async_remote_copy(
                    src_ref=theirs, dst_ref=theirs, send_sem=send_sems.at[3 * w + j], recv_sem=recv_sems.at[3 * w + j],
                    device_id=(x, y, 1 - c), device_id_type=MESH).wait_recv()
        for cp in cps:
            cp.wait_send()

    return pl.pallas_call(
        body, name="gather_finish", in_specs=[HBM_SPEC] * n, out_specs=[HBM_SPEC] * n,
        out_shape=[jax.ShapeDtypeStruct(b.shape, b.dtype) for b in bufs],
        input_output_aliases={i: i for i in range(n)},
        scratch_shapes=[pltpu.SemaphoreType.DMA((3 * n,)), pltpu.SemaphoreType.DMA((3 * n,))],
    )(*bufs)


def _half(a):
    return a.shape[-1] // 2


def _exchange_pairs(parts, name):
    n = len(parts)

    def body(*refs):
        p_refs, r_refs = refs[:n], refs[n:2 * n]
        send_sems, recv_sems = refs[2 * n:]
        x, y, c, _ = _place()
        cps = []
        for w in range(n):
            h = _half(parts[w])
            cps.append(pltpu.make_async_remote_copy(
                src_ref=p_refs[w].at[:, :, pl.ds((1 - c) * h, h)], dst_ref=r_refs[w],
                send_sem=send_sems.at[w], recv_sem=recv_sems.at[w], device_id=(x, y, 1 - c), device_id_type=MESH))
        for cp in cps:
            cp.start()
        for cp in cps:
            cp.wait()

    return pl.pallas_call(
        body, name=name, in_specs=[HBM_SPEC] * n, out_specs=[HBM_SPEC] * n,
        out_shape=[jax.ShapeDtypeStruct((N_CHIPS, p.shape[1], _half(p)), BF16) for p in parts],
        scratch_shapes=[pltpu.SemaphoreType.DMA((n,)), pltpu.SemaphoreType.DMA((n,))],
    )(*parts)


def _sibling_part(ref, w, n_whole, shape, c):
    if w < n_whole:
        return ref
    h = shape[-1] // 2
    return ref.at[:, :, pl.ds((1 - c) * h, h)]


def _pairs_start(parts, all_loss, n_whole):
    n = len(parts)

    def body(*refs):
        p_refs, r_refs, loss_ref = refs[:n], refs[n:2 * n], refs[2 * n]
        send_sems, recv_sems, token = refs[2 * n + 1], refs[2 * n + 2], refs[-1]
        x, y, c, _ = _place()
        for w in range(n):
            h = _half(parts[w])
            pltpu.make_async_remote_copy(
                src_ref=_sibling_part(p_refs[w], w, n_whole, parts[w].shape, c), dst_ref=r_refs[w],
                send_sem=send_sems.at[w], recv_sem=recv_sems.at[w], device_id=(x, y, 1 - c),
                device_id_type=MESH).start()
        me = 4 * x + 2 * y + c
        for t in range(1, N_DEV):
            d = (me + t) % N_DEV
            pltpu.make_async_remote_copy(
                src_ref=loss_ref.at[me], dst_ref=loss_ref.at[me], send_sem=send_sems.at[n + t - 1],
                recv_sem=recv_sems.at[n + t - 1], device_id=(d // 4, (d // 2) % 2, d % 2), device_id_type=MESH).start()
        token[...] = jnp.zeros_like(token)

    lands = [pltpu.HBM(p.shape if w < n_whole else (N_CHIPS, p.shape[1], _half(p)), BF16)
             for w, p in enumerate(parts)]
    nsem = n + N_DEV - 1
    outs = pl.pallas_call(
        body, name="pairs_start",
        out_shape=(pltpu.SemaphoreType.DMA((nsem,)), pltpu.SemaphoreType.DMA((nsem,)),
                   *[pltpu.HBM(p.shape, p.dtype) for p in parts], *lands, pltpu.HBM(all_loss.shape, F32),
                   jax.ShapeDtypeStruct(LOSS_TILE, F32)),
        in_specs=[HBM_SPEC_STRICT] * (2 * n + 1),
        out_specs=(SEM_SPEC, SEM_SPEC, *[HBM_SPEC_STRICT] * (2 * n + 1), VMEM_SPEC),
        input_output_aliases={i: 2 + i for i in range(2 * n + 1)},
        compiler_params=pltpu.CompilerParams(has_side_effects=SPLIT_EFFECT),
    )(*[pltpu.with_memory_space_constraint(p, pltpu.HBM) for p in parts],
      *[pltpu.with_memory_space_constraint(lax.empty(l.shape, BF16), pltpu.HBM) for l in lands],
      pltpu.with_memory_space_constraint(all_loss, pltpu.HBM))
    return outs[0], outs[1], list(outs[2:2 + n]), list(outs[2 + n:2 + 2 * n]), outs[2 + 2 * n], outs[-1]


def _pairs_wait(send_sems, recv_sems, parts, lands, all_loss, after, n_whole):
    n = len(parts)

    def body(*refs):
        p_refs, r_refs, loss_ref = refs[:n], refs[n:2 * n], refs[2 * n]
        send_sems, recv_sems = refs[2 * n + 1], refs[2 * n + 2]
        x, y, c, _ = _place()
        for w in range(n):
            h = _half(parts[w])
            cp = pltpu.make_async_remote_copy(
                src_ref=_sibling_part(p_refs[w], w, n_whole, parts[w].shape, c), dst_ref=r_refs[w],
                send_sem=send_sems.at[w],
                recv_sem=recv_sems.at[w], device_id=(x, y, 1 - c), device_id_type=MESH)
            cp.wait_send()
            cp.wait_recv()
        me = 4 * x + 2 * y + c
        for t in range(1, N_DEV):
            d = (me + N_DEV - t) % N_DEV
            cp = pltpu.make_async_remote_copy(
                src_ref=loss_ref.at[me], dst_ref=loss_ref.at[d], send_sem=send_sems.at[n + t - 1],
                recv_sem=recv_sems.at[n + t - 1], device_id=(d // 4, (d // 2) % 2, d % 2), device_id_type=MESH)
            cp.wait_send()
            cp.wait_recv()

    bufs = (*parts, *lands, all_loss)
    outs = pl.pallas_call(
        body, name="pairs_wait", out_shape=tuple(pltpu.HBM(a.shape, a.dtype) for a in bufs),
        in_specs=[HBM_SPEC_STRICT] * len(bufs) + [SEM_SPEC, SEM_SPEC, HBM_SPEC],
        out_specs=tuple([HBM_SPEC_STRICT] * len(bufs)), input_output_aliases={i: i for i in range(len(bufs))},
        compiler_params=pltpu.CompilerParams(has_side_effects=SPLIT_EFFECT),
    )(*bufs, send_sems, recv_sems, after)
    return list(outs[:n]), list(outs[n:2 * n]), outs[2 * n]


def _add_pair_tiled(p, r):
    rows, h = r.shape[1:]

    def body(p_ref, r_ref, q_ref):
        q_ref[...] = (p_ref[...].astype(F32) + r_ref[...].astype(F32)).astype(BF16)

    spec = pl.BlockSpec((None, rows, h), lambda k: (k, 0, 0))
    return pl.pallas_call(
        body, name="add_pair_w_in", grid=(N_CHIPS,), in_specs=[spec, spec], out_specs=spec,
        out_shape=jax.ShapeDtypeStruct(r.shape, BF16),
    )(p, r)


def _add_pair_small(ps, rs, c, name):
    n = len(ps)

    def body(c_ref, *refs):
        for w in range(n):
            h = _half(ps[w])
            mine = refs[w][:, :, pl.ds(pl.multiple_of(c_ref[0] * h, 128), h)]
            refs[2 * n + w][...] = (mine.astype(F32) + refs[n + w][...].astype(F32)).astype(BF16)

    return pl.pallas_call(
        body, name=name,
        in_specs=[pl.BlockSpec(memory_space=pltpu.SMEM)] + [VMEM_SPEC] * (2 * n), out_specs=[VMEM_SPEC] * n,
        out_shape=[jax.ShapeDtypeStruct(r.shape, BF16) for r in rs],
        compiler_params=pltpu.CompilerParams(vmem_limit_bytes=VMEM_LIMIT),
    )(c, *ps, *rs)


def _exchange_chips(qs):
    n = len(qs)

    def body(*refs):
        q_refs, r_refs = refs[:n], refs[n:2 * n]
        send_sems, recv_sems = refs[2 * n:]
        x, y, c, others = _place()
        me = 2 * x + y
        cps = []
        for w in range(n):
            for j, (px, py) in enumerate(others):
                cps.append(pltpu.make_async_remote_copy(
                    src_ref=q_refs[w].at[2 * px + py], dst_ref=r_refs[w].at[me], send_sem=send_sems.at[3 * w + j],
                    recv_sem=recv_sems.at[3 * w + j], device_id=(px, py, c), device_id_type=MESH))
        for cp in cps:
            cp.start()
        for w in range(n):
            for j, (px, py) in enumerate(others):
                pltpu.make_async_remote_copy(
                    src_ref=q_refs[w].at[me], dst_ref=r_refs[w].at[2 * px + py], send_sem=send_sems.at[3 * w + j],
                    recv_sem=recv_sems.at[3 * w + j], device_id=(px, py, c), device_id_type=MESH).wait_recv()
        for cp in cps:
            cp.wait_send()

    return pl.pallas_call(
        body, name="exchange_chips", in_specs=[HBM_SPEC] * n, out_specs=[HBM_SPEC] * n,
        out_shape=[jax.ShapeDtypeStruct(q.shape, BF16) for q in qs],
        scratch_shapes=[pltpu.SemaphoreType.DMA((3 * n,)), pltpu.SemaphoreType.DMA((3 * n,))],
    )(*qs)


SEM_SPEC = pl.BlockSpec(memory_space=pltpu.SEMAPHORE)
SPLIT_EFFECT = pltpu.SideEffectType.DATAFLOW_SIDE_EFFECTING


def _chips_start(qs, name):
    n = len(qs)

    def body(*refs):
        q_refs, land_refs = refs[:n], refs[n:2 * n]
        send_sems, recv_sems, token = refs[2 * n], refs[2 * n + 1], refs[-1]
        x, y, c, others = _place()
        me = 2 * x + y
        for w in range(n):
            for j, (px, py) in enumerate(others):
                pltpu.make_async_remote_copy(
                    src_ref=q_refs[w].at[2 * px + py], dst_ref=land_refs[w].at[me], send_sem=send_sems.at[3 * w + j],
                    recv_sem=recv_sems.at[3 * w + j], device_id=(px, py, c), device_id_type=MESH).start()
        token[...] = jnp.zeros_like(token)

    hbm = [pltpu.HBM(q.shape, BF16) for q in qs]
    outs = pl.pallas_call(
        body, name=name,
        out_shape=(pltpu.SemaphoreType.DMA((3 * n,)), pltpu.SemaphoreType.DMA((3 * n,)), *hbm, *hbm,
                   jax.ShapeDtypeStruct(LOSS_TILE, F32)),
        in_specs=[HBM_SPEC_STRICT] * (2 * n),
        out_specs=(SEM_SPEC, SEM_SPEC, *[HBM_SPEC_STRICT] * (2 * n), VMEM_SPEC),
        input_output_aliases={i: 2 + i for i in range(2 * n)},
        compiler_params=pltpu.CompilerParams(has_side_effects=SPLIT_EFFECT),
    )(*[pltpu.with_memory_space_constraint(q, pltpu.HBM) for q in qs],
      *[pltpu.with_memory_space_constraint(lax.empty(q.shape, BF16), pltpu.HBM) for q in qs])
    return outs[0], outs[1], outs[2:2 + n], outs[2 + n:2 + 2 * n], outs[-1]


def _chips_wait(send_sems, recv_sems, q_thru, land_thru, after, name):
    n = len(q_thru)

    def body(*refs):
        q_refs, land_refs = refs[:n], refs[n:2 * n]
        send_sems, recv_sems = refs[2 * n], refs[2 * n + 1]
        x, y, c, others = _place()
        me = 2 * x + y
        for w in range(n):
            for j, (px, py) in enumerate(others):
                cp = pltpu.make_async_remote_copy(
                    src_ref=q_refs[w].at[2 * px + py], dst_ref=land_refs[w].at[2 * px + py],
                    send_sem=send_sems.at[3 * w + j], recv_sem=recv_sems.at[3 * w + j], device_id=(px, py, c),
                    device_id_type=MESH)
                cp.wait_send()
                cp.wait_recv()

    outs = pl.pallas_call(
        body, name=name, out_shape=tuple(pltpu.HBM(a.shape, a.dtype) for a in (*q_thru, *land_thru)),
        in_specs=[HBM_SPEC_STRICT] * (2 * n) + [SEM_SPEC, SEM_SPEC, HBM_SPEC],
        out_specs=tuple([HBM_SPEC_STRICT] * (2 * n)), input_output_aliases={i: i for i in range(2 * n)},
        compiler_params=pltpu.CompilerParams(has_side_effects=SPLIT_EFFECT),
    )(*q_thru, *land_thru, send_sems, recv_sems, after)
    return list(outs[:n]), list(outs[n:])


def _sum_chips_tiled(q, r, idx, tile):
    rows, h = r.shape[1:]
    nt = h // tile

    def body(idx_ref, q_ref, r0_ref, r1_ref, r2_ref, g_ref):
        g_ref[...] = (q_ref[...].astype(F32) + r0_ref[...].astype(F32) + r1_ref[...].astype(F32)
                      + r2_ref[...].astype(F32))

    def slab(t):
        return pl.BlockSpec((None, rows, tile), lambda i, idx_ref: (idx_ref[t], 0, i))

    return pl.pallas_call(
        body, name="sum_chips_w_in",
        grid_spec=pltpu.PrefetchScalarGridSpec(
            num_scalar_prefetch=1, grid=(nt,), in_specs=[slab(0), slab(1), slab(2), slab(3)],
            out_specs=pl.BlockSpec((rows, tile), lambda i, idx_ref: (0, idx_ref[4] * nt + i))),
        out_shape=jax.ShapeDtypeStruct((rows, 2 * h), F32),
    )(idx, q, r, r, r)


def _sum_chips_small(qs, rs, idx, all_dtypes):
    n = len(rs)
    n_all = len(all_dtypes)

    def body(idx_ref, *refs):
        c = idx_ref[4]
        for w in range(n):
            q_ref, r_ref, g_ref = refs[w], refs[n + w], refs[2 * n + w]
            acc = q_ref[idx_ref[0]].astype(F32)
            for t in range(1, N_CHIPS):
                acc = acc + r_ref[idx_ref[t]].astype(F32)
            h = rs[w].shape[2]
            mine = pl.ds(pl.multiple_of(c * h, 128), h)
            g_ref[...] = jnp.zeros_like(g_ref)
            if w >= n - n_all:
                g_ref[idx_ref[0], :, mine] = acc.astype(g_ref.dtype)
            else:
                g_ref[:, mine] = acc

    shapes = [jax.ShapeDtypeStruct((r.shape[1], 2 * r.shape[2]), F32) for r in rs[:n - n_all]]
    shapes += [jax.ShapeDtypeStruct((N_CHIPS, r.shape[1], 2 * r.shape[2]), dt)
               for r, dt in zip(rs[n - n_all:], all_dtypes)]
    return pl.pallas_call(
        body, name="sum_chips_small",
        in_specs=[pl.BlockSpec(memory_space=pltpu.SMEM)] + [VMEM_SPEC] * (2 * n), out_specs=[VMEM_SPEC] * n,
        out_shape=shapes, compiler_params=pltpu.CompilerParams(vmem_limit_bytes=VMEM_LIMIT),
    )(idx, *qs, *rs)


def _share(shards, alls):
    n, na = len(shards), len(alls)
    total = n + na

    def body(*refs):
        g_refs, a_refs = refs[total:total + n], refs[total + n:2 * total]
        send_sems, recv_sems = refs[2 * total:]
        x, y, c, others = _place()
        me = 2 * x + y
        sibling = (x, y, 1 - c)

        def cols_of(w, half):
            h = shards[w].shape[1] // 2
            return g_refs[w].at[:, pl.ds(half * h, h)]

        def slab(a, chip, half):
            h = alls[a].shape[2] // 2
            return a_refs[a].at[chip, :, pl.ds(half * h, h)]

        def copy(src, dst, k, to):
            return pltpu.make_async_remote_copy(src_ref=src, dst_ref=dst, send_sem=send_sems.at[k],
                                                recv_sem=recv_sems.at[k], device_id=to, device_id_type=MESH)

        cps = [copy(cols_of(w, c), cols_of(w, c), w, sibling) for w in range(n)]
        for a in range(na):
            base = n + 7 * a
            cps.append(copy(slab(a, me, c), slab(a, me, c), base, sibling))
            for j, (px, py) in enumerate(others):
                cps.append(copy(slab(a, me, c), slab(a, me, c), base + 1 + j, (px, py, c)))
        for cp in cps:
            cp.start()
        fwd = []
        for a in range(na):
            base = n + 7 * a
            for j, (px, py) in enumerate(others):
                chip = 2 * px + py
                copy(slab(a, me, c), slab(a, chip, c), base + 1 + j, (px, py, c)).wait_recv()
                cp = copy(slab(a, chip, c), slab(a, chip, c), base + 4 + j, sibling)
                cp.start()
                fwd.append(cp)
        for a in range(na):
            base = n + 7 * a
            for j, (px, py) in enumerate(others):
                chip = 2 * px + py
                copy(slab(a, chip, c), slab(a, chip, 1 - c), base + 4 + j, sibling).wait_recv()
            copy(slab(a, me, c), slab(a, me, 1 - c), base, sibling).wait_recv()
        for w in range(n):
            copy(cols_of(w, c), cols_of(w, 1 - c), w, sibling).wait_recv()
        for cp in cps + fwd:
            cp.wait_send()

    nsem = n + 7 * na
    return pl.pallas_call(
        body, name="share", in_specs=[HBM_SPEC] * total, out_specs=[HBM_SPEC] * total,
        out_shape=[jax.ShapeDtypeStruct(a.shape, a.dtype) for a in (*shards, *alls)],
        input_output_aliases={i: i for i in range(total)},
        scratch_shapes=[pltpu.SemaphoreType.DMA((nsem,)), pltpu.SemaphoreType.DMA((nsem,))],
    )(*shards, *alls)


def _adamw(w, g, m, v):
    m2 = ADAM_B1 * m + (1.0 - ADAM_B1) * g
    v2 = ADAM_B2 * v + (1.0 - ADAM_B2) * (g * g)
    m_hat = m2 / (1.0 - ADAM_B1 ** ADAM_STEP)
    v_hat = v2 / (1.0 - ADAM_B2 ** ADAM_STEP)
    return -ADAM_LR * (m_hat / (jnp.sqrt(v_hat) + ADAM_EPS) + ADAM_WD * w), m2, v2


def _update_w_in(wt, gt, mt, vt, lat, owner, tile):
    nlat = lat.shape[0] // tile

    def body(owner_ref, w_ref, g_ref, m_ref, v_ref, lat_ref, g2_ref, d_ref, m2_ref, v2_ref):
        row = pl.program_id(0) * tile + lax.broadcasted_iota(jnp.int32, (tile, 1), 0)
        g = jnp.where((row < LAT_COLS) & (owner_ref[0] == 1), lat_ref[...].astype(F32), g_ref[...])
        g2_ref[...] = g
        d_ref[...], m2_ref[...], v2_ref[...] = _adamw(w_ref[...], g, m_ref[...], v_ref[...])

    spec = pl.BlockSpec((tile, wt.shape[1]), lambda i, o: (i, 0))
    return pl.pallas_call(
        body, name="update_w_in",
        grid_spec=pltpu.PrefetchScalarGridSpec(
            num_scalar_prefetch=1, grid=(wt.shape[0] // tile,),
            in_specs=[spec] * 4 + [pl.BlockSpec((tile, wt.shape[1]), lambda i, o: (jnp.minimum(i, nlat - 1), 0))],
            out_specs=[spec] * 4),
        out_shape=[jax.ShapeDtypeStruct(wt.shape, F32)] * 4,
        compiler_params=_params(("parallel",)),
    )(owner, wt, gt, mt, vt, lat)


def _update_small(ws, gs, ms, vs):
    n = len(ws)

    def body(*refs):
        for k in range(n):
            w_ref, g_ref, m_ref, v_ref = refs[k], refs[n + k], refs[2 * n + k], refs[3 * n + k]
            d, m2, v2 = _adamw(w_ref[...], g_ref[...], m_ref[...], v_ref[...])
            refs[4 * n + k][...] = d
            refs[5 * n + k][...] = m2
            refs[6 * n + k][...] = v2

    shapes = [jax.ShapeDtypeStruct(w.shape, F32) for w in ws]
    outs = pl.pallas_call(
        body, name="update_small", in_specs=[VMEM_SPEC] * (4 * n), out_specs=[VMEM_SPEC] * (3 * n),
        out_shape=shapes * 3,
        compiler_params=pltpu.CompilerParams(vmem_limit_bytes=VMEM_LIMIT),
    )(*ws, *gs, *ms, *vs)
    return outs[:n], outs[n:2 * n], outs[2 * n:]


SHARDED = ("w_in", "w_uq", "w_oa", "w_ob", "w_out")
REPLICATED = ("b_in", "g_q", "g_kv", "w_ukv", "sgu_ln_g", "sgu_ln_b", "w_s", "b_s", "ln_g", "ln_b")
ORDER = ("w_in", "b_in", "g_q", "w_uq", "g_kv", "w_ukv", "w_oa", "sgu_ln_g", "sgu_ln_b", "w_s", "b_s", "w_ob", "w_out",
         "ln_g", "ln_b")


def kernel(x, positions, w_in, b_in, g_q, w_uq, g_kv, w_ukv, w_oa, sgu_ln_g, sgu_ln_b, w_s, b_s, w_ob, w_out, ln_g, ln_b, loss_target, m_w_in, m_b_in, m_g_q, m_w_uq, m_g_kv, m_w_ukv, m_w_oa, m_sgu_ln_g, m_sgu_ln_b, m_w_s, m_b_s, m_w_ob, m_w_out, m_ln_g, m_ln_b, v_w_in, v_b_in, v_g_q, v_w_uq, v_g_kv, v_w_ukv, v_w_oa, v_sgu_ln_g, v_sgu_ln_b, v_w_s, v_b_s, v_w_ob, v_w_out, v_ln_g, v_ln_b):
    w = dict(w_in=w_in, b_in=b_in, g_q=g_q, w_uq=w_uq, g_kv=g_kv, w_ukv=w_ukv, w_oa=w_oa, sgu_ln_g=sgu_ln_g,
             sgu_ln_b=sgu_ln_b, w_s=w_s, b_s=b_s, w_ob=w_ob, w_out=w_out, ln_g=ln_g, ln_b=ln_b)
    m = dict(w_in=m_w_in, b_in=m_b_in, g_q=m_g_q, w_uq=m_w_uq, g_kv=m_g_kv, w_ukv=m_w_ukv, w_oa=m_w_oa,
             sgu_ln_g=m_sgu_ln_g, sgu_ln_b=m_sgu_ln_b, w_s=m_w_s, b_s=m_b_s, w_ob=m_w_ob, w_out=m_w_out, ln_g=m_ln_g,
             ln_b=m_ln_b)
    v = dict(w_in=v_w_in, b_in=v_b_in, g_q=v_g_q, w_uq=v_w_uq, g_kv=v_g_kv, w_ukv=v_w_ukv, w_oa=v_w_oa,
             sgu_ln_g=v_sgu_ln_g, sgu_ln_b=v_sgu_ln_b, w_s=v_w_s, b_s=v_b_s, w_ob=v_w_ob, w_out=v_w_out, ln_g=v_ln_g,
             ln_b=v_ln_b)
    w, m, v = ({n: a[0] for n, a in d.items()} for d in (w, m, v))
    c = lax.axis_index("c")

    wt_shard, mt_shard, vt_shard = (jnp.transpose(d["w_in"]) for d in (w, m, v))
    xi, yi = lax.axis_index("x"), lax.axis_index("y")
    me1 = (2 * xi + yi).reshape(1).astype(jnp.int32)
    first = _first_start(*_cast_first(wt_shard[:LAT_COLS], w["w_uq"].reshape(Q_RANK // 4, HEADS * QK_DIM), me1))
    bufs = _cast_own([wt_shard, w["w_oa"], w["w_ob"], w["w_out"]], me1, first[4])
    send0, recv0, bufs, token0 = _gather_start(bufs, first[4])
    g_lat, g_uq = _first_forward(*_first_wait(*first[:4], token0))
    st = _local_attention(x[0], positions[0], g_lat, w["b_in"], w["g_q"], g_uq.reshape(Q_RANK, HEADS, QK_DIM),
                          w["g_kv"], w["w_ukv"], token0)
    g_in, g_oa, g_ob, g_out = _gather_finish(_gather_wait(send0, recv0, bufs, st["o"]))
    wt = g_in.reshape(IN_W, D_MODEL)

    loss, early, st = _local_head(
        st, x[0], loss_target[0], wt, g_oa, w["sgu_ln_g"], w["sgu_ln_b"], w["w_s"], w["b_s"], g_ob,
        g_out.reshape(D_MODEL, D_MODEL), w["ln_g"], w["ln_b"])

    c1 = c.reshape(1).astype(jnp.int32)
    idx = jnp.stack([2 * xi + yi, 2 * (1 - xi) + yi, 2 * xi + (1 - yi), 2 * (1 - xi) + (1 - yi), c]).astype(jnp.int32)
    slabs = lambda a: a.reshape(N_CHIPS, IN_W // N_CHIPS, D_MODEL // 2)
    theirs = slabs(_dwt_early(st["dhmt"], st["dhgt"], st["xb2"], 1 - c1, c1, "dwt_theirs"))
    parts1 = [theirs, early["w_oa"].astype(BF16), early["w_ob"].astype(BF16),
              early["w_out"].reshape(N_CHIPS, SLAB_W, D_MODEL).astype(BF16)]
    my_loss = lax.dynamic_update_slice(jnp.zeros((N_DEV,) + LOSS_TILE, F32), jnp.broadcast_to(loss, (1,) + LOSS_TILE),
                                       (4 * xi + 2 * yi + c, 0, 0))
    sems0 = _pairs_start(parts1, my_loss, 1)
    mine = slabs(_dwt_early(st["dhmt"], st["dhgt"], st["xb2"], c1, sems0[5], "dwt_mine"))
    parts1, recv1, all_loss = _pairs_wait(*sems0[:5], mine, 1)
    pairs1 = [_add_pair_tiled(mine, recv1[0]), *_add_pair_small(parts1[1:], recv1[1:], c1, "add_pair_early")]
    sems1 = _chips_start(pairs1, "chips_start_early")
    st["delta"] = st["delta"] + sems1[4][0, 0]
    dq, dk, dv = _local_attn_bwd(st)
    dhl, late = _local_tail(st, dq, dk, dv, dv)

    grads = {**early, **late}
    rep = jnp.concatenate([_rows8(grads[n]) for n in REPLICATED], axis=0)
    rep = jnp.pad(rep, ((0, N_CHIPS * REP_ROWS - rep.shape[0]), (0, 0))).reshape(N_CHIPS, REP_ROWS, D_MODEL)
    parts2 = [late["w_uq"].reshape(N_CHIPS, Q_RANK // N_CHIPS, HEADS * QK_DIM).astype(BF16), rep.astype(BF16),
              late["w_lat"].reshape(N_CHIPS, LAT_ROWS_PAD // N_CHIPS, D_MODEL)]
    pairs2 = _add_pair_small(parts2, _exchange_pairs(parts2, "exchange_pairs_late"), c1, "add_pair_late")
    sems2 = _chips_start(pairs2, "chips_start_late")
    dx = _dx(st["dr"], st["dhg"], st["dhm"], dhl, st["wt"], sems2[4])
    pairs2, landed2 = _chips_wait(*sems2[:4], dx, "chips_wait_late")
    pairs1, landed1 = _chips_wait(*sems1[:4], landed2[0], "chips_wait_early")
    sums = [_sum_chips_tiled(pairs1[0], landed1[0], idx, 128),
            *_sum_chips_small([*pairs1[1:], *pairs2], [*landed1[1:], *landed2], idx, (F32, BF16))]
    *shards, g_rep, g_lat = _share(sums[:-2], sums[-2:])
    loss = jnp.sum(all_loss[:, 0, 0])

    red = {n: s.reshape(w[n].shape) for n, s in zip(("w_oa", "w_ob", "w_out", "w_uq"), shards[1:])}
    g_rep = g_rep.reshape(N_CHIPS * REP_ROWS, D_MODEL)
    off = 0
    for n in REPLICATED:
        rows = _rows8(w[n]).shape[0]
        red[n] = g_rep[off:off + rows].reshape(-1)[:w[n].size].reshape(w[n].shape)
        off += rows
    owner = (2 * xi + yi == 0).astype(jnp.int32).reshape(1)
    gt, dt, mt, vt2 = _update_w_in(wt_shard, shards[0], mt_shard, vt_shard,
                                   g_lat.reshape(LAT_ROWS_PAD, D_MODEL).astype(F32), owner, 232)
    red["w_in"] = jnp.transpose(gt)
    small = [n for n in ORDER if n != "w_in"]
    as2d = lambda a: a.reshape(-1, a.shape[-1])
    ds, ms, vs = _update_small([as2d(w[n]) for n in small], [as2d(red[n]) for n in small],
                               [as2d(m[n]) for n in small], [as2d(v[n]) for n in small])
    delta, new_m, new_v = {"w_in": jnp.transpose(dt)}, {"w_in": jnp.transpose(mt)}, {"w_in": jnp.transpose(vt2)}
    for i, n in enumerate(small):
        delta[n], new_m[n], new_v[n] = (a[i].reshape(w[n].shape) for a in (ds, ms, vs))

    lead = lambda a: a[None]
    return (loss, dx[None], *[lead(red[n]) for n in ORDER], *[lead(delta[n]) for n in ORDER],
            *[lead(new_m[n]) for n in ORDER], *[lead(new_v[n]) for n in ORDER])
```

```python
import functools
import math

import jax
import jax.numpy as jnp
from jax import lax
from jax.experimental import pallas as pl
from jax.experimental.pallas import tpu as pltpu

F32 = jnp.float32
BF16 = jnp.bfloat16

D_MODEL = 1024
HEADS = 8
Q_RANK = 384
KV_RANK = 128
NOPE = 64
ROPE = 32
V_DIM = 64
QK_DIM = NOPE + ROPE
HEAD_PAD = 128
MLA_W = HEADS * V_DIM
SGU_W = 512
GROUPS = 8
CHUNK = 128
IN_W = 4640
RMS_EPS = 1e-6
LN_EPS = 1e-5
ALPHA = 2.0 ** 0.25
ROPE_THETA = 10000.0
SCALE = QK_DIM ** -0.5

GATE_W = 2 * D_MODEL
MID_W = 4 * SGU_W
LAT_W = Q_RANK + KV_RANK + HEAD_PAD
PAD_W = GATE_W + MID_W + LAT_W
LAT_COLS = Q_RANK + KV_RANK + ROPE
ROW_GATE = LAT_COLS + MID_W
LAT_ROWS_PAD = 704
N_SLABS = 4
SLAB_W = D_MODEL // N_SLABS

ROW_TILE = 256
MATMUL_ROW_TILE = 512
ATT_TQ = 2048
ATT_TK = 256
ATT_BWD_TQ = 512
ATT_BWD_TK = 256
LOG2E = 1.4426950408889634
LN2 = 0.6931471805599453
Q_SCALE = SCALE * LOG2E
VMEM_LIMIT = 56 * 1024 * 1024

ADAM_LR = 0.001
ADAM_B1 = 0.9
ADAM_B2 = 0.999
ADAM_EPS = 1e-08
ADAM_WD = 0.01
ADAM_STEP = 10


def _dot(a, b):
    return jnp.dot(a, b, preferred_element_type=F32)


def _dot_nt(a, b):
    return lax.dot_general(a, b, (((1,), (1,)), ((), ())), preferred_element_type=F32)


def _dot_tn(a, b):
    return lax.dot_general(a, b, (((0,), (0,)), ((), ())), preferred_element_type=F32)


def _sigmoid(z):
    return 0.5 * jnp.tanh(0.5 * z) + 0.5


_GELU_C = math.sqrt(2.0 / math.pi)


def _gelu_and_grad(x):
    x2 = x * x
    t = jnp.tanh(_GELU_C * (x + 0.044715 * x * x2))
    g = 0.5 * x * (1.0 + t)
    dg = 0.5 * (1.0 + t) + 0.5 * x * (1.0 - t * t) * (_GELU_C * (1.0 + 3.0 * 0.044715 * x2))
    return g, dg


def _silu_and_grad(z):
    s = _sigmoid(z)
    return z * s, s * (1.0 + z * (1.0 - s))


def _rope(xb, c, sl, sh):
    return xb * c + pltpu.roll(xb, 112, 1) * sl + pltpu.roll(xb, 16, 1) * sh


def _rope_t(dy, c, sl, sh):
    return dy * c + pltpu.roll(dy * sl, 16, 1) + pltpu.roll(dy * sh, 112, 1)


def _params(sem=("arbitrary",)):
    return pltpu.CompilerParams(dimension_semantics=sem, vmem_limit_bytes=VMEM_LIMIT)


def _row_spec(tile, width):
    return pl.BlockSpec((tile, width), lambda i: (i, 0))


def _full_spec(shape):
    nd = len(shape)
    return pl.BlockSpec(shape, lambda i: (0,) * nd)


def _kpe_rows(wt_ref):
    z = lambda n: jnp.zeros((n, D_MODEL), BF16)
    return jnp.concatenate([z(NOPE), wt_ref[Q_RANK + KV_RANK:LAT_COLS, :], z(HEAD_PAD - QK_DIM)], axis=0)


def _fwd_rest(xb, wt, b_g, b_m):
    s = xb.shape[0]
    ts = MATMUL_ROW_TILE

    def body(xb_ref, wt_ref, bg_ref, bm_ref, hg_ref, hm_ref):
        xb_ = xb_ref[...]
        hg_ref[...] = _dot_nt(xb_, wt_ref[ROW_GATE:IN_W, :]) + bg_ref[...]
        hm_ref[...] = _dot_nt(xb_, wt_ref[LAT_COLS:ROW_GATE, :]) + bm_ref[...]

    return pl.pallas_call(
        body, name="fwd_rest", grid=(s // ts,),
        in_specs=[_row_spec(ts, D_MODEL), _full_spec(wt.shape), _full_spec(b_g.shape), _full_spec(b_m.shape)],
        out_specs=[_row_spec(ts, GATE_W), _row_spec(ts, MID_W)],
        out_shape=[jax.ShapeDtypeStruct((s, GATE_W), F32), jax.ShapeDtypeStruct((s, MID_W), F32)],
        compiler_params=_params(),
    )(xb, wt, b_g, b_m)


def _fwd_lat(x, wlat, b_l, g_q, wuq, g_kv, wk, wv, rc, rsl, rsh, after):
    s = x.shape[0]
    ts = ROW_TILE

    def body(x_ref, wt_ref, bl_ref, gq_ref, wuq_ref, gkv_ref, wk_ref, wv_ref, rc_ref, rsl_ref,
             rsh_ref, after_ref, hl_ref, q_ref, k_ref, v_ref, xb_ref, qt_ref, kt_ref, vt_ref, xb2_ref):
        xb = x_ref[...].astype(BF16)
        xb_ref[...] = xb
        xb2_ref[0] = xb[:, :D_MODEL // 2]
        xb2_ref[1] = xb[:, D_MODEL // 2:]
        hl = jnp.concatenate([_dot_nt(xb, wt_ref[0:Q_RANK + KV_RANK, :]), _dot_nt(xb, _kpe_rows(wt_ref))],
                             axis=1) + bl_ref[...]
        hl_ref[...] = hl
        c, sl, sh = rc_ref[...], rsl_ref[...], rsh_ref[...]
        cq = hl[:, :Q_RANK]
        cqn = cq * lax.rsqrt(jnp.mean(cq * cq, axis=-1, keepdims=True) + RMS_EPS) * gq_ref[...]
        q = _dot(cqn.astype(BF16), wuq_ref[...])
        ckv = hl[:, Q_RANK:Q_RANK + KV_RANK]
        ckvn = (ckv * lax.rsqrt(jnp.mean(ckv * ckv, axis=-1, keepdims=True) + RMS_EPS) * gkv_ref[...]).astype(BF16)
        k = _dot(ckvn, wk_ref[...])
        vb = _dot(ckvn, wv_ref[...]).astype(BF16)
        v_ref[...] = vb
        vt_ref[...] = vb.T
        kpe = _rope(hl[:, Q_RANK + KV_RANK:], c, sl, sh)
        for hd in range(HEADS):
            lanes = slice(hd * HEAD_PAD, (hd + 1) * HEAD_PAD)
            qb = (_rope(q[:, lanes], c, sl, sh) * Q_SCALE).astype(BF16)
            kb = (k[:, lanes] + kpe).astype(BF16)
            q_ref[:, lanes] = qb
            k_ref[:, lanes] = kb
            qt_ref[lanes, :] = qb.T
            kt_ref[lanes, :] = kb.T

    qk_w = HEADS * HEAD_PAD
    col_spec = lambda rows: pl.BlockSpec((rows, ts), lambda i: (0, i))
    return pl.pallas_call(
        body, name="fwd_lat", grid=(s // ts,),
        in_specs=[_row_spec(ts, D_MODEL), _full_spec(wlat.shape),
                  _full_spec(b_l.shape), _full_spec(g_q.shape),
                  _full_spec(wuq.shape), _full_spec(g_kv.shape), _full_spec(wk.shape), _full_spec(wv.shape),
                  _row_spec(ts, HEAD_PAD), _row_spec(ts, HEAD_PAD), _row_spec(ts, HEAD_PAD),
                  pl.BlockSpec(memory_space=pl.ANY)],
        out_specs=[_row_spec(ts, LAT_W), _row_spec(ts, qk_w),
                   _row_spec(ts, qk_w), _row_spec(ts, MLA_W), _row_spec(ts, D_MODEL), col_spec(qk_w), col_spec(qk_w),
                   col_spec(MLA_W), pl.BlockSpec((2, ts, D_MODEL // 2), lambda i: (0, i, 0))],
        out_shape=[jax.ShapeDtypeStruct((s, LAT_W), F32), jax.ShapeDtypeStruct((s, qk_w), BF16),
                   jax.ShapeDtypeStruct((s, qk_w), BF16), jax.ShapeDtypeStruct((s, MLA_W), BF16),
                   jax.ShapeDtypeStruct((s, D_MODEL), BF16), jax.ShapeDtypeStruct((qk_w, s), BF16),
                   jax.ShapeDtypeStruct((qk_w, s), BF16), jax.ShapeDtypeStruct((MLA_W, s), BF16),
                   jax.ShapeDtypeStruct((2, s, D_MODEL // 2), BF16)],
        compiler_params=_params(),
    )(x, wlat, b_l, g_q, wuq, g_kv, wk, wv, rc, rsl, rsh, after)


def _attn_fwd(qt, k, vt):
    s = k.shape[0]
    tq, tk = ATT_TQ, ATT_TK
    r = tq // tk
    pairs = HEADS // 2

    def body(qt_ref, k_ref, vt_ref, o_ref, lse_ref):
        i = pl.program_id(1)
        qts = [qt_ref[hh * HEAD_PAD:(hh + 1) * HEAD_PAD, :] for hh in range(2)]

        def scores(j, lo):
            koff = pl.multiple_of(j * tk, tk)
            return tuple(_dot(k_ref[pl.ds(koff, tk), hh * HEAD_PAD:(hh + 1) * HEAD_PAD], qts[hh][:, lo:])
                         for hh in range(2))

        def weighted(j, ps):
            koff = pl.multiple_of(j * tk, tk)
            return tuple(_dot(vt_ref[hh * V_DIM:(hh + 1) * V_DIM, pl.ds(koff, tk)], ps[hh]) for hh in range(2))

        def from_lane(full, lo, part):
            return part if lo == 0 else jnp.concatenate([full[:, :lo], part], axis=1)

        def step(j, carry, diag, last):
            st, ps, stats = carry
            lo = 0 if diag is None else diag * tk
            lo_prev = 0 if not diag else (diag - 1) * tk
            st_next = None if last else scores(j + 1, 0 if diag is None else lo + tk)
            pvs = weighted(jnp.maximum(j - 1, 0), ps)
            new_ps, new_stats = [], []
            for hh in range(2):
                m, l, acc = stats[hh]
                s_ = st[hh]
                if diag is not None:
                    krow = lax.broadcasted_iota(jnp.int32, s_.shape, 0)
                    qcol = lax.broadcasted_iota(jnp.int32, s_.shape, 1)
                    s_ = jnp.where(krow <= qcol, s_, -jnp.inf)
                acc = from_lane(acc, lo_prev, acc[:, lo_prev:] + pvs[hh])
                m_old = m[:, lo:]
                m_new = jnp.maximum(m_old, jnp.max(s_, axis=0, keepdims=True))
                a = jnp.exp2(m_old - m_new)
                p = jnp.exp2(s_ - m_new)
                new_stats.append((from_lane(m, lo, m_new),
                                  from_lane(l, lo, a * l[:, lo:] + jnp.sum(p, axis=0, keepdims=True)),
                                  from_lane(acc, lo, a * acc[:, lo:])))
                new_ps.append(p.astype(BF16))
            return st_next, tuple(new_ps), tuple(new_stats)

        one = (jnp.full((1, tq), -jnp.inf, F32), jnp.zeros((1, tq), F32), jnp.zeros((V_DIM, tq), F32))
        zero_p = jnp.zeros((tk, tq), BF16)
        nfull = i * r
        carry = lax.fori_loop(0, nfull, functools.partial(step, diag=None, last=False),
                              (scores(0, 0), (zero_p, zero_p), (one, one)))
        for d in range(r):
            carry = step(nfull + d, carry, d, d == r - 1)
        _, ps, stats = carry
        pvs = weighted(nfull + r - 1, ps)
        lo = (r - 1) * tk
        ot = jnp.concatenate([from_lane(stats[hh][2], lo, stats[hh][2][:, lo:] + pvs[hh]) / stats[hh][1]
                              for hh in range(2)], axis=0)
        o_ref[...] = ot.T
        lse = [stats[hh][0] + jnp.log(stats[hh][1]) * LOG2E for hh in range(2)]
        lse_ref[...] = jnp.concatenate(lse + [jnp.zeros((6, tq), F32)], axis=0)

    return pl.pallas_call(
        body, name="attn_fwd", grid=(pairs, s // tq),
        in_specs=[pl.BlockSpec((2 * HEAD_PAD, tq), lambda p, i: (p, i)),
                  pl.BlockSpec((s, 2 * HEAD_PAD), lambda p, i: (0, p)),
                  pl.BlockSpec((2 * V_DIM, s), lambda p, i: (p, 0))],
        out_specs=[pl.BlockSpec((tq, 2 * V_DIM), lambda p, i: (i, p)),
                   pl.BlockSpec((None, 8, tq), lambda p, i: (p, 0, i))],
        out_shape=[jax.ShapeDtypeStruct((s, MLA_W), F32), jax.ShapeDtypeStruct((pairs, 8, s), F32)],
        compiler_params=_params(("arbitrary", "arbitrary")),
    )(qt, k, vt)


def _attn_bwd(q, qt, k, kt, v, do, dot, lse, delta):
    s = k.shape[0]
    tk = ATT_BWD_TK
    nk = s // tk
    pairs = HEADS // 2

    def body(q_ref, qt_ref, k_ref, kt_ref, v_ref, do_ref, dot_ref, lse_ref, dl_ref, dqt_ref, dk_ref, dv_ref):
        krow = lax.broadcasted_iota(jnp.int32, (tk, tk), 0)
        qcol = lax.broadcasted_iota(jnp.int32, (tk, tk), 1)
        lane = lax.broadcasted_iota(jnp.int32, (tk, 2 * V_DIM), 1)
        drow = lax.broadcasted_iota(jnp.int32, (2 * V_DIM, s), 0)
        dotb = dot_ref[...]
        dots = [jnp.where((drow < V_DIM) if hh == 0 else (drow >= V_DIM), dotb, jnp.zeros_like(dotb))
                for hh in range(2)]
        for j in range(nk):
            lo = j * tk
            vb = v_ref[lo:lo + tk, :]
            dob = do_ref[lo:, :]
            dvs = []
            for hh in range(2):
                rows = slice(hh * HEAD_PAD, (hh + 1) * HEAD_PAD)
                st = _dot(k_ref[lo:lo + tk, rows], qt_ref[rows, lo:])
                diag = jnp.where(krow <= qcol, st[:, :tk], -jnp.inf)
                st = diag if j == nk - 1 else jnp.concatenate([diag, st[:, tk:]], axis=1)
                p = jnp.exp2(st - lse_ref[hh:hh + 1, lo:])
                dpt = _dot(vb, dots[hh][:, lo:])
                dst = (p * (dpt - dl_ref[hh:hh + 1, lo:])).astype(BF16)
                dvs.append(_dot(p.astype(BF16), dob))
                dk_ref[lo:lo + tk, rows] = _dot(dst, q_ref[lo:, rows]) * LN2
                dqt = _dot(kt_ref[rows, lo:lo + tk], dst)
                if j == 0:
                    dqt_ref[rows, :] = dqt
                else:
                    dqt_ref[rows, lo:] += dqt
            dv_ref[lo:lo + tk, :] = jnp.where(lane < V_DIM, dvs[0], dvs[1])
        dqt_ref[...] = dqt_ref[...] * SCALE

    pair_rows = lambda w: pl.BlockSpec((s, w), lambda p: (0, p))
    pair_cols = lambda w: pl.BlockSpec((w, s), lambda p: (p, 0))
    stats = pl.BlockSpec((None, 8, s), lambda p: (p, 0, 0))
    return pl.pallas_call(
        body, name="attn_bwd", grid=(pairs,),
        in_specs=[pair_rows(2 * HEAD_PAD), pair_cols(2 * HEAD_PAD), pair_rows(2 * HEAD_PAD), pair_cols(2 * HEAD_PAD),
                  pair_rows(2 * V_DIM), pair_rows(2 * V_DIM), pair_cols(2 * V_DIM), stats, stats],
        out_specs=[pair_cols(2 * HEAD_PAD), pair_rows(2 * HEAD_PAD), pair_rows(2 * V_DIM)],
        out_shape=[jax.ShapeDtypeStruct((HEADS * HEAD_PAD, s), F32), jax.ShapeDtypeStruct((s, HEADS * HEAD_PAD), F32),
                   jax.ShapeDtypeStruct((s, MLA_W), F32)],
        compiler_params=_params(("arbitrary",)),
    )(q, qt, k, kt, v, do, dot, lse, delta)


def _attn_bwd_dynamic(q, qt, k, kt, v, do, dot, lse, delta):
    s = k.shape[0]
    tq, tk = ATT_BWD_TQ, ATT_BWD_TK
    r = tq // tk
    nq = s // tq
    nk = s // tk
    pairs = HEADS // 2

    def body(q_ref, qt_ref, k_ref, kt_ref, v_ref, do_ref, dot_ref, lse_ref, dl_ref, dqt_ref, dk_ref, dv_ref):
        j = pl.program_id(1)
        krow = lax.broadcasted_iota(jnp.int32, (tk, tq), 0)
        qcol = lax.broadcasted_iota(jnp.int32, (tk, tq), 1)
        lane = lax.broadcasted_iota(jnp.int32, (tk, 2 * V_DIM), 1)
        drow = lax.broadcasted_iota(jnp.int32, (2 * V_DIM, tq), 0)

        @pl.when(j == 0)
        def _():
            dqt_ref[...] = jnp.zeros_like(dqt_ref)

        koff = pl.multiple_of(j * tk, tk)
        vb = v_ref[pl.ds(koff, tk), :]
        kbs = [k_ref[pl.ds(koff, tk), hh * HEAD_PAD:(hh + 1) * HEAD_PAD] for hh in range(2)]
        ktbs = [kt_ref[hh * HEAD_PAD:(hh + 1) * HEAD_PAD, pl.ds(koff, tk)] for hh in range(2)]
        i0 = j // r

        def front(i):
            qoff = pl.multiple_of(i * tq, tq)
            dotb = dot_ref[:, pl.ds(qoff, tq)]
            out = []
            for hh in range(2):
                mine = (drow < V_DIM) if hh == 0 else (drow >= V_DIM)
                st = _dot(kbs[hh], qt_ref[hh * HEAD_PAD:(hh + 1) * HEAD_PAD, pl.ds(qoff, tq)])
                out.append((st, _dot(vb, jnp.where(mine, dotb, jnp.zeros_like(dotb)))))
            return tuple(out)

        def middle(i, tiles, diag):
            qoff = pl.multiple_of(i * tq, tq)
            out = []
            for hh in range(2):
                st, dpt = tiles[hh]
                if diag:
                    st = jnp.where(krow + (j - i0 * r) * tk <= qcol, st, -jnp.inf)
                p = jnp.exp2(st - lse_ref[hh:hh + 1, pl.ds(qoff, tq)])
                out.append((p.astype(BF16), (p * (dpt - dl_ref[hh:hh + 1, pl.ds(qoff, tq)])).astype(BF16)))
            return tuple(out)

        def back(i, pd, accs):
            qoff = pl.multiple_of(i * tq, tq)
            dob = do_ref[pl.ds(qoff, tq), :]
            out = []
            for hh in range(2):
                rows = slice(hh * HEAD_PAD, (hh + 1) * HEAD_PAD)
                p, dst = pd[hh]
                dk_acc, dv_acc = accs[hh]
                dv_acc = dv_acc + _dot(p, dob)
                dk_acc = dk_acc + _dot(dst, q_ref[pl.ds(qoff, tq), rows])
                dqt_ref[rows, pl.ds(qoff, tq)] += _dot(ktbs[hh], dst)
                out.append((dk_acc, dv_acc))
            return tuple(out)

        def step(i, accs, diag):
            return back(i, middle(i, front(i), diag), accs)

        zero_acc = (jnp.zeros((tk, HEAD_PAD), F32), jnp.zeros((tk, 2 * V_DIM), F32))
        accs = step(i0, (zero_acc, zero_acc), True)
        accs = lax.fori_loop(i0 + 1, nq, functools.partial(step, diag=False), accs)
        for hh in range(2):
            dk_ref[:, hh * HEAD_PAD:(hh + 1) * HEAD_PAD] = accs[hh][0] * LN2
        dv_ref[...] = jnp.where(lane < V_DIM, accs[0][1], accs[1][1])

        @pl.when(j == nk - 1)
        def _():
            dqt_ref[...] = dqt_ref[...] * SCALE

    pair_rows = lambda w: pl.BlockSpec((s, w), lambda p, j: (0, p))
    pair_cols = lambda w: pl.BlockSpec((w, s), lambda p, j: (p, 0))
    stats = pl.BlockSpec((None, 8, s), lambda p, j: (p, 0, 0))
    return pl.pallas_call(
        body, name="attn_bwd", grid=(pairs, nk),
        in_specs=[pair_rows(2 * HEAD_PAD), pair_cols(2 * HEAD_PAD), pair_rows(2 * HEAD_PAD), pair_cols(2 * HEAD_PAD),
                  pair_rows(2 * V_DIM), pair_rows(2 * V_DIM), pair_cols(2 * V_DIM), stats, stats],
        out_specs=[pair_cols(2 * HEAD_PAD),
                   pl.BlockSpec((tk, 2 * HEAD_PAD), lambda p, j: (j, p)),
                   pl.BlockSpec((tk, 2 * V_DIM), lambda p, j: (j, p))],
        out_shape=[jax.ShapeDtypeStruct((HEADS * HEAD_PAD, s), F32), jax.ShapeDtypeStruct((s, HEADS * HEAD_PAD), F32),
                   jax.ShapeDtypeStruct((s, MLA_W), F32)],
        compiler_params=_params(("arbitrary", "arbitrary")),
    )(q, qt, k, kt, v, do, dot, lse, delta)


def _split3(a):
    hi = a.astype(BF16)
    r1 = a - hi.astype(F32)
    mid = r1.astype(BF16)
    lo = (r1 - mid.astype(F32)).astype(BF16)
    return hi, mid, lo


def _mid(x, tgt, o, hm, hg, woa, wob, wout, ln_g, ln_b, sg_g, sg_b, w_s, bsb):
    s = x.shape[0]
    ts = ROW_TILE
    nsteps = s // ts
    nch = ts // CHUNK
    npair = GROUPS // 2

    def body(x_ref, t_ref, o_ref, hm_ref, hg_ref, woa_ref, wob_ref, wout_ref, lng_ref, lnb_ref, sgg_ref, sgb_ref,
             ws_ref, bsb_ref,
             dr_ref, dhg_ref, dhm_ref, do_ref, dot_ref, dl_ref, dhgt_ref, dhmt_ref,
             dwout_ref, dwoa_ref, dwob_ref, dws_ref, dbs_ref, dlng_ref, dlnb_ref, dsgg_ref, dsgb_ref, loss_ref,
             dbg_ref, dbm_ref, dbacc_ref, awout_ref, awoa_ref, awob_ref):
        i = pl.program_id(0)

        @pl.when(i == 0)
        def _():
            for r in (awout_ref, awoa_ref, awob_ref, dws_ref, dlng_ref, dlnb_ref, dsgg_ref, dsgb_ref, loss_ref,
                      dbg_ref, dbm_ref, dbacc_ref):
                r[...] = jnp.zeros_like(r)

        def emit(ref, tref, bref, lo, val):
            vb = val.astype(BF16)
            n = val.shape[1]
            ref[:, lo:lo + n] = vb
            tref[lo:lo + n, :] = vb.T
            bref[:, lo:lo + n] += jnp.sum(val, axis=0, keepdims=True)

        lane = lax.broadcasted_iota(jnp.int32, (CHUNK, CHUNK), 1)
        left = lane < V_DIM
        tril = lax.broadcasted_iota(jnp.int32, (CHUNK, CHUNK), 0) >= lane
        ms = [jnp.where(tril, ws_ref[g], 0.0).astype(BF16) for g in range(GROUPS)]

        z_a = hm_ref[:, 0:SGU_W]
        u = hm_ref[:, SGU_W:2 * SGU_W]
        v = hm_ref[:, 2 * SGU_W:3 * SGU_W]
        z_b = hm_ref[:, 3 * SGU_W:4 * SGU_W]
        o = o_ref[...]
        sa, dsa = _silu_and_grad(z_a)
        y_a = (o * sa).astype(BF16)
        gu, dgu = _gelu_and_grad(u)
        gv, dgv = _gelu_and_grad(v)
        mu = jnp.mean(gv, axis=-1, keepdims=True)
        vc = gv - mu
        rstd_v = lax.rsqrt(jnp.mean(vc * vc, axis=-1, keepdims=True) + LN_EPS)
        vhat = vc * rstd_v
        vn = (vhat * sgg_ref[...] + sgb_ref[...]).astype(BF16)
        rows = []
        for c in range(nch):
            blocks = []
            for p in range(npair):
                blk = vn[c * CHUNK:(c + 1) * CHUNK, p * CHUNK:(p + 1) * CHUNK]
                blocks.append(jnp.where(left, _dot(ms[2 * p], blk), _dot(ms[2 * p + 1], blk)))
            rows.append(jnp.concatenate(blocks, axis=1) + bsb_ref[...])
        mixed = jnp.concatenate(rows, axis=0)
        sgu = gu * mixed
        sb, dsb = _silu_and_grad(z_b)
        y_b = (sgu * sb).astype(BF16)
        pa = jnp.concatenate([_dot(y_a, woa_ref[k]) for k in range(N_SLABS)], axis=1)
        pb = jnp.concatenate([_dot(y_b, wob_ref[k]) for k in range(N_SLABS)], axis=1)
        sga = _sigmoid(hg_ref[:, :D_MODEL])
        sgb = _sigmoid(hg_ref[:, D_MODEL:])
        m2 = (sga * pa + sgb * pb).astype(BF16)
        r = ALPHA * x_ref[...] + _dot(m2, wout_ref[...])
        rmu = jnp.mean(r, axis=-1, keepdims=True)
        rc = r - rmu
        rstd = lax.rsqrt(jnp.mean(rc * rc, axis=-1, keepdims=True) + LN_EPS)
        xhat = rc * rstd
        y = xhat * lng_ref[...] + lnb_ref[...]
        err = y - t_ref[...]
        loss_ref[...] += jnp.full(loss_ref.shape, 0.5 / D_MODEL, F32) * jnp.sum(err * err)

        dy = err * (1.0 / D_MODEL)
        dlng_ref[...] += jnp.sum(dy * xhat, axis=0, keepdims=True)
        dlnb_ref[...] += jnp.sum(dy, axis=0, keepdims=True)
        dxh = dy * lng_ref[...]
        dr = rstd * (dxh - jnp.mean(dxh, axis=-1, keepdims=True) - xhat * jnp.mean(dxh * xhat, axis=-1, keepdims=True))
        dr_ref[...] = dr
        drb = dr.astype(BF16)
        awout_ref[...] += _dot_tn(m2, drb)
        dm2 = _dot_nt(drb, wout_ref[...])
        emit(dhg_ref, dhgt_ref, dbg_ref, 0, dm2 * pa * sga * (1.0 - sga))
        emit(dhg_ref, dhgt_ref, dbg_ref, D_MODEL, dm2 * pb * sgb * (1.0 - sgb))
        dpa = (dm2 * sga).astype(BF16)
        dpb = (dm2 * sgb).astype(BF16)
        dy_a = jnp.zeros((ts, MLA_W), F32)
        dy_b = jnp.zeros((ts, SGU_W), F32)
        y_at, y_bt = y_a.T, y_b.T
        for k in range(N_SLABS):
            cols = slice(k * SLAB_W, (k + 1) * SLAB_W)
            awoa_ref[k] += _dot(y_at, dpa[:, cols])
            awob_ref[k] += _dot(y_bt, dpb[:, cols])
            dy_a = dy_a + _dot_nt(dpa[:, cols], woa_ref[k])
            dy_b = dy_b + _dot_nt(dpb[:, cols], wob_ref[k])
        dob = (dy_a * sa).astype(BF16)
        do_ref[...] = dob
        dot_ref[...] = dob.T
        head = (lax.broadcasted_iota(jnp.int32, (HEADS, MLA_W), 1) // V_DIM
                == lax.broadcasted_iota(jnp.int32, (HEADS, MLA_W), 0)).astype(BF16)
        dl_ref[...] = sum(_dot_nt(head, term) for term in _split3(dob.astype(F32) * o))
        emit(dhm_ref, dhmt_ref, dbm_ref, 0, dy_a * o * dsa)
        dsg = dy_b * sb
        emit(dhm_ref, dhmt_ref, dbm_ref, 3 * SGU_W, dy_b * sgu * dsb)
        emit(dhm_ref, dhmt_ref, dbm_ref, SGU_W, dsg * mixed * dgu)
        dmixed = dsg * gu
        dvn_rows = []
        dbs_sum = jnp.zeros((CHUNK, SGU_W), F32)
        for c in range(nch):
            dm_c = dmixed[c * CHUNK:(c + 1) * CHUNK, :]
            dbs_sum = dbs_sum + dm_c
            blocks = []
            for p in range(npair):
                dmb = dm_c[:, p * CHUNK:(p + 1) * CHUNK].astype(BF16)
                blk = vn[c * CHUNK:(c + 1) * CHUNK, p * CHUNK:(p + 1) * CHUNK]
                blocks.append(jnp.where(left, _dot_tn(ms[2 * p], dmb), _dot_tn(ms[2 * p + 1], dmb)))
                zero = jnp.zeros_like(dmb)
                dws_ref[2 * p] += jnp.where(tril, _dot_nt(jnp.where(left, dmb, zero), blk), 0.0)
                dws_ref[2 * p + 1] += jnp.where(tril, _dot_nt(jnp.where(left, zero, dmb), blk), 0.0)
            dvn_rows.append(jnp.concatenate(blocks, axis=1))
        dbacc_ref[...] += dbs_sum
        dvn = jnp.concatenate(dvn_rows, axis=0)
        dsgg_ref[...] += jnp.sum(dvn * vhat, axis=0, keepdims=True)
        dsgb_ref[...] += jnp.sum(dvn, axis=0, keepdims=True)
        dvh = dvn * sgg_ref[...]
        dgv_in = rstd_v * (dvh - jnp.mean(dvh, axis=-1, keepdims=True)
                           - vhat * jnp.mean(dvh * vhat, axis=-1, keepdims=True))
        emit(dhm_ref, dhmt_ref, dbm_ref, 2 * SGU_W, dgv_in * dgv)

        @pl.when(i == nsteps - 1)
        def _():
            dwout_ref[...] = awout_ref[...].astype(BF16)
            dwoa_ref[...] = awoa_ref[...].astype(BF16)
            dwob_ref[...] = awob_ref[...].astype(BF16)
            grp = (lax.broadcasted_iota(jnp.int32, (SGU_W, CHUNK), 0) // V_DIM
                   == lax.broadcasted_iota(jnp.int32, (SGU_W, CHUNK), 1)).astype(BF16)
            hi, mid, lo = _split3(dbacc_ref[...])
            dbs_ref[...] = _dot(hi, grp) + _dot(mid, grp) + _dot(lo, grp)

    acc_shapes = [(D_MODEL, D_MODEL), woa.shape, wob.shape, (GROUPS, CHUNK, CHUNK), (CHUNK, CHUNK),
                  (1, D_MODEL), (1, D_MODEL), (1, SGU_W), (1, SGU_W), (1, 128), (1, GATE_W), (1, MID_W)]
    col_spec = lambda rows: pl.BlockSpec((rows, ts), lambda i: (0, i))
    return pl.pallas_call(
        body, name="mid", grid=(nsteps,),
        in_specs=[_row_spec(ts, D_MODEL), _row_spec(ts, D_MODEL), _row_spec(ts, MLA_W), _row_spec(ts, MID_W),
                  _row_spec(ts, GATE_W), _full_spec(woa.shape), _full_spec(wob.shape), _full_spec(wout.shape),
                  _full_spec(ln_g.shape), _full_spec(ln_b.shape), _full_spec(sg_g.shape), _full_spec(sg_b.shape),
                  _full_spec(w_s.shape), _full_spec(bsb.shape)],
        out_specs=[_row_spec(ts, D_MODEL), _row_spec(ts, GATE_W), _row_spec(ts, MID_W), _row_spec(ts, MLA_W),
                   col_spec(MLA_W), col_spec(HEADS), col_spec(GATE_W), col_spec(MID_W)]
        + [_full_spec(sh) for sh in acc_shapes],
        out_shape=[jax.ShapeDtypeStruct((s, D_MODEL), F32), jax.ShapeDtypeStruct((s, GATE_W), BF16),
                   jax.ShapeDtypeStruct((s, MID_W), BF16), jax.ShapeDtypeStruct((s, MLA_W), BF16),
                   jax.ShapeDtypeStruct((MLA_W, s), BF16), jax.ShapeDtypeStruct((HEADS, s), F32),
                   jax.ShapeDtypeStruct((GATE_W, s), BF16), jax.ShapeDtypeStruct((MID_W, s), BF16)]
        + [jax.ShapeDtypeStruct(sh, BF16 if n < 3 else F32) for n, sh in enumerate(acc_shapes)],
        scratch_shapes=[pltpu.VMEM((CHUNK, SGU_W), F32)] + [pltpu.VMEM(sh, F32) for sh in acc_shapes[:3]],
        compiler_params=_params(),
    )(x, tgt, o, hm, hg, woa, wob, wout, ln_g, ln_b, sg_g, sg_b, w_s, bsb)


def _lat_bwd(dq, dk, dv, hl, rc, rsl, rsh, g_q, g_kv, wuq, wk, wv, after):
    s = dk.shape[0]
    ts = ROW_TILE
    qk_w = HEADS * HEAD_PAD

    def body(dq_ref, dk_ref, dv_ref, hl_ref, rc_ref, rsl_ref, rsh_ref, gq_ref, gkv_ref, wuq_ref, wk_ref, wv_ref,
             after_ref, dhl_ref, dhlt_ref, dwuq_ref, dwk_ref, dwv_ref, dgq_ref, dgkv_ref, dbl_ref):
        i = pl.program_id(0)

        @pl.when(i == 0)
        def _():
            for r in (dwuq_ref, dwk_ref, dwv_ref, dgq_ref, dgkv_ref, dbl_ref):
                r[...] = jnp.zeros_like(r)

        def emit(lo, val):
            vb = val.astype(BF16)
            n = val.shape[1]
            dhl_ref[:, lo:lo + n] = vb
            dhlt_ref[lo:lo + n, :] = vb.T
            dbl_ref[:, lo:lo + n] += jnp.sum(val, axis=0, keepdims=True)

        c, sl, sh = rc_ref[...], rsl_ref[...], rsh_ref[...]
        lane = lax.broadcasted_iota(jnp.int32, (ts, HEAD_PAD), 1)
        pe = (lane >= NOPE) & (lane < QK_DIM)
        dkpe = jnp.zeros((ts, HEAD_PAD), F32)
        dqu = []
        for hd in range(HEADS):
            lanes = slice(hd * HEAD_PAD, (hd + 1) * HEAD_PAD)
            dqu.append(_rope_t(dq_ref[lanes, :].T, c, sl, sh).astype(BF16))
            dkpe = dkpe + dk_ref[:, lanes]
        dqu = jnp.concatenate(dqu, axis=1)
        dkpe = _rope_t(jnp.where(pe, dkpe, 0.0), c, sl, sh)

        cq = hl_ref[:, :Q_RANK]
        rq = lax.rsqrt(jnp.mean(cq * cq, axis=-1, keepdims=True) + RMS_EPS)
        cqh = cq * rq
        cqn = (cqh * gq_ref[...]).astype(BF16)
        dwuq_ref[...] += _dot_tn(cqn, dqu)
        dcqn = _dot_nt(dqu, wuq_ref[...])
        dgq_ref[...] += jnp.sum(dcqn * cqh, axis=0, keepdims=True)
        dch = dcqn * gq_ref[...]
        emit(0, rq * (dch - cqh * jnp.mean(dch * cqh, axis=-1, keepdims=True)))

        ckv = hl_ref[:, Q_RANK:Q_RANK + KV_RANK]
        rk = lax.rsqrt(jnp.mean(ckv * ckv, axis=-1, keepdims=True) + RMS_EPS)
        ckh = ckv * rk
        ckn = (ckh * gkv_ref[...]).astype(BF16)
        dkb = dk_ref[...].astype(BF16)
        dvb = dv_ref[...].astype(BF16)
        dwk_ref[...] += _dot_tn(ckn, dkb)
        dwv_ref[...] += _dot_tn(ckn, dvb)
        dckn = _dot_nt(dkb, wk_ref[...]) + _dot_nt(dvb, wv_ref[...])
        dgkv_ref[...] += jnp.sum(dckn * ckh, axis=0, keepdims=True)
        dkh = dckn * gkv_ref[...]
        emit(Q_RANK, rk * (dkh - ckh * jnp.mean(dkh * ckh, axis=-1, keepdims=True)))
        emit(Q_RANK + KV_RANK, dkpe)

    acc_shapes = [wuq.shape, wk.shape, wv.shape, g_q.shape, g_kv.shape, (1, LAT_W)]
    return pl.pallas_call(
        body, name="lat_bwd", grid=(s // ts,),
        in_specs=[pl.BlockSpec((qk_w, ts), lambda i: (0, i)), _row_spec(ts, qk_w), _row_spec(ts, MLA_W),
                  _row_spec(ts, LAT_W), _row_spec(ts, HEAD_PAD), _row_spec(ts, HEAD_PAD), _row_spec(ts, HEAD_PAD),
                  _full_spec(g_q.shape), _full_spec(g_kv.shape), _full_spec(wuq.shape), _full_spec(wk.shape),
                  _full_spec(wv.shape), pl.BlockSpec(memory_space=pl.ANY)],
        out_specs=[_row_spec(ts, LAT_W), pl.BlockSpec((LAT_W, ts), lambda i: (0, i))]
        + [_full_spec(sh) for sh in acc_shapes],
        out_shape=[jax.ShapeDtypeStruct((s, LAT_W), BF16), jax.ShapeDtypeStruct((LAT_W, s), BF16)]
        + [jax.ShapeDtypeStruct(sh, F32) for sh in acc_shapes],
        compiler_params=_params(),
    )(dq, dk, dv, hl, rc, rsl, rsh, g_q, g_kv, wuq, wk, wv, after)


def _dx(dr, dhg, dhm, dhl, wt, after):
    s = dr.shape[0]
    ts = MATMUL_ROW_TILE

    def body(dr_ref, dhg_ref, dhm_ref, dhl_ref, wt_ref, after_ref, dx_ref):
        dx_ref[...] = (ALPHA * dr_ref[...]
                       + _dot(dhg_ref[...], wt_ref[ROW_GATE:IN_W, :])
                       + _dot(dhm_ref[...], wt_ref[LAT_COLS:ROW_GATE, :])
                       + _dot(dhl_ref[:, 0:Q_RANK + KV_RANK], wt_ref[0:Q_RANK + KV_RANK, :])
                       + _dot(dhl_ref[:, Q_RANK + KV_RANK:], _kpe_rows(wt_ref)))

    return pl.pallas_call(
        body, name="dx", grid=(s // ts,),
        in_specs=[_row_spec(ts, D_MODEL), _row_spec(ts, GATE_W), _row_spec(ts, MID_W), _row_spec(ts, LAT_W),
                  _full_spec(wt.shape), pl.BlockSpec(memory_space=pl.ANY)],
        out_specs=_row_spec(ts, D_MODEL),
        out_shape=jax.ShapeDtypeStruct((s, D_MODEL), F32),
        compiler_params=_params(),
    )(dr, dhg, dhm, dhl, wt, after)


def _dwt_early(dhmt, dhgt, xb, col, after, name):
    tn = 512
    nm, ng = MID_W // tn, GATE_W // tn
    s = dhmt.shape[1]
    hc = D_MODEL // 2

    ks = s // 2

    def body(col_ref, dma_ref, dmb_ref, dga_ref, dgb_ref, xba_ref, xbb_ref, after_ref, dw_ref):
        i = pl.program_id(0)

        @pl.when(i < nm)
        def _():
            dw_ref[...] = (_dot(dma_ref[...], xba_ref[...]) + _dot(dmb_ref[...], xbb_ref[...])).astype(BF16)

        @pl.when(i >= nm)
        def _():
            dw_ref[...] = (_dot(dga_ref[...], xba_ref[...]) + _dot(dgb_ref[...], xbb_ref[...])).astype(BF16)

    def dh_spec(first, part):
        if first:
            return pl.BlockSpec((tn, ks), lambda i, col_ref: (jnp.minimum(i, nm - 1), part))
        return pl.BlockSpec((tn, ks), lambda i, col_ref: (jnp.maximum(i - nm, 0), part))

    rows = pl.pallas_call(
        body, name=name,
        grid_spec=pltpu.PrefetchScalarGridSpec(
            num_scalar_prefetch=1, grid=(nm + ng,),
            in_specs=[dh_spec(True, 0), dh_spec(True, 1), dh_spec(False, 0), dh_spec(False, 1),
                      pl.BlockSpec((None, ks, hc), lambda i, col_ref: (col_ref[0], 0, 0)),
                      pl.BlockSpec((None, ks, hc), lambda i, col_ref: (col_ref[0], 1, 0)),
                      pl.BlockSpec(memory_space=pl.ANY)],
            out_specs=pl.BlockSpec((pl.Element(tn), pl.Element(hc)),
                                   lambda i, col_ref: (pl.multiple_of(LAT_COLS + i * tn, 32), 0))),
        out_shape=jax.ShapeDtypeStruct((IN_W, hc), BF16),
        compiler_params=_params(),
    )(col, dhmt, dhmt, dhgt, dhgt, xb, xb, after)

    def zero(buf_ref, out_ref):
        out_ref[...] = jnp.zeros_like(out_ref)

    return pl.pallas_call(
        zero, name=name + "_zero_lat", grid=(1,), in_specs=[pl.BlockSpec(memory_space=pl.ANY)],
        out_specs=pl.BlockSpec((LAT_COLS, hc), lambda i: (0, 0)),
        out_shape=jax.ShapeDtypeStruct((IN_W, hc), BF16), input_output_aliases={0: 0},
    )(rows)


def _dwt_lat(dhlt, xb):
    n, s = dhlt.shape

    def body(dht_ref, xb_ref, dw_ref):
        dw = _dot(dht_ref[...], xb_ref[...]).astype(BF16)
        kpe = Q_RANK + KV_RANK + NOPE
        dw_ref[0:Q_RANK + KV_RANK, :] = dw[0:Q_RANK + KV_RANK]
        dw_ref[Q_RANK + KV_RANK:LAT_COLS, :] = dw[kpe:kpe + ROPE]
        dw_ref[LAT_COLS:, :] = jnp.zeros((LAT_ROWS_PAD - LAT_COLS, D_MODEL), BF16)

    return pl.pallas_call(
        body, name="dwt_lat", in_specs=[VMEM_SPEC, VMEM_SPEC], out_specs=VMEM_SPEC,
        out_shape=jax.ShapeDtypeStruct((LAT_ROWS_PAD, D_MODEL), BF16),
        compiler_params=pltpu.CompilerParams(vmem_limit_bytes=VMEM_LIMIT),
    )(dhlt, xb)


def _split_bias(b):
    z = lambda n: jnp.zeros((n,), b.dtype)
    lat = jnp.concatenate([b[:Q_RANK + KV_RANK], z(NOPE), b[Q_RANK + KV_RANK:LAT_COLS], z(HEAD_PAD - QK_DIM)])
    return b[None, ROW_GATE:], b[None, LAT_COLS:ROW_GATE], lat[None, :]


def _join_bias(g, m, l):
    kpe = Q_RANK + KV_RANK + NOPE
    return jnp.concatenate([l[0, :Q_RANK + KV_RANK], l[0, kpe:kpe + ROPE], m[0], g[0]])


def _rope_tables(positions):
    half = ROPE // 2
    inv_freq = ROPE_THETA ** (-jnp.arange(0, ROPE, 2, dtype=F32) / ROPE)
    ang = positions.astype(F32)[:, None] * inv_freq
    cos, sin = jnp.cos(ang), jnp.sin(ang)
    n = positions.shape[0]
    one, zero = jnp.ones((n, NOPE), F32), jnp.zeros((n, half), F32)
    tail1, tail0 = jnp.ones((n, HEAD_PAD - QK_DIM), F32), jnp.zeros((n, HEAD_PAD - QK_DIM), F32)
    z64 = jnp.zeros((n, NOPE), F32)
    rc = jnp.concatenate([one, cos, cos, tail1], axis=1)
    rsl = jnp.concatenate([z64, -sin, zero, tail0], axis=1)
    rsh = jnp.concatenate([z64, zero, sin, tail0], axis=1)
    return rc, rsl, rsh


def _local_attention(x, positions, wlat, b_in, g_q, w_uq, g_kv, w_ukv, after):
    rc, rsl, rsh = _rope_tables(positions)
    b_g, b_m, b_l = _split_bias(b_in)
    wuq = jnp.pad(w_uq, ((0, 0), (0, 0), (0, HEAD_PAD - QK_DIM))).reshape(Q_RANK, HEADS * HEAD_PAD).astype(BF16)
    wk = jnp.pad(w_ukv[:, :, :NOPE], ((0, 0), (0, 0), (0, HEAD_PAD - NOPE))).reshape(KV_RANK, HEADS * HEAD_PAD).astype(BF16)
    wv = w_ukv[:, :, NOPE:].reshape(KV_RANK, MLA_W).astype(BF16)
    gq2, gkv2 = g_q[None, :], g_kv[None, :]
    hl, q, k, v, xb, qt, kt, vt, xb2 = _fwd_lat(x, wlat, b_l, gq2, wuq, gkv2, wk, wv, rc, rsl, rsh, after)
    o, lse = _attn_fwd(qt, k, vt)
    return dict(q=q, qt=qt, k=k, kt=kt, v=v, o=o, lse=lse, hl=hl, rc=rc, rsl=rsl, rsh=rsh, gq2=gq2, gkv2=gkv2,
                wuq=wuq, wk=wk, wv=wv, xb=xb, xb2=xb2, b_g=b_g, b_m=b_m)


def _local_head(st, x, tgt, wt, w_oa, sg_g, sg_b, w_s, b_s, w_ob, w_out, ln_g, ln_b):
    q, qt, k, kt, v, o, lse, hl, xb = (st[n] for n in ("q", "qt", "k", "kt", "v", "o", "lse", "hl", "xb"))
    rc, rsl, rsh, gq2, gkv2, wuq, wk, wv = (st[n] for n in ("rc", "rsl", "rsh", "gq2", "gkv2", "wuq", "wk", "wv"))
    bsb = jnp.repeat(b_s.T, V_DIM, axis=1)
    hg, hm = _fwd_rest(xb, wt, st["b_g"], st["b_m"])
    (dr, dhg, dhm, do, dot, delta, dhgt, dhmt, dwout, dwoa, dwob, dws, dbs, dlng, dlnb, dsgg, dsgb, loss, dbg,
     dbm) = _mid(x, tgt, o, hm, hg, w_oa, w_ob, w_out, ln_g[None, :], ln_b[None, :], sg_g[None, :], sg_b[None, :],
                 w_s, bsb)
    delta = jnp.pad(delta.reshape(HEADS // 2, 2, -1), ((0, 0), (0, 6), (0, 0)))
    early = {
        "w_oa": dwoa, "sgu_ln_g": dsgg[0], "sgu_ln_b": dsgb[0], "w_s": dws, "b_s": dbs[:, :GROUPS].T,
        "w_ob": dwob, "w_out": dwout, "ln_g": dlng[0], "ln_b": dlnb[0],
    }
    state = dict(q=q, qt=qt, k=k, kt=kt, v=v, do=do, dot=dot, lse=lse, delta=delta, hl=hl, rc=rc, rsl=rsl, rsh=rsh,
                 gq2=gq2, gkv2=gkv2, wuq=wuq, wk=wk, wv=wv, dr=dr, dhg=dhg, dhm=dhm, wt=wt, xb=xb, dbg=dbg, dbm=dbm,
                 dhgt=dhgt, dhmt=dhmt, xb2=st["xb2"])
    return loss, early, state


def _local_attn_bwd(st):
    return _attn_bwd(st["q"], st["qt"], st["k"], st["kt"], st["v"], st["do"], st["dot"], st["lse"], st["delta"])


def _local_tail(st, dq, dk, dv, after):
    dhl, dhlt, dwuq, dwk, dwv, dgq, dgkv, dbl = _lat_bwd(dq, dk, dv, st["hl"], st["rc"], st["rsl"], st["rsh"],
                                                         st["gq2"], st["gkv2"], st["wuq"], st["wk"], st["wv"], after)
    late = {
        "w_lat": _dwt_lat(dhlt, st["xb"]),
        "b_in": _join_bias(st["dbg"], st["dbm"], dbl),
        "g_q": dgq[0],
        "w_uq": dwuq.reshape(Q_RANK, HEADS, HEAD_PAD)[:, :, :QK_DIM],
        "g_kv": dgkv[0],
        "w_ukv": jnp.concatenate([dwk.reshape(KV_RANK, HEADS, HEAD_PAD)[:, :, :NOPE],
                                  dwv.reshape(KV_RANK, HEADS, V_DIM)], axis=2),
    }
    return dhl, late


def _local_step(x, positions, tgt, wt, b_in, g_q, w_uq, g_kv, w_ukv, w_oa, sg_g, sg_b, w_s, b_s, w_ob, w_out, ln_g,
                ln_b):
    st = _local_attention(x, positions, wt[:LAT_COLS], b_in, g_q, w_uq, g_kv, w_ukv, b_in)
    loss, early, st = _local_head(st, x, tgt, wt, w_oa, sg_g, sg_b, w_s, b_s, w_ob, w_out, ln_g, ln_b)
    dq, dk, dv = _local_attn_bwd(st)
    dhl, late = _local_tail(st, dq, dk, dv, dv)
    dx = _dx(st["dr"], st["dhg"], st["dhm"], dhl, st["wt"], dhl)
    grads = {**early, **late}
    halves = [_dwt_early(st["dhmt"], st["dhgt"], st["xb2"], jnp.full((1,), h, jnp.int32), dhl, "dwt_half%d" % h)
              for h in range(2)]
    grads["w_in"] = jnp.concatenate([grads.pop("w_lat")[:LAT_COLS], jnp.concatenate(halves, axis=1)[LAT_COLS:]],
                                    axis=0)
    return loss, dx, grads


MESH = pl.DeviceIdType.MESH
N_CHIPS = 4
HBM_SPEC = pl.BlockSpec(memory_space=pl.ANY)
HBM_SPEC_STRICT = pl.BlockSpec(memory_space=pltpu.HBM)
VMEM_SPEC = pl.BlockSpec(memory_space=pltpu.VMEM)

REP_ROWS = 80


def _rows8(a):
    flat = a.reshape(-1)
    n = -(-flat.shape[0] // (8 * D_MODEL)) * 8 * D_MODEL
    return jnp.pad(flat, (0, n - flat.shape[0])).reshape(-1, D_MODEL)


def _place():
    x, y, c = lax.axis_index("x"), lax.axis_index("y"), lax.axis_index("c")
    others = [(1 - x, y), (x, 1 - y), (1 - x, 1 - y)]
    return x, y, c, others


def _gather_weights(shards):
    n = len(shards)

    def body(*refs):
        ins, outs, bufs = refs[:n], refs[n:2 * n], refs[2 * n:3 * n]
        send_sems, recv_sems, local_sems = refs[3 * n:]
        x, y, c, others = _place()
        me = 2 * x + y
        sibling = (x, y, 1 - c)
        for src, buf in zip(ins, bufs):
            buf[...] = src[...].astype(BF16)
        own = [pltpu.make_async_copy(bufs[w], outs[w].at[me], local_sems.at[w]) for w in range(n)]
        for cp in own:
            cp.start()

        def part(w, chip, half):
            hc = shards[w].shape[1] // 2
            return outs[w].at[chip, :, pl.ds(half * hc, hc)]

        def sent(w, j):
            hc = shards[w].shape[1] // 2
            return pltpu.make_async_remote_copy(
                src_ref=bufs[w].at[:, pl.ds(c * hc, hc)], dst_ref=part(w, me, c),
                send_sem=send_sems.at[w * 3 + j], recv_sem=recv_sems.at[w * 3 + j],
                device_id=(*others[j], c), device_id_type=MESH)

        def landed(w, j):
            px, py = others[j]
            return pltpu.make_async_remote_copy(
                src_ref=part(w, 2 * px + py, c), dst_ref=part(w, 2 * px + py, c),
                send_sem=send_sems.at[w * 3 + j], recv_sem=recv_sems.at[w * 3 + j],
                device_id=(px, py, c), device_id_type=MESH)

        def passed(w, j, half):
            px, py = others[j]
            k = n * 3 + w * 3 + j
            return pltpu.make_async_remote_copy(
                src_ref=part(w, 2 * px + py, half), dst_ref=part(w, 2 * px + py, half),
                send_sem=send_sems.at[k], recv_sem=recv_sems.at[k], device_id=sibling, device_id_type=MESH)

        first = [sent(w, j) for w in range(n) for j in range(3)]
        for cp in first:
            cp.start()
        fwd = []
        for w in range(n):
            for j in range(3):
                landed(w, j).wait_recv()
                cp = passed(w, j, c)
                cp.start()
                fwd.append(cp)
        for w in range(n):
            for j in range(3):
                passed(w, j, 1 - c).wait_recv()
        for cp in first + fwd:
            cp.wait_send()
        for cp in own:
            cp.wait()

    return pl.pallas_call(
        body, name="gather_weights",
        in_specs=[VMEM_SPEC] * n, out_specs=[HBM_SPEC] * n,
        out_shape=[jax.ShapeDtypeStruct((N_CHIPS,) + s.shape, BF16) for s in shards],
        scratch_shapes=[pltpu.VMEM(s.shape, BF16) for s in shards]
        + [pltpu.SemaphoreType.DMA((6 * n,)), pltpu.SemaphoreType.DMA((6 * n,)), pltpu.SemaphoreType.DMA((n,))],
        compiler_params=pltpu.CompilerParams(vmem_limit_bytes=VMEM_LIMIT),
    )(*shards)


N_DEV = 8
LOSS_TILE = (8, 128)


def _gather_first(lat, uq):
    hl, hu = lat.shape[1] // 2, uq.shape[1] // 2

    def body(lat_ref, uq_ref, wlat_ref, guq_ref, lat_buf, uq_buf, send_sems, recv_sems, local_sems):
        x, y, c, others = _place()
        me = 2 * x + y
        sibling = (x, y, 1 - c)
        lat_buf[...] = lat_ref[...].astype(BF16)
        uq_buf[...] = uq_ref[...].astype(BF16)

        def copy(src, dst, k, to):
            return pltpu.make_async_remote_copy(src_ref=src, dst_ref=dst, send_sem=send_sems.at[k],
                                                recv_sem=recv_sems.at[k], device_id=to, device_id_type=MESH)

        def uq_part(chip, half):
            return guq_ref.at[chip, :, pl.ds(half * hu, hu)]

        def lat_part(half):
            return wlat_ref.at[:, pl.ds(half * hl, hl)]

        own = pltpu.make_async_copy(uq_buf, guq_ref.at[me], local_sems.at[0])
        own.start()
        first = [copy(uq_buf.at[:, pl.ds(c * hu, hu)], uq_part(me, c), j, (*others[j], c)) for j in range(3)]
        for cp in first:
            cp.start()

        @pl.when(me == 0)
        def _():
            mine = pltpu.make_async_copy(lat_buf, wlat_ref, local_sems.at[1])
            mine.start()
            cps = [copy(lat_buf.at[:, pl.ds(c * hl, hl)], lat_part(c), 6 + j, (*others[j], c)) for j in range(3)]
            for cp in cps:
                cp.start()
            for cp in cps:
                cp.wait_send()
            mine.wait()

        @pl.when(me != 0)
        def _():
            j0 = x + 2 * y - 1
            copy(lat_part(c), lat_part(c), 6 + j0, (0, 0, c)).wait_recv()
            fwd = copy(lat_part(c), lat_part(c), 9, sibling)
            fwd.start()
            copy(lat_part(1 - c), lat_part(1 - c), 9, sibling).wait_recv()
            fwd.wait_send()

        fwd = []
        for j, (px, py) in enumerate(others):
            chip = 2 * px + py
            copy(uq_part(chip, c), uq_part(chip, c), j, (px, py, c)).wait_recv()
            cp = copy(uq_part(chip, c), uq_part(chip, c), 3 + j, sibling)
            cp.start()
            fwd.append(cp)
        for j, (px, py) in enumerate(others):
            chip = 2 * px + py
            copy(uq_part(chip, 1 - c), uq_part(chip, 1 - c), 3 + j, sibling).wait_recv()
        for cp in first + fwd:
            cp.wait_send()
        own.wait()

    return pl.pallas_call(
        body, name="gather_first", in_specs=[VMEM_SPEC, VMEM_SPEC], out_specs=[HBM_SPEC, HBM_SPEC],
        out_shape=[jax.ShapeDtypeStruct(lat.shape, BF16), jax.ShapeDtypeStruct((N_CHIPS,) + uq.shape, BF16)],
        scratch_shapes=[pltpu.VMEM(lat.shape, BF16), pltpu.VMEM(uq.shape, BF16), pltpu.SemaphoreType.DMA((10,)),
                        pltpu.SemaphoreType.DMA((10,)), pltpu.SemaphoreType.DMA((2,))],
        compiler_params=pltpu.CompilerParams(vmem_limit_bytes=VMEM_LIMIT),
    )(lat, uq)


def _cast_own(shards, me, after):
    n = len(shards)

    def body(me_ref, *refs):
        for w in range(n):
            refs[n + 1 + w][...] = refs[w][...].astype(BF16)

    return pl.pallas_call(
        body, name="cast_own",
        grid_spec=pltpu.PrefetchScalarGridSpec(
            num_scalar_prefetch=1, grid=(1,),
            in_specs=[pl.BlockSpec(s.shape, lambda i, me_ref: (0, 0)) for s in shards]
            + [pl.BlockSpec(memory_space=pl.ANY)],
            out_specs=[pl.BlockSpec((None,) + s.shape, lambda i, me_ref: (me_ref[0], 0, 0)) for s in shards]),
        out_shape=[jax.ShapeDtypeStruct((N_CHIPS,) + s.shape, BF16) for s in shards],
        compiler_params=pltpu.CompilerParams(vmem_limit_bytes=VMEM_LIMIT),
    )(me, *shards, after)


def _cast_first(lat, uq, me):
    def body(me_ref, lat_ref, uq_ref, wlat_ref, guq_ref):
        wlat_ref[...] = lat_ref[...].astype(BF16)
        guq_ref[...] = uq_ref[...].astype(BF16)

    return pl.pallas_call(
        body, name="cast_first",
        grid_spec=pltpu.PrefetchScalarGridSpec(
            num_scalar_prefetch=1, grid=(1,),
            in_specs=[pl.BlockSpec(lat.shape, lambda i, me_ref: (0, 0)),
                      pl.BlockSpec(uq.shape, lambda i, me_ref: (0, 0))],
            out_specs=[pl.BlockSpec(lat.shape, lambda i, me_ref: (0, 0)),
                       pl.BlockSpec((None,) + uq.shape, lambda i, me_ref: (me_ref[0], 0, 0))]),
        out_shape=[jax.ShapeDtypeStruct(lat.shape, BF16), jax.ShapeDtypeStruct((N_CHIPS,) + uq.shape, BF16)],
    )(me, lat, uq)


def _first_copies(wlat_ref, guq_ref, send_sems, recv_sems, shapes):
    x, y, c, others = _place()
    me = 2 * x + y
    hl, hu = shapes[0][1] // 2, shapes[1][2] // 2
    lat_half = wlat_ref.at[:, pl.ds(c * hl, hl)]

    def copy(src, dst, k, to):
        return pltpu.make_async_remote_copy(src_ref=src, dst_ref=dst, send_sem=send_sems.at[k],
                                            recv_sem=recv_sems.at[k], device_id=to, device_id_type=MESH)

    def uq_half(chip):
        return guq_ref.at[chip, :, pl.ds(c * hu, hu)]

    lat_out = [copy(lat_half, lat_half, j, (*others[j], c)) for j in range(3)]
    uq_out = [copy(uq_half(me), uq_half(me), 3 + j, (*others[j], c)) for j in range(3)]
    j0 = jnp.maximum(x + 2 * y - 1, 0)
    lat_in = copy(lat_half, lat_half, j0, (0, 0, c))
    uq_in = [copy(uq_half(me), uq_half(2 * px + py), 3 + j, (px, py, c)) for j, (px, py) in enumerate(others)]
    return me, lat_out, uq_out, lat_in, uq_in


def _first_start(wlat, guq):
    shapes = (wlat.shape, guq.shape)

    def body(wlat_ref, guq_ref, send_sems, recv_sems, wlat_thru, guq_thru, token):
        me, lat_out, uq_out, _, _ = _first_copies(wlat_ref, guq_ref, send_sems, recv_sems, shapes)

        @pl.when(me == 0)
        def _():
            for cp in lat_out:
                cp.start()

        for cp in uq_out:
            cp.start()
        token[...] = jnp.zeros_like(token)

    outs = pl.pallas_call(
        body, name="first_start",
        out_shape=(pltpu.SemaphoreType.DMA((6,)), pltpu.SemaphoreType.DMA((6,)), pltpu.HBM(wlat.shape, BF16),
                   pltpu.HBM(guq.shape, BF16), jax.ShapeDtypeStruct(LOSS_TILE, F32)),
        in_specs=[HBM_SPEC_STRICT] * 2, out_specs=(SEM_SPEC, SEM_SPEC, HBM_SPEC_STRICT, HBM_SPEC_STRICT, VMEM_SPEC),
        input_output_aliases={0: 2, 1: 3},
        compiler_params=pltpu.CompilerParams(has_side_effects=SPLIT_EFFECT),
    )(pltpu.with_memory_space_constraint(wlat, pltpu.HBM), pltpu.with_memory_space_constraint(guq, pltpu.HBM))
    return outs


def _first_wait(send_sems, recv_sems, wlat, guq, after):
    shapes = (wlat.shape, guq.shape)

    def body(wlat_ref, guq_ref, send_sems, recv_sems, after_ref, wlat_out, guq_out):
        me, lat_out, uq_out, lat_in, uq_in = _first_copies(wlat_ref, guq_ref, send_sems, recv_sems, shapes)

        @pl.when(me == 0)
        def _():
            for cp in lat_out:
                cp.wait_send()

        @pl.when(me != 0)
        def _():
            lat_in.wait_recv()

        for cp in uq_out:
            cp.wait_send()
        for cp in uq_in:
            cp.wait_recv()

    return pl.pallas_call(
        body, name="first_wait", out_shape=(pltpu.HBM(wlat.shape, BF16), pltpu.HBM(guq.shape, BF16)),
        in_specs=[HBM_SPEC_STRICT, HBM_SPEC_STRICT, SEM_SPEC, SEM_SPEC, HBM_SPEC],
        out_specs=(HBM_SPEC_STRICT, HBM_SPEC_STRICT), input_output_aliases={0: 0, 1: 1},
        compiler_params=pltpu.CompilerParams(has_side_effects=SPLIT_EFFECT),
    )(wlat, guq, send_sems, recv_sems, after)


def _first_forward(wlat, guq):
    hl, hu = wlat.shape[1] // 2, guq.shape[2] // 2

    def body(wlat_in, guq_in, wlat_ref, guq_ref, send_sems, recv_sems):
        x, y, c, others = _place()
        me = 2 * x + y
        sibling = (x, y, 1 - c)

        def copy(part, k):
            return pltpu.make_async_remote_copy(src_ref=part, dst_ref=part, send_sem=send_sems.at[k],
                                                recv_sem=recv_sems.at[k], device_id=sibling, device_id_type=MESH)

        def uq_part(j, half):
            px, py = others[j]
            return guq_ref.at[2 * px + py, :, pl.ds(half * hu, hu)]

        cps = [copy(uq_part(j, c), j) for j in range(3)]
        for cp in cps:
            cp.start()

        @pl.when(me != 0)
        def _():
            mine = copy(wlat_ref.at[:, pl.ds(c * hl, hl)], 3)
            mine.start()
            copy(wlat_ref.at[:, pl.ds((1 - c) * hl, hl)], 3).wait_recv()
            mine.wait_send()

        for j in range(3):
            copy(uq_part(j, 1 - c), j).wait_recv()
        for cp in cps:
            cp.wait_send()

    return pl.pallas_call(
        body, name="first_forward", in_specs=[HBM_SPEC, HBM_SPEC], out_specs=[HBM_SPEC, HBM_SPEC],
        out_shape=[jax.ShapeDtypeStruct(wlat.shape, BF16), jax.ShapeDtypeStruct(guq.shape, BF16)],
        input_output_aliases={0: 0, 1: 1},
        scratch_shapes=[pltpu.SemaphoreType.DMA((4,)), pltpu.SemaphoreType.DMA((4,))],
    )(wlat, guq)


def _gather_start(bufs, after):
    n = len(bufs)

    def body(*refs):
        b_refs = refs[:n]
        send_sems, recv_sems, token = refs[n + 1], refs[n + 2], refs[-1]
        x, y, c, others = _place()
        me = 2 * x + y
        for w in range(n):
            hc = _half(bufs[w])
            mine = b_refs[w].at[me, :, pl.ds(c * hc, hc)]
            for j, (px, py) in enumerate(others):
                pltpu.make_async_remote_copy(
                    src_ref=mine, dst_ref=mine, send_sem=send_sems.at[3 * w + j], recv_sem=recv_sems.at[3 * w + j],
                    device_id=(px, py, c), device_id_type=MESH).start()
        token[...] = jnp.zeros_like(token)

    hbm = [pltpu.HBM(b.shape, BF16) for b in bufs]
    outs = pl.pallas_call(
        body, name="gather_start",
        out_shape=(pltpu.SemaphoreType.DMA((3 * n,)), pltpu.SemaphoreType.DMA((3 * n,)), *hbm,
                   jax.ShapeDtypeStruct(LOSS_TILE, F32)),
        in_specs=[HBM_SPEC_STRICT] * n + [HBM_SPEC],
        out_specs=(SEM_SPEC, SEM_SPEC, *[HBM_SPEC_STRICT] * n, VMEM_SPEC),
        input_output_aliases={i: 2 + i for i in range(n)},
        compiler_params=pltpu.CompilerParams(has_side_effects=SPLIT_EFFECT),
    )(*[pltpu.with_memory_space_constraint(b, pltpu.HBM) for b in bufs], after)
    return outs[0], outs[1], list(outs[2:2 + n]), outs[-1]


def _gather_wait(send_sems, recv_sems, bufs, after):
    n = len(bufs)

    def body(*refs):
        b_refs = refs[:n]
        send_sems, recv_sems = refs[n], refs[n + 1]
        x, y, c, others = _place()
        me = 2 * x + y
        for w in range(n):
            hc = _half(bufs[w])
            for j, (px, py) in enumerate(others):
                cp = pltpu.make_async_remote_copy(
                    src_ref=b_refs[w].at[me, :, pl.ds(c * hc, hc)],
                    dst_ref=b_refs[w].at[2 * px + py, :, pl.ds(c * hc, hc)],
                    send_sem=send_sems.at[3 * w + j], recv_sem=recv_sems.at[3 * w + j], device_id=(px, py, c),
                    device_id_type=MESH)
                cp.wait_send()
                cp.wait_recv()

    outs = pl.pallas_call(
        body, name="gather_wait", out_shape=tuple(pltpu.HBM(b.shape, b.dtype) for b in bufs),
        in_specs=[HBM_SPEC_STRICT] * n + [SEM_SPEC, SEM_SPEC, HBM_SPEC],
        out_specs=tuple([HBM_SPEC_STRICT] * n), input_output_aliases={i: i for i in range(n)},
        compiler_params=pltpu.CompilerParams(has_side_effects=SPLIT_EFFECT),
    )(*bufs, send_sems, recv_sems, after)
    return list(outs)


def _gather_finish(bufs):
    n = len(bufs)

    def body(*refs):
        b_refs = refs[n:2 * n]
        send_sems, recv_sems = refs[2 * n:]
        x, y, c, others = _place()
        cps = []
        for w in range(n):
            hc = _half(bufs[w])
            for j, (px, py) in enumerate(others):
                part = b_refs[w].at[2 * px + py, :, pl.ds(c * hc, hc)]
                cps.append(pltpu.make_async_remote_copy(
                    src_ref=part, dst_ref=part, send_sem=send_sems.at[3 * w + j], recv_sem=recv_sems.at[3 * w + j],
                    device_id=(x, y, 1 - c), device_id_type=MESH))
        for cp in cps:
            cp.start()
        for w in range(n):
            hc = _half(bufs[w])
            for j, (px, py) in enumerate(others):
                theirs = b_refs[w].at[2 * px + py, :, pl.ds((1 - c) * hc, hc)]
                pltpu.make_async_remote_copy(
                    src_ref=theirs, dst_ref=theirs, send_sem=send_sems.at[3 * w + j], recv_sem=recv_sems.at[3 * w + j],
                    device_id=(x, y, 1 - c), device_id_type=MESH).wait_recv()
        for cp in cps:
            cp.wait_send()

    return pl.pallas_call(
        body, name="gather_finish", in_specs=[HBM_SPEC] * n, out_specs=[HBM_SPEC] * n,
        out_shape=[jax.ShapeDtypeStruct(b.shape, b.dtype) for b in bufs],
        input_output_aliases={i: i for i in range(n)},
        scratch_shapes=[pltpu.SemaphoreType.DMA((3 * n,)), pltpu.SemaphoreType.DMA((3 * n,))],
    )(*bufs)


def _half(a):
    return a.shape[-1] // 2


def _exchange_pairs(parts, name):
    n = len(parts)

    def body(*refs):
        p_refs, r_refs = refs[:n], refs[n:2 * n]
        send_sems, recv_sems = refs[2 * n:]
        x, y, c, _ = _place()
        cps = []
        for w in range(n):
            h = _half(parts[w])
            cps.append(pltpu.make_async_remote_copy(
                src_ref=p_refs[w].at[:, :, pl.ds((1 - c) * h, h)], dst_ref=r_refs[w],
                send_sem=send_sems.at[w], recv_sem=recv_sems.at[w], device_id=(x, y, 1 - c), device_id_type=MESH))
        for cp in cps:
            cp.start()
        for cp in cps:
            cp.wait()

    return pl.pallas_call(
        body, name=name, in_specs=[HBM_SPEC] * n, out_specs=[HBM_SPEC] * n,
        out_shape=[jax.ShapeDtypeStruct((N_CHIPS, p.shape[1], _half(p)), BF16) for p in parts],
        scratch_shapes=[pltpu.SemaphoreType.DMA((n,)), pltpu.SemaphoreType.DMA((n,))],
    )(*parts)


def _sibling_part(ref, w, n_whole, shape, c):
    if w < n_whole:
        return ref
    h = shape[-1] // 2
    return ref.at[:, :, pl.ds((1 - c) * h, h)]


def _pairs_start(parts, all_loss, n_whole):
    n = len(parts)

    def body(*refs):
        p_refs, r_refs, loss_ref = refs[:n], refs[n:2 * n], refs[2 * n]
        send_sems, recv_sems, token = refs[2 * n + 1], refs[2 * n + 2], refs[-1]
        x, y, c, _ = _place()
        for w in range(n):
            h = _half(parts[w])
            pltpu.make_async_remote_copy(
                src_ref=_sibling_part(p_refs[w], w, n_whole, parts[w].shape, c), dst_ref=r_refs[w],
                send_sem=send_sems.at[w], recv_sem=recv_sems.at[w], device_id=(x, y, 1 - c),
                device_id_type=MESH).start()
        me = 4 * x + 2 * y + c
        for t in range(1, N_DEV):
            d = (me + t) % N_DEV
            pltpu.make_async_remote_copy(
                src_ref=loss_ref.at[me], dst_ref=loss_ref.at[me], send_sem=send_sems.at[n + t - 1],
                recv_sem=recv_sems.at[n + t - 1], device_id=(d // 4, (d // 2) % 2, d % 2), device_id_type=MESH).start()
        token[...] = jnp.zeros_like(token)

    lands = [pltpu.HBM(p.shape if w < n_whole else (N_CHIPS, p.shape[1], _half(p)), BF16)
             for w, p in enumerate(parts)]
    nsem = n + N_DEV - 1
    outs = pl.pallas_call(
        body, name="pairs_start",
        out_shape=(pltpu.SemaphoreType.DMA((nsem,)), pltpu.SemaphoreType.DMA((nsem,)),
                   *[pltpu.HBM(p.shape, p.dtype) for p in parts], *lands, pltpu.HBM(all_loss.shape, F32),
                   jax.ShapeDtypeStruct(LOSS_TILE, F32)),
        in_specs=[HBM_SPEC_STRICT] * (2 * n + 1),
        out_specs=(SEM_SPEC, SEM_SPEC, *[HBM_SPEC_STRICT] * (2 * n + 1), VMEM_SPEC),
        input_output_aliases={i: 2 + i for i in range(2 * n + 1)},
        compiler_params=pltpu.CompilerParams(has_side_effects=SPLIT_EFFECT),
    )(*[pltpu.with_memory_space_constraint(p, pltpu.HBM) for p in parts],
      *[pltpu.with_memory_space_constraint(lax.empty(l.shape, BF16), pltpu.HBM) for l in lands],
      pltpu.with_memory_space_constraint(all_loss, pltpu.HBM))
    return outs[0], outs[1], list(outs[2:2 + n]), list(outs[2 + n:2 + 2 * n]), outs[2 + 2 * n], outs[-1]


def _pairs_wait(send_sems, recv_sems, parts, lands, all_loss, after, n_whole):
    n = len(parts)

    def body(*refs):
        p_refs, r_refs, loss_ref = refs[:n], refs[n:2 * n], refs[2 * n]
        send_sems, recv_sems = refs[2 * n + 1], refs[2 * n + 2]
        x, y, c, _ = _place()
        for w in range(n):
            h = _half(parts[w])
            cp = pltpu.make_async_remote_copy(
                src_ref=_sibling_part(p_refs[w], w, n_whole, parts[w].shape, c), dst_ref=r_refs[w],
                send_sem=send_sems.at[w],
                recv_sem=recv_sems.at[w], device_id=(x, y, 1 - c), device_id_type=MESH)
            cp.wait_send()
            cp.wait_recv()
        me = 4 * x + 2 * y + c
        for t in range(1, N_DEV):
            d = (me + N_DEV - t) % N_DEV
            cp = pltpu.make_async_remote_copy(
                src_ref=loss_ref.at[me], dst_ref=loss_ref.at[d], send_sem=send_sems.at[n + t - 1],
                recv_sem=recv_sems.at[n + t - 1], device_id=(d // 4, (d // 2) % 2, d % 2), device_id_type=MESH)
            cp.wait_send()
            cp.wait_recv()

    bufs = (*parts, *lands, all_loss)
    outs = pl.pallas_call(
        body, name="pairs_wait", out_shape=tuple(pltpu.HBM(a.shape, a.dtype) for a in bufs),
        in_specs=[HBM_SPEC_STRICT] * len(bufs) + [SEM_SPEC, SEM_SPEC, HBM_SPEC],
        out_specs=tuple([HBM_SPEC_STRICT] * len(bufs)), input_output_aliases={i: i for i in range(len(bufs))},
        compiler_params=pltpu.CompilerParams(has_side_effects=SPLIT_EFFECT),
    )(*bufs, send_sems, recv_sems, after)
    return list(outs[:n]), list(outs[n:2 * n]), outs[2 * n]


def _add_pair_tiled(p, r):
    rows, h = r.shape[1:]

    def body(p_ref, r_ref, q_ref):
        q_ref[...] = (p_ref[...].astype(F32) + r_ref[...].astype(F32)).astype(BF16)

    spec = pl.BlockSpec((None, rows, h), lambda k: (k, 0, 0))
    return pl.pallas_call(
        body, name="add_pair_w_in", grid=(N_CHIPS,), in_specs=[spec, spec], out_specs=spec,
        out_shape=jax.ShapeDtypeStruct(r.shape, BF16),
    )(p, r)


def _add_pair_small(ps, rs, c, name):
    n = len(ps)

    def body(c_ref, *refs):
        for w in range(n):
            h = _half(ps[w])
            mine = refs[w][:, :, pl.ds(pl.multiple_of(c_ref[0] * h, 128), h)]
            refs[2 * n + w][...] = (mine.astype(F32) + refs[n + w][...].astype(F32)).astype(BF16)

    return pl.pallas_call(
        body, name=name,
        in_specs=[pl.BlockSpec(memory_space=pltpu.SMEM)] + [VMEM_SPEC] * (2 * n), out_specs=[VMEM_SPEC] * n,
        out_shape=[jax.ShapeDtypeStruct(r.shape, BF16) for r in rs],
        compiler_params=pltpu.CompilerParams(vmem_limit_bytes=VMEM_LIMIT),
    )(c, *ps, *rs)


def _exchange_chips(qs):
    n = len(qs)

    def body(*refs):
        q_refs, r_refs = refs[:n], refs[n:2 * n]
        send_sems, recv_sems = refs[2 * n:]
        x, y, c, others = _place()
        me = 2 * x + y
        cps = []
        for w in range(n):
            for j, (px, py) in enumerate(others):
                cps.append(pltpu.make_async_remote_copy(
                    src_ref=q_refs[w].at[2 * px + py], dst_ref=r_refs[w].at[me], send_sem=send_sems.at[3 * w + j],
                    recv_sem=recv_sems.at[3 * w + j], device_id=(px, py, c), device_id_type=MESH))
        for cp in cps:
            cp.start()
        for w in range(n):
            for j, (px, py) in enumerate(others):
                pltpu.make_async_remote_copy(
                    src_ref=q_refs[w].at[me], dst_ref=r_refs[w].at[2 * px + py], send_sem=send_sems.at[3 * w + j],
                    recv_sem=recv_sems.at[3 * w + j], device_id=(px, py, c), device_id_type=MESH).wait_recv()
        for cp in cps:
            cp.wait_send()

    return pl.pallas_call(
        body, name="exchange_chips", in_specs=[HBM_SPEC] * n, out_specs=[HBM_SPEC] * n,
        out_shape=[jax.ShapeDtypeStruct(q.shape, BF16) for q in qs],
        scratch_shapes=[pltpu.SemaphoreType.DMA((3 * n,)), pltpu.SemaphoreType.DMA((3 * n,))],
    )(*qs)


SEM_SPEC = pl.BlockSpec(memory_space=pltpu.SEMAPHORE)
SPLIT_EFFECT = pltpu.SideEffectType.DATAFLOW_SIDE_EFFECTING


def _chips_start(qs, name):
    n = len(qs)

    def body(*refs):
        q_refs, land_refs = refs[:n], refs[n:2 * n]
        send_sems, recv_sems, token = refs[2 * n], refs[2 * n + 1], refs[-1]
        x, y, c, others = _place()
        me = 2 * x + y
        for w in range(n):
            for j, (px, py) in enumerate(others):
                pltpu.make_async_remote_copy(
                    src_ref=q_refs[w].at[2 * px + py], dst_ref=land_refs[w].at[me], send_sem=send_sems.at[3 * w + j],
                    recv_sem=recv_sems.at[3 * w + j], device_id=(px, py, c), device_id_type=MESH).start()
        token[...] = jnp.zeros_like(token)

    hbm = [pltpu.HBM(q.shape, BF16) for q in qs]
    outs = pl.pallas_call(
        body, name=name,
        out_shape=(pltpu.SemaphoreType.DMA((3 * n,)), pltpu.SemaphoreType.DMA((3 * n,)), *hbm, *hbm,
                   jax.ShapeDtypeStruct(LOSS_TILE, F32)),
        in_specs=[HBM_SPEC_STRICT] * (2 * n),
        out_specs=(SEM_SPEC, SEM_SPEC, *[HBM_SPEC_STRICT] * (2 * n), VMEM_SPEC),
        input_output_aliases={i: 2 + i for i in range(2 * n)},
        compiler_params=pltpu.CompilerParams(has_side_effects=SPLIT_EFFECT),
    )(*[pltpu.with_memory_space_constraint(q, pltpu.HBM) for q in qs],
      *[pltpu.with_memory_space_constraint(lax.empty(q.shape, BF16), pltpu.HBM) for q in qs])
    return outs[0], outs[1], outs[2:2 + n], outs[2 + n:2 + 2 * n], outs[-1]


def _chips_wait(send_sems, recv_sems, q_thru, land_thru, after, name):
    n = len(q_thru)

    def body(*refs):
        q_refs, land_refs = refs[:n], refs[n:2 * n]
        send_sems, recv_sems = refs[2 * n], refs[2 * n + 1]
        x, y, c, others = _place()
        me = 2 * x + y
        for w in range(n):
            for j, (px, py) in enumerate(others):
                cp = pltpu.make_async_remote_copy(
                    src_ref=q_refs[w].at[2 * px + py], dst_ref=land_refs[w].at[2 * px + py],
                    send_sem=send_sems.at[3 * w + j], recv_sem=recv_sems.at[3 * w + j], device_id=(px, py, c),
                    device_id_type=MESH)
                cp.wait_send()
                cp.wait_recv()

    outs = pl.pallas_call(
        body, name=name, out_shape=tuple(pltpu.HBM(a.shape, a.dtype) for a in (*q_thru, *land_thru)),
        in_specs=[HBM_SPEC_STRICT] * (2 * n) + [SEM_SPEC, SEM_SPEC, HBM_SPEC],
        out_specs=tuple([HBM_SPEC_STRICT] * (2 * n)), input_output_aliases={i: i for i in range(2 * n)},
        compiler_params=pltpu.CompilerParams(has_side_effects=SPLIT_EFFECT),
    )(*q_thru, *land_thru, send_sems, recv_sems, after)
    return list(outs[:n]), list(outs[n:])


def _sum_chips_tiled(q, r, idx, tile):
    rows, h = r.shape[1:]
    nt = h // tile

    def body(idx_ref, q_ref, r0_ref, r1_ref, r2_ref, g_ref):
        g_ref[...] = (q_ref[...].astype(F32) + r0_ref[...].astype(F32) + r1_ref[...].astype(F32)
                      + r2_ref[...].astype(F32))

    def slab(t):
        return pl.BlockSpec((None, rows, tile), lambda i, idx_ref: (idx_ref[t], 0, i))

    return pl.pallas_call(
        body, name="sum_chips_w_in",
        grid_spec=pltpu.PrefetchScalarGridSpec(
            num_scalar_prefetch=1, grid=(nt,), in_specs=[slab(0), slab(1), slab(2), slab(3)],
            out_specs=pl.BlockSpec((rows, tile), lambda i, idx_ref: (0, idx_ref[4] * nt + i))),
        out_shape=jax.ShapeDtypeStruct((rows, 2 * h), F32),
    )(idx, q, r, r, r)


def _sum_chips_small(qs, rs, idx, all_dtypes):
    n = len(rs)
    n_all = len(all_dtypes)

    def body(idx_ref, *refs):
        c = idx_ref[4]
        for w in range(n):
            q_ref, r_ref, g_ref = refs[w], refs[n + w], refs[2 * n + w]
            acc = q_ref[idx_ref[0]].astype(F32)
            for t in range(1, N_CHIPS):
                acc = acc + r_ref[idx_ref[t]].astype(F32)
            h = rs[w].shape[2]
            mine = pl.ds(pl.multiple_of(c * h, 128), h)
            g_ref[...] = jnp.zeros_like(g_ref)
            if w >= n - n_all:
                g_ref[idx_ref[0], :, mine] = acc.astype(g_ref.dtype)
            else:
                g_ref[:, mine] = acc

    shapes = [jax.ShapeDtypeStruct((r.shape[1], 2 * r.shape[2]), F32) for r in rs[:n - n_all]]
    shapes += [jax.ShapeDtypeStruct((N_CHIPS, r.shape[1], 2 * r.shape[2]), dt)
               for r, dt in zip(rs[n - n_all:], all_dtypes)]
    return pl.pallas_call(
        body, name="sum_chips_small",
        in_specs=[pl.BlockSpec(memory_space=pltpu.SMEM)] + [VMEM_SPEC] * (2 * n), out_specs=[VMEM_SPEC] * n,
        out_shape=shapes, compiler_params=pltpu.CompilerParams(vmem_limit_bytes=VMEM_LIMIT),
    )(idx, *qs, *rs)


def _share(shards, alls):
    n, na = len(shards), len(alls)
    total = n + na

    def body(*refs):
        g_refs, a_refs = refs[total:total + n], refs[total + n:2 * total]
        send_sems, recv_sems = refs[2 * total:]
        x, y, c, others = _place()
        me = 2 * x + y
        sibling = (x, y, 1 - c)

        def cols_of(w, half):
            h = shards[w].shape[1] // 2
            return g_refs[w].at[:, pl.ds(half * h, h)]

        def slab(a, chip, half):
            h = alls[a].shape[2] // 2
            return a_refs[a].at[chip, :, pl.ds(half * h, h)]

        def copy(src, dst, k, to):
            return pltpu.make_async_remote_copy(src_ref=src, dst_ref=dst, send_sem=send_sems.at[k],
                                                recv_sem=recv_sems.at[k], device_id=to, device_id_type=MESH)

        cps = [copy(cols_of(w, c), cols_of(w, c), w, sibling) for w in range(n)]
        for a in range(na):
            base = n + 7 * a
            cps.append(copy(slab(a, me, c), slab(a, me, c), base, sibling))
            for j, (px, py) in enumerate(others):
                cps.append(copy(slab(a, me, c), slab(a, me, c), base + 1 + j, (px, py, c)))
        for cp in cps:
            cp.start()
        fwd = []
        for a in range(na):
            base = n + 7 * a
            for j, (px, py) in enumerate(others):
                chip = 2 * px + py
                copy(slab(a, me, c), slab(a, chip, c), base + 1 + j, (px, py, c)).wait_recv()
                cp = copy(slab(a, chip, c), slab(a, chip, c), base + 4 + j, sibling)
                cp.start()
                fwd.append(cp)
        for a in range(na):
            base = n + 7 * a
            for j, (px, py) in enumerate(others):
                chip = 2 * px + py
                copy(slab(a, chip, c), slab(a, chip, 1 - c), base + 4 + j, sibling).wait_recv()
            copy(slab(a, me, c), slab(a, me, 1 - c), base, sibling).wait_recv()
        for w in range(n):
            copy(cols_of(w, c), cols_of(w, 1 - c), w, sibling).wait_recv()
        for cp in cps + fwd:
            cp.wait_send()

    nsem = n + 7 * na
    return pl.pallas_call(
        body, name="share", in_specs=[HBM_SPEC] * total, out_specs=[HBM_SPEC] * total,
        out_shape=[jax.ShapeDtypeStruct(a.shape, a.dtype) for a in (*shards, *alls)],
        input_output_aliases={i: i for i in range(total)},
        scratch_shapes=[pltpu.SemaphoreType.DMA((nsem,)), pltpu.SemaphoreType.DMA((nsem,))],
    )(*shards, *alls)


def _adamw(w, g, m, v):
    m2 = ADAM_B1 * m + (1.0 - ADAM_B1) * g
    v2 = ADAM_B2 * v + (1.0 - ADAM_B2) * (g * g)
    m_hat = m2 / (1.0 - ADAM_B1 ** ADAM_STEP)
    v_hat = v2 / (1.0 - ADAM_B2 ** ADAM_STEP)
    return -ADAM_LR * (m_hat / (jnp.sqrt(v_hat) + ADAM_EPS) + ADAM_WD * w), m2, v2


def _update_w_in(wt, gt, mt, vt, lat, owner, tile):
    nlat = lat.shape[0] // tile

    def body(owner_ref, w_ref, g_ref, m_ref, v_ref, lat_ref, g2_ref, d_ref, m2_ref, v2_ref):
        row = pl.program_id(0) * tile + lax.broadcasted_iota(jnp.int32, (tile, 1), 0)
        g = jnp.where((row < LAT_COLS) & (owner_ref[0] == 1), lat_ref[...].astype(F32), g_ref[...])
        g2_ref[...] = g
        d_ref[...], m2_ref[...], v2_ref[...] = _adamw(w_ref[...], g, m_ref[...], v_ref[...])

    spec = pl.BlockSpec((tile, wt.shape[1]), lambda i, o: (i, 0))
    return pl.pallas_call(
        body, name="update_w_in",
        grid_spec=pltpu.PrefetchScalarGridSpec(
            num_scalar_prefetch=1, grid=(wt.shape[0] // tile,),
            in_specs=[spec] * 4 + [pl.BlockSpec((tile, wt.shape[1]), lambda i, o: (jnp.minimum(i, nlat - 1), 0))],
            out_specs=[spec] * 4),
        out_shape=[jax.ShapeDtypeStruct(wt.shape, F32)] * 4,
        compiler_params=_params(("parallel",)),
    )(owner, wt, gt, mt, vt, lat)


def _update_small(ws, gs, ms, vs):
    n = len(ws)

    def body(*refs):
        for k in range(n):
            w_ref, g_ref, m_ref, v_ref = refs[k], refs[n + k], refs[2 * n + k], refs[3 * n + k]
            d, m2, v2 = _adamw(w_ref[...], g_ref[...], m_ref[...], v_ref[...])
            refs[4 * n + k][...] = d
            refs[5 * n + k][...] = m2
            refs[6 * n + k][...] = v2

    shapes = [jax.ShapeDtypeStruct(w.shape, F32) for w in ws]
    outs = pl.pallas_call(
        body, name="update_small", in_specs=[VMEM_SPEC] * (4 * n), out_specs=[VMEM_SPEC] * (3 * n),
        out_shape=shapes * 3,
        compiler_params=pltpu.CompilerParams(vmem_limit_bytes=VMEM_LIMIT),
    )(*ws, *gs, *ms, *vs)
    return outs[:n], outs[n:2 * n], outs[2 * n:]


SHARDED = ("w_in", "w_uq", "w_oa", "w_ob", "w_out")
REPLICATED = ("b_in", "g_q", "g_kv", "w_ukv", "sgu_ln_g", "sgu_ln_b", "w_s", "b_s", "ln_g", "ln_b")
ORDER = ("w_in", "b_in", "g_q", "w_uq", "g_kv", "w_ukv", "w_oa", "sgu_ln_g", "sgu_ln_b", "w_s", "b_s", "w_ob", "w_out",
         "ln_g", "ln_b")


def kernel(x, positions, w_in, b_in, g_q, w_uq, g_kv, w_ukv, w_oa, sgu_ln_g, sgu_ln_b, w_s, b_s, w_ob, w_out, ln_g, ln_b, loss_target, m_w_in, m_b_in, m_g_q, m_w_uq, m_g_kv, m_w_ukv, m_w_oa, m_sgu_ln_g, m_sgu_ln_b, m_w_s, m_b_s, m_w_ob, m_w_out, m_ln_g, m_ln_b, v_w_in, v_b_in, v_g_q, v_w_uq, v_g_kv, v_w_ukv, v_w_oa, v_sgu_ln_g, v_sgu_ln_b, v_w_s, v_b_s, v_w_ob, v_w_out, v_ln_g, v_ln_b):
    w = dict(w_in=w_in, b_in=b_in, g_q=g_q, w_uq=w_uq, g_kv=g_kv, w_ukv=w_ukv, w_oa=w_oa, sgu_ln_g=sgu_ln_g,
             sgu_ln_b=sgu_ln_b, w_s=w_s, b_s=b_s, w_ob=w_ob, w_out=w_out, ln_g=ln_g, ln_b=ln_b)
    m = dict(w_in=m_w_in, b_in=m_b_in, g_q=m_g_q, w_uq=m_w_uq, g_kv=m_g_kv, w_ukv=m_w_ukv, w_oa=m_w_oa,
             sgu_ln_g=m_sgu_ln_g, sgu_ln_b=m_sgu_ln_b, w_s=m_w_s, b_s=m_b_s, w_ob=m_w_ob, w_out=m_w_out, ln_g=m_ln_g,
             ln_b=m_ln_b)
    v = dict(w_in=v_w_in, b_in=v_b_in, g_q=v_g_q, w_uq=v_w_uq, g_kv=v_g_kv, w_ukv=v_w_ukv, w_oa=v_w_oa,
             sgu_ln_g=v_sgu_ln_g, sgu_ln_b=v_sgu_ln_b, w_s=v_w_s, b_s=v_b_s, w_ob=v_w_ob, w_out=v_w_out, ln_g=v_ln_g,
             ln_b=v_ln_b)
    w, m, v = ({n: a[0] for n, a in d.items()} for d in (w, m, v))
    c = lax.axis_index("c")

    wt_shard, mt_shard, vt_shard = (jnp.transpose(d["w_in"]) for d in (w, m, v))
    xi, yi = lax.axis_index("x"), lax.axis_index("y")
    me1 = (2 * xi + yi).reshape(1).astype(jnp.int32)
    first = _first_start(*_cast_first(wt_shard[:LAT_COLS], w["w_uq"].reshape(Q_RANK // 4, HEADS * QK_DIM), me1))
    bufs = _cast_own([wt_shard, w["w_oa"], w["w_ob"], w["w_out"]], me1, first[4])
    send0, recv0, bufs, token0 = _gather_start(bufs, first[4])
    g_lat, g_uq = _first_forward(*_first_wait(*first[:4], token0))
    st = _local_attention(x[0], positions[0], g_lat, w["b_in"], w["g_q"], g_uq.reshape(Q_RANK, HEADS, QK_DIM),
                          w["g_kv"], w["w_ukv"], token0)
    g_in, g_oa, g_ob, g_out = _gather_finish(_gather_wait(send0, recv0, bufs, st["o"]))
    wt = g_in.reshape(IN_W, D_MODEL)

    loss, early, st = _local_head(
        st, x[0], loss_target[0], wt, g_oa, w["sgu_ln_g"], w["sgu_ln_b"], w["w_s"], w["b_s"], g_ob,
        g_out.reshape(D_MODEL, D_MODEL), w["ln_g"], w["ln_b"])

    c1 = c.reshape(1).astype(jnp.int32)
    idx = jnp.stack([2 * xi + yi, 2 * (1 - xi) + yi, 2 * xi + (1 - yi), 2 * (1 - xi) + (1 - yi), c]).astype(jnp.int32)
    slabs = lambda a: a.reshape(N_CHIPS, IN_W // N_CHIPS, D_MODEL // 2)
    theirs = slabs(_dwt_early(st["dhmt"], st["dhgt"], st["xb2"], 1 - c1, c1, "dwt_theirs"))
    parts1 = [theirs, early["w_oa"].astype(BF16), early["w_ob"].astype(BF16),
              early["w_out"].reshape(N_CHIPS, SLAB_W, D_MODEL).astype(BF16)]
    my_loss = lax.dynamic_update_slice(jnp.zeros((N_DEV,) + LOSS_TILE, F32), jnp.broadcast_to(loss, (1,) + LOSS_TILE),
                                       (4 * xi + 2 * yi + c, 0, 0))
    sems0 = _pairs_start(parts1, my_loss, 1)
    mine = slabs(_dwt_early(st["dhmt"], st["dhgt"], st["xb2"], c1, sems0[5], "dwt_mine"))
    parts1, recv1, all_loss = _pairs_wait(*sems0[:5], mine, 1)
    pairs1 = [_add_pair_tiled(mine, recv1[0]), *_add_pair_small(parts1[1:], recv1[1:], c1, "add_pair_early")]
    sems1 = _chips_start(pairs1, "chips_start_early")
    st["delta"] = st["delta"] + sems1[4][0, 0]
    dq, dk, dv = _local_attn_bwd(st)
    dhl, late = _local_tail(st, dq, dk, dv, dv)

    grads = {**early, **late}
    rep = jnp.concatenate([_rows8(grads[n]) for n in REPLICATED], axis=0)
    rep = jnp.pad(rep, ((0, N_CHIPS * REP_ROWS - rep.shape[0]), (0, 0))).reshape(N_CHIPS, REP_ROWS, D_MODEL)
    parts2 = [late["w_uq"].reshape(N_CHIPS, Q_RANK // N_CHIPS, HEADS * QK_DIM).astype(BF16), rep.astype(BF16),
              late["w_lat"].reshape(N_CHIPS, LAT_ROWS_PAD // N_CHIPS, D_MODEL)]
    pairs2 = _add_pair_small(parts2, _exchange_pairs(parts2, "exchange_pairs_late"), c1, "add_pair_late")
    sems2 = _chips_start(pairs2, "chips_start_late")
    dx = _dx(st["dr"], st["dhg"], st["dhm"], dhl, st["wt"], sems2[4])
    pairs2, landed2 = _chips_wait(*sems2[:4], dx, "chips_wait_late")
    pairs1, landed1 = _chips_wait(*sems1[:4], landed2[0], "chips_wait_early")
    sums = [_sum_chips_tiled(pairs1[0], landed1[0], idx, 128),
            *_sum_chips_small([*pairs1[1:], *pairs2], [*landed1[1:], *landed2], idx, (F32, BF16))]
    *shards, g_rep, g_lat = _share(sums[:-2], sums[-2:])
    loss = jnp.sum(all_loss[:, 0, 0])

    red = {n: s.reshape(w[n].shape) for n, s in zip(("w_oa", "w_ob", "w_out", "w_uq"), shards[1:])}
    g_rep = g_rep.reshape(N_CHIPS * REP_ROWS, D_MODEL)
    off = 0
    for n in REPLICATED:
        rows = _rows8(w[n]).shape[0]
        red[n] = g_rep[off:off + rows].reshape(-1)[:w[n].size].reshape(w[n].shape)
        off += rows
    owner = (2 * xi + yi == 0).astype(jnp.int32).reshape(1)
    gt, dt, mt, vt2 = _update_w_in(wt_shard, shards[0], mt_shard, vt_shard,
                                   g_lat.reshape(LAT_ROWS_PAD, D_MODEL).astype(F32), owner, 232)
    red["w_in"] = jnp.transpose(gt)
    small = [n for n in ORDER if n != "w_in"]
    as2d = lambda a: a.reshape(-1, a.shape[-1])
    ds, ms, vs = _update_small([as2d(w[n]) for n in small], [as2d(red[n]) for n in small],
                               [as2d(m[n]) for n in small], [as2d(v[n]) for n in small])
    delta, new_m, new_v = {"w_in": jnp.transpose(dt)}, {"w_in": jnp.transpose(mt)}, {"w_in": jnp.transpose(vt2)}
    for i, n in enumerate(small):
        delta[n], new_m[n], new_v[n] = (a[i].reshape(w[n].shape) for a in (ds, ms, vs))

    lead = lambda a: a[None]
    return (loss, dx[None], *[lead(red[n]) for n in ORDER], *[lead(delta[n]) for n in ORDER],
            *[lead(new_m[n]) for n in ORDER], *[lead(new_v[n]) for n in ORDER])
```

```python
import functools
import math

import jax
import jax.numpy as jnp
from jax import lax
from jax.experimental import pallas as pl
from jax.experimental.pallas import tpu as pltpu

F32 = jnp.float32
BF16 = jnp.bfloat16

D_MODEL = 1024
HEADS = 8
Q_RANK = 384
KV_RANK = 128
NOPE = 64
ROPE = 32
V_DIM = 64
QK_DIM = NOPE + ROPE
HEAD_PAD = 128
MLA_W = HEADS * V_DIM
SGU_W = 512
GROUPS = 8
CHUNK = 128
IN_W = 4640
RMS_EPS = 1e-6
LN_EPS = 1e-5
ALPHA = 2.0 ** 0.25
ROPE_THETA = 10000.0
SCALE = QK_DIM ** -0.5

GATE_W = 2 * D_MODEL
MID_W = 4 * SGU_W
LAT_W = Q_RANK + KV_RANK + HEAD_PAD
PAD_W = GATE_W + MID_W + LAT_W
LAT_COLS = Q_RANK + KV_RANK + ROPE
ROW_GATE = LAT_COLS + MID_W
LAT_ROWS_PAD = 704
N_SLABS = 4
SLAB_W = D_MODEL // N_SLABS

ROW_TILE = 256
MATMUL_ROW_TILE = 512
MID_ROW_TILE = 256
ATT_TQ = 2048
ATT_TK = 256
ATT_BWD_TQ = 512
ATT_BWD_TK = 256
LOG2E = 1.4426950408889634
LN2 = 0.6931471805599453
Q_SCALE = SCALE * LOG2E
VMEM_LIMIT = 56 * 1024 * 1024

ADAM_LR = 0.001
ADAM_B1 = 0.9
ADAM_B2 = 0.999
ADAM_EPS = 1e-08
ADAM_WD = 0.01
ADAM_STEP = 10


def _dot(a, b):
    return jnp.dot(a, b, preferred_element_type=F32)


def _dot_nt(a, b):
    return lax.dot_general(a, b, (((1,), (1,)), ((), ())), preferred_element_type=F32)


def _dot_tn(a, b):
    return lax.dot_general(a, b, (((0,), (0,)), ((), ())), preferred_element_type=F32)


def _sigmoid(z):
    return 0.5 * jnp.tanh(0.5 * z) + 0.5


_GELU_C = math.sqrt(2.0 / math.pi)


def _gelu_and_grad(x):
    x2 = x * x
    t = jnp.tanh(_GELU_C * (x + 0.044715 * x * x2))
    g = 0.5 * x * (1.0 + t)
    dg = 0.5 * (1.0 + t) + 0.5 * x * (1.0 - t * t) * (_GELU_C * (1.0 + 3.0 * 0.044715 * x2))
    return g, dg


def _silu_and_grad(z):
    s = _sigmoid(z)
    return z * s, s * (1.0 + z * (1.0 - s))


def _rope(xb, c, sl, sh):
    return xb * c + pltpu.roll(xb, 112, 1) * sl + pltpu.roll(xb, 16, 1) * sh


def _rope_t(dy, c, sl, sh):
    return dy * c + pltpu.roll(dy * sl, 16, 1) + pltpu.roll(dy * sh, 112, 1)


def _params(sem=("arbitrary",)):
    return pltpu.CompilerParams(dimension_semantics=sem, vmem_limit_bytes=VMEM_LIMIT)


def _row_spec(tile, width):
    return pl.BlockSpec((tile, width), lambda i: (i, 0))


def _full_spec(shape):
    nd = len(shape)
    return pl.BlockSpec(shape, lambda i: (0,) * nd)


def _kpe_rows(wt_ref):
    z = lambda n: jnp.zeros((n, D_MODEL), BF16)
    return jnp.concatenate([z(NOPE), wt_ref[Q_RANK + KV_RANK:LAT_COLS, :], z(HEAD_PAD - QK_DIM)], axis=0)


def _fwd_rest(xb, wt, b_g, b_m):
    s = xb.shape[0]
    ts = MATMUL_ROW_TILE

    def body(xb_ref, wt_ref, bg_ref, bm_ref, hg_ref, hm_ref):
        xb_ = xb_ref[...]
        hg_ref[...] = _dot_nt(xb_, wt_ref[ROW_GATE:IN_W, :]) + bg_ref[...]
        hm_ref[...] = _dot_nt(xb_, wt_ref[LAT_COLS:ROW_GATE, :]) + bm_ref[...]

    return pl.pallas_call(
        body, name="fwd_rest", grid=(s // ts,),
        in_specs=[_row_spec(ts, D_MODEL), _full_spec(wt.shape), _full_spec(b_g.shape), _full_spec(b_m.shape)],
        out_specs=[_row_spec(ts, GATE_W), _row_spec(ts, MID_W)],
        out_shape=[jax.ShapeDtypeStruct((s, GATE_W), F32), jax.ShapeDtypeStruct((s, MID_W), F32)],
        compiler_params=_params(),
    )(xb, wt, b_g, b_m)


def _fwd_lat(x, wlat, b_l, g_q, wuq, g_kv, wk, wv, rc, rsl, rsh, after):
    s = x.shape[0]
    ts = ROW_TILE

    def body(x_ref, wt_ref, bl_ref, gq_ref, wuq_ref, gkv_ref, wk_ref, wv_ref, rc_ref, rsl_ref,
             rsh_ref, after_ref, hl_ref, q_ref, k_ref, v_ref, xb_ref, qt_ref, kt_ref, vt_ref, xb2_ref):
        xb = x_ref[...].astype(BF16)
        xb_ref[...] = xb
        xb2_ref[0] = xb[:, :D_MODEL // 2]
        xb2_ref[1] = xb[:, D_MODEL // 2:]
        hl = jnp.concatenate([_dot_nt(xb, wt_ref[0:Q_RANK + KV_RANK, :]), _dot_nt(xb, _kpe_rows(wt_ref))],
                             axis=1) + bl_ref[...]
        hl_ref[...] = hl
        c, sl, sh = rc_ref[...], rsl_ref[...], rsh_ref[...]
        cq = hl[:, :Q_RANK]
        cqn = cq * lax.rsqrt(jnp.mean(cq * cq, axis=-1, keepdims=True) + RMS_EPS) * gq_ref[...]
        q = _dot(cqn.astype(BF16), wuq_ref[...])
        ckv = hl[:, Q_RANK:Q_RANK + KV_RANK]
        ckvn = (ckv * lax.rsqrt(jnp.mean(ckv * ckv, axis=-1, keepdims=True) + RMS_EPS) * gkv_ref[...]).astype(BF16)
        k = _dot(ckvn, wk_ref[...])
        vb = _dot(ckvn, wv_ref[...]).astype(BF16)
        v_ref[...] = vb
        vt_ref[...] = vb.T
        kpe = _rope(hl[:, Q_RANK + KV_RANK:], c, sl, sh)
        for hd in range(HEADS):
            lanes = slice(hd * HEAD_PAD, (hd + 1) * HEAD_PAD)
            qb = (_rope(q[:, lanes], c, sl, sh) * Q_SCALE).astype(BF16)
            kb = (k[:, lanes] + kpe).astype(BF16)
            q_ref[:, lanes] = qb
            k_ref[:, lanes] = kb
            qt_ref[lanes, :] = qb.T
            kt_ref[lanes, :] = kb.T

    qk_w = HEADS * HEAD_PAD
    col_spec = lambda rows: pl.BlockSpec((rows, ts), lambda i: (0, i))
    return pl.pallas_call(
        body, name="fwd_lat", grid=(s // ts,),
        in_specs=[_row_spec(ts, D_MODEL), _full_spec(wlat.shape),
                  _full_spec(b_l.shape), _full_spec(g_q.shape),
                  _full_spec(wuq.shape), _full_spec(g_kv.shape), _full_spec(wk.shape), _full_spec(wv.shape),
                  _row_spec(ts, HEAD_PAD), _row_spec(ts, HEAD_PAD), _row_spec(ts, HEAD_PAD),
                  pl.BlockSpec(memory_space=pl.ANY)],
        out_specs=[_row_spec(ts, LAT_W), _row_spec(ts, qk_w),
                   _row_spec(ts, qk_w), _row_spec(ts, MLA_W), _row_spec(ts, D_MODEL), col_spec(qk_w), col_spec(qk_w),
                   col_spec(MLA_W), pl.BlockSpec((2, ts, D_MODEL // 2), lambda i: (0, i, 0))],
        out_shape=[jax.ShapeDtypeStruct((s, LAT_W), F32), jax.ShapeDtypeStruct((s, qk_w), BF16),
                   jax.ShapeDtypeStruct((s, qk_w), BF16), jax.ShapeDtypeStruct((s, MLA_W), BF16),
                   jax.ShapeDtypeStruct((s, D_MODEL), BF16), jax.ShapeDtypeStruct((qk_w, s), BF16),
                   jax.ShapeDtypeStruct((qk_w, s), BF16), jax.ShapeDtypeStruct((MLA_W, s), BF16),
                   jax.ShapeDtypeStruct((2, s, D_MODEL // 2), BF16)],
        compiler_params=_params(),
    )(x, wlat, b_l, g_q, wuq, g_kv, wk, wv, rc, rsl, rsh, after)


def _attn_fwd(qt, k, vt):
    s = k.shape[0]
    tq, tk = ATT_TQ, ATT_TK
    r = tq // tk
    pairs = HEADS // 2

    def body(qt_ref, k_ref, vt_ref, o_ref, lse_ref):
        i = pl.program_id(1)
        qts = [qt_ref[hh * HEAD_PAD:(hh + 1) * HEAD_PAD, :] for hh in range(2)]

        def scores(j, lo):
            koff = pl.multiple_of(j * tk, tk)
            return tuple(_dot(k_ref[pl.ds(koff, tk), hh * HEAD_PAD:(hh + 1) * HEAD_PAD], qts[hh][:, lo:])
                         for hh in range(2))

        def weighted(j, ps):
            koff = pl.multiple_of(j * tk, tk)
            return tuple(_dot(vt_ref[hh * V_DIM:(hh + 1) * V_DIM, pl.ds(koff, tk)], ps[hh]) for hh in range(2))

        def from_lane(full, lo, part):
            return part if lo == 0 else jnp.concatenate([full[:, :lo], part], axis=1)

        def step(j, carry, diag, last):
            st, ps, stats = carry
            lo = 0 if diag is None else diag * tk
            lo_prev = 0 if not diag else (diag - 1) * tk
            st_next = None if last else scores(j + 1, 0 if diag is None else lo + tk)
            pvs = weighted(jnp.maximum(j - 1, 0), ps)
            new_ps, new_stats = [], []
            for hh in range(2):
                m, l, acc = stats[hh]
                s_ = st[hh]
                if diag is not None:
                    krow = lax.broadcasted_iota(jnp.int32, s_.shape, 0)
                    qcol = lax.broadcasted_iota(jnp.int32, s_.shape, 1)
                    s_ = jnp.where(krow <= qcol, s_, -jnp.inf)
                acc = from_lane(acc, lo_prev, acc[:, lo_prev:] + pvs[hh])
                m_old = m[:, lo:]
                m_new = jnp.maximum(m_old, jnp.max(s_, axis=0, keepdims=True))
                a = jnp.exp2(m_old - m_new)
                p = jnp.exp2(s_ - m_new)
                new_stats.append((from_lane(m, lo, m_new),
                                  from_lane(l, lo, a * l[:, lo:] + jnp.sum(p, axis=0, keepdims=True)),
                                  from_lane(acc, lo, a * acc[:, lo:])))
                new_ps.append(p.astype(BF16))
            return st_next, tuple(new_ps), tuple(new_stats)

        one = (jnp.full((1, tq), -jnp.inf, F32), jnp.zeros((1, tq), F32), jnp.zeros((V_DIM, tq), F32))
        zero_p = jnp.zeros((tk, tq), BF16)
        nfull = i * r
        carry = lax.fori_loop(0, nfull, functools.partial(step, diag=None, last=False),
                              (scores(0, 0), (zero_p, zero_p), (one, one)))
        for d in range(r):
            carry = step(nfull + d, carry, d, d == r - 1)
        _, ps, stats = carry
        pvs = weighted(nfull + r - 1, ps)
        lo = (r - 1) * tk
        ot = jnp.concatenate([from_lane(stats[hh][2], lo, stats[hh][2][:, lo:] + pvs[hh]) / stats[hh][1]
                              for hh in range(2)], axis=0)
        o_ref[...] = ot.T
        lse = [stats[hh][0] + jnp.log(stats[hh][1]) * LOG2E for hh in range(2)]
        lse_ref[...] = jnp.concatenate(lse + [jnp.zeros((6, tq), F32)], axis=0)

    return pl.pallas_call(
        body, name="attn_fwd", grid=(pairs, s // tq),
        in_specs=[pl.BlockSpec((2 * HEAD_PAD, tq), lambda p, i: (p, i)),
                  pl.BlockSpec((s, 2 * HEAD_PAD), lambda p, i: (0, p)),
                  pl.BlockSpec((2 * V_DIM, s), lambda p, i: (p, 0))],
        out_specs=[pl.BlockSpec((tq, 2 * V_DIM), lambda p, i: (i, p)),
                   pl.BlockSpec((None, 8, tq), lambda p, i: (p, 0, i))],
        out_shape=[jax.ShapeDtypeStruct((s, MLA_W), F32), jax.ShapeDtypeStruct((pairs, 8, s), F32)],
        compiler_params=_params(("arbitrary", "arbitrary")),
    )(qt, k, vt)


def _attn_bwd(q, qt, k, kt, v, do, dot, lse, delta):
    s = k.shape[0]
    tk = ATT_BWD_TK
    nk = s // tk
    pairs = HEADS // 2

    def body(q_ref, qt_ref, k_ref, kt_ref, v_ref, do_ref, dot_ref, lse_ref, dl_ref, dqt_ref, dk_ref, dv_ref):
        krow = lax.broadcasted_iota(jnp.int32, (tk, tk), 0)
        qcol = lax.broadcasted_iota(jnp.int32, (tk, tk), 1)
        lane = lax.broadcasted_iota(jnp.int32, (tk, 2 * V_DIM), 1)
        drow = lax.broadcasted_iota(jnp.int32, (2 * V_DIM, s), 0)
        dotb = dot_ref[...]
        dots = [jnp.where((drow < V_DIM) if hh == 0 else (drow >= V_DIM), dotb, jnp.zeros_like(dotb))
                for hh in range(2)]
        for j in range(nk):
            lo = j * tk
            vb = v_ref[lo:lo + tk, :]
            dob = do_ref[lo:, :]
            dvs = []
            for hh in range(2):
                rows = slice(hh * HEAD_PAD, (hh + 1) * HEAD_PAD)
                st = _dot(k_ref[lo:lo + tk, rows], qt_ref[rows, lo:])
                diag = jnp.where(krow <= qcol, st[:, :tk], -jnp.inf)
                st = diag if j == nk - 1 else jnp.concatenate([diag, st[:, tk:]], axis=1)
                p = jnp.exp2(st - lse_ref[hh:hh + 1, lo:])
                dpt = _dot(vb, dots[hh][:, lo:])
                dst = (p * (dpt - dl_ref[hh:hh + 1, lo:])).astype(BF16)
                dvs.append(_dot(p.astype(BF16), dob))
                dk_ref[lo:lo + tk, rows] = _dot(dst, q_ref[lo:, rows]) * LN2
                dqt = _dot(kt_ref[rows, lo:lo + tk], dst)
                if j == 0:
                    dqt_ref[rows, :] = dqt
                else:
                    dqt_ref[rows, lo:] += dqt
            dv_ref[lo:lo + tk, :] = jnp.where(lane < V_DIM, dvs[0], dvs[1])
        dqt_ref[...] = dqt_ref[...] * SCALE

    pair_rows = lambda w: pl.BlockSpec((s, w), lambda p: (0, p))
    pair_cols = lambda w: pl.BlockSpec((w, s), lambda p: (p, 0))
    stats = pl.BlockSpec((None, 8, s), lambda p: (p, 0, 0))
    return pl.pallas_call(
        body, name="attn_bwd", grid=(pairs,),
        in_specs=[pair_rows(2 * HEAD_PAD), pair_cols(2 * HEAD_PAD), pair_rows(2 * HEAD_PAD), pair_cols(2 * HEAD_PAD),
                  pair_rows(2 * V_DIM), pair_rows(2 * V_DIM), pair_cols(2 * V_DIM), stats, stats],
        out_specs=[pair_cols(2 * HEAD_PAD), pair_rows(2 * HEAD_PAD), pair_rows(2 * V_DIM)],
        out_shape=[jax.ShapeDtypeStruct((HEADS * HEAD_PAD, s), F32), jax.ShapeDtypeStruct((s, HEADS * HEAD_PAD), F32),
                   jax.ShapeDtypeStruct((s, MLA_W), F32)],
        compiler_params=_params(("arbitrary",)),
    )(q, qt, k, kt, v, do, dot, lse, delta)


def _attn_bwd_dynamic(q, qt, k, kt, v, do, dot, lse, delta):
    s = k.shape[0]
    tq, tk = ATT_BWD_TQ, ATT_BWD_TK
    r = tq // tk
    nq = s // tq
    nk = s // tk
    pairs = HEADS // 2

    def body(q_ref, qt_ref, k_ref, kt_ref, v_ref, do_ref, dot_ref, lse_ref, dl_ref, dqt_ref, dk_ref, dv_ref):
        j = pl.program_id(1)
        krow = lax.broadcasted_iota(jnp.int32, (tk, tq), 0)
        qcol = lax.broadcasted_iota(jnp.int32, (tk, tq), 1)
        lane = lax.broadcasted_iota(jnp.int32, (tk, 2 * V_DIM), 1)
        drow = lax.broadcasted_iota(jnp.int32, (2 * V_DIM, tq), 0)

        @pl.when(j == 0)
        def _():
            dqt_ref[...] = jnp.zeros_like(dqt_ref)

        koff = pl.multiple_of(j * tk, tk)
        vb = v_ref[pl.ds(koff, tk), :]
        kbs = [k_ref[pl.ds(koff, tk), hh * HEAD_PAD:(hh + 1) * HEAD_PAD] for hh in range(2)]
        ktbs = [kt_ref[hh * HEAD_PAD:(hh + 1) * HEAD_PAD, pl.ds(koff, tk)] for hh in range(2)]
        i0 = j // r

        def front(i):
            qoff = pl.multiple_of(i * tq, tq)
            dotb = dot_ref[:, pl.ds(qoff, tq)]
            out = []
            for hh in range(2):
                mine = (drow < V_DIM) if hh == 0 else (drow >= V_DIM)
                st = _dot(kbs[hh], qt_ref[hh * HEAD_PAD:(hh + 1) * HEAD_PAD, pl.ds(qoff, tq)])
                out.append((st, _dot(vb, jnp.where(mine, dotb, jnp.zeros_like(dotb)))))
            return tuple(out)

        def middle(i, tiles, diag):
            qoff = pl.multiple_of(i * tq, tq)
            out = []
            for hh in range(2):
                st, dpt = tiles[hh]
                if diag:
                    st = jnp.where(krow + (j - i0 * r) * tk <= qcol, st, -jnp.inf)
                p = jnp.exp2(st - lse_ref[hh:hh + 1, pl.ds(qoff, tq)])
                out.append((p.astype(BF16), (p * (dpt - dl_ref[hh:hh + 1, pl.ds(qoff, tq)])).astype(BF16)))
            return tuple(out)

        def back(i, pd, accs):
            qoff = pl.multiple_of(i * tq, tq)
            dob = do_ref[pl.ds(qoff, tq), :]
            out = []
            for hh in range(2):
                rows = slice(hh * HEAD_PAD, (hh + 1) * HEAD_PAD)
                p, dst = pd[hh]
                dk_acc, dv_acc = accs[hh]
                dv_acc = dv_acc + _dot(p, dob)
                dk_acc = dk_acc + _dot(dst, q_ref[pl.ds(qoff, tq), rows])
                dqt_ref[rows, pl.ds(qoff, tq)] += _dot(ktbs[hh], dst)
                out.append((dk_acc, dv_acc))
            return tuple(out)

        def step(i, accs, diag):
            return back(i, middle(i, front(i), diag), accs)

        zero_acc = (jnp.zeros((tk, HEAD_PAD), F32), jnp.zeros((tk, 2 * V_DIM), F32))
        accs = step(i0, (zero_acc, zero_acc), True)
        accs = lax.fori_loop(i0 + 1, nq, functools.partial(step, diag=False), accs)
        for hh in range(2):
            dk_ref[:, hh * HEAD_PAD:(hh + 1) * HEAD_PAD] = accs[hh][0] * LN2
        dv_ref[...] = jnp.where(lane < V_DIM, accs[0][1], accs[1][1])

        @pl.when(j == nk - 1)
        def _():
            dqt_ref[...] = dqt_ref[...] * SCALE

    pair_rows = lambda w: pl.BlockSpec((s, w), lambda p, j: (0, p))
    pair_cols = lambda w: pl.BlockSpec((w, s), lambda p, j: (p, 0))
    stats = pl.BlockSpec((None, 8, s), lambda p, j: (p, 0, 0))
    return pl.pallas_call(
        body, name="attn_bwd", grid=(pairs, nk),
        in_specs=[pair_rows(2 * HEAD_PAD), pair_cols(2 * HEAD_PAD), pair_rows(2 * HEAD_PAD), pair_cols(2 * HEAD_PAD),
                  pair_rows(2 * V_DIM), pair_rows(2 * V_DIM), pair_cols(2 * V_DIM), stats, stats],
        out_specs=[pair_cols(2 * HEAD_PAD),
                   pl.BlockSpec((tk, 2 * HEAD_PAD), lambda p, j: (j, p)),
                   pl.BlockSpec((tk, 2 * V_DIM), lambda p, j: (j, p))],
        out_shape=[jax.ShapeDtypeStruct((HEADS * HEAD_PAD, s), F32), jax.ShapeDtypeStruct((s, HEADS * HEAD_PAD), F32),
                   jax.ShapeDtypeStruct((s, MLA_W), F32)],
        compiler_params=_params(("arbitrary", "arbitrary")),
    )(q, qt, k, kt, v, do, dot, lse, delta)


def _split3(a):
    hi = a.astype(BF16)
    r1 = a - hi.astype(F32)
    mid = r1.astype(BF16)
    lo = (r1 - mid.astype(F32)).astype(BF16)
    return hi, mid, lo


def _mid(x, tgt, o, hm, hg, woa, wob, wout, ln_g, ln_b, sg_g, sg_b, w_s, bsb):
    s = x.shape[0]
    ts = MID_ROW_TILE
    nsteps = s // ts
    nch = ts // CHUNK
    npair = GROUPS // 2

    def body(x_ref, t_ref, o_ref, hm_ref, hg_ref, woa_ref, wob_ref, wout_ref, lng_ref, lnb_ref, sgg_ref, sgb_ref,
             ws_ref, bsb_ref,
             dr_ref, dhg_ref, dhm_ref, do_ref, dot_ref, dl_ref, dhgt_ref, dhmt_ref,
             dwout_ref, dwoa_ref, dwob_ref, dws_ref, dbs_ref, dlng_ref, dlnb_ref, dsgg_ref, dsgb_ref, loss_ref,
             dbg_ref, dbm_ref, dbacc_ref, awout_ref, awoa_ref, awob_ref):
        i = pl.program_id(0)

        @pl.when(i == 0)
        def _():
            for r in (awout_ref, awoa_ref, awob_ref, dws_ref, dlng_ref, dlnb_ref, dsgg_ref, dsgb_ref, loss_ref,
                      dbg_ref, dbm_ref, dbacc_ref):
                r[...] = jnp.zeros_like(r)

        def emit(ref, tref, bref, lo, val):
            vb = val.astype(BF16)
            n = val.shape[1]
            ref[:, lo:lo + n] = vb
            tref[lo:lo + n, :] = vb.T
            bref[:, lo:lo + n] += jnp.sum(val, axis=0, keepdims=True)

        lane = lax.broadcasted_iota(jnp.int32, (CHUNK, CHUNK), 1)
        left = lane < V_DIM
        tril = lax.broadcasted_iota(jnp.int32, (CHUNK, CHUNK), 0) >= lane
        ms = [jnp.where(tril, ws_ref[g], 0.0).astype(BF16) for g in range(GROUPS)]

        z_a = hm_ref[:, 0:SGU_W]
        u = hm_ref[:, SGU_W:2 * SGU_W]
        v = hm_ref[:, 2 * SGU_W:3 * SGU_W]
        z_b = hm_ref[:, 3 * SGU_W:4 * SGU_W]
        o = o_ref[...]
        sa, dsa = _silu_and_grad(z_a)
        y_a = (o * sa).astype(BF16)
        gu, dgu = _gelu_and_grad(u)
        gv, dgv = _gelu_and_grad(v)
        mu = jnp.mean(gv, axis=-1, keepdims=True)
        vc = gv - mu
        rstd_v = lax.rsqrt(jnp.mean(vc * vc, axis=-1, keepdims=True) + LN_EPS)
        vhat = vc * rstd_v
        vn = (vhat * sgg_ref[...] + sgb_ref[...]).astype(BF16)
        rows = []
        for c in range(nch):
            blocks = []
            for p in range(npair):
                blk = vn[c * CHUNK:(c + 1) * CHUNK, p * CHUNK:(p + 1) * CHUNK]
                blocks.append(jnp.where(left, _dot(ms[2 * p], blk), _dot(ms[2 * p + 1], blk)))
            rows.append(jnp.concatenate(blocks, axis=1) + bsb_ref[...])
        mixed = jnp.concatenate(rows, axis=0)
        sgu = gu * mixed
        sb, dsb = _silu_and_grad(z_b)
        y_b = (sgu * sb).astype(BF16)
        pa = jnp.concatenate([_dot(y_a, woa_ref[k]) for k in range(N_SLABS)], axis=1)
        pb = jnp.concatenate([_dot(y_b, wob_ref[k]) for k in range(N_SLABS)], axis=1)
        sga = _sigmoid(hg_ref[:, :D_MODEL])
        sgb = _sigmoid(hg_ref[:, D_MODEL:])
        m2 = (sga * pa + sgb * pb).astype(BF16)
        r = ALPHA * x_ref[...] + _dot(m2, wout_ref[...])
        rmu = jnp.mean(r, axis=-1, keepdims=True)
        rc = r - rmu
        rstd = lax.rsqrt(jnp.mean(rc * rc, axis=-1, keepdims=True) + LN_EPS)
        xhat = rc * rstd
        y = xhat * lng_ref[...] + lnb_ref[...]
        err = y - t_ref[...]
        loss_ref[...] += jnp.full(loss_ref.shape, 0.5 / D_MODEL, F32) * jnp.sum(err * err)

        dy = err * (1.0 / D_MODEL)
        dlng_ref[...] += jnp.sum(dy * xhat, axis=0, keepdims=True)
        dlnb_ref[...] += jnp.sum(dy, axis=0, keepdims=True)
        dxh = dy * lng_ref[...]
        dr = rstd * (dxh - jnp.mean(dxh, axis=-1, keepdims=True) - xhat * jnp.mean(dxh * xhat, axis=-1, keepdims=True))
        dr_ref[...] = dr
        drb = dr.astype(BF16)
        awout_ref[...] += _dot_tn(m2, drb)
        dm2 = _dot_nt(drb, wout_ref[...])
        emit(dhg_ref, dhgt_ref, dbg_ref, 0, dm2 * pa * sga * (1.0 - sga))
        emit(dhg_ref, dhgt_ref, dbg_ref, D_MODEL, dm2 * pb * sgb * (1.0 - sgb))
        dpa = (dm2 * sga).astype(BF16)
        dpb = (dm2 * sgb).astype(BF16)
        dy_a = jnp.zeros((ts, MLA_W), F32)
        dy_b = jnp.zeros((ts, SGU_W), F32)
        y_at, y_bt = y_a.T, y_b.T
        for k in range(N_SLABS):
            cols = slice(k * SLAB_W, (k + 1) * SLAB_W)
            awoa_ref[k] += _dot(y_at, dpa[:, cols])
            awob_ref[k] += _dot(y_bt, dpb[:, cols])
            dy_a = dy_a + _dot_nt(dpa[:, cols], woa_ref[k])
            dy_b = dy_b + _dot_nt(dpb[:, cols], wob_ref[k])
        dob = (dy_a * sa).astype(BF16)
        do_ref[...] = dob
        dot_ref[...] = dob.T
        head = (lax.broadcasted_iota(jnp.int32, (HEADS, MLA_W), 1) // V_DIM
                == lax.broadcasted_iota(jnp.int32, (HEADS, MLA_W), 0)).astype(BF16)
        dl_ref[...] = sum(_dot_nt(head, term) for term in _split3(dob.astype(F32) * o))
        emit(dhm_ref, dhmt_ref, dbm_ref, 0, dy_a * o * dsa)
        dsg = dy_b * sb
        emit(dhm_ref, dhmt_ref, dbm_ref, 3 * SGU_W, dy_b * sgu * dsb)
        emit(dhm_ref, dhmt_ref, dbm_ref, SGU_W, dsg * mixed * dgu)
        dmixed = dsg * gu
        dvn_rows = []
        dbs_sum = jnp.zeros((CHUNK, SGU_W), F32)
        for c in range(nch):
            dm_c = dmixed[c * CHUNK:(c + 1) * CHUNK, :]
            dbs_sum = dbs_sum + dm_c
            blocks = []
            for p in range(npair):
                dmb = dm_c[:, p * CHUNK:(p + 1) * CHUNK].astype(BF16)
                blk = vn[c * CHUNK:(c + 1) * CHUNK, p * CHUNK:(p + 1) * CHUNK]
                blocks.append(jnp.where(left, _dot_tn(ms[2 * p], dmb), _dot_tn(ms[2 * p + 1], dmb)))
                zero = jnp.zeros_like(dmb)
                dws_ref[2 * p] += jnp.where(tril, _dot_nt(jnp.where(left, dmb, zero), blk), 0.0)
                dws_ref[2 * p + 1] += jnp.where(tril, _dot_nt(jnp.where(left, zero, dmb), blk), 0.0)
            dvn_rows.append(jnp.concatenate(blocks, axis=1))
        dbacc_ref[...] += dbs_sum
        dvn = jnp.concatenate(dvn_rows, axis=0)
        dsgg_ref[...] += jnp.sum(dvn * vhat, axis=0, keepdims=True)
        dsgb_ref[...] += jnp.sum(dvn, axis=0, keepdims=True)
        dvh = dvn * sgg_ref[...]
        dgv_in = rstd_v * (dvh - jnp.mean(dvh, axis=-1, keepdims=True)
                           - vhat * jnp.mean(dvh * vhat, axis=-1, keepdims=True))
        emit(dhm_ref, dhmt_ref, dbm_ref, 2 * SGU_W, dgv_in * dgv)

        @pl.when(i == nsteps - 1)
        def _():
            dwout_ref[...] = awout_ref[...].astype(BF16)
            dwoa_ref[...] = awoa_ref[...].astype(BF16)
            dwob_ref[...] = awob_ref[...].astype(BF16)
            grp = (lax.broadcasted_iota(jnp.int32, (SGU_W, CHUNK), 0) // V_DIM
                   == lax.broadcasted_iota(jnp.int32, (SGU_W, CHUNK), 1)).astype(BF16)
            hi, mid, lo = _split3(dbacc_ref[...])
            dbs_ref[...] = _dot(hi, grp) + _dot(mid, grp) + _dot(lo, grp)

    acc_shapes = [(D_MODEL, D_MODEL), woa.shape, wob.shape, (GROUPS, CHUNK, CHUNK), (CHUNK, CHUNK),
                  (1, D_MODEL), (1, D_MODEL), (1, SGU_W), (1, SGU_W), (1, 128), (1, GATE_W), (1, MID_W)]
    col_spec = lambda rows: pl.BlockSpec((rows, ts), lambda i: (0, i))
    return pl.pallas_call(
        body, name="mid", grid=(nsteps,),
        in_specs=[_row_spec(ts, D_MODEL), _row_spec(ts, D_MODEL), _row_spec(ts, MLA_W), _row_spec(ts, MID_W),
                  _row_spec(ts, GATE_W), _full_spec(woa.shape), _full_spec(wob.shape), _full_spec(wout.shape),
                  _full_spec(ln_g.shape), _full_spec(ln_b.shape), _full_spec(sg_g.shape), _full_spec(sg_b.shape),
                  _full_spec(w_s.shape), _full_spec(bsb.shape)],
        out_specs=[_row_spec(ts, D_MODEL), _row_spec(ts, GATE_W), _row_spec(ts, MID_W), _row_spec(ts, MLA_W),
                   col_spec(MLA_W), col_spec(HEADS), col_spec(GATE_W), col_spec(MID_W)]
        + [_full_spec(sh) for sh in acc_shapes],
        out_shape=[jax.ShapeDtypeStruct((s, D_MODEL), F32), jax.ShapeDtypeStruct((s, GATE_W), BF16),
                   jax.ShapeDtypeStruct((s, MID_W), BF16), jax.ShapeDtypeStruct((s, MLA_W), BF16),
                   jax.ShapeDtypeStruct((MLA_W, s), BF16), jax.ShapeDtypeStruct((HEADS, s), F32),
                   jax.ShapeDtypeStruct((GATE_W, s), BF16), jax.ShapeDtypeStruct((MID_W, s), BF16)]
        + [jax.ShapeDtypeStruct(sh, BF16 if n < 3 else F32) for n, sh in enumerate(acc_shapes)],
        scratch_shapes=[pltpu.VMEM((CHUNK, SGU_W), F32)] + [pltpu.VMEM(sh, F32) for sh in acc_shapes[:3]],
        compiler_params=_params(),
    )(x, tgt, o, hm, hg, woa, wob, wout, ln_g, ln_b, sg_g, sg_b, w_s, bsb)


def _lat_bwd(dq, dk, dv, hl, rc, rsl, rsh, g_q, g_kv, wuq, wk, wv, after):
    s = dk.shape[0]
    ts = ROW_TILE
    qk_w = HEADS * HEAD_PAD

    def body(dq_ref, dk_ref, dv_ref, hl_ref, rc_ref, rsl_ref, rsh_ref, gq_ref, gkv_ref, wuq_ref, wk_ref, wv_ref,
             after_ref, dhl_ref, dhlt_ref, dwuq_ref, dwk_ref, dwv_ref, dgq_ref, dgkv_ref, dbl_ref):
        i = pl.program_id(0)

        @pl.when(i == 0)
        def _():
            for r in (dwuq_ref, dwk_ref, dwv_ref, dgq_ref, dgkv_ref, dbl_ref):
                r[...] = jnp.zeros_like(r)

        def emit(lo, val):
            vb = val.astype(BF16)
            n = val.shape[1]
            dhl_ref[:, lo:lo + n] = vb
            dhlt_ref[lo:lo + n, :] = vb.T
            dbl_ref[:, lo:lo + n] += jnp.sum(val, axis=0, keepdims=True)

        c, sl, sh = rc_ref[...], rsl_ref[...], rsh_ref[...]
        lane = lax.broadcasted_iota(jnp.int32, (ts, HEAD_PAD), 1)
        pe = (lane >= NOPE) & (lane < QK_DIM)
        dkpe = jnp.zeros((ts, HEAD_PAD), F32)
        dqu = []
        for hd in range(HEADS):
            lanes = slice(hd * HEAD_PAD, (hd + 1) * HEAD_PAD)
            dqu.append(_rope_t(dq_ref[lanes, :].T, c, sl, sh).astype(BF16))
            dkpe = dkpe + dk_ref[:, lanes]
        dqu = jnp.concatenate(dqu, axis=1)
        dkpe = _rope_t(jnp.where(pe, dkpe, 0.0), c, sl, sh)

        cq = hl_ref[:, :Q_RANK]
        rq = lax.rsqrt(jnp.mean(cq * cq, axis=-1, keepdims=True) + RMS_EPS)
        cqh = cq * rq
        cqn = (cqh * gq_ref[...]).astype(BF16)
        dwuq_ref[...] += _dot_tn(cqn, dqu)
        dcqn = _dot_nt(dqu, wuq_ref[...])
        dgq_ref[...] += jnp.sum(dcqn * cqh, axis=0, keepdims=True)
        dch = dcqn * gq_ref[...]
        emit(0, rq * (dch - cqh * jnp.mean(dch * cqh, axis=-1, keepdims=True)))

        ckv = hl_ref[:, Q_RANK:Q_RANK + KV_RANK]
        rk = lax.rsqrt(jnp.mean(ckv * ckv, axis=-1, keepdims=True) + RMS_EPS)
        ckh = ckv * rk
        ckn = (ckh * gkv_ref[...]).astype(BF16)
        dkb = dk_ref[...].astype(BF16)
        dvb = dv_ref[...].astype(BF16)
        dwk_ref[...] += _dot_tn(ckn, dkb)
        dwv_ref[...] += _dot_tn(ckn, dvb)
        dckn = _dot_nt(dkb, wk_ref[...]) + _dot_nt(dvb, wv_ref[...])
        dgkv_ref[...] += jnp.sum(dckn * ckh, axis=0, keepdims=True)
        dkh = dckn * gkv_ref[...]
        emit(Q_RANK, rk * (dkh - ckh * jnp.mean(dkh * ckh, axis=-1, keepdims=True)))
        emit(Q_RANK + KV_RANK, dkpe)

    acc_shapes = [wuq.shape, wk.shape, wv.shape, g_q.shape, g_kv.shape, (1, LAT_W)]
    return pl.pallas_call(
        body, name="lat_bwd", grid=(s // ts,),
        in_specs=[pl.BlockSpec((qk_w, ts), lambda i: (0, i)), _row_spec(ts, qk_w), _row_spec(ts, MLA_W),
                  _row_spec(ts, LAT_W), _row_spec(ts, HEAD_PAD), _row_spec(ts, HEAD_PAD), _row_spec(ts, HEAD_PAD),
                  _full_spec(g_q.shape), _full_spec(g_kv.shape), _full_spec(wuq.shape), _full_spec(wk.shape),
                  _full_spec(wv.shape), pl.BlockSpec(memory_space=pl.ANY)],
        out_specs=[_row_spec(ts, LAT_W), pl.BlockSpec((LAT_W, ts), lambda i: (0, i))]
        + [_full_spec(sh) for sh in acc_shapes],
        out_shape=[jax.ShapeDtypeStruct((s, LAT_W), BF16), jax.ShapeDtypeStruct((LAT_W, s), BF16)]
        + [jax.ShapeDtypeStruct(sh, F32) for sh in acc_shapes],
        compiler_params=_params(),
    )(dq, dk, dv, hl, rc, rsl, rsh, g_q, g_kv, wuq, wk, wv, after)


def _dx(dr, dhg, dhm, dhl, wt, after):
    s = dr.shape[0]
    ts = MATMUL_ROW_TILE

    def body(dr_ref, dhg_ref, dhm_ref, dhl_ref, wt_ref, after_ref, dx_ref):
        dx_ref[...] = (ALPHA * dr_ref[...]
                       + _dot(dhg_ref[...], wt_ref[ROW_GATE:IN_W, :])
                       + _dot(dhm_ref[...], wt_ref[LAT_COLS:ROW_GATE, :])
                       + _dot(dhl_ref[:, 0:Q_RANK + KV_RANK], wt_ref[0:Q_RANK + KV_RANK, :])
                       + _dot(dhl_ref[:, Q_RANK + KV_RANK:], _kpe_rows(wt_ref)))

    return pl.pallas_call(
        body, name="dx", grid=(s // ts,),
        in_specs=[_row_spec(ts, D_MODEL), _row_spec(ts, GATE_W), _row_spec(ts, MID_W), _row_spec(ts, LAT_W),
                  _full_spec(wt.shape), pl.BlockSpec(memory_space=pl.ANY)],
        out_specs=_row_spec(ts, D_MODEL),
        out_shape=jax.ShapeDtypeStruct((s, D_MODEL), F32),
        compiler_params=_params(),
    )(dr, dhg, dhm, dhl, wt, after)


def _dwt_early(dhmt, dhgt, xb, col, after, name):
    tn = 512
    nm, ng = MID_W // tn, GATE_W // tn
    s = dhmt.shape[1]
    hc = D_MODEL // 2

    ks = s // 2

    def body(col_ref, dma_ref, dmb_ref, dga_ref, dgb_ref, xba_ref, xbb_ref, after_ref, dw_ref):
        i = pl.program_id(0)

        @pl.when(i < nm)
        def _():
            dw_ref[...] = (_dot(dma_ref[...], xba_ref[...]) + _dot(dmb_ref[...], xbb_ref[...])).astype(BF16)

        @pl.when(i >= nm)
        def _():
            dw_ref[...] = (_dot(dga_ref[...], xba_ref[...]) + _dot(dgb_ref[...], xbb_ref[...])).astype(BF16)

    def dh_spec(first, part):
        if first:
            return pl.BlockSpec((tn, ks), lambda i, col_ref: (jnp.minimum(i, nm - 1), part))
        return pl.BlockSpec((tn, ks), lambda i, col_ref: (jnp.maximum(i - nm, 0), part))

    rows = pl.pallas_call(
        body, name=name,
        grid_spec=pltpu.PrefetchScalarGridSpec(
            num_scalar_prefetch=1, grid=(nm + ng,),
            in_specs=[dh_spec(True, 0), dh_spec(True, 1), dh_spec(False, 0), dh_spec(False, 1),
                      pl.BlockSpec((None, ks, hc), lambda i, col_ref: (col_ref[0], 0, 0)),
                      pl.BlockSpec((None, ks, hc), lambda i, col_ref: (col_ref[0], 1, 0)),
                      pl.BlockSpec(memory_space=pl.ANY)],
            out_specs=pl.BlockSpec((pl.Element(tn), pl.Element(hc)),
                                   lambda i, col_ref: (pl.multiple_of(LAT_COLS + i * tn, 32), 0))),
        out_shape=jax.ShapeDtypeStruct((IN_W, hc), BF16),
        compiler_params=_params(),
    )(col, dhmt, dhmt, dhgt, dhgt, xb, xb, after)

    def zero(buf_ref, out_ref):
        out_ref[...] = jnp.zeros_like(out_ref)

    return pl.pallas_call(
        zero, name=name + "_zero_lat", grid=(1,), in_specs=[pl.BlockSpec(memory_space=pl.ANY)],
        out_specs=pl.BlockSpec((LAT_COLS, hc), lambda i: (0, 0)),
        out_shape=jax.ShapeDtypeStruct((IN_W, hc), BF16), input_output_aliases={0: 0},
    )(rows)


def _dwt_lat(dhlt, xb):
    n, s = dhlt.shape

    def body(dht_ref, xb_ref, dw_ref):
        dw = _dot(dht_ref[...], xb_ref[...]).astype(BF16)
        kpe = Q_RANK + KV_RANK + NOPE
        dw_ref[0:Q_RANK + KV_RANK, :] = dw[0:Q_RANK + KV_RANK]
        dw_ref[Q_RANK + KV_RANK:LAT_COLS, :] = dw[kpe:kpe + ROPE]
        dw_ref[LAT_COLS:, :] = jnp.zeros((LAT_ROWS_PAD - LAT_COLS, D_MODEL), BF16)

    return pl.pallas_call(
        body, name="dwt_lat", in_specs=[VMEM_SPEC, VMEM_SPEC], out_specs=VMEM_SPEC,
        out_shape=jax.ShapeDtypeStruct((LAT_ROWS_PAD, D_MODEL), BF16),
        compiler_params=pltpu.CompilerParams(vmem_limit_bytes=VMEM_LIMIT),
    )(dhlt, xb)


def _split_bias(b):
    z = lambda n: jnp.zeros((n,), b.dtype)
    lat = jnp.concatenate([b[:Q_RANK + KV_RANK], z(NOPE), b[Q_RANK + KV_RANK:LAT_COLS], z(HEAD_PAD - QK_DIM)])
    return b[None, ROW_GATE:], b[None, LAT_COLS:ROW_GATE], lat[None, :]


def _join_bias(g, m, l):
    kpe = Q_RANK + KV_RANK + NOPE
    return jnp.concatenate([l[0, :Q_RANK + KV_RANK], l[0, kpe:kpe + ROPE], m[0], g[0]])


def _rope_tables(positions):
    half = ROPE // 2
    inv_freq = ROPE_THETA ** (-jnp.arange(0, ROPE, 2, dtype=F32) / ROPE)
    ang = positions.astype(F32)[:, None] * inv_freq
    cos, sin = jnp.cos(ang), jnp.sin(ang)
    n = positions.shape[0]
    one, zero = jnp.ones((n, NOPE), F32), jnp.zeros((n, half), F32)
    tail1, tail0 = jnp.ones((n, HEAD_PAD - QK_DIM), F32), jnp.zeros((n, HEAD_PAD - QK_DIM), F32)
    z64 = jnp.zeros((n, NOPE), F32)
    rc = jnp.concatenate([one, cos, cos, tail1], axis=1)
    rsl = jnp.concatenate([z64, -sin, zero, tail0], axis=1)
    rsh = jnp.concatenate([z64, zero, sin, tail0], axis=1)
    return rc, rsl, rsh


def _local_attention(x, tables, wlat, b_in, g_q, w_uq, g_kv, w_ukv, after):
    rc, rsl, rsh = tables
    b_g, b_m, b_l = _split_bias(b_in)
    wuq = jnp.pad(w_uq, ((0, 0), (0, 0), (0, HEAD_PAD - QK_DIM))).reshape(Q_RANK, HEADS * HEAD_PAD).astype(BF16)
    wk = jnp.pad(w_ukv[:, :, :NOPE], ((0, 0), (0, 0), (0, HEAD_PAD - NOPE))).reshape(KV_RANK, HEADS * HEAD_PAD).astype(BF16)
    wv = w_ukv[:, :, NOPE:].reshape(KV_RANK, MLA_W).astype(BF16)
    gq2, gkv2 = g_q[None, :], g_kv[None, :]
    hl, q, k, v, xb, qt, kt, vt, xb2 = _fwd_lat(x, wlat, b_l, gq2, wuq, gkv2, wk, wv, rc, rsl, rsh, after)
    o, lse = _attn_fwd(qt, k, vt)
    return dict(q=q, qt=qt, k=k, kt=kt, v=v, o=o, lse=lse, hl=hl, rc=rc, rsl=rsl, rsh=rsh, gq2=gq2, gkv2=gkv2,
                wuq=wuq, wk=wk, wv=wv, xb=xb, xb2=xb2, b_g=b_g, b_m=b_m)


def _local_head(st, x, tgt, wt, w_oa, sg_g, sg_b, w_s, b_s, w_ob, w_out, ln_g, ln_b):
    q, qt, k, kt, v, o, lse, hl, xb = (st[n] for n in ("q", "qt", "k", "kt", "v", "o", "lse", "hl", "xb"))
    rc, rsl, rsh, gq2, gkv2, wuq, wk, wv = (st[n] for n in ("rc", "rsl", "rsh", "gq2", "gkv2", "wuq", "wk", "wv"))
    bsb = jnp.repeat(b_s.T, V_DIM, axis=1)
    hg, hm = _fwd_rest(xb, wt, st["b_g"], st["b_m"])
    (dr, dhg, dhm, do, dot, delta, dhgt, dhmt, dwout, dwoa, dwob, dws, dbs, dlng, dlnb, dsgg, dsgb, loss, dbg,
     dbm) = _mid(x, tgt, o, hm, hg, w_oa, w_ob, w_out, ln_g[None, :], ln_b[None, :], sg_g[None, :], sg_b[None, :],
                 w_s, bsb)
    delta = jnp.pad(delta.reshape(HEADS // 2, 2, -1), ((0, 0), (0, 6), (0, 0)))
    early = {
        "w_oa": dwoa, "sgu_ln_g": dsgg[0], "sgu_ln_b": dsgb[0], "w_s": dws, "b_s": dbs[:, :GROUPS].T,
        "w_ob": dwob, "w_out": dwout, "ln_g": dlng[0], "ln_b": dlnb[0],
    }
    state = dict(q=q, qt=qt, k=k, kt=kt, v=v, do=do, dot=dot, lse=lse, delta=delta, hl=hl, rc=rc, rsl=rsl, rsh=rsh,
                 gq2=gq2, gkv2=gkv2, wuq=wuq, wk=wk, wv=wv, dr=dr, dhg=dhg, dhm=dhm, wt=wt, xb=xb, dbg=dbg, dbm=dbm,
                 dhgt=dhgt, dhmt=dhmt, xb2=st["xb2"])
    return loss, early, state


def _local_attn_bwd(st):
    return _attn_bwd(st["q"], st["qt"], st["k"], st["kt"], st["v"], st["do"], st["dot"], st["lse"], st["delta"])


def _local_tail(st, dq, dk, dv, after):
    dhl, dhlt, dwuq, dwk, dwv, dgq, dgkv, dbl = _lat_bwd(dq, dk, dv, st["hl"], st["rc"], st["rsl"], st["rsh"],
                                                         st["gq2"], st["gkv2"], st["wuq"], st["wk"], st["wv"], after)
    late = {
        "w_lat": _dwt_lat(dhlt, st["xb"]),
        "b_in": _join_bias(st["dbg"], st["dbm"], dbl),
        "g_q": dgq[0],
        "w_uq": dwuq.reshape(Q_RANK, HEADS, HEAD_PAD)[:, :, :QK_DIM],
        "g_kv": dgkv[0],
        "w_ukv": jnp.concatenate([dwk.reshape(KV_RANK, HEADS, HEAD_PAD)[:, :, :NOPE],
                                  dwv.reshape(KV_RANK, HEADS, V_DIM)], axis=2),
    }
    return dhl, late


def _local_step(x, positions, tgt, wt, b_in, g_q, w_uq, g_kv, w_ukv, w_oa, sg_g, sg_b, w_s, b_s, w_ob, w_out, ln_g,
                ln_b):
    st = _local_attention(x, _rope_tables(positions), wt[:LAT_COLS], b_in, g_q, w_uq, g_kv, w_ukv, b_in)
    loss, early, st = _local_head(st, x, tgt, wt, w_oa, sg_g, sg_b, w_s, b_s, w_ob, w_out, ln_g, ln_b)
    dq, dk, dv = _local_attn_bwd(st)
    dhl, late = _local_tail(st, dq, dk, dv, dv)
    dx = _dx(st["dr"], st["dhg"], st["dhm"], dhl, st["wt"], dhl)
    grads = {**early, **late}
    halves = [_dwt_early(st["dhmt"], st["dhgt"], st["xb2"], jnp.full((1,), h, jnp.int32), dhl, "dwt_half%d" % h)
              for h in range(2)]
    grads["w_in"] = jnp.concatenate([grads.pop("w_lat")[:LAT_COLS], jnp.concatenate(halves, axis=1)[LAT_COLS:]],
                                    axis=0)
    return loss, dx, grads


MESH = pl.DeviceIdType.MESH
N_CHIPS = 4
HBM_SPEC = pl.BlockSpec(memory_space=pl.ANY)
HBM_SPEC_STRICT = pl.BlockSpec(memory_space=pltpu.HBM)
VMEM_SPEC = pl.BlockSpec(memory_space=pltpu.VMEM)

REP_ROWS = 80


def _rows8(a):
    flat = a.reshape(-1)
    n = -(-flat.shape[0] // (8 * D_MODEL)) * 8 * D_MODEL
    return jnp.pad(flat, (0, n - flat.shape[0])).reshape(-1, D_MODEL)


def _place():
    x, y, c = lax.axis_index("x"), lax.axis_index("y"), lax.axis_index("c")
    others = [(1 - x, y), (x, 1 - y), (1 - x, 1 - y)]
    return x, y, c, others


def _gather_weights(shards):
    n = len(shards)

    def body(*refs):
        ins, outs, bufs = refs[:n], refs[n:2 * n], refs[2 * n:3 * n]
        send_sems, recv_sems, local_sems = refs[3 * n:]
        x, y, c, others = _place()
        me = 2 * x + y
        sibling = (x, y, 1 - c)
        for src, buf in zip(ins, bufs):
            buf[...] = src[...].astype(BF16)
        own = [pltpu.make_async_copy(bufs[w], outs[w].at[me], local_sems.at[w]) for w in range(n)]
        for cp in own:
            cp.start()

        def part(w, chip, half):
            hc = shards[w].shape[1] // 2
            return outs[w].at[chip, :, pl.ds(half * hc, hc)]

        def sent(w, j):
            hc = shards[w].shape[1] // 2
            return pltpu.make_async_remote_copy(
                src_ref=bufs[w].at[:, pl.ds(c * hc, hc)], dst_ref=part(w, me, c),
                send_sem=send_sems.at[w * 3 + j], recv_sem=recv_sems.at[w * 3 + j],
                device_id=(*others[j], c), device_id_type=MESH)

        def landed(w, j):
            px, py = others[j]
            return pltpu.make_async_remote_copy(
                src_ref=part(w, 2 * px + py, c), dst_ref=part(w, 2 * px + py, c),
                send_sem=send_sems.at[w * 3 + j], recv_sem=recv_sems.at[w * 3 + j],
                device_id=(px, py, c), device_id_type=MESH)

        def passed(w, j, half):
            px, py = others[j]
            k = n * 3 + w * 3 + j
            return pltpu.make_async_remote_copy(
                src_ref=part(w, 2 * px + py, half), dst_ref=part(w, 2 * px + py, half),
                send_sem=send_sems.at[k], recv_sem=recv_sems.at[k], device_id=sibling, device_id_type=MESH)

        first = [sent(w, j) for w in range(n) for j in range(3)]
        for cp in first:
            cp.start()
        fwd = []
        for w in range(n):
            for j in range(3):
                landed(w, j).wait_recv()
                cp = passed(w, j, c)
                cp.start()
                fwd.append(cp)
        for w in range(n):
            for j in range(3):
                passed(w, j, 1 - c).wait_recv()
        for cp in first + fwd:
            cp.wait_send()
        for cp in own:
            cp.wait()

    return pl.pallas_call(
        body, name="gather_weights",
        in_specs=[VMEM_SPEC] * n, out_specs=[HBM_SPEC] * n,
        out_shape=[jax.ShapeDtypeStruct((N_CHIPS,) + s.shape, BF16) for s in shards],
        scratch_shapes=[pltpu.VMEM(s.shape, BF16) for s in shards]
        + [pltpu.SemaphoreType.DMA((6 * n,)), pltpu.SemaphoreType.DMA((6 * n,)), pltpu.SemaphoreType.DMA((n,))],
        compiler_params=pltpu.CompilerParams(vmem_limit_bytes=VMEM_LIMIT),
    )(*shards)


N_DEV = 8
LOSS_TILE = (8, 128)


def _gather_first(lat, uq):
    hl, hu = lat.shape[1] // 2, uq.shape[1] // 2

    def body(lat_ref, uq_ref, wlat_ref, guq_ref, lat_buf, uq_buf, send_sems, recv_sems, local_sems):
        x, y, c, others = _place()
        me = 2 * x + y
        sibling = (x, y, 1 - c)
        lat_buf[...] = lat_ref[...].astype(BF16)
        uq_buf[...] = uq_ref[...].astype(BF16)

        def copy(src, dst, k, to):
            return pltpu.make_async_remote_copy(src_ref=src, dst_ref=dst, send_sem=send_sems.at[k],
                                                recv_sem=recv_sems.at[k], device_id=to, device_id_type=MESH)

        def uq_part(chip, half):
            return guq_ref.at[chip, :, pl.ds(half * hu, hu)]

        def lat_part(half):
            return wlat_ref.at[:, pl.ds(half * hl, hl)]

        own = pltpu.make_async_copy(uq_buf, guq_ref.at[me], local_sems.at[0])
        own.start()
        first = [copy(uq_buf.at[:, pl.ds(c * hu, hu)], uq_part(me, c), j, (*others[j], c)) for j in range(3)]
        for cp in first:
            cp.start()

        @pl.when(me == 0)
        def _():
            mine = pltpu.make_async_copy(lat_buf, wlat_ref, local_sems.at[1])
            mine.start()
            cps = [copy(lat_buf.at[:, pl.ds(c * hl, hl)], lat_part(c), 6 + j, (*others[j], c)) for j in range(3)]
            for cp in cps:
                cp.start()
            for cp in cps:
                cp.wait_send()
            mine.wait()

        @pl.when(me != 0)
        def _():
            j0 = x + 2 * y - 1
            copy(lat_part(c), lat_part(c), 6 + j0, (0, 0, c)).wait_recv()
            fwd = copy(lat_part(c), lat_part(c), 9, sibling)
            fwd.start()
            copy(lat_part(1 - c), lat_part(1 - c), 9, sibling).wait_recv()
            fwd.wait_send()

        fwd = []
        for j, (px, py) in enumerate(others):
            chip = 2 * px + py
            copy(uq_part(chip, c), uq_part(chip, c), j, (px, py, c)).wait_recv()
            cp = copy(uq_part(chip, c), uq_part(chip, c), 3 + j, sibling)
            cp.start()
            fwd.append(cp)
        for j, (px, py) in enumerate(others):
            chip = 2 * px + py
            copy(uq_part(chip, 1 - c), uq_part(chip, 1 - c), 3 + j, sibling).wait_recv()
        for cp in first + fwd:
            cp.wait_send()
        own.wait()

    return pl.pallas_call(
        body, name="gather_first", in_specs=[VMEM_SPEC, VMEM_SPEC], out_specs=[HBM_SPEC, HBM_SPEC],
        out_shape=[jax.ShapeDtypeStruct(lat.shape, BF16), jax.ShapeDtypeStruct((N_CHIPS,) + uq.shape, BF16)],
        scratch_shapes=[pltpu.VMEM(lat.shape, BF16), pltpu.VMEM(uq.shape, BF16), pltpu.SemaphoreType.DMA((10,)),
                        pltpu.SemaphoreType.DMA((10,)), pltpu.SemaphoreType.DMA((2,))],
        compiler_params=pltpu.CompilerParams(vmem_limit_bytes=VMEM_LIMIT),
    )(lat, uq)


def _cast_own(shards, me, after):
    n = len(shards)

    def body(me_ref, *refs):
        for w in range(n):
            refs[n + 1 + w][...] = refs[w][...].astype(BF16)

    return pl.pallas_call(
        body, name="cast_own",
        grid_spec=pltpu.PrefetchScalarGridSpec(
            num_scalar_prefetch=1, grid=(1,),
            in_specs=[pl.BlockSpec(s.shape, lambda i, me_ref: (0, 0)) for s in shards]
            + [pl.BlockSpec(memory_space=pl.ANY)],
            out_specs=[pl.BlockSpec((None,) + s.shape, lambda i, me_ref: (me_ref[0], 0, 0)) for s in shards]),
        out_shape=[jax.ShapeDtypeStruct((N_CHIPS,) + s.shape, BF16) for s in shards],
        compiler_params=pltpu.CompilerParams(vmem_limit_bytes=VMEM_LIMIT),
    )(me, *shards, after)


def _cast_first(lat, uq, me):
    def body(me_ref, lat_ref, uq_ref, wlat_ref, guq_ref):
        wlat_ref[...] = lat_ref[...].astype(BF16)
        guq_ref[...] = uq_ref[...].astype(BF16)

    return pl.pallas_call(
        body, name="cast_first",
        grid_spec=pltpu.PrefetchScalarGridSpec(
            num_scalar_prefetch=1, grid=(1,),
            in_specs=[pl.BlockSpec((LAT_COLS, D_MODEL), lambda i, me_ref: (0, 0)),
                      pl.BlockSpec(uq.shape, lambda i, me_ref: (0, 0))],
            out_specs=[pl.BlockSpec((LAT_COLS, D_MODEL), lambda i, me_ref: (0, 0)),
                       pl.BlockSpec((None,) + uq.shape, lambda i, me_ref: (me_ref[0], 0, 0))]),
        out_shape=[jax.ShapeDtypeStruct((LAT_COLS, D_MODEL), BF16),
                   jax.ShapeDtypeStruct((N_CHIPS,) + uq.shape, BF16)],
    )(me, lat, uq)


def _first_copies(wlat_ref, guq_ref, send_sems, recv_sems, shapes):
    x, y, c, others = _place()
    me = 2 * x + y
    hl, hu = shapes[0][1] // 2, shapes[1][2] // 2
    lat_half = wlat_ref.at[:, pl.ds(c * hl, hl)]

    def copy(src, dst, k, to):
        return pltpu.make_async_remote_copy(src_ref=src, dst_ref=dst, send_sem=send_sems.at[k],
                                            recv_sem=recv_sems.at[k], device_id=to, device_id_type=MESH)

    def uq_half(chip):
        return guq_ref.at[chip, :, pl.ds(c * hu, hu)]

    lat_out = [copy(lat_half, lat_half, j, (*others[j], c)) for j in range(3)]
    uq_out = [copy(uq_half(me), uq_half(me), 3 + j, (*others[j], c)) for j in range(3)]
    j0 = jnp.maximum(x + 2 * y - 1, 0)
    lat_in = copy(lat_half, lat_half, j0, (0, 0, c))
    uq_in = [copy(uq_half(me), uq_half(2 * px + py), 3 + j, (px, py, c)) for j, (px, py) in enumerate(others)]
    return me, lat_out, uq_out, lat_in, uq_in


def _first_start(wlat, guq):
    shapes = (wlat.shape, guq.shape)

    def body(wlat_ref, guq_ref, send_sems, recv_sems, wlat_thru, guq_thru, token):
        me, lat_out, uq_out, _, _ = _first_copies(wlat_ref, guq_ref, send_sems, recv_sems, shapes)

        @pl.when(me == 0)
        def _():
            for cp in lat_out:
                cp.start()

        for cp in uq_out:
            cp.start()
        token[...] = jnp.zeros_like(token)

    outs = pl.pallas_call(
        body, name="first_start",
        out_shape=(pltpu.SemaphoreType.DMA((6,)), pltpu.SemaphoreType.DMA((6,)), pltpu.HBM(wlat.shape, BF16),
                   pltpu.HBM(guq.shape, BF16), jax.ShapeDtypeStruct(LOSS_TILE, F32)),
        in_specs=[HBM_SPEC_STRICT] * 2, out_specs=(SEM_SPEC, SEM_SPEC, HBM_SPEC_STRICT, HBM_SPEC_STRICT, VMEM_SPEC),
        input_output_aliases={0: 2, 1: 3},
        compiler_params=pltpu.CompilerParams(has_side_effects=SPLIT_EFFECT),
    )(pltpu.with_memory_space_constraint(wlat, pltpu.HBM), pltpu.with_memory_space_constraint(guq, pltpu.HBM))
    return outs


def _first_wait(send_sems, recv_sems, wlat, guq, *after):
    shapes = (wlat.shape, guq.shape)

    def body(wlat_ref, guq_ref, send_sems, recv_sems, *rest):
        me, lat_out, uq_out, lat_in, uq_in = _first_copies(wlat_ref, guq_ref, send_sems, recv_sems, shapes)

        @pl.when(me == 0)
        def _():
            for cp in lat_out:
                cp.wait_send()

        @pl.when(me != 0)
        def _():
            lat_in.wait_recv()

        for cp in uq_out:
            cp.wait_send()
        for cp in uq_in:
            cp.wait_recv()

    return pl.pallas_call(
        body, name="first_wait", out_shape=(pltpu.HBM(wlat.shape, BF16), pltpu.HBM(guq.shape, BF16)),
        in_specs=[HBM_SPEC_STRICT, HBM_SPEC_STRICT, SEM_SPEC, SEM_SPEC] + [HBM_SPEC] * len(after),
        out_specs=(HBM_SPEC_STRICT, HBM_SPEC_STRICT), input_output_aliases={0: 0, 1: 1},
        compiler_params=pltpu.CompilerParams(has_side_effects=SPLIT_EFFECT),
    )(wlat, guq, send_sems, recv_sems, *after)


def _first_forward(wlat, guq):
    hl, hu = wlat.shape[1] // 2, guq.shape[2] // 2

    def body(wlat_in, guq_in, wlat_ref, guq_ref, send_sems, recv_sems):
        x, y, c, others = _place()
        me = 2 * x + y
        sibling = (x, y, 1 - c)

        def copy(part, k):
            return pltpu.make_async_remote_copy(src_ref=part, dst_ref=part, send_sem=send_sems.at[k],
                                                recv_sem=recv_sems.at[k], device_id=sibling, device_id_type=MESH)

        def uq_part(j, half):
            px, py = others[j]
            return guq_ref.at[2 * px + py, :, pl.ds(half * hu, hu)]

        cps = [copy(uq_part(j, c), j) for j in range(3)]
        for cp in cps:
            cp.start()

        @pl.when(me != 0)
        def _():
            mine = copy(wlat_ref.at[:, pl.ds(c * hl, hl)], 3)
            mine.start()
            copy(wlat_ref.at[:, pl.ds((1 - c) * hl, hl)], 3).wait_recv()
            mine.wait_send()

        for j in range(3):
            copy(uq_part(j, 1 - c), j).wait_recv()
        for cp in cps:
            cp.wait_send()

    return pl.pallas_call(
        body, name="first_forward", in_specs=[HBM_SPEC, HBM_SPEC], out_specs=[HBM_SPEC, HBM_SPEC],
        out_shape=[jax.ShapeDtypeStruct(wlat.shape, BF16), jax.ShapeDtypeStruct(guq.shape, BF16)],
        input_output_aliases={0: 0, 1: 1},
        scratch_shapes=[pltpu.SemaphoreType.DMA((4,)), pltpu.SemaphoreType.DMA((4,))],
    )(wlat, guq)


def _gather_start(bufs, after):
    n = len(bufs)

    def body(*refs):
        b_refs = refs[:n]
        send_sems, recv_sems, token = refs[n + 1], refs[n + 2], refs[-1]
        x, y, c, others = _place()
        me = 2 * x + y
        for w in range(n):
            hc = _half(bufs[w])
            mine = b_refs[w].at[me, :, pl.ds(c * hc, hc)]
            for j, (px, py) in enumerate(others):
                pltpu.make_async_remote_copy(
                    src_ref=mine, dst_ref=mine, send_sem=send_sems.at[3 * w + j], recv_sem=recv_sems.at[3 * w + j],
                    device_id=(px, py, c), device_id_type=MESH).start()
        token[...] = jnp.zeros_like(token)

    hbm = [pltpu.HBM(b.shape, BF16) for b in bufs]
    outs = pl.pallas_call(
        body, name="gather_start",
        out_shape=(pltpu.SemaphoreType.DMA((3 * n,)), pltpu.SemaphoreType.DMA((3 * n,)), *hbm,
                   jax.ShapeDtypeStruct(LOSS_TILE, F32)),
        in_specs=[HBM_SPEC_STRICT] * n + [HBM_SPEC],
        out_specs=(SEM_SPEC, SEM_SPEC, *[HBM_SPEC_STRICT] * n, VMEM_SPEC),
        input_output_aliases={i: 2 + i for i in range(n)},
        compiler_params=pltpu.CompilerParams(has_side_effects=SPLIT_EFFECT),
    )(*[pltpu.with_memory_space_constraint(b, pltpu.HBM) for b in bufs], after)
    return outs[0], outs[1], list(outs[2:2 + n]), outs[-1]


def _gather_wait(send_sems, recv_sems, bufs, after):
    n = len(bufs)

    def body(*refs):
        b_refs = refs[:n]
        send_sems, recv_sems = refs[n], refs[n + 1]
        x, y, c, others = _place()
        me = 2 * x + y
        for w in range(n):
            hc = _half(bufs[w])
            for j, (px, py) in enumerate(others):
                cp = pltpu.make_async_remote_copy(
                    src_ref=b_refs[w].at[me, :, pl.ds(c * hc, hc)],
                    dst_ref=b_refs[w].at[2 * px + py, :, pl.ds(c * hc, hc)],
                    send_sem=send_sems.at[3 * w + j], recv_sem=recv_sems.at[3 * w + j], device_id=(px, py, c),
                    device_id_type=MESH)
                cp.wait_send()
                cp.wait_recv()

    outs = pl.pallas_call(
        body, name="gather_wait", out_shape=tuple(pltpu.HBM(b.shape, b.dtype) for b in bufs),
        in_specs=[HBM_SPEC_STRICT] * n + [SEM_SPEC, SEM_SPEC, HBM_SPEC],
        out_specs=tuple([HBM_SPEC_STRICT] * n), input_output_aliases={i: i for i in range(n)},
        compiler_params=pltpu.CompilerParams(has_side_effects=SPLIT_EFFECT),
    )(*bufs, send_sems, recv_sems, after)
    return list(outs)


def _gather_finish(bufs):
    n = len(bufs)

    def body(*refs):
        b_refs = refs[n:2 * n]
        send_sems, recv_sems = refs[2 * n:]
        x, y, c, others = _place()
        cps = []
        for w in range(n):
            hc = _half(bufs[w])
            for j, (px, py) in enumerate(others):
                part = b_refs[w].at[2 * px + py, :, pl.ds(c * hc, hc)]
                cps.append(pltpu.make_async_remote_copy(
                    src_ref=part, dst_ref=part, send_sem=send_sems.at[3 * w + j], recv_sem=recv_sems.at[3 * w + j],
                    device_id=(x, y, 1 - c), device_id_type=MESH))
        for cp in cps:
            cp.start()
        for w in range(n):
            hc = _half(bufs[w])
            for j, (px, py) in enumerate(others):
                theirs = b_refs[w].at[2 * px + py, :, pl.ds((1 - c) * hc, hc)]
                pltpu.make_async_remote_copy(
                    src_ref=theirs, dst_ref=theirs, send_sem=send_sems.at[3 * w + j], recv_sem=recv_sems.at[3 * w + j],
                    device_id=(x, y, 1 - c), device_id_type=MESH).wait_recv()
        for cp in cps:
            cp.wait_send()

    return pl.pallas_call(
        body, name="gather_finish", in_specs=[HBM_SPEC] * n, out_specs=[HBM_SPEC] * n,
        out_shape=[jax.ShapeDtypeStruct(b.shape, b.dtype) for b in bufs],
        input_output_aliases={i: i for i in range(n)},
        scratch_shapes=[pltpu.SemaphoreType.DMA((3 * n,)), pltpu.SemaphoreType.DMA((3 * n,))],
    )(*bufs)


def _half(a):
    return a.shape[-1] // 2


def _exchange_pairs(parts, name):
    n = len(parts)

    def body(*refs):
        p_refs, r_refs = refs[:n], refs[n:2 * n]
        send_sems, recv_sems = refs[2 * n:]
        x, y, c, _ = _place()
        cps = []
        for w in range(n):
            h = _half(parts[w])
            cps.append(pltpu.make_async_remote_copy(
                src_ref=p_refs[w].at[:, :, pl.ds((1 - c) * h, h)], dst_ref=r_refs[w],
                send_sem=send_sems.at[w], recv_sem=recv_sems.at[w], device_id=(x, y, 1 - c), device_id_type=MESH))
        for cp in cps:
            cp.start()
        for cp in cps:
            cp.wait()

    return pl.pallas_call(
        body, name=name, in_specs=[HBM_SPEC] * n, out_specs=[HBM_SPEC] * n,
        out_shape=[jax.ShapeDtypeStruct((N_CHIPS, p.shape[1], _half(p)), BF16) for p in parts],
        scratch_shapes=[pltpu.SemaphoreType.DMA((n,)), pltpu.SemaphoreType.DMA((n,))],
    )(*parts)


def _sibling_part(ref, w, n_whole, shape, c):
    if w < n_whole:
        return ref
    h = shape[-1] // 2
    return ref.at[:, :, pl.ds((1 - c) * h, h)]


def _pairs_start(parts, all_loss, n_whole):
    n = len(parts)

    def body(*refs):
        p_refs, r_refs, loss_ref = refs[:n], refs[n:2 * n], refs[2 * n]
        send_sems, recv_sems, token = refs[2 * n + 1], refs[2 * n + 2], refs[-1]
        x, y, c, _ = _place()
        for w in range(n):
            h = _half(parts[w])
            pltpu.make_async_remote_copy(
                src_ref=_sibling_part(p_refs[w], w, n_whole, parts[w].shape, c), dst_ref=r_refs[w],
                send_sem=send_sems.at[w], recv_sem=recv_sems.at[w], device_id=(x, y, 1 - c),
                device_id_type=MESH).start()
        me = 4 * x + 2 * y + c
        for t in range(1, N_DEV):
            d = (me + t) % N_DEV
            pltpu.make_async_remote_copy(
                src_ref=loss_ref.at[me], dst_ref=loss_ref.at[me], send_sem=send_sems.at[n + t - 1],
                recv_sem=recv_sems.at[n + t - 1], device_id=(d // 4, (d // 2) % 2, d % 2), device_id_type=MESH).start()
        token[...] = jnp.zeros_like(token)

    lands = [pltpu.HBM(p.shape if w < n_whole else (N_CHIPS, p.shape[1], _half(p)), BF16)
             for w, p in enumerate(parts)]
    nsem = n + N_DEV - 1
    outs = pl.pallas_call(
        body, name="pairs_start",
        out_shape=(pltpu.SemaphoreType.DMA((nsem,)), pltpu.SemaphoreType.DMA((nsem,)),
                   *[pltpu.HBM(p.shape, p.dtype) for p in parts], *lands, pltpu.HBM(all_loss.shape, F32),
                   jax.ShapeDtypeStruct(LOSS_TILE, F32)),
        in_specs=[HBM_SPEC_STRICT] * (2 * n + 1),
        out_specs=(SEM_SPEC, SEM_SPEC, *[HBM_SPEC_STRICT] * (2 * n + 1), VMEM_SPEC),
        input_output_aliases={i: 2 + i for i in range(2 * n + 1)},
        compiler_params=pltpu.CompilerParams(has_side_effects=SPLIT_EFFECT),
    )(*[pltpu.with_memory_space_constraint(p, pltpu.HBM) for p in parts],
      *[pltpu.with_memory_space_constraint(lax.empty(l.shape, BF16), pltpu.HBM) for l in lands],
      pltpu.with_memory_space_constraint(all_loss, pltpu.HBM))
    return outs[0], outs[1], list(outs[2:2 + n]), list(outs[2 + n:2 + 2 * n]), outs[2 + 2 * n], outs[-1]


def _pairs_wait(send_sems, recv_sems, parts, lands, all_loss, after, n_whole):
    n = len(parts)

    def body(*refs):
        p_refs, r_refs, loss_ref = refs[:n], refs[n:2 * n], refs[2 * n]
        send_sems, recv_sems = refs[2 * n + 1], refs[2 * n + 2]
        x, y, c, _ = _place()
        for w in range(n):
            h = _half(parts[w])
            cp = pltpu.make_async_remote_copy(
                src_ref=_sibling_part(p_refs[w], w, n_whole, parts[w].shape, c), dst_ref=r_refs[w],
                send_sem=send_sems.at[w],
                recv_sem=recv_sems.at[w], device_id=(x, y, 1 - c), device_id_type=MESH)
            cp.wait_send()
            cp.wait_recv()
        me = 4 * x + 2 * y + c
        for t in range(1, N_DEV):
            d = (me + N_DEV - t) % N_DEV
            cp = pltpu.make_async_remote_copy(
                src_ref=loss_ref.at[me], dst_ref=loss_ref.at[d], send_sem=send_sems.at[n + t - 1],
                recv_sem=recv_sems.at[n + t - 1], device_id=(d // 4, (d // 2) % 2, d % 2), device_id_type=MESH)
            cp.wait_send()
            cp.wait_recv()

    bufs = (*parts, *lands, all_loss)
    outs = pl.pallas_call(
        body, name="pairs_wait", out_shape=tuple(pltpu.HBM(a.shape, a.dtype) for a in bufs),
        in_specs=[HBM_SPEC_STRICT] * len(bufs) + [SEM_SPEC, SEM_SPEC, HBM_SPEC],
        out_specs=tuple([HBM_SPEC_STRICT] * len(bufs)), input_output_aliases={i: i for i in range(len(bufs))},
        compiler_params=pltpu.CompilerParams(has_side_effects=SPLIT_EFFECT),
    )(*bufs, send_sems, recv_sems, after)
    return list(outs[:n]), list(outs[n:2 * n]), outs[2 * n]


def _add_pair_tiled(p, r):
    rows, h = r.shape[1:]

    def body(p_ref, r_ref, q_ref):
        q_ref[...] = (p_ref[...].astype(F32) + r_ref[...].astype(F32)).astype(BF16)

    spec = pl.BlockSpec((None, rows, h), lambda k: (k, 0, 0))
    return pl.pallas_call(
        body, name="add_pair_w_in", grid=(N_CHIPS,), in_specs=[spec, spec], out_specs=spec,
        out_shape=jax.ShapeDtypeStruct(r.shape, BF16),
    )(p, r)


def _add_pair_small(ps, rs, c, name):
    n = len(ps)

    def body(c_ref, *refs):
        for w in range(n):
            h = _half(ps[w])
            mine = refs[w][:, :, pl.ds(pl.multiple_of(c_ref[0] * h, 128), h)]
            refs[2 * n + w][...] = (mine.astype(F32) + refs[n + w][...].astype(F32)).astype(BF16)

    return pl.pallas_call(
        body, name=name,
        in_specs=[pl.BlockSpec(memory_space=pltpu.SMEM)] + [VMEM_SPEC] * (2 * n), out_specs=[VMEM_SPEC] * n,
        out_shape=[jax.ShapeDtypeStruct(r.shape, BF16) for r in rs],
        compiler_params=pltpu.CompilerParams(vmem_limit_bytes=VMEM_LIMIT),
    )(c, *ps, *rs)


def _exchange_chips(qs):
    n = len(qs)

    def body(*refs):
        q_refs, r_refs = refs[:n], refs[n:2 * n]
        send_sems, recv_sems = refs[2 * n:]
        x, y, c, others = _place()
        me = 2 * x + y
        cps = []
        for w in range(n):
            for j, (px, py) in enumerate(others):
                cps.append(pltpu.make_async_remote_copy(
                    src_ref=q_refs[w].at[2 * px + py], dst_ref=r_refs[w].at[me], send_sem=send_sems.at[3 * w + j],
                    recv_sem=recv_sems.at[3 * w + j], device_id=(px, py, c), device_id_type=MESH))
        for cp in cps:
            cp.start()
        for w in range(n):
            for j, (px, py) in enumerate(others):
                pltpu.make_async_remote_copy(
                    src_ref=q_refs[w].at[me], dst_ref=r_refs[w].at[2 * px + py], send_sem=send_sems.at[3 * w + j],
                    recv_sem=recv_sems.at[3 * w + j], device_id=(px, py, c), device_id_type=MESH).wait_recv()
        for cp in cps:
            cp.wait_send()

    return pl.pallas_call(
        body, name="exchange_chips", in_specs=[HBM_SPEC] * n, out_specs=[HBM_SPEC] * n,
        out_shape=[jax.ShapeDtypeStruct(q.shape, BF16) for q in qs],
        scratch_shapes=[pltpu.SemaphoreType.DMA((3 * n,)), pltpu.SemaphoreType.DMA((3 * n,))],
    )(*qs)


SEM_SPEC = pl.BlockSpec(memory_space=pltpu.SEMAPHORE)
SPLIT_EFFECT = pltpu.SideEffectType.DATAFLOW_SIDE_EFFECTING


def _chips_start(qs, name):
    n = len(qs)

    def body(*refs):
        q_refs, land_refs = refs[:n], refs[n:2 * n]
        send_sems, recv_sems, token = refs[2 * n], refs[2 * n + 1], refs[-1]
        x, y, c, others = _place()
        me = 2 * x + y
        for w in range(n):
            for j, (px, py) in enumerate(others):
                pltpu.make_async_remote_copy(
                    src_ref=q_refs[w].at[2 * px + py], dst_ref=land_refs[w].at[me], send_sem=send_sems.at[3 * w + j],
                    recv_sem=recv_sems.at[3 * w + j], device_id=(px, py, c), device_id_type=MESH).start()
        token[...] = jnp.zeros_like(token)

    hbm = [pltpu.HBM(q.shape, BF16) for q in qs]
    outs = pl.pallas_call(
        body, name=name,
        out_shape=(pltpu.SemaphoreType.DMA((3 * n,)), pltpu.SemaphoreType.DMA((3 * n,)), *hbm, *hbm,
                   jax.ShapeDtypeStruct(LOSS_TILE, F32)),
        in_specs=[HBM_SPEC_STRICT] * (2 * n),
        out_specs=(SEM_SPEC, SEM_SPEC, *[HBM_SPEC_STRICT] * (2 * n), VMEM_SPEC),
        input_output_aliases={i: 2 + i for i in range(2 * n)},
        compiler_params=pltpu.CompilerParams(has_side_effects=SPLIT_EFFECT),
    )(*[pltpu.with_memory_space_constraint(q, pltpu.HBM) for q in qs],
      *[pltpu.with_memory_space_constraint(lax.empty(q.shape, BF16), pltpu.HBM) for q in qs])
    return outs[0], outs[1], outs[2:2 + n], outs[2 + n:2 + 2 * n], outs[-1]


def _chips_wait(send_sems, recv_sems, q_thru, land_thru, after, name):
    n = len(q_thru)

    def body(*refs):
        q_refs, land_refs = refs[:n], refs[n:2 * n]
        send_sems, recv_sems = refs[2 * n], refs[2 * n + 1]
        x, y, c, others = _place()
        me = 2 * x + y
        for w in range(n):
            for j, (px, py) in enumerate(others):
                cp = pltpu.make_async_remote_copy(
                    src_ref=q_refs[w].at[2 * px + py], dst_ref=land_refs[w].at[2 * px + py],
                    send_sem=send_sems.at[3 * w + j], recv_sem=recv_sems.at[3 * w + j], device_id=(px, py, c),
                    device_id_type=MESH)
                cp.wait_send()
                cp.wait_recv()

    outs = pl.pallas_call(
        body, name=name, out_shape=tuple(pltpu.HBM(a.shape, a.dtype) for a in (*q_thru, *land_thru)),
        in_specs=[HBM_SPEC_STRICT] * (2 * n) + [SEM_SPEC, SEM_SPEC, HBM_SPEC],
        out_specs=tuple([HBM_SPEC_STRICT] * (2 * n)), input_output_aliases={i: i for i in range(2 * n)},
        compiler_params=pltpu.CompilerParams(has_side_effects=SPLIT_EFFECT),
    )(*q_thru, *land_thru, send_sems, recv_sems, after)
    return list(outs[:n]), list(outs[n:])


def _sum_chips_tiled(q, r, idx, tile):
    rows, h = r.shape[1:]
    nt = h // tile

    def body(idx_ref, q_ref, r0_ref, r1_ref, r2_ref, g_ref):
        g_ref[...] = (q_ref[...].astype(F32) + r0_ref[...].astype(F32) + r1_ref[...].astype(F32)
                      + r2_ref[...].astype(F32))

    def slab(t):
        return pl.BlockSpec((None, rows, tile), lambda i, idx_ref: (idx_ref[t], 0, i))

    return pl.pallas_call(
        body, name="sum_chips_w_in",
        grid_spec=pltpu.PrefetchScalarGridSpec(
            num_scalar_prefetch=1, grid=(nt,), in_specs=[slab(0), slab(1), slab(2), slab(3)],
            out_specs=pl.BlockSpec((rows, tile), lambda i, idx_ref: (0, idx_ref[4] * nt + i))),
        out_shape=jax.ShapeDtypeStruct((rows, 2 * h), F32),
    )(idx, q, r, r, r)


def _sum_chips_small(qs, rs, idx, all_dtypes):
    n = len(rs)
    n_all = len(all_dtypes)

    def body(idx_ref, *refs):
        c = idx_ref[4]
        for w in range(n):
            q_ref, r_ref, g_ref = refs[w], refs[n + w], refs[2 * n + w]
            acc = q_ref[idx_ref[0]].astype(F32)
            for t in range(1, N_CHIPS):
                acc = acc + r_ref[idx_ref[t]].astype(F32)
            h = rs[w].shape[2]
            mine = pl.ds(pl.multiple_of(c * h, 128), h)
            g_ref[...] = jnp.zeros_like(g_ref)
            if w >= n - n_all:
                g_ref[idx_ref[0], :, mine] = acc.astype(g_ref.dtype)
            else:
                g_ref[:, mine] = acc

    shapes = [jax.ShapeDtypeStruct((r.shape[1], 2 * r.shape[2]), F32) for r in rs[:n - n_all]]
    shapes += [jax.ShapeDtypeStruct((N_CHIPS, r.shape[1], 2 * r.shape[2]), dt)
               for r, dt in zip(rs[n - n_all:], all_dtypes)]
    return pl.pallas_call(
        body, name="sum_chips_small",
        in_specs=[pl.BlockSpec(memory_space=pltpu.SMEM)] + [VMEM_SPEC] * (2 * n), out_specs=[VMEM_SPEC] * n,
        out_shape=shapes, compiler_params=pltpu.CompilerParams(vmem_limit_bytes=VMEM_LIMIT),
    )(idx, *qs, *rs)


def _share(shards, alls):
    n, na = len(shards), len(alls)
    total = n + na

    def body(*refs):
        g_refs, a_refs = refs[total:total + n], refs[total + n:2 * total]
        send_sems, recv_sems = refs[2 * total:]
        x, y, c, others = _place()
        me = 2 * x + y
        sibling = (x, y, 1 - c)

        def cols_of(w, half):
            h = shards[w].shape[1] // 2
            return g_refs[w].at[:, pl.ds(half * h, h)]

        def slab(a, chip, half):
            h = alls[a].shape[2] // 2
            return a_refs[a].at[chip, :, pl.ds(half * h, h)]

        def copy(src, dst, k, to):
            return pltpu.make_async_remote_copy(src_ref=src, dst_ref=dst, send_sem=send_sems.at[k],
                                                recv_sem=recv_sems.at[k], device_id=to, device_id_type=MESH)

        cps = [copy(cols_of(w, c), cols_of(w, c), w, sibling) for w in range(n)]
        for a in range(na):
            base = n + 7 * a
            cps.append(copy(slab(a, me, c), slab(a, me, c), base, sibling))
            for j, (px, py) in enumerate(others):
                cps.append(copy(slab(a, me, c), slab(a, me, c), base + 1 + j, (px, py, c)))
        for cp in cps:
            cp.start()
        fwd = []
        for a in range(na):
            base = n + 7 * a
            for j, (px, py) in enumerate(others):
                chip = 2 * px + py
                copy(slab(a, me, c), slab(a, chip, c), base + 1 + j, (px, py, c)).wait_recv()
                cp = copy(slab(a, chip, c), slab(a, chip, c), base + 4 + j, sibling)
                cp.start()
                fwd.append(cp)
        for a in range(na):
            base = n + 7 * a
            for j, (px, py) in enumerate(others):
                chip = 2 * px + py
                copy(slab(a, chip, c), slab(a, chip, 1 - c), base + 4 + j, sibling).wait_recv()
            copy(slab(a, me, c), slab(a, me, 1 - c), base, sibling).wait_recv()
        for w in range(n):
            copy(cols_of(w, c), cols_of(w, 1 - c), w, sibling).wait_recv()
        for cp in cps + fwd:
            cp.wait_send()

    nsem = n + 7 * na
    return pl.pallas_call(
        body, name="share", in_specs=[HBM_SPEC] * total, out_specs=[HBM_SPEC] * total,
        out_shape=[jax.ShapeDtypeStruct(a.shape, a.dtype) for a in (*shards, *alls)],
        input_output_aliases={i: i for i in range(total)},
        scratch_shapes=[pltpu.SemaphoreType.DMA((nsem,)), pltpu.SemaphoreType.DMA((nsem,))],
    )(*shards, *alls)


def _adamw(w, g, m, v):
    m2 = ADAM_B1 * m + (1.0 - ADAM_B1) * g
    v2 = ADAM_B2 * v + (1.0 - ADAM_B2) * (g * g)
    m_hat = m2 / (1.0 - ADAM_B1 ** ADAM_STEP)
    v_hat = v2 / (1.0 - ADAM_B2 ** ADAM_STEP)
    return -ADAM_LR * (m_hat / (jnp.sqrt(v_hat) + ADAM_EPS) + ADAM_WD * w), m2, v2


def _update_w_in(wt, gt, mt, vt, lat, owner, tile):
    nlat = lat.shape[0] // tile

    def body(owner_ref, w_ref, g_ref, m_ref, v_ref, lat_ref, g2_ref, d_ref, m2_ref, v2_ref):
        row = pl.program_id(0) * tile + lax.broadcasted_iota(jnp.int32, (tile, 1), 0)
        g = jnp.where((row < LAT_COLS) & (owner_ref[0] == 1), lat_ref[...].astype(F32), g_ref[...])
        g2_ref[...] = g
        d_ref[...], m2_ref[...], v2_ref[...] = _adamw(w_ref[...], g, m_ref[...], v_ref[...])

    spec = pl.BlockSpec((tile, wt.shape[1]), lambda i, o: (i, 0))
    return pl.pallas_call(
        body, name="update_w_in",
        grid_spec=pltpu.PrefetchScalarGridSpec(
            num_scalar_prefetch=1, grid=(wt.shape[0] // tile,),
            in_specs=[spec] * 4 + [pl.BlockSpec((tile, wt.shape[1]), lambda i, o: (jnp.minimum(i, nlat - 1), 0))],
            out_specs=[spec] * 4),
        out_shape=[jax.ShapeDtypeStruct(wt.shape, F32)] * 4,
        compiler_params=_params(("parallel",)),
    )(owner, wt, gt, mt, vt, lat)


def _update_small(ws, gs, ms, vs):
    n = len(ws)

    def body(*refs):
        for k in range(n):
            w_ref, g_ref, m_ref, v_ref = refs[k], refs[n + k], refs[2 * n + k], refs[3 * n + k]
            d, m2, v2 = _adamw(w_ref[...], g_ref[...], m_ref[...], v_ref[...])
            refs[4 * n + k][...] = d
            refs[5 * n + k][...] = m2
            refs[6 * n + k][...] = v2

    shapes = [jax.ShapeDtypeStruct(w.shape, F32) for w in ws]
    outs = pl.pallas_call(
        body, name="update_small", in_specs=[VMEM_SPEC] * (4 * n), out_specs=[VMEM_SPEC] * (3 * n),
        out_shape=shapes * 3,
        compiler_params=pltpu.CompilerParams(vmem_limit_bytes=VMEM_LIMIT),
    )(*ws, *gs, *ms, *vs)
    return outs[:n], outs[n:2 * n], outs[2 * n:]


SHARDED = ("w_in", "w_uq", "w_oa", "w_ob", "w_out")
REPLICATED = ("b_in", "g_q", "g_kv", "w_ukv", "sgu_ln_g", "sgu_ln_b", "w_s", "b_s", "ln_g", "ln_b")
ORDER = ("w_in", "b_in", "g_q", "w_uq", "g_kv", "w_ukv", "w_oa", "sgu_ln_g", "sgu_ln_b", "w_s", "b_s", "w_ob", "w_out",
         "ln_g", "ln_b")


def kernel(x, positions, w_in, b_in, g_q, w_uq, g_kv, w_ukv, w_oa, sgu_ln_g, sgu_ln_b, w_s, b_s, w_ob, w_out, ln_g, ln_b, loss_target, m_w_in, m_b_in, m_g_q, m_w_uq, m_g_kv, m_w_ukv, m_w_oa, m_sgu_ln_g, m_sgu_ln_b, m_w_s, m_b_s, m_w_ob, m_w_out, m_ln_g, m_ln_b, v_w_in, v_b_in, v_g_q, v_w_uq, v_g_kv, v_w_ukv, v_w_oa, v_sgu_ln_g, v_sgu_ln_b, v_w_s, v_b_s, v_w_ob, v_w_out, v_ln_g, v_ln_b):
    w = dict(w_in=w_in, b_in=b_in, g_q=g_q, w_uq=w_uq, g_kv=g_kv, w_ukv=w_ukv, w_oa=w_oa, sgu_ln_g=sgu_ln_g,
             sgu_ln_b=sgu_ln_b, w_s=w_s, b_s=b_s, w_ob=w_ob, w_out=w_out, ln_g=ln_g, ln_b=ln_b)
    m = dict(w_in=m_w_in, b_in=m_b_in, g_q=m_g_q, w_uq=m_w_uq, g_kv=m_g_kv, w_ukv=m_w_ukv, w_oa=m_w_oa,
             sgu_ln_g=m_sgu_ln_g, sgu_ln_b=m_sgu_ln_b, w_s=m_w_s, b_s=m_b_s, w_ob=m_w_ob, w_out=m_w_out, ln_g=m_ln_g,
             ln_b=m_ln_b)
    v = dict(w_in=v_w_in, b_in=v_b_in, g_q=v_g_q, w_uq=v_w_uq, g_kv=v_g_kv, w_ukv=v_w_ukv, w_oa=v_w_oa,
             sgu_ln_g=v_sgu_ln_g, sgu_ln_b=v_sgu_ln_b, w_s=v_w_s, b_s=v_b_s, w_ob=v_w_ob, w_out=v_w_out, ln_g=v_ln_g,
             ln_b=v_ln_b)
    w, m, v = ({n: a[0] for n, a in d.items()} for d in (w, m, v))
    c = lax.axis_index("c")

    wt_shard, mt_shard, vt_shard = (jnp.transpose(d["w_in"]) for d in (w, m, v))
    xi, yi = lax.axis_index("x"), lax.axis_index("y")
    me1 = (2 * xi + yi).reshape(1).astype(jnp.int32)
    first = _first_start(*_cast_first(wt_shard, w["w_uq"].reshape(Q_RANK // 4, HEADS * QK_DIM), me1))
    tables = _rope_tables(positions[0])
    bufs = _cast_own([wt_shard, w["w_oa"], w["w_ob"], w["w_out"]], me1, first[4])
    send0, recv0, bufs, token0 = _gather_start(bufs, first[4])
    g_lat, g_uq = _first_forward(*_first_wait(*first[:4], token0, *tables))
    st = _local_attention(x[0], tables, g_lat, w["b_in"], w["g_q"], g_uq.reshape(Q_RANK, HEADS, QK_DIM),
                          w["g_kv"], w["w_ukv"], token0)
    g_in, g_oa, g_ob, g_out = _gather_finish(_gather_wait(send0, recv0, bufs, st["o"]))
    wt = g_in.reshape(IN_W, D_MODEL)

    loss, early, st = _local_head(
        st, x[0], loss_target[0], wt, g_oa, w["sgu_ln_g"], w["sgu_ln_b"], w["w_s"], w["b_s"], g_ob,
        g_out.reshape(D_MODEL, D_MODEL), w["ln_g"], w["ln_b"])

    c1 = c.reshape(1).astype(jnp.int32)
    idx = jnp.stack([2 * xi + yi, 2 * (1 - xi) + yi, 2 * xi + (1 - yi), 2 * (1 - xi) + (1 - yi), c]).astype(jnp.int32)
    slabs = lambda a: a.reshape(N_CHIPS, IN_W // N_CHIPS, D_MODEL // 2)
    theirs = slabs(_dwt_early(st["dhmt"], st["dhgt"], st["xb2"], 1 - c1, c1, "dwt_theirs"))
    parts1 = [theirs, early["w_oa"].astype(BF16), early["w_ob"].astype(BF16),
              early["w_out"].reshape(N_CHIPS, SLAB_W, D_MODEL).astype(BF16)]
    my_loss = lax.dynamic_update_slice(jnp.zeros((N_DEV,) + LOSS_TILE, F32), jnp.broadcast_to(loss, (1,) + LOSS_TILE),
                                       (4 * xi + 2 * yi + c, 0, 0))
    sems0 = _pairs_start(parts1, my_loss, 1)
    mine = slabs(_dwt_early(st["dhmt"], st["dhgt"], st["xb2"], c1, sems0[5], "dwt_mine"))
    parts1, recv1, all_loss = _pairs_wait(*sems0[:5], mine, 1)
    pairs1 = [_add_pair_tiled(mine, recv1[0]), *_add_pair_small(parts1[1:], recv1[1:], c1, "add_pair_early")]
    sems1 = _chips_start(pairs1, "chips_start_early")
    st["delta"] = st["delta"] + sems1[4][0, 0]
    dq, dk, dv = _local_attn_bwd(st)
    dhl, late = _local_tail(st, dq, dk, dv, dv)

    grads = {**early, **late}
    rep = jnp.concatenate([_rows8(grads[n]) for n in REPLICATED], axis=0)
    rep = jnp.pad(rep, ((0, N_CHIPS * REP_ROWS - rep.shape[0]), (0, 0))).reshape(N_CHIPS, REP_ROWS, D_MODEL)
    parts2 = [late["w_uq"].reshape(N_CHIPS, Q_RANK // N_CHIPS, HEADS * QK_DIM).astype(BF16), rep.astype(BF16),
              late["w_lat"].reshape(N_CHIPS, LAT_ROWS_PAD // N_CHIPS, D_MODEL)]
    pairs2 = _add_pair_small(parts2, _exchange_pairs(parts2, "exchange_pairs_late"), c1, "add_pair_late")
    sems2 = _chips_start(pairs2, "chips_start_late")
    dx = _dx(st["dr"], st["dhg"], st["dhm"], dhl, st["wt"], sems2[4])
    pairs2, landed2 = _chips_wait(*sems2[:4], dx, "chips_wait_late")
    pairs1, landed1 = _chips_wait(*sems1[:4], landed2[0], "chips_wait_early")
    sums = [_sum_chips_tiled(pairs1[0], landed1[0], idx, 128),
            *_sum_chips_small([*pairs1[1:], *pairs2], [*landed1[1:], *landed2], idx, (F32, BF16))]
    *shards, g_rep, g_lat = _share(sums[:-2], sums[-2:])
    loss = jnp.sum(all_loss[:, 0, 0])

    red = {n: s.reshape(w[n].shape) for n, s in zip(("w_oa", "w_ob", "w_out", "w_uq"), shards[1:])}
    g_rep = g_rep.reshape(N_CHIPS * REP_ROWS, D_MODEL)
    off = 0
    for n in REPLICATED:
        rows = _rows8(w[n]).shape[0]
        red[n] = g_rep[off:off + rows].reshape(-1)[:w[n].size].reshape(w[n].shape)
        off += rows
    owner = (2 * xi + yi == 0).astype(jnp.int32).reshape(1)
    gt, dt, mt, vt2 = _update_w_in(wt_shard, shards[0], mt_shard, vt_shard,
                                   g_lat.reshape(LAT_ROWS_PAD, D_MODEL).astype(F32), owner, 232)
    red["w_in"] = jnp.transpose(gt)
    small = [n for n in ORDER if n != "w_in"]
    as2d = lambda a: a.reshape(-1, a.shape[-1])
    ds, ms, vs = _update_small([as2d(w[n]) for n in small], [as2d(red[n]) for n in small],
                               [as2d(m[n]) for n in small], [as2d(v[n]) for n in small])
    delta, new_m, new_v = {"w_in": jnp.transpose(dt)}, {"w_in": jnp.transpose(mt)}, {"w_in": jnp.transpose(vt2)}
    for i, n in enumerate(small):
        delta[n], new_m[n], new_v[n] = (a[i].reshape(w[n].shape) for a in (ds, ms, vs))

    lead = lambda a: a[None]
    return (loss, dx[None], *[lead(red[n]) for n in ORDER], *[lead(delta[n]) for n in ORDER],
            *[lead(new_m[n]) for n in ORDER], *[lead(new_v[n]) for n in ORDER])
```

```python
import functools
import math

import jax
import jax.numpy as jnp
from jax import lax
from jax.experimental import pallas as pl
from jax.experimental.pallas import tpu as pltpu

F32 = jnp.float32
BF16 = jnp.bfloat16

D_MODEL = 1024
HEADS = 8
Q_RANK = 384
KV_RANK = 128
NOPE = 64
ROPE = 32
V_DIM = 64
QK_DIM = NOPE + ROPE
HEAD_PAD = 128
MLA_W = HEADS * V_DIM
SGU_W = 512
GROUPS = 8
CHUNK = 128
IN_W = 4640
RMS_EPS = 1e-6
LN_EPS = 1e-5
ALPHA = 2.0 ** 0.25
ROPE_THETA = 10000.0
SCALE = QK_DIM ** -0.5

GATE_W = 2 * D_MODEL
MID_W = 4 * SGU_W
LAT_W = Q_RANK + KV_RANK + HEAD_PAD
LAT_COLS = Q_RANK + KV_RANK + ROPE
ROW_GATE = LAT_COLS + MID_W
LAT_ROWS_PAD = 704
N_SLABS = 4
SLAB_W = D_MODEL // N_SLABS

ROW_TILE = 256
MATMUL_ROW_TILE = 512
MID_ROW_TILE = 256
ATT_TQ = 2048
ATT_TK = 256
ATT_BWD_TK = 256
LOG2E = 1.4426950408889634
LN2 = 0.6931471805599453
Q_SCALE = SCALE * LOG2E
VMEM_LIMIT = 56 * 1024 * 1024

ADAM_LR = 0.001
ADAM_B1 = 0.9
ADAM_B2 = 0.999
ADAM_EPS = 1e-08
ADAM_WD = 0.01
ADAM_STEP = 10


def _dot(a, b):
    return jnp.dot(a, b, preferred_element_type=F32)


def _dot_nt(a, b):
    return lax.dot_general(a, b, (((1,), (1,)), ((), ())), preferred_element_type=F32)


def _dot_tn(a, b):
    return lax.dot_general(a, b, (((0,), (0,)), ((), ())), preferred_element_type=F32)


def _sigmoid(z):
    return 0.5 * jnp.tanh(0.5 * z) + 0.5


_GELU_C = math.sqrt(2.0 / math.pi)


def _gelu_and_grad(x):
    x2 = x * x
    t = jnp.tanh(_GELU_C * (x + 0.044715 * x * x2))
    g = 0.5 * x * (1.0 + t)
    dg = 0.5 * (1.0 + t) + 0.5 * x * (1.0 - t * t) * (_GELU_C * (1.0 + 3.0 * 0.044715 * x2))
    return g, dg


def _silu_and_grad(z):
    s = _sigmoid(z)
    return z * s, s * (1.0 + z * (1.0 - s))


def _rope(xb, c, sl, sh):
    return xb * c + pltpu.roll(xb, 112, 1) * sl + pltpu.roll(xb, 16, 1) * sh


def _rope_t(dy, c, sl, sh):
    return dy * c + pltpu.roll(dy * sl, 16, 1) + pltpu.roll(dy * sh, 112, 1)


def _params(sem=("arbitrary",)):
    return pltpu.CompilerParams(dimension_semantics=sem, vmem_limit_bytes=VMEM_LIMIT)


def _row_spec(tile, width):
    return pl.BlockSpec((tile, width), lambda i: (i, 0))


def _full_spec(shape):
    nd = len(shape)
    return pl.BlockSpec(shape, lambda i: (0,) * nd)


def _kpe_rows(wt_ref):
    z = lambda n: jnp.zeros((n, D_MODEL), BF16)
    return jnp.concatenate([z(NOPE), wt_ref[Q_RANK + KV_RANK:LAT_COLS, :], z(HEAD_PAD - QK_DIM)], axis=0)


def _fwd_rest(xb, wt, b_g, b_m):
    s = xb.shape[0]
    ts = MATMUL_ROW_TILE

    def body(xb_ref, wt_ref, bg_ref, bm_ref, hg_ref, hm_ref):
        xb_ = xb_ref[...]
        hg_ref[...] = _dot_nt(xb_, wt_ref[ROW_GATE:IN_W, :]) + bg_ref[...]
        hm_ref[...] = _dot_nt(xb_, wt_ref[LAT_COLS:ROW_GATE, :]) + bm_ref[...]

    return pl.pallas_call(
        body, name="fwd_rest", grid=(s // ts,),
        in_specs=[_row_spec(ts, D_MODEL), _full_spec(wt.shape), _full_spec(b_g.shape), _full_spec(b_m.shape)],
        out_specs=[_row_spec(ts, GATE_W), _row_spec(ts, MID_W)],
        out_shape=[jax.ShapeDtypeStruct((s, GATE_W), F32), jax.ShapeDtypeStruct((s, MID_W), F32)],
        compiler_params=_params(),
    )(xb, wt, b_g, b_m)


def _fwd_lat(x, wlat, b_l, g_q, wuq, g_kv, wk, wv, rc, rsl, rsh, after):
    s = x.shape[0]
    ts = ROW_TILE

    def body(x_ref, wt_ref, bl_ref, gq_ref, wuq_ref, gkv_ref, wk_ref, wv_ref, rc_ref, rsl_ref,
             rsh_ref, after_ref, hl_ref, q_ref, k_ref, v_ref, xb_ref, qt_ref, kt_ref, vt_ref, xb2_ref):
        xb = x_ref[...].astype(BF16)
        xb_ref[...] = xb
        xb2_ref[0] = xb[:, :D_MODEL // 2]
        xb2_ref[1] = xb[:, D_MODEL // 2:]
        hl = jnp.concatenate([_dot_nt(xb, wt_ref[0:Q_RANK + KV_RANK, :]), _dot_nt(xb, _kpe_rows(wt_ref))],
                             axis=1) + bl_ref[...]
        hl_ref[...] = hl
        c, sl, sh = rc_ref[...], rsl_ref[...], rsh_ref[...]
        cq = hl[:, :Q_RANK]
        cqn = cq * lax.rsqrt(jnp.mean(cq * cq, axis=-1, keepdims=True) + RMS_EPS) * gq_ref[...]
        q = _dot(cqn.astype(BF16), wuq_ref[...])
        ckv = hl[:, Q_RANK:Q_RANK + KV_RANK]
        ckvn = (ckv * lax.rsqrt(jnp.mean(ckv * ckv, axis=-1, keepdims=True) + RMS_EPS) * gkv_ref[...]).astype(BF16)
        k = _dot(ckvn, wk_ref[...])
        vb = _dot(ckvn, wv_ref[...]).astype(BF16)
        v_ref[...] = vb
        vt_ref[...] = vb.T
        kpe = _rope(hl[:, Q_RANK + KV_RANK:], c, sl, sh)
        for hd in range(HEADS):
            lanes = slice(hd * HEAD_PAD, (hd + 1) * HEAD_PAD)
            qb = (_rope(q[:, lanes], c, sl, sh) * Q_SCALE).astype(BF16)
            kb = (k[:, lanes] + kpe).astype(BF16)
            q_ref[:, lanes] = qb
            k_ref[:, lanes] = kb
            qt_ref[lanes, :] = qb.T
            kt_ref[lanes, :] = kb.T

    qk_w = HEADS * HEAD_PAD
    col_spec = lambda rows: pl.BlockSpec((rows, ts), lambda i: (0, i))
    return pl.pallas_call(
        body, name="fwd_lat", grid=(s // ts,),
        in_specs=[_row_spec(ts, D_MODEL), _full_spec(wlat.shape),
                  _full_spec(b_l.shape), _full_spec(g_q.shape),
                  _full_spec(wuq.shape), _full_spec(g_kv.shape), _full_spec(wk.shape), _full_spec(wv.shape),
                  _row_spec(ts, HEAD_PAD), _row_spec(ts, HEAD_PAD), _row_spec(ts, HEAD_PAD),
                  pl.BlockSpec(memory_space=pl.ANY)],
        out_specs=[_row_spec(ts, LAT_W), _row_spec(ts, qk_w),
                   _row_spec(ts, qk_w), _row_spec(ts, MLA_W), _row_spec(ts, D_MODEL), col_spec(qk_w), col_spec(qk_w),
                   col_spec(MLA_W), pl.BlockSpec((2, ts, D_MODEL // 2), lambda i: (0, i, 0))],
        out_shape=[jax.ShapeDtypeStruct((s, LAT_W), F32), jax.ShapeDtypeStruct((s, qk_w), BF16),
                   jax.ShapeDtypeStruct((s, qk_w), BF16), jax.ShapeDtypeStruct((s, MLA_W), BF16),
                   jax.ShapeDtypeStruct((s, D_MODEL), BF16), jax.ShapeDtypeStruct((qk_w, s), BF16),
                   jax.ShapeDtypeStruct((qk_w, s), BF16), jax.ShapeDtypeStruct((MLA_W, s), BF16),
                   jax.ShapeDtypeStruct((2, s, D_MODEL // 2), BF16)],
        compiler_params=_params(),
    )(x, wlat, b_l, g_q, wuq, g_kv, wk, wv, rc, rsl, rsh, after)


def _attn_fwd(qt, k, vt):
    s = k.shape[0]
    tq, tk = ATT_TQ, ATT_TK
    r = tq // tk
    pairs = HEADS // 2

    def body(qt_ref, k_ref, vt_ref, o_ref, lse_ref):
        i = pl.program_id(1)
        qts = [qt_ref[hh * HEAD_PAD:(hh + 1) * HEAD_PAD, :] for hh in range(2)]

        def scores(j, lo):
            koff = pl.multiple_of(j * tk, tk)
            return tuple(_dot(k_ref[pl.ds(koff, tk), hh * HEAD_PAD:(hh + 1) * HEAD_PAD], qts[hh][:, lo:])
                         for hh in range(2))

        def weighted(j, ps):
            koff = pl.multiple_of(j * tk, tk)
            return tuple(_dot(vt_ref[hh * V_DIM:(hh + 1) * V_DIM, pl.ds(koff, tk)], ps[hh]) for hh in range(2))

        def from_lane(full, lo, part):
            return part if lo == 0 else jnp.concatenate([full[:, :lo], part], axis=1)

        def step(j, carry, diag, last):
            st, ps, stats = carry
            lo = 0 if diag is None else diag * tk
            lo_prev = 0 if not diag else (diag - 1) * tk
            st_next = None if last else scores(j + 1, 0 if diag is None else lo + tk)
            pvs = weighted(jnp.maximum(j - 1, 0), ps)
            new_ps, new_stats = [], []
            for hh in range(2):
                m, l, acc = stats[hh]
                s_ = st[hh]
                if diag is not None:
                    krow = lax.broadcasted_iota(jnp.int32, s_.shape, 0)
                    qcol = lax.broadcasted_iota(jnp.int32, s_.shape, 1)
                    s_ = jnp.where(krow <= qcol, s_, -jnp.inf)
                acc = from_lane(acc, lo_prev, acc[:, lo_prev:] + pvs[hh])
                m_old = m[:, lo:]
                m_new = jnp.maximum(m_old, jnp.max(s_, axis=0, keepdims=True))
                a = jnp.exp2(m_old - m_new)
                p = jnp.exp2(s_ - m_new)
                new_stats.append((from_lane(m, lo, m_new),
                                  from_lane(l, lo, a * l[:, lo:] + jnp.sum(p, axis=0, keepdims=True)),
                                  from_lane(acc, lo, a * acc[:, lo:])))
                new_ps.append(p.astype(BF16))
            return st_next, tuple(new_ps), tuple(new_stats)

        one = (jnp.full((1, tq), -jnp.inf, F32), jnp.zeros((1, tq), F32), jnp.zeros((V_DIM, tq), F32))
        zero_p = jnp.zeros((tk, tq), BF16)
        nfull = i * r
        carry = lax.fori_loop(0, nfull, functools.partial(step, diag=None, last=False),
                              (scores(0, 0), (zero_p, zero_p), (one, one)))
        for d in range(r):
            carry = step(nfull + d, carry, d, d == r - 1)
        _, ps, stats = carry
        pvs = weighted(nfull + r - 1, ps)
        lo = (r - 1) * tk
        ot = jnp.concatenate([from_lane(stats[hh][2], lo, stats[hh][2][:, lo:] + pvs[hh]) / stats[hh][1]
                              for hh in range(2)], axis=0)
        o_ref[...] = ot.T
        lse = [stats[hh][0] + jnp.log(stats[hh][1]) * LOG2E for hh in range(2)]
        lse_ref[...] = jnp.concatenate(lse + [jnp.zeros((6, tq), F32)], axis=0)

    return pl.pallas_call(
        body, name="attn_fwd", grid=(pairs, s // tq),
        in_specs=[pl.BlockSpec((2 * HEAD_PAD, tq), lambda p, i: (p, i)),
                  pl.BlockSpec((s, 2 * HEAD_PAD), lambda p, i: (0, p)),
                  pl.BlockSpec((2 * V_DIM, s), lambda p, i: (p, 0))],
        out_specs=[pl.BlockSpec((tq, 2 * V_DIM), lambda p, i: (i, p)),
                   pl.BlockSpec((None, 8, tq), lambda p, i: (p, 0, i))],
        out_shape=[jax.ShapeDtypeStruct((s, MLA_W), F32), jax.ShapeDtypeStruct((pairs, 8, s), F32)],
        compiler_params=_params(("arbitrary", "arbitrary")),
    )(qt, k, vt)


def _attn_bwd(q, qt, k, kt, v, do, dot, lse, delta):
    s = k.shape[0]
    tk = ATT_BWD_TK
    nk = s // tk
    pairs = HEADS // 2

    def body(q_ref, qt_ref, k_ref, kt_ref, v_ref, do_ref, dot_ref, lse_ref, dl_ref, dqt_ref, dk_ref, dv_ref):
        krow = lax.broadcasted_iota(jnp.int32, (tk, tk), 0)
        qcol = lax.broadcasted_iota(jnp.int32, (tk, tk), 1)
        lane = lax.broadcasted_iota(jnp.int32, (tk, 2 * V_DIM), 1)
        drow = lax.broadcasted_iota(jnp.int32, (2 * V_DIM, s), 0)
        dotb = dot_ref[...]
        dots = [jnp.where((drow < V_DIM) if hh == 0 else (drow >= V_DIM), dotb, jnp.zeros_like(dotb))
                for hh in range(2)]
        for j in range(nk):
            lo = j * tk
            vb = v_ref[lo:lo + tk, :]
            dob = do_ref[lo:, :]
            dvs = []
            for hh in range(2):
                rows = slice(hh * HEAD_PAD, (hh + 1) * HEAD_PAD)
                st = _dot(k_ref[lo:lo + tk, rows], qt_ref[rows, lo:])
                diag = jnp.where(krow <= qcol, st[:, :tk], -jnp.inf)
                st = diag if j == nk - 1 else jnp.concatenate([diag, st[:, tk:]], axis=1)
                p = jnp.exp2(st - lse_ref[hh:hh + 1, lo:])
                dpt = _dot(vb, dots[hh][:, lo:])
                dst = (p * (dpt - dl_ref[hh:hh + 1, lo:])).astype(BF16)
                dvs.append(_dot(p.astype(BF16), dob))
                dk_ref[lo:lo + tk, rows] = _dot(dst, q_ref[lo:, rows]) * LN2
                dqt = _dot(kt_ref[rows, lo:lo + tk], dst)
                if j == 0:
                    dqt_ref[rows, :] = dqt
                else:
                    dqt_ref[rows, lo:] += dqt
            dv_ref[lo:lo + tk, :] = jnp.where(lane < V_DIM, dvs[0], dvs[1])
        dqt_ref[...] = dqt_ref[...] * SCALE

    pair_rows = lambda w: pl.BlockSpec((s, w), lambda p: (0, p))
    pair_cols = lambda w: pl.BlockSpec((w, s), lambda p: (p, 0))
    stats = pl.BlockSpec((None, 8, s), lambda p: (p, 0, 0))
    return pl.pallas_call(
        body, name="attn_bwd", grid=(pairs,),
        in_specs=[pair_rows(2 * HEAD_PAD), pair_cols(2 * HEAD_PAD), pair_rows(2 * HEAD_PAD), pair_cols(2 * HEAD_PAD),
                  pair_rows(2 * V_DIM), pair_rows(2 * V_DIM), pair_cols(2 * V_DIM), stats, stats],
        out_specs=[pair_cols(2 * HEAD_PAD), pair_rows(2 * HEAD_PAD), pair_rows(2 * V_DIM)],
        out_shape=[jax.ShapeDtypeStruct((HEADS * HEAD_PAD, s), F32), jax.ShapeDtypeStruct((s, HEADS * HEAD_PAD), F32),
                   jax.ShapeDtypeStruct((s, MLA_W), F32)],
        compiler_params=_params(("arbitrary",)),
    )(q, qt, k, kt, v, do, dot, lse, delta)


def _split3(a):
    hi = a.astype(BF16)
    r1 = a - hi.astype(F32)
    mid = r1.astype(BF16)
    lo = (r1 - mid.astype(F32)).astype(BF16)
    return hi, mid, lo


def _mid(x, tgt, o, hm, hg, woa, wob, wout, ln_g, ln_b, sg_g, sg_b, w_s, bsb):
    s = x.shape[0]
    ts = MID_ROW_TILE
    nsteps = s // ts
    nch = ts // CHUNK
    npair = GROUPS // 2

    def body(x_ref, t_ref, o_ref, hm_ref, hg_ref, woa_ref, wob_ref, wout_ref, lng_ref, lnb_ref, sgg_ref, sgb_ref,
             ws_ref, bsb_ref,
             dr_ref, dhg_ref, dhm_ref, do_ref, dot_ref, dl_ref, dhgt_ref, dhmt_ref,
             dwout_ref, dwoa_ref, dwob_ref, dws_ref, dbs_ref, dlng_ref, dlnb_ref, dsgg_ref, dsgb_ref, loss_ref,
             dbg_ref, dbm_ref, dbacc_ref, awout_ref, awoa_ref, awob_ref):
        i = pl.program_id(0)

        @pl.when(i == 0)
        def _():
            for r in (awout_ref, awoa_ref, awob_ref, dws_ref, dlng_ref, dlnb_ref, dsgg_ref, dsgb_ref, loss_ref,
                      dbg_ref, dbm_ref, dbacc_ref):
                r[...] = jnp.zeros_like(r)

        def emit(ref, tref, bref, lo, val):
            vb = val.astype(BF16)
            n = val.shape[1]
            ref[:, lo:lo + n] = vb
            tref[lo:lo + n, :] = vb.T
            bref[:, lo:lo + n] += jnp.sum(val, axis=0, keepdims=True)

        lane = lax.broadcasted_iota(jnp.int32, (CHUNK, CHUNK), 1)
        left = lane < V_DIM
        tril = lax.broadcasted_iota(jnp.int32, (CHUNK, CHUNK), 0) >= lane
        ms = [jnp.where(tril, ws_ref[g], 0.0).astype(BF16) for g in range(GROUPS)]

        z_a = hm_ref[:, 0:SGU_W]
        u = hm_ref[:, SGU_W:2 * SGU_W]
        v = hm_ref[:, 2 * SGU_W:3 * SGU_W]
        z_b = hm_ref[:, 3 * SGU_W:4 * SGU_W]
        o = o_ref[...]
        sa, dsa = _silu_and_grad(z_a)
        y_a = (o * sa).astype(BF16)
        gu, dgu = _gelu_and_grad(u)
        gv, dgv = _gelu_and_grad(v)
        mu = jnp.mean(gv, axis=-1, keepdims=True)
        vc = gv - mu
        rstd_v = lax.rsqrt(jnp.mean(vc * vc, axis=-1, keepdims=True) + LN_EPS)
        vhat = vc * rstd_v
        vn = (vhat * sgg_ref[...] + sgb_ref[...]).astype(BF16)
        rows = []
        for c in range(nch):
            blocks = []
            for p in range(npair):
                blk = vn[c * CHUNK:(c + 1) * CHUNK, p * CHUNK:(p + 1) * CHUNK]
                blocks.append(jnp.where(left, _dot(ms[2 * p], blk), _dot(ms[2 * p + 1], blk)))
            rows.append(jnp.concatenate(blocks, axis=1) + bsb_ref[...])
        mixed = jnp.concatenate(rows, axis=0)
        sgu = gu * mixed
        sb, dsb = _silu_and_grad(z_b)
        y_b = (sgu * sb).astype(BF16)
        pa = jnp.concatenate([_dot(y_a, woa_ref[k]) for k in range(N_SLABS)], axis=1)
        pb = jnp.concatenate([_dot(y_b, wob_ref[k]) for k in range(N_SLABS)], axis=1)
        sga = _sigmoid(hg_ref[:, :D_MODEL])
        sgb = _sigmoid(hg_ref[:, D_MODEL:])
        m2 = (sga * pa + sgb * pb).astype(BF16)
        r = ALPHA * x_ref[...] + _dot(m2, wout_ref[...])
        rmu = jnp.mean(r, axis=-1, keepdims=True)
        rc = r - rmu
        rstd = lax.rsqrt(jnp.mean(rc * rc, axis=-1, keepdims=True) + LN_EPS)
        xhat = rc * rstd
        y = xhat * lng_ref[...] + lnb_ref[...]
        err = y - t_ref[...]
        loss_ref[...] += jnp.full(loss_ref.shape, 0.5 / D_MODEL, F32) * jnp.sum(err * err)

        dy = err * (1.0 / D_MODEL)
        dlng_ref[...] += jnp.sum(dy * xhat, axis=0, keepdims=True)
        dlnb_ref[...] += jnp.sum(dy, axis=0, keepdims=True)
        dxh = dy * lng_ref[...]
        dr = rstd * (dxh - jnp.mean(dxh, axis=-1, keepdims=True) - xhat * jnp.mean(dxh * xhat, axis=-1, keepdims=True))
        dr_ref[...] = dr
        drb = dr.astype(BF16)
        awout_ref[...] += _dot_tn(m2, drb)
        dm2 = _dot_nt(drb, wout_ref[...])
        emit(dhg_ref, dhgt_ref, dbg_ref, 0, dm2 * pa * sga * (1.0 - sga))
        emit(dhg_ref, dhgt_ref, dbg_ref, D_MODEL, dm2 * pb * sgb * (1.0 - sgb))
        dpa = (dm2 * sga).astype(BF16)
        dpb = (dm2 * sgb).astype(BF16)
        dy_a = jnp.zeros((ts, MLA_W), F32)
        dy_b = jnp.zeros((ts, SGU_W), F32)
        y_at, y_bt = y_a.T, y_b.T
        for k in range(N_SLABS):
            cols = slice(k * SLAB_W, (k + 1) * SLAB_W)
            awoa_ref[k] += _dot(y_at, dpa[:, cols])
            awob_ref[k] += _dot(y_bt, dpb[:, cols])
            dy_a = dy_a + _dot_nt(dpa[:, cols], woa_ref[k])
            dy_b = dy_b + _dot_nt(dpb[:, cols], wob_ref[k])
        dob = (dy_a * sa).astype(BF16)
        do_ref[...] = dob
        dot_ref[...] = dob.T
        head = (lax.broadcasted_iota(jnp.int32, (HEADS, MLA_W), 1) // V_DIM
                == lax.broadcasted_iota(jnp.int32, (HEADS, MLA_W), 0)).astype(BF16)
        dl_ref[...] = sum(_dot_nt(head, term) for term in _split3(dob.astype(F32) * o))
        emit(dhm_ref, dhmt_ref, dbm_ref, 0, dy_a * o * dsa)
        dsg = dy_b * sb
        emit(dhm_ref, dhmt_ref, dbm_ref, 3 * SGU_W, dy_b * sgu * dsb)
        emit(dhm_ref, dhmt_ref, dbm_ref, SGU_W, dsg * mixed * dgu)
        dmixed = dsg * gu
        dvn_rows = []
        dbs_sum = jnp.zeros((CHUNK, SGU_W), F32)
        for c in range(nch):
            dm_c = dmixed[c * CHUNK:(c + 1) * CHUNK, :]
            dbs_sum = dbs_sum + dm_c
            blocks = []
            for p in range(npair):
                dmb = dm_c[:, p * CHUNK:(p + 1) * CHUNK].astype(BF16)
                blk = vn[c * CHUNK:(c + 1) * CHUNK, p * CHUNK:(p + 1) * CHUNK]
                blocks.append(jnp.where(left, _dot_tn(ms[2 * p], dmb), _dot_tn(ms[2 * p + 1], dmb)))
                zero = jnp.zeros_like(dmb)
                dws_ref[2 * p] += jnp.where(tril, _dot_nt(jnp.where(left, dmb, zero), blk), 0.0)
                dws_ref[2 * p + 1] += jnp.where(tril, _dot_nt(jnp.where(left, zero, dmb), blk), 0.0)
            dvn_rows.append(jnp.concatenate(blocks, axis=1))
        dbacc_ref[...] += dbs_sum
        dvn = jnp.concatenate(dvn_rows, axis=0)
        dsgg_ref[...] += jnp.sum(dvn * vhat, axis=0, keepdims=True)
        dsgb_ref[...] += jnp.sum(dvn, axis=0, keepdims=True)
        dvh = dvn * sgg_ref[...]
        dgv_in = rstd_v * (dvh - jnp.mean(dvh, axis=-1, keepdims=True)
                           - vhat * jnp.mean(dvh * vhat, axis=-1, keepdims=True))
        emit(dhm_ref, dhmt_ref, dbm_ref, 2 * SGU_W, dgv_in * dgv)

        @pl.when(i == nsteps - 1)
        def _():
            dwout_ref[...] = awout_ref[...].astype(BF16)
            dwoa_ref[...] = awoa_ref[...].astype(BF16)
            dwob_ref[...] = awob_ref[...].astype(BF16)
            grp = (lax.broadcasted_iota(jnp.int32, (SGU_W, CHUNK), 0) // V_DIM
                   == lax.broadcasted_iota(jnp.int32, (SGU_W, CHUNK), 1)).astype(BF16)
            hi, mid, lo = _split3(dbacc_ref[...])
            dbs_ref[...] = _dot(hi, grp) + _dot(mid, grp) + _dot(lo, grp)

    acc_shapes = [(D_MODEL, D_MODEL), woa.shape, wob.shape, (GROUPS, CHUNK, CHUNK), (CHUNK, CHUNK),
                  (1, D_MODEL), (1, D_MODEL), (1, SGU_W), (1, SGU_W), (1, 128), (1, GATE_W), (1, MID_W)]
    col_spec = lambda rows: pl.BlockSpec((rows, ts), lambda i: (0, i))
    return pl.pallas_call(
        body, name="mid", grid=(nsteps,),
        in_specs=[_row_spec(ts, D_MODEL), _row_spec(ts, D_MODEL), _row_spec(ts, MLA_W), _row_spec(ts, MID_W),
                  _row_spec(ts, GATE_W), _full_spec(woa.shape), _full_spec(wob.shape), _full_spec(wout.shape),
                  _full_spec(ln_g.shape), _full_spec(ln_b.shape), _full_spec(sg_g.shape), _full_spec(sg_b.shape),
                  _full_spec(w_s.shape), _full_spec(bsb.shape)],
        out_specs=[_row_spec(ts, D_MODEL), _row_spec(ts, GATE_W), _row_spec(ts, MID_W), _row_spec(ts, MLA_W),
                   col_spec(MLA_W), col_spec(HEADS), col_spec(GATE_W), col_spec(MID_W)]
        + [_full_spec(sh) for sh in acc_shapes],
        out_shape=[jax.ShapeDtypeStruct((s, D_MODEL), F32), jax.ShapeDtypeStruct((s, GATE_W), BF16),
                   jax.ShapeDtypeStruct((s, MID_W), BF16), jax.ShapeDtypeStruct((s, MLA_W), BF16),
                   jax.ShapeDtypeStruct((MLA_W, s), BF16), jax.ShapeDtypeStruct((HEADS, s), F32),
                   jax.ShapeDtypeStruct((GATE_W, s), BF16), jax.ShapeDtypeStruct((MID_W, s), BF16)]
        + [jax.ShapeDtypeStruct(sh, BF16 if n < 3 else F32) for n, sh in enumerate(acc_shapes)],
        scratch_shapes=[pltpu.VMEM((CHUNK, SGU_W), F32)] + [pltpu.VMEM(sh, F32) for sh in acc_shapes[:3]],
        compiler_params=_params(),
    )(x, tgt, o, hm, hg, woa, wob, wout, ln_g, ln_b, sg_g, sg_b, w_s, bsb)


def _lat_bwd(dq, dk, dv, hl, rc, rsl, rsh, g_q, g_kv, wuq, wk, wv, after):
    s = dk.shape[0]
    ts = ROW_TILE
    qk_w = HEADS * HEAD_PAD

    def body(dq_ref, dk_ref, dv_ref, hl_ref, rc_ref, rsl_ref, rsh_ref, gq_ref, gkv_ref, wuq_ref, wk_ref, wv_ref,
             after_ref, dhl_ref, dhlt_ref, dwuq_ref, dwk_ref, dwv_ref, dgq_ref, dgkv_ref, dbl_ref):
        i = pl.program_id(0)

        @pl.when(i == 0)
        def _():
            for r in (dwuq_ref, dwk_ref, dwv_ref, dgq_ref, dgkv_ref, dbl_ref):
                r[...] = jnp.zeros_like(r)

        def emit(lo, val):
            vb = val.astype(BF16)
            n = val.shape[1]
            dhl_ref[:, lo:lo + n] = vb
            dhlt_ref[lo:lo + n, :] = vb.T
            dbl_ref[:, lo:lo + n] += jnp.sum(val, axis=0, keepdims=True)

        c, sl, sh = rc_ref[...], rsl_ref[...], rsh_ref[...]
        lane = lax.broadcasted_iota(jnp.int32, (ts, HEAD_PAD), 1)
        pe = (lane >= NOPE) & (lane < QK_DIM)
        dkpe = jnp.zeros((ts, HEAD_PAD), F32)
        dqu = []
        for hd in range(HEADS):
            lanes = slice(hd * HEAD_PAD, (hd + 1) * HEAD_PAD)
            dqu.append(_rope_t(dq_ref[lanes, :].T, c, sl, sh).astype(BF16))
            dkpe = dkpe + dk_ref[:, lanes]
        dqu = jnp.concatenate(dqu, axis=1)
        dkpe = _rope_t(jnp.where(pe, dkpe, 0.0), c, sl, sh)

        cq = hl_ref[:, :Q_RANK]
        rq = lax.rsqrt(jnp.mean(cq * cq, axis=-1, keepdims=True) + RMS_EPS)
        cqh = cq * rq
        cqn = (cqh * gq_ref[...]).astype(BF16)
        dwuq_ref[...] += _dot_tn(cqn, dqu)
        dcqn = _dot_nt(dqu, wuq_ref[...])
        dgq_ref[...] += jnp.sum(dcqn * cqh, axis=0, keepdims=True)
        dch = dcqn * gq_ref[...]
        emit(0, rq * (dch - cqh * jnp.mean(dch * cqh, axis=-1, keepdims=True)))

        ckv = hl_ref[:, Q_RANK:Q_RANK + KV_RANK]
        rk = lax.rsqrt(jnp.mean(ckv * ckv, axis=-1, keepdims=True) + RMS_EPS)
        ckh = ckv * rk
        ckn = (ckh * gkv_ref[...]).astype(BF16)
        dkb = dk_ref[...].astype(BF16)
        dvb = dv_ref[...].astype(BF16)
        dwk_ref[...] += _dot_tn(ckn, dkb)
        dwv_ref[...] += _dot_tn(ckn, dvb)
        dckn = _dot_nt(dkb, wk_ref[...]) + _dot_nt(dvb, wv_ref[...])
        dgkv_ref[...] += jnp.sum(dckn * ckh, axis=0, keepdims=True)
        dkh = dckn * gkv_ref[...]
        emit(Q_RANK, rk * (dkh - ckh * jnp.mean(dkh * ckh, axis=-1, keepdims=True)))
        emit(Q_RANK + KV_RANK, dkpe)

    acc_shapes = [wuq.shape, wk.shape, wv.shape, g_q.shape, g_kv.shape, (1, LAT_W)]
    return pl.pallas_call(
        body, name="lat_bwd", grid=(s // ts,),
        in_specs=[pl.BlockSpec((qk_w, ts), lambda i: (0, i)), _row_spec(ts, qk_w), _row_spec(ts, MLA_W),
                  _row_spec(ts, LAT_W), _row_spec(ts, HEAD_PAD), _row_spec(ts, HEAD_PAD), _row_spec(ts, HEAD_PAD),
                  _full_spec(g_q.shape), _full_spec(g_kv.shape), _full_spec(wuq.shape), _full_spec(wk.shape),
                  _full_spec(wv.shape), pl.BlockSpec(memory_space=pl.ANY)],
        out_specs=[_row_spec(ts, LAT_W), pl.BlockSpec((LAT_W, ts), lambda i: (0, i))]
        + [_full_spec(sh) for sh in acc_shapes],
        out_shape=[jax.ShapeDtypeStruct((s, LAT_W), BF16), jax.ShapeDtypeStruct((LAT_W, s), BF16)]
        + [jax.ShapeDtypeStruct(sh, F32) for sh in acc_shapes],
        compiler_params=_params(),
    )(dq, dk, dv, hl, rc, rsl, rsh, g_q, g_kv, wuq, wk, wv, after)


def _dx(dr, dhg, dhm, dhl, wt, after):
    s = dr.shape[0]
    ts = MATMUL_ROW_TILE

    def body(dr_ref, dhg_ref, dhm_ref, dhl_ref, wt_ref, after_ref, dx_ref):
        dx_ref[...] = (ALPHA * dr_ref[...]
                       + _dot(dhg_ref[...], wt_ref[ROW_GATE:IN_W, :])
                       + _dot(dhm_ref[...], wt_ref[LAT_COLS:ROW_GATE, :])
                       + _dot(dhl_ref[:, 0:Q_RANK + KV_RANK], wt_ref[0:Q_RANK + KV_RANK, :])
                       + _dot(dhl_ref[:, Q_RANK + KV_RANK:], _kpe_rows(wt_ref)))

    return pl.pallas_call(
        body, name="dx", grid=(s // ts,),
        in_specs=[_row_spec(ts, D_MODEL), _row_spec(ts, GATE_W), _row_spec(ts, MID_W), _row_spec(ts, LAT_W),
                  _full_spec(wt.shape), pl.BlockSpec(memory_space=pl.ANY)],
        out_specs=_row_spec(ts, D_MODEL),
        out_shape=jax.ShapeDtypeStruct((s, D_MODEL), F32),
        compiler_params=_params(),
    )(dr, dhg, dhm, dhl, wt, after)


def _dwt_early(dhmt, dhgt, xb, col, after, name):
    tn = 512
    nm, ng = MID_W // tn, GATE_W // tn
    s = dhmt.shape[1]
    hc = D_MODEL // 2

    ks = s // 2

    def body(col_ref, dma_ref, dmb_ref, dga_ref, dgb_ref, xba_ref, xbb_ref, after_ref, dw_ref):
        i = pl.program_id(0)

        @pl.when(i < nm)
        def _():
            dw_ref[...] = (_dot(dma_ref[...], xba_ref[...]) + _dot(dmb_ref[...], xbb_ref[...])).astype(BF16)

        @pl.when(i >= nm)
        def _():
            dw_ref[...] = (_dot(dga_ref[...], xba_ref[...]) + _dot(dgb_ref[...], xbb_ref[...])).astype(BF16)

    def dh_spec(first, part):
        if first:
            return pl.BlockSpec((tn, ks), lambda i, col_ref: (jnp.minimum(i, nm - 1), part))
        return pl.BlockSpec((tn, ks), lambda i, col_ref: (jnp.maximum(i - nm, 0), part))

    rows = pl.pallas_call(
        body, name=name,
        grid_spec=pltpu.PrefetchScalarGridSpec(
            num_scalar_prefetch=1, grid=(nm + ng,),
            in_specs=[dh_spec(True, 0), dh_spec(True, 1), dh_spec(False, 0), dh_spec(False, 1),
                      pl.BlockSpec((None, ks, hc), lambda i, col_ref: (col_ref[0], 0, 0)),
                      pl.BlockSpec((None, ks, hc), lambda i, col_ref: (col_ref[0], 1, 0)),
                      pl.BlockSpec(memory_space=pl.ANY)],
            out_specs=pl.BlockSpec((pl.Element(tn), pl.Element(hc)),
                                   lambda i, col_ref: (pl.multiple_of(LAT_COLS + i * tn, 32), 0))),
        out_shape=jax.ShapeDtypeStruct((IN_W, hc), BF16),
        compiler_params=_params(),
    )(col, dhmt, dhmt, dhgt, dhgt, xb, xb, after)

    def zero(buf_ref, out_ref):
        out_ref[...] = jnp.zeros_like(out_ref)

    return pl.pallas_call(
        zero, name=name + "_zero_lat", grid=(1,), in_specs=[pl.BlockSpec(memory_space=pl.ANY)],
        out_specs=pl.BlockSpec((LAT_COLS, hc), lambda i: (0, 0)),
        out_shape=jax.ShapeDtypeStruct((IN_W, hc), BF16), input_output_aliases={0: 0},
    )(rows)


def _dwt_lat(dhlt, xb):
    n, s = dhlt.shape

    def body(dht_ref, xb_ref, dw_ref):
        dw = _dot(dht_ref[...], xb_ref[...]).astype(BF16)
        kpe = Q_RANK + KV_RANK + NOPE
        dw_ref[0:Q_RANK + KV_RANK, :] = dw[0:Q_RANK + KV_RANK]
        dw_ref[Q_RANK + KV_RANK:LAT_COLS, :] = dw[kpe:kpe + ROPE]
        dw_ref[LAT_COLS:, :] = jnp.zeros((LAT_ROWS_PAD - LAT_COLS, D_MODEL), BF16)

    return pl.pallas_call(
        body, name="dwt_lat", in_specs=[VMEM_SPEC, VMEM_SPEC], out_specs=VMEM_SPEC,
        out_shape=jax.ShapeDtypeStruct((LAT_ROWS_PAD, D_MODEL), BF16),
        compiler_params=pltpu.CompilerParams(vmem_limit_bytes=VMEM_LIMIT),
    )(dhlt, xb)


def _split_bias(b):
    z = lambda n: jnp.zeros((n,), b.dtype)
    lat = jnp.concatenate([b[:Q_RANK + KV_RANK], z(NOPE), b[Q_RANK + KV_RANK:LAT_COLS], z(HEAD_PAD - QK_DIM)])
    return b[None, ROW_GATE:], b[None, LAT_COLS:ROW_GATE], lat[None, :]


def _join_bias(g, m, l):
    kpe = Q_RANK + KV_RANK + NOPE
    return jnp.concatenate([l[0, :Q_RANK + KV_RANK], l[0, kpe:kpe + ROPE], m[0], g[0]])


def _rope_tables(positions):
    half = ROPE // 2
    inv_freq = ROPE_THETA ** (-jnp.arange(0, ROPE, 2, dtype=F32) / ROPE)
    ang = positions.astype(F32)[:, None] * inv_freq
    cos, sin = jnp.cos(ang), jnp.sin(ang)
    n = positions.shape[0]
    one, zero = jnp.ones((n, NOPE), F32), jnp.zeros((n, half), F32)
    tail1, tail0 = jnp.ones((n, HEAD_PAD - QK_DIM), F32), jnp.zeros((n, HEAD_PAD - QK_DIM), F32)
    z64 = jnp.zeros((n, NOPE), F32)
    rc = jnp.concatenate([one, cos, cos, tail1], axis=1)
    rsl = jnp.concatenate([z64, -sin, zero, tail0], axis=1)
    rsh = jnp.concatenate([z64, zero, sin, tail0], axis=1)
    return rc, rsl, rsh


def _local_attention(x, tables, wlat, b_in, g_q, w_uq, g_kv, w_ukv, after):
    rc, rsl, rsh = tables
    b_g, b_m, b_l = _split_bias(b_in)
    wuq = jnp.pad(w_uq, ((0, 0), (0, 0), (0, HEAD_PAD - QK_DIM))).reshape(Q_RANK, HEADS * HEAD_PAD).astype(BF16)
    wk = jnp.pad(w_ukv[:, :, :NOPE], ((0, 0), (0, 0), (0, HEAD_PAD - NOPE))).reshape(KV_RANK, HEADS * HEAD_PAD).astype(BF16)
    wv = w_ukv[:, :, NOPE:].reshape(KV_RANK, MLA_W).astype(BF16)
    gq2, gkv2 = g_q[None, :], g_kv[None, :]
    hl, q, k, v, xb, qt, kt, vt, xb2 = _fwd_lat(x, wlat, b_l, gq2, wuq, gkv2, wk, wv, rc, rsl, rsh, after)
    o, lse = _attn_fwd(qt, k, vt)
    return dict(q=q, qt=qt, k=k, kt=kt, v=v, o=o, lse=lse, hl=hl, rc=rc, rsl=rsl, rsh=rsh, gq2=gq2, gkv2=gkv2,
                wuq=wuq, wk=wk, wv=wv, xb=xb, xb2=xb2, b_g=b_g, b_m=b_m)


def _local_head(st, x, tgt, wt, w_oa, sg_g, sg_b, w_s, b_s, w_ob, w_out, ln_g, ln_b):
    q, qt, k, kt, v, o, lse, hl, xb = (st[n] for n in ("q", "qt", "k", "kt", "v", "o", "lse", "hl", "xb"))
    rc, rsl, rsh, gq2, gkv2, wuq, wk, wv = (st[n] for n in ("rc", "rsl", "rsh", "gq2", "gkv2", "wuq", "wk", "wv"))
    bsb = jnp.repeat(b_s.T, V_DIM, axis=1)
    hg, hm = _fwd_rest(xb, wt, st["b_g"], st["b_m"])
    (dr, dhg, dhm, do, dot, delta, dhgt, dhmt, dwout, dwoa, dwob, dws, dbs, dlng, dlnb, dsgg, dsgb, loss, dbg,
     dbm) = _mid(x, tgt, o, hm, hg, w_oa, w_ob, w_out, ln_g[None, :], ln_b[None, :], sg_g[None, :], sg_b[None, :],
                 w_s, bsb)
    delta = jnp.pad(delta.reshape(HEADS // 2, 2, -1), ((0, 0), (0, 6), (0, 0)))
    early = {
        "w_oa": dwoa, "sgu_ln_g": dsgg[0], "sgu_ln_b": dsgb[0], "w_s": dws, "b_s": dbs[:, :GROUPS].T,
        "w_ob": dwob, "w_out": dwout, "ln_g": dlng[0], "ln_b": dlnb[0],
    }
    state = dict(q=q, qt=qt, k=k, kt=kt, v=v, do=do, dot=dot, lse=lse, delta=delta, hl=hl, rc=rc, rsl=rsl, rsh=rsh,
                 gq2=gq2, gkv2=gkv2, wuq=wuq, wk=wk, wv=wv, dr=dr, dhg=dhg, dhm=dhm, wt=wt, xb=xb, dbg=dbg, dbm=dbm,
                 dhgt=dhgt, dhmt=dhmt, xb2=st["xb2"])
    return loss, early, state


def _local_attn_bwd(st):
    return _attn_bwd(st["q"], st["qt"], st["k"], st["kt"], st["v"], st["do"], st["dot"], st["lse"], st["delta"])


def _local_tail(st, dq, dk, dv, after):
    dhl, dhlt, dwuq, dwk, dwv, dgq, dgkv, dbl = _lat_bwd(dq, dk, dv, st["hl"], st["rc"], st["rsl"], st["rsh"],
                                                         st["gq2"], st["gkv2"], st["wuq"], st["wk"], st["wv"], after)
    late = {
        "w_lat": _dwt_lat(dhlt, st["xb"]),
        "b_in": _join_bias(st["dbg"], st["dbm"], dbl),
        "g_q": dgq[0],
        "w_uq": dwuq.reshape(Q_RANK, HEADS, HEAD_PAD)[:, :, :QK_DIM],
        "g_kv": dgkv[0],
        "w_ukv": jnp.concatenate([dwk.reshape(KV_RANK, HEADS, HEAD_PAD)[:, :, :NOPE],
                                  dwv.reshape(KV_RANK, HEADS, V_DIM)], axis=2),
    }
    return dhl, late


def _local_step(x, positions, tgt, wt, b_in, g_q, w_uq, g_kv, w_ukv, w_oa, sg_g, sg_b, w_s, b_s, w_ob, w_out, ln_g,
                ln_b):
    st = _local_attention(x, _rope_tables(positions), wt[:LAT_COLS], b_in, g_q, w_uq, g_kv, w_ukv, b_in)
    loss, early, st = _local_head(st, x, tgt, wt, w_oa, sg_g, sg_b, w_s, b_s, w_ob, w_out, ln_g, ln_b)
    dq, dk, dv = _local_attn_bwd(st)
    dhl, late = _local_tail(st, dq, dk, dv, dv)
    dx = _dx(st["dr"], st["dhg"], st["dhm"], dhl, st["wt"], dhl)
    grads = {**early, **late}
    halves = [_dwt_early(st["dhmt"], st["dhgt"], st["xb2"], jnp.full((1,), h, jnp.int32), dhl, "dwt_half%d" % h)
              for h in range(2)]
    grads["w_in"] = jnp.concatenate([grads.pop("w_lat")[:LAT_COLS], jnp.concatenate(halves, axis=1)[LAT_COLS:]],
                                    axis=0)
    return loss, dx, grads


MESH = pl.DeviceIdType.MESH
N_CHIPS = 4
HBM_SPEC = pl.BlockSpec(memory_space=pl.ANY)
HBM_SPEC_STRICT = pl.BlockSpec(memory_space=pltpu.HBM)
VMEM_SPEC = pl.BlockSpec(memory_space=pltpu.VMEM)

REP_ROWS = 80


def _rows8(a):
    flat = a.reshape(-1)
    n = -(-flat.shape[0] // (8 * D_MODEL)) * 8 * D_MODEL
    return jnp.pad(flat, (0, n - flat.shape[0])).reshape(-1, D_MODEL)


def _place():
    x, y, c = lax.axis_index("x"), lax.axis_index("y"), lax.axis_index("c")
    others = [(1 - x, y), (x, 1 - y), (1 - x, 1 - y)]
    return x, y, c, others


N_DEV = 8
LOSS_TILE = (8, 128)


def _cast_own(shards, me, after):
    n = len(shards)

    def body(me_ref, *refs):
        for w in range(n):
            refs[n + 1 + w][...] = refs[w][...].astype(BF16)

    return pl.pallas_call(
        body, name="cast_own",
        grid_spec=pltpu.PrefetchScalarGridSpec(
            num_scalar_prefetch=1, grid=(1,),
            in_specs=[pl.BlockSpec(s.shape, lambda i, me_ref: (0, 0)) for s in shards]
            + [pl.BlockSpec(memory_space=pl.ANY)],
            out_specs=[pl.BlockSpec((None,) + s.shape, lambda i, me_ref: (me_ref[0], 0, 0)) for s in shards]),
        out_shape=[jax.ShapeDtypeStruct((N_CHIPS,) + s.shape, BF16) for s in shards],
        compiler_params=pltpu.CompilerParams(vmem_limit_bytes=VMEM_LIMIT),
    )(me, *shards, after)


def _cast_first(lat, uq, me):
    def body(me_ref, lat_ref, uq_ref, wlat_ref, guq_ref):
        wlat_ref[...] = lat_ref[...].astype(BF16)
        guq_ref[...] = uq_ref[...].astype(BF16)

    return pl.pallas_call(
        body, name="cast_first",
        grid_spec=pltpu.PrefetchScalarGridSpec(
            num_scalar_prefetch=1, grid=(1,),
            in_specs=[pl.BlockSpec((LAT_COLS, D_MODEL), lambda i, me_ref: (0, 0)),
                      pl.BlockSpec(uq.shape, lambda i, me_ref: (0, 0))],
            out_specs=[pl.BlockSpec((LAT_COLS, D_MODEL), lambda i, me_ref: (0, 0)),
                       pl.BlockSpec((None,) + uq.shape, lambda i, me_ref: (me_ref[0], 0, 0))]),
        out_shape=[jax.ShapeDtypeStruct((LAT_COLS, D_MODEL), BF16),
                   jax.ShapeDtypeStruct((N_CHIPS,) + uq.shape, BF16)],
    )(me, lat, uq)


def _first_copies(wlat_ref, guq_ref, send_sems, recv_sems, shapes):
    x, y, c, others = _place()
    me = 2 * x + y
    hl, hu = shapes[0][1] // 2, shapes[1][2] // 2
    lat_half = wlat_ref.at[:, pl.ds(c * hl, hl)]

    def copy(src, dst, k, to):
        return pltpu.make_async_remote_copy(src_ref=src, dst_ref=dst, send_sem=send_sems.at[k],
                                            recv_sem=recv_sems.at[k], device_id=to, device_id_type=MESH)

    def uq_half(chip):
        return guq_ref.at[chip, :, pl.ds(c * hu, hu)]

    lat_out = [copy(lat_half, lat_half, j, (*others[j], c)) for j in range(3)]
    uq_out = [copy(uq_half(me), uq_half(me), 3 + j, (*others[j], c)) for j in range(3)]
    j0 = jnp.maximum(x + 2 * y - 1, 0)
    lat_in = copy(lat_half, lat_half, j0, (0, 0, c))
    uq_in = [copy(uq_half(me), uq_half(2 * px + py), 3 + j, (px, py, c)) for j, (px, py) in enumerate(others)]
    return me, lat_out, uq_out, lat_in, uq_in


def _first_start(wlat, guq):
    shapes = (wlat.shape, guq.shape)

    def body(wlat_ref, guq_ref, send_sems, recv_sems, wlat_thru, guq_thru, token):
        me, lat_out, uq_out, _, _ = _first_copies(wlat_ref, guq_ref, send_sems, recv_sems, shapes)

        @pl.when(me == 0)
        def _():
            for cp in lat_out:
                cp.start()

        for cp in uq_out:
            cp.start()
        token[...] = jnp.zeros_like(token)

    outs = pl.pallas_call(
        body, name="first_start",
        out_shape=(pltpu.SemaphoreType.DMA((6,)), pltpu.SemaphoreType.DMA((6,)), pltpu.HBM(wlat.shape, BF16),
                   pltpu.HBM(guq.shape, BF16), jax.ShapeDtypeStruct(LOSS_TILE, F32)),
        in_specs=[HBM_SPEC_STRICT] * 2, out_specs=(SEM_SPEC, SEM_SPEC, HBM_SPEC_STRICT, HBM_SPEC_STRICT, VMEM_SPEC),
        input_output_aliases={0: 2, 1: 3},
        compiler_params=pltpu.CompilerParams(has_side_effects=SPLIT_EFFECT),
    )(pltpu.with_memory_space_constraint(wlat, pltpu.HBM), pltpu.with_memory_space_constraint(guq, pltpu.HBM))
    return outs


def _first_wait(send_sems, recv_sems, wlat, guq, *after):
    shapes = (wlat.shape, guq.shape)

    def body(wlat_ref, guq_ref, send_sems, recv_sems, *rest):
        me, lat_out, uq_out, lat_in, uq_in = _first_copies(wlat_ref, guq_ref, send_sems, recv_sems, shapes)

        @pl.when(me == 0)
        def _():
            for cp in lat_out:
                cp.wait_send()

        @pl.when(me != 0)
        def _():
            lat_in.wait_recv()

        for cp in uq_out:
            cp.wait_send()
        for cp in uq_in:
            cp.wait_recv()

    return pl.pallas_call(
        body, name="first_wait", out_shape=(pltpu.HBM(wlat.shape, BF16), pltpu.HBM(guq.shape, BF16)),
        in_specs=[HBM_SPEC_STRICT, HBM_SPEC_STRICT, SEM_SPEC, SEM_SPEC] + [HBM_SPEC] * len(after),
        out_specs=(HBM_SPEC_STRICT, HBM_SPEC_STRICT), input_output_aliases={0: 0, 1: 1},
        compiler_params=pltpu.CompilerParams(has_side_effects=SPLIT_EFFECT),
    )(wlat, guq, send_sems, recv_sems, *after)


def _first_forward(wlat, guq):
    hl, hu = wlat.shape[1] // 2, guq.shape[2] // 2

    def body(wlat_in, guq_in, wlat_ref, guq_ref, send_sems, recv_sems):
        x, y, c, others = _place()
        me = 2 * x + y
        sibling = (x, y, 1 - c)

        def copy(part, k):
            return pltpu.make_async_remote_copy(src_ref=part, dst_ref=part, send_sem=send_sems.at[k],
                                                recv_sem=recv_sems.at[k], device_id=sibling, device_id_type=MESH)

        def uq_part(j, half):
            px, py = others[j]
            return guq_ref.at[2 * px + py, :, pl.ds(half * hu, hu)]

        cps = [copy(uq_part(j, c), j) for j in range(3)]
        for cp in cps:
            cp.start()

        @pl.when(me != 0)
        def _():
            mine = copy(wlat_ref.at[:, pl.ds(c * hl, hl)], 3)
            mine.start()
            copy(wlat_ref.at[:, pl.ds((1 - c) * hl, hl)], 3).wait_recv()
            mine.wait_send()

        for j in range(3):
            copy(uq_part(j, 1 - c), j).wait_recv()
        for cp in cps:
            cp.wait_send()

    return pl.pallas_call(
        body, name="first_forward", in_specs=[HBM_SPEC, HBM_SPEC], out_specs=[HBM_SPEC, HBM_SPEC],
        out_shape=[jax.ShapeDtypeStruct(wlat.shape, BF16), jax.ShapeDtypeStruct(guq.shape, BF16)],
        input_output_aliases={0: 0, 1: 1},
        scratch_shapes=[pltpu.SemaphoreType.DMA((4,)), pltpu.SemaphoreType.DMA((4,))],
    )(wlat, guq)


def _gather_start(bufs, after):
    n = len(bufs)

    def body(*refs):
        b_refs = refs[:n]
        send_sems, recv_sems, token = refs[n + 1], refs[n + 2], refs[-1]
        x, y, c, others = _place()
        me = 2 * x + y
        for w in range(n):
            hc = _half(bufs[w])
            mine = b_refs[w].at[me, :, pl.ds(c * hc, hc)]
            for j, (px, py) in enumerate(others):
                pltpu.make_async_remote_copy(
                    src_ref=mine, dst_ref=mine, send_sem=send_sems.at[3 * w + j], recv_sem=recv_sems.at[3 * w + j],
                    device_id=(px, py, c), device_id_type=MESH).start()
        token[...] = jnp.zeros_like(token)

    hbm = [pltpu.HBM(b.shape, BF16) for b in bufs]
    outs = pl.pallas_call(
        body, name="gather_start",
        out_shape=(pltpu.SemaphoreType.DMA((3 * n,)), pltpu.SemaphoreType.DMA((3 * n,)), *hbm,
                   jax.ShapeDtypeStruct(LOSS_TILE, F32)),
        in_specs=[HBM_SPEC_STRICT] * n + [HBM_SPEC],
        out_specs=(SEM_SPEC, SEM_SPEC, *[HBM_SPEC_STRICT] * n, VMEM_SPEC),
        input_output_aliases={i: 2 + i for i in range(n)},
        compiler_params=pltpu.CompilerParams(has_side_effects=SPLIT_EFFECT),
    )(*[pltpu.with_memory_space_constraint(b, pltpu.HBM) for b in bufs], after)
    return outs[0], outs[1], list(outs[2:2 + n]), outs[-1]


def _gather_wait(send_sems, recv_sems, bufs, after):
    n = len(bufs)

    def body(*refs):
        b_refs = refs[:n]
        send_sems, recv_sems = refs[n], refs[n + 1]
        x, y, c, others = _place()
        me = 2 * x + y
        for w in range(n):
            hc = _half(bufs[w])
            for j, (px, py) in enumerate(others):
                cp = pltpu.make_async_remote_copy(
                    src_ref=b_refs[w].at[me, :, pl.ds(c * hc, hc)],
                    dst_ref=b_refs[w].at[2 * px + py, :, pl.ds(c * hc, hc)],
                    send_sem=send_sems.at[3 * w + j], recv_sem=recv_sems.at[3 * w + j], device_id=(px, py, c),
                    device_id_type=MESH)
                cp.wait_send()
                cp.wait_recv()

    outs = pl.pallas_call(
        body, name="gather_wait", out_shape=tuple(pltpu.HBM(b.shape, b.dtype) for b in bufs),
        in_specs=[HBM_SPEC_STRICT] * n + [SEM_SPEC, SEM_SPEC, HBM_SPEC],
        out_specs=tuple([HBM_SPEC_STRICT] * n), input_output_aliases={i: i for i in range(n)},
        compiler_params=pltpu.CompilerParams(has_side_effects=SPLIT_EFFECT),
    )(*bufs, send_sems, recv_sems, after)
    return list(outs)


def _gather_finish(bufs):
    n = len(bufs)

    def body(*refs):
        b_refs = refs[n:2 * n]
        send_sems, recv_sems = refs[2 * n:]
        x, y, c, others = _place()
        cps = []
        for w in range(n):
            hc = _half(bufs[w])
            for j, (px, py) in enumerate(others):
                part = b_refs[w].at[2 * px + py, :, pl.ds(c * hc, hc)]
                cps.append(pltpu.make_async_remote_copy(
                    src_ref=part, dst_ref=part, send_sem=send_sems.at[3 * w + j], recv_sem=recv_sems.at[3 * w + j],
                    device_id=(x, y, 1 - c), device_id_type=MESH))
        for cp in cps:
            cp.start()
        for w in range(n):
            hc = _half(bufs[w])
            for j, (px, py) in enumerate(others):
                theirs = b_refs[w].at[2 * px + py, :, pl.ds((1 - c) * hc, hc)]
                pltpu.make_async_remote_copy(
                    src_ref=theirs, dst_ref=theirs, send_sem=send_sems.at[3 * w + j], recv_sem=recv_sems.at[3 * w + j],
                    device_id=(x, y, 1 - c), device_id_type=MESH).wait_recv()
        for cp in cps:
            cp.wait_send()

    return pl.pallas_call(
        body, name="gather_finish", in_specs=[HBM_SPEC] * n, out_specs=[HBM_SPEC] * n,
        out_shape=[jax.ShapeDtypeStruct(b.shape, b.dtype) for b in bufs],
        input_output_aliases={i: i for i in range(n)},
        scratch_shapes=[pltpu.SemaphoreType.DMA((3 * n,)), pltpu.SemaphoreType.DMA((3 * n,))],
    )(*bufs)


def _half(a):
    return a.shape[-1] // 2


def _exchange_pairs(parts, name):
    n = len(parts)

    def body(*refs):
        p_refs, r_refs = refs[:n], refs[n:2 * n]
        send_sems, recv_sems = refs[2 * n:]
        x, y, c, _ = _place()
        cps = []
        for w in range(n):
            h = _half(parts[w])
            cps.append(pltpu.make_async_remote_copy(
                src_ref=p_refs[w].at[:, :, pl.ds((1 - c) * h, h)], dst_ref=r_refs[w],
                send_sem=send_sems.at[w], recv_sem=recv_sems.at[w], device_id=(x, y, 1 - c), device_id_type=MESH))
        for cp in cps:
            cp.start()
        for cp in cps:
            cp.wait()

    return pl.pallas_call(
        body, name=name, in_specs=[HBM_SPEC] * n, out_specs=[HBM_SPEC] * n,
        out_shape=[jax.ShapeDtypeStruct((N_CHIPS, p.shape[1], _half(p)), BF16) for p in parts],
        scratch_shapes=[pltpu.SemaphoreType.DMA((n,)), pltpu.SemaphoreType.DMA((n,))],
    )(*parts)


def _sibling_part(ref, w, n_whole, shape, c):
    if w < n_whole:
        return ref
    h = shape[-1] // 2
    return ref.at[:, :, pl.ds((1 - c) * h, h)]


def _pairs_start(parts, all_loss, n_whole):
    n = len(parts)

    def body(*refs):
        p_refs, r_refs, loss_ref = refs[:n], refs[n:2 * n], refs[2 * n]
        send_sems, recv_sems, token = refs[2 * n + 1], refs[2 * n + 2], refs[-1]
        x, y, c, _ = _place()
        for w in range(n):
            h = _half(parts[w])
            pltpu.make_async_remote_copy(
                src_ref=_sibling_part(p_refs[w], w, n_whole, parts[w].shape, c), dst_ref=r_refs[w],
                send_sem=send_sems.at[w], recv_sem=recv_sems.at[w], device_id=(x, y, 1 - c),
                device_id_type=MESH).start()
        me = 4 * x + 2 * y + c
        for t in range(1, N_DEV):
            d = (me + t) % N_DEV
            pltpu.make_async_remote_copy(
                src_ref=loss_ref.at[me], dst_ref=loss_ref.at[me], send_sem=send_sems.at[n + t - 1],
                recv_sem=recv_sems.at[n + t - 1], device_id=(d // 4, (d // 2) % 2, d % 2), device_id_type=MESH).start()
        token[...] = jnp.zeros_like(token)

    lands = [pltpu.HBM(p.shape if w < n_whole else (N_CHIPS, p.shape[1], _half(p)), BF16)
             for w, p in enumerate(parts)]
    nsem = n + N_DEV - 1
    outs = pl.pallas_call(
        body, name="pairs_start",
        out_shape=(pltpu.SemaphoreType.DMA((nsem,)), pltpu.SemaphoreType.DMA((nsem,)),
                   *[pltpu.HBM(p.shape, p.dtype) for p in parts], *lands, pltpu.HBM(all_loss.shape, F32),
                   jax.ShapeDtypeStruct(LOSS_TILE, F32)),
        in_specs=[HBM_SPEC_STRICT] * (2 * n + 1),
        out_specs=(SEM_SPEC, SEM_SPEC, *[HBM_SPEC_STRICT] * (2 * n + 1), VMEM_SPEC),
        input_output_aliases={i: 2 + i for i in range(2 * n + 1)},
        compiler_params=pltpu.CompilerParams(has_side_effects=SPLIT_EFFECT),
    )(*[pltpu.with_memory_space_constraint(p, pltpu.HBM) for p in parts],
      *[pltpu.with_memory_space_constraint(lax.empty(l.shape, BF16), pltpu.HBM) for l in lands],
      pltpu.with_memory_space_constraint(all_loss, pltpu.HBM))
    return outs[0], outs[1], list(outs[2:2 + n]), list(outs[2 + n:2 + 2 * n]), outs[2 + 2 * n], outs[-1]


def _pairs_wait(send_sems, recv_sems, parts, lands, all_loss, after, n_whole):
    n = len(parts)

    def body(*refs):
        p_refs, r_refs, loss_ref = refs[:n], refs[n:2 * n], refs[2 * n]
        send_sems, recv_sems = refs[2 * n + 1], refs[2 * n + 2]
        x, y, c, _ = _place()
        for w in range(n):
            h = _half(parts[w])
            cp = pltpu.make_async_remote_copy(
                src_ref=_sibling_part(p_refs[w], w, n_whole, parts[w].shape, c), dst_ref=r_refs[w],
                send_sem=send_sems.at[w],
                recv_sem=recv_sems.at[w], device_id=(x, y, 1 - c), device_id_type=MESH)
            cp.wait_send()
            cp.wait_recv()
        me = 4 * x + 2 * y + c
        for t in range(1, N_DEV):
            d = (me + N_DEV - t) % N_DEV
            cp = pltpu.make_async_remote_copy(
                src_ref=loss_ref.at[me], dst_ref=loss_ref.at[d], send_sem=send_sems.at[n + t - 1],
                recv_sem=recv_sems.at[n + t - 1], device_id=(d // 4, (d // 2) % 2, d % 2), device_id_type=MESH)
            cp.wait_send()
            cp.wait_recv()

    bufs = (*parts, *lands, all_loss)
    outs = pl.pallas_call(
        body, name="pairs_wait", out_shape=tuple(pltpu.HBM(a.shape, a.dtype) for a in bufs),
        in_specs=[HBM_SPEC_STRICT] * len(bufs) + [SEM_SPEC, SEM_SPEC, HBM_SPEC],
        out_specs=tuple([HBM_SPEC_STRICT] * len(bufs)), input_output_aliases={i: i for i in range(len(bufs))},
        compiler_params=pltpu.CompilerParams(has_side_effects=SPLIT_EFFECT),
    )(*bufs, send_sems, recv_sems, after)
    return list(outs[:n]), list(outs[n:2 * n]), outs[2 * n]


def _add_pair(ps, rs, c, name, n_whole=0):
    n = len(ps)

    def body(c_ref, *refs):
        for w in range(n):
            if w < n_whole:
                mine = refs[w][...]
            else:
                h = _half(ps[w])
                mine = refs[w][:, :, pl.ds(pl.multiple_of(c_ref[0] * h, 128), h)]
            refs[2 * n + w][...] = (mine.astype(F32) + refs[n + w][...].astype(F32)).astype(BF16)

    return pl.pallas_call(
        body, name=name,
        in_specs=[pl.BlockSpec(memory_space=pltpu.SMEM)] + [VMEM_SPEC] * (2 * n), out_specs=[VMEM_SPEC] * n,
        out_shape=[jax.ShapeDtypeStruct(r.shape, BF16) for r in rs],
        compiler_params=pltpu.CompilerParams(vmem_limit_bytes=VMEM_LIMIT),
    )(c, *ps, *rs)


SEM_SPEC = pl.BlockSpec(memory_space=pltpu.SEMAPHORE)
SPLIT_EFFECT = pltpu.SideEffectType.DATAFLOW_SIDE_EFFECTING


def _chips_start(qs, name):
    n = len(qs)

    def body(*refs):
        q_refs, land_refs = refs[:n], refs[n:2 * n]
        send_sems, recv_sems, token = refs[2 * n], refs[2 * n + 1], refs[-1]
        x, y, c, others = _place()
        me = 2 * x + y
        for w in range(n):
            for j, (px, py) in enumerate(others):
                pltpu.make_async_remote_copy(
                    src_ref=q_refs[w].at[2 * px + py], dst_ref=land_refs[w].at[me], send_sem=send_sems.at[3 * w + j],
                    recv_sem=recv_sems.at[3 * w + j], device_id=(px, py, c), device_id_type=MESH).start()
        token[...] = jnp.zeros_like(token)

    hbm = [pltpu.HBM(q.shape, BF16) for q in qs]
    outs = pl.pallas_call(
        body, name=name,
        out_shape=(pltpu.SemaphoreType.DMA((3 * n,)), pltpu.SemaphoreType.DMA((3 * n,)), *hbm, *hbm,
                   jax.ShapeDtypeStruct(LOSS_TILE, F32)),
        in_specs=[HBM_SPEC_STRICT] * (2 * n),
        out_specs=(SEM_SPEC, SEM_SPEC, *[HBM_SPEC_STRICT] * (2 * n), VMEM_SPEC),
        input_output_aliases={i: 2 + i for i in range(2 * n)},
        compiler_params=pltpu.CompilerParams(has_side_effects=SPLIT_EFFECT),
    )(*[pltpu.with_memory_space_constraint(q, pltpu.HBM) for q in qs],
      *[pltpu.with_memory_space_constraint(lax.empty(q.shape, BF16), pltpu.HBM) for q in qs])
    return outs[0], outs[1], outs[2:2 + n], outs[2 + n:2 + 2 * n], outs[-1]


def _chips_wait(send_sems, recv_sems, q_thru, land_thru, after, name):
    n = len(q_thru)

    def body(*refs):
        q_refs, land_refs = refs[:n], refs[n:2 * n]
        send_sems, recv_sems = refs[2 * n], refs[2 * n + 1]
        x, y, c, others = _place()
        me = 2 * x + y
        for w in range(n):
            for j, (px, py) in enumerate(others):
                cp = pltpu.make_async_remote_copy(
                    src_ref=q_refs[w].at[2 * px + py], dst_ref=land_refs[w].at[2 * px + py],
                    send_sem=send_sems.at[3 * w + j], recv_sem=recv_sems.at[3 * w + j], device_id=(px, py, c),
                    device_id_type=MESH)
                cp.wait_send()
                cp.wait_recv()

    outs = pl.pallas_call(
        body, name=name, out_shape=tuple(pltpu.HBM(a.shape, a.dtype) for a in (*q_thru, *land_thru)),
        in_specs=[HBM_SPEC_STRICT] * (2 * n) + [SEM_SPEC, SEM_SPEC, HBM_SPEC],
        out_specs=tuple([HBM_SPEC_STRICT] * (2 * n)), input_output_aliases={i: i for i in range(2 * n)},
        compiler_params=pltpu.CompilerParams(has_side_effects=SPLIT_EFFECT),
    )(*q_thru, *land_thru, send_sems, recv_sems, after)
    return list(outs[:n]), list(outs[n:])


def _sum_chips(qs, rs, idx, all_dtypes):
    n = len(rs)
    n_all = len(all_dtypes)

    def body(idx_ref, *refs):
        c = idx_ref[4]
        for w in range(n):
            q_ref, r_ref, g_ref = refs[w], refs[n + w], refs[2 * n + w]
            acc = q_ref[idx_ref[0]].astype(F32)
            for t in range(1, N_CHIPS):
                acc = acc + r_ref[idx_ref[t]].astype(F32)
            h = rs[w].shape[2]
            mine = pl.ds(pl.multiple_of(c * h, 128), h)
            g_ref[...] = jnp.zeros_like(g_ref)
            if w >= n - n_all:
                g_ref[idx_ref[0], :, mine] = acc.astype(g_ref.dtype)
            else:
                g_ref[:, mine] = acc

    shapes = [jax.ShapeDtypeStruct((r.shape[1], 2 * r.shape[2]), F32) for r in rs[:n - n_all]]
    shapes += [jax.ShapeDtypeStruct((N_CHIPS, r.shape[1], 2 * r.shape[2]), dt)
               for r, dt in zip(rs[n - n_all:], all_dtypes)]
    return pl.pallas_call(
        body, name="sum_chips",
        in_specs=[pl.BlockSpec(memory_space=pltpu.SMEM)] + [VMEM_SPEC] * (2 * n), out_specs=[VMEM_SPEC] * n,
        out_shape=shapes, compiler_params=pltpu.CompilerParams(vmem_limit_bytes=VMEM_LIMIT),
    )(idx, *qs, *rs)


def _share(shards, alls):
    n, na = len(shards), len(alls)
    total = n + na

    def body(*refs):
        g_refs, a_refs = refs[total:total + n], refs[total + n:2 * total]
        send_sems, recv_sems = refs[2 * total:]
        x, y, c, others = _place()
        me = 2 * x + y
        sibling = (x, y, 1 - c)

        def cols_of(w, half):
            h = shards[w].shape[1] // 2
            return g_refs[w].at[:, pl.ds(half * h, h)]

        def slab(a, chip, half):
            h = alls[a].shape[2] // 2
            return a_refs[a].at[chip, :, pl.ds(half * h, h)]

        def copy(src, dst, k, to):
            return pltpu.make_async_remote_copy(src_ref=src, dst_ref=dst, send_sem=send_sems.at[k],
                                                recv_sem=recv_sems.at[k], device_id=to, device_id_type=MESH)

        cps = [copy(cols_of(w, c), cols_of(w, c), w, sibling) for w in range(n)]
        for a in range(na):
            base = n + 7 * a
            cps.append(copy(slab(a, me, c), slab(a, me, c), base, sibling))
            for j, (px, py) in enumerate(others):
                cps.append(copy(slab(a, me, c), slab(a, me, c), base + 1 + j, (px, py, c)))
        for cp in cps:
            cp.start()
        fwd = []
        for a in range(na):
            base = n + 7 * a
            for j, (px, py) in enumerate(others):
                chip = 2 * px + py
                copy(slab(a, me, c), slab(a, chip, c), base + 1 + j, (px, py, c)).wait_recv()
                cp = copy(slab(a, chip, c), slab(a, chip, c), base + 4 + j, sibling)
                cp.start()
                fwd.append(cp)
        for a in range(na):
            base = n + 7 * a
            for j, (px, py) in enumerate(others):
                chip = 2 * px + py
                copy(slab(a, chip, c), slab(a, chip, 1 - c), base + 4 + j, sibling).wait_recv()
            copy(slab(a, me, c), slab(a, me, 1 - c), base, sibling).wait_recv()
        for w in range(n):
            copy(cols_of(w, c), cols_of(w, 1 - c), w, sibling).wait_recv()
        for cp in cps + fwd:
            cp.wait_send()

    nsem = n + 7 * na
    return pl.pallas_call(
        body, name="share", in_specs=[HBM_SPEC] * total, out_specs=[HBM_SPEC] * total,
        out_shape=[jax.ShapeDtypeStruct(a.shape, a.dtype) for a in (*shards, *alls)],
        input_output_aliases={i: i for i in range(total)},
        scratch_shapes=[pltpu.SemaphoreType.DMA((nsem,)), pltpu.SemaphoreType.DMA((nsem,))],
    )(*shards, *alls)


def _adamw(w, g, m, v):
    m2 = ADAM_B1 * m + (1.0 - ADAM_B1) * g
    v2 = ADAM_B2 * v + (1.0 - ADAM_B2) * (g * g)
    m_hat = m2 / (1.0 - ADAM_B1 ** ADAM_STEP)
    v_hat = v2 / (1.0 - ADAM_B2 ** ADAM_STEP)
    return -ADAM_LR * (m_hat / (jnp.sqrt(v_hat) + ADAM_EPS) + ADAM_WD * w), m2, v2


def _update_w_in(wt, gt, mt, vt, lat, owner, tile):
    nlat = lat.shape[0] // tile

    def body(owner_ref, w_ref, g_ref, m_ref, v_ref, lat_ref, g2_ref, d_ref, m2_ref, v2_ref):
        row = pl.program_id(0) * tile + lax.broadcasted_iota(jnp.int32, (tile, 1), 0)
        g = jnp.where((row < LAT_COLS) & (owner_ref[0] == 1), lat_ref[...].astype(F32), g_ref[...])
        g2_ref[...] = g
        d_ref[...], m2_ref[...], v2_ref[...] = _adamw(w_ref[...], g, m_ref[...], v_ref[...])

    spec = pl.BlockSpec((tile, wt.shape[1]), lambda i, o: (i, 0))
    return pl.pallas_call(
        body, name="update_w_in",
        grid_spec=pltpu.PrefetchScalarGridSpec(
            num_scalar_prefetch=1, grid=(wt.shape[0] // tile,),
            in_specs=[spec] * 4 + [pl.BlockSpec((tile, wt.shape[1]), lambda i, o: (jnp.minimum(i, nlat - 1), 0))],
            out_specs=[spec] * 4),
        out_shape=[jax.ShapeDtypeStruct(wt.shape, F32)] * 4,
        compiler_params=_params(("parallel",)),
    )(owner, wt, gt, mt, vt, lat)


def _update_small(ws, gs, ms, vs):
    n = len(ws)

    def body(*refs):
        for k in range(n):
            w_ref, g_ref, m_ref, v_ref = refs[k], refs[n + k], refs[2 * n + k], refs[3 * n + k]
            d, m2, v2 = _adamw(w_ref[...], g_ref[...], m_ref[...], v_ref[...])
            refs[4 * n + k][...] = d
            refs[5 * n + k][...] = m2
            refs[6 * n + k][...] = v2

    shapes = [jax.ShapeDtypeStruct(w.shape, F32) for w in ws]
    outs = pl.pallas_call(
        body, name="update_small", in_specs=[VMEM_SPEC] * (4 * n), out_specs=[VMEM_SPEC] * (3 * n),
        out_shape=shapes * 3,
        compiler_params=pltpu.CompilerParams(vmem_limit_bytes=VMEM_LIMIT),
    )(*ws, *gs, *ms, *vs)
    return outs[:n], outs[n:2 * n], outs[2 * n:]


REPLICATED = ("b_in", "g_q", "g_kv", "w_ukv", "sgu_ln_g", "sgu_ln_b", "w_s", "b_s", "ln_g", "ln_b")
ORDER = ("w_in", "b_in", "g_q", "w_uq", "g_kv", "w_ukv", "w_oa", "sgu_ln_g", "sgu_ln_b", "w_s", "b_s", "w_ob", "w_out",
         "ln_g", "ln_b")


def kernel(x, positions, w_in, b_in, g_q, w_uq, g_kv, w_ukv, w_oa, sgu_ln_g, sgu_ln_b, w_s, b_s, w_ob, w_out, ln_g, ln_b, loss_target, m_w_in, m_b_in, m_g_q, m_w_uq, m_g_kv, m_w_ukv, m_w_oa, m_sgu_ln_g, m_sgu_ln_b, m_w_s, m_b_s, m_w_ob, m_w_out, m_ln_g, m_ln_b, v_w_in, v_b_in, v_g_q, v_w_uq, v_g_kv, v_w_ukv, v_w_oa, v_sgu_ln_g, v_sgu_ln_b, v_w_s, v_b_s, v_w_ob, v_w_out, v_ln_g, v_ln_b):
    w = dict(w_in=w_in, b_in=b_in, g_q=g_q, w_uq=w_uq, g_kv=g_kv, w_ukv=w_ukv, w_oa=w_oa, sgu_ln_g=sgu_ln_g,
             sgu_ln_b=sgu_ln_b, w_s=w_s, b_s=b_s, w_ob=w_ob, w_out=w_out, ln_g=ln_g, ln_b=ln_b)
    m = dict(w_in=m_w_in, b_in=m_b_in, g_q=m_g_q, w_uq=m_w_uq, g_kv=m_g_kv, w_ukv=m_w_ukv, w_oa=m_w_oa,
             sgu_ln_g=m_sgu_ln_g, sgu_ln_b=m_sgu_ln_b, w_s=m_w_s, b_s=m_b_s, w_ob=m_w_ob, w_out=m_w_out, ln_g=m_ln_g,
             ln_b=m_ln_b)
    v = dict(w_in=v_w_in, b_in=v_b_in, g_q=v_g_q, w_uq=v_w_uq, g_kv=v_g_kv, w_ukv=v_w_ukv, w_oa=v_w_oa,
             sgu_ln_g=v_sgu_ln_g, sgu_ln_b=v_sgu_ln_b, w_s=v_w_s, b_s=v_b_s, w_ob=v_w_ob, w_out=v_w_out, ln_g=v_ln_g,
             ln_b=v_ln_b)
    w, m, v = ({n: a[0] for n, a in d.items()} for d in (w, m, v))
    c = lax.axis_index("c")

    wt_shard, mt_shard, vt_shard = (jnp.transpose(d["w_in"]) for d in (w, m, v))
    xi, yi = lax.axis_index("x"), lax.axis_index("y")
    me1 = (2 * xi + yi).reshape(1).astype(jnp.int32)
    first = _first_start(*_cast_first(wt_shard, w["w_uq"].reshape(Q_RANK // 4, HEADS * QK_DIM), me1))
    tables = _rope_tables(positions[0])
    bufs = _cast_own([wt_shard, w["w_oa"], w["w_ob"], w["w_out"]], me1, first[4])
    send0, recv0, bufs, token0 = _gather_start(bufs, first[4])
    g_lat, g_uq = _first_forward(*_first_wait(*first[:4], token0, *tables))
    st = _local_attention(x[0], tables, g_lat, w["b_in"], w["g_q"], g_uq.reshape(Q_RANK, HEADS, QK_DIM),
                          w["g_kv"], w["w_ukv"], token0)
    g_in, g_oa, g_ob, g_out = _gather_finish(_gather_wait(send0, recv0, bufs, st["o"]))
    wt = g_in.reshape(IN_W, D_MODEL)

    loss, early, st = _local_head(
        st, x[0], loss_target[0], wt, g_oa, w["sgu_ln_g"], w["sgu_ln_b"], w["w_s"], w["b_s"], g_ob,
        g_out.reshape(D_MODEL, D_MODEL), w["ln_g"], w["ln_b"])

    c1 = c.reshape(1).astype(jnp.int32)
    idx = jnp.stack([2 * xi + yi, 2 * (1 - xi) + yi, 2 * xi + (1 - yi), 2 * (1 - xi) + (1 - yi), c]).astype(jnp.int32)
    slabs = lambda a: a.reshape(N_CHIPS, IN_W // N_CHIPS, D_MODEL // 2)
    theirs = slabs(_dwt_early(st["dhmt"], st["dhgt"], st["xb2"], 1 - c1, c1, "dwt_theirs"))
    parts1 = [theirs, early["w_oa"].astype(BF16), early["w_ob"].astype(BF16),
              early["w_out"].reshape(N_CHIPS, SLAB_W, D_MODEL).astype(BF16)]
    my_loss = lax.dynamic_update_slice(jnp.zeros((N_DEV,) + LOSS_TILE, F32), jnp.broadcast_to(loss, (1,) + LOSS_TILE),
                                       (4 * xi + 2 * yi + c, 0, 0))
    sems0 = _pairs_start(parts1, my_loss, 1)
    mine = slabs(_dwt_early(st["dhmt"], st["dhgt"], st["xb2"], c1, sems0[5], "dwt_mine"))
    parts1, recv1, all_loss = _pairs_wait(*sems0[:5], mine, 1)
    pairs1 = _add_pair([mine, *parts1[1:]], recv1, c1, "add_pair_early", n_whole=1)
    sems1 = _chips_start(pairs1, "chips_start_early")
    st["delta"] = st["delta"] + sems1[4][0, 0]
    dq, dk, dv = _local_attn_bwd(st)
    dhl, late = _local_tail(st, dq, dk, dv, dv)

    grads = {**early, **late}
    rep = jnp.concatenate([_rows8(grads[n]) for n in REPLICATED], axis=0)
    rep = jnp.pad(rep, ((0, N_CHIPS * REP_ROWS - rep.shape[0]), (0, 0))).reshape(N_CHIPS, REP_ROWS, D_MODEL)
    parts2 = [late["w_uq"].reshape(N_CHIPS, Q_RANK // N_CHIPS, HEADS * QK_DIM).astype(BF16), rep.astype(BF16),
              late["w_lat"].reshape(N_CHIPS, LAT_ROWS_PAD // N_CHIPS, D_MODEL)]
    pairs2 = _add_pair(parts2, _exchange_pairs(parts2, "exchange_pairs_late"), c1, "add_pair_late")
    sems2 = _chips_start(pairs2, "chips_start_late")
    dx = _dx(st["dr"], st["dhg"], st["dhm"], dhl, st["wt"], sems2[4])
    pairs2, landed2 = _chips_wait(*sems2[:4], dx, "chips_wait_late")
    pairs1, landed1 = _chips_wait(*sems1[:4], landed2[0], "chips_wait_early")
    sums = _sum_chips([*pairs1, *pairs2], [*landed1, *landed2], idx, (F32, BF16))
    *shards, g_rep, g_lat = _share(sums[:-2], sums[-2:])
    loss = jnp.sum(all_loss[:, 0, 0])

    red = {n: s.reshape(w[n].shape) for n, s in zip(("w_oa", "w_ob", "w_out", "w_uq"), shards[1:])}
    g_rep = g_rep.reshape(N_CHIPS * REP_ROWS, D_MODEL)
    off = 0
    for n in REPLICATED:
        rows = _rows8(w[n]).shape[0]
        red[n] = g_rep[off:off + rows].reshape(-1)[:w[n].size].reshape(w[n].shape)
        off += rows
    owner = (2 * xi + yi == 0).astype(jnp.int32).reshape(1)
    gt, dt, mt, vt2 = _update_w_in(wt_shard, shards[0], mt_shard, vt_shard,
                                   g_lat.reshape(LAT_ROWS_PAD, D_MODEL).astype(F32), owner, 232)
    red["w_in"] = jnp.transpose(gt)
    small = [n for n in ORDER if n != "w_in"]
    as2d = lambda a: a.reshape(-1, a.shape[-1])
    ds, ms, vs = _update_small([as2d(w[n]) for n in small], [as2d(red[n]) for n in small],
                               [as2d(m[n]) for n in small], [as2d(v[n]) for n in small])
    delta, new_m, new_v = {"w_in": jnp.transpose(dt)}, {"w_in": jnp.transpose(mt)}, {"w_in": jnp.transpose(vt2)}
    for i, n in enumerate(small):
        delta[n], new_m[n], new_v[n] = (a[i].reshape(w[n].shape) for a in (ds, ms, vs))

    lead = lambda a: a[None]
    return (loss, dx[None], *[lead(red[n]) for n in ORDER], *[lead(delta[n]) for n in ORDER],
            *[lead(new_m[n]) for n in ORDER], *[lead(new_v[n]) for n in ORDER])
```

```python
import math

import jax
import jax.numpy as jnp
from jax import lax
from jax.experimental import pallas as pl
from jax.experimental.pallas import tpu as pltpu

F32 = jnp.float32
BF16 = jnp.bfloat16

D_MODEL = 1024
HEADS = 8
Q_RANK = 384
KV_RANK = 128
NOPE = 64
ROPE = 32
V_DIM = 64
QK_DIM = NOPE + ROPE
HEAD_PAD = 128
MLA_W = HEADS * V_DIM
SGU_W = 512
GROUPS = 8
CHUNK = 128
IN_W = 4640
RMS_EPS = 1e-6
LN_EPS = 1e-5
ALPHA = 2.0 ** 0.25
ROPE_THETA = 10000.0
SCALE = QK_DIM ** -0.5

GATE_W = 2 * D_MODEL
MID_W = 4 * SGU_W
LAT_W = Q_RANK + KV_RANK + HEAD_PAD
LAT_COLS = Q_RANK + KV_RANK + ROPE
ROW_GATE = LAT_COLS + MID_W
LAT_ROWS_PAD = 704
N_SLABS = 4
SLAB_W = D_MODEL // N_SLABS

ROW_TILE = 256
MATMUL_ROW_TILE = 512
MID_ROW_TILE = 256
ATT_TK = 256
ATT_BWD_TK = 256
SUM_ROWS = 16
LOG2E = 1.4426950408889634
LN2 = 0.6931471805599453
Q_SCALE = SCALE * LOG2E
VMEM_LIMIT = 56 * 1024 * 1024

ADAM_LR = 0.001
ADAM_B1 = 0.9
ADAM_B2 = 0.999
ADAM_EPS = 1e-08
ADAM_WD = 0.01
ADAM_STEP = 10


def _dot(a, b):
    return jnp.dot(a, b, preferred_element_type=F32)


def _dot_nt(a, b):
    return lax.dot_general(a, b, (((1,), (1,)), ((), ())), preferred_element_type=F32)


def _dot_tn(a, b):
    return lax.dot_general(a, b, (((0,), (0,)), ((), ())), preferred_element_type=F32)


def _sigmoid(z):
    return 0.5 * jnp.tanh(0.5 * z) + 0.5


_GELU_C = math.sqrt(2.0 / math.pi)


def _gelu_and_grad(x):
    x2 = x * x
    t = jnp.tanh(_GELU_C * (x + 0.044715 * x * x2))
    g = 0.5 * x * (1.0 + t)
    dg = 0.5 * (1.0 + t) + 0.5 * x * (1.0 - t * t) * (_GELU_C * (1.0 + 3.0 * 0.044715 * x2))
    return g, dg


def _silu_and_grad(z):
    s = _sigmoid(z)
    return z * s, s * (1.0 + z * (1.0 - s))


def _rope(xb, c, sl, sh):
    return xb * c + pltpu.roll(xb, 112, 1) * sl + pltpu.roll(xb, 16, 1) * sh


def _rope_t(dy, c, sl, sh):
    return dy * c + pltpu.roll(dy * sl, 16, 1) + pltpu.roll(dy * sh, 112, 1)


def _params(sem=("arbitrary",)):
    return pltpu.CompilerParams(dimension_semantics=sem, vmem_limit_bytes=VMEM_LIMIT)


def _row_spec(tile, width):
    return pl.BlockSpec((tile, width), lambda i: (i, 0))


def _full_spec(shape):
    nd = len(shape)
    return pl.BlockSpec(shape, lambda i: (0,) * nd)


def _kpe_rows(wt_ref):
    z = lambda n: jnp.zeros((n, D_MODEL), BF16)
    return jnp.concatenate([z(NOPE), wt_ref[Q_RANK + KV_RANK:LAT_COLS, :], z(HEAD_PAD - QK_DIM)], axis=0)


def _fwd_rest(xb, wt, b_g, b_m):
    s = xb.shape[0]
    ts = MATMUL_ROW_TILE
    tn = D_MODEL
    blocks = ([(ROW_GATE + c0, 0, c0) for c0 in range(0, GATE_W, tn)]
              + [(LAT_COLS + c0, 1, c0) for c0 in range(0, MID_W, tn)])

    def body(xb_ref, wt_hbm, bg_ref, bm_ref, hg_ref, hm_ref, wt_ref, sems):
        copies = [pltpu.make_async_copy(wt_hbm.at[lo:lo + tn], wt_ref.at[lo:lo + tn], sems.at[n])
                  for n, (lo, _, _) in enumerate(blocks)]

        def compute(first):
            xb_ = xb_ref[...]
            for cp, (lo, which, c0) in zip(copies, blocks):
                if first:
                    cp.wait()
                out_ref, b_ref = ((hg_ref, bg_ref), (hm_ref, bm_ref))[which]
                out_ref[:, c0:c0 + tn] = (_dot_nt(xb_, wt_ref[lo:lo + tn, :]) + b_ref[:, c0:c0 + tn]).astype(BF16)

        @pl.when(pl.program_id(0) == 0)
        def _():
            for cp in copies:
                cp.start()
            compute(True)

        @pl.when(pl.program_id(0) > 0)
        def _():
            compute(False)

    return pl.pallas_call(
        body, name="fwd_rest", grid=(s // ts,),
        in_specs=[_row_spec(ts, D_MODEL), HBM_SPEC, _full_spec(b_g.shape), _full_spec(b_m.shape)],
        out_specs=[_row_spec(ts, GATE_W), _row_spec(ts, MID_W)],
        out_shape=[jax.ShapeDtypeStruct((s, GATE_W), BF16), jax.ShapeDtypeStruct((s, MID_W), BF16)],
        scratch_shapes=[pltpu.VMEM(wt.shape, BF16), pltpu.SemaphoreType.DMA((len(blocks),))],
        compiler_params=_params(),
    )(xb, wt, b_g, b_m)


def _fwd_lat(x, wlat, b_l, g_q, wuq, g_kv, wk, wv, rc, rsl, rsh, after):
    s = x.shape[0]
    ts = ROW_TILE

    def body(x_ref, wt_ref, bl_ref, gq_ref, wuq_ref, gkv_ref, wk_ref, wv_ref, rc_ref, rsl_ref,
             rsh_ref, after_ref, hl_ref, q_ref, k_ref, v_ref, xb_ref, qt_ref, kt_ref, vt_ref, xb2_ref):
        xb = x_ref[...].astype(BF16)
        xb_ref[...] = xb
        xb2_ref[0] = xb[:, :D_MODEL // 2]
        xb2_ref[1] = xb[:, D_MODEL // 2:]
        hl = jnp.concatenate([_dot_nt(xb, wt_ref[0:Q_RANK + KV_RANK, :]), _dot_nt(xb, _kpe_rows(wt_ref))],
                             axis=1) + bl_ref[...]
        hl_ref[...] = hl
        c, sl, sh = rc_ref[...], rsl_ref[...], rsh_ref[...]
        cq = hl[:, :Q_RANK]
        cqn = cq * lax.rsqrt(jnp.mean(cq * cq, axis=-1, keepdims=True) + RMS_EPS) * gq_ref[...]
        q = _dot(cqn.astype(BF16), wuq_ref[...])
        ckv = hl[:, Q_RANK:Q_RANK + KV_RANK]
        ckvn = (ckv * lax.rsqrt(jnp.mean(ckv * ckv, axis=-1, keepdims=True) + RMS_EPS) * gkv_ref[...]).astype(BF16)
        k = _dot(ckvn, wk_ref[...])
        vb = _dot(ckvn, wv_ref[...]).astype(BF16)
        v_ref[...] = vb
        vt_ref[...] = vb.T
        kpe = _rope(hl[:, Q_RANK + KV_RANK:], c, sl, sh)
        for hd in range(HEADS):
            lanes = slice(hd * HEAD_PAD, (hd + 1) * HEAD_PAD)
            qb = (_rope(q[:, lanes], c, sl, sh) * Q_SCALE).astype(BF16)
            kb = (k[:, lanes] + kpe).astype(BF16)
            q_ref[:, lanes] = qb
            k_ref[:, lanes] = kb
            qt_ref[lanes, :] = qb.T
            kt_ref[lanes, :] = kb.T

    qk_w = HEADS * HEAD_PAD
    col_spec = lambda rows: pl.BlockSpec((rows, ts), lambda i: (0, i))
    return pl.pallas_call(
        body, name="fwd_lat", grid=(s // ts,),
        in_specs=[_row_spec(ts, D_MODEL), _full_spec(wlat.shape),
                  _full_spec(b_l.shape), _full_spec(g_q.shape),
                  _full_spec(wuq.shape), _full_spec(g_kv.shape), _full_spec(wk.shape), _full_spec(wv.shape),
                  _row_spec(ts, HEAD_PAD), _row_spec(ts, HEAD_PAD), _row_spec(ts, HEAD_PAD),
                  pl.BlockSpec(memory_space=pl.ANY)],
        out_specs=[_row_spec(ts, LAT_W), _row_spec(ts, qk_w),
                   _row_spec(ts, qk_w), _row_spec(ts, MLA_W), _row_spec(ts, D_MODEL), col_spec(qk_w), col_spec(qk_w),
                   col_spec(MLA_W), pl.BlockSpec((2, ts, D_MODEL // 2), lambda i: (0, i, 0))],
        out_shape=[jax.ShapeDtypeStruct((s, LAT_W), F32), jax.ShapeDtypeStruct((s, qk_w), BF16),
                   jax.ShapeDtypeStruct((s, qk_w), BF16), jax.ShapeDtypeStruct((s, MLA_W), BF16),
                   jax.ShapeDtypeStruct((s, D_MODEL), BF16), jax.ShapeDtypeStruct((qk_w, s), BF16),
                   jax.ShapeDtypeStruct((qk_w, s), BF16), jax.ShapeDtypeStruct((MLA_W, s), BF16),
                   jax.ShapeDtypeStruct((2, s, D_MODEL // 2), BF16)],
        compiler_params=_params(),
    )(x, wlat, b_l, g_q, wuq, g_kv, wk, wv, rc, rsl, rsh, after)


def _attn_fwd(qt, k, vt):
    s = k.shape[0]
    tk = ATT_TK
    nk = s // tk
    pairs = HEADS // 2

    def body(qt_ref, k_ref, vt_ref, o_ref, lse_ref):
        qts = [qt_ref[hh * HEAD_PAD:(hh + 1) * HEAD_PAD, :] for hh in range(2)]
        ones = jnp.ones((SUM_ROWS, tk), BF16)

        def scores(j):
            return tuple(_dot(k_ref[j * tk:(j + 1) * tk, hh * HEAD_PAD:(hh + 1) * HEAD_PAD], qts[hh][:, j * tk:])
                         for hh in range(2))

        def weighted(j, ps):
            return tuple(_dot(jnp.concatenate([vt_ref[hh * V_DIM:(hh + 1) * V_DIM, j * tk:(j + 1) * tk], ones], axis=0),
                              ps[hh]) for hh in range(2))

        def from_lane(full, lo, part):
            return part if lo == 0 else jnp.concatenate([full[:, :lo], part], axis=1)

        krow = lax.broadcasted_iota(jnp.int32, (tk, tk), 0)
        qcol = lax.broadcasted_iota(jnp.int32, (tk, tk), 1)
        st = scores(0)
        ps = None
        stats = [(jnp.full((1, s), -jnp.inf, F32), jnp.zeros((V_DIM + SUM_ROWS, s), F32))] * 2
        for j in range(nk):
            lo, lo_prev = j * tk, max(j - 1, 0) * tk
            st_next = scores(j + 1) if j + 1 < nk else None
            pvs = weighted(j - 1, ps) if j else None
            new_ps, new_stats = [], []
            for hh in range(2):
                m, acc = stats[hh]
                diag = jnp.where(krow <= qcol, st[hh][:, :tk], -jnp.inf)
                s_ = diag if j == nk - 1 else jnp.concatenate([diag, st[hh][:, tk:]], axis=1)
                if j:
                    acc = from_lane(acc, lo_prev, acc[:, lo_prev:] + pvs[hh])
                m_old = m[:, lo:]
                m_new = jnp.maximum(m_old, jnp.max(s_, axis=0, keepdims=True))
                a = jnp.exp2(m_old - m_new)
                p = jnp.exp2(s_ - m_new)
                new_stats.append((from_lane(m, lo, m_new), from_lane(acc, lo, a * acc[:, lo:])))
                new_ps.append(p.astype(BF16))
            st, ps, stats = st_next, new_ps, new_stats
        pvs = weighted(nk - 1, ps)
        lo = (nk - 1) * tk
        accs = [from_lane(stats[hh][1], lo, stats[hh][1][:, lo:] + pvs[hh]) for hh in range(2)]
        sums = [acc[V_DIM:V_DIM + 1, :] for acc in accs]
        ot = jnp.concatenate([accs[hh][:V_DIM, :] / sums[hh] for hh in range(2)], axis=0)
        o_ref[...] = ot.T
        lse = [stats[hh][0] + jnp.log(sums[hh]) * LOG2E for hh in range(2)]
        lse_ref[...] = jnp.concatenate(lse + [jnp.zeros((6, s), F32)], axis=0)

    return pl.pallas_call(
        body, name="attn_fwd", grid=(pairs,),
        in_specs=[pl.BlockSpec((2 * HEAD_PAD, s), lambda p: (p, 0)),
                  pl.BlockSpec((s, 2 * HEAD_PAD), lambda p: (0, p)),
                  pl.BlockSpec((2 * V_DIM, s), lambda p: (p, 0))],
        out_specs=[pl.BlockSpec((s, 2 * V_DIM), lambda p: (0, p)),
                   pl.BlockSpec((None, 8, s), lambda p: (p, 0, 0))],
        out_shape=[jax.ShapeDtypeStruct((s, MLA_W), F32), jax.ShapeDtypeStruct((pairs, 8, s), F32)],
        compiler_params=_params(("arbitrary",)),
    )(qt, k, vt)


def _attn_bwd(q, qt, k, kt, v, do, dot, lse, delta):
    s = k.shape[0]
    tk = ATT_BWD_TK
    nk = s // tk
    pairs = HEADS // 2

    def body(q_ref, qt_ref, k_ref, kt_ref, v_ref, do_ref, dot_ref, lse_ref, dl_ref, dqt_ref, dk_ref, dv_ref):
        krow = lax.broadcasted_iota(jnp.int32, (tk, tk), 0)
        qcol = lax.broadcasted_iota(jnp.int32, (tk, tk), 1)
        lane = lax.broadcasted_iota(jnp.int32, (tk, 2 * V_DIM), 1)
        drow = lax.broadcasted_iota(jnp.int32, (2 * V_DIM, s), 0)
        dotb = dot_ref[...]
        dots = [jnp.where((drow < V_DIM) if hh == 0 else (drow >= V_DIM), dotb, jnp.zeros_like(dotb))
                for hh in range(2)]
        for j in range(nk):
            lo = j * tk
            vb = v_ref[lo:lo + tk, :]
            dob = do_ref[lo:, :]
            dvs = []
            for hh in range(2):
                rows = slice(hh * HEAD_PAD, (hh + 1) * HEAD_PAD)
                st = _dot(k_ref[lo:lo + tk, rows], qt_ref[rows, lo:])
                diag = jnp.where(krow <= qcol, st[:, :tk], -jnp.inf)
                st = diag if j == nk - 1 else jnp.concatenate([diag, st[:, tk:]], axis=1)
                p = jnp.exp2(st - lse_ref[hh:hh + 1, lo:])
                dpt = _dot(vb, dots[hh][:, lo:])
                dst = (p * (dpt - dl_ref[hh:hh + 1, lo:])).astype(BF16)
                dvs.append(_dot(p.astype(BF16), dob))
                dk_ref[lo:lo + tk, rows] = _dot(dst, q_ref[lo:, rows]) * LN2
                dqt = _dot(kt_ref[rows, lo:lo + tk], dst)
                if j == 0:
                    dqt_ref[rows, :] = dqt
                else:
                    dqt_ref[rows, lo:] += dqt
            dv_ref[lo:lo + tk, :] = jnp.where(lane < V_DIM, dvs[0], dvs[1])
        dqt_ref[...] = dqt_ref[...] * SCALE

    pair_rows = lambda w: pl.BlockSpec((s, w), lambda p: (0, p))
    pair_cols = lambda w: pl.BlockSpec((w, s), lambda p: (p, 0))
    stats = pl.BlockSpec((None, 8, s), lambda p: (p, 0, 0))
    return pl.pallas_call(
        body, name="attn_bwd", grid=(pairs,),
        in_specs=[pair_rows(2 * HEAD_PAD), pair_cols(2 * HEAD_PAD), pair_rows(2 * HEAD_PAD), pair_cols(2 * HEAD_PAD),
                  pair_rows(2 * V_DIM), pair_rows(2 * V_DIM), pair_cols(2 * V_DIM), stats, stats],
        out_specs=[pair_cols(2 * HEAD_PAD), pair_rows(2 * HEAD_PAD), pair_rows(2 * V_DIM)],
        out_shape=[jax.ShapeDtypeStruct((HEADS * HEAD_PAD, s), F32), jax.ShapeDtypeStruct((s, HEADS * HEAD_PAD), F32),
                   jax.ShapeDtypeStruct((s, MLA_W), F32)],
        compiler_params=_params(("arbitrary",)),
    )(q, qt, k, kt, v, do, dot, lse, delta)


def _split3(a):
    hi = a.astype(BF16)
    r1 = a - hi.astype(F32)
    mid = r1.astype(BF16)
    lo = (r1 - mid.astype(F32)).astype(BF16)
    return hi, mid, lo


def _mid(x, tgt, o, hm, hg, woa, wob, wout, ln_g, ln_b, sg_g, sg_b, w_s, bsb):
    s = x.shape[0]
    ts = MID_ROW_TILE
    nsteps = s // ts
    nch = ts // CHUNK
    npair = GROUPS // 2

    def body(x_ref, t_ref, o_ref, hm_ref, hg_ref, woa_ref, wob_ref, wout_ref, lng_ref, lnb_ref, sgg_ref, sgb_ref,
             ws_ref, bsb_ref,
             dr_ref, dhg_ref, dhm_ref, do_ref, dot_ref, dl_ref, dhgt_ref, dhmt_ref,
             dwout_ref, dwoa_ref, dwob_ref, dws_ref, dbs_ref, dlng_ref, dlnb_ref, dsgg_ref, dsgb_ref, loss_ref,
             dbg_ref, dbm_ref, dbacc_ref, awout_ref, awoa_ref, awob_ref):
        i = pl.program_id(0)

        @pl.when(i == 0)
        def _():
            for r in (awout_ref, awoa_ref, awob_ref, dws_ref, dlng_ref, dlnb_ref, dsgg_ref, dsgb_ref, loss_ref,
                      dbg_ref, dbm_ref, dbacc_ref):
                r[...] = jnp.zeros_like(r)

        def emit(ref, tref, bref, lo, val):
            vb = val.astype(BF16)
            n = val.shape[1]
            ref[:, lo:lo + n] = vb
            tref[lo:lo + n, :] = vb.T
            bref[:, lo:lo + n] += jnp.sum(val, axis=0, keepdims=True)

        lane = lax.broadcasted_iota(jnp.int32, (CHUNK, CHUNK), 1)
        left = lane < V_DIM
        tril = lax.broadcasted_iota(jnp.int32, (CHUNK, CHUNK), 0) >= lane
        ms = [jnp.where(tril, ws_ref[g], 0.0).astype(BF16) for g in range(GROUPS)]

        z_a = hm_ref[:, 0:SGU_W].astype(F32)
        u = hm_ref[:, SGU_W:2 * SGU_W].astype(F32)
        v = hm_ref[:, 2 * SGU_W:3 * SGU_W].astype(F32)
        z_b = hm_ref[:, 3 * SGU_W:4 * SGU_W].astype(F32)
        o = o_ref[...]
        sa, dsa = _silu_and_grad(z_a)
        y_a = (o * sa).astype(BF16)
        gu, dgu = _gelu_and_grad(u)
        gv, dgv = _gelu_and_grad(v)
        mu = jnp.mean(gv, axis=-1, keepdims=True)
        vc = gv - mu
        rstd_v = lax.rsqrt(jnp.mean(vc * vc, axis=-1, keepdims=True) + LN_EPS)
        vhat = vc * rstd_v
        vn = (vhat * sgg_ref[...] + sgb_ref[...]).astype(BF16)
        rows = []
        for c in range(nch):
            blocks = []
            for p in range(npair):
                blk = vn[c * CHUNK:(c + 1) * CHUNK, p * CHUNK:(p + 1) * CHUNK]
                blocks.append(jnp.where(left, _dot(ms[2 * p], blk), _dot(ms[2 * p + 1], blk)))
            rows.append(jnp.concatenate(blocks, axis=1) + bsb_ref[...])
        mixed = jnp.concatenate(rows, axis=0)
        sgu = gu * mixed
        sb, dsb = _silu_and_grad(z_b)
        y_b = (sgu * sb).astype(BF16)
        pa = jnp.concatenate([_dot(y_a, woa_ref[k]) for k in range(N_SLABS)], axis=1)
        pb = jnp.concatenate([_dot(y_b, wob_ref[k]) for k in range(N_SLABS)], axis=1)
        sga = _sigmoid(hg_ref[:, :D_MODEL].astype(F32))
        sgb = _sigmoid(hg_ref[:, D_MODEL:].astype(F32))
        m2 = (sga * pa + sgb * pb).astype(BF16)
        r = ALPHA * x_ref[...] + _dot(m2, wout_ref[...])
        rmu = jnp.mean(r, axis=-1, keepdims=True)
        rc = r - rmu
        rstd = lax.rsqrt(jnp.mean(rc * rc, axis=-1, keepdims=True) + LN_EPS)
        xhat = rc * rstd
        y = xhat * lng_ref[...] + lnb_ref[...]
        err = y - t_ref[...]
        loss_ref[...] += jnp.full(loss_ref.shape, 0.5 / D_MODEL, F32) * jnp.sum(err * err)

        dy = err * (1.0 / D_MODEL)
        dlng_ref[...] += jnp.sum(dy * xhat, axis=0, keepdims=True)
        dlnb_ref[...] += jnp.sum(dy, axis=0, keepdims=True)
        dxh = dy * lng_ref[...]
        dr = rstd * (dxh - jnp.mean(dxh, axis=-1, keepdims=True) - xhat * jnp.mean(dxh * xhat, axis=-1, keepdims=True))
        dr_ref[...] = dr
        drb = dr.astype(BF16)
        awout_ref[...] += _dot_tn(m2, drb)
        dm2 = _dot_nt(drb, wout_ref[...])
        emit(dhg_ref, dhgt_ref, dbg_ref, 0, dm2 * pa * sga * (1.0 - sga))
        emit(dhg_ref, dhgt_ref, dbg_ref, D_MODEL, dm2 * pb * sgb * (1.0 - sgb))
        dpa = (dm2 * sga).astype(BF16)
        dpb = (dm2 * sgb).astype(BF16)
        dy_a = jnp.zeros((ts, MLA_W), F32)
        dy_b = jnp.zeros((ts, SGU_W), F32)
        y_at, y_bt = y_a.T, y_b.T
        for k in range(N_SLABS):
            cols = slice(k * SLAB_W, (k + 1) * SLAB_W)
            awoa_ref[k] += _dot(y_at, dpa[:, cols])
            awob_ref[k] += _dot(y_bt, dpb[:, cols])
            dy_a = dy_a + _dot_nt(dpa[:, cols], woa_ref[k])
            dy_b = dy_b + _dot_nt(dpb[:, cols], wob_ref[k])
        dob = (dy_a * sa).astype(BF16)
        do_ref[...] = dob
        dot_ref[...] = dob.T
        head = (lax.broadcasted_iota(jnp.int32, (HEADS, MLA_W), 1) // V_DIM
                == lax.broadcasted_iota(jnp.int32, (HEADS, MLA_W), 0)).astype(BF16)
        dl_ref[...] = sum(_dot_nt(head, term) for term in _split3(dob.astype(F32) * o))
        emit(dhm_ref, dhmt_ref, dbm_ref, 0, dy_a * o * dsa)
        dsg = dy_b * sb
        emit(dhm_ref, dhmt_ref, dbm_ref, 3 * SGU_W, dy_b * sgu * dsb)
        emit(dhm_ref, dhmt_ref, dbm_ref, SGU_W, dsg * mixed * dgu)
        dmixed = dsg * gu
        dvn_rows = []
        dbs_sum = jnp.zeros((CHUNK, SGU_W), F32)
        for c in range(nch):
            dm_c = dmixed[c * CHUNK:(c + 1) * CHUNK, :]
            dbs_sum = dbs_sum + dm_c
            blocks = []
            for p in range(npair):
                dmb = dm_c[:, p * CHUNK:(p + 1) * CHUNK].astype(BF16)
                blk = vn[c * CHUNK:(c + 1) * CHUNK, p * CHUNK:(p + 1) * CHUNK]
                blocks.append(jnp.where(left, _dot_tn(ms[2 * p], dmb), _dot_tn(ms[2 * p + 1], dmb)))
                zero = jnp.zeros_like(dmb)
                dws_ref[2 * p] += jnp.where(tril, _dot_nt(jnp.where(left, dmb, zero), blk), 0.0)
                dws_ref[2 * p + 1] += jnp.where(tril, _dot_nt(jnp.where(left, zero, dmb), blk), 0.0)
            dvn_rows.append(jnp.concatenate(blocks, axis=1))
        dbacc_ref[...] += dbs_sum
        dvn = jnp.concatenate(dvn_rows, axis=0)
        dsgg_ref[...] += jnp.sum(dvn * vhat, axis=0, keepdims=True)
        dsgb_ref[...] += jnp.sum(dvn, axis=0, keepdims=True)
        dvh = dvn * sgg_ref[...]
        dgv_in = rstd_v * (dvh - jnp.mean(dvh, axis=-1, keepdims=True)
                           - vhat * jnp.mean(dvh * vhat, axis=-1, keepdims=True))
        emit(dhm_ref, dhmt_ref, dbm_ref, 2 * SGU_W, dgv_in * dgv)

        @pl.when(i == nsteps - 1)
        def _():
            dwout_ref[...] = awout_ref[...].astype(BF16)
            dwoa_ref[...] = awoa_ref[...].astype(BF16)
            dwob_ref[...] = awob_ref[...].astype(BF16)
            grp = (lax.broadcasted_iota(jnp.int32, (SGU_W, CHUNK), 0) // V_DIM
                   == lax.broadcasted_iota(jnp.int32, (SGU_W, CHUNK), 1)).astype(BF16)
            hi, mid, lo = _split3(dbacc_ref[...])
            dbs_ref[...] = _dot(hi, grp) + _dot(mid, grp) + _dot(lo, grp)

    acc_shapes = [(D_MODEL, D_MODEL), woa.shape, wob.shape, (GROUPS, CHUNK, CHUNK), (CHUNK, CHUNK),
                  (1, D_MODEL), (1, D_MODEL), (1, SGU_W), (1, SGU_W), (1, 128), (1, GATE_W), (1, MID_W)]
    col_spec = lambda rows: pl.BlockSpec((rows, ts), lambda i: (0, i))
    return pl.pallas_call(
        body, name="mid", grid=(nsteps,),
        in_specs=[_row_spec(ts, D_MODEL), _row_spec(ts, D_MODEL), _row_spec(ts, MLA_W), _row_spec(ts, MID_W),
                  _row_spec(ts, GATE_W), _full_spec(woa.shape), _full_spec(wob.shape), _full_spec(wout.shape),
                  _full_spec(ln_g.shape), _full_spec(ln_b.shape), _full_spec(sg_g.shape), _full_spec(sg_b.shape),
                  _full_spec(w_s.shape), _full_spec(bsb.shape)],
        out_specs=[_row_spec(ts, D_MODEL), _row_spec(ts, GATE_W), _row_spec(ts, MID_W), _row_spec(ts, MLA_W),
                   col_spec(MLA_W), col_spec(HEADS), col_spec(GATE_W), col_spec(MID_W)]
        + [_full_spec(sh) for sh in acc_shapes],
        out_shape=[jax.ShapeDtypeStruct((s, D_MODEL), F32), jax.ShapeDtypeStruct((s, GATE_W), BF16),
                   jax.ShapeDtypeStruct((s, MID_W), BF16), jax.ShapeDtypeStruct((s, MLA_W), BF16),
                   jax.ShapeDtypeStruct((MLA_W, s), BF16), jax.ShapeDtypeStruct((HEADS, s), F32),
                   jax.ShapeDtypeStruct((GATE_W, s), BF16), jax.ShapeDtypeStruct((MID_W, s), BF16)]
        + [jax.ShapeDtypeStruct(sh, BF16 if n < 3 else F32) for n, sh in enumerate(acc_shapes)],
        scratch_shapes=[pltpu.VMEM((CHUNK, SGU_W), F32)] + [pltpu.VMEM(sh, F32) for sh in acc_shapes[:3]],
        compiler_params=_params(),
    )(x, tgt, o, hm, hg, woa, wob, wout, ln_g, ln_b, sg_g, sg_b, w_s, bsb)


def _lat_bwd(dq, dk, dv, hl, rc, rsl, rsh, g_q, g_kv, wuq, wk, wv, after):
    s = dk.shape[0]
    ts = ROW_TILE
    qk_w = HEADS * HEAD_PAD

    def body(dq_ref, dk_ref, dv_ref, hl_ref, rc_ref, rsl_ref, rsh_ref, gq_ref, gkv_ref, wuq_ref, wk_ref, wv_ref,
             after_ref, dhl_ref, dhlt_ref, dwuq_ref, dwk_ref, dwv_ref, dgq_ref, dgkv_ref, dbl_ref):
        i = pl.program_id(0)

        @pl.when(i == 0)
        def _():
            for r in (dwuq_ref, dwk_ref, dwv_ref, dgq_ref, dgkv_ref, dbl_ref):
                r[...] = jnp.zeros_like(r)

        def emit(lo, val):
            vb = val.astype(BF16)
            n = val.shape[1]
            dhl_ref[:, lo:lo + n] = vb
            dhlt_ref[lo:lo + n, :] = vb.T
            dbl_ref[:, lo:lo + n] += jnp.sum(val, axis=0, keepdims=True)

        c, sl, sh = rc_ref[...], rsl_ref[...], rsh_ref[...]
        lane = lax.broadcasted_iota(jnp.int32, (ts, HEAD_PAD), 1)
        pe = (lane >= NOPE) & (lane < QK_DIM)
        dkpe = jnp.zeros((ts, HEAD_PAD), F32)
        dqu = []
        for hd in range(HEADS):
            lanes = slice(hd * HEAD_PAD, (hd + 1) * HEAD_PAD)
            dqu.append(_rope_t(dq_ref[lanes, :].T, c, sl, sh).astype(BF16))
            dkpe = dkpe + dk_ref[:, lanes]
        dqu = jnp.concatenate(dqu, axis=1)
        dkpe = _rope_t(jnp.where(pe, dkpe, 0.0), c, sl, sh)

        cq = hl_ref[:, :Q_RANK]
        rq = lax.rsqrt(jnp.mean(cq * cq, axis=-1, keepdims=True) + RMS_EPS)
        cqh = cq * rq
        cqn = (cqh * gq_ref[...]).astype(BF16)
        dwuq_ref[...] += _dot_tn(cqn, dqu)
        dcqn = _dot_nt(dqu, wuq_ref[...])
        dgq_ref[...] += jnp.sum(dcqn * cqh, axis=0, keepdims=True)
        dch = dcqn * gq_ref[...]
        emit(0, rq * (dch - cqh * jnp.mean(dch * cqh, axis=-1, keepdims=True)))

        ckv = hl_ref[:, Q_RANK:Q_RANK + KV_RANK]
        rk = lax.rsqrt(jnp.mean(ckv * ckv, axis=-1, keepdims=True) + RMS_EPS)
        ckh = ckv * rk
        ckn = (ckh * gkv_ref[...]).astype(BF16)
        dkb = dk_ref[...].astype(BF16)
        dvb = dv_ref[...].astype(BF16)
        dwk_ref[...] += _dot_tn(ckn, dkb)
        dwv_ref[...] += _dot_tn(ckn, dvb)
        dckn = _dot_nt(dkb, wk_ref[...]) + _dot_nt(dvb, wv_ref[...])
        dgkv_ref[...] += jnp.sum(dckn * ckh, axis=0, keepdims=True)
        dkh = dckn * gkv_ref[...]
        emit(Q_RANK, rk * (dkh - ckh * jnp.mean(dkh * ckh, axis=-1, keepdims=True)))
        emit(Q_RANK + KV_RANK, dkpe)

    acc_shapes = [wuq.shape, wk.shape, wv.shape, g_q.shape, g_kv.shape, (1, LAT_W)]
    return pl.pallas_call(
        body, name="lat_bwd", grid=(s // ts,),
        in_specs=[pl.BlockSpec((qk_w, ts), lambda i: (0, i)), _row_spec(ts, qk_w), _row_spec(ts, MLA_W),
                  _row_spec(ts, LAT_W), _row_spec(ts, HEAD_PAD), _row_spec(ts, HEAD_PAD), _row_spec(ts, HEAD_PAD),
                  _full_spec(g_q.shape), _full_spec(g_kv.shape), _full_spec(wuq.shape), _full_spec(wk.shape),
                  _full_spec(wv.shape), pl.BlockSpec(memory_space=pl.ANY)],
        out_specs=[_row_spec(ts, LAT_W), pl.BlockSpec((LAT_W, ts), lambda i: (0, i))]
        + [_full_spec(sh) for sh in acc_shapes],
        out_shape=[jax.ShapeDtypeStruct((s, LAT_W), BF16), jax.ShapeDtypeStruct((LAT_W, s), BF16)]
        + [jax.ShapeDtypeStruct(sh, F32) for sh in acc_shapes],
        compiler_params=_params(),
    )(dq, dk, dv, hl, rc, rsl, rsh, g_q, g_kv, wuq, wk, wv, after)


def _dx(dr, dhg, dhm, dhl, wt, after):
    s = dr.shape[0]
    ts = MATMUL_ROW_TILE

    tk = D_MODEL
    bounds = ([(ROW_GATE + r0, ROW_GATE + r0 + tk) for r0 in range(0, GATE_W, tk)]
              + [(LAT_COLS + r0, LAT_COLS + r0 + tk) for r0 in range(0, MID_W, tk)] + [(0, LAT_COLS)])

    def body(dr_ref, dhg_ref, dhm_ref, dhl_ref, wt_hbm, after_ref, dx_ref, wt_ref, sems):
        copies = [pltpu.make_async_copy(wt_hbm.at[lo:hi], wt_ref.at[lo:hi], sems.at[n])
                  for n, (lo, hi) in enumerate(bounds)]

        def compute(first):
            acc = ALPHA * dr_ref[...]
            for n, (lo, hi) in enumerate(bounds):
                if first:
                    copies[n].wait()
                if lo >= ROW_GATE:
                    acc += _dot(dhg_ref[:, lo - ROW_GATE:hi - ROW_GATE], wt_ref[lo:hi, :])
                elif lo >= LAT_COLS:
                    acc += _dot(dhm_ref[:, lo - LAT_COLS:hi - LAT_COLS], wt_ref[lo:hi, :])
                else:
                    acc += (_dot(dhl_ref[:, 0:Q_RANK + KV_RANK], wt_ref[0:Q_RANK + KV_RANK, :])
                            + _dot(dhl_ref[:, Q_RANK + KV_RANK:], _kpe_rows(wt_ref)))
            dx_ref[...] = acc

        @pl.when(pl.program_id(0) == 0)
        def _():
            for cp in copies:
                cp.start()
            compute(True)

        @pl.when(pl.program_id(0) > 0)
        def _():
            compute(False)

    return pl.pallas_call(
        body, name="dx", grid=(s // ts,),
        in_specs=[_row_spec(ts, D_MODEL), _row_spec(ts, GATE_W), _row_spec(ts, MID_W), _row_spec(ts, LAT_W),
                  HBM_SPEC, HBM_SPEC],
        out_specs=_row_spec(ts, D_MODEL),
        out_shape=jax.ShapeDtypeStruct((s, D_MODEL), F32),
        scratch_shapes=[pltpu.VMEM(wt.shape, BF16), pltpu.SemaphoreType.DMA((len(bounds),))],
        compiler_params=_params(),
    )(dr, dhg, dhm, dhl, wt, after)


def _dwt_early(dhmt, dhgt, xb, col, after, name):
    tn = 512
    nm, ng = MID_W // tn, GATE_W // tn
    s = dhmt.shape[1]
    hc = D_MODEL // 2

    ks = s // 2

    def body(col_ref, dma_ref, dmb_ref, dga_ref, dgb_ref, xba_ref, xbb_ref, after_ref, dw_ref):
        i = pl.program_id(0)

        @pl.when(i < nm)
        def _():
            dw_ref[...] = (_dot(dma_ref[...], xba_ref[...]) + _dot(dmb_ref[...], xbb_ref[...])).astype(BF16)

        @pl.when(i >= nm)
        def _():
            dw_ref[...] = (_dot(dga_ref[...], xba_ref[...]) + _dot(dgb_ref[...], xbb_ref[...])).astype(BF16)

    def dh_spec(first, part):
        if first:
            return pl.BlockSpec((tn, ks), lambda i, col_ref: (jnp.minimum(i, nm - 1), part))
        return pl.BlockSpec((tn, ks), lambda i, col_ref: (jnp.maximum(i - nm, 0), part))

    rows = pl.pallas_call(
        body, name=name,
        grid_spec=pltpu.PrefetchScalarGridSpec(
            num_scalar_prefetch=1, grid=(nm + ng,),
            in_specs=[dh_spec(True, 0), dh_spec(True, 1), dh_spec(False, 0), dh_spec(False, 1),
                      pl.BlockSpec((None, ks, hc), lambda i, col_ref: (col_ref[0], 0, 0)),
                      pl.BlockSpec((None, ks, hc), lambda i, col_ref: (col_ref[0], 1, 0)),
                      pl.BlockSpec(memory_space=pl.ANY)],
            out_specs=pl.BlockSpec((pl.Element(tn), pl.Element(hc)),
                                   lambda i, col_ref: (pl.multiple_of(LAT_COLS + i * tn, 32), 0))),
        out_shape=jax.ShapeDtypeStruct((IN_W, hc), BF16),
        compiler_params=_params(),
    )(col, dhmt, dhmt, dhgt, dhgt, xb, xb, after)

    def zero(buf_ref, out_ref):
        out_ref[...] = jnp.zeros_like(out_ref)

    return pl.pallas_call(
        zero, name=name + "_zero_lat", grid=(1,), in_specs=[pl.BlockSpec(memory_space=pl.ANY)],
        out_specs=pl.BlockSpec((LAT_COLS, hc), lambda i: (0, 0)),
        out_shape=jax.ShapeDtypeStruct((IN_W, hc), BF16), input_output_aliases={0: 0},
    )(rows)


def _dwt_lat(dhlt, xb):
    n, s = dhlt.shape

    def body(dht_ref, xb_ref, dw_ref):
        dw = _dot(dht_ref[...], xb_ref[...]).astype(BF16)
        kpe = Q_RANK + KV_RANK + NOPE
        dw_ref[0:Q_RANK + KV_RANK, :] = dw[0:Q_RANK + KV_RANK]
        dw_ref[Q_RANK + KV_RANK:LAT_COLS, :] = dw[kpe:kpe + ROPE]
        dw_ref[LAT_COLS:, :] = jnp.zeros((LAT_ROWS_PAD - LAT_COLS, D_MODEL), BF16)

    return pl.pallas_call(
        body, name="dwt_lat", in_specs=[VMEM_SPEC, VMEM_SPEC], out_specs=VMEM_SPEC,
        out_shape=jax.ShapeDtypeStruct((LAT_ROWS_PAD, D_MODEL), BF16),
        compiler_params=pltpu.CompilerParams(vmem_limit_bytes=VMEM_LIMIT),
    )(dhlt, xb)


def _split_bias(b):
    z = lambda n: jnp.zeros((n,), b.dtype)
    lat = jnp.concatenate([b[:Q_RANK + KV_RANK], z(NOPE), b[Q_RANK + KV_RANK:LAT_COLS], z(HEAD_PAD - QK_DIM)])
    return b[None, ROW_GATE:], b[None, LAT_COLS:ROW_GATE], lat[None, :]


def _join_bias(g, m, l):
    kpe = Q_RANK + KV_RANK + NOPE
    return jnp.concatenate([l[0, :Q_RANK + KV_RANK], l[0, kpe:kpe + ROPE], m[0], g[0]])


def _rope_tables(positions):
    half = ROPE // 2
    inv_freq = ROPE_THETA ** (-jnp.arange(0, ROPE, 2, dtype=F32) / ROPE)
    ang = positions.astype(F32)[:, None] * inv_freq
    cos, sin = jnp.cos(ang), jnp.sin(ang)
    n = positions.shape[0]
    one, zero = jnp.ones((n, NOPE), F32), jnp.zeros((n, half), F32)
    tail1, tail0 = jnp.ones((n, HEAD_PAD - QK_DIM), F32), jnp.zeros((n, HEAD_PAD - QK_DIM), F32)
    z64 = jnp.zeros((n, NOPE), F32)
    rc = jnp.concatenate([one, cos, cos, tail1], axis=1)
    rsl = jnp.concatenate([z64, -sin, zero, tail0], axis=1)
    rsh = jnp.concatenate([z64, zero, sin, tail0], axis=1)
    return rc, rsl, rsh


def _local_attention(x, tables, wlat, b_in, g_q, w_uq, g_kv, w_ukv, after):
    rc, rsl, rsh = tables
    b_g, b_m, b_l = _split_bias(b_in)
    wuq = jnp.pad(w_uq, ((0, 0), (0, 0), (0, HEAD_PAD - QK_DIM))).reshape(Q_RANK, HEADS * HEAD_PAD).astype(BF16)
    wk = jnp.pad(w_ukv[:, :, :NOPE], ((0, 0), (0, 0), (0, HEAD_PAD - NOPE))).reshape(KV_RANK, HEADS * HEAD_PAD).astype(BF16)
    wv = w_ukv[:, :, NOPE:].reshape(KV_RANK, MLA_W).astype(BF16)
    gq2, gkv2 = g_q[None, :], g_kv[None, :]
    hl, q, k, v, xb, qt, kt, vt, xb2 = _fwd_lat(x, wlat, b_l, gq2, wuq, gkv2, wk, wv, rc, rsl, rsh, after)
    o, lse = _attn_fwd(qt, k, vt)
    return dict(q=q, qt=qt, k=k, kt=kt, v=v, o=o, lse=lse, hl=hl, rc=rc, rsl=rsl, rsh=rsh, gq2=gq2, gkv2=gkv2,
                wuq=wuq, wk=wk, wv=wv, xb=xb, xb2=xb2, b_g=b_g, b_m=b_m)


def _local_head(st, x, tgt, wt, w_oa, sg_g, sg_b, w_s, b_s, w_ob, w_out, ln_g, ln_b):
    q, qt, k, kt, v, o, lse, hl, xb = (st[n] for n in ("q", "qt", "k", "kt", "v", "o", "lse", "hl", "xb"))
    rc, rsl, rsh, gq2, gkv2, wuq, wk, wv = (st[n] for n in ("rc", "rsl", "rsh", "gq2", "gkv2", "wuq", "wk", "wv"))
    bsb = jnp.repeat(b_s.T, V_DIM, axis=1)
    hg, hm = _fwd_rest(xb, wt, st["b_g"], st["b_m"])
    (dr, dhg, dhm, do, dot, delta, dhgt, dhmt, dwout, dwoa, dwob, dws, dbs, dlng, dlnb, dsgg, dsgb, loss, dbg,
     dbm) = _mid(x, tgt, o, hm, hg, w_oa, w_ob, w_out, ln_g[None, :], ln_b[None, :], sg_g[None, :], sg_b[None, :],
                 w_s, bsb)
    delta = jnp.pad(delta.reshape(HEADS // 2, 2, -1), ((0, 0), (0, 6), (0, 0)))
    early = {
        "w_oa": dwoa, "sgu_ln_g": dsgg[0], "sgu_ln_b": dsgb[0], "w_s": dws, "b_s": dbs[:, :GROUPS].T,
        "w_ob": dwob, "w_out": dwout, "ln_g": dlng[0], "ln_b": dlnb[0],
    }
    state = dict(q=q, qt=qt, k=k, kt=kt, v=v, do=do, dot=dot, lse=lse, delta=delta, hl=hl, rc=rc, rsl=rsl, rsh=rsh,
                 gq2=gq2, gkv2=gkv2, wuq=wuq, wk=wk, wv=wv, dr=dr, dhg=dhg, dhm=dhm, wt=wt, xb=xb, dbg=dbg, dbm=dbm,
                 dhgt=dhgt, dhmt=dhmt, xb2=st["xb2"])
    return loss, early, state


def _local_attn_bwd(st):
    return _attn_bwd(st["q"], st["qt"], st["k"], st["kt"], st["v"], st["do"], st["dot"], st["lse"], st["delta"])


def _local_tail(st, dq, dk, dv, after):
    dhl, dhlt, dwuq, dwk, dwv, dgq, dgkv, dbl = _lat_bwd(dq, dk, dv, st["hl"], st["rc"], st["rsl"], st["rsh"],
                                                         st["gq2"], st["gkv2"], st["wuq"], st["wk"], st["wv"], after)
    late = {
        "w_lat": _dwt_lat(dhlt, st["xb"]),
        "b_in": _join_bias(st["dbg"], st["dbm"], dbl),
        "g_q": dgq[0],
        "w_uq": dwuq.reshape(Q_RANK, HEADS, HEAD_PAD)[:, :, :QK_DIM],
        "g_kv": dgkv[0],
        "w_ukv": jnp.concatenate([dwk.reshape(KV_RANK, HEADS, HEAD_PAD)[:, :, :NOPE],
                                  dwv.reshape(KV_RANK, HEADS, V_DIM)], axis=2),
    }
    return dhl, late


def _local_step(x, positions, tgt, wt, b_in, g_q, w_uq, g_kv, w_ukv, w_oa, sg_g, sg_b, w_s, b_s, w_ob, w_out, ln_g,
                ln_b):
    st = _local_attention(x, _rope_tables(positions), wt[:LAT_COLS], b_in, g_q, w_uq, g_kv, w_ukv, b_in)
    loss, early, st = _local_head(st, x, tgt, wt, w_oa, sg_g, sg_b, w_s, b_s, w_ob, w_out, ln_g, ln_b)
    dq, dk, dv = _local_attn_bwd(st)
    dhl, late = _local_tail(st, dq, dk, dv, dv)
    dx = _dx(st["dr"], st["dhg"], st["dhm"], dhl, st["wt"], dhl)
    grads = {**early, **late}
    halves = [_dwt_early(st["dhmt"], st["dhgt"], st["xb2"], jnp.full((1,), h, jnp.int32), dhl, "dwt_half%d" % h)
              for h in range(2)]
    grads["w_in"] = jnp.concatenate([grads.pop("w_lat")[:LAT_COLS], jnp.concatenate(halves, axis=1)[LAT_COLS:]],
                                    axis=0)
    return loss, dx, grads


MESH = pl.DeviceIdType.MESH
N_CHIPS = 4
HBM_SPEC = pl.BlockSpec(memory_space=pl.ANY)
HBM_SPEC_STRICT = pl.BlockSpec(memory_space=pltpu.HBM)
VMEM_SPEC = pl.BlockSpec(memory_space=pltpu.VMEM)

REP_ROWS = 80


def _rows8(a):
    flat = a.reshape(-1)
    n = -(-flat.shape[0] // (8 * D_MODEL)) * 8 * D_MODEL
    return jnp.pad(flat, (0, n - flat.shape[0])).reshape(-1, D_MODEL)


def _place():
    x, y, c = lax.axis_index("x"), lax.axis_index("y"), lax.axis_index("c")
    others = [(1 - x, y), (x, 1 - y), (1 - x, 1 - y)]
    return x, y, c, others


N_DEV = 8
LOSS_TILE = (8, 128)


def _cast_own(shards, me, after):
    n = len(shards)

    def body(me_ref, *refs):
        for w in range(n):
            refs[n + 1 + w][...] = refs[w][...].astype(BF16)

    return pl.pallas_call(
        body, name="cast_own",
        grid_spec=pltpu.PrefetchScalarGridSpec(
            num_scalar_prefetch=1, grid=(1,),
            in_specs=[pl.BlockSpec(s.shape, lambda i, me_ref: (0, 0)) for s in shards]
            + [pl.BlockSpec(memory_space=pl.ANY)],
            out_specs=[pl.BlockSpec((None,) + s.shape, lambda i, me_ref: (me_ref[0], 0, 0)) for s in shards]),
        out_shape=[jax.ShapeDtypeStruct((N_CHIPS,) + s.shape, BF16) for s in shards],
        compiler_params=pltpu.CompilerParams(vmem_limit_bytes=VMEM_LIMIT),
    )(me, *shards, after)


def _cast_first(lat, uq, me):
    def body(me_ref, lat_ref, uq_ref, wlat_ref, guq_ref):
        wlat_ref[...] = lat_ref[...].astype(BF16)
        guq_ref[...] = uq_ref[...].astype(BF16)

    return pl.pallas_call(
        body, name="cast_first",
        grid_spec=pltpu.PrefetchScalarGridSpec(
            num_scalar_prefetch=1, grid=(1,),
            in_specs=[pl.BlockSpec((LAT_COLS, D_MODEL), lambda i, me_ref: (0, 0)),
                      pl.BlockSpec(uq.shape, lambda i, me_ref: (0, 0))],
            out_specs=[pl.BlockSpec((LAT_COLS, D_MODEL), lambda i, me_ref: (0, 0)),
                       pl.BlockSpec((None,) + uq.shape, lambda i, me_ref: (me_ref[0], 0, 0))]),
        out_shape=[jax.ShapeDtypeStruct((LAT_COLS, D_MODEL), BF16),
                   jax.ShapeDtypeStruct((N_CHIPS,) + uq.shape, BF16)],
    )(me, lat, uq)


def _first_copies(wlat_ref, guq_ref, send_sems, recv_sems, shapes):
    x, y, c, others = _place()
    me = 2 * x + y
    hl, hu = shapes[0][1] // 2, shapes[1][2] // 2
    lat_half = wlat_ref.at[:, pl.ds(c * hl, hl)]

    def copy(src, dst, k, to):
        return pltpu.make_async_remote_copy(src_ref=src, dst_ref=dst, send_sem=send_sems.at[k],
                                            recv_sem=recv_sems.at[k], device_id=to, device_id_type=MESH)

    def uq_half(chip):
        return guq_ref.at[chip, :, pl.ds(c * hu, hu)]

    lat_out = [copy(lat_half, lat_half, j, (*others[j], c)) for j in range(3)]
    uq_out = [copy(uq_half(me), uq_half(me), 3 + j, (*others[j], c)) for j in range(3)]
    j0 = jnp.maximum(x + 2 * y - 1, 0)
    lat_in = copy(lat_half, lat_half, j0, (0, 0, c))
    uq_in = [copy(uq_half(me), uq_half(2 * px + py), 3 + j, (px, py, c)) for j, (px, py) in enumerate(others)]
    return me, lat_out, uq_out, lat_in, uq_in


def _first_start(wlat, guq):
    shapes = (wlat.shape, guq.shape)

    def body(wlat_ref, guq_ref, send_sems, recv_sems, wlat_thru, guq_thru, token):
        me, lat_out, uq_out, _, _ = _first_copies(wlat_ref, guq_ref, send_sems, recv_sems, shapes)

        @pl.when(me == 0)
        def _():
            for cp in lat_out:
                cp.start()

        for cp in uq_out:
            cp.start()
        token[...] = jnp.zeros_like(token)

    outs = pl.pallas_call(
        body, name="first_start",
        out_shape=(pltpu.SemaphoreType.DMA((6,)), pltpu.SemaphoreType.DMA((6,)), pltpu.HBM(wlat.shape, BF16),
                   pltpu.HBM(guq.shape, BF16), jax.ShapeDtypeStruct(LOSS_TILE, F32)),
        in_specs=[HBM_SPEC_STRICT] * 2, out_specs=(SEM_SPEC, SEM_SPEC, HBM_SPEC_STRICT, HBM_SPEC_STRICT, VMEM_SPEC),
        input_output_aliases={0: 2, 1: 3},
        compiler_params=pltpu.CompilerParams(has_side_effects=SPLIT_EFFECT),
    )(pltpu.with_memory_space_constraint(wlat, pltpu.HBM), pltpu.with_memory_space_constraint(guq, pltpu.HBM))
    return outs


def _first_wait(send_sems, recv_sems, wlat, guq, *after):
    shapes = (wlat.shape, guq.shape)

    def body(wlat_ref, guq_ref, send_sems, recv_sems, *rest):
        me, lat_out, uq_out, lat_in, uq_in = _first_copies(wlat_ref, guq_ref, send_sems, recv_sems, shapes)

        @pl.when(me == 0)
        def _():
            for cp in lat_out:
                cp.wait_send()

        @pl.when(me != 0)
        def _():
            lat_in.wait_recv()

        for cp in uq_out:
            cp.wait_send()
        for cp in uq_in:
            cp.wait_recv()

    return pl.pallas_call(
        body, name="first_wait", out_shape=(pltpu.HBM(wlat.shape, BF16), pltpu.HBM(guq.shape, BF16)),
        in_specs=[HBM_SPEC_STRICT, HBM_SPEC_STRICT, SEM_SPEC, SEM_SPEC] + [HBM_SPEC] * len(after),
        out_specs=(HBM_SPEC_STRICT, HBM_SPEC_STRICT), input_output_aliases={0: 0, 1: 1},
        compiler_params=pltpu.CompilerParams(has_side_effects=SPLIT_EFFECT),
    )(wlat, guq, send_sems, recv_sems, *after)


def _first_forward(wlat, guq):
    hl, hu = wlat.shape[1] // 2, guq.shape[2] // 2

    def body(wlat_in, guq_in, wlat_ref, guq_ref, send_sems, recv_sems):
        x, y, c, others = _place()
        me = 2 * x + y
        sibling = (x, y, 1 - c)

        def copy(part, k):
            return pltpu.make_async_remote_copy(src_ref=part, dst_ref=part, send_sem=send_sems.at[k],
                                                recv_sem=recv_sems.at[k], device_id=sibling, device_id_type=MESH)

        def uq_part(j, half):
            px, py = others[j]
            return guq_ref.at[2 * px + py, :, pl.ds(half * hu, hu)]

        cps = [copy(uq_part(j, c), j) for j in range(3)]
        for cp in cps:
            cp.start()

        @pl.when(me != 0)
        def _():
            mine = copy(wlat_ref.at[:, pl.ds(c * hl, hl)], 3)
            mine.start()
            copy(wlat_ref.at[:, pl.ds((1 - c) * hl, hl)], 3).wait_recv()
            mine.wait_send()

        for j in range(3):
            copy(uq_part(j, 1 - c), j).wait_recv()
        for cp in cps:
            cp.wait_send()

    return pl.pallas_call(
        body, name="first_forward", in_specs=[HBM_SPEC, HBM_SPEC], out_specs=[HBM_SPEC, HBM_SPEC],
        out_shape=[jax.ShapeDtypeStruct(wlat.shape, BF16), jax.ShapeDtypeStruct(guq.shape, BF16)],
        input_output_aliases={0: 0, 1: 1},
        scratch_shapes=[pltpu.SemaphoreType.DMA((4,)), pltpu.SemaphoreType.DMA((4,))],
    )(wlat, guq)


def _gather_start(bufs, after):
    n = len(bufs)

    def body(*refs):
        b_refs = refs[:n]
        send_sems, recv_sems, token = refs[n + 1], refs[n + 2], refs[-1]
        x, y, c, others = _place()
        me = 2 * x + y
        for w in range(n):
            hc = _half(bufs[w])
            mine = b_refs[w].at[me, :, pl.ds(c * hc, hc)]
            for j, (px, py) in enumerate(others):
                pltpu.make_async_remote_copy(
                    src_ref=mine, dst_ref=mine, send_sem=send_sems.at[3 * w + j], recv_sem=recv_sems.at[3 * w + j],
                    device_id=(px, py, c), device_id_type=MESH).start()
        token[...] = jnp.zeros_like(token)

    hbm = [pltpu.HBM(b.shape, BF16) for b in bufs]
    outs = pl.pallas_call(
        body, name="gather_start",
        out_shape=(pltpu.SemaphoreType.DMA((3 * n,)), pltpu.SemaphoreType.DMA((3 * n,)), *hbm,
                   jax.ShapeDtypeStruct(LOSS_TILE, F32)),
        in_specs=[HBM_SPEC_STRICT] * n + [HBM_SPEC],
        out_specs=(SEM_SPEC, SEM_SPEC, *[HBM_SPEC_STRICT] * n, VMEM_SPEC),
        input_output_aliases={i: 2 + i for i in range(n)},
        compiler_params=pltpu.CompilerParams(has_side_effects=SPLIT_EFFECT),
    )(*[pltpu.with_memory_space_constraint(b, pltpu.HBM) for b in bufs], after)
    return outs[0], outs[1], list(outs[2:2 + n]), outs[-1]


def _gather_wait(send_sems, recv_sems, bufs, after):
    n = len(bufs)

    def body(*refs):
        b_refs = refs[:n]
        send_sems, recv_sems = refs[n], refs[n + 1]
        x, y, c, others = _place()
        me = 2 * x + y
        for w in range(n):
            hc = _half(bufs[w])
            for j, (px, py) in enumerate(others):
                cp = pltpu.make_async_remote_copy(
                    src_ref=b_refs[w].at[me, :, pl.ds(c * hc, hc)],
                    dst_ref=b_refs[w].at[2 * px + py, :, pl.ds(c * hc, hc)],
                    send_sem=send_sems.at[3 * w + j], recv_sem=recv_sems.at[3 * w + j], device_id=(px, py, c),
                    device_id_type=MESH)
                cp.wait_send()
                cp.wait_recv()

    outs = pl.pallas_call(
        body, name="gather_wait", out_shape=tuple(pltpu.HBM(b.shape, b.dtype) for b in bufs),
        in_specs=[HBM_SPEC_STRICT] * n + [SEM_SPEC, SEM_SPEC, HBM_SPEC],
        out_specs=tuple([HBM_SPEC_STRICT] * n), input_output_aliases={i: i for i in range(n)},
        compiler_params=pltpu.CompilerParams(has_side_effects=SPLIT_EFFECT),
    )(*bufs, send_sems, recv_sems, after)
    return list(outs)


def _gather_finish(bufs):
    n = len(bufs)

    def body(*refs):
        b_refs = refs[n:2 * n]
        send_sems, recv_sems = refs[2 * n:]
        x, y, c, others = _place()
        cps = []
        for w in range(n):
            hc = _half(bufs[w])
            for j, (px, py) in enumerate(others):
                part = b_refs[w].at[2 * px + py, :, pl.ds(c * hc, hc)]
                cps.append(pltpu.make_async_remote_copy(
                    src_ref=part, dst_ref=part, send_sem=send_sems.at[3 * w + j], recv_sem=recv_sems.at[3 * w + j],
                    device_id=(x, y, 1 - c), device_id_type=MESH))
        for cp in cps:
            cp.start()
        for w in range(n):
            hc = _half(bufs[w])
            for j, (px, py) in enumerate(others):
                theirs = b_refs[w].at[2 * px + py, :, pl.ds((1 - c) * hc, hc)]
                pltpu.make_async_remote_copy(
                    src_ref=theirs, dst_ref=theirs, send_sem=send_sems.at[3 * w + j], recv_sem=recv_sems.at[3 * w + j],
                    device_id=(x, y, 1 - c), device_id_type=MESH).wait_recv()
        for cp in cps:
            cp.wait_send()

    return pl.pallas_call(
        body, name="gather_finish", in_specs=[HBM_SPEC] * n, out_specs=[HBM_SPEC] * n,
        out_shape=[jax.ShapeDtypeStruct(b.shape, b.dtype) for b in bufs],
        input_output_aliases={i: i for i in range(n)},
        scratch_shapes=[pltpu.SemaphoreType.DMA((3 * n,)), pltpu.SemaphoreType.DMA((3 * n,))],
    )(*bufs)


def _half(a):
    return a.shape[-1] // 2


def _exchange_pairs(parts, name):
    n = len(parts)

    def body(*refs):
        p_refs, r_refs = refs[:n], refs[n:2 * n]
        send_sems, recv_sems = refs[2 * n:]
        x, y, c, _ = _place()
        cps = []
        for w in range(n):
            h = _half(parts[w])
            cps.append(pltpu.make_async_remote_copy(
                src_ref=p_refs[w].at[:, :, pl.ds((1 - c) * h, h)], dst_ref=r_refs[w],
                send_sem=send_sems.at[w], recv_sem=recv_sems.at[w], device_id=(x, y, 1 - c), device_id_type=MESH))
        for cp in cps:
            cp.start()
        for cp in cps:
            cp.wait()

    return pl.pallas_call(
        body, name=name, in_specs=[HBM_SPEC] * n, out_specs=[HBM_SPEC] * n,
        out_shape=[jax.ShapeDtypeStruct((N_CHIPS, p.shape[1], _half(p)), BF16) for p in parts],
        scratch_shapes=[pltpu.SemaphoreType.DMA((n,)), pltpu.SemaphoreType.DMA((n,))],
    )(*parts)


def _sibling_part(ref, w, n_whole, shape, c):
    if w < n_whole:
        return ref
    h = shape[-1] // 2
    return ref.at[:, :, pl.ds((1 - c) * h, h)]


def _pairs_start(parts, all_loss, n_whole):
    n = len(parts)

    def body(*refs):
        p_refs, r_refs, loss_ref = refs[:n], refs[n:2 * n], refs[2 * n]
        send_sems, recv_sems, token = refs[2 * n + 1], refs[2 * n + 2], refs[-1]
        x, y, c, _ = _place()
        for w in range(n):
            h = _half(parts[w])
            pltpu.make_async_remote_copy(
                src_ref=_sibling_part(p_refs[w], w, n_whole, parts[w].shape, c), dst_ref=r_refs[w],
                send_sem=send_sems.at[w], recv_sem=recv_sems.at[w], device_id=(x, y, 1 - c),
                device_id_type=MESH).start()
        me = 4 * x + 2 * y + c
        for t in range(1, N_DEV):
            d = (me + t) % N_DEV
            pltpu.make_async_remote_copy(
                src_ref=loss_ref.at[me], dst_ref=loss_ref.at[me], send_sem=send_sems.at[n + t - 1],
                recv_sem=recv_sems.at[n + t - 1], device_id=(d // 4, (d // 2) % 2, d % 2), device_id_type=MESH).start()
        token[...] = jnp.zeros_like(token)

    lands = [pltpu.HBM(p.shape if w < n_whole else (N_CHIPS, p.shape[1], _half(p)), BF16)
             for w, p in enumerate(parts)]
    nsem = n + N_DEV - 1
    outs = pl.pallas_call(
        body, name="pairs_start",
        out_shape=(pltpu.SemaphoreType.DMA((nsem,)), pltpu.SemaphoreType.DMA((nsem,)),
                   *[pltpu.HBM(p.shape, p.dtype) for p in parts], *lands, pltpu.HBM(all_loss.shape, F32),
                   jax.ShapeDtypeStruct(LOSS_TILE, F32)),
        in_specs=[HBM_SPEC_STRICT] * (2 * n + 1),
        out_specs=(SEM_SPEC, SEM_SPEC, *[HBM_SPEC_STRICT] * (2 * n + 1), VMEM_SPEC),
        input_output_aliases={i: 2 + i for i in range(2 * n + 1)},
        compiler_params=pltpu.CompilerParams(has_side_effects=SPLIT_EFFECT),
    )(*[pltpu.with_memory_space_constraint(p, pltpu.HBM) for p in parts],
      *[pltpu.with_memory_space_constraint(lax.empty(l.shape, BF16), pltpu.HBM) for l in lands],
      pltpu.with_memory_space_constraint(all_loss, pltpu.HBM))
    return outs[0], outs[1], list(outs[2:2 + n]), list(outs[2 + n:2 + 2 * n]), outs[2 + 2 * n], outs[-1]


def _pairs_wait(send_sems, recv_sems, parts, lands, all_loss, after, n_whole):
    n = len(parts)

    def body(*refs):
        p_refs, r_refs, loss_ref = refs[:n], refs[n:2 * n], refs[2 * n]
        send_sems, recv_sems = refs[2 * n + 1], refs[2 * n + 2]
        x, y, c, _ = _place()
        for w in range(n):
            h = _half(parts[w])
            cp = pltpu.make_async_remote_copy(
                src_ref=_sibling_part(p_refs[w], w, n_whole, parts[w].shape, c), dst_ref=r_refs[w],
                send_sem=send_sems.at[w],
                recv_sem=recv_sems.at[w], device_id=(x, y, 1 - c), device_id_type=MESH)
            cp.wait_send()
            cp.wait_recv()
        me = 4 * x + 2 * y + c
        for t in range(1, N_DEV):
            d = (me + N_DEV - t) % N_DEV
            cp = pltpu.make_async_remote_copy(
                src_ref=loss_ref.at[me], dst_ref=loss_ref.at[d], send_sem=send_sems.at[n + t - 1],
                recv_sem=recv_sems.at[n + t - 1], device_id=(d // 4, (d // 2) % 2, d % 2), device_id_type=MESH)
            cp.wait_send()
            cp.wait_recv()

    bufs = (*parts, *lands, all_loss)
    outs = pl.pallas_call(
        body, name="pairs_wait", out_shape=tuple(pltpu.HBM(a.shape, a.dtype) for a in bufs),
        in_specs=[HBM_SPEC_STRICT] * len(bufs) + [SEM_SPEC, SEM_SPEC, HBM_SPEC],
        out_specs=tuple([HBM_SPEC_STRICT] * len(bufs)), input_output_aliases={i: i for i in range(len(bufs))},
        compiler_params=pltpu.CompilerParams(has_side_effects=SPLIT_EFFECT),
    )(*bufs, send_sems, recv_sems, after)
    return list(outs[:n]), list(outs[n:2 * n]), outs[2 * n]


def _add_pair(ps, rs, c, name, n_whole=0):
    n = len(ps)

    def body(c_ref, *refs):
        for w in range(n):
            if w < n_whole:
                mine = refs[w][...]
            else:
                h = _half(ps[w])
                mine = refs[w][:, :, pl.ds(pl.multiple_of(c_ref[0] * h, 128), h)]
            refs[2 * n + w][...] = (mine.astype(F32) + refs[n + w][...].astype(F32)).astype(BF16)

    return pl.pallas_call(
        body, name=name,
        in_specs=[pl.BlockSpec(memory_space=pltpu.SMEM)] + [VMEM_SPEC] * (2 * n), out_specs=[VMEM_SPEC] * n,
        out_shape=[jax.ShapeDtypeStruct(r.shape, BF16) for r in rs],
        compiler_params=pltpu.CompilerParams(vmem_limit_bytes=VMEM_LIMIT),
    )(c, *ps, *rs)


SEM_SPEC = pl.BlockSpec(memory_space=pltpu.SEMAPHORE)
SPLIT_EFFECT = pltpu.SideEffectType.DATAFLOW_SIDE_EFFECTING


def _chips_start(qs, name):
    n = len(qs)

    def body(*refs):
        q_refs, land_refs = refs[:n], refs[n:2 * n]
        send_sems, recv_sems, token = refs[2 * n], refs[2 * n + 1], refs[-1]
        x, y, c, others = _place()
        me = 2 * x + y
        for w in range(n):
            for j, (px, py) in enumerate(others):
                pltpu.make_async_remote_copy(
                    src_ref=q_refs[w].at[2 * px + py], dst_ref=land_refs[w].at[me], send_sem=send_sems.at[3 * w + j],
                    recv_sem=recv_sems.at[3 * w + j], device_id=(px, py, c), device_id_type=MESH).start()
        token[...] = jnp.zeros_like(token)

    hbm = [pltpu.HBM(q.shape, BF16) for q in qs]
    outs = pl.pallas_call(
        body, name=name,
        out_shape=(pltpu.SemaphoreType.DMA((3 * n,)), pltpu.SemaphoreType.DMA((3 * n,)), *hbm, *hbm,
                   jax.ShapeDtypeStruct(LOSS_TILE, F32)),
        in_specs=[HBM_SPEC_STRICT] * (2 * n),
        out_specs=(SEM_SPEC, SEM_SPEC, *[HBM_SPEC_STRICT] * (2 * n), VMEM_SPEC),
        input_output_aliases={i: 2 + i for i in range(2 * n)},
        compiler_params=pltpu.CompilerParams(has_side_effects=SPLIT_EFFECT),
    )(*[pltpu.with_memory_space_constraint(q, pltpu.HBM) for q in qs],
      *[pltpu.with_memory_space_constraint(lax.empty(q.shape, BF16), pltpu.HBM) for q in qs])
    return outs[0], outs[1], outs[2:2 + n], outs[2 + n:2 + 2 * n], outs[-1]


def _chips_wait(send_sems, recv_sems, q_thru, land_thru, after, name):
    n = len(q_thru)

    def body(*refs):
        q_refs, land_refs = refs[:n], refs[n:2 * n]
        send_sems, recv_sems = refs[2 * n], refs[2 * n + 1]
        x, y, c, others = _place()
        me = 2 * x + y
        for w in range(n):
            for j, (px, py) in enumerate(others):
                cp = pltpu.make_async_remote_copy(
                    src_ref=q_refs[w].at[2 * px + py], dst_ref=land_refs[w].at[2 * px + py],
                    send_sem=send_sems.at[3 * w + j], recv_sem=recv_sems.at[3 * w + j], device_id=(px, py, c),
                    device_id_type=MESH)
                cp.wait_send()
                cp.wait_recv()

    outs = pl.pallas_call(
        body, name=name, out_shape=tuple(pltpu.HBM(a.shape, a.dtype) for a in (*q_thru, *land_thru)),
        in_specs=[HBM_SPEC_STRICT] * (2 * n) + [SEM_SPEC, SEM_SPEC, HBM_SPEC],
        out_specs=tuple([HBM_SPEC_STRICT] * (2 * n)), input_output_aliases={i: i for i in range(2 * n)},
        compiler_params=pltpu.CompilerParams(has_side_effects=SPLIT_EFFECT),
    )(*q_thru, *land_thru, send_sems, recv_sems, after)
    return list(outs[:n]), list(outs[n:])


def _sum_chips(qs, rs, idx, all_dtypes):
    n = len(rs)
    n_all = len(all_dtypes)

    def body(idx_ref, *refs):
        c = idx_ref[4]
        for w in range(n):
            q_ref, r_ref, g_ref = refs[w], refs[n + w], refs[2 * n + w]
            acc = q_ref[idx_ref[0]].astype(F32)
            for t in range(1, N_CHIPS):
                acc = acc + r_ref[idx_ref[t]].astype(F32)
            h = rs[w].shape[2]
            mine = pl.ds(pl.multiple_of(c * h, 128), h)
            g_ref[...] = jnp.zeros_like(g_ref)
            if w >= n - n_all:
                g_ref[idx_ref[0], :, mine] = acc.astype(g_ref.dtype)
            else:
                g_ref[:, mine] = acc

    shapes = [jax.ShapeDtypeStruct((r.shape[1], 2 * r.shape[2]), F32) for r in rs[:n - n_all]]
    shapes += [jax.ShapeDtypeStruct((N_CHIPS, r.shape[1], 2 * r.shape[2]), dt)
               for r, dt in zip(rs[n - n_all:], all_dtypes)]
    return pl.pallas_call(
        body, name="sum_chips",
        in_specs=[pl.BlockSpec(memory_space=pltpu.SMEM)] + [VMEM_SPEC] * (2 * n), out_specs=[VMEM_SPEC] * n,
        out_shape=shapes, compiler_params=pltpu.CompilerParams(vmem_limit_bytes=VMEM_LIMIT),
    )(idx, *qs, *rs)


def _share(shards, alls):
    n, na = len(shards), len(alls)
    total = n + na

    def body(*refs):
        g_refs, a_refs = refs[total:total + n], refs[total + n:2 * total]
        send_sems, recv_sems = refs[2 * total:]
        x, y, c, others = _place()
        me = 2 * x + y
        sibling = (x, y, 1 - c)

        def cols_of(w, half):
            h = shards[w].shape[1] // 2
            return g_refs[w].at[:, pl.ds(half * h, h)]

        def slab(a, chip, half):
            h = alls[a].shape[2] // 2
            return a_refs[a].at[chip, :, pl.ds(half * h, h)]

        def copy(src, dst, k, to):
            return pltpu.make_async_remote_copy(src_ref=src, dst_ref=dst, send_sem=send_sems.at[k],
                                                recv_sem=recv_sems.at[k], device_id=to, device_id_type=MESH)

        cps = [copy(cols_of(w, c), cols_of(w, c), w, sibling) for w in range(n)]
        for a in range(na):
            base = n + 7 * a
            cps.append(copy(slab(a, me, c), slab(a, me, c), base, sibling))
            for j, (px, py) in enumerate(others):
                cps.append(copy(slab(a, me, c), slab(a, me, c), base + 1 + j, (px, py, c)))
        for cp in cps:
            cp.start()
        fwd = []
        for a in range(na):
            base = n + 7 * a
            for j, (px, py) in enumerate(others):
                chip = 2 * px + py
                copy(slab(a, me, c), slab(a, chip, c), base + 1 + j, (px, py, c)).wait_recv()
                cp = copy(slab(a, chip, c), slab(a, chip, c), base + 4 + j, sibling)
                cp.start()
                fwd.append(cp)
        for a in range(na):
            base = n + 7 * a
            for j, (px, py) in enumerate(others):
                chip = 2 * px + py
                copy(slab(a, chip, c), slab(a, chip, 1 - c), base + 4 + j, sibling).wait_recv()
            copy(slab(a, me, c), slab(a, me, 1 - c), base, sibling).wait_recv()
        for w in range(n):
            copy(cols_of(w, c), cols_of(w, 1 - c), w, sibling).wait_recv()
        for cp in cps + fwd:
            cp.wait_send()

    nsem = n + 7 * na
    return pl.pallas_call(
        body, name="share", in_specs=[HBM_SPEC] * total, out_specs=[HBM_SPEC] * total,
        out_shape=[jax.ShapeDtypeStruct(a.shape, a.dtype) for a in (*shards, *alls)],
        input_output_aliases={i: i for i in range(total)},
        scratch_shapes=[pltpu.SemaphoreType.DMA((nsem,)), pltpu.SemaphoreType.DMA((nsem,))],
    )(*shards, *alls)


def _adamw(w, g, m, v):
    m2 = ADAM_B1 * m + (1.0 - ADAM_B1) * g
    v2 = ADAM_B2 * v + (1.0 - ADAM_B2) * (g * g)
    m_hat = m2 / (1.0 - ADAM_B1 ** ADAM_STEP)
    v_hat = v2 / (1.0 - ADAM_B2 ** ADAM_STEP)
    return -ADAM_LR * (m_hat / (jnp.sqrt(v_hat) + ADAM_EPS) + ADAM_WD * w), m2, v2


def _update_w_in(wt, gt, mt, vt, lat, owner, tile):
    nlat = lat.shape[0] // tile

    def body(owner_ref, w_ref, g_ref, m_ref, v_ref, lat_ref, g2_ref, d_ref, m2_ref, v2_ref):
        row = pl.program_id(0) * tile + lax.broadcasted_iota(jnp.int32, (tile, 1), 0)
        g = jnp.where((row < LAT_COLS) & (owner_ref[0] == 1), lat_ref[...].astype(F32), g_ref[...])
        g2_ref[...] = g
        d_ref[...], m2_ref[...], v2_ref[...] = _adamw(w_ref[...], g, m_ref[...], v_ref[...])

    spec = pl.BlockSpec((tile, wt.shape[1]), lambda i, o: (i, 0))
    return pl.pallas_call(
        body, name="update_w_in",
        grid_spec=pltpu.PrefetchScalarGridSpec(
            num_scalar_prefetch=1, grid=(wt.shape[0] // tile,),
            in_specs=[spec] * 4 + [pl.BlockSpec((tile, wt.shape[1]), lambda i, o: (jnp.minimum(i, nlat - 1), 0))],
            out_specs=[spec] * 4),
        out_shape=[jax.ShapeDtypeStruct(wt.shape, F32)] * 4,
        compiler_params=_params(("parallel",)),
    )(owner, wt, gt, mt, vt, lat)


def _update_small(ws, gs, ms, vs):
    n = len(ws)

    def body(*refs):
        for k in range(n):
            w_ref, g_ref, m_ref, v_ref = refs[k], refs[n + k], refs[2 * n + k], refs[3 * n + k]
            d, m2, v2 = _adamw(w_ref[...], g_ref[...], m_ref[...], v_ref[...])
            refs[4 * n + k][...] = d
            refs[5 * n + k][...] = m2
            refs[6 * n + k][...] = v2

    shapes = [jax.ShapeDtypeStruct(w.shape, F32) for w in ws]
    outs = pl.pallas_call(
        body, name="update_small", in_specs=[VMEM_SPEC] * (4 * n), out_specs=[VMEM_SPEC] * (3 * n),
        out_shape=shapes * 3,
        compiler_params=pltpu.CompilerParams(vmem_limit_bytes=VMEM_LIMIT),
    )(*ws, *gs, *ms, *vs)
    return outs[:n], outs[n:2 * n], outs[2 * n:]


REPLICATED = ("b_in", "g_q", "g_kv", "w_ukv", "sgu_ln_g", "sgu_ln_b", "w_s", "b_s", "ln_g", "ln_b")
ORDER = ("w_in", "b_in", "g_q", "w_uq", "g_kv", "w_ukv", "w_oa", "sgu_ln_g", "sgu_ln_b", "w_s", "b_s", "w_ob", "w_out",
         "ln_g", "ln_b")


def kernel(x, positions, w_in, b_in, g_q, w_uq, g_kv, w_ukv, w_oa, sgu_ln_g, sgu_ln_b, w_s, b_s, w_ob, w_out, ln_g, ln_b, loss_target, m_w_in, m_b_in, m_g_q, m_w_uq, m_g_kv, m_w_ukv, m_w_oa, m_sgu_ln_g, m_sgu_ln_b, m_w_s, m_b_s, m_w_ob, m_w_out, m_ln_g, m_ln_b, v_w_in, v_b_in, v_g_q, v_w_uq, v_g_kv, v_w_ukv, v_w_oa, v_sgu_ln_g, v_sgu_ln_b, v_w_s, v_b_s, v_w_ob, v_w_out, v_ln_g, v_ln_b):
    w = dict(w_in=w_in, b_in=b_in, g_q=g_q, w_uq=w_uq, g_kv=g_kv, w_ukv=w_ukv, w_oa=w_oa, sgu_ln_g=sgu_ln_g,
             sgu_ln_b=sgu_ln_b, w_s=w_s, b_s=b_s, w_ob=w_ob, w_out=w_out, ln_g=ln_g, ln_b=ln_b)
    m = dict(w_in=m_w_in, b_in=m_b_in, g_q=m_g_q, w_uq=m_w_uq, g_kv=m_g_kv, w_ukv=m_w_ukv, w_oa=m_w_oa,
             sgu_ln_g=m_sgu_ln_g, sgu_ln_b=m_sgu_ln_b, w_s=m_w_s, b_s=m_b_s, w_ob=m_w_ob, w_out=m_w_out, ln_g=m_ln_g,
             ln_b=m_ln_b)
    v = dict(w_in=v_w_in, b_in=v_b_in, g_q=v_g_q, w_uq=v_w_uq, g_kv=v_g_kv, w_ukv=v_w_ukv, w_oa=v_w_oa,
             sgu_ln_g=v_sgu_ln_g, sgu_ln_b=v_sgu_ln_b, w_s=v_w_s, b_s=v_b_s, w_ob=v_w_ob, w_out=v_w_out, ln_g=v_ln_g,
             ln_b=v_ln_b)
    w, m, v = ({n: a[0] for n, a in d.items()} for d in (w, m, v))
    c = lax.axis_index("c")

    wt_shard, mt_shard, vt_shard = (jnp.transpose(d["w_in"]) for d in (w, m, v))
    xi, yi = lax.axis_index("x"), lax.axis_index("y")
    me1 = (2 * xi + yi).reshape(1).astype(jnp.int32)
    first = _first_start(*_cast_first(wt_shard, w["w_uq"].reshape(Q_RANK // 4, HEADS * QK_DIM), me1))
    tables = _rope_tables(positions[0])
    bufs = _cast_own([wt_shard, w["w_oa"], w["w_ob"], w["w_out"]], me1, first[4])
    send0, recv0, bufs, token0 = _gather_start(bufs, first[4])
    g_lat, g_uq = _first_forward(*_first_wait(*first[:4], token0, *tables))
    st = _local_attention(x[0], tables, g_lat, w["b_in"], w["g_q"], g_uq.reshape(Q_RANK, HEADS, QK_DIM),
                          w["g_kv"], w["w_ukv"], token0)
    g_in, g_oa, g_ob, g_out = _gather_finish(_gather_wait(send0, recv0, bufs, st["o"]))
    wt = g_in.reshape(IN_W, D_MODEL)

    loss, early, st = _local_head(
        st, x[0], loss_target[0], wt, g_oa, w["sgu_ln_g"], w["sgu_ln_b"], w["w_s"], w["b_s"], g_ob,
        g_out.reshape(D_MODEL, D_MODEL), w["ln_g"], w["ln_b"])

    c1 = c.reshape(1).astype(jnp.int32)
    idx = jnp.stack([2 * xi + yi, 2 * (1 - xi) + yi, 2 * xi + (1 - yi), 2 * (1 - xi) + (1 - yi), c]).astype(jnp.int32)
    slabs = lambda a: a.reshape(N_CHIPS, IN_W // N_CHIPS, D_MODEL // 2)
    theirs = slabs(_dwt_early(st["dhmt"], st["dhgt"], st["xb2"], 1 - c1, c1, "dwt_theirs"))
    parts1 = [theirs, early["w_oa"].astype(BF16), early["w_ob"].astype(BF16),
              early["w_out"].reshape(N_CHIPS, SLAB_W, D_MODEL).astype(BF16)]
    my_loss = lax.dynamic_update_slice(jnp.zeros((N_DEV,) + LOSS_TILE, F32), jnp.broadcast_to(loss, (1,) + LOSS_TILE),
                                       (4 * xi + 2 * yi + c, 0, 0))
    sems0 = _pairs_start(parts1, my_loss, 1)
    mine = slabs(_dwt_early(st["dhmt"], st["dhgt"], st["xb2"], c1, sems0[5], "dwt_mine"))
    parts1, recv1, all_loss = _pairs_wait(*sems0[:5], mine, 1)
    pairs1 = _add_pair([mine, *parts1[1:]], recv1, c1, "add_pair_early", n_whole=1)
    sems1 = _chips_start(pairs1, "chips_start_early")
    st["delta"] = st["delta"] + sems1[4][0, 0]
    dq, dk, dv = _local_attn_bwd(st)
    dhl, late = _local_tail(st, dq, dk, dv, dv)

    grads = {**early, **late}
    rep = jnp.concatenate([_rows8(grads[n]) for n in REPLICATED], axis=0)
    rep = jnp.pad(rep, ((0, N_CHIPS * REP_ROWS - rep.shape[0]), (0, 0))).reshape(N_CHIPS, REP_ROWS, D_MODEL)
    parts2 = [late["w_uq"].reshape(N_CHIPS, Q_RANK // N_CHIPS, HEADS * QK_DIM).astype(BF16), rep.astype(BF16),
              late["w_lat"].reshape(N_CHIPS, LAT_ROWS_PAD // N_CHIPS, D_MODEL)]
    pairs2 = _add_pair(parts2, _exchange_pairs(parts2, "exchange_pairs_late"), c1, "add_pair_late")
    sems2 = _chips_start(pairs2, "chips_start_late")
    dx = _dx(st["dr"], st["dhg"], st["dhm"], dhl, st["wt"], sems2[4])
    pairs2, landed2 = _chips_wait(*sems2[:4], dx, "chips_wait_late")
    pairs1, landed1 = _chips_wait(*sems1[:4], landed2[0], "chips_wait_early")
    sums = _sum_chips([*pairs1, *pairs2], [*landed1, *landed2], idx, (F32, BF16))
    *shards, g_rep, g_lat = _share(sums[:-2], sums[-2:])
    loss = jnp.sum(all_loss[:, 0, 0])

    red = {n: s.reshape(w[n].shape) for n, s in zip(("w_oa", "w_ob", "w_out", "w_uq"), shards[1:])}
    g_rep = g_rep.reshape(N_CHIPS * REP_ROWS, D_MODEL)
    off = 0
    for n in REPLICATED:
        rows = _rows8(w[n]).shape[0]
        red[n] = g_rep[off:off + rows].reshape(-1)[:w[n].size].reshape(w[n].shape)
        off += rows
    owner = (2 * xi + yi == 0).astype(jnp.int32).reshape(1)
    gt, dt, mt, vt2 = _update_w_in(wt_shard, shards[0], mt_shard, vt_shard,
                                   g_lat.reshape(LAT_ROWS_PAD, D_MODEL).astype(F32), owner, 232)
    red["w_in"] = jnp.transpose(gt)
    small = [n for n in ORDER if n != "w_in"]
    as2d = lambda a: a.reshape(-1, a.shape[-1])
    ds, ms, vs = _update_small([as2d(w[n]) for n in small], [as2d(red[n]) for n in small],
                               [as2d(m[n]) for n in small], [as2d(v[n]) for n in small])
    delta, new_m, new_v = {"w_in": jnp.transpose(dt)}, {"w_in": jnp.transpose(mt)}, {"w_in": jnp.transpose(vt2)}
    for i, n in enumerate(small):
        delta[n], new_m[n], new_v[n] = (a[i].reshape(w[n].shape) for a in (ds, ms, vs))

    lead = lambda a: a[None]
    return (loss, dx[None], *[lead(red[n]) for n in ORDER], *[lead(delta[n]) for n in ORDER],
            *[lead(new_m[n]) for n in ORDER], *[lead(new_v[n]) for n in ORDER])
```

```python
import math

import jax
import jax.numpy as jnp
from jax import lax
from jax.experimental import pallas as pl
from jax.experimental.pallas import tpu as pltpu

F32 = jnp.float32
BF16 = jnp.bfloat16

D_MODEL = 1024
HEADS = 8
Q_RANK = 384
KV_RANK = 128
NOPE = 64
ROPE = 32
V_DIM = 64
QK_DIM = NOPE + ROPE
HEAD_PAD = 128
MLA_W = HEADS * V_DIM
SGU_W = 512
GROUPS = 8
CHUNK = 128
IN_W = 4640
RMS_EPS = 1e-6
LN_EPS = 1e-5
ALPHA = 2.0 ** 0.25
ROPE_THETA = 10000.0
SCALE = QK_DIM ** -0.5

GATE_W = 2 * D_MODEL
MID_W = 4 * SGU_W
LAT_W = Q_RANK + KV_RANK + HEAD_PAD
LAT_COLS = Q_RANK + KV_RANK + ROPE
ROW_GATE = LAT_COLS + MID_W
LAT_ROWS_PAD = 704
N_SLABS = 4
SLAB_W = D_MODEL // N_SLABS

ROW_TILE = 256
MATMUL_ROW_TILE = 512
MID_ROW_TILE = 256
ATT_TK = 256
ATT_BWD_TK = 256
SUM_ROWS = 16
LOG2E = 1.4426950408889634
LN2 = 0.6931471805599453
Q_SCALE = SCALE * LOG2E
VMEM_LIMIT = 56 * 1024 * 1024

ADAM_LR = 0.001
ADAM_B1 = 0.9
ADAM_B2 = 0.999
ADAM_EPS = 1e-08
ADAM_WD = 0.01
ADAM_STEP = 10


def _dot(a, b):
    return jnp.dot(a, b, preferred_element_type=F32)


def _dot_nt(a, b):
    return lax.dot_general(a, b, (((1,), (1,)), ((), ())), preferred_element_type=F32)


def _dot_tn(a, b):
    return lax.dot_general(a, b, (((0,), (0,)), ((), ())), preferred_element_type=F32)


def _sigmoid(z):
    return 0.5 * jnp.tanh(0.5 * z) + 0.5


_GELU_C = math.sqrt(2.0 / math.pi)


def _gelu_and_grad(x):
    x2 = x * x
    t = jnp.tanh(_GELU_C * (x + 0.044715 * x * x2))
    g = 0.5 * x * (1.0 + t)
    dg = 0.5 * (1.0 + t) + 0.5 * x * (1.0 - t * t) * (_GELU_C * (1.0 + 3.0 * 0.044715 * x2))
    return g, dg


def _silu_and_grad(z):
    s = _sigmoid(z)
    return z * s, s * (1.0 + z * (1.0 - s))


def _rope(xb, c, sl, sh):
    return xb * c + pltpu.roll(xb, 112, 1) * sl + pltpu.roll(xb, 16, 1) * sh


def _rope_t(dy, c, sl, sh):
    return dy * c + pltpu.roll(dy * sl, 16, 1) + pltpu.roll(dy * sh, 112, 1)


def _params(sem=("arbitrary",)):
    return pltpu.CompilerParams(dimension_semantics=sem, vmem_limit_bytes=VMEM_LIMIT)


def _row_spec(tile, width):
    return pl.BlockSpec((tile, width), lambda i: (i, 0))


def _full_spec(shape):
    nd = len(shape)
    return pl.BlockSpec(shape, lambda i: (0,) * nd)


def _kpe_rows(wt_ref):
    z = lambda n: jnp.zeros((n, D_MODEL), BF16)
    return jnp.concatenate([z(NOPE), wt_ref[Q_RANK + KV_RANK:LAT_COLS, :], z(HEAD_PAD - QK_DIM)], axis=0)


def _fwd_rest(xb, wt, b_g, b_m):
    s = xb.shape[0]
    ts = MATMUL_ROW_TILE
    tn = D_MODEL
    blocks = ([(ROW_GATE + c0, 0, c0) for c0 in range(0, GATE_W, tn)]
              + [(LAT_COLS + c0, 1, c0) for c0 in range(0, MID_W, tn)])

    def body(xb_ref, wt_hbm, bg_ref, bm_ref, hg_ref, hm_ref, wt_ref, sems):
        copies = [pltpu.make_async_copy(wt_hbm.at[lo:lo + tn], wt_ref.at[lo:lo + tn], sems.at[n])
                  for n, (lo, _, _) in enumerate(blocks)]

        def compute(first):
            xb_ = xb_ref[...]
            for cp, (lo, which, c0) in zip(copies, blocks):
                if first:
                    cp.wait()
                out_ref, b_ref = ((hg_ref, bg_ref), (hm_ref, bm_ref))[which]
                out_ref[:, c0:c0 + tn] = (_dot_nt(xb_, wt_ref[lo:lo + tn, :]) + b_ref[:, c0:c0 + tn]).astype(BF16)

        @pl.when(pl.program_id(0) == 0)
        def _():
            for cp in copies:
                cp.start()
            compute(True)

        @pl.when(pl.program_id(0) > 0)
        def _():
            compute(False)

    return pl.pallas_call(
        body, name="fwd_rest", grid=(s // ts,),
        in_specs=[_row_spec(ts, D_MODEL), HBM_SPEC, _full_spec(b_g.shape), _full_spec(b_m.shape)],
        out_specs=[_row_spec(ts, GATE_W), _row_spec(ts, MID_W)],
        out_shape=[jax.ShapeDtypeStruct((s, GATE_W), BF16), jax.ShapeDtypeStruct((s, MID_W), BF16)],
        scratch_shapes=[pltpu.VMEM(wt.shape, BF16), pltpu.SemaphoreType.DMA((len(blocks),))],
        compiler_params=_params(),
    )(xb, wt, b_g, b_m)


def _fwd_lat(x, wlat, b_l, g_q, wuq, g_kv, wk, wv, rc, rsl, rsh, after):
    s = x.shape[0]
    ts = ROW_TILE

    def body(x_ref, wt_ref, bl_ref, gq_ref, wuq_ref, gkv_ref, wk_ref, wv_ref, rc_ref, rsl_ref,
             rsh_ref, after_ref, hl_ref, q_ref, k_ref, v_ref, xb_ref, qt_ref, kt_ref, vt_ref, xb2_ref):
        xb = x_ref[...].astype(BF16)
        xb_ref[...] = xb
        xb2_ref[0] = xb[:, :D_MODEL // 2]
        xb2_ref[1] = xb[:, D_MODEL // 2:]
        hl = jnp.concatenate([_dot_nt(xb, wt_ref[0:Q_RANK + KV_RANK, :]), _dot_nt(xb, _kpe_rows(wt_ref))],
                             axis=1) + bl_ref[...]
        hl_ref[...] = hl
        c, sl, sh = rc_ref[...], rsl_ref[...], rsh_ref[...]
        cq = hl[:, :Q_RANK]
        cqn = cq * lax.rsqrt(jnp.mean(cq * cq, axis=-1, keepdims=True) + RMS_EPS) * gq_ref[...]
        q = _dot(cqn.astype(BF16), wuq_ref[...])
        ckv = hl[:, Q_RANK:Q_RANK + KV_RANK]
        ckvn = (ckv * lax.rsqrt(jnp.mean(ckv * ckv, axis=-1, keepdims=True) + RMS_EPS) * gkv_ref[...]).astype(BF16)
        k = _dot(ckvn, wk_ref[...])
        vb = _dot(ckvn, wv_ref[...]).astype(BF16)
        v_ref[...] = vb
        vt_ref[...] = vb.T
        kpe = _rope(hl[:, Q_RANK + KV_RANK:], c, sl, sh)
        for hd in range(HEADS):
            lanes = slice(hd * HEAD_PAD, (hd + 1) * HEAD_PAD)
            qb = (_rope(q[:, lanes], c, sl, sh) * Q_SCALE).astype(BF16)
            kb = (k[:, lanes] + kpe).astype(BF16)
            q_ref[:, lanes] = qb
            k_ref[:, lanes] = kb
            qt_ref[lanes, :] = qb.T
            kt_ref[lanes, :] = kb.T

    qk_w = HEADS * HEAD_PAD
    col_spec = lambda rows: pl.BlockSpec((rows, ts), lambda i: (0, i))
    return pl.pallas_call(
        body, name="fwd_lat", grid=(s // ts,),
        in_specs=[_row_spec(ts, D_MODEL), _full_spec(wlat.shape),
                  _full_spec(b_l.shape), _full_spec(g_q.shape),
                  _full_spec(wuq.shape), _full_spec(g_kv.shape), _full_spec(wk.shape), _full_spec(wv.shape),
                  _row_spec(ts, HEAD_PAD), _row_spec(ts, HEAD_PAD), _row_spec(ts, HEAD_PAD),
                  pl.BlockSpec(memory_space=pl.ANY)],
        out_specs=[_row_spec(ts, LAT_W), _row_spec(ts, qk_w),
                   _row_spec(ts, qk_w), _row_spec(ts, MLA_W), _row_spec(ts, D_MODEL), col_spec(qk_w), col_spec(qk_w),
                   col_spec(MLA_W), pl.BlockSpec((2, ts, D_MODEL // 2), lambda i: (0, i, 0))],
        out_shape=[jax.ShapeDtypeStruct((s, LAT_W), F32), jax.ShapeDtypeStruct((s, qk_w), BF16),
                   jax.ShapeDtypeStruct((s, qk_w), BF16), jax.ShapeDtypeStruct((s, MLA_W), BF16),
                   jax.ShapeDtypeStruct((s, D_MODEL), BF16), jax.ShapeDtypeStruct((qk_w, s), BF16),
                   jax.ShapeDtypeStruct((qk_w, s), BF16), jax.ShapeDtypeStruct((MLA_W, s), BF16),
                   jax.ShapeDtypeStruct((2, s, D_MODEL // 2), BF16)],
        compiler_params=_params(),
    )(x, wlat, b_l, g_q, wuq, g_kv, wk, wv, rc, rsl, rsh, after)


def _attn_fwd(qt, k, vt):
    s = k.shape[0]
    tk = ATT_TK
    nk = s // tk
    pairs = HEADS // 2

    def body(qt_ref, k_ref, vt_ref, o_ref, lse_ref):
        qts = [qt_ref[hh * HEAD_PAD:(hh + 1) * HEAD_PAD, :] for hh in range(2)]
        ones = jnp.ones((SUM_ROWS, tk), BF16)

        def scores(j):
            return tuple(_dot(k_ref[j * tk:(j + 1) * tk, hh * HEAD_PAD:(hh + 1) * HEAD_PAD], qts[hh][:, j * tk:])
                         for hh in range(2))

        def weighted(j, ps):
            return tuple(_dot(jnp.concatenate([vt_ref[hh * V_DIM:(hh + 1) * V_DIM, j * tk:(j + 1) * tk], ones], axis=0),
                              ps[hh]) for hh in range(2))

        def from_lane(full, lo, part):
            return part if lo == 0 else jnp.concatenate([full[:, :lo], part], axis=1)

        krow = lax.broadcasted_iota(jnp.int32, (tk, tk), 0)
        qcol = lax.broadcasted_iota(jnp.int32, (tk, tk), 1)
        st = scores(0)
        ps = None
        stats = [(jnp.full((1, s), -jnp.inf, F32), jnp.zeros((V_DIM + SUM_ROWS, s), F32))] * 2
        for j in range(nk):
            lo, lo_prev = j * tk, max(j - 1, 0) * tk
            st_next = scores(j + 1) if j + 1 < nk else None
            pvs = weighted(j - 1, ps) if j else None
            new_ps, new_stats = [], []
            for hh in range(2):
                m, acc = stats[hh]
                diag = jnp.where(krow <= qcol, st[hh][:, :tk], -jnp.inf)
                s_ = diag if j == nk - 1 else jnp.concatenate([diag, st[hh][:, tk:]], axis=1)
                if j:
                    acc = from_lane(acc, lo_prev, acc[:, lo_prev:] + pvs[hh])
                m_old = m[:, lo:]
                m_new = jnp.maximum(m_old, jnp.max(s_, axis=0, keepdims=True))
                a = jnp.exp2(m_old - m_new)
                p = jnp.exp2(s_ - m_new)
                new_stats.append((from_lane(m, lo, m_new), from_lane(acc, lo, a * acc[:, lo:])))
                new_ps.append(p.astype(BF16))
            st, ps, stats = st_next, new_ps, new_stats
        pvs = weighted(nk - 1, ps)
        lo = (nk - 1) * tk
        accs = [from_lane(stats[hh][1], lo, stats[hh][1][:, lo:] + pvs[hh]) for hh in range(2)]
        sums = [acc[V_DIM:V_DIM + 1, :] for acc in accs]
        ot = jnp.concatenate([accs[hh][:V_DIM, :] / sums[hh] for hh in range(2)], axis=0)
        o_ref[...] = ot.T
        lse = [stats[hh][0] + jnp.log(sums[hh]) * LOG2E for hh in range(2)]
        lse_ref[...] = jnp.concatenate(lse + [jnp.zeros((6, s), F32)], axis=0)

    return pl.pallas_call(
        body, name="attn_fwd", grid=(pairs,),
        in_specs=[pl.BlockSpec((2 * HEAD_PAD, s), lambda p: (p, 0)),
                  pl.BlockSpec((s, 2 * HEAD_PAD), lambda p: (0, p)),
                  pl.BlockSpec((2 * V_DIM, s), lambda p: (p, 0))],
        out_specs=[pl.BlockSpec((s, 2 * V_DIM), lambda p: (0, p)),
                   pl.BlockSpec((None, 8, s), lambda p: (p, 0, 0))],
        out_shape=[jax.ShapeDtypeStruct((s, MLA_W), F32), jax.ShapeDtypeStruct((pairs, 8, s), F32)],
        compiler_params=_params(("arbitrary",)),
    )(qt, k, vt)


def _attn_bwd(q, qt, k, kt, v, do, dot, lse, delta):
    s = k.shape[0]
    tk = ATT_BWD_TK
    nk = s // tk
    pairs = HEADS // 2

    def body(q_ref, qt_ref, k_ref, kt_ref, v_ref, do_ref, dot_ref, lse_ref, dl_ref, dqt_ref, dk_ref, dv_ref):
        krow = lax.broadcasted_iota(jnp.int32, (tk, tk), 0)
        qcol = lax.broadcasted_iota(jnp.int32, (tk, tk), 1)
        lane = lax.broadcasted_iota(jnp.int32, (tk, 2 * V_DIM), 1)
        drow = lax.broadcasted_iota(jnp.int32, (2 * V_DIM, s), 0)
        dotb = dot_ref[...]
        dots = [jnp.where((drow < V_DIM) if hh == 0 else (drow >= V_DIM), dotb, jnp.zeros_like(dotb))
                for hh in range(2)]
        for j in range(nk):
            lo = j * tk
            vb = v_ref[lo:lo + tk, :]
            dob = do_ref[lo:, :]
            dvs = []
            for hh in range(2):
                rows = slice(hh * HEAD_PAD, (hh + 1) * HEAD_PAD)
                st = _dot(k_ref[lo:lo + tk, rows], qt_ref[rows, lo:])
                diag = jnp.where(krow <= qcol, st[:, :tk], -jnp.inf)
                st = diag if j == nk - 1 else jnp.concatenate([diag, st[:, tk:]], axis=1)
                p = jnp.exp2(st - lse_ref[hh:hh + 1, lo:])
                dpt = _dot(vb, dots[hh][:, lo:])
                dst = (p * (dpt - dl_ref[hh:hh + 1, lo:])).astype(BF16)
                dvs.append(_dot(p.astype(BF16), dob))
                dk_ref[lo:lo + tk, rows] = _dot(dst, q_ref[lo:, rows]) * LN2
                dqt = _dot(kt_ref[rows, lo:lo + tk], dst)
                if j == 0:
                    dqt_ref[rows, :] = dqt
                else:
                    dqt_ref[rows, lo:] += dqt
            dv_ref[lo:lo + tk, :] = jnp.where(lane < V_DIM, dvs[0], dvs[1])
        dqt_ref[...] = dqt_ref[...] * SCALE

    pair_rows = lambda w: pl.BlockSpec((s, w), lambda p: (0, p))
    pair_cols = lambda w: pl.BlockSpec((w, s), lambda p: (p, 0))
    stats = pl.BlockSpec((None, 8, s), lambda p: (p, 0, 0))
    return pl.pallas_call(
        body, name="attn_bwd", grid=(pairs,),
        in_specs=[pair_rows(2 * HEAD_PAD), pair_cols(2 * HEAD_PAD), pair_rows(2 * HEAD_PAD), pair_cols(2 * HEAD_PAD),
                  pair_rows(2 * V_DIM), pair_rows(2 * V_DIM), pair_cols(2 * V_DIM), stats, stats],
        out_specs=[pair_cols(2 * HEAD_PAD), pair_rows(2 * HEAD_PAD), pair_rows(2 * V_DIM)],
        out_shape=[jax.ShapeDtypeStruct((HEADS * HEAD_PAD, s), F32), jax.ShapeDtypeStruct((s, HEADS * HEAD_PAD), F32),
                   jax.ShapeDtypeStruct((s, MLA_W), F32)],
        compiler_params=_params(("arbitrary",)),
    )(q, qt, k, kt, v, do, dot, lse, delta)


def _split3(a):
    hi = a.astype(BF16)
    r1 = a - hi.astype(F32)
    mid = r1.astype(BF16)
    lo = (r1 - mid.astype(F32)).astype(BF16)
    return hi, mid, lo


def _mid(x, tgt, o, hm, hg, woa, wob, wout, ln_g, ln_b, sg_g, sg_b, w_s, bsb):
    s = x.shape[0]
    ts = MID_ROW_TILE
    nsteps = s // ts
    nch = ts // CHUNK
    npair = GROUPS // 2

    def body(x_ref, t_ref, o_ref, hm_ref, hg_ref, woa_ref, wob_ref, wout_ref, lng_ref, lnb_ref, sgg_ref, sgb_ref,
             ws_ref, bsb_ref,
             dr_ref, dhg_ref, dhm_ref, do_ref, dot_ref, dl_ref, dhgt_ref, dhmt_ref,
             dwout_ref, dwoa_ref, dwob_ref, dws_ref, dbs_ref, dlng_ref, dlnb_ref, dsgg_ref, dsgb_ref, loss_ref,
             dbg_ref, dbm_ref, dbacc_ref, awout_ref, awoa_ref, awob_ref):
        i = pl.program_id(0)

        @pl.when(i == 0)
        def _():
            for r in (awout_ref, awoa_ref, awob_ref, dws_ref, dlng_ref, dlnb_ref, dsgg_ref, dsgb_ref, loss_ref,
                      dbg_ref, dbm_ref, dbacc_ref):
                r[...] = jnp.zeros_like(r)

        def emit(ref, tref, bref, lo, val):
            vb = val.astype(BF16)
            n = val.shape[1]
            ref[:, lo:lo + n] = vb
            tref[lo:lo + n, :] = vb.T
            bref[:, lo:lo + n] += jnp.sum(val, axis=0, keepdims=True)

        lane = lax.broadcasted_iota(jnp.int32, (CHUNK, CHUNK), 1)
        left = lane < V_DIM
        tril = lax.broadcasted_iota(jnp.int32, (CHUNK, CHUNK), 0) >= lane
        ms = [jnp.where(tril, ws_ref[g], 0.0).astype(BF16) for g in range(GROUPS)]

        z_a = hm_ref[:, 0:SGU_W].astype(F32)
        u = hm_ref[:, SGU_W:2 * SGU_W].astype(F32)
        v = hm_ref[:, 2 * SGU_W:3 * SGU_W].astype(F32)
        z_b = hm_ref[:, 3 * SGU_W:4 * SGU_W].astype(F32)
        o = o_ref[...]
        sa, dsa = _silu_and_grad(z_a)
        y_a = (o * sa).astype(BF16)
        gu, dgu = _gelu_and_grad(u)
        gv, dgv = _gelu_and_grad(v)
        mu = jnp.mean(gv, axis=-1, keepdims=True)
        vc = gv - mu
        rstd_v = lax.rsqrt(jnp.mean(vc * vc, axis=-1, keepdims=True) + LN_EPS)
        vhat = vc * rstd_v
        vn = (vhat * sgg_ref[...] + sgb_ref[...]).astype(BF16)
        rows = []
        for c in range(nch):
            blocks = []
            for p in range(npair):
                blk = vn[c * CHUNK:(c + 1) * CHUNK, p * CHUNK:(p + 1) * CHUNK]
                blocks.append(jnp.where(left, _dot(ms[2 * p], blk), _dot(ms[2 * p + 1], blk)))
            rows.append(jnp.concatenate(blocks, axis=1) + bsb_ref[...])
        mixed = jnp.concatenate(rows, axis=0)
        sgu = gu * mixed
        sb, dsb = _silu_and_grad(z_b)
        y_b = (sgu * sb).astype(BF16)
        pa = jnp.concatenate([_dot(y_a, woa_ref[k]) for k in range(N_SLABS)], axis=1)
        pb = jnp.concatenate([_dot(y_b, wob_ref[k]) for k in range(N_SLABS)], axis=1)
        sga = _sigmoid(hg_ref[:, :D_MODEL].astype(F32))
        sgb = _sigmoid(hg_ref[:, D_MODEL:].astype(F32))
        m2 = (sga * pa + sgb * pb).astype(BF16)
        r = ALPHA * x_ref[...] + _dot(m2, wout_ref[...])
        rmu = jnp.mean(r, axis=-1, keepdims=True)
        rc = r - rmu
        rstd = lax.rsqrt(jnp.mean(rc * rc, axis=-1, keepdims=True) + LN_EPS)
        xhat = rc * rstd
        y = xhat * lng_ref[...] + lnb_ref[...]
        err = y - t_ref[...]
        loss_ref[...] += jnp.full(loss_ref.shape, 0.5 / D_MODEL, F32) * jnp.sum(err * err)

        dy = err * (1.0 / D_MODEL)
        dlng_ref[...] += jnp.sum(dy * xhat, axis=0, keepdims=True)
        dlnb_ref[...] += jnp.sum(dy, axis=0, keepdims=True)
        dxh = dy * lng_ref[...]
        dr = rstd * (dxh - jnp.mean(dxh, axis=-1, keepdims=True) - xhat * jnp.mean(dxh * xhat, axis=-1, keepdims=True))
        dr_ref[...] = dr
        drb = dr.astype(BF16)
        awout_ref[...] += _dot_tn(m2, drb)
        dm2 = _dot_nt(drb, wout_ref[...])
        emit(dhg_ref, dhgt_ref, dbg_ref, 0, dm2 * pa * sga * (1.0 - sga))
        emit(dhg_ref, dhgt_ref, dbg_ref, D_MODEL, dm2 * pb * sgb * (1.0 - sgb))
        dpa = (dm2 * sga).astype(BF16)
        dpb = (dm2 * sgb).astype(BF16)
        dy_a = jnp.zeros((ts, MLA_W), F32)
        dy_b = jnp.zeros((ts, SGU_W), F32)
        y_at, y_bt = y_a.T, y_b.T
        for k in range(N_SLABS):
            cols = slice(k * SLAB_W, (k + 1) * SLAB_W)
            awoa_ref[k] += _dot(y_at, dpa[:, cols])
            awob_ref[k] += _dot(y_bt, dpb[:, cols])
            dy_a = dy_a + _dot_nt(dpa[:, cols], woa_ref[k])
            dy_b = dy_b + _dot_nt(dpb[:, cols], wob_ref[k])
        dob = (dy_a * sa).astype(BF16)
        do_ref[...] = dob
        dot_ref[...] = dob.T
        head = (lax.broadcasted_iota(jnp.int32, (HEADS, MLA_W), 1) // V_DIM
                == lax.broadcasted_iota(jnp.int32, (HEADS, MLA_W), 0)).astype(BF16)
        dl_ref[...] = sum(_dot_nt(head, term) for term in _split3(dob.astype(F32) * o))
        emit(dhm_ref, dhmt_ref, dbm_ref, 0, dy_a * o * dsa)
        dsg = dy_b * sb
        emit(dhm_ref, dhmt_ref, dbm_ref, 3 * SGU_W, dy_b * sgu * dsb)
        emit(dhm_ref, dhmt_ref, dbm_ref, SGU_W, dsg * mixed * dgu)
        dmixed = dsg * gu
        dvn_rows = []
        dbs_sum = jnp.zeros((CHUNK, SGU_W), F32)
        for c in range(nch):
            dm_c = dmixed[c * CHUNK:(c + 1) * CHUNK, :]
            dbs_sum = dbs_sum + dm_c
            blocks = []
            for p in range(npair):
                dmb = dm_c[:, p * CHUNK:(p + 1) * CHUNK].astype(BF16)
                blk = vn[c * CHUNK:(c + 1) * CHUNK, p * CHUNK:(p + 1) * CHUNK]
                blocks.append(jnp.where(left, _dot_tn(ms[2 * p], dmb), _dot_tn(ms[2 * p + 1], dmb)))
                zero = jnp.zeros_like(dmb)
                dws_ref[2 * p] += jnp.where(tril, _dot_nt(jnp.where(left, dmb, zero), blk), 0.0)
                dws_ref[2 * p + 1] += jnp.where(tril, _dot_nt(jnp.where(left, zero, dmb), blk), 0.0)
            dvn_rows.append(jnp.concatenate(blocks, axis=1))
        dbacc_ref[...] += dbs_sum
        dvn = jnp.concatenate(dvn_rows, axis=0)
        dsgg_ref[...] += jnp.sum(dvn * vhat, axis=0, keepdims=True)
        dsgb_ref[...] += jnp.sum(dvn, axis=0, keepdims=True)
        dvh = dvn * sgg_ref[...]
        dgv_in = rstd_v * (dvh - jnp.mean(dvh, axis=-1, keepdims=True)
                           - vhat * jnp.mean(dvh * vhat, axis=-1, keepdims=True))
        emit(dhm_ref, dhmt_ref, dbm_ref, 2 * SGU_W, dgv_in * dgv)

        @pl.when(i == nsteps - 1)
        def _():
            dwout_ref[...] = awout_ref[...].astype(BF16)
            dwoa_ref[...] = awoa_ref[...].astype(BF16)
            dwob_ref[...] = awob_ref[...].astype(BF16)
            grp = (lax.broadcasted_iota(jnp.int32, (SGU_W, CHUNK), 0) // V_DIM
                   == lax.broadcasted_iota(jnp.int32, (SGU_W, CHUNK), 1)).astype(BF16)
            hi, mid, lo = _split3(dbacc_ref[...])
            dbs_ref[...] = _dot(hi, grp) + _dot(mid, grp) + _dot(lo, grp)

    acc_shapes = [(D_MODEL, D_MODEL), woa.shape, wob.shape, (GROUPS, CHUNK, CHUNK), (CHUNK, CHUNK),
                  (1, D_MODEL), (1, D_MODEL), (1, SGU_W), (1, SGU_W), (1, 128), (1, GATE_W), (1, MID_W)]
    col_spec = lambda rows: pl.BlockSpec((rows, ts), lambda i: (0, i))
    return pl.pallas_call(
        body, name="mid", grid=(nsteps,),
        in_specs=[_row_spec(ts, D_MODEL), _row_spec(ts, D_MODEL), _row_spec(ts, MLA_W), _row_spec(ts, MID_W),
                  _row_spec(ts, GATE_W), _full_spec(woa.shape), _full_spec(wob.shape), _full_spec(wout.shape),
                  _full_spec(ln_g.shape), _full_spec(ln_b.shape), _full_spec(sg_g.shape), _full_spec(sg_b.shape),
                  _full_spec(w_s.shape), _full_spec(bsb.shape)],
        out_specs=[_row_spec(ts, D_MODEL), _row_spec(ts, GATE_W), _row_spec(ts, MID_W), _row_spec(ts, MLA_W),
                   col_spec(MLA_W), col_spec(HEADS), col_spec(GATE_W), col_spec(MID_W)]
        + [_full_spec(sh) for sh in acc_shapes],
        out_shape=[jax.ShapeDtypeStruct((s, D_MODEL), F32), jax.ShapeDtypeStruct((s, GATE_W), BF16),
                   jax.ShapeDtypeStruct((s, MID_W), BF16), jax.ShapeDtypeStruct((s, MLA_W), BF16),
                   jax.ShapeDtypeStruct((MLA_W, s), BF16), jax.ShapeDtypeStruct((HEADS, s), F32),
                   jax.ShapeDtypeStruct((GATE_W, s), BF16), jax.ShapeDtypeStruct((MID_W, s), BF16)]
        + [jax.ShapeDtypeStruct(sh, BF16 if n < 3 else F32) for n, sh in enumerate(acc_shapes)],
        scratch_shapes=[pltpu.VMEM((CHUNK, SGU_W), F32)] + [pltpu.VMEM(sh, F32) for sh in acc_shapes[:3]],
        compiler_params=_params(),
    )(x, tgt, o, hm, hg, woa, wob, wout, ln_g, ln_b, sg_g, sg_b, w_s, bsb)


def _lat_bwd(dq, dk, dv, hl, rc, rsl, rsh, g_q, g_kv, wuq, wk, wv, after):
    s = dk.shape[0]
    ts = ROW_TILE
    qk_w = HEADS * HEAD_PAD

    def body(dq_ref, dk_ref, dv_ref, hl_ref, rc_ref, rsl_ref, rsh_ref, gq_ref, gkv_ref, wuq_ref, wk_ref, wv_ref,
             after_ref, dhl_ref, dhlt_ref, dwuq_ref, dwk_ref, dwv_ref, dgq_ref, dgkv_ref, dbl_ref):
        i = pl.program_id(0)

        @pl.when(i == 0)
        def _():
            for r in (dwuq_ref, dwk_ref, dwv_ref, dgq_ref, dgkv_ref, dbl_ref):
                r[...] = jnp.zeros_like(r)

        def emit(lo, val):
            vb = val.astype(BF16)
            n = val.shape[1]
            dhl_ref[:, lo:lo + n] = vb
            dhlt_ref[lo:lo + n, :] = vb.T
            dbl_ref[:, lo:lo + n] += jnp.sum(val, axis=0, keepdims=True)

        c, sl, sh = rc_ref[...], rsl_ref[...], rsh_ref[...]
        lane = lax.broadcasted_iota(jnp.int32, (ts, HEAD_PAD), 1)
        pe = (lane >= NOPE) & (lane < QK_DIM)
        dkpe = jnp.zeros((ts, HEAD_PAD), F32)
        dqu = []
        for hd in range(HEADS):
            lanes = slice(hd * HEAD_PAD, (hd + 1) * HEAD_PAD)
            dqu.append(_rope_t(dq_ref[lanes, :].T, c, sl, sh).astype(BF16))
            dkpe = dkpe + dk_ref[:, lanes]
        dqu = jnp.concatenate(dqu, axis=1)
        dkpe = _rope_t(jnp.where(pe, dkpe, 0.0), c, sl, sh)

        cq = hl_ref[:, :Q_RANK]
        rq = lax.rsqrt(jnp.mean(cq * cq, axis=-1, keepdims=True) + RMS_EPS)
        cqh = cq * rq
        cqn = (cqh * gq_ref[...]).astype(BF16)
        dwuq_ref[...] += _dot_tn(cqn, dqu)
        dcqn = _dot_nt(dqu, wuq_ref[...])
        dgq_ref[...] += jnp.sum(dcqn * cqh, axis=0, keepdims=True)
        dch = dcqn * gq_ref[...]
        emit(0, rq * (dch - cqh * jnp.mean(dch * cqh, axis=-1, keepdims=True)))

        ckv = hl_ref[:, Q_RANK:Q_RANK + KV_RANK]
        rk = lax.rsqrt(jnp.mean(ckv * ckv, axis=-1, keepdims=True) + RMS_EPS)
        ckh = ckv * rk
        ckn = (ckh * gkv_ref[...]).astype(BF16)
        dkb = dk_ref[...].astype(BF16)
        dvb = dv_ref[...].astype(BF16)
        dwk_ref[...] += _dot_tn(ckn, dkb)
        dwv_ref[...] += _dot_tn(ckn, dvb)
        dckn = _dot_nt(dkb, wk_ref[...]) + _dot_nt(dvb, wv_ref[...])
        dgkv_ref[...] += jnp.sum(dckn * ckh, axis=0, keepdims=True)
        dkh = dckn * gkv_ref[...]
        emit(Q_RANK, rk * (dkh - ckh * jnp.mean(dkh * ckh, axis=-1, keepdims=True)))
        emit(Q_RANK + KV_RANK, dkpe)

    acc_shapes = [wuq.shape, wk.shape, wv.shape, g_q.shape, g_kv.shape, (1, LAT_W)]
    return pl.pallas_call(
        body, name="lat_bwd", grid=(s // ts,),
        in_specs=[pl.BlockSpec((qk_w, ts), lambda i: (0, i)), _row_spec(ts, qk_w), _row_spec(ts, MLA_W),
                  _row_spec(ts, LAT_W), _row_spec(ts, HEAD_PAD), _row_spec(ts, HEAD_PAD), _row_spec(ts, HEAD_PAD),
                  _full_spec(g_q.shape), _full_spec(g_kv.shape), _full_spec(wuq.shape), _full_spec(wk.shape),
                  _full_spec(wv.shape), pl.BlockSpec(memory_space=pl.ANY)],
        out_specs=[_row_spec(ts, LAT_W), pl.BlockSpec((LAT_W, ts), lambda i: (0, i))]
        + [_full_spec(sh) for sh in acc_shapes],
        out_shape=[jax.ShapeDtypeStruct((s, LAT_W), BF16), jax.ShapeDtypeStruct((LAT_W, s), BF16)]
        + [jax.ShapeDtypeStruct(sh, F32) for sh in acc_shapes],
        compiler_params=_params(),
    )(dq, dk, dv, hl, rc, rsl, rsh, g_q, g_kv, wuq, wk, wv, after)


def _dx(dr, dhg, dhm, dhl, wt, wlat, after):
    s = dr.shape[0]
    ts = MATMUL_ROW_TILE

    tk = D_MODEL
    bounds = ([(ROW_GATE + r0, ROW_GATE + r0 + tk) for r0 in range(0, GATE_W, tk)]
              + [(LAT_COLS + r0, LAT_COLS + r0 + tk) for r0 in range(0, MID_W, tk)])

    def body(dr_ref, dhg_ref, dhm_ref, dhl_ref, wt_hbm, wlat_ref, after_ref, dx_ref, wt_ref, sems):
        copies = [pltpu.make_async_copy(wt_hbm.at[lo:hi], wt_ref.at[lo:hi], sems.at[n])
                  for n, (lo, hi) in enumerate(bounds)]

        def compute(first):
            acc = (ALPHA * dr_ref[...] + _dot(dhl_ref[:, 0:Q_RANK + KV_RANK], wlat_ref[0:Q_RANK + KV_RANK, :])
                   + _dot(dhl_ref[:, Q_RANK + KV_RANK:], _kpe_rows(wlat_ref)))
            for n, (lo, hi) in enumerate(bounds):
                if first:
                    copies[n].wait()
                if lo >= ROW_GATE:
                    acc += _dot(dhg_ref[:, lo - ROW_GATE:hi - ROW_GATE], wt_ref[lo:hi, :])
                else:
                    acc += _dot(dhm_ref[:, lo - LAT_COLS:hi - LAT_COLS], wt_ref[lo:hi, :])
            dx_ref[...] = acc

        @pl.when(pl.program_id(0) == 0)
        def _():
            for cp in copies:
                cp.start()
            compute(True)

        @pl.when(pl.program_id(0) > 0)
        def _():
            compute(False)

    return pl.pallas_call(
        body, name="dx", grid=(s // ts,),
        in_specs=[_row_spec(ts, D_MODEL), _row_spec(ts, GATE_W), _row_spec(ts, MID_W), _row_spec(ts, LAT_W),
                  HBM_SPEC, _full_spec(wlat.shape), HBM_SPEC],
        out_specs=_row_spec(ts, D_MODEL),
        out_shape=jax.ShapeDtypeStruct((s, D_MODEL), F32),
        scratch_shapes=[pltpu.VMEM(wt.shape, BF16), pltpu.SemaphoreType.DMA((len(bounds),))],
        compiler_params=_params(),
    )(dr, dhg, dhm, dhl, wt, wlat, after)


def _dwt_early(dhmt, dhgt, xb, col, after, name):
    tn = 512
    nm, ng = MID_W // tn, GATE_W // tn
    s = dhmt.shape[1]
    hc = D_MODEL // 2

    ks = s // 2

    def body(col_ref, dma_ref, dmb_ref, dga_ref, dgb_ref, xba_ref, xbb_ref, after_ref, dw_ref):
        i = pl.program_id(0)

        @pl.when(i < nm)
        def _():
            dw_ref[...] = (_dot(dma_ref[...], xba_ref[...]) + _dot(dmb_ref[...], xbb_ref[...])).astype(BF16)

        @pl.when(i >= nm)
        def _():
            dw_ref[...] = (_dot(dga_ref[...], xba_ref[...]) + _dot(dgb_ref[...], xbb_ref[...])).astype(BF16)

    def dh_spec(first, part):
        if first:
            return pl.BlockSpec((tn, ks), lambda i, col_ref: (jnp.minimum(i, nm - 1), part))
        return pl.BlockSpec((tn, ks), lambda i, col_ref: (jnp.maximum(i - nm, 0), part))

    rows = pl.pallas_call(
        body, name=name,
        grid_spec=pltpu.PrefetchScalarGridSpec(
            num_scalar_prefetch=1, grid=(nm + ng,),
            in_specs=[dh_spec(True, 0), dh_spec(True, 1), dh_spec(False, 0), dh_spec(False, 1),
                      pl.BlockSpec((None, ks, hc), lambda i, col_ref: (col_ref[0], 0, 0)),
                      pl.BlockSpec((None, ks, hc), lambda i, col_ref: (col_ref[0], 1, 0)),
                      pl.BlockSpec(memory_space=pl.ANY)],
            out_specs=pl.BlockSpec((pl.Element(tn), pl.Element(hc)),
                                   lambda i, col_ref: (pl.multiple_of(LAT_COLS + i * tn, 32), 0))),
        out_shape=jax.ShapeDtypeStruct((IN_W, hc), BF16),
        compiler_params=_params(),
    )(col, dhmt, dhmt, dhgt, dhgt, xb, xb, after)

    def zero(buf_ref, out_ref):
        out_ref[...] = jnp.zeros_like(out_ref)

    return pl.pallas_call(
        zero, name=name + "_zero_lat", grid=(1,), in_specs=[pl.BlockSpec(memory_space=pl.ANY)],
        out_specs=pl.BlockSpec((LAT_COLS, hc), lambda i: (0, 0)),
        out_shape=jax.ShapeDtypeStruct((IN_W, hc), BF16), input_output_aliases={0: 0},
    )(rows)


def _dwt_lat(dhlt, xb):
    n, s = dhlt.shape

    def body(dht_ref, xb_ref, dw_ref):
        dw = _dot(dht_ref[...], xb_ref[...]).astype(BF16)
        kpe = Q_RANK + KV_RANK + NOPE
        dw_ref[0:Q_RANK + KV_RANK, :] = dw[0:Q_RANK + KV_RANK]
        dw_ref[Q_RANK + KV_RANK:LAT_COLS, :] = dw[kpe:kpe + ROPE]
        dw_ref[LAT_COLS:, :] = jnp.zeros((LAT_ROWS_PAD - LAT_COLS, D_MODEL), BF16)

    return pl.pallas_call(
        body, name="dwt_lat", in_specs=[VMEM_SPEC, VMEM_SPEC], out_specs=VMEM_SPEC,
        out_shape=jax.ShapeDtypeStruct((LAT_ROWS_PAD, D_MODEL), BF16),
        compiler_params=pltpu.CompilerParams(vmem_limit_bytes=VMEM_LIMIT),
    )(dhlt, xb)


def _split_bias(b):
    z = lambda n: jnp.zeros((n,), b.dtype)
    lat = jnp.concatenate([b[:Q_RANK + KV_RANK], z(NOPE), b[Q_RANK + KV_RANK:LAT_COLS], z(HEAD_PAD - QK_DIM)])
    return b[None, ROW_GATE:], b[None, LAT_COLS:ROW_GATE], lat[None, :]


def _join_bias(g, m, l):
    kpe = Q_RANK + KV_RANK + NOPE
    return jnp.concatenate([l[0, :Q_RANK + KV_RANK], l[0, kpe:kpe + ROPE], m[0], g[0]])


def _rope_tables(positions):
    half = ROPE // 2
    inv_freq = ROPE_THETA ** (-jnp.arange(0, ROPE, 2, dtype=F32) / ROPE)
    ang = positions.astype(F32)[:, None] * inv_freq
    cos, sin = jnp.cos(ang), jnp.sin(ang)
    n = positions.shape[0]
    one, zero = jnp.ones((n, NOPE), F32), jnp.zeros((n, half), F32)
    tail1, tail0 = jnp.ones((n, HEAD_PAD - QK_DIM), F32), jnp.zeros((n, HEAD_PAD - QK_DIM), F32)
    z64 = jnp.zeros((n, NOPE), F32)
    rc = jnp.concatenate([one, cos, cos, tail1], axis=1)
    rsl = jnp.concatenate([z64, -sin, zero, tail0], axis=1)
    rsh = jnp.concatenate([z64, zero, sin, tail0], axis=1)
    return rc, rsl, rsh


def _local_attention(x, tables, wlat, b_in, g_q, w_uq, g_kv, w_ukv, after):
    rc, rsl, rsh = tables
    b_g, b_m, b_l = _split_bias(b_in)
    wuq = jnp.pad(w_uq, ((0, 0), (0, 0), (0, HEAD_PAD - QK_DIM))).reshape(Q_RANK, HEADS * HEAD_PAD).astype(BF16)
    wk = jnp.pad(w_ukv[:, :, :NOPE], ((0, 0), (0, 0), (0, HEAD_PAD - NOPE))).reshape(KV_RANK, HEADS * HEAD_PAD).astype(BF16)
    wv = w_ukv[:, :, NOPE:].reshape(KV_RANK, MLA_W).astype(BF16)
    gq2, gkv2 = g_q[None, :], g_kv[None, :]
    hl, q, k, v, xb, qt, kt, vt, xb2 = _fwd_lat(x, wlat, b_l, gq2, wuq, gkv2, wk, wv, rc, rsl, rsh, after)
    o, lse = _attn_fwd(qt, k, vt)
    return dict(q=q, qt=qt, k=k, kt=kt, v=v, o=o, lse=lse, hl=hl, rc=rc, rsl=rsl, rsh=rsh, gq2=gq2, gkv2=gkv2,
                wuq=wuq, wk=wk, wv=wv, xb=xb, xb2=xb2, b_g=b_g, b_m=b_m, wlat=wlat)


def _local_head(st, x, tgt, wt, w_oa, sg_g, sg_b, w_s, b_s, w_ob, w_out, ln_g, ln_b):
    q, qt, k, kt, v, o, lse, hl, xb = (st[n] for n in ("q", "qt", "k", "kt", "v", "o", "lse", "hl", "xb"))
    rc, rsl, rsh, gq2, gkv2, wuq, wk, wv = (st[n] for n in ("rc", "rsl", "rsh", "gq2", "gkv2", "wuq", "wk", "wv"))
    bsb = jnp.repeat(b_s.T, V_DIM, axis=1)
    hg, hm = _fwd_rest(xb, wt, st["b_g"], st["b_m"])
    (dr, dhg, dhm, do, dot, delta, dhgt, dhmt, dwout, dwoa, dwob, dws, dbs, dlng, dlnb, dsgg, dsgb, loss, dbg,
     dbm) = _mid(x, tgt, o, hm, hg, w_oa, w_ob, w_out, ln_g[None, :], ln_b[None, :], sg_g[None, :], sg_b[None, :],
                 w_s, bsb)
    delta = jnp.pad(delta.reshape(HEADS // 2, 2, -1), ((0, 0), (0, 6), (0, 0)))
    early = {
        "w_oa": dwoa, "sgu_ln_g": dsgg[0], "sgu_ln_b": dsgb[0], "w_s": dws, "b_s": dbs[:, :GROUPS].T,
        "w_ob": dwob, "w_out": dwout, "ln_g": dlng[0], "ln_b": dlnb[0],
    }
    state = dict(q=q, qt=qt, k=k, kt=kt, v=v, do=do, dot=dot, lse=lse, delta=delta, hl=hl, rc=rc, rsl=rsl, rsh=rsh,
                 gq2=gq2, gkv2=gkv2, wuq=wuq, wk=wk, wv=wv, dr=dr, dhg=dhg, dhm=dhm, wt=wt, xb=xb, dbg=dbg, dbm=dbm,
                 dhgt=dhgt, dhmt=dhmt, xb2=st["xb2"], wlat=st["wlat"])
    return loss, early, state


def _local_attn_bwd(st):
    return _attn_bwd(st["q"], st["qt"], st["k"], st["kt"], st["v"], st["do"], st["dot"], st["lse"], st["delta"])


def _local_tail(st, dq, dk, dv, after):
    dhl, dhlt, dwuq, dwk, dwv, dgq, dgkv, dbl = _lat_bwd(dq, dk, dv, st["hl"], st["rc"], st["rsl"], st["rsh"],
                                                         st["gq2"], st["gkv2"], st["wuq"], st["wk"], st["wv"], after)
    late = {
        "w_lat": _dwt_lat(dhlt, st["xb"]),
        "b_in": _join_bias(st["dbg"], st["dbm"], dbl),
        "g_q": dgq[0],
        "w_uq": dwuq.reshape(Q_RANK, HEADS, HEAD_PAD)[:, :, :QK_DIM],
        "g_kv": dgkv[0],
        "w_ukv": jnp.concatenate([dwk.reshape(KV_RANK, HEADS, HEAD_PAD)[:, :, :NOPE],
                                  dwv.reshape(KV_RANK, HEADS, V_DIM)], axis=2),
    }
    return dhl, late


def _local_step(x, positions, tgt, wt, b_in, g_q, w_uq, g_kv, w_ukv, w_oa, sg_g, sg_b, w_s, b_s, w_ob, w_out, ln_g,
                ln_b):
    st = _local_attention(x, _rope_tables(positions), wt[:LAT_COLS], b_in, g_q, w_uq, g_kv, w_ukv, b_in)
    loss, early, st = _local_head(st, x, tgt, wt, w_oa, sg_g, sg_b, w_s, b_s, w_ob, w_out, ln_g, ln_b)
    dq, dk, dv = _local_attn_bwd(st)
    dhl, late = _local_tail(st, dq, dk, dv, dv)
    dx = _dx(st["dr"], st["dhg"], st["dhm"], dhl, st["wt"], st["wlat"], dhl)
    grads = {**early, **late}
    halves = [_dwt_early(st["dhmt"], st["dhgt"], st["xb2"], jnp.full((1,), h, jnp.int32), dhl, "dwt_half%d" % h)
              for h in range(2)]
    grads["w_in"] = jnp.concatenate([grads.pop("w_lat")[:LAT_COLS], jnp.concatenate(halves, axis=1)[LAT_COLS:]],
                                    axis=0)
    return loss, dx, grads


MESH = pl.DeviceIdType.MESH
N_CHIPS = 4
HBM_SPEC = pl.BlockSpec(memory_space=pl.ANY)
HBM_SPEC_STRICT = pl.BlockSpec(memory_space=pltpu.HBM)
VMEM_SPEC = pl.BlockSpec(memory_space=pltpu.VMEM)

REP_ROWS = 80


def _rows8(a):
    flat = a.reshape(-1)
    n = -(-flat.shape[0] // (8 * D_MODEL)) * 8 * D_MODEL
    return jnp.pad(flat, (0, n - flat.shape[0])).reshape(-1, D_MODEL)


def _place():
    x, y, c = lax.axis_index("x"), lax.axis_index("y"), lax.axis_index("c")
    others = [(1 - x, y), (x, 1 - y), (1 - x, 1 - y)]
    return x, y, c, others


N_DEV = 8
LOSS_TILE = (8, 128)


def _cast_own(shards, me, after):
    n = len(shards)

    def body(me_ref, *refs):
        for w in range(n):
            refs[n + 1 + w][...] = refs[w][...].astype(BF16)

    return pl.pallas_call(
        body, name="cast_own",
        grid_spec=pltpu.PrefetchScalarGridSpec(
            num_scalar_prefetch=1, grid=(1,),
            in_specs=[pl.BlockSpec(s.shape, lambda i, me_ref: (0, 0)) for s in shards]
            + [pl.BlockSpec(memory_space=pl.ANY)],
            out_specs=[pl.BlockSpec((None,) + s.shape, lambda i, me_ref: (me_ref[0], 0, 0)) for s in shards]),
        out_shape=[jax.ShapeDtypeStruct((N_CHIPS,) + s.shape, BF16) for s in shards],
        compiler_params=pltpu.CompilerParams(vmem_limit_bytes=VMEM_LIMIT),
    )(me, *shards, after)


def _cast_first(lat, uq, me):
    def body(me_ref, lat_ref, uq_ref, wlat_ref, guq_ref):
        wlat_ref[...] = lat_ref[...].astype(BF16)
        guq_ref[...] = uq_ref[...].astype(BF16)

    return pl.pallas_call(
        body, name="cast_first",
        grid_spec=pltpu.PrefetchScalarGridSpec(
            num_scalar_prefetch=1, grid=(1,),
            in_specs=[pl.BlockSpec((LAT_COLS, D_MODEL), lambda i, me_ref: (0, 0)),
                      pl.BlockSpec(uq.shape, lambda i, me_ref: (0, 0))],
            out_specs=[pl.BlockSpec((LAT_COLS, D_MODEL), lambda i, me_ref: (0, 0)),
                       pl.BlockSpec((None,) + uq.shape, lambda i, me_ref: (me_ref[0], 0, 0))]),
        out_shape=[jax.ShapeDtypeStruct((LAT_COLS, D_MODEL), BF16),
                   jax.ShapeDtypeStruct((N_CHIPS,) + uq.shape, BF16)],
    )(me, lat, uq)


def _first_copies(wlat_ref, guq_ref, send_sems, recv_sems, shapes):
    x, y, c, others = _place()
    me = 2 * x + y
    hl, hu = shapes[0][1] // 2, shapes[1][2] // 2
    lat_half = wlat_ref.at[:, pl.ds(c * hl, hl)]

    def copy(src, dst, k, to):
        return pltpu.make_async_remote_copy(src_ref=src, dst_ref=dst, send_sem=send_sems.at[k],
                                            recv_sem=recv_sems.at[k], device_id=to, device_id_type=MESH)

    def uq_half(chip):
        return guq_ref.at[chip, :, pl.ds(c * hu, hu)]

    lat_out = [copy(lat_half, lat_half, j, (*others[j], c)) for j in range(3)]
    uq_out = [copy(uq_half(me), uq_half(me), 3 + j, (*others[j], c)) for j in range(3)]
    j0 = jnp.maximum(x + 2 * y - 1, 0)
    lat_in = copy(lat_half, lat_half, j0, (0, 0, c))
    uq_in = [copy(uq_half(me), uq_half(2 * px + py), 3 + j, (px, py, c)) for j, (px, py) in enumerate(others)]
    return me, lat_out, uq_out, lat_in, uq_in


def _first_start(wlat, guq):
    shapes = (wlat.shape, guq.shape)

    def body(wlat_ref, guq_ref, send_sems, recv_sems, wlat_thru, guq_thru, token):
        me, lat_out, uq_out, _, _ = _first_copies(wlat_ref, guq_ref, send_sems, recv_sems, shapes)

        @pl.when(me == 0)
        def _():
            for cp in lat_out:
                cp.start()

        for cp in uq_out:
            cp.start()
        token[...] = jnp.zeros_like(token)

    outs = pl.pallas_call(
        body, name="first_start",
        out_shape=(pltpu.SemaphoreType.DMA((6,)), pltpu.SemaphoreType.DMA((6,)), pltpu.HBM(wlat.shape, BF16),
                   pltpu.HBM(guq.shape, BF16), jax.ShapeDtypeStruct(LOSS_TILE, F32)),
        in_specs=[HBM_SPEC_STRICT] * 2, out_specs=(SEM_SPEC, SEM_SPEC, HBM_SPEC_STRICT, HBM_SPEC_STRICT, VMEM_SPEC),
        input_output_aliases={0: 2, 1: 3},
        compiler_params=pltpu.CompilerParams(has_side_effects=SPLIT_EFFECT),
    )(pltpu.with_memory_space_constraint(wlat, pltpu.HBM), pltpu.with_memory_space_constraint(guq, pltpu.HBM))
    return outs


def _first_wait(send_sems, recv_sems, wlat, guq, *after):
    shapes = (wlat.shape, guq.shape)

    def body(wlat_ref, guq_ref, send_sems, recv_sems, *rest):
        me, lat_out, uq_out, lat_in, uq_in = _first_copies(wlat_ref, guq_ref, send_sems, recv_sems, shapes)

        @pl.when(me == 0)
        def _():
            for cp in lat_out:
                cp.wait_send()

        @pl.when(me != 0)
        def _():
            lat_in.wait_recv()

        for cp in uq_out:
            cp.wait_send()
        for cp in uq_in:
            cp.wait_recv()

    return pl.pallas_call(
        body, name="first_wait", out_shape=(pltpu.HBM(wlat.shape, BF16), pltpu.HBM(guq.shape, BF16)),
        in_specs=[HBM_SPEC_STRICT, HBM_SPEC_STRICT, SEM_SPEC, SEM_SPEC] + [HBM_SPEC] * len(after),
        out_specs=(HBM_SPEC_STRICT, HBM_SPEC_STRICT), input_output_aliases={0: 0, 1: 1},
        compiler_params=pltpu.CompilerParams(has_side_effects=SPLIT_EFFECT),
    )(wlat, guq, send_sems, recv_sems, *after)


def _first_forward(wlat, guq):
    hl, hu = wlat.shape[1] // 2, guq.shape[2] // 2

    def body(wlat_in, guq_in, wlat_ref, guq_ref, send_sems, recv_sems):
        x, y, c, others = _place()
        me = 2 * x + y
        sibling = (x, y, 1 - c)

        def copy(part, k):
            return pltpu.make_async_remote_copy(src_ref=part, dst_ref=part, send_sem=send_sems.at[k],
                                                recv_sem=recv_sems.at[k], device_id=sibling, device_id_type=MESH)

        def uq_part(j, half):
            px, py = others[j]
            return guq_ref.at[2 * px + py, :, pl.ds(half * hu, hu)]

        cps = [copy(uq_part(j, c), j) for j in range(3)]
        for cp in cps:
            cp.start()

        @pl.when(me != 0)
        def _():
            mine = copy(wlat_ref.at[:, pl.ds(c * hl, hl)], 3)
            mine.start()
            copy(wlat_ref.at[:, pl.ds((1 - c) * hl, hl)], 3).wait_recv()
            mine.wait_send()

        for j in range(3):
            copy(uq_part(j, 1 - c), j).wait_recv()
        for cp in cps:
            cp.wait_send()

    return pl.pallas_call(
        body, name="first_forward", in_specs=[HBM_SPEC, HBM_SPEC], out_specs=[HBM_SPEC, HBM_SPEC],
        out_shape=[jax.ShapeDtypeStruct(wlat.shape, BF16), jax.ShapeDtypeStruct(guq.shape, BF16)],
        input_output_aliases={0: 0, 1: 1},
        scratch_shapes=[pltpu.SemaphoreType.DMA((4,)), pltpu.SemaphoreType.DMA((4,))],
    )(wlat, guq)


def _gather_copy(b_ref, buf, w, j, src_chip, dst_chip, to, rows, send_sems, recv_sems):
    _, _, c, _ = _place()
    hc = _half(buf)
    return pltpu.make_async_remote_copy(
        src_ref=b_ref.at[src_chip, rows, pl.ds(c * hc, hc)], dst_ref=b_ref.at[dst_chip, rows, pl.ds(c * hc, hc)],
        send_sem=send_sems.at[3 * w + j], recv_sem=recv_sems.at[3 * w + j], device_id=to, device_id_type=MESH)


def _gather_rows(buf, w, chip, fn):
    if w != 0:
        fn(slice(None))
        return
    pl.when(chip == 0)(lambda: fn(pl.ds(LAT_COLS, buf.shape[1] - LAT_COLS)))
    pl.when(chip != 0)(lambda: fn(slice(None)))


def _gather_start(bufs, after):
    n = len(bufs)

    def body(*refs):
        b_refs = refs[:n]
        send_sems, recv_sems, token = refs[n + 1], refs[n + 2], refs[-1]
        x, y, c, others = _place()
        me = 2 * x + y
        for w in range(n):
            def start(rows, w=w):
                for j, (px, py) in enumerate(others):
                    _gather_copy(b_refs[w], bufs[w], w, j, me, me, (px, py, c), rows, send_sems, recv_sems).start()
            _gather_rows(bufs[w], w, me, start)
        token[...] = jnp.zeros_like(token)

    hbm = [pltpu.HBM(b.shape, BF16) for b in bufs]
    outs = pl.pallas_call(
        body, name="gather_start",
        out_shape=(pltpu.SemaphoreType.DMA((3 * n,)), pltpu.SemaphoreType.DMA((3 * n,)), *hbm,
                   jax.ShapeDtypeStruct(LOSS_TILE, F32)),
        in_specs=[HBM_SPEC_STRICT] * n + [HBM_SPEC],
        out_specs=(SEM_SPEC, SEM_SPEC, *[HBM_SPEC_STRICT] * n, VMEM_SPEC),
        input_output_aliases={i: 2 + i for i in range(n)},
        compiler_params=pltpu.CompilerParams(has_side_effects=SPLIT_EFFECT),
    )(*[pltpu.with_memory_space_constraint(b, pltpu.HBM) for b in bufs], after)
    return outs[0], outs[1], list(outs[2:2 + n]), outs[-1]


def _gather_wait(send_sems, recv_sems, bufs, after):
    n = len(bufs)

    def body(*refs):
        b_refs = refs[:n]
        send_sems, recv_sems = refs[n], refs[n + 1]
        x, y, c, others = _place()
        me = 2 * x + y
        for w in range(n):
            for j, (px, py) in enumerate(others):
                def copy(rows, w=w, j=j, px=px, py=py):
                    return _gather_copy(b_refs[w], bufs[w], w, j, me, 2 * px + py, (px, py, c), rows, send_sems,
                                        recv_sems)
                _gather_rows(bufs[w], w, me, lambda rows, copy=copy: copy(rows).wait_send())
                _gather_rows(bufs[w], w, 2 * px + py, lambda rows, copy=copy: copy(rows).wait_recv())

    outs = pl.pallas_call(
        body, name="gather_wait", out_shape=tuple(pltpu.HBM(b.shape, b.dtype) for b in bufs),
        in_specs=[HBM_SPEC_STRICT] * n + [SEM_SPEC, SEM_SPEC, HBM_SPEC],
        out_specs=tuple([HBM_SPEC_STRICT] * n), input_output_aliases={i: i for i in range(n)},
        compiler_params=pltpu.CompilerParams(has_side_effects=SPLIT_EFFECT),
    )(*bufs, send_sems, recv_sems, after)
    return list(outs)


def _gather_finish(bufs):
    n = len(bufs)

    def body(*refs):
        b_refs = refs[n:2 * n]
        send_sems, recv_sems = refs[2 * n:]
        x, y, c, others = _place()
        cps = []
        for w in range(n):
            hc = _half(bufs[w])
            for j, (px, py) in enumerate(others):
                part = b_refs[w].at[2 * px + py, :, pl.ds(c * hc, hc)]
                cps.append(pltpu.make_async_remote_copy(
                    src_ref=part, dst_ref=part, send_sem=send_sems.at[3 * w + j], recv_sem=recv_sems.at[3 * w + j],
                    device_id=(x, y, 1 - c), device_id_type=MESH))
        for cp in cps:
            cp.start()
        for w in range(n):
            hc = _half(bufs[w])
            for j, (px, py) in enumerate(others):
                theirs = b_refs[w].at[2 * px + py, :, pl.ds((1 - c) * hc, hc)]
                pltpu.make_async_remote_copy(
                    src_ref=theirs, dst_ref=theirs, send_sem=send_sems.at[3 * w + j], recv_sem=recv_sems.at[3 * w + j],
                    device_id=(x, y, 1 - c), device_id_type=MESH).wait_recv()
        for cp in cps:
            cp.wait_send()

    return pl.pallas_call(
        body, name="gather_finish", in_specs=[HBM_SPEC] * n, out_specs=[HBM_SPEC] * n,
        out_shape=[jax.ShapeDtypeStruct(b.shape, b.dtype) for b in bufs],
        input_output_aliases={i: i for i in range(n)},
        scratch_shapes=[pltpu.SemaphoreType.DMA((3 * n,)), pltpu.SemaphoreType.DMA((3 * n,))],
    )(*bufs)


def _half(a):
    return a.shape[-1] // 2


def _exchange_pairs(parts, name):
    n = len(parts)

    def body(*refs):
        p_refs, r_refs = refs[:n], refs[n:2 * n]
        send_sems, recv_sems = refs[2 * n:]
        x, y, c, _ = _place()
        cps = []
        for w in range(n):
            h = _half(parts[w])
            cps.append(pltpu.make_async_remote_copy(
                src_ref=p_refs[w].at[:, :, pl.ds((1 - c) * h, h)], dst_ref=r_refs[w],
                send_sem=send_sems.at[w], recv_sem=recv_sems.at[w], device_id=(x, y, 1 - c), device_id_type=MESH))
        for cp in cps:
            cp.start()
        for cp in cps:
            cp.wait()

    return pl.pallas_call(
        body, name=name, in_specs=[HBM_SPEC] * n, out_specs=[HBM_SPEC] * n,
        out_shape=[jax.ShapeDtypeStruct((N_CHIPS, p.shape[1], _half(p)), BF16) for p in parts],
        scratch_shapes=[pltpu.SemaphoreType.DMA((n,)), pltpu.SemaphoreType.DMA((n,))],
    )(*parts)


def _sibling_part(ref, w, n_whole, shape, c):
    if w < n_whole:
        return ref
    h = shape[-1] // 2
    return ref.at[:, :, pl.ds((1 - c) * h, h)]


def _pairs_start(parts, all_loss, n_whole):
    n = len(parts)

    def body(*refs):
        p_refs, r_refs, loss_ref = refs[:n], refs[n:2 * n], refs[2 * n]
        send_sems, recv_sems, token = refs[2 * n + 1], refs[2 * n + 2], refs[-1]
        x, y, c, _ = _place()
        for w in range(n):
            h = _half(parts[w])
            pltpu.make_async_remote_copy(
                src_ref=_sibling_part(p_refs[w], w, n_whole, parts[w].shape, c), dst_ref=r_refs[w],
                send_sem=send_sems.at[w], recv_sem=recv_sems.at[w], device_id=(x, y, 1 - c),
                device_id_type=MESH).start()
        me = 4 * x + 2 * y + c
        for t in range(1, N_DEV):
            d = (me + t) % N_DEV
            pltpu.make_async_remote_copy(
                src_ref=loss_ref.at[me], dst_ref=loss_ref.at[me], send_sem=send_sems.at[n + t - 1],
                recv_sem=recv_sems.at[n + t - 1], device_id=(d // 4, (d // 2) % 2, d % 2), device_id_type=MESH).start()
        token[...] = jnp.zeros_like(token)

    lands = [pltpu.HBM(p.shape if w < n_whole else (N_CHIPS, p.shape[1], _half(p)), BF16)
             for w, p in enumerate(parts)]
    nsem = n + N_DEV - 1
    outs = pl.pallas_call(
        body, name="pairs_start",
        out_shape=(pltpu.SemaphoreType.DMA((nsem,)), pltpu.SemaphoreType.DMA((nsem,)),
                   *[pltpu.HBM(p.shape, p.dtype) for p in parts], *lands, pltpu.HBM(all_loss.shape, F32),
                   jax.ShapeDtypeStruct(LOSS_TILE, F32)),
        in_specs=[HBM_SPEC_STRICT] * (2 * n + 1),
        out_specs=(SEM_SPEC, SEM_SPEC, *[HBM_SPEC_STRICT] * (2 * n + 1), VMEM_SPEC),
        input_output_aliases={i: 2 + i for i in range(2 * n + 1)},
        compiler_params=pltpu.CompilerParams(has_side_effects=SPLIT_EFFECT),
    )(*[pltpu.with_memory_space_constraint(p, pltpu.HBM) for p in parts],
      *[pltpu.with_memory_space_constraint(lax.empty(l.shape, BF16), pltpu.HBM) for l in lands],
      pltpu.with_memory_space_constraint(all_loss, pltpu.HBM))
    return outs[0], outs[1], list(outs[2:2 + n]), list(outs[2 + n:2 + 2 * n]), outs[2 + 2 * n], outs[-1]


def _pairs_wait(send_sems, recv_sems, parts, lands, all_loss, after, n_whole):
    n = len(parts)

    def body(*refs):
        p_refs, r_refs, loss_ref = refs[:n], refs[n:2 * n], refs[2 * n]
        send_sems, recv_sems = refs[2 * n + 1], refs[2 * n + 2]
        x, y, c, _ = _place()
        for w in range(n):
            h = _half(parts[w])
            cp = pltpu.make_async_remote_copy(
                src_ref=_sibling_part(p_refs[w], w, n_whole, parts[w].shape, c), dst_ref=r_refs[w],
                send_sem=send_sems.at[w],
                recv_sem=recv_sems.at[w], device_id=(x, y, 1 - c), device_id_type=MESH)
            cp.wait_send()
            cp.wait_recv()
        me = 4 * x + 2 * y + c
        for t in range(1, N_DEV):
            d = (me + N_DEV - t) % N_DEV
            cp = pltpu.make_async_remote_copy(
                src_ref=loss_ref.at[me], dst_ref=loss_ref.at[d], send_sem=send_sems.at[n + t - 1],
                recv_sem=recv_sems.at[n + t - 1], device_id=(d // 4, (d // 2) % 2, d % 2), device_id_type=MESH)
            cp.wait_send()
            cp.wait_recv()

    bufs = (*parts, *lands, all_loss)
    outs = pl.pallas_call(
        body, name="pairs_wait", out_shape=tuple(pltpu.HBM(a.shape, a.dtype) for a in bufs),
        in_specs=[HBM_SPEC_STRICT] * len(bufs) + [SEM_SPEC, SEM_SPEC, HBM_SPEC],
        out_specs=tuple([HBM_SPEC_STRICT] * len(bufs)), input_output_aliases={i: i for i in range(len(bufs))},
        compiler_params=pltpu.CompilerParams(has_side_effects=SPLIT_EFFECT),
    )(*bufs, send_sems, recv_sems, after)
    return list(outs[:n]), list(outs[n:2 * n]), outs[2 * n]


def _add_pair(ps, rs, c, name, n_whole=0):
    n = len(ps)

    def body(c_ref, *refs):
        for w in range(n):
            if w < n_whole:
                mine = refs[w][...]
            else:
                h = _half(ps[w])
                mine = refs[w][:, :, pl.ds(pl.multiple_of(c_ref[0] * h, 128), h)]
            refs[2 * n + w][...] = (mine.astype(F32) + refs[n + w][...].astype(F32)).astype(BF16)

    return pl.pallas_call(
        body, name=name,
        in_specs=[pl.BlockSpec(memory_space=pltpu.SMEM)] + [VMEM_SPEC] * (2 * n), out_specs=[VMEM_SPEC] * n,
        out_shape=[jax.ShapeDtypeStruct(r.shape, BF16) for r in rs],
        compiler_params=pltpu.CompilerParams(vmem_limit_bytes=VMEM_LIMIT),
    )(c, *ps, *rs)


SEM_SPEC = pl.BlockSpec(memory_space=pltpu.SEMAPHORE)
SPLIT_EFFECT = pltpu.SideEffectType.DATAFLOW_SIDE_EFFECTING


def _chips_start(qs, name):
    n = len(qs)

    def body(*refs):
        q_refs, land_refs = refs[:n], refs[n:2 * n]
        send_sems, recv_sems, token = refs[2 * n], refs[2 * n + 1], refs[-1]
        x, y, c, others = _place()
        me = 2 * x + y
        for w in range(n):
            for j, (px, py) in enumerate(others):
                pltpu.make_async_remote_copy(
                    src_ref=q_refs[w].at[2 * px + py], dst_ref=land_refs[w].at[me], send_sem=send_sems.at[3 * w + j],
                    recv_sem=recv_sems.at[3 * w + j], device_id=(px, py, c), device_id_type=MESH).start()
        token[...] = jnp.zeros_like(token)

    hbm = [pltpu.HBM(q.shape, BF16) for q in qs]
    outs = pl.pallas_call(
        body, name=name,
        out_shape=(pltpu.SemaphoreType.DMA((3 * n,)), pltpu.SemaphoreType.DMA((3 * n,)), *hbm, *hbm,
                   jax.ShapeDtypeStruct(LOSS_TILE, F32)),
        in_specs=[HBM_SPEC_STRICT] * (2 * n),
        out_specs=(SEM_SPEC, SEM_SPEC, *[HBM_SPEC_STRICT] * (2 * n), VMEM_SPEC),
        input_output_aliases={i: 2 + i for i in range(2 * n)},
        compiler_params=pltpu.CompilerParams(has_side_effects=SPLIT_EFFECT),
    )(*[pltpu.with_memory_space_constraint(q, pltpu.HBM) for q in qs],
      *[pltpu.with_memory_space_constraint(lax.empty(q.shape, BF16), pltpu.HBM) for q in qs])
    return outs[0], outs[1], outs[2:2 + n], outs[2 + n:2 + 2 * n], outs[-1]


def _chips_wait(send_sems, recv_sems, q_thru, land_thru, after, name):
    n = len(q_thru)

    def body(*refs):
        q_refs, land_refs = refs[:n], refs[n:2 * n]
        send_sems, recv_sems = refs[2 * n], refs[2 * n + 1]
        x, y, c, others = _place()
        me = 2 * x + y
        for w in range(n):
            for j, (px, py) in enumerate(others):
                cp = pltpu.make_async_remote_copy(
                    src_ref=q_refs[w].at[2 * px + py], dst_ref=land_refs[w].at[2 * px + py],
                    send_sem=send_sems.at[3 * w + j], recv_sem=recv_sems.at[3 * w + j], device_id=(px, py, c),
                    device_id_type=MESH)
                cp.wait_send()
                cp.wait_recv()

    outs = pl.pallas_call(
        body, name=name, out_shape=tuple(pltpu.HBM(a.shape, a.dtype) for a in (*q_thru, *land_thru)),
        in_specs=[HBM_SPEC_STRICT] * (2 * n) + [SEM_SPEC, SEM_SPEC, HBM_SPEC],
        out_specs=tuple([HBM_SPEC_STRICT] * (2 * n)), input_output_aliases={i: i for i in range(2 * n)},
        compiler_params=pltpu.CompilerParams(has_side_effects=SPLIT_EFFECT),
    )(*q_thru, *land_thru, send_sems, recv_sems, after)
    return list(outs[:n]), list(outs[n:])


def _sum_chips(qs, rs, idx, all_dtypes):
    n = len(rs)
    n_all = len(all_dtypes)

    def body(idx_ref, *refs):
        c = idx_ref[4]
        for w in range(n):
            q_ref, r_ref, g_ref = refs[w], refs[n + w], refs[2 * n + w]
            acc = q_ref[idx_ref[0]].astype(F32)
            for t in range(1, N_CHIPS):
                acc = acc + r_ref[idx_ref[t]].astype(F32)
            h = rs[w].shape[2]
            mine = pl.ds(pl.multiple_of(c * h, 128), h)
            g_ref[...] = jnp.zeros_like(g_ref)
            if w >= n - n_all:
                g_ref[idx_ref[0], :, mine] = acc.astype(g_ref.dtype)
            else:
                g_ref[:, mine] = acc

    shapes = [jax.ShapeDtypeStruct((r.shape[1], 2 * r.shape[2]), F32) for r in rs[:n - n_all]]
    shapes += [jax.ShapeDtypeStruct((N_CHIPS, r.shape[1], 2 * r.shape[2]), dt)
               for r, dt in zip(rs[n - n_all:], all_dtypes)]
    return pl.pallas_call(
        body, name="sum_chips",
        in_specs=[pl.BlockSpec(memory_space=pltpu.SMEM)] + [VMEM_SPEC] * (2 * n), out_specs=[VMEM_SPEC] * n,
        out_shape=shapes, compiler_params=pltpu.CompilerParams(vmem_limit_bytes=VMEM_LIMIT),
    )(idx, *qs, *rs)


def _share(shards, alls):
    n, na = len(shards), len(alls)
    total = n + na

    def body(*refs):
        g_refs, a_refs = refs[total:total + n], refs[total + n:2 * total]
        send_sems, recv_sems = refs[2 * total:]
        x, y, c, others = _place()
        me = 2 * x + y
        sibling = (x, y, 1 - c)

        def cols_of(w, half):
            h = shards[w].shape[1] // 2
            return g_refs[w].at[:, pl.ds(half * h, h)]

        def slab(a, chip, half):
            h = alls[a].shape[2] // 2
            return a_refs[a].at[chip, :, pl.ds(half * h, h)]

        def copy(src, dst, k, to):
            return pltpu.make_async_remote_copy(src_ref=src, dst_ref=dst, send_sem=send_sems.at[k],
                                                recv_sem=recv_sems.at[k], device_id=to, device_id_type=MESH)

        cps = [copy(cols_of(w, c), cols_of(w, c), w, sibling) for w in range(n)]
        for a in range(na):
            base = n + 7 * a
            cps.append(copy(slab(a, me, c), slab(a, me, c), base, sibling))
            for j, (px, py) in enumerate(others):
                cps.append(copy(slab(a, me, c), slab(a, me, c), base + 1 + j, (px, py, c)))
        for cp in cps:
            cp.start()
        fwd = []
        for a in range(na):
            base = n + 7 * a
            for j, (px, py) in enumerate(others):
                chip = 2 * px + py
                copy(slab(a, me, c), slab(a, chip, c), base + 1 + j, (px, py, c)).wait_recv()
                cp = copy(slab(a, chip, c), slab(a, chip, c), base + 4 + j, sibling)
                cp.start()
                fwd.append(cp)
        for a in range(na):
            base = n + 7 * a
            for j, (px, py) in enumerate(others):
                chip = 2 * px + py
                copy(slab(a, chip, c), slab(a, chip, 1 - c), base + 4 + j, sibling).wait_recv()
            copy(slab(a, me, c), slab(a, me, 1 - c), base, sibling).wait_recv()
        for w in range(n):
            copy(cols_of(w, c), cols_of(w, 1 - c), w, sibling).wait_recv()
        for cp in cps + fwd:
            cp.wait_send()

    nsem = n + 7 * na
    return pl.pallas_call(
        body, name="share", in_specs=[HBM_SPEC] * total, out_specs=[HBM_SPEC] * total,
        out_shape=[jax.ShapeDtypeStruct(a.shape, a.dtype) for a in (*shards, *alls)],
        input_output_aliases={i: i for i in range(total)},
        scratch_shapes=[pltpu.SemaphoreType.DMA((nsem,)), pltpu.SemaphoreType.DMA((nsem,))],
    )(*shards, *alls)


def _adamw(w, g, m, v):
    m2 = ADAM_B1 * m + (1.0 - ADAM_B1) * g
    v2 = ADAM_B2 * v + (1.0 - ADAM_B2) * (g * g)
    m_hat = m2 / (1.0 - ADAM_B1 ** ADAM_STEP)
    v_hat = v2 / (1.0 - ADAM_B2 ** ADAM_STEP)
    return -ADAM_LR * (m_hat / (jnp.sqrt(v_hat) + ADAM_EPS) + ADAM_WD * w), m2, v2


def _update_w_in(wt, gt, mt, vt, lat, owner, tile):
    nlat = lat.shape[0] // tile

    def body(owner_ref, w_ref, g_ref, m_ref, v_ref, lat_ref, g2_ref, d_ref, m2_ref, v2_ref):
        row = pl.program_id(0) * tile + lax.broadcasted_iota(jnp.int32, (tile, 1), 0)
        g = jnp.where((row < LAT_COLS) & (owner_ref[0] == 1), lat_ref[...].astype(F32), g_ref[...])
        g2_ref[...] = g
        d_ref[...], m2_ref[...], v2_ref[...] = _adamw(w_ref[...], g, m_ref[...], v_ref[...])

    spec = pl.BlockSpec((tile, wt.shape[1]), lambda i, o: (i, 0))
    return pl.pallas_call(
        body, name="update_w_in",
        grid_spec=pltpu.PrefetchScalarGridSpec(
            num_scalar_prefetch=1, grid=(wt.shape[0] // tile,),
            in_specs=[spec] * 4 + [pl.BlockSpec((tile, wt.shape[1]), lambda i, o: (jnp.minimum(i, nlat - 1), 0))],
            out_specs=[spec] * 4),
        out_shape=[jax.ShapeDtypeStruct(wt.shape, F32)] * 4,
        compiler_params=_params(("parallel",)),
    )(owner, wt, gt, mt, vt, lat)


def _update_small(ws, gs, ms, vs):
    n = len(ws)

    def body(*refs):
        for k in range(n):
            w_ref, g_ref, m_ref, v_ref = refs[k], refs[n + k], refs[2 * n + k], refs[3 * n + k]
            d, m2, v2 = _adamw(w_ref[...], g_ref[...], m_ref[...], v_ref[...])
            refs[4 * n + k][...] = d
            refs[5 * n + k][...] = m2
            refs[6 * n + k][...] = v2

    shapes = [jax.ShapeDtypeStruct(w.shape, F32) for w in ws]
    outs = pl.pallas_call(
        body, name="update_small", in_specs=[VMEM_SPEC] * (4 * n), out_specs=[VMEM_SPEC] * (3 * n),
        out_shape=shapes * 3,
        compiler_params=pltpu.CompilerParams(vmem_limit_bytes=VMEM_LIMIT),
    )(*ws, *gs, *ms, *vs)
    return outs[:n], outs[n:2 * n], outs[2 * n:]


REPLICATED = ("b_in", "g_q", "g_kv", "w_ukv", "sgu_ln_g", "sgu_ln_b", "w_s", "b_s", "ln_g", "ln_b")
ORDER = ("w_in", "b_in", "g_q", "w_uq", "g_kv", "w_ukv", "w_oa", "sgu_ln_g", "sgu_ln_b", "w_s", "b_s", "w_ob", "w_out",
         "ln_g", "ln_b")


def kernel(x, positions, w_in, b_in, g_q, w_uq, g_kv, w_ukv, w_oa, sgu_ln_g, sgu_ln_b, w_s, b_s, w_ob, w_out, ln_g, ln_b, loss_target, m_w_in, m_b_in, m_g_q, m_w_uq, m_g_kv, m_w_ukv, m_w_oa, m_sgu_ln_g, m_sgu_ln_b, m_w_s, m_b_s, m_w_ob, m_w_out, m_ln_g, m_ln_b, v_w_in, v_b_in, v_g_q, v_w_uq, v_g_kv, v_w_ukv, v_w_oa, v_sgu_ln_g, v_sgu_ln_b, v_w_s, v_b_s, v_w_ob, v_w_out, v_ln_g, v_ln_b):
    w = dict(w_in=w_in, b_in=b_in, g_q=g_q, w_uq=w_uq, g_kv=g_kv, w_ukv=w_ukv, w_oa=w_oa, sgu_ln_g=sgu_ln_g,
             sgu_ln_b=sgu_ln_b, w_s=w_s, b_s=b_s, w_ob=w_ob, w_out=w_out, ln_g=ln_g, ln_b=ln_b)
    m = dict(w_in=m_w_in, b_in=m_b_in, g_q=m_g_q, w_uq=m_w_uq, g_kv=m_g_kv, w_ukv=m_w_ukv, w_oa=m_w_oa,
             sgu_ln_g=m_sgu_ln_g, sgu_ln_b=m_sgu_ln_b, w_s=m_w_s, b_s=m_b_s, w_ob=m_w_ob, w_out=m_w_out, ln_g=m_ln_g,
             ln_b=m_ln_b)
    v = dict(w_in=v_w_in, b_in=v_b_in, g_q=v_g_q, w_uq=v_w_uq, g_kv=v_g_kv, w_ukv=v_w_ukv, w_oa=v_w_oa,
             sgu_ln_g=v_sgu_ln_g, sgu_ln_b=v_sgu_ln_b, w_s=v_w_s, b_s=v_b_s, w_ob=v_w_ob, w_out=v_w_out, ln_g=v_ln_g,
             ln_b=v_ln_b)
    w, m, v = ({n: a[0] for n, a in d.items()} for d in (w, m, v))
    c = lax.axis_index("c")

    wt_shard, mt_shard, vt_shard = (jnp.transpose(d["w_in"]) for d in (w, m, v))
    xi, yi = lax.axis_index("x"), lax.axis_index("y")
    me1 = (2 * xi + yi).reshape(1).astype(jnp.int32)
    first = _first_start(*_cast_first(wt_shard, w["w_uq"].reshape(Q_RANK // 4, HEADS * QK_DIM), me1))
    tables = _rope_tables(positions[0])
    bufs = _cast_own([wt_shard, w["w_oa"], w["w_ob"], w["w_out"]], me1, first[4])
    send0, recv0, bufs, token0 = _gather_start(bufs, first[4])
    g_lat, g_uq = _first_forward(*_first_wait(*first[:4], token0, *tables))
    st = _local_attention(x[0], tables, g_lat, w["b_in"], w["g_q"], g_uq.reshape(Q_RANK, HEADS, QK_DIM),
                          w["g_kv"], w["w_ukv"], token0)
    g_in, g_oa, g_ob, g_out = _gather_finish(_gather_wait(send0, recv0, bufs, st["o"]))
    wt = g_in.reshape(IN_W, D_MODEL)

    loss, early, st = _local_head(
        st, x[0], loss_target[0], wt, g_oa, w["sgu_ln_g"], w["sgu_ln_b"], w["w_s"], w["b_s"], g_ob,
        g_out.reshape(D_MODEL, D_MODEL), w["ln_g"], w["ln_b"])

    c1 = c.reshape(1).astype(jnp.int32)
    idx = jnp.stack([2 * xi + yi, 2 * (1 - xi) + yi, 2 * xi + (1 - yi), 2 * (1 - xi) + (1 - yi), c]).astype(jnp.int32)
    slabs = lambda a: a.reshape(N_CHIPS, IN_W // N_CHIPS, D_MODEL // 2)
    theirs = slabs(_dwt_early(st["dhmt"], st["dhgt"], st["xb2"], 1 - c1, c1, "dwt_theirs"))
    parts1 = [theirs, early["w_oa"].astype(BF16), early["w_ob"].astype(BF16),
              early["w_out"].reshape(N_CHIPS, SLAB_W, D_MODEL).astype(BF16)]
    my_loss = lax.dynamic_update_slice(jnp.zeros((N_DEV,) + LOSS_TILE, F32), jnp.broadcast_to(loss, (1,) + LOSS_TILE),
                                       (4 * xi + 2 * yi + c, 0, 0))
    sems0 = _pairs_start(parts1, my_loss, 1)
    mine = slabs(_dwt_early(st["dhmt"], st["dhgt"], st["xb2"], c1, sems0[5], "dwt_mine"))
    parts1, recv1, all_loss = _pairs_wait(*sems0[:5], mine, 1)
    pairs1 = _add_pair([mine, *parts1[1:]], recv1, c1, "add_pair_early", n_whole=1)
    sems1 = _chips_start(pairs1, "chips_start_early")
    st["delta"] = st["delta"] + sems1[4][0, 0]
    dq, dk, dv = _local_attn_bwd(st)
    dhl, late = _local_tail(st, dq, dk, dv, dv)

    grads = {**early, **late}
    rep = jnp.concatenate([_rows8(grads[n]) for n in REPLICATED], axis=0)
    rep = jnp.pad(rep, ((0, N_CHIPS * REP_ROWS - rep.shape[0]), (0, 0))).reshape(N_CHIPS, REP_ROWS, D_MODEL)
    parts2 = [late["w_uq"].reshape(N_CHIPS, Q_RANK // N_CHIPS, HEADS * QK_DIM).astype(BF16), rep.astype(BF16),
              late["w_lat"].reshape(N_CHIPS, LAT_ROWS_PAD // N_CHIPS, D_MODEL)]
    pairs2 = _add_pair(parts2, _exchange_pairs(parts2, "exchange_pairs_late"), c1, "add_pair_late")
    sems2 = _chips_start(pairs2, "chips_start_late")
    dx = _dx(st["dr"], st["dhg"], st["dhm"], dhl, st["wt"], st["wlat"], sems2[4])
    pairs2, landed2 = _chips_wait(*sems2[:4], dx, "chips_wait_late")
    pairs1, landed1 = _chips_wait(*sems1[:4], landed2[0], "chips_wait_early")
    sums = _sum_chips([*pairs1, *pairs2], [*landed1, *landed2], idx, (F32, BF16))
    *shards, g_rep, g_lat = _share(sums[:-2], sums[-2:])
    loss = jnp.sum(all_loss[:, 0, 0])

    red = {n: s.reshape(w[n].shape) for n, s in zip(("w_oa", "w_ob", "w_out", "w_uq"), shards[1:])}
    g_rep = g_rep.reshape(N_CHIPS * REP_ROWS, D_MODEL)
    off = 0
    for n in REPLICATED:
        rows = _rows8(w[n]).shape[0]
        red[n] = g_rep[off:off + rows].reshape(-1)[:w[n].size].reshape(w[n].shape)
        off += rows
    owner = (2 * xi + yi == 0).astype(jnp.int32).reshape(1)
    gt, dt, mt, vt2 = _update_w_in(wt_shard, shards[0], mt_shard, vt_shard,
                                   g_lat.reshape(LAT_ROWS_PAD, D_MODEL).astype(F32), owner, 232)
    red["w_in"] = jnp.transpose(gt)
    small = [n for n in ORDER if n != "w_in"]
    as2d = lambda a: a.reshape(-1, a.shape[-1])
    ds, ms, vs = _update_small([as2d(w[n]) for n in small], [as2d(red[n]) for n in small],
                               [as2d(m[n]) for n in small], [as2d(v[n]) for n in small])
    delta, new_m, new_v = {"w_in": jnp.transpose(dt)}, {"w_in": jnp.transpose(mt)}, {"w_in": jnp.transpose(vt2)}
    for i, n in enumerate(small):
        delta[n], new_m[n], new_v[n] = (a[i].reshape(w[n].shape) for a in (ds, ms, vs))

    lead = lambda a: a[None]
    return (loss, dx[None], *[lead(red[n]) for n in ORDER], *[lead(delta[n]) for n in ORDER],
            *[lead(new_m[n]) for n in ORDER], *[lead(new_v[n]) for n in ORDER])
```

```python
import math

import jax
import jax.numpy as jnp
from jax import lax
from jax.experimental import pallas as pl
from jax.experimental.pallas import tpu as pltpu

F32 = jnp.float32
BF16 = jnp.bfloat16

D_MODEL = 1024
HEADS = 8
Q_RANK = 384
KV_RANK = 128
NOPE = 64
ROPE = 32
V_DIM = 64
QK_DIM = NOPE + ROPE
HEAD_PAD = 128
MLA_W = HEADS * V_DIM
SGU_W = 512
GROUPS = 8
CHUNK = 128
IN_W = 4640
RMS_EPS = 1e-6
LN_EPS = 1e-5
ALPHA = 2.0 ** 0.25
ROPE_THETA = 10000.0
SCALE = QK_DIM ** -0.5

GATE_W = 2 * D_MODEL
MID_W = 4 * SGU_W
LAT_W = Q_RANK + KV_RANK + HEAD_PAD
LAT_COLS = Q_RANK + KV_RANK + ROPE
ROW_GATE = LAT_COLS + MID_W
LAT_ROWS_PAD = 704
N_SLABS = 4
SLAB_W = D_MODEL // N_SLABS

ROW_TILE = 256
MATMUL_ROW_TILE = 256
MID_ROW_TILE = 256
ATT_TK = 256
ATT_BWD_TK = 256
SUM_ROWS = 16
LOG2E = 1.4426950408889634
LN2 = 0.6931471805599453
Q_SCALE = SCALE * LOG2E
VMEM_LIMIT = 56 * 1024 * 1024

ADAM_LR = 0.001
ADAM_B1 = 0.9
ADAM_B2 = 0.999
ADAM_EPS = 1e-08
ADAM_WD = 0.01
ADAM_STEP = 10


def _dot(a, b):
    return jnp.dot(a, b, preferred_element_type=F32)


def _dot_nt(a, b):
    return lax.dot_general(a, b, (((1,), (1,)), ((), ())), preferred_element_type=F32)


def _dot_tn(a, b):
    return lax.dot_general(a, b, (((0,), (0,)), ((), ())), preferred_element_type=F32)


def _sigmoid(z):
    return 0.5 * jnp.tanh(0.5 * z) + 0.5


_GELU_C = math.sqrt(2.0 / math.pi)


def _gelu_and_grad(x):
    x2 = x * x
    t = jnp.tanh(_GELU_C * (x + 0.044715 * x * x2))
    g = 0.5 * x * (1.0 + t)
    dg = 0.5 * (1.0 + t) + 0.5 * x * (1.0 - t * t) * (_GELU_C * (1.0 + 3.0 * 0.044715 * x2))
    return g, dg


def _silu_and_grad(z):
    s = _sigmoid(z)
    return z * s, s * (1.0 + z * (1.0 - s))


def _rope(xb, c, sl, sh):
    return xb * c + pltpu.roll(xb, 112, 1) * sl + pltpu.roll(xb, 16, 1) * sh


def _rope_t(dy, c, sl, sh):
    return dy * c + pltpu.roll(dy * sl, 16, 1) + pltpu.roll(dy * sh, 112, 1)


def _params(sem=("arbitrary",)):
    return pltpu.CompilerParams(dimension_semantics=sem, vmem_limit_bytes=VMEM_LIMIT)


def _row_spec(tile, width):
    return pl.BlockSpec((tile, width), lambda i: (i, 0))


def _full_spec(shape):
    nd = len(shape)
    return pl.BlockSpec(shape, lambda i: (0,) * nd)


def _kpe_rows(wt_ref):
    z = lambda n: jnp.zeros((n, D_MODEL), BF16)
    return jnp.concatenate([z(NOPE), wt_ref[Q_RANK + KV_RANK:LAT_COLS, :], z(HEAD_PAD - QK_DIM)], axis=0)


def _fwd_rest(xb, wt, b_g, b_m):
    s = xb.shape[0]
    ts = MATMUL_ROW_TILE
    tn = D_MODEL
    blocks = ([(ROW_GATE + c0, 0, c0) for c0 in range(0, GATE_W, tn)]
              + [(LAT_COLS + c0, 1, c0) for c0 in range(0, MID_W, tn)])

    def body(xb_ref, wt_hbm, bg_ref, bm_ref, hg_ref, hm_ref, wt_ref, sems):
        copies = [pltpu.make_async_copy(wt_hbm.at[lo:lo + tn], wt_ref.at[lo:lo + tn], sems.at[n])
                  for n, (lo, _, _) in enumerate(blocks)]

        def compute(first):
            xb_ = xb_ref[...]
            for cp, (lo, which, c0) in zip(copies, blocks):
                if first:
                    cp.wait()
                out_ref, b_ref = ((hg_ref, bg_ref), (hm_ref, bm_ref))[which]
                out_ref[:, c0:c0 + tn] = (_dot_nt(xb_, wt_ref[lo:lo + tn, :]) + b_ref[:, c0:c0 + tn]).astype(BF16)

        @pl.when(pl.program_id(0) == 0)
        def _():
            for cp in copies:
                cp.start()
            compute(True)

        @pl.when(pl.program_id(0) > 0)
        def _():
            compute(False)

    return pl.pallas_call(
        body, name="fwd_rest", grid=(s // ts,),
        in_specs=[_row_spec(ts, D_MODEL), HBM_SPEC, _full_spec(b_g.shape), _full_spec(b_m.shape)],
        out_specs=[_row_spec(ts, GATE_W), _row_spec(ts, MID_W)],
        out_shape=[jax.ShapeDtypeStruct((s, GATE_W), BF16), jax.ShapeDtypeStruct((s, MID_W), BF16)],
        scratch_shapes=[pltpu.VMEM(wt.shape, BF16), pltpu.SemaphoreType.DMA((len(blocks),))],
        compiler_params=_params(),
    )(xb, wt, b_g, b_m)


def _fwd_lat(x, wlat, b_l, g_q, wuq, g_kv, wk, wv, rc, rsl, rsh, after):
    s = x.shape[0]
    ts = ROW_TILE

    def body(x_ref, wt_ref, bl_ref, gq_ref, wuq_ref, gkv_ref, wk_ref, wv_ref, rc_ref, rsl_ref,
             rsh_ref, after_ref, hl_ref, q_ref, k_ref, v_ref, xb_ref, qt_ref, kt_ref, vt_ref, xb2_ref):
        xb = x_ref[...].astype(BF16)
        xb_ref[...] = xb
        xb2_ref[0] = xb[:, :D_MODEL // 2]
        xb2_ref[1] = xb[:, D_MODEL // 2:]
        hl = jnp.concatenate([_dot_nt(xb, wt_ref[0:Q_RANK + KV_RANK, :]), _dot_nt(xb, _kpe_rows(wt_ref))],
                             axis=1) + bl_ref[...]
        hl_ref[...] = hl
        c, sl, sh = rc_ref[...], rsl_ref[...], rsh_ref[...]
        cq = hl[:, :Q_RANK]
        cqn = cq * lax.rsqrt(jnp.mean(cq * cq, axis=-1, keepdims=True) + RMS_EPS) * gq_ref[...]
        q = _dot(cqn.astype(BF16), wuq_ref[...])
        ckv = hl[:, Q_RANK:Q_RANK + KV_RANK]
        ckvn = (ckv * lax.rsqrt(jnp.mean(ckv * ckv, axis=-1, keepdims=True) + RMS_EPS) * gkv_ref[...]).astype(BF16)
        k = _dot(ckvn, wk_ref[...])
        vb = _dot(ckvn, wv_ref[...]).astype(BF16)
        v_ref[...] = vb
        vt_ref[...] = vb.T
        kpe = _rope(hl[:, Q_RANK + KV_RANK:], c, sl, sh)
        for hd in range(HEADS):
            lanes = slice(hd * HEAD_PAD, (hd + 1) * HEAD_PAD)
            qb = (_rope(q[:, lanes], c, sl, sh) * Q_SCALE).astype(BF16)
            kb = (k[:, lanes] + kpe).astype(BF16)
            q_ref[:, lanes] = qb
            k_ref[:, lanes] = kb
            qt_ref[lanes, :] = qb.T
            kt_ref[lanes, :] = kb.T

    qk_w = HEADS * HEAD_PAD
    col_spec = lambda rows: pl.BlockSpec((rows, ts), lambda i: (0, i))
    return pl.pallas_call(
        body, name="fwd_lat", grid=(s // ts,),
        in_specs=[_row_spec(ts, D_MODEL), _full_spec(wlat.shape),
                  _full_spec(b_l.shape), _full_spec(g_q.shape),
                  _full_spec(wuq.shape), _full_spec(g_kv.shape), _full_spec(wk.shape), _full_spec(wv.shape),
                  _row_spec(ts, HEAD_PAD), _row_spec(ts, HEAD_PAD), _row_spec(ts, HEAD_PAD),
                  pl.BlockSpec(memory_space=pl.ANY)],
        out_specs=[_row_spec(ts, LAT_W), _row_spec(ts, qk_w),
                   _row_spec(ts, qk_w), _row_spec(ts, MLA_W), _row_spec(ts, D_MODEL), col_spec(qk_w), col_spec(qk_w),
                   col_spec(MLA_W), pl.BlockSpec((2, ts, D_MODEL // 2), lambda i: (0, i, 0))],
        out_shape=[jax.ShapeDtypeStruct((s, LAT_W), F32), jax.ShapeDtypeStruct((s, qk_w), BF16),
                   jax.ShapeDtypeStruct((s, qk_w), BF16), jax.ShapeDtypeStruct((s, MLA_W), BF16),
                   jax.ShapeDtypeStruct((s, D_MODEL), BF16), jax.ShapeDtypeStruct((qk_w, s), BF16),
                   jax.ShapeDtypeStruct((qk_w, s), BF16), jax.ShapeDtypeStruct((MLA_W, s), BF16),
                   jax.ShapeDtypeStruct((2, s, D_MODEL // 2), BF16)],
        compiler_params=_params(),
    )(x, wlat, b_l, g_q, wuq, g_kv, wk, wv, rc, rsl, rsh, after)


def _attn_fwd(qt, k, vt):
    s = k.shape[0]
    tk = ATT_TK
    nk = s // tk
    pairs = HEADS // 2

    def body(qt_ref, k_ref, vt_ref, o_ref, lse_ref):
        qts = [qt_ref[hh * HEAD_PAD:(hh + 1) * HEAD_PAD, :] for hh in range(2)]
        ones = jnp.ones((SUM_ROWS, tk), BF16)

        def scores(j):
            return tuple(_dot(k_ref[j * tk:(j + 1) * tk, hh * HEAD_PAD:(hh + 1) * HEAD_PAD], qts[hh][:, j * tk:])
                         for hh in range(2))

        def weighted(j, ps):
            return tuple(_dot(jnp.concatenate([vt_ref[hh * V_DIM:(hh + 1) * V_DIM, j * tk:(j + 1) * tk], ones], axis=0),
                              ps[hh]) for hh in range(2))

        def from_lane(full, lo, part):
            return part if lo == 0 else jnp.concatenate([full[:, :lo], part], axis=1)

        krow = lax.broadcasted_iota(jnp.int32, (tk, tk), 0)
        qcol = lax.broadcasted_iota(jnp.int32, (tk, tk), 1)
        st = scores(0)
        ps = None
        stats = [(jnp.full((1, s), -jnp.inf, F32), jnp.zeros((V_DIM + SUM_ROWS, s), F32))] * 2
        for j in range(nk):
            lo, lo_prev = j * tk, max(j - 1, 0) * tk
            st_next = scores(j + 1) if j + 1 < nk else None
            pvs = weighted(j - 1, ps) if j else None
            new_ps, new_stats = [], []
            for hh in range(2):
                m, acc = stats[hh]
                diag = jnp.where(krow <= qcol, st[hh][:, :tk], -jnp.inf)
                s_ = diag if j == nk - 1 else jnp.concatenate([diag, st[hh][:, tk:]], axis=1)
                if j:
                    acc = from_lane(acc, lo_prev, acc[:, lo_prev:] + pvs[hh])
                m_old = m[:, lo:]
                m_new = jnp.maximum(m_old, jnp.max(s_, axis=0, keepdims=True))
                a = jnp.exp2(m_old - m_new)
                p = jnp.exp2(s_ - m_new)
                new_stats.append((from_lane(m, lo, m_new), from_lane(acc, lo, a * acc[:, lo:])))
                new_ps.append(p.astype(BF16))
            st, ps, stats = st_next, new_ps, new_stats
        pvs = weighted(nk - 1, ps)
        lo = (nk - 1) * tk
        accs = [from_lane(stats[hh][1], lo, stats[hh][1][:, lo:] + pvs[hh]) for hh in range(2)]
        sums = [acc[V_DIM:V_DIM + 1, :] for acc in accs]
        ot = jnp.concatenate([accs[hh][:V_DIM, :] / sums[hh] for hh in range(2)], axis=0)
        o_ref[...] = ot.T
        lse = [stats[hh][0] + jnp.log(sums[hh]) * LOG2E for hh in range(2)]
        lse_ref[...] = jnp.concatenate(lse + [jnp.zeros((6, s), F32)], axis=0)

    return pl.pallas_call(
        body, name="attn_fwd", grid=(pairs,),
        in_specs=[pl.BlockSpec((2 * HEAD_PAD, s), lambda p: (p, 0)),
                  pl.BlockSpec((s, 2 * HEAD_PAD), lambda p: (0, p)),
                  pl.BlockSpec((2 * V_DIM, s), lambda p: (p, 0))],
        out_specs=[pl.BlockSpec((s, 2 * V_DIM), lambda p: (0, p)),
                   pl.BlockSpec((None, 8, s), lambda p: (p, 0, 0))],
        out_shape=[jax.ShapeDtypeStruct((s, MLA_W), F32), jax.ShapeDtypeStruct((pairs, 8, s), F32)],
        compiler_params=_params(("arbitrary",)),
    )(qt, k, vt)


def _attn_bwd(q, qt, k, kt, v, do, dot, lse, delta, after):
    s = k.shape[0]
    tk = ATT_BWD_TK
    nk = s // tk
    pairs = HEADS // 2

    def body(q_ref, qt_ref, k_ref, kt_ref, v_ref, do_ref, dot_ref, lse_ref, dl_ref, after_ref, dqt_ref, dk_ref,
             dv_ref):
        krow = lax.broadcasted_iota(jnp.int32, (tk, tk), 0)
        qcol = lax.broadcasted_iota(jnp.int32, (tk, tk), 1)
        lane = lax.broadcasted_iota(jnp.int32, (tk, 2 * V_DIM), 1)
        drow = lax.broadcasted_iota(jnp.int32, (2 * V_DIM, s), 0)
        dotb = dot_ref[...]
        dots = [jnp.where((drow < V_DIM) if hh == 0 else (drow >= V_DIM), dotb, jnp.zeros_like(dotb))
                for hh in range(2)]
        for j in range(nk):
            lo = j * tk
            vb = v_ref[lo:lo + tk, :]
            dob = do_ref[lo:, :]
            dvs = []
            for hh in range(2):
                rows = slice(hh * HEAD_PAD, (hh + 1) * HEAD_PAD)
                st = _dot(k_ref[lo:lo + tk, rows], qt_ref[rows, lo:])
                diag = jnp.where(krow <= qcol, st[:, :tk], -jnp.inf)
                st = diag if j == nk - 1 else jnp.concatenate([diag, st[:, tk:]], axis=1)
                p = jnp.exp2(st - lse_ref[hh:hh + 1, lo:])
                dpt = _dot(vb, dots[hh][:, lo:])
                dst = (p * (dpt - dl_ref[hh:hh + 1, lo:])).astype(BF16)
                dvs.append(_dot(p.astype(BF16), dob))
                dk_ref[lo:lo + tk, rows] = _dot(dst, q_ref[lo:, rows]) * LN2
                dqt = _dot(kt_ref[rows, lo:lo + tk], dst)
                if j == 0:
                    dqt_ref[rows, :] = dqt
                else:
                    dqt_ref[rows, lo:] += dqt
            dv_ref[lo:lo + tk, :] = jnp.where(lane < V_DIM, dvs[0], dvs[1])
        dqt_ref[...] = dqt_ref[...] * SCALE

    pair_rows = lambda w: pl.BlockSpec((s, w), lambda p: (0, p))
    pair_cols = lambda w: pl.BlockSpec((w, s), lambda p: (p, 0))
    stats = pl.BlockSpec((None, 8, s), lambda p: (p, 0, 0))
    return pl.pallas_call(
        body, name="attn_bwd", grid=(pairs,),
        in_specs=[pair_rows(2 * HEAD_PAD), pair_cols(2 * HEAD_PAD), pair_rows(2 * HEAD_PAD), pair_cols(2 * HEAD_PAD),
                  pair_rows(2 * V_DIM), pair_rows(2 * V_DIM), pair_cols(2 * V_DIM), stats, stats,
                  pl.BlockSpec(memory_space=pl.ANY)],
        out_specs=[pair_cols(2 * HEAD_PAD), pair_rows(2 * HEAD_PAD), pair_rows(2 * V_DIM)],
        out_shape=[jax.ShapeDtypeStruct((HEADS * HEAD_PAD, s), F32), jax.ShapeDtypeStruct((s, HEADS * HEAD_PAD), F32),
                   jax.ShapeDtypeStruct((s, MLA_W), F32)],
        compiler_params=_params(("arbitrary",)),
    )(q, qt, k, kt, v, do, dot, lse, delta, after)


def _split3(a):
    hi = a.astype(BF16)
    r1 = a - hi.astype(F32)
    mid = r1.astype(BF16)
    lo = (r1 - mid.astype(F32)).astype(BF16)
    return hi, mid, lo


def _mid(x, tgt, o, hm, hg, woa, wob, wout, ln_g, ln_b, sg_g, sg_b, w_s, bsb):
    s = x.shape[0]
    ts = MID_ROW_TILE
    nsteps = s // ts
    nch = ts // CHUNK
    npair = GROUPS // 2

    def body(x_ref, t_ref, o_ref, hm_ref, hg_ref, woa_ref, wob_ref, wout_ref, lng_ref, lnb_ref, sgg_ref, sgb_ref,
             ws_ref, bsb_ref,
             dr_ref, dhg_ref, dhm_ref, do_ref, dot_ref, dl_ref, dhgt_ref, dhmt_ref,
             dwout_ref, dwoa_ref, dwob_ref, dws_ref, dbs_ref, dlng_ref, dlnb_ref, dsgg_ref, dsgb_ref, loss_ref,
             dbg_ref, dbm_ref, dbacc_ref, awout_ref, awoa_ref, awob_ref):
        i = pl.program_id(0)

        @pl.when(i == 0)
        def _():
            for r in (awout_ref, awoa_ref, awob_ref, dws_ref, dlng_ref, dlnb_ref, dsgg_ref, dsgb_ref, loss_ref,
                      dbg_ref, dbm_ref, dbacc_ref):
                r[...] = jnp.zeros_like(r)

        def emit(ref, tref, bref, lo, val):
            vb = val.astype(BF16)
            n = val.shape[1]
            ref[:, lo:lo + n] = vb
            tref[lo:lo + n, :] = vb.T
            bref[:, lo:lo + n] += jnp.sum(val, axis=0, keepdims=True)

        lane = lax.broadcasted_iota(jnp.int32, (CHUNK, CHUNK), 1)
        left = lane < V_DIM
        tril = lax.broadcasted_iota(jnp.int32, (CHUNK, CHUNK), 0) >= lane
        ms = [jnp.where(tril, ws_ref[g], 0.0).astype(BF16) for g in range(GROUPS)]

        z_a = hm_ref[:, 0:SGU_W].astype(F32)
        u = hm_ref[:, SGU_W:2 * SGU_W].astype(F32)
        v = hm_ref[:, 2 * SGU_W:3 * SGU_W].astype(F32)
        z_b = hm_ref[:, 3 * SGU_W:4 * SGU_W].astype(F32)
        o = o_ref[...]
        sa, dsa = _silu_and_grad(z_a)
        y_a = (o * sa).astype(BF16)
        gu, dgu = _gelu_and_grad(u)
        gv, dgv = _gelu_and_grad(v)
        mu = jnp.mean(gv, axis=-1, keepdims=True)
        vc = gv - mu
        rstd_v = lax.rsqrt(jnp.mean(vc * vc, axis=-1, keepdims=True) + LN_EPS)
        vhat = vc * rstd_v
        vn = (vhat * sgg_ref[...] + sgb_ref[...]).astype(BF16)
        rows = []
        for c in range(nch):
            blocks = []
            for p in range(npair):
                blk = vn[c * CHUNK:(c + 1) * CHUNK, p * CHUNK:(p + 1) * CHUNK]
                blocks.append(jnp.where(left, _dot(ms[2 * p], blk), _dot(ms[2 * p + 1], blk)))
            rows.append(jnp.concatenate(blocks, axis=1) + bsb_ref[...])
        mixed = jnp.concatenate(rows, axis=0)
        sgu = gu * mixed
        sb, dsb = _silu_and_grad(z_b)
        y_b = (sgu * sb).astype(BF16)
        pa = jnp.concatenate([_dot(y_a, woa_ref[k]) for k in range(N_SLABS)], axis=1)
        pb = jnp.concatenate([_dot(y_b, wob_ref[k]) for k in range(N_SLABS)], axis=1)
        sga = _sigmoid(hg_ref[:, :D_MODEL].astype(F32))
        sgb = _sigmoid(hg_ref[:, D_MODEL:].astype(F32))
        m2 = (sga * pa + sgb * pb).astype(BF16)
        r = ALPHA * x_ref[...] + _dot(m2, wout_ref[...])
        rmu = jnp.mean(r, axis=-1, keepdims=True)
        rc = r - rmu
        rstd = lax.rsqrt(jnp.mean(rc * rc, axis=-1, keepdims=True) + LN_EPS)
        xhat = rc * rstd
        y = xhat * lng_ref[...] + lnb_ref[...]
        err = y - t_ref[...]
        loss_ref[...] += jnp.full(loss_ref.shape, 0.5 / D_MODEL, F32) * jnp.sum(err * err)

        dy = err * (1.0 / D_MODEL)
        dlng_ref[...] += jnp.sum(dy * xhat, axis=0, keepdims=True)
        dlnb_ref[...] += jnp.sum(dy, axis=0, keepdims=True)
        dxh = dy * lng_ref[...]
        dr = rstd * (dxh - jnp.mean(dxh, axis=-1, keepdims=True) - xhat * jnp.mean(dxh * xhat, axis=-1, keepdims=True))
        dr_ref[...] = dr
        drb = dr.astype(BF16)
        awout_ref[...] += _dot_tn(m2, drb)
        dm2 = _dot_nt(drb, wout_ref[...])
        emit(dhg_ref, dhgt_ref, dbg_ref, 0, dm2 * pa * sga * (1.0 - sga))
        emit(dhg_ref, dhgt_ref, dbg_ref, D_MODEL, dm2 * pb * sgb * (1.0 - sgb))
        dpa = (dm2 * sga).astype(BF16)
        dpb = (dm2 * sgb).astype(BF16)
        dy_a = jnp.zeros((ts, MLA_W), F32)
        dy_b = jnp.zeros((ts, SGU_W), F32)
        y_at, y_bt = y_a.T, y_b.T
        for k in range(N_SLABS):
            cols = slice(k * SLAB_W, (k + 1) * SLAB_W)
            awoa_ref[k] += _dot(y_at, dpa[:, cols])
            awob_ref[k] += _dot(y_bt, dpb[:, cols])
            dy_a = dy_a + _dot_nt(dpa[:, cols], woa_ref[k])
            dy_b = dy_b + _dot_nt(dpb[:, cols], wob_ref[k])
        dob = (dy_a * sa).astype(BF16)
        do_ref[...] = dob
        dot_ref[...] = dob.T
        head = (lax.broadcasted_iota(jnp.int32, (HEADS, MLA_W), 1) // V_DIM
                == lax.broadcasted_iota(jnp.int32, (HEADS, MLA_W), 0)).astype(BF16)
        dl = sum(_dot_nt(head, term) for term in _split3(dob.astype(F32) * o))
        for p in range(HEADS // 2):
            dl_ref[p] = jnp.concatenate([dl[2 * p:2 * p + 2], jnp.zeros((6, ts), F32)], axis=0)
        emit(dhm_ref, dhmt_ref, dbm_ref, 0, dy_a * o * dsa)
        dsg = dy_b * sb
        emit(dhm_ref, dhmt_ref, dbm_ref, 3 * SGU_W, dy_b * sgu * dsb)
        emit(dhm_ref, dhmt_ref, dbm_ref, SGU_W, dsg * mixed * dgu)
        dmixed = dsg * gu
        dvn_rows = []
        dbs_sum = jnp.zeros((CHUNK, SGU_W), F32)
        for c in range(nch):
            dm_c = dmixed[c * CHUNK:(c + 1) * CHUNK, :]
            dbs_sum = dbs_sum + dm_c
            blocks = []
            for p in range(npair):
                dmb = dm_c[:, p * CHUNK:(p + 1) * CHUNK].astype(BF16)
                blk = vn[c * CHUNK:(c + 1) * CHUNK, p * CHUNK:(p + 1) * CHUNK]
                blocks.append(jnp.where(left, _dot_tn(ms[2 * p], dmb), _dot_tn(ms[2 * p + 1], dmb)))
                zero = jnp.zeros_like(dmb)
                dws_ref[2 * p] += jnp.where(tril, _dot_nt(jnp.where(left, dmb, zero), blk), 0.0)
                dws_ref[2 * p + 1] += jnp.where(tril, _dot_nt(jnp.where(left, zero, dmb), blk), 0.0)
            dvn_rows.append(jnp.concatenate(blocks, axis=1))
        dbacc_ref[...] += dbs_sum
        dvn = jnp.concatenate(dvn_rows, axis=0)
        dsgg_ref[...] += jnp.sum(dvn * vhat, axis=0, keepdims=True)
        dsgb_ref[...] += jnp.sum(dvn, axis=0, keepdims=True)
        dvh = dvn * sgg_ref[...]
        dgv_in = rstd_v * (dvh - jnp.mean(dvh, axis=-1, keepdims=True)
                           - vhat * jnp.mean(dvh * vhat, axis=-1, keepdims=True))
        emit(dhm_ref, dhmt_ref, dbm_ref, 2 * SGU_W, dgv_in * dgv)

        @pl.when(i == nsteps - 1)
        def _():
            dwout_ref[...] = awout_ref[...].astype(BF16)
            dwoa_ref[...] = awoa_ref[...].astype(BF16)
            dwob_ref[...] = awob_ref[...].astype(BF16)
            grp = (lax.broadcasted_iota(jnp.int32, (SGU_W, CHUNK), 0) // V_DIM
                   == lax.broadcasted_iota(jnp.int32, (SGU_W, CHUNK), 1)).astype(BF16)
            hi, mid, lo = _split3(dbacc_ref[...])
            dbs_ref[...] = _dot(hi, grp) + _dot(mid, grp) + _dot(lo, grp)

    acc_shapes = [(D_MODEL, D_MODEL), woa.shape, wob.shape, (GROUPS, CHUNK, CHUNK), (CHUNK, CHUNK),
                  (1, D_MODEL), (1, D_MODEL), (1, SGU_W), (1, SGU_W), (1, 128), (1, GATE_W), (1, MID_W)]
    col_spec = lambda rows: pl.BlockSpec((rows, ts), lambda i: (0, i))
    return pl.pallas_call(
        body, name="mid", grid=(nsteps,),
        in_specs=[_row_spec(ts, D_MODEL), _row_spec(ts, D_MODEL), _row_spec(ts, MLA_W), _row_spec(ts, MID_W),
                  _row_spec(ts, GATE_W), _full_spec(woa.shape), _full_spec(wob.shape), _full_spec(wout.shape),
                  _full_spec(ln_g.shape), _full_spec(ln_b.shape), _full_spec(sg_g.shape), _full_spec(sg_b.shape),
                  _full_spec(w_s.shape), _full_spec(bsb.shape)],
        out_specs=[_row_spec(ts, D_MODEL), _row_spec(ts, GATE_W), _row_spec(ts, MID_W), _row_spec(ts, MLA_W),
                   col_spec(MLA_W), pl.BlockSpec((HEADS // 2, 8, ts), lambda i: (0, 0, i)), col_spec(GATE_W),
                   col_spec(MID_W)]
        + [_full_spec(sh) for sh in acc_shapes],
        out_shape=[jax.ShapeDtypeStruct((s, D_MODEL), F32), jax.ShapeDtypeStruct((s, GATE_W), BF16),
                   jax.ShapeDtypeStruct((s, MID_W), BF16), jax.ShapeDtypeStruct((s, MLA_W), BF16),
                   jax.ShapeDtypeStruct((MLA_W, s), BF16), jax.ShapeDtypeStruct((HEADS // 2, 8, s), F32),
                   jax.ShapeDtypeStruct((GATE_W, s), BF16), jax.ShapeDtypeStruct((MID_W, s), BF16)]
        + [jax.ShapeDtypeStruct(sh, BF16 if n < 3 else F32) for n, sh in enumerate(acc_shapes)],
        scratch_shapes=[pltpu.VMEM((CHUNK, SGU_W), F32)] + [pltpu.VMEM(sh, F32) for sh in acc_shapes[:3]],
        compiler_params=_params(),
    )(x, tgt, o, hm, hg, woa, wob, wout, ln_g, ln_b, sg_g, sg_b, w_s, bsb)


def _lat_bwd(dq, dk, dv, hl, rc, rsl, rsh, g_q, g_kv, wuq, wk, wv, after):
    s = dk.shape[0]
    ts = ROW_TILE
    qk_w = HEADS * HEAD_PAD

    def body(dq_ref, dk_ref, dv_ref, hl_ref, rc_ref, rsl_ref, rsh_ref, gq_ref, gkv_ref, wuq_ref, wk_ref, wv_ref,
             after_ref, dhl_ref, dhlt_ref, dwuq_ref, dwk_ref, dwv_ref, dgq_ref, dgkv_ref, dbl_ref):
        i = pl.program_id(0)

        @pl.when(i == 0)
        def _():
            for r in (dwuq_ref, dwk_ref, dwv_ref, dgq_ref, dgkv_ref, dbl_ref):
                r[...] = jnp.zeros_like(r)

        def emit(lo, val):
            vb = val.astype(BF16)
            n = val.shape[1]
            dhl_ref[:, lo:lo + n] = vb
            dhlt_ref[lo:lo + n, :] = vb.T
            dbl_ref[:, lo:lo + n] += jnp.sum(val, axis=0, keepdims=True)

        c, sl, sh = rc_ref[...], rsl_ref[...], rsh_ref[...]
        lane = lax.broadcasted_iota(jnp.int32, (ts, HEAD_PAD), 1)
        pe = (lane >= NOPE) & (lane < QK_DIM)
        dkpe = jnp.zeros((ts, HEAD_PAD), F32)
        dqu = []
        for hd in range(HEADS):
            lanes = slice(hd * HEAD_PAD, (hd + 1) * HEAD_PAD)
            dqu.append(_rope_t(dq_ref[lanes, :].T, c, sl, sh).astype(BF16))
            dkpe = dkpe + dk_ref[:, lanes]
        dqu = jnp.concatenate(dqu, axis=1)
        dkpe = _rope_t(jnp.where(pe, dkpe, 0.0), c, sl, sh)

        cq = hl_ref[:, :Q_RANK]
        rq = lax.rsqrt(jnp.mean(cq * cq, axis=-1, keepdims=True) + RMS_EPS)
        cqh = cq * rq
        cqn = (cqh * gq_ref[...]).astype(BF16)
        dwuq_ref[...] += _dot_tn(cqn, dqu)
        dcqn = _dot_nt(dqu, wuq_ref[...])
        dgq_ref[...] += jnp.sum(dcqn * cqh, axis=0, keepdims=True)
        dch = dcqn * gq_ref[...]
        emit(0, rq * (dch - cqh * jnp.mean(dch * cqh, axis=-1, keepdims=True)))

        ckv = hl_ref[:, Q_RANK:Q_RANK + KV_RANK]
        rk = lax.rsqrt(jnp.mean(ckv * ckv, axis=-1, keepdims=True) + RMS_EPS)
        ckh = ckv * rk
        ckn = (ckh * gkv_ref[...]).astype(BF16)
        dkb = dk_ref[...].astype(BF16)
        dvb = dv_ref[...].astype(BF16)
        dwk_ref[...] += _dot_tn(ckn, dkb)
        dwv_ref[...] += _dot_tn(ckn, dvb)
        dckn = _dot_nt(dkb, wk_ref[...]) + _dot_nt(dvb, wv_ref[...])
        dgkv_ref[...] += jnp.sum(dckn * ckh, axis=0, keepdims=True)
        dkh = dckn * gkv_ref[...]
        emit(Q_RANK, rk * (dkh - ckh * jnp.mean(dkh * ckh, axis=-1, keepdims=True)))
        emit(Q_RANK + KV_RANK, dkpe)

    acc_shapes = [wuq.shape, wk.shape, wv.shape, g_q.shape, g_kv.shape, (1, LAT_W)]
    return pl.pallas_call(
        body, name="lat_bwd", grid=(s // ts,),
        in_specs=[pl.BlockSpec((qk_w, ts), lambda i: (0, i)), _row_spec(ts, qk_w), _row_spec(ts, MLA_W),
                  _row_spec(ts, LAT_W), _row_spec(ts, HEAD_PAD), _row_spec(ts, HEAD_PAD), _row_spec(ts, HEAD_PAD),
                  _full_spec(g_q.shape), _full_spec(g_kv.shape), _full_spec(wuq.shape), _full_spec(wk.shape),
                  _full_spec(wv.shape), pl.BlockSpec(memory_space=pl.ANY)],
        out_specs=[_row_spec(ts, LAT_W), pl.BlockSpec((LAT_W, ts), lambda i: (0, i))]
        + [_full_spec(sh) for sh in acc_shapes],
        out_shape=[jax.ShapeDtypeStruct((s, LAT_W), BF16), jax.ShapeDtypeStruct((LAT_W, s), BF16)]
        + [jax.ShapeDtypeStruct(sh, F32) for sh in acc_shapes],
        compiler_params=_params(),
    )(dq, dk, dv, hl, rc, rsl, rsh, g_q, g_kv, wuq, wk, wv, after)


def _dx(dr, dhg, dhm, dhl, wt, wlat, after):
    s = dr.shape[0]
    ts = MATMUL_ROW_TILE

    tk = D_MODEL
    bounds = ([(ROW_GATE + r0, ROW_GATE + r0 + tk) for r0 in range(0, GATE_W, tk)]
              + [(LAT_COLS + r0, LAT_COLS + r0 + tk) for r0 in range(0, MID_W, tk)])

    def body(dr_ref, dhg_ref, dhm_ref, dhl_ref, wt_hbm, wlat_ref, after_ref, dx_ref, wt_ref, sems):
        copies = [pltpu.make_async_copy(wt_hbm.at[lo:hi], wt_ref.at[lo:hi], sems.at[n])
                  for n, (lo, hi) in enumerate(bounds)]

        def compute(first):
            acc = (ALPHA * dr_ref[...] + _dot(dhl_ref[:, 0:Q_RANK + KV_RANK], wlat_ref[0:Q_RANK + KV_RANK, :])
                   + _dot(dhl_ref[:, Q_RANK + KV_RANK:], _kpe_rows(wlat_ref)))
            for n, (lo, hi) in enumerate(bounds):
                if first:
                    copies[n].wait()
                if lo >= ROW_GATE:
                    acc += _dot(dhg_ref[:, lo - ROW_GATE:hi - ROW_GATE], wt_ref[lo:hi, :])
                else:
                    acc += _dot(dhm_ref[:, lo - LAT_COLS:hi - LAT_COLS], wt_ref[lo:hi, :])
            dx_ref[...] = acc

        @pl.when(pl.program_id(0) == 0)
        def _():
            for cp in copies:
                cp.start()
            compute(True)

        @pl.when(pl.program_id(0) > 0)
        def _():
            compute(False)

    return pl.pallas_call(
        body, name="dx", grid=(s // ts,),
        in_specs=[_row_spec(ts, D_MODEL), _row_spec(ts, GATE_W), _row_spec(ts, MID_W), _row_spec(ts, LAT_W),
                  HBM_SPEC, _full_spec(wlat.shape), HBM_SPEC],
        out_specs=_row_spec(ts, D_MODEL),
        out_shape=jax.ShapeDtypeStruct((s, D_MODEL), F32),
        scratch_shapes=[pltpu.VMEM(wt.shape, BF16), pltpu.SemaphoreType.DMA((len(bounds),))],
        compiler_params=_params(),
    )(dr, dhg, dhm, dhl, wt, wlat, after)


def _dwt_early(dhmt, dhgt, xb, col, after, name):
    tn = 512
    nm, ng = MID_W // tn, GATE_W // tn
    s = dhmt.shape[1]
    hc = D_MODEL // 2

    ks = s // 2

    def body(col_ref, dma_ref, dmb_ref, dga_ref, dgb_ref, xba_ref, xbb_ref, after_ref, dw_ref):
        i = pl.program_id(0)

        @pl.when(i < nm)
        def _():
            dw_ref[...] = (_dot(dma_ref[...], xba_ref[...]) + _dot(dmb_ref[...], xbb_ref[...])).astype(BF16)

        @pl.when(i >= nm)
        def _():
            dw_ref[...] = (_dot(dga_ref[...], xba_ref[...]) + _dot(dgb_ref[...], xbb_ref[...])).astype(BF16)

    def dh_spec(first, part):
        if first:
            return pl.BlockSpec((tn, ks), lambda i, col_ref: (jnp.minimum(i, nm - 1), part))
        return pl.BlockSpec((tn, ks), lambda i, col_ref: (jnp.maximum(i - nm, 0), part))

    rows = pl.pallas_call(
        body, name=name,
        grid_spec=pltpu.PrefetchScalarGridSpec(
            num_scalar_prefetch=1, grid=(nm + ng,),
            in_specs=[dh_spec(True, 0), dh_spec(True, 1), dh_spec(False, 0), dh_spec(False, 1),
                      pl.BlockSpec((None, ks, hc), lambda i, col_ref: (col_ref[0], 0, 0)),
                      pl.BlockSpec((None, ks, hc), lambda i, col_ref: (col_ref[0], 1, 0)),
                      pl.BlockSpec(memory_space=pl.ANY)],
            out_specs=pl.BlockSpec((pl.Element(tn), pl.Element(hc)),
                                   lambda i, col_ref: (pl.multiple_of(LAT_COLS + i * tn, 32), 0))),
        out_shape=jax.ShapeDtypeStruct((IN_W, hc), BF16),
        compiler_params=_params(),
    )(col, dhmt, dhmt, dhgt, dhgt, xb, xb, after)

    def zero(buf_ref, out_ref):
        out_ref[...] = jnp.zeros_like(out_ref)

    return pl.pallas_call(
        zero, name=name + "_zero_lat", grid=(1,), in_specs=[pl.BlockSpec(memory_space=pl.ANY)],
        out_specs=pl.BlockSpec((LAT_COLS, hc), lambda i: (0, 0)),
        out_shape=jax.ShapeDtypeStruct((IN_W, hc), BF16), input_output_aliases={0: 0},
    )(rows)


def _dwt_lat(dhlt, xb):
    n, s = dhlt.shape

    def body(dht_ref, xb_ref, dw_ref):
        dw = _dot(dht_ref[...], xb_ref[...]).astype(BF16)
        kpe = Q_RANK + KV_RANK + NOPE
        dw_ref[0:Q_RANK + KV_RANK, :] = dw[0:Q_RANK + KV_RANK]
        dw_ref[Q_RANK + KV_RANK:LAT_COLS, :] = dw[kpe:kpe + ROPE]
        dw_ref[LAT_COLS:, :] = jnp.zeros((LAT_ROWS_PAD - LAT_COLS, D_MODEL), BF16)

    return pl.pallas_call(
        body, name="dwt_lat", in_specs=[VMEM_SPEC, VMEM_SPEC], out_specs=VMEM_SPEC,
        out_shape=jax.ShapeDtypeStruct((LAT_ROWS_PAD, D_MODEL), BF16),
        compiler_params=pltpu.CompilerParams(vmem_limit_bytes=VMEM_LIMIT),
    )(dhlt, xb)


def _split_bias(b):
    z = lambda n: jnp.zeros((n,), b.dtype)
    lat = jnp.concatenate([b[:Q_RANK + KV_RANK], z(NOPE), b[Q_RANK + KV_RANK:LAT_COLS], z(HEAD_PAD - QK_DIM)])
    return b[None, ROW_GATE:], b[None, LAT_COLS:ROW_GATE], lat[None, :]


def _join_bias(g, m, l):
    kpe = Q_RANK + KV_RANK + NOPE
    return jnp.concatenate([l[0, :Q_RANK + KV_RANK], l[0, kpe:kpe + ROPE], m[0], g[0]])


def _rope_tables(positions):
    half = ROPE // 2
    inv_freq = ROPE_THETA ** (-jnp.arange(0, ROPE, 2, dtype=F32) / ROPE)
    ang = positions.astype(F32)[:, None] * inv_freq
    cos, sin = jnp.cos(ang), jnp.sin(ang)
    n = positions.shape[0]
    one, zero = jnp.ones((n, NOPE), F32), jnp.zeros((n, half), F32)
    tail1, tail0 = jnp.ones((n, HEAD_PAD - QK_DIM), F32), jnp.zeros((n, HEAD_PAD - QK_DIM), F32)
    z64 = jnp.zeros((n, NOPE), F32)
    rc = jnp.concatenate([one, cos, cos, tail1], axis=1)
    rsl = jnp.concatenate([z64, -sin, zero, tail0], axis=1)
    rsh = jnp.concatenate([z64, zero, sin, tail0], axis=1)
    return rc, rsl, rsh


def _pad_heads(w_uq):
    return jnp.pad(w_uq, ((0, 0), (0, 0), (0, HEAD_PAD - QK_DIM))).reshape(w_uq.shape[0], HEADS * HEAD_PAD)


def _prep_local(b_in, g_q, g_kv, w_ukv):
    b_g, b_m, b_l = _split_bias(b_in)
    wk = jnp.pad(w_ukv[:, :, :NOPE], ((0, 0), (0, 0), (0, HEAD_PAD - NOPE))).reshape(KV_RANK, HEADS * HEAD_PAD).astype(BF16)
    wv = w_ukv[:, :, NOPE:].reshape(KV_RANK, MLA_W).astype(BF16)
    return dict(b_g=b_g, b_m=b_m, b_l=b_l, wk=wk, wv=wv, gq2=g_q[None, :], gkv2=g_kv[None, :])


def _local_attention(x, tables, wlat, prep, wuq, after):
    rc, rsl, rsh = tables
    b_g, b_m, b_l, wk, wv, gq2, gkv2 = (prep[n] for n in ("b_g", "b_m", "b_l", "wk", "wv", "gq2", "gkv2"))
    hl, q, k, v, xb, qt, kt, vt, xb2 = _fwd_lat(x, wlat, b_l, gq2, wuq, gkv2, wk, wv, rc, rsl, rsh, after)
    o, lse = _attn_fwd(qt, k, vt)
    return dict(q=q, qt=qt, k=k, kt=kt, v=v, o=o, lse=lse, hl=hl, rc=rc, rsl=rsl, rsh=rsh, gq2=gq2, gkv2=gkv2,
                wuq=wuq, wk=wk, wv=wv, xb=xb, xb2=xb2, b_g=b_g, b_m=b_m, wlat=wlat)


def _local_head(st, x, tgt, wt, w_oa, sg_g, sg_b, w_s, b_s, w_ob, w_out, ln_g, ln_b):
    q, qt, k, kt, v, o, lse, hl, xb = (st[n] for n in ("q", "qt", "k", "kt", "v", "o", "lse", "hl", "xb"))
    rc, rsl, rsh, gq2, gkv2, wuq, wk, wv = (st[n] for n in ("rc", "rsl", "rsh", "gq2", "gkv2", "wuq", "wk", "wv"))
    bsb = jnp.repeat(b_s.T, V_DIM, axis=1)
    hg, hm = _fwd_rest(xb, wt, st["b_g"], st["b_m"])
    (dr, dhg, dhm, do, dot, delta, dhgt, dhmt, dwout, dwoa, dwob, dws, dbs, dlng, dlnb, dsgg, dsgb, loss, dbg,
     dbm) = _mid(x, tgt, o, hm, hg, w_oa, w_ob, w_out, ln_g[None, :], ln_b[None, :], sg_g[None, :], sg_b[None, :],
                 w_s, bsb)
    early = {
        "w_oa": dwoa, "sgu_ln_g": dsgg[0], "sgu_ln_b": dsgb[0], "w_s": dws, "b_s": dbs[:, :GROUPS].T,
        "w_ob": dwob, "w_out": dwout, "ln_g": dlng[0], "ln_b": dlnb[0],
    }
    state = dict(q=q, qt=qt, k=k, kt=kt, v=v, do=do, dot=dot, lse=lse, delta=delta, hl=hl, rc=rc, rsl=rsl, rsh=rsh,
                 gq2=gq2, gkv2=gkv2, wuq=wuq, wk=wk, wv=wv, dr=dr, dhg=dhg, dhm=dhm, wt=wt, xb=xb, dbg=dbg, dbm=dbm,
                 dhgt=dhgt, dhmt=dhmt, xb2=st["xb2"], wlat=st["wlat"])
    return loss, early, state


def _local_attn_bwd(st, after):
    return _attn_bwd(st["q"], st["qt"], st["k"], st["kt"], st["v"], st["do"], st["dot"], st["lse"], st["delta"],
                     after)


def _local_tail(st, dq, dk, dv, after):
    dhl, dhlt, dwuq, dwk, dwv, dgq, dgkv, dbl = _lat_bwd(dq, dk, dv, st["hl"], st["rc"], st["rsl"], st["rsh"],
                                                         st["gq2"], st["gkv2"], st["wuq"], st["wk"], st["wv"], after)
    late = {
        "w_lat": _dwt_lat(dhlt, st["xb"]),
        "b_in": _join_bias(st["dbg"], st["dbm"], dbl),
        "g_q": dgq[0],
        "w_uq": dwuq.reshape(Q_RANK, HEADS, HEAD_PAD)[:, :, :QK_DIM],
        "g_kv": dgkv[0],
        "w_ukv": jnp.concatenate([dwk.reshape(KV_RANK, HEADS, HEAD_PAD)[:, :, :NOPE],
                                  dwv.reshape(KV_RANK, HEADS, V_DIM)], axis=2),
    }
    return dhl, late


def _local_step(x, positions, tgt, wt, b_in, g_q, w_uq, g_kv, w_ukv, w_oa, sg_g, sg_b, w_s, b_s, w_ob, w_out, ln_g,
                ln_b):
    st = _local_attention(x, _rope_tables(positions), wt[:LAT_COLS], _prep_local(b_in, g_q, g_kv, w_ukv),
                          _pad_heads(w_uq).astype(BF16), b_in)
    loss, early, st = _local_head(st, x, tgt, wt, w_oa, sg_g, sg_b, w_s, b_s, w_ob, w_out, ln_g, ln_b)
    dq, dk, dv = _local_attn_bwd(st, loss)
    dhl, late = _local_tail(st, dq, dk, dv, dv)
    dx = _dx(st["dr"], st["dhg"], st["dhm"], dhl, st["wt"], st["wlat"], dhl)
    grads = {**early, **late}
    halves = [_dwt_early(st["dhmt"], st["dhgt"], st["xb2"], jnp.full((1,), h, jnp.int32), dhl, "dwt_half%d" % h)
              for h in range(2)]
    grads["w_in"] = jnp.concatenate([grads.pop("w_lat")[:LAT_COLS], jnp.concatenate(halves, axis=1)[LAT_COLS:]],
                                    axis=0)
    return loss, dx, grads


MESH = pl.DeviceIdType.MESH
N_CHIPS = 4
HBM_SPEC = pl.BlockSpec(memory_space=pl.ANY)
HBM_SPEC_STRICT = pl.BlockSpec(memory_space=pltpu.HBM)
VMEM_SPEC = pl.BlockSpec(memory_space=pltpu.VMEM)

REP_ROWS = 80


def _rows8(a):
    flat = a.reshape(-1)
    n = -(-flat.shape[0] // (8 * D_MODEL)) * 8 * D_MODEL
    return jnp.pad(flat, (0, n - flat.shape[0])).reshape(-1, D_MODEL)


def _place():
    x, y, c = lax.axis_index("x"), lax.axis_index("y"), lax.axis_index("c")
    others = [(1 - x, y), (x, 1 - y), (1 - x, 1 - y)]
    return x, y, c, others


N_DEV = 8
LOSS_TILE = (8, 128)


def _cast_own(shards, me, after):
    n = len(shards)

    def body(me_ref, *refs):
        for w in range(n):
            refs[n + 1 + w][...] = refs[w][...].astype(BF16)

    return pl.pallas_call(
        body, name="cast_own",
        grid_spec=pltpu.PrefetchScalarGridSpec(
            num_scalar_prefetch=1, grid=(1,),
            in_specs=[pl.BlockSpec(s.shape, lambda i, me_ref: (0, 0)) for s in shards]
            + [pl.BlockSpec(memory_space=pl.ANY)],
            out_specs=[pl.BlockSpec((None,) + s.shape, lambda i, me_ref: (me_ref[0], 0, 0)) for s in shards]),
        out_shape=[jax.ShapeDtypeStruct((N_CHIPS,) + s.shape, BF16) for s in shards],
        compiler_params=pltpu.CompilerParams(vmem_limit_bytes=VMEM_LIMIT),
    )(me, *shards, after)


def _cast_first(lat, uq, me):
    def body(me_ref, lat_ref, uq_ref, wlat_ref, guq_ref):
        wlat_ref[...] = lat_ref[...].astype(BF16)
        guq_ref[...] = uq_ref[...].astype(BF16)

    return pl.pallas_call(
        body, name="cast_first",
        grid_spec=pltpu.PrefetchScalarGridSpec(
            num_scalar_prefetch=1, grid=(1,),
            in_specs=[pl.BlockSpec((LAT_COLS, D_MODEL), lambda i, me_ref: (0, 0)),
                      pl.BlockSpec(uq.shape, lambda i, me_ref: (0, 0))],
            out_specs=[pl.BlockSpec((LAT_COLS, D_MODEL), lambda i, me_ref: (0, 0)),
                       pl.BlockSpec((None,) + uq.shape, lambda i, me_ref: (me_ref[0], 0, 0))]),
        out_shape=[jax.ShapeDtypeStruct((LAT_COLS, D_MODEL), BF16),
                   jax.ShapeDtypeStruct((N_CHIPS,) + uq.shape, BF16)],
    )(me, lat, uq)


def _first_copies(wlat_ref, guq_ref, send_sems, recv_sems, shapes):
    x, y, c, others = _place()
    me = 2 * x + y
    hl, hu = shapes[0][1] // 2, shapes[1][2] // 2
    lat_half = wlat_ref.at[:, pl.ds(c * hl, hl)]

    def copy(src, dst, k, to):
        return pltpu.make_async_remote_copy(src_ref=src, dst_ref=dst, send_sem=send_sems.at[k],
                                            recv_sem=recv_sems.at[k], device_id=to, device_id_type=MESH)

    def uq_half(chip):
        return guq_ref.at[chip, :, pl.ds(c * hu, hu)]

    lat_out = [copy(lat_half, lat_half, j, (*others[j], c)) for j in range(3)]
    uq_out = [copy(uq_half(me), uq_half(me), 3 + j, (*others[j], c)) for j in range(3)]
    j0 = jnp.maximum(x + 2 * y - 1, 0)
    lat_in = copy(lat_half, lat_half, j0, (0, 0, c))
    uq_in = [copy(uq_half(me), uq_half(2 * px + py), 3 + j, (px, py, c)) for j, (px, py) in enumerate(others)]
    return me, lat_out, uq_out, lat_in, uq_in


def _first_start(wlat, guq):
    shapes = (wlat.shape, guq.shape)

    def body(wlat_ref, guq_ref, send_sems, recv_sems, wlat_thru, guq_thru, token):
        me, lat_out, uq_out, _, _ = _first_copies(wlat_ref, guq_ref, send_sems, recv_sems, shapes)

        @pl.when(me == 0)
        def _():
            for cp in lat_out:
                cp.start()

        for cp in uq_out:
            cp.start()
        token[...] = jnp.zeros_like(token)

    outs = pl.pallas_call(
        body, name="first_start",
        out_shape=(pltpu.SemaphoreType.DMA((6,)), pltpu.SemaphoreType.DMA((6,)), pltpu.HBM(wlat.shape, BF16),
                   pltpu.HBM(guq.shape, BF16), jax.ShapeDtypeStruct(LOSS_TILE, F32)),
        in_specs=[HBM_SPEC_STRICT] * 2, out_specs=(SEM_SPEC, SEM_SPEC, HBM_SPEC_STRICT, HBM_SPEC_STRICT, VMEM_SPEC),
        input_output_aliases={0: 2, 1: 3},
        compiler_params=pltpu.CompilerParams(has_side_effects=SPLIT_EFFECT),
    )(pltpu.with_memory_space_constraint(wlat, pltpu.HBM), pltpu.with_memory_space_constraint(guq, pltpu.HBM))
    return outs


def _first_wait(send_sems, recv_sems, wlat, guq, *after):
    shapes = (wlat.shape, guq.shape)

    def body(wlat_ref, guq_ref, send_sems, recv_sems, *rest):
        me, lat_out, uq_out, lat_in, uq_in = _first_copies(wlat_ref, guq_ref, send_sems, recv_sems, shapes)

        @pl.when(me == 0)
        def _():
            for cp in lat_out:
                cp.wait_send()

        @pl.when(me != 0)
        def _():
            lat_in.wait_recv()

        for cp in uq_out:
            cp.wait_send()
        for cp in uq_in:
            cp.wait_recv()

    return pl.pallas_call(
        body, name="first_wait", out_shape=(pltpu.HBM(wlat.shape, BF16), pltpu.HBM(guq.shape, BF16)),
        in_specs=[HBM_SPEC_STRICT, HBM_SPEC_STRICT, SEM_SPEC, SEM_SPEC] + [HBM_SPEC] * len(after),
        out_specs=(HBM_SPEC_STRICT, HBM_SPEC_STRICT), input_output_aliases={0: 0, 1: 1},
        compiler_params=pltpu.CompilerParams(has_side_effects=SPLIT_EFFECT),
    )(wlat, guq, send_sems, recv_sems, *after)


def _first_forward(wlat, guq):
    hl, hu = wlat.shape[1] // 2, guq.shape[2] // 2

    def body(wlat_in, guq_in, wlat_ref, guq_ref, send_sems, recv_sems):
        x, y, c, others = _place()
        me = 2 * x + y
        sibling = (x, y, 1 - c)

        def copy(part, k):
            return pltpu.make_async_remote_copy(src_ref=part, dst_ref=part, send_sem=send_sems.at[k],
                                                recv_sem=recv_sems.at[k], device_id=sibling, device_id_type=MESH)

        def uq_part(j, half):
            px, py = others[j]
            return guq_ref.at[2 * px + py, :, pl.ds(half * hu, hu)]

        cps = [copy(uq_part(j, c), j) for j in range(3)]
        for cp in cps:
            cp.start()

        @pl.when(me != 0)
        def _():
            mine = copy(wlat_ref.at[:, pl.ds(c * hl, hl)], 3)
            mine.start()
            copy(wlat_ref.at[:, pl.ds((1 - c) * hl, hl)], 3).wait_recv()
            mine.wait_send()

        for j in range(3):
            copy(uq_part(j, 1 - c), j).wait_recv()
        for cp in cps:
            cp.wait_send()

    return pl.pallas_call(
        body, name="first_forward", in_specs=[HBM_SPEC, HBM_SPEC], out_specs=[HBM_SPEC, HBM_SPEC],
        out_shape=[jax.ShapeDtypeStruct(wlat.shape, BF16), jax.ShapeDtypeStruct(guq.shape, BF16)],
        input_output_aliases={0: 0, 1: 1},
        scratch_shapes=[pltpu.SemaphoreType.DMA((4,)), pltpu.SemaphoreType.DMA((4,))],
    )(wlat, guq)


def _gather_copy(b_ref, buf, w, j, src_chip, dst_chip, to, rows, send_sems, recv_sems):
    _, _, c, _ = _place()
    hc = _half(buf)
    return pltpu.make_async_remote_copy(
        src_ref=b_ref.at[src_chip, rows, pl.ds(c * hc, hc)], dst_ref=b_ref.at[dst_chip, rows, pl.ds(c * hc, hc)],
        send_sem=send_sems.at[3 * w + j], recv_sem=recv_sems.at[3 * w + j], device_id=to, device_id_type=MESH)


def _gather_rows(buf, w, chip, fn):
    if w != 0:
        fn(slice(None))
        return
    pl.when(chip == 0)(lambda: fn(pl.ds(LAT_COLS, buf.shape[1] - LAT_COLS)))
    pl.when(chip != 0)(lambda: fn(slice(None)))


def _gather_start(bufs, after):
    n = len(bufs)

    def body(*refs):
        b_refs = refs[:n]
        send_sems, recv_sems, token = refs[n + 1], refs[n + 2], refs[-1]
        x, y, c, others = _place()
        me = 2 * x + y
        for w in range(n):
            def start(rows, w=w):
                for j, (px, py) in enumerate(others):
                    _gather_copy(b_refs[w], bufs[w], w, j, me, me, (px, py, c), rows, send_sems, recv_sems).start()
            _gather_rows(bufs[w], w, me, start)
        token[...] = jnp.zeros_like(token)

    hbm = [pltpu.HBM(b.shape, BF16) for b in bufs]
    outs = pl.pallas_call(
        body, name="gather_start",
        out_shape=(pltpu.SemaphoreType.DMA((3 * n,)), pltpu.SemaphoreType.DMA((3 * n,)), *hbm,
                   jax.ShapeDtypeStruct(LOSS_TILE, F32)),
        in_specs=[HBM_SPEC_STRICT] * n + [HBM_SPEC],
        out_specs=(SEM_SPEC, SEM_SPEC, *[HBM_SPEC_STRICT] * n, VMEM_SPEC),
        input_output_aliases={i: 2 + i for i in range(n)},
        compiler_params=pltpu.CompilerParams(has_side_effects=SPLIT_EFFECT),
    )(*[pltpu.with_memory_space_constraint(b, pltpu.HBM) for b in bufs], after)
    return outs[0], outs[1], list(outs[2:2 + n]), outs[-1]


def _gather_wait(send_sems, recv_sems, bufs, after):
    n = len(bufs)

    def body(*refs):
        b_refs = refs[:n]
        send_sems, recv_sems = refs[n], refs[n + 1]
        x, y, c, others = _place()
        me = 2 * x + y
        for w in range(n):
            for j, (px, py) in enumerate(others):
                def copy(rows, w=w, j=j, px=px, py=py):
                    return _gather_copy(b_refs[w], bufs[w], w, j, me, 2 * px + py, (px, py, c), rows, send_sems,
                                        recv_sems)
                _gather_rows(bufs[w], w, me, lambda rows, copy=copy: copy(rows).wait_send())
                _gather_rows(bufs[w], w, 2 * px + py, lambda rows, copy=copy: copy(rows).wait_recv())

    outs = pl.pallas_call(
        body, name="gather_wait", out_shape=tuple(pltpu.HBM(b.shape, b.dtype) for b in bufs),
        in_specs=[HBM_SPEC_STRICT] * n + [SEM_SPEC, SEM_SPEC, HBM_SPEC],
        out_specs=tuple([HBM_SPEC_STRICT] * n), input_output_aliases={i: i for i in range(n)},
        compiler_params=pltpu.CompilerParams(has_side_effects=SPLIT_EFFECT),
    )(*bufs, send_sems, recv_sems, after)
    return list(outs)


def _gather_finish(bufs):
    n = len(bufs)

    def body(*refs):
        b_refs = refs[n:2 * n]
        send_sems, recv_sems = refs[2 * n:]
        x, y, c, others = _place()
        cps = []
        for w in range(n):
            hc = _half(bufs[w])
            for j, (px, py) in enumerate(others):
                part = b_refs[w].at[2 * px + py, :, pl.ds(c * hc, hc)]
                cps.append(pltpu.make_async_remote_copy(
                    src_ref=part, dst_ref=part, send_sem=send_sems.at[3 * w + j], recv_sem=recv_sems.at[3 * w + j],
                    device_id=(x, y, 1 - c), device_id_type=MESH))
        for cp in cps:
            cp.start()
        for w in range(n):
            hc = _half(bufs[w])
            for j, (px, py) in enumerate(others):
                theirs = b_refs[w].at[2 * px + py, :, pl.ds((1 - c) * hc, hc)]
                pltpu.make_async_remote_copy(
                    src_ref=theirs, dst_ref=theirs, send_sem=send_sems.at[3 * w + j], recv_sem=recv_sems.at[3 * w + j],
                    device_id=(x, y, 1 - c), device_id_type=MESH).wait_recv()
        for cp in cps:
            cp.wait_send()

    return pl.pallas_call(
        body, name="gather_finish", in_specs=[HBM_SPEC] * n, out_specs=[HBM_SPEC] * n,
        out_shape=[jax.ShapeDtypeStruct(b.shape, b.dtype) for b in bufs],
        input_output_aliases={i: i for i in range(n)},
        scratch_shapes=[pltpu.SemaphoreType.DMA((3 * n,)), pltpu.SemaphoreType.DMA((3 * n,))],
    )(*bufs)


def _half(a):
    return a.shape[-1] // 2


def _exchange_pairs(parts, name):
    n = len(parts)

    def body(*refs):
        p_refs, r_refs = refs[:n], refs[n:2 * n]
        send_sems, recv_sems = refs[2 * n:]
        x, y, c, _ = _place()
        cps = []
        for w in range(n):
            h = _half(parts[w])
            cps.append(pltpu.make_async_remote_copy(
                src_ref=p_refs[w].at[:, :, pl.ds((1 - c) * h, h)], dst_ref=r_refs[w],
                send_sem=send_sems.at[w], recv_sem=recv_sems.at[w], device_id=(x, y, 1 - c), device_id_type=MESH))
        for cp in cps:
            cp.start()
        for cp in cps:
            cp.wait()

    return pl.pallas_call(
        body, name=name, in_specs=[HBM_SPEC] * n, out_specs=[HBM_SPEC] * n,
        out_shape=[jax.ShapeDtypeStruct((N_CHIPS, p.shape[1], _half(p)), BF16) for p in parts],
        scratch_shapes=[pltpu.SemaphoreType.DMA((n,)), pltpu.SemaphoreType.DMA((n,))],
    )(*parts)


def _sibling_part(ref, w, n_whole, shape, c):
    if w < n_whole:
        return ref
    h = shape[-1] // 2
    return ref.at[:, :, pl.ds((1 - c) * h, h)]


def _pairs_start(parts, all_loss, n_whole):
    n = len(parts)

    def body(*refs):
        p_refs, r_refs, loss_ref = refs[:n], refs[n:2 * n], refs[2 * n]
        send_sems, recv_sems, token = refs[2 * n + 1], refs[2 * n + 2], refs[-1]
        x, y, c, _ = _place()
        for w in range(n):
            h = _half(parts[w])
            pltpu.make_async_remote_copy(
                src_ref=_sibling_part(p_refs[w], w, n_whole, parts[w].shape, c), dst_ref=r_refs[w],
                send_sem=send_sems.at[w], recv_sem=recv_sems.at[w], device_id=(x, y, 1 - c),
                device_id_type=MESH).start()
        me = 4 * x + 2 * y + c
        for t in range(1, N_DEV):
            d = (me + t) % N_DEV
            pltpu.make_async_remote_copy(
                src_ref=loss_ref.at[me], dst_ref=loss_ref.at[me], send_sem=send_sems.at[n + t - 1],
                recv_sem=recv_sems.at[n + t - 1], device_id=(d // 4, (d // 2) % 2, d % 2), device_id_type=MESH).start()
        token[...] = jnp.zeros_like(token)

    lands = [pltpu.HBM(p.shape if w < n_whole else (N_CHIPS, p.shape[1], _half(p)), BF16)
             for w, p in enumerate(parts)]
    nsem = n + N_DEV - 1
    outs = pl.pallas_call(
        body, name="pairs_start",
        out_shape=(pltpu.SemaphoreType.DMA((nsem,)), pltpu.SemaphoreType.DMA((nsem,)),
                   *[pltpu.HBM(p.shape, p.dtype) for p in parts], *lands, pltpu.HBM(all_loss.shape, F32),
                   jax.ShapeDtypeStruct(LOSS_TILE, F32)),
        in_specs=[HBM_SPEC_STRICT] * (2 * n + 1),
        out_specs=(SEM_SPEC, SEM_SPEC, *[HBM_SPEC_STRICT] * (2 * n + 1), VMEM_SPEC),
        input_output_aliases={i: 2 + i for i in range(2 * n + 1)},
        compiler_params=pltpu.CompilerParams(has_side_effects=SPLIT_EFFECT),
    )(*[pltpu.with_memory_space_constraint(p, pltpu.HBM) for p in parts],
      *[pltpu.with_memory_space_constraint(lax.empty(l.shape, BF16), pltpu.HBM) for l in lands],
      pltpu.with_memory_space_constraint(all_loss, pltpu.HBM))
    return outs[0], outs[1], list(outs[2:2 + n]), list(outs[2 + n:2 + 2 * n]), outs[2 + 2 * n], outs[-1]


def _pairs_wait(send_sems, recv_sems, parts, lands, all_loss, after, n_whole):
    n = len(parts)

    def body(*refs):
        p_refs, r_refs, loss_ref = refs[:n], refs[n:2 * n], refs[2 * n]
        send_sems, recv_sems = refs[2 * n + 1], refs[2 * n + 2]
        x, y, c, _ = _place()
        for w in range(n):
            h = _half(parts[w])
            cp = pltpu.make_async_remote_copy(
                src_ref=_sibling_part(p_refs[w], w, n_whole, parts[w].shape, c), dst_ref=r_refs[w],
                send_sem=send_sems.at[w],
                recv_sem=recv_sems.at[w], device_id=(x, y, 1 - c), device_id_type=MESH)
            cp.wait_send()
            cp.wait_recv()
        me = 4 * x + 2 * y + c
        for t in range(1, N_DEV):
            d = (me + N_DEV - t) % N_DEV
            cp = pltpu.make_async_remote_copy(
                src_ref=loss_ref.at[me], dst_ref=loss_ref.at[d], send_sem=send_sems.at[n + t - 1],
                recv_sem=recv_sems.at[n + t - 1], device_id=(d // 4, (d // 2) % 2, d % 2), device_id_type=MESH)
            cp.wait_send()
            cp.wait_recv()

    bufs = (*parts, *lands, all_loss)
    outs = pl.pallas_call(
        body, name="pairs_wait", out_shape=tuple(pltpu.HBM(a.shape, a.dtype) for a in bufs),
        in_specs=[HBM_SPEC_STRICT] * len(bufs) + [SEM_SPEC, SEM_SPEC, HBM_SPEC],
        out_specs=tuple([HBM_SPEC_STRICT] * len(bufs)), input_output_aliases={i: i for i in range(len(bufs))},
        compiler_params=pltpu.CompilerParams(has_side_effects=SPLIT_EFFECT),
    )(*bufs, send_sems, recv_sems, after)
    return list(outs[:n]), list(outs[n:2 * n]), outs[2 * n]


def _add_pair(ps, rs, c, name, n_whole=0):
    n = len(ps)

    def body(c_ref, *refs):
        for w in range(n):
            if w < n_whole:
                mine = refs[w][...]
            else:
                h = _half(ps[w])
                mine = refs[w][:, :, pl.ds(pl.multiple_of(c_ref[0] * h, 128), h)]
            refs[2 * n + w][...] = (mine.astype(F32) + refs[n + w][...].astype(F32)).astype(BF16)

    return pl.pallas_call(
        body, name=name,
        in_specs=[pl.BlockSpec(memory_space=pltpu.SMEM)] + [VMEM_SPEC] * (2 * n), out_specs=[VMEM_SPEC] * n,
        out_shape=[jax.ShapeDtypeStruct(r.shape, BF16) for r in rs],
        compiler_params=pltpu.CompilerParams(vmem_limit_bytes=VMEM_LIMIT),
    )(c, *ps, *rs)


SEM_SPEC = pl.BlockSpec(memory_space=pltpu.SEMAPHORE)
SPLIT_EFFECT = pltpu.SideEffectType.DATAFLOW_SIDE_EFFECTING


def _chips_start(qs, name):
    n = len(qs)

    def body(*refs):
        q_refs, land_refs = refs[:n], refs[n:2 * n]
        send_sems, recv_sems, token = refs[2 * n], refs[2 * n + 1], refs[-1]
        x, y, c, others = _place()
        me = 2 * x + y
        for w in range(n):
            for j, (px, py) in enumerate(others):
                pltpu.make_async_remote_copy(
                    src_ref=q_refs[w].at[2 * px + py], dst_ref=land_refs[w].at[me], send_sem=send_sems.at[3 * w + j],
                    recv_sem=recv_sems.at[3 * w + j], device_id=(px, py, c), device_id_type=MESH).start()
        token[...] = jnp.zeros_like(token)

    hbm = [pltpu.HBM(q.shape, BF16) for q in qs]
    outs = pl.pallas_call(
        body, name=name,
        out_shape=(pltpu.SemaphoreType.DMA((3 * n,)), pltpu.SemaphoreType.DMA((3 * n,)), *hbm, *hbm,
                   jax.ShapeDtypeStruct(LOSS_TILE, F32)),
        in_specs=[HBM_SPEC_STRICT] * (2 * n),
        out_specs=(SEM_SPEC, SEM_SPEC, *[HBM_SPEC_STRICT] * (2 * n), VMEM_SPEC),
        input_output_aliases={i: 2 + i for i in range(2 * n)},
        compiler_params=pltpu.CompilerParams(has_side_effects=SPLIT_EFFECT),
    )(*[pltpu.with_memory_space_constraint(q, pltpu.HBM) for q in qs],
      *[pltpu.with_memory_space_constraint(lax.empty(q.shape, BF16), pltpu.HBM) for q in qs])
    return outs[0], outs[1], outs[2:2 + n], outs[2 + n:2 + 2 * n], outs[-1]


def _chips_wait(send_sems, recv_sems, q_thru, land_thru, after, name):
    n = len(q_thru)

    def body(*refs):
        q_refs, land_refs = refs[:n], refs[n:2 * n]
        send_sems, recv_sems = refs[2 * n], refs[2 * n + 1]
        x, y, c, others = _place()
        me = 2 * x + y
        for w in range(n):
            for j, (px, py) in enumerate(others):
                cp = pltpu.make_async_remote_copy(
                    src_ref=q_refs[w].at[2 * px + py], dst_ref=land_refs[w].at[2 * px + py],
                    send_sem=send_sems.at[3 * w + j], recv_sem=recv_sems.at[3 * w + j], device_id=(px, py, c),
                    device_id_type=MESH)
                cp.wait_send()
                cp.wait_recv()

    outs = pl.pallas_call(
        body, name=name, out_shape=tuple(pltpu.HBM(a.shape, a.dtype) for a in (*q_thru, *land_thru)),
        in_specs=[HBM_SPEC_STRICT] * (2 * n) + [SEM_SPEC, SEM_SPEC, HBM_SPEC],
        out_specs=tuple([HBM_SPEC_STRICT] * (2 * n)), input_output_aliases={i: i for i in range(2 * n)},
        compiler_params=pltpu.CompilerParams(has_side_effects=SPLIT_EFFECT),
    )(*q_thru, *land_thru, send_sems, recv_sems, after)
    return list(outs[:n]), list(outs[n:])


def _sum_chips(qs, rs, idx, all_dtypes):
    n = len(rs)
    n_all = len(all_dtypes)

    def body(idx_ref, *refs):
        c = idx_ref[4]
        for w in range(n):
            q_ref, r_ref, g_ref = refs[w], refs[n + w], refs[2 * n + w]
            acc = q_ref[idx_ref[0]].astype(F32)
            for t in range(1, N_CHIPS):
                acc = acc + r_ref[idx_ref[t]].astype(F32)
            h = rs[w].shape[2]
            mine = pl.ds(pl.multiple_of(c * h, 128), h)
            g_ref[...] = jnp.zeros_like(g_ref)
            if w >= n - n_all:
                g_ref[idx_ref[0], :, mine] = acc.astype(g_ref.dtype)
            else:
                g_ref[:, mine] = acc

    shapes = [jax.ShapeDtypeStruct((r.shape[1], 2 * r.shape[2]), F32) for r in rs[:n - n_all]]
    shapes += [jax.ShapeDtypeStruct((N_CHIPS, r.shape[1], 2 * r.shape[2]), dt)
               for r, dt in zip(rs[n - n_all:], all_dtypes)]
    return pl.pallas_call(
        body, name="sum_chips",
        in_specs=[pl.BlockSpec(memory_space=pltpu.SMEM)] + [VMEM_SPEC] * (2 * n), out_specs=[VMEM_SPEC] * n,
        out_shape=shapes, compiler_params=pltpu.CompilerParams(vmem_limit_bytes=VMEM_LIMIT),
    )(idx, *qs, *rs)


def _share(shards, alls):
    n, na = len(shards), len(alls)
    total = n + na

    def body(*refs):
        g_refs, a_refs = refs[total:total + n], refs[total + n:2 * total]
        send_sems, recv_sems = refs[2 * total:]
        x, y, c, others = _place()
        me = 2 * x + y
        sibling = (x, y, 1 - c)

        def cols_of(w, half):
            h = shards[w].shape[1] // 2
            return g_refs[w].at[:, pl.ds(half * h, h)]

        def slab(a, chip, half):
            h = alls[a].shape[2] // 2
            return a_refs[a].at[chip, :, pl.ds(half * h, h)]

        def copy(src, dst, k, to):
            return pltpu.make_async_remote_copy(src_ref=src, dst_ref=dst, send_sem=send_sems.at[k],
                                                recv_sem=recv_sems.at[k], device_id=to, device_id_type=MESH)

        cps = [copy(cols_of(w, c), cols_of(w, c), w, sibling) for w in range(n)]
        for a in range(na):
            base = n + 7 * a
            cps.append(copy(slab(a, me, c), slab(a, me, c), base, sibling))
            for j, (px, py) in enumerate(others):
                cps.append(copy(slab(a, me, c), slab(a, me, c), base + 1 + j, (px, py, c)))
        for cp in cps:
            cp.start()
        fwd = []
        for a in range(na):
            base = n + 7 * a
            for j, (px, py) in enumerate(others):
                chip = 2 * px + py
                copy(slab(a, me, c), slab(a, chip, c), base + 1 + j, (px, py, c)).wait_recv()
                cp = copy(slab(a, chip, c), slab(a, chip, c), base + 4 + j, sibling)
                cp.start()
                fwd.append(cp)
        for a in range(na):
            base = n + 7 * a
            for j, (px, py) in enumerate(others):
                chip = 2 * px + py
                copy(slab(a, chip, c), slab(a, chip, 1 - c), base + 4 + j, sibling).wait_recv()
            copy(slab(a, me, c), slab(a, me, 1 - c), base, sibling).wait_recv()
        for w in range(n):
            copy(cols_of(w, c), cols_of(w, 1 - c), w, sibling).wait_recv()
        for cp in cps + fwd:
            cp.wait_send()

    nsem = n + 7 * na
    return pl.pallas_call(
        body, name="share", in_specs=[HBM_SPEC] * total, out_specs=[HBM_SPEC] * total,
        out_shape=[jax.ShapeDtypeStruct(a.shape, a.dtype) for a in (*shards, *alls)],
        input_output_aliases={i: i for i in range(total)},
        scratch_shapes=[pltpu.SemaphoreType.DMA((nsem,)), pltpu.SemaphoreType.DMA((nsem,))],
    )(*shards, *alls)


def _adamw(w, g, m, v):
    m2 = ADAM_B1 * m + (1.0 - ADAM_B1) * g
    v2 = ADAM_B2 * v + (1.0 - ADAM_B2) * (g * g)
    m_hat = m2 / (1.0 - ADAM_B1 ** ADAM_STEP)
    v_hat = v2 / (1.0 - ADAM_B2 ** ADAM_STEP)
    return -ADAM_LR * (m_hat / (jnp.sqrt(v_hat) + ADAM_EPS) + ADAM_WD * w), m2, v2


def _update_w_in(wt, gt, mt, vt, lat, owner, tile):
    nlat = lat.shape[0] // tile

    def body(owner_ref, w_ref, g_ref, m_ref, v_ref, lat_ref, g2_ref, d_ref, m2_ref, v2_ref):
        row = pl.program_id(0) * tile + lax.broadcasted_iota(jnp.int32, (tile, 1), 0)
        g = jnp.where((row < LAT_COLS) & (owner_ref[0] == 1), lat_ref[...].astype(F32), g_ref[...])
        g2_ref[...] = g
        d_ref[...], m2_ref[...], v2_ref[...] = _adamw(w_ref[...], g, m_ref[...], v_ref[...])

    spec = pl.BlockSpec((tile, wt.shape[1]), lambda i, o: (i, 0))
    return pl.pallas_call(
        body, name="update_w_in",
        grid_spec=pltpu.PrefetchScalarGridSpec(
            num_scalar_prefetch=1, grid=(wt.shape[0] // tile,),
            in_specs=[spec] * 4 + [pl.BlockSpec((tile, wt.shape[1]), lambda i, o: (jnp.minimum(i, nlat - 1), 0))],
            out_specs=[spec] * 4),
        out_shape=[jax.ShapeDtypeStruct(wt.shape, F32)] * 4,
        compiler_params=_params(("parallel",)),
    )(owner, wt, gt, mt, vt, lat)


def _update_small(ws, gs, ms, vs):
    n = len(ws)

    def body(*refs):
        for k in range(n):
            w_ref, g_ref, m_ref, v_ref = refs[k], refs[n + k], refs[2 * n + k], refs[3 * n + k]
            d, m2, v2 = _adamw(w_ref[...], g_ref[...], m_ref[...], v_ref[...])
            refs[4 * n + k][...] = d
            refs[5 * n + k][...] = m2
            refs[6 * n + k][...] = v2

    shapes = [jax.ShapeDtypeStruct(w.shape, F32) for w in ws]
    outs = pl.pallas_call(
        body, name="update_small", in_specs=[VMEM_SPEC] * (4 * n), out_specs=[VMEM_SPEC] * (3 * n),
        out_shape=shapes * 3,
        compiler_params=pltpu.CompilerParams(vmem_limit_bytes=VMEM_LIMIT),
    )(*ws, *gs, *ms, *vs)
    return outs[:n], outs[n:2 * n], outs[2 * n:]


REPLICATED = ("b_in", "g_q", "g_kv", "w_ukv", "sgu_ln_g", "sgu_ln_b", "w_s", "b_s", "ln_g", "ln_b")
ORDER = ("w_in", "b_in", "g_q", "w_uq", "g_kv", "w_ukv", "w_oa", "sgu_ln_g", "sgu_ln_b", "w_s", "b_s", "w_ob", "w_out",
         "ln_g", "ln_b")


def kernel(x, positions, w_in, b_in, g_q, w_uq, g_kv, w_ukv, w_oa, sgu_ln_g, sgu_ln_b, w_s, b_s, w_ob, w_out, ln_g, ln_b, loss_target, m_w_in, m_b_in, m_g_q, m_w_uq, m_g_kv, m_w_ukv, m_w_oa, m_sgu_ln_g, m_sgu_ln_b, m_w_s, m_b_s, m_w_ob, m_w_out, m_ln_g, m_ln_b, v_w_in, v_b_in, v_g_q, v_w_uq, v_g_kv, v_w_ukv, v_w_oa, v_sgu_ln_g, v_sgu_ln_b, v_w_s, v_b_s, v_w_ob, v_w_out, v_ln_g, v_ln_b):
    w = dict(w_in=w_in, b_in=b_in, g_q=g_q, w_uq=w_uq, g_kv=g_kv, w_ukv=w_ukv, w_oa=w_oa, sgu_ln_g=sgu_ln_g,
             sgu_ln_b=sgu_ln_b, w_s=w_s, b_s=b_s, w_ob=w_ob, w_out=w_out, ln_g=ln_g, ln_b=ln_b)
    m = dict(w_in=m_w_in, b_in=m_b_in, g_q=m_g_q, w_uq=m_w_uq, g_kv=m_g_kv, w_ukv=m_w_ukv, w_oa=m_w_oa,
             sgu_ln_g=m_sgu_ln_g, sgu_ln_b=m_sgu_ln_b, w_s=m_w_s, b_s=m_b_s, w_ob=m_w_ob, w_out=m_w_out, ln_g=m_ln_g,
             ln_b=m_ln_b)
    v = dict(w_in=v_w_in, b_in=v_b_in, g_q=v_g_q, w_uq=v_w_uq, g_kv=v_g_kv, w_ukv=v_w_ukv, w_oa=v_w_oa,
             sgu_ln_g=v_sgu_ln_g, sgu_ln_b=v_sgu_ln_b, w_s=v_w_s, b_s=v_b_s, w_ob=v_w_ob, w_out=v_w_out, ln_g=v_ln_g,
             ln_b=v_ln_b)
    w, m, v = ({n: a[0] for n, a in d.items()} for d in (w, m, v))
    c = lax.axis_index("c")

    wt_shard, mt_shard, vt_shard = (jnp.transpose(d["w_in"]) for d in (w, m, v))
    xi, yi = lax.axis_index("x"), lax.axis_index("y")
    me1 = (2 * xi + yi).reshape(1).astype(jnp.int32)
    first = _first_start(*_cast_first(wt_shard, _pad_heads(w["w_uq"]), me1))
    tables = _rope_tables(positions[0])
    prep = _prep_local(w["b_in"], w["g_q"], w["g_kv"], w["w_ukv"])
    bufs = _cast_own([wt_shard, w["w_oa"], w["w_ob"], w["w_out"]], me1, first[4])
    send0, recv0, bufs, token0 = _gather_start(bufs, first[4])
    g_lat, g_uq = _first_forward(*_first_wait(*first[:4], token0, *tables, *prep.values()))
    st = _local_attention(x[0], tables, g_lat, prep, g_uq.reshape(Q_RANK, HEADS * HEAD_PAD), token0)
    g_in, g_oa, g_ob, g_out = _gather_finish(_gather_wait(send0, recv0, bufs, st["o"]))
    wt = g_in.reshape(IN_W, D_MODEL)

    loss, early, st = _local_head(
        st, x[0], loss_target[0], wt, g_oa, w["sgu_ln_g"], w["sgu_ln_b"], w["w_s"], w["b_s"], g_ob,
        g_out.reshape(D_MODEL, D_MODEL), w["ln_g"], w["ln_b"])

    c1 = c.reshape(1).astype(jnp.int32)
    idx = jnp.stack([2 * xi + yi, 2 * (1 - xi) + yi, 2 * xi + (1 - yi), 2 * (1 - xi) + (1 - yi), c]).astype(jnp.int32)
    slabs = lambda a: a.reshape(N_CHIPS, IN_W // N_CHIPS, D_MODEL // 2)
    theirs = slabs(_dwt_early(st["dhmt"], st["dhgt"], st["xb2"], 1 - c1, c1, "dwt_theirs"))
    parts1 = [theirs, early["w_oa"].astype(BF16), early["w_ob"].astype(BF16),
              early["w_out"].reshape(N_CHIPS, SLAB_W, D_MODEL).astype(BF16)]
    my_loss = lax.dynamic_update_slice(jnp.zeros((N_DEV,) + LOSS_TILE, F32), jnp.broadcast_to(loss, (1,) + LOSS_TILE),
                                       (4 * xi + 2 * yi + c, 0, 0))
    sems0 = _pairs_start(parts1, my_loss, 1)
    mine = slabs(_dwt_early(st["dhmt"], st["dhgt"], st["xb2"], c1, sems0[5], "dwt_mine"))
    parts1, recv1, all_loss = _pairs_wait(*sems0[:5], mine, 1)
    pairs1 = _add_pair([mine, *parts1[1:]], recv1, c1, "add_pair_early", n_whole=1)
    sems1 = _chips_start(pairs1, "chips_start_early")
    dq, dk, dv = _local_attn_bwd(st, sems1[4])
    dhl, late = _local_tail(st, dq, dk, dv, dv)

    grads = {**early, **late}
    rep = jnp.concatenate([_rows8(grads[n]) for n in REPLICATED], axis=0)
    rep = jnp.pad(rep, ((0, N_CHIPS * REP_ROWS - rep.shape[0]), (0, 0))).reshape(N_CHIPS, REP_ROWS, D_MODEL)
    parts2 = [late["w_uq"].reshape(N_CHIPS, Q_RANK // N_CHIPS, HEADS * QK_DIM).astype(BF16), rep.astype(BF16),
              late["w_lat"].reshape(N_CHIPS, LAT_ROWS_PAD // N_CHIPS, D_MODEL)]
    pairs2 = _add_pair(parts2, _exchange_pairs(parts2, "exchange_pairs_late"), c1, "add_pair_late")
    sems2 = _chips_start(pairs2, "chips_start_late")
    dx = _dx(st["dr"], st["dhg"], st["dhm"], dhl, st["wt"], st["wlat"], sems2[4])
    pairs2, landed2 = _chips_wait(*sems2[:4], dx, "chips_wait_late")
    pairs1, landed1 = _chips_wait(*sems1[:4], landed2[0], "chips_wait_early")
    sums = _sum_chips([*pairs1, *pairs2], [*landed1, *landed2], idx, (F32, BF16))
    *shards, g_rep, g_lat = _share(sums[:-2], sums[-2:])
    loss = jnp.sum(all_loss[:, 0, 0])

    red = {n: s.reshape(w[n].shape) for n, s in zip(("w_oa", "w_ob", "w_out", "w_uq"), shards[1:])}
    g_rep = g_rep.reshape(N_CHIPS * REP_ROWS, D_MODEL)
    off = 0
    for n in REPLICATED:
        rows = _rows8(w[n]).shape[0]
        red[n] = g_rep[off:off + rows].reshape(-1)[:w[n].size].reshape(w[n].shape)
        off += rows
    owner = (2 * xi + yi == 0).astype(jnp.int32).reshape(1)
    gt, dt, mt, vt2 = _update_w_in(wt_shard, shards[0], mt_shard, vt_shard,
                                   g_lat.reshape(LAT_ROWS_PAD, D_MODEL).astype(F32), owner, 232)
    red["w_in"] = jnp.transpose(gt)
    small = [n for n in ORDER if n != "w_in"]
    as2d = lambda a: a.reshape(-1, a.shape[-1])
    ds, ms, vs = _update_small([as2d(w[n]) for n in small], [as2d(red[n]) for n in small],
                               [as2d(m[n]) for n in small], [as2d(v[n]) for n in small])
    delta, new_m, new_v = {"w_in": jnp.transpose(dt)}, {"w_in": jnp.transpose(mt)}, {"w_in": jnp.transpose(vt2)}
    for i, n in enumerate(small):
        delta[n], new_m[n], new_v[n] = (a[i].reshape(w[n].shape) for a in (ds, ms, vs))

    lead = lambda a: a[None]
    return (loss, dx[None], *[lead(red[n]) for n in ORDER], *[lead(delta[n]) for n in ORDER],
            *[lead(new_m[n]) for n in ORDER], *[lead(new_v[n]) for n in ORDER])
```

```python
import math

import jax
import jax.numpy as jnp
from jax import lax
from jax.experimental import pallas as pl
from jax.experimental.pallas import tpu as pltpu

F32 = jnp.float32
BF16 = jnp.bfloat16

D_MODEL = 1024
HEADS = 8
Q_RANK = 384
KV_RANK = 128
NOPE = 64
ROPE = 32
V_DIM = 64
QK_DIM = NOPE + ROPE
HEAD_PAD = 128
MLA_W = HEADS * V_DIM
SGU_W = 512
GROUPS = 8
CHUNK = 128
IN_W = 4640
RMS_EPS = 1e-6
LN_EPS = 1e-5
ALPHA = 2.0 ** 0.25
ROPE_THETA = 10000.0
SCALE = QK_DIM ** -0.5

GATE_W = 2 * D_MODEL
MID_W = 4 * SGU_W
LAT_W = Q_RANK + KV_RANK + HEAD_PAD
LAT_COLS = Q_RANK + KV_RANK + ROPE
ROW_GATE = LAT_COLS + MID_W
LAT_ROWS_PAD = 704
N_SLABS = 4
SLAB_W = D_MODEL // N_SLABS

ROW_TILE = 256
MATMUL_ROW_TILE = 512
MID_ROW_TILE = 256
ATT_TK = 256
ATT_BWD_TK = 256
SUM_ROWS = 16
LOG2E = 1.4426950408889634
LN2 = 0.6931471805599453
Q_SCALE = SCALE * LOG2E
VMEM_LIMIT = 56 * 1024 * 1024

ADAM_LR = 0.001
ADAM_B1 = 0.9
ADAM_B2 = 0.999
ADAM_EPS = 1e-08
ADAM_WD = 0.01
ADAM_STEP = 10


def _dot(a, b):
    return jnp.dot(a, b, preferred_element_type=F32)


def _dot_nt(a, b):
    return lax.dot_general(a, b, (((1,), (1,)), ((), ())), preferred_element_type=F32)


def _dot_tn(a, b):
    return lax.dot_general(a, b, (((0,), (0,)), ((), ())), preferred_element_type=F32)


def _sigmoid(z):
    return 0.5 * jnp.tanh(0.5 * z) + 0.5


_GELU_C = math.sqrt(2.0 / math.pi)


def _gelu_and_grad(x):
    x2 = x * x
    t = jnp.tanh(_GELU_C * (x + 0.044715 * x * x2))
    g = 0.5 * x * (1.0 + t)
    dg = 0.5 * (1.0 + t) + 0.5 * x * (1.0 - t * t) * (_GELU_C * (1.0 + 3.0 * 0.044715 * x2))
    return g, dg


def _silu_and_grad(z):
    s = _sigmoid(z)
    return z * s, s * (1.0 + z * (1.0 - s))


def _rope(xb, c, sl, sh):
    return xb * c + pltpu.roll(xb, 112, 1) * sl + pltpu.roll(xb, 16, 1) * sh


def _rope_t(dy, c, sl, sh):
    return dy * c + pltpu.roll(dy * sl, 16, 1) + pltpu.roll(dy * sh, 112, 1)


def _params(sem=("arbitrary",)):
    return pltpu.CompilerParams(dimension_semantics=sem, vmem_limit_bytes=VMEM_LIMIT)


def _row_spec(tile, width):
    return pl.BlockSpec((tile, width), lambda i: (i, 0))


def _full_spec(shape):
    nd = len(shape)
    return pl.BlockSpec(shape, lambda i: (0,) * nd)


def _kpe_rows(wt_ref):
    z = lambda n: jnp.zeros((n, D_MODEL), BF16)
    return jnp.concatenate([z(NOPE), wt_ref[Q_RANK + KV_RANK:LAT_COLS, :], z(HEAD_PAD - QK_DIM)], axis=0)


def _fwd_rest(xb2, wt, b_g, b_m):
    s = xb2.shape[1]
    ts = MATMUL_ROW_TILE
    tn = D_MODEL
    blocks = ([(ROW_GATE + c0, 0, c0) for c0 in range(0, GATE_W, tn)]
              + [(LAT_COLS + c0, 1, c0) for c0 in range(0, MID_W, tn)])

    def body(xb_ref, wt_hbm, bg_ref, bm_ref, hg_ref, hm_ref, wt_ref, sems):
        copies = [pltpu.make_async_copy(wt_hbm.at[lo:lo + tn], wt_ref.at[lo:lo + tn], sems.at[n])
                  for n, (lo, _, _) in enumerate(blocks)]

        def compute(first):
            xb_ = jnp.concatenate([xb_ref[0], xb_ref[1]], axis=1)
            for cp, (lo, which, c0) in zip(copies, blocks):
                if first:
                    cp.wait()
                out_ref, b_ref = ((hg_ref, bg_ref), (hm_ref, bm_ref))[which]
                out_ref[:, c0:c0 + tn] = (_dot_nt(xb_, wt_ref[lo:lo + tn, :]) + b_ref[:, c0:c0 + tn]).astype(BF16)

        @pl.when(pl.program_id(0) == 0)
        def _():
            for cp in copies:
                cp.start()
            compute(True)

        @pl.when(pl.program_id(0) > 0)
        def _():
            compute(False)

    return pl.pallas_call(
        body, name="fwd_rest", grid=(s // ts,),
        in_specs=[pl.BlockSpec((2, ts, D_MODEL // 2), lambda i: (0, i, 0)), HBM_SPEC, _full_spec(b_g.shape),
                  _full_spec(b_m.shape)],
        out_specs=[_row_spec(ts, GATE_W), _row_spec(ts, MID_W)],
        out_shape=[jax.ShapeDtypeStruct((s, GATE_W), BF16), jax.ShapeDtypeStruct((s, MID_W), BF16)],
        scratch_shapes=[pltpu.VMEM(wt.shape, BF16), pltpu.SemaphoreType.DMA((len(blocks),))],
        compiler_params=_params(),
    )(xb2, wt, b_g, b_m)


def _fwd_lat(x, wlat, b_l, g_q, wuq, g_kv, wk, wv, rc, rsl, rsh, after):
    s = x.shape[0]
    ts = ROW_TILE

    def body(x_ref, wt_ref, bl_ref, gq_ref, wuq_ref, gkv_ref, wk_ref, wv_ref, rc_ref, rsl_ref,
             rsh_ref, after_ref, hl_ref, q_ref, k_ref, v_ref, qt_ref, kt_ref, vt_ref, xb2_ref):
        xb = x_ref[...].astype(BF16)
        xb2_ref[0] = xb[:, :D_MODEL // 2]
        xb2_ref[1] = xb[:, D_MODEL // 2:]
        hl = jnp.concatenate([_dot_nt(xb, wt_ref[0:Q_RANK + KV_RANK, :]), _dot_nt(xb, _kpe_rows(wt_ref))],
                             axis=1) + bl_ref[...]
        hl_ref[...] = hl
        c, sl, sh = rc_ref[...], rsl_ref[...], rsh_ref[...]
        cq = hl[:, :Q_RANK]
        cqn = cq * lax.rsqrt(jnp.mean(cq * cq, axis=-1, keepdims=True) + RMS_EPS) * gq_ref[...]
        q = _dot(cqn.astype(BF16), wuq_ref[...])
        ckv = hl[:, Q_RANK:Q_RANK + KV_RANK]
        ckvn = (ckv * lax.rsqrt(jnp.mean(ckv * ckv, axis=-1, keepdims=True) + RMS_EPS) * gkv_ref[...]).astype(BF16)
        k = _dot(ckvn, wk_ref[...])
        vb = _dot(ckvn, wv_ref[...]).astype(BF16)
        v_ref[...] = vb
        vt_ref[...] = vb.T
        kpe = _rope(hl[:, Q_RANK + KV_RANK:], c, sl, sh)
        for hd in range(HEADS):
            lanes = slice(hd * HEAD_PAD, (hd + 1) * HEAD_PAD)
            qb = (_rope(q[:, lanes], c, sl, sh) * Q_SCALE).astype(BF16)
            kb = (k[:, lanes] + kpe).astype(BF16)
            q_ref[:, lanes] = qb
            k_ref[:, lanes] = kb
            qt_ref[lanes, :] = qb.T
            kt_ref[lanes, :] = kb.T

    qk_w = HEADS * HEAD_PAD
    col_spec = lambda rows: pl.BlockSpec((rows, ts), lambda i: (0, i))
    return pl.pallas_call(
        body, name="fwd_lat", grid=(s // ts,),
        in_specs=[_row_spec(ts, D_MODEL), _full_spec(wlat.shape),
                  _full_spec(b_l.shape), _full_spec(g_q.shape),
                  _full_spec(wuq.shape), _full_spec(g_kv.shape), _full_spec(wk.shape), _full_spec(wv.shape),
                  _row_spec(ts, HEAD_PAD), _row_spec(ts, HEAD_PAD), _row_spec(ts, HEAD_PAD),
                  pl.BlockSpec(memory_space=pl.ANY)],
        out_specs=[_row_spec(ts, LAT_W), _row_spec(ts, qk_w),
                   _row_spec(ts, qk_w), _row_spec(ts, MLA_W), col_spec(qk_w), col_spec(qk_w),
                   col_spec(MLA_W), pl.BlockSpec((2, ts, D_MODEL // 2), lambda i: (0, i, 0))],
        out_shape=[jax.ShapeDtypeStruct((s, LAT_W), F32), jax.ShapeDtypeStruct((s, qk_w), BF16),
                   jax.ShapeDtypeStruct((s, qk_w), BF16), jax.ShapeDtypeStruct((s, MLA_W), BF16),
                   jax.ShapeDtypeStruct((qk_w, s), BF16),
                   jax.ShapeDtypeStruct((qk_w, s), BF16), jax.ShapeDtypeStruct((MLA_W, s), BF16),
                   jax.ShapeDtypeStruct((2, s, D_MODEL // 2), BF16)],
        compiler_params=_params(),
    )(x, wlat, b_l, g_q, wuq, g_kv, wk, wv, rc, rsl, rsh, after)


def _attn_fwd(qt, k, vt):
    s = k.shape[0]
    tk = ATT_TK
    nk = s // tk
    pairs = HEADS // 2

    def body(qt_ref, k_ref, vt_ref, o_ref, lse_ref):
        qts = [qt_ref[hh * HEAD_PAD:(hh + 1) * HEAD_PAD, :] for hh in range(2)]
        ones = jnp.ones((SUM_ROWS, tk), BF16)

        def scores(j):
            return tuple(_dot(k_ref[j * tk:(j + 1) * tk, hh * HEAD_PAD:(hh + 1) * HEAD_PAD], qts[hh][:, j * tk:])
                         for hh in range(2))

        def weighted(j, ps):
            return tuple(_dot(jnp.concatenate([vt_ref[hh * V_DIM:(hh + 1) * V_DIM, j * tk:(j + 1) * tk], ones], axis=0),
                              ps[hh]) for hh in range(2))

        def from_lane(full, lo, part):
            return part if lo == 0 else jnp.concatenate([full[:, :lo], part], axis=1)

        krow = lax.broadcasted_iota(jnp.int32, (tk, tk), 0)
        qcol = lax.broadcasted_iota(jnp.int32, (tk, tk), 1)
        st = scores(0)
        ps = None
        stats = [(jnp.full((1, s), -jnp.inf, F32), jnp.zeros((V_DIM + SUM_ROWS, s), F32))] * 2
        for j in range(nk):
            lo, lo_prev = j * tk, max(j - 1, 0) * tk
            st_next = scores(j + 1) if j + 1 < nk else None
            pvs = weighted(j - 1, ps) if j else None
            new_ps, new_stats = [], []
            for hh in range(2):
                m, acc = stats[hh]
                diag = jnp.where(krow <= qcol, st[hh][:, :tk], -jnp.inf)
                s_ = diag if j == nk - 1 else jnp.concatenate([diag, st[hh][:, tk:]], axis=1)
                if j:
                    acc = from_lane(acc, lo_prev, acc[:, lo_prev:] + pvs[hh])
                m_old = m[:, lo:]
                m_new = jnp.maximum(m_old, jnp.max(s_, axis=0, keepdims=True))
                a = jnp.exp2(m_old - m_new)
                p = jnp.exp2(s_ - m_new)
                new_stats.append((from_lane(m, lo, m_new), from_lane(acc, lo, a * acc[:, lo:])))
                new_ps.append(p.astype(BF16))
            st, ps, stats = st_next, new_ps, new_stats
        pvs = weighted(nk - 1, ps)
        lo = (nk - 1) * tk
        accs = [from_lane(stats[hh][1], lo, stats[hh][1][:, lo:] + pvs[hh]) for hh in range(2)]
        sums = [acc[V_DIM:V_DIM + 1, :] for acc in accs]
        ot = jnp.concatenate([accs[hh][:V_DIM, :] / sums[hh] for hh in range(2)], axis=0)
        o_ref[...] = ot.T
        lse = [stats[hh][0] + jnp.log(sums[hh]) * LOG2E for hh in range(2)]
        lse_ref[...] = jnp.concatenate(lse + [jnp.zeros((6, s), F32)], axis=0)

    return pl.pallas_call(
        body, name="attn_fwd", grid=(pairs,),
        in_specs=[pl.BlockSpec((2 * HEAD_PAD, s), lambda p: (p, 0)),
                  pl.BlockSpec((s, 2 * HEAD_PAD), lambda p: (0, p)),
                  pl.BlockSpec((2 * V_DIM, s), lambda p: (p, 0))],
        out_specs=[pl.BlockSpec((s, 2 * V_DIM), lambda p: (0, p)),
                   pl.BlockSpec((None, 8, s), lambda p: (p, 0, 0))],
        out_shape=[jax.ShapeDtypeStruct((s, MLA_W), F32), jax.ShapeDtypeStruct((pairs, 8, s), F32)],
        compiler_params=_params(("arbitrary",)),
    )(qt, k, vt)


def _attn_bwd(q, qt, k, kt, v, do, dot, lse, delta, after):
    s = k.shape[0]
    tk = ATT_BWD_TK
    nk = s // tk
    pairs = HEADS // 2

    def body(q_ref, qt_ref, k_ref, kt_ref, v_ref, do_ref, dot_ref, lse_ref, dl_ref, after_ref, dqt_ref, dk_ref,
             dv_ref):
        krow = lax.broadcasted_iota(jnp.int32, (tk, tk), 0)
        qcol = lax.broadcasted_iota(jnp.int32, (tk, tk), 1)
        lane = lax.broadcasted_iota(jnp.int32, (tk, 2 * V_DIM), 1)
        drow = lax.broadcasted_iota(jnp.int32, (2 * V_DIM, s), 0)
        dotb = dot_ref[...]
        dots = [jnp.where((drow < V_DIM) if hh == 0 else (drow >= V_DIM), dotb, jnp.zeros_like(dotb))
                for hh in range(2)]
        for j in range(nk):
            lo = j * tk
            vb = v_ref[lo:lo + tk, :]
            dob = do_ref[lo:, :]
            dvs = []
            for hh in range(2):
                rows = slice(hh * HEAD_PAD, (hh + 1) * HEAD_PAD)
                st = _dot(k_ref[lo:lo + tk, rows], qt_ref[rows, lo:])
                diag = jnp.where(krow <= qcol, st[:, :tk], -jnp.inf)
                st = diag if j == nk - 1 else jnp.concatenate([diag, st[:, tk:]], axis=1)
                p = jnp.exp2(st - lse_ref[hh:hh + 1, lo:])
                dpt = _dot(vb, dots[hh][:, lo:])
                dst = (p * (dpt - dl_ref[hh:hh + 1, lo:])).astype(BF16)
                dvs.append(_dot(p.astype(BF16), dob))
                dk_ref[lo:lo + tk, rows] = (_dot(dst, q_ref[lo:, rows]) * LN2).astype(BF16)
                dqt = _dot(kt_ref[rows, lo:lo + tk], dst)
                if j == 0:
                    dqt_ref[rows, :] = dqt
                else:
                    dqt_ref[rows, lo:] += dqt
            dv_ref[lo:lo + tk, :] = jnp.where(lane < V_DIM, dvs[0], dvs[1]).astype(BF16)
        dqt_ref[...] = dqt_ref[...] * SCALE

    pair_rows = lambda w: pl.BlockSpec((s, w), lambda p: (0, p))
    pair_cols = lambda w: pl.BlockSpec((w, s), lambda p: (p, 0))
    stats = pl.BlockSpec((None, 8, s), lambda p: (p, 0, 0))
    return pl.pallas_call(
        body, name="attn_bwd", grid=(pairs,),
        in_specs=[pair_rows(2 * HEAD_PAD), pair_cols(2 * HEAD_PAD), pair_rows(2 * HEAD_PAD), pair_cols(2 * HEAD_PAD),
                  pair_rows(2 * V_DIM), pair_rows(2 * V_DIM), pair_cols(2 * V_DIM), stats, stats,
                  pl.BlockSpec(memory_space=pl.ANY)],
        out_specs=[pair_cols(2 * HEAD_PAD), pair_rows(2 * HEAD_PAD), pair_rows(2 * V_DIM)],
        out_shape=[jax.ShapeDtypeStruct((HEADS * HEAD_PAD, s), F32), jax.ShapeDtypeStruct((s, HEADS * HEAD_PAD), BF16),
                   jax.ShapeDtypeStruct((s, MLA_W), BF16)],
        compiler_params=_params(("arbitrary",)),
    )(q, qt, k, kt, v, do, dot, lse, delta, after)


def _split3(a):
    hi = a.astype(BF16)
    r1 = a - hi.astype(F32)
    mid = r1.astype(BF16)
    lo = (r1 - mid.astype(F32)).astype(BF16)
    return hi, mid, lo


def _mid(x, tgt, o, hm, hg, woa, wob, wout, ln_g, ln_b, sg_g, sg_b, w_s, bsb):
    s = x.shape[0]
    ts = MID_ROW_TILE
    nsteps = s // ts
    nch = ts // CHUNK
    npair = GROUPS // 2

    def body(x_ref, t_ref, o_ref, hm_ref, hg_ref, woa_ref, wob_ref, wout_ref, lng_ref, lnb_ref, sgg_ref, sgb_ref,
             ws_ref, bsb_ref,
             dr_ref, dhg_ref, dhm_ref, do_ref, dot_ref, dl_ref, dhgt_ref, dhmt_ref,
             dwout_ref, dwoa_ref, dwob_ref, dws_ref, dbs_ref, dlng_ref, dlnb_ref, dsgg_ref, dsgb_ref, loss_ref,
             dbg_ref, dbm_ref, dbacc_ref, awout_ref, awoa_ref, awob_ref):
        i = pl.program_id(0)

        @pl.when(i == 0)
        def _():
            for r in (awout_ref, awoa_ref, awob_ref, dws_ref, dlng_ref, dlnb_ref, dsgg_ref, dsgb_ref, loss_ref,
                      dbg_ref, dbm_ref, dbacc_ref):
                r[...] = jnp.zeros_like(r)

        def emit(ref, tref, bref, lo, val):
            vb = val.astype(BF16)
            n = val.shape[1]
            ref[:, lo:lo + n] = vb
            tref[lo:lo + n, :] = vb.T
            bref[:, lo:lo + n] += jnp.sum(val, axis=0, keepdims=True)

        lane = lax.broadcasted_iota(jnp.int32, (CHUNK, CHUNK), 1)
        left = lane < V_DIM
        tril = lax.broadcasted_iota(jnp.int32, (CHUNK, CHUNK), 0) >= lane
        ms = [jnp.where(tril, ws_ref[g], 0.0).astype(BF16) for g in range(GROUPS)]

        z_a = hm_ref[:, 0:SGU_W].astype(F32)
        u = hm_ref[:, SGU_W:2 * SGU_W].astype(F32)
        v = hm_ref[:, 2 * SGU_W:3 * SGU_W].astype(F32)
        z_b = hm_ref[:, 3 * SGU_W:4 * SGU_W].astype(F32)
        o = o_ref[...]
        sa, dsa = _silu_and_grad(z_a)
        y_a = (o * sa).astype(BF16)
        gu, dgu = _gelu_and_grad(u)
        gv, dgv = _gelu_and_grad(v)
        mu = jnp.mean(gv, axis=-1, keepdims=True)
        vc = gv - mu
        rstd_v = lax.rsqrt(jnp.mean(vc * vc, axis=-1, keepdims=True) + LN_EPS)
        vhat = vc * rstd_v
        vn = (vhat * sgg_ref[...] + sgb_ref[...]).astype(BF16)
        rows = []
        for c in range(nch):
            blocks = []
            for p in range(npair):
                blk = vn[c * CHUNK:(c + 1) * CHUNK, p * CHUNK:(p + 1) * CHUNK]
                blocks.append(jnp.where(left, _dot(ms[2 * p], blk), _dot(ms[2 * p + 1], blk)))
            rows.append(jnp.concatenate(blocks, axis=1) + bsb_ref[...])
        mixed = jnp.concatenate(rows, axis=0)
        sgu = gu * mixed
        sb, dsb = _silu_and_grad(z_b)
        y_b = (sgu * sb).astype(BF16)
        pa = jnp.concatenate([_dot(y_a, woa_ref[k]) for k in range(N_SLABS)], axis=1)
        pb = jnp.concatenate([_dot(y_b, wob_ref[k]) for k in range(N_SLABS)], axis=1)
        sga = _sigmoid(hg_ref[:, :D_MODEL].astype(F32))
        sgb = _sigmoid(hg_ref[:, D_MODEL:].astype(F32))
        m2 = (sga * pa + sgb * pb).astype(BF16)
        r = ALPHA * x_ref[...] + _dot(m2, wout_ref[...])
        rmu = jnp.mean(r, axis=-1, keepdims=True)
        rc = r - rmu
        rstd = lax.rsqrt(jnp.mean(rc * rc, axis=-1, keepdims=True) + LN_EPS)
        xhat = rc * rstd
        y = xhat * lng_ref[...] + lnb_ref[...]
        err = y - t_ref[...]
        loss_ref[...] += jnp.full(loss_ref.shape, 0.5 / D_MODEL, F32) * jnp.sum(err * err)

        dy = err * (1.0 / D_MODEL)
        dlng_ref[...] += jnp.sum(dy * xhat, axis=0, keepdims=True)
        dlnb_ref[...] += jnp.sum(dy, axis=0, keepdims=True)
        dxh = dy * lng_ref[...]
        dr = rstd * (dxh - jnp.mean(dxh, axis=-1, keepdims=True) - xhat * jnp.mean(dxh * xhat, axis=-1, keepdims=True))
        dr_ref[...] = dr
        drb = dr.astype(BF16)
        awout_ref[...] += _dot_tn(m2, drb)
        dm2 = _dot_nt(drb, wout_ref[...])
        emit(dhg_ref, dhgt_ref, dbg_ref, 0, dm2 * pa * sga * (1.0 - sga))
        emit(dhg_ref, dhgt_ref, dbg_ref, D_MODEL, dm2 * pb * sgb * (1.0 - sgb))
        dpa = (dm2 * sga).astype(BF16)
        dpb = (dm2 * sgb).astype(BF16)
        dy_a = jnp.zeros((ts, MLA_W), F32)
        dy_b = jnp.zeros((ts, SGU_W), F32)
        y_at, y_bt = y_a.T, y_b.T
        for k in range(N_SLABS):
            cols = slice(k * SLAB_W, (k + 1) * SLAB_W)
            awoa_ref[k] += _dot(y_at, dpa[:, cols])
            awob_ref[k] += _dot(y_bt, dpb[:, cols])
            dy_a = dy_a + _dot_nt(dpa[:, cols], woa_ref[k])
            dy_b = dy_b + _dot_nt(dpb[:, cols], wob_ref[k])
        dob = (dy_a * sa).astype(BF16)
        do_ref[...] = dob
        dot_ref[...] = dob.T
        head = (lax.broadcasted_iota(jnp.int32, (HEADS, MLA_W), 1) // V_DIM
                == lax.broadcasted_iota(jnp.int32, (HEADS, MLA_W), 0)).astype(BF16)
        dl = sum(_dot_nt(head, term) for term in _split3(dob.astype(F32) * o))
        for p in range(HEADS // 2):
            dl_ref[p] = jnp.concatenate([dl[2 * p:2 * p + 2], jnp.zeros((6, ts), F32)], axis=0)
        emit(dhm_ref, dhmt_ref, dbm_ref, 0, dy_a * o * dsa)
        dsg = dy_b * sb
        emit(dhm_ref, dhmt_ref, dbm_ref, 3 * SGU_W, dy_b * sgu * dsb)
        emit(dhm_ref, dhmt_ref, dbm_ref, SGU_W, dsg * mixed * dgu)
        dmixed = dsg * gu
        dvn_rows = []
        dbs_sum = jnp.zeros((CHUNK, SGU_W), F32)
        for c in range(nch):
            dm_c = dmixed[c * CHUNK:(c + 1) * CHUNK, :]
            dbs_sum = dbs_sum + dm_c
            blocks = []
            for p in range(npair):
                dmb = dm_c[:, p * CHUNK:(p + 1) * CHUNK].astype(BF16)
                blk = vn[c * CHUNK:(c + 1) * CHUNK, p * CHUNK:(p + 1) * CHUNK]
                blocks.append(jnp.where(left, _dot_tn(ms[2 * p], dmb), _dot_tn(ms[2 * p + 1], dmb)))
                zero = jnp.zeros_like(dmb)
                dws_ref[2 * p] += jnp.where(tril, _dot_nt(jnp.where(left, dmb, zero), blk), 0.0)
                dws_ref[2 * p + 1] += jnp.where(tril, _dot_nt(jnp.where(left, zero, dmb), blk), 0.0)
            dvn_rows.append(jnp.concatenate(blocks, axis=1))
        dbacc_ref[...] += dbs_sum
        dvn = jnp.concatenate(dvn_rows, axis=0)
        dsgg_ref[...] += jnp.sum(dvn * vhat, axis=0, keepdims=True)
        dsgb_ref[...] += jnp.sum(dvn, axis=0, keepdims=True)
        dvh = dvn * sgg_ref[...]
        dgv_in = rstd_v * (dvh - jnp.mean(dvh, axis=-1, keepdims=True)
                           - vhat * jnp.mean(dvh * vhat, axis=-1, keepdims=True))
        emit(dhm_ref, dhmt_ref, dbm_ref, 2 * SGU_W, dgv_in * dgv)

        @pl.when(i == nsteps - 1)
        def _():
            dwout_ref[...] = awout_ref[...].astype(BF16)
            dwoa_ref[...] = awoa_ref[...].astype(BF16)
            dwob_ref[...] = awob_ref[...].astype(BF16)
            grp = (lax.broadcasted_iota(jnp.int32, (SGU_W, CHUNK), 0) // V_DIM
                   == lax.broadcasted_iota(jnp.int32, (SGU_W, CHUNK), 1)).astype(BF16)
            hi, mid, lo = _split3(dbacc_ref[...])
            dbs_ref[...] = _dot(hi, grp) + _dot(mid, grp) + _dot(lo, grp)

    acc_shapes = [(D_MODEL, D_MODEL), woa.shape, wob.shape, (GROUPS, CHUNK, CHUNK), (CHUNK, CHUNK),
                  (1, D_MODEL), (1, D_MODEL), (1, SGU_W), (1, SGU_W), (1, 128), (1, GATE_W), (1, MID_W)]
    col_spec = lambda rows: pl.BlockSpec((rows, ts), lambda i: (0, i))
    return pl.pallas_call(
        body, name="mid", grid=(nsteps,),
        in_specs=[_row_spec(ts, D_MODEL), _row_spec(ts, D_MODEL), _row_spec(ts, MLA_W), _row_spec(ts, MID_W),
                  _row_spec(ts, GATE_W), _full_spec(woa.shape), _full_spec(wob.shape), _full_spec(wout.shape),
                  _full_spec(ln_g.shape), _full_spec(ln_b.shape), _full_spec(sg_g.shape), _full_spec(sg_b.shape),
                  _full_spec(w_s.shape), _full_spec(bsb.shape)],
        out_specs=[_row_spec(ts, D_MODEL), _row_spec(ts, GATE_W), _row_spec(ts, MID_W), _row_spec(ts, MLA_W),
                   col_spec(MLA_W), pl.BlockSpec((HEADS // 2, 8, ts), lambda i: (0, 0, i)), col_spec(GATE_W),
                   col_spec(MID_W)]
        + [_full_spec(sh) for sh in acc_shapes],
        out_shape=[jax.ShapeDtypeStruct((s, D_MODEL), F32), jax.ShapeDtypeStruct((s, GATE_W), BF16),
                   jax.ShapeDtypeStruct((s, MID_W), BF16), jax.ShapeDtypeStruct((s, MLA_W), BF16),
                   jax.ShapeDtypeStruct((MLA_W, s), BF16), jax.ShapeDtypeStruct((HEADS // 2, 8, s), F32),
                   jax.ShapeDtypeStruct((GATE_W, s), BF16), jax.ShapeDtypeStruct((MID_W, s), BF16)]
        + [jax.ShapeDtypeStruct(sh, BF16 if n < 3 else F32) for n, sh in enumerate(acc_shapes)],
        scratch_shapes=[pltpu.VMEM((CHUNK, SGU_W), F32)] + [pltpu.VMEM(sh, F32) for sh in acc_shapes[:3]],
        compiler_params=_params(),
    )(x, tgt, o, hm, hg, woa, wob, wout, ln_g, ln_b, sg_g, sg_b, w_s, bsb)


def _lat_bwd(dq, dk, dv, hl, rc, rsl, rsh, g_q, g_kv, wuq, wk, wv, after):
    s = dk.shape[0]
    ts = ROW_TILE
    qk_w = HEADS * HEAD_PAD

    def body(dq_ref, dk_ref, dv_ref, hl_ref, rc_ref, rsl_ref, rsh_ref, gq_ref, gkv_ref, wuq_ref, wk_ref, wv_ref,
             after_ref, dhl_ref, dhlt_ref, dwuq_ref, dwk_ref, dwv_ref, dgq_ref, dgkv_ref, dbl_ref):
        i = pl.program_id(0)

        @pl.when(i == 0)
        def _():
            for r in (dwuq_ref, dwk_ref, dwv_ref, dgq_ref, dgkv_ref, dbl_ref):
                r[...] = jnp.zeros_like(r)

        def emit(lo, val):
            vb = val.astype(BF16)
            n = val.shape[1]
            dhl_ref[:, lo:lo + n] = vb
            dhlt_ref[lo:lo + n, :] = vb.T
            dbl_ref[:, lo:lo + n] += jnp.sum(val, axis=0, keepdims=True)

        c, sl, sh = rc_ref[...], rsl_ref[...], rsh_ref[...]
        lane = lax.broadcasted_iota(jnp.int32, (ts, HEAD_PAD), 1)
        pe = (lane >= NOPE) & (lane < QK_DIM)
        dkpe = jnp.zeros((ts, HEAD_PAD), F32)
        dqu = []
        for hd in range(HEADS):
            lanes = slice(hd * HEAD_PAD, (hd + 1) * HEAD_PAD)
            dqu.append(_rope_t(dq_ref[lanes, :].T, c, sl, sh).astype(BF16))
            dkpe = dkpe + dk_ref[:, lanes]
        dqu = jnp.concatenate(dqu, axis=1)
        dkpe = _rope_t(jnp.where(pe, dkpe, 0.0), c, sl, sh)

        cq = hl_ref[:, :Q_RANK]
        rq = lax.rsqrt(jnp.mean(cq * cq, axis=-1, keepdims=True) + RMS_EPS)
        cqh = cq * rq
        cqn = (cqh * gq_ref[...]).astype(BF16)
        dwuq_ref[...] += _dot_tn(cqn, dqu)
        dcqn = _dot_nt(dqu, wuq_ref[...])
        dgq_ref[...] += jnp.sum(dcqn * cqh, axis=0, keepdims=True)
        dch = dcqn * gq_ref[...]
        emit(0, rq * (dch - cqh * jnp.mean(dch * cqh, axis=-1, keepdims=True)))

        ckv = hl_ref[:, Q_RANK:Q_RANK + KV_RANK]
        rk = lax.rsqrt(jnp.mean(ckv * ckv, axis=-1, keepdims=True) + RMS_EPS)
        ckh = ckv * rk
        ckn = (ckh * gkv_ref[...]).astype(BF16)
        dkb = dk_ref[...].astype(BF16)
        dvb = dv_ref[...].astype(BF16)
        dwk_ref[...] += _dot_tn(ckn, dkb)
        dwv_ref[...] += _dot_tn(ckn, dvb)
        dckn = _dot_nt(dkb, wk_ref[...]) + _dot_nt(dvb, wv_ref[...])
        dgkv_ref[...] += jnp.sum(dckn * ckh, axis=0, keepdims=True)
        dkh = dckn * gkv_ref[...]
        emit(Q_RANK, rk * (dkh - ckh * jnp.mean(dkh * ckh, axis=-1, keepdims=True)))
        emit(Q_RANK + KV_RANK, dkpe)

    acc_shapes = [wuq.shape, wk.shape, wv.shape, g_q.shape, g_kv.shape, (1, LAT_W)]
    return pl.pallas_call(
        body, name="lat_bwd", grid=(s // ts,),
        in_specs=[pl.BlockSpec((qk_w, ts), lambda i: (0, i)), _row_spec(ts, qk_w), _row_spec(ts, MLA_W),
                  _row_spec(ts, LAT_W), _row_spec(ts, HEAD_PAD), _row_spec(ts, HEAD_PAD), _row_spec(ts, HEAD_PAD),
                  _full_spec(g_q.shape), _full_spec(g_kv.shape), _full_spec(wuq.shape), _full_spec(wk.shape),
                  _full_spec(wv.shape), pl.BlockSpec(memory_space=pl.ANY)],
        out_specs=[_row_spec(ts, LAT_W), pl.BlockSpec((LAT_W, ts), lambda i: (0, i))]
        + [_full_spec(sh) for sh in acc_shapes],
        out_shape=[jax.ShapeDtypeStruct((s, LAT_W), BF16), jax.ShapeDtypeStruct((LAT_W, s), BF16)]
        + [jax.ShapeDtypeStruct(sh, F32) for sh in acc_shapes],
        compiler_params=_params(),
    )(dq, dk, dv, hl, rc, rsl, rsh, g_q, g_kv, wuq, wk, wv, after)


def _dx(dr, dhg, dhm, dhl, wt, wlat, after):
    s = dr.shape[0]
    ts = MATMUL_ROW_TILE

    tk = D_MODEL
    bounds = ([(ROW_GATE + r0, ROW_GATE + r0 + tk) for r0 in range(0, GATE_W, tk)]
              + [(LAT_COLS + r0, LAT_COLS + r0 + tk) for r0 in range(0, MID_W, tk)])

    def body(dr_ref, dhg_ref, dhm_ref, dhl_ref, wt_hbm, wlat_ref, after_ref, dx_ref, wt_ref, sems):
        copies = [pltpu.make_async_copy(wt_hbm.at[lo:hi], wt_ref.at[lo:hi], sems.at[n])
                  for n, (lo, hi) in enumerate(bounds)]

        def compute(first):
            acc = (ALPHA * dr_ref[...] + _dot(dhl_ref[:, 0:Q_RANK + KV_RANK], wlat_ref[0:Q_RANK + KV_RANK, :])
                   + _dot(dhl_ref[:, Q_RANK + KV_RANK:], _kpe_rows(wlat_ref)))
            for n, (lo, hi) in enumerate(bounds):
                if first:
                    copies[n].wait()
                if lo >= ROW_GATE:
                    acc += _dot(dhg_ref[:, lo - ROW_GATE:hi - ROW_GATE], wt_ref[lo:hi, :])
                else:
                    acc += _dot(dhm_ref[:, lo - LAT_COLS:hi - LAT_COLS], wt_ref[lo:hi, :])
            dx_ref[...] = acc

        @pl.when(pl.program_id(0) == 0)
        def _():
            for cp in copies:
                cp.start()
            compute(True)

        @pl.when(pl.program_id(0) > 0)
        def _():
            compute(False)

    return pl.pallas_call(
        body, name="dx", grid=(s // ts,),
        in_specs=[_row_spec(ts, D_MODEL), _row_spec(ts, GATE_W), _row_spec(ts, MID_W), _row_spec(ts, LAT_W),
                  HBM_SPEC, _full_spec(wlat.shape), HBM_SPEC],
        out_specs=_row_spec(ts, D_MODEL),
        out_shape=jax.ShapeDtypeStruct((s, D_MODEL), F32),
        scratch_shapes=[pltpu.VMEM(wt.shape, BF16), pltpu.SemaphoreType.DMA((len(bounds),))],
        compiler_params=_params(),
    )(dr, dhg, dhm, dhl, wt, wlat, after)


def _dwt_early(dhmt, dhgt, xb, col, after, name):
    tn = 512
    nm, ng = MID_W // tn, GATE_W // tn
    s = dhmt.shape[1]
    hc = D_MODEL // 2

    ks = s // 2

    def body(col_ref, dma_ref, dmb_ref, dga_ref, dgb_ref, xba_ref, xbb_ref, after_ref, dw_ref):
        i = pl.program_id(0)

        @pl.when(i < nm)
        def _():
            dw_ref[...] = (_dot(dma_ref[...], xba_ref[...]) + _dot(dmb_ref[...], xbb_ref[...])).astype(BF16)

        @pl.when(i >= nm)
        def _():
            dw_ref[...] = (_dot(dga_ref[...], xba_ref[...]) + _dot(dgb_ref[...], xbb_ref[...])).astype(BF16)

    def dh_spec(first, part):
        if first:
            return pl.BlockSpec((tn, ks), lambda i, col_ref: (jnp.minimum(i, nm - 1), part))
        return pl.BlockSpec((tn, ks), lambda i, col_ref: (jnp.maximum(i - nm, 0), part))

    rows = pl.pallas_call(
        body, name=name,
        grid_spec=pltpu.PrefetchScalarGridSpec(
            num_scalar_prefetch=1, grid=(nm + ng,),
            in_specs=[dh_spec(True, 0), dh_spec(True, 1), dh_spec(False, 0), dh_spec(False, 1),
                      pl.BlockSpec((None, ks, hc), lambda i, col_ref: (col_ref[0], 0, 0)),
                      pl.BlockSpec((None, ks, hc), lambda i, col_ref: (col_ref[0], 1, 0)),
                      pl.BlockSpec(memory_space=pl.ANY)],
            out_specs=pl.BlockSpec((pl.Element(tn), pl.Element(hc)),
                                   lambda i, col_ref: (pl.multiple_of(LAT_COLS + i * tn, 32), 0))),
        out_shape=jax.ShapeDtypeStruct((IN_W, hc), BF16),
        compiler_params=_params(),
    )(col, dhmt, dhmt, dhgt, dhgt, xb, xb, after)

    def zero(buf_ref, out_ref):
        out_ref[...] = jnp.zeros_like(out_ref)

    return pl.pallas_call(
        zero, name=name + "_zero_lat", grid=(1,), in_specs=[pl.BlockSpec(memory_space=pl.ANY)],
        out_specs=pl.BlockSpec((LAT_COLS, hc), lambda i: (0, 0)),
        out_shape=jax.ShapeDtypeStruct((IN_W, hc), BF16), input_output_aliases={0: 0},
    )(rows)


def _dwt_lat(dhlt, xb2):
    def body(dht_ref, xb_ref, dw_ref):
        dht = dht_ref[...]
        dw = jnp.concatenate([_dot(dht, xb_ref[0]), _dot(dht, xb_ref[1])], axis=1).astype(BF16)
        kpe = Q_RANK + KV_RANK + NOPE
        dw_ref[0:Q_RANK + KV_RANK, :] = dw[0:Q_RANK + KV_RANK]
        dw_ref[Q_RANK + KV_RANK:LAT_COLS, :] = dw[kpe:kpe + ROPE]
        dw_ref[LAT_COLS:, :] = jnp.zeros((LAT_ROWS_PAD - LAT_COLS, D_MODEL), BF16)

    return pl.pallas_call(
        body, name="dwt_lat", in_specs=[VMEM_SPEC, VMEM_SPEC], out_specs=VMEM_SPEC,
        out_shape=jax.ShapeDtypeStruct((LAT_ROWS_PAD, D_MODEL), BF16),
        compiler_params=pltpu.CompilerParams(vmem_limit_bytes=VMEM_LIMIT),
    )(dhlt, xb2)


def _split_bias(b):
    z = lambda n: jnp.zeros((n,), b.dtype)
    lat = jnp.concatenate([b[:Q_RANK + KV_RANK], z(NOPE), b[Q_RANK + KV_RANK:LAT_COLS], z(HEAD_PAD - QK_DIM)])
    return b[None, ROW_GATE:], b[None, LAT_COLS:ROW_GATE], lat[None, :]


def _join_bias(g, m, l):
    kpe = Q_RANK + KV_RANK + NOPE
    return jnp.concatenate([l[0, :Q_RANK + KV_RANK], l[0, kpe:kpe + ROPE], m[0], g[0]])


def _rope_tables(positions):
    half = ROPE // 2
    inv_freq = ROPE_THETA ** (-jnp.arange(0, ROPE, 2, dtype=F32) / ROPE)
    ang = positions.astype(F32)[:, None] * inv_freq
    cos, sin = jnp.cos(ang), jnp.sin(ang)
    n = positions.shape[0]
    one, zero = jnp.ones((n, NOPE), F32), jnp.zeros((n, half), F32)
    tail1, tail0 = jnp.ones((n, HEAD_PAD - QK_DIM), F32), jnp.zeros((n, HEAD_PAD - QK_DIM), F32)
    z64 = jnp.zeros((n, NOPE), F32)
    rc = jnp.concatenate([one, cos, cos, tail1], axis=1)
    rsl = jnp.concatenate([z64, -sin, zero, tail0], axis=1)
    rsh = jnp.concatenate([z64, zero, sin, tail0], axis=1)
    return rc, rsl, rsh


def _pad_heads(w_uq):
    return jnp.pad(w_uq, ((0, 0), (0, 0), (0, HEAD_PAD - QK_DIM))).reshape(w_uq.shape[0], HEADS * HEAD_PAD)


def _prep_local(b_in, g_q, g_kv, w_ukv):
    b_g, b_m, b_l = _split_bias(b_in)
    wk = jnp.pad(w_ukv[:, :, :NOPE], ((0, 0), (0, 0), (0, HEAD_PAD - NOPE))).reshape(KV_RANK, HEADS * HEAD_PAD).astype(BF16)
    wv = w_ukv[:, :, NOPE:].reshape(KV_RANK, MLA_W).astype(BF16)
    return dict(b_g=b_g, b_m=b_m, b_l=b_l, wk=wk, wv=wv, gq2=g_q[None, :], gkv2=g_kv[None, :])


def _local_attention(x, tables, wlat, prep, wuq, after):
    rc, rsl, rsh = tables
    b_g, b_m, b_l, wk, wv, gq2, gkv2 = (prep[n] for n in ("b_g", "b_m", "b_l", "wk", "wv", "gq2", "gkv2"))
    hl, q, k, v, qt, kt, vt, xb2 = _fwd_lat(x, wlat, b_l, gq2, wuq, gkv2, wk, wv, rc, rsl, rsh, after)
    o, lse = _attn_fwd(qt, k, vt)
    return dict(q=q, qt=qt, k=k, kt=kt, v=v, o=o, lse=lse, hl=hl, rc=rc, rsl=rsl, rsh=rsh, gq2=gq2, gkv2=gkv2,
                wuq=wuq, wk=wk, wv=wv, xb2=xb2, b_g=b_g, b_m=b_m, wlat=wlat)


def _local_head(st, x, tgt, wt, w_oa, sg_g, sg_b, w_s, b_s, w_ob, w_out, ln_g, ln_b):
    q, qt, k, kt, v, o, lse, hl = (st[n] for n in ("q", "qt", "k", "kt", "v", "o", "lse", "hl"))
    rc, rsl, rsh, gq2, gkv2, wuq, wk, wv = (st[n] for n in ("rc", "rsl", "rsh", "gq2", "gkv2", "wuq", "wk", "wv"))
    bsb = jnp.repeat(b_s.T, V_DIM, axis=1)
    hg, hm = _fwd_rest(st["xb2"], wt, st["b_g"], st["b_m"])
    (dr, dhg, dhm, do, dot, delta, dhgt, dhmt, dwout, dwoa, dwob, dws, dbs, dlng, dlnb, dsgg, dsgb, loss, dbg,
     dbm) = _mid(x, tgt, o, hm, hg, w_oa, w_ob, w_out, ln_g[None, :], ln_b[None, :], sg_g[None, :], sg_b[None, :],
                 w_s, bsb)
    early = {
        "w_oa": dwoa, "sgu_ln_g": dsgg[0], "sgu_ln_b": dsgb[0], "w_s": dws, "b_s": dbs[:, :GROUPS].T,
        "w_ob": dwob, "w_out": dwout, "ln_g": dlng[0], "ln_b": dlnb[0],
    }
    state = dict(q=q, qt=qt, k=k, kt=kt, v=v, do=do, dot=dot, lse=lse, delta=delta, hl=hl, rc=rc, rsl=rsl, rsh=rsh,
                 gq2=gq2, gkv2=gkv2, wuq=wuq, wk=wk, wv=wv, dr=dr, dhg=dhg, dhm=dhm, wt=wt, dbg=dbg, dbm=dbm,
                 dhgt=dhgt, dhmt=dhmt, xb2=st["xb2"], wlat=st["wlat"])
    return loss, early, state


def _local_attn_bwd(st, after):
    return _attn_bwd(st["q"], st["qt"], st["k"], st["kt"], st["v"], st["do"], st["dot"], st["lse"], st["delta"],
                     after)


def _local_tail(st, dq, dk, dv, after):
    dhl, dhlt, dwuq, dwk, dwv, dgq, dgkv, dbl = _lat_bwd(dq, dk, dv, st["hl"], st["rc"], st["rsl"], st["rsh"],
                                                         st["gq2"], st["gkv2"], st["wuq"], st["wk"], st["wv"], after)
    late = {
        "w_lat": _dwt_lat(dhlt, st["xb2"]),
        "b_in": _join_bias(st["dbg"], st["dbm"], dbl),
        "g_q": dgq[0],
        "w_uq": dwuq.reshape(Q_RANK, HEADS, HEAD_PAD)[:, :, :QK_DIM],
        "g_kv": dgkv[0],
        "w_ukv": jnp.concatenate([dwk.reshape(KV_RANK, HEADS, HEAD_PAD)[:, :, :NOPE],
                                  dwv.reshape(KV_RANK, HEADS, V_DIM)], axis=2),
    }
    return dhl, late


def _local_step(x, positions, tgt, wt, b_in, g_q, w_uq, g_kv, w_ukv, w_oa, sg_g, sg_b, w_s, b_s, w_ob, w_out, ln_g,
                ln_b):
    st = _local_attention(x, _rope_tables(positions), wt[:LAT_COLS], _prep_local(b_in, g_q, g_kv, w_ukv),
                          _pad_heads(w_uq).astype(BF16), b_in)
    loss, early, st = _local_head(st, x, tgt, wt, w_oa, sg_g, sg_b, w_s, b_s, w_ob, w_out, ln_g, ln_b)
    dq, dk, dv = _local_attn_bwd(st, loss)
    dhl, late = _local_tail(st, dq, dk, dv, dv)
    dx = _dx(st["dr"], st["dhg"], st["dhm"], dhl, st["wt"], st["wlat"], dhl)
    grads = {**early, **late}
    halves = [_dwt_early(st["dhmt"], st["dhgt"], st["xb2"], jnp.full((1,), h, jnp.int32), dhl, "dwt_half%d" % h)
              for h in range(2)]
    grads["w_in"] = jnp.concatenate([grads.pop("w_lat")[:LAT_COLS], jnp.concatenate(halves, axis=1)[LAT_COLS:]],
                                    axis=0)
    return loss, dx, grads


MESH = pl.DeviceIdType.MESH
N_CHIPS = 4
HBM_SPEC = pl.BlockSpec(memory_space=pl.ANY)
HBM_SPEC_STRICT = pl.BlockSpec(memory_space=pltpu.HBM)
VMEM_SPEC = pl.BlockSpec(memory_space=pltpu.VMEM)

REP_ROWS = 80


def _rows8(a):
    flat = a.reshape(-1)
    n = -(-flat.shape[0] // (8 * D_MODEL)) * 8 * D_MODEL
    return jnp.pad(flat, (0, n - flat.shape[0])).reshape(-1, D_MODEL)


def _place():
    x, y, c = lax.axis_index("x"), lax.axis_index("y"), lax.axis_index("c")
    others = [(1 - x, y), (x, 1 - y), (1 - x, 1 - y)]
    return x, y, c, others


N_DEV = 8
LOSS_TILE = (8, 128)


def _cast_own(shards, me, after):
    n = len(shards)

    def body(me_ref, *refs):
        for w in range(n):
            refs[n + 1 + w][...] = refs[w][...].astype(BF16)

    return pl.pallas_call(
        body, name="cast_own",
        grid_spec=pltpu.PrefetchScalarGridSpec(
            num_scalar_prefetch=1, grid=(1,),
            in_specs=[pl.BlockSpec(s.shape, lambda i, me_ref: (0, 0)) for s in shards]
            + [pl.BlockSpec(memory_space=pl.ANY)],
            out_specs=[pl.BlockSpec((None,) + s.shape, lambda i, me_ref: (me_ref[0], 0, 0)) for s in shards]),
        out_shape=[jax.ShapeDtypeStruct((N_CHIPS,) + s.shape, BF16) for s in shards],
        compiler_params=pltpu.CompilerParams(vmem_limit_bytes=VMEM_LIMIT),
    )(me, *shards, after)


def _cast_first(lat, uq, me):
    def body(me_ref, lat_ref, uq_ref, wlat_ref, guq_ref):
        wlat_ref[...] = lat_ref[...].astype(BF16)
        guq_ref[...] = uq_ref[...].astype(BF16)

    return pl.pallas_call(
        body, name="cast_first",
        grid_spec=pltpu.PrefetchScalarGridSpec(
            num_scalar_prefetch=1, grid=(1,),
            in_specs=[pl.BlockSpec((LAT_COLS, D_MODEL), lambda i, me_ref: (0, 0)),
                      pl.BlockSpec(uq.shape, lambda i, me_ref: (0, 0))],
            out_specs=[pl.BlockSpec((LAT_COLS, D_MODEL), lambda i, me_ref: (0, 0)),
                       pl.BlockSpec((None,) + uq.shape, lambda i, me_ref: (me_ref[0], 0, 0))]),
        out_shape=[jax.ShapeDtypeStruct((LAT_COLS, D_MODEL), BF16),
                   jax.ShapeDtypeStruct((N_CHIPS,) + uq.shape, BF16)],
    )(me, lat, uq)


def _first_copies(wlat_ref, guq_ref, send_sems, recv_sems, shapes):
    x, y, c, others = _place()
    me = 2 * x + y
    hl, hu = shapes[0][1] // 2, shapes[1][2] // 2
    lat_half = wlat_ref.at[:, pl.ds(c * hl, hl)]

    def copy(src, dst, k, to):
        return pltpu.make_async_remote_copy(src_ref=src, dst_ref=dst, send_sem=send_sems.at[k],
                                            recv_sem=recv_sems.at[k], device_id=to, device_id_type=MESH)

    def uq_half(chip):
        return guq_ref.at[chip, :, pl.ds(c * hu, hu)]

    lat_out = [copy(lat_half, lat_half, j, (*others[j], c)) for j in range(3)]
    uq_out = [copy(uq_half(me), uq_half(me), 3 + j, (*others[j], c)) for j in range(3)]
    j0 = jnp.maximum(x + 2 * y - 1, 0)
    lat_in = copy(lat_half, lat_half, j0, (0, 0, c))
    uq_in = [copy(uq_half(me), uq_half(2 * px + py), 3 + j, (px, py, c)) for j, (px, py) in enumerate(others)]
    return me, lat_out, uq_out, lat_in, uq_in


def _first_start(wlat, guq):
    shapes = (wlat.shape, guq.shape)

    def body(wlat_ref, guq_ref, send_sems, recv_sems, wlat_thru, guq_thru, token):
        me, lat_out, uq_out, _, _ = _first_copies(wlat_ref, guq_ref, send_sems, recv_sems, shapes)

        @pl.when(me == 0)
        def _():
            for cp in lat_out:
                cp.start()

        for cp in uq_out:
            cp.start()
        token[...] = jnp.zeros_like(token)

    outs = pl.pallas_call(
        body, name="first_start",
        out_shape=(pltpu.SemaphoreType.DMA((6,)), pltpu.SemaphoreType.DMA((6,)), pltpu.HBM(wlat.shape, BF16),
                   pltpu.HBM(guq.shape, BF16), jax.ShapeDtypeStruct(LOSS_TILE, F32)),
        in_specs=[HBM_SPEC_STRICT] * 2, out_specs=(SEM_SPEC, SEM_SPEC, HBM_SPEC_STRICT, HBM_SPEC_STRICT, VMEM_SPEC),
        input_output_aliases={0: 2, 1: 3},
        compiler_params=pltpu.CompilerParams(has_side_effects=SPLIT_EFFECT),
    )(pltpu.with_memory_space_constraint(wlat, pltpu.HBM), pltpu.with_memory_space_constraint(guq, pltpu.HBM))
    return outs


def _first_wait(send_sems, recv_sems, wlat, guq, *after):
    shapes = (wlat.shape, guq.shape)

    def body(wlat_ref, guq_ref, send_sems, recv_sems, *rest):
        me, lat_out, uq_out, lat_in, uq_in = _first_copies(wlat_ref, guq_ref, send_sems, recv_sems, shapes)

        @pl.when(me == 0)
        def _():
            for cp in lat_out:
                cp.wait_send()

        @pl.when(me != 0)
        def _():
            lat_in.wait_recv()

        for cp in uq_out:
            cp.wait_send()
        for cp in uq_in:
            cp.wait_recv()

    return pl.pallas_call(
        body, name="first_wait", out_shape=(pltpu.HBM(wlat.shape, BF16), pltpu.HBM(guq.shape, BF16)),
        in_specs=[HBM_SPEC_STRICT, HBM_SPEC_STRICT, SEM_SPEC, SEM_SPEC] + [HBM_SPEC] * len(after),
        out_specs=(HBM_SPEC_STRICT, HBM_SPEC_STRICT), input_output_aliases={0: 0, 1: 1},
        compiler_params=pltpu.CompilerParams(has_side_effects=SPLIT_EFFECT),
    )(wlat, guq, send_sems, recv_sems, *after)


def _first_forward(wlat, guq):
    hl, hu = wlat.shape[1] // 2, guq.shape[2] // 2

    def body(wlat_in, guq_in, wlat_ref, guq_ref, send_sems, recv_sems):
        x, y, c, others = _place()
        me = 2 * x + y
        sibling = (x, y, 1 - c)

        def copy(part, k):
            return pltpu.make_async_remote_copy(src_ref=part, dst_ref=part, send_sem=send_sems.at[k],
                                                recv_sem=recv_sems.at[k], device_id=sibling, device_id_type=MESH)

        def uq_part(j, half):
            px, py = others[j]
            return guq_ref.at[2 * px + py, :, pl.ds(half * hu, hu)]

        cps = [copy(uq_part(j, c), j) for j in range(3)]
        for cp in cps:
            cp.start()

        @pl.when(me != 0)
        def _():
            mine = copy(wlat_ref.at[:, pl.ds(c * hl, hl)], 3)
            mine.start()
            copy(wlat_ref.at[:, pl.ds((1 - c) * hl, hl)], 3).wait_recv()
            mine.wait_send()

        for j in range(3):
            copy(uq_part(j, 1 - c), j).wait_recv()
        for cp in cps:
            cp.wait_send()

    return pl.pallas_call(
        body, name="first_forward", in_specs=[HBM_SPEC, HBM_SPEC], out_specs=[HBM_SPEC, HBM_SPEC],
        out_shape=[jax.ShapeDtypeStruct(wlat.shape, BF16), jax.ShapeDtypeStruct(guq.shape, BF16)],
        input_output_aliases={0: 0, 1: 1},
        scratch_shapes=[pltpu.SemaphoreType.DMA((4,)), pltpu.SemaphoreType.DMA((4,))],
    )(wlat, guq)


def _gather_copy(b_ref, buf, w, j, src_chip, dst_chip, to, rows, send_sems, recv_sems):
    _, _, c, _ = _place()
    hc = _half(buf)
    return pltpu.make_async_remote_copy(
        src_ref=b_ref.at[src_chip, rows, pl.ds(c * hc, hc)], dst_ref=b_ref.at[dst_chip, rows, pl.ds(c * hc, hc)],
        send_sem=send_sems.at[3 * w + j], recv_sem=recv_sems.at[3 * w + j], device_id=to, device_id_type=MESH)


def _gather_rows(buf, w, chip, fn):
    if w != 0:
        fn(slice(None))
        return
    pl.when(chip == 0)(lambda: fn(pl.ds(LAT_COLS, buf.shape[1] - LAT_COLS)))
    pl.when(chip != 0)(lambda: fn(slice(None)))


def _gather_start(bufs, after):
    n = len(bufs)

    def body(*refs):
        b_refs = refs[:n]
        send_sems, recv_sems, token = refs[n + 1], refs[n + 2], refs[-1]
        x, y, c, others = _place()
        me = 2 * x + y
        for w in range(n):
            def start(rows, w=w):
                for j, (px, py) in enumerate(others):
                    _gather_copy(b_refs[w], bufs[w], w, j, me, me, (px, py, c), rows, send_sems, recv_sems).start()
            _gather_rows(bufs[w], w, me, start)
        token[...] = jnp.zeros_like(token)

    hbm = [pltpu.HBM(b.shape, BF16) for b in bufs]
    outs = pl.pallas_call(
        body, name="gather_start",
        out_shape=(pltpu.SemaphoreType.DMA((3 * n,)), pltpu.SemaphoreType.DMA((3 * n,)), *hbm,
                   jax.ShapeDtypeStruct(LOSS_TILE, F32)),
        in_specs=[HBM_SPEC_STRICT] * n + [HBM_SPEC],
        out_specs=(SEM_SPEC, SEM_SPEC, *[HBM_SPEC_STRICT] * n, VMEM_SPEC),
        input_output_aliases={i: 2 + i for i in range(n)},
        compiler_params=pltpu.CompilerParams(has_side_effects=SPLIT_EFFECT),
    )(*[pltpu.with_memory_space_constraint(b, pltpu.HBM) for b in bufs], after)
    return outs[0], outs[1], list(outs[2:2 + n]), outs[-1]


def _gather_wait(send_sems, recv_sems, bufs, after):
    n = len(bufs)

    def body(*refs):
        b_refs = refs[:n]
        send_sems, recv_sems = refs[n], refs[n + 1]
        x, y, c, others = _place()
        me = 2 * x + y
        for w in range(n):
            for j, (px, py) in enumerate(others):
                def copy(rows, w=w, j=j, px=px, py=py):
                    return _gather_copy(b_refs[w], bufs[w], w, j, me, 2 * px + py, (px, py, c), rows, send_sems,
                                        recv_sems)
                _gather_rows(bufs[w], w, me, lambda rows, copy=copy: copy(rows).wait_send())
                _gather_rows(bufs[w], w, 2 * px + py, lambda rows, copy=copy: copy(rows).wait_recv())

    outs = pl.pallas_call(
        body, name="gather_wait", out_shape=tuple(pltpu.HBM(b.shape, b.dtype) for b in bufs),
        in_specs=[HBM_SPEC_STRICT] * n + [SEM_SPEC, SEM_SPEC, HBM_SPEC],
        out_specs=tuple([HBM_SPEC_STRICT] * n), input_output_aliases={i: i for i in range(n)},
        compiler_params=pltpu.CompilerParams(has_side_effects=SPLIT_EFFECT),
    )(*bufs, send_sems, recv_sems, after)
    return list(outs)


def _gather_finish(bufs):
    n = len(bufs)

    def body(*refs):
        b_refs = refs[n:2 * n]
        send_sems, recv_sems = refs[2 * n:]
        x, y, c, others = _place()
        cps = []
        for w in range(n):
            hc = _half(bufs[w])
            for j, (px, py) in enumerate(others):
                part = b_refs[w].at[2 * px + py, :, pl.ds(c * hc, hc)]
                cps.append(pltpu.make_async_remote_copy(
                    src_ref=part, dst_ref=part, send_sem=send_sems.at[3 * w + j], recv_sem=recv_sems.at[3 * w + j],
                    device_id=(x, y, 1 - c), device_id_type=MESH))
        for cp in cps:
            cp.start()
        for w in range(n):
            hc = _half(bufs[w])
            for j, (px, py) in enumerate(others):
                theirs = b_refs[w].at[2 * px + py, :, pl.ds((1 - c) * hc, hc)]
                pltpu.make_async_remote_copy(
                    src_ref=theirs, dst_ref=theirs, send_sem=send_sems.at[3 * w + j], recv_sem=recv_sems.at[3 * w + j],
                    device_id=(x, y, 1 - c), device_id_type=MESH).wait_recv()
        for cp in cps:
            cp.wait_send()

    return pl.pallas_call(
        body, name="gather_finish", in_specs=[HBM_SPEC] * n, out_specs=[HBM_SPEC] * n,
        out_shape=[jax.ShapeDtypeStruct(b.shape, b.dtype) for b in bufs],
        input_output_aliases={i: i for i in range(n)},
        scratch_shapes=[pltpu.SemaphoreType.DMA((3 * n,)), pltpu.SemaphoreType.DMA((3 * n,))],
    )(*bufs)


def _half(a):
    return a.shape[-1] // 2


def _exchange_pairs(parts, name):
    n = len(parts)

    def body(*refs):
        p_refs, r_refs = refs[:n], refs[n:2 * n]
        send_sems, recv_sems = refs[2 * n:]
        x, y, c, _ = _place()
        cps = []
        for w in range(n):
            h = _half(parts[w])
            cps.append(pltpu.make_async_remote_copy(
                src_ref=p_refs[w].at[:, :, pl.ds((1 - c) * h, h)], dst_ref=r_refs[w],
                send_sem=send_sems.at[w], recv_sem=recv_sems.at[w], device_id=(x, y, 1 - c), device_id_type=MESH))
        for cp in cps:
            cp.start()
        for cp in cps:
            cp.wait()

    return pl.pallas_call(
        body, name=name, in_specs=[HBM_SPEC] * n, out_specs=[HBM_SPEC] * n,
        out_shape=[jax.ShapeDtypeStruct((N_CHIPS, p.shape[1], _half(p)), BF16) for p in parts],
        scratch_shapes=[pltpu.SemaphoreType.DMA((n,)), pltpu.SemaphoreType.DMA((n,))],
    )(*parts)


def _sibling_part(ref, w, n_whole, shape, c):
    if w < n_whole:
        return ref
    h = shape[-1] // 2
    return ref.at[:, :, pl.ds((1 - c) * h, h)]


def _pairs_start(parts, all_loss, n_whole):
    n = len(parts)

    def body(*refs):
        p_refs, r_refs, loss_ref = refs[:n], refs[n:2 * n], refs[2 * n]
        send_sems, recv_sems, token = refs[2 * n + 1], refs[2 * n + 2], refs[-1]
        x, y, c, _ = _place()
        for w in range(n):
            h = _half(parts[w])
            pltpu.make_async_remote_copy(
                src_ref=_sibling_part(p_refs[w], w, n_whole, parts[w].shape, c), dst_ref=r_refs[w],
                send_sem=send_sems.at[w], recv_sem=recv_sems.at[w], device_id=(x, y, 1 - c),
                device_id_type=MESH).start()
        me = 4 * x + 2 * y + c
        for t in range(1, N_DEV):
            d = (me + t) % N_DEV
            pltpu.make_async_remote_copy(
                src_ref=loss_ref.at[me], dst_ref=loss_ref.at[me], send_sem=send_sems.at[n + t - 1],
                recv_sem=recv_sems.at[n + t - 1], device_id=(d // 4, (d // 2) % 2, d % 2), device_id_type=MESH).start()
        token[...] = jnp.zeros_like(token)

    lands = [pltpu.HBM(p.shape if w < n_whole else (N_CHIPS, p.shape[1], _half(p)), BF16)
             for w, p in enumerate(parts)]
    nsem = n + N_DEV - 1
    outs = pl.pallas_call(
        body, name="pairs_start",
        out_shape=(pltpu.SemaphoreType.DMA((nsem,)), pltpu.SemaphoreType.DMA((nsem,)),
                   *[pltpu.HBM(p.shape, p.dtype) for p in parts], *lands, pltpu.HBM(all_loss.shape, F32),
                   jax.ShapeDtypeStruct(LOSS_TILE, F32)),
        in_specs=[HBM_SPEC_STRICT] * (2 * n + 1),
        out_specs=(SEM_SPEC, SEM_SPEC, *[HBM_SPEC_STRICT] * (2 * n + 1), VMEM_SPEC),
        input_output_aliases={i: 2 + i for i in range(2 * n + 1)},
        compiler_params=pltpu.CompilerParams(has_side_effects=SPLIT_EFFECT),
    )(*[pltpu.with_memory_space_constraint(p, pltpu.HBM) for p in parts],
      *[pltpu.with_memory_space_constraint(lax.empty(l.shape, BF16), pltpu.HBM) for l in lands],
      pltpu.with_memory_space_constraint(all_loss, pltpu.HBM))
    return outs[0], outs[1], list(outs[2:2 + n]), list(outs[2 + n:2 + 2 * n]), outs[2 + 2 * n], outs[-1]


def _pairs_wait(send_sems, recv_sems, parts, lands, all_loss, after, n_whole):
    n = len(parts)

    def body(*refs):
        p_refs, r_refs, loss_ref = refs[:n], refs[n:2 * n], refs[2 * n]
        send_sems, recv_sems = refs[2 * n + 1], refs[2 * n + 2]
        x, y, c, _ = _place()
        for w in range(n):
            h = _half(parts[w])
            cp = pltpu.make_async_remote_copy(
                src_ref=_sibling_part(p_refs[w], w, n_whole, parts[w].shape, c), dst_ref=r_refs[w],
                send_sem=send_sems.at[w],
                recv_sem=recv_sems.at[w], device_id=(x, y, 1 - c), device_id_type=MESH)
            cp.wait_send()
            cp.wait_recv()
        me = 4 * x + 2 * y + c
        for t in range(1, N_DEV):
            d = (me + N_DEV - t) % N_DEV
            cp = pltpu.make_async_remote_copy(
                src_ref=loss_ref.at[me], dst_ref=loss_ref.at[d], send_sem=send_sems.at[n + t - 1],
                recv_sem=recv_sems.at[n + t - 1], device_id=(d // 4, (d // 2) % 2, d % 2), device_id_type=MESH)
            cp.wait_send()
            cp.wait_recv()

    bufs = (*parts, *lands, all_loss)
    outs = pl.pallas_call(
        body, name="pairs_wait", out_shape=tuple(pltpu.HBM(a.shape, a.dtype) for a in bufs),
        in_specs=[HBM_SPEC_STRICT] * len(bufs) + [SEM_SPEC, SEM_SPEC, HBM_SPEC],
        out_specs=tuple([HBM_SPEC_STRICT] * len(bufs)), input_output_aliases={i: i for i in range(len(bufs))},
        compiler_params=pltpu.CompilerParams(has_side_effects=SPLIT_EFFECT),
    )(*bufs, send_sems, recv_sems, after)
    return list(outs[:n]), list(outs[n:2 * n]), outs[2 * n]


def _add_pair(ps, rs, c, name, n_whole=0):
    n = len(ps)

    def body(c_ref, *refs):
        for w in range(n):
            if w < n_whole:
                mine = refs[w][...]
            else:
                h = _half(ps[w])
                mine = refs[w][:, :, pl.ds(pl.multiple_of(c_ref[0] * h, 128), h)]
            refs[2 * n + w][...] = (mine.astype(F32) + refs[n + w][...].astype(F32)).astype(BF16)

    return pl.pallas_call(
        body, name=name,
        in_specs=[pl.BlockSpec(memory_space=pltpu.SMEM)] + [VMEM_SPEC] * (2 * n), out_specs=[VMEM_SPEC] * n,
        out_shape=[jax.ShapeDtypeStruct(r.shape, BF16) for r in rs],
        compiler_params=pltpu.CompilerParams(vmem_limit_bytes=VMEM_LIMIT),
    )(c, *ps, *rs)


SEM_SPEC = pl.BlockSpec(memory_space=pltpu.SEMAPHORE)
SPLIT_EFFECT = pltpu.SideEffectType.DATAFLOW_SIDE_EFFECTING


def _chips_start(qs, name):
    n = len(qs)

    def body(*refs):
        q_refs, land_refs = refs[:n], refs[n:2 * n]
        send_sems, recv_sems, token = refs[2 * n], refs[2 * n + 1], refs[-1]
        x, y, c, others = _place()
        me = 2 * x + y
        for w in range(n):
            for j, (px, py) in enumerate(others):
                pltpu.make_async_remote_copy(
                    src_ref=q_refs[w].at[2 * px + py], dst_ref=land_refs[w].at[me], send_sem=send_sems.at[3 * w + j],
                    recv_sem=recv_sems.at[3 * w + j], device_id=(px, py, c), device_id_type=MESH).start()
        token[...] = jnp.zeros_like(token)

    hbm = [pltpu.HBM(q.shape, BF16) for q in qs]
    outs = pl.pallas_call(
        body, name=name,
        out_shape=(pltpu.SemaphoreType.DMA((3 * n,)), pltpu.SemaphoreType.DMA((3 * n,)), *hbm, *hbm,
                   jax.ShapeDtypeStruct(LOSS_TILE, F32)),
        in_specs=[HBM_SPEC_STRICT] * (2 * n),
        out_specs=(SEM_SPEC, SEM_SPEC, *[HBM_SPEC_STRICT] * (2 * n), VMEM_SPEC),
        input_output_aliases={i: 2 + i for i in range(2 * n)},
        compiler_params=pltpu.CompilerParams(has_side_effects=SPLIT_EFFECT),
    )(*[pltpu.with_memory_space_constraint(q, pltpu.HBM) for q in qs],
      *[pltpu.with_memory_space_constraint(lax.empty(q.shape, BF16), pltpu.HBM) for q in qs])
    return outs[0], outs[1], outs[2:2 + n], outs[2 + n:2 + 2 * n], outs[-1]


def _chips_wait(send_sems, recv_sems, q_thru, land_thru, after, name):
    n = len(q_thru)

    def body(*refs):
        q_refs, land_refs = refs[:n], refs[n:2 * n]
        send_sems, recv_sems = refs[2 * n], refs[2 * n + 1]
        x, y, c, others = _place()
        me = 2 * x + y
        for w in range(n):
            for j, (px, py) in enumerate(others):
                cp = pltpu.make_async_remote_copy(
                    src_ref=q_refs[w].at[2 * px + py], dst_ref=land_refs[w].at[2 * px + py],
                    send_sem=send_sems.at[3 * w + j], recv_sem=recv_sems.at[3 * w + j], device_id=(px, py, c),
                    device_id_type=MESH)
                cp.wait_send()
                cp.wait_recv()

    outs = pl.pallas_call(
        body, name=name, out_shape=tuple(pltpu.HBM(a.shape, a.dtype) for a in (*q_thru, *land_thru)),
        in_specs=[HBM_SPEC_STRICT] * (2 * n) + [SEM_SPEC, SEM_SPEC, HBM_SPEC],
        out_specs=tuple([HBM_SPEC_STRICT] * (2 * n)), input_output_aliases={i: i for i in range(2 * n)},
        compiler_params=pltpu.CompilerParams(has_side_effects=SPLIT_EFFECT),
    )(*q_thru, *land_thru, send_sems, recv_sems, after)
    return list(outs[:n]), list(outs[n:])


def _sum_chips(qs, rs, idx, all_dtypes):
    n = len(rs)
    n_all = len(all_dtypes)

    def body(idx_ref, *refs):
        c = idx_ref[4]
        for w in range(n):
            q_ref, r_ref, g_ref = refs[w], refs[n + w], refs[2 * n + w]
            acc = q_ref[idx_ref[0]].astype(F32)
            for t in range(1, N_CHIPS):
                acc = acc + r_ref[idx_ref[t]].astype(F32)
            h = rs[w].shape[2]
            mine = pl.ds(pl.multiple_of(c * h, 128), h)
            g_ref[...] = jnp.zeros_like(g_ref)
            if w >= n - n_all:
                g_ref[idx_ref[0], :, mine] = acc.astype(g_ref.dtype)
            else:
                g_ref[:, mine] = acc

    shapes = [jax.ShapeDtypeStruct((r.shape[1], 2 * r.shape[2]), F32) for r in rs[:n - n_all]]
    shapes += [jax.ShapeDtypeStruct((N_CHIPS, r.shape[1], 2 * r.shape[2]), dt)
               for r, dt in zip(rs[n - n_all:], all_dtypes)]
    return pl.pallas_call(
        body, name="sum_chips",
        in_specs=[pl.BlockSpec(memory_space=pltpu.SMEM)] + [VMEM_SPEC] * (2 * n), out_specs=[VMEM_SPEC] * n,
        out_shape=shapes, compiler_params=pltpu.CompilerParams(vmem_limit_bytes=VMEM_LIMIT),
    )(idx, *qs, *rs)


def _share(shards, alls):
    n, na = len(shards), len(alls)
    total = n + na

    def body(*refs):
        g_refs, a_refs = refs[total:total + n], refs[total + n:2 * total]
        send_sems, recv_sems = refs[2 * total:]
        x, y, c, others = _place()
        me = 2 * x + y
        sibling = (x, y, 1 - c)

        def cols_of(w, half):
            h = shards[w].shape[1] // 2
            return g_refs[w].at[:, pl.ds(half * h, h)]

        def slab(a, chip, half):
            h = alls[a].shape[2] // 2
            return a_refs[a].at[chip, :, pl.ds(half * h, h)]

        def copy(src, dst, k, to):
            return pltpu.make_async_remote_copy(src_ref=src, dst_ref=dst, send_sem=send_sems.at[k],
                                                recv_sem=recv_sems.at[k], device_id=to, device_id_type=MESH)

        cps = []
        for a in range(na):
            base = n + 7 * a
            for j, (px, py) in reversed(list(enumerate(others))):
                cps.append(copy(slab(a, me, c), slab(a, me, c), base + 1 + j, (px, py, c)))
            cps.append(copy(slab(a, me, c), slab(a, me, c), base, sibling))
        cps += [copy(cols_of(w, c), cols_of(w, c), w, sibling) for w in range(n)]
        for cp in cps:
            cp.start()
        fwd = []
        for a in range(na):
            base = n + 7 * a
            for j, (px, py) in enumerate(others):
                chip = 2 * px + py
                copy(slab(a, me, c), slab(a, chip, c), base + 1 + j, (px, py, c)).wait_recv()
                cp = copy(slab(a, chip, c), slab(a, chip, c), base + 4 + j, sibling)
                cp.start()
                fwd.append(cp)
        for a in range(na):
            base = n + 7 * a
            for j, (px, py) in enumerate(others):
                chip = 2 * px + py
                copy(slab(a, chip, c), slab(a, chip, 1 - c), base + 4 + j, sibling).wait_recv()
            copy(slab(a, me, c), slab(a, me, 1 - c), base, sibling).wait_recv()
        for w in range(n):
            copy(cols_of(w, c), cols_of(w, 1 - c), w, sibling).wait_recv()
        for cp in cps + fwd:
            cp.wait_send()

    nsem = n + 7 * na
    return pl.pallas_call(
        body, name="share", in_specs=[HBM_SPEC] * total, out_specs=[HBM_SPEC] * total,
        out_shape=[jax.ShapeDtypeStruct(a.shape, a.dtype) for a in (*shards, *alls)],
        input_output_aliases={i: i for i in range(total)},
        scratch_shapes=[pltpu.SemaphoreType.DMA((nsem,)), pltpu.SemaphoreType.DMA((nsem,))],
    )(*shards, *alls)


def _adamw(w, g, m, v):
    m2 = ADAM_B1 * m + (1.0 - ADAM_B1) * g
    v2 = ADAM_B2 * v + (1.0 - ADAM_B2) * (g * g)
    m_hat = m2 / (1.0 - ADAM_B1 ** ADAM_STEP)
    v_hat = v2 / (1.0 - ADAM_B2 ** ADAM_STEP)
    return -ADAM_LR * (m_hat / (jnp.sqrt(v_hat) + ADAM_EPS) + ADAM_WD * w), m2, v2


def _update_w_in(wt, gt, mt, vt, lat, owner, tile):
    nlat = lat.shape[0] // tile

    def body(owner_ref, w_ref, g_ref, m_ref, v_ref, lat_ref, g2_ref, d_ref, m2_ref, v2_ref):
        row = pl.program_id(0) * tile + lax.broadcasted_iota(jnp.int32, (tile, 1), 0)
        g = jnp.where((row < LAT_COLS) & (owner_ref[0] == 1), lat_ref[...].astype(F32), g_ref[...])
        g2_ref[...] = g
        d_ref[...], m2_ref[...], v2_ref[...] = _adamw(w_ref[...], g, m_ref[...], v_ref[...])

    spec = pl.BlockSpec((tile, wt.shape[1]), lambda i, o: (i, 0))
    return pl.pallas_call(
        body, name="update_w_in",
        grid_spec=pltpu.PrefetchScalarGridSpec(
            num_scalar_prefetch=1, grid=(wt.shape[0] // tile,),
            in_specs=[spec] * 4 + [pl.BlockSpec((tile, wt.shape[1]), lambda i, o: (jnp.minimum(i, nlat - 1), 0))],
            out_specs=[spec] * 4),
        out_shape=[jax.ShapeDtypeStruct(wt.shape, F32)] * 4,
        compiler_params=_params(("parallel",)),
    )(owner, wt, gt, mt, vt, lat)


def _update_small(ws, gs, ms, vs):
    n = len(ws)

    def body(*refs):
        for k in range(n):
            w_ref, g_ref, m_ref, v_ref = refs[k], refs[n + k], refs[2 * n + k], refs[3 * n + k]
            d, m2, v2 = _adamw(w_ref[...], g_ref[...], m_ref[...], v_ref[...])
            refs[4 * n + k][...] = d
            refs[5 * n + k][...] = m2
            refs[6 * n + k][...] = v2

    shapes = [jax.ShapeDtypeStruct(w.shape, F32) for w in ws]
    outs = pl.pallas_call(
        body, name="update_small", in_specs=[VMEM_SPEC] * (4 * n), out_specs=[VMEM_SPEC] * (3 * n),
        out_shape=shapes * 3,
        compiler_params=pltpu.CompilerParams(vmem_limit_bytes=VMEM_LIMIT),
    )(*ws, *gs, *ms, *vs)
    return outs[:n], outs[n:2 * n], outs[2 * n:]


REPLICATED = ("b_in", "g_q", "g_kv", "w_ukv", "sgu_ln_g", "sgu_ln_b", "w_s", "b_s", "ln_g", "ln_b")
ORDER = ("w_in", "b_in", "g_q", "w_uq", "g_kv", "w_ukv", "w_oa", "sgu_ln_g", "sgu_ln_b", "w_s", "b_s", "w_ob", "w_out",
         "ln_g", "ln_b")


def kernel(x, positions, w_in, b_in, g_q, w_uq, g_kv, w_ukv, w_oa, sgu_ln_g, sgu_ln_b, w_s, b_s, w_ob, w_out, ln_g, ln_b, loss_target, m_w_in, m_b_in, m_g_q, m_w_uq, m_g_kv, m_w_ukv, m_w_oa, m_sgu_ln_g, m_sgu_ln_b, m_w_s, m_b_s, m_w_ob, m_w_out, m_ln_g, m_ln_b, v_w_in, v_b_in, v_g_q, v_w_uq, v_g_kv, v_w_ukv, v_w_oa, v_sgu_ln_g, v_sgu_ln_b, v_w_s, v_b_s, v_w_ob, v_w_out, v_ln_g, v_ln_b):
    w = dict(w_in=w_in, b_in=b_in, g_q=g_q, w_uq=w_uq, g_kv=g_kv, w_ukv=w_ukv, w_oa=w_oa, sgu_ln_g=sgu_ln_g,
             sgu_ln_b=sgu_ln_b, w_s=w_s, b_s=b_s, w_ob=w_ob, w_out=w_out, ln_g=ln_g, ln_b=ln_b)
    m = dict(w_in=m_w_in, b_in=m_b_in, g_q=m_g_q, w_uq=m_w_uq, g_kv=m_g_kv, w_ukv=m_w_ukv, w_oa=m_w_oa,
             sgu_ln_g=m_sgu_ln_g, sgu_ln_b=m_sgu_ln_b, w_s=m_w_s, b_s=m_b_s, w_ob=m_w_ob, w_out=m_w_out, ln_g=m_ln_g,
             ln_b=m_ln_b)
    v = dict(w_in=v_w_in, b_in=v_b_in, g_q=v_g_q, w_uq=v_w_uq, g_kv=v_g_kv, w_ukv=v_w_ukv, w_oa=v_w_oa,
             sgu_ln_g=v_sgu_ln_g, sgu_ln_b=v_sgu_ln_b, w_s=v_w_s, b_s=v_b_s, w_ob=v_w_ob, w_out=v_w_out, ln_g=v_ln_g,
             ln_b=v_ln_b)
    w, m, v = ({n: a[0] for n, a in d.items()} for d in (w, m, v))
    c = lax.axis_index("c")

    wt_shard, mt_shard, vt_shard = (jnp.transpose(d["w_in"]) for d in (w, m, v))
    xi, yi = lax.axis_index("x"), lax.axis_index("y")
    me1 = (2 * xi + yi).reshape(1).astype(jnp.int32)
    first = _first_start(*_cast_first(wt_shard, _pad_heads(w["w_uq"]), me1))
    tables = _rope_tables(positions[0])
    prep = _prep_local(w["b_in"], w["g_q"], w["g_kv"], w["w_ukv"])
    bufs = _cast_own([wt_shard, w["w_oa"], w["w_ob"], w["w_out"]], me1, first[4])
    send0, recv0, bufs, token0 = _gather_start(bufs, first[4])
    g_lat, g_uq = _first_forward(*_first_wait(*first[:4], token0, *tables, *prep.values()))
    st = _local_attention(x[0], tables, g_lat, prep, g_uq.reshape(Q_RANK, HEADS * HEAD_PAD), token0)
    g_in, g_oa, g_ob, g_out = _gather_finish(_gather_wait(send0, recv0, bufs, st["o"]))
    wt = g_in.reshape(IN_W, D_MODEL)

    loss, early, st = _local_head(
        st, x[0], loss_target[0], wt, g_oa, w["sgu_ln_g"], w["sgu_ln_b"], w["w_s"], w["b_s"], g_ob,
        g_out.reshape(D_MODEL, D_MODEL), w["ln_g"], w["ln_b"])

    c1 = c.reshape(1).astype(jnp.int32)
    idx = jnp.stack([2 * xi + yi, 2 * (1 - xi) + yi, 2 * xi + (1 - yi), 2 * (1 - xi) + (1 - yi), c]).astype(jnp.int32)
    slabs = lambda a: a.reshape(N_CHIPS, IN_W // N_CHIPS, D_MODEL // 2)
    theirs = slabs(_dwt_early(st["dhmt"], st["dhgt"], st["xb2"], 1 - c1, c1, "dwt_theirs"))
    parts1 = [theirs, early["w_oa"].astype(BF16), early["w_ob"].astype(BF16),
              early["w_out"].reshape(N_CHIPS, SLAB_W, D_MODEL).astype(BF16)]
    my_loss = lax.dynamic_update_slice(jnp.zeros((N_DEV,) + LOSS_TILE, F32), jnp.broadcast_to(loss, (1,) + LOSS_TILE),
                                       (4 * xi + 2 * yi + c, 0, 0))
    sems0 = _pairs_start(parts1, my_loss, 1)
    mine = slabs(_dwt_early(st["dhmt"], st["dhgt"], st["xb2"], c1, sems0[5], "dwt_mine"))
    parts1, recv1, all_loss = _pairs_wait(*sems0[:5], mine, 1)
    pairs1 = _add_pair([mine, *parts1[1:]], recv1, c1, "add_pair_early", n_whole=1)
    sems1 = _chips_start(pairs1, "chips_start_early")
    dq, dk, dv = _local_attn_bwd(st, sems1[4])
    dhl, late = _local_tail(st, dq, dk, dv, dv)

    grads = {**early, **late}
    rep = jnp.concatenate([_rows8(grads[n]) for n in REPLICATED], axis=0)
    rep = jnp.pad(rep, ((0, N_CHIPS * REP_ROWS - rep.shape[0]), (0, 0))).reshape(N_CHIPS, REP_ROWS, D_MODEL)
    parts2 = [late["w_uq"].reshape(N_CHIPS, Q_RANK // N_CHIPS, HEADS * QK_DIM).astype(BF16), rep.astype(BF16),
              late["w_lat"].reshape(N_CHIPS, LAT_ROWS_PAD // N_CHIPS, D_MODEL)]
    pairs2 = _add_pair(parts2, _exchange_pairs(parts2, "exchange_pairs_late"), c1, "add_pair_late")
    sems2 = _chips_start(pairs2, "chips_start_late")
    dx = _dx(st["dr"], st["dhg"], st["dhm"], dhl, st["wt"], st["wlat"], sems2[4])
    pairs2, landed2 = _chips_wait(*sems2[:4], dx, "chips_wait_late")
    pairs1, landed1 = _chips_wait(*sems1[:4], landed2[0], "chips_wait_early")
    sums = _sum_chips([*pairs1, *pairs2], [*landed1, *landed2], idx, (F32, BF16))
    *shards, g_rep, g_lat = _share(sums[:-2], sums[-2:])
    loss = jnp.sum(all_loss[:, 0, 0])

    red = {n: s.reshape(w[n].shape) for n, s in zip(("w_oa", "w_ob", "w_out", "w_uq"), shards[1:])}
    g_rep = g_rep.reshape(N_CHIPS * REP_ROWS, D_MODEL)
    off = 0
    for n in REPLICATED:
        rows = _rows8(w[n]).shape[0]
        red[n] = g_rep[off:off + rows].reshape(-1)[:w[n].size].reshape(w[n].shape)
        off += rows
    owner = (2 * xi + yi == 0).astype(jnp.int32).reshape(1)
    gt, dt, mt, vt2 = _update_w_in(wt_shard, shards[0], mt_shard, vt_shard,
                                   g_lat.reshape(LAT_ROWS_PAD, D_MODEL).astype(F32), owner, 232)
    red["w_in"] = jnp.transpose(gt)
    small = [n for n in ORDER if n != "w_in"]
    as2d = lambda a: a.reshape(-1, a.shape[-1])
    ds, ms, vs = _update_small([as2d(w[n]) for n in small], [as2d(red[n]) for n in small],
                               [as2d(m[n]) for n in small], [as2d(v[n]) for n in small])
    delta, new_m, new_v = {"w_in": jnp.transpose(dt)}, {"w_in": jnp.transpose(mt)}, {"w_in": jnp.transpose(vt2)}
    for i, n in enumerate(small):
        delta[n], new_m[n], new_v[n] = (a[i].reshape(w[n].shape) for a in (ds, ms, vs))

    lead = lambda a: a[None]
    return (loss, dx[None], *[lead(red[n]) for n in ORDER], *[lead(delta[n]) for n in ORDER],
            *[lead(new_m[n]) for n in ORDER], *[lead(new_v[n]) for n in ORDER])
```

```python
import math

import jax
import jax.numpy as jnp
from jax import lax
from jax.experimental import pallas as pl
from jax.experimental.pallas import tpu as pltpu

F32 = jnp.float32
BF16 = jnp.bfloat16

D_MODEL = 1024
HEADS = 8
Q_RANK = 384
KV_RANK = 128
NOPE = 64
ROPE = 32
V_DIM = 64
QK_DIM = NOPE + ROPE
HEAD_PAD = 128
MLA_W = HEADS * V_DIM
SGU_W = 512
GROUPS = 8
CHUNK = 128
IN_W = 4640
RMS_EPS = 1e-6
LN_EPS = 1e-5
ALPHA = 2.0 ** 0.25
ROPE_THETA = 10000.0
SCALE = QK_DIM ** -0.5

GATE_W = 2 * D_MODEL
MID_W = 4 * SGU_W
LAT_W = Q_RANK + KV_RANK + HEAD_PAD
LAT_COLS = Q_RANK + KV_RANK + ROPE
ROW_GATE = LAT_COLS + MID_W
LAT_ROWS_PAD = 704
N_SLABS = 4
SLAB_W = D_MODEL // N_SLABS

ROW_TILE = 256
MATMUL_ROW_TILE = 512
MID_ROW_TILE = 256
ATT_TK = 256
ATT_BWD_TK = 256
SUM_ROWS = 16
LOG2E = 1.4426950408889634
LN2 = 0.6931471805599453
Q_SCALE = SCALE * LOG2E
VMEM_LIMIT = 56 * 1024 * 1024

ADAM_LR = 0.001
ADAM_B1 = 0.9
ADAM_B2 = 0.999
ADAM_EPS = 1e-08
ADAM_WD = 0.01
ADAM_STEP = 10


def _dot(a, b):
    return jnp.dot(a, b, preferred_element_type=F32)


def _dot_nt(a, b):
    return lax.dot_general(a, b, (((1,), (1,)), ((), ())), preferred_element_type=F32)


def _dot_tn(a, b):
    return lax.dot_general(a, b, (((0,), (0,)), ((), ())), preferred_element_type=F32)


def _sigmoid(z):
    return 0.5 * jnp.tanh(0.5 * z) + 0.5


_GELU_C = math.sqrt(2.0 / math.pi)


def _gelu_and_grad(x):
    x2 = x * x
    t = jnp.tanh(_GELU_C * (x + 0.044715 * x * x2))
    g = 0.5 * x * (1.0 + t)
    dg = 0.5 * (1.0 + t) + 0.5 * x * (1.0 - t * t) * (_GELU_C * (1.0 + 3.0 * 0.044715 * x2))
    return g, dg


def _silu_and_grad(z):
    s = _sigmoid(z)
    return z * s, s * (1.0 + z * (1.0 - s))


def _rope(xb, c, sl, sh):
    return xb * c + pltpu.roll(xb, 112, 1) * sl + pltpu.roll(xb, 16, 1) * sh


def _rope_t(dy, c, sl, sh):
    return dy * c + pltpu.roll(dy * sl, 16, 1) + pltpu.roll(dy * sh, 112, 1)


def _params(sem=("arbitrary",)):
    return pltpu.CompilerParams(dimension_semantics=sem, vmem_limit_bytes=VMEM_LIMIT)


def _row_spec(tile, width):
    return pl.BlockSpec((tile, width), lambda i: (i, 0))


def _full_spec(shape):
    nd = len(shape)
    return pl.BlockSpec(shape, lambda i: (0,) * nd)


def _kpe_rows(wt_ref):
    z = lambda n: jnp.zeros((n, D_MODEL), BF16)
    return jnp.concatenate([z(NOPE), wt_ref[Q_RANK + KV_RANK:LAT_COLS, :], z(HEAD_PAD - QK_DIM)], axis=0)


def _fwd_rest(xb2, wt, b_g, b_m):
    s = xb2.shape[1]
    ts = MATMUL_ROW_TILE
    tn = D_MODEL
    blocks = ([(ROW_GATE + c0, 0, c0) for c0 in range(0, GATE_W, tn)]
              + [(LAT_COLS + c0, 1, c0) for c0 in range(0, MID_W, tn)])

    def body(xb_ref, wt_hbm, bg_ref, bm_ref, hg_ref, hm_ref, wt_ref, sems):
        copies = [pltpu.make_async_copy(wt_hbm.at[lo:lo + tn], wt_ref.at[lo:lo + tn], sems.at[n])
                  for n, (lo, _, _) in enumerate(blocks)]

        def compute(first):
            xb_ = jnp.concatenate([xb_ref[0], xb_ref[1]], axis=1)
            for cp, (lo, which, c0) in zip(copies, blocks):
                if first:
                    cp.wait()
                out_ref, b_ref = ((hg_ref, bg_ref), (hm_ref, bm_ref))[which]
                out_ref[:, c0:c0 + tn] = (_dot_nt(xb_, wt_ref[lo:lo + tn, :]) + b_ref[:, c0:c0 + tn]).astype(BF16)

        @pl.when(pl.program_id(0) == 0)
        def _():
            for cp in copies:
                cp.start()
            compute(True)

        @pl.when(pl.program_id(0) > 0)
        def _():
            compute(False)

    return pl.pallas_call(
        body, name="fwd_rest", grid=(s // ts,),
        in_specs=[pl.BlockSpec((2, ts, D_MODEL // 2), lambda i: (0, i, 0)), HBM_SPEC, _full_spec(b_g.shape),
                  _full_spec(b_m.shape)],
        out_specs=[_row_spec(ts, GATE_W), _row_spec(ts, MID_W)],
        out_shape=[jax.ShapeDtypeStruct((s, GATE_W), BF16), jax.ShapeDtypeStruct((s, MID_W), BF16)],
        scratch_shapes=[pltpu.VMEM(wt.shape, BF16), pltpu.SemaphoreType.DMA((len(blocks),))],
        compiler_params=_params(),
    )(xb2, wt, b_g, b_m)


def _fwd_lat(x, wlat, b_l, g_q, wuq, g_kv, wk, wv, rc, rsl, rsh, after):
    s = x.shape[0]
    ts = ROW_TILE

    def body(x_ref, wt_ref, bl_ref, gq_ref, wuq_ref, gkv_ref, wk_ref, wv_ref, rc_ref, rsl_ref,
             rsh_ref, after_ref, hl_ref, q_ref, k_ref, v_ref, qt_ref, kt_ref, vt_ref, xb2_ref):
        xb = x_ref[...].astype(BF16)
        xb2_ref[0] = xb[:, :D_MODEL // 2]
        xb2_ref[1] = xb[:, D_MODEL // 2:]
        hl = jnp.concatenate([_dot_nt(xb, wt_ref[0:Q_RANK + KV_RANK, :]), _dot_nt(xb, _kpe_rows(wt_ref))],
                             axis=1) + bl_ref[...]
        hl_ref[...] = hl
        c, sl, sh = rc_ref[...], rsl_ref[...], rsh_ref[...]
        cq = hl[:, :Q_RANK]
        cqn = cq * lax.rsqrt(jnp.mean(cq * cq, axis=-1, keepdims=True) + RMS_EPS) * gq_ref[...]
        q = _dot(cqn.astype(BF16), wuq_ref[...])
        ckv = hl[:, Q_RANK:Q_RANK + KV_RANK]
        ckvn = (ckv * lax.rsqrt(jnp.mean(ckv * ckv, axis=-1, keepdims=True) + RMS_EPS) * gkv_ref[...]).astype(BF16)
        k = _dot(ckvn, wk_ref[...])
        vb = _dot(ckvn, wv_ref[...]).astype(BF16)
        v_ref[...] = vb
        vt_ref[...] = vb.T
        kpe = _rope(hl[:, Q_RANK + KV_RANK:], c, sl, sh)
        for hd in range(HEADS):
            lanes = slice(hd * HEAD_PAD, (hd + 1) * HEAD_PAD)
            qb = (_rope(q[:, lanes], c, sl, sh) * Q_SCALE).astype(BF16)
            kb = (k[:, lanes] + kpe).astype(BF16)
            q_ref[:, lanes] = qb
            k_ref[:, lanes] = kb
            qt_ref[lanes, :] = qb.T
            kt_ref[lanes, :] = kb.T

    qk_w = HEADS * HEAD_PAD
    col_spec = lambda rows: pl.BlockSpec((rows, ts), lambda i: (0, i))
    return pl.pallas_call(
        body, name="fwd_lat", grid=(s // ts,),
        in_specs=[_row_spec(ts, D_MODEL), _full_spec(wlat.shape),
                  _full_spec(b_l.shape), _full_spec(g_q.shape),
                  _full_spec(wuq.shape), _full_spec(g_kv.shape), _full_spec(wk.shape), _full_spec(wv.shape),
                  _row_spec(ts, HEAD_PAD), _row_spec(ts, HEAD_PAD), _row_spec(ts, HEAD_PAD),
                  pl.BlockSpec(memory_space=pl.ANY)],
        out_specs=[_row_spec(ts, LAT_W), _row_spec(ts, qk_w),
                   _row_spec(ts, qk_w), _row_spec(ts, MLA_W), col_spec(qk_w), col_spec(qk_w),
                   col_spec(MLA_W), pl.BlockSpec((2, ts, D_MODEL // 2), lambda i: (0, i, 0))],
        out_shape=[jax.ShapeDtypeStruct((s, LAT_W), F32), jax.ShapeDtypeStruct((s, qk_w), BF16),
                   jax.ShapeDtypeStruct((s, qk_w), BF16), jax.ShapeDtypeStruct((s, MLA_W), BF16),
                   jax.ShapeDtypeStruct((qk_w, s), BF16),
                   jax.ShapeDtypeStruct((qk_w, s), BF16), jax.ShapeDtypeStruct((MLA_W, s), BF16),
                   jax.ShapeDtypeStruct((2, s, D_MODEL // 2), BF16)],
        compiler_params=_params(),
    )(x, wlat, b_l, g_q, wuq, g_kv, wk, wv, rc, rsl, rsh, after)


def _attn_fwd(qt, k, vt):
    s = k.shape[0]
    tk = ATT_TK
    nk = s // tk
    pairs = HEADS // 2

    def body(qt_ref, k_ref, vt_ref, o_ref, lse_ref):
        qts = [qt_ref[hh * HEAD_PAD:(hh + 1) * HEAD_PAD, :] for hh in range(2)]
        ones = jnp.ones((SUM_ROWS, tk), BF16)

        def scores(j):
            return tuple(_dot(k_ref[j * tk:(j + 1) * tk, hh * HEAD_PAD:(hh + 1) * HEAD_PAD], qts[hh][:, j * tk:])
                         for hh in range(2))

        def weighted(j, ps):
            return tuple(_dot(jnp.concatenate([vt_ref[hh * V_DIM:(hh + 1) * V_DIM, j * tk:(j + 1) * tk], ones], axis=0),
                              ps[hh]) for hh in range(2))

        def from_lane(full, lo, part):
            return part if lo == 0 else jnp.concatenate([full[:, :lo], part], axis=1)

        krow = lax.broadcasted_iota(jnp.int32, (tk, tk), 0)
        qcol = lax.broadcasted_iota(jnp.int32, (tk, tk), 1)
        st = scores(0)
        ps = None
        stats = [(jnp.full((1, s), -jnp.inf, F32), jnp.zeros((V_DIM + SUM_ROWS, s), F32))] * 2
        for j in range(nk):
            lo, lo_prev = j * tk, max(j - 1, 0) * tk
            st_next = scores(j + 1) if j + 1 < nk else None
            pvs = weighted(j - 1, ps) if j else None
            new_ps, new_stats = [], []
            for hh in range(2):
                m, acc = stats[hh]
                diag = jnp.where(krow <= qcol, st[hh][:, :tk], -jnp.inf)
                s_ = diag if j == nk - 1 else jnp.concatenate([diag, st[hh][:, tk:]], axis=1)
                if j:
                    acc = from_lane(acc, lo_prev, acc[:, lo_prev:] + pvs[hh])
                m_old = m[:, lo:]
                m_new = jnp.maximum(m_old, jnp.max(s_, axis=0, keepdims=True))
                a = jnp.exp2(m_old - m_new)
                p = jnp.exp2(s_ - m_new)
                new_stats.append((from_lane(m, lo, m_new), from_lane(acc, lo, a * acc[:, lo:])))
                new_ps.append(p.astype(BF16))
            st, ps, stats = st_next, new_ps, new_stats
        pvs = weighted(nk - 1, ps)
        lo = (nk - 1) * tk
        accs = [from_lane(stats[hh][1], lo, stats[hh][1][:, lo:] + pvs[hh]) for hh in range(2)]
        sums = [acc[V_DIM:V_DIM + 1, :] for acc in accs]
        ot = jnp.concatenate([accs[hh][:V_DIM, :] / sums[hh] for hh in range(2)], axis=0)
        o_ref[...] = ot.T
        lse = [stats[hh][0] + jnp.log(sums[hh]) * LOG2E for hh in range(2)]
        lse_ref[...] = jnp.concatenate(lse + [jnp.zeros((6, s), F32)], axis=0)

    return pl.pallas_call(
        body, name="attn_fwd", grid=(pairs,),
        in_specs=[pl.BlockSpec((2 * HEAD_PAD, s), lambda p: (p, 0)),
                  pl.BlockSpec((s, 2 * HEAD_PAD), lambda p: (0, p)),
                  pl.BlockSpec((2 * V_DIM, s), lambda p: (p, 0))],
        out_specs=[pl.BlockSpec((s, 2 * V_DIM), lambda p: (0, p)),
                   pl.BlockSpec((None, 8, s), lambda p: (p, 0, 0))],
        out_shape=[jax.ShapeDtypeStruct((s, MLA_W), F32), jax.ShapeDtypeStruct((pairs, 8, s), F32)],
        compiler_params=_params(("arbitrary",)),
    )(qt, k, vt)


def _attn_bwd(q, qt, k, kt, v, do, dot, lse, delta, after):
    s = k.shape[0]
    tk = ATT_BWD_TK
    nk = s // tk
    pairs = HEADS // 2

    def body(q_ref, qt_ref, k_ref, kt_ref, v_ref, do_ref, dot_ref, lse_ref, dl_ref, after_ref, dqt_ref, dk_ref,
             dv_ref):
        krow = lax.broadcasted_iota(jnp.int32, (tk, tk), 0)
        qcol = lax.broadcasted_iota(jnp.int32, (tk, tk), 1)
        lane = lax.broadcasted_iota(jnp.int32, (tk, 2 * V_DIM), 1)
        drow = lax.broadcasted_iota(jnp.int32, (2 * V_DIM, s), 0)
        dotb = dot_ref[...]
        dots = [jnp.where((drow < V_DIM) if hh == 0 else (drow >= V_DIM), dotb, jnp.zeros_like(dotb))
                for hh in range(2)]
        for j in range(nk):
            lo = j * tk
            vb = v_ref[lo:lo + tk, :]
            dob = do_ref[lo:, :]
            dvs = []
            for hh in range(2):
                rows = slice(hh * HEAD_PAD, (hh + 1) * HEAD_PAD)
                st = _dot(k_ref[lo:lo + tk, rows], qt_ref[rows, lo:])
                diag = jnp.where(krow <= qcol, st[:, :tk], -jnp.inf)
                st = diag if j == nk - 1 else jnp.concatenate([diag, st[:, tk:]], axis=1)
                p = jnp.exp2(st - lse_ref[hh:hh + 1, lo:])
                dpt = _dot(vb, dots[hh][:, lo:])
                dst = (p * (dpt - dl_ref[hh:hh + 1, lo:])).astype(BF16)
                dvs.append(_dot(p.astype(BF16), dob))
                dk_ref[lo:lo + tk, rows] = (_dot(dst, q_ref[lo:, rows]) * LN2).astype(BF16)
                dqt = _dot(kt_ref[rows, lo:lo + tk], dst)
                if j == 0:
                    dqt_ref[rows, :] = dqt
                else:
                    dqt_ref[rows, lo:] += dqt
            dv_ref[lo:lo + tk, :] = jnp.where(lane < V_DIM, dvs[0], dvs[1]).astype(BF16)
        dqt_ref[...] = dqt_ref[...] * SCALE

    pair_rows = lambda w: pl.BlockSpec((s, w), lambda p: (0, p))
    pair_cols = lambda w: pl.BlockSpec((w, s), lambda p: (p, 0))
    stats = pl.BlockSpec((None, 8, s), lambda p: (p, 0, 0))
    return pl.pallas_call(
        body, name="attn_bwd", grid=(pairs,),
        in_specs=[pair_rows(2 * HEAD_PAD), pair_cols(2 * HEAD_PAD), pair_rows(2 * HEAD_PAD), pair_cols(2 * HEAD_PAD),
                  pair_rows(2 * V_DIM), pair_rows(2 * V_DIM), pair_cols(2 * V_DIM), stats, stats,
                  pl.BlockSpec(memory_space=pl.ANY)],
        out_specs=[pair_cols(2 * HEAD_PAD), pair_rows(2 * HEAD_PAD), pair_rows(2 * V_DIM)],
        out_shape=[jax.ShapeDtypeStruct((HEADS * HEAD_PAD, s), F32), jax.ShapeDtypeStruct((s, HEADS * HEAD_PAD), BF16),
                   jax.ShapeDtypeStruct((s, MLA_W), BF16)],
        compiler_params=_params(("arbitrary",)),
    )(q, qt, k, kt, v, do, dot, lse, delta, after)


def _split3(a):
    hi = a.astype(BF16)
    r1 = a - hi.astype(F32)
    mid = r1.astype(BF16)
    lo = (r1 - mid.astype(F32)).astype(BF16)
    return hi, mid, lo


def _mid(x, tgt, o, hm, hg, woa, wob, wout, ln_g, ln_b, sg_g, sg_b, w_s, bsb):
    s = x.shape[0]
    ts = MID_ROW_TILE
    nsteps = s // ts
    nch = ts // CHUNK
    npair = GROUPS // 2

    def body(x_ref, t_ref, o_ref, hm_ref, hg_ref, woa_ref, wob_ref, wout_ref, lng_ref, lnb_ref, sgg_ref, sgb_ref,
             ws_ref, bsb_ref,
             dr_ref, dhg_ref, dhm_ref, do_ref, dot_ref, dl_ref, dhgt_ref, dhmt_ref,
             dwout_ref, dwoa_ref, dwob_ref, dws_ref, dbs_ref, dlng_ref, dlnb_ref, dsgg_ref, dsgb_ref, loss_ref,
             dbg_ref, dbm_ref, dbacc_ref, awout_ref, awoa_ref, awob_ref):
        i = pl.program_id(0)

        @pl.when(i == 0)
        def _():
            for r in (awout_ref, awoa_ref, awob_ref, dws_ref, dlng_ref, dlnb_ref, dsgg_ref, dsgb_ref, loss_ref,
                      dbg_ref, dbm_ref, dbacc_ref):
                r[...] = jnp.zeros_like(r)

        def emit(ref, tref, bref, lo, val):
            vb = val.astype(BF16)
            n = val.shape[1]
            ref[:, lo:lo + n] = vb
            tref[lo:lo + n, :] = vb.T
            bref[:, lo:lo + n] += jnp.sum(val, axis=0, keepdims=True)

        lane = lax.broadcasted_iota(jnp.int32, (CHUNK, CHUNK), 1)
        left = lane < V_DIM
        tril = lax.broadcasted_iota(jnp.int32, (CHUNK, CHUNK), 0) >= lane
        ms = [jnp.where(tril, ws_ref[g], 0.0).astype(BF16) for g in range(GROUPS)]

        z_a = hm_ref[:, 0:SGU_W].astype(F32)
        u = hm_ref[:, SGU_W:2 * SGU_W].astype(F32)
        v = hm_ref[:, 2 * SGU_W:3 * SGU_W].astype(F32)
        z_b = hm_ref[:, 3 * SGU_W:4 * SGU_W].astype(F32)
        o = o_ref[...]
        sa, dsa = _silu_and_grad(z_a)
        y_a = (o * sa).astype(BF16)
        gu, dgu = _gelu_and_grad(u)
        gv, dgv = _gelu_and_grad(v)
        mu = jnp.mean(gv, axis=-1, keepdims=True)
        vc = gv - mu
        rstd_v = lax.rsqrt(jnp.mean(vc * vc, axis=-1, keepdims=True) + LN_EPS)
        vhat = vc * rstd_v
        vn = (vhat * sgg_ref[...] + sgb_ref[...]).astype(BF16)
        rows = []
        for c in range(nch):
            blocks = []
            for p in range(npair):
                blk = vn[c * CHUNK:(c + 1) * CHUNK, p * CHUNK:(p + 1) * CHUNK]
                blocks.append(jnp.where(left, _dot(ms[2 * p], blk), _dot(ms[2 * p + 1], blk)))
            rows.append(jnp.concatenate(blocks, axis=1) + bsb_ref[...])
        mixed = jnp.concatenate(rows, axis=0)
        sgu = gu * mixed
        sb, dsb = _silu_and_grad(z_b)
        y_b = (sgu * sb).astype(BF16)
        pa = jnp.concatenate([_dot(y_a, woa_ref[k]) for k in range(N_SLABS)], axis=1)
        pb = jnp.concatenate([_dot(y_b, wob_ref[k]) for k in range(N_SLABS)], axis=1)
        sga = _sigmoid(hg_ref[:, :D_MODEL].astype(F32))
        sgb = _sigmoid(hg_ref[:, D_MODEL:].astype(F32))
        m2 = (sga * pa + sgb * pb).astype(BF16)
        r = ALPHA * x_ref[...] + _dot(m2, wout_ref[...])
        rmu = jnp.mean(r, axis=-1, keepdims=True)
        rc = r - rmu
        rstd = lax.rsqrt(jnp.mean(rc * rc, axis=-1, keepdims=True) + LN_EPS)
        xhat = rc * rstd
        y = xhat * lng_ref[...] + lnb_ref[...]
        err = y - t_ref[...]
        loss_ref[...] += jnp.full(loss_ref.shape, 0.5 / D_MODEL, F32) * jnp.sum(err * err)

        dy = err * (1.0 / D_MODEL)
        dlng_ref[...] += jnp.sum(dy * xhat, axis=0, keepdims=True)
        dlnb_ref[...] += jnp.sum(dy, axis=0, keepdims=True)
        dxh = dy * lng_ref[...]
        dr = rstd * (dxh - jnp.mean(dxh, axis=-1, keepdims=True) - xhat * jnp.mean(dxh * xhat, axis=-1, keepdims=True))
        dr_ref[...] = dr
        drb = dr.astype(BF16)
        awout_ref[...] += _dot_tn(m2, drb)
        dm2 = _dot_nt(drb, wout_ref[...])
        emit(dhg_ref, dhgt_ref, dbg_ref, 0, dm2 * pa * sga * (1.0 - sga))
        emit(dhg_ref, dhgt_ref, dbg_ref, D_MODEL, dm2 * pb * sgb * (1.0 - sgb))
        dpa = (dm2 * sga).astype(BF16)
        dpb = (dm2 * sgb).astype(BF16)
        dy_a = jnp.zeros((ts, MLA_W), F32)
        dy_b = jnp.zeros((ts, SGU_W), F32)
        y_at, y_bt = y_a.T, y_b.T
        for k in range(N_SLABS):
            cols = slice(k * SLAB_W, (k + 1) * SLAB_W)
            awoa_ref[k] += _dot(y_at, dpa[:, cols])
            awob_ref[k] += _dot(y_bt, dpb[:, cols])
            dy_a = dy_a + _dot_nt(dpa[:, cols], woa_ref[k])
            dy_b = dy_b + _dot_nt(dpb[:, cols], wob_ref[k])
        dob = (dy_a * sa).astype(BF16)
        do_ref[...] = dob
        dot_ref[...] = dob.T
        head = (lax.broadcasted_iota(jnp.int32, (HEADS, MLA_W), 1) // V_DIM
                == lax.broadcasted_iota(jnp.int32, (HEADS, MLA_W), 0)).astype(BF16)
        dl = sum(_dot_nt(head, term) for term in _split3(dob.astype(F32) * o))
        for p in range(HEADS // 2):
            dl_ref[p] = jnp.concatenate([dl[2 * p:2 * p + 2], jnp.zeros((6, ts), F32)], axis=0)
        emit(dhm_ref, dhmt_ref, dbm_ref, 0, dy_a * o * dsa)
        dsg = dy_b * sb
        emit(dhm_ref, dhmt_ref, dbm_ref, 3 * SGU_W, dy_b * sgu * dsb)
        emit(dhm_ref, dhmt_ref, dbm_ref, SGU_W, dsg * mixed * dgu)
        dmixed = dsg * gu
        dvn_rows = []
        dbs_sum = jnp.zeros((CHUNK, SGU_W), F32)
        for c in range(nch):
            dm_c = dmixed[c * CHUNK:(c + 1) * CHUNK, :]
            dbs_sum = dbs_sum + dm_c
            blocks = []
            for p in range(npair):
                dmb = dm_c[:, p * CHUNK:(p + 1) * CHUNK].astype(BF16)
                blk = vn[c * CHUNK:(c + 1) * CHUNK, p * CHUNK:(p + 1) * CHUNK]
                blocks.append(jnp.where(left, _dot_tn(ms[2 * p], dmb), _dot_tn(ms[2 * p + 1], dmb)))
                zero = jnp.zeros_like(dmb)
                dws_ref[2 * p] += jnp.where(tril, _dot_nt(jnp.where(left, dmb, zero), blk), 0.0)
                dws_ref[2 * p + 1] += jnp.where(tril, _dot_nt(jnp.where(left, zero, dmb), blk), 0.0)
            dvn_rows.append(jnp.concatenate(blocks, axis=1))
        dbacc_ref[...] += dbs_sum
        dvn = jnp.concatenate(dvn_rows, axis=0)
        dsgg_ref[...] += jnp.sum(dvn * vhat, axis=0, keepdims=True)
        dsgb_ref[...] += jnp.sum(dvn, axis=0, keepdims=True)
        dvh = dvn * sgg_ref[...]
        dgv_in = rstd_v * (dvh - jnp.mean(dvh, axis=-1, keepdims=True)
                           - vhat * jnp.mean(dvh * vhat, axis=-1, keepdims=True))
        emit(dhm_ref, dhmt_ref, dbm_ref, 2 * SGU_W, dgv_in * dgv)

        @pl.when(i == nsteps - 1)
        def _():
            dwout_ref[...] = awout_ref[...].astype(BF16)
            dwoa_ref[...] = awoa_ref[...].astype(BF16)
            dwob_ref[...] = awob_ref[...].astype(BF16)
            grp = (lax.broadcasted_iota(jnp.int32, (SGU_W, CHUNK), 0) // V_DIM
                   == lax.broadcasted_iota(jnp.int32, (SGU_W, CHUNK), 1)).astype(BF16)
            hi, mid, lo = _split3(dbacc_ref[...])
            dbs_ref[...] = _dot(hi, grp) + _dot(mid, grp) + _dot(lo, grp)

    acc_shapes = [(D_MODEL, D_MODEL), woa.shape, wob.shape, (GROUPS, CHUNK, CHUNK), (CHUNK, CHUNK),
                  (1, D_MODEL), (1, D_MODEL), (1, SGU_W), (1, SGU_W), (1, 128), (1, GATE_W), (1, MID_W)]
    col_spec = lambda rows: pl.BlockSpec((rows, ts), lambda i: (0, i))
    return pl.pallas_call(
        body, name="mid", grid=(nsteps,),
        in_specs=[_row_spec(ts, D_MODEL), _row_spec(ts, D_MODEL), _row_spec(ts, MLA_W), _row_spec(ts, MID_W),
                  _row_spec(ts, GATE_W), _full_spec(woa.shape), _full_spec(wob.shape), _full_spec(wout.shape),
                  _full_spec(ln_g.shape), _full_spec(ln_b.shape), _full_spec(sg_g.shape), _full_spec(sg_b.shape),
                  _full_spec(w_s.shape), _full_spec(bsb.shape)],
        out_specs=[_row_spec(ts, D_MODEL), _row_spec(ts, GATE_W), _row_spec(ts, MID_W), _row_spec(ts, MLA_W),
                   col_spec(MLA_W), pl.BlockSpec((HEADS // 2, 8, ts), lambda i: (0, 0, i)), col_spec(GATE_W),
                   col_spec(MID_W)]
        + [_full_spec(sh) for sh in acc_shapes],
        out_shape=[jax.ShapeDtypeStruct((s, D_MODEL), F32), jax.ShapeDtypeStruct((s, GATE_W), BF16),
                   jax.ShapeDtypeStruct((s, MID_W), BF16), jax.ShapeDtypeStruct((s, MLA_W), BF16),
                   jax.ShapeDtypeStruct((MLA_W, s), BF16), jax.ShapeDtypeStruct((HEADS // 2, 8, s), F32),
                   jax.ShapeDtypeStruct((GATE_W, s), BF16), jax.ShapeDtypeStruct((MID_W, s), BF16)]
        + [jax.ShapeDtypeStruct(sh, BF16 if n < 3 else F32) for n, sh in enumerate(acc_shapes)],
        scratch_shapes=[pltpu.VMEM((CHUNK, SGU_W), F32)] + [pltpu.VMEM(sh, F32) for sh in acc_shapes[:3]],
        compiler_params=_params(),
    )(x, tgt, o, hm, hg, woa, wob, wout, ln_g, ln_b, sg_g, sg_b, w_s, bsb)


def _lat_bwd(dq, dk, dv, hl, rc, rsl, rsh, g_q, g_kv, wuq, wk, wv, after):
    s = dk.shape[0]
    ts = ROW_TILE
    qk_w = HEADS * HEAD_PAD

    def body(dq_ref, dk_ref, dv_ref, hl_ref, rc_ref, rsl_ref, rsh_ref, gq_ref, gkv_ref, wuq_ref, wk_ref, wv_ref,
             after_ref, dhl_ref, dhlt_ref, dwuq_ref, dwk_ref, dwv_ref, dgq_ref, dgkv_ref, dbl_ref):
        i = pl.program_id(0)

        @pl.when(i == 0)
        def _():
            for r in (dwuq_ref, dwk_ref, dwv_ref, dgq_ref, dgkv_ref, dbl_ref):
                r[...] = jnp.zeros_like(r)

        def emit(lo, val):
            vb = val.astype(BF16)
            n = val.shape[1]
            dhl_ref[:, lo:lo + n] = vb
            dhlt_ref[lo:lo + n, :] = vb.T
            dbl_ref[:, lo:lo + n] += jnp.sum(val, axis=0, keepdims=True)

        c, sl, sh = rc_ref[...], rsl_ref[...], rsh_ref[...]
        lane = lax.broadcasted_iota(jnp.int32, (ts, HEAD_PAD), 1)
        pe = (lane >= NOPE) & (lane < QK_DIM)
        dkpe = jnp.zeros((ts, HEAD_PAD), F32)
        dqu = []
        for hd in range(HEADS):
            lanes = slice(hd * HEAD_PAD, (hd + 1) * HEAD_PAD)
            dqu.append(_rope_t(dq_ref[lanes, :].T, c, sl, sh).astype(BF16))
            dkpe = dkpe + dk_ref[:, lanes]
        dqu = jnp.concatenate(dqu, axis=1)
        dkpe = _rope_t(jnp.where(pe, dkpe, 0.0), c, sl, sh)

        cq = hl_ref[:, :Q_RANK]
        rq = lax.rsqrt(jnp.mean(cq * cq, axis=-1, keepdims=True) + RMS_EPS)
        cqh = cq * rq
        cqn = (cqh * gq_ref[...]).astype(BF16)
        dwuq_ref[...] += _dot_tn(cqn, dqu)
        dcqn = _dot_nt(dqu, wuq_ref[...])
        dgq_ref[...] += jnp.sum(dcqn * cqh, axis=0, keepdims=True)
        dch = dcqn * gq_ref[...]
        emit(0, rq * (dch - cqh * jnp.mean(dch * cqh, axis=-1, keepdims=True)))

        ckv = hl_ref[:, Q_RANK:Q_RANK + KV_RANK]
        rk = lax.rsqrt(jnp.mean(ckv * ckv, axis=-1, keepdims=True) + RMS_EPS)
        ckh = ckv * rk
        ckn = (ckh * gkv_ref[...]).astype(BF16)
        dkb = dk_ref[...].astype(BF16)
        dvb = dv_ref[...].astype(BF16)
        dwk_ref[...] += _dot_tn(ckn, dkb)
        dwv_ref[...] += _dot_tn(ckn, dvb)
        dckn = _dot_nt(dkb, wk_ref[...]) + _dot_nt(dvb, wv_ref[...])
        dgkv_ref[...] += jnp.sum(dckn * ckh, axis=0, keepdims=True)
        dkh = dckn * gkv_ref[...]
        emit(Q_RANK, rk * (dkh - ckh * jnp.mean(dkh * ckh, axis=-1, keepdims=True)))
        emit(Q_RANK + KV_RANK, dkpe)

    acc_shapes = [wuq.shape, wk.shape, wv.shape, g_q.shape, g_kv.shape, (1, LAT_W)]
    return pl.pallas_call(
        body, name="lat_bwd", grid=(s // ts,),
        in_specs=[pl.BlockSpec((qk_w, ts), lambda i: (0, i)), _row_spec(ts, qk_w), _row_spec(ts, MLA_W),
                  _row_spec(ts, LAT_W), _row_spec(ts, HEAD_PAD), _row_spec(ts, HEAD_PAD), _row_spec(ts, HEAD_PAD),
                  _full_spec(g_q.shape), _full_spec(g_kv.shape), _full_spec(wuq.shape), _full_spec(wk.shape),
                  _full_spec(wv.shape), pl.BlockSpec(memory_space=pl.ANY)],
        out_specs=[_row_spec(ts, LAT_W), pl.BlockSpec((LAT_W, ts), lambda i: (0, i))]
        + [_full_spec(sh) for sh in acc_shapes],
        out_shape=[jax.ShapeDtypeStruct((s, LAT_W), BF16), jax.ShapeDtypeStruct((LAT_W, s), BF16)]
        + [jax.ShapeDtypeStruct(sh, F32) for sh in acc_shapes],
        compiler_params=_params(),
    )(dq, dk, dv, hl, rc, rsl, rsh, g_q, g_kv, wuq, wk, wv, after)


def _dx(dr, dhg, dhm, dhl, wt, wlat, after):
    s = dr.shape[0]
    ts = MATMUL_ROW_TILE

    tk = D_MODEL
    bounds = ([(ROW_GATE + r0, ROW_GATE + r0 + tk) for r0 in range(0, GATE_W, tk)]
              + [(LAT_COLS + r0, LAT_COLS + r0 + tk) for r0 in range(0, MID_W, tk)])

    def body(dr_ref, dhg_ref, dhm_ref, dhl_ref, wt_hbm, wlat_ref, after_ref, dx_ref, wt_ref, sems):
        copies = [pltpu.make_async_copy(wt_hbm.at[lo:hi], wt_ref.at[lo:hi], sems.at[n])
                  for n, (lo, hi) in enumerate(bounds)]

        def compute(first):
            acc = (ALPHA * dr_ref[...] + _dot(dhl_ref[:, 0:Q_RANK + KV_RANK], wlat_ref[0:Q_RANK + KV_RANK, :])
                   + _dot(dhl_ref[:, Q_RANK + KV_RANK:], _kpe_rows(wlat_ref)))
            for n, (lo, hi) in enumerate(bounds):
                if first:
                    copies[n].wait()
                if lo >= ROW_GATE:
                    acc += _dot(dhg_ref[:, lo - ROW_GATE:hi - ROW_GATE], wt_ref[lo:hi, :])
                else:
                    acc += _dot(dhm_ref[:, lo - LAT_COLS:hi - LAT_COLS], wt_ref[lo:hi, :])
            dx_ref[...] = acc

        @pl.when(pl.program_id(0) == 0)
        def _():
            for cp in copies:
                cp.start()
            compute(True)

        @pl.when(pl.program_id(0) > 0)
        def _():
            compute(False)

    return pl.pallas_call(
        body, name="dx", grid=(s // ts,),
        in_specs=[_row_spec(ts, D_MODEL), _row_spec(ts, GATE_W), _row_spec(ts, MID_W), _row_spec(ts, LAT_W),
                  HBM_SPEC, _full_spec(wlat.shape), HBM_SPEC],
        out_specs=_row_spec(ts, D_MODEL),
        out_shape=jax.ShapeDtypeStruct((s, D_MODEL), F32),
        scratch_shapes=[pltpu.VMEM(wt.shape, BF16), pltpu.SemaphoreType.DMA((len(bounds),))],
        compiler_params=_params(),
    )(dr, dhg, dhm, dhl, wt, wlat, after)


def _dwt_early(dhmt, dhgt, xb, col, after, name):
    tn = 512
    nm, ng = MID_W // tn, GATE_W // tn
    s = dhmt.shape[1]
    hc = D_MODEL // 2

    ks = s // 2

    def body(col_ref, dma_ref, dmb_ref, dga_ref, dgb_ref, xba_ref, xbb_ref, after_ref, dw_ref):
        i = pl.program_id(0)

        @pl.when(i < nm)
        def _():
            dw_ref[...] = (_dot(dma_ref[...], xba_ref[...]) + _dot(dmb_ref[...], xbb_ref[...])).astype(BF16)

        @pl.when(i >= nm)
        def _():
            dw_ref[...] = (_dot(dga_ref[...], xba_ref[...]) + _dot(dgb_ref[...], xbb_ref[...])).astype(BF16)

    def dh_spec(first, part):
        if first:
            return pl.BlockSpec((tn, ks), lambda i, col_ref: (jnp.minimum(i, nm - 1), part))
        return pl.BlockSpec((tn, ks), lambda i, col_ref: (jnp.maximum(i - nm, 0), part))

    rows = pl.pallas_call(
        body, name=name,
        grid_spec=pltpu.PrefetchScalarGridSpec(
            num_scalar_prefetch=1, grid=(nm + ng,),
            in_specs=[dh_spec(True, 0), dh_spec(True, 1), dh_spec(False, 0), dh_spec(False, 1),
                      pl.BlockSpec((None, ks, hc), lambda i, col_ref: (col_ref[0], 0, 0)),
                      pl.BlockSpec((None, ks, hc), lambda i, col_ref: (col_ref[0], 1, 0)),
                      pl.BlockSpec(memory_space=pl.ANY)],
            out_specs=pl.BlockSpec((pl.Element(tn), pl.Element(hc)),
                                   lambda i, col_ref: (pl.multiple_of(LAT_COLS + i * tn, 32), 0))),
        out_shape=jax.ShapeDtypeStruct((IN_W, hc), BF16),
        compiler_params=_params(),
    )(col, dhmt, dhmt, dhgt, dhgt, xb, xb, after)

    def zero(buf_ref, out_ref):
        out_ref[...] = jnp.zeros_like(out_ref)

    return pl.pallas_call(
        zero, name=name + "_zero_lat", grid=(1,), in_specs=[pl.BlockSpec(memory_space=pl.ANY)],
        out_specs=pl.BlockSpec((LAT_COLS, hc), lambda i: (0, 0)),
        out_shape=jax.ShapeDtypeStruct((IN_W, hc), BF16), input_output_aliases={0: 0},
    )(rows)


def _dwt_lat(dhlt, xb2):
    def body(dht_ref, xb_ref, dw_ref):
        dht = dht_ref[...]
        dw = jnp.concatenate([_dot(dht, xb_ref[0]), _dot(dht, xb_ref[1])], axis=1).astype(BF16)
        kpe = Q_RANK + KV_RANK + NOPE
        dw_ref[0:Q_RANK + KV_RANK, :] = dw[0:Q_RANK + KV_RANK]
        dw_ref[Q_RANK + KV_RANK:LAT_COLS, :] = dw[kpe:kpe + ROPE]
        dw_ref[LAT_COLS:, :] = jnp.zeros((LAT_ROWS_PAD - LAT_COLS, D_MODEL), BF16)

    return pl.pallas_call(
        body, name="dwt_lat", in_specs=[VMEM_SPEC, VMEM_SPEC], out_specs=VMEM_SPEC,
        out_shape=jax.ShapeDtypeStruct((LAT_ROWS_PAD, D_MODEL), BF16),
        compiler_params=pltpu.CompilerParams(vmem_limit_bytes=VMEM_LIMIT),
    )(dhlt, xb2)


def _split_bias(b):
    z = lambda n: jnp.zeros((n,), b.dtype)
    lat = jnp.concatenate([b[:Q_RANK + KV_RANK], z(NOPE), b[Q_RANK + KV_RANK:LAT_COLS], z(HEAD_PAD - QK_DIM)])
    return b[None, ROW_GATE:], b[None, LAT_COLS:ROW_GATE], lat[None, :]


def _join_bias(g, m, l):
    kpe = Q_RANK + KV_RANK + NOPE
    return jnp.concatenate([l[0, :Q_RANK + KV_RANK], l[0, kpe:kpe + ROPE], m[0], g[0]])


def _rope_tables(positions):
    half = ROPE // 2
    inv_freq = ROPE_THETA ** (-jnp.arange(0, ROPE, 2, dtype=F32) / ROPE)
    ang = positions.astype(F32)[:, None] * inv_freq
    cos, sin = jnp.cos(ang), jnp.sin(ang)
    n = positions.shape[0]
    one, zero = jnp.ones((n, NOPE), F32), jnp.zeros((n, half), F32)
    tail1, tail0 = jnp.ones((n, HEAD_PAD - QK_DIM), F32), jnp.zeros((n, HEAD_PAD - QK_DIM), F32)
    z64 = jnp.zeros((n, NOPE), F32)
    rc = jnp.concatenate([one, cos, cos, tail1], axis=1)
    rsl = jnp.concatenate([z64, -sin, zero, tail0], axis=1)
    rsh = jnp.concatenate([z64, zero, sin, tail0], axis=1)
    return rc, rsl, rsh


def _pad_heads(w_uq):
    return jnp.pad(w_uq, ((0, 0), (0, 0), (0, HEAD_PAD - QK_DIM))).reshape(w_uq.shape[0], HEADS * HEAD_PAD)


def _prep_local(b_in, g_q, g_kv, w_ukv):
    b_g, b_m, b_l = _split_bias(b_in)
    wk = jnp.pad(w_ukv[:, :, :NOPE], ((0, 0), (0, 0), (0, HEAD_PAD - NOPE))).reshape(KV_RANK, HEADS * HEAD_PAD).astype(BF16)
    wv = w_ukv[:, :, NOPE:].reshape(KV_RANK, MLA_W).astype(BF16)
    return dict(b_g=b_g, b_m=b_m, b_l=b_l, wk=wk, wv=wv, gq2=g_q[None, :], gkv2=g_kv[None, :])


def _local_attention(x, tables, wlat, prep, wuq, after):
    rc, rsl, rsh = tables
    b_g, b_m, b_l, wk, wv, gq2, gkv2 = (prep[n] for n in ("b_g", "b_m", "b_l", "wk", "wv", "gq2", "gkv2"))
    hl, q, k, v, qt, kt, vt, xb2 = _fwd_lat(x, wlat, b_l, gq2, wuq, gkv2, wk, wv, rc, rsl, rsh, after)
    o, lse = _attn_fwd(qt, k, vt)
    return dict(q=q, qt=qt, k=k, kt=kt, v=v, o=o, lse=lse, hl=hl, rc=rc, rsl=rsl, rsh=rsh, gq2=gq2, gkv2=gkv2,
                wuq=wuq, wk=wk, wv=wv, xb2=xb2, b_g=b_g, b_m=b_m, wlat=wlat)


def _local_head(st, x, tgt, wt, w_oa, sg_g, sg_b, w_s, b_s, w_ob, w_out, ln_g, ln_b):
    q, qt, k, kt, v, o, lse, hl = (st[n] for n in ("q", "qt", "k", "kt", "v", "o", "lse", "hl"))
    rc, rsl, rsh, gq2, gkv2, wuq, wk, wv = (st[n] for n in ("rc", "rsl", "rsh", "gq2", "gkv2", "wuq", "wk", "wv"))
    bsb = jnp.repeat(b_s.T, V_DIM, axis=1)
    hg, hm = _fwd_rest(st["xb2"], wt, st["b_g"], st["b_m"])
    (dr, dhg, dhm, do, dot, delta, dhgt, dhmt, dwout, dwoa, dwob, dws, dbs, dlng, dlnb, dsgg, dsgb, loss, dbg,
     dbm) = _mid(x, tgt, o, hm, hg, w_oa, w_ob, w_out, ln_g[None, :], ln_b[None, :], sg_g[None, :], sg_b[None, :],
                 w_s, bsb)
    early = {
        "w_oa": dwoa, "sgu_ln_g": dsgg[0], "sgu_ln_b": dsgb[0], "w_s": dws, "b_s": dbs[:, :GROUPS].T,
        "w_ob": dwob, "w_out": dwout, "ln_g": dlng[0], "ln_b": dlnb[0],
    }
    state = dict(q=q, qt=qt, k=k, kt=kt, v=v, do=do, dot=dot, lse=lse, delta=delta, hl=hl, rc=rc, rsl=rsl, rsh=rsh,
                 gq2=gq2, gkv2=gkv2, wuq=wuq, wk=wk, wv=wv, dr=dr, dhg=dhg, dhm=dhm, wt=wt, dbg=dbg, dbm=dbm,
                 dhgt=dhgt, dhmt=dhmt, xb2=st["xb2"], wlat=st["wlat"])
    return loss, early, state


def _local_attn_bwd(st, after):
    return _attn_bwd(st["q"], st["qt"], st["k"], st["kt"], st["v"], st["do"], st["dot"], st["lse"], st["delta"],
                     after)


def _local_tail(st, dq, dk, dv, after):
    dhl, dhlt, dwuq, dwk, dwv, dgq, dgkv, dbl = _lat_bwd(dq, dk, dv, st["hl"], st["rc"], st["rsl"], st["rsh"],
                                                         st["gq2"], st["gkv2"], st["wuq"], st["wk"], st["wv"], after)
    late = {
        "w_lat": _dwt_lat(dhlt, st["xb2"]),
        "b_in": _join_bias(st["dbg"], st["dbm"], dbl),
        "g_q": dgq[0],
        "w_uq": dwuq.reshape(Q_RANK, HEADS, HEAD_PAD)[:, :, :QK_DIM],
        "g_kv": dgkv[0],
        "w_ukv": jnp.concatenate([dwk.reshape(KV_RANK, HEADS, HEAD_PAD)[:, :, :NOPE],
                                  dwv.reshape(KV_RANK, HEADS, V_DIM)], axis=2),
    }
    return dhl, late


def _local_step(x, positions, tgt, wt, b_in, g_q, w_uq, g_kv, w_ukv, w_oa, sg_g, sg_b, w_s, b_s, w_ob, w_out, ln_g,
                ln_b):
    st = _local_attention(x, _rope_tables(positions), wt[:LAT_COLS], _prep_local(b_in, g_q, g_kv, w_ukv),
                          _pad_heads(w_uq).astype(BF16), b_in)
    loss, early, st = _local_head(st, x, tgt, wt, w_oa, sg_g, sg_b, w_s, b_s, w_ob, w_out, ln_g, ln_b)
    dq, dk, dv = _local_attn_bwd(st, loss)
    dhl, late = _local_tail(st, dq, dk, dv, dv)
    dx = _dx(st["dr"], st["dhg"], st["dhm"], dhl, st["wt"], st["wlat"], dhl)
    grads = {**early, **late}
    halves = [_dwt_early(st["dhmt"], st["dhgt"], st["xb2"], jnp.full((1,), h, jnp.int32), dhl, "dwt_half%d" % h)
              for h in range(2)]
    grads["w_in"] = jnp.concatenate([grads.pop("w_lat")[:LAT_COLS], jnp.concatenate(halves, axis=1)[LAT_COLS:]],
                                    axis=0)
    return loss, dx, grads


MESH = pl.DeviceIdType.MESH
N_CHIPS = 4
HBM_SPEC = pl.BlockSpec(memory_space=pl.ANY)
HBM_SPEC_STRICT = pl.BlockSpec(memory_space=pltpu.HBM)
VMEM_SPEC = pl.BlockSpec(memory_space=pltpu.VMEM)

REP_ROWS = 80


def _rows8(a):
    flat = a.reshape(-1)
    n = -(-flat.shape[0] // (8 * D_MODEL)) * 8 * D_MODEL
    return jnp.pad(flat, (0, n - flat.shape[0])).reshape(-1, D_MODEL)


def _place():
    x, y, c = lax.axis_index("x"), lax.axis_index("y"), lax.axis_index("c")
    others = [(1 - x, y), (x, 1 - y), (1 - x, 1 - y)]
    return x, y, c, others


N_DEV = 8
LOSS_TILE = (8, 128)


def _cast_own(shards, me, after):
    n = len(shards)

    def body(me_ref, *refs):
        for w in range(n):
            refs[n + 1 + w][...] = refs[w][...].astype(BF16)

    return pl.pallas_call(
        body, name="cast_own",
        grid_spec=pltpu.PrefetchScalarGridSpec(
            num_scalar_prefetch=1, grid=(1,),
            in_specs=[pl.BlockSpec(s.shape, lambda i, me_ref: (0, 0)) for s in shards]
            + [pl.BlockSpec(memory_space=pl.ANY)],
            out_specs=[pl.BlockSpec((None,) + s.shape, lambda i, me_ref: (me_ref[0], 0, 0)) for s in shards]),
        out_shape=[jax.ShapeDtypeStruct((N_CHIPS,) + s.shape, BF16) for s in shards],
        compiler_params=pltpu.CompilerParams(vmem_limit_bytes=VMEM_LIMIT),
    )(me, *shards, after)


def _cast_first(lat, uq, me):
    def body(me_ref, lat_ref, uq_ref, wlat_ref, guq_ref):
        wlat_ref[...] = lat_ref[...].astype(BF16)
        guq_ref[...] = uq_ref[...].astype(BF16)

    return pl.pallas_call(
        body, name="cast_first",
        grid_spec=pltpu.PrefetchScalarGridSpec(
            num_scalar_prefetch=1, grid=(1,),
            in_specs=[pl.BlockSpec((LAT_COLS, D_MODEL), lambda i, me_ref: (0, 0)),
                      pl.BlockSpec(uq.shape, lambda i, me_ref: (0, 0))],
            out_specs=[pl.BlockSpec((LAT_COLS, D_MODEL), lambda i, me_ref: (0, 0)),
                       pl.BlockSpec((None,) + uq.shape, lambda i, me_ref: (me_ref[0], 0, 0))]),
        out_shape=[jax.ShapeDtypeStruct((LAT_COLS, D_MODEL), BF16),
                   jax.ShapeDtypeStruct((N_CHIPS,) + uq.shape, BF16)],
    )(me, lat, uq)


def _first_copies(wlat_ref, guq_ref, send_sems, recv_sems, shapes):
    x, y, c, others = _place()
    me = 2 * x + y
    hl, hu = shapes[0][1] // 2, shapes[1][2] // 2
    lat_half = wlat_ref.at[:, pl.ds(c * hl, hl)]

    def copy(src, dst, k, to):
        return pltpu.make_async_remote_copy(src_ref=src, dst_ref=dst, send_sem=send_sems.at[k],
                                            recv_sem=recv_sems.at[k], device_id=to, device_id_type=MESH)

    def uq_half(chip):
        return guq_ref.at[chip, :, pl.ds(c * hu, hu)]

    lat_out = [copy(lat_half, lat_half, j, (*others[j], c)) for j in range(3)]
    uq_out = [copy(uq_half(me), uq_half(me), 3 + j, (*others[j], c)) for j in range(3)]
    j0 = jnp.maximum(x + 2 * y - 1, 0)
    lat_in = copy(lat_half, lat_half, j0, (0, 0, c))
    uq_in = [copy(uq_half(me), uq_half(2 * px + py), 3 + j, (px, py, c)) for j, (px, py) in enumerate(others)]
    return me, lat_out, uq_out, lat_in, uq_in


def _first_start(wlat, guq):
    shapes = (wlat.shape, guq.shape)

    def body(wlat_ref, guq_ref, send_sems, recv_sems, wlat_thru, guq_thru, token):
        me, lat_out, uq_out, _, _ = _first_copies(wlat_ref, guq_ref, send_sems, recv_sems, shapes)

        @pl.when(me == 0)
        def _():
            for cp in lat_out:
                cp.start()

        for cp in uq_out:
            cp.start()
        token[...] = jnp.zeros_like(token)

    outs = pl.pallas_call(
        body, name="first_start",
        out_shape=(pltpu.SemaphoreType.DMA((6,)), pltpu.SemaphoreType.DMA((6,)), pltpu.HBM(wlat.shape, BF16),
                   pltpu.HBM(guq.shape, BF16), jax.ShapeDtypeStruct(LOSS_TILE, F32)),
        in_specs=[HBM_SPEC_STRICT] * 2, out_specs=(SEM_SPEC, SEM_SPEC, HBM_SPEC_STRICT, HBM_SPEC_STRICT, VMEM_SPEC),
        input_output_aliases={0: 2, 1: 3},
        compiler_params=pltpu.CompilerParams(has_side_effects=SPLIT_EFFECT),
    )(pltpu.with_memory_space_constraint(wlat, pltpu.HBM), pltpu.with_memory_space_constraint(guq, pltpu.HBM))
    return outs


def _first_wait(send_sems, recv_sems, wlat, guq, *after):
    shapes = (wlat.shape, guq.shape)

    def body(wlat_ref, guq_ref, send_sems, recv_sems, *rest):
        me, lat_out, uq_out, lat_in, uq_in = _first_copies(wlat_ref, guq_ref, send_sems, recv_sems, shapes)

        @pl.when(me == 0)
        def _():
            for cp in lat_out:
                cp.wait_send()

        @pl.when(me != 0)
        def _():
            lat_in.wait_recv()

        for cp in uq_out:
            cp.wait_send()
        for cp in uq_in:
            cp.wait_recv()

    return pl.pallas_call(
        body, name="first_wait", out_shape=(pltpu.HBM(wlat.shape, BF16), pltpu.HBM(guq.shape, BF16)),
        in_specs=[HBM_SPEC_STRICT, HBM_SPEC_STRICT, SEM_SPEC, SEM_SPEC] + [HBM_SPEC] * len(after),
        out_specs=(HBM_SPEC_STRICT, HBM_SPEC_STRICT), input_output_aliases={0: 0, 1: 1},
        compiler_params=pltpu.CompilerParams(has_side_effects=SPLIT_EFFECT),
    )(wlat, guq, send_sems, recv_sems, *after)


def _first_forward(wlat, guq):
    hl, hu = wlat.shape[1] // 2, guq.shape[2] // 2

    def body(wlat_in, guq_in, wlat_ref, guq_ref, send_sems, recv_sems):
        x, y, c, others = _place()
        me = 2 * x + y
        sibling = (x, y, 1 - c)

        def copy(part, k):
            return pltpu.make_async_remote_copy(src_ref=part, dst_ref=part, send_sem=send_sems.at[k],
                                                recv_sem=recv_sems.at[k], device_id=sibling, device_id_type=MESH)

        def uq_part(j, half):
            px, py = others[j]
            return guq_ref.at[2 * px + py, :, pl.ds(half * hu, hu)]

        cps = [copy(uq_part(j, c), j) for j in range(3)]
        for cp in cps:
            cp.start()

        @pl.when(me != 0)
        def _():
            mine = copy(wlat_ref.at[:, pl.ds(c * hl, hl)], 3)
            mine.start()
            copy(wlat_ref.at[:, pl.ds((1 - c) * hl, hl)], 3).wait_recv()
            mine.wait_send()

        for j in range(3):
            copy(uq_part(j, 1 - c), j).wait_recv()
        for cp in cps:
            cp.wait_send()

    return pl.pallas_call(
        body, name="first_forward", in_specs=[HBM_SPEC, HBM_SPEC], out_specs=[HBM_SPEC, HBM_SPEC],
        out_shape=[jax.ShapeDtypeStruct(wlat.shape, BF16), jax.ShapeDtypeStruct(guq.shape, BF16)],
        input_output_aliases={0: 0, 1: 1},
        scratch_shapes=[pltpu.SemaphoreType.DMA((4,)), pltpu.SemaphoreType.DMA((4,))],
    )(wlat, guq)


def _gather_copy(b_ref, buf, w, j, src_chip, dst_chip, to, rows, send_sems, recv_sems):
    _, _, c, _ = _place()
    hc = _half(buf)
    return pltpu.make_async_remote_copy(
        src_ref=b_ref.at[src_chip, rows, pl.ds(c * hc, hc)], dst_ref=b_ref.at[dst_chip, rows, pl.ds(c * hc, hc)],
        send_sem=send_sems.at[3 * w + j], recv_sem=recv_sems.at[3 * w + j], device_id=to, device_id_type=MESH)


def _gather_rows(buf, w, chip, fn):
    if w != 0:
        fn(slice(None))
        return
    pl.when(chip == 0)(lambda: fn(pl.ds(LAT_COLS, buf.shape[1] - LAT_COLS)))
    pl.when(chip != 0)(lambda: fn(slice(None)))


def _gather_start(bufs, after):
    n = len(bufs)

    def body(*refs):
        b_refs = refs[:n]
        send_sems, recv_sems, token = refs[n + 1], refs[n + 2], refs[-1]
        x, y, c, others = _place()
        me = 2 * x + y
        for w in range(n):
            def start(rows, w=w):
                for j, (px, py) in enumerate(others):
                    _gather_copy(b_refs[w], bufs[w], w, j, me, me, (px, py, c), rows, send_sems, recv_sems).start()
            _gather_rows(bufs[w], w, me, start)
        token[...] = jnp.zeros_like(token)

    hbm = [pltpu.HBM(b.shape, BF16) for b in bufs]
    outs = pl.pallas_call(
        body, name="gather_start",
        out_shape=(pltpu.SemaphoreType.DMA((3 * n,)), pltpu.SemaphoreType.DMA((3 * n,)), *hbm,
                   jax.ShapeDtypeStruct(LOSS_TILE, F32)),
        in_specs=[HBM_SPEC_STRICT] * n + [HBM_SPEC],
        out_specs=(SEM_SPEC, SEM_SPEC, *[HBM_SPEC_STRICT] * n, VMEM_SPEC),
        input_output_aliases={i: 2 + i for i in range(n)},
        compiler_params=pltpu.CompilerParams(has_side_effects=SPLIT_EFFECT),
    )(*[pltpu.with_memory_space_constraint(b, pltpu.HBM) for b in bufs], after)
    return outs[0], outs[1], list(outs[2:2 + n]), outs[-1]


def _gather_wait(send_sems, recv_sems, bufs, after):
    n = len(bufs)

    def body(*refs):
        b_refs = refs[:n]
        send_sems, recv_sems = refs[n], refs[n + 1]
        x, y, c, others = _place()
        me = 2 * x + y
        for w in range(n):
            for j, (px, py) in enumerate(others):
                def copy(rows, w=w, j=j, px=px, py=py):
                    return _gather_copy(b_refs[w], bufs[w], w, j, me, 2 * px + py, (px, py, c), rows, send_sems,
                                        recv_sems)
                _gather_rows(bufs[w], w, me, lambda rows, copy=copy: copy(rows).wait_send())
                _gather_rows(bufs[w], w, 2 * px + py, lambda rows, copy=copy: copy(rows).wait_recv())

    outs = pl.pallas_call(
        body, name="gather_wait", out_shape=tuple(pltpu.HBM(b.shape, b.dtype) for b in bufs),
        in_specs=[HBM_SPEC_STRICT] * n + [SEM_SPEC, SEM_SPEC, HBM_SPEC],
        out_specs=tuple([HBM_SPEC_STRICT] * n), input_output_aliases={i: i for i in range(n)},
        compiler_params=pltpu.CompilerParams(has_side_effects=SPLIT_EFFECT),
    )(*bufs, send_sems, recv_sems, after)
    return list(outs)


def _gather_finish(bufs):
    n = len(bufs)

    def body(*refs):
        b_refs = refs[n:2 * n]
        send_sems, recv_sems = refs[2 * n:]
        x, y, c, others = _place()
        cps = []
        for w in range(n):
            hc = _half(bufs[w])
            for j, (px, py) in enumerate(others):
                part = b_refs[w].at[2 * px + py, :, pl.ds(c * hc, hc)]
                cps.append(pltpu.make_async_remote_copy(
                    src_ref=part, dst_ref=part, send_sem=send_sems.at[3 * w + j], recv_sem=recv_sems.at[3 * w + j],
                    device_id=(x, y, 1 - c), device_id_type=MESH))
        for cp in cps:
            cp.start()
        for w in range(n):
            hc = _half(bufs[w])
            for j, (px, py) in enumerate(others):
                theirs = b_refs[w].at[2 * px + py, :, pl.ds((1 - c) * hc, hc)]
                pltpu.make_async_remote_copy(
                    src_ref=theirs, dst_ref=theirs, send_sem=send_sems.at[3 * w + j], recv_sem=recv_sems.at[3 * w + j],
                    device_id=(x, y, 1 - c), device_id_type=MESH).wait_recv()
        for cp in cps:
            cp.wait_send()

    return pl.pallas_call(
        body, name="gather_finish", in_specs=[HBM_SPEC] * n, out_specs=[HBM_SPEC] * n,
        out_shape=[jax.ShapeDtypeStruct(b.shape, b.dtype) for b in bufs],
        input_output_aliases={i: i for i in range(n)},
        scratch_shapes=[pltpu.SemaphoreType.DMA((3 * n,)), pltpu.SemaphoreType.DMA((3 * n,))],
    )(*bufs)


def _half(a):
    return a.shape[-1] // 2


def _exchange_pairs(parts, name):
    n = len(parts)

    def body(*refs):
        p_refs, r_refs = refs[:n], refs[n:2 * n]
        send_sems, recv_sems = refs[2 * n:]
        x, y, c, _ = _place()
        cps = []
        for w in range(n):
            h = _half(parts[w])
            cps.append(pltpu.make_async_remote_copy(
                src_ref=p_refs[w].at[:, :, pl.ds((1 - c) * h, h)], dst_ref=r_refs[w],
                send_sem=send_sems.at[w], recv_sem=recv_sems.at[w], device_id=(x, y, 1 - c), device_id_type=MESH))
        for cp in cps:
            cp.start()
        for cp in cps:
            cp.wait()

    return pl.pallas_call(
        body, name=name, in_specs=[HBM_SPEC] * n, out_specs=[HBM_SPEC] * n,
        out_shape=[jax.ShapeDtypeStruct((N_CHIPS, p.shape[1], _half(p)), BF16) for p in parts],
        scratch_shapes=[pltpu.SemaphoreType.DMA((n,)), pltpu.SemaphoreType.DMA((n,))],
    )(*parts)


def _sibling_part(ref, w, n_whole, shape, c):
    if w < n_whole:
        return ref
    h = shape[-1] // 2
    return ref.at[:, :, pl.ds((1 - c) * h, h)]


def _pairs_start(parts, all_loss, n_whole):
    n = len(parts)

    def body(*refs):
        p_refs, r_refs, loss_ref = refs[:n], refs[n:2 * n], refs[2 * n]
        send_sems, recv_sems, token = refs[2 * n + 1], refs[2 * n + 2], refs[-1]
        x, y, c, _ = _place()
        for w in range(n):
            h = _half(parts[w])
            pltpu.make_async_remote_copy(
                src_ref=_sibling_part(p_refs[w], w, n_whole, parts[w].shape, c), dst_ref=r_refs[w],
                send_sem=send_sems.at[w], recv_sem=recv_sems.at[w], device_id=(x, y, 1 - c),
                device_id_type=MESH).start()
        me = 4 * x + 2 * y + c
        for t in range(1, N_DEV):
            d = (me + t) % N_DEV
            pltpu.make_async_remote_copy(
                src_ref=loss_ref.at[me], dst_ref=loss_ref.at[me], send_sem=send_sems.at[n + t - 1],
                recv_sem=recv_sems.at[n + t - 1], device_id=(d // 4, (d // 2) % 2, d % 2), device_id_type=MESH).start()
        token[...] = jnp.zeros_like(token)

    lands = [pltpu.HBM(p.shape if w < n_whole else (N_CHIPS, p.shape[1], _half(p)), BF16)
             for w, p in enumerate(parts)]
    nsem = n + N_DEV - 1
    outs = pl.pallas_call(
        body, name="pairs_start",
        out_shape=(pltpu.SemaphoreType.DMA((nsem,)), pltpu.SemaphoreType.DMA((nsem,)),
                   *[pltpu.HBM(p.shape, p.dtype) for p in parts], *lands, pltpu.HBM(all_loss.shape, F32),
                   jax.ShapeDtypeStruct(LOSS_TILE, F32)),
        in_specs=[HBM_SPEC_STRICT] * (2 * n + 1),
        out_specs=(SEM_SPEC, SEM_SPEC, *[HBM_SPEC_STRICT] * (2 * n + 1), VMEM_SPEC),
        input_output_aliases={i: 2 + i for i in range(2 * n + 1)},
        compiler_params=pltpu.CompilerParams(has_side_effects=SPLIT_EFFECT),
    )(*[pltpu.with_memory_space_constraint(p, pltpu.HBM) for p in parts],
      *[pltpu.with_memory_space_constraint(lax.empty(l.shape, BF16), pltpu.HBM) for l in lands],
      pltpu.with_memory_space_constraint(all_loss, pltpu.HBM))
    return outs[0], outs[1], list(outs[2:2 + n]), list(outs[2 + n:2 + 2 * n]), outs[2 + 2 * n], outs[-1]


def _pairs_wait(send_sems, recv_sems, parts, lands, all_loss, after, n_whole):
    n = len(parts)

    def body(*refs):
        p_refs, r_refs, loss_ref = refs[:n], refs[n:2 * n], refs[2 * n]
        send_sems, recv_sems = refs[2 * n + 1], refs[2 * n + 2]
        x, y, c, _ = _place()
        for w in range(n):
            h = _half(parts[w])
            cp = pltpu.make_async_remote_copy(
                src_ref=_sibling_part(p_refs[w], w, n_whole, parts[w].shape, c), dst_ref=r_refs[w],
                send_sem=send_sems.at[w],
                recv_sem=recv_sems.at[w], device_id=(x, y, 1 - c), device_id_type=MESH)
            cp.wait_send()
            cp.wait_recv()
        me = 4 * x + 2 * y + c
        for t in range(1, N_DEV):
            d = (me + N_DEV - t) % N_DEV
            cp = pltpu.make_async_remote_copy(
                src_ref=loss_ref.at[me], dst_ref=loss_ref.at[d], send_sem=send_sems.at[n + t - 1],
                recv_sem=recv_sems.at[n + t - 1], device_id=(d // 4, (d // 2) % 2, d % 2), device_id_type=MESH)
            cp.wait_send()
            cp.wait_recv()

    bufs = (*parts, *lands, all_loss)
    outs = pl.pallas_call(
        body, name="pairs_wait", out_shape=tuple(pltpu.HBM(a.shape, a.dtype) for a in bufs),
        in_specs=[HBM_SPEC_STRICT] * len(bufs) + [SEM_SPEC, SEM_SPEC, HBM_SPEC],
        out_specs=tuple([HBM_SPEC_STRICT] * len(bufs)), input_output_aliases={i: i for i in range(len(bufs))},
        compiler_params=pltpu.CompilerParams(has_side_effects=SPLIT_EFFECT),
    )(*bufs, send_sems, recv_sems, after)
    return list(outs[:n]), list(outs[n:2 * n]), outs[2 * n]


def _add_pair(ps, rs, c, name, n_whole=0):
    n = len(ps)

    def body(c_ref, *refs):
        for w in range(n):
            refs[2 * n + w][...] = (refs[w][...].astype(F32) + refs[n + w][...].astype(F32)).astype(BF16)

    def slab_spec(a):
        return pl.BlockSpec((None,) + a.shape[1:], lambda k, c_ref: (k, 0, 0))

    def my_half_spec(a):
        return pl.BlockSpec((None, a.shape[1], _half(a)), lambda k, c_ref: (k, 0, c_ref[0]))

    return pl.pallas_call(
        body, name=name,
        grid_spec=pltpu.PrefetchScalarGridSpec(
            num_scalar_prefetch=1, grid=(N_CHIPS,),
            in_specs=[slab_spec(p) if w < n_whole else my_half_spec(p) for w, p in enumerate(ps)]
            + [slab_spec(r) for r in rs],
            out_specs=[slab_spec(r) for r in rs]),
        out_shape=[jax.ShapeDtypeStruct(r.shape, BF16) for r in rs],
        compiler_params=_params(),
    )(c, *ps, *rs)


SEM_SPEC = pl.BlockSpec(memory_space=pltpu.SEMAPHORE)
SPLIT_EFFECT = pltpu.SideEffectType.DATAFLOW_SIDE_EFFECTING


def _chips_start(qs, name):
    n = len(qs)

    def body(*refs):
        q_refs, land_refs = refs[:n], refs[n:2 * n]
        send_sems, recv_sems, token = refs[2 * n], refs[2 * n + 1], refs[-1]
        x, y, c, others = _place()
        me = 2 * x + y
        for w in range(n):
            for j, (px, py) in enumerate(others):
                pltpu.make_async_remote_copy(
                    src_ref=q_refs[w].at[2 * px + py], dst_ref=land_refs[w].at[me], send_sem=send_sems.at[3 * w + j],
                    recv_sem=recv_sems.at[3 * w + j], device_id=(px, py, c), device_id_type=MESH).start()
        token[...] = jnp.zeros_like(token)

    hbm = [pltpu.HBM(q.shape, BF16) for q in qs]
    outs = pl.pallas_call(
        body, name=name,
        out_shape=(pltpu.SemaphoreType.DMA((3 * n,)), pltpu.SemaphoreType.DMA((3 * n,)), *hbm, *hbm,
                   jax.ShapeDtypeStruct(LOSS_TILE, F32)),
        in_specs=[HBM_SPEC_STRICT] * (2 * n),
        out_specs=(SEM_SPEC, SEM_SPEC, *[HBM_SPEC_STRICT] * (2 * n), VMEM_SPEC),
        input_output_aliases={i: 2 + i for i in range(2 * n)},
        compiler_params=pltpu.CompilerParams(has_side_effects=SPLIT_EFFECT),
    )(*[pltpu.with_memory_space_constraint(q, pltpu.HBM) for q in qs],
      *[pltpu.with_memory_space_constraint(lax.empty(q.shape, BF16), pltpu.HBM) for q in qs])
    return outs[0], outs[1], outs[2:2 + n], outs[2 + n:2 + 2 * n], outs[-1]


def _chips_wait(send_sems, recv_sems, q_thru, land_thru, after, name):
    n = len(q_thru)

    def body(*refs):
        q_refs, land_refs = refs[:n], refs[n:2 * n]
        send_sems, recv_sems = refs[2 * n], refs[2 * n + 1]
        x, y, c, others = _place()
        me = 2 * x + y
        for w in range(n):
            for j, (px, py) in enumerate(others):
                cp = pltpu.make_async_remote_copy(
                    src_ref=q_refs[w].at[2 * px + py], dst_ref=land_refs[w].at[2 * px + py],
                    send_sem=send_sems.at[3 * w + j], recv_sem=recv_sems.at[3 * w + j], device_id=(px, py, c),
                    device_id_type=MESH)
                cp.wait_send()
                cp.wait_recv()

    outs = pl.pallas_call(
        body, name=name, out_shape=tuple(pltpu.HBM(a.shape, a.dtype) for a in (*q_thru, *land_thru)),
        in_specs=[HBM_SPEC_STRICT] * (2 * n) + [SEM_SPEC, SEM_SPEC, HBM_SPEC],
        out_specs=tuple([HBM_SPEC_STRICT] * (2 * n)), input_output_aliases={i: i for i in range(2 * n)},
        compiler_params=pltpu.CompilerParams(has_side_effects=SPLIT_EFFECT),
    )(*q_thru, *land_thru, send_sems, recv_sems, after)
    return list(outs[:n]), list(outs[n:])


def _sum_chips(qs, rs, idx, all_dtypes):
    n = len(rs)
    n_all = len(all_dtypes)

    def body(idx_ref, *refs):
        c = idx_ref[4]
        for w in range(n):
            q_ref, g_ref = refs[w], refs[4 * n + w]
            acc = q_ref[...].astype(F32)
            for t in range(1, N_CHIPS):
                acc = acc + refs[n + 3 * w + t - 1][...].astype(F32)
            h = rs[w].shape[2]
            mine = pl.ds(pl.multiple_of(c * h, 128), h)
            g_ref[...] = jnp.zeros_like(g_ref)
            if w >= n - n_all:
                g_ref[idx_ref[0], :, mine] = acc.astype(g_ref.dtype)
            else:
                g_ref[:, mine] = acc

    shapes = [jax.ShapeDtypeStruct((r.shape[1], 2 * r.shape[2]), F32) for r in rs[:n - n_all]]
    shapes += [jax.ShapeDtypeStruct((N_CHIPS, r.shape[1], 2 * r.shape[2]), dt)
               for r, dt in zip(rs[n - n_all:], all_dtypes)]

    def slab_spec(a, t):
        return pl.BlockSpec((None,) + a.shape[1:], lambda i, idx_ref: (idx_ref[t], 0, 0))

    def whole_spec(sh):
        return pl.BlockSpec(sh.shape, lambda i, idx_ref: (0,) * len(sh.shape))

    return pl.pallas_call(
        body, name="sum_chips",
        grid_spec=pltpu.PrefetchScalarGridSpec(
            num_scalar_prefetch=1, grid=(1,),
            in_specs=[slab_spec(q, 0) for q in qs] + [slab_spec(r, t) for r in rs for t in range(1, N_CHIPS)],
            out_specs=[whole_spec(sh) for sh in shapes]),
        out_shape=shapes, compiler_params=_params(),
    )(idx, *qs, *[r for r in rs for _ in range(1, N_CHIPS)])


def _share(shards, alls):
    n, na = len(shards), len(alls)
    total = n + na

    def body(*refs):
        g_refs, a_refs = refs[total:total + n], refs[total + n:2 * total]
        send_sems, recv_sems = refs[2 * total:]
        x, y, c, others = _place()
        me = 2 * x + y
        sibling = (x, y, 1 - c)

        def cols_of(w, half):
            h = shards[w].shape[1] // 2
            return g_refs[w].at[:, pl.ds(half * h, h)]

        def slab(a, chip, half):
            h = alls[a].shape[2] // 2
            return a_refs[a].at[chip, :, pl.ds(half * h, h)]

        def copy(src, dst, k, to):
            return pltpu.make_async_remote_copy(src_ref=src, dst_ref=dst, send_sem=send_sems.at[k],
                                                recv_sem=recv_sems.at[k], device_id=to, device_id_type=MESH)

        cps = []
        for a in range(na):
            base = n + 7 * a
            for j, (px, py) in reversed(list(enumerate(others))):
                cps.append(copy(slab(a, me, c), slab(a, me, c), base + 1 + j, (px, py, c)))
            cps.append(copy(slab(a, me, c), slab(a, me, c), base, sibling))
        cps += [copy(cols_of(w, c), cols_of(w, c), w, sibling) for w in range(n)]
        for cp in cps:
            cp.start()
        fwd = []
        for a in range(na):
            base = n + 7 * a
            for j, (px, py) in enumerate(others):
                chip = 2 * px + py
                copy(slab(a, me, c), slab(a, chip, c), base + 1 + j, (px, py, c)).wait_recv()
                cp = copy(slab(a, chip, c), slab(a, chip, c), base + 4 + j, sibling)
                cp.start()
                fwd.append(cp)
        for a in range(na):
            base = n + 7 * a
            for j, (px, py) in enumerate(others):
                chip = 2 * px + py
                copy(slab(a, chip, c), slab(a, chip, 1 - c), base + 4 + j, sibling).wait_recv()
            copy(slab(a, me, c), slab(a, me, 1 - c), base, sibling).wait_recv()
        for w in range(n):
            copy(cols_of(w, c), cols_of(w, 1 - c), w, sibling).wait_recv()
        for cp in cps + fwd:
            cp.wait_send()

    nsem = n + 7 * na
    return pl.pallas_call(
        body, name="share", in_specs=[HBM_SPEC] * total, out_specs=[HBM_SPEC] * total,
        out_shape=[jax.ShapeDtypeStruct(a.shape, a.dtype) for a in (*shards, *alls)],
        input_output_aliases={i: i for i in range(total)},
        scratch_shapes=[pltpu.SemaphoreType.DMA((nsem,)), pltpu.SemaphoreType.DMA((nsem,))],
    )(*shards, *alls)


def _adamw(w, g, m, v):
    m2 = ADAM_B1 * m + (1.0 - ADAM_B1) * g
    v2 = ADAM_B2 * v + (1.0 - ADAM_B2) * (g * g)
    m_hat = m2 / (1.0 - ADAM_B1 ** ADAM_STEP)
    v_hat = v2 / (1.0 - ADAM_B2 ** ADAM_STEP)
    return -ADAM_LR * (m_hat / (jnp.sqrt(v_hat) + ADAM_EPS) + ADAM_WD * w), m2, v2


def _update_w_in(wt, gt, mt, vt, lat, owner, tile):
    nlat = lat.shape[0] // tile

    def body(owner_ref, w_ref, g_ref, m_ref, v_ref, lat_ref, g2_ref, d_ref, m2_ref, v2_ref):
        row = pl.program_id(0) * tile + lax.broadcasted_iota(jnp.int32, (tile, 1), 0)
        g = jnp.where((row < LAT_COLS) & (owner_ref[0] == 1), lat_ref[...].astype(F32), g_ref[...])
        g2_ref[...] = g
        d_ref[...], m2_ref[...], v2_ref[...] = _adamw(w_ref[...], g, m_ref[...], v_ref[...])

    spec = pl.BlockSpec((tile, wt.shape[1]), lambda i, o: (i, 0))
    return pl.pallas_call(
        body, name="update_w_in",
        grid_spec=pltpu.PrefetchScalarGridSpec(
            num_scalar_prefetch=1, grid=(wt.shape[0] // tile,),
            in_specs=[spec] * 4 + [pl.BlockSpec((tile, wt.shape[1]), lambda i, o: (jnp.minimum(i, nlat - 1), 0))],
            out_specs=[spec] * 4),
        out_shape=[jax.ShapeDtypeStruct(wt.shape, F32)] * 4,
        compiler_params=_params(("parallel",)),
    )(owner, wt, gt, mt, vt, lat)


def _update_small(ws, gs, ms, vs):
    n = len(ws)

    def body(*refs):
        for k in range(n):
            w_ref, g_ref, m_ref, v_ref = refs[k], refs[n + k], refs[2 * n + k], refs[3 * n + k]
            d, m2, v2 = _adamw(w_ref[...], g_ref[...], m_ref[...], v_ref[...])
            refs[4 * n + k][...] = d
            refs[5 * n + k][...] = m2
            refs[6 * n + k][...] = v2

    shapes = [jax.ShapeDtypeStruct(w.shape, F32) for w in ws]
    outs = pl.pallas_call(
        body, name="update_small", in_specs=[VMEM_SPEC] * (4 * n), out_specs=[VMEM_SPEC] * (3 * n),
        out_shape=shapes * 3,
        compiler_params=pltpu.CompilerParams(vmem_limit_bytes=VMEM_LIMIT),
    )(*ws, *gs, *ms, *vs)
    return outs[:n], outs[n:2 * n], outs[2 * n:]


REPLICATED = ("b_in", "g_q", "g_kv", "w_ukv", "sgu_ln_g", "sgu_ln_b", "w_s", "b_s", "ln_g", "ln_b")
ORDER = ("w_in", "b_in", "g_q", "w_uq", "g_kv", "w_ukv", "w_oa", "sgu_ln_g", "sgu_ln_b", "w_s", "b_s", "w_ob", "w_out",
         "ln_g", "ln_b")


def kernel(x, positions, w_in, b_in, g_q, w_uq, g_kv, w_ukv, w_oa, sgu_ln_g, sgu_ln_b, w_s, b_s, w_ob, w_out, ln_g, ln_b, loss_target, m_w_in, m_b_in, m_g_q, m_w_uq, m_g_kv, m_w_ukv, m_w_oa, m_sgu_ln_g, m_sgu_ln_b, m_w_s, m_b_s, m_w_ob, m_w_out, m_ln_g, m_ln_b, v_w_in, v_b_in, v_g_q, v_w_uq, v_g_kv, v_w_ukv, v_w_oa, v_sgu_ln_g, v_sgu_ln_b, v_w_s, v_b_s, v_w_ob, v_w_out, v_ln_g, v_ln_b):
    w = dict(w_in=w_in, b_in=b_in, g_q=g_q, w_uq=w_uq, g_kv=g_kv, w_ukv=w_ukv, w_oa=w_oa, sgu_ln_g=sgu_ln_g,
             sgu_ln_b=sgu_ln_b, w_s=w_s, b_s=b_s, w_ob=w_ob, w_out=w_out, ln_g=ln_g, ln_b=ln_b)
    m = dict(w_in=m_w_in, b_in=m_b_in, g_q=m_g_q, w_uq=m_w_uq, g_kv=m_g_kv, w_ukv=m_w_ukv, w_oa=m_w_oa,
             sgu_ln_g=m_sgu_ln_g, sgu_ln_b=m_sgu_ln_b, w_s=m_w_s, b_s=m_b_s, w_ob=m_w_ob, w_out=m_w_out, ln_g=m_ln_g,
             ln_b=m_ln_b)
    v = dict(w_in=v_w_in, b_in=v_b_in, g_q=v_g_q, w_uq=v_w_uq, g_kv=v_g_kv, w_ukv=v_w_ukv, w_oa=v_w_oa,
             sgu_ln_g=v_sgu_ln_g, sgu_ln_b=v_sgu_ln_b, w_s=v_w_s, b_s=v_b_s, w_ob=v_w_ob, w_out=v_w_out, ln_g=v_ln_g,
             ln_b=v_ln_b)
    w, m, v = ({n: a[0] for n, a in d.items()} for d in (w, m, v))
    c = lax.axis_index("c")

    wt_shard, mt_shard, vt_shard = (jnp.transpose(d["w_in"]) for d in (w, m, v))
    xi, yi = lax.axis_index("x"), lax.axis_index("y")
    me1 = (2 * xi + yi).reshape(1).astype(jnp.int32)
    first = _first_start(*_cast_first(wt_shard, _pad_heads(w["w_uq"]), me1))
    tables = _rope_tables(positions[0])
    prep = _prep_local(w["b_in"], w["g_q"], w["g_kv"], w["w_ukv"])
    bufs = _cast_own([wt_shard, w["w_oa"], w["w_ob"], w["w_out"]], me1, first[4])
    send0, recv0, bufs, token0 = _gather_start(bufs, first[4])
    g_lat, g_uq = _first_forward(*_first_wait(*first[:4], token0, *tables, *prep.values()))
    st = _local_attention(x[0], tables, g_lat, prep, g_uq.reshape(Q_RANK, HEADS * HEAD_PAD), token0)
    g_in, g_oa, g_ob, g_out = _gather_finish(_gather_wait(send0, recv0, bufs, st["o"]))
    wt = g_in.reshape(IN_W, D_MODEL)

    loss, early, st = _local_head(
        st, x[0], loss_target[0], wt, g_oa, w["sgu_ln_g"], w["sgu_ln_b"], w["w_s"], w["b_s"], g_ob,
        g_out.reshape(D_MODEL, D_MODEL), w["ln_g"], w["ln_b"])

    c1 = c.reshape(1).astype(jnp.int32)
    idx = jnp.stack([2 * xi + yi, 2 * (1 - xi) + yi, 2 * xi + (1 - yi), 2 * (1 - xi) + (1 - yi), c]).astype(jnp.int32)
    slabs = lambda a: a.reshape(N_CHIPS, IN_W // N_CHIPS, D_MODEL // 2)
    theirs = slabs(_dwt_early(st["dhmt"], st["dhgt"], st["xb2"], 1 - c1, c1, "dwt_theirs"))
    parts1 = [theirs, early["w_oa"].astype(BF16), early["w_ob"].astype(BF16),
              early["w_out"].reshape(N_CHIPS, SLAB_W, D_MODEL).astype(BF16)]
    my_loss = lax.dynamic_update_slice(jnp.zeros((N_DEV,) + LOSS_TILE, F32), jnp.broadcast_to(loss, (1,) + LOSS_TILE),
                                       (4 * xi + 2 * yi + c, 0, 0))
    sems0 = _pairs_start(parts1, my_loss, 1)
    mine = slabs(_dwt_early(st["dhmt"], st["dhgt"], st["xb2"], c1, sems0[5], "dwt_mine"))
    parts1, recv1, all_loss = _pairs_wait(*sems0[:5], mine, 1)
    pairs1 = _add_pair([mine, *parts1[1:]], recv1, c1, "add_pair_early", n_whole=1)
    sems1 = _chips_start(pairs1, "chips_start_early")
    dq, dk, dv = _local_attn_bwd(st, sems1[4])
    dhl, late = _local_tail(st, dq, dk, dv, dv)

    grads = {**early, **late}
    rep = jnp.concatenate([_rows8(grads[n]) for n in REPLICATED], axis=0)
    rep = jnp.pad(rep, ((0, N_CHIPS * REP_ROWS - rep.shape[0]), (0, 0))).reshape(N_CHIPS, REP_ROWS, D_MODEL)
    parts2 = [late["w_uq"].reshape(N_CHIPS, Q_RANK // N_CHIPS, HEADS * QK_DIM).astype(BF16), rep.astype(BF16),
              late["w_lat"].reshape(N_CHIPS, LAT_ROWS_PAD // N_CHIPS, D_MODEL)]
    pairs2 = _add_pair(parts2, _exchange_pairs(parts2, "exchange_pairs_late"), c1, "add_pair_late")
    sems2 = _chips_start(pairs2, "chips_start_late")
    dx = _dx(st["dr"], st["dhg"], st["dhm"], dhl, st["wt"], st["wlat"], sems2[4])
    pairs2, landed2 = _chips_wait(*sems2[:4], dx, "chips_wait_late")
    pairs1, landed1 = _chips_wait(*sems1[:4], landed2[0], "chips_wait_early")
    sums = _sum_chips([*pairs1, *pairs2], [*landed1, *landed2], idx, (F32, BF16))
    *shards, g_rep, g_lat = _share(sums[:-2], sums[-2:])
    loss = jnp.sum(all_loss[:, 0, 0])

    red = {n: s.reshape(w[n].shape) for n, s in zip(("w_oa", "w_ob", "w_out", "w_uq"), shards[1:])}
    g_rep = g_rep.reshape(N_CHIPS * REP_ROWS, D_MODEL)
    off = 0
    for n in REPLICATED:
        rows = _rows8(w[n]).shape[0]
        red[n] = g_rep[off:off + rows].reshape(-1)[:w[n].size].reshape(w[n].shape)
        off += rows
    owner = (2 * xi + yi == 0).astype(jnp.int32).reshape(1)
    gt, dt, mt, vt2 = _update_w_in(wt_shard, shards[0], mt_shard, vt_shard,
                                   g_lat.reshape(LAT_ROWS_PAD, D_MODEL).astype(F32), owner, 232)
    red["w_in"] = jnp.transpose(gt)
    small = [n for n in ORDER if n != "w_in"]
    as2d = lambda a: a.reshape(-1, a.shape[-1])
    ds, ms, vs = _update_small([as2d(w[n]) for n in small], [as2d(red[n]) for n in small],
                               [as2d(m[n]) for n in small], [as2d(v[n]) for n in small])
    delta, new_m, new_v = {"w_in": jnp.transpose(dt)}, {"w_in": jnp.transpose(mt)}, {"w_in": jnp.transpose(vt2)}
    for i, n in enumerate(small):
        delta[n], new_m[n], new_v[n] = (a[i].reshape(w[n].shape) for a in (ds, ms, vs))

    lead = lambda a: a[None]
    return (loss, dx[None], *[lead(red[n]) for n in ORDER], *[lead(delta[n]) for n in ORDER],
            *[lead(new_m[n]) for n in ORDER], *[lead(new_v[n]) for n in ORDER])
```

```python
import math

import jax
import jax.numpy as jnp
from jax import lax
from jax.experimental import pallas as pl
from jax.experimental.pallas import tpu as pltpu

F32 = jnp.float32
BF16 = jnp.bfloat16

D_MODEL = 1024
HEADS = 8
Q_RANK = 384
KV_RANK = 128
NOPE = 64
ROPE = 32
V_DIM = 64
QK_DIM = NOPE + ROPE
HEAD_PAD = 128
MLA_W = HEADS * V_DIM
SGU_W = 512
GROUPS = 8
CHUNK = 128
IN_W = 4640
RMS_EPS = 1e-6
LN_EPS = 1e-5
ALPHA = 2.0 ** 0.25
ROPE_THETA = 10000.0
SCALE = QK_DIM ** -0.5

GATE_W = 2 * D_MODEL
MID_W = 4 * SGU_W
LAT_W = Q_RANK + KV_RANK + HEAD_PAD
LAT_COLS = Q_RANK + KV_RANK + ROPE
ROW_GATE = LAT_COLS + MID_W
LAT_ROWS_PAD = 704
N_SLABS = 4
SLAB_W = D_MODEL // N_SLABS

ROW_TILE = 256
MATMUL_ROW_TILE = 512
MID_ROW_TILE = 256
ATT_TK = 256
ATT_BWD_TK = 256
SUM_ROWS = 16
LOG2E = 1.4426950408889634
LN2 = 0.6931471805599453
Q_SCALE = SCALE * LOG2E
VMEM_LIMIT = 56 * 1024 * 1024

ADAM_LR = 0.001
ADAM_B1 = 0.9
ADAM_B2 = 0.999
ADAM_EPS = 1e-08
ADAM_WD = 0.01
ADAM_STEP = 10


def _dot(a, b):
    return jnp.dot(a, b, preferred_element_type=F32)


def _dot_nt(a, b):
    return lax.dot_general(a, b, (((1,), (1,)), ((), ())), preferred_element_type=F32)


def _dot_tn(a, b):
    return lax.dot_general(a, b, (((0,), (0,)), ((), ())), preferred_element_type=F32)


def _sigmoid(z):
    return 0.5 * jnp.tanh(0.5 * z) + 0.5


_GELU_C = math.sqrt(2.0 / math.pi)


def _gelu_and_grad(x):
    x2 = x * x
    t = jnp.tanh(_GELU_C * (x + 0.044715 * x * x2))
    g = 0.5 * x * (1.0 + t)
    dg = 0.5 * (1.0 + t) + 0.5 * x * (1.0 - t * t) * (_GELU_C * (1.0 + 3.0 * 0.044715 * x2))
    return g, dg


def _silu_and_grad(z):
    s = _sigmoid(z)
    return z * s, s * (1.0 + z * (1.0 - s))


def _rope(xb, c, sl, sh):
    return xb * c + pltpu.roll(xb, 112, 1) * sl + pltpu.roll(xb, 16, 1) * sh


def _rope_t(dy, c, sl, sh):
    return dy * c + pltpu.roll(dy * sl, 16, 1) + pltpu.roll(dy * sh, 112, 1)


def _params(sem=("arbitrary",)):
    return pltpu.CompilerParams(dimension_semantics=sem, vmem_limit_bytes=VMEM_LIMIT)


def _row_spec(tile, width):
    return pl.BlockSpec((tile, width), lambda i: (i, 0))


def _full_spec(shape):
    nd = len(shape)
    return pl.BlockSpec(shape, lambda i: (0,) * nd)


def _kpe_rows(wt_ref):
    z = lambda n: jnp.zeros((n, D_MODEL), BF16)
    return jnp.concatenate([z(NOPE), wt_ref[Q_RANK + KV_RANK:LAT_COLS, :], z(HEAD_PAD - QK_DIM)], axis=0)


def _fwd_rest(xb2, wt, b_g, b_m):
    s = xb2.shape[1]
    ts = MATMUL_ROW_TILE
    tn = D_MODEL
    blocks = ([(ROW_GATE + c0, 0, c0) for c0 in range(0, GATE_W, tn)]
              + [(LAT_COLS + c0, 1, c0) for c0 in range(0, MID_W, tn)])

    def body(xb_ref, wt_hbm, bg_ref, bm_ref, hg_ref, hm_ref, wt_ref, sems):
        copies = [pltpu.make_async_copy(wt_hbm.at[lo:lo + tn], wt_ref.at[lo:lo + tn], sems.at[n])
                  for n, (lo, _, _) in enumerate(blocks)]

        def compute(first):
            xb_ = jnp.concatenate([xb_ref[0], xb_ref[1]], axis=1)
            for cp, (lo, which, c0) in zip(copies, blocks):
                if first:
                    cp.wait()
                out_ref, b_ref = ((hg_ref, bg_ref), (hm_ref, bm_ref))[which]
                out_ref[:, c0:c0 + tn] = (_dot_nt(xb_, wt_ref[lo:lo + tn, :]) + b_ref[:, c0:c0 + tn]).astype(BF16)

        @pl.when(pl.program_id(0) == 0)
        def _():
            for cp in copies:
                cp.start()
            compute(True)

        @pl.when(pl.program_id(0) > 0)
        def _():
            compute(False)

    return pl.pallas_call(
        body, name="fwd_rest", grid=(s // ts,),
        in_specs=[pl.BlockSpec((2, ts, D_MODEL // 2), lambda i: (0, i, 0)), HBM_SPEC, _full_spec(b_g.shape),
                  _full_spec(b_m.shape)],
        out_specs=[_row_spec(ts, GATE_W), _row_spec(ts, MID_W)],
        out_shape=[jax.ShapeDtypeStruct((s, GATE_W), BF16), jax.ShapeDtypeStruct((s, MID_W), BF16)],
        scratch_shapes=[pltpu.VMEM(wt.shape, BF16), pltpu.SemaphoreType.DMA((len(blocks),))],
        compiler_params=_params(),
    )(xb2, wt, b_g, b_m)


def _fwd_lat(x, wlat, b_l, g_q, wuq, g_kv, wk, wv, rc, rsl, rsh, after):
    s = x.shape[0]
    ts = ROW_TILE

    def body(x_ref, wt_ref, bl_ref, gq_ref, wuq_ref, gkv_ref, wk_ref, wv_ref, rc_ref, rsl_ref,
             rsh_ref, after_ref, hl_ref, q_ref, k_ref, v_ref, qt_ref, kt_ref, vt_ref, xb2_ref):
        xb = x_ref[...].astype(BF16)
        xb2_ref[0] = xb[:, :D_MODEL // 2]
        xb2_ref[1] = xb[:, D_MODEL // 2:]
        hl = jnp.concatenate([_dot_nt(xb, wt_ref[0:Q_RANK + KV_RANK, :]), _dot_nt(xb, _kpe_rows(wt_ref))],
                             axis=1) + bl_ref[...]
        hl_ref[...] = hl
        c, sl, sh = rc_ref[...], rsl_ref[...], rsh_ref[...]
        cq = hl[:, :Q_RANK]
        cqn = cq * lax.rsqrt(jnp.mean(cq * cq, axis=-1, keepdims=True) + RMS_EPS) * gq_ref[...]
        q = _dot(cqn.astype(BF16), wuq_ref[...])
        ckv = hl[:, Q_RANK:Q_RANK + KV_RANK]
        ckvn = (ckv * lax.rsqrt(jnp.mean(ckv * ckv, axis=-1, keepdims=True) + RMS_EPS) * gkv_ref[...]).astype(BF16)
        k = _dot(ckvn, wk_ref[...])
        vb = _dot(ckvn, wv_ref[...]).astype(BF16)
        v_ref[...] = vb
        vt_ref[...] = vb.T
        kpe = _rope(hl[:, Q_RANK + KV_RANK:], c, sl, sh)
        for hd in range(HEADS):
            lanes = slice(hd * HEAD_PAD, (hd + 1) * HEAD_PAD)
            qb = (_rope(q[:, lanes], c, sl, sh) * Q_SCALE).astype(BF16)
            kb = (k[:, lanes] + kpe).astype(BF16)
            q_ref[:, lanes] = qb
            k_ref[:, lanes] = kb
            qt_ref[lanes, :] = qb.T
            kt_ref[lanes, :] = kb.T

    qk_w = HEADS * HEAD_PAD
    col_spec = lambda rows: pl.BlockSpec((rows, ts), lambda i: (0, i))
    return pl.pallas_call(
        body, name="fwd_lat", grid=(s // ts,),
        in_specs=[_row_spec(ts, D_MODEL), _full_spec(wlat.shape),
                  _full_spec(b_l.shape), _full_spec(g_q.shape),
                  _full_spec(wuq.shape), _full_spec(g_kv.shape), _full_spec(wk.shape), _full_spec(wv.shape),
                  _row_spec(ts, HEAD_PAD), _row_spec(ts, HEAD_PAD), _row_spec(ts, HEAD_PAD),
                  pl.BlockSpec(memory_space=pl.ANY)],
        out_specs=[_row_spec(ts, LAT_W), _row_spec(ts, qk_w),
                   _row_spec(ts, qk_w), _row_spec(ts, MLA_W), col_spec(qk_w), col_spec(qk_w),
                   col_spec(MLA_W), pl.BlockSpec((2, ts, D_MODEL // 2), lambda i: (0, i, 0))],
        out_shape=[jax.ShapeDtypeStruct((s, LAT_W), F32), jax.ShapeDtypeStruct((s, qk_w), BF16),
                   jax.ShapeDtypeStruct((s, qk_w), BF16), jax.ShapeDtypeStruct((s, MLA_W), BF16),
                   jax.ShapeDtypeStruct((qk_w, s), BF16),
                   jax.ShapeDtypeStruct((qk_w, s), BF16), jax.ShapeDtypeStruct((MLA_W, s), BF16),
                   jax.ShapeDtypeStruct((2, s, D_MODEL // 2), BF16)],
        compiler_params=_params(),
    )(x, wlat, b_l, g_q, wuq, g_kv, wk, wv, rc, rsl, rsh, after)


def _attn_fwd(qt, k, vt):
    s = k.shape[0]
    tk = ATT_TK
    nk = s // tk
    pairs = HEADS // 2

    def body(qt_ref, k_ref, vt_ref, o_ref, lse_ref):
        qts = [qt_ref[hh * HEAD_PAD:(hh + 1) * HEAD_PAD, :] for hh in range(2)]
        ones = jnp.ones((SUM_ROWS, tk), BF16)

        def scores(j):
            return tuple(_dot(k_ref[j * tk:(j + 1) * tk, hh * HEAD_PAD:(hh + 1) * HEAD_PAD], qts[hh][:, j * tk:])
                         for hh in range(2))

        def weighted(j, ps):
            return tuple(_dot(jnp.concatenate([vt_ref[hh * V_DIM:(hh + 1) * V_DIM, j * tk:(j + 1) * tk], ones], axis=0),
                              ps[hh]) for hh in range(2))

        def from_lane(full, lo, part):
            return part if lo == 0 else jnp.concatenate([full[:, :lo], part], axis=1)

        krow = lax.broadcasted_iota(jnp.int32, (tk, tk), 0)
        qcol = lax.broadcasted_iota(jnp.int32, (tk, tk), 1)
        st = scores(0)
        ps = None
        stats = [(jnp.full((1, s), -jnp.inf, F32), jnp.zeros((V_DIM + SUM_ROWS, s), F32))] * 2
        for j in range(nk):
            lo, lo_prev = j * tk, max(j - 1, 0) * tk
            st_next = scores(j + 1) if j + 1 < nk else None
            pvs = weighted(j - 1, ps) if j else None
            new_ps, new_stats = [], []
            for hh in range(2):
                m, acc = stats[hh]
                diag = jnp.where(krow <= qcol, st[hh][:, :tk], -jnp.inf)
                s_ = diag if j == nk - 1 else jnp.concatenate([diag, st[hh][:, tk:]], axis=1)
                if j:
                    acc = from_lane(acc, lo_prev, acc[:, lo_prev:] + pvs[hh])
                m_old = m[:, lo:]
                m_new = jnp.maximum(m_old, jnp.max(s_, axis=0, keepdims=True))
                a = jnp.exp2(m_old - m_new)
                p = jnp.exp2(s_ - m_new)
                new_stats.append((from_lane(m, lo, m_new), from_lane(acc, lo, a * acc[:, lo:])))
                new_ps.append(p.astype(BF16))
            st, ps, stats = st_next, new_ps, new_stats
        pvs = weighted(nk - 1, ps)
        lo = (nk - 1) * tk
        accs = [from_lane(stats[hh][1], lo, stats[hh][1][:, lo:] + pvs[hh]) for hh in range(2)]
        sums = [acc[V_DIM:V_DIM + 1, :] for acc in accs]
        ot = jnp.concatenate([accs[hh][:V_DIM, :] / sums[hh] for hh in range(2)], axis=0)
        o_ref[...] = ot.T
        lse = [stats[hh][0] + jnp.log(sums[hh]) * LOG2E for hh in range(2)]
        lse_ref[...] = jnp.concatenate(lse + [jnp.zeros((6, s), F32)], axis=0)

    return pl.pallas_call(
        body, name="attn_fwd", grid=(pairs,),
        in_specs=[pl.BlockSpec((2 * HEAD_PAD, s), lambda p: (p, 0)),
                  pl.BlockSpec((s, 2 * HEAD_PAD), lambda p: (0, p)),
                  pl.BlockSpec((2 * V_DIM, s), lambda p: (p, 0))],
        out_specs=[pl.BlockSpec((s, 2 * V_DIM), lambda p: (0, p)),
                   pl.BlockSpec((None, 8, s), lambda p: (p, 0, 0))],
        out_shape=[jax.ShapeDtypeStruct((s, MLA_W), F32), jax.ShapeDtypeStruct((pairs, 8, s), F32)],
        compiler_params=_params(("arbitrary",)),
    )(qt, k, vt)


def _attn_bwd(q, qt, k, kt, v, do, dot, lse, delta, after):
    s = k.shape[0]
    tk = ATT_BWD_TK
    nk = s // tk
    pairs = HEADS // 2

    def body(q_ref, qt_ref, k_ref, kt_ref, v_ref, do_ref, dot_ref, lse_ref, dl_ref, after_ref, dqt_ref, dk_ref,
             dv_ref):
        krow = lax.broadcasted_iota(jnp.int32, (tk, tk), 0)
        qcol = lax.broadcasted_iota(jnp.int32, (tk, tk), 1)
        lane = lax.broadcasted_iota(jnp.int32, (tk, 2 * V_DIM), 1)
        drow = lax.broadcasted_iota(jnp.int32, (2 * V_DIM, s), 0)
        dotb = dot_ref[...]
        dots = [jnp.where((drow < V_DIM) if hh == 0 else (drow >= V_DIM), dotb, jnp.zeros_like(dotb))
                for hh in range(2)]
        for j in range(nk):
            lo = j * tk
            vb = v_ref[lo:lo + tk, :]
            dob = do_ref[lo:, :]
            dvs = []
            for hh in range(2):
                rows = slice(hh * HEAD_PAD, (hh + 1) * HEAD_PAD)
                st = _dot(k_ref[lo:lo + tk, rows], qt_ref[rows, lo:])
                diag = jnp.where(krow <= qcol, st[:, :tk], -jnp.inf)
                st = diag if j == nk - 1 else jnp.concatenate([diag, st[:, tk:]], axis=1)
                p = jnp.exp2(st - lse_ref[hh:hh + 1, lo:])
                dpt = _dot(vb, dots[hh][:, lo:])
                dst = (p * (dpt - dl_ref[hh:hh + 1, lo:])).astype(BF16)
                dvs.append(_dot(p.astype(BF16), dob))
                dk_ref[lo:lo + tk, rows] = (_dot(dst, q_ref[lo:, rows]) * LN2).astype(BF16)
                dqt = _dot(kt_ref[rows, lo:lo + tk], dst)
                if j == 0:
                    dqt_ref[rows, :] = dqt
                else:
                    dqt_ref[rows, lo:] += dqt
            dv_ref[lo:lo + tk, :] = jnp.where(lane < V_DIM, dvs[0], dvs[1]).astype(BF16)
        dqt_ref[...] = dqt_ref[...] * SCALE

    pair_rows = lambda w: pl.BlockSpec((s, w), lambda p: (0, p))
    pair_cols = lambda w: pl.BlockSpec((w, s), lambda p: (p, 0))
    stats = pl.BlockSpec((None, 8, s), lambda p: (p, 0, 0))
    return pl.pallas_call(
        body, name="attn_bwd", grid=(pairs,),
        in_specs=[pair_rows(2 * HEAD_PAD), pair_cols(2 * HEAD_PAD), pair_rows(2 * HEAD_PAD), pair_cols(2 * HEAD_PAD),
                  pair_rows(2 * V_DIM), pair_rows(2 * V_DIM), pair_cols(2 * V_DIM), stats, stats,
                  pl.BlockSpec(memory_space=pl.ANY)],
        out_specs=[pair_cols(2 * HEAD_PAD), pair_rows(2 * HEAD_PAD), pair_rows(2 * V_DIM)],
        out_shape=[jax.ShapeDtypeStruct((HEADS * HEAD_PAD, s), F32), jax.ShapeDtypeStruct((s, HEADS * HEAD_PAD), BF16),
                   jax.ShapeDtypeStruct((s, MLA_W), BF16)],
        compiler_params=_params(("arbitrary",)),
    )(q, qt, k, kt, v, do, dot, lse, delta, after)


def _split3(a):
    hi = a.astype(BF16)
    r1 = a - hi.astype(F32)
    mid = r1.astype(BF16)
    lo = (r1 - mid.astype(F32)).astype(BF16)
    return hi, mid, lo


def _mid(x, tgt, o, hm, hg, woa, wob, wout, ln_g, ln_b, sg_g, sg_b, w_s, bsb):
    s = x.shape[0]
    ts = MID_ROW_TILE
    nsteps = s // ts
    nch = ts // CHUNK
    npair = GROUPS // 2

    def body(x_ref, t_ref, o_ref, hm_ref, hg_ref, woa_ref, wob_ref, wout_ref, lng_ref, lnb_ref, sgg_ref, sgb_ref,
             ws_ref, bsb_ref,
             dr_ref, dhg_ref, dhm_ref, do_ref, dot_ref, dl_ref, dhgt_ref, dhmt_ref,
             dwout_ref, dwoa_ref, dwob_ref, dws_ref, dbs_ref, dlng_ref, dlnb_ref, dsgg_ref, dsgb_ref, loss_ref,
             dbg_ref, dbm_ref, dbacc_ref, awout_ref, awoa_ref, awob_ref):
        i = pl.program_id(0)

        @pl.when(i == 0)
        def _():
            for r in (awout_ref, awoa_ref, awob_ref, dws_ref, dlng_ref, dlnb_ref, dsgg_ref, dsgb_ref, loss_ref,
                      dbg_ref, dbm_ref, dbacc_ref):
                r[...] = jnp.zeros_like(r)

        def emit(ref, tref, bref, lo, val):
            vb = val.astype(BF16)
            n = val.shape[1]
            ref[:, lo:lo + n] = vb
            tref[lo:lo + n, :] = vb.T
            bref[:, lo:lo + n] += jnp.sum(val, axis=0, keepdims=True)

        lane = lax.broadcasted_iota(jnp.int32, (CHUNK, CHUNK), 1)
        left = lane < V_DIM
        tril = lax.broadcasted_iota(jnp.int32, (CHUNK, CHUNK), 0) >= lane
        ms = [jnp.where(tril, ws_ref[g], 0.0).astype(BF16) for g in range(GROUPS)]

        z_a = hm_ref[:, 0:SGU_W].astype(F32)
        u = hm_ref[:, SGU_W:2 * SGU_W].astype(F32)
        v = hm_ref[:, 2 * SGU_W:3 * SGU_W].astype(F32)
        z_b = hm_ref[:, 3 * SGU_W:4 * SGU_W].astype(F32)
        o = o_ref[...]
        sa, dsa = _silu_and_grad(z_a)
        y_a = (o * sa).astype(BF16)
        gu, dgu = _gelu_and_grad(u)
        gv, dgv = _gelu_and_grad(v)
        mu = jnp.mean(gv, axis=-1, keepdims=True)
        vc = gv - mu
        rstd_v = lax.rsqrt(jnp.mean(vc * vc, axis=-1, keepdims=True) + LN_EPS)
        vhat = vc * rstd_v
        vn = (vhat * sgg_ref[...] + sgb_ref[...]).astype(BF16)
        rows = []
        for c in range(nch):
            blocks = []
            for p in range(npair):
                blk = vn[c * CHUNK:(c + 1) * CHUNK, p * CHUNK:(p + 1) * CHUNK]
                blocks.append(jnp.where(left, _dot(ms[2 * p], blk), _dot(ms[2 * p + 1], blk)))
            rows.append(jnp.concatenate(blocks, axis=1) + bsb_ref[...])
        mixed = jnp.concatenate(rows, axis=0)
        sgu = gu * mixed
        sb, dsb = _silu_and_grad(z_b)
        y_b = (sgu * sb).astype(BF16)
        pa = jnp.concatenate([_dot(y_a, woa_ref[k]) for k in range(N_SLABS)], axis=1)
        pb = jnp.concatenate([_dot(y_b, wob_ref[k]) for k in range(N_SLABS)], axis=1)
        sga = _sigmoid(hg_ref[:, :D_MODEL].astype(F32))
        sgb = _sigmoid(hg_ref[:, D_MODEL:].astype(F32))
        m2 = (sga * pa + sgb * pb).astype(BF16)
        r = ALPHA * x_ref[...] + _dot(m2, wout_ref[...])
        rmu = jnp.mean(r, axis=-1, keepdims=True)
        rc = r - rmu
        rstd = lax.rsqrt(jnp.mean(rc * rc, axis=-1, keepdims=True) + LN_EPS)
        xhat = rc * rstd
        y = xhat * lng_ref[...] + lnb_ref[...]
        err = y - t_ref[...]
        loss_ref[...] += jnp.full(loss_ref.shape, 0.5 / D_MODEL, F32) * jnp.sum(err * err)

        dy = err * (1.0 / D_MODEL)
        dlng_ref[...] += jnp.sum(dy * xhat, axis=0, keepdims=True)
        dlnb_ref[...] += jnp.sum(dy, axis=0, keepdims=True)
        dxh = dy * lng_ref[...]
        dr = rstd * (dxh - jnp.mean(dxh, axis=-1, keepdims=True) - xhat * jnp.mean(dxh * xhat, axis=-1, keepdims=True))
        dr_ref[...] = dr
        drb = dr.astype(BF16)
        awout_ref[...] += _dot_tn(m2, drb)
        dm2 = _dot_nt(drb, wout_ref[...])
        emit(dhg_ref, dhgt_ref, dbg_ref, 0, dm2 * pa * sga * (1.0 - sga))
        emit(dhg_ref, dhgt_ref, dbg_ref, D_MODEL, dm2 * pb * sgb * (1.0 - sgb))
        dpa = (dm2 * sga).astype(BF16)
        dpb = (dm2 * sgb).astype(BF16)
        dy_a = jnp.zeros((ts, MLA_W), F32)
        dy_b = jnp.zeros((ts, SGU_W), F32)
        y_at, y_bt = y_a.T, y_b.T
        for k in range(N_SLABS):
            cols = slice(k * SLAB_W, (k + 1) * SLAB_W)
            awoa_ref[k] += _dot(y_at, dpa[:, cols])
            awob_ref[k] += _dot(y_bt, dpb[:, cols])
            dy_a = dy_a + _dot_nt(dpa[:, cols], woa_ref[k])
            dy_b = dy_b + _dot_nt(dpb[:, cols], wob_ref[k])
        dob = (dy_a * sa).astype(BF16)
        do_ref[...] = dob
        dot_ref[...] = dob.T
        head = (lax.broadcasted_iota(jnp.int32, (HEADS, MLA_W), 1) // V_DIM
                == lax.broadcasted_iota(jnp.int32, (HEADS, MLA_W), 0)).astype(BF16)
        dl = sum(_dot_nt(head, term) for term in _split3(dob.astype(F32) * o))
        for p in range(HEADS // 2):
            dl_ref[p] = jnp.concatenate([dl[2 * p:2 * p + 2], jnp.zeros((6, ts), F32)], axis=0)
        emit(dhm_ref, dhmt_ref, dbm_ref, 0, dy_a * o * dsa)
        dsg = dy_b * sb
        emit(dhm_ref, dhmt_ref, dbm_ref, 3 * SGU_W, dy_b * sgu * dsb)
        emit(dhm_ref, dhmt_ref, dbm_ref, SGU_W, dsg * mixed * dgu)
        dmixed = dsg * gu
        dvn_rows = []
        dbs_sum = jnp.zeros((CHUNK, SGU_W), F32)
        for c in range(nch):
            dm_c = dmixed[c * CHUNK:(c + 1) * CHUNK, :]
            dbs_sum = dbs_sum + dm_c
            blocks = []
            for p in range(npair):
                dmb = dm_c[:, p * CHUNK:(p + 1) * CHUNK].astype(BF16)
                blk = vn[c * CHUNK:(c + 1) * CHUNK, p * CHUNK:(p + 1) * CHUNK]
                blocks.append(jnp.where(left, _dot_tn(ms[2 * p], dmb), _dot_tn(ms[2 * p + 1], dmb)))
                zero = jnp.zeros_like(dmb)
                dws_ref[2 * p] += jnp.where(tril, _dot_nt(jnp.where(left, dmb, zero), blk), 0.0)
                dws_ref[2 * p + 1] += jnp.where(tril, _dot_nt(jnp.where(left, zero, dmb), blk), 0.0)
            dvn_rows.append(jnp.concatenate(blocks, axis=1))
        dbacc_ref[...] += dbs_sum
        dvn = jnp.concatenate(dvn_rows, axis=0)
        dsgg_ref[...] += jnp.sum(dvn * vhat, axis=0, keepdims=True)
        dsgb_ref[...] += jnp.sum(dvn, axis=0, keepdims=True)
        dvh = dvn * sgg_ref[...]
        dgv_in = rstd_v * (dvh - jnp.mean(dvh, axis=-1, keepdims=True)
                           - vhat * jnp.mean(dvh * vhat, axis=-1, keepdims=True))
        emit(dhm_ref, dhmt_ref, dbm_ref, 2 * SGU_W, dgv_in * dgv)

        @pl.when(i == nsteps - 1)
        def _():
            dwout_ref[...] = awout_ref[...].astype(BF16)
            dwoa_ref[...] = awoa_ref[...].astype(BF16)
            dwob_ref[...] = awob_ref[...].astype(BF16)
            grp = (lax.broadcasted_iota(jnp.int32, (SGU_W, CHUNK), 0) // V_DIM
                   == lax.broadcasted_iota(jnp.int32, (SGU_W, CHUNK), 1)).astype(BF16)
            hi, mid, lo = _split3(dbacc_ref[...])
            dbs_ref[...] = _dot(hi, grp) + _dot(mid, grp) + _dot(lo, grp)

    acc_shapes = [(D_MODEL, D_MODEL), woa.shape, wob.shape, (GROUPS, CHUNK, CHUNK), (CHUNK, CHUNK),
                  (1, D_MODEL), (1, D_MODEL), (1, SGU_W), (1, SGU_W), (1, 128), (1, GATE_W), (1, MID_W)]
    col_spec = lambda rows: pl.BlockSpec((rows, ts), lambda i: (0, i))
    return pl.pallas_call(
        body, name="mid", grid=(nsteps,),
        in_specs=[_row_spec(ts, D_MODEL), _row_spec(ts, D_MODEL), _row_spec(ts, MLA_W), _row_spec(ts, MID_W),
                  _row_spec(ts, GATE_W), _full_spec(woa.shape), _full_spec(wob.shape), _full_spec(wout.shape),
                  _full_spec(ln_g.shape), _full_spec(ln_b.shape), _full_spec(sg_g.shape), _full_spec(sg_b.shape),
                  _full_spec(w_s.shape), _full_spec(bsb.shape)],
        out_specs=[_row_spec(ts, D_MODEL), _row_spec(ts, GATE_W), _row_spec(ts, MID_W), _row_spec(ts, MLA_W),
                   col_spec(MLA_W), pl.BlockSpec((HEADS // 2, 8, ts), lambda i: (0, 0, i)), col_spec(GATE_W),
                   col_spec(MID_W)]
        + [_full_spec(sh) for sh in acc_shapes],
        out_shape=[jax.ShapeDtypeStruct((s, D_MODEL), F32), jax.ShapeDtypeStruct((s, GATE_W), BF16),
                   jax.ShapeDtypeStruct((s, MID_W), BF16), jax.ShapeDtypeStruct((s, MLA_W), BF16),
                   jax.ShapeDtypeStruct((MLA_W, s), BF16), jax.ShapeDtypeStruct((HEADS // 2, 8, s), F32),
                   jax.ShapeDtypeStruct((GATE_W, s), BF16), jax.ShapeDtypeStruct((MID_W, s), BF16)]
        + [jax.ShapeDtypeStruct(sh, BF16 if n < 3 else F32) for n, sh in enumerate(acc_shapes)],
        scratch_shapes=[pltpu.VMEM((CHUNK, SGU_W), F32)] + [pltpu.VMEM(sh, F32) for sh in acc_shapes[:3]],
        compiler_params=_params(),
    )(x, tgt, o, hm, hg, woa, wob, wout, ln_g, ln_b, sg_g, sg_b, w_s, bsb)


def _lat_bwd(dq, dk, dv, hl, rc, rsl, rsh, g_q, g_kv, wuq, wk, wv, after):
    s = dk.shape[0]
    ts = ROW_TILE
    qk_w = HEADS * HEAD_PAD

    def body(dq_ref, dk_ref, dv_ref, hl_ref, rc_ref, rsl_ref, rsh_ref, gq_ref, gkv_ref, wuq_ref, wk_ref, wv_ref,
             after_ref, dhl_ref, dhlt_ref, dwuq_ref, dwk_ref, dwv_ref, dgq_ref, dgkv_ref, dbl_ref):
        i = pl.program_id(0)

        @pl.when(i == 0)
        def _():
            for r in (dwuq_ref, dwk_ref, dwv_ref, dgq_ref, dgkv_ref, dbl_ref):
                r[...] = jnp.zeros_like(r)

        def emit(lo, val):
            vb = val.astype(BF16)
            n = val.shape[1]
            dhl_ref[:, lo:lo + n] = vb
            dhlt_ref[lo:lo + n, :] = vb.T
            dbl_ref[:, lo:lo + n] += jnp.sum(val, axis=0, keepdims=True)

        c, sl, sh = rc_ref[...], rsl_ref[...], rsh_ref[...]
        lane = lax.broadcasted_iota(jnp.int32, (ts, HEAD_PAD), 1)
        pe = (lane >= NOPE) & (lane < QK_DIM)
        dkpe = jnp.zeros((ts, HEAD_PAD), F32)
        dqu = []
        for hd in range(HEADS):
            lanes = slice(hd * HEAD_PAD, (hd + 1) * HEAD_PAD)
            dqu.append(_rope_t(dq_ref[lanes, :].T, c, sl, sh).astype(BF16))
            dkpe = dkpe + dk_ref[:, lanes]
        dqu = jnp.concatenate(dqu, axis=1)
        dkpe = _rope_t(jnp.where(pe, dkpe, 0.0), c, sl, sh)

        cq = hl_ref[:, :Q_RANK]
        rq = lax.rsqrt(jnp.mean(cq * cq, axis=-1, keepdims=True) + RMS_EPS)
        cqh = cq * rq
        cqn = (cqh * gq_ref[...]).astype(BF16)
        dwuq_ref[...] += _dot_tn(cqn, dqu)
        dcqn = _dot_nt(dqu, wuq_ref[...])
        dgq_ref[...] += jnp.sum(dcqn * cqh, axis=0, keepdims=True)
        dch = dcqn * gq_ref[...]
        emit(0, rq * (dch - cqh * jnp.mean(dch * cqh, axis=-1, keepdims=True)))

        ckv = hl_ref[:, Q_RANK:Q_RANK + KV_RANK]
        rk = lax.rsqrt(jnp.mean(ckv * ckv, axis=-1, keepdims=True) + RMS_EPS)
        ckh = ckv * rk
        ckn = (ckh * gkv_ref[...]).astype(BF16)
        dkb = dk_ref[...].astype(BF16)
        dvb = dv_ref[...].astype(BF16)
        dwk_ref[...] += _dot_tn(ckn, dkb)
        dwv_ref[...] += _dot_tn(ckn, dvb)
        dckn = _dot_nt(dkb, wk_ref[...]) + _dot_nt(dvb, wv_ref[...])
        dgkv_ref[...] += jnp.sum(dckn * ckh, axis=0, keepdims=True)
        dkh = dckn * gkv_ref[...]
        emit(Q_RANK, rk * (dkh - ckh * jnp.mean(dkh * ckh, axis=-1, keepdims=True)))
        emit(Q_RANK + KV_RANK, dkpe)

    acc_shapes = [wuq.shape, wk.shape, wv.shape, g_q.shape, g_kv.shape, (1, LAT_W)]
    return pl.pallas_call(
        body, name="lat_bwd", grid=(s // ts,),
        in_specs=[pl.BlockSpec((qk_w, ts), lambda i: (0, i)), _row_spec(ts, qk_w), _row_spec(ts, MLA_W),
                  _row_spec(ts, LAT_W), _row_spec(ts, HEAD_PAD), _row_spec(ts, HEAD_PAD), _row_spec(ts, HEAD_PAD),
                  _full_spec(g_q.shape), _full_spec(g_kv.shape), _full_spec(wuq.shape), _full_spec(wk.shape),
                  _full_spec(wv.shape), pl.BlockSpec(memory_space=pl.ANY)],
        out_specs=[_row_spec(ts, LAT_W), pl.BlockSpec((LAT_W, ts), lambda i: (0, i))]
        + [_full_spec(sh) for sh in acc_shapes],
        out_shape=[jax.ShapeDtypeStruct((s, LAT_W), BF16), jax.ShapeDtypeStruct((LAT_W, s), BF16)]
        + [jax.ShapeDtypeStruct(sh, F32) for sh in acc_shapes],
        compiler_params=_params(),
    )(dq, dk, dv, hl, rc, rsl, rsh, g_q, g_kv, wuq, wk, wv, after)


def _dx(dr, dhg, dhm, dhl, wt, wlat, after):
    s = dr.shape[0]
    ts = MATMUL_ROW_TILE

    tk = D_MODEL
    bounds = ([(ROW_GATE + r0, ROW_GATE + r0 + tk) for r0 in range(0, GATE_W, tk)]
              + [(LAT_COLS + r0, LAT_COLS + r0 + tk) for r0 in range(0, MID_W, tk)])

    def body(dr_ref, dhg_ref, dhm_ref, dhl_ref, wt_hbm, wlat_ref, after_ref, dx_ref, wt_ref, sems):
        copies = [pltpu.make_async_copy(wt_hbm.at[lo:hi], wt_ref.at[lo:hi], sems.at[n])
                  for n, (lo, hi) in enumerate(bounds)]

        def compute(first):
            acc = (ALPHA * dr_ref[...] + _dot(dhl_ref[:, 0:Q_RANK + KV_RANK], wlat_ref[0:Q_RANK + KV_RANK, :])
                   + _dot(dhl_ref[:, Q_RANK + KV_RANK:], _kpe_rows(wlat_ref)))
            for n, (lo, hi) in enumerate(bounds):
                if first:
                    copies[n].wait()
                if lo >= ROW_GATE:
                    acc += _dot(dhg_ref[:, lo - ROW_GATE:hi - ROW_GATE], wt_ref[lo:hi, :])
                else:
                    acc += _dot(dhm_ref[:, lo - LAT_COLS:hi - LAT_COLS], wt_ref[lo:hi, :])
            dx_ref[...] = acc

        @pl.when(pl.program_id(0) == 0)
        def _():
            for cp in copies:
                cp.start()
            compute(True)

        @pl.when(pl.program_id(0) > 0)
        def _():
            compute(False)

    return pl.pallas_call(
        body, name="dx", grid=(s // ts,),
        in_specs=[_row_spec(ts, D_MODEL), _row_spec(ts, GATE_W), _row_spec(ts, MID_W), _row_spec(ts, LAT_W),
                  HBM_SPEC, _full_spec(wlat.shape), HBM_SPEC],
        out_specs=_row_spec(ts, D_MODEL),
        out_shape=jax.ShapeDtypeStruct((s, D_MODEL), F32),
        scratch_shapes=[pltpu.VMEM(wt.shape, BF16), pltpu.SemaphoreType.DMA((len(bounds),))],
        compiler_params=_params(),
    )(dr, dhg, dhm, dhl, wt, wlat, after)


def _dwt_early(dhmt, dhgt, xb, col, after, name):
    tn = 512
    nm, ng = MID_W // tn, GATE_W // tn
    s = dhmt.shape[1]
    hc = D_MODEL // 2

    ks = s // 2

    def body(col_ref, dma_ref, dmb_ref, dga_ref, dgb_ref, xba_ref, xbb_ref, after_ref, dw_ref):
        i = pl.program_id(0)

        @pl.when(i < nm)
        def _():
            dw_ref[...] = (_dot(dma_ref[...], xba_ref[...]) + _dot(dmb_ref[...], xbb_ref[...])).astype(BF16)

        @pl.when(i >= nm)
        def _():
            dw_ref[...] = (_dot(dga_ref[...], xba_ref[...]) + _dot(dgb_ref[...], xbb_ref[...])).astype(BF16)

    def dh_spec(first, part):
        if first:
            return pl.BlockSpec((tn, ks), lambda i, col_ref: (jnp.minimum(i, nm - 1), part))
        return pl.BlockSpec((tn, ks), lambda i, col_ref: (jnp.maximum(i - nm, 0), part))

    rows = pl.pallas_call(
        body, name=name,
        grid_spec=pltpu.PrefetchScalarGridSpec(
            num_scalar_prefetch=1, grid=(nm + ng,),
            in_specs=[dh_spec(True, 0), dh_spec(True, 1), dh_spec(False, 0), dh_spec(False, 1),
                      pl.BlockSpec((None, ks, hc), lambda i, col_ref: (col_ref[0], 0, 0)),
                      pl.BlockSpec((None, ks, hc), lambda i, col_ref: (col_ref[0], 1, 0)),
                      pl.BlockSpec(memory_space=pl.ANY)],
            out_specs=pl.BlockSpec((pl.Element(tn), pl.Element(hc)),
                                   lambda i, col_ref: (pl.multiple_of(LAT_COLS + i * tn, 32), 0))),
        out_shape=jax.ShapeDtypeStruct((IN_W, hc), BF16),
        compiler_params=_params(),
    )(col, dhmt, dhmt, dhgt, dhgt, xb, xb, after)

    def zero(buf_ref, out_ref):
        out_ref[...] = jnp.zeros_like(out_ref)

    return pl.pallas_call(
        zero, name=name + "_zero_lat", grid=(1,), in_specs=[pl.BlockSpec(memory_space=pl.ANY)],
        out_specs=pl.BlockSpec((LAT_COLS, hc), lambda i: (0, 0)),
        out_shape=jax.ShapeDtypeStruct((IN_W, hc), BF16), input_output_aliases={0: 0},
    )(rows)


def _dwt_lat(dhlt, xb2):
    n, s = dhlt.shape
    tk = MATMUL_ROW_TILE
    nsteps = s // tk

    def body(dht_ref, xb_ref, dw_ref, acc_ref):
        i = pl.program_id(0)

        @pl.when(i == 0)
        def _():
            acc_ref[...] = jnp.zeros_like(acc_ref)

        dht = dht_ref[...]
        acc_ref[...] += jnp.concatenate([_dot(dht, xb_ref[0]), _dot(dht, xb_ref[1])], axis=1)

        @pl.when(i == nsteps - 1)
        def _():
            kpe = Q_RANK + KV_RANK + NOPE
            dw_ref[0:Q_RANK + KV_RANK, :] = acc_ref[0:Q_RANK + KV_RANK, :].astype(BF16)
            dw_ref[Q_RANK + KV_RANK:LAT_COLS, :] = acc_ref[kpe:kpe + ROPE, :].astype(BF16)
            dw_ref[LAT_COLS:, :] = jnp.zeros((LAT_ROWS_PAD - LAT_COLS, D_MODEL), BF16)

    return pl.pallas_call(
        body, name="dwt_lat", grid=(nsteps,),
        in_specs=[pl.BlockSpec((n, tk), lambda i: (0, i)), pl.BlockSpec((2, tk, D_MODEL // 2), lambda i: (0, i, 0))],
        out_specs=_full_spec((LAT_ROWS_PAD, D_MODEL)),
        out_shape=jax.ShapeDtypeStruct((LAT_ROWS_PAD, D_MODEL), BF16),
        scratch_shapes=[pltpu.VMEM((n, D_MODEL), F32)],
        compiler_params=_params(),
    )(dhlt, xb2)


def _split_bias(b):
    z = lambda n: jnp.zeros((n,), b.dtype)
    lat = jnp.concatenate([b[:Q_RANK + KV_RANK], z(NOPE), b[Q_RANK + KV_RANK:LAT_COLS], z(HEAD_PAD - QK_DIM)])
    return b[None, ROW_GATE:], b[None, LAT_COLS:ROW_GATE], lat[None, :]


def _join_bias(g, m, l):
    kpe = Q_RANK + KV_RANK + NOPE
    return jnp.concatenate([l[0, :Q_RANK + KV_RANK], l[0, kpe:kpe + ROPE], m[0], g[0]])


def _rope_tables(positions):
    half = ROPE // 2
    inv_freq = ROPE_THETA ** (-jnp.arange(0, ROPE, 2, dtype=F32) / ROPE)
    ang = positions.astype(F32)[:, None] * inv_freq
    cos, sin = jnp.cos(ang), jnp.sin(ang)
    n = positions.shape[0]
    one, zero = jnp.ones((n, NOPE), F32), jnp.zeros((n, half), F32)
    tail1, tail0 = jnp.ones((n, HEAD_PAD - QK_DIM), F32), jnp.zeros((n, HEAD_PAD - QK_DIM), F32)
    z64 = jnp.zeros((n, NOPE), F32)
    rc = jnp.concatenate([one, cos, cos, tail1], axis=1)
    rsl = jnp.concatenate([z64, -sin, zero, tail0], axis=1)
    rsh = jnp.concatenate([z64, zero, sin, tail0], axis=1)
    return rc, rsl, rsh


def _pad_heads(w_uq):
    return jnp.pad(w_uq, ((0, 0), (0, 0), (0, HEAD_PAD - QK_DIM))).reshape(w_uq.shape[0], HEADS * HEAD_PAD)


def _prep_local(b_in, g_q, g_kv, w_ukv):
    b_g, b_m, b_l = _split_bias(b_in)
    wk = jnp.pad(w_ukv[:, :, :NOPE], ((0, 0), (0, 0), (0, HEAD_PAD - NOPE))).reshape(KV_RANK, HEADS * HEAD_PAD).astype(BF16)
    wv = w_ukv[:, :, NOPE:].reshape(KV_RANK, MLA_W).astype(BF16)
    return dict(b_g=b_g, b_m=b_m, b_l=b_l, wk=wk, wv=wv, gq2=g_q[None, :], gkv2=g_kv[None, :])


def _local_attention(x, tables, wlat, prep, wuq, after):
    rc, rsl, rsh = tables
    b_g, b_m, b_l, wk, wv, gq2, gkv2 = (prep[n] for n in ("b_g", "b_m", "b_l", "wk", "wv", "gq2", "gkv2"))
    hl, q, k, v, qt, kt, vt, xb2 = _fwd_lat(x, wlat, b_l, gq2, wuq, gkv2, wk, wv, rc, rsl, rsh, after)
    o, lse = _attn_fwd(qt, k, vt)
    return dict(q=q, qt=qt, k=k, kt=kt, v=v, o=o, lse=lse, hl=hl, rc=rc, rsl=rsl, rsh=rsh, gq2=gq2, gkv2=gkv2,
                wuq=wuq, wk=wk, wv=wv, xb2=xb2, b_g=b_g, b_m=b_m, wlat=wlat)


def _bias_lanes(b_s):
    return jnp.repeat(b_s.T, V_DIM, axis=1)


def _local_head(st, x, tgt, wt, w_oa, sg_g, sg_b, w_s, bsb, w_ob, w_out, ln_g, ln_b):
    q, qt, k, kt, v, o, lse, hl = (st[n] for n in ("q", "qt", "k", "kt", "v", "o", "lse", "hl"))
    rc, rsl, rsh, gq2, gkv2, wuq, wk, wv = (st[n] for n in ("rc", "rsl", "rsh", "gq2", "gkv2", "wuq", "wk", "wv"))
    hg, hm = _fwd_rest(st["xb2"], wt, st["b_g"], st["b_m"])
    (dr, dhg, dhm, do, dot, delta, dhgt, dhmt, dwout, dwoa, dwob, dws, dbs, dlng, dlnb, dsgg, dsgb, loss, dbg,
     dbm) = _mid(x, tgt, o, hm, hg, w_oa, w_ob, w_out, ln_g[None, :], ln_b[None, :], sg_g[None, :], sg_b[None, :],
                 w_s, bsb)
    early = {
        "w_oa": dwoa, "sgu_ln_g": dsgg[0], "sgu_ln_b": dsgb[0], "w_s": dws, "b_s": dbs[:, :GROUPS].T,
        "w_ob": dwob, "w_out": dwout, "ln_g": dlng[0], "ln_b": dlnb[0],
    }
    state = dict(q=q, qt=qt, k=k, kt=kt, v=v, do=do, dot=dot, lse=lse, delta=delta, hl=hl, rc=rc, rsl=rsl, rsh=rsh,
                 gq2=gq2, gkv2=gkv2, wuq=wuq, wk=wk, wv=wv, dr=dr, dhg=dhg, dhm=dhm, wt=wt, dbg=dbg, dbm=dbm,
                 dhgt=dhgt, dhmt=dhmt, xb2=st["xb2"], wlat=st["wlat"])
    return loss, early, state


def _local_attn_bwd(st, after):
    return _attn_bwd(st["q"], st["qt"], st["k"], st["kt"], st["v"], st["do"], st["dot"], st["lse"], st["delta"],
                     after)


def _local_tail(st, dq, dk, dv, after):
    dhl, dhlt, dwuq, dwk, dwv, dgq, dgkv, dbl = _lat_bwd(dq, dk, dv, st["hl"], st["rc"], st["rsl"], st["rsh"],
                                                         st["gq2"], st["gkv2"], st["wuq"], st["wk"], st["wv"], after)
    late = {
        "w_lat": _dwt_lat(dhlt, st["xb2"]),
        "b_in": _join_bias(st["dbg"], st["dbm"], dbl),
        "g_q": dgq[0],
        "w_uq": dwuq.reshape(Q_RANK, HEADS, HEAD_PAD)[:, :, :QK_DIM],
        "g_kv": dgkv[0],
        "w_ukv": jnp.concatenate([dwk.reshape(KV_RANK, HEADS, HEAD_PAD)[:, :, :NOPE],
                                  dwv.reshape(KV_RANK, HEADS, V_DIM)], axis=2),
    }
    return dhl, late


def _local_step(x, positions, tgt, wt, b_in, g_q, w_uq, g_kv, w_ukv, w_oa, sg_g, sg_b, w_s, b_s, w_ob, w_out, ln_g,
                ln_b):
    st = _local_attention(x, _rope_tables(positions), wt[:LAT_COLS], _prep_local(b_in, g_q, g_kv, w_ukv),
                          _pad_heads(w_uq).astype(BF16), b_in)
    loss, early, st = _local_head(st, x, tgt, wt, w_oa, sg_g, sg_b, w_s, _bias_lanes(b_s), w_ob, w_out, ln_g, ln_b)
    dq, dk, dv = _local_attn_bwd(st, loss)
    dhl, late = _local_tail(st, dq, dk, dv, dv)
    dx = _dx(st["dr"], st["dhg"], st["dhm"], dhl, st["wt"], st["wlat"], dhl)
    grads = {**early, **late}
    halves = [_dwt_early(st["dhmt"], st["dhgt"], st["xb2"], jnp.full((1,), h, jnp.int32), dhl, "dwt_half%d" % h)
              for h in range(2)]
    grads["w_in"] = jnp.concatenate([grads.pop("w_lat")[:LAT_COLS], jnp.concatenate(halves, axis=1)[LAT_COLS:]],
                                    axis=0)
    return loss, dx, grads


MESH = pl.DeviceIdType.MESH
N_CHIPS = 4
HBM_SPEC = pl.BlockSpec(memory_space=pl.ANY)
HBM_SPEC_STRICT = pl.BlockSpec(memory_space=pltpu.HBM)
VMEM_SPEC = pl.BlockSpec(memory_space=pltpu.VMEM)

REP_ROWS = 80


def _rows8(a):
    flat = a.reshape(-1)
    n = -(-flat.shape[0] // (8 * D_MODEL)) * 8 * D_MODEL
    return jnp.pad(flat, (0, n - flat.shape[0])).reshape(-1, D_MODEL)


def _place():
    x, y, c = lax.axis_index("x"), lax.axis_index("y"), lax.axis_index("c")
    others = [(1 - x, y), (x, 1 - y), (1 - x, 1 - y)]
    return x, y, c, others


N_DEV = 8
LOSS_TILE = (8, 128)


def _cast_own(shards, me, after):
    n = len(shards)

    def body(me_ref, *refs):
        for w in range(n):
            refs[n + 1 + w][...] = refs[w][...].astype(BF16)

    return pl.pallas_call(
        body, name="cast_own",
        grid_spec=pltpu.PrefetchScalarGridSpec(
            num_scalar_prefetch=1, grid=(1,),
            in_specs=[pl.BlockSpec(s.shape, lambda i, me_ref: (0, 0)) for s in shards]
            + [pl.BlockSpec(memory_space=pl.ANY)],
            out_specs=[pl.BlockSpec((None,) + s.shape, lambda i, me_ref: (me_ref[0], 0, 0)) for s in shards]),
        out_shape=[jax.ShapeDtypeStruct((N_CHIPS,) + s.shape, BF16) for s in shards],
        compiler_params=pltpu.CompilerParams(vmem_limit_bytes=VMEM_LIMIT),
    )(me, *shards, after)


def _cast_first(lat, uq, me):
    def body(me_ref, lat_ref, uq_ref, wlat_ref, guq_ref):
        wlat_ref[...] = lat_ref[...].astype(BF16)
        guq_ref[...] = uq_ref[...].astype(BF16)

    return pl.pallas_call(
        body, name="cast_first",
        grid_spec=pltpu.PrefetchScalarGridSpec(
            num_scalar_prefetch=1, grid=(1,),
            in_specs=[pl.BlockSpec((LAT_COLS, D_MODEL), lambda i, me_ref: (0, 0)),
                      pl.BlockSpec(uq.shape, lambda i, me_ref: (0, 0))],
            out_specs=[pl.BlockSpec((LAT_COLS, D_MODEL), lambda i, me_ref: (0, 0)),
                       pl.BlockSpec((None,) + uq.shape, lambda i, me_ref: (me_ref[0], 0, 0))]),
        out_shape=[jax.ShapeDtypeStruct((LAT_COLS, D_MODEL), BF16),
                   jax.ShapeDtypeStruct((N_CHIPS,) + uq.shape, BF16)],
    )(me, lat, uq)


def _first_copies(wlat_ref, guq_ref, send_sems, recv_sems, shapes):
    x, y, c, others = _place()
    me = 2 * x + y
    hl, hu = shapes[0][1] // 2, shapes[1][2] // 2
    lat_half = wlat_ref.at[:, pl.ds(c * hl, hl)]

    def copy(src, dst, k, to):
        return pltpu.make_async_remote_copy(src_ref=src, dst_ref=dst, send_sem=send_sems.at[k],
                                            recv_sem=recv_sems.at[k], device_id=to, device_id_type=MESH)

    def uq_half(chip):
        return guq_ref.at[chip, :, pl.ds(c * hu, hu)]

    lat_out = [copy(lat_half, lat_half, j, (*others[j], c)) for j in range(3)]
    uq_out = [copy(uq_half(me), uq_half(me), 3 + j, (*others[j], c)) for j in range(3)]
    j0 = jnp.maximum(x + 2 * y - 1, 0)
    lat_in = copy(lat_half, lat_half, j0, (0, 0, c))
    uq_in = [copy(uq_half(me), uq_half(2 * px + py), 3 + j, (px, py, c)) for j, (px, py) in enumerate(others)]
    return me, lat_out, uq_out, lat_in, uq_in


def _first_start(wlat, guq):
    shapes = (wlat.shape, guq.shape)

    def body(wlat_ref, guq_ref, send_sems, recv_sems, wlat_thru, guq_thru, token):
        me, lat_out, uq_out, _, _ = _first_copies(wlat_ref, guq_ref, send_sems, recv_sems, shapes)

        @pl.when(me == 0)
        def _():
            for cp in lat_out:
                cp.start()

        for cp in uq_out:
            cp.start()
        token[...] = jnp.zeros_like(token)

    outs = pl.pallas_call(
        body, name="first_start",
        out_shape=(pltpu.SemaphoreType.DMA((6,)), pltpu.SemaphoreType.DMA((6,)), pltpu.HBM(wlat.shape, BF16),
                   pltpu.HBM(guq.shape, BF16), jax.ShapeDtypeStruct(LOSS_TILE, F32)),
        in_specs=[HBM_SPEC_STRICT] * 2, out_specs=(SEM_SPEC, SEM_SPEC, HBM_SPEC_STRICT, HBM_SPEC_STRICT, VMEM_SPEC),
        input_output_aliases={0: 2, 1: 3},
        compiler_params=pltpu.CompilerParams(has_side_effects=SPLIT_EFFECT),
    )(pltpu.with_memory_space_constraint(wlat, pltpu.HBM), pltpu.with_memory_space_constraint(guq, pltpu.HBM))
    return outs


def _first_wait(send_sems, recv_sems, wlat, guq, *after):
    shapes = (wlat.shape, guq.shape)

    def body(wlat_ref, guq_ref, send_sems, recv_sems, *rest):
        me, lat_out, uq_out, lat_in, uq_in = _first_copies(wlat_ref, guq_ref, send_sems, recv_sems, shapes)

        @pl.when(me == 0)
        def _():
            for cp in lat_out:
                cp.wait_send()

        @pl.when(me != 0)
        def _():
            lat_in.wait_recv()

        for cp in uq_out:
            cp.wait_send()
        for cp in uq_in:
            cp.wait_recv()

    return pl.pallas_call(
        body, name="first_wait", out_shape=(pltpu.HBM(wlat.shape, BF16), pltpu.HBM(guq.shape, BF16)),
        in_specs=[HBM_SPEC_STRICT, HBM_SPEC_STRICT, SEM_SPEC, SEM_SPEC] + [HBM_SPEC] * len(after),
        out_specs=(HBM_SPEC_STRICT, HBM_SPEC_STRICT), input_output_aliases={0: 0, 1: 1},
        compiler_params=pltpu.CompilerParams(has_side_effects=SPLIT_EFFECT),
    )(wlat, guq, send_sems, recv_sems, *after)


def _first_forward(wlat, guq):
    hl, hu = wlat.shape[1] // 2, guq.shape[2] // 2

    def body(wlat_in, guq_in, wlat_ref, guq_ref, send_sems, recv_sems):
        x, y, c, others = _place()
        me = 2 * x + y
        sibling = (x, y, 1 - c)

        def copy(part, k):
            return pltpu.make_async_remote_copy(src_ref=part, dst_ref=part, send_sem=send_sems.at[k],
                                                recv_sem=recv_sems.at[k], device_id=sibling, device_id_type=MESH)

        def uq_part(j, half):
            px, py = others[j]
            return guq_ref.at[2 * px + py, :, pl.ds(half * hu, hu)]

        cps = [copy(uq_part(j, c), j) for j in range(3)]
        for cp in cps:
            cp.start()

        @pl.when(me != 0)
        def _():
            mine = copy(wlat_ref.at[:, pl.ds(c * hl, hl)], 3)
            mine.start()
            copy(wlat_ref.at[:, pl.ds((1 - c) * hl, hl)], 3).wait_recv()
            mine.wait_send()

        for j in range(3):
            copy(uq_part(j, 1 - c), j).wait_recv()
        for cp in cps:
            cp.wait_send()

    return pl.pallas_call(
        body, name="first_forward", in_specs=[HBM_SPEC, HBM_SPEC], out_specs=[HBM_SPEC, HBM_SPEC],
        out_shape=[jax.ShapeDtypeStruct(wlat.shape, BF16), jax.ShapeDtypeStruct(guq.shape, BF16)],
        input_output_aliases={0: 0, 1: 1},
        scratch_shapes=[pltpu.SemaphoreType.DMA((4,)), pltpu.SemaphoreType.DMA((4,))],
    )(wlat, guq)


def _gather_copy(b_ref, buf, w, j, src_chip, dst_chip, to, rows, send_sems, recv_sems):
    _, _, c, _ = _place()
    hc = _half(buf)
    return pltpu.make_async_remote_copy(
        src_ref=b_ref.at[src_chip, rows, pl.ds(c * hc, hc)], dst_ref=b_ref.at[dst_chip, rows, pl.ds(c * hc, hc)],
        send_sem=send_sems.at[3 * w + j], recv_sem=recv_sems.at[3 * w + j], device_id=to, device_id_type=MESH)


def _gather_rows(buf, w, chip, fn):
    if w != 0:
        fn(slice(None))
        return
    pl.when(chip == 0)(lambda: fn(pl.ds(LAT_COLS, buf.shape[1] - LAT_COLS)))
    pl.when(chip != 0)(lambda: fn(slice(None)))


def _gather_start(bufs, after):
    n = len(bufs)

    def body(*refs):
        b_refs = refs[:n]
        send_sems, recv_sems, token = refs[n + 1], refs[n + 2], refs[-1]
        x, y, c, others = _place()
        me = 2 * x + y
        for w in range(n):
            def start(rows, w=w):
                for j, (px, py) in enumerate(others):
                    _gather_copy(b_refs[w], bufs[w], w, j, me, me, (px, py, c), rows, send_sems, recv_sems).start()
            _gather_rows(bufs[w], w, me, start)
        token[...] = jnp.zeros_like(token)

    hbm = [pltpu.HBM(b.shape, BF16) for b in bufs]
    outs = pl.pallas_call(
        body, name="gather_start",
        out_shape=(pltpu.SemaphoreType.DMA((3 * n,)), pltpu.SemaphoreType.DMA((3 * n,)), *hbm,
                   jax.ShapeDtypeStruct(LOSS_TILE, F32)),
        in_specs=[HBM_SPEC_STRICT] * n + [HBM_SPEC],
        out_specs=(SEM_SPEC, SEM_SPEC, *[HBM_SPEC_STRICT] * n, VMEM_SPEC),
        input_output_aliases={i: 2 + i for i in range(n)},
        compiler_params=pltpu.CompilerParams(has_side_effects=SPLIT_EFFECT),
    )(*[pltpu.with_memory_space_constraint(b, pltpu.HBM) for b in bufs], after)
    return outs[0], outs[1], list(outs[2:2 + n]), outs[-1]


def _gather_wait(send_sems, recv_sems, bufs, after):
    n = len(bufs)

    def body(*refs):
        b_refs = refs[:n]
        send_sems, recv_sems = refs[n], refs[n + 1]
        x, y, c, others = _place()
        me = 2 * x + y
        for w in range(n):
            for j, (px, py) in enumerate(others):
                def copy(rows, w=w, j=j, px=px, py=py):
                    return _gather_copy(b_refs[w], bufs[w], w, j, me, 2 * px + py, (px, py, c), rows, send_sems,
                                        recv_sems)
                _gather_rows(bufs[w], w, me, lambda rows, copy=copy: copy(rows).wait_send())
                _gather_rows(bufs[w], w, 2 * px + py, lambda rows, copy=copy: copy(rows).wait_recv())

    outs = pl.pallas_call(
        body, name="gather_wait", out_shape=tuple(pltpu.HBM(b.shape, b.dtype) for b in bufs),
        in_specs=[HBM_SPEC_STRICT] * n + [SEM_SPEC, SEM_SPEC, HBM_SPEC],
        out_specs=tuple([HBM_SPEC_STRICT] * n), input_output_aliases={i: i for i in range(n)},
        compiler_params=pltpu.CompilerParams(has_side_effects=SPLIT_EFFECT),
    )(*bufs, send_sems, recv_sems, after)
    return list(outs)


def _gather_finish(bufs):
    n = len(bufs)

    def body(*refs):
        b_refs = refs[n:2 * n]
        send_sems, recv_sems = refs[2 * n:]
        x, y, c, others = _place()
        cps = []
        for w in range(n):
            hc = _half(bufs[w])
            for j, (px, py) in enumerate(others):
                part = b_refs[w].at[2 * px + py, :, pl.ds(c * hc, hc)]
                cps.append(pltpu.make_async_remote_copy(
                    src_ref=part, dst_ref=part, send_sem=send_sems.at[3 * w + j], recv_sem=recv_sems.at[3 * w + j],
                    device_id=(x, y, 1 - c), device_id_type=MESH))
        for cp in cps:
            cp.start()
        for w in range(n):
            hc = _half(bufs[w])
            for j, (px, py) in enumerate(others):
                theirs = b_refs[w].at[2 * px + py, :, pl.ds((1 - c) * hc, hc)]
                pltpu.make_async_remote_copy(
                    src_ref=theirs, dst_ref=theirs, send_sem=send_sems.at[3 * w + j], recv_sem=recv_sems.at[3 * w + j],
                    device_id=(x, y, 1 - c), device_id_type=MESH).wait_recv()
        for cp in cps:
            cp.wait_send()

    return pl.pallas_call(
        body, name="gather_finish", in_specs=[HBM_SPEC] * n, out_specs=[HBM_SPEC] * n,
        out_shape=[jax.ShapeDtypeStruct(b.shape, b.dtype) for b in bufs],
        input_output_aliases={i: i for i in range(n)},
        scratch_shapes=[pltpu.SemaphoreType.DMA((3 * n,)), pltpu.SemaphoreType.DMA((3 * n,))],
    )(*bufs)


def _half(a):
    return a.shape[-1] // 2


def _exchange_pairs(parts, name):
    n = len(parts)

    def body(*refs):
        p_refs, r_refs = refs[:n], refs[n:2 * n]
        send_sems, recv_sems = refs[2 * n:]
        x, y, c, _ = _place()
        cps = []
        for w in range(n):
            h = _half(parts[w])
            cps.append(pltpu.make_async_remote_copy(
                src_ref=p_refs[w].at[:, :, pl.ds((1 - c) * h, h)], dst_ref=r_refs[w],
                send_sem=send_sems.at[w], recv_sem=recv_sems.at[w], device_id=(x, y, 1 - c), device_id_type=MESH))
        for cp in cps:
            cp.start()
        for cp in cps:
            cp.wait()

    return pl.pallas_call(
        body, name=name, in_specs=[HBM_SPEC] * n, out_specs=[HBM_SPEC] * n,
        out_shape=[jax.ShapeDtypeStruct((N_CHIPS, p.shape[1], _half(p)), BF16) for p in parts],
        scratch_shapes=[pltpu.SemaphoreType.DMA((n,)), pltpu.SemaphoreType.DMA((n,))],
    )(*parts)


def _sibling_part(ref, w, n_whole, shape, c):
    if w < n_whole:
        return ref
    h = shape[-1] // 2
    return ref.at[:, :, pl.ds((1 - c) * h, h)]


def _pairs_start(parts, all_loss, n_whole):
    n = len(parts)

    def body(*refs):
        p_refs, r_refs, loss_ref = refs[:n], refs[n:2 * n], refs[2 * n]
        send_sems, recv_sems, token = refs[2 * n + 1], refs[2 * n + 2], refs[-1]
        x, y, c, _ = _place()
        for w in range(n):
            h = _half(parts[w])
            pltpu.make_async_remote_copy(
                src_ref=_sibling_part(p_refs[w], w, n_whole, parts[w].shape, c), dst_ref=r_refs[w],
                send_sem=send_sems.at[w], recv_sem=recv_sems.at[w], device_id=(x, y, 1 - c),
                device_id_type=MESH).start()
        me = 4 * x + 2 * y + c
        for t in range(1, N_DEV):
            d = (me + t) % N_DEV
            pltpu.make_async_remote_copy(
                src_ref=loss_ref.at[me], dst_ref=loss_ref.at[me], send_sem=send_sems.at[n + t - 1],
                recv_sem=recv_sems.at[n + t - 1], device_id=(d // 4, (d // 2) % 2, d % 2), device_id_type=MESH).start()
        token[...] = jnp.zeros_like(token)

    lands = [pltpu.HBM(p.shape if w < n_whole else (N_CHIPS, p.shape[1], _half(p)), BF16)
             for w, p in enumerate(parts)]
    nsem = n + N_DEV - 1
    outs = pl.pallas_call(
        body, name="pairs_start",
        out_shape=(pltpu.SemaphoreType.DMA((nsem,)), pltpu.SemaphoreType.DMA((nsem,)),
                   *[pltpu.HBM(p.shape, p.dtype) for p in parts], *lands, pltpu.HBM(all_loss.shape, F32),
                   jax.ShapeDtypeStruct(LOSS_TILE, F32)),
        in_specs=[HBM_SPEC_STRICT] * (2 * n + 1),
        out_specs=(SEM_SPEC, SEM_SPEC, *[HBM_SPEC_STRICT] * (2 * n + 1), VMEM_SPEC),
        input_output_aliases={i: 2 + i for i in range(2 * n + 1)},
        compiler_params=pltpu.CompilerParams(has_side_effects=SPLIT_EFFECT),
    )(*[pltpu.with_memory_space_constraint(p, pltpu.HBM) for p in parts],
      *[pltpu.with_memory_space_constraint(lax.empty(l.shape, BF16), pltpu.HBM) for l in lands],
      pltpu.with_memory_space_constraint(all_loss, pltpu.HBM))
    return outs[0], outs[1], list(outs[2:2 + n]), list(outs[2 + n:2 + 2 * n]), outs[2 + 2 * n], outs[-1]


def _pairs_wait(send_sems, recv_sems, parts, lands, all_loss, after, n_whole):
    n = len(parts)

    def body(*refs):
        p_refs, r_refs, loss_ref = refs[:n], refs[n:2 * n], refs[2 * n]
        send_sems, recv_sems = refs[2 * n + 1], refs[2 * n + 2]
        x, y, c, _ = _place()
        for w in range(n):
            h = _half(parts[w])
            cp = pltpu.make_async_remote_copy(
                src_ref=_sibling_part(p_refs[w], w, n_whole, parts[w].shape, c), dst_ref=r_refs[w],
                send_sem=send_sems.at[w],
                recv_sem=recv_sems.at[w], device_id=(x, y, 1 - c), device_id_type=MESH)
            cp.wait_send()
            cp.wait_recv()
        me = 4 * x + 2 * y + c
        for t in range(1, N_DEV):
            d = (me + N_DEV - t) % N_DEV
            cp = pltpu.make_async_remote_copy(
                src_ref=loss_ref.at[me], dst_ref=loss_ref.at[d], send_sem=send_sems.at[n + t - 1],
                recv_sem=recv_sems.at[n + t - 1], device_id=(d // 4, (d // 2) % 2, d % 2), device_id_type=MESH)
            cp.wait_send()
            cp.wait_recv()

    bufs = (*parts, *lands, all_loss)
    outs = pl.pallas_call(
        body, name="pairs_wait", out_shape=tuple(pltpu.HBM(a.shape, a.dtype) for a in bufs),
        in_specs=[HBM_SPEC_STRICT] * len(bufs) + [SEM_SPEC, SEM_SPEC, HBM_SPEC],
        out_specs=tuple([HBM_SPEC_STRICT] * len(bufs)), input_output_aliases={i: i for i in range(len(bufs))},
        compiler_params=pltpu.CompilerParams(has_side_effects=SPLIT_EFFECT),
    )(*bufs, send_sems, recv_sems, after)
    return list(outs[:n]), list(outs[n:2 * n]), outs[2 * n]


def _add_pair(ps, rs, c, name, n_whole=0):
    n = len(ps)

    def body(c_ref, *refs):
        for w in range(n):
            refs[2 * n + w][...] = (refs[w][...].astype(F32) + refs[n + w][...].astype(F32)).astype(BF16)

    def slab_spec(a):
        return pl.BlockSpec((None,) + a.shape[1:], lambda k, c_ref: (k, 0, 0))

    def my_half_spec(a):
        return pl.BlockSpec((None, a.shape[1], _half(a)), lambda k, c_ref: (k, 0, c_ref[0]))

    return pl.pallas_call(
        body, name=name,
        grid_spec=pltpu.PrefetchScalarGridSpec(
            num_scalar_prefetch=1, grid=(N_CHIPS,),
            in_specs=[slab_spec(p) if w < n_whole else my_half_spec(p) for w, p in enumerate(ps)]
            + [slab_spec(r) for r in rs],
            out_specs=[slab_spec(r) for r in rs]),
        out_shape=[jax.ShapeDtypeStruct(r.shape, BF16) for r in rs],
        compiler_params=_params(),
    )(c, *ps, *rs)


SEM_SPEC = pl.BlockSpec(memory_space=pltpu.SEMAPHORE)
SPLIT_EFFECT = pltpu.SideEffectType.DATAFLOW_SIDE_EFFECTING


def _chips_start(qs, name):
    n = len(qs)

    def body(*refs):
        q_refs, land_refs = refs[:n], refs[n:2 * n]
        send_sems, recv_sems, token = refs[2 * n], refs[2 * n + 1], refs[-1]
        x, y, c, others = _place()
        me = 2 * x + y
        for w in range(n):
            for j, (px, py) in enumerate(others):
                pltpu.make_async_remote_copy(
                    src_ref=q_refs[w].at[2 * px + py], dst_ref=land_refs[w].at[me], send_sem=send_sems.at[3 * w + j],
                    recv_sem=recv_sems.at[3 * w + j], device_id=(px, py, c), device_id_type=MESH).start()
        token[...] = jnp.zeros_like(token)

    hbm = [pltpu.HBM(q.shape, BF16) for q in qs]
    outs = pl.pallas_call(
        body, name=name,
        out_shape=(pltpu.SemaphoreType.DMA((3 * n,)), pltpu.SemaphoreType.DMA((3 * n,)), *hbm, *hbm,
                   jax.ShapeDtypeStruct(LOSS_TILE, F32)),
        in_specs=[HBM_SPEC_STRICT] * (2 * n),
        out_specs=(SEM_SPEC, SEM_SPEC, *[HBM_SPEC_STRICT] * (2 * n), VMEM_SPEC),
        input_output_aliases={i: 2 + i for i in range(2 * n)},
        compiler_params=pltpu.CompilerParams(has_side_effects=SPLIT_EFFECT),
    )(*[pltpu.with_memory_space_constraint(q, pltpu.HBM) for q in qs],
      *[pltpu.with_memory_space_constraint(lax.empty(q.shape, BF16), pltpu.HBM) for q in qs])
    return outs[0], outs[1], outs[2:2 + n], outs[2 + n:2 + 2 * n], outs[-1]


def _chips_wait(send_sems, recv_sems, q_thru, land_thru, after, name):
    n = len(q_thru)

    def body(*refs):
        q_refs, land_refs = refs[:n], refs[n:2 * n]
        send_sems, recv_sems = refs[2 * n], refs[2 * n + 1]
        x, y, c, others = _place()
        me = 2 * x + y
        for w in range(n):
            for j, (px, py) in enumerate(others):
                cp = pltpu.make_async_remote_copy(
                    src_ref=q_refs[w].at[2 * px + py], dst_ref=land_refs[w].at[2 * px + py],
                    send_sem=send_sems.at[3 * w + j], recv_sem=recv_sems.at[3 * w + j], device_id=(px, py, c),
                    device_id_type=MESH)
                cp.wait_send()
                cp.wait_recv()

    outs = pl.pallas_call(
        body, name=name, out_shape=tuple(pltpu.HBM(a.shape, a.dtype) for a in (*q_thru, *land_thru)),
        in_specs=[HBM_SPEC_STRICT] * (2 * n) + [SEM_SPEC, SEM_SPEC, HBM_SPEC],
        out_specs=tuple([HBM_SPEC_STRICT] * (2 * n)), input_output_aliases={i: i for i in range(2 * n)},
        compiler_params=pltpu.CompilerParams(has_side_effects=SPLIT_EFFECT),
    )(*q_thru, *land_thru, send_sems, recv_sems, after)
    return list(outs[:n]), list(outs[n:])


def _sum_chips(qs, rs, idx, all_dtypes):
    n = len(rs)
    n_all = len(all_dtypes)

    def body(idx_ref, *refs):
        c = idx_ref[4]
        for w in range(n):
            q_ref, g_ref = refs[w], refs[4 * n + w]
            acc = q_ref[...].astype(F32)
            for t in range(1, N_CHIPS):
                acc = acc + refs[n + 3 * w + t - 1][...].astype(F32)
            h = rs[w].shape[2]
            mine = pl.ds(pl.multiple_of(c * h, 128), h)
            g_ref[...] = jnp.zeros_like(g_ref)
            if w >= n - n_all:
                g_ref[idx_ref[0], :, mine] = acc.astype(g_ref.dtype)
            else:
                g_ref[:, mine] = acc

    shapes = [jax.ShapeDtypeStruct((r.shape[1], 2 * r.shape[2]), F32) for r in rs[:n - n_all]]
    shapes += [jax.ShapeDtypeStruct((N_CHIPS, r.shape[1], 2 * r.shape[2]), dt)
               for r, dt in zip(rs[n - n_all:], all_dtypes)]

    def slab_spec(a, t):
        return pl.BlockSpec((None,) + a.shape[1:], lambda i, idx_ref: (idx_ref[t], 0, 0))

    def whole_spec(sh):
        return pl.BlockSpec(sh.shape, lambda i, idx_ref: (0,) * len(sh.shape))

    return pl.pallas_call(
        body, name="sum_chips",
        grid_spec=pltpu.PrefetchScalarGridSpec(
            num_scalar_prefetch=1, grid=(1,),
            in_specs=[slab_spec(q, 0) for q in qs] + [slab_spec(r, t) for r in rs for t in range(1, N_CHIPS)],
            out_specs=[whole_spec(sh) for sh in shapes]),
        out_shape=shapes, compiler_params=_params(),
    )(idx, *qs, *[r for r in rs for _ in range(1, N_CHIPS)])


def _share(shards, alls):
    n, na = len(shards), len(alls)
    total = n + na

    def body(*refs):
        g_refs, a_refs = refs[total:total + n], refs[total + n:2 * total]
        send_sems, recv_sems = refs[2 * total:]
        x, y, c, others = _place()
        me = 2 * x + y
        sibling = (x, y, 1 - c)

        def cols_of(w, half):
            h = shards[w].shape[1] // 2
            return g_refs[w].at[:, pl.ds(half * h, h)]

        def slab(a, chip, half):
            h = alls[a].shape[2] // 2
            return a_refs[a].at[chip, :, pl.ds(half * h, h)]

        def copy(src, dst, k, to):
            return pltpu.make_async_remote_copy(src_ref=src, dst_ref=dst, send_sem=send_sems.at[k],
                                                recv_sem=recv_sems.at[k], device_id=to, device_id_type=MESH)

        cps = []
        for a in range(na):
            base = n + 7 * a
            for j, (px, py) in reversed(list(enumerate(others))):
                cps.append(copy(slab(a, me, c), slab(a, me, c), base + 1 + j, (px, py, c)))
            cps.append(copy(slab(a, me, c), slab(a, me, c), base, sibling))
        cps += [copy(cols_of(w, c), cols_of(w, c), w, sibling) for w in range(n)]
        for cp in cps:
            cp.start()
        fwd = []
        for a in range(na):
            base = n + 7 * a
            for j, (px, py) in enumerate(others):
                chip = 2 * px + py
                copy(slab(a, me, c), slab(a, chip, c), base + 1 + j, (px, py, c)).wait_recv()
                cp = copy(slab(a, chip, c), slab(a, chip, c), base + 4 + j, sibling)
                cp.start()
                fwd.append(cp)
        for a in range(na):
            base = n + 7 * a
            for j, (px, py) in enumerate(others):
                chip = 2 * px + py
                copy(slab(a, chip, c), slab(a, chip, 1 - c), base + 4 + j, sibling).wait_recv()
            copy(slab(a, me, c), slab(a, me, 1 - c), base, sibling).wait_recv()
        for w in range(n):
            copy(cols_of(w, c), cols_of(w, 1 - c), w, sibling).wait_recv()
        for cp in cps + fwd:
            cp.wait_send()

    nsem = n + 7 * na
    return pl.pallas_call(
        body, name="share", in_specs=[HBM_SPEC] * total, out_specs=[HBM_SPEC] * total,
        out_shape=[jax.ShapeDtypeStruct(a.shape, a.dtype) for a in (*shards, *alls)],
        input_output_aliases={i: i for i in range(total)},
        scratch_shapes=[pltpu.SemaphoreType.DMA((nsem,)), pltpu.SemaphoreType.DMA((nsem,))],
    )(*shards, *alls)


def _adamw(w, g, m, v):
    m2 = ADAM_B1 * m + (1.0 - ADAM_B1) * g
    v2 = ADAM_B2 * v + (1.0 - ADAM_B2) * (g * g)
    m_hat = m2 / (1.0 - ADAM_B1 ** ADAM_STEP)
    v_hat = v2 / (1.0 - ADAM_B2 ** ADAM_STEP)
    return -ADAM_LR * (m_hat / (jnp.sqrt(v_hat) + ADAM_EPS) + ADAM_WD * w), m2, v2


def _update_w_in(wt, gt, mt, vt, lat, owner, tile):
    nlat = lat.shape[0] // tile

    def body(owner_ref, w_ref, g_ref, m_ref, v_ref, lat_ref, g2_ref, d_ref, m2_ref, v2_ref):
        row = pl.program_id(0) * tile + lax.broadcasted_iota(jnp.int32, (tile, 1), 0)
        g = jnp.where((row < LAT_COLS) & (owner_ref[0] == 1), lat_ref[...].astype(F32), g_ref[...])
        g2_ref[...] = g
        d_ref[...], m2_ref[...], v2_ref[...] = _adamw(w_ref[...], g, m_ref[...], v_ref[...])

    spec = pl.BlockSpec((tile, wt.shape[1]), lambda i, o: (i, 0))
    return pl.pallas_call(
        body, name="update_w_in",
        grid_spec=pltpu.PrefetchScalarGridSpec(
            num_scalar_prefetch=1, grid=(wt.shape[0] // tile,),
            in_specs=[spec] * 4 + [pl.BlockSpec((tile, wt.shape[1]), lambda i, o: (jnp.minimum(i, nlat - 1), 0))],
            out_specs=[spec] * 4),
        out_shape=[jax.ShapeDtypeStruct(wt.shape, F32)] * 4,
        compiler_params=_params(("parallel",)),
    )(owner, wt, gt, mt, vt, lat)


def _update_small(ws, gs, ms, vs):
    n = len(ws)

    def body(*refs):
        for k in range(n):
            w_ref, g_ref, m_ref, v_ref = refs[k], refs[n + k], refs[2 * n + k], refs[3 * n + k]
            d, m2, v2 = _adamw(w_ref[...], g_ref[...], m_ref[...], v_ref[...])
            refs[4 * n + k][...] = d
            refs[5 * n + k][...] = m2
            refs[6 * n + k][...] = v2

    shapes = [jax.ShapeDtypeStruct(w.shape, F32) for w in ws]
    outs = pl.pallas_call(
        body, name="update_small", in_specs=[VMEM_SPEC] * (4 * n), out_specs=[VMEM_SPEC] * (3 * n),
        out_shape=shapes * 3,
        compiler_params=pltpu.CompilerParams(vmem_limit_bytes=VMEM_LIMIT),
    )(*ws, *gs, *ms, *vs)
    return outs[:n], outs[n:2 * n], outs[2 * n:]


REPLICATED = ("b_in", "g_q", "g_kv", "w_ukv", "sgu_ln_g", "sgu_ln_b", "w_s", "b_s", "ln_g", "ln_b")
ORDER = ("w_in", "b_in", "g_q", "w_uq", "g_kv", "w_ukv", "w_oa", "sgu_ln_g", "sgu_ln_b", "w_s", "b_s", "w_ob", "w_out",
         "ln_g", "ln_b")


def kernel(x, positions, w_in, b_in, g_q, w_uq, g_kv, w_ukv, w_oa, sgu_ln_g, sgu_ln_b, w_s, b_s, w_ob, w_out, ln_g, ln_b, loss_target, m_w_in, m_b_in, m_g_q, m_w_uq, m_g_kv, m_w_ukv, m_w_oa, m_sgu_ln_g, m_sgu_ln_b, m_w_s, m_b_s, m_w_ob, m_w_out, m_ln_g, m_ln_b, v_w_in, v_b_in, v_g_q, v_w_uq, v_g_kv, v_w_ukv, v_w_oa, v_sgu_ln_g, v_sgu_ln_b, v_w_s, v_b_s, v_w_ob, v_w_out, v_ln_g, v_ln_b):
    w = dict(w_in=w_in, b_in=b_in, g_q=g_q, w_uq=w_uq, g_kv=g_kv, w_ukv=w_ukv, w_oa=w_oa, sgu_ln_g=sgu_ln_g,
             sgu_ln_b=sgu_ln_b, w_s=w_s, b_s=b_s, w_ob=w_ob, w_out=w_out, ln_g=ln_g, ln_b=ln_b)
    m = dict(w_in=m_w_in, b_in=m_b_in, g_q=m_g_q, w_uq=m_w_uq, g_kv=m_g_kv, w_ukv=m_w_ukv, w_oa=m_w_oa,
             sgu_ln_g=m_sgu_ln_g, sgu_ln_b=m_sgu_ln_b, w_s=m_w_s, b_s=m_b_s, w_ob=m_w_ob, w_out=m_w_out, ln_g=m_ln_g,
             ln_b=m_ln_b)
    v = dict(w_in=v_w_in, b_in=v_b_in, g_q=v_g_q, w_uq=v_w_uq, g_kv=v_g_kv, w_ukv=v_w_ukv, w_oa=v_w_oa,
             sgu_ln_g=v_sgu_ln_g, sgu_ln_b=v_sgu_ln_b, w_s=v_w_s, b_s=v_b_s, w_ob=v_w_ob, w_out=v_w_out, ln_g=v_ln_g,
             ln_b=v_ln_b)
    w, m, v = ({n: a[0] for n, a in d.items()} for d in (w, m, v))
    c = lax.axis_index("c")

    wt_shard, mt_shard, vt_shard = (jnp.transpose(d["w_in"]) for d in (w, m, v))
    xi, yi = lax.axis_index("x"), lax.axis_index("y")
    me1 = (2 * xi + yi).reshape(1).astype(jnp.int32)
    first = _first_start(*_cast_first(wt_shard, _pad_heads(w["w_uq"]), me1))
    tables = _rope_tables(positions[0])
    prep = _prep_local(w["b_in"], w["g_q"], w["g_kv"], w["w_ukv"])
    bsb = _bias_lanes(w["b_s"])
    c1 = c.reshape(1).astype(jnp.int32)
    idx = jnp.stack([2 * xi + yi, 2 * (1 - xi) + yi, 2 * xi + (1 - yi), 2 * (1 - xi) + (1 - yi), c]).astype(jnp.int32)
    bufs = _cast_own([wt_shard, w["w_oa"], w["w_ob"], w["w_out"]], me1, first[4])
    send0, recv0, bufs, token0 = _gather_start(bufs, first[4])
    g_lat, g_uq = _first_forward(*_first_wait(*first[:4], token0, *tables, *prep.values(), bsb, idx))
    st = _local_attention(x[0], tables, g_lat, prep, g_uq.reshape(Q_RANK, HEADS * HEAD_PAD), token0)
    g_in, g_oa, g_ob, g_out = _gather_finish(_gather_wait(send0, recv0, bufs, st["o"]))
    wt = g_in.reshape(IN_W, D_MODEL)

    loss, early, st = _local_head(
        st, x[0], loss_target[0], wt, g_oa, w["sgu_ln_g"], w["sgu_ln_b"], w["w_s"], bsb, g_ob,
        g_out.reshape(D_MODEL, D_MODEL), w["ln_g"], w["ln_b"])

    slabs = lambda a: a.reshape(N_CHIPS, IN_W // N_CHIPS, D_MODEL // 2)
    theirs = slabs(_dwt_early(st["dhmt"], st["dhgt"], st["xb2"], 1 - c1, c1, "dwt_theirs"))
    parts1 = [theirs, early["w_oa"].astype(BF16), early["w_ob"].astype(BF16),
              early["w_out"].reshape(N_CHIPS, SLAB_W, D_MODEL).astype(BF16)]
    my_loss = lax.dynamic_update_slice(jnp.zeros((N_DEV,) + LOSS_TILE, F32), jnp.broadcast_to(loss, (1,) + LOSS_TILE),
                                       (4 * xi + 2 * yi + c, 0, 0))
    sems0 = _pairs_start(parts1, my_loss, 1)
    mine = slabs(_dwt_early(st["dhmt"], st["dhgt"], st["xb2"], c1, sems0[5], "dwt_mine"))
    parts1, recv1, all_loss = _pairs_wait(*sems0[:5], mine, 1)
    pairs1 = _add_pair([mine, *parts1[1:]], recv1, c1, "add_pair_early", n_whole=1)
    sems1 = _chips_start(pairs1, "chips_start_early")
    dq, dk, dv = _local_attn_bwd(st, sems1[4])
    dhl, late = _local_tail(st, dq, dk, dv, dv)

    grads = {**early, **late}
    rep = jnp.concatenate([_rows8(grads[n]) for n in REPLICATED], axis=0)
    rep = jnp.pad(rep, ((0, N_CHIPS * REP_ROWS - rep.shape[0]), (0, 0))).reshape(N_CHIPS, REP_ROWS, D_MODEL)
    parts2 = [late["w_uq"].reshape(N_CHIPS, Q_RANK // N_CHIPS, HEADS * QK_DIM).astype(BF16), rep.astype(BF16),
              late["w_lat"].reshape(N_CHIPS, LAT_ROWS_PAD // N_CHIPS, D_MODEL)]
    pairs2 = _add_pair(parts2, _exchange_pairs(parts2, "exchange_pairs_late"), c1, "add_pair_late")
    sems2 = _chips_start(pairs2, "chips_start_late")
    dx = _dx(st["dr"], st["dhg"], st["dhm"], dhl, st["wt"], st["wlat"], sems2[4])
    pairs2, landed2 = _chips_wait(*sems2[:4], dx, "chips_wait_late")
    pairs1, landed1 = _chips_wait(*sems1[:4], landed2[0], "chips_wait_early")
    sums = _sum_chips([*pairs1, *pairs2], [*landed1, *landed2], idx, (F32, BF16))
    *shards, g_rep, g_lat = _share(sums[:-2], sums[-2:])
    loss = jnp.sum(all_loss[:, 0, 0])

    red = {n: s.reshape(w[n].shape) for n, s in zip(("w_oa", "w_ob", "w_out", "w_uq"), shards[1:])}
    g_rep = g_rep.reshape(N_CHIPS * REP_ROWS, D_MODEL)
    off = 0
    for n in REPLICATED:
        rows = _rows8(w[n]).shape[0]
        red[n] = g_rep[off:off + rows].reshape(-1)[:w[n].size].reshape(w[n].shape)
        off += rows
    owner = (2 * xi + yi == 0).astype(jnp.int32).reshape(1)
    gt, dt, mt, vt2 = _update_w_in(wt_shard, shards[0], mt_shard, vt_shard,
                                   g_lat.reshape(LAT_ROWS_PAD, D_MODEL).astype(F32), owner, 232)
    red["w_in"] = jnp.transpose(gt)
    small = [n for n in ORDER if n != "w_in"]
    as2d = lambda a: a.reshape(-1, a.shape[-1])
    ds, ms, vs = _update_small([as2d(w[n]) for n in small], [as2d(red[n]) for n in small],
                               [as2d(m[n]) for n in small], [as2d(v[n]) for n in small])
    delta, new_m, new_v = {"w_in": jnp.transpose(dt)}, {"w_in": jnp.transpose(mt)}, {"w_in": jnp.transpose(vt2)}
    for i, n in enumerate(small):
        delta[n], new_m[n], new_v[n] = (a[i].reshape(w[n].shape) for a in (ds, ms, vs))

    lead = lambda a: a[None]
    return (loss, dx[None], *[lead(red[n]) for n in ORDER], *[lead(delta[n]) for n in ORDER],
            *[lead(new_m[n]) for n in ORDER], *[lead(new_v[n]) for n in ORDER])
```

```python
import math

import jax
import jax.numpy as jnp
from jax import lax
from jax.experimental import pallas as pl
from jax.experimental.pallas import tpu as pltpu

F32 = jnp.float32
BF16 = jnp.bfloat16

D_MODEL = 1024
HEADS = 8
Q_RANK = 384
KV_RANK = 128
NOPE = 64
ROPE = 32
V_DIM = 64
QK_DIM = NOPE + ROPE
HEAD_PAD = 128
MLA_W = HEADS * V_DIM
SGU_W = 512
GROUPS = 8
CHUNK = 128
IN_W = 4640
RMS_EPS = 1e-6
LN_EPS = 1e-5
ALPHA = 2.0 ** 0.25
ROPE_THETA = 10000.0
SCALE = QK_DIM ** -0.5

GATE_W = 2 * D_MODEL
MID_W = 4 * SGU_W
LAT_W = Q_RANK + KV_RANK + HEAD_PAD
LAT_COLS = Q_RANK + KV_RANK + ROPE
ROW_GATE = LAT_COLS + MID_W
LAT_ROWS_PAD = 704
N_SLABS = 4
SLAB_W = D_MODEL // N_SLABS

ROW_TILE = 256
MATMUL_ROW_TILE = 512
MID_ROW_TILE = 256
ATT_TK = 256
ATT_BWD_TK = 256
SUM_ROWS = 16
LOG2E = 1.4426950408889634
LN2 = 0.6931471805599453
Q_SCALE = SCALE * LOG2E
VMEM_LIMIT = 56 * 1024 * 1024

ADAM_LR = 0.001
ADAM_B1 = 0.9
ADAM_B2 = 0.999
ADAM_EPS = 1e-08
ADAM_WD = 0.01
ADAM_STEP = 10


def _dot(a, b):
    return jnp.dot(a, b, preferred_element_type=F32)


def _dot_nt(a, b):
    return lax.dot_general(a, b, (((1,), (1,)), ((), ())), preferred_element_type=F32)


def _dot_tn(a, b):
    return lax.dot_general(a, b, (((0,), (0,)), ((), ())), preferred_element_type=F32)


def _sigmoid(z):
    return 0.5 * jnp.tanh(0.5 * z) + 0.5


_GELU_C = math.sqrt(2.0 / math.pi)


def _gelu_and_grad(x):
    x2 = x * x
    t = jnp.tanh(_GELU_C * (x + 0.044715 * x * x2))
    g = 0.5 * x * (1.0 + t)
    dg = 0.5 * (1.0 + t) + 0.5 * x * (1.0 - t * t) * (_GELU_C * (1.0 + 3.0 * 0.044715 * x2))
    return g, dg


def _silu_and_grad(z):
    s = _sigmoid(z)
    return z * s, s * (1.0 + z * (1.0 - s))


def _rope(xb, c, sl, sh):
    return xb * c + pltpu.roll(xb, 112, 1) * sl + pltpu.roll(xb, 16, 1) * sh


def _rope_t(dy, c, sl, sh):
    return dy * c + pltpu.roll(dy * sl, 16, 1) + pltpu.roll(dy * sh, 112, 1)


def _params(sem=("arbitrary",)):
    return pltpu.CompilerParams(dimension_semantics=sem, vmem_limit_bytes=VMEM_LIMIT)


def _row_spec(tile, width):
    return pl.BlockSpec((tile, width), lambda i: (i, 0))


def _full_spec(shape):
    nd = len(shape)
    return pl.BlockSpec(shape, lambda i: (0,) * nd)


def _kpe_rows(wt_ref):
    z = lambda n: jnp.zeros((n, D_MODEL), BF16)
    return jnp.concatenate([z(NOPE), wt_ref[Q_RANK + KV_RANK:LAT_COLS, :], z(HEAD_PAD - QK_DIM)], axis=0)


def _fwd_rest(xb2, wt, b_g, b_m):
    s = xb2.shape[1]
    ts = MATMUL_ROW_TILE
    tn = D_MODEL
    blocks = ([(ROW_GATE + c0, 0, c0) for c0 in range(0, GATE_W, tn)]
              + [(LAT_COLS + c0, 1, c0) for c0 in range(0, MID_W, tn)])

    def body(xb_ref, wt_hbm, bg_ref, bm_ref, hg_ref, hm_ref, wt_ref, sems):
        copies = [pltpu.make_async_copy(wt_hbm.at[lo:lo + tn], wt_ref.at[lo:lo + tn], sems.at[n])
                  for n, (lo, _, _) in enumerate(blocks)]

        def compute(first):
            xb_ = jnp.concatenate([xb_ref[0], xb_ref[1]], axis=1)
            for cp, (lo, which, c0) in zip(copies, blocks):
                if first:
                    cp.wait()
                out_ref, b_ref = ((hg_ref, bg_ref), (hm_ref, bm_ref))[which]
                out_ref[:, c0:c0 + tn] = (_dot_nt(xb_, wt_ref[lo:lo + tn, :]) + b_ref[:, c0:c0 + tn]).astype(BF16)

        @pl.when(pl.program_id(0) == 0)
        def _():
            for cp in copies:
                cp.start()
            compute(True)

        @pl.when(pl.program_id(0) > 0)
        def _():
            compute(False)

    return pl.pallas_call(
        body, name="fwd_rest", grid=(s // ts,),
        in_specs=[pl.BlockSpec((2, ts, D_MODEL // 2), lambda i: (0, i, 0)), HBM_SPEC, _full_spec(b_g.shape),
                  _full_spec(b_m.shape)],
        out_specs=[_row_spec(ts, GATE_W), _row_spec(ts, MID_W)],
        out_shape=[jax.ShapeDtypeStruct((s, GATE_W), BF16), jax.ShapeDtypeStruct((s, MID_W), BF16)],
        scratch_shapes=[pltpu.VMEM(wt.shape, BF16), pltpu.SemaphoreType.DMA((len(blocks),))],
        compiler_params=_params(),
    )(xb2, wt, b_g, b_m)


def _fwd_lat(x, wlat, b_l, g_q, wuq, g_kv, wk, wv, rc, rsl, rsh, after):
    s = x.shape[0]
    ts = ROW_TILE

    def body(x_ref, wt_ref, bl_ref, gq_ref, wuq_ref, gkv_ref, wk_ref, wv_ref, rc_ref, rsl_ref,
             rsh_ref, after_ref, hl_ref, q_ref, k_ref, v_ref, qt_ref, kt_ref, vt_ref, xb2_ref):
        xb = x_ref[...].astype(BF16)
        xb2_ref[0] = xb[:, :D_MODEL // 2]
        xb2_ref[1] = xb[:, D_MODEL // 2:]
        hl = jnp.concatenate([_dot_nt(xb, wt_ref[0:Q_RANK + KV_RANK, :]), _dot_nt(xb, _kpe_rows(wt_ref))],
                             axis=1) + bl_ref[...]
        hl_ref[...] = hl
        c, sl, sh = rc_ref[...], rsl_ref[...], rsh_ref[...]
        cq = hl[:, :Q_RANK]
        cqn = cq * lax.rsqrt(jnp.mean(cq * cq, axis=-1, keepdims=True) + RMS_EPS) * gq_ref[...]
        q = _dot(cqn.astype(BF16), wuq_ref[...])
        ckv = hl[:, Q_RANK:Q_RANK + KV_RANK]
        ckvn = (ckv * lax.rsqrt(jnp.mean(ckv * ckv, axis=-1, keepdims=True) + RMS_EPS) * gkv_ref[...]).astype(BF16)
        k = _dot(ckvn, wk_ref[...])
        vb = _dot(ckvn, wv_ref[...]).astype(BF16)
        v_ref[...] = vb
        vt_ref[...] = vb.T
        kpe = _rope(hl[:, Q_RANK + KV_RANK:], c, sl, sh)
        for hd in range(HEADS):
            lanes = slice(hd * HEAD_PAD, (hd + 1) * HEAD_PAD)
            qb = (_rope(q[:, lanes], c, sl, sh) * Q_SCALE).astype(BF16)
            kb = (k[:, lanes] + kpe).astype(BF16)
            q_ref[:, lanes] = qb
            k_ref[:, lanes] = kb
            qt_ref[lanes, :] = qb.T
            kt_ref[lanes, :] = kb.T

    qk_w = HEADS * HEAD_PAD
    col_spec = lambda rows: pl.BlockSpec((rows, ts), lambda i: (0, i))
    return pl.pallas_call(
        body, name="fwd_lat", grid=(s // ts,),
        in_specs=[_row_spec(ts, D_MODEL), _full_spec(wlat.shape),
                  _full_spec(b_l.shape), _full_spec(g_q.shape),
                  _full_spec(wuq.shape), _full_spec(g_kv.shape), _full_spec(wk.shape), _full_spec(wv.shape),
                  _row_spec(ts, HEAD_PAD), _row_spec(ts, HEAD_PAD), _row_spec(ts, HEAD_PAD),
                  pl.BlockSpec(memory_space=pl.ANY)],
        out_specs=[_row_spec(ts, LAT_W), _row_spec(ts, qk_w),
                   _row_spec(ts, qk_w), _row_spec(ts, MLA_W), col_spec(qk_w), col_spec(qk_w),
                   col_spec(MLA_W), pl.BlockSpec((2, ts, D_MODEL // 2), lambda i: (0, i, 0))],
        out_shape=[jax.ShapeDtypeStruct((s, LAT_W), F32), jax.ShapeDtypeStruct((s, qk_w), BF16),
                   jax.ShapeDtypeStruct((s, qk_w), BF16), jax.ShapeDtypeStruct((s, MLA_W), BF16),
                   jax.ShapeDtypeStruct((qk_w, s), BF16),
                   jax.ShapeDtypeStruct((qk_w, s), BF16), jax.ShapeDtypeStruct((MLA_W, s), BF16),
                   jax.ShapeDtypeStruct((2, s, D_MODEL // 2), BF16)],
        compiler_params=_params(),
    )(x, wlat, b_l, g_q, wuq, g_kv, wk, wv, rc, rsl, rsh, after)


def _attn_fwd(qt, k, vt):
    s = k.shape[0]
    tk = ATT_TK
    nk = s // tk
    pairs = HEADS // 2

    def body(qt_ref, k_ref, vt_ref, o_ref, lse_ref):
        qts = [qt_ref[hh * HEAD_PAD:(hh + 1) * HEAD_PAD, :] for hh in range(2)]
        ones = jnp.ones((SUM_ROWS, tk), BF16)

        def scores(j):
            return tuple(_dot(k_ref[j * tk:(j + 1) * tk, hh * HEAD_PAD:(hh + 1) * HEAD_PAD], qts[hh][:, j * tk:])
                         for hh in range(2))

        def weighted(j, ps):
            return tuple(_dot(jnp.concatenate([vt_ref[hh * V_DIM:(hh + 1) * V_DIM, j * tk:(j + 1) * tk], ones], axis=0),
                              ps[hh]) for hh in range(2))

        def from_lane(full, lo, part):
            return part if lo == 0 else jnp.concatenate([full[:, :lo], part], axis=1)

        krow = lax.broadcasted_iota(jnp.int32, (tk, tk), 0)
        qcol = lax.broadcasted_iota(jnp.int32, (tk, tk), 1)
        st = scores(0)
        ps = None
        stats = [(jnp.full((1, s), -jnp.inf, F32), jnp.zeros((V_DIM + SUM_ROWS, s), F32))] * 2
        for j in range(nk):
            lo, lo_prev = j * tk, max(j - 1, 0) * tk
            st_next = scores(j + 1) if j + 1 < nk else None
            pvs = weighted(j - 1, ps) if j else None
            new_ps, new_stats = [], []
            for hh in range(2):
                m, acc = stats[hh]
                diag = jnp.where(krow <= qcol, st[hh][:, :tk], -jnp.inf)
                s_ = diag if j == nk - 1 else jnp.concatenate([diag, st[hh][:, tk:]], axis=1)
                if j:
                    acc = from_lane(acc, lo_prev, acc[:, lo_prev:] + pvs[hh])
                m_old = m[:, lo:]
                m_new = jnp.maximum(m_old, jnp.max(s_, axis=0, keepdims=True))
                a = jnp.exp2(m_old - m_new)
                p = jnp.exp2(s_ - m_new)
                new_stats.append((from_lane(m, lo, m_new), from_lane(acc, lo, a * acc[:, lo:])))
                new_ps.append(p.astype(BF16))
            st, ps, stats = st_next, new_ps, new_stats
        pvs = weighted(nk - 1, ps)
        lo = (nk - 1) * tk
        accs = [from_lane(stats[hh][1], lo, stats[hh][1][:, lo:] + pvs[hh]) for hh in range(2)]
        sums = [acc[V_DIM:V_DIM + 1, :] for acc in accs]
        ot = jnp.concatenate([accs[hh][:V_DIM, :] / sums[hh] for hh in range(2)], axis=0)
        o_ref[...] = ot.T
        lse = [stats[hh][0] + jnp.log(sums[hh]) * LOG2E for hh in range(2)]
        lse_ref[...] = jnp.concatenate(lse + [jnp.zeros((6, s), F32)], axis=0)

    return pl.pallas_call(
        body, name="attn_fwd", grid=(pairs,),
        in_specs=[pl.BlockSpec((2 * HEAD_PAD, s), lambda p: (p, 0)),
                  pl.BlockSpec((s, 2 * HEAD_PAD), lambda p: (0, p)),
                  pl.BlockSpec((2 * V_DIM, s), lambda p: (p, 0))],
        out_specs=[pl.BlockSpec((s, 2 * V_DIM), lambda p: (0, p)),
                   pl.BlockSpec((None, 8, s), lambda p: (p, 0, 0))],
        out_shape=[jax.ShapeDtypeStruct((s, MLA_W), F32), jax.ShapeDtypeStruct((pairs, 8, s), F32)],
        compiler_params=_params(("arbitrary",)),
    )(qt, k, vt)


def _attn_bwd(q, qt, k, kt, v, do, dot, lse, delta, after):
    s = k.shape[0]
    tk = ATT_BWD_TK
    nk = s // tk
    pairs = HEADS // 2

    def body(q_ref, qt_ref, k_ref, kt_ref, v_ref, do_ref, dot_ref, lse_ref, dl_ref, after_ref, dqt_ref, dk_ref,
             dv_ref):
        krow = lax.broadcasted_iota(jnp.int32, (tk, tk), 0)
        qcol = lax.broadcasted_iota(jnp.int32, (tk, tk), 1)
        lane = lax.broadcasted_iota(jnp.int32, (tk, 2 * V_DIM), 1)
        drow = lax.broadcasted_iota(jnp.int32, (2 * V_DIM, s), 0)
        dotb = dot_ref[...]
        dots = [jnp.where((drow < V_DIM) if hh == 0 else (drow >= V_DIM), dotb, jnp.zeros_like(dotb))
                for hh in range(2)]
        for j in range(nk):
            lo = j * tk
            vb = v_ref[lo:lo + tk, :]
            dob = do_ref[lo:, :]
            dvs = []
            for hh in range(2):
                rows = slice(hh * HEAD_PAD, (hh + 1) * HEAD_PAD)
                st = _dot(k_ref[lo:lo + tk, rows], qt_ref[rows, lo:])
                diag = jnp.where(krow <= qcol, st[:, :tk], -jnp.inf)
                st = diag if j == nk - 1 else jnp.concatenate([diag, st[:, tk:]], axis=1)
                p = jnp.exp2(st - lse_ref[hh:hh + 1, lo:])
                dpt = _dot(vb, dots[hh][:, lo:])
                dst = (p * (dpt - dl_ref[hh:hh + 1, lo:])).astype(BF16)
                dvs.append(_dot(p.astype(BF16), dob))
                dk_ref[lo:lo + tk, rows] = (_dot(dst, q_ref[lo:, rows]) * LN2).astype(BF16)
                dqt = _dot(kt_ref[rows, lo:lo + tk], dst)
                if j == 0:
                    dqt_ref[rows, :] = dqt
                else:
                    dqt_ref[rows, lo:] += dqt
            dv_ref[lo:lo + tk, :] = jnp.where(lane < V_DIM, dvs[0], dvs[1]).astype(BF16)
        dqt_ref[...] = dqt_ref[...] * SCALE

    pair_rows = lambda w: pl.BlockSpec((s, w), lambda p: (0, p))
    pair_cols = lambda w: pl.BlockSpec((w, s), lambda p: (p, 0))
    stats = pl.BlockSpec((None, 8, s), lambda p: (p, 0, 0))
    return pl.pallas_call(
        body, name="attn_bwd", grid=(pairs,),
        in_specs=[pair_rows(2 * HEAD_PAD), pair_cols(2 * HEAD_PAD), pair_rows(2 * HEAD_PAD), pair_cols(2 * HEAD_PAD),
                  pair_rows(2 * V_DIM), pair_rows(2 * V_DIM), pair_cols(2 * V_DIM), stats, stats,
                  pl.BlockSpec(memory_space=pl.ANY)],
        out_specs=[pair_cols(2 * HEAD_PAD), pair_rows(2 * HEAD_PAD), pair_rows(2 * V_DIM)],
        out_shape=[jax.ShapeDtypeStruct((HEADS * HEAD_PAD, s), F32), jax.ShapeDtypeStruct((s, HEADS * HEAD_PAD), BF16),
                   jax.ShapeDtypeStruct((s, MLA_W), BF16)],
        compiler_params=_params(("arbitrary",)),
    )(q, qt, k, kt, v, do, dot, lse, delta, after)


def _split3(a):
    hi = a.astype(BF16)
    r1 = a - hi.astype(F32)
    mid = r1.astype(BF16)
    lo = (r1 - mid.astype(F32)).astype(BF16)
    return hi, mid, lo


def _mid(x, tgt, o, hm, hg, woa, wob, wout, ln_g, ln_b, sg_g, sg_b, w_s, bsb):
    s = x.shape[0]
    ts = MID_ROW_TILE
    nsteps = s // ts
    nch = ts // CHUNK
    npair = GROUPS // 2

    def body(x_ref, t_ref, o_ref, hm_ref, hg_ref, woa_ref, wob_ref, wout_ref, lng_ref, lnb_ref, sgg_ref, sgb_ref,
             ws_ref, bsb_ref,
             dr_ref, dhg_ref, dhm_ref, do_ref, dot_ref, dl_ref, dhgt_ref, dhmt_ref,
             dwout_ref, dwoa_ref, dwob_ref, dws_ref, dbs_ref, dlng_ref, dlnb_ref, dsgg_ref, dsgb_ref, loss_ref,
             dbg_ref, dbm_ref, dbacc_ref, awout_ref, awoa_ref, awob_ref):
        i = pl.program_id(0)

        @pl.when(i == 0)
        def _():
            for r in (awout_ref, awoa_ref, awob_ref, dws_ref, dlng_ref, dlnb_ref, dsgg_ref, dsgb_ref, loss_ref,
                      dbg_ref, dbm_ref, dbacc_ref):
                r[...] = jnp.zeros_like(r)

        def emit(ref, tref, bref, lo, val):
            vb = val.astype(BF16)
            n = val.shape[1]
            ref[:, lo:lo + n] = vb
            tref[lo:lo + n, :] = vb.T
            bref[:, lo:lo + n] += jnp.sum(val, axis=0, keepdims=True)

        lane = lax.broadcasted_iota(jnp.int32, (CHUNK, CHUNK), 1)
        left = lane < V_DIM
        tril = lax.broadcasted_iota(jnp.int32, (CHUNK, CHUNK), 0) >= lane
        ms = [jnp.where(tril, ws_ref[g], 0.0).astype(BF16) for g in range(GROUPS)]

        z_a = hm_ref[:, 0:SGU_W].astype(F32)
        u = hm_ref[:, SGU_W:2 * SGU_W].astype(F32)
        v = hm_ref[:, 2 * SGU_W:3 * SGU_W].astype(F32)
        z_b = hm_ref[:, 3 * SGU_W:4 * SGU_W].astype(F32)
        o = o_ref[...]
        sa, dsa = _silu_and_grad(z_a)
        y_a = (o * sa).astype(BF16)
        gu, dgu = _gelu_and_grad(u)
        gv, dgv = _gelu_and_grad(v)
        mu = jnp.mean(gv, axis=-1, keepdims=True)
        vc = gv - mu
        rstd_v = lax.rsqrt(jnp.mean(vc * vc, axis=-1, keepdims=True) + LN_EPS)
        vhat = vc * rstd_v
        vn = (vhat * sgg_ref[...] + sgb_ref[...]).astype(BF16)
        rows = []
        for c in range(nch):
            blocks = []
            for p in range(npair):
                blk = vn[c * CHUNK:(c + 1) * CHUNK, p * CHUNK:(p + 1) * CHUNK]
                blocks.append(jnp.where(left, _dot(ms[2 * p], blk), _dot(ms[2 * p + 1], blk)))
            rows.append(jnp.concatenate(blocks, axis=1) + bsb_ref[...])
        mixed = jnp.concatenate(rows, axis=0)
        sgu = gu * mixed
        sb, dsb = _silu_and_grad(z_b)
        y_b = (sgu * sb).astype(BF16)
        pa = jnp.concatenate([_dot(y_a, woa_ref[k]) for k in range(N_SLABS)], axis=1)
        pb = jnp.concatenate([_dot(y_b, wob_ref[k]) for k in range(N_SLABS)], axis=1)
        sga = _sigmoid(hg_ref[:, :D_MODEL].astype(F32))
        sgb = _sigmoid(hg_ref[:, D_MODEL:].astype(F32))
        m2 = (sga * pa + sgb * pb).astype(BF16)
        r = ALPHA * x_ref[...] + _dot(m2, wout_ref[...])
        rmu = jnp.mean(r, axis=-1, keepdims=True)
        rc = r - rmu
        rstd = lax.rsqrt(jnp.mean(rc * rc, axis=-1, keepdims=True) + LN_EPS)
        xhat = rc * rstd
        y = xhat * lng_ref[...] + lnb_ref[...]
        err = y - t_ref[...]
        loss_ref[...] += jnp.full(loss_ref.shape, 0.5 / D_MODEL, F32) * jnp.sum(err * err)

        dy = err * (1.0 / D_MODEL)
        dlng_ref[...] += jnp.sum(dy * xhat, axis=0, keepdims=True)
        dlnb_ref[...] += jnp.sum(dy, axis=0, keepdims=True)
        dxh = dy * lng_ref[...]
        dr = rstd * (dxh - jnp.mean(dxh, axis=-1, keepdims=True) - xhat * jnp.mean(dxh * xhat, axis=-1, keepdims=True))
        dr_ref[...] = dr
        drb = dr.astype(BF16)
        awout_ref[...] += _dot_tn(m2, drb)
        dm2 = _dot_nt(drb, wout_ref[...])
        emit(dhg_ref, dhgt_ref, dbg_ref, 0, dm2 * pa * sga * (1.0 - sga))
        emit(dhg_ref, dhgt_ref, dbg_ref, D_MODEL, dm2 * pb * sgb * (1.0 - sgb))
        dpa = (dm2 * sga).astype(BF16)
        dpb = (dm2 * sgb).astype(BF16)
        dy_a = jnp.zeros((ts, MLA_W), F32)
        dy_b = jnp.zeros((ts, SGU_W), F32)
        y_at, y_bt = y_a.T, y_b.T
        for k in range(N_SLABS):
            cols = slice(k * SLAB_W, (k + 1) * SLAB_W)
            awoa_ref[k] += _dot(y_at, dpa[:, cols])
            awob_ref[k] += _dot(y_bt, dpb[:, cols])
            dy_a = dy_a + _dot_nt(dpa[:, cols], woa_ref[k])
            dy_b = dy_b + _dot_nt(dpb[:, cols], wob_ref[k])
        dob = (dy_a * sa).astype(BF16)
        do_ref[...] = dob
        dot_ref[...] = dob.T
        head = (lax.broadcasted_iota(jnp.int32, (HEADS, MLA_W), 1) // V_DIM
                == lax.broadcasted_iota(jnp.int32, (HEADS, MLA_W), 0)).astype(BF16)
        dl = sum(_dot_nt(head, term) for term in _split3(dob.astype(F32) * o))
        for p in range(HEADS // 2):
            dl_ref[p] = jnp.concatenate([dl[2 * p:2 * p + 2], jnp.zeros((6, ts), F32)], axis=0)
        emit(dhm_ref, dhmt_ref, dbm_ref, 0, dy_a * o * dsa)
        dsg = dy_b * sb
        emit(dhm_ref, dhmt_ref, dbm_ref, 3 * SGU_W, dy_b * sgu * dsb)
        emit(dhm_ref, dhmt_ref, dbm_ref, SGU_W, dsg * mixed * dgu)
        dmixed = dsg * gu
        dvn_rows = []
        dbs_sum = jnp.zeros((CHUNK, SGU_W), F32)
        for c in range(nch):
            dm_c = dmixed[c * CHUNK:(c + 1) * CHUNK, :]
            dbs_sum = dbs_sum + dm_c
            blocks = []
            for p in range(npair):
                dmb = dm_c[:, p * CHUNK:(p + 1) * CHUNK].astype(BF16)
                blk = vn[c * CHUNK:(c + 1) * CHUNK, p * CHUNK:(p + 1) * CHUNK]
                blocks.append(jnp.where(left, _dot_tn(ms[2 * p], dmb), _dot_tn(ms[2 * p + 1], dmb)))
                zero = jnp.zeros_like(dmb)
                dws_ref[2 * p] += jnp.where(tril, _dot_nt(jnp.where(left, dmb, zero), blk), 0.0)
                dws_ref[2 * p + 1] += jnp.where(tril, _dot_nt(jnp.where(left, zero, dmb), blk), 0.0)
            dvn_rows.append(jnp.concatenate(blocks, axis=1))
        dbacc_ref[...] += dbs_sum
        dvn = jnp.concatenate(dvn_rows, axis=0)
        dsgg_ref[...] += jnp.sum(dvn * vhat, axis=0, keepdims=True)
        dsgb_ref[...] += jnp.sum(dvn, axis=0, keepdims=True)
        dvh = dvn * sgg_ref[...]
        dgv_in = rstd_v * (dvh - jnp.mean(dvh, axis=-1, keepdims=True)
                           - vhat * jnp.mean(dvh * vhat, axis=-1, keepdims=True))
        emit(dhm_ref, dhmt_ref, dbm_ref, 2 * SGU_W, dgv_in * dgv)

        @pl.when(i == nsteps - 1)
        def _():
            dwout_ref[...] = awout_ref[...].astype(BF16)
            dwoa_ref[...] = awoa_ref[...].astype(BF16)
            dwob_ref[...] = awob_ref[...].astype(BF16)
            grp = (lax.broadcasted_iota(jnp.int32, (SGU_W, CHUNK), 0) // V_DIM
                   == lax.broadcasted_iota(jnp.int32, (SGU_W, CHUNK), 1)).astype(BF16)
            hi, mid, lo = _split3(dbacc_ref[...])
            dbs_ref[...] = _dot(hi, grp) + _dot(mid, grp) + _dot(lo, grp)

    acc_shapes = [(D_MODEL, D_MODEL), woa.shape, wob.shape, (GROUPS, CHUNK, CHUNK), (CHUNK, CHUNK),
                  (1, D_MODEL), (1, D_MODEL), (1, SGU_W), (1, SGU_W), (1, 128), (1, GATE_W), (1, MID_W)]
    col_spec = lambda rows: pl.BlockSpec((rows, ts), lambda i: (0, i))
    return pl.pallas_call(
        body, name="mid", grid=(nsteps,),
        in_specs=[_row_spec(ts, D_MODEL), _row_spec(ts, D_MODEL), _row_spec(ts, MLA_W), _row_spec(ts, MID_W),
                  _row_spec(ts, GATE_W), _full_spec(woa.shape), _full_spec(wob.shape), _full_spec(wout.shape),
                  _full_spec(ln_g.shape), _full_spec(ln_b.shape), _full_spec(sg_g.shape), _full_spec(sg_b.shape),
                  _full_spec(w_s.shape), _full_spec(bsb.shape)],
        out_specs=[_row_spec(ts, D_MODEL), _row_spec(ts, GATE_W), _row_spec(ts, MID_W), _row_spec(ts, MLA_W),
                   col_spec(MLA_W), pl.BlockSpec((HEADS // 2, 8, ts), lambda i: (0, 0, i)), col_spec(GATE_W),
                   col_spec(MID_W)]
        + [_full_spec(sh) for sh in acc_shapes],
        out_shape=[jax.ShapeDtypeStruct((s, D_MODEL), F32), jax.ShapeDtypeStruct((s, GATE_W), BF16),
                   jax.ShapeDtypeStruct((s, MID_W), BF16), jax.ShapeDtypeStruct((s, MLA_W), BF16),
                   jax.ShapeDtypeStruct((MLA_W, s), BF16), jax.ShapeDtypeStruct((HEADS // 2, 8, s), F32),
                   jax.ShapeDtypeStruct((GATE_W, s), BF16), jax.ShapeDtypeStruct((MID_W, s), BF16)]
        + [jax.ShapeDtypeStruct(sh, BF16 if n < 3 else F32) for n, sh in enumerate(acc_shapes)],
        scratch_shapes=[pltpu.VMEM((CHUNK, SGU_W), F32)] + [pltpu.VMEM(sh, F32) for sh in acc_shapes[:3]],
        compiler_params=_params(),
    )(x, tgt, o, hm, hg, woa, wob, wout, ln_g, ln_b, sg_g, sg_b, w_s, bsb)


def _lat_bwd(dq, dk, dv, hl, rc, rsl, rsh, g_q, g_kv, wuq, wk, wv, after):
    s = dk.shape[0]
    ts = ROW_TILE
    qk_w = HEADS * HEAD_PAD

    def body(dq_ref, dk_ref, dv_ref, hl_ref, rc_ref, rsl_ref, rsh_ref, gq_ref, gkv_ref, wuq_ref, wk_ref, wv_ref,
             after_ref, dhl_ref, dhlt_ref, dwuq_ref, dwk_ref, dwv_ref, dgq_ref, dgkv_ref, dbl_ref):
        i = pl.program_id(0)

        @pl.when(i == 0)
        def _():
            for r in (dwuq_ref, dwk_ref, dwv_ref, dgq_ref, dgkv_ref, dbl_ref):
                r[...] = jnp.zeros_like(r)

        def emit(lo, val):
            vb = val.astype(BF16)
            n = val.shape[1]
            dhl_ref[:, lo:lo + n] = vb
            dhlt_ref[lo:lo + n, :] = vb.T
            dbl_ref[:, lo:lo + n] += jnp.sum(val, axis=0, keepdims=True)

        c, sl, sh = rc_ref[...], rsl_ref[...], rsh_ref[...]
        lane = lax.broadcasted_iota(jnp.int32, (ts, HEAD_PAD), 1)
        pe = (lane >= NOPE) & (lane < QK_DIM)
        dkpe = jnp.zeros((ts, HEAD_PAD), F32)
        dqu = []
        for hd in range(HEADS):
            lanes = slice(hd * HEAD_PAD, (hd + 1) * HEAD_PAD)
            dqu.append(_rope_t(dq_ref[lanes, :].T, c, sl, sh).astype(BF16))
            dkpe = dkpe + dk_ref[:, lanes]
        dqu = jnp.concatenate(dqu, axis=1)
        dkpe = _rope_t(jnp.where(pe, dkpe, 0.0), c, sl, sh)

        cq = hl_ref[:, :Q_RANK]
        rq = lax.rsqrt(jnp.mean(cq * cq, axis=-1, keepdims=True) + RMS_EPS)
        cqh = cq * rq
        cqn = (cqh * gq_ref[...]).astype(BF16)
        dwuq_ref[...] += _dot_tn(cqn, dqu)
        dcqn = _dot_nt(dqu, wuq_ref[...])
        dgq_ref[...] += jnp.sum(dcqn * cqh, axis=0, keepdims=True)
        dch = dcqn * gq_ref[...]
        emit(0, rq * (dch - cqh * jnp.mean(dch * cqh, axis=-1, keepdims=True)))

        ckv = hl_ref[:, Q_RANK:Q_RANK + KV_RANK]
        rk = lax.rsqrt(jnp.mean(ckv * ckv, axis=-1, keepdims=True) + RMS_EPS)
        ckh = ckv * rk
        ckn = (ckh * gkv_ref[...]).astype(BF16)
        dkb = dk_ref[...].astype(BF16)
        dvb = dv_ref[...].astype(BF16)
        dwk_ref[...] += _dot_tn(ckn, dkb)
        dwv_ref[...] += _dot_tn(ckn, dvb)
        dckn = _dot_nt(dkb, wk_ref[...]) + _dot_nt(dvb, wv_ref[...])
        dgkv_ref[...] += jnp.sum(dckn * ckh, axis=0, keepdims=True)
        dkh = dckn * gkv_ref[...]
        emit(Q_RANK, rk * (dkh - ckh * jnp.mean(dkh * ckh, axis=-1, keepdims=True)))
        emit(Q_RANK + KV_RANK, dkpe)

    acc_shapes = [wuq.shape, wk.shape, wv.shape, g_q.shape, g_kv.shape, (1, LAT_W)]
    return pl.pallas_call(
        body, name="lat_bwd", grid=(s // ts,),
        in_specs=[pl.BlockSpec((qk_w, ts), lambda i: (0, i)), _row_spec(ts, qk_w), _row_spec(ts, MLA_W),
                  _row_spec(ts, LAT_W), _row_spec(ts, HEAD_PAD), _row_spec(ts, HEAD_PAD), _row_spec(ts, HEAD_PAD),
                  _full_spec(g_q.shape), _full_spec(g_kv.shape), _full_spec(wuq.shape), _full_spec(wk.shape),
                  _full_spec(wv.shape), pl.BlockSpec(memory_space=pl.ANY)],
        out_specs=[_row_spec(ts, LAT_W), pl.BlockSpec((LAT_W, ts), lambda i: (0, i))]
        + [_full_spec(sh) for sh in acc_shapes],
        out_shape=[jax.ShapeDtypeStruct((s, LAT_W), BF16), jax.ShapeDtypeStruct((LAT_W, s), BF16)]
        + [jax.ShapeDtypeStruct(sh, F32) for sh in acc_shapes],
        compiler_params=_params(),
    )(dq, dk, dv, hl, rc, rsl, rsh, g_q, g_kv, wuq, wk, wv, after)


def _dx(dr, dhg, dhm, dhl, wt, wlat, after):
    s = dr.shape[0]
    ts = MATMUL_ROW_TILE

    tk = D_MODEL
    bounds = ([(ROW_GATE + r0, ROW_GATE + r0 + tk) for r0 in range(0, GATE_W, tk)]
              + [(LAT_COLS + r0, LAT_COLS + r0 + tk) for r0 in range(0, MID_W, tk)])

    def body(dr_ref, dhg_ref, dhm_ref, dhl_ref, wt_hbm, wlat_ref, after_ref, dx_ref, wt_ref, sems):
        copies = [pltpu.make_async_copy(wt_hbm.at[lo:hi], wt_ref.at[lo:hi], sems.at[n])
                  for n, (lo, hi) in enumerate(bounds)]

        def compute(first):
            acc = (ALPHA * dr_ref[...] + _dot(dhl_ref[:, 0:Q_RANK + KV_RANK], wlat_ref[0:Q_RANK + KV_RANK, :])
                   + _dot(dhl_ref[:, Q_RANK + KV_RANK:], _kpe_rows(wlat_ref)))
            for n, (lo, hi) in enumerate(bounds):
                if first:
                    copies[n].wait()
                if lo >= ROW_GATE:
                    acc += _dot(dhg_ref[:, lo - ROW_GATE:hi - ROW_GATE], wt_ref[lo:hi, :])
                else:
                    acc += _dot(dhm_ref[:, lo - LAT_COLS:hi - LAT_COLS], wt_ref[lo:hi, :])
            dx_ref[...] = acc

        @pl.when(pl.program_id(0) == 0)
        def _():
            for cp in copies:
                cp.start()
            compute(True)

        @pl.when(pl.program_id(0) > 0)
        def _():
            compute(False)

    return pl.pallas_call(
        body, name="dx", grid=(s // ts,),
        in_specs=[_row_spec(ts, D_MODEL), _row_spec(ts, GATE_W), _row_spec(ts, MID_W), _row_spec(ts, LAT_W),
                  HBM_SPEC, _full_spec(wlat.shape), HBM_SPEC],
        out_specs=_row_spec(ts, D_MODEL),
        out_shape=jax.ShapeDtypeStruct((s, D_MODEL), F32),
        scratch_shapes=[pltpu.VMEM(wt.shape, BF16), pltpu.SemaphoreType.DMA((len(bounds),))],
        compiler_params=_params(),
    )(dr, dhg, dhm, dhl, wt, wlat, after)


def _dwt_early(dhmt, dhgt, xb, col, after, name):
    tn = 512
    nm, ng = MID_W // tn, GATE_W // tn
    s = dhmt.shape[1]
    hc = D_MODEL // 2

    ks = s // 2

    def body(col_ref, dma_ref, dmb_ref, dga_ref, dgb_ref, xba_ref, xbb_ref, after_ref, dw_ref):
        i = pl.program_id(0)

        @pl.when(i < nm)
        def _():
            dw_ref[...] = (_dot(dma_ref[...], xba_ref[...]) + _dot(dmb_ref[...], xbb_ref[...])).astype(BF16)

        @pl.when(i >= nm)
        def _():
            dw_ref[...] = (_dot(dga_ref[...], xba_ref[...]) + _dot(dgb_ref[...], xbb_ref[...])).astype(BF16)

    def dh_spec(first, part):
        if first:
            return pl.BlockSpec((tn, ks), lambda i, col_ref: (jnp.minimum(i, nm - 1), part))
        return pl.BlockSpec((tn, ks), lambda i, col_ref: (jnp.maximum(i - nm, 0), part))

    rows = pl.pallas_call(
        body, name=name,
        grid_spec=pltpu.PrefetchScalarGridSpec(
            num_scalar_prefetch=1, grid=(nm + ng,),
            in_specs=[dh_spec(True, 0), dh_spec(True, 1), dh_spec(False, 0), dh_spec(False, 1),
                      pl.BlockSpec((None, ks, hc), lambda i, col_ref: (col_ref[0], 0, 0)),
                      pl.BlockSpec((None, ks, hc), lambda i, col_ref: (col_ref[0], 1, 0)),
                      pl.BlockSpec(memory_space=pl.ANY)],
            out_specs=pl.BlockSpec((pl.Element(tn), pl.Element(hc)),
                                   lambda i, col_ref: (pl.multiple_of(LAT_COLS + i * tn, 32), 0))),
        out_shape=jax.ShapeDtypeStruct((IN_W, hc), BF16),
        compiler_params=_params(),
    )(col, dhmt, dhmt, dhgt, dhgt, xb, xb, after)

    def zero(buf_ref, out_ref):
        out_ref[...] = jnp.zeros_like(out_ref)

    return pl.pallas_call(
        zero, name=name + "_zero_lat", grid=(1,), in_specs=[pl.BlockSpec(memory_space=pl.ANY)],
        out_specs=pl.BlockSpec((LAT_COLS, hc), lambda i: (0, 0)),
        out_shape=jax.ShapeDtypeStruct((IN_W, hc), BF16), input_output_aliases={0: 0},
    )(rows)


def _dwt_lat(dhlt, xb2):
    n, s = dhlt.shape
    tk = MATMUL_ROW_TILE
    nsteps = s // tk

    def body(dht_ref, xb_ref, dw_ref, acc_ref):
        i = pl.program_id(0)

        @pl.when(i == 0)
        def _():
            acc_ref[...] = jnp.zeros_like(acc_ref)

        dht = dht_ref[...]
        acc_ref[...] += jnp.concatenate([_dot(dht, xb_ref[0]), _dot(dht, xb_ref[1])], axis=1)

        @pl.when(i == nsteps - 1)
        def _():
            kpe = Q_RANK + KV_RANK + NOPE
            dw_ref[0:Q_RANK + KV_RANK, :] = acc_ref[0:Q_RANK + KV_RANK, :].astype(BF16)
            dw_ref[Q_RANK + KV_RANK:LAT_COLS, :] = acc_ref[kpe:kpe + ROPE, :].astype(BF16)
            dw_ref[LAT_COLS:, :] = jnp.zeros((LAT_ROWS_PAD - LAT_COLS, D_MODEL), BF16)

    return pl.pallas_call(
        body, name="dwt_lat", grid=(nsteps,),
        in_specs=[pl.BlockSpec((n, tk), lambda i: (0, i)), pl.BlockSpec((2, tk, D_MODEL // 2), lambda i: (0, i, 0))],
        out_specs=_full_spec((LAT_ROWS_PAD, D_MODEL)),
        out_shape=jax.ShapeDtypeStruct((LAT_ROWS_PAD, D_MODEL), BF16),
        scratch_shapes=[pltpu.VMEM((n, D_MODEL), F32)],
        compiler_params=_params(),
    )(dhlt, xb2)


def _split_bias(b):
    z = lambda n: jnp.zeros((n,), b.dtype)
    lat = jnp.concatenate([b[:Q_RANK + KV_RANK], z(NOPE), b[Q_RANK + KV_RANK:LAT_COLS], z(HEAD_PAD - QK_DIM)])
    return b[None, ROW_GATE:], b[None, LAT_COLS:ROW_GATE], lat[None, :]


def _join_bias(g, m, l):
    kpe = Q_RANK + KV_RANK + NOPE
    return jnp.concatenate([l[0, :Q_RANK + KV_RANK], l[0, kpe:kpe + ROPE], m[0], g[0]])


def _rope_tables(positions):
    half = ROPE // 2
    inv_freq = ROPE_THETA ** (-jnp.arange(0, ROPE, 2, dtype=F32) / ROPE)
    ang = positions.astype(F32)[:, None] * inv_freq
    cos, sin = jnp.cos(ang), jnp.sin(ang)
    n = positions.shape[0]
    one, zero = jnp.ones((n, NOPE), F32), jnp.zeros((n, half), F32)
    tail1, tail0 = jnp.ones((n, HEAD_PAD - QK_DIM), F32), jnp.zeros((n, HEAD_PAD - QK_DIM), F32)
    z64 = jnp.zeros((n, NOPE), F32)
    rc = jnp.concatenate([one, cos, cos, tail1], axis=1)
    rsl = jnp.concatenate([z64, -sin, zero, tail0], axis=1)
    rsh = jnp.concatenate([z64, zero, sin, tail0], axis=1)
    return rc, rsl, rsh


def _pad_heads(w_uq):
    return jnp.pad(w_uq, ((0, 0), (0, 0), (0, HEAD_PAD - QK_DIM))).reshape(w_uq.shape[0], HEADS * HEAD_PAD)


def _prep_local(b_in, g_q, g_kv, w_ukv):
    b_g, b_m, b_l = _split_bias(b_in)
    wk = jnp.pad(w_ukv[:, :, :NOPE], ((0, 0), (0, 0), (0, HEAD_PAD - NOPE))).reshape(KV_RANK, HEADS * HEAD_PAD).astype(BF16)
    wv = w_ukv[:, :, NOPE:].reshape(KV_RANK, MLA_W).astype(BF16)
    return dict(b_g=b_g, b_m=b_m, b_l=b_l, wk=wk, wv=wv, gq2=g_q[None, :], gkv2=g_kv[None, :])


def _local_attention(x, tables, wlat, prep, wuq, after):
    rc, rsl, rsh = tables
    b_g, b_m, b_l, wk, wv, gq2, gkv2 = (prep[n] for n in ("b_g", "b_m", "b_l", "wk", "wv", "gq2", "gkv2"))
    hl, q, k, v, qt, kt, vt, xb2 = _fwd_lat(x, wlat, b_l, gq2, wuq, gkv2, wk, wv, rc, rsl, rsh, after)
    o, lse = _attn_fwd(qt, k, vt)
    return dict(q=q, qt=qt, k=k, kt=kt, v=v, o=o, lse=lse, hl=hl, rc=rc, rsl=rsl, rsh=rsh, gq2=gq2, gkv2=gkv2,
                wuq=wuq, wk=wk, wv=wv, xb2=xb2, b_g=b_g, b_m=b_m, wlat=wlat)


def _bias_lanes(b_s):
    return jnp.repeat(b_s.T, V_DIM, axis=1)


def _local_head(st, x, tgt, wt, w_oa, sg_g, sg_b, w_s, bsb, w_ob, w_out, ln_g, ln_b):
    q, qt, k, kt, v, o, lse, hl = (st[n] for n in ("q", "qt", "k", "kt", "v", "o", "lse", "hl"))
    rc, rsl, rsh, gq2, gkv2, wuq, wk, wv = (st[n] for n in ("rc", "rsl", "rsh", "gq2", "gkv2", "wuq", "wk", "wv"))
    hg, hm = _fwd_rest(st["xb2"], wt, st["b_g"], st["b_m"])
    (dr, dhg, dhm, do, dot, delta, dhgt, dhmt, dwout, dwoa, dwob, dws, dbs, dlng, dlnb, dsgg, dsgb, loss, dbg,
     dbm) = _mid(x, tgt, o, hm, hg, w_oa, w_ob, w_out, ln_g[None, :], ln_b[None, :], sg_g[None, :], sg_b[None, :],
                 w_s, bsb)
    early = {
        "w_oa": dwoa, "sgu_ln_g": dsgg[0], "sgu_ln_b": dsgb[0], "w_s": dws, "b_s": dbs[:, :GROUPS].T,
        "w_ob": dwob, "w_out": dwout, "ln_g": dlng[0], "ln_b": dlnb[0],
    }
    state = dict(q=q, qt=qt, k=k, kt=kt, v=v, do=do, dot=dot, lse=lse, delta=delta, hl=hl, rc=rc, rsl=rsl, rsh=rsh,
                 gq2=gq2, gkv2=gkv2, wuq=wuq, wk=wk, wv=wv, dr=dr, dhg=dhg, dhm=dhm, wt=wt, dbg=dbg, dbm=dbm,
                 dhgt=dhgt, dhmt=dhmt, xb2=st["xb2"], wlat=st["wlat"])
    return loss, early, state


def _local_attn_bwd(st, after):
    return _attn_bwd(st["q"], st["qt"], st["k"], st["kt"], st["v"], st["do"], st["dot"], st["lse"], st["delta"],
                     after)


def _local_tail(st, dq, dk, dv, after):
    dhl, dhlt, dwuq, dwk, dwv, dgq, dgkv, dbl = _lat_bwd(dq, dk, dv, st["hl"], st["rc"], st["rsl"], st["rsh"],
                                                         st["gq2"], st["gkv2"], st["wuq"], st["wk"], st["wv"], after)
    late = {
        "w_lat": _dwt_lat(dhlt, st["xb2"]),
        "b_in": _join_bias(st["dbg"], st["dbm"], dbl),
        "g_q": dgq[0],
        "w_uq": dwuq.reshape(Q_RANK, HEADS, HEAD_PAD)[:, :, :QK_DIM],
        "g_kv": dgkv[0],
        "w_ukv": jnp.concatenate([dwk.reshape(KV_RANK, HEADS, HEAD_PAD)[:, :, :NOPE],
                                  dwv.reshape(KV_RANK, HEADS, V_DIM)], axis=2),
    }
    return dhl, late


def _local_step(x, positions, tgt, wt, b_in, g_q, w_uq, g_kv, w_ukv, w_oa, sg_g, sg_b, w_s, b_s, w_ob, w_out, ln_g,
                ln_b):
    st = _local_attention(x, _rope_tables(positions), wt[:LAT_COLS], _prep_local(b_in, g_q, g_kv, w_ukv),
                          _pad_heads(w_uq).astype(BF16), b_in)
    loss, early, st = _local_head(st, x, tgt, wt, w_oa, sg_g, sg_b, w_s, _bias_lanes(b_s), w_ob, w_out, ln_g, ln_b)
    dq, dk, dv = _local_attn_bwd(st, loss)
    dhl, late = _local_tail(st, dq, dk, dv, dv)
    dx = _dx(st["dr"], st["dhg"], st["dhm"], dhl, st["wt"], st["wlat"], dhl)
    grads = {**early, **late}
    halves = [_dwt_early(st["dhmt"], st["dhgt"], st["xb2"], jnp.full((1,), h, jnp.int32), dhl, "dwt_half%d" % h)
              for h in range(2)]
    grads["w_in"] = jnp.concatenate([grads.pop("w_lat")[:LAT_COLS], jnp.concatenate(halves, axis=1)[LAT_COLS:]],
                                    axis=0)
    return loss, dx, grads


MESH = pl.DeviceIdType.MESH
N_CHIPS = 4
HBM_SPEC = pl.BlockSpec(memory_space=pl.ANY)
HBM_SPEC_STRICT = pl.BlockSpec(memory_space=pltpu.HBM)
VMEM_SPEC = pl.BlockSpec(memory_space=pltpu.VMEM)

REP_ROWS = 80


def _rows8(a):
    flat = a.reshape(-1)
    n = -(-flat.shape[0] // (8 * D_MODEL)) * 8 * D_MODEL
    return jnp.pad(flat, (0, n - flat.shape[0])).reshape(-1, D_MODEL)


def _place():
    x, y, c = lax.axis_index("x"), lax.axis_index("y"), lax.axis_index("c")
    others = [(1 - x, y), (x, 1 - y), (1 - x, 1 - y)]
    return x, y, c, others


N_DEV = 8
LOSS_TILE = (8, 128)


def _cast_own(shards, me, after):
    n = len(shards)

    def body(me_ref, *refs):
        for w in range(n):
            refs[n + 1 + w][...] = refs[w][...].astype(BF16)

    return pl.pallas_call(
        body, name="cast_own",
        grid_spec=pltpu.PrefetchScalarGridSpec(
            num_scalar_prefetch=1, grid=(1,),
            in_specs=[pl.BlockSpec(s.shape, lambda i, me_ref: (0, 0)) for s in shards]
            + [pl.BlockSpec(memory_space=pl.ANY)],
            out_specs=[pl.BlockSpec((None,) + s.shape, lambda i, me_ref: (me_ref[0], 0, 0)) for s in shards]),
        out_shape=[jax.ShapeDtypeStruct((N_CHIPS,) + s.shape, BF16) for s in shards],
        compiler_params=pltpu.CompilerParams(vmem_limit_bytes=VMEM_LIMIT),
    )(me, *shards, after)


def _cast_first(lat, uq, me):
    def body(me_ref, lat_ref, uq_ref, wlat_ref, guq_ref):
        wlat_ref[...] = lat_ref[...].astype(BF16)
        guq_ref[...] = uq_ref[...].astype(BF16)

    return pl.pallas_call(
        body, name="cast_first",
        grid_spec=pltpu.PrefetchScalarGridSpec(
            num_scalar_prefetch=1, grid=(1,),
            in_specs=[pl.BlockSpec((LAT_COLS, D_MODEL), lambda i, me_ref: (0, 0)),
                      pl.BlockSpec(uq.shape, lambda i, me_ref: (0, 0))],
            out_specs=[pl.BlockSpec((LAT_COLS, D_MODEL), lambda i, me_ref: (0, 0)),
                       pl.BlockSpec((None,) + uq.shape, lambda i, me_ref: (me_ref[0], 0, 0))]),
        out_shape=[jax.ShapeDtypeStruct((LAT_COLS, D_MODEL), BF16),
                   jax.ShapeDtypeStruct((N_CHIPS,) + uq.shape, BF16)],
    )(me, lat, uq)


def _first_copies(wlat_ref, guq_ref, send_sems, recv_sems, shapes):
    x, y, c, others = _place()
    me = 2 * x + y
    hl, hu = shapes[0][1] // 2, shapes[1][2] // 2
    lat_half = wlat_ref.at[:, pl.ds(c * hl, hl)]

    def copy(src, dst, k, to):
        return pltpu.make_async_remote_copy(src_ref=src, dst_ref=dst, send_sem=send_sems.at[k],
                                            recv_sem=recv_sems.at[k], device_id=to, device_id_type=MESH)

    def uq_half(chip):
        return guq_ref.at[chip, :, pl.ds(c * hu, hu)]

    lat_out = [copy(lat_half, lat_half, j, (*others[j], c)) for j in range(3)]
    uq_out = [copy(uq_half(me), uq_half(me), 3 + j, (*others[j], c)) for j in range(3)]
    j0 = jnp.maximum(x + 2 * y - 1, 0)
    lat_in = copy(lat_half, lat_half, j0, (0, 0, c))
    uq_in = [copy(uq_half(me), uq_half(2 * px + py), 3 + j, (px, py, c)) for j, (px, py) in enumerate(others)]
    return me, lat_out, uq_out, lat_in, uq_in


def _first_start(wlat, guq):
    shapes = (wlat.shape, guq.shape)

    def body(wlat_ref, guq_ref, send_sems, recv_sems, wlat_thru, guq_thru, token):
        me, lat_out, uq_out, _, _ = _first_copies(wlat_ref, guq_ref, send_sems, recv_sems, shapes)

        @pl.when(me == 0)
        def _():
            for cp in lat_out:
                cp.start()

        for cp in uq_out:
            cp.start()
        token[...] = jnp.zeros_like(token)

    outs = pl.pallas_call(
        body, name="first_start",
        out_shape=(pltpu.SemaphoreType.DMA((6,)), pltpu.SemaphoreType.DMA((6,)), pltpu.HBM(wlat.shape, BF16),
                   pltpu.HBM(guq.shape, BF16), jax.ShapeDtypeStruct(LOSS_TILE, F32)),
        in_specs=[HBM_SPEC_STRICT] * 2, out_specs=(SEM_SPEC, SEM_SPEC, HBM_SPEC_STRICT, HBM_SPEC_STRICT, VMEM_SPEC),
        input_output_aliases={0: 2, 1: 3},
        compiler_params=pltpu.CompilerParams(has_side_effects=SPLIT_EFFECT),
    )(pltpu.with_memory_space_constraint(wlat, pltpu.HBM), pltpu.with_memory_space_constraint(guq, pltpu.HBM))
    return outs


def _first_wait(send_sems, recv_sems, wlat, guq, *after):
    shapes = (wlat.shape, guq.shape)

    def body(wlat_ref, guq_ref, send_sems, recv_sems, *rest):
        me, lat_out, uq_out, lat_in, uq_in = _first_copies(wlat_ref, guq_ref, send_sems, recv_sems, shapes)

        @pl.when(me == 0)
        def _():
            for cp in lat_out:
                cp.wait_send()

        @pl.when(me != 0)
        def _():
            lat_in.wait_recv()

        for cp in uq_out:
            cp.wait_send()
        for cp in uq_in:
            cp.wait_recv()

    return pl.pallas_call(
        body, name="first_wait", out_shape=(pltpu.HBM(wlat.shape, BF16), pltpu.HBM(guq.shape, BF16)),
        in_specs=[HBM_SPEC_STRICT, HBM_SPEC_STRICT, SEM_SPEC, SEM_SPEC] + [HBM_SPEC] * len(after),
        out_specs=(HBM_SPEC_STRICT, HBM_SPEC_STRICT), input_output_aliases={0: 0, 1: 1},
        compiler_params=pltpu.CompilerParams(has_side_effects=SPLIT_EFFECT),
    )(wlat, guq, send_sems, recv_sems, *after)


def _first_forward(wlat, guq):
    hl, hu = wlat.shape[1] // 2, guq.shape[2] // 2

    def body(wlat_in, guq_in, wlat_ref, guq_ref, send_sems, recv_sems):
        x, y, c, others = _place()
        me = 2 * x + y
        sibling = (x, y, 1 - c)

        def copy(part, k):
            return pltpu.make_async_remote_copy(src_ref=part, dst_ref=part, send_sem=send_sems.at[k],
                                                recv_sem=recv_sems.at[k], device_id=sibling, device_id_type=MESH)

        def uq_part(j, half):
            px, py = others[j]
            return guq_ref.at[2 * px + py, :, pl.ds(half * hu, hu)]

        cps = [copy(uq_part(j, c), j) for j in range(3)]
        for cp in cps:
            cp.start()

        @pl.when(me != 0)
        def _():
            mine = copy(wlat_ref.at[:, pl.ds(c * hl, hl)], 3)
            mine.start()
            copy(wlat_ref.at[:, pl.ds((1 - c) * hl, hl)], 3).wait_recv()
            mine.wait_send()

        for j in range(3):
            copy(uq_part(j, 1 - c), j).wait_recv()
        for cp in cps:
            cp.wait_send()

    return pl.pallas_call(
        body, name="first_forward", in_specs=[HBM_SPEC, HBM_SPEC], out_specs=[HBM_SPEC, HBM_SPEC],
        out_shape=[jax.ShapeDtypeStruct(wlat.shape, BF16), jax.ShapeDtypeStruct(guq.shape, BF16)],
        input_output_aliases={0: 0, 1: 1},
        scratch_shapes=[pltpu.SemaphoreType.DMA((4,)), pltpu.SemaphoreType.DMA((4,))],
    )(wlat, guq)


def _gather_copy(b_ref, buf, w, j, src_chip, dst_chip, to, rows, send_sems, recv_sems):
    _, _, c, _ = _place()
    hc = _half(buf)
    return pltpu.make_async_remote_copy(
        src_ref=b_ref.at[src_chip, rows, pl.ds(c * hc, hc)], dst_ref=b_ref.at[dst_chip, rows, pl.ds(c * hc, hc)],
        send_sem=send_sems.at[3 * w + j], recv_sem=recv_sems.at[3 * w + j], device_id=to, device_id_type=MESH)


def _gather_rows(buf, w, chip, fn):
    if w != 0:
        fn(slice(None))
        return
    pl.when(chip == 0)(lambda: fn(pl.ds(LAT_COLS, buf.shape[1] - LAT_COLS)))
    pl.when(chip != 0)(lambda: fn(slice(None)))


def _gather_start(bufs, after):
    n = len(bufs)

    def body(*refs):
        b_refs = refs[:n]
        send_sems, recv_sems, token = refs[n + 1], refs[n + 2], refs[-1]
        x, y, c, others = _place()
        me = 2 * x + y
        for w in range(n):
            def start(rows, w=w):
                for j, (px, py) in enumerate(others):
                    _gather_copy(b_refs[w], bufs[w], w, j, me, me, (px, py, c), rows, send_sems, recv_sems).start()
            _gather_rows(bufs[w], w, me, start)
        token[...] = jnp.zeros_like(token)

    hbm = [pltpu.HBM(b.shape, BF16) for b in bufs]
    outs = pl.pallas_call(
        body, name="gather_start",
        out_shape=(pltpu.SemaphoreType.DMA((3 * n,)), pltpu.SemaphoreType.DMA((3 * n,)), *hbm,
                   jax.ShapeDtypeStruct(LOSS_TILE, F32)),
        in_specs=[HBM_SPEC_STRICT] * n + [HBM_SPEC],
        out_specs=(SEM_SPEC, SEM_SPEC, *[HBM_SPEC_STRICT] * n, VMEM_SPEC),
        input_output_aliases={i: 2 + i for i in range(n)},
        compiler_params=pltpu.CompilerParams(has_side_effects=SPLIT_EFFECT),
    )(*[pltpu.with_memory_space_constraint(b, pltpu.HBM) for b in bufs], after)
    return outs[0], outs[1], list(outs[2:2 + n]), outs[-1]


def _gather_wait(send_sems, recv_sems, bufs, after):
    n = len(bufs)

    def body(*refs):
        b_refs = refs[:n]
        send_sems, recv_sems = refs[n], refs[n + 1]
        x, y, c, others = _place()
        me = 2 * x + y
        for w in range(n):
            for j, (px, py) in enumerate(others):
                def copy(rows, w=w, j=j, px=px, py=py):
                    return _gather_copy(b_refs[w], bufs[w], w, j, me, 2 * px + py, (px, py, c), rows, send_sems,
                                        recv_sems)
                _gather_rows(bufs[w], w, me, lambda rows, copy=copy: copy(rows).wait_send())
                _gather_rows(bufs[w], w, 2 * px + py, lambda rows, copy=copy: copy(rows).wait_recv())

    outs = pl.pallas_call(
        body, name="gather_wait", out_shape=tuple(pltpu.HBM(b.shape, b.dtype) for b in bufs),
        in_specs=[HBM_SPEC_STRICT] * n + [SEM_SPEC, SEM_SPEC, HBM_SPEC],
        out_specs=tuple([HBM_SPEC_STRICT] * n), input_output_aliases={i: i for i in range(n)},
        compiler_params=pltpu.CompilerParams(has_side_effects=SPLIT_EFFECT),
    )(*bufs, send_sems, recv_sems, after)
    return list(outs)


def _gather_finish(bufs):
    n = len(bufs)

    def body(*refs):
        b_refs = refs[n:2 * n]
        send_sems, recv_sems = refs[2 * n:]
        x, y, c, others = _place()
        cps = []
        for w in range(n):
            hc = _half(bufs[w])
            for j, (px, py) in enumerate(others):
                part = b_refs[w].at[2 * px + py, :, pl.ds(c * hc, hc)]
                cps.append(pltpu.make_async_remote_copy(
                    src_ref=part, dst_ref=part, send_sem=send_sems.at[3 * w + j], recv_sem=recv_sems.at[3 * w + j],
                    device_id=(x, y, 1 - c), device_id_type=MESH))
        for cp in cps:
            cp.start()
        for w in range(n):
            hc = _half(bufs[w])
            for j, (px, py) in enumerate(others):
                theirs = b_refs[w].at[2 * px + py, :, pl.ds((1 - c) * hc, hc)]
                pltpu.make_async_remote_copy(
                    src_ref=theirs, dst_ref=theirs, send_sem=send_sems.at[3 * w + j], recv_sem=recv_sems.at[3 * w + j],
                    device_id=(x, y, 1 - c), device_id_type=MESH).wait_recv()
        for cp in cps:
            cp.wait_send()

    return pl.pallas_call(
        body, name="gather_finish", in_specs=[HBM_SPEC] * n, out_specs=[HBM_SPEC] * n,
        out_shape=[jax.ShapeDtypeStruct(b.shape, b.dtype) for b in bufs],
        input_output_aliases={i: i for i in range(n)},
        scratch_shapes=[pltpu.SemaphoreType.DMA((3 * n,)), pltpu.SemaphoreType.DMA((3 * n,))],
    )(*bufs)


def _half(a):
    return a.shape[-1] // 2


def _exchange_pairs(parts, name):
    n = len(parts)

    def body(*refs):
        p_refs, r_refs = refs[:n], refs[n:2 * n]
        send_sems, recv_sems = refs[2 * n:]
        x, y, c, _ = _place()
        cps = []
        for w in range(n):
            h = _half(parts[w])
            cps.append(pltpu.make_async_remote_copy(
                src_ref=p_refs[w].at[:, :, pl.ds((1 - c) * h, h)], dst_ref=r_refs[w],
                send_sem=send_sems.at[w], recv_sem=recv_sems.at[w], device_id=(x, y, 1 - c), device_id_type=MESH))
        for cp in cps:
            cp.start()
        for cp in cps:
            cp.wait()

    return pl.pallas_call(
        body, name=name, in_specs=[HBM_SPEC] * n, out_specs=[HBM_SPEC] * n,
        out_shape=[jax.ShapeDtypeStruct((N_CHIPS, p.shape[1], _half(p)), BF16) for p in parts],
        scratch_shapes=[pltpu.SemaphoreType.DMA((n,)), pltpu.SemaphoreType.DMA((n,))],
    )(*parts)


def _sibling_part(ref, w, n_whole, shape, c):
    if w < n_whole:
        return ref
    h = shape[-1] // 2
    return ref.at[:, :, pl.ds((1 - c) * h, h)]


def _pairs_start(parts, all_loss, n_whole):
    n = len(parts)

    def body(*refs):
        p_refs, r_refs, loss_ref = refs[:n], refs[n:2 * n], refs[2 * n]
        send_sems, recv_sems, token = refs[2 * n + 1], refs[2 * n + 2], refs[-1]
        x, y, c, _ = _place()
        for w in range(n):
            h = _half(parts[w])
            pltpu.make_async_remote_copy(
                src_ref=_sibling_part(p_refs[w], w, n_whole, parts[w].shape, c), dst_ref=r_refs[w],
                send_sem=send_sems.at[w], recv_sem=recv_sems.at[w], device_id=(x, y, 1 - c),
                device_id_type=MESH).start()
        me = 4 * x + 2 * y + c
        for t in range(1, N_DEV):
            d = (me + t) % N_DEV
            pltpu.make_async_remote_copy(
                src_ref=loss_ref.at[me], dst_ref=loss_ref.at[me], send_sem=send_sems.at[n + t - 1],
                recv_sem=recv_sems.at[n + t - 1], device_id=(d // 4, (d // 2) % 2, d % 2), device_id_type=MESH).start()
        token[...] = jnp.zeros_like(token)

    lands = [pltpu.HBM(p.shape if w < n_whole else (N_CHIPS, p.shape[1], _half(p)), BF16)
             for w, p in enumerate(parts)]
    nsem = n + N_DEV - 1
    outs = pl.pallas_call(
        body, name="pairs_start",
        out_shape=(pltpu.SemaphoreType.DMA((nsem,)), pltpu.SemaphoreType.DMA((nsem,)),
                   *[pltpu.HBM(p.shape, p.dtype) for p in parts], *lands, pltpu.HBM(all_loss.shape, F32),
                   jax.ShapeDtypeStruct(LOSS_TILE, F32)),
        in_specs=[HBM_SPEC_STRICT] * (2 * n + 1),
        out_specs=(SEM_SPEC, SEM_SPEC, *[HBM_SPEC_STRICT] * (2 * n + 1), VMEM_SPEC),
        input_output_aliases={i: 2 + i for i in range(2 * n + 1)},
        compiler_params=pltpu.CompilerParams(has_side_effects=SPLIT_EFFECT),
    )(*[pltpu.with_memory_space_constraint(p, pltpu.HBM) for p in parts],
      *[pltpu.with_memory_space_constraint(lax.empty(l.shape, BF16), pltpu.HBM) for l in lands],
      pltpu.with_memory_space_constraint(all_loss, pltpu.HBM))
    return outs[0], outs[1], list(outs[2:2 + n]), list(outs[2 + n:2 + 2 * n]), outs[2 + 2 * n], outs[-1]


def _pairs_wait(send_sems, recv_sems, parts, lands, all_loss, after, n_whole):
    n = len(parts)

    def body(*refs):
        p_refs, r_refs, loss_ref = refs[:n], refs[n:2 * n], refs[2 * n]
        send_sems, recv_sems = refs[2 * n + 1], refs[2 * n + 2]
        x, y, c, _ = _place()
        for w in range(n):
            h = _half(parts[w])
            cp = pltpu.make_async_remote_copy(
                src_ref=_sibling_part(p_refs[w], w, n_whole, parts[w].shape, c), dst_ref=r_refs[w],
                send_sem=send_sems.at[w],
                recv_sem=recv_sems.at[w], device_id=(x, y, 1 - c), device_id_type=MESH)
            cp.wait_send()
            cp.wait_recv()
        me = 4 * x + 2 * y + c
        for t in range(1, N_DEV):
            d = (me + N_DEV - t) % N_DEV
            cp = pltpu.make_async_remote_copy(
                src_ref=loss_ref.at[me], dst_ref=loss_ref.at[d], send_sem=send_sems.at[n + t - 1],
                recv_sem=recv_sems.at[n + t - 1], device_id=(d // 4, (d // 2) % 2, d % 2), device_id_type=MESH)
            cp.wait_send()
            cp.wait_recv()

    bufs = (*parts, *lands, all_loss)
    outs = pl.pallas_call(
        body, name="pairs_wait", out_shape=tuple(pltpu.HBM(a.shape, a.dtype) for a in bufs),
        in_specs=[HBM_SPEC_STRICT] * len(bufs) + [SEM_SPEC, SEM_SPEC, HBM_SPEC],
        out_specs=tuple([HBM_SPEC_STRICT] * len(bufs)), input_output_aliases={i: i for i in range(len(bufs))},
        compiler_params=pltpu.CompilerParams(has_side_effects=SPLIT_EFFECT),
    )(*bufs, send_sems, recv_sems, after)
    return list(outs[:n]), list(outs[n:2 * n]), outs[2 * n]


def _add_pair(ps, rs, c, name, n_whole=0):
    n = len(ps)

    def body(c_ref, *refs):
        for w in range(n):
            refs[2 * n + w][...] = (refs[w][...].astype(F32) + refs[n + w][...].astype(F32)).astype(BF16)

    def slab_spec(a):
        return pl.BlockSpec((None,) + a.shape[1:], lambda k, c_ref: (k, 0, 0))

    def my_half_spec(a):
        return pl.BlockSpec((None, a.shape[1], _half(a)), lambda k, c_ref: (k, 0, c_ref[0]))

    return pl.pallas_call(
        body, name=name,
        grid_spec=pltpu.PrefetchScalarGridSpec(
            num_scalar_prefetch=1, grid=(N_CHIPS,),
            in_specs=[slab_spec(p) if w < n_whole else my_half_spec(p) for w, p in enumerate(ps)]
            + [slab_spec(r) for r in rs],
            out_specs=[slab_spec(r) for r in rs]),
        out_shape=[jax.ShapeDtypeStruct(r.shape, BF16) for r in rs],
        compiler_params=_params(),
    )(c, *ps, *rs)


SEM_SPEC = pl.BlockSpec(memory_space=pltpu.SEMAPHORE)
SPLIT_EFFECT = pltpu.SideEffectType.DATAFLOW_SIDE_EFFECTING


def _chips_start(qs, name):
    n = len(qs)

    def body(*refs):
        q_refs, land_refs = refs[:n], refs[n:2 * n]
        send_sems, recv_sems, token = refs[2 * n], refs[2 * n + 1], refs[-1]
        x, y, c, others = _place()
        me = 2 * x + y
        for w in range(n):
            for j, (px, py) in enumerate(others):
                pltpu.make_async_remote_copy(
                    src_ref=q_refs[w].at[2 * px + py], dst_ref=land_refs[w].at[me], send_sem=send_sems.at[3 * w + j],
                    recv_sem=recv_sems.at[3 * w + j], device_id=(px, py, c), device_id_type=MESH).start()
        token[...] = jnp.zeros_like(token)

    hbm = [pltpu.HBM(q.shape, BF16) for q in qs]
    outs = pl.pallas_call(
        body, name=name,
        out_shape=(pltpu.SemaphoreType.DMA((3 * n,)), pltpu.SemaphoreType.DMA((3 * n,)), *hbm, *hbm,
                   jax.ShapeDtypeStruct(LOSS_TILE, F32)),
        in_specs=[HBM_SPEC_STRICT] * (2 * n),
        out_specs=(SEM_SPEC, SEM_SPEC, *[HBM_SPEC_STRICT] * (2 * n), VMEM_SPEC),
        input_output_aliases={i: 2 + i for i in range(2 * n)},
        compiler_params=pltpu.CompilerParams(has_side_effects=SPLIT_EFFECT),
    )(*[pltpu.with_memory_space_constraint(q, pltpu.HBM) for q in qs],
      *[pltpu.with_memory_space_constraint(lax.empty(q.shape, BF16), pltpu.HBM) for q in qs])
    return outs[0], outs[1], outs[2:2 + n], outs[2 + n:2 + 2 * n], outs[-1]


def _chips_wait(send_sems, recv_sems, q_thru, land_thru, after, name):
    n = len(q_thru)

    def body(*refs):
        q_refs, land_refs = refs[:n], refs[n:2 * n]
        send_sems, recv_sems = refs[2 * n], refs[2 * n + 1]
        x, y, c, others = _place()
        me = 2 * x + y
        for w in range(n):
            for j, (px, py) in enumerate(others):
                cp = pltpu.make_async_remote_copy(
                    src_ref=q_refs[w].at[2 * px + py], dst_ref=land_refs[w].at[2 * px + py],
                    send_sem=send_sems.at[3 * w + j], recv_sem=recv_sems.at[3 * w + j], device_id=(px, py, c),
                    device_id_type=MESH)
                cp.wait_send()
                cp.wait_recv()

    outs = pl.pallas_call(
        body, name=name, out_shape=tuple(pltpu.HBM(a.shape, a.dtype) for a in (*q_thru, *land_thru)),
        in_specs=[HBM_SPEC_STRICT] * (2 * n) + [SEM_SPEC, SEM_SPEC, HBM_SPEC],
        out_specs=tuple([HBM_SPEC_STRICT] * (2 * n)), input_output_aliases={i: i for i in range(2 * n)},
        compiler_params=pltpu.CompilerParams(has_side_effects=SPLIT_EFFECT),
    )(*q_thru, *land_thru, send_sems, recv_sems, after)
    return list(outs[:n]), list(outs[n:])


def _sum_chips(qs, rs, idx, all_dtypes):
    n = len(rs)
    n_all = len(all_dtypes)

    def body(idx_ref, *refs):
        c = idx_ref[4]
        for w in range(n):
            q_ref, g_ref = refs[w], refs[4 * n + w]
            acc = q_ref[...].astype(F32)
            for t in range(1, N_CHIPS):
                acc = acc + refs[n + 3 * w + t - 1][...].astype(F32)
            h = rs[w].shape[2]
            mine = pl.ds(pl.multiple_of(c * h, 128), h)
            g_ref[...] = jnp.zeros_like(g_ref)
            if w >= n - n_all:
                g_ref[idx_ref[0], :, mine] = acc.astype(g_ref.dtype)
            else:
                g_ref[:, mine] = acc

    shapes = [jax.ShapeDtypeStruct((r.shape[1], 2 * r.shape[2]), F32) for r in rs[:n - n_all]]
    shapes += [jax.ShapeDtypeStruct((N_CHIPS, r.shape[1], 2 * r.shape[2]), dt)
               for r, dt in zip(rs[n - n_all:], all_dtypes)]

    def slab_spec(a, t):
        return pl.BlockSpec((None,) + a.shape[1:], lambda i, idx_ref: (idx_ref[t], 0, 0))

    def whole_spec(sh):
        return pl.BlockSpec(sh.shape, lambda i, idx_ref: (0,) * len(sh.shape))

    return pl.pallas_call(
        body, name="sum_chips",
        grid_spec=pltpu.PrefetchScalarGridSpec(
            num_scalar_prefetch=1, grid=(1,),
            in_specs=[slab_spec(q, 0) for q in qs] + [slab_spec(r, t) for r in rs for t in range(1, N_CHIPS)],
            out_specs=[whole_spec(sh) for sh in shapes]),
        out_shape=shapes, compiler_params=_params(),
    )(idx, *qs, *[r for r in rs for _ in range(1, N_CHIPS)])


def _share(shards, alls):
    n, na = len(shards), len(alls)
    total = n + na

    def body(*refs):
        g_refs, a_refs = refs[total:total + n], refs[total + n:2 * total]
        send_sems, recv_sems = refs[2 * total:]
        x, y, c, others = _place()
        me = 2 * x + y
        sibling = (x, y, 1 - c)

        def cols_of(w, half):
            h = shards[w].shape[1] // 2
            return g_refs[w].at[:, pl.ds(half * h, h)]

        def slab(a, chip, half):
            h = alls[a].shape[2] // 2
            return a_refs[a].at[chip, :, pl.ds(half * h, h)]

        def copy(src, dst, k, to):
            return pltpu.make_async_remote_copy(src_ref=src, dst_ref=dst, send_sem=send_sems.at[k],
                                                recv_sem=recv_sems.at[k], device_id=to, device_id_type=MESH)

        cps = []
        for a in range(na):
            base = n + 7 * a
            for j, (px, py) in reversed(list(enumerate(others))):
                cps.append(copy(slab(a, me, c), slab(a, me, c), base + 1 + j, (px, py, c)))
            cps.append(copy(slab(a, me, c), slab(a, me, c), base, sibling))
        cps += [copy(cols_of(w, c), cols_of(w, c), w, sibling) for w in range(n)]
        for cp in cps:
            cp.start()
        fwd = []
        for a in range(na):
            base = n + 7 * a
            for j, (px, py) in enumerate(others):
                chip = 2 * px + py
                copy(slab(a, me, c), slab(a, chip, c), base + 1 + j, (px, py, c)).wait_recv()
                cp = copy(slab(a, chip, c), slab(a, chip, c), base + 4 + j, sibling)
                cp.start()
                fwd.append(cp)
        for a in range(na):
            base = n + 7 * a
            for j, (px, py) in enumerate(others):
                chip = 2 * px + py
                copy(slab(a, chip, c), slab(a, chip, 1 - c), base + 4 + j, sibling).wait_recv()
            copy(slab(a, me, c), slab(a, me, 1 - c), base, sibling).wait_recv()
        for w in range(n):
            copy(cols_of(w, c), cols_of(w, 1 - c), w, sibling).wait_recv()
        for cp in cps + fwd:
            cp.wait_send()

    nsem = n + 7 * na
    return pl.pallas_call(
        body, name="share", in_specs=[HBM_SPEC_STRICT] * total, out_specs=[HBM_SPEC_STRICT] * total,
        out_shape=[pltpu.HBM(a.shape, a.dtype) for a in (*shards, *alls)],
        input_output_aliases={i: i for i in range(total)},
        scratch_shapes=[pltpu.SemaphoreType.DMA((nsem,)), pltpu.SemaphoreType.DMA((nsem,))],
    )(*[pltpu.with_memory_space_constraint(a, pltpu.HBM) for a in (*shards, *alls)])


def _adamw(w, g, m, v):
    m2 = ADAM_B1 * m + (1.0 - ADAM_B1) * g
    v2 = ADAM_B2 * v + (1.0 - ADAM_B2) * (g * g)
    m_hat = m2 / (1.0 - ADAM_B1 ** ADAM_STEP)
    v_hat = v2 / (1.0 - ADAM_B2 ** ADAM_STEP)
    return -ADAM_LR * (m_hat / (jnp.sqrt(v_hat) + ADAM_EPS) + ADAM_WD * w), m2, v2


def _update_w_in(wt, gt, mt, vt, lat, owner, tile):
    nlat = lat.shape[0] // tile

    def body(owner_ref, w_ref, g_ref, m_ref, v_ref, lat_ref, g2_ref, d_ref, m2_ref, v2_ref):
        row = pl.program_id(0) * tile + lax.broadcasted_iota(jnp.int32, (tile, 1), 0)
        g = jnp.where((row < LAT_COLS) & (owner_ref[0] == 1), lat_ref[...].astype(F32), g_ref[...])
        g2_ref[...] = g
        d_ref[...], m2_ref[...], v2_ref[...] = _adamw(w_ref[...], g, m_ref[...], v_ref[...])

    spec = pl.BlockSpec((tile, wt.shape[1]), lambda i, o: (i, 0))
    return pl.pallas_call(
        body, name="update_w_in",
        grid_spec=pltpu.PrefetchScalarGridSpec(
            num_scalar_prefetch=1, grid=(wt.shape[0] // tile,),
            in_specs=[spec] * 4 + [pl.BlockSpec((tile, wt.shape[1]), lambda i, o: (jnp.minimum(i, nlat - 1), 0))],
            out_specs=[spec] * 4),
        out_shape=[jax.ShapeDtypeStruct(wt.shape, F32)] * 4,
        compiler_params=_params(("parallel",)),
    )(owner, wt, gt, mt, vt, lat)


def _update_small(ws, gs, ms, vs):
    n = len(ws)

    def body(*refs):
        for k in range(n):
            w_ref, g_ref, m_ref, v_ref = refs[k], refs[n + k], refs[2 * n + k], refs[3 * n + k]
            d, m2, v2 = _adamw(w_ref[...], g_ref[...], m_ref[...], v_ref[...])
            refs[4 * n + k][...] = d
            refs[5 * n + k][...] = m2
            refs[6 * n + k][...] = v2

    shapes = [jax.ShapeDtypeStruct(w.shape, F32) for w in ws]
    outs = pl.pallas_call(
        body, name="update_small", in_specs=[VMEM_SPEC] * (4 * n), out_specs=[VMEM_SPEC] * (3 * n),
        out_shape=shapes * 3,
        compiler_params=pltpu.CompilerParams(vmem_limit_bytes=VMEM_LIMIT),
    )(*ws, *gs, *ms, *vs)
    return outs[:n], outs[n:2 * n], outs[2 * n:]


REPLICATED = ("b_in", "g_q", "g_kv", "w_ukv", "sgu_ln_g", "sgu_ln_b", "w_s", "b_s", "ln_g", "ln_b")
ORDER = ("w_in", "b_in", "g_q", "w_uq", "g_kv", "w_ukv", "w_oa", "sgu_ln_g", "sgu_ln_b", "w_s", "b_s", "w_ob", "w_out",
         "ln_g", "ln_b")


def kernel(x, positions, w_in, b_in, g_q, w_uq, g_kv, w_ukv, w_oa, sgu_ln_g, sgu_ln_b, w_s, b_s, w_ob, w_out, ln_g, ln_b, loss_target, m_w_in, m_b_in, m_g_q, m_w_uq, m_g_kv, m_w_ukv, m_w_oa, m_sgu_ln_g, m_sgu_ln_b, m_w_s, m_b_s, m_w_ob, m_w_out, m_ln_g, m_ln_b, v_w_in, v_b_in, v_g_q, v_w_uq, v_g_kv, v_w_ukv, v_w_oa, v_sgu_ln_g, v_sgu_ln_b, v_w_s, v_b_s, v_w_ob, v_w_out, v_ln_g, v_ln_b):
    w = dict(w_in=w_in, b_in=b_in, g_q=g_q, w_uq=w_uq, g_kv=g_kv, w_ukv=w_ukv, w_oa=w_oa, sgu_ln_g=sgu_ln_g,
             sgu_ln_b=sgu_ln_b, w_s=w_s, b_s=b_s, w_ob=w_ob, w_out=w_out, ln_g=ln_g, ln_b=ln_b)
    m = dict(w_in=m_w_in, b_in=m_b_in, g_q=m_g_q, w_uq=m_w_uq, g_kv=m_g_kv, w_ukv=m_w_ukv, w_oa=m_w_oa,
             sgu_ln_g=m_sgu_ln_g, sgu_ln_b=m_sgu_ln_b, w_s=m_w_s, b_s=m_b_s, w_ob=m_w_ob, w_out=m_w_out, ln_g=m_ln_g,
             ln_b=m_ln_b)
    v = dict(w_in=v_w_in, b_in=v_b_in, g_q=v_g_q, w_uq=v_w_uq, g_kv=v_g_kv, w_ukv=v_w_ukv, w_oa=v_w_oa,
             sgu_ln_g=v_sgu_ln_g, sgu_ln_b=v_sgu_ln_b, w_s=v_w_s, b_s=v_b_s, w_ob=v_w_ob, w_out=v_w_out, ln_g=v_ln_g,
             ln_b=v_ln_b)
    w, m, v = ({n: a[0] for n, a in d.items()} for d in (w, m, v))
    c = lax.axis_index("c")

    wt_shard, mt_shard, vt_shard = (jnp.transpose(d["w_in"]) for d in (w, m, v))
    xi, yi = lax.axis_index("x"), lax.axis_index("y")
    me1 = (2 * xi + yi).reshape(1).astype(jnp.int32)
    first = _first_start(*_cast_first(wt_shard, _pad_heads(w["w_uq"]), me1))
    tables = _rope_tables(positions[0])
    prep = _prep_local(w["b_in"], w["g_q"], w["g_kv"], w["w_ukv"])
    bsb = _bias_lanes(w["b_s"])
    c1 = c.reshape(1).astype(jnp.int32)
    idx = jnp.stack([2 * xi + yi, 2 * (1 - xi) + yi, 2 * xi + (1 - yi), 2 * (1 - xi) + (1 - yi), c]).astype(jnp.int32)
    bufs = _cast_own([wt_shard, w["w_oa"], w["w_ob"], w["w_out"]], me1, first[4])
    send0, recv0, bufs, token0 = _gather_start(bufs, first[4])
    g_lat, g_uq = _first_forward(*_first_wait(*first[:4], token0, *tables, *prep.values(), bsb, idx))
    st = _local_attention(x[0], tables, g_lat, prep, g_uq.reshape(Q_RANK, HEADS * HEAD_PAD), token0)
    g_in, g_oa, g_ob, g_out = _gather_finish(_gather_wait(send0, recv0, bufs, st["o"]))
    wt = g_in.reshape(IN_W, D_MODEL)

    loss, early, st = _local_head(
        st, x[0], loss_target[0], wt, g_oa, w["sgu_ln_g"], w["sgu_ln_b"], w["w_s"], bsb, g_ob,
        g_out.reshape(D_MODEL, D_MODEL), w["ln_g"], w["ln_b"])

    slabs = lambda a: a.reshape(N_CHIPS, IN_W // N_CHIPS, D_MODEL // 2)
    theirs = slabs(_dwt_early(st["dhmt"], st["dhgt"], st["xb2"], 1 - c1, c1, "dwt_theirs"))
    parts1 = [theirs, early["w_oa"].astype(BF16), early["w_ob"].astype(BF16),
              early["w_out"].reshape(N_CHIPS, SLAB_W, D_MODEL).astype(BF16)]
    my_loss = lax.dynamic_update_slice(jnp.zeros((N_DEV,) + LOSS_TILE, F32), jnp.broadcast_to(loss, (1,) + LOSS_TILE),
                                       (4 * xi + 2 * yi + c, 0, 0))
    sems0 = _pairs_start(parts1, my_loss, 1)
    mine = slabs(_dwt_early(st["dhmt"], st["dhgt"], st["xb2"], c1, sems0[5], "dwt_mine"))
    parts1, recv1, all_loss = _pairs_wait(*sems0[:5], mine, 1)
    pairs1 = _add_pair([mine, *parts1[1:]], recv1, c1, "add_pair_early", n_whole=1)
    sems1 = _chips_start(pairs1, "chips_start_early")
    dq, dk, dv = _local_attn_bwd(st, sems1[4])
    dhl, late = _local_tail(st, dq, dk, dv, dv)

    grads = {**early, **late}
    rep = jnp.concatenate([_rows8(grads[n]) for n in REPLICATED], axis=0)
    rep = jnp.pad(rep, ((0, N_CHIPS * REP_ROWS - rep.shape[0]), (0, 0))).reshape(N_CHIPS, REP_ROWS, D_MODEL)
    parts2 = [late["w_uq"].reshape(N_CHIPS, Q_RANK // N_CHIPS, HEADS * QK_DIM).astype(BF16), rep.astype(BF16),
              late["w_lat"].reshape(N_CHIPS, LAT_ROWS_PAD // N_CHIPS, D_MODEL)]
    pairs2 = _add_pair(parts2, _exchange_pairs(parts2, "exchange_pairs_late"), c1, "add_pair_late")
    sems2 = _chips_start(pairs2, "chips_start_late")
    dx = _dx(st["dr"], st["dhg"], st["dhm"], dhl, st["wt"], st["wlat"], sems2[4])
    pairs2, landed2 = _chips_wait(*sems2[:4], dx, "chips_wait_late")
    pairs1, landed1 = _chips_wait(*sems1[:4], landed2[0], "chips_wait_early")
    sums = _sum_chips([*pairs1, *pairs2], [*landed1, *landed2], idx, (F32, BF16))
    *shards, g_rep, g_lat = _share(sums[:-2], sums[-2:])
    loss = jnp.sum(all_loss[:, 0, 0])

    red = {n: s.reshape(w[n].shape) for n, s in zip(("w_oa", "w_ob", "w_out", "w_uq"), shards[1:])}
    g_rep = g_rep.reshape(N_CHIPS * REP_ROWS, D_MODEL)
    off = 0
    for n in REPLICATED:
        rows = _rows8(w[n]).shape[0]
        red[n] = g_rep[off:off + rows].reshape(-1)[:w[n].size].reshape(w[n].shape)
        off += rows
    owner = (2 * xi + yi == 0).astype(jnp.int32).reshape(1)
    gt, dt, mt, vt2 = _update_w_in(wt_shard, shards[0], mt_shard, vt_shard,
                                   g_lat.reshape(LAT_ROWS_PAD, D_MODEL).astype(F32), owner, 232)
    red["w_in"] = jnp.transpose(gt)
    small = [n for n in ORDER if n != "w_in"]
    as2d = lambda a: a.reshape(-1, a.shape[-1])
    ds, ms, vs = _update_small([as2d(w[n]) for n in small], [as2d(red[n]) for n in small],
                               [as2d(m[n]) for n in small], [as2d(v[n]) for n in small])
    delta, new_m, new_v = {"w_in": jnp.transpose(dt)}, {"w_in": jnp.transpose(mt)}, {"w_in": jnp.transpose(vt2)}
    for i, n in enumerate(small):
        delta[n], new_m[n], new_v[n] = (a[i].reshape(w[n].shape) for a in (ds, ms, vs))

    lead = lambda a: a[None]
    return (loss, dx[None], *[lead(red[n]) for n in ORDER], *[lead(delta[n]) for n in ORDER],
            *[lead(new_m[n]) for n in ORDER], *[lead(new_v[n]) for n in ORDER])
```

```python
import math

import jax
import jax.numpy as jnp
from jax import lax
from jax.experimental import pallas as pl
from jax.experimental.pallas import tpu as pltpu

F32 = jnp.float32
BF16 = jnp.bfloat16

D_MODEL = 1024
HEADS = 8
Q_RANK = 384
KV_RANK = 128
NOPE = 64
ROPE = 32
V_DIM = 64
QK_DIM = NOPE + ROPE
HEAD_PAD = 128
MLA_W = HEADS * V_DIM
SGU_W = 512
GROUPS = 8
CHUNK = 128
IN_W = 4640
RMS_EPS = 1e-6
LN_EPS = 1e-5
ALPHA = 2.0 ** 0.25
ROPE_THETA = 10000.0
SCALE = QK_DIM ** -0.5

GATE_W = 2 * D_MODEL
MID_W = 4 * SGU_W
LAT_W = Q_RANK + KV_RANK + HEAD_PAD
LAT_COLS = Q_RANK + KV_RANK + ROPE
ROW_GATE = LAT_COLS + MID_W
LAT_ROWS_PAD = 704
N_SLABS = 4
SLAB_W = D_MODEL // N_SLABS

ROW_TILE = 256
MATMUL_ROW_TILE = 512
MID_ROW_TILE = 256
ATT_TK = 256
ATT_BWD_TK = 256
SUM_ROWS = 16
LOG2E = 1.4426950408889634
LN2 = 0.6931471805599453
Q_SCALE = SCALE * LOG2E
VMEM_LIMIT = 56 * 1024 * 1024

ADAM_LR = 0.001
ADAM_B1 = 0.9
ADAM_B2 = 0.999
ADAM_EPS = 1e-08
ADAM_WD = 0.01
ADAM_STEP = 10


def _dot(a, b):
    return jnp.dot(a, b, preferred_element_type=F32)


def _dot_nt(a, b):
    return lax.dot_general(a, b, (((1,), (1,)), ((), ())), preferred_element_type=F32)


def _dot_tn(a, b):
    return lax.dot_general(a, b, (((0,), (0,)), ((), ())), preferred_element_type=F32)


def _sigmoid(z):
    return 0.5 * jnp.tanh(0.5 * z) + 0.5


_GELU_C = math.sqrt(2.0 / math.pi)


def _gelu_and_grad(x):
    x2 = x * x
    t = jnp.tanh(_GELU_C * (x + 0.044715 * x * x2))
    g = 0.5 * x * (1.0 + t)
    dg = 0.5 * (1.0 + t) + 0.5 * x * (1.0 - t * t) * (_GELU_C * (1.0 + 3.0 * 0.044715 * x2))
    return g, dg


def _silu_and_grad(z):
    s = _sigmoid(z)
    return z * s, s * (1.0 + z * (1.0 - s))


def _rope(xb, c, sl, sh):
    return xb * c + pltpu.roll(xb, 112, 1) * sl + pltpu.roll(xb, 16, 1) * sh


def _rope_t(dy, c, sl, sh):
    return dy * c + pltpu.roll(dy * sl, 16, 1) + pltpu.roll(dy * sh, 112, 1)


def _params(sem=("arbitrary",)):
    return pltpu.CompilerParams(dimension_semantics=sem, vmem_limit_bytes=VMEM_LIMIT)


def _row_spec(tile, width):
    return pl.BlockSpec((tile, width), lambda i: (i, 0))


def _full_spec(shape):
    nd = len(shape)
    return pl.BlockSpec(shape, lambda i: (0,) * nd)


def _kpe_rows(wt_ref):
    z = lambda n: jnp.zeros((n, D_MODEL), BF16)
    return jnp.concatenate([z(NOPE), wt_ref[Q_RANK + KV_RANK:LAT_COLS, :], z(HEAD_PAD - QK_DIM)], axis=0)


def _fwd_rest(xb2, wt, b_g, b_m):
    s = xb2.shape[1]
    ts = MATMUL_ROW_TILE
    tn = D_MODEL
    blocks = ([(ROW_GATE + c0, 0, c0) for c0 in range(0, GATE_W, tn)]
              + [(LAT_COLS + c0, 1, c0) for c0 in range(0, MID_W, tn)])

    def body(xb_ref, wt_hbm, bg_ref, bm_ref, hg_ref, hm_ref, wt_ref, sems):
        copies = [pltpu.make_async_copy(wt_hbm.at[lo:lo + tn], wt_ref.at[lo:lo + tn], sems.at[n])
                  for n, (lo, _, _) in enumerate(blocks)]

        def compute(first):
            xb_ = jnp.concatenate([xb_ref[0], xb_ref[1]], axis=1)
            for cp, (lo, which, c0) in zip(copies, blocks):
                if first:
                    cp.wait()
                out_ref, b_ref = ((hg_ref, bg_ref), (hm_ref, bm_ref))[which]
                out_ref[:, c0:c0 + tn] = (_dot_nt(xb_, wt_ref[lo:lo + tn, :]) + b_ref[:, c0:c0 + tn]).astype(BF16)

        @pl.when(pl.program_id(0) == 0)
        def _():
            for cp in copies:
                cp.start()
            compute(True)

        @pl.when(pl.program_id(0) > 0)
        def _():
            compute(False)

    return pl.pallas_call(
        body, name="fwd_rest", grid=(s // ts,),
        in_specs=[pl.BlockSpec((2, ts, D_MODEL // 2), lambda i: (0, i, 0)), HBM_SPEC, _full_spec(b_g.shape),
                  _full_spec(b_m.shape)],
        out_specs=[_row_spec(ts, GATE_W), _row_spec(ts, MID_W)],
        out_shape=[jax.ShapeDtypeStruct((s, GATE_W), BF16), jax.ShapeDtypeStruct((s, MID_W), BF16)],
        scratch_shapes=[pltpu.VMEM(wt.shape, BF16), pltpu.SemaphoreType.DMA((len(blocks),))],
        compiler_params=_params(),
    )(xb2, wt, b_g, b_m)


def _fwd_lat(x, wlat, b_l, g_q, wuq, g_kv, wk, wv, rc, rsl, rsh, after):
    s = x.shape[0]
    ts = ROW_TILE

    def body(x_ref, wt_ref, bl_ref, gq_ref, wuq_ref, gkv_ref, wk_ref, wv_ref, rc_ref, rsl_ref,
             rsh_ref, after_ref, hl_ref, q_ref, k_ref, v_ref, qt_ref, kt_ref, vt_ref, xb2_ref):
        xb = x_ref[...].astype(BF16)
        xb2_ref[0] = xb[:, :D_MODEL // 2]
        xb2_ref[1] = xb[:, D_MODEL // 2:]
        hl = jnp.concatenate([_dot_nt(xb, wt_ref[0:Q_RANK + KV_RANK, :]), _dot_nt(xb, _kpe_rows(wt_ref))],
                             axis=1) + bl_ref[...]
        hl_ref[...] = hl
        c, sl, sh = rc_ref[...], rsl_ref[...], rsh_ref[...]
        cq = hl[:, :Q_RANK]
        cqn = cq * lax.rsqrt(jnp.mean(cq * cq, axis=-1, keepdims=True) + RMS_EPS) * gq_ref[...]
        q = _dot(cqn.astype(BF16), wuq_ref[...])
        ckv = hl[:, Q_RANK:Q_RANK + KV_RANK]
        ckvn = (ckv * lax.rsqrt(jnp.mean(ckv * ckv, axis=-1, keepdims=True) + RMS_EPS) * gkv_ref[...]).astype(BF16)
        k = _dot(ckvn, wk_ref[...])
        vb = _dot(ckvn, wv_ref[...]).astype(BF16)
        v_ref[...] = vb
        vt_ref[...] = vb.T
        kpe = _rope(hl[:, Q_RANK + KV_RANK:], c, sl, sh)
        for hd in range(HEADS):
            lanes = slice(hd * HEAD_PAD, (hd + 1) * HEAD_PAD)
            qb = (_rope(q[:, lanes], c, sl, sh) * Q_SCALE).astype(BF16)
            kb = (k[:, lanes] + kpe).astype(BF16)
            q_ref[:, lanes] = qb
            k_ref[:, lanes] = kb
            qt_ref[lanes, :] = qb.T
            kt_ref[lanes, :] = kb.T

    qk_w = HEADS * HEAD_PAD
    col_spec = lambda rows: pl.BlockSpec((rows, ts), lambda i: (0, i))
    return pl.pallas_call(
        body, name="fwd_lat", grid=(s // ts,),
        in_specs=[_row_spec(ts, D_MODEL), _full_spec(wlat.shape),
                  _full_spec(b_l.shape), _full_spec(g_q.shape),
                  _full_spec(wuq.shape), _full_spec(g_kv.shape), _full_spec(wk.shape), _full_spec(wv.shape),
                  _row_spec(ts, HEAD_PAD), _row_spec(ts, HEAD_PAD), _row_spec(ts, HEAD_PAD),
                  pl.BlockSpec(memory_space=pl.ANY)],
        out_specs=[_row_spec(ts, LAT_W), _row_spec(ts, qk_w),
                   _row_spec(ts, qk_w), _row_spec(ts, MLA_W), col_spec(qk_w), col_spec(qk_w),
                   col_spec(MLA_W), pl.BlockSpec((2, ts, D_MODEL // 2), lambda i: (0, i, 0))],
        out_shape=[jax.ShapeDtypeStruct((s, LAT_W), F32), jax.ShapeDtypeStruct((s, qk_w), BF16),
                   jax.ShapeDtypeStruct((s, qk_w), BF16), jax.ShapeDtypeStruct((s, MLA_W), BF16),
                   jax.ShapeDtypeStruct((qk_w, s), BF16),
                   jax.ShapeDtypeStruct((qk_w, s), BF16), jax.ShapeDtypeStruct((MLA_W, s), BF16),
                   jax.ShapeDtypeStruct((2, s, D_MODEL // 2), BF16)],
        compiler_params=_params(),
    )(x, wlat, b_l, g_q, wuq, g_kv, wk, wv, rc, rsl, rsh, after)


def _attn_fwd(qt, k, vt):
    s = k.shape[0]
    tk = ATT_TK
    nk = s // tk
    pairs = HEADS // 2

    def body(qt_ref, k_ref, vt_ref, o_ref, lse_ref):
        qts = [qt_ref[hh * HEAD_PAD:(hh + 1) * HEAD_PAD, :] for hh in range(2)]
        ones = jnp.ones((SUM_ROWS, tk), BF16)

        def scores(j):
            return tuple(_dot(k_ref[j * tk:(j + 1) * tk, hh * HEAD_PAD:(hh + 1) * HEAD_PAD], qts[hh][:, j * tk:])
                         for hh in range(2))

        def weighted(j, ps):
            return tuple(_dot(jnp.concatenate([vt_ref[hh * V_DIM:(hh + 1) * V_DIM, j * tk:(j + 1) * tk], ones], axis=0),
                              ps[hh]) for hh in range(2))

        def from_lane(full, lo, part):
            return part if lo == 0 else jnp.concatenate([full[:, :lo], part], axis=1)

        krow = lax.broadcasted_iota(jnp.int32, (tk, tk), 0)
        qcol = lax.broadcasted_iota(jnp.int32, (tk, tk), 1)
        st = scores(0)
        ps = None
        stats = [(jnp.full((1, s), -jnp.inf, F32), jnp.zeros((V_DIM + SUM_ROWS, s), F32))] * 2
        for j in range(nk):
            lo, lo_prev = j * tk, max(j - 1, 0) * tk
            st_next = scores(j + 1) if j + 1 < nk else None
            pvs = weighted(j - 1, ps) if j else None
            new_ps, new_stats = [], []
            for hh in range(2):
                m, acc = stats[hh]
                diag = jnp.where(krow <= qcol, st[hh][:, :tk], -jnp.inf)
                s_ = diag if j == nk - 1 else jnp.concatenate([diag, st[hh][:, tk:]], axis=1)
                if j:
                    acc = from_lane(acc, lo_prev, acc[:, lo_prev:] + pvs[hh])
                m_old = m[:, lo:]
                m_new = jnp.maximum(m_old, jnp.max(s_, axis=0, keepdims=True))
                a = jnp.exp2(m_old - m_new)
                p = jnp.exp2(s_ - m_new)
                new_stats.append((from_lane(m, lo, m_new), from_lane(acc, lo, a * acc[:, lo:])))
                new_ps.append(p.astype(BF16))
            st, ps, stats = st_next, new_ps, new_stats
        pvs = weighted(nk - 1, ps)
        lo = (nk - 1) * tk
        accs = [from_lane(stats[hh][1], lo, stats[hh][1][:, lo:] + pvs[hh]) for hh in range(2)]
        sums = [acc[V_DIM:V_DIM + 1, :] for acc in accs]
        ot = jnp.concatenate([accs[hh][:V_DIM, :] / sums[hh] for hh in range(2)], axis=0)
        o_ref[...] = ot.T
        lse = [stats[hh][0] + jnp.log(sums[hh]) * LOG2E for hh in range(2)]
        lse_ref[...] = jnp.concatenate(lse + [jnp.zeros((6, s), F32)], axis=0)

    return pl.pallas_call(
        body, name="attn_fwd", grid=(pairs,),
        in_specs=[pl.BlockSpec((2 * HEAD_PAD, s), lambda p: (p, 0)),
                  pl.BlockSpec((s, 2 * HEAD_PAD), lambda p: (0, p)),
                  pl.BlockSpec((2 * V_DIM, s), lambda p: (p, 0))],
        out_specs=[pl.BlockSpec((s, 2 * V_DIM), lambda p: (0, p)),
                   pl.BlockSpec((None, 8, s), lambda p: (p, 0, 0))],
        out_shape=[jax.ShapeDtypeStruct((s, MLA_W), F32), jax.ShapeDtypeStruct((pairs, 8, s), F32)],
        compiler_params=_params(("arbitrary",)),
    )(qt, k, vt)


def _attn_bwd(q, qt, k, kt, v, do, dot, lse, delta, after, first_pair, n_pairs, into, name):
    s = k.shape[0]
    tk = ATT_BWD_TK
    nk = s // tk

    def body(q_ref, qt_ref, k_ref, kt_ref, v_ref, do_ref, dot_ref, lse_ref, dl_ref, after_ref, *rest):
        dqt_ref, dk_ref, dv_ref = rest[-3:]
        krow = lax.broadcasted_iota(jnp.int32, (tk, tk), 0)
        qcol = lax.broadcasted_iota(jnp.int32, (tk, tk), 1)
        lane = lax.broadcasted_iota(jnp.int32, (tk, 2 * V_DIM), 1)
        drow = lax.broadcasted_iota(jnp.int32, (2 * V_DIM, s), 0)
        dotb = dot_ref[...]
        dots = [jnp.where((drow < V_DIM) if hh == 0 else (drow >= V_DIM), dotb, jnp.zeros_like(dotb))
                for hh in range(2)]
        for j in range(nk):
            lo = j * tk
            vb = v_ref[lo:lo + tk, :]
            dob = do_ref[lo:, :]
            dvs = []
            for hh in range(2):
                rows = slice(hh * HEAD_PAD, (hh + 1) * HEAD_PAD)
                st = _dot(k_ref[lo:lo + tk, rows], qt_ref[rows, lo:])
                diag = jnp.where(krow <= qcol, st[:, :tk], -jnp.inf)
                st = diag if j == nk - 1 else jnp.concatenate([diag, st[:, tk:]], axis=1)
                p = jnp.exp2(st - lse_ref[hh:hh + 1, lo:])
                dpt = _dot(vb, dots[hh][:, lo:])
                dst = (p * (dpt - dl_ref[hh:hh + 1, lo:])).astype(BF16)
                dvs.append(_dot(p.astype(BF16), dob))
                dk_ref[lo:lo + tk, rows] = (_dot(dst, q_ref[lo:, rows]) * LN2).astype(BF16)
                dqt = _dot(kt_ref[rows, lo:lo + tk], dst)
                if j == 0:
                    dqt_ref[rows, :] = dqt
                else:
                    dqt_ref[rows, lo:] += dqt
            dv_ref[lo:lo + tk, :] = jnp.where(lane < V_DIM, dvs[0], dvs[1]).astype(BF16)
        dqt_ref[...] = dqt_ref[...] * SCALE

    pair_rows = lambda w: pl.BlockSpec((s, w), lambda p: (0, p + first_pair))
    pair_cols = lambda w: pl.BlockSpec((w, s), lambda p: (p + first_pair, 0))
    stats = pl.BlockSpec((None, 8, s), lambda p: (p + first_pair, 0, 0))
    any_spec = pl.BlockSpec(memory_space=pl.ANY)
    return pl.pallas_call(
        body, name=name, grid=(n_pairs,),
        in_specs=[pair_rows(2 * HEAD_PAD), pair_cols(2 * HEAD_PAD), pair_rows(2 * HEAD_PAD), pair_cols(2 * HEAD_PAD),
                  pair_rows(2 * V_DIM), pair_rows(2 * V_DIM), pair_cols(2 * V_DIM), stats, stats, any_spec]
        + [any_spec] * len(into),
        out_specs=[pair_cols(2 * HEAD_PAD), pair_rows(2 * HEAD_PAD), pair_rows(2 * V_DIM)],
        out_shape=[jax.ShapeDtypeStruct((HEADS * HEAD_PAD, s), F32), jax.ShapeDtypeStruct((s, HEADS * HEAD_PAD), BF16),
                   jax.ShapeDtypeStruct((s, MLA_W), BF16)],
        input_output_aliases={10 + n: n for n in range(len(into))},
        compiler_params=_params(("arbitrary",)),
    )(q, qt, k, kt, v, do, dot, lse, delta, after, *into)


def _split3(a):
    hi = a.astype(BF16)
    r1 = a - hi.astype(F32)
    mid = r1.astype(BF16)
    lo = (r1 - mid.astype(F32)).astype(BF16)
    return hi, mid, lo


def _mid(x, tgt, o, hm, hg, woa, wob, wout, ln_g, ln_b, sg_g, sg_b, w_s, bsb):
    s = x.shape[0]
    ts = MID_ROW_TILE
    nsteps = s // ts
    nch = ts // CHUNK
    npair = GROUPS // 2

    def body(x_ref, t_ref, o_ref, hm_ref, hg_ref, woa_ref, wob_ref, wout_ref, lng_ref, lnb_ref, sgg_ref, sgb_ref,
             ws_ref, bsb_ref,
             dr_ref, dhg_ref, dhm_ref, do_ref, dot_ref, dl_ref, dhgt_ref, dhmt_ref,
             dwout_ref, dwoa_ref, dwob_ref, dws_ref, dbs_ref, dlng_ref, dlnb_ref, dsgg_ref, dsgb_ref, loss_ref,
             dbg_ref, dbm_ref, dbacc_ref, awout_ref, awoa_ref, awob_ref):
        i = pl.program_id(0)

        @pl.when(i == 0)
        def _():
            for r in (awout_ref, awoa_ref, awob_ref, dws_ref, dlng_ref, dlnb_ref, dsgg_ref, dsgb_ref, loss_ref,
                      dbg_ref, dbm_ref, dbacc_ref):
                r[...] = jnp.zeros_like(r)

        def emit(ref, tref, bref, lo, val):
            vb = val.astype(BF16)
            n = val.shape[1]
            ref[:, lo:lo + n] = vb
            tref[lo:lo + n, :] = vb.T
            bref[:, lo:lo + n] += jnp.sum(val, axis=0, keepdims=True)

        lane = lax.broadcasted_iota(jnp.int32, (CHUNK, CHUNK), 1)
        left = lane < V_DIM
        tril = lax.broadcasted_iota(jnp.int32, (CHUNK, CHUNK), 0) >= lane
        ms = [jnp.where(tril, ws_ref[g], 0.0).astype(BF16) for g in range(GROUPS)]

        z_a = hm_ref[:, 0:SGU_W].astype(F32)
        u = hm_ref[:, SGU_W:2 * SGU_W].astype(F32)
        v = hm_ref[:, 2 * SGU_W:3 * SGU_W].astype(F32)
        z_b = hm_ref[:, 3 * SGU_W:4 * SGU_W].astype(F32)
        o = o_ref[...]
        sa, dsa = _silu_and_grad(z_a)
        y_a = (o * sa).astype(BF16)
        gu, dgu = _gelu_and_grad(u)
        gv, dgv = _gelu_and_grad(v)
        mu = jnp.mean(gv, axis=-1, keepdims=True)
        vc = gv - mu
        rstd_v = lax.rsqrt(jnp.mean(vc * vc, axis=-1, keepdims=True) + LN_EPS)
        vhat = vc * rstd_v
        vn = (vhat * sgg_ref[...] + sgb_ref[...]).astype(BF16)
        rows = []
        for c in range(nch):
            blocks = []
            for p in range(npair):
                blk = vn[c * CHUNK:(c + 1) * CHUNK, p * CHUNK:(p + 1) * CHUNK]
                blocks.append(jnp.where(left, _dot(ms[2 * p], blk), _dot(ms[2 * p + 1], blk)))
            rows.append(jnp.concatenate(blocks, axis=1) + bsb_ref[...])
        mixed = jnp.concatenate(rows, axis=0)
        sgu = gu * mixed
        sb, dsb = _silu_and_grad(z_b)
        y_b = (sgu * sb).astype(BF16)
        pa = jnp.concatenate([_dot(y_a, woa_ref[k]) for k in range(N_SLABS)], axis=1)
        pb = jnp.concatenate([_dot(y_b, wob_ref[k]) for k in range(N_SLABS)], axis=1)
        sga = _sigmoid(hg_ref[:, :D_MODEL].astype(F32))
        sgb = _sigmoid(hg_ref[:, D_MODEL:].astype(F32))
        m2 = (sga * pa + sgb * pb).astype(BF16)
        r = ALPHA * x_ref[...] + _dot(m2, wout_ref[...])
        rmu = jnp.mean(r, axis=-1, keepdims=True)
        rc = r - rmu
        rstd = lax.rsqrt(jnp.mean(rc * rc, axis=-1, keepdims=True) + LN_EPS)
        xhat = rc * rstd
        y = xhat * lng_ref[...] + lnb_ref[...]
        err = y - t_ref[...]
        loss_ref[...] += jnp.full(loss_ref.shape, 0.5 / D_MODEL, F32) * jnp.sum(err * err)

        dy = err * (1.0 / D_MODEL)
        dlng_ref[...] += jnp.sum(dy * xhat, axis=0, keepdims=True)
        dlnb_ref[...] += jnp.sum(dy, axis=0, keepdims=True)
        dxh = dy * lng_ref[...]
        dr = rstd * (dxh - jnp.mean(dxh, axis=-1, keepdims=True) - xhat * jnp.mean(dxh * xhat, axis=-1, keepdims=True))
        dr_ref[...] = dr
        drb = dr.astype(BF16)
        awout_ref[...] += _dot_tn(m2, drb)
        dm2 = _dot_nt(drb, wout_ref[...])
        emit(dhg_ref, dhgt_ref, dbg_ref, 0, dm2 * pa * sga * (1.0 - sga))
        emit(dhg_ref, dhgt_ref, dbg_ref, D_MODEL, dm2 * pb * sgb * (1.0 - sgb))
        dpa = (dm2 * sga).astype(BF16)
        dpb = (dm2 * sgb).astype(BF16)
        dy_a = jnp.zeros((ts, MLA_W), F32)
        dy_b = jnp.zeros((ts, SGU_W), F32)
        y_at, y_bt = y_a.T, y_b.T
        for k in range(N_SLABS):
            cols = slice(k * SLAB_W, (k + 1) * SLAB_W)
            awoa_ref[k] += _dot(y_at, dpa[:, cols])
            awob_ref[k] += _dot(y_bt, dpb[:, cols])
            dy_a = dy_a + _dot_nt(dpa[:, cols], woa_ref[k])
            dy_b = dy_b + _dot_nt(dpb[:, cols], wob_ref[k])
        dob = (dy_a * sa).astype(BF16)
        do_ref[...] = dob
        dot_ref[...] = dob.T
        head = (lax.broadcasted_iota(jnp.int32, (HEADS, MLA_W), 1) // V_DIM
                == lax.broadcasted_iota(jnp.int32, (HEADS, MLA_W), 0)).astype(BF16)
        dl = sum(_dot_nt(head, term) for term in _split3(dob.astype(F32) * o))
        for p in range(HEADS // 2):
            dl_ref[p] = jnp.concatenate([dl[2 * p:2 * p + 2], jnp.zeros((6, ts), F32)], axis=0)
        emit(dhm_ref, dhmt_ref, dbm_ref, 0, dy_a * o * dsa)
        dsg = dy_b * sb
        emit(dhm_ref, dhmt_ref, dbm_ref, 3 * SGU_W, dy_b * sgu * dsb)
        emit(dhm_ref, dhmt_ref, dbm_ref, SGU_W, dsg * mixed * dgu)
        dmixed = dsg * gu
        dvn_rows = []
        dbs_sum = jnp.zeros((CHUNK, SGU_W), F32)
        for c in range(nch):
            dm_c = dmixed[c * CHUNK:(c + 1) * CHUNK, :]
            dbs_sum = dbs_sum + dm_c
            blocks = []
            for p in range(npair):
                dmb = dm_c[:, p * CHUNK:(p + 1) * CHUNK].astype(BF16)
                blk = vn[c * CHUNK:(c + 1) * CHUNK, p * CHUNK:(p + 1) * CHUNK]
                blocks.append(jnp.where(left, _dot_tn(ms[2 * p], dmb), _dot_tn(ms[2 * p + 1], dmb)))
                zero = jnp.zeros_like(dmb)
                dws_ref[2 * p] += jnp.where(tril, _dot_nt(jnp.where(left, dmb, zero), blk), 0.0)
                dws_ref[2 * p + 1] += jnp.where(tril, _dot_nt(jnp.where(left, zero, dmb), blk), 0.0)
            dvn_rows.append(jnp.concatenate(blocks, axis=1))
        dbacc_ref[...] += dbs_sum
        dvn = jnp.concatenate(dvn_rows, axis=0)
        dsgg_ref[...] += jnp.sum(dvn * vhat, axis=0, keepdims=True)
        dsgb_ref[...] += jnp.sum(dvn, axis=0, keepdims=True)
        dvh = dvn * sgg_ref[...]
        dgv_in = rstd_v * (dvh - jnp.mean(dvh, axis=-1, keepdims=True)
                           - vhat * jnp.mean(dvh * vhat, axis=-1, keepdims=True))
        emit(dhm_ref, dhmt_ref, dbm_ref, 2 * SGU_W, dgv_in * dgv)

        @pl.when(i == nsteps - 1)
        def _():
            dwout_ref[...] = awout_ref[...].astype(BF16)
            dwoa_ref[...] = awoa_ref[...].astype(BF16)
            dwob_ref[...] = awob_ref[...].astype(BF16)
            grp = (lax.broadcasted_iota(jnp.int32, (SGU_W, CHUNK), 0) // V_DIM
                   == lax.broadcasted_iota(jnp.int32, (SGU_W, CHUNK), 1)).astype(BF16)
            hi, mid, lo = _split3(dbacc_ref[...])
            dbs_ref[...] = _dot(hi, grp) + _dot(mid, grp) + _dot(lo, grp)

    acc_shapes = [(D_MODEL, D_MODEL), woa.shape, wob.shape, (GROUPS, CHUNK, CHUNK), (CHUNK, CHUNK),
                  (1, D_MODEL), (1, D_MODEL), (1, SGU_W), (1, SGU_W), (1, 128), (1, GATE_W), (1, MID_W)]
    col_spec = lambda rows: pl.BlockSpec((rows, ts), lambda i: (0, i))
    return pl.pallas_call(
        body, name="mid", grid=(nsteps,),
        in_specs=[_row_spec(ts, D_MODEL), _row_spec(ts, D_MODEL), _row_spec(ts, MLA_W), _row_spec(ts, MID_W),
                  _row_spec(ts, GATE_W), _full_spec(woa.shape), _full_spec(wob.shape), _full_spec(wout.shape),
                  _full_spec(ln_g.shape), _full_spec(ln_b.shape), _full_spec(sg_g.shape), _full_spec(sg_b.shape),
                  _full_spec(w_s.shape), _full_spec(bsb.shape)],
        out_specs=[_row_spec(ts, D_MODEL), _row_spec(ts, GATE_W), _row_spec(ts, MID_W), _row_spec(ts, MLA_W),
                   col_spec(MLA_W), pl.BlockSpec((HEADS // 2, 8, ts), lambda i: (0, 0, i)), col_spec(GATE_W),
                   col_spec(MID_W)]
        + [_full_spec(sh) for sh in acc_shapes],
        out_shape=[jax.ShapeDtypeStruct((s, D_MODEL), F32), jax.ShapeDtypeStruct((s, GATE_W), BF16),
                   jax.ShapeDtypeStruct((s, MID_W), BF16), jax.ShapeDtypeStruct((s, MLA_W), BF16),
                   jax.ShapeDtypeStruct((MLA_W, s), BF16), jax.ShapeDtypeStruct((HEADS // 2, 8, s), F32),
                   jax.ShapeDtypeStruct((GATE_W, s), BF16), jax.ShapeDtypeStruct((MID_W, s), BF16)]
        + [jax.ShapeDtypeStruct(sh, BF16 if n < 3 else F32) for n, sh in enumerate(acc_shapes)],
        scratch_shapes=[pltpu.VMEM((CHUNK, SGU_W), F32)] + [pltpu.VMEM(sh, F32) for sh in acc_shapes[:3]],
        compiler_params=_params(),
    )(x, tgt, o, hm, hg, woa, wob, wout, ln_g, ln_b, sg_g, sg_b, w_s, bsb)


def _lat_bwd(dq, dk, dv, hl, rc, rsl, rsh, g_q, g_kv, wuq, wk, wv, after):
    s = dk.shape[0]
    ts = ROW_TILE
    qk_w = HEADS * HEAD_PAD

    def body(dq_ref, dk_ref, dv_ref, hl_ref, rc_ref, rsl_ref, rsh_ref, gq_ref, gkv_ref, wuq_ref, wk_ref, wv_ref,
             after_ref, dhl_ref, dhlt_ref, dwuq_ref, dwk_ref, dwv_ref, dgq_ref, dgkv_ref, dbl_ref):
        i = pl.program_id(0)

        @pl.when(i == 0)
        def _():
            for r in (dwuq_ref, dwk_ref, dwv_ref, dgq_ref, dgkv_ref, dbl_ref):
                r[...] = jnp.zeros_like(r)

        def emit(lo, val):
            vb = val.astype(BF16)
            n = val.shape[1]
            dhl_ref[:, lo:lo + n] = vb
            dhlt_ref[lo:lo + n, :] = vb.T
            dbl_ref[:, lo:lo + n] += jnp.sum(val, axis=0, keepdims=True)

        c, sl, sh = rc_ref[...], rsl_ref[...], rsh_ref[...]
        lane = lax.broadcasted_iota(jnp.int32, (ts, HEAD_PAD), 1)
        pe = (lane >= NOPE) & (lane < QK_DIM)
        dkpe = jnp.zeros((ts, HEAD_PAD), F32)
        dqu = []
        for hd in range(HEADS):
            lanes = slice(hd * HEAD_PAD, (hd + 1) * HEAD_PAD)
            dqu.append(_rope_t(dq_ref[lanes, :].T, c, sl, sh).astype(BF16))
            dkpe = dkpe + dk_ref[:, lanes]
        dqu = jnp.concatenate(dqu, axis=1)
        dkpe = _rope_t(jnp.where(pe, dkpe, 0.0), c, sl, sh)

        cq = hl_ref[:, :Q_RANK]
        rq = lax.rsqrt(jnp.mean(cq * cq, axis=-1, keepdims=True) + RMS_EPS)
        cqh = cq * rq
        cqn = (cqh * gq_ref[...]).astype(BF16)
        dwuq_ref[...] += _dot_tn(cqn, dqu)
        dcqn = _dot_nt(dqu, wuq_ref[...])
        dgq_ref[...] += jnp.sum(dcqn * cqh, axis=0, keepdims=True)
        dch = dcqn * gq_ref[...]
        emit(0, rq * (dch - cqh * jnp.mean(dch * cqh, axis=-1, keepdims=True)))

        ckv = hl_ref[:, Q_RANK:Q_RANK + KV_RANK]
        rk = lax.rsqrt(jnp.mean(ckv * ckv, axis=-1, keepdims=True) + RMS_EPS)
        ckh = ckv * rk
        ckn = (ckh * gkv_ref[...]).astype(BF16)
        dkb = dk_ref[...].astype(BF16)
        dvb = dv_ref[...].astype(BF16)
        dwk_ref[...] += _dot_tn(ckn, dkb)
        dwv_ref[...] += _dot_tn(ckn, dvb)
        dckn = _dot_nt(dkb, wk_ref[...]) + _dot_nt(dvb, wv_ref[...])
        dgkv_ref[...] += jnp.sum(dckn * ckh, axis=0, keepdims=True)
        dkh = dckn * gkv_ref[...]
        emit(Q_RANK, rk * (dkh - ckh * jnp.mean(dkh * ckh, axis=-1, keepdims=True)))
        emit(Q_RANK + KV_RANK, dkpe)

    acc_shapes = [wuq.shape, wk.shape, wv.shape, g_q.shape, g_kv.shape, (1, LAT_W)]
    return pl.pallas_call(
        body, name="lat_bwd", grid=(s // ts,),
        in_specs=[pl.BlockSpec((qk_w, ts), lambda i: (0, i)), _row_spec(ts, qk_w), _row_spec(ts, MLA_W),
                  _row_spec(ts, LAT_W), _row_spec(ts, HEAD_PAD), _row_spec(ts, HEAD_PAD), _row_spec(ts, HEAD_PAD),
                  _full_spec(g_q.shape), _full_spec(g_kv.shape), _full_spec(wuq.shape), _full_spec(wk.shape),
                  _full_spec(wv.shape), pl.BlockSpec(memory_space=pl.ANY)],
        out_specs=[_row_spec(ts, LAT_W), pl.BlockSpec((LAT_W, ts), lambda i: (0, i))]
        + [_full_spec(sh) for sh in acc_shapes],
        out_shape=[jax.ShapeDtypeStruct((s, LAT_W), BF16), jax.ShapeDtypeStruct((LAT_W, s), BF16)]
        + [jax.ShapeDtypeStruct(sh, F32) for sh in acc_shapes],
        compiler_params=_params(),
    )(dq, dk, dv, hl, rc, rsl, rsh, g_q, g_kv, wuq, wk, wv, after)


def _dx(dr, dhg, dhm, dhl, wt, wlat, after):
    s = dr.shape[0]
    ts = MATMUL_ROW_TILE

    tk = D_MODEL
    bounds = ([(ROW_GATE + r0, ROW_GATE + r0 + tk) for r0 in range(0, GATE_W, tk)]
              + [(LAT_COLS + r0, LAT_COLS + r0 + tk) for r0 in range(0, MID_W, tk)])

    def body(dr_ref, dhg_ref, dhm_ref, dhl_ref, wt_hbm, wlat_ref, after_ref, dx_ref, wt_ref, sems):
        copies = [pltpu.make_async_copy(wt_hbm.at[lo:hi], wt_ref.at[lo:hi], sems.at[n])
                  for n, (lo, hi) in enumerate(bounds)]

        def compute(first):
            acc = (ALPHA * dr_ref[...] + _dot(dhl_ref[:, 0:Q_RANK + KV_RANK], wlat_ref[0:Q_RANK + KV_RANK, :])
                   + _dot(dhl_ref[:, Q_RANK + KV_RANK:], _kpe_rows(wlat_ref)))
            for n, (lo, hi) in enumerate(bounds):
                if first:
                    copies[n].wait()
                if lo >= ROW_GATE:
                    acc += _dot(dhg_ref[:, lo - ROW_GATE:hi - ROW_GATE], wt_ref[lo:hi, :])
                else:
                    acc += _dot(dhm_ref[:, lo - LAT_COLS:hi - LAT_COLS], wt_ref[lo:hi, :])
            dx_ref[...] = acc

        @pl.when(pl.program_id(0) == 0)
        def _():
            for cp in copies:
                cp.start()
            compute(True)

        @pl.when(pl.program_id(0) > 0)
        def _():
            compute(False)

    return pl.pallas_call(
        body, name="dx", grid=(s // ts,),
        in_specs=[_row_spec(ts, D_MODEL), _row_spec(ts, GATE_W), _row_spec(ts, MID_W), _row_spec(ts, LAT_W),
                  HBM_SPEC, _full_spec(wlat.shape), HBM_SPEC],
        out_specs=_row_spec(ts, D_MODEL),
        out_shape=jax.ShapeDtypeStruct((s, D_MODEL), F32),
        scratch_shapes=[pltpu.VMEM(wt.shape, BF16), pltpu.SemaphoreType.DMA((len(bounds),))],
        compiler_params=_params(),
    )(dr, dhg, dhm, dhl, wt, wlat, after)


def _dwt_early(dhmt, dhgt, xb2, col, after):
    tn = 512
    nm, ng = MID_W // tn, GATE_W // tn
    s = dhmt.shape[1]
    hc = D_MODEL // 2

    ks = s // 2

    def body(col_ref, dma_ref, dmb_ref, dga_ref, dgb_ref, xoa_ref, xob_ref, xca_ref, xcb_ref, after_ref,
             dwo_ref, dwc_ref):
        i = pl.program_id(0)

        def both(da_ref, db_ref):
            da, db = da_ref[...], db_ref[...]
            dwo_ref[...] = (_dot(da, xoa_ref[...]) + _dot(db, xob_ref[...])).astype(BF16)
            dwc_ref[...] = (_dot(da, xca_ref[...]) + _dot(db, xcb_ref[...])).astype(BF16)

        @pl.when(i < nm)
        def _():
            both(dma_ref, dmb_ref)

        @pl.when(i >= nm)
        def _():
            both(dga_ref, dgb_ref)

    def dh_spec(first, part):
        if first:
            return pl.BlockSpec((tn, ks), lambda i, col_ref: (jnp.minimum(i, nm - 1), part))
        return pl.BlockSpec((tn, ks), lambda i, col_ref: (jnp.maximum(i - nm, 0), part))

    def x_spec(other, part):
        if other:
            return pl.BlockSpec((None, ks, hc), lambda i, col_ref: (1 - col_ref[0], part, 0))
        return pl.BlockSpec((None, ks, hc), lambda i, col_ref: (col_ref[0], part, 0))

    out_spec = pl.BlockSpec((pl.Element(tn), pl.Element(hc)),
                            lambda i, col_ref: (pl.multiple_of(LAT_COLS + i * tn, 32), 0))
    rows = pl.pallas_call(
        body, name="dwt_early",
        grid_spec=pltpu.PrefetchScalarGridSpec(
            num_scalar_prefetch=1, grid=(nm + ng,),
            in_specs=[dh_spec(True, 0), dh_spec(True, 1), dh_spec(False, 0), dh_spec(False, 1),
                      x_spec(True, 0), x_spec(True, 1), x_spec(False, 0), x_spec(False, 1),
                      pl.BlockSpec(memory_space=pl.ANY)],
            out_specs=[out_spec, out_spec]),
        out_shape=[jax.ShapeDtypeStruct((IN_W, hc), BF16)] * 2,
        compiler_params=_params(),
    )(col, dhmt, dhmt, dhgt, dhgt, xb2, xb2, xb2, xb2, after)

    def zero(other_ref, mine_ref, other_out, mine_out):
        other_out[...] = jnp.zeros_like(other_out)
        mine_out[...] = jnp.zeros_like(mine_out)

    lat_rows = pl.BlockSpec((LAT_COLS, hc), lambda i: (0, 0))
    return pl.pallas_call(
        zero, name="dwt_early_zero_lat", grid=(1,), in_specs=[pl.BlockSpec(memory_space=pl.ANY)] * 2,
        out_specs=[lat_rows, lat_rows],
        out_shape=[jax.ShapeDtypeStruct((IN_W, hc), BF16)] * 2, input_output_aliases={0: 0, 1: 1},
    )(*rows)


def _dwt_lat(dhlt, xb2):
    n, s = dhlt.shape
    tk = MATMUL_ROW_TILE
    nsteps = s // tk

    def body(dht_ref, xb_ref, dw_ref, acc_ref):
        i = pl.program_id(0)

        @pl.when(i == 0)
        def _():
            acc_ref[...] = jnp.zeros_like(acc_ref)

        dht = dht_ref[...]
        acc_ref[...] += jnp.concatenate([_dot(dht, xb_ref[0]), _dot(dht, xb_ref[1])], axis=1)

        @pl.when(i == nsteps - 1)
        def _():
            kpe = Q_RANK + KV_RANK + NOPE
            dw_ref[0:Q_RANK + KV_RANK, :] = acc_ref[0:Q_RANK + KV_RANK, :].astype(BF16)
            dw_ref[Q_RANK + KV_RANK:LAT_COLS, :] = acc_ref[kpe:kpe + ROPE, :].astype(BF16)
            dw_ref[LAT_COLS:, :] = jnp.zeros((LAT_ROWS_PAD - LAT_COLS, D_MODEL), BF16)

    return pl.pallas_call(
        body, name="dwt_lat", grid=(nsteps,),
        in_specs=[pl.BlockSpec((n, tk), lambda i: (0, i)), pl.BlockSpec((2, tk, D_MODEL // 2), lambda i: (0, i, 0))],
        out_specs=_full_spec((LAT_ROWS_PAD, D_MODEL)),
        out_shape=jax.ShapeDtypeStruct((LAT_ROWS_PAD, D_MODEL), BF16),
        scratch_shapes=[pltpu.VMEM((n, D_MODEL), F32)],
        compiler_params=_params(),
    )(dhlt, xb2)


def _split_bias(b):
    z = lambda n: jnp.zeros((n,), b.dtype)
    lat = jnp.concatenate([b[:Q_RANK + KV_RANK], z(NOPE), b[Q_RANK + KV_RANK:LAT_COLS], z(HEAD_PAD - QK_DIM)])
    return b[None, ROW_GATE:], b[None, LAT_COLS:ROW_GATE], lat[None, :]


def _join_bias(g, m, l):
    kpe = Q_RANK + KV_RANK + NOPE
    return jnp.concatenate([l[0, :Q_RANK + KV_RANK], l[0, kpe:kpe + ROPE], m[0], g[0]])


def _rope_tables(positions):
    half = ROPE // 2
    inv_freq = ROPE_THETA ** (-jnp.arange(0, ROPE, 2, dtype=F32) / ROPE)
    ang = positions.astype(F32)[:, None] * inv_freq
    cos, sin = jnp.cos(ang), jnp.sin(ang)
    n = positions.shape[0]
    one, zero = jnp.ones((n, NOPE), F32), jnp.zeros((n, half), F32)
    tail1, tail0 = jnp.ones((n, HEAD_PAD - QK_DIM), F32), jnp.zeros((n, HEAD_PAD - QK_DIM), F32)
    z64 = jnp.zeros((n, NOPE), F32)
    rc = jnp.concatenate([one, cos, cos, tail1], axis=1)
    rsl = jnp.concatenate([z64, -sin, zero, tail0], axis=1)
    rsh = jnp.concatenate([z64, zero, sin, tail0], axis=1)
    return rc, rsl, rsh


def _pad_heads(w_uq):
    return jnp.pad(w_uq, ((0, 0), (0, 0), (0, HEAD_PAD - QK_DIM))).reshape(w_uq.shape[0], HEADS * HEAD_PAD)


def _prep_local(b_in, g_q, g_kv, w_ukv):
    b_g, b_m, b_l = _split_bias(b_in)
    wk = jnp.pad(w_ukv[:, :, :NOPE], ((0, 0), (0, 0), (0, HEAD_PAD - NOPE))).reshape(KV_RANK, HEADS * HEAD_PAD).astype(BF16)
    wv = w_ukv[:, :, NOPE:].reshape(KV_RANK, MLA_W).astype(BF16)
    return dict(b_g=b_g, b_m=b_m, b_l=b_l, wk=wk, wv=wv, gq2=g_q[None, :], gkv2=g_kv[None, :])


def _local_attention(x, tables, wlat, prep, wuq, after):
    rc, rsl, rsh = tables
    b_g, b_m, b_l, wk, wv, gq2, gkv2 = (prep[n] for n in ("b_g", "b_m", "b_l", "wk", "wv", "gq2", "gkv2"))
    hl, q, k, v, qt, kt, vt, xb2 = _fwd_lat(x, wlat, b_l, gq2, wuq, gkv2, wk, wv, rc, rsl, rsh, after)
    o, lse = _attn_fwd(qt, k, vt)
    return dict(q=q, qt=qt, k=k, kt=kt, v=v, o=o, lse=lse, hl=hl, rc=rc, rsl=rsl, rsh=rsh, gq2=gq2, gkv2=gkv2,
                wuq=wuq, wk=wk, wv=wv, xb2=xb2, b_g=b_g, b_m=b_m, wlat=wlat)


def _bias_lanes(b_s):
    return jnp.repeat(b_s.T, V_DIM, axis=1)


def _local_head(st, x, tgt, wt, w_oa, sg_g, sg_b, w_s, bsb, w_ob, w_out, ln_g, ln_b):
    q, qt, k, kt, v, o, lse, hl = (st[n] for n in ("q", "qt", "k", "kt", "v", "o", "lse", "hl"))
    rc, rsl, rsh, gq2, gkv2, wuq, wk, wv = (st[n] for n in ("rc", "rsl", "rsh", "gq2", "gkv2", "wuq", "wk", "wv"))
    hg, hm = _fwd_rest(st["xb2"], wt, st["b_g"], st["b_m"])
    (dr, dhg, dhm, do, dot, delta, dhgt, dhmt, dwout, dwoa, dwob, dws, dbs, dlng, dlnb, dsgg, dsgb, loss, dbg,
     dbm) = _mid(x, tgt, o, hm, hg, w_oa, w_ob, w_out, ln_g[None, :], ln_b[None, :], sg_g[None, :], sg_b[None, :],
                 w_s, bsb)
    early = {
        "w_oa": dwoa, "sgu_ln_g": dsgg[0], "sgu_ln_b": dsgb[0], "w_s": dws, "b_s": dbs[:, :GROUPS].T,
        "w_ob": dwob, "w_out": dwout, "ln_g": dlng[0], "ln_b": dlnb[0],
    }
    state = dict(q=q, qt=qt, k=k, kt=kt, v=v, do=do, dot=dot, lse=lse, delta=delta, hl=hl, rc=rc, rsl=rsl, rsh=rsh,
                 gq2=gq2, gkv2=gkv2, wuq=wuq, wk=wk, wv=wv, dr=dr, dhg=dhg, dhm=dhm, wt=wt, dbg=dbg, dbm=dbm,
                 dhgt=dhgt, dhmt=dhmt, xb2=st["xb2"], wlat=st["wlat"])
    return loss, early, state


def _local_attn_bwd(st, after, first_pair=0, n_pairs=HEADS // 2, into=(), name="attn_bwd"):
    return _attn_bwd(st["q"], st["qt"], st["k"], st["kt"], st["v"], st["do"], st["dot"], st["lse"], st["delta"],
                     after, first_pair, n_pairs, into, name)


def _local_tail(st, dq, dk, dv, after):
    dhl, dhlt, dwuq, dwk, dwv, dgq, dgkv, dbl = _lat_bwd(dq, dk, dv, st["hl"], st["rc"], st["rsl"], st["rsh"],
                                                         st["gq2"], st["gkv2"], st["wuq"], st["wk"], st["wv"], after)
    late = {
        "w_lat": _dwt_lat(dhlt, st["xb2"]),
        "b_in": _join_bias(st["dbg"], st["dbm"], dbl),
        "g_q": dgq[0],
        "w_uq": dwuq.reshape(Q_RANK, HEADS, HEAD_PAD)[:, :, :QK_DIM],
        "g_kv": dgkv[0],
        "w_ukv": jnp.concatenate([dwk.reshape(KV_RANK, HEADS, HEAD_PAD)[:, :, :NOPE],
                                  dwv.reshape(KV_RANK, HEADS, V_DIM)], axis=2),
    }
    return dhl, late


def _local_step(x, positions, tgt, wt, b_in, g_q, w_uq, g_kv, w_ukv, w_oa, sg_g, sg_b, w_s, b_s, w_ob, w_out, ln_g,
                ln_b):
    st = _local_attention(x, _rope_tables(positions), wt[:LAT_COLS], _prep_local(b_in, g_q, g_kv, w_ukv),
                          _pad_heads(w_uq).astype(BF16), b_in)
    loss, early, st = _local_head(st, x, tgt, wt, w_oa, sg_g, sg_b, w_s, _bias_lanes(b_s), w_ob, w_out, ln_g, ln_b)
    dq, dk, dv = _local_attn_bwd(st, loss)
    dhl, late = _local_tail(st, dq, dk, dv, dv)
    dx = _dx(st["dr"], st["dhg"], st["dhm"], dhl, st["wt"], st["wlat"], dhl)
    grads = {**early, **late}
    right, left = _dwt_early(st["dhmt"], st["dhgt"], st["xb2"], jnp.zeros((1,), jnp.int32), dhl)
    grads["w_in"] = jnp.concatenate([grads.pop("w_lat")[:LAT_COLS], jnp.concatenate([left, right], axis=1)[LAT_COLS:]],
                                    axis=0)
    return loss, dx, grads


MESH = pl.DeviceIdType.MESH
N_CHIPS = 4
HBM_SPEC = pl.BlockSpec(memory_space=pl.ANY)
HBM_SPEC_STRICT = pl.BlockSpec(memory_space=pltpu.HBM)
VMEM_SPEC = pl.BlockSpec(memory_space=pltpu.VMEM)

REP_ROWS = 80


def _rows8(a):
    flat = a.reshape(-1)
    n = -(-flat.shape[0] // (8 * D_MODEL)) * 8 * D_MODEL
    return jnp.pad(flat, (0, n - flat.shape[0])).reshape(-1, D_MODEL)


def _place():
    x, y, c = lax.axis_index("x"), lax.axis_index("y"), lax.axis_index("c")
    others = [(1 - x, y), (x, 1 - y), (1 - x, 1 - y)]
    return x, y, c, others


N_DEV = 8
LOSS_TILE = (8, 128)


def _cast_own(shards, me, after):
    n = len(shards)

    def body(me_ref, *refs):
        for w in range(n):
            refs[n + 1 + w][...] = refs[w][...].astype(BF16)

    return pl.pallas_call(
        body, name="cast_own",
        grid_spec=pltpu.PrefetchScalarGridSpec(
            num_scalar_prefetch=1, grid=(1,),
            in_specs=[pl.BlockSpec(s.shape, lambda i, me_ref: (0, 0)) for s in shards]
            + [pl.BlockSpec(memory_space=pl.ANY)],
            out_specs=[pl.BlockSpec((None,) + s.shape, lambda i, me_ref: (me_ref[0], 0, 0)) for s in shards]),
        out_shape=[jax.ShapeDtypeStruct((N_CHIPS,) + s.shape, BF16) for s in shards],
        compiler_params=pltpu.CompilerParams(vmem_limit_bytes=VMEM_LIMIT),
    )(me, *shards, after)


def _cast_first(lat, uq, me):
    def body(me_ref, lat_ref, uq_ref, wlat_ref, guq_ref):
        wlat_ref[...] = lat_ref[...].astype(BF16)
        guq_ref[...] = uq_ref[...].astype(BF16)

    return pl.pallas_call(
        body, name="cast_first",
        grid_spec=pltpu.PrefetchScalarGridSpec(
            num_scalar_prefetch=1, grid=(1,),
            in_specs=[pl.BlockSpec((LAT_COLS, D_MODEL), lambda i, me_ref: (0, 0)),
                      pl.BlockSpec(uq.shape, lambda i, me_ref: (0, 0))],
            out_specs=[pl.BlockSpec((LAT_COLS, D_MODEL), lambda i, me_ref: (0, 0)),
                       pl.BlockSpec((None,) + uq.shape, lambda i, me_ref: (me_ref[0], 0, 0))]),
        out_shape=[jax.ShapeDtypeStruct((LAT_COLS, D_MODEL), BF16),
                   jax.ShapeDtypeStruct((N_CHIPS,) + uq.shape, BF16)],
    )(me, lat, uq)


def _first_copies(wlat_ref, guq_ref, send_sems, recv_sems, shapes):
    x, y, c, others = _place()
    me = 2 * x + y
    hl, hu = shapes[0][1] // 2, shapes[1][2] // 2
    lat_half = wlat_ref.at[:, pl.ds(c * hl, hl)]

    def copy(src, dst, k, to):
        return pltpu.make_async_remote_copy(src_ref=src, dst_ref=dst, send_sem=send_sems.at[k],
                                            recv_sem=recv_sems.at[k], device_id=to, device_id_type=MESH)

    def uq_half(chip):
        return guq_ref.at[chip, :, pl.ds(c * hu, hu)]

    lat_out = [copy(lat_half, lat_half, j, (*others[j], c)) for j in range(3)]
    uq_out = [copy(uq_half(me), uq_half(me), 3 + j, (*others[j], c)) for j in range(3)]
    j0 = jnp.maximum(x + 2 * y - 1, 0)
    lat_in = copy(lat_half, lat_half, j0, (0, 0, c))
    uq_in = [copy(uq_half(me), uq_half(2 * px + py), 3 + j, (px, py, c)) for j, (px, py) in enumerate(others)]
    return me, lat_out, uq_out, lat_in, uq_in


def _first_start(wlat, guq):
    shapes = (wlat.shape, guq.shape)

    def body(wlat_ref, guq_ref, send_sems, recv_sems, wlat_thru, guq_thru, token):
        me, lat_out, uq_out, _, _ = _first_copies(wlat_ref, guq_ref, send_sems, recv_sems, shapes)

        @pl.when(me == 0)
        def _():
            for cp in lat_out:
                cp.start()

        for cp in uq_out:
            cp.start()
        token[...] = jnp.zeros_like(token)

    outs = pl.pallas_call(
        body, name="first_start",
        out_shape=(pltpu.SemaphoreType.DMA((6,)), pltpu.SemaphoreType.DMA((6,)), pltpu.HBM(wlat.shape, BF16),
                   pltpu.HBM(guq.shape, BF16), jax.ShapeDtypeStruct(LOSS_TILE, F32)),
        in_specs=[HBM_SPEC_STRICT] * 2, out_specs=(SEM_SPEC, SEM_SPEC, HBM_SPEC_STRICT, HBM_SPEC_STRICT, VMEM_SPEC),
        input_output_aliases={0: 2, 1: 3},
        compiler_params=pltpu.CompilerParams(has_side_effects=SPLIT_EFFECT),
    )(pltpu.with_memory_space_constraint(wlat, pltpu.HBM), pltpu.with_memory_space_constraint(guq, pltpu.HBM))
    return outs


def _first_wait(send_sems, recv_sems, wlat, guq, *after):
    shapes = (wlat.shape, guq.shape)

    def body(wlat_ref, guq_ref, send_sems, recv_sems, *rest):
        me, lat_out, uq_out, lat_in, uq_in = _first_copies(wlat_ref, guq_ref, send_sems, recv_sems, shapes)

        @pl.when(me == 0)
        def _():
            for cp in lat_out:
                cp.wait_send()

        @pl.when(me != 0)
        def _():
            lat_in.wait_recv()

        for cp in uq_out:
            cp.wait_send()
        for cp in uq_in:
            cp.wait_recv()

    return pl.pallas_call(
        body, name="first_wait", out_shape=(pltpu.HBM(wlat.shape, BF16), pltpu.HBM(guq.shape, BF16)),
        in_specs=[HBM_SPEC_STRICT, HBM_SPEC_STRICT, SEM_SPEC, SEM_SPEC] + [HBM_SPEC] * len(after),
        out_specs=(HBM_SPEC_STRICT, HBM_SPEC_STRICT), input_output_aliases={0: 0, 1: 1},
        compiler_params=pltpu.CompilerParams(has_side_effects=SPLIT_EFFECT),
    )(wlat, guq, send_sems, recv_sems, *after)


def _first_forward(wlat, guq):
    hl, hu = wlat.shape[1] // 2, guq.shape[2] // 2

    def body(wlat_in, guq_in, wlat_ref, guq_ref, send_sems, recv_sems):
        x, y, c, others = _place()
        me = 2 * x + y
        sibling = (x, y, 1 - c)

        def copy(part, k):
            return pltpu.make_async_remote_copy(src_ref=part, dst_ref=part, send_sem=send_sems.at[k],
                                                recv_sem=recv_sems.at[k], device_id=sibling, device_id_type=MESH)

        def uq_part(j, half):
            px, py = others[j]
            return guq_ref.at[2 * px + py, :, pl.ds(half * hu, hu)]

        cps = [copy(uq_part(j, c), j) for j in range(3)]
        for cp in cps:
            cp.start()

        @pl.when(me != 0)
        def _():
            mine = copy(wlat_ref.at[:, pl.ds(c * hl, hl)], 3)
            mine.start()
            copy(wlat_ref.at[:, pl.ds((1 - c) * hl, hl)], 3).wait_recv()
            mine.wait_send()

        for j in range(3):
            copy(uq_part(j, 1 - c), j).wait_recv()
        for cp in cps:
            cp.wait_send()

    return pl.pallas_call(
        body, name="first_forward", in_specs=[HBM_SPEC, HBM_SPEC], out_specs=[HBM_SPEC, HBM_SPEC],
        out_shape=[jax.ShapeDtypeStruct(wlat.shape, BF16), jax.ShapeDtypeStruct(guq.shape, BF16)],
        input_output_aliases={0: 0, 1: 1},
        scratch_shapes=[pltpu.SemaphoreType.DMA((4,)), pltpu.SemaphoreType.DMA((4,))],
    )(wlat, guq)


def _gather_copy(b_ref, buf, w, j, src_chip, dst_chip, to, rows, send_sems, recv_sems):
    _, _, c, _ = _place()
    hc = _half(buf)
    return pltpu.make_async_remote_copy(
        src_ref=b_ref.at[src_chip, rows, pl.ds(c * hc, hc)], dst_ref=b_ref.at[dst_chip, rows, pl.ds(c * hc, hc)],
        send_sem=send_sems.at[3 * w + j], recv_sem=recv_sems.at[3 * w + j], device_id=to, device_id_type=MESH)


def _gather_rows(buf, w, chip, fn):
    if w != 0:
        fn(slice(None))
        return
    pl.when(chip == 0)(lambda: fn(pl.ds(LAT_COLS, buf.shape[1] - LAT_COLS)))
    pl.when(chip != 0)(lambda: fn(slice(None)))


def _gather_start(bufs, after):
    n = len(bufs)

    def body(*refs):
        b_refs = refs[:n]
        send_sems, recv_sems, token = refs[n + 1], refs[n + 2], refs[-1]
        x, y, c, others = _place()
        me = 2 * x + y
        for w in range(n):
            def start(rows, w=w):
                for j, (px, py) in enumerate(others):
                    _gather_copy(b_refs[w], bufs[w], w, j, me, me, (px, py, c), rows, send_sems, recv_sems).start()
            _gather_rows(bufs[w], w, me, start)
        token[...] = jnp.zeros_like(token)

    hbm = [pltpu.HBM(b.shape, BF16) for b in bufs]
    outs = pl.pallas_call(
        body, name="gather_start",
        out_shape=(pltpu.SemaphoreType.DMA((3 * n,)), pltpu.SemaphoreType.DMA((3 * n,)), *hbm,
                   jax.ShapeDtypeStruct(LOSS_TILE, F32)),
        in_specs=[HBM_SPEC_STRICT] * n + [HBM_SPEC],
        out_specs=(SEM_SPEC, SEM_SPEC, *[HBM_SPEC_STRICT] * n, VMEM_SPEC),
        input_output_aliases={i: 2 + i for i in range(n)},
        compiler_params=pltpu.CompilerParams(has_side_effects=SPLIT_EFFECT),
    )(*[pltpu.with_memory_space_constraint(b, pltpu.HBM) for b in bufs], after)
    return outs[0], outs[1], list(outs[2:2 + n]), outs[-1]


def _gather_wait(send_sems, recv_sems, bufs, after):
    n = len(bufs)

    def body(*refs):
        b_refs = refs[:n]
        send_sems, recv_sems = refs[n], refs[n + 1]
        x, y, c, others = _place()
        me = 2 * x + y
        for w in range(n):
            for j, (px, py) in enumerate(others):
                def copy(rows, w=w, j=j, px=px, py=py):
                    return _gather_copy(b_refs[w], bufs[w], w, j, me, 2 * px + py, (px, py, c), rows, send_sems,
                                        recv_sems)
                _gather_rows(bufs[w], w, me, lambda rows, copy=copy: copy(rows).wait_send())
                _gather_rows(bufs[w], w, 2 * px + py, lambda rows, copy=copy: copy(rows).wait_recv())

    outs = pl.pallas_call(
        body, name="gather_wait", out_shape=tuple(pltpu.HBM(b.shape, b.dtype) for b in bufs),
        in_specs=[HBM_SPEC_STRICT] * n + [SEM_SPEC, SEM_SPEC, HBM_SPEC],
        out_specs=tuple([HBM_SPEC_STRICT] * n), input_output_aliases={i: i for i in range(n)},
        compiler_params=pltpu.CompilerParams(has_side_effects=SPLIT_EFFECT),
    )(*bufs, send_sems, recv_sems, after)
    return list(outs)


def _gather_finish(bufs):
    n = len(bufs)

    def body(*refs):
        b_refs = refs[n:2 * n]
        send_sems, recv_sems = refs[2 * n:]
        x, y, c, others = _place()
        cps = []
        for w in range(n):
            hc = _half(bufs[w])
            for j, (px, py) in enumerate(others):
                part = b_refs[w].at[2 * px + py, :, pl.ds(c * hc, hc)]
                cps.append(pltpu.make_async_remote_copy(
                    src_ref=part, dst_ref=part, send_sem=send_sems.at[3 * w + j], recv_sem=recv_sems.at[3 * w + j],
                    device_id=(x, y, 1 - c), device_id_type=MESH))
        for cp in cps:
            cp.start()
        for w in range(n):
            hc = _half(bufs[w])
            for j, (px, py) in enumerate(others):
                theirs = b_refs[w].at[2 * px + py, :, pl.ds((1 - c) * hc, hc)]
                pltpu.make_async_remote_copy(
                    src_ref=theirs, dst_ref=theirs, send_sem=send_sems.at[3 * w + j], recv_sem=recv_sems.at[3 * w + j],
                    device_id=(x, y, 1 - c), device_id_type=MESH).wait_recv()
        for cp in cps:
            cp.wait_send()

    return pl.pallas_call(
        body, name="gather_finish", in_specs=[HBM_SPEC] * n, out_specs=[HBM_SPEC] * n,
        out_shape=[jax.ShapeDtypeStruct(b.shape, b.dtype) for b in bufs],
        input_output_aliases={i: i for i in range(n)},
        scratch_shapes=[pltpu.SemaphoreType.DMA((3 * n,)), pltpu.SemaphoreType.DMA((3 * n,))],
    )(*bufs)


def _half(a):
    return a.shape[-1] // 2


def _exchange_pairs(parts, name):
    n = len(parts)

    def body(*refs):
        p_refs, r_refs = refs[:n], refs[n:2 * n]
        send_sems, recv_sems = refs[2 * n:]
        x, y, c, _ = _place()
        cps = []
        for w in range(n):
            h = _half(parts[w])
            cps.append(pltpu.make_async_remote_copy(
                src_ref=p_refs[w].at[:, :, pl.ds((1 - c) * h, h)], dst_ref=r_refs[w],
                send_sem=send_sems.at[w], recv_sem=recv_sems.at[w], device_id=(x, y, 1 - c), device_id_type=MESH))
        for cp in cps:
            cp.start()
        for cp in cps:
            cp.wait()

    return pl.pallas_call(
        body, name=name, in_specs=[HBM_SPEC] * n, out_specs=[HBM_SPEC] * n,
        out_shape=[jax.ShapeDtypeStruct((N_CHIPS, p.shape[1], _half(p)), BF16) for p in parts],
        scratch_shapes=[pltpu.SemaphoreType.DMA((n,)), pltpu.SemaphoreType.DMA((n,))],
    )(*parts)


def _sibling_part(ref, w, n_whole, shape, c):
    if w < n_whole:
        return ref
    h = shape[-1] // 2
    return ref.at[:, :, pl.ds((1 - c) * h, h)]


def _pairs_start(parts, all_loss, n_whole):
    n = len(parts)

    def body(*refs):
        p_refs, r_refs, loss_ref = refs[:n], refs[n:2 * n], refs[2 * n]
        send_sems, recv_sems, token = refs[2 * n + 1], refs[2 * n + 2], refs[-1]
        x, y, c, _ = _place()
        for w in range(n):
            h = _half(parts[w])
            pltpu.make_async_remote_copy(
                src_ref=_sibling_part(p_refs[w], w, n_whole, parts[w].shape, c), dst_ref=r_refs[w],
                send_sem=send_sems.at[w], recv_sem=recv_sems.at[w], device_id=(x, y, 1 - c),
                device_id_type=MESH).start()
        me = 4 * x + 2 * y + c
        for t in range(1, N_DEV):
            d = (me + t) % N_DEV
            pltpu.make_async_remote_copy(
                src_ref=loss_ref.at[me], dst_ref=loss_ref.at[me], send_sem=send_sems.at[n + t - 1],
                recv_sem=recv_sems.at[n + t - 1], device_id=(d // 4, (d // 2) % 2, d % 2), device_id_type=MESH).start()
        token[...] = jnp.zeros_like(token)

    lands = [pltpu.HBM(p.shape if w < n_whole else (N_CHIPS, p.shape[1], _half(p)), BF16)
             for w, p in enumerate(parts)]
    nsem = n + N_DEV - 1
    outs = pl.pallas_call(
        body, name="pairs_start",
        out_shape=(pltpu.SemaphoreType.DMA((nsem,)), pltpu.SemaphoreType.DMA((nsem,)),
                   *[pltpu.HBM(p.shape, p.dtype) for p in parts], *lands, pltpu.HBM(all_loss.shape, F32),
                   jax.ShapeDtypeStruct(LOSS_TILE, F32)),
        in_specs=[HBM_SPEC_STRICT] * (2 * n + 1),
        out_specs=(SEM_SPEC, SEM_SPEC, *[HBM_SPEC_STRICT] * (2 * n + 1), VMEM_SPEC),
        input_output_aliases={i: 2 + i for i in range(2 * n + 1)},
        compiler_params=pltpu.CompilerParams(has_side_effects=SPLIT_EFFECT),
    )(*[pltpu.with_memory_space_constraint(p, pltpu.HBM) for p in parts],
      *[pltpu.with_memory_space_constraint(lax.empty(l.shape, BF16), pltpu.HBM) for l in lands],
      pltpu.with_memory_space_constraint(all_loss, pltpu.HBM))
    return outs[0], outs[1], list(outs[2:2 + n]), list(outs[2 + n:2 + 2 * n]), outs[2 + 2 * n], outs[-1]


def _pairs_wait(send_sems, recv_sems, parts, lands, all_loss, after, n_whole):
    n = len(parts)

    def body(*refs):
        p_refs, r_refs, loss_ref = refs[:n], refs[n:2 * n], refs[2 * n]
        send_sems, recv_sems = refs[2 * n + 1], refs[2 * n + 2]
        x, y, c, _ = _place()
        for w in range(n):
            h = _half(parts[w])
            cp = pltpu.make_async_remote_copy(
                src_ref=_sibling_part(p_refs[w], w, n_whole, parts[w].shape, c), dst_ref=r_refs[w],
                send_sem=send_sems.at[w],
                recv_sem=recv_sems.at[w], device_id=(x, y, 1 - c), device_id_type=MESH)
            cp.wait_send()
            cp.wait_recv()
        me = 4 * x + 2 * y + c
        for t in range(1, N_DEV):
            d = (me + N_DEV - t) % N_DEV
            cp = pltpu.make_async_remote_copy(
                src_ref=loss_ref.at[me], dst_ref=loss_ref.at[d], send_sem=send_sems.at[n + t - 1],
                recv_sem=recv_sems.at[n + t - 1], device_id=(d // 4, (d // 2) % 2, d % 2), device_id_type=MESH)
            cp.wait_send()
            cp.wait_recv()

    bufs = (*parts, *lands, all_loss)
    outs = pl.pallas_call(
        body, name="pairs_wait", out_shape=tuple(pltpu.HBM(a.shape, a.dtype) for a in bufs),
        in_specs=[HBM_SPEC_STRICT] * len(bufs) + [SEM_SPEC, SEM_SPEC, HBM_SPEC],
        out_specs=tuple([HBM_SPEC_STRICT] * len(bufs)), input_output_aliases={i: i for i in range(len(bufs))},
        compiler_params=pltpu.CompilerParams(has_side_effects=SPLIT_EFFECT),
    )(*bufs, send_sems, recv_sems, after)
    return list(outs[:n]), list(outs[n:2 * n]), outs[2 * n]


def _add_pair(ps, rs, c, name, n_whole=0):
    n = len(ps)

    def body(c_ref, *refs):
        for w in range(n):
            refs[2 * n + w][...] = (refs[w][...].astype(F32) + refs[n + w][...].astype(F32)).astype(BF16)

    def slab_spec(a):
        return pl.BlockSpec((None,) + a.shape[1:], lambda k, c_ref: (k, 0, 0))

    def my_half_spec(a):
        return pl.BlockSpec((None, a.shape[1], _half(a)), lambda k, c_ref: (k, 0, c_ref[0]))

    return pl.pallas_call(
        body, name=name,
        grid_spec=pltpu.PrefetchScalarGridSpec(
            num_scalar_prefetch=1, grid=(N_CHIPS,),
            in_specs=[slab_spec(p) if w < n_whole else my_half_spec(p) for w, p in enumerate(ps)]
            + [slab_spec(r) for r in rs],
            out_specs=[slab_spec(r) for r in rs]),
        out_shape=[jax.ShapeDtypeStruct(r.shape, BF16) for r in rs],
        compiler_params=_params(),
    )(c, *ps, *rs)


SEM_SPEC = pl.BlockSpec(memory_space=pltpu.SEMAPHORE)
SPLIT_EFFECT = pltpu.SideEffectType.DATAFLOW_SIDE_EFFECTING


def _chips_start(qs, name):
    n = len(qs)

    def body(*refs):
        q_refs, land_refs = refs[:n], refs[n:2 * n]
        send_sems, recv_sems, token = refs[2 * n], refs[2 * n + 1], refs[-1]
        x, y, c, others = _place()
        me = 2 * x + y
        for w in range(n):
            for j, (px, py) in enumerate(others):
                pltpu.make_async_remote_copy(
                    src_ref=q_refs[w].at[2 * px + py], dst_ref=land_refs[w].at[me], send_sem=send_sems.at[3 * w + j],
                    recv_sem=recv_sems.at[3 * w + j], device_id=(px, py, c), device_id_type=MESH).start()
        token[...] = jnp.zeros_like(token)

    hbm = [pltpu.HBM(q.shape, BF16) for q in qs]
    outs = pl.pallas_call(
        body, name=name,
        out_shape=(pltpu.SemaphoreType.DMA((3 * n,)), pltpu.SemaphoreType.DMA((3 * n,)), *hbm, *hbm,
                   jax.ShapeDtypeStruct(LOSS_TILE, F32)),
        in_specs=[HBM_SPEC_STRICT] * (2 * n),
        out_specs=(SEM_SPEC, SEM_SPEC, *[HBM_SPEC_STRICT] * (2 * n), VMEM_SPEC),
        input_output_aliases={i: 2 + i for i in range(2 * n)},
        compiler_params=pltpu.CompilerParams(has_side_effects=SPLIT_EFFECT),
    )(*[pltpu.with_memory_space_constraint(q, pltpu.HBM) for q in qs],
      *[pltpu.with_memory_space_constraint(lax.empty(q.shape, BF16), pltpu.HBM) for q in qs])
    return outs[0], outs[1], outs[2:2 + n], outs[2 + n:2 + 2 * n], outs[-1]


def _chips_wait(send_sems, recv_sems, q_thru, land_thru, after, name):
    n = len(q_thru)

    def body(*refs):
        q_refs, land_refs = refs[:n], refs[n:2 * n]
        send_sems, recv_sems = refs[2 * n], refs[2 * n + 1]
        x, y, c, others = _place()
        me = 2 * x + y
        for w in range(n):
            for j, (px, py) in enumerate(others):
                cp = pltpu.make_async_remote_copy(
                    src_ref=q_refs[w].at[2 * px + py], dst_ref=land_refs[w].at[2 * px + py],
                    send_sem=send_sems.at[3 * w + j], recv_sem=recv_sems.at[3 * w + j], device_id=(px, py, c),
                    device_id_type=MESH)
                cp.wait_send()
                cp.wait_recv()

    outs = pl.pallas_call(
        body, name=name, out_shape=tuple(pltpu.HBM(a.shape, a.dtype) for a in (*q_thru, *land_thru)),
        in_specs=[HBM_SPEC_STRICT] * (2 * n) + [SEM_SPEC, SEM_SPEC, HBM_SPEC],
        out_specs=tuple([HBM_SPEC_STRICT] * (2 * n)), input_output_aliases={i: i for i in range(2 * n)},
        compiler_params=pltpu.CompilerParams(has_side_effects=SPLIT_EFFECT),
    )(*q_thru, *land_thru, send_sems, recv_sems, after)
    return list(outs[:n]), list(outs[n:])


def _sum_chips(qs, rs, idx, all_dtypes):
    n = len(rs)
    n_all = len(all_dtypes)

    def body(idx_ref, *refs):
        c = idx_ref[4]
        for w in range(n):
            q_ref, g_ref = refs[w], refs[4 * n + w]
            acc = q_ref[...].astype(F32)
            for t in range(1, N_CHIPS):
                acc = acc + refs[n + 3 * w + t - 1][...].astype(F32)
            h = rs[w].shape[2]
            mine = pl.ds(pl.multiple_of(c * h, 128), h)
            g_ref[...] = jnp.zeros_like(g_ref)
            if w >= n - n_all:
                g_ref[idx_ref[0], :, mine] = acc.astype(g_ref.dtype)
            else:
                g_ref[:, mine] = acc

    shapes = [jax.ShapeDtypeStruct((r.shape[1], 2 * r.shape[2]), F32) for r in rs[:n - n_all]]
    shapes += [jax.ShapeDtypeStruct((N_CHIPS, r.shape[1], 2 * r.shape[2]), dt)
               for r, dt in zip(rs[n - n_all:], all_dtypes)]

    def slab_spec(a, t):
        return pl.BlockSpec((None,) + a.shape[1:], lambda i, idx_ref: (idx_ref[t], 0, 0))

    def whole_spec(sh):
        return pl.BlockSpec(sh.shape, lambda i, idx_ref: (0,) * len(sh.shape))

    return pl.pallas_call(
        body, name="sum_chips",
        grid_spec=pltpu.PrefetchScalarGridSpec(
            num_scalar_prefetch=1, grid=(1,),
            in_specs=[slab_spec(q, 0) for q in qs] + [slab_spec(r, t) for r in rs for t in range(1, N_CHIPS)],
            out_specs=[whole_spec(sh) for sh in shapes]),
        out_shape=shapes, compiler_params=_params(),
    )(idx, *qs, *[r for r in rs for _ in range(1, N_CHIPS)])


def _share(shards, alls):
    n, na = len(shards), len(alls)
    total = n + na

    def body(*refs):
        g_refs, a_refs = refs[total:total + n], refs[total + n:2 * total]
        send_sems, recv_sems = refs[2 * total:]
        x, y, c, others = _place()
        me = 2 * x + y
        sibling = (x, y, 1 - c)

        def cols_of(w, half):
            h = shards[w].shape[1] // 2
            return g_refs[w].at[:, pl.ds(half * h, h)]

        def slab(a, chip, half):
            h = alls[a].shape[2] // 2
            return a_refs[a].at[chip, :, pl.ds(half * h, h)]

        def copy(src, dst, k, to):
            return pltpu.make_async_remote_copy(src_ref=src, dst_ref=dst, send_sem=send_sems.at[k],
                                                recv_sem=recv_sems.at[k], device_id=to, device_id_type=MESH)

        cps = []
        for a in range(na):
            base = n + 7 * a
            for j, (px, py) in reversed(list(enumerate(others))):
                cps.append(copy(slab(a, me, c), slab(a, me, c), base + 1 + j, (px, py, c)))
            cps.append(copy(slab(a, me, c), slab(a, me, c), base, sibling))
        cps += [copy(cols_of(w, c), cols_of(w, c), w, sibling) for w in range(n)]
        for cp in cps:
            cp.start()
        fwd = []
        for a in range(na):
            base = n + 7 * a
            for j, (px, py) in enumerate(others):
                chip = 2 * px + py
                copy(slab(a, me, c), slab(a, chip, c), base + 1 + j, (px, py, c)).wait_recv()
                cp = copy(slab(a, chip, c), slab(a, chip, c), base + 4 + j, sibling)
                cp.start()
                fwd.append(cp)
        for a in range(na):
            base = n + 7 * a
            for j, (px, py) in enumerate(others):
                chip = 2 * px + py
                copy(slab(a, chip, c), slab(a, chip, 1 - c), base + 4 + j, sibling).wait_recv()
            copy(slab(a, me, c), slab(a, me, 1 - c), base, sibling).wait_recv()
        for w in range(n):
            copy(cols_of(w, c), cols_of(w, 1 - c), w, sibling).wait_recv()
        for cp in cps + fwd:
            cp.wait_send()

    nsem = n + 7 * na
    return pl.pallas_call(
        body, name="share", in_specs=[HBM_SPEC_STRICT] * total, out_specs=[HBM_SPEC_STRICT] * total,
        out_shape=[pltpu.HBM(a.shape, a.dtype) for a in (*shards, *alls)],
        input_output_aliases={i: i for i in range(total)},
        scratch_shapes=[pltpu.SemaphoreType.DMA((nsem,)), pltpu.SemaphoreType.DMA((nsem,))],
    )(*[pltpu.with_memory_space_constraint(a, pltpu.HBM) for a in (*shards, *alls)])


def _adamw(w, g, m, v):
    m2 = ADAM_B1 * m + (1.0 - ADAM_B1) * g
    v2 = ADAM_B2 * v + (1.0 - ADAM_B2) * (g * g)
    m_hat = m2 / (1.0 - ADAM_B1 ** ADAM_STEP)
    v_hat = v2 / (1.0 - ADAM_B2 ** ADAM_STEP)
    return -ADAM_LR * (m_hat / (jnp.sqrt(v_hat) + ADAM_EPS) + ADAM_WD * w), m2, v2


def _update_w_in(wt, gt, mt, vt, lat, owner, tile):
    nlat = lat.shape[0] // tile

    def body(owner_ref, w_ref, g_ref, m_ref, v_ref, lat_ref, g2_ref, d_ref, m2_ref, v2_ref):
        row = pl.program_id(0) * tile + lax.broadcasted_iota(jnp.int32, (tile, 1), 0)
        g = jnp.where((row < LAT_COLS) & (owner_ref[0] == 1), lat_ref[...].astype(F32), g_ref[...])
        g2_ref[...] = g
        d_ref[...], m2_ref[...], v2_ref[...] = _adamw(w_ref[...], g, m_ref[...], v_ref[...])

    spec = pl.BlockSpec((tile, wt.shape[1]), lambda i, o: (i, 0))
    return pl.pallas_call(
        body, name="update_w_in",
        grid_spec=pltpu.PrefetchScalarGridSpec(
            num_scalar_prefetch=1, grid=(wt.shape[0] // tile,),
            in_specs=[spec] * 4 + [pl.BlockSpec((tile, wt.shape[1]), lambda i, o: (jnp.minimum(i, nlat - 1), 0))],
            out_specs=[spec] * 4),
        out_shape=[jax.ShapeDtypeStruct(wt.shape, F32)] * 4,
        compiler_params=_params(("parallel",)),
    )(owner, wt, gt, mt, vt, lat)


def _update_small(ws, gs, ms, vs):
    n = len(ws)

    def body(*refs):
        for k in range(n):
            w_ref, g_ref, m_ref, v_ref = refs[k], refs[n + k], refs[2 * n + k], refs[3 * n + k]
            d, m2, v2 = _adamw(w_ref[...], g_ref[...], m_ref[...], v_ref[...])
            refs[4 * n + k][...] = d
            refs[5 * n + k][...] = m2
            refs[6 * n + k][...] = v2

    shapes = [jax.ShapeDtypeStruct(w.shape, F32) for w in ws]
    outs = pl.pallas_call(
        body, name="update_small", in_specs=[VMEM_SPEC] * (4 * n), out_specs=[VMEM_SPEC] * (3 * n),
        out_shape=shapes * 3,
        compiler_params=pltpu.CompilerParams(vmem_limit_bytes=VMEM_LIMIT),
    )(*ws, *gs, *ms, *vs)
    return outs[:n], outs[n:2 * n], outs[2 * n:]


REPLICATED = ("b_in", "g_q", "g_kv", "w_ukv", "sgu_ln_g", "sgu_ln_b", "w_s", "b_s", "ln_g", "ln_b")
ORDER = ("w_in", "b_in", "g_q", "w_uq", "g_kv", "w_ukv", "w_oa", "sgu_ln_g", "sgu_ln_b", "w_s", "b_s", "w_ob", "w_out",
         "ln_g", "ln_b")


def kernel(x, positions, w_in, b_in, g_q, w_uq, g_kv, w_ukv, w_oa, sgu_ln_g, sgu_ln_b, w_s, b_s, w_ob, w_out, ln_g, ln_b, loss_target, m_w_in, m_b_in, m_g_q, m_w_uq, m_g_kv, m_w_ukv, m_w_oa, m_sgu_ln_g, m_sgu_ln_b, m_w_s, m_b_s, m_w_ob, m_w_out, m_ln_g, m_ln_b, v_w_in, v_b_in, v_g_q, v_w_uq, v_g_kv, v_w_ukv, v_w_oa, v_sgu_ln_g, v_sgu_ln_b, v_w_s, v_b_s, v_w_ob, v_w_out, v_ln_g, v_ln_b):
    w = dict(w_in=w_in, b_in=b_in, g_q=g_q, w_uq=w_uq, g_kv=g_kv, w_ukv=w_ukv, w_oa=w_oa, sgu_ln_g=sgu_ln_g,
             sgu_ln_b=sgu_ln_b, w_s=w_s, b_s=b_s, w_ob=w_ob, w_out=w_out, ln_g=ln_g, ln_b=ln_b)
    m = dict(w_in=m_w_in, b_in=m_b_in, g_q=m_g_q, w_uq=m_w_uq, g_kv=m_g_kv, w_ukv=m_w_ukv, w_oa=m_w_oa,
             sgu_ln_g=m_sgu_ln_g, sgu_ln_b=m_sgu_ln_b, w_s=m_w_s, b_s=m_b_s, w_ob=m_w_ob, w_out=m_w_out, ln_g=m_ln_g,
             ln_b=m_ln_b)
    v = dict(w_in=v_w_in, b_in=v_b_in, g_q=v_g_q, w_uq=v_w_uq, g_kv=v_g_kv, w_ukv=v_w_ukv, w_oa=v_w_oa,
             sgu_ln_g=v_sgu_ln_g, sgu_ln_b=v_sgu_ln_b, w_s=v_w_s, b_s=v_b_s, w_ob=v_w_ob, w_out=v_w_out, ln_g=v_ln_g,
             ln_b=v_ln_b)
    w, m, v = ({n: a[0] for n, a in d.items()} for d in (w, m, v))
    c = lax.axis_index("c")

    wt_shard, mt_shard, vt_shard = (jnp.transpose(d["w_in"]) for d in (w, m, v))
    xi, yi = lax.axis_index("x"), lax.axis_index("y")
    me1 = (2 * xi + yi).reshape(1).astype(jnp.int32)
    first = _first_start(*_cast_first(wt_shard, _pad_heads(w["w_uq"]), me1))
    tables = _rope_tables(positions[0])
    prep = _prep_local(w["b_in"], w["g_q"], w["g_kv"], w["w_ukv"])
    bsb = _bias_lanes(w["b_s"])
    c1 = c.reshape(1).astype(jnp.int32)
    idx = jnp.stack([2 * xi + yi, 2 * (1 - xi) + yi, 2 * xi + (1 - yi), 2 * (1 - xi) + (1 - yi), c]).astype(jnp.int32)
    bufs = _cast_own([wt_shard, w["w_oa"], w["w_ob"], w["w_out"]], me1, first[4])
    send0, recv0, bufs, token0 = _gather_start(bufs, first[4])
    g_lat, g_uq = _first_forward(*_first_wait(*first[:4], token0, *tables, *prep.values(), bsb, idx))
    st = _local_attention(x[0], tables, g_lat, prep, g_uq.reshape(Q_RANK, HEADS * HEAD_PAD), token0)
    g_in, g_oa, g_ob, g_out = _gather_finish(_gather_wait(send0, recv0, bufs, st["o"]))
    wt = g_in.reshape(IN_W, D_MODEL)

    loss, early, st = _local_head(
        st, x[0], loss_target[0], wt, g_oa, w["sgu_ln_g"], w["sgu_ln_b"], w["w_s"], bsb, g_ob,
        g_out.reshape(D_MODEL, D_MODEL), w["ln_g"], w["ln_b"])

    slabs = lambda a: a.reshape(N_CHIPS, IN_W // N_CHIPS, D_MODEL // 2)
    theirs, mine = (slabs(a) for a in _dwt_early(st["dhmt"], st["dhgt"], st["xb2"], c1, c1))
    parts1 = [theirs, early["w_oa"].astype(BF16), early["w_ob"].astype(BF16),
              early["w_out"].reshape(N_CHIPS, SLAB_W, D_MODEL).astype(BF16)]
    my_loss = lax.dynamic_update_slice(jnp.zeros((N_DEV,) + LOSS_TILE, F32), jnp.broadcast_to(loss, (1,) + LOSS_TILE),
                                       (4 * xi + 2 * yi + c, 0, 0))
    sems0 = _pairs_start(parts1, my_loss, 1)
    half = HEADS // 4
    first_pairs = _local_attn_bwd(st, sems0[5], 0, half, (), "attn_bwd_first")
    parts1, recv1, all_loss = _pairs_wait(*sems0[:5], first_pairs[1], 1)
    pairs1 = _add_pair([mine, *parts1[1:]], recv1, c1, "add_pair_early", n_whole=1)
    sems1 = _chips_start(pairs1, "chips_start_early")
    dq, dk, dv = _local_attn_bwd(st, sems1[4], half, half, tuple(first_pairs), "attn_bwd_rest")
    dhl, late = _local_tail(st, dq, dk, dv, dv)

    grads = {**early, **late}
    rep = jnp.concatenate([_rows8(grads[n]) for n in REPLICATED], axis=0)
    rep = jnp.pad(rep, ((0, N_CHIPS * REP_ROWS - rep.shape[0]), (0, 0))).reshape(N_CHIPS, REP_ROWS, D_MODEL)
    parts2 = [late["w_uq"].reshape(N_CHIPS, Q_RANK // N_CHIPS, HEADS * QK_DIM).astype(BF16), rep.astype(BF16),
              late["w_lat"].reshape(N_CHIPS, LAT_ROWS_PAD // N_CHIPS, D_MODEL)]
    pairs2 = _add_pair(parts2, _exchange_pairs(parts2, "exchange_pairs_late"), c1, "add_pair_late")
    sems2 = _chips_start(pairs2, "chips_start_late")
    dx = _dx(st["dr"], st["dhg"], st["dhm"], dhl, st["wt"], st["wlat"], sems2[4])
    pairs2, landed2 = _chips_wait(*sems2[:4], dx, "chips_wait_late")
    pairs1, landed1 = _chips_wait(*sems1[:4], landed2[0], "chips_wait_early")
    sums = _sum_chips([*pairs1, *pairs2], [*landed1, *landed2], idx, (F32, BF16))
    *shards, g_rep, g_lat = _share(sums[:-2], sums[-2:])
    loss = jnp.sum(all_loss[:, 0, 0])

    red = {n: s.reshape(w[n].shape) for n, s in zip(("w_oa", "w_ob", "w_out", "w_uq"), shards[1:])}
    g_rep = g_rep.reshape(N_CHIPS * REP_ROWS, D_MODEL)
    off = 0
    for n in REPLICATED:
        rows = _rows8(w[n]).shape[0]
        red[n] = g_rep[off:off + rows].reshape(-1)[:w[n].size].reshape(w[n].shape)
        off += rows
    owner = (2 * xi + yi == 0).astype(jnp.int32).reshape(1)
    gt, dt, mt, vt2 = _update_w_in(wt_shard, shards[0], mt_shard, vt_shard,
                                   g_lat.reshape(LAT_ROWS_PAD, D_MODEL).astype(F32), owner, 232)
    red["w_in"] = jnp.transpose(gt)
    small = [n for n in ORDER if n != "w_in"]
    as2d = lambda a: a.reshape(-1, a.shape[-1])
    ds, ms, vs = _update_small([as2d(w[n]) for n in small], [as2d(red[n]) for n in small],
                               [as2d(m[n]) for n in small], [as2d(v[n]) for n in small])
    delta, new_m, new_v = {"w_in": jnp.transpose(dt)}, {"w_in": jnp.transpose(mt)}, {"w_in": jnp.transpose(vt2)}
    for i, n in enumerate(small):
        delta[n], new_m[n], new_v[n] = (a[i].reshape(w[n].shape) for a in (ds, ms, vs))

    lead = lambda a: a[None]
    return (loss, dx[None], *[lead(red[n]) for n in ORDER], *[lead(delta[n]) for n in ORDER],
            *[lead(new_m[n]) for n in ORDER], *[lead(new_v[n]) for n in ORDER])
```

```python
import math

import jax
import jax.numpy as jnp
from jax import lax
from jax.experimental import pallas as pl
from jax.experimental.pallas import tpu as pltpu

F32 = jnp.float32
BF16 = jnp.bfloat16

D_MODEL = 1024
HEADS = 8
Q_RANK = 384
KV_RANK = 128
NOPE = 64
ROPE = 32
V_DIM = 64
QK_DIM = NOPE + ROPE
HEAD_PAD = 128
MLA_W = HEADS * V_DIM
SGU_W = 512
GROUPS = 8
CHUNK = 128
IN_W = 4640
RMS_EPS = 1e-6
LN_EPS = 1e-5
ALPHA = 2.0 ** 0.25
ROPE_THETA = 10000.0
SCALE = QK_DIM ** -0.5

GATE_W = 2 * D_MODEL
MID_W = 4 * SGU_W
LAT_W = Q_RANK + KV_RANK + HEAD_PAD
LAT_COLS = Q_RANK + KV_RANK + ROPE
ROW_GATE = LAT_COLS + MID_W
LAT_ROWS_PAD = 704
N_SLABS = 4
SLAB_W = D_MODEL // N_SLABS

ROW_TILE = 256
MATMUL_ROW_TILE = 512
MID_ROW_TILE = 256
ATT_TK = 256
ATT_BWD_TK = 256
SUM_ROWS = 16
LOG2E = 1.4426950408889634
LN2 = 0.6931471805599453
Q_SCALE = SCALE * LOG2E
VMEM_LIMIT = 56 * 1024 * 1024

ADAM_LR = 0.001
ADAM_B1 = 0.9
ADAM_B2 = 0.999
ADAM_EPS = 1e-08
ADAM_WD = 0.01
ADAM_STEP = 10


def _dot(a, b):
    return jnp.dot(a, b, preferred_element_type=F32)


def _dot_nt(a, b):
    return lax.dot_general(a, b, (((1,), (1,)), ((), ())), preferred_element_type=F32)


def _dot_tn(a, b):
    return lax.dot_general(a, b, (((0,), (0,)), ((), ())), preferred_element_type=F32)


def _sigmoid(z):
    return 0.5 * jnp.tanh(0.5 * z) + 0.5


_GELU_C = math.sqrt(2.0 / math.pi)


def _gelu_and_grad(x):
    x2 = x * x
    t = jnp.tanh(_GELU_C * (x + 0.044715 * x * x2))
    g = 0.5 * x * (1.0 + t)
    dg = 0.5 * (1.0 + t) + 0.5 * x * (1.0 - t * t) * (_GELU_C * (1.0 + 3.0 * 0.044715 * x2))
    return g, dg


def _silu_and_grad(z):
    s = _sigmoid(z)
    return z * s, s * (1.0 + z * (1.0 - s))


def _rope(xb, c, sl, sh):
    return xb * c + pltpu.roll(xb, 112, 1) * sl + pltpu.roll(xb, 16, 1) * sh


def _rope_t(dy, c, sl, sh):
    return dy * c + pltpu.roll(dy * sl, 16, 1) + pltpu.roll(dy * sh, 112, 1)


def _params(sem=("arbitrary",)):
    return pltpu.CompilerParams(dimension_semantics=sem, vmem_limit_bytes=VMEM_LIMIT)


def _row_spec(tile, width):
    return pl.BlockSpec((tile, width), lambda i: (i, 0))


def _full_spec(shape):
    nd = len(shape)
    return pl.BlockSpec(shape, lambda i: (0,) * nd)


def _kpe_rows(wt_ref):
    z = lambda n: jnp.zeros((n, D_MODEL), BF16)
    return jnp.concatenate([z(NOPE), wt_ref[Q_RANK + KV_RANK:LAT_COLS, :], z(HEAD_PAD - QK_DIM)], axis=0)


def _fwd_rest(xb2, wt, b_g, b_m):
    s = xb2.shape[1]
    ts = MATMUL_ROW_TILE
    tn = D_MODEL
    blocks = ([(ROW_GATE + c0, 0, c0) for c0 in range(0, GATE_W, tn)]
              + [(LAT_COLS + c0, 1, c0) for c0 in range(0, MID_W, tn)])

    def body(xb_ref, wt_hbm, bg_ref, bm_ref, hg_ref, hm_ref, wt_ref, sems):
        copies = [pltpu.make_async_copy(wt_hbm.at[lo:lo + tn], wt_ref.at[lo:lo + tn], sems.at[n])
                  for n, (lo, _, _) in enumerate(blocks)]

        def compute(first):
            xb_ = jnp.concatenate([xb_ref[0], xb_ref[1]], axis=1)
            for cp, (lo, which, c0) in zip(copies, blocks):
                if first:
                    cp.wait()
                out_ref, b_ref = ((hg_ref, bg_ref), (hm_ref, bm_ref))[which]
                out_ref[:, c0:c0 + tn] = (_dot_nt(xb_, wt_ref[lo:lo + tn, :]) + b_ref[:, c0:c0 + tn]).astype(BF16)

        @pl.when(pl.program_id(0) == 0)
        def _():
            for cp in copies:
                cp.start()
            compute(True)

        @pl.when(pl.program_id(0) > 0)
        def _():
            compute(False)

    return pl.pallas_call(
        body, name="fwd_rest", grid=(s // ts,),
        in_specs=[pl.BlockSpec((2, ts, D_MODEL // 2), lambda i: (0, i, 0)), HBM_SPEC, _full_spec(b_g.shape),
                  _full_spec(b_m.shape)],
        out_specs=[_row_spec(ts, GATE_W), _row_spec(ts, MID_W)],
        out_shape=[jax.ShapeDtypeStruct((s, GATE_W), BF16), jax.ShapeDtypeStruct((s, MID_W), BF16)],
        scratch_shapes=[pltpu.VMEM(wt.shape, BF16), pltpu.SemaphoreType.DMA((len(blocks),))],
        compiler_params=_params(),
    )(xb2, wt, b_g, b_m)


def _fwd_lat(x, wlat, b_l, g_q, wuq, g_kv, wk, wv, rc, rsl, rsh, after):
    s = x.shape[0]
    ts = ROW_TILE

    def body(x_ref, wt_ref, bl_ref, gq_ref, wuq_ref, gkv_ref, wk_ref, wv_ref, rc_ref, rsl_ref,
             rsh_ref, after_ref, hl_ref, q_ref, k_ref, v_ref, qt_ref, kt_ref, vt_ref, xb2_ref):
        xb = x_ref[...].astype(BF16)
        xb2_ref[0] = xb[:, :D_MODEL // 2]
        xb2_ref[1] = xb[:, D_MODEL // 2:]
        hl = jnp.concatenate([_dot_nt(xb, wt_ref[0:Q_RANK + KV_RANK, :]), _dot_nt(xb, _kpe_rows(wt_ref))],
                             axis=1) + bl_ref[...]
        hl_ref[...] = hl
        c, sl, sh = rc_ref[...], rsl_ref[...], rsh_ref[...]
        cq = hl[:, :Q_RANK]
        cqn = cq * lax.rsqrt(jnp.mean(cq * cq, axis=-1, keepdims=True) + RMS_EPS) * gq_ref[...]
        q = _dot(cqn.astype(BF16), wuq_ref[...])
        ckv = hl[:, Q_RANK:Q_RANK + KV_RANK]
        ckvn = (ckv * lax.rsqrt(jnp.mean(ckv * ckv, axis=-1, keepdims=True) + RMS_EPS) * gkv_ref[...]).astype(BF16)
        k = _dot(ckvn, wk_ref[...])
        vb = _dot(ckvn, wv_ref[...]).astype(BF16)
        v_ref[...] = vb
        vt_ref[...] = vb.T
        kpe = _rope(hl[:, Q_RANK + KV_RANK:], c, sl, sh)
        for hd in range(HEADS):
            lanes = slice(hd * HEAD_PAD, (hd + 1) * HEAD_PAD)
            qb = (_rope(q[:, lanes], c, sl, sh) * Q_SCALE).astype(BF16)
            kb = (k[:, lanes] + kpe).astype(BF16)
            q_ref[:, lanes] = qb
            k_ref[:, lanes] = kb
            qt_ref[lanes, :] = qb.T
            kt_ref[lanes, :] = kb.T

    qk_w = HEADS * HEAD_PAD
    col_spec = lambda rows: pl.BlockSpec((rows, ts), lambda i: (0, i))
    return pl.pallas_call(
        body, name="fwd_lat", grid=(s // ts,),
        in_specs=[_row_spec(ts, D_MODEL), _full_spec(wlat.shape),
                  _full_spec(b_l.shape), _full_spec(g_q.shape),
                  _full_spec(wuq.shape), _full_spec(g_kv.shape), _full_spec(wk.shape), _full_spec(wv.shape),
                  _row_spec(ts, HEAD_PAD), _row_spec(ts, HEAD_PAD), _row_spec(ts, HEAD_PAD),
                  pl.BlockSpec(memory_space=pl.ANY)],
        out_specs=[_row_spec(ts, LAT_W), _row_spec(ts, qk_w),
                   _row_spec(ts, qk_w), _row_spec(ts, MLA_W), col_spec(qk_w), col_spec(qk_w),
                   col_spec(MLA_W), pl.BlockSpec((2, ts, D_MODEL // 2), lambda i: (0, i, 0))],
        out_shape=[jax.ShapeDtypeStruct((s, LAT_W), F32), jax.ShapeDtypeStruct((s, qk_w), BF16),
                   jax.ShapeDtypeStruct((s, qk_w), BF16), jax.ShapeDtypeStruct((s, MLA_W), BF16),
                   jax.ShapeDtypeStruct((qk_w, s), BF16),
                   jax.ShapeDtypeStruct((qk_w, s), BF16), jax.ShapeDtypeStruct((MLA_W, s), BF16),
                   jax.ShapeDtypeStruct((2, s, D_MODEL // 2), BF16)],
        compiler_params=_params(),
    )(x, wlat, b_l, g_q, wuq, g_kv, wk, wv, rc, rsl, rsh, after)


def _attn_fwd(qt, k, vt):
    s = k.shape[0]
    tk = ATT_TK
    nk = s // tk
    pairs = HEADS // 2

    def body(qt_ref, k_ref, vt_ref, o_ref, lse_ref):
        qts = [qt_ref[hh * HEAD_PAD:(hh + 1) * HEAD_PAD, :] for hh in range(2)]
        ones = jnp.ones((SUM_ROWS, tk), BF16)

        def scores(j):
            return tuple(_dot(k_ref[j * tk:(j + 1) * tk, hh * HEAD_PAD:(hh + 1) * HEAD_PAD], qts[hh][:, j * tk:])
                         for hh in range(2))

        def weighted(j, ps):
            return tuple(_dot(jnp.concatenate([vt_ref[hh * V_DIM:(hh + 1) * V_DIM, j * tk:(j + 1) * tk], ones], axis=0),
                              ps[hh]) for hh in range(2))

        def from_lane(full, lo, part):
            return part if lo == 0 else jnp.concatenate([full[:, :lo], part], axis=1)

        krow = lax.broadcasted_iota(jnp.int32, (tk, tk), 0)
        qcol = lax.broadcasted_iota(jnp.int32, (tk, tk), 1)
        st = scores(0)
        ps = None
        stats = [(jnp.full((1, s), -jnp.inf, F32), jnp.zeros((V_DIM + SUM_ROWS, s), F32))] * 2
        for j in range(nk):
            lo, lo_prev = j * tk, max(j - 1, 0) * tk
            st_next = scores(j + 1) if j + 1 < nk else None
            pvs = weighted(j - 1, ps) if j else None
            new_ps, new_stats = [], []
            for hh in range(2):
                m, acc = stats[hh]
                diag = jnp.where(krow <= qcol, st[hh][:, :tk], -jnp.inf)
                s_ = diag if j == nk - 1 else jnp.concatenate([diag, st[hh][:, tk:]], axis=1)
                if j:
                    acc = from_lane(acc, lo_prev, acc[:, lo_prev:] + pvs[hh])
                m_old = m[:, lo:]
                m_new = jnp.maximum(m_old, jnp.max(s_, axis=0, keepdims=True))
                a = jnp.exp2(m_old - m_new)
                p = jnp.exp2(s_ - m_new)
                new_stats.append((from_lane(m, lo, m_new), from_lane(acc, lo, a * acc[:, lo:])))
                new_ps.append(p.astype(BF16))
            st, ps, stats = st_next, new_ps, new_stats
        pvs = weighted(nk - 1, ps)
        lo = (nk - 1) * tk
        accs = [from_lane(stats[hh][1], lo, stats[hh][1][:, lo:] + pvs[hh]) for hh in range(2)]
        sums = [acc[V_DIM:V_DIM + 1, :] for acc in accs]
        ot = jnp.concatenate([accs[hh][:V_DIM, :] / sums[hh] for hh in range(2)], axis=0)
        o_ref[...] = ot.T
        lse = [stats[hh][0] + jnp.log(sums[hh]) * LOG2E for hh in range(2)]
        lse_ref[...] = jnp.concatenate(lse + [jnp.zeros((6, s), F32)], axis=0)

    return pl.pallas_call(
        body, name="attn_fwd", grid=(pairs,),
        in_specs=[pl.BlockSpec((2 * HEAD_PAD, s), lambda p: (p, 0)),
                  pl.BlockSpec((s, 2 * HEAD_PAD), lambda p: (0, p)),
                  pl.BlockSpec((2 * V_DIM, s), lambda p: (p, 0))],
        out_specs=[pl.BlockSpec((s, 2 * V_DIM), lambda p: (0, p)),
                   pl.BlockSpec((None, 8, s), lambda p: (p, 0, 0))],
        out_shape=[jax.ShapeDtypeStruct((s, MLA_W), F32), jax.ShapeDtypeStruct((pairs, 8, s), F32)],
        compiler_params=_params(("arbitrary",)),
    )(qt, k, vt)


def _attn_bwd(q, qt, k, kt, v, do, dot, lse, delta, after, first_pair, n_pairs, into, name):
    s = k.shape[0]
    tk = ATT_BWD_TK
    nk = s // tk

    def body(q_ref, qt_ref, k_ref, kt_ref, v_ref, do_ref, dot_ref, lse_ref, dl_ref, after_ref, *rest):
        dqt_ref, dk_ref, dv_ref = rest[-3:]
        krow = lax.broadcasted_iota(jnp.int32, (tk, tk), 0)
        qcol = lax.broadcasted_iota(jnp.int32, (tk, tk), 1)
        lane = lax.broadcasted_iota(jnp.int32, (tk, 2 * V_DIM), 1)
        drow = lax.broadcasted_iota(jnp.int32, (2 * V_DIM, s), 0)
        dotb = dot_ref[...]
        dots = [jnp.where((drow < V_DIM) if hh == 0 else (drow >= V_DIM), dotb, jnp.zeros_like(dotb))
                for hh in range(2)]
        for j in range(nk):
            lo = j * tk
            vb = v_ref[lo:lo + tk, :]
            dob = do_ref[lo:, :]
            dvs = []
            for hh in range(2):
                rows = slice(hh * HEAD_PAD, (hh + 1) * HEAD_PAD)
                st = _dot(k_ref[lo:lo + tk, rows], qt_ref[rows, lo:])
                diag = jnp.where(krow <= qcol, st[:, :tk], -jnp.inf)
                st = diag if j == nk - 1 else jnp.concatenate([diag, st[:, tk:]], axis=1)
                p = jnp.exp2(st - lse_ref[hh:hh + 1, lo:])
                dpt = _dot(vb, dots[hh][:, lo:])
                dst = (p * (dpt - dl_ref[hh:hh + 1, lo:])).astype(BF16)
                dvs.append(_dot(p.astype(BF16), dob))
                dk_ref[lo:lo + tk, rows] = (_dot(dst, q_ref[lo:, rows]) * LN2).astype(BF16)
                dqt = _dot(kt_ref[rows, lo:lo + tk], dst)
                if j == 0:
                    dqt_ref[rows, :] = dqt
                else:
                    dqt_ref[rows, lo:] += dqt
            dv_ref[lo:lo + tk, :] = jnp.where(lane < V_DIM, dvs[0], dvs[1]).astype(BF16)
        dqt_ref[...] = dqt_ref[...] * SCALE

    pair_rows = lambda w: pl.BlockSpec((s, w), lambda p: (0, p + first_pair))
    pair_cols = lambda w: pl.BlockSpec((w, s), lambda p: (p + first_pair, 0))
    stats = pl.BlockSpec((None, 8, s), lambda p: (p + first_pair, 0, 0))
    any_spec = pl.BlockSpec(memory_space=pl.ANY)
    return pl.pallas_call(
        body, name=name, grid=(n_pairs,),
        in_specs=[pair_rows(2 * HEAD_PAD), pair_cols(2 * HEAD_PAD), pair_rows(2 * HEAD_PAD), pair_cols(2 * HEAD_PAD),
                  pair_rows(2 * V_DIM), pair_rows(2 * V_DIM), pair_cols(2 * V_DIM), stats, stats, any_spec]
        + [any_spec] * len(into),
        out_specs=[pair_cols(2 * HEAD_PAD), pair_rows(2 * HEAD_PAD), pair_rows(2 * V_DIM)],
        out_shape=[jax.ShapeDtypeStruct((HEADS * HEAD_PAD, s), F32), jax.ShapeDtypeStruct((s, HEADS * HEAD_PAD), BF16),
                   jax.ShapeDtypeStruct((s, MLA_W), BF16)],
        input_output_aliases={10 + n: n for n in range(len(into))},
        compiler_params=_params(("arbitrary",)),
    )(q, qt, k, kt, v, do, dot, lse, delta, after, *into)


def _split3(a):
    hi = a.astype(BF16)
    r1 = a - hi.astype(F32)
    mid = r1.astype(BF16)
    lo = (r1 - mid.astype(F32)).astype(BF16)
    return hi, mid, lo


def _mid(x, tgt, o, hm, hg, woa, wob, wout, ln_g, ln_b, sg_g, sg_b, w_s, bsb):
    s = x.shape[0]
    ts = MID_ROW_TILE
    nsteps = s // ts
    nch = ts // CHUNK
    npair = GROUPS // 2

    def body(x_ref, t_ref, o_ref, hm_ref, hg_ref, woa_ref, wob_ref, wout_ref, lng_ref, lnb_ref, sgg_ref, sgb_ref,
             ws_ref, bsb_ref,
             dr_ref, dhg_ref, dhm_ref, do_ref, dot_ref, dl_ref, dhgt_ref, dhmt_ref,
             dwout_ref, dwoa_ref, dwob_ref, dws_ref, dbs_ref, dlng_ref, dlnb_ref, dsgg_ref, dsgb_ref, loss_ref,
             dbg_ref, dbm_ref, dbacc_ref, awout_ref, awoa_ref, awob_ref):
        i = pl.program_id(0)

        @pl.when(i == 0)
        def _():
            for r in (awout_ref, awoa_ref, awob_ref, dws_ref, dlng_ref, dlnb_ref, dsgg_ref, dsgb_ref, loss_ref,
                      dbg_ref, dbm_ref, dbacc_ref):
                r[...] = jnp.zeros_like(r)

        def emit(ref, tref, bref, lo, val):
            vb = val.astype(BF16)
            n = val.shape[1]
            ref[:, lo:lo + n] = vb
            tref[lo:lo + n, :] = vb.T
            bref[:, lo:lo + n] += jnp.sum(val, axis=0, keepdims=True)

        lane = lax.broadcasted_iota(jnp.int32, (CHUNK, CHUNK), 1)
        left = lane < V_DIM
        tril = lax.broadcasted_iota(jnp.int32, (CHUNK, CHUNK), 0) >= lane
        ms = [jnp.where(tril, ws_ref[g], 0.0).astype(BF16) for g in range(GROUPS)]

        z_a = hm_ref[:, 0:SGU_W].astype(F32)
        u = hm_ref[:, SGU_W:2 * SGU_W].astype(F32)
        v = hm_ref[:, 2 * SGU_W:3 * SGU_W].astype(F32)
        z_b = hm_ref[:, 3 * SGU_W:4 * SGU_W].astype(F32)
        o = o_ref[...]
        sa, dsa = _silu_and_grad(z_a)
        y_a = (o * sa).astype(BF16)
        gu, dgu = _gelu_and_grad(u)
        gv, dgv = _gelu_and_grad(v)
        mu = jnp.mean(gv, axis=-1, keepdims=True)
        vc = gv - mu
        rstd_v = lax.rsqrt(jnp.mean(vc * vc, axis=-1, keepdims=True) + LN_EPS)
        vhat = vc * rstd_v
        vn = (vhat * sgg_ref[...] + sgb_ref[...]).astype(BF16)
        rows = []
        for c in range(nch):
            blocks = []
            for p in range(npair):
                blk = vn[c * CHUNK:(c + 1) * CHUNK, p * CHUNK:(p + 1) * CHUNK]
                blocks.append(jnp.where(left, _dot(ms[2 * p], blk), _dot(ms[2 * p + 1], blk)))
            rows.append(jnp.concatenate(blocks, axis=1) + bsb_ref[...])
        mixed = jnp.concatenate(rows, axis=0)
        sgu = gu * mixed
        sb, dsb = _silu_and_grad(z_b)
        y_b = (sgu * sb).astype(BF16)
        pa = jnp.concatenate([_dot(y_a, woa_ref[k]) for k in range(N_SLABS)], axis=1)
        pb = jnp.concatenate([_dot(y_b, wob_ref[k]) for k in range(N_SLABS)], axis=1)
        sga = _sigmoid(hg_ref[:, :D_MODEL].astype(F32))
        sgb = _sigmoid(hg_ref[:, D_MODEL:].astype(F32))
        m2 = (sga * pa + sgb * pb).astype(BF16)
        r = ALPHA * x_ref[...] + _dot(m2, wout_ref[...])
        rmu = jnp.mean(r, axis=-1, keepdims=True)
        rc = r - rmu
        rstd = lax.rsqrt(jnp.mean(rc * rc, axis=-1, keepdims=True) + LN_EPS)
        xhat = rc * rstd
        y = xhat * lng_ref[...] + lnb_ref[...]
        err = y - t_ref[...]
        loss_ref[...] += jnp.full(loss_ref.shape, 0.5 / D_MODEL, F32) * jnp.sum(err * err)

        dy = err * (1.0 / D_MODEL)
        dlng_ref[...] += jnp.sum(dy * xhat, axis=0, keepdims=True)
        dlnb_ref[...] += jnp.sum(dy, axis=0, keepdims=True)
        dxh = dy * lng_ref[...]
        dr = rstd * (dxh - jnp.mean(dxh, axis=-1, keepdims=True) - xhat * jnp.mean(dxh * xhat, axis=-1, keepdims=True))
        dr_ref[...] = dr
        drb = dr.astype(BF16)
        awout_ref[...] += _dot_tn(m2, drb)
        dm2 = _dot_nt(drb, wout_ref[...])
        emit(dhg_ref, dhgt_ref, dbg_ref, 0, dm2 * pa * sga * (1.0 - sga))
        emit(dhg_ref, dhgt_ref, dbg_ref, D_MODEL, dm2 * pb * sgb * (1.0 - sgb))
        dpa = (dm2 * sga).astype(BF16)
        dpb = (dm2 * sgb).astype(BF16)
        dy_a = jnp.zeros((ts, MLA_W), F32)
        dy_b = jnp.zeros((ts, SGU_W), F32)
        y_at, y_bt = y_a.T, y_b.T
        for k in range(N_SLABS):
            cols = slice(k * SLAB_W, (k + 1) * SLAB_W)
            awoa_ref[k] += _dot(y_at, dpa[:, cols])
            awob_ref[k] += _dot(y_bt, dpb[:, cols])
            dy_a = dy_a + _dot_nt(dpa[:, cols], woa_ref[k])
            dy_b = dy_b + _dot_nt(dpb[:, cols], wob_ref[k])
        dob = (dy_a * sa).astype(BF16)
        do_ref[...] = dob
        dot_ref[...] = dob.T
        head = (lax.broadcasted_iota(jnp.int32, (HEADS, MLA_W), 1) // V_DIM
                == lax.broadcasted_iota(jnp.int32, (HEADS, MLA_W), 0)).astype(BF16)
        dl = sum(_dot_nt(head, term) for term in _split3(dob.astype(F32) * o))
        for p in range(HEADS // 2):
            dl_ref[p] = jnp.concatenate([dl[2 * p:2 * p + 2], jnp.zeros((6, ts), F32)], axis=0)
        emit(dhm_ref, dhmt_ref, dbm_ref, 0, dy_a * o * dsa)
        dsg = dy_b * sb
        emit(dhm_ref, dhmt_ref, dbm_ref, 3 * SGU_W, dy_b * sgu * dsb)
        emit(dhm_ref, dhmt_ref, dbm_ref, SGU_W, dsg * mixed * dgu)
        dmixed = dsg * gu
        dvn_rows = []
        dbs_sum = jnp.zeros((CHUNK, SGU_W), F32)
        for c in range(nch):
            dm_c = dmixed[c * CHUNK:(c + 1) * CHUNK, :]
            dbs_sum = dbs_sum + dm_c
            blocks = []
            for p in range(npair):
                dmb = dm_c[:, p * CHUNK:(p + 1) * CHUNK].astype(BF16)
                blk = vn[c * CHUNK:(c + 1) * CHUNK, p * CHUNK:(p + 1) * CHUNK]
                blocks.append(jnp.where(left, _dot_tn(ms[2 * p], dmb), _dot_tn(ms[2 * p + 1], dmb)))
                zero = jnp.zeros_like(dmb)
                dws_ref[2 * p] += jnp.where(tril, _dot_nt(jnp.where(left, dmb, zero), blk), 0.0)
                dws_ref[2 * p + 1] += jnp.where(tril, _dot_nt(jnp.where(left, zero, dmb), blk), 0.0)
            dvn_rows.append(jnp.concatenate(blocks, axis=1))
        dbacc_ref[...] += dbs_sum
        dvn = jnp.concatenate(dvn_rows, axis=0)
        dsgg_ref[...] += jnp.sum(dvn * vhat, axis=0, keepdims=True)
        dsgb_ref[...] += jnp.sum(dvn, axis=0, keepdims=True)
        dvh = dvn * sgg_ref[...]
        dgv_in = rstd_v * (dvh - jnp.mean(dvh, axis=-1, keepdims=True)
                           - vhat * jnp.mean(dvh * vhat, axis=-1, keepdims=True))
        emit(dhm_ref, dhmt_ref, dbm_ref, 2 * SGU_W, dgv_in * dgv)

        @pl.when(i == nsteps - 1)
        def _():
            dwout_ref[...] = awout_ref[...].astype(BF16)
            dwoa_ref[...] = awoa_ref[...].astype(BF16)
            dwob_ref[...] = awob_ref[...].astype(BF16)
            grp = (lax.broadcasted_iota(jnp.int32, (SGU_W, CHUNK), 0) // V_DIM
                   == lax.broadcasted_iota(jnp.int32, (SGU_W, CHUNK), 1)).astype(BF16)
            hi, mid, lo = _split3(dbacc_ref[...])
            dbs_ref[...] = _dot(hi, grp) + _dot(mid, grp) + _dot(lo, grp)

    acc_shapes = [(D_MODEL, D_MODEL), woa.shape, wob.shape, (GROUPS, CHUNK, CHUNK), (CHUNK, CHUNK),
                  (1, D_MODEL), (1, D_MODEL), (1, SGU_W), (1, SGU_W), (1, 128), (1, GATE_W), (1, MID_W)]
    col_spec = lambda rows: pl.BlockSpec((rows, ts), lambda i: (0, i))
    return pl.pallas_call(
        body, name="mid", grid=(nsteps,),
        in_specs=[_row_spec(ts, D_MODEL), _row_spec(ts, D_MODEL), _row_spec(ts, MLA_W), _row_spec(ts, MID_W),
                  _row_spec(ts, GATE_W), _full_spec(woa.shape), _full_spec(wob.shape), _full_spec(wout.shape),
                  _full_spec(ln_g.shape), _full_spec(ln_b.shape), _full_spec(sg_g.shape), _full_spec(sg_b.shape),
                  _full_spec(w_s.shape), _full_spec(bsb.shape)],
        out_specs=[_row_spec(ts, D_MODEL), _row_spec(ts, GATE_W), _row_spec(ts, MID_W), _row_spec(ts, MLA_W),
                   col_spec(MLA_W), pl.BlockSpec((HEADS // 2, 8, ts), lambda i: (0, 0, i)), col_spec(GATE_W),
                   col_spec(MID_W)]
        + [_full_spec(sh) for sh in acc_shapes],
        out_shape=[jax.ShapeDtypeStruct((s, D_MODEL), F32), jax.ShapeDtypeStruct((s, GATE_W), BF16),
                   jax.ShapeDtypeStruct((s, MID_W), BF16), jax.ShapeDtypeStruct((s, MLA_W), BF16),
                   jax.ShapeDtypeStruct((MLA_W, s), BF16), jax.ShapeDtypeStruct((HEADS // 2, 8, s), F32),
                   jax.ShapeDtypeStruct((GATE_W, s), BF16), jax.ShapeDtypeStruct((MID_W, s), BF16)]
        + [jax.ShapeDtypeStruct(sh, BF16 if n < 3 else F32) for n, sh in enumerate(acc_shapes)],
        scratch_shapes=[pltpu.VMEM((CHUNK, SGU_W), F32)] + [pltpu.VMEM(sh, F32) for sh in acc_shapes[:3]],
        compiler_params=_params(),
    )(x, tgt, o, hm, hg, woa, wob, wout, ln_g, ln_b, sg_g, sg_b, w_s, bsb)


def _lat_bwd(dq, dk, dv, hl, rc, rsl, rsh, g_q, g_kv, wuq, wk, wv, after):
    s = dk.shape[0]
    ts = ROW_TILE
    qk_w = HEADS * HEAD_PAD

    def body(dq_ref, dk_ref, dv_ref, hl_ref, rc_ref, rsl_ref, rsh_ref, gq_ref, gkv_ref, wuq_ref, wk_ref, wv_ref,
             after_ref, dhl_ref, dhlt_ref, dwuq_ref, dwk_ref, dwv_ref, dgq_ref, dgkv_ref, dbl_ref):
        i = pl.program_id(0)

        @pl.when(i == 0)
        def _():
            for r in (dwuq_ref, dwk_ref, dwv_ref, dgq_ref, dgkv_ref, dbl_ref):
                r[...] = jnp.zeros_like(r)

        def emit(lo, val):
            vb = val.astype(BF16)
            n = val.shape[1]
            dhl_ref[:, lo:lo + n] = vb
            dhlt_ref[lo:lo + n, :] = vb.T
            dbl_ref[:, lo:lo + n] += jnp.sum(val, axis=0, keepdims=True)

        c, sl, sh = rc_ref[...], rsl_ref[...], rsh_ref[...]
        lane = lax.broadcasted_iota(jnp.int32, (ts, HEAD_PAD), 1)
        pe = (lane >= NOPE) & (lane < QK_DIM)
        dkpe = jnp.zeros((ts, HEAD_PAD), F32)
        dqu = []
        for hd in range(HEADS):
            lanes = slice(hd * HEAD_PAD, (hd + 1) * HEAD_PAD)
            dqu.append(_rope_t(dq_ref[lanes, :].T, c, sl, sh).astype(BF16))
            dkpe = dkpe + dk_ref[:, lanes]
        dqu = jnp.concatenate(dqu, axis=1)
        dkpe = _rope_t(jnp.where(pe, dkpe, 0.0), c, sl, sh)

        cq = hl_ref[:, :Q_RANK]
        rq = lax.rsqrt(jnp.mean(cq * cq, axis=-1, keepdims=True) + RMS_EPS)
        cqh = cq * rq
        cqn = (cqh * gq_ref[...]).astype(BF16)
        dwuq_ref[...] += _dot_tn(cqn, dqu)
        dcqn = _dot_nt(dqu, wuq_ref[...])
        dgq_ref[...] += jnp.sum(dcqn * cqh, axis=0, keepdims=True)
        dch = dcqn * gq_ref[...]
        emit(0, rq * (dch - cqh * jnp.mean(dch * cqh, axis=-1, keepdims=True)))

        ckv = hl_ref[:, Q_RANK:Q_RANK + KV_RANK]
        rk = lax.rsqrt(jnp.mean(ckv * ckv, axis=-1, keepdims=True) + RMS_EPS)
        ckh = ckv * rk
        ckn = (ckh * gkv_ref[...]).astype(BF16)
        dkb = dk_ref[...].astype(BF16)
        dvb = dv_ref[...].astype(BF16)
        dwk_ref[...] += _dot_tn(ckn, dkb)
        dwv_ref[...] += _dot_tn(ckn, dvb)
        dckn = _dot_nt(dkb, wk_ref[...]) + _dot_nt(dvb, wv_ref[...])
        dgkv_ref[...] += jnp.sum(dckn * ckh, axis=0, keepdims=True)
        dkh = dckn * gkv_ref[...]
        emit(Q_RANK, rk * (dkh - ckh * jnp.mean(dkh * ckh, axis=-1, keepdims=True)))
        emit(Q_RANK + KV_RANK, dkpe)

    acc_shapes = [wuq.shape, wk.shape, wv.shape, g_q.shape, g_kv.shape, (1, LAT_W)]
    return pl.pallas_call(
        body, name="lat_bwd", grid=(s // ts,),
        in_specs=[pl.BlockSpec((qk_w, ts), lambda i: (0, i)), _row_spec(ts, qk_w), _row_spec(ts, MLA_W),
                  _row_spec(ts, LAT_W), _row_spec(ts, HEAD_PAD), _row_spec(ts, HEAD_PAD), _row_spec(ts, HEAD_PAD),
                  _full_spec(g_q.shape), _full_spec(g_kv.shape), _full_spec(wuq.shape), _full_spec(wk.shape),
                  _full_spec(wv.shape), pl.BlockSpec(memory_space=pl.ANY)],
        out_specs=[_row_spec(ts, LAT_W), pl.BlockSpec((LAT_W, ts), lambda i: (0, i))]
        + [_full_spec(sh) for sh in acc_shapes],
        out_shape=[jax.ShapeDtypeStruct((s, LAT_W), BF16), jax.ShapeDtypeStruct((LAT_W, s), BF16)]
        + [jax.ShapeDtypeStruct(sh, F32) for sh in acc_shapes],
        compiler_params=_params(),
    )(dq, dk, dv, hl, rc, rsl, rsh, g_q, g_kv, wuq, wk, wv, after)


def _dx(dr, dhg, dhm, dhl, wt, wlat, after):
    s = dr.shape[0]
    ts = MATMUL_ROW_TILE

    tk = D_MODEL
    bounds = ([(ROW_GATE + r0, ROW_GATE + r0 + tk) for r0 in range(0, GATE_W, tk)]
              + [(LAT_COLS + r0, LAT_COLS + r0 + tk) for r0 in range(0, MID_W, tk)])

    def body(dr_ref, dhg_ref, dhm_ref, dhl_ref, wt_hbm, wlat_ref, after_ref, dx_ref, wt_ref, sems):
        copies = [pltpu.make_async_copy(wt_hbm.at[lo:hi], wt_ref.at[lo:hi], sems.at[n])
                  for n, (lo, hi) in enumerate(bounds)]

        def compute(first):
            acc = (ALPHA * dr_ref[...] + _dot(dhl_ref[:, 0:Q_RANK + KV_RANK], wlat_ref[0:Q_RANK + KV_RANK, :])
                   + _dot(dhl_ref[:, Q_RANK + KV_RANK:], _kpe_rows(wlat_ref)))
            for n, (lo, hi) in enumerate(bounds):
                if first:
                    copies[n].wait()
                if lo >= ROW_GATE:
                    acc += _dot(dhg_ref[:, lo - ROW_GATE:hi - ROW_GATE], wt_ref[lo:hi, :])
                else:
                    acc += _dot(dhm_ref[:, lo - LAT_COLS:hi - LAT_COLS], wt_ref[lo:hi, :])
            dx_ref[...] = acc

        @pl.when(pl.program_id(0) == 0)
        def _():
            for cp in copies:
                cp.start()
            compute(True)

        @pl.when(pl.program_id(0) > 0)
        def _():
            compute(False)

    return pl.pallas_call(
        body, name="dx", grid=(s // ts,),
        in_specs=[_row_spec(ts, D_MODEL), _row_spec(ts, GATE_W), _row_spec(ts, MID_W), _row_spec(ts, LAT_W),
                  HBM_SPEC, _full_spec(wlat.shape), HBM_SPEC],
        out_specs=_row_spec(ts, D_MODEL),
        out_shape=jax.ShapeDtypeStruct((s, D_MODEL), F32),
        scratch_shapes=[pltpu.VMEM(wt.shape, BF16), pltpu.SemaphoreType.DMA((len(bounds),))],
        compiler_params=_params(),
    )(dr, dhg, dhm, dhl, wt, wlat, after)


def _dwt_early(dhmt, dhgt, xb2, col, after):
    tn = 512
    nm, ng = MID_W // tn, GATE_W // tn
    s = dhmt.shape[1]
    hc = D_MODEL // 2

    ks = s // 2

    def body(col_ref, dma_ref, dmb_ref, dga_ref, dgb_ref, xoa_ref, xob_ref, xca_ref, xcb_ref, after_ref,
             dwo_ref, dwc_ref):
        i = pl.program_id(0)

        def both(da_ref, db_ref):
            da, db = da_ref[...], db_ref[...]
            dwo_ref[...] = (_dot(da, xoa_ref[...]) + _dot(db, xob_ref[...])).astype(BF16)
            dwc_ref[...] = (_dot(da, xca_ref[...]) + _dot(db, xcb_ref[...])).astype(BF16)

        @pl.when(i < nm)
        def _():
            both(dma_ref, dmb_ref)

        @pl.when(i >= nm)
        def _():
            both(dga_ref, dgb_ref)

    def dh_spec(first, part):
        if first:
            return pl.BlockSpec((tn, ks), lambda i, col_ref: (jnp.minimum(i, nm - 1), part))
        return pl.BlockSpec((tn, ks), lambda i, col_ref: (jnp.maximum(i - nm, 0), part))

    def x_spec(other, part):
        if other:
            return pl.BlockSpec((None, ks, hc), lambda i, col_ref: (1 - col_ref[0], part, 0))
        return pl.BlockSpec((None, ks, hc), lambda i, col_ref: (col_ref[0], part, 0))

    out_spec = pl.BlockSpec((pl.Element(tn), pl.Element(hc)),
                            lambda i, col_ref: (pl.multiple_of(LAT_COLS + i * tn, 32), 0))
    rows = pl.pallas_call(
        body, name="dwt_early",
        grid_spec=pltpu.PrefetchScalarGridSpec(
            num_scalar_prefetch=1, grid=(nm + ng,),
            in_specs=[dh_spec(True, 0), dh_spec(True, 1), dh_spec(False, 0), dh_spec(False, 1),
                      x_spec(True, 0), x_spec(True, 1), x_spec(False, 0), x_spec(False, 1),
                      pl.BlockSpec(memory_space=pl.ANY)],
            out_specs=[out_spec, out_spec]),
        out_shape=[jax.ShapeDtypeStruct((IN_W, hc), BF16)] * 2,
        compiler_params=_params(),
    )(col, dhmt, dhmt, dhgt, dhgt, xb2, xb2, xb2, xb2, after)

    def zero(other_ref, mine_ref, other_out, mine_out):
        other_out[...] = jnp.zeros_like(other_out)
        mine_out[...] = jnp.zeros_like(mine_out)

    lat_rows = pl.BlockSpec((LAT_COLS, hc), lambda i: (0, 0))
    return pl.pallas_call(
        zero, name="dwt_early_zero_lat", grid=(1,), in_specs=[pl.BlockSpec(memory_space=pl.ANY)] * 2,
        out_specs=[lat_rows, lat_rows],
        out_shape=[jax.ShapeDtypeStruct((IN_W, hc), BF16)] * 2, input_output_aliases={0: 0, 1: 1},
    )(*rows)


def _dwt_lat(dhlt, xb2):
    n, s = dhlt.shape
    tk = MATMUL_ROW_TILE
    nsteps = s // tk

    def body(dht_ref, xb_ref, dw_ref, acc_ref):
        i = pl.program_id(0)

        @pl.when(i == 0)
        def _():
            acc_ref[...] = jnp.zeros_like(acc_ref)

        dht = dht_ref[...]
        acc_ref[...] += jnp.concatenate([_dot(dht, xb_ref[0]), _dot(dht, xb_ref[1])], axis=1)

        @pl.when(i == nsteps - 1)
        def _():
            kpe = Q_RANK + KV_RANK + NOPE
            dw_ref[0:Q_RANK + KV_RANK, :] = acc_ref[0:Q_RANK + KV_RANK, :].astype(BF16)
            dw_ref[Q_RANK + KV_RANK:LAT_COLS, :] = acc_ref[kpe:kpe + ROPE, :].astype(BF16)
            dw_ref[LAT_COLS:, :] = jnp.zeros((LAT_ROWS_PAD - LAT_COLS, D_MODEL), BF16)

    return pl.pallas_call(
        body, name="dwt_lat", grid=(nsteps,),
        in_specs=[pl.BlockSpec((n, tk), lambda i: (0, i)), pl.BlockSpec((2, tk, D_MODEL // 2), lambda i: (0, i, 0))],
        out_specs=_full_spec((LAT_ROWS_PAD, D_MODEL)),
        out_shape=jax.ShapeDtypeStruct((LAT_ROWS_PAD, D_MODEL), BF16),
        scratch_shapes=[pltpu.VMEM((n, D_MODEL), F32)],
        compiler_params=_params(),
    )(dhlt, xb2)


def _split_bias(b):
    z = lambda n: jnp.zeros((n,), b.dtype)
    lat = jnp.concatenate([b[:Q_RANK + KV_RANK], z(NOPE), b[Q_RANK + KV_RANK:LAT_COLS], z(HEAD_PAD - QK_DIM)])
    return b[None, ROW_GATE:], b[None, LAT_COLS:ROW_GATE], lat[None, :]


def _join_bias(g, m, l):
    kpe = Q_RANK + KV_RANK + NOPE
    return jnp.concatenate([l[0, :Q_RANK + KV_RANK], l[0, kpe:kpe + ROPE], m[0], g[0]])


def _rope_tables(positions):
    half = ROPE // 2
    inv_freq = ROPE_THETA ** (-jnp.arange(0, ROPE, 2, dtype=F32) / ROPE)
    ang = positions.astype(F32)[:, None] * inv_freq
    cos, sin = jnp.cos(ang), jnp.sin(ang)
    n = positions.shape[0]
    one, zero = jnp.ones((n, NOPE), F32), jnp.zeros((n, half), F32)
    tail1, tail0 = jnp.ones((n, HEAD_PAD - QK_DIM), F32), jnp.zeros((n, HEAD_PAD - QK_DIM), F32)
    z64 = jnp.zeros((n, NOPE), F32)
    rc = jnp.concatenate([one, cos, cos, tail1], axis=1)
    rsl = jnp.concatenate([z64, -sin, zero, tail0], axis=1)
    rsh = jnp.concatenate([z64, zero, sin, tail0], axis=1)
    return rc, rsl, rsh


def _pad_heads(w_uq):
    return jnp.pad(w_uq, ((0, 0), (0, 0), (0, HEAD_PAD - QK_DIM))).reshape(w_uq.shape[0], HEADS * HEAD_PAD)


def _prep_local(b_in, g_q, g_kv, w_ukv):
    b_g, b_m, b_l = _split_bias(b_in)
    wk = jnp.pad(w_ukv[:, :, :NOPE], ((0, 0), (0, 0), (0, HEAD_PAD - NOPE))).reshape(KV_RANK, HEADS * HEAD_PAD).astype(BF16)
    wv = w_ukv[:, :, NOPE:].reshape(KV_RANK, MLA_W).astype(BF16)
    return dict(b_g=b_g, b_m=b_m, b_l=b_l, wk=wk, wv=wv, gq2=g_q[None, :], gkv2=g_kv[None, :])


def _local_attention(x, tables, wlat, prep, wuq, after):
    rc, rsl, rsh = tables
    b_g, b_m, b_l, wk, wv, gq2, gkv2 = (prep[n] for n in ("b_g", "b_m", "b_l", "wk", "wv", "gq2", "gkv2"))
    hl, q, k, v, qt, kt, vt, xb2 = _fwd_lat(x, wlat, b_l, gq2, wuq, gkv2, wk, wv, rc, rsl, rsh, after)
    o, lse = _attn_fwd(qt, k, vt)
    return dict(q=q, qt=qt, k=k, kt=kt, v=v, o=o, lse=lse, hl=hl, rc=rc, rsl=rsl, rsh=rsh, gq2=gq2, gkv2=gkv2,
                wuq=wuq, wk=wk, wv=wv, xb2=xb2, b_g=b_g, b_m=b_m, wlat=wlat)


def _bias_lanes(b_s):
    return jnp.repeat(b_s.T, V_DIM, axis=1)


def _local_head(st, x, tgt, wt, w_oa, sg_g, sg_b, w_s, bsb, w_ob, w_out, ln_g, ln_b):
    q, qt, k, kt, v, o, lse, hl = (st[n] for n in ("q", "qt", "k", "kt", "v", "o", "lse", "hl"))
    rc, rsl, rsh, gq2, gkv2, wuq, wk, wv = (st[n] for n in ("rc", "rsl", "rsh", "gq2", "gkv2", "wuq", "wk", "wv"))
    hg, hm = _fwd_rest(st["xb2"], wt, st["b_g"], st["b_m"])
    (dr, dhg, dhm, do, dot, delta, dhgt, dhmt, dwout, dwoa, dwob, dws, dbs, dlng, dlnb, dsgg, dsgb, loss, dbg,
     dbm) = _mid(x, tgt, o, hm, hg, w_oa, w_ob, w_out, ln_g[None, :], ln_b[None, :], sg_g[None, :], sg_b[None, :],
                 w_s, bsb)
    early = {
        "w_oa": dwoa, "sgu_ln_g": dsgg[0], "sgu_ln_b": dsgb[0], "w_s": dws, "b_s": dbs[:, :GROUPS].T,
        "w_ob": dwob, "w_out": dwout, "ln_g": dlng[0], "ln_b": dlnb[0],
    }
    state = dict(q=q, qt=qt, k=k, kt=kt, v=v, do=do, dot=dot, lse=lse, delta=delta, hl=hl, rc=rc, rsl=rsl, rsh=rsh,
                 gq2=gq2, gkv2=gkv2, wuq=wuq, wk=wk, wv=wv, dr=dr, dhg=dhg, dhm=dhm, wt=wt, dbg=dbg, dbm=dbm,
                 dhgt=dhgt, dhmt=dhmt, xb2=st["xb2"], wlat=st["wlat"])
    return loss, early, state


def _local_attn_bwd(st, after, first_pair=0, n_pairs=HEADS // 2, into=(), name="attn_bwd"):
    return _attn_bwd(st["q"], st["qt"], st["k"], st["kt"], st["v"], st["do"], st["dot"], st["lse"], st["delta"],
                     after, first_pair, n_pairs, into, name)


def _local_tail(st, dq, dk, dv, after):
    dhl, dhlt, dwuq, dwk, dwv, dgq, dgkv, dbl = _lat_bwd(dq, dk, dv, st["hl"], st["rc"], st["rsl"], st["rsh"],
                                                         st["gq2"], st["gkv2"], st["wuq"], st["wk"], st["wv"], after)
    late = {
        "w_lat": _dwt_lat(dhlt, st["xb2"]),
        "b_in": _join_bias(st["dbg"], st["dbm"], dbl),
        "g_q": dgq[0],
        "w_uq": dwuq.reshape(Q_RANK, HEADS, HEAD_PAD)[:, :, :QK_DIM],
        "g_kv": dgkv[0],
        "w_ukv": jnp.concatenate([dwk.reshape(KV_RANK, HEADS, HEAD_PAD)[:, :, :NOPE],
                                  dwv.reshape(KV_RANK, HEADS, V_DIM)], axis=2),
    }
    return dhl, late


def _local_step(x, positions, tgt, wt, b_in, g_q, w_uq, g_kv, w_ukv, w_oa, sg_g, sg_b, w_s, b_s, w_ob, w_out, ln_g,
                ln_b):
    st = _local_attention(x, _rope_tables(positions), wt[:LAT_COLS], _prep_local(b_in, g_q, g_kv, w_ukv),
                          _pad_heads(w_uq).astype(BF16), b_in)
    loss, early, st = _local_head(st, x, tgt, wt, w_oa, sg_g, sg_b, w_s, _bias_lanes(b_s), w_ob, w_out, ln_g, ln_b)
    dq, dk, dv = _local_attn_bwd(st, loss)
    dhl, late = _local_tail(st, dq, dk, dv, dv)
    dx = _dx(st["dr"], st["dhg"], st["dhm"], dhl, st["wt"], st["wlat"], dhl)
    grads = {**early, **late}
    right, left = _dwt_early(st["dhmt"], st["dhgt"], st["xb2"], jnp.zeros((1,), jnp.int32), dhl)
    grads["w_in"] = jnp.concatenate([grads.pop("w_lat")[:LAT_COLS], jnp.concatenate([left, right], axis=1)[LAT_COLS:]],
                                    axis=0)
    return loss, dx, grads


MESH = pl.DeviceIdType.MESH
N_CHIPS = 4
HBM_SPEC = pl.BlockSpec(memory_space=pl.ANY)
HBM_SPEC_STRICT = pl.BlockSpec(memory_space=pltpu.HBM)
VMEM_SPEC = pl.BlockSpec(memory_space=pltpu.VMEM)

REP_ROWS = 80


def _rows8(a):
    flat = a.reshape(-1)
    n = -(-flat.shape[0] // (8 * D_MODEL)) * 8 * D_MODEL
    return jnp.pad(flat, (0, n - flat.shape[0])).reshape(-1, D_MODEL)


def _place():
    x, y, c = lax.axis_index("x"), lax.axis_index("y"), lax.axis_index("c")
    others = [(1 - x, y), (x, 1 - y), (1 - x, 1 - y)]
    return x, y, c, others


N_DEV = 8
LOSS_TILE = (8, 128)


def _cast_own(shards, me, after):
    n = len(shards)

    def body(me_ref, *refs):
        for w in range(n):
            refs[n + 1 + w][...] = refs[w][...].astype(BF16)

    return pl.pallas_call(
        body, name="cast_own",
        grid_spec=pltpu.PrefetchScalarGridSpec(
            num_scalar_prefetch=1, grid=(1,),
            in_specs=[pl.BlockSpec(s.shape, lambda i, me_ref: (0, 0)) for s in shards]
            + [pl.BlockSpec(memory_space=pl.ANY)],
            out_specs=[pl.BlockSpec((None,) + s.shape, lambda i, me_ref: (me_ref[0], 0, 0)) for s in shards]),
        out_shape=[jax.ShapeDtypeStruct((N_CHIPS,) + s.shape, BF16) for s in shards],
        compiler_params=pltpu.CompilerParams(vmem_limit_bytes=VMEM_LIMIT),
    )(me, *shards, after)


def _cast_first(lat, uq, me):
    def body(me_ref, lat_ref, uq_ref, wlat_ref, guq_ref):
        wlat_ref[...] = lat_ref[...].astype(BF16)
        guq_ref[...] = uq_ref[...].astype(BF16)

    return pl.pallas_call(
        body, name="cast_first",
        grid_spec=pltpu.PrefetchScalarGridSpec(
            num_scalar_prefetch=1, grid=(1,),
            in_specs=[pl.BlockSpec((LAT_COLS, D_MODEL), lambda i, me_ref: (0, 0)),
                      pl.BlockSpec(uq.shape, lambda i, me_ref: (0, 0))],
            out_specs=[pl.BlockSpec((LAT_COLS, D_MODEL), lambda i, me_ref: (0, 0)),
                       pl.BlockSpec((None,) + uq.shape, lambda i, me_ref: (me_ref[0], 0, 0))]),
        out_shape=[jax.ShapeDtypeStruct((LAT_COLS, D_MODEL), BF16),
                   jax.ShapeDtypeStruct((N_CHIPS,) + uq.shape, BF16)],
    )(me, lat, uq)


def _first_copies(wlat_ref, guq_ref, send_sems, recv_sems, shapes):
    x, y, c, others = _place()
    me = 2 * x + y
    hl, hu = shapes[0][1] // 2, shapes[1][2] // 2
    lat_half = wlat_ref.at[:, pl.ds(c * hl, hl)]

    def copy(src, dst, k, to):
        return pltpu.make_async_remote_copy(src_ref=src, dst_ref=dst, send_sem=send_sems.at[k],
                                            recv_sem=recv_sems.at[k], device_id=to, device_id_type=MESH)

    def uq_half(chip):
        return guq_ref.at[chip, :, pl.ds(c * hu, hu)]

    lat_out = [copy(lat_half, lat_half, j, (*others[j], c)) for j in range(3)]
    uq_out = [copy(uq_half(me), uq_half(me), 3 + j, (*others[j], c)) for j in range(3)]
    j0 = jnp.maximum(x + 2 * y - 1, 0)
    lat_in = copy(lat_half, lat_half, j0, (0, 0, c))
    uq_in = [copy(uq_half(me), uq_half(2 * px + py), 3 + j, (px, py, c)) for j, (px, py) in enumerate(others)]
    return me, lat_out, uq_out, lat_in, uq_in


def _first_start(wlat, guq):
    shapes = (wlat.shape, guq.shape)

    def body(wlat_ref, guq_ref, send_sems, recv_sems, wlat_thru, guq_thru, token):
        me, lat_out, uq_out, _, _ = _first_copies(wlat_ref, guq_ref, send_sems, recv_sems, shapes)

        @pl.when(me == 0)
        def _():
            for cp in lat_out:
                cp.start()

        for cp in uq_out:
            cp.start()
        token[...] = jnp.zeros_like(token)

    outs = pl.pallas_call(
        body, name="first_start",
        out_shape=(pltpu.SemaphoreType.DMA((6,)), pltpu.SemaphoreType.DMA((6,)), pltpu.HBM(wlat.shape, BF16),
                   pltpu.HBM(guq.shape, BF16), jax.ShapeDtypeStruct(LOSS_TILE, F32)),
        in_specs=[HBM_SPEC_STRICT] * 2, out_specs=(SEM_SPEC, SEM_SPEC, HBM_SPEC_STRICT, HBM_SPEC_STRICT, VMEM_SPEC),
        input_output_aliases={0: 2, 1: 3},
        compiler_params=pltpu.CompilerParams(has_side_effects=SPLIT_EFFECT),
    )(pltpu.with_memory_space_constraint(wlat, pltpu.HBM), pltpu.with_memory_space_constraint(guq, pltpu.HBM))
    return outs


def _first_wait(send_sems, recv_sems, wlat, guq, *after):
    shapes = (wlat.shape, guq.shape)

    def body(wlat_ref, guq_ref, send_sems, recv_sems, *rest):
        me, lat_out, uq_out, lat_in, uq_in = _first_copies(wlat_ref, guq_ref, send_sems, recv_sems, shapes)

        @pl.when(me == 0)
        def _():
            for cp in lat_out:
                cp.wait_send()

        @pl.when(me != 0)
        def _():
            lat_in.wait_recv()

        for cp in uq_out:
            cp.wait_send()
        for cp in uq_in:
            cp.wait_recv()

    return pl.pallas_call(
        body, name="first_wait", out_shape=(pltpu.HBM(wlat.shape, BF16), pltpu.HBM(guq.shape, BF16)),
        in_specs=[HBM_SPEC_STRICT, HBM_SPEC_STRICT, SEM_SPEC, SEM_SPEC] + [HBM_SPEC] * len(after),
        out_specs=(HBM_SPEC_STRICT, HBM_SPEC_STRICT), input_output_aliases={0: 0, 1: 1},
        compiler_params=pltpu.CompilerParams(has_side_effects=SPLIT_EFFECT),
    )(wlat, guq, send_sems, recv_sems, *after)


def _first_forward(wlat, guq):
    hl, hu = wlat.shape[1] // 2, guq.shape[2] // 2

    def body(wlat_in, guq_in, wlat_ref, guq_ref, send_sems, recv_sems):
        x, y, c, others = _place()
        me = 2 * x + y
        sibling = (x, y, 1 - c)

        def copy(part, k):
            return pltpu.make_async_remote_copy(src_ref=part, dst_ref=part, send_sem=send_sems.at[k],
                                                recv_sem=recv_sems.at[k], device_id=sibling, device_id_type=MESH)

        def uq_part(j, half):
            px, py = others[j]
            return guq_ref.at[2 * px + py, :, pl.ds(half * hu, hu)]

        cps = [copy(uq_part(j, c), j) for j in range(3)]
        for cp in cps:
            cp.start()

        @pl.when(me != 0)
        def _():
            mine = copy(wlat_ref.at[:, pl.ds(c * hl, hl)], 3)
            mine.start()
            copy(wlat_ref.at[:, pl.ds((1 - c) * hl, hl)], 3).wait_recv()
            mine.wait_send()

        for j in range(3):
            copy(uq_part(j, 1 - c), j).wait_recv()
        for cp in cps:
            cp.wait_send()

    return pl.pallas_call(
        body, name="first_forward", in_specs=[HBM_SPEC, HBM_SPEC], out_specs=[HBM_SPEC, HBM_SPEC],
        out_shape=[jax.ShapeDtypeStruct(wlat.shape, BF16), jax.ShapeDtypeStruct(guq.shape, BF16)],
        input_output_aliases={0: 0, 1: 1},
        scratch_shapes=[pltpu.SemaphoreType.DMA((4,)), pltpu.SemaphoreType.DMA((4,))],
    )(wlat, guq)


def _gather_copy(b_ref, buf, w, j, src_chip, dst_chip, to, rows, send_sems, recv_sems):
    _, _, c, _ = _place()
    hc = _half(buf)
    return pltpu.make_async_remote_copy(
        src_ref=b_ref.at[src_chip, rows, pl.ds(c * hc, hc)], dst_ref=b_ref.at[dst_chip, rows, pl.ds(c * hc, hc)],
        send_sem=send_sems.at[3 * w + j], recv_sem=recv_sems.at[3 * w + j], device_id=to, device_id_type=MESH)


def _gather_rows(buf, w, chip, fn):
    if w != 0:
        fn(slice(None))
        return
    pl.when(chip == 0)(lambda: fn(pl.ds(LAT_COLS, buf.shape[1] - LAT_COLS)))
    pl.when(chip != 0)(lambda: fn(slice(None)))


def _gather_start(bufs, after):
    n = len(bufs)

    def body(*refs):
        b_refs = refs[:n]
        send_sems, recv_sems, token = refs[n + 1], refs[n + 2], refs[-1]
        x, y, c, others = _place()
        me = 2 * x + y
        for w in range(n):
            def start(rows, w=w):
                for j, (px, py) in enumerate(others):
                    _gather_copy(b_refs[w], bufs[w], w, j, me, me, (px, py, c), rows, send_sems, recv_sems).start()
            _gather_rows(bufs[w], w, me, start)
        token[...] = jnp.zeros_like(token)

    hbm = [pltpu.HBM(b.shape, BF16) for b in bufs]
    outs = pl.pallas_call(
        body, name="gather_start",
        out_shape=(pltpu.SemaphoreType.DMA((3 * n,)), pltpu.SemaphoreType.DMA((3 * n,)), *hbm,
                   jax.ShapeDtypeStruct(LOSS_TILE, F32)),
        in_specs=[HBM_SPEC_STRICT] * n + [HBM_SPEC],
        out_specs=(SEM_SPEC, SEM_SPEC, *[HBM_SPEC_STRICT] * n, VMEM_SPEC),
        input_output_aliases={i: 2 + i for i in range(n)},
        compiler_params=pltpu.CompilerParams(has_side_effects=SPLIT_EFFECT),
    )(*[pltpu.with_memory_space_constraint(b, pltpu.HBM) for b in bufs], after)
    return outs[0], outs[1], list(outs[2:2 + n]), outs[-1]


def _gather_wait(send_sems, recv_sems, bufs, after):
    n = len(bufs)

    def body(*refs):
        b_refs = refs[:n]
        send_sems, recv_sems = refs[n], refs[n + 1]
        x, y, c, others = _place()
        me = 2 * x + y
        for w in range(n):
            for j, (px, py) in enumerate(others):
                def copy(rows, w=w, j=j, px=px, py=py):
                    return _gather_copy(b_refs[w], bufs[w], w, j, me, 2 * px + py, (px, py, c), rows, send_sems,
                                        recv_sems)
                _gather_rows(bufs[w], w, me, lambda rows, copy=copy: copy(rows).wait_send())
                _gather_rows(bufs[w], w, 2 * px + py, lambda rows, copy=copy: copy(rows).wait_recv())

    outs = pl.pallas_call(
        body, name="gather_wait", out_shape=tuple(pltpu.HBM(b.shape, b.dtype) for b in bufs),
        in_specs=[HBM_SPEC_STRICT] * n + [SEM_SPEC, SEM_SPEC, HBM_SPEC],
        out_specs=tuple([HBM_SPEC_STRICT] * n), input_output_aliases={i: i for i in range(n)},
        compiler_params=pltpu.CompilerParams(has_side_effects=SPLIT_EFFECT),
    )(*bufs, send_sems, recv_sems, after)
    return list(outs)


def _gather_finish(bufs):
    n = len(bufs)

    def body(*refs):
        b_refs = refs[n:2 * n]
        send_sems, recv_sems = refs[2 * n:]
        x, y, c, others = _place()
        cps = []
        for w in range(n):
            hc = _half(bufs[w])
            for j, (px, py) in enumerate(others):
                part = b_refs[w].at[2 * px + py, :, pl.ds(c * hc, hc)]
                cps.append(pltpu.make_async_remote_copy(
                    src_ref=part, dst_ref=part, send_sem=send_sems.at[3 * w + j], recv_sem=recv_sems.at[3 * w + j],
                    device_id=(x, y, 1 - c), device_id_type=MESH))
        for cp in cps:
            cp.start()
        for w in range(n):
            hc = _half(bufs[w])
            for j, (px, py) in enumerate(others):
                theirs = b_refs[w].at[2 * px + py, :, pl.ds((1 - c) * hc, hc)]
                pltpu.make_async_remote_copy(
                    src_ref=theirs, dst_ref=theirs, send_sem=send_sems.at[3 * w + j], recv_sem=recv_sems.at[3 * w + j],
                    device_id=(x, y, 1 - c), device_id_type=MESH).wait_recv()
        for cp in cps:
            cp.wait_send()

    return pl.pallas_call(
        body, name="gather_finish", in_specs=[HBM_SPEC] * n, out_specs=[HBM_SPEC] * n,
        out_shape=[jax.ShapeDtypeStruct(b.shape, b.dtype) for b in bufs],
        input_output_aliases={i: i for i in range(n)},
        scratch_shapes=[pltpu.SemaphoreType.DMA((3 * n,)), pltpu.SemaphoreType.DMA((3 * n,))],
    )(*bufs)


def _half(a):
    return a.shape[-1] // 2


def _exchange_pairs(parts, name):
    n = len(parts)

    def body(*refs):
        p_refs, r_refs = refs[:n], refs[n:2 * n]
        send_sems, recv_sems = refs[2 * n:]
        x, y, c, _ = _place()
        cps = []
        for w in range(n):
            h = _half(parts[w])
            cps.append(pltpu.make_async_remote_copy(
                src_ref=p_refs[w].at[:, :, pl.ds((1 - c) * h, h)], dst_ref=r_refs[w],
                send_sem=send_sems.at[w], recv_sem=recv_sems.at[w], device_id=(x, y, 1 - c), device_id_type=MESH))
        for cp in cps:
            cp.start()
        for cp in cps:
            cp.wait()

    return pl.pallas_call(
        body, name=name, in_specs=[HBM_SPEC] * n, out_specs=[HBM_SPEC] * n,
        out_shape=[jax.ShapeDtypeStruct((N_CHIPS, p.shape[1], _half(p)), BF16) for p in parts],
        scratch_shapes=[pltpu.SemaphoreType.DMA((n,)), pltpu.SemaphoreType.DMA((n,))],
    )(*parts)


def _sibling_part(ref, w, n_whole, shape, c):
    if w < n_whole:
        return ref
    h = shape[-1] // 2
    return ref.at[:, :, pl.ds((1 - c) * h, h)]


def _pairs_start(parts, all_loss, n_whole):
    n = len(parts)

    def body(*refs):
        p_refs, r_refs, loss_ref = refs[:n], refs[n:2 * n], refs[2 * n]
        send_sems, recv_sems, token = refs[2 * n + 1], refs[2 * n + 2], refs[-1]
        x, y, c, _ = _place()
        for w in range(n):
            h = _half(parts[w])
            pltpu.make_async_remote_copy(
                src_ref=_sibling_part(p_refs[w], w, n_whole, parts[w].shape, c), dst_ref=r_refs[w],
                send_sem=send_sems.at[w], recv_sem=recv_sems.at[w], device_id=(x, y, 1 - c),
                device_id_type=MESH).start()
        me = 4 * x + 2 * y + c
        for t in range(1, N_DEV):
            d = (me + t) % N_DEV
            pltpu.make_async_remote_copy(
                src_ref=loss_ref.at[me], dst_ref=loss_ref.at[me], send_sem=send_sems.at[n + t - 1],
                recv_sem=recv_sems.at[n + t - 1], device_id=(d // 4, (d // 2) % 2, d % 2), device_id_type=MESH).start()
        token[...] = jnp.zeros_like(token)

    lands = [pltpu.HBM(p.shape if w < n_whole else (N_CHIPS, p.shape[1], _half(p)), BF16)
             for w, p in enumerate(parts)]
    nsem = n + N_DEV - 1
    outs = pl.pallas_call(
        body, name="pairs_start",
        out_shape=(pltpu.SemaphoreType.DMA((nsem,)), pltpu.SemaphoreType.DMA((nsem,)),
                   *[pltpu.HBM(p.shape, p.dtype) for p in parts], *lands, pltpu.HBM(all_loss.shape, F32),
                   jax.ShapeDtypeStruct(LOSS_TILE, F32)),
        in_specs=[HBM_SPEC_STRICT] * (2 * n + 1),
        out_specs=(SEM_SPEC, SEM_SPEC, *[HBM_SPEC_STRICT] * (2 * n + 1), VMEM_SPEC),
        input_output_aliases={i: 2 + i for i in range(2 * n + 1)},
        compiler_params=pltpu.CompilerParams(has_side_effects=SPLIT_EFFECT),
    )(*[pltpu.with_memory_space_constraint(p, pltpu.HBM) for p in parts],
      *[pltpu.with_memory_space_constraint(lax.empty(l.shape, BF16), pltpu.HBM) for l in lands],
      pltpu.with_memory_space_constraint(all_loss, pltpu.HBM))
    return outs[0], outs[1], list(outs[2:2 + n]), list(outs[2 + n:2 + 2 * n]), outs[2 + 2 * n], outs[-1]


def _pairs_wait(send_sems, recv_sems, parts, lands, all_loss, after, n_whole):
    n = len(parts)

    def body(*refs):
        p_refs, r_refs, loss_ref = refs[:n], refs[n:2 * n], refs[2 * n]
        send_sems, recv_sems = refs[2 * n + 1], refs[2 * n + 2]
        x, y, c, _ = _place()
        for w in range(n):
            h = _half(parts[w])
            cp = pltpu.make_async_remote_copy(
                src_ref=_sibling_part(p_refs[w], w, n_whole, parts[w].shape, c), dst_ref=r_refs[w],
                send_sem=send_sems.at[w],
                recv_sem=recv_sems.at[w], device_id=(x, y, 1 - c), device_id_type=MESH)
            cp.wait_send()
            cp.wait_recv()
        me = 4 * x + 2 * y + c
        for t in range(1, N_DEV):
            d = (me + N_DEV - t) % N_DEV
            cp = pltpu.make_async_remote_copy(
                src_ref=loss_ref.at[me], dst_ref=loss_ref.at[d], send_sem=send_sems.at[n + t - 1],
                recv_sem=recv_sems.at[n + t - 1], device_id=(d // 4, (d // 2) % 2, d % 2), device_id_type=MESH)
            cp.wait_send()
            cp.wait_recv()

    bufs = (*parts, *lands, all_loss)
    outs = pl.pallas_call(
        body, name="pairs_wait", out_shape=tuple(pltpu.HBM(a.shape, a.dtype) for a in bufs),
        in_specs=[HBM_SPEC_STRICT] * len(bufs) + [SEM_SPEC, SEM_SPEC, HBM_SPEC],
        out_specs=tuple([HBM_SPEC_STRICT] * len(bufs)), input_output_aliases={i: i for i in range(len(bufs))},
        compiler_params=pltpu.CompilerParams(has_side_effects=SPLIT_EFFECT),
    )(*bufs, send_sems, recv_sems, after)
    return list(outs[:n]), list(outs[n:2 * n]), outs[2 * n]


def _add_pair(ps, rs, c, name, n_whole=0):
    n = len(ps)

    def body(c_ref, *refs):
        for w in range(n):
            refs[2 * n + w][...] = (refs[w][...].astype(F32) + refs[n + w][...].astype(F32)).astype(BF16)

    def slab_spec(a):
        return pl.BlockSpec((None,) + a.shape[1:], lambda k, c_ref: (k, 0, 0))

    def my_half_spec(a):
        return pl.BlockSpec((None, a.shape[1], _half(a)), lambda k, c_ref: (k, 0, c_ref[0]))

    return pl.pallas_call(
        body, name=name,
        grid_spec=pltpu.PrefetchScalarGridSpec(
            num_scalar_prefetch=1, grid=(N_CHIPS,),
            in_specs=[slab_spec(p) if w < n_whole else my_half_spec(p) for w, p in enumerate(ps)]
            + [slab_spec(r) for r in rs],
            out_specs=[slab_spec(r) for r in rs]),
        out_shape=[jax.ShapeDtypeStruct(r.shape, BF16) for r in rs],
        compiler_params=_params(),
    )(c, *ps, *rs)


SEM_SPEC = pl.BlockSpec(memory_space=pltpu.SEMAPHORE)
SPLIT_EFFECT = pltpu.SideEffectType.DATAFLOW_SIDE_EFFECTING


def _chips_start(qs, name):
    n = len(qs)

    def body(*refs):
        q_refs, land_refs = refs[:n], refs[n:2 * n]
        send_sems, recv_sems, token = refs[2 * n], refs[2 * n + 1], refs[-1]
        x, y, c, others = _place()
        me = 2 * x + y
        for w in range(n):
            for j, (px, py) in enumerate(others):
                pltpu.make_async_remote_copy(
                    src_ref=q_refs[w].at[2 * px + py], dst_ref=land_refs[w].at[me], send_sem=send_sems.at[3 * w + j],
                    recv_sem=recv_sems.at[3 * w + j], device_id=(px, py, c), device_id_type=MESH).start()
        token[...] = jnp.zeros_like(token)

    hbm = [pltpu.HBM(q.shape, BF16) for q in qs]
    outs = pl.pallas_call(
        body, name=name,
        out_shape=(pltpu.SemaphoreType.DMA((3 * n,)), pltpu.SemaphoreType.DMA((3 * n,)), *hbm, *hbm,
                   jax.ShapeDtypeStruct(LOSS_TILE, F32)),
        in_specs=[HBM_SPEC_STRICT] * (2 * n),
        out_specs=(SEM_SPEC, SEM_SPEC, *[HBM_SPEC_STRICT] * (2 * n), VMEM_SPEC),
        input_output_aliases={i: 2 + i for i in range(2 * n)},
        compiler_params=pltpu.CompilerParams(has_side_effects=SPLIT_EFFECT),
    )(*[pltpu.with_memory_space_constraint(q, pltpu.HBM) for q in qs],
      *[pltpu.with_memory_space_constraint(lax.empty(q.shape, BF16), pltpu.HBM) for q in qs])
    return outs[0], outs[1], outs[2:2 + n], outs[2 + n:2 + 2 * n], outs[-1]


def _chips_wait(send_sems, recv_sems, q_thru, land_thru, after, name):
    n = len(q_thru)

    def body(*refs):
        q_refs, land_refs = refs[:n], refs[n:2 * n]
        send_sems, recv_sems = refs[2 * n], refs[2 * n + 1]
        x, y, c, others = _place()
        me = 2 * x + y
        for w in range(n):
            for j, (px, py) in enumerate(others):
                cp = pltpu.make_async_remote_copy(
                    src_ref=q_refs[w].at[2 * px + py], dst_ref=land_refs[w].at[2 * px + py],
                    send_sem=send_sems.at[3 * w + j], recv_sem=recv_sems.at[3 * w + j], device_id=(px, py, c),
                    device_id_type=MESH)
                cp.wait_send()
                cp.wait_recv()

    outs = pl.pallas_call(
        body, name=name, out_shape=tuple(pltpu.HBM(a.shape, a.dtype) for a in (*q_thru, *land_thru)),
        in_specs=[HBM_SPEC_STRICT] * (2 * n) + [SEM_SPEC, SEM_SPEC, HBM_SPEC],
        out_specs=tuple([HBM_SPEC_STRICT] * (2 * n)), input_output_aliases={i: i for i in range(2 * n)},
        compiler_params=pltpu.CompilerParams(has_side_effects=SPLIT_EFFECT),
    )(*q_thru, *land_thru, send_sems, recv_sems, after)
    return list(outs[:n]), list(outs[n:])


def _sum_chips(qs, rs, idx, all_dtypes):
    n = len(rs)
    n_all = len(all_dtypes)

    def body(idx_ref, *refs):
        c = idx_ref[4]
        for w in range(n):
            q_ref, g_ref = refs[w], refs[4 * n + w]
            acc = q_ref[...].astype(F32)
            for t in range(1, N_CHIPS):
                acc = acc + refs[n + 3 * w + t - 1][...].astype(F32)
            h = rs[w].shape[2]
            mine = pl.ds(pl.multiple_of(c * h, 128), h)
            g_ref[...] = jnp.zeros_like(g_ref)
            if w >= n - n_all:
                g_ref[idx_ref[0], :, mine] = acc.astype(g_ref.dtype)
            else:
                g_ref[:, mine] = acc

    shapes = [jax.ShapeDtypeStruct((r.shape[1], 2 * r.shape[2]), F32) for r in rs[:n - n_all]]
    shapes += [jax.ShapeDtypeStruct((N_CHIPS, r.shape[1], 2 * r.shape[2]), dt)
               for r, dt in zip(rs[n - n_all:], all_dtypes)]

    def slab_spec(a, t):
        return pl.BlockSpec((None,) + a.shape[1:], lambda i, idx_ref: (idx_ref[t], 0, 0))

    def whole_spec(sh):
        return pl.BlockSpec(sh.shape, lambda i, idx_ref: (0,) * len(sh.shape))

    return pl.pallas_call(
        body, name="sum_chips",
        grid_spec=pltpu.PrefetchScalarGridSpec(
            num_scalar_prefetch=1, grid=(1,),
            in_specs=[slab_spec(q, 0) for q in qs] + [slab_spec(r, t) for r in rs for t in range(1, N_CHIPS)],
            out_specs=[whole_spec(sh) for sh in shapes]),
        out_shape=shapes, compiler_params=_params(),
    )(idx, *qs, *[r for r in rs for _ in range(1, N_CHIPS)])


def _share(shards, alls):
    n, na = len(shards), len(alls)
    total = n + na

    def body(*refs):
        g_refs, a_refs = refs[total:total + n], refs[total + n:2 * total]
        send_sems, recv_sems = refs[2 * total:]
        x, y, c, others = _place()
        me = 2 * x + y
        sibling = (x, y, 1 - c)

        def cols_of(w, half):
            h = shards[w].shape[1] // 2
            return g_refs[w].at[:, pl.ds(half * h, h)]

        def slab(a, chip, half):
            h = alls[a].shape[2] // 2
            return a_refs[a].at[chip, :, pl.ds(half * h, h)]

        def copy(src, dst, k, to):
            return pltpu.make_async_remote_copy(src_ref=src, dst_ref=dst, send_sem=send_sems.at[k],
                                                recv_sem=recv_sems.at[k], device_id=to, device_id_type=MESH)

        cps = []
        for a in range(na):
            base = n + 7 * a
            for j, (px, py) in reversed(list(enumerate(others))):
                cps.append(copy(slab(a, me, c), slab(a, me, c), base + 1 + j, (px, py, c)))
            cps.append(copy(slab(a, me, c), slab(a, me, c), base, sibling))
        cps += [copy(cols_of(w, c), cols_of(w, c), w, sibling) for w in range(n)]
        for cp in cps:
            cp.start()
        fwd = []
        for a in range(na):
            base = n + 7 * a
            for j, (px, py) in enumerate(others):
                chip = 2 * px + py
                copy(slab(a, me, c), slab(a, chip, c), base + 1 + j, (px, py, c)).wait_recv()
                cp = copy(slab(a, chip, c), slab(a, chip, c), base + 4 + j, sibling)
                cp.start()
                fwd.append(cp)
        for a in range(na):
            base = n + 7 * a
            for j, (px, py) in enumerate(others):
                chip = 2 * px + py
                copy(slab(a, chip, c), slab(a, chip, 1 - c), base + 4 + j, sibling).wait_recv()
            copy(slab(a, me, c), slab(a, me, 1 - c), base, sibling).wait_recv()
        for w in range(n):
            copy(cols_of(w, c), cols_of(w, 1 - c), w, sibling).wait_recv()
        for cp in cps + fwd:
            cp.wait_send()

    nsem = n + 7 * na
    return pl.pallas_call(
        body, name="share", in_specs=[HBM_SPEC_STRICT] * total, out_specs=[HBM_SPEC_STRICT] * total,
        out_shape=[pltpu.HBM(a.shape, a.dtype) for a in (*shards, *alls)],
        input_output_aliases={i: i for i in range(total)},
        scratch_shapes=[pltpu.SemaphoreType.DMA((nsem,)), pltpu.SemaphoreType.DMA((nsem,))],
    )(*[pltpu.with_memory_space_constraint(a, pltpu.HBM) for a in (*shards, *alls)])


def _adamw(w, g, m, v):
    m2 = ADAM_B1 * m + (1.0 - ADAM_B1) * g
    v2 = ADAM_B2 * v + (1.0 - ADAM_B2) * (g * g)
    m_hat = m2 / (1.0 - ADAM_B1 ** ADAM_STEP)
    v_hat = v2 / (1.0 - ADAM_B2 ** ADAM_STEP)
    return -ADAM_LR * (m_hat / (jnp.sqrt(v_hat) + ADAM_EPS) + ADAM_WD * w), m2, v2


def _update_w_in(wt, gt, mt, vt, lat, owner, tile):
    nlat = lat.shape[0] // tile

    def body(owner_ref, w_ref, g_ref, m_ref, v_ref, lat_ref, g2_ref, d_ref, m2_ref, v2_ref):
        row = pl.program_id(0) * tile + lax.broadcasted_iota(jnp.int32, (tile, 1), 0)
        g = jnp.where((row < LAT_COLS) & (owner_ref[0] == 1), lat_ref[...].astype(F32), g_ref[...])
        g2_ref[...] = g
        d_ref[...], m2_ref[...], v2_ref[...] = _adamw(w_ref[...], g, m_ref[...], v_ref[...])

    spec = pl.BlockSpec((tile, wt.shape[1]), lambda i, o: (i, 0))
    return pl.pallas_call(
        body, name="update_w_in",
        grid_spec=pltpu.PrefetchScalarGridSpec(
            num_scalar_prefetch=1, grid=(wt.shape[0] // tile,),
            in_specs=[spec] * 4 + [pl.BlockSpec((tile, wt.shape[1]), lambda i, o: (jnp.minimum(i, nlat - 1), 0))],
            out_specs=[spec] * 4),
        out_shape=[jax.ShapeDtypeStruct(wt.shape, F32)] * 4,
        compiler_params=_params(("parallel",)),
    )(owner, wt, gt, mt, vt, lat)


def _update_small(ws, gs, ms, vs):
    n = len(ws)

    def body(*refs):
        for k in range(n):
            w_ref, g_ref, m_ref, v_ref = refs[k], refs[n + k], refs[2 * n + k], refs[3 * n + k]
            d, m2, v2 = _adamw(w_ref[...], g_ref[...], m_ref[...], v_ref[...])
            refs[4 * n + k][...] = d
            refs[5 * n + k][...] = m2
            refs[6 * n + k][...] = v2

    shapes = [jax.ShapeDtypeStruct(w.shape, F32) for w in ws]
    outs = pl.pallas_call(
        body, name="update_small", in_specs=[VMEM_SPEC] * (4 * n), out_specs=[VMEM_SPEC] * (3 * n),
        out_shape=shapes * 3,
        compiler_params=pltpu.CompilerParams(vmem_limit_bytes=VMEM_LIMIT),
    )(*ws, *gs, *ms, *vs)
    return outs[:n], outs[n:2 * n], outs[2 * n:]


REPLICATED = ("b_in", "g_q", "g_kv", "w_ukv", "sgu_ln_g", "sgu_ln_b", "w_s", "b_s", "ln_g", "ln_b")
ORDER = ("w_in", "b_in", "g_q", "w_uq", "g_kv", "w_ukv", "w_oa", "sgu_ln_g", "sgu_ln_b", "w_s", "b_s", "w_ob", "w_out",
         "ln_g", "ln_b")


def kernel(x, positions, w_in, b_in, g_q, w_uq, g_kv, w_ukv, w_oa, sgu_ln_g, sgu_ln_b, w_s, b_s, w_ob, w_out, ln_g, ln_b, loss_target, m_w_in, m_b_in, m_g_q, m_w_uq, m_g_kv, m_w_ukv, m_w_oa, m_sgu_ln_g, m_sgu_ln_b, m_w_s, m_b_s, m_w_ob, m_w_out, m_ln_g, m_ln_b, v_w_in, v_b_in, v_g_q, v_w_uq, v_g_kv, v_w_ukv, v_w_oa, v_sgu_ln_g, v_sgu_ln_b, v_w_s, v_b_s, v_w_ob, v_w_out, v_ln_g, v_ln_b):
    w = dict(w_in=w_in, b_in=b_in, g_q=g_q, w_uq=w_uq, g_kv=g_kv, w_ukv=w_ukv, w_oa=w_oa, sgu_ln_g=sgu_ln_g,
             sgu_ln_b=sgu_ln_b, w_s=w_s, b_s=b_s, w_ob=w_ob, w_out=w_out, ln_g=ln_g, ln_b=ln_b)
    m = dict(w_in=m_w_in, b_in=m_b_in, g_q=m_g_q, w_uq=m_w_uq, g_kv=m_g_kv, w_ukv=m_w_ukv, w_oa=m_w_oa,
             sgu_ln_g=m_sgu_ln_g, sgu_ln_b=m_sgu_ln_b, w_s=m_w_s, b_s=m_b_s, w_ob=m_w_ob, w_out=m_w_out, ln_g=m_ln_g,
             ln_b=m_ln_b)
    v = dict(w_in=v_w_in, b_in=v_b_in, g_q=v_g_q, w_uq=v_w_uq, g_kv=v_g_kv, w_ukv=v_w_ukv, w_oa=v_w_oa,
             sgu_ln_g=v_sgu_ln_g, sgu_ln_b=v_sgu_ln_b, w_s=v_w_s, b_s=v_b_s, w_ob=v_w_ob, w_out=v_w_out, ln_g=v_ln_g,
             ln_b=v_ln_b)
    w, m, v = ({n: a[0] for n, a in d.items()} for d in (w, m, v))
    c = lax.axis_index("c")

    wt_shard, mt_shard, vt_shard = (jnp.transpose(d["w_in"]) for d in (w, m, v))
    xi, yi = lax.axis_index("x"), lax.axis_index("y")
    me1 = (2 * xi + yi).reshape(1).astype(jnp.int32)
    first = _first_start(*_cast_first(wt_shard, _pad_heads(w["w_uq"]), me1))
    tables = _rope_tables(positions[0])
    prep = _prep_local(w["b_in"], w["g_q"], w["g_kv"], w["w_ukv"])
    bsb = _bias_lanes(w["b_s"])
    c1 = c.reshape(1).astype(jnp.int32)
    idx = jnp.stack([2 * xi + yi, 2 * (1 - xi) + yi, 2 * xi + (1 - yi), 2 * (1 - xi) + (1 - yi), c]).astype(jnp.int32)
    bufs = _cast_own([wt_shard, w["w_oa"], w["w_ob"], w["w_out"]], me1, first[4])
    send0, recv0, bufs, token0 = _gather_start(bufs, first[4])
    g_lat, g_uq = _first_forward(*_first_wait(*first[:4], token0, *tables, *prep.values(), bsb, idx))
    st = _local_attention(x[0], tables, g_lat, prep, g_uq.reshape(Q_RANK, HEADS * HEAD_PAD), token0)
    g_in, g_oa, g_ob, g_out = _gather_finish(_gather_wait(send0, recv0, bufs, st["o"]))
    wt = g_in.reshape(IN_W, D_MODEL)

    loss, early, st = _local_head(
        st, x[0], loss_target[0], wt, g_oa, w["sgu_ln_g"], w["sgu_ln_b"], w["w_s"], bsb, g_ob,
        g_out.reshape(D_MODEL, D_MODEL), w["ln_g"], w["ln_b"])

    slabs = lambda a: a.reshape(N_CHIPS, IN_W // N_CHIPS, D_MODEL // 2)
    theirs, mine = (slabs(a) for a in _dwt_early(st["dhmt"], st["dhgt"], st["xb2"], c1, c1))
    parts1 = [theirs, early["w_oa"].astype(BF16), early["w_ob"].astype(BF16),
              early["w_out"].reshape(N_CHIPS, SLAB_W, D_MODEL).astype(BF16)]
    my_loss = lax.dynamic_update_slice(jnp.zeros((N_DEV,) + LOSS_TILE, F32), jnp.broadcast_to(loss, (1,) + LOSS_TILE),
                                       (4 * xi + 2 * yi + c, 0, 0))
    sems0 = _pairs_start(parts1, my_loss, 1)
    dq, dk, dv = _local_attn_bwd(st, sems0[5])
    parts1, recv1, all_loss = _pairs_wait(*sems0[:5], dk, 1)
    pairs1 = _add_pair([mine, *parts1[1:]], recv1, c1, "add_pair_early", n_whole=1)
    sems1 = _chips_start(pairs1, "chips_start_early")
    dhl, late = _local_tail(st, dq, dk, dv, sems1[4])

    grads = {**early, **late}
    rep = jnp.concatenate([_rows8(grads[n]) for n in REPLICATED], axis=0)
    rep = jnp.pad(rep, ((0, N_CHIPS * REP_ROWS - rep.shape[0]), (0, 0))).reshape(N_CHIPS, REP_ROWS, D_MODEL)
    parts2 = [late["w_uq"].reshape(N_CHIPS, Q_RANK // N_CHIPS, HEADS * QK_DIM).astype(BF16), rep.astype(BF16),
              late["w_lat"].reshape(N_CHIPS, LAT_ROWS_PAD // N_CHIPS, D_MODEL)]
    pairs2 = _add_pair(parts2, _exchange_pairs(parts2, "exchange_pairs_late"), c1, "add_pair_late")
    sems2 = _chips_start(pairs2, "chips_start_late")
    dx = _dx(st["dr"], st["dhg"], st["dhm"], dhl, st["wt"], st["wlat"], sems2[4])
    pairs2, landed2 = _chips_wait(*sems2[:4], dx, "chips_wait_late")
    pairs1, landed1 = _chips_wait(*sems1[:4], landed2[0], "chips_wait_early")
    sums = _sum_chips([*pairs1, *pairs2], [*landed1, *landed2], idx, (F32, BF16))
    *shards, g_rep, g_lat = _share(sums[:-2], sums[-2:])
    loss = jnp.sum(all_loss[:, 0, 0])

    red = {n: s.reshape(w[n].shape) for n, s in zip(("w_oa", "w_ob", "w_out", "w_uq"), shards[1:])}
    g_rep = g_rep.reshape(N_CHIPS * REP_ROWS, D_MODEL)
    off = 0
    for n in REPLICATED:
        rows = _rows8(w[n]).shape[0]
        red[n] = g_rep[off:off + rows].reshape(-1)[:w[n].size].reshape(w[n].shape)
        off += rows
    owner = (2 * xi + yi == 0).astype(jnp.int32).reshape(1)
    gt, dt, mt, vt2 = _update_w_in(wt_shard, shards[0], mt_shard, vt_shard,
                                   g_lat.reshape(LAT_ROWS_PAD, D_MODEL).astype(F32), owner, 232)
    red["w_in"] = jnp.transpose(gt)
    small = [n for n in ORDER if n != "w_in"]
    as2d = lambda a: a.reshape(-1, a.shape[-1])
    ds, ms, vs = _update_small([as2d(w[n]) for n in small], [as2d(red[n]) for n in small],
                               [as2d(m[n]) for n in small], [as2d(v[n]) for n in small])
    delta, new_m, new_v = {"w_in": jnp.transpose(dt)}, {"w_in": jnp.transpose(mt)}, {"w_in": jnp.transpose(vt2)}
    for i, n in enumerate(small):
        delta[n], new_m[n], new_v[n] = (a[i].reshape(w[n].shape) for a in (ds, ms, vs))

    lead = lambda a: a[None]
    return (loss, dx[None], *[lead(red[n]) for n in ORDER], *[lead(delta[n]) for n in ORDER],
            *[lead(new_m[n]) for n in ORDER], *[lead(new_v[n]) for n in ORDER])
```

```python
import math

import jax
import jax.numpy as jnp
from jax import lax
from jax.experimental import pallas as pl
from jax.experimental.pallas import tpu as pltpu

F32 = jnp.float32
BF16 = jnp.bfloat16

D_MODEL = 1024
HEADS = 8
Q_RANK = 384
KV_RANK = 128
NOPE = 64
ROPE = 32
V_DIM = 64
QK_DIM = NOPE + ROPE
HEAD_PAD = 128
MLA_W = HEADS * V_DIM
SGU_W = 512
GROUPS = 8
CHUNK = 128
IN_W = 4640
RMS_EPS = 1e-6
LN_EPS = 1e-5
ALPHA = 2.0 ** 0.25
ROPE_THETA = 10000.0
SCALE = QK_DIM ** -0.5

GATE_W = 2 * D_MODEL
MID_W = 4 * SGU_W
LAT_W = Q_RANK + KV_RANK + HEAD_PAD
LAT_COLS = Q_RANK + KV_RANK + ROPE
ROW_GATE = LAT_COLS + MID_W
LAT_ROWS_PAD = 704
N_SLABS = 4
SLAB_W = D_MODEL // N_SLABS

ROW_TILE = 256
MATMUL_ROW_TILE = 512
DX_RING = 3
MID_ROW_TILE = 256
ATT_TK = 256
ATT_BWD_TK = 256
SUM_ROWS = 16
LOG2E = 1.4426950408889634
LN2 = 0.6931471805599453
Q_SCALE = SCALE * LOG2E
VMEM_LIMIT = 56 * 1024 * 1024

ADAM_LR = 0.001
ADAM_B1 = 0.9
ADAM_B2 = 0.999
ADAM_EPS = 1e-08
ADAM_WD = 0.01
ADAM_STEP = 10


def _dot(a, b):
    return jnp.dot(a, b, preferred_element_type=F32)


def _dot_nt(a, b):
    return lax.dot_general(a, b, (((1,), (1,)), ((), ())), preferred_element_type=F32)


def _dot_tn(a, b):
    return lax.dot_general(a, b, (((0,), (0,)), ((), ())), preferred_element_type=F32)


def _sigmoid(z):
    return 0.5 * jnp.tanh(0.5 * z) + 0.5


_GELU_C = math.sqrt(2.0 / math.pi)


def _gelu_and_grad(x):
    x2 = x * x
    t = jnp.tanh(_GELU_C * (x + 0.044715 * x * x2))
    g = 0.5 * x * (1.0 + t)
    dg = 0.5 * (1.0 + t) + 0.5 * x * (1.0 - t * t) * (_GELU_C * (1.0 + 3.0 * 0.044715 * x2))
    return g, dg


def _silu_and_grad(z):
    s = _sigmoid(z)
    return z * s, s * (1.0 + z * (1.0 - s))


def _rope(xb, c, sl, sh):
    return xb * c + pltpu.roll(xb, 112, 1) * sl + pltpu.roll(xb, 16, 1) * sh


def _rope_t(dy, c, sl, sh):
    return dy * c + pltpu.roll(dy * sl, 16, 1) + pltpu.roll(dy * sh, 112, 1)


def _params(sem=("arbitrary",)):
    return pltpu.CompilerParams(dimension_semantics=sem, vmem_limit_bytes=VMEM_LIMIT)


def _row_spec(tile, width):
    return pl.BlockSpec((tile, width), lambda i: (i, 0))


def _full_spec(shape):
    nd = len(shape)
    return pl.BlockSpec(shape, lambda i: (0,) * nd)


def _kpe_rows(wt_ref):
    z = lambda n: jnp.zeros((n, D_MODEL), BF16)
    return jnp.concatenate([z(NOPE), wt_ref[Q_RANK + KV_RANK:LAT_COLS, :], z(HEAD_PAD - QK_DIM)], axis=0)


def _fwd_rest(xb2, wt, b_g, b_m):
    s = xb2.shape[1]
    ts = MATMUL_ROW_TILE
    tn = D_MODEL
    blocks = ([(ROW_GATE + c0, 0, c0) for c0 in range(0, GATE_W, tn)]
              + [(LAT_COLS + c0, 1, c0) for c0 in range(0, MID_W, tn)])

    def body(xb_ref, wt_hbm, bg_ref, bm_ref, hg_ref, hm_ref, wt_ref, sems):
        copies = [pltpu.make_async_copy(wt_hbm.at[lo:lo + tn], wt_ref.at[lo:lo + tn], sems.at[n])
                  for n, (lo, _, _) in enumerate(blocks)]

        def compute(first):
            xb_ = jnp.concatenate([xb_ref[0], xb_ref[1]], axis=1)
            for cp, (lo, which, c0) in zip(copies, blocks):
                if first:
                    cp.wait()
                out_ref, b_ref = ((hg_ref, bg_ref), (hm_ref, bm_ref))[which]
                out_ref[:, c0:c0 + tn] = (_dot_nt(xb_, wt_ref[lo:lo + tn, :]) + b_ref[:, c0:c0 + tn]).astype(BF16)

        @pl.when(pl.program_id(0) == 0)
        def _():
            for cp in copies:
                cp.start()
            compute(True)

        @pl.when(pl.program_id(0) > 0)
        def _():
            compute(False)

    return pl.pallas_call(
        body, name="fwd_rest", grid=(s // ts,),
        in_specs=[pl.BlockSpec((2, ts, D_MODEL // 2), lambda i: (0, i, 0)), HBM_SPEC, _full_spec(b_g.shape),
                  _full_spec(b_m.shape)],
        out_specs=[_row_spec(ts, GATE_W), _row_spec(ts, MID_W)],
        out_shape=[jax.ShapeDtypeStruct((s, GATE_W), BF16), jax.ShapeDtypeStruct((s, MID_W), BF16)],
        scratch_shapes=[pltpu.VMEM(wt.shape, BF16), pltpu.SemaphoreType.DMA((len(blocks),))],
        compiler_params=_params(),
    )(xb2, wt, b_g, b_m)


def _fwd_lat(x, wlat, b_l, g_q, wuq, g_kv, wk, wv, rc, rsl, rsh, after):
    s = x.shape[0]
    ts = ROW_TILE

    def body(x_ref, wt_ref, bl_ref, gq_ref, wuq_ref, gkv_ref, wk_ref, wv_ref, rc_ref, rsl_ref,
             rsh_ref, after_ref, hl_ref, q_ref, k_ref, v_ref, qt_ref, kt_ref, vt_ref, xb2_ref):
        xb = x_ref[...].astype(BF16)
        xb2_ref[0] = xb[:, :D_MODEL // 2]
        xb2_ref[1] = xb[:, D_MODEL // 2:]
        hl = jnp.concatenate([_dot_nt(xb, wt_ref[0:Q_RANK + KV_RANK, :]), _dot_nt(xb, _kpe_rows(wt_ref))],
                             axis=1) + bl_ref[...]
        hl_ref[...] = hl
        c, sl, sh = rc_ref[...], rsl_ref[...], rsh_ref[...]
        cq = hl[:, :Q_RANK]
        cqn = cq * lax.rsqrt(jnp.mean(cq * cq, axis=-1, keepdims=True) + RMS_EPS) * gq_ref[...]
        q = _dot(cqn.astype(BF16), wuq_ref[...])
        ckv = hl[:, Q_RANK:Q_RANK + KV_RANK]
        ckvn = (ckv * lax.rsqrt(jnp.mean(ckv * ckv, axis=-1, keepdims=True) + RMS_EPS) * gkv_ref[...]).astype(BF16)
        k = _dot(ckvn, wk_ref[...])
        vb = _dot(ckvn, wv_ref[...]).astype(BF16)
        v_ref[...] = vb
        vt_ref[...] = vb.T
        kpe = _rope(hl[:, Q_RANK + KV_RANK:], c, sl, sh)
        for hd in range(HEADS):
            lanes = slice(hd * HEAD_PAD, (hd + 1) * HEAD_PAD)
            qb = (_rope(q[:, lanes], c, sl, sh) * Q_SCALE).astype(BF16)
            kb = (k[:, lanes] + kpe).astype(BF16)
            q_ref[:, lanes] = qb
            k_ref[:, lanes] = kb
            qt_ref[lanes, :] = qb.T
            kt_ref[lanes, :] = kb.T

    qk_w = HEADS * HEAD_PAD
    col_spec = lambda rows: pl.BlockSpec((rows, ts), lambda i: (0, i))
    return pl.pallas_call(
        body, name="fwd_lat", grid=(s // ts,),
        in_specs=[_row_spec(ts, D_MODEL), _full_spec(wlat.shape),
                  _full_spec(b_l.shape), _full_spec(g_q.shape),
                  _full_spec(wuq.shape), _full_spec(g_kv.shape), _full_spec(wk.shape), _full_spec(wv.shape),
                  _row_spec(ts, HEAD_PAD), _row_spec(ts, HEAD_PAD), _row_spec(ts, HEAD_PAD),
                  pl.BlockSpec(memory_space=pl.ANY)],
        out_specs=[_row_spec(ts, LAT_W), _row_spec(ts, qk_w),
                   _row_spec(ts, qk_w), _row_spec(ts, MLA_W), col_spec(qk_w), col_spec(qk_w),
                   col_spec(MLA_W), pl.BlockSpec((2, ts, D_MODEL // 2), lambda i: (0, i, 0))],
        out_shape=[jax.ShapeDtypeStruct((s, LAT_W), F32), jax.ShapeDtypeStruct((s, qk_w), BF16),
                   jax.ShapeDtypeStruct((s, qk_w), BF16), jax.ShapeDtypeStruct((s, MLA_W), BF16),
                   jax.ShapeDtypeStruct((qk_w, s), BF16),
                   jax.ShapeDtypeStruct((qk_w, s), BF16), jax.ShapeDtypeStruct((MLA_W, s), BF16),
                   jax.ShapeDtypeStruct((2, s, D_MODEL // 2), BF16)],
        compiler_params=_params(),
    )(x, wlat, b_l, g_q, wuq, g_kv, wk, wv, rc, rsl, rsh, after)


def _attn_fwd(qt, k, vt):
    s = k.shape[0]
    tk = ATT_TK
    nk = s // tk
    pairs = HEADS // 2

    def body(qt_ref, k_ref, vt_ref, o_ref, lse_ref):
        qts = [qt_ref[hh * HEAD_PAD:(hh + 1) * HEAD_PAD, :] for hh in range(2)]
        ones = jnp.ones((SUM_ROWS, tk), BF16)

        def scores(j):
            return tuple(_dot(k_ref[j * tk:(j + 1) * tk, hh * HEAD_PAD:(hh + 1) * HEAD_PAD], qts[hh][:, j * tk:])
                         for hh in range(2))

        def weighted(j, ps):
            return tuple(_dot(jnp.concatenate([vt_ref[hh * V_DIM:(hh + 1) * V_DIM, j * tk:(j + 1) * tk], ones], axis=0),
                              ps[hh]) for hh in range(2))

        def from_lane(full, lo, part):
            return part if lo == 0 else jnp.concatenate([full[:, :lo], part], axis=1)

        krow = lax.broadcasted_iota(jnp.int32, (tk, tk), 0)
        qcol = lax.broadcasted_iota(jnp.int32, (tk, tk), 1)
        st = scores(0)
        ps = None
        stats = [(jnp.full((1, s), -jnp.inf, F32), jnp.zeros((V_DIM + SUM_ROWS, s), F32))] * 2
        for j in range(nk):
            lo, lo_prev = j * tk, max(j - 1, 0) * tk
            st_next = scores(j + 1) if j + 1 < nk else None
            pvs = weighted(j - 1, ps) if j else None
            new_ps, new_stats = [], []
            for hh in range(2):
                m, acc = stats[hh]
                diag = jnp.where(krow <= qcol, st[hh][:, :tk], -jnp.inf)
                s_ = diag if j == nk - 1 else jnp.concatenate([diag, st[hh][:, tk:]], axis=1)
                if j:
                    acc = from_lane(acc, lo_prev, acc[:, lo_prev:] + pvs[hh])
                m_old = m[:, lo:]
                m_new = jnp.maximum(m_old, jnp.max(s_, axis=0, keepdims=True))
                a = jnp.exp2(m_old - m_new)
                p = jnp.exp2(s_ - m_new)
                new_stats.append((from_lane(m, lo, m_new), from_lane(acc, lo, a * acc[:, lo:])))
                new_ps.append(p.astype(BF16))
            st, ps, stats = st_next, new_ps, new_stats
        pvs = weighted(nk - 1, ps)
        lo = (nk - 1) * tk
        accs = [from_lane(stats[hh][1], lo, stats[hh][1][:, lo:] + pvs[hh]) for hh in range(2)]
        sums = [acc[V_DIM:V_DIM + 1, :] for acc in accs]
        ot = jnp.concatenate([accs[hh][:V_DIM, :] / sums[hh] for hh in range(2)], axis=0)
        o_ref[...] = ot.T
        lse = [stats[hh][0] + jnp.log(sums[hh]) * LOG2E for hh in range(2)]
        lse_ref[...] = jnp.concatenate(lse + [jnp.zeros((6, s), F32)], axis=0)

    return pl.pallas_call(
        body, name="attn_fwd", grid=(pairs,),
        in_specs=[pl.BlockSpec((2 * HEAD_PAD, s), lambda p: (p, 0)),
                  pl.BlockSpec((s, 2 * HEAD_PAD), lambda p: (0, p)),
                  pl.BlockSpec((2 * V_DIM, s), lambda p: (p, 0))],
        out_specs=[pl.BlockSpec((s, 2 * V_DIM), lambda p: (0, p)),
                   pl.BlockSpec((None, 8, s), lambda p: (p, 0, 0))],
        out_shape=[jax.ShapeDtypeStruct((s, MLA_W), F32), jax.ShapeDtypeStruct((pairs, 8, s), F32)],
        compiler_params=_params(("arbitrary",)),
    )(qt, k, vt)


def _attn_bwd(q, qt, k, kt, v, do, dot, lse, delta, after, first_pair, n_pairs, into, name):
    s = k.shape[0]
    tk = ATT_BWD_TK
    nk = s // tk

    def body(q_ref, qt_ref, k_ref, kt_ref, v_ref, do_ref, dot_ref, lse_ref, dl_ref, after_ref, *rest):
        dqt_ref, dk_ref, dv_ref = rest[-3:]
        krow = lax.broadcasted_iota(jnp.int32, (tk, tk), 0)
        qcol = lax.broadcasted_iota(jnp.int32, (tk, tk), 1)
        lane = lax.broadcasted_iota(jnp.int32, (tk, 2 * V_DIM), 1)
        drow = lax.broadcasted_iota(jnp.int32, (2 * V_DIM, s), 0)
        dotb = dot_ref[...]
        dots = [jnp.where((drow < V_DIM) if hh == 0 else (drow >= V_DIM), dotb, jnp.zeros_like(dotb))
                for hh in range(2)]
        for j in range(nk):
            lo = j * tk
            vb = v_ref[lo:lo + tk, :]
            dob = do_ref[lo:, :]
            dvs = []
            for hh in range(2):
                rows = slice(hh * HEAD_PAD, (hh + 1) * HEAD_PAD)
                st = _dot(k_ref[lo:lo + tk, rows], qt_ref[rows, lo:])
                diag = jnp.where(krow <= qcol, st[:, :tk], -jnp.inf)
                st = diag if j == nk - 1 else jnp.concatenate([diag, st[:, tk:]], axis=1)
                p = jnp.exp2(st - lse_ref[hh:hh + 1, lo:])
                dpt = _dot(vb, dots[hh][:, lo:])
                dst = (p * (dpt - dl_ref[hh:hh + 1, lo:])).astype(BF16)
                dvs.append(_dot(p.astype(BF16), dob))
                dk_ref[lo:lo + tk, rows] = (_dot(dst, q_ref[lo:, rows]) * LN2).astype(BF16)
                dqt = _dot(kt_ref[rows, lo:lo + tk], dst)
                if j == 0:
                    dqt_ref[rows, :] = dqt
                else:
                    dqt_ref[rows, lo:] += dqt
            dv_ref[lo:lo + tk, :] = jnp.where(lane < V_DIM, dvs[0], dvs[1]).astype(BF16)
        dqt_ref[...] = dqt_ref[...] * SCALE

    pair_rows = lambda w: pl.BlockSpec((s, w), lambda p: (0, p + first_pair))
    pair_cols = lambda w: pl.BlockSpec((w, s), lambda p: (p + first_pair, 0))
    stats = pl.BlockSpec((None, 8, s), lambda p: (p + first_pair, 0, 0))
    any_spec = pl.BlockSpec(memory_space=pl.ANY)
    return pl.pallas_call(
        body, name=name, grid=(n_pairs,),
        in_specs=[pair_rows(2 * HEAD_PAD), pair_cols(2 * HEAD_PAD), pair_rows(2 * HEAD_PAD), pair_cols(2 * HEAD_PAD),
                  pair_rows(2 * V_DIM), pair_rows(2 * V_DIM), pair_cols(2 * V_DIM), stats, stats, any_spec]
        + [any_spec] * len(into),
        out_specs=[pair_cols(2 * HEAD_PAD), pair_rows(2 * HEAD_PAD), pair_rows(2 * V_DIM)],
        out_shape=[jax.ShapeDtypeStruct((HEADS * HEAD_PAD, s), F32), jax.ShapeDtypeStruct((s, HEADS * HEAD_PAD), BF16),
                   jax.ShapeDtypeStruct((s, MLA_W), BF16)],
        input_output_aliases={10 + n: n for n in range(len(into))},
        compiler_params=_params(("arbitrary",)),
    )(q, qt, k, kt, v, do, dot, lse, delta, after, *into)


def _split3(a):
    hi = a.astype(BF16)
    r1 = a - hi.astype(F32)
    mid = r1.astype(BF16)
    lo = (r1 - mid.astype(F32)).astype(BF16)
    return hi, mid, lo


def _mid(x, tgt, o, hm, hg, woa, wob, wout, ln_g, ln_b, sg_g, sg_b, w_s, bsb):
    s = x.shape[0]
    ts = MID_ROW_TILE
    nsteps = s // ts
    nch = ts // CHUNK
    npair = GROUPS // 2

    def body(x_ref, t_ref, o_ref, hm_ref, hg_ref, woa_ref, wob_ref, wout_ref, lng_ref, lnb_ref, sgg_ref, sgb_ref,
             ws_ref, bsb_ref,
             dr_ref, dhg_ref, dhm_ref, do_ref, dot_ref, dl_ref, dhgt_ref, dhmt_ref,
             dwout_ref, dwoa_ref, dwob_ref, dws_ref, dbs_ref, dlng_ref, dlnb_ref, dsgg_ref, dsgb_ref, loss_ref,
             dbg_ref, dbm_ref, dbacc_ref, awout_ref, awoa_ref, awob_ref):
        i = pl.program_id(0)

        @pl.when(i == 0)
        def _():
            for r in (awout_ref, awoa_ref, awob_ref, dws_ref, dlng_ref, dlnb_ref, dsgg_ref, dsgb_ref, loss_ref,
                      dbg_ref, dbm_ref, dbacc_ref):
                r[...] = jnp.zeros_like(r)

        def emit(ref, tref, bref, lo, val):
            vb = val.astype(BF16)
            n = val.shape[1]
            ref[:, lo:lo + n] = vb
            tref[lo:lo + n, :] = vb.T
            bref[:, lo:lo + n] += jnp.sum(val, axis=0, keepdims=True)

        lane = lax.broadcasted_iota(jnp.int32, (CHUNK, CHUNK), 1)
        left = lane < V_DIM
        tril = lax.broadcasted_iota(jnp.int32, (CHUNK, CHUNK), 0) >= lane
        ms = [jnp.where(tril, ws_ref[g], 0.0).astype(BF16) for g in range(GROUPS)]

        z_a = hm_ref[:, 0:SGU_W].astype(F32)
        u = hm_ref[:, SGU_W:2 * SGU_W].astype(F32)
        v = hm_ref[:, 2 * SGU_W:3 * SGU_W].astype(F32)
        z_b = hm_ref[:, 3 * SGU_W:4 * SGU_W].astype(F32)
        o = o_ref[...]
        sa, dsa = _silu_and_grad(z_a)
        y_a = (o * sa).astype(BF16)
        gu, dgu = _gelu_and_grad(u)
        gv, dgv = _gelu_and_grad(v)
        mu = jnp.mean(gv, axis=-1, keepdims=True)
        vc = gv - mu
        rstd_v = lax.rsqrt(jnp.mean(vc * vc, axis=-1, keepdims=True) + LN_EPS)
        vhat = vc * rstd_v
        vn = (vhat * sgg_ref[...] + sgb_ref[...]).astype(BF16)
        rows = []
        for c in range(nch):
            blocks = []
            for p in range(npair):
                blk = vn[c * CHUNK:(c + 1) * CHUNK, p * CHUNK:(p + 1) * CHUNK]
                blocks.append(jnp.where(left, _dot(ms[2 * p], blk), _dot(ms[2 * p + 1], blk)))
            rows.append(jnp.concatenate(blocks, axis=1) + bsb_ref[...])
        mixed = jnp.concatenate(rows, axis=0)
        sgu = gu * mixed
        sb, dsb = _silu_and_grad(z_b)
        y_b = (sgu * sb).astype(BF16)
        pa = jnp.concatenate([_dot(y_a, woa_ref[k]) for k in range(N_SLABS)], axis=1)
        pb = jnp.concatenate([_dot(y_b, wob_ref[k]) for k in range(N_SLABS)], axis=1)
        sga = _sigmoid(hg_ref[:, :D_MODEL].astype(F32))
        sgb = _sigmoid(hg_ref[:, D_MODEL:].astype(F32))
        m2 = (sga * pa + sgb * pb).astype(BF16)
        r = ALPHA * x_ref[...] + _dot(m2, wout_ref[...])
        rmu = jnp.mean(r, axis=-1, keepdims=True)
        rc = r - rmu
        rstd = lax.rsqrt(jnp.mean(rc * rc, axis=-1, keepdims=True) + LN_EPS)
        xhat = rc * rstd
        y = xhat * lng_ref[...] + lnb_ref[...]
        err = y - t_ref[...]
        loss_ref[...] += jnp.full(loss_ref.shape, 0.5 / D_MODEL, F32) * jnp.sum(err * err)

        dy = err * (1.0 / D_MODEL)
        dlng_ref[...] += jnp.sum(dy * xhat, axis=0, keepdims=True)
        dlnb_ref[...] += jnp.sum(dy, axis=0, keepdims=True)
        dxh = dy * lng_ref[...]
        dr = rstd * (dxh - jnp.mean(dxh, axis=-1, keepdims=True) - xhat * jnp.mean(dxh * xhat, axis=-1, keepdims=True))
        dr_ref[...] = dr
        drb = dr.astype(BF16)
        awout_ref[...] += _dot_tn(m2, drb)
        dm2 = _dot_nt(drb, wout_ref[...])
        emit(dhg_ref, dhgt_ref, dbg_ref, 0, dm2 * pa * sga * (1.0 - sga))
        emit(dhg_ref, dhgt_ref, dbg_ref, D_MODEL, dm2 * pb * sgb * (1.0 - sgb))
        dpa = (dm2 * sga).astype(BF16)
        dpb = (dm2 * sgb).astype(BF16)
        dy_a = jnp.zeros((ts, MLA_W), F32)
        dy_b = jnp.zeros((ts, SGU_W), F32)
        y_at, y_bt = y_a.T, y_b.T
        for k in range(N_SLABS):
            cols = slice(k * SLAB_W, (k + 1) * SLAB_W)
            awoa_ref[k] += _dot(y_at, dpa[:, cols])
            awob_ref[k] += _dot(y_bt, dpb[:, cols])
            dy_a = dy_a + _dot_nt(dpa[:, cols], woa_ref[k])
            dy_b = dy_b + _dot_nt(dpb[:, cols], wob_ref[k])
        dob = (dy_a * sa).astype(BF16)
        do_ref[...] = dob
        dot_ref[...] = dob.T
        head = (lax.broadcasted_iota(jnp.int32, (HEADS, MLA_W), 1) // V_DIM
                == lax.broadcasted_iota(jnp.int32, (HEADS, MLA_W), 0)).astype(BF16)
        dl = sum(_dot_nt(head, term) for term in _split3(dob.astype(F32) * o))
        for p in range(HEADS // 2):
            dl_ref[p] = jnp.concatenate([dl[2 * p:2 * p + 2], jnp.zeros((6, ts), F32)], axis=0)
        emit(dhm_ref, dhmt_ref, dbm_ref, 0, dy_a * o * dsa)
        dsg = dy_b * sb
        emit(dhm_ref, dhmt_ref, dbm_ref, 3 * SGU_W, dy_b * sgu * dsb)
        emit(dhm_ref, dhmt_ref, dbm_ref, SGU_W, dsg * mixed * dgu)
        dmixed = dsg * gu
        dvn_rows = []
        dbs_sum = jnp.zeros((CHUNK, SGU_W), F32)
        for c in range(nch):
            dm_c = dmixed[c * CHUNK:(c + 1) * CHUNK, :]
            dbs_sum = dbs_sum + dm_c
            blocks = []
            for p in range(npair):
                dmb = dm_c[:, p * CHUNK:(p + 1) * CHUNK].astype(BF16)
                blk = vn[c * CHUNK:(c + 1) * CHUNK, p * CHUNK:(p + 1) * CHUNK]
                blocks.append(jnp.where(left, _dot_tn(ms[2 * p], dmb), _dot_tn(ms[2 * p + 1], dmb)))
                zero = jnp.zeros_like(dmb)
                dws_ref[2 * p] += jnp.where(tril, _dot_nt(jnp.where(left, dmb, zero), blk), 0.0)
                dws_ref[2 * p + 1] += jnp.where(tril, _dot_nt(jnp.where(left, zero, dmb), blk), 0.0)
            dvn_rows.append(jnp.concatenate(blocks, axis=1))
        dbacc_ref[...] += dbs_sum
        dvn = jnp.concatenate(dvn_rows, axis=0)
        dsgg_ref[...] += jnp.sum(dvn * vhat, axis=0, keepdims=True)
        dsgb_ref[...] += jnp.sum(dvn, axis=0, keepdims=True)
        dvh = dvn * sgg_ref[...]
        dgv_in = rstd_v * (dvh - jnp.mean(dvh, axis=-1, keepdims=True)
                           - vhat * jnp.mean(dvh * vhat, axis=-1, keepdims=True))
        emit(dhm_ref, dhmt_ref, dbm_ref, 2 * SGU_W, dgv_in * dgv)

        @pl.when(i == nsteps - 1)
        def _():
            dwout_ref[...] = awout_ref[...].astype(BF16)
            dwoa_ref[...] = awoa_ref[...].astype(BF16)
            dwob_ref[...] = awob_ref[...].astype(BF16)
            grp = (lax.broadcasted_iota(jnp.int32, (SGU_W, CHUNK), 0) // V_DIM
                   == lax.broadcasted_iota(jnp.int32, (SGU_W, CHUNK), 1)).astype(BF16)
            hi, mid, lo = _split3(dbacc_ref[...])
            dbs_ref[...] = _dot(hi, grp) + _dot(mid, grp) + _dot(lo, grp)

    acc_shapes = [(D_MODEL, D_MODEL), woa.shape, wob.shape, (GROUPS, CHUNK, CHUNK), (CHUNK, CHUNK),
                  (1, D_MODEL), (1, D_MODEL), (1, SGU_W), (1, SGU_W), (1, 128), (1, GATE_W), (1, MID_W)]
    col_spec = lambda rows: pl.BlockSpec((rows, ts), lambda i: (0, i))
    return pl.pallas_call(
        body, name="mid", grid=(nsteps,),
        in_specs=[_row_spec(ts, D_MODEL), _row_spec(ts, D_MODEL), _row_spec(ts, MLA_W), _row_spec(ts, MID_W),
                  _row_spec(ts, GATE_W), _full_spec(woa.shape), _full_spec(wob.shape), _full_spec(wout.shape),
                  _full_spec(ln_g.shape), _full_spec(ln_b.shape), _full_spec(sg_g.shape), _full_spec(sg_b.shape),
                  _full_spec(w_s.shape), _full_spec(bsb.shape)],
        out_specs=[_row_spec(ts, D_MODEL), _row_spec(ts, GATE_W), _row_spec(ts, MID_W), _row_spec(ts, MLA_W),
                   col_spec(MLA_W), pl.BlockSpec((HEADS // 2, 8, ts), lambda i: (0, 0, i)), col_spec(GATE_W),
                   col_spec(MID_W)]
        + [_full_spec(sh) for sh in acc_shapes],
        out_shape=[jax.ShapeDtypeStruct((s, D_MODEL), F32), jax.ShapeDtypeStruct((s, GATE_W), BF16),
                   jax.ShapeDtypeStruct((s, MID_W), BF16), jax.ShapeDtypeStruct((s, MLA_W), BF16),
                   jax.ShapeDtypeStruct((MLA_W, s), BF16), jax.ShapeDtypeStruct((HEADS // 2, 8, s), F32),
                   jax.ShapeDtypeStruct((GATE_W, s), BF16), jax.ShapeDtypeStruct((MID_W, s), BF16)]
        + [jax.ShapeDtypeStruct(sh, BF16 if n < 3 else F32) for n, sh in enumerate(acc_shapes)],
        scratch_shapes=[pltpu.VMEM((CHUNK, SGU_W), F32)] + [pltpu.VMEM(sh, F32) for sh in acc_shapes[:3]],
        compiler_params=_params(),
    )(x, tgt, o, hm, hg, woa, wob, wout, ln_g, ln_b, sg_g, sg_b, w_s, bsb)


def _lat_bwd(dq, dk, dv, hl, rc, rsl, rsh, g_q, g_kv, wuq, wk, wv, after):
    s = dk.shape[0]
    ts = ROW_TILE
    qk_w = HEADS * HEAD_PAD

    def body(dq_ref, dk_ref, dv_ref, hl_ref, rc_ref, rsl_ref, rsh_ref, gq_ref, gkv_ref, wuq_ref, wk_ref, wv_ref,
             after_ref, dhl_ref, dhlt_ref, dwuq_ref, dwk_ref, dwv_ref, dgq_ref, dgkv_ref, dbl_ref):
        i = pl.program_id(0)

        @pl.when(i == 0)
        def _():
            for r in (dwuq_ref, dwk_ref, dwv_ref, dgq_ref, dgkv_ref, dbl_ref):
                r[...] = jnp.zeros_like(r)

        def emit(lo, val):
            vb = val.astype(BF16)
            n = val.shape[1]
            dhl_ref[:, lo:lo + n] = vb
            dhlt_ref[lo:lo + n, :] = vb.T
            dbl_ref[:, lo:lo + n] += jnp.sum(val, axis=0, keepdims=True)

        c, sl, sh = rc_ref[...], rsl_ref[...], rsh_ref[...]
        lane = lax.broadcasted_iota(jnp.int32, (ts, HEAD_PAD), 1)
        pe = (lane >= NOPE) & (lane < QK_DIM)
        dkpe = jnp.zeros((ts, HEAD_PAD), F32)
        dqu = []
        for hd in range(HEADS):
            lanes = slice(hd * HEAD_PAD, (hd + 1) * HEAD_PAD)
            dqu.append(_rope_t(dq_ref[lanes, :].T, c, sl, sh).astype(BF16))
            dkpe = dkpe + dk_ref[:, lanes]
        dqu = jnp.concatenate(dqu, axis=1)
        dkpe = _rope_t(jnp.where(pe, dkpe, 0.0), c, sl, sh)

        cq = hl_ref[:, :Q_RANK]
        rq = lax.rsqrt(jnp.mean(cq * cq, axis=-1, keepdims=True) + RMS_EPS)
        cqh = cq * rq
        cqn = (cqh * gq_ref[...]).astype(BF16)
        dwuq_ref[...] += _dot_tn(cqn, dqu)
        dcqn = _dot_nt(dqu, wuq_ref[...])
        dgq_ref[...] += jnp.sum(dcqn * cqh, axis=0, keepdims=True)
        dch = dcqn * gq_ref[...]
        emit(0, rq * (dch - cqh * jnp.mean(dch * cqh, axis=-1, keepdims=True)))

        ckv = hl_ref[:, Q_RANK:Q_RANK + KV_RANK]
        rk = lax.rsqrt(jnp.mean(ckv * ckv, axis=-1, keepdims=True) + RMS_EPS)
        ckh = ckv * rk
        ckn = (ckh * gkv_ref[...]).astype(BF16)
        dkb = dk_ref[...].astype(BF16)
        dvb = dv_ref[...].astype(BF16)
        dwk_ref[...] += _dot_tn(ckn, dkb)
        dwv_ref[...] += _dot_tn(ckn, dvb)
        dckn = _dot_nt(dkb, wk_ref[...]) + _dot_nt(dvb, wv_ref[...])
        dgkv_ref[...] += jnp.sum(dckn * ckh, axis=0, keepdims=True)
        dkh = dckn * gkv_ref[...]
        emit(Q_RANK, rk * (dkh - ckh * jnp.mean(dkh * ckh, axis=-1, keepdims=True)))
        emit(Q_RANK + KV_RANK, dkpe)

    acc_shapes = [wuq.shape, wk.shape, wv.shape, g_q.shape, g_kv.shape, (1, LAT_W)]
    return pl.pallas_call(
        body, name="lat_bwd", grid=(s // ts,),
        in_specs=[pl.BlockSpec((qk_w, ts), lambda i: (0, i)), _row_spec(ts, qk_w), _row_spec(ts, MLA_W),
                  _row_spec(ts, LAT_W), _row_spec(ts, HEAD_PAD), _row_spec(ts, HEAD_PAD), _row_spec(ts, HEAD_PAD),
                  _full_spec(g_q.shape), _full_spec(g_kv.shape), _full_spec(wuq.shape), _full_spec(wk.shape),
                  _full_spec(wv.shape), pl.BlockSpec(memory_space=pl.ANY)],
        out_specs=[_row_spec(ts, LAT_W), pl.BlockSpec((LAT_W, ts), lambda i: (0, i))]
        + [_full_spec(sh) for sh in acc_shapes],
        out_shape=[jax.ShapeDtypeStruct((s, LAT_W), BF16), jax.ShapeDtypeStruct((LAT_W, s), BF16)]
        + [jax.ShapeDtypeStruct(sh, F32) for sh in acc_shapes],
        compiler_params=_params(),
    )(dq, dk, dv, hl, rc, rsl, rsh, g_q, g_kv, wuq, wk, wv, after)


def _dx(dr, dhg, dhm, dhl, wt, wlat, after):
    s = dr.shape[0]
    ts = MATMUL_ROW_TILE
    nsteps = s // ts

    tk = D_MODEL
    bounds = ([(ROW_GATE + r0, ROW_GATE + r0 + tk) for r0 in range(0, GATE_W, tk)]
              + [(LAT_COLS + r0, LAT_COLS + r0 + tk) for r0 in range(0, MID_W, tk)])
    n_gate = GATE_W // tk
    streams = (dr, dhg, dhm, dhl)

    def body(dr_hbm, dhg_hbm, dhm_hbm, dhl_hbm, wt_hbm, wlat_ref, after_ref, dx_ref, wt_ref, w_sems,
             dr_buf, dhg_buf, dhm_buf, dhl_buf, tile_sems):
        i = pl.program_id(0)
        w_copies = [pltpu.make_async_copy(wt_hbm.at[lo:hi], wt_ref.at[lo:hi], w_sems.at[n])
                    for n, (lo, hi) in enumerate(bounds)]

        def tile_copies(step):
            start = step * ts if isinstance(step, int) else pl.multiple_of(step * ts, ts)
            slot = step % DX_RING
            return [pltpu.make_async_copy(src.at[pl.ds(start, ts)], buf.at[slot], tile_sems.at[len(streams) * slot + n])
                    for n, (src, buf) in enumerate(zip((dr_hbm, dhg_hbm, dhm_hbm, dhl_hbm),
                                                       (dr_buf, dhg_buf, dhm_buf, dhl_buf)))]

        @pl.when(i == 0)
        def _():
            t_dr, t_dhg, t_dhm, t_dhl = tile_copies(0)
            t_dhg.start()
            for cp in w_copies[:n_gate]:
                cp.start()
            t_dhm.start()
            for cp in w_copies[n_gate:]:
                cp.start()
            t_dr.start()
            t_dhl.start()
            for step in range(1, min(DX_RING - 1, nsteps)):
                for cp in tile_copies(step):
                    cp.start()

        @pl.when(i + DX_RING - 1 < nsteps)
        def _():
            for cp in tile_copies(i + DX_RING - 1):
                cp.start()

        def compute(first):
            slot = i % DX_RING
            t_dr, t_dhg, t_dhm, t_dhl = tile_copies(i)
            acc = None
            for n, (lo, hi) in enumerate(bounds):
                if first:
                    w_copies[n].wait()
                if n == 0:
                    t_dhg.wait()
                if n == n_gate:
                    t_dhm.wait()
                if lo >= ROW_GATE:
                    part = _dot(dhg_buf[slot, :, lo - ROW_GATE:hi - ROW_GATE], wt_ref[lo:hi, :])
                else:
                    part = _dot(dhm_buf[slot, :, lo - LAT_COLS:hi - LAT_COLS], wt_ref[lo:hi, :])
                acc = part if acc is None else acc + part
            t_dr.wait()
            t_dhl.wait()
            dx_ref[...] = (acc + ALPHA * dr_buf[slot]
                           + _dot(dhl_buf[slot, :, 0:Q_RANK + KV_RANK], wlat_ref[0:Q_RANK + KV_RANK, :])
                           + _dot(dhl_buf[slot, :, Q_RANK + KV_RANK:], _kpe_rows(wlat_ref)))

        @pl.when(i == 0)
        def _():
            compute(True)

        @pl.when(i > 0)
        def _():
            compute(False)

    return pl.pallas_call(
        body, name="dx", grid=(nsteps,),
        in_specs=[HBM_SPEC] * 5 + [_full_spec(wlat.shape), HBM_SPEC],
        out_specs=_row_spec(ts, D_MODEL),
        out_shape=jax.ShapeDtypeStruct((s, D_MODEL), F32),
        scratch_shapes=[pltpu.VMEM(wt.shape, BF16), pltpu.SemaphoreType.DMA((len(bounds),))]
        + [pltpu.VMEM((DX_RING, ts, a.shape[1]), a.dtype) for a in streams]
        + [pltpu.SemaphoreType.DMA((DX_RING * len(streams),))],
        compiler_params=_params(),
    )(dr, dhg, dhm, dhl, wt, wlat, after)


def _dwt_early(dhmt, dhgt, xb2, col, after):
    tn = 512
    nm, ng = MID_W // tn, GATE_W // tn
    s = dhmt.shape[1]
    hc = D_MODEL // 2

    ks = s // 2

    def body(col_ref, dma_ref, dmb_ref, dga_ref, dgb_ref, xoa_ref, xob_ref, xca_ref, xcb_ref, after_ref,
             dwo_ref, dwc_ref):
        i = pl.program_id(0)

        def both(da_ref, db_ref):
            da, db = da_ref[...], db_ref[...]
            dwo_ref[...] = (_dot(da, xoa_ref[...]) + _dot(db, xob_ref[...])).astype(BF16)
            dwc_ref[...] = (_dot(da, xca_ref[...]) + _dot(db, xcb_ref[...])).astype(BF16)

        @pl.when(i < nm)
        def _():
            both(dma_ref, dmb_ref)

        @pl.when(i >= nm)
        def _():
            both(dga_ref, dgb_ref)

    def dh_spec(first, part):
        if first:
            return pl.BlockSpec((tn, ks), lambda i, col_ref: (jnp.minimum(i, nm - 1), part))
        return pl.BlockSpec((tn, ks), lambda i, col_ref: (jnp.maximum(i - nm, 0), part))

    def x_spec(other, part):
        if other:
            return pl.BlockSpec((None, ks, hc), lambda i, col_ref: (1 - col_ref[0], part, 0))
        return pl.BlockSpec((None, ks, hc), lambda i, col_ref: (col_ref[0], part, 0))

    out_spec = pl.BlockSpec((pl.Element(tn), pl.Element(hc)),
                            lambda i, col_ref: (pl.multiple_of(LAT_COLS + i * tn, 32), 0))
    rows = pl.pallas_call(
        body, name="dwt_early",
        grid_spec=pltpu.PrefetchScalarGridSpec(
            num_scalar_prefetch=1, grid=(nm + ng,),
            in_specs=[dh_spec(True, 0), dh_spec(True, 1), dh_spec(False, 0), dh_spec(False, 1),
                      x_spec(True, 0), x_spec(True, 1), x_spec(False, 0), x_spec(False, 1),
                      pl.BlockSpec(memory_space=pl.ANY)],
            out_specs=[out_spec, out_spec]),
        out_shape=[jax.ShapeDtypeStruct((IN_W, hc), BF16)] * 2,
        compiler_params=_params(),
    )(col, dhmt, dhmt, dhgt, dhgt, xb2, xb2, xb2, xb2, after)

    def zero(other_ref, mine_ref, other_out, mine_out):
        other_out[...] = jnp.zeros_like(other_out)
        mine_out[...] = jnp.zeros_like(mine_out)

    lat_rows = pl.BlockSpec((LAT_COLS, hc), lambda i: (0, 0))
    return pl.pallas_call(
        zero, name="dwt_early_zero_lat", grid=(1,), in_specs=[pl.BlockSpec(memory_space=pl.ANY)] * 2,
        out_specs=[lat_rows, lat_rows],
        out_shape=[jax.ShapeDtypeStruct((IN_W, hc), BF16)] * 2, input_output_aliases={0: 0, 1: 1},
    )(*rows)


def _dwt_lat(dhlt, xb2):
    n, s = dhlt.shape
    tk = MATMUL_ROW_TILE
    nsteps = s // tk

    def body(dht_ref, xb_ref, dw_ref, acc_ref):
        i = pl.program_id(0)

        @pl.when(i == 0)
        def _():
            acc_ref[...] = jnp.zeros_like(acc_ref)

        dht = dht_ref[...]
        acc_ref[...] += jnp.concatenate([_dot(dht, xb_ref[0]), _dot(dht, xb_ref[1])], axis=1)

        @pl.when(i == nsteps - 1)
        def _():
            kpe = Q_RANK + KV_RANK + NOPE
            dw_ref[0:Q_RANK + KV_RANK, :] = acc_ref[0:Q_RANK + KV_RANK, :].astype(BF16)
            dw_ref[Q_RANK + KV_RANK:LAT_COLS, :] = acc_ref[kpe:kpe + ROPE, :].astype(BF16)
            dw_ref[LAT_COLS:, :] = jnp.zeros((LAT_ROWS_PAD - LAT_COLS, D_MODEL), BF16)

    return pl.pallas_call(
        body, name="dwt_lat", grid=(nsteps,),
        in_specs=[pl.BlockSpec((n, tk), lambda i: (0, i)), pl.BlockSpec((2, tk, D_MODEL // 2), lambda i: (0, i, 0))],
        out_specs=_full_spec((LAT_ROWS_PAD, D_MODEL)),
        out_shape=jax.ShapeDtypeStruct((LAT_ROWS_PAD, D_MODEL), BF16),
        scratch_shapes=[pltpu.VMEM((n, D_MODEL), F32)],
        compiler_params=_params(),
    )(dhlt, xb2)


def _split_bias(b):
    z = lambda n: jnp.zeros((n,), b.dtype)
    lat = jnp.concatenate([b[:Q_RANK + KV_RANK], z(NOPE), b[Q_RANK + KV_RANK:LAT_COLS], z(HEAD_PAD - QK_DIM)])
    return b[None, ROW_GATE:], b[None, LAT_COLS:ROW_GATE], lat[None, :]


def _join_bias(g, m, l):
    kpe = Q_RANK + KV_RANK + NOPE
    return jnp.concatenate([l[0, :Q_RANK + KV_RANK], l[0, kpe:kpe + ROPE], m[0], g[0]])


def _rope_tables(positions):
    half = ROPE // 2
    inv_freq = ROPE_THETA ** (-jnp.arange(0, ROPE, 2, dtype=F32) / ROPE)
    ang = positions.astype(F32)[:, None] * inv_freq
    cos, sin = jnp.cos(ang), jnp.sin(ang)
    n = positions.shape[0]
    one, zero = jnp.ones((n, NOPE), F32), jnp.zeros((n, half), F32)
    tail1, tail0 = jnp.ones((n, HEAD_PAD - QK_DIM), F32), jnp.zeros((n, HEAD_PAD - QK_DIM), F32)
    z64 = jnp.zeros((n, NOPE), F32)
    rc = jnp.concatenate([one, cos, cos, tail1], axis=1)
    rsl = jnp.concatenate([z64, -sin, zero, tail0], axis=1)
    rsh = jnp.concatenate([z64, zero, sin, tail0], axis=1)
    return rc, rsl, rsh


def _pad_heads(w_uq):
    return jnp.pad(w_uq, ((0, 0), (0, 0), (0, HEAD_PAD - QK_DIM))).reshape(w_uq.shape[0], HEADS * HEAD_PAD)


def _prep_local(b_in, g_q, g_kv, w_ukv):
    b_g, b_m, b_l = _split_bias(b_in)
    wk = jnp.pad(w_ukv[:, :, :NOPE], ((0, 0), (0, 0), (0, HEAD_PAD - NOPE))).reshape(KV_RANK, HEADS * HEAD_PAD).astype(BF16)
    wv = w_ukv[:, :, NOPE:].reshape(KV_RANK, MLA_W).astype(BF16)
    return dict(b_g=b_g, b_m=b_m, b_l=b_l, wk=wk, wv=wv, gq2=g_q[None, :], gkv2=g_kv[None, :])


def _local_attention(x, tables, wlat, prep, wuq, after):
    rc, rsl, rsh = tables
    b_g, b_m, b_l, wk, wv, gq2, gkv2 = (prep[n] for n in ("b_g", "b_m", "b_l", "wk", "wv", "gq2", "gkv2"))
    hl, q, k, v, qt, kt, vt, xb2 = _fwd_lat(x, wlat, b_l, gq2, wuq, gkv2, wk, wv, rc, rsl, rsh, after)
    o, lse = _attn_fwd(qt, k, vt)
    return dict(q=q, qt=qt, k=k, kt=kt, v=v, o=o, lse=lse, hl=hl, rc=rc, rsl=rsl, rsh=rsh, gq2=gq2, gkv2=gkv2,
                wuq=wuq, wk=wk, wv=wv, xb2=xb2, b_g=b_g, b_m=b_m, wlat=wlat)


def _bias_lanes(b_s):
    return jnp.repeat(b_s.T, V_DIM, axis=1)


def _local_head(st, x, tgt, wt, w_oa, sg_g, sg_b, w_s, bsb, w_ob, w_out, ln_g, ln_b):
    q, qt, k, kt, v, o, lse, hl = (st[n] for n in ("q", "qt", "k", "kt", "v", "o", "lse", "hl"))
    rc, rsl, rsh, gq2, gkv2, wuq, wk, wv = (st[n] for n in ("rc", "rsl", "rsh", "gq2", "gkv2", "wuq", "wk", "wv"))
    hg, hm = _fwd_rest(st["xb2"], wt, st["b_g"], st["b_m"])
    (dr, dhg, dhm, do, dot, delta, dhgt, dhmt, dwout, dwoa, dwob, dws, dbs, dlng, dlnb, dsgg, dsgb, loss, dbg,
     dbm) = _mid(x, tgt, o, hm, hg, w_oa, w_ob, w_out, ln_g[None, :], ln_b[None, :], sg_g[None, :], sg_b[None, :],
                 w_s, bsb)
    early = {
        "w_oa": dwoa, "sgu_ln_g": dsgg[0], "sgu_ln_b": dsgb[0], "w_s": dws, "b_s": dbs[:, :GROUPS].T,
        "w_ob": dwob, "w_out": dwout, "ln_g": dlng[0], "ln_b": dlnb[0],
    }
    state = dict(q=q, qt=qt, k=k, kt=kt, v=v, do=do, dot=dot, lse=lse, delta=delta, hl=hl, rc=rc, rsl=rsl, rsh=rsh,
                 gq2=gq2, gkv2=gkv2, wuq=wuq, wk=wk, wv=wv, dr=dr, dhg=dhg, dhm=dhm, wt=wt, dbg=dbg, dbm=dbm,
                 dhgt=dhgt, dhmt=dhmt, xb2=st["xb2"], wlat=st["wlat"])
    return loss, early, state


def _local_attn_bwd(st, after, first_pair=0, n_pairs=HEADS // 2, into=(), name="attn_bwd"):
    return _attn_bwd(st["q"], st["qt"], st["k"], st["kt"], st["v"], st["do"], st["dot"], st["lse"], st["delta"],
                     after, first_pair, n_pairs, into, name)


def _local_tail(st, dq, dk, dv, after):
    dhl, dhlt, dwuq, dwk, dwv, dgq, dgkv, dbl = _lat_bwd(dq, dk, dv, st["hl"], st["rc"], st["rsl"], st["rsh"],
                                                         st["gq2"], st["gkv2"], st["wuq"], st["wk"], st["wv"], after)
    late = {
        "w_lat": _dwt_lat(dhlt, st["xb2"]),
        "b_in": _join_bias(st["dbg"], st["dbm"], dbl),
        "g_q": dgq[0],
        "w_uq": dwuq.reshape(Q_RANK, HEADS, HEAD_PAD)[:, :, :QK_DIM],
        "g_kv": dgkv[0],
        "w_ukv": jnp.concatenate([dwk.reshape(KV_RANK, HEADS, HEAD_PAD)[:, :, :NOPE],
                                  dwv.reshape(KV_RANK, HEADS, V_DIM)], axis=2),
    }
    return dhl, late


def _local_step(x, positions, tgt, wt, b_in, g_q, w_uq, g_kv, w_ukv, w_oa, sg_g, sg_b, w_s, b_s, w_ob, w_out, ln_g,
                ln_b):
    st = _local_attention(x, _rope_tables(positions), wt[:LAT_COLS], _prep_local(b_in, g_q, g_kv, w_ukv),
                          _pad_heads(w_uq).astype(BF16), b_in)
    loss, early, st = _local_head(st, x, tgt, wt, w_oa, sg_g, sg_b, w_s, _bias_lanes(b_s), w_ob, w_out, ln_g, ln_b)
    dq, dk, dv = _local_attn_bwd(st, loss)
    dhl, late = _local_tail(st, dq, dk, dv, dv)
    dx = _dx(st["dr"], st["dhg"], st["dhm"], dhl, st["wt"], st["wlat"], dhl)
    grads = {**early, **late}
    right, left = _dwt_early(st["dhmt"], st["dhgt"], st["xb2"], jnp.zeros((1,), jnp.int32), dhl)
    grads["w_in"] = jnp.concatenate([grads.pop("w_lat")[:LAT_COLS], jnp.concatenate([left, right], axis=1)[LAT_COLS:]],
                                    axis=0)
    return loss, dx, grads


MESH = pl.DeviceIdType.MESH
N_CHIPS = 4
HBM_SPEC = pl.BlockSpec(memory_space=pl.ANY)
HBM_SPEC_STRICT = pl.BlockSpec(memory_space=pltpu.HBM)
VMEM_SPEC = pl.BlockSpec(memory_space=pltpu.VMEM)

REP_ROWS = 80


def _rows8(a):
    flat = a.reshape(-1)
    n = -(-flat.shape[0] // (8 * D_MODEL)) * 8 * D_MODEL
    return jnp.pad(flat, (0, n - flat.shape[0])).reshape(-1, D_MODEL)


def _place():
    x, y, c = lax.axis_index("x"), lax.axis_index("y"), lax.axis_index("c")
    others = [(1 - x, y), (x, 1 - y), (1 - x, 1 - y)]
    return x, y, c, others


N_DEV = 8
LOSS_TILE = (8, 128)


def _cast_own(shards, me, after):
    n = len(shards)

    def body(me_ref, *refs):
        for w in range(n):
            refs[n + 1 + w][...] = refs[w][...].astype(BF16)

    return pl.pallas_call(
        body, name="cast_own",
        grid_spec=pltpu.PrefetchScalarGridSpec(
            num_scalar_prefetch=1, grid=(1,),
            in_specs=[pl.BlockSpec(s.shape, lambda i, me_ref: (0, 0)) for s in shards]
            + [pl.BlockSpec(memory_space=pl.ANY)],
            out_specs=[pl.BlockSpec((None,) + s.shape, lambda i, me_ref: (me_ref[0], 0, 0)) for s in shards]),
        out_shape=[jax.ShapeDtypeStruct((N_CHIPS,) + s.shape, BF16) for s in shards],
        compiler_params=pltpu.CompilerParams(vmem_limit_bytes=VMEM_LIMIT),
    )(me, *shards, after)


def _cast_first(lat, uq, me):
    def body(me_ref, lat_ref, uq_ref, wlat_ref, guq_ref):
        wlat_ref[...] = lat_ref[...].astype(BF16)
        guq_ref[...] = uq_ref[...].astype(BF16)

    return pl.pallas_call(
        body, name="cast_first",
        grid_spec=pltpu.PrefetchScalarGridSpec(
            num_scalar_prefetch=1, grid=(1,),
            in_specs=[pl.BlockSpec((LAT_COLS, D_MODEL), lambda i, me_ref: (0, 0)),
                      pl.BlockSpec(uq.shape, lambda i, me_ref: (0, 0))],
            out_specs=[pl.BlockSpec((LAT_COLS, D_MODEL), lambda i, me_ref: (0, 0)),
                       pl.BlockSpec((None,) + uq.shape, lambda i, me_ref: (me_ref[0], 0, 0))]),
        out_shape=[jax.ShapeDtypeStruct((LAT_COLS, D_MODEL), BF16),
                   jax.ShapeDtypeStruct((N_CHIPS,) + uq.shape, BF16)],
    )(me, lat, uq)


def _first_copies(wlat_ref, guq_ref, send_sems, recv_sems, shapes):
    x, y, c, others = _place()
    me = 2 * x + y
    hl, hu = shapes[0][1] // 2, shapes[1][2] // 2
    lat_half = wlat_ref.at[:, pl.ds(c * hl, hl)]

    def copy(src, dst, k, to):
        return pltpu.make_async_remote_copy(src_ref=src, dst_ref=dst, send_sem=send_sems.at[k],
                                            recv_sem=recv_sems.at[k], device_id=to, device_id_type=MESH)

    def uq_half(chip):
        return guq_ref.at[chip, :, pl.ds(c * hu, hu)]

    lat_out = [copy(lat_half, lat_half, j, (*others[j], c)) for j in range(3)]
    uq_out = [copy(uq_half(me), uq_half(me), 3 + j, (*others[j], c)) for j in range(3)]
    j0 = jnp.maximum(x + 2 * y - 1, 0)
    lat_in = copy(lat_half, lat_half, j0, (0, 0, c))
    uq_in = [copy(uq_half(me), uq_half(2 * px + py), 3 + j, (px, py, c)) for j, (px, py) in enumerate(others)]
    return me, lat_out, uq_out, lat_in, uq_in


def _first_start(wlat, guq):
    shapes = (wlat.shape, guq.shape)

    def body(wlat_ref, guq_ref, send_sems, recv_sems, wlat_thru, guq_thru, token):
        me, lat_out, uq_out, _, _ = _first_copies(wlat_ref, guq_ref, send_sems, recv_sems, shapes)

        @pl.when(me == 0)
        def _():
            for cp in lat_out:
                cp.start()

        for cp in uq_out:
            cp.start()
        token[...] = jnp.zeros_like(token)

    outs = pl.pallas_call(
        body, name="first_start",
        out_shape=(pltpu.SemaphoreType.DMA((6,)), pltpu.SemaphoreType.DMA((6,)), pltpu.HBM(wlat.shape, BF16),
                   pltpu.HBM(guq.shape, BF16), jax.ShapeDtypeStruct(LOSS_TILE, F32)),
        in_specs=[HBM_SPEC_STRICT] * 2, out_specs=(SEM_SPEC, SEM_SPEC, HBM_SPEC_STRICT, HBM_SPEC_STRICT, VMEM_SPEC),
        input_output_aliases={0: 2, 1: 3},
        compiler_params=pltpu.CompilerParams(has_side_effects=SPLIT_EFFECT),
    )(pltpu.with_memory_space_constraint(wlat, pltpu.HBM), pltpu.with_memory_space_constraint(guq, pltpu.HBM))
    return outs


def _first_wait(send_sems, recv_sems, wlat, guq, *after):
    shapes = (wlat.shape, guq.shape)

    def body(wlat_ref, guq_ref, send_sems, recv_sems, *rest):
        me, lat_out, uq_out, lat_in, uq_in = _first_copies(wlat_ref, guq_ref, send_sems, recv_sems, shapes)

        @pl.when(me == 0)
        def _():
            for cp in lat_out:
                cp.wait_send()

        @pl.when(me != 0)
        def _():
            lat_in.wait_recv()

        for cp in uq_out:
            cp.wait_send()
        for cp in uq_in:
            cp.wait_recv()

    return pl.pallas_call(
        body, name="first_wait", out_shape=(pltpu.HBM(wlat.shape, BF16), pltpu.HBM(guq.shape, BF16)),
        in_specs=[HBM_SPEC_STRICT, HBM_SPEC_STRICT, SEM_SPEC, SEM_SPEC] + [HBM_SPEC] * len(after),
        out_specs=(HBM_SPEC_STRICT, HBM_SPEC_STRICT), input_output_aliases={0: 0, 1: 1},
        compiler_params=pltpu.CompilerParams(has_side_effects=SPLIT_EFFECT),
    )(wlat, guq, send_sems, recv_sems, *after)


def _first_forward(wlat, guq):
    hl, hu = wlat.shape[1] // 2, guq.shape[2] // 2

    def body(wlat_in, guq_in, wlat_ref, guq_ref, send_sems, recv_sems):
        x, y, c, others = _place()
        me = 2 * x + y
        sibling = (x, y, 1 - c)

        def copy(part, k):
            return pltpu.make_async_remote_copy(src_ref=part, dst_ref=part, send_sem=send_sems.at[k],
                                                recv_sem=recv_sems.at[k], device_id=sibling, device_id_type=MESH)

        def uq_part(j, half):
            px, py = others[j]
            return guq_ref.at[2 * px + py, :, pl.ds(half * hu, hu)]

        cps = [copy(uq_part(j, c), j) for j in range(3)]
        for cp in cps:
            cp.start()

        @pl.when(me != 0)
        def _():
            mine = copy(wlat_ref.at[:, pl.ds(c * hl, hl)], 3)
            mine.start()
            copy(wlat_ref.at[:, pl.ds((1 - c) * hl, hl)], 3).wait_recv()
            mine.wait_send()

        for j in range(3):
            copy(uq_part(j, 1 - c), j).wait_recv()
        for cp in cps:
            cp.wait_send()

    return pl.pallas_call(
        body, name="first_forward", in_specs=[HBM_SPEC, HBM_SPEC], out_specs=[HBM_SPEC, HBM_SPEC],
        out_shape=[jax.ShapeDtypeStruct(wlat.shape, BF16), jax.ShapeDtypeStruct(guq.shape, BF16)],
        input_output_aliases={0: 0, 1: 1},
        scratch_shapes=[pltpu.SemaphoreType.DMA((4,)), pltpu.SemaphoreType.DMA((4,))],
    )(wlat, guq)


def _gather_copy(b_ref, buf, w, j, src_chip, dst_chip, to, rows, send_sems, recv_sems):
    _, _, c, _ = _place()
    hc = _half(buf)
    return pltpu.make_async_remote_copy(
        src_ref=b_ref.at[src_chip, rows, pl.ds(c * hc, hc)], dst_ref=b_ref.at[dst_chip, rows, pl.ds(c * hc, hc)],
        send_sem=send_sems.at[3 * w + j], recv_sem=recv_sems.at[3 * w + j], device_id=to, device_id_type=MESH)


def _gather_rows(buf, w, chip, fn):
    if w != 0:
        fn(slice(None))
        return
    pl.when(chip == 0)(lambda: fn(pl.ds(LAT_COLS, buf.shape[1] - LAT_COLS)))
    pl.when(chip != 0)(lambda: fn(slice(None)))


def _gather_start(bufs, after):
    n = len(bufs)

    def body(*refs):
        b_refs = refs[:n]
        send_sems, recv_sems, token = refs[n + 1], refs[n + 2], refs[-1]
        x, y, c, others = _place()
        me = 2 * x + y
        for w in range(n):
            def start(rows, w=w):
                for j, (px, py) in enumerate(others):
                    _gather_copy(b_refs[w], bufs[w], w, j, me, me, (px, py, c), rows, send_sems, recv_sems).start()
            _gather_rows(bufs[w], w, me, start)
        token[...] = jnp.zeros_like(token)

    hbm = [pltpu.HBM(b.shape, BF16) for b in bufs]
    outs = pl.pallas_call(
        body, name="gather_start",
        out_shape=(pltpu.SemaphoreType.DMA((3 * n,)), pltpu.SemaphoreType.DMA((3 * n,)), *hbm,
                   jax.ShapeDtypeStruct(LOSS_TILE, F32)),
        in_specs=[HBM_SPEC_STRICT] * n + [HBM_SPEC],
        out_specs=(SEM_SPEC, SEM_SPEC, *[HBM_SPEC_STRICT] * n, VMEM_SPEC),
        input_output_aliases={i: 2 + i for i in range(n)},
        compiler_params=pltpu.CompilerParams(has_side_effects=SPLIT_EFFECT),
    )(*[pltpu.with_memory_space_constraint(b, pltpu.HBM) for b in bufs], after)
    return outs[0], outs[1], list(outs[2:2 + n]), outs[-1]


def _gather_wait(send_sems, recv_sems, bufs, after):
    n = len(bufs)

    def body(*refs):
        b_refs = refs[:n]
        send_sems, recv_sems = refs[n], refs[n + 1]
        x, y, c, others = _place()
        me = 2 * x + y
        for w in range(n):
            for j, (px, py) in enumerate(others):
                def copy(rows, w=w, j=j, px=px, py=py):
                    return _gather_copy(b_refs[w], bufs[w], w, j, me, 2 * px + py, (px, py, c), rows, send_sems,
                                        recv_sems)
                _gather_rows(bufs[w], w, me, lambda rows, copy=copy: copy(rows).wait_send())
                _gather_rows(bufs[w], w, 2 * px + py, lambda rows, copy=copy: copy(rows).wait_recv())

    outs = pl.pallas_call(
        body, name="gather_wait", out_shape=tuple(pltpu.HBM(b.shape, b.dtype) for b in bufs),
        in_specs=[HBM_SPEC_STRICT] * n + [SEM_SPEC, SEM_SPEC, HBM_SPEC],
        out_specs=tuple([HBM_SPEC_STRICT] * n), input_output_aliases={i: i for i in range(n)},
        compiler_params=pltpu.CompilerParams(has_side_effects=SPLIT_EFFECT),
    )(*bufs, send_sems, recv_sems, after)
    return list(outs)


def _gather_finish(bufs):
    n = len(bufs)

    def body(*refs):
        b_refs = refs[n:2 * n]
        send_sems, recv_sems = refs[2 * n:]
        x, y, c, others = _place()
        cps = []
        for w in range(n):
            hc = _half(bufs[w])
            for j, (px, py) in enumerate(others):
                part = b_refs[w].at[2 * px + py, :, pl.ds(c * hc, hc)]
                cps.append(pltpu.make_async_remote_copy(
                    src_ref=part, dst_ref=part, send_sem=send_sems.at[3 * w + j], recv_sem=recv_sems.at[3 * w + j],
                    device_id=(x, y, 1 - c), device_id_type=MESH))
        for cp in cps:
            cp.start()
        for w in range(n):
            hc = _half(bufs[w])
            for j, (px, py) in enumerate(others):
                theirs = b_refs[w].at[2 * px + py, :, pl.ds((1 - c) * hc, hc)]
                pltpu.make_async_remote_copy(
                    src_ref=theirs, dst_ref=theirs, send_sem=send_sems.at[3 * w + j], recv_sem=recv_sems.at[3 * w + j],
                    device_id=(x, y, 1 - c), device_id_type=MESH).wait_recv()
        for cp in cps:
            cp.wait_send()

    return pl.pallas_call(
        body, name="gather_finish", in_specs=[HBM_SPEC] * n, out_specs=[HBM_SPEC] * n,
        out_shape=[jax.ShapeDtypeStruct(b.shape, b.dtype) for b in bufs],
        input_output_aliases={i: i for i in range(n)},
        scratch_shapes=[pltpu.SemaphoreType.DMA((3 * n,)), pltpu.SemaphoreType.DMA((3 * n,))],
    )(*bufs)


def _half(a):
    return a.shape[-1] // 2


def _exchange_pairs(parts, name):
    n = len(parts)

    def body(*refs):
        p_refs, r_refs = refs[:n], refs[n:2 * n]
        send_sems, recv_sems = refs[2 * n:]
        x, y, c, _ = _place()
        cps = []
        for w in range(n):
            h = _half(parts[w])
            cps.append(pltpu.make_async_remote_copy(
                src_ref=p_refs[w].at[:, :, pl.ds((1 - c) * h, h)], dst_ref=r_refs[w],
                send_sem=send_sems.at[w], recv_sem=recv_sems.at[w], device_id=(x, y, 1 - c), device_id_type=MESH))
        for cp in cps:
            cp.start()
        for cp in cps:
            cp.wait()

    return pl.pallas_call(
        body, name=name, in_specs=[HBM_SPEC] * n, out_specs=[HBM_SPEC] * n,
        out_shape=[jax.ShapeDtypeStruct((N_CHIPS, p.shape[1], _half(p)), BF16) for p in parts],
        scratch_shapes=[pltpu.SemaphoreType.DMA((n,)), pltpu.SemaphoreType.DMA((n,))],
    )(*parts)


def _sibling_part(ref, w, n_whole, shape, c):
    if w < n_whole:
        return ref
    h = shape[-1] // 2
    return ref.at[:, :, pl.ds((1 - c) * h, h)]


def _pairs_start(parts, all_loss, n_whole):
    n = len(parts)

    def body(*refs):
        p_refs, r_refs, loss_ref = refs[:n], refs[n:2 * n], refs[2 * n]
        send_sems, recv_sems, token = refs[2 * n + 1], refs[2 * n + 2], refs[-1]
        x, y, c, _ = _place()
        for w in range(n):
            h = _half(parts[w])
            pltpu.make_async_remote_copy(
                src_ref=_sibling_part(p_refs[w], w, n_whole, parts[w].shape, c), dst_ref=r_refs[w],
                send_sem=send_sems.at[w], recv_sem=recv_sems.at[w], device_id=(x, y, 1 - c),
                device_id_type=MESH).start()
        me = 4 * x + 2 * y + c
        for t in range(1, N_DEV):
            d = (me + t) % N_DEV
            pltpu.make_async_remote_copy(
                src_ref=loss_ref.at[me], dst_ref=loss_ref.at[me], send_sem=send_sems.at[n + t - 1],
                recv_sem=recv_sems.at[n + t - 1], device_id=(d // 4, (d // 2) % 2, d % 2), device_id_type=MESH).start()
        token[...] = jnp.zeros_like(token)

    lands = [pltpu.HBM(p.shape if w < n_whole else (N_CHIPS, p.shape[1], _half(p)), BF16)
             for w, p in enumerate(parts)]
    nsem = n + N_DEV - 1
    outs = pl.pallas_call(
        body, name="pairs_start",
        out_shape=(pltpu.SemaphoreType.DMA((nsem,)), pltpu.SemaphoreType.DMA((nsem,)),
                   *[pltpu.HBM(p.shape, p.dtype) for p in parts], *lands, pltpu.HBM(all_loss.shape, F32),
                   jax.ShapeDtypeStruct(LOSS_TILE, F32)),
        in_specs=[HBM_SPEC_STRICT] * (2 * n + 1),
        out_specs=(SEM_SPEC, SEM_SPEC, *[HBM_SPEC_STRICT] * (2 * n + 1), VMEM_SPEC),
        input_output_aliases={i: 2 + i for i in range(2 * n + 1)},
        compiler_params=pltpu.CompilerParams(has_side_effects=SPLIT_EFFECT),
    )(*[pltpu.with_memory_space_constraint(p, pltpu.HBM) for p in parts],
      *[pltpu.with_memory_space_constraint(lax.empty(l.shape, BF16), pltpu.HBM) for l in lands],
      pltpu.with_memory_space_constraint(all_loss, pltpu.HBM))
    return outs[0], outs[1], list(outs[2:2 + n]), list(outs[2 + n:2 + 2 * n]), outs[2 + 2 * n], outs[-1]


def _pairs_wait(send_sems, recv_sems, parts, lands, all_loss, after, n_whole):
    n = len(parts)

    def body(*refs):
        p_refs, r_refs, loss_ref = refs[:n], refs[n:2 * n], refs[2 * n]
        send_sems, recv_sems = refs[2 * n + 1], refs[2 * n + 2]
        x, y, c, _ = _place()
        for w in range(n):
            h = _half(parts[w])
            cp = pltpu.make_async_remote_copy(
                src_ref=_sibling_part(p_refs[w], w, n_whole, parts[w].shape, c), dst_ref=r_refs[w],
                send_sem=send_sems.at[w],
                recv_sem=recv_sems.at[w], device_id=(x, y, 1 - c), device_id_type=MESH)
            cp.wait_send()
            cp.wait_recv()
        me = 4 * x + 2 * y + c
        for t in range(1, N_DEV):
            d = (me + N_DEV - t) % N_DEV
            cp = pltpu.make_async_remote_copy(
                src_ref=loss_ref.at[me], dst_ref=loss_ref.at[d], send_sem=send_sems.at[n + t - 1],
                recv_sem=recv_sems.at[n + t - 1], device_id=(d // 4, (d // 2) % 2, d % 2), device_id_type=MESH)
            cp.wait_send()
            cp.wait_recv()

    bufs = (*parts, *lands, all_loss)
    outs = pl.pallas_call(
        body, name="pairs_wait", out_shape=tuple(pltpu.HBM(a.shape, a.dtype) for a in bufs),
        in_specs=[HBM_SPEC_STRICT] * len(bufs) + [SEM_SPEC, SEM_SPEC, HBM_SPEC],
        out_specs=tuple([HBM_SPEC_STRICT] * len(bufs)), input_output_aliases={i: i for i in range(len(bufs))},
        compiler_params=pltpu.CompilerParams(has_side_effects=SPLIT_EFFECT),
    )(*bufs, send_sems, recv_sems, after)
    return list(outs[:n]), list(outs[n:2 * n]), outs[2 * n]


def _add_pair(ps, rs, c, name, n_whole=0):
    n = len(ps)

    def body(c_ref, *refs):
        for w in range(n):
            refs[2 * n + w][...] = (refs[w][...].astype(F32) + refs[n + w][...].astype(F32)).astype(BF16)

    def slab_spec(a):
        return pl.BlockSpec((None,) + a.shape[1:], lambda k, c_ref: (k, 0, 0))

    def my_half_spec(a):
        return pl.BlockSpec((None, a.shape[1], _half(a)), lambda k, c_ref: (k, 0, c_ref[0]))

    return pl.pallas_call(
        body, name=name,
        grid_spec=pltpu.PrefetchScalarGridSpec(
            num_scalar_prefetch=1, grid=(N_CHIPS,),
            in_specs=[slab_spec(p) if w < n_whole else my_half_spec(p) for w, p in enumerate(ps)]
            + [slab_spec(r) for r in rs],
            out_specs=[slab_spec(r) for r in rs]),
        out_shape=[jax.ShapeDtypeStruct(r.shape, BF16) for r in rs],
        compiler_params=_params(),
    )(c, *ps, *rs)


SEM_SPEC = pl.BlockSpec(memory_space=pltpu.SEMAPHORE)
SPLIT_EFFECT = pltpu.SideEffectType.DATAFLOW_SIDE_EFFECTING


def _chips_start(qs, name):
    n = len(qs)

    def body(*refs):
        q_refs, land_refs = refs[:n], refs[n:2 * n]
        send_sems, recv_sems, token = refs[2 * n], refs[2 * n + 1], refs[-1]
        x, y, c, others = _place()
        me = 2 * x + y
        for w in range(n):
            for j, (px, py) in enumerate(others):
                pltpu.make_async_remote_copy(
                    src_ref=q_refs[w].at[2 * px + py], dst_ref=land_refs[w].at[me], send_sem=send_sems.at[3 * w + j],
                    recv_sem=recv_sems.at[3 * w + j], device_id=(px, py, c), device_id_type=MESH).start()
        token[...] = jnp.zeros_like(token)

    hbm = [pltpu.HBM(q.shape, BF16) for q in qs]
    outs = pl.pallas_call(
        body, name=name,
        out_shape=(pltpu.SemaphoreType.DMA((3 * n,)), pltpu.SemaphoreType.DMA((3 * n,)), *hbm, *hbm,
                   jax.ShapeDtypeStruct(LOSS_TILE, F32)),
        in_specs=[HBM_SPEC_STRICT] * (2 * n),
        out_specs=(SEM_SPEC, SEM_SPEC, *[HBM_SPEC_STRICT] * (2 * n), VMEM_SPEC),
        input_output_aliases={i: 2 + i for i in range(2 * n)},
        compiler_params=pltpu.CompilerParams(has_side_effects=SPLIT_EFFECT),
    )(*[pltpu.with_memory_space_constraint(q, pltpu.HBM) for q in qs],
      *[pltpu.with_memory_space_constraint(lax.empty(q.shape, BF16), pltpu.HBM) for q in qs])
    return outs[0], outs[1], outs[2:2 + n], outs[2 + n:2 + 2 * n], outs[-1]


def _chips_wait(send_sems, recv_sems, q_thru, land_thru, after, name):
    n = len(q_thru)

    def body(*refs):
        q_refs, land_refs = refs[:n], refs[n:2 * n]
        send_sems, recv_sems = refs[2 * n], refs[2 * n + 1]
        x, y, c, others = _place()
        me = 2 * x + y
        for w in range(n):
            for j, (px, py) in enumerate(others):
                cp = pltpu.make_async_remote_copy(
                    src_ref=q_refs[w].at[2 * px + py], dst_ref=land_refs[w].at[2 * px + py],
                    send_sem=send_sems.at[3 * w + j], recv_sem=recv_sems.at[3 * w + j], device_id=(px, py, c),
                    device_id_type=MESH)
                cp.wait_send()
                cp.wait_recv()

    outs = pl.pallas_call(
        body, name=name, out_shape=tuple(pltpu.HBM(a.shape, a.dtype) for a in (*q_thru, *land_thru)),
        in_specs=[HBM_SPEC_STRICT] * (2 * n) + [SEM_SPEC, SEM_SPEC, HBM_SPEC],
        out_specs=tuple([HBM_SPEC_STRICT] * (2 * n)), input_output_aliases={i: i for i in range(2 * n)},
        compiler_params=pltpu.CompilerParams(has_side_effects=SPLIT_EFFECT),
    )(*q_thru, *land_thru, send_sems, recv_sems, after)
    return list(outs[:n]), list(outs[n:])


def _sum_chips(qs, rs, idx, all_dtypes):
    n = len(rs)
    n_all = len(all_dtypes)

    def body(idx_ref, *refs):
        c = idx_ref[4]
        for w in range(n):
            q_ref, g_ref = refs[w], refs[4 * n + w]
            acc = q_ref[...].astype(F32)
            for t in range(1, N_CHIPS):
                acc = acc + refs[n + 3 * w + t - 1][...].astype(F32)
            h = rs[w].shape[2]
            mine = pl.ds(pl.multiple_of(c * h, 128), h)
            g_ref[...] = jnp.zeros_like(g_ref)
            if w >= n - n_all:
                g_ref[idx_ref[0], :, mine] = acc.astype(g_ref.dtype)
            else:
                g_ref[:, mine] = acc

    shapes = [jax.ShapeDtypeStruct((r.shape[1], 2 * r.shape[2]), F32) for r in rs[:n - n_all]]
    shapes += [jax.ShapeDtypeStruct((N_CHIPS, r.shape[1], 2 * r.shape[2]), dt)
               for r, dt in zip(rs[n - n_all:], all_dtypes)]

    def slab_spec(a, t):
        return pl.BlockSpec((None,) + a.shape[1:], lambda i, idx_ref: (idx_ref[t], 0, 0))

    def whole_spec(sh):
        return pl.BlockSpec(sh.shape, lambda i, idx_ref: (0,) * len(sh.shape))

    return pl.pallas_call(
        body, name="sum_chips",
        grid_spec=pltpu.PrefetchScalarGridSpec(
            num_scalar_prefetch=1, grid=(1,),
            in_specs=[slab_spec(q, 0) for q in qs] + [slab_spec(r, t) for r in rs for t in range(1, N_CHIPS)],
            out_specs=[whole_spec(sh) for sh in shapes]),
        out_shape=shapes, compiler_params=_params(),
    )(idx, *qs, *[r for r in rs for _ in range(1, N_CHIPS)])


def _share(shards, alls):
    n, na = len(shards), len(alls)
    total = n + na

    def body(*refs):
        g_refs, a_refs = refs[total:total + n], refs[total + n:2 * total]
        send_sems, recv_sems = refs[2 * total:]
        x, y, c, others = _place()
        me = 2 * x + y
        sibling = (x, y, 1 - c)

        def cols_of(w, half):
            h = shards[w].shape[1] // 2
            return g_refs[w].at[:, pl.ds(half * h, h)]

        def slab(a, chip, half):
            h = alls[a].shape[2] // 2
            return a_refs[a].at[chip, :, pl.ds(half * h, h)]

        def copy(src, dst, k, to):
            return pltpu.make_async_remote_copy(src_ref=src, dst_ref=dst, send_sem=send_sems.at[k],
                                                recv_sem=recv_sems.at[k], device_id=to, device_id_type=MESH)

        cps = []
        for a in range(na):
            base = n + 7 * a
            for j, (px, py) in reversed(list(enumerate(others))):
                cps.append(copy(slab(a, me, c), slab(a, me, c), base + 1 + j, (px, py, c)))
            cps.append(copy(slab(a, me, c), slab(a, me, c), base, sibling))
        cps += [copy(cols_of(w, c), cols_of(w, c), w, sibling) for w in range(n)]
        for cp in cps:
            cp.start()
        fwd = []
        for a in range(na):
            base = n + 7 * a
            for j, (px, py) in enumerate(others):
                chip = 2 * px + py
                copy(slab(a, me, c), slab(a, chip, c), base + 1 + j, (px, py, c)).wait_recv()
                cp = copy(slab(a, chip, c), slab(a, chip, c), base + 4 + j, sibling)
                cp.start()
                fwd.append(cp)
        for a in range(na):
            base = n + 7 * a
            for j, (px, py) in enumerate(others):
                chip = 2 * px + py
                copy(slab(a, chip, c), slab(a, chip, 1 - c), base + 4 + j, sibling).wait_recv()
            copy(slab(a, me, c), slab(a, me, 1 - c), base, sibling).wait_recv()
        for w in range(n):
            copy(cols_of(w, c), cols_of(w, 1 - c), w, sibling).wait_recv()
        for cp in cps + fwd:
            cp.wait_send()

    nsem = n + 7 * na
    return pl.pallas_call(
        body, name="share", in_specs=[HBM_SPEC_STRICT] * total, out_specs=[HBM_SPEC_STRICT] * total,
        out_shape=[pltpu.HBM(a.shape, a.dtype) for a in (*shards, *alls)],
        input_output_aliases={i: i for i in range(total)},
        scratch_shapes=[pltpu.SemaphoreType.DMA((nsem,)), pltpu.SemaphoreType.DMA((nsem,))],
    )(*[pltpu.with_memory_space_constraint(a, pltpu.HBM) for a in (*shards, *alls)])


def _adamw(w, g, m, v):
    m2 = ADAM_B1 * m + (1.0 - ADAM_B1) * g
    v2 = ADAM_B2 * v + (1.0 - ADAM_B2) * (g * g)
    m_hat = m2 / (1.0 - ADAM_B1 ** ADAM_STEP)
    v_hat = v2 / (1.0 - ADAM_B2 ** ADAM_STEP)
    return -ADAM_LR * (m_hat / (jnp.sqrt(v_hat) + ADAM_EPS) + ADAM_WD * w), m2, v2


def _update_w_in(wt, gt, mt, vt, lat, owner, tile):
    nlat = lat.shape[0] // tile

    def body(owner_ref, w_ref, g_ref, m_ref, v_ref, lat_ref, g2_ref, d_ref, m2_ref, v2_ref):
        row = pl.program_id(0) * tile + lax.broadcasted_iota(jnp.int32, (tile, 1), 0)
        g = jnp.where((row < LAT_COLS) & (owner_ref[0] == 1), lat_ref[...].astype(F32), g_ref[...])
        g2_ref[...] = g
        d_ref[...], m2_ref[...], v2_ref[...] = _adamw(w_ref[...], g, m_ref[...], v_ref[...])

    spec = pl.BlockSpec((tile, wt.shape[1]), lambda i, o: (i, 0))
    return pl.pallas_call(
        body, name="update_w_in",
        grid_spec=pltpu.PrefetchScalarGridSpec(
            num_scalar_prefetch=1, grid=(wt.shape[0] // tile,),
            in_specs=[spec] * 4 + [pl.BlockSpec((tile, wt.shape[1]), lambda i, o: (jnp.minimum(i, nlat - 1), 0))],
            out_specs=[spec] * 4),
        out_shape=[jax.ShapeDtypeStruct(wt.shape, F32)] * 4,
        compiler_params=_params(("parallel",)),
    )(owner, wt, gt, mt, vt, lat)


def _update_small(ws, gs, ms, vs):
    n = len(ws)

    def body(*refs):
        for k in range(n):
            w_ref, g_ref, m_ref, v_ref = refs[k], refs[n + k], refs[2 * n + k], refs[3 * n + k]
            d, m2, v2 = _adamw(w_ref[...], g_ref[...], m_ref[...], v_ref[...])
            refs[4 * n + k][...] = d
            refs[5 * n + k][...] = m2
            refs[6 * n + k][...] = v2

    shapes = [jax.ShapeDtypeStruct(w.shape, F32) for w in ws]
    outs = pl.pallas_call(
        body, name="update_small", in_specs=[VMEM_SPEC] * (4 * n), out_specs=[VMEM_SPEC] * (3 * n),
        out_shape=shapes * 3,
        compiler_params=pltpu.CompilerParams(vmem_limit_bytes=VMEM_LIMIT),
    )(*ws, *gs, *ms, *vs)
    return outs[:n], outs[n:2 * n], outs[2 * n:]


REPLICATED = ("b_in", "g_q", "g_kv", "w_ukv", "sgu_ln_g", "sgu_ln_b", "w_s", "b_s", "ln_g", "ln_b")
ORDER = ("w_in", "b_in", "g_q", "w_uq", "g_kv", "w_ukv", "w_oa", "sgu_ln_g", "sgu_ln_b", "w_s", "b_s", "w_ob", "w_out",
         "ln_g", "ln_b")


def kernel(x, positions, w_in, b_in, g_q, w_uq, g_kv, w_ukv, w_oa, sgu_ln_g, sgu_ln_b, w_s, b_s, w_ob, w_out, ln_g, ln_b, loss_target, m_w_in, m_b_in, m_g_q, m_w_uq, m_g_kv, m_w_ukv, m_w_oa, m_sgu_ln_g, m_sgu_ln_b, m_w_s, m_b_s, m_w_ob, m_w_out, m_ln_g, m_ln_b, v_w_in, v_b_in, v_g_q, v_w_uq, v_g_kv, v_w_ukv, v_w_oa, v_sgu_ln_g, v_sgu_ln_b, v_w_s, v_b_s, v_w_ob, v_w_out, v_ln_g, v_ln_b):
    w = dict(w_in=w_in, b_in=b_in, g_q=g_q, w_uq=w_uq, g_kv=g_kv, w_ukv=w_ukv, w_oa=w_oa, sgu_ln_g=sgu_ln_g,
             sgu_ln_b=sgu_ln_b, w_s=w_s, b_s=b_s, w_ob=w_ob, w_out=w_out, ln_g=ln_g, ln_b=ln_b)
    m = dict(w_in=m_w_in, b_in=m_b_in, g_q=m_g_q, w_uq=m_w_uq, g_kv=m_g_kv, w_ukv=m_w_ukv, w_oa=m_w_oa,
             sgu_ln_g=m_sgu_ln_g, sgu_ln_b=m_sgu_ln_b, w_s=m_w_s, b_s=m_b_s, w_ob=m_w_ob, w_out=m_w_out, ln_g=m_ln_g,
             ln_b=m_ln_b)
    v = dict(w_in=v_w_in, b_in=v_b_in, g_q=v_g_q, w_uq=v_w_uq, g_kv=v_g_kv, w_ukv=v_w_ukv, w_oa=v_w_oa,
             sgu_ln_g=v_sgu_ln_g, sgu_ln_b=v_sgu_ln_b, w_s=v_w_s, b_s=v_b_s, w_ob=v_w_ob, w_out=v_w_out, ln_g=v_ln_g,
             ln_b=v_ln_b)
    w, m, v = ({n: a[0] for n, a in d.items()} for d in (w, m, v))
    c = lax.axis_index("c")

    wt_shard, mt_shard, vt_shard = (jnp.transpose(d["w_in"]) for d in (w, m, v))
    xi, yi = lax.axis_index("x"), lax.axis_index("y")
    me1 = (2 * xi + yi).reshape(1).astype(jnp.int32)
    first = _first_start(*_cast_first(wt_shard, _pad_heads(w["w_uq"]), me1))
    tables = _rope_tables(positions[0])
    prep = _prep_local(w["b_in"], w["g_q"], w["g_kv"], w["w_ukv"])
    bsb = _bias_lanes(w["b_s"])
    c1 = c.reshape(1).astype(jnp.int32)
    idx = jnp.stack([2 * xi + yi, 2 * (1 - xi) + yi, 2 * xi + (1 - yi), 2 * (1 - xi) + (1 - yi), c]).astype(jnp.int32)
    bufs = _cast_own([wt_shard, w["w_oa"], w["w_ob"], w["w_out"]], me1, first[4])
    send0, recv0, bufs, token0 = _gather_start(bufs, first[4])
    g_lat, g_uq = _first_forward(*_first_wait(*first[:4], token0, *tables, *prep.values(), bsb, idx))
    st = _local_attention(x[0], tables, g_lat, prep, g_uq.reshape(Q_RANK, HEADS * HEAD_PAD), token0)
    g_in, g_oa, g_ob, g_out = _gather_finish(_gather_wait(send0, recv0, bufs, st["o"]))
    wt = g_in.reshape(IN_W, D_MODEL)

    loss, early, st = _local_head(
        st, x[0], loss_target[0], wt, g_oa, w["sgu_ln_g"], w["sgu_ln_b"], w["w_s"], bsb, g_ob,
        g_out.reshape(D_MODEL, D_MODEL), w["ln_g"], w["ln_b"])

    slabs = lambda a: a.reshape(N_CHIPS, IN_W // N_CHIPS, D_MODEL // 2)
    theirs, mine = (slabs(a) for a in _dwt_early(st["dhmt"], st["dhgt"], st["xb2"], c1, c1))
    parts1 = [theirs, early["w_oa"].astype(BF16), early["w_ob"].astype(BF16),
              early["w_out"].reshape(N_CHIPS, SLAB_W, D_MODEL).astype(BF16)]
    my_loss = lax.dynamic_update_slice(jnp.zeros((N_DEV,) + LOSS_TILE, F32), jnp.broadcast_to(loss, (1,) + LOSS_TILE),
                                       (4 * xi + 2 * yi + c, 0, 0))
    sems0 = _pairs_start(parts1, my_loss, 1)
    half = HEADS // 4
    first_pairs = _local_attn_bwd(st, sems0[5], 0, half, (), "attn_bwd_first")
    parts1, recv1, all_loss = _pairs_wait(*sems0[:5], first_pairs[1], 1)
    pairs1 = _add_pair([mine, *parts1[1:]], recv1, c1, "add_pair_early", n_whole=1)
    sems1 = _chips_start(pairs1, "chips_start_early")
    dq, dk, dv = _local_attn_bwd(st, sems1[4], half, half, tuple(first_pairs), "attn_bwd_rest")
    dhl, late = _local_tail(st, dq, dk, dv, dv)

    grads = {**early, **late}
    rep = jnp.concatenate([_rows8(grads[n]) for n in REPLICATED], axis=0)
    rep = jnp.pad(rep, ((0, N_CHIPS * REP_ROWS - rep.shape[0]), (0, 0))).reshape(N_CHIPS, REP_ROWS, D_MODEL)
    parts2 = [late["w_uq"].reshape(N_CHIPS, Q_RANK // N_CHIPS, HEADS * QK_DIM).astype(BF16), rep.astype(BF16),
              late["w_lat"].reshape(N_CHIPS, LAT_ROWS_PAD // N_CHIPS, D_MODEL)]
    pairs2 = _add_pair(parts2, _exchange_pairs(parts2, "exchange_pairs_late"), c1, "add_pair_late")
    sems2 = _chips_start(pairs2, "chips_start_late")
    dx = _dx(st["dr"], st["dhg"], st["dhm"], dhl, st["wt"], st["wlat"], sems2[4])
    pairs2, landed2 = _chips_wait(*sems2[:4], dx, "chips_wait_late")
    pairs1, landed1 = _chips_wait(*sems1[:4], landed2[0], "chips_wait_early")
    sums = _sum_chips([*pairs1, *pairs2], [*landed1, *landed2], idx, (F32, BF16))
    *shards, g_rep, g_lat = _share(sums[:-2], sums[-2:])
    loss = jnp.sum(all_loss[:, 0, 0])

    red = {n: s.reshape(w[n].shape) for n, s in zip(("w_oa", "w_ob", "w_out", "w_uq"), shards[1:])}
    g_rep = g_rep.reshape(N_CHIPS * REP_ROWS, D_MODEL)
    off = 0
    for n in REPLICATED:
        rows = _rows8(w[n]).shape[0]
        red[n] = g_rep[off:off + rows].reshape(-1)[:w[n].size].reshape(w[n].shape)
        off += rows
    owner = (2 * xi + yi == 0).astype(jnp.int32).reshape(1)
    gt, dt, mt, vt2 = _update_w_in(wt_shard, shards[0], mt_shard, vt_shard,
                                   g_lat.reshape(LAT_ROWS_PAD, D_MODEL).astype(F32), owner, 232)
    red["w_in"] = jnp.transpose(gt)
    small = [n for n in ORDER if n != "w_in"]
    as2d = lambda a: a.reshape(-1, a.shape[-1])
    ds, ms, vs = _update_small([as2d(w[n]) for n in small], [as2d(red[n]) for n in small],
                               [as2d(m[n]) for n in small], [as2d(v[n]) for n in small])
    delta, new_m, new_v = {"w_in": jnp.transpose(dt)}, {"w_in": jnp.transpose(mt)}, {"w_in": jnp.transpose(vt2)}
    for i, n in enumerate(small):
        delta[n], new_m[n], new_v[n] = (a[i].reshape(w[n].shape) for a in (ds, ms, vs))

    lead = lambda a: a[None]
    return (loss, dx[None], *[lead(red[n]) for n in ORDER], *[lead(delta[n]) for n in ORDER],
            *[lead(new_m[n]) for n in ORDER], *[lead(new_v[n]) for n in ORDER])
```

```python
import math

import jax
import jax.numpy as jnp
from jax import lax
from jax.experimental import pallas as pl
from jax.experimental.pallas import tpu as pltpu

F32 = jnp.float32
BF16 = jnp.bfloat16

D_MODEL = 1024
HEADS = 8
Q_RANK = 384
KV_RANK = 128
NOPE = 64
ROPE = 32
V_DIM = 64
QK_DIM = NOPE + ROPE
HEAD_PAD = 128
MLA_W = HEADS * V_DIM
SGU_W = 512
GROUPS = 8
CHUNK = 128
IN_W = 4640
RMS_EPS = 1e-6
LN_EPS = 1e-5
ALPHA = 2.0 ** 0.25
ROPE_THETA = 10000.0
SCALE = QK_DIM ** -0.5

GATE_W = 2 * D_MODEL
MID_W = 4 * SGU_W
LAT_W = Q_RANK + KV_RANK + HEAD_PAD
LAT_COLS = Q_RANK + KV_RANK + ROPE
ROW_GATE = LAT_COLS + MID_W
LAT_ROWS_PAD = 704
N_SLABS = 4
SLAB_W = D_MODEL // N_SLABS

ROW_TILE = 256
MATMUL_ROW_TILE = 512
MID_ROW_TILE = 256
ATT_TK = 256
ATT_BWD_TK = 256
SUM_ROWS = 16
LOG2E = 1.4426950408889634
LN2 = 0.6931471805599453
Q_SCALE = SCALE * LOG2E
VMEM_LIMIT = 56 * 1024 * 1024

ADAM_LR = 0.001
ADAM_B1 = 0.9
ADAM_B2 = 0.999
ADAM_EPS = 1e-08
ADAM_WD = 0.01
ADAM_STEP = 10


def _dot(a, b):
    return jnp.dot(a, b, preferred_element_type=F32)


def _dot_nt(a, b):
    return lax.dot_general(a, b, (((1,), (1,)), ((), ())), preferred_element_type=F32)


def _dot_tn(a, b):
    return lax.dot_general(a, b, (((0,), (0,)), ((), ())), preferred_element_type=F32)


def _sigmoid(z):
    return 0.5 * jnp.tanh(0.5 * z) + 0.5


_GELU_C = math.sqrt(2.0 / math.pi)


def _gelu_and_grad(x):
    x2 = x * x
    t = jnp.tanh(_GELU_C * (x + 0.044715 * x * x2))
    g = 0.5 * x * (1.0 + t)
    dg = 0.5 * (1.0 + t) + 0.5 * x * (1.0 - t * t) * (_GELU_C * (1.0 + 3.0 * 0.044715 * x2))
    return g, dg


def _silu_and_grad(z):
    s = _sigmoid(z)
    return z * s, s * (1.0 + z * (1.0 - s))


def _rope(xb, c, sl, sh):
    return xb * c + pltpu.roll(xb, 112, 1) * sl + pltpu.roll(xb, 16, 1) * sh


def _rope_t(dy, c, sl, sh):
    return dy * c + pltpu.roll(dy * sl, 16, 1) + pltpu.roll(dy * sh, 112, 1)


def _params(sem=("arbitrary",)):
    return pltpu.CompilerParams(dimension_semantics=sem, vmem_limit_bytes=VMEM_LIMIT)


def _row_spec(tile, width):
    return pl.BlockSpec((tile, width), lambda i: (i, 0))


def _full_spec(shape):
    nd = len(shape)
    return pl.BlockSpec(shape, lambda i: (0,) * nd)


def _kpe_rows(wt_ref):
    z = lambda n: jnp.zeros((n, D_MODEL), BF16)
    return jnp.concatenate([z(NOPE), wt_ref[Q_RANK + KV_RANK:LAT_COLS, :], z(HEAD_PAD - QK_DIM)], axis=0)


def _fwd_rest(xb2, wt, b_g, b_m):
    s = xb2.shape[1]
    ts = MATMUL_ROW_TILE
    tn = D_MODEL
    blocks = ([(ROW_GATE + c0, 0, c0) for c0 in range(0, GATE_W, tn)]
              + [(LAT_COLS + c0, 1, c0) for c0 in range(0, MID_W, tn)])

    def body(xb_ref, wt_hbm, bg_ref, bm_ref, hg_ref, hm_ref, wt_ref, sems):
        copies = [pltpu.make_async_copy(wt_hbm.at[lo:lo + tn], wt_ref.at[lo:lo + tn], sems.at[n])
                  for n, (lo, _, _) in enumerate(blocks)]

        def compute(first):
            xb_ = jnp.concatenate([xb_ref[0], xb_ref[1]], axis=1)
            for cp, (lo, which, c0) in zip(copies, blocks):
                if first:
                    cp.wait()
                out_ref, b_ref = ((hg_ref, bg_ref), (hm_ref, bm_ref))[which]
                out_ref[:, c0:c0 + tn] = (_dot_nt(xb_, wt_ref[lo:lo + tn, :]) + b_ref[:, c0:c0 + tn]).astype(BF16)

        @pl.when(pl.program_id(0) == 0)
        def _():
            for cp in copies:
                cp.start()
            compute(True)

        @pl.when(pl.program_id(0) > 0)
        def _():
            compute(False)

    return pl.pallas_call(
        body, name="fwd_rest", grid=(s // ts,),
        in_specs=[pl.BlockSpec((2, ts, D_MODEL // 2), lambda i: (0, i, 0)), HBM_SPEC, _full_spec(b_g.shape),
                  _full_spec(b_m.shape)],
        out_specs=[_row_spec(ts, GATE_W), _row_spec(ts, MID_W)],
        out_shape=[jax.ShapeDtypeStruct((s, GATE_W), BF16), jax.ShapeDtypeStruct((s, MID_W), BF16)],
        scratch_shapes=[pltpu.VMEM(wt.shape, BF16), pltpu.SemaphoreType.DMA((len(blocks),))],
        compiler_params=_params(),
    )(xb2, wt, b_g, b_m)


def _fwd_lat(x, wlat, b_l, g_q, wuq, g_kv, wk, wv, rc, rsl, rsh, after):
    s = x.shape[0]
    ts = ROW_TILE

    def body(x_ref, wt_ref, bl_ref, gq_ref, wuq_ref, gkv_ref, wk_ref, wv_ref, rc_ref, rsl_ref,
             rsh_ref, after_ref, hl_ref, q_ref, k_ref, v_ref, qt_ref, kt_ref, vt_ref, xb2_ref):
        xb = x_ref[...].astype(BF16)
        xb2_ref[0] = xb[:, :D_MODEL // 2]
        xb2_ref[1] = xb[:, D_MODEL // 2:]
        hl = jnp.concatenate([_dot_nt(xb, wt_ref[0:Q_RANK + KV_RANK, :]), _dot_nt(xb, _kpe_rows(wt_ref))],
                             axis=1) + bl_ref[...]
        hl_ref[...] = hl
        c, sl, sh = rc_ref[...], rsl_ref[...], rsh_ref[...]
        cq = hl[:, :Q_RANK]
        cqn = cq * lax.rsqrt(jnp.mean(cq * cq, axis=-1, keepdims=True) + RMS_EPS) * gq_ref[...]
        q = _dot(cqn.astype(BF16), wuq_ref[...])
        ckv = hl[:, Q_RANK:Q_RANK + KV_RANK]
        ckvn = (ckv * lax.rsqrt(jnp.mean(ckv * ckv, axis=-1, keepdims=True) + RMS_EPS) * gkv_ref[...]).astype(BF16)
        k = _dot(ckvn, wk_ref[...])
        vb = _dot(ckvn, wv_ref[...]).astype(BF16)
        v_ref[...] = vb
        vt_ref[...] = vb.T
        kpe = _rope(hl[:, Q_RANK + KV_RANK:], c, sl, sh)
        for hd in range(HEADS):
            lanes = slice(hd * HEAD_PAD, (hd + 1) * HEAD_PAD)
            qb = (_rope(q[:, lanes], c, sl, sh) * Q_SCALE).astype(BF16)
            kb = (k[:, lanes] + kpe).astype(BF16)
            q_ref[:, lanes] = qb
            k_ref[:, lanes] = kb
            qt_ref[lanes, :] = qb.T
            kt_ref[lanes, :] = kb.T

    qk_w = HEADS * HEAD_PAD
    col_spec = lambda rows: pl.BlockSpec((rows, ts), lambda i: (0, i))
    return pl.pallas_call(
        body, name="fwd_lat", grid=(s // ts,),
        in_specs=[_row_spec(ts, D_MODEL), _full_spec(wlat.shape),
                  _full_spec(b_l.shape), _full_spec(g_q.shape),
                  _full_spec(wuq.shape), _full_spec(g_kv.shape), _full_spec(wk.shape), _full_spec(wv.shape),
                  _row_spec(ts, HEAD_PAD), _row_spec(ts, HEAD_PAD), _row_spec(ts, HEAD_PAD),
                  pl.BlockSpec(memory_space=pl.ANY)],
        out_specs=[_row_spec(ts, LAT_W), _row_spec(ts, qk_w),
                   _row_spec(ts, qk_w), _row_spec(ts, MLA_W), col_spec(qk_w), col_spec(qk_w),
                   col_spec(MLA_W), pl.BlockSpec((2, ts, D_MODEL // 2), lambda i: (0, i, 0))],
        out_shape=[jax.ShapeDtypeStruct((s, LAT_W), F32), jax.ShapeDtypeStruct((s, qk_w), BF16),
                   jax.ShapeDtypeStruct((s, qk_w), BF16), jax.ShapeDtypeStruct((s, MLA_W), BF16),
                   jax.ShapeDtypeStruct((qk_w, s), BF16),
                   jax.ShapeDtypeStruct((qk_w, s), BF16), jax.ShapeDtypeStruct((MLA_W, s), BF16),
                   jax.ShapeDtypeStruct((2, s, D_MODEL // 2), BF16)],
        compiler_params=_params(),
    )(x, wlat, b_l, g_q, wuq, g_kv, wk, wv, rc, rsl, rsh, after)


def _attn_fwd(qt, k, vt):
    s = k.shape[0]
    tk = ATT_TK
    nk = s // tk
    pairs = HEADS // 2

    def body(qt_ref, k_ref, vt_ref, o_ref, lse_ref):
        qts = [qt_ref[hh * HEAD_PAD:(hh + 1) * HEAD_PAD, :] for hh in range(2)]
        ones = jnp.ones((SUM_ROWS, tk), BF16)

        def scores(j):
            return tuple(_dot(k_ref[j * tk:(j + 1) * tk, hh * HEAD_PAD:(hh + 1) * HEAD_PAD], qts[hh][:, j * tk:])
                         for hh in range(2))

        def weighted(j, ps):
            return tuple(_dot(jnp.concatenate([vt_ref[hh * V_DIM:(hh + 1) * V_DIM, j * tk:(j + 1) * tk], ones], axis=0),
                              ps[hh]) for hh in range(2))

        def from_lane(full, lo, part):
            return part if lo == 0 else jnp.concatenate([full[:, :lo], part], axis=1)

        krow = lax.broadcasted_iota(jnp.int32, (tk, tk), 0)
        qcol = lax.broadcasted_iota(jnp.int32, (tk, tk), 1)
        st = scores(0)
        ps = None
        stats = [(jnp.full((1, s), -jnp.inf, F32), jnp.zeros((V_DIM + SUM_ROWS, s), F32))] * 2
        for j in range(nk):
            lo, lo_prev = j * tk, max(j - 1, 0) * tk
            st_next = scores(j + 1) if j + 1 < nk else None
            pvs = weighted(j - 1, ps) if j else None
            new_ps, new_stats = [], []
            for hh in range(2):
                m, acc = stats[hh]
                diag = jnp.where(krow <= qcol, st[hh][:, :tk], -jnp.inf)
                s_ = diag if j == nk - 1 else jnp.concatenate([diag, st[hh][:, tk:]], axis=1)
                if j:
                    acc = from_lane(acc, lo_prev, acc[:, lo_prev:] + pvs[hh])
                m_old = m[:, lo:]
                m_new = jnp.maximum(m_old, jnp.max(s_, axis=0, keepdims=True))
                a = jnp.exp2(m_old - m_new)
                p = jnp.exp2(s_ - m_new)
                new_stats.append((from_lane(m, lo, m_new), from_lane(acc, lo, a * acc[:, lo:])))
                new_ps.append(p.astype(BF16))
            st, ps, stats = st_next, new_ps, new_stats
        pvs = weighted(nk - 1, ps)
        lo = (nk - 1) * tk
        accs = [from_lane(stats[hh][1], lo, stats[hh][1][:, lo:] + pvs[hh]) for hh in range(2)]
        sums = [acc[V_DIM:V_DIM + 1, :] for acc in accs]
        ot = jnp.concatenate([accs[hh][:V_DIM, :] / sums[hh] for hh in range(2)], axis=0)
        o_ref[...] = ot.T
        lse = [stats[hh][0] + jnp.log(sums[hh]) * LOG2E for hh in range(2)]
        lse_ref[...] = jnp.concatenate(lse + [jnp.zeros((6, s), F32)], axis=0)

    return pl.pallas_call(
        body, name="attn_fwd", grid=(pairs,),
        in_specs=[pl.BlockSpec((2 * HEAD_PAD, s), lambda p: (p, 0)),
                  pl.BlockSpec((s, 2 * HEAD_PAD), lambda p: (0, p)),
                  pl.BlockSpec((2 * V_DIM, s), lambda p: (p, 0))],
        out_specs=[pl.BlockSpec((s, 2 * V_DIM), lambda p: (0, p)),
                   pl.BlockSpec((None, 8, s), lambda p: (p, 0, 0))],
        out_shape=[jax.ShapeDtypeStruct((s, MLA_W), F32), jax.ShapeDtypeStruct((pairs, 8, s), F32)],
        compiler_params=_params(("arbitrary",)),
    )(qt, k, vt)


def _attn_bwd(q, qt, k, kt, v, do, dot, lse, delta, after, first_pair, n_pairs, into, name):
    s = k.shape[0]
    tk = ATT_BWD_TK
    nk = s // tk

    def body(q_ref, qt_ref, k_ref, kt_ref, v_ref, do_ref, dot_ref, lse_ref, dl_ref, after_ref, *rest):
        dqt_ref, dk_ref, dv_ref = rest[-3:]
        krow = lax.broadcasted_iota(jnp.int32, (tk, tk), 0)
        qcol = lax.broadcasted_iota(jnp.int32, (tk, tk), 1)
        lane = lax.broadcasted_iota(jnp.int32, (tk, 2 * V_DIM), 1)
        drow = lax.broadcasted_iota(jnp.int32, (2 * V_DIM, s), 0)
        dotb = dot_ref[...]
        dots = [jnp.where((drow < V_DIM) if hh == 0 else (drow >= V_DIM), dotb, jnp.zeros_like(dotb))
                for hh in range(2)]
        for j in range(nk):
            lo = j * tk
            vb = v_ref[lo:lo + tk, :]
            dob = do_ref[lo:, :]
            dvs = []
            for hh in range(2):
                rows = slice(hh * HEAD_PAD, (hh + 1) * HEAD_PAD)
                st = _dot(k_ref[lo:lo + tk, rows], qt_ref[rows, lo:])
                diag = jnp.where(krow <= qcol, st[:, :tk], -jnp.inf)
                st = diag if j == nk - 1 else jnp.concatenate([diag, st[:, tk:]], axis=1)
                p = jnp.exp2(st - lse_ref[hh:hh + 1, lo:])
                dpt = _dot(vb, dots[hh][:, lo:])
                dst = (p * (dpt - dl_ref[hh:hh + 1, lo:])).astype(BF16)
                dvs.append(_dot(p.astype(BF16), dob))
                dk_ref[lo:lo + tk, rows] = (_dot(dst, q_ref[lo:, rows]) * LN2).astype(BF16)
                dqt = _dot(kt_ref[rows, lo:lo + tk], dst)
                if j == 0:
                    dqt_ref[rows, :] = dqt
                else:
                    dqt_ref[rows, lo:] += dqt
            dv_ref[lo:lo + tk, :] = jnp.where(lane < V_DIM, dvs[0], dvs[1]).astype(BF16)
        dqt_ref[...] = dqt_ref[...] * SCALE

    pair_rows = lambda w: pl.BlockSpec((s, w), lambda p: (0, p + first_pair))
    pair_cols = lambda w: pl.BlockSpec((w, s), lambda p: (p + first_pair, 0))
    stats = pl.BlockSpec((None, 8, s), lambda p: (p + first_pair, 0, 0))
    any_spec = pl.BlockSpec(memory_space=pl.ANY)
    return pl.pallas_call(
        body, name=name, grid=(n_pairs,),
        in_specs=[pair_rows(2 * HEAD_PAD), pair_cols(2 * HEAD_PAD), pair_rows(2 * HEAD_PAD), pair_cols(2 * HEAD_PAD),
                  pair_rows(2 * V_DIM), pair_rows(2 * V_DIM), pair_cols(2 * V_DIM), stats, stats, any_spec]
        + [any_spec] * len(into),
        out_specs=[pair_cols(2 * HEAD_PAD), pair_rows(2 * HEAD_PAD), pair_rows(2 * V_DIM)],
        out_shape=[jax.ShapeDtypeStruct((HEADS * HEAD_PAD, s), F32), jax.ShapeDtypeStruct((s, HEADS * HEAD_PAD), BF16),
                   jax.ShapeDtypeStruct((s, MLA_W), BF16)],
        input_output_aliases={10 + n: n for n in range(len(into))},
        compiler_params=_params(("arbitrary",)),
    )(q, qt, k, kt, v, do, dot, lse, delta, after, *into)


def _split3(a):
    hi = a.astype(BF16)
    r1 = a - hi.astype(F32)
    mid = r1.astype(BF16)
    lo = (r1 - mid.astype(F32)).astype(BF16)
    return hi, mid, lo


def _mid(x, tgt, o, hm, hg, woa, wob, wout, ln_g, ln_b, sg_g, sg_b, w_s, bsb):
    s = x.shape[0]
    ts = MID_ROW_TILE
    nsteps = s // ts
    nch = ts // CHUNK
    npair = GROUPS // 2

    def body(x_ref, t_ref, o_ref, hm_ref, hg_ref, woa_ref, wob_ref, wout_ref, lng_ref, lnb_ref, sgg_ref, sgb_ref,
             ws_ref, bsb_ref,
             dr_ref, dhg_ref, dhm_ref, do_ref, dot_ref, dl_ref,
             dwout_ref, dwoa_ref, dwob_ref, dws_ref, dbs_ref, dlng_ref, dlnb_ref, dsgg_ref, dsgb_ref, loss_ref,
             dbg_ref, dbm_ref, dbacc_ref, awout_ref, awoa_ref, awob_ref):
        i = pl.program_id(0)

        @pl.when(i == 0)
        def _():
            for r in (awout_ref, awoa_ref, awob_ref, dws_ref, dlng_ref, dlnb_ref, dsgg_ref, dsgb_ref, loss_ref,
                      dbg_ref, dbm_ref, dbacc_ref):
                r[...] = jnp.zeros_like(r)

        def emit(ref, bref, lo, val):
            n = val.shape[1]
            ref[:, lo:lo + n] = val.astype(BF16)
            bref[:, lo:lo + n] += jnp.sum(val, axis=0, keepdims=True)

        lane = lax.broadcasted_iota(jnp.int32, (CHUNK, CHUNK), 1)
        left = lane < V_DIM
        tril = lax.broadcasted_iota(jnp.int32, (CHUNK, CHUNK), 0) >= lane
        ms = [jnp.where(tril, ws_ref[g], 0.0).astype(BF16) for g in range(GROUPS)]

        z_a = hm_ref[:, 0:SGU_W].astype(F32)
        u = hm_ref[:, SGU_W:2 * SGU_W].astype(F32)
        v = hm_ref[:, 2 * SGU_W:3 * SGU_W].astype(F32)
        z_b = hm_ref[:, 3 * SGU_W:4 * SGU_W].astype(F32)
        o = o_ref[...]
        sa, dsa = _silu_and_grad(z_a)
        y_a = (o * sa).astype(BF16)
        gu, dgu = _gelu_and_grad(u)
        gv, dgv = _gelu_and_grad(v)
        mu = jnp.mean(gv, axis=-1, keepdims=True)
        vc = gv - mu
        rstd_v = lax.rsqrt(jnp.mean(vc * vc, axis=-1, keepdims=True) + LN_EPS)
        vhat = vc * rstd_v
        vn = (vhat * sgg_ref[...] + sgb_ref[...]).astype(BF16)
        rows = []
        for c in range(nch):
            blocks = []
            for p in range(npair):
                blk = vn[c * CHUNK:(c + 1) * CHUNK, p * CHUNK:(p + 1) * CHUNK]
                blocks.append(jnp.where(left, _dot(ms[2 * p], blk), _dot(ms[2 * p + 1], blk)))
            rows.append(jnp.concatenate(blocks, axis=1) + bsb_ref[...])
        mixed = jnp.concatenate(rows, axis=0)
        sgu = gu * mixed
        sb, dsb = _silu_and_grad(z_b)
        y_b = (sgu * sb).astype(BF16)
        pa = jnp.concatenate([_dot(y_a, woa_ref[k]) for k in range(N_SLABS)], axis=1)
        pb = jnp.concatenate([_dot(y_b, wob_ref[k]) for k in range(N_SLABS)], axis=1)
        sga = _sigmoid(hg_ref[:, :D_MODEL].astype(F32))
        sgb = _sigmoid(hg_ref[:, D_MODEL:].astype(F32))
        m2 = (sga * pa + sgb * pb).astype(BF16)
        r = ALPHA * x_ref[...] + _dot(m2, wout_ref[...])
        rmu = jnp.mean(r, axis=-1, keepdims=True)
        rc = r - rmu
        rstd = lax.rsqrt(jnp.mean(rc * rc, axis=-1, keepdims=True) + LN_EPS)
        xhat = rc * rstd
        y = xhat * lng_ref[...] + lnb_ref[...]
        err = y - t_ref[...]
        loss_ref[...] += jnp.full(loss_ref.shape, 0.5 / D_MODEL, F32) * jnp.sum(err * err)

        dy = err * (1.0 / D_MODEL)
        dlng_ref[...] += jnp.sum(dy * xhat, axis=0, keepdims=True)
        dlnb_ref[...] += jnp.sum(dy, axis=0, keepdims=True)
        dxh = dy * lng_ref[...]
        dr = rstd * (dxh - jnp.mean(dxh, axis=-1, keepdims=True) - xhat * jnp.mean(dxh * xhat, axis=-1, keepdims=True))
        dr_ref[...] = dr
        drb = dr.astype(BF16)
        awout_ref[...] += _dot_tn(m2, drb)
        dm2 = _dot_nt(drb, wout_ref[...])
        emit(dhg_ref, dbg_ref,0, dm2 * pa * sga * (1.0 - sga))
        emit(dhg_ref, dbg_ref,D_MODEL, dm2 * pb * sgb * (1.0 - sgb))
        dpa = (dm2 * sga).astype(BF16)
        dpb = (dm2 * sgb).astype(BF16)
        dy_a = jnp.zeros((ts, MLA_W), F32)
        dy_b = jnp.zeros((ts, SGU_W), F32)
        y_at, y_bt = y_a.T, y_b.T
        for k in range(N_SLABS):
            cols = slice(k * SLAB_W, (k + 1) * SLAB_W)
            awoa_ref[k] += _dot(y_at, dpa[:, cols])
            awob_ref[k] += _dot(y_bt, dpb[:, cols])
            dy_a = dy_a + _dot_nt(dpa[:, cols], woa_ref[k])
            dy_b = dy_b + _dot_nt(dpb[:, cols], wob_ref[k])
        dob = (dy_a * sa).astype(BF16)
        do_ref[...] = dob
        dot_ref[...] = dob.T
        head = (lax.broadcasted_iota(jnp.int32, (HEADS, MLA_W), 1) // V_DIM
                == lax.broadcasted_iota(jnp.int32, (HEADS, MLA_W), 0)).astype(BF16)
        dl = sum(_dot_nt(head, term) for term in _split3(dob.astype(F32) * o))
        for p in range(HEADS // 2):
            dl_ref[p] = jnp.concatenate([dl[2 * p:2 * p + 2], jnp.zeros((6, ts), F32)], axis=0)
        emit(dhm_ref, dbm_ref,0, dy_a * o * dsa)
        dsg = dy_b * sb
        emit(dhm_ref, dbm_ref,3 * SGU_W, dy_b * sgu * dsb)
        emit(dhm_ref, dbm_ref,SGU_W, dsg * mixed * dgu)
        dmixed = dsg * gu
        dvn_rows = []
        dbs_sum = jnp.zeros((CHUNK, SGU_W), F32)
        for c in range(nch):
            dm_c = dmixed[c * CHUNK:(c + 1) * CHUNK, :]
            dbs_sum = dbs_sum + dm_c
            blocks = []
            for p in range(npair):
                dmb = dm_c[:, p * CHUNK:(p + 1) * CHUNK].astype(BF16)
                blk = vn[c * CHUNK:(c + 1) * CHUNK, p * CHUNK:(p + 1) * CHUNK]
                blocks.append(jnp.where(left, _dot_tn(ms[2 * p], dmb), _dot_tn(ms[2 * p + 1], dmb)))
                zero = jnp.zeros_like(dmb)
                dws_ref[2 * p] += jnp.where(tril, _dot_nt(jnp.where(left, dmb, zero), blk), 0.0)
                dws_ref[2 * p + 1] += jnp.where(tril, _dot_nt(jnp.where(left, zero, dmb), blk), 0.0)
            dvn_rows.append(jnp.concatenate(blocks, axis=1))
        dbacc_ref[...] += dbs_sum
        dvn = jnp.concatenate(dvn_rows, axis=0)
        dsgg_ref[...] += jnp.sum(dvn * vhat, axis=0, keepdims=True)
        dsgb_ref[...] += jnp.sum(dvn, axis=0, keepdims=True)
        dvh = dvn * sgg_ref[...]
        dgv_in = rstd_v * (dvh - jnp.mean(dvh, axis=-1, keepdims=True)
                           - vhat * jnp.mean(dvh * vhat, axis=-1, keepdims=True))
        emit(dhm_ref, dbm_ref,2 * SGU_W, dgv_in * dgv)

        @pl.when(i == nsteps - 1)
        def _():
            dwout_ref[...] = awout_ref[...].astype(BF16)
            dwoa_ref[...] = awoa_ref[...].astype(BF16)
            dwob_ref[...] = awob_ref[...].astype(BF16)
            grp = (lax.broadcasted_iota(jnp.int32, (SGU_W, CHUNK), 0) // V_DIM
                   == lax.broadcasted_iota(jnp.int32, (SGU_W, CHUNK), 1)).astype(BF16)
            hi, mid, lo = _split3(dbacc_ref[...])
            dbs_ref[...] = _dot(hi, grp) + _dot(mid, grp) + _dot(lo, grp)

    acc_shapes = [(D_MODEL, D_MODEL), woa.shape, wob.shape, (GROUPS, CHUNK, CHUNK), (CHUNK, CHUNK),
                  (1, D_MODEL), (1, D_MODEL), (1, SGU_W), (1, SGU_W), (1, 128), (1, GATE_W), (1, MID_W)]
    col_spec = lambda rows: pl.BlockSpec((rows, ts), lambda i: (0, i))
    return pl.pallas_call(
        body, name="mid", grid=(nsteps,),
        in_specs=[_row_spec(ts, D_MODEL), _row_spec(ts, D_MODEL), _row_spec(ts, MLA_W), _row_spec(ts, MID_W),
                  _row_spec(ts, GATE_W), _full_spec(woa.shape), _full_spec(wob.shape), _full_spec(wout.shape),
                  _full_spec(ln_g.shape), _full_spec(ln_b.shape), _full_spec(sg_g.shape), _full_spec(sg_b.shape),
                  _full_spec(w_s.shape), _full_spec(bsb.shape)],
        out_specs=[_row_spec(ts, D_MODEL), _row_spec(ts, GATE_W), _row_spec(ts, MID_W), _row_spec(ts, MLA_W),
                   col_spec(MLA_W), pl.BlockSpec((HEADS // 2, 8, ts), lambda i: (0, 0, i))]
        + [_full_spec(sh) for sh in acc_shapes],
        out_shape=[jax.ShapeDtypeStruct((s, D_MODEL), F32), jax.ShapeDtypeStruct((s, GATE_W), BF16),
                   jax.ShapeDtypeStruct((s, MID_W), BF16), jax.ShapeDtypeStruct((s, MLA_W), BF16),
                   jax.ShapeDtypeStruct((MLA_W, s), BF16), jax.ShapeDtypeStruct((HEADS // 2, 8, s), F32)]
        + [jax.ShapeDtypeStruct(sh, BF16 if n < 3 else F32) for n, sh in enumerate(acc_shapes)],
        scratch_shapes=[pltpu.VMEM((CHUNK, SGU_W), F32)] + [pltpu.VMEM(sh, F32) for sh in acc_shapes[:3]],
        compiler_params=_params(),
    )(x, tgt, o, hm, hg, woa, wob, wout, ln_g, ln_b, sg_g, sg_b, w_s, bsb)


def _lat_bwd(dq, dk, dv, hl, rc, rsl, rsh, g_q, g_kv, wuq, wk, wv, after):
    s = dk.shape[0]
    ts = ROW_TILE
    qk_w = HEADS * HEAD_PAD

    def body(dq_ref, dk_ref, dv_ref, hl_ref, rc_ref, rsl_ref, rsh_ref, gq_ref, gkv_ref, wuq_ref, wk_ref, wv_ref,
             after_ref, dhl_ref, dhlt_ref, dwuq_ref, dwk_ref, dwv_ref, dgq_ref, dgkv_ref, dbl_ref):
        i = pl.program_id(0)

        @pl.when(i == 0)
        def _():
            for r in (dwuq_ref, dwk_ref, dwv_ref, dgq_ref, dgkv_ref, dbl_ref):
                r[...] = jnp.zeros_like(r)

        def emit(lo, val):
            vb = val.astype(BF16)
            n = val.shape[1]
            dhl_ref[:, lo:lo + n] = vb
            dhlt_ref[lo:lo + n, :] = vb.T
            dbl_ref[:, lo:lo + n] += jnp.sum(val, axis=0, keepdims=True)

        c, sl, sh = rc_ref[...], rsl_ref[...], rsh_ref[...]
        lane = lax.broadcasted_iota(jnp.int32, (ts, HEAD_PAD), 1)
        pe = (lane >= NOPE) & (lane < QK_DIM)
        dkpe = jnp.zeros((ts, HEAD_PAD), F32)
        dqu = []
        for hd in range(HEADS):
            lanes = slice(hd * HEAD_PAD, (hd + 1) * HEAD_PAD)
            dqu.append(_rope_t(dq_ref[lanes, :].T, c, sl, sh).astype(BF16))
            dkpe = dkpe + dk_ref[:, lanes]
        dqu = jnp.concatenate(dqu, axis=1)
        dkpe = _rope_t(jnp.where(pe, dkpe, 0.0), c, sl, sh)

        cq = hl_ref[:, :Q_RANK]
        rq = lax.rsqrt(jnp.mean(cq * cq, axis=-1, keepdims=True) + RMS_EPS)
        cqh = cq * rq
        cqn = (cqh * gq_ref[...]).astype(BF16)
        dwuq_ref[...] += _dot_tn(cqn, dqu)
        dcqn = _dot_nt(dqu, wuq_ref[...])
        dgq_ref[...] += jnp.sum(dcqn * cqh, axis=0, keepdims=True)
        dch = dcqn * gq_ref[...]
        emit(0, rq * (dch - cqh * jnp.mean(dch * cqh, axis=-1, keepdims=True)))

        ckv = hl_ref[:, Q_RANK:Q_RANK + KV_RANK]
        rk = lax.rsqrt(jnp.mean(ckv * ckv, axis=-1, keepdims=True) + RMS_EPS)
        ckh = ckv * rk
        ckn = (ckh * gkv_ref[...]).astype(BF16)
        dkb = dk_ref[...].astype(BF16)
        dvb = dv_ref[...].astype(BF16)
        dwk_ref[...] += _dot_tn(ckn, dkb)
        dwv_ref[...] += _dot_tn(ckn, dvb)
        dckn = _dot_nt(dkb, wk_ref[...]) + _dot_nt(dvb, wv_ref[...])
        dgkv_ref[...] += jnp.sum(dckn * ckh, axis=0, keepdims=True)
        dkh = dckn * gkv_ref[...]
        emit(Q_RANK, rk * (dkh - ckh * jnp.mean(dkh * ckh, axis=-1, keepdims=True)))
        emit(Q_RANK + KV_RANK, dkpe)

    acc_shapes = [wuq.shape, wk.shape, wv.shape, g_q.shape, g_kv.shape, (1, LAT_W)]
    return pl.pallas_call(
        body, name="lat_bwd", grid=(s // ts,),
        in_specs=[pl.BlockSpec((qk_w, ts), lambda i: (0, i)), _row_spec(ts, qk_w), _row_spec(ts, MLA_W),
                  _row_spec(ts, LAT_W), _row_spec(ts, HEAD_PAD), _row_spec(ts, HEAD_PAD), _row_spec(ts, HEAD_PAD),
                  _full_spec(g_q.shape), _full_spec(g_kv.shape), _full_spec(wuq.shape), _full_spec(wk.shape),
                  _full_spec(wv.shape), pl.BlockSpec(memory_space=pl.ANY)],
        out_specs=[_row_spec(ts, LAT_W), pl.BlockSpec((LAT_W, ts), lambda i: (0, i))]
        + [_full_spec(sh) for sh in acc_shapes],
        out_shape=[jax.ShapeDtypeStruct((s, LAT_W), BF16), jax.ShapeDtypeStruct((LAT_W, s), BF16)]
        + [jax.ShapeDtypeStruct(sh, F32) for sh in acc_shapes],
        compiler_params=_params(),
    )(dq, dk, dv, hl, rc, rsl, rsh, g_q, g_kv, wuq, wk, wv, after)


def _dx(dr, dhg, dhm, dhl, wt, wlat, after):
    s = dr.shape[0]
    ts = MATMUL_ROW_TILE

    tk = D_MODEL
    bounds = ([(ROW_GATE + r0, ROW_GATE + r0 + tk) for r0 in range(0, GATE_W, tk)]
              + [(LAT_COLS + r0, LAT_COLS + r0 + tk) for r0 in range(0, MID_W, tk)])

    def body(dr_ref, dhg_ref, dhm_ref, dhl_ref, wt_hbm, wlat_ref, after_ref, dx_ref, wt_ref, sems):
        copies = [pltpu.make_async_copy(wt_hbm.at[lo:hi], wt_ref.at[lo:hi], sems.at[n])
                  for n, (lo, hi) in enumerate(bounds)]

        def compute(first):
            acc = (ALPHA * dr_ref[...] + _dot(dhl_ref[:, 0:Q_RANK + KV_RANK], wlat_ref[0:Q_RANK + KV_RANK, :])
                   + _dot(dhl_ref[:, Q_RANK + KV_RANK:], _kpe_rows(wlat_ref)))
            for n, (lo, hi) in enumerate(bounds):
                if first:
                    copies[n].wait()
                if lo >= ROW_GATE:
                    acc += _dot(dhg_ref[:, lo - ROW_GATE:hi - ROW_GATE], wt_ref[lo:hi, :])
                else:
                    acc += _dot(dhm_ref[:, lo - LAT_COLS:hi - LAT_COLS], wt_ref[lo:hi, :])
            dx_ref[...] = acc

        @pl.when(pl.program_id(0) == 0)
        def _():
            for cp in copies:
                cp.start()
            compute(True)

        @pl.when(pl.program_id(0) > 0)
        def _():
            compute(False)

    return pl.pallas_call(
        body, name="dx", grid=(s // ts,),
        in_specs=[_row_spec(ts, D_MODEL), _row_spec(ts, GATE_W), _row_spec(ts, MID_W), _row_spec(ts, LAT_W),
                  HBM_SPEC, _full_spec(wlat.shape), HBM_SPEC],
        out_specs=_row_spec(ts, D_MODEL),
        out_shape=jax.ShapeDtypeStruct((s, D_MODEL), F32),
        scratch_shapes=[pltpu.VMEM(wt.shape, BF16), pltpu.SemaphoreType.DMA((len(bounds),))],
        compiler_params=_params(),
    )(dr, dhg, dhm, dhl, wt, wlat, after)


def _dwt_early(dhmt, dhgt, xb2, col, after):
    tn = 512
    nm, ng = MID_W // tn, GATE_W // tn
    s = dhmt.shape[0]
    hc = D_MODEL // 2

    ks = s // 2

    def body(col_ref, dma_ref, dmb_ref, dga_ref, dgb_ref, xoa_ref, xob_ref, xca_ref, xcb_ref, after_ref,
             dwo_ref, dwc_ref):
        i = pl.program_id(0)

        def both(da_ref, db_ref):
            da, db = da_ref[...], db_ref[...]
            dwo_ref[...] = (_dot_tn(da, xoa_ref[...]) + _dot_tn(db, xob_ref[...])).astype(BF16)
            dwc_ref[...] = (_dot_tn(da, xca_ref[...]) + _dot_tn(db, xcb_ref[...])).astype(BF16)

        @pl.when(i < nm)
        def _():
            both(dma_ref, dmb_ref)

        @pl.when(i >= nm)
        def _():
            both(dga_ref, dgb_ref)

    def dh_spec(first, part):
        if first:
            return pl.BlockSpec((ks, tn), lambda i, col_ref: (part, jnp.minimum(i, nm - 1)))
        return pl.BlockSpec((ks, tn), lambda i, col_ref: (part, jnp.maximum(i - nm, 0)))

    def x_spec(other, part):
        if other:
            return pl.BlockSpec((None, ks, hc), lambda i, col_ref: (1 - col_ref[0], part, 0))
        return pl.BlockSpec((None, ks, hc), lambda i, col_ref: (col_ref[0], part, 0))

    out_spec = pl.BlockSpec((pl.Element(tn), pl.Element(hc)),
                            lambda i, col_ref: (pl.multiple_of(LAT_COLS + i * tn, 32), 0))
    rows = pl.pallas_call(
        body, name="dwt_early",
        grid_spec=pltpu.PrefetchScalarGridSpec(
            num_scalar_prefetch=1, grid=(nm + ng,),
            in_specs=[dh_spec(True, 0), dh_spec(True, 1), dh_spec(False, 0), dh_spec(False, 1),
                      x_spec(True, 0), x_spec(True, 1), x_spec(False, 0), x_spec(False, 1),
                      pl.BlockSpec(memory_space=pl.ANY)],
            out_specs=[out_spec, out_spec]),
        out_shape=[jax.ShapeDtypeStruct((IN_W, hc), BF16)] * 2,
        compiler_params=_params(),
    )(col, dhmt, dhmt, dhgt, dhgt, xb2, xb2, xb2, xb2, after)

    def zero(other_ref, mine_ref, other_out, mine_out):
        other_out[...] = jnp.zeros_like(other_out)
        mine_out[...] = jnp.zeros_like(mine_out)

    lat_rows = pl.BlockSpec((LAT_COLS, hc), lambda i: (0, 0))
    return pl.pallas_call(
        zero, name="dwt_early_zero_lat", grid=(1,), in_specs=[pl.BlockSpec(memory_space=pl.ANY)] * 2,
        out_specs=[lat_rows, lat_rows],
        out_shape=[jax.ShapeDtypeStruct((IN_W, hc), BF16)] * 2, input_output_aliases={0: 0, 1: 1},
    )(*rows)


def _dwt_lat(dhlt, xb2):
    n, s = dhlt.shape
    tk = MATMUL_ROW_TILE
    nsteps = s // tk

    def body(dht_ref, xb_ref, dw_ref, acc_ref):
        i = pl.program_id(0)

        @pl.when(i == 0)
        def _():
            acc_ref[...] = jnp.zeros_like(acc_ref)

        dht = dht_ref[...]
        acc_ref[...] += jnp.concatenate([_dot(dht, xb_ref[0]), _dot(dht, xb_ref[1])], axis=1)

        @pl.when(i == nsteps - 1)
        def _():
            kpe = Q_RANK + KV_RANK + NOPE
            dw_ref[0:Q_RANK + KV_RANK, :] = acc_ref[0:Q_RANK + KV_RANK, :].astype(BF16)
            dw_ref[Q_RANK + KV_RANK:LAT_COLS, :] = acc_ref[kpe:kpe + ROPE, :].astype(BF16)
            dw_ref[LAT_COLS:, :] = jnp.zeros((LAT_ROWS_PAD - LAT_COLS, D_MODEL), BF16)

    return pl.pallas_call(
        body, name="dwt_lat", grid=(nsteps,),
        in_specs=[pl.BlockSpec((n, tk), lambda i: (0, i)), pl.BlockSpec((2, tk, D_MODEL // 2), lambda i: (0, i, 0))],
        out_specs=_full_spec((LAT_ROWS_PAD, D_MODEL)),
        out_shape=jax.ShapeDtypeStruct((LAT_ROWS_PAD, D_MODEL), BF16),
        scratch_shapes=[pltpu.VMEM((n, D_MODEL), F32)],
        compiler_params=_params(),
    )(dhlt, xb2)


def _split_bias(b):
    z = lambda n: jnp.zeros((n,), b.dtype)
    lat = jnp.concatenate([b[:Q_RANK + KV_RANK], z(NOPE), b[Q_RANK + KV_RANK:LAT_COLS], z(HEAD_PAD - QK_DIM)])
    return b[None, ROW_GATE:], b[None, LAT_COLS:ROW_GATE], lat[None, :]


def _join_bias(g, m, l):
    kpe = Q_RANK + KV_RANK + NOPE
    return jnp.concatenate([l[0, :Q_RANK + KV_RANK], l[0, kpe:kpe + ROPE], m[0], g[0]])


def _rope_tables(positions):
    half = ROPE // 2
    inv_freq = ROPE_THETA ** (-jnp.arange(0, ROPE, 2, dtype=F32) / ROPE)
    ang = positions.astype(F32)[:, None] * inv_freq
    cos, sin = jnp.cos(ang), jnp.sin(ang)
    n = positions.shape[0]
    one, zero = jnp.ones((n, NOPE), F32), jnp.zeros((n, half), F32)
    tail1, tail0 = jnp.ones((n, HEAD_PAD - QK_DIM), F32), jnp.zeros((n, HEAD_PAD - QK_DIM), F32)
    z64 = jnp.zeros((n, NOPE), F32)
    rc = jnp.concatenate([one, cos, cos, tail1], axis=1)
    rsl = jnp.concatenate([z64, -sin, zero, tail0], axis=1)
    rsh = jnp.concatenate([z64, zero, sin, tail0], axis=1)
    return rc, rsl, rsh


def _pad_heads(w_uq):
    return jnp.pad(w_uq, ((0, 0), (0, 0), (0, HEAD_PAD - QK_DIM))).reshape(w_uq.shape[0], HEADS * HEAD_PAD)


def _prep_local(b_in, g_q, g_kv, w_ukv):
    b_g, b_m, b_l = _split_bias(b_in)
    wk = jnp.pad(w_ukv[:, :, :NOPE], ((0, 0), (0, 0), (0, HEAD_PAD - NOPE))).reshape(KV_RANK, HEADS * HEAD_PAD).astype(BF16)
    wv = w_ukv[:, :, NOPE:].reshape(KV_RANK, MLA_W).astype(BF16)
    return dict(b_g=b_g, b_m=b_m, b_l=b_l, wk=wk, wv=wv, gq2=g_q[None, :], gkv2=g_kv[None, :])


def _local_attention(x, tables, wlat, prep, wuq, after):
    rc, rsl, rsh = tables
    b_g, b_m, b_l, wk, wv, gq2, gkv2 = (prep[n] for n in ("b_g", "b_m", "b_l", "wk", "wv", "gq2", "gkv2"))
    hl, q, k, v, qt, kt, vt, xb2 = _fwd_lat(x, wlat, b_l, gq2, wuq, gkv2, wk, wv, rc, rsl, rsh, after)
    o, lse = _attn_fwd(qt, k, vt)
    return dict(q=q, qt=qt, k=k, kt=kt, v=v, o=o, lse=lse, hl=hl, rc=rc, rsl=rsl, rsh=rsh, gq2=gq2, gkv2=gkv2,
                wuq=wuq, wk=wk, wv=wv, xb2=xb2, b_g=b_g, b_m=b_m, wlat=wlat)


def _bias_lanes(b_s):
    return jnp.repeat(b_s.T, V_DIM, axis=1)


def _local_head(st, x, tgt, wt, w_oa, sg_g, sg_b, w_s, bsb, w_ob, w_out, ln_g, ln_b):
    q, qt, k, kt, v, o, lse, hl = (st[n] for n in ("q", "qt", "k", "kt", "v", "o", "lse", "hl"))
    rc, rsl, rsh, gq2, gkv2, wuq, wk, wv = (st[n] for n in ("rc", "rsl", "rsh", "gq2", "gkv2", "wuq", "wk", "wv"))
    hg, hm = _fwd_rest(st["xb2"], wt, st["b_g"], st["b_m"])
    (dr, dhg, dhm, do, dot, delta, dwout, dwoa, dwob, dws, dbs, dlng, dlnb, dsgg, dsgb, loss, dbg,
     dbm) = _mid(x, tgt, o, hm, hg, w_oa, w_ob, w_out, ln_g[None, :], ln_b[None, :], sg_g[None, :], sg_b[None, :],
                 w_s, bsb)
    early = {
        "w_oa": dwoa, "sgu_ln_g": dsgg[0], "sgu_ln_b": dsgb[0], "w_s": dws, "b_s": dbs[:, :GROUPS].T,
        "w_ob": dwob, "w_out": dwout, "ln_g": dlng[0], "ln_b": dlnb[0],
    }
    state = dict(q=q, qt=qt, k=k, kt=kt, v=v, do=do, dot=dot, lse=lse, delta=delta, hl=hl, rc=rc, rsl=rsl, rsh=rsh,
                 gq2=gq2, gkv2=gkv2, wuq=wuq, wk=wk, wv=wv, dr=dr, dhg=dhg, dhm=dhm, wt=wt, dbg=dbg, dbm=dbm,
                 xb2=st["xb2"], wlat=st["wlat"])
    return loss, early, state


def _local_attn_bwd(st, after, first_pair=0, n_pairs=HEADS // 2, into=(), name="attn_bwd"):
    return _attn_bwd(st["q"], st["qt"], st["k"], st["kt"], st["v"], st["do"], st["dot"], st["lse"], st["delta"],
                     after, first_pair, n_pairs, into, name)


def _local_tail(st, dq, dk, dv, after):
    dhl, dhlt, dwuq, dwk, dwv, dgq, dgkv, dbl = _lat_bwd(dq, dk, dv, st["hl"], st["rc"], st["rsl"], st["rsh"],
                                                         st["gq2"], st["gkv2"], st["wuq"], st["wk"], st["wv"], after)
    late = {
        "w_lat": _dwt_lat(dhlt, st["xb2"]),
        "b_in": _join_bias(st["dbg"], st["dbm"], dbl),
        "g_q": dgq[0],
        "w_uq": dwuq.reshape(Q_RANK, HEADS, HEAD_PAD)[:, :, :QK_DIM],
        "g_kv": dgkv[0],
        "w_ukv": jnp.concatenate([dwk.reshape(KV_RANK, HEADS, HEAD_PAD)[:, :, :NOPE],
                                  dwv.reshape(KV_RANK, HEADS, V_DIM)], axis=2),
    }
    return dhl, late


def _local_step(x, positions, tgt, wt, b_in, g_q, w_uq, g_kv, w_ukv, w_oa, sg_g, sg_b, w_s, b_s, w_ob, w_out, ln_g,
                ln_b):
    st = _local_attention(x, _rope_tables(positions), wt[:LAT_COLS], _prep_local(b_in, g_q, g_kv, w_ukv),
                          _pad_heads(w_uq).astype(BF16), b_in)
    loss, early, st = _local_head(st, x, tgt, wt, w_oa, sg_g, sg_b, w_s, _bias_lanes(b_s), w_ob, w_out, ln_g, ln_b)
    dq, dk, dv = _local_attn_bwd(st, loss)
    dhl, late = _local_tail(st, dq, dk, dv, dv)
    dx = _dx(st["dr"], st["dhg"], st["dhm"], dhl, st["wt"], st["wlat"], dhl)
    grads = {**early, **late}
    right, left = _dwt_early(st["dhm"], st["dhg"], st["xb2"], jnp.zeros((1,), jnp.int32), dhl)
    grads["w_in"] = jnp.concatenate([grads.pop("w_lat")[:LAT_COLS], jnp.concatenate([left, right], axis=1)[LAT_COLS:]],
                                    axis=0)
    return loss, dx, grads


MESH = pl.DeviceIdType.MESH
N_CHIPS = 4
HBM_SPEC = pl.BlockSpec(memory_space=pl.ANY)
HBM_SPEC_STRICT = pl.BlockSpec(memory_space=pltpu.HBM)
VMEM_SPEC = pl.BlockSpec(memory_space=pltpu.VMEM)

REP_ROWS = 80


def _rows8(a):
    flat = a.reshape(-1)
    n = -(-flat.shape[0] // (8 * D_MODEL)) * 8 * D_MODEL
    return jnp.pad(flat, (0, n - flat.shape[0])).reshape(-1, D_MODEL)


def _place():
    x, y, c = lax.axis_index("x"), lax.axis_index("y"), lax.axis_index("c")
    others = [(1 - x, y), (x, 1 - y), (1 - x, 1 - y)]
    return x, y, c, others


N_DEV = 8
LOSS_TILE = (8, 128)


def _cast_own(shards, me, after):
    n = len(shards)

    def body(me_ref, *refs):
        for w in range(n):
            refs[n + 1 + w][...] = refs[w][...].astype(BF16)

    return pl.pallas_call(
        body, name="cast_own",
        grid_spec=pltpu.PrefetchScalarGridSpec(
            num_scalar_prefetch=1, grid=(1,),
            in_specs=[pl.BlockSpec(s.shape, lambda i, me_ref: (0, 0)) for s in shards]
            + [pl.BlockSpec(memory_space=pl.ANY)],
            out_specs=[pl.BlockSpec((None,) + s.shape, lambda i, me_ref: (me_ref[0], 0, 0)) for s in shards]),
        out_shape=[jax.ShapeDtypeStruct((N_CHIPS,) + s.shape, BF16) for s in shards],
        compiler_params=pltpu.CompilerParams(vmem_limit_bytes=VMEM_LIMIT),
    )(me, *shards, after)


def _cast_first(lat, uq, me):
    def body(me_ref, lat_ref, uq_ref, wlat_ref, guq_ref):
        wlat_ref[...] = lat_ref[...].astype(BF16)
        guq_ref[...] = uq_ref[...].astype(BF16)

    return pl.pallas_call(
        body, name="cast_first",
        grid_spec=pltpu.PrefetchScalarGridSpec(
            num_scalar_prefetch=1, grid=(1,),
            in_specs=[pl.BlockSpec((LAT_COLS, D_MODEL), lambda i, me_ref: (0, 0)),
                      pl.BlockSpec(uq.shape, lambda i, me_ref: (0, 0))],
            out_specs=[pl.BlockSpec((LAT_COLS, D_MODEL), lambda i, me_ref: (0, 0)),
                       pl.BlockSpec((None,) + uq.shape, lambda i, me_ref: (me_ref[0], 0, 0))]),
        out_shape=[jax.ShapeDtypeStruct((LAT_COLS, D_MODEL), BF16),
                   jax.ShapeDtypeStruct((N_CHIPS,) + uq.shape, BF16)],
    )(me, lat, uq)


def _first_copies(wlat_ref, guq_ref, send_sems, recv_sems, shapes):
    x, y, c, others = _place()
    me = 2 * x + y
    hl, hu = shapes[0][1] // 2, shapes[1][2] // 2
    lat_half = wlat_ref.at[:, pl.ds(c * hl, hl)]

    def copy(src, dst, k, to):
        return pltpu.make_async_remote_copy(src_ref=src, dst_ref=dst, send_sem=send_sems.at[k],
                                            recv_sem=recv_sems.at[k], device_id=to, device_id_type=MESH)

    def uq_half(chip):
        return guq_ref.at[chip, :, pl.ds(c * hu, hu)]

    lat_out = [copy(lat_half, lat_half, j, (*others[j], c)) for j in range(3)]
    uq_out = [copy(uq_half(me), uq_half(me), 3 + j, (*others[j], c)) for j in range(3)]
    j0 = jnp.maximum(x + 2 * y - 1, 0)
    lat_in = copy(lat_half, lat_half, j0, (0, 0, c))
    uq_in = [copy(uq_half(me), uq_half(2 * px + py), 3 + j, (px, py, c)) for j, (px, py) in enumerate(others)]
    return me, lat_out, uq_out, lat_in, uq_in


def _first_start(wlat, guq):
    shapes = (wlat.shape, guq.shape)

    def body(wlat_ref, guq_ref, send_sems, recv_sems, wlat_thru, guq_thru, token):
        me, lat_out, uq_out, _, _ = _first_copies(wlat_ref, guq_ref, send_sems, recv_sems, shapes)

        @pl.when(me == 0)
        def _():
            for cp in lat_out:
                cp.start()

        for cp in uq_out:
            cp.start()
        token[...] = jnp.zeros_like(token)

    outs = pl.pallas_call(
        body, name="first_start",
        out_shape=(pltpu.SemaphoreType.DMA((6,)), pltpu.SemaphoreType.DMA((6,)), pltpu.HBM(wlat.shape, BF16),
                   pltpu.HBM(guq.shape, BF16), jax.ShapeDtypeStruct(LOSS_TILE, F32)),
        in_specs=[HBM_SPEC_STRICT] * 2, out_specs=(SEM_SPEC, SEM_SPEC, HBM_SPEC_STRICT, HBM_SPEC_STRICT, VMEM_SPEC),
        input_output_aliases={0: 2, 1: 3},
        compiler_params=pltpu.CompilerParams(has_side_effects=SPLIT_EFFECT),
    )(pltpu.with_memory_space_constraint(wlat, pltpu.HBM), pltpu.with_memory_space_constraint(guq, pltpu.HBM))
    return outs


def _first_wait(send_sems, recv_sems, wlat, guq, *after):
    shapes = (wlat.shape, guq.shape)

    def body(wlat_ref, guq_ref, send_sems, recv_sems, *rest):
        me, lat_out, uq_out, lat_in, uq_in = _first_copies(wlat_ref, guq_ref, send_sems, recv_sems, shapes)

        @pl.when(me == 0)
        def _():
            for cp in lat_out:
                cp.wait_send()

        @pl.when(me != 0)
        def _():
            lat_in.wait_recv()

        for cp in uq_out:
            cp.wait_send()
        for cp in uq_in:
            cp.wait_recv()

    return pl.pallas_call(
        body, name="first_wait", out_shape=(pltpu.HBM(wlat.shape, BF16), pltpu.HBM(guq.shape, BF16)),
        in_specs=[HBM_SPEC_STRICT, HBM_SPEC_STRICT, SEM_SPEC, SEM_SPEC] + [HBM_SPEC] * len(after),
        out_specs=(HBM_SPEC_STRICT, HBM_SPEC_STRICT), input_output_aliases={0: 0, 1: 1},
        compiler_params=pltpu.CompilerParams(has_side_effects=SPLIT_EFFECT),
    )(wlat, guq, send_sems, recv_sems, *after)


def _first_forward(wlat, guq):
    hl, hu = wlat.shape[1] // 2, guq.shape[2] // 2

    def body(wlat_in, guq_in, wlat_ref, guq_ref, send_sems, recv_sems):
        x, y, c, others = _place()
        me = 2 * x + y
        sibling = (x, y, 1 - c)

        def copy(part, k):
            return pltpu.make_async_remote_copy(src_ref=part, dst_ref=part, send_sem=send_sems.at[k],
                                                recv_sem=recv_sems.at[k], device_id=sibling, device_id_type=MESH)

        def uq_part(j, half):
            px, py = others[j]
            return guq_ref.at[2 * px + py, :, pl.ds(half * hu, hu)]

        cps = [copy(uq_part(j, c), j) for j in range(3)]
        for cp in cps:
            cp.start()

        @pl.when(me != 0)
        def _():
            mine = copy(wlat_ref.at[:, pl.ds(c * hl, hl)], 3)
            mine.start()
            copy(wlat_ref.at[:, pl.ds((1 - c) * hl, hl)], 3).wait_recv()
            mine.wait_send()

        for j in range(3):
            copy(uq_part(j, 1 - c), j).wait_recv()
        for cp in cps:
            cp.wait_send()

    return pl.pallas_call(
        body, name="first_forward", in_specs=[HBM_SPEC, HBM_SPEC], out_specs=[HBM_SPEC, HBM_SPEC],
        out_shape=[jax.ShapeDtypeStruct(wlat.shape, BF16), jax.ShapeDtypeStruct(guq.shape, BF16)],
        input_output_aliases={0: 0, 1: 1},
        scratch_shapes=[pltpu.SemaphoreType.DMA((4,)), pltpu.SemaphoreType.DMA((4,))],
    )(wlat, guq)


def _gather_copy(b_ref, buf, w, j, src_chip, dst_chip, to, rows, send_sems, recv_sems):
    _, _, c, _ = _place()
    hc = _half(buf)
    return pltpu.make_async_remote_copy(
        src_ref=b_ref.at[src_chip, rows, pl.ds(c * hc, hc)], dst_ref=b_ref.at[dst_chip, rows, pl.ds(c * hc, hc)],
        send_sem=send_sems.at[3 * w + j], recv_sem=recv_sems.at[3 * w + j], device_id=to, device_id_type=MESH)


def _gather_rows(buf, w, chip, fn):
    if w != 0:
        fn(slice(None))
        return
    pl.when(chip == 0)(lambda: fn(pl.ds(LAT_COLS, buf.shape[1] - LAT_COLS)))
    pl.when(chip != 0)(lambda: fn(slice(None)))


def _gather_start(bufs, after):
    n = len(bufs)

    def body(*refs):
        b_refs = refs[:n]
        send_sems, recv_sems, token = refs[n + 1], refs[n + 2], refs[-1]
        x, y, c, others = _place()
        me = 2 * x + y
        for w in range(n):
            def start(rows, w=w):
                for j, (px, py) in enumerate(others):
                    _gather_copy(b_refs[w], bufs[w], w, j, me, me, (px, py, c), rows, send_sems, recv_sems).start()
            _gather_rows(bufs[w], w, me, start)
        token[...] = jnp.zeros_like(token)

    hbm = [pltpu.HBM(b.shape, BF16) for b in bufs]
    outs = pl.pallas_call(
        body, name="gather_start",
        out_shape=(pltpu.SemaphoreType.DMA((3 * n,)), pltpu.SemaphoreType.DMA((3 * n,)), *hbm,
                   jax.ShapeDtypeStruct(LOSS_TILE, F32)),
        in_specs=[HBM_SPEC_STRICT] * n + [HBM_SPEC],
        out_specs=(SEM_SPEC, SEM_SPEC, *[HBM_SPEC_STRICT] * n, VMEM_SPEC),
        input_output_aliases={i: 2 + i for i in range(n)},
        compiler_params=pltpu.CompilerParams(has_side_effects=SPLIT_EFFECT),
    )(*[pltpu.with_memory_space_constraint(b, pltpu.HBM) for b in bufs], after)
    return outs[0], outs[1], list(outs[2:2 + n]), outs[-1]


def _gather_wait(send_sems, recv_sems, bufs, after):
    n = len(bufs)

    def body(*refs):
        b_refs = refs[:n]
        send_sems, recv_sems = refs[n], refs[n + 1]
        x, y, c, others = _place()
        me = 2 * x + y
        for w in range(n):
            for j, (px, py) in enumerate(others):
                def copy(rows, w=w, j=j, px=px, py=py):
                    return _gather_copy(b_refs[w], bufs[w], w, j, me, 2 * px + py, (px, py, c), rows, send_sems,
                                        recv_sems)
                _gather_rows(bufs[w], w, me, lambda rows, copy=copy: copy(rows).wait_send())
                _gather_rows(bufs[w], w, 2 * px + py, lambda rows, copy=copy: copy(rows).wait_recv())

    outs = pl.pallas_call(
        body, name="gather_wait", out_shape=tuple(pltpu.HBM(b.shape, b.dtype) for b in bufs),
        in_specs=[HBM_SPEC_STRICT] * n + [SEM_SPEC, SEM_SPEC, HBM_SPEC],
        out_specs=tuple([HBM_SPEC_STRICT] * n), input_output_aliases={i: i for i in range(n)},
        compiler_params=pltpu.CompilerParams(has_side_effects=SPLIT_EFFECT),
    )(*bufs, send_sems, recv_sems, after)
    return list(outs)


def _gather_finish(bufs):
    n = len(bufs)

    def body(*refs):
        b_refs = refs[n:2 * n]
        send_sems, recv_sems = refs[2 * n:]
        x, y, c, others = _place()
        cps = []
        for w in range(n):
            hc = _half(bufs[w])
            for j, (px, py) in enumerate(others):
                part = b_refs[w].at[2 * px + py, :, pl.ds(c * hc, hc)]
                cps.append(pltpu.make_async_remote_copy(
                    src_ref=part, dst_ref=part, send_sem=send_sems.at[3 * w + j], recv_sem=recv_sems.at[3 * w + j],
                    device_id=(x, y, 1 - c), device_id_type=MESH))
        for cp in cps:
            cp.start()
        for w in range(n):
            hc = _half(bufs[w])
            for j, (px, py) in enumerate(others):
                theirs = b_refs[w].at[2 * px + py, :, pl.ds((1 - c) * hc, hc)]
                pltpu.make_async_remote_copy(
                    src_ref=theirs, dst_ref=theirs, send_sem=send_sems.at[3 * w + j], recv_sem=recv_sems.at[3 * w + j],
                    device_id=(x, y, 1 - c), device_id_type=MESH).wait_recv()
        for cp in cps:
            cp.wait_send()

    return pl.pallas_call(
        body, name="gather_finish", in_specs=[HBM_SPEC] * n, out_specs=[HBM_SPEC] * n,
        out_shape=[jax.ShapeDtypeStruct(b.shape, b.dtype) for b in bufs],
        input_output_aliases={i: i for i in range(n)},
        scratch_shapes=[pltpu.SemaphoreType.DMA((3 * n,)), pltpu.SemaphoreType.DMA((3 * n,))],
    )(*bufs)


def _half(a):
    return a.shape[-1] // 2


def _exchange_pairs(parts, name):
    n = len(parts)

    def body(*refs):
        p_refs, r_refs = refs[:n], refs[n:2 * n]
        send_sems, recv_sems = refs[2 * n:]
        x, y, c, _ = _place()
        cps = []
        for w in range(n):
            h = _half(parts[w])
            cps.append(pltpu.make_async_remote_copy(
                src_ref=p_refs[w].at[:, :, pl.ds((1 - c) * h, h)], dst_ref=r_refs[w],
                send_sem=send_sems.at[w], recv_sem=recv_sems.at[w], device_id=(x, y, 1 - c), device_id_type=MESH))
        for cp in cps:
            cp.start()
        for cp in cps:
            cp.wait()

    return pl.pallas_call(
        body, name=name, in_specs=[HBM_SPEC] * n, out_specs=[HBM_SPEC] * n,
        out_shape=[jax.ShapeDtypeStruct((N_CHIPS, p.shape[1], _half(p)), BF16) for p in parts],
        scratch_shapes=[pltpu.SemaphoreType.DMA((n,)), pltpu.SemaphoreType.DMA((n,))],
    )(*parts)


def _sibling_part(ref, w, n_whole, shape, c):
    if w < n_whole:
        return ref
    h = shape[-1] // 2
    return ref.at[:, :, pl.ds((1 - c) * h, h)]


def _pairs_start(parts, all_loss, n_whole):
    n = len(parts)

    def body(*refs):
        p_refs, r_refs, loss_ref = refs[:n], refs[n:2 * n], refs[2 * n]
        send_sems, recv_sems, token = refs[2 * n + 1], refs[2 * n + 2], refs[-1]
        x, y, c, _ = _place()
        for w in range(n):
            h = _half(parts[w])
            pltpu.make_async_remote_copy(
                src_ref=_sibling_part(p_refs[w], w, n_whole, parts[w].shape, c), dst_ref=r_refs[w],
                send_sem=send_sems.at[w], recv_sem=recv_sems.at[w], device_id=(x, y, 1 - c),
                device_id_type=MESH).start()
        me = 4 * x + 2 * y + c
        for t in range(1, N_DEV):
            d = (me + t) % N_DEV
            pltpu.make_async_remote_copy(
                src_ref=loss_ref.at[me], dst_ref=loss_ref.at[me], send_sem=send_sems.at[n + t - 1],
                recv_sem=recv_sems.at[n + t - 1], device_id=(d // 4, (d // 2) % 2, d % 2), device_id_type=MESH).start()
        token[...] = jnp.zeros_like(token)

    lands = [pltpu.HBM(p.shape if w < n_whole else (N_CHIPS, p.shape[1], _half(p)), BF16)
             for w, p in enumerate(parts)]
    nsem = n + N_DEV - 1
    outs = pl.pallas_call(
        body, name="pairs_start",
        out_shape=(pltpu.SemaphoreType.DMA((nsem,)), pltpu.SemaphoreType.DMA((nsem,)),
                   *[pltpu.HBM(p.shape, p.dtype) for p in parts], *lands, pltpu.HBM(all_loss.shape, F32),
                   jax.ShapeDtypeStruct(LOSS_TILE, F32)),
        in_specs=[HBM_SPEC_STRICT] * (2 * n + 1),
        out_specs=(SEM_SPEC, SEM_SPEC, *[HBM_SPEC_STRICT] * (2 * n + 1), VMEM_SPEC),
        input_output_aliases={i: 2 + i for i in range(2 * n + 1)},
        compiler_params=pltpu.CompilerParams(has_side_effects=SPLIT_EFFECT),
    )(*[pltpu.with_memory_space_constraint(p, pltpu.HBM) for p in parts],
      *[pltpu.with_memory_space_constraint(lax.empty(l.shape, BF16), pltpu.HBM) for l in lands],
      pltpu.with_memory_space_constraint(all_loss, pltpu.HBM))
    return outs[0], outs[1], list(outs[2:2 + n]), list(outs[2 + n:2 + 2 * n]), outs[2 + 2 * n], outs[-1]


def _pairs_wait(send_sems, recv_sems, parts, lands, all_loss, after, n_whole):
    n = len(parts)

    def body(*refs):
        p_refs, r_refs, loss_ref = refs[:n], refs[n:2 * n], refs[2 * n]
        send_sems, recv_sems = refs[2 * n + 1], refs[2 * n + 2]
        x, y, c, _ = _place()
        for w in range(n):
            h = _half(parts[w])
            cp = pltpu.make_async_remote_copy(
                src_ref=_sibling_part(p_refs[w], w, n_whole, parts[w].shape, c), dst_ref=r_refs[w],
                send_sem=send_sems.at[w],
                recv_sem=recv_sems.at[w], device_id=(x, y, 1 - c), device_id_type=MESH)
            cp.wait_send()
            cp.wait_recv()
        me = 4 * x + 2 * y + c
        for t in range(1, N_DEV):
            d = (me + N_DEV - t) % N_DEV
            cp = pltpu.make_async_remote_copy(
                src_ref=loss_ref.at[me], dst_ref=loss_ref.at[d], send_sem=send_sems.at[n + t - 1],
                recv_sem=recv_sems.at[n + t - 1], device_id=(d // 4, (d // 2) % 2, d % 2), device_id_type=MESH)
            cp.wait_send()
            cp.wait_recv()

    bufs = (*parts, *lands, all_loss)
    outs = pl.pallas_call(
        body, name="pairs_wait", out_shape=tuple(pltpu.HBM(a.shape, a.dtype) for a in bufs),
        in_specs=[HBM_SPEC_STRICT] * len(bufs) + [SEM_SPEC, SEM_SPEC, HBM_SPEC],
        out_specs=tuple([HBM_SPEC_STRICT] * len(bufs)), input_output_aliases={i: i for i in range(len(bufs))},
        compiler_params=pltpu.CompilerParams(has_side_effects=SPLIT_EFFECT),
    )(*bufs, send_sems, recv_sems, after)
    return list(outs[:n]), list(outs[n:2 * n]), outs[2 * n]


def _add_pair(ps, rs, c, name, n_whole=0):
    n = len(ps)

    def body(c_ref, *refs):
        for w in range(n):
            refs[2 * n + w][...] = (refs[w][...].astype(F32) + refs[n + w][...].astype(F32)).astype(BF16)

    def slab_spec(a):
        return pl.BlockSpec((None,) + a.shape[1:], lambda k, c_ref: (k, 0, 0))

    def my_half_spec(a):
        return pl.BlockSpec((None, a.shape[1], _half(a)), lambda k, c_ref: (k, 0, c_ref[0]))

    return pl.pallas_call(
        body, name=name,
        grid_spec=pltpu.PrefetchScalarGridSpec(
            num_scalar_prefetch=1, grid=(N_CHIPS,),
            in_specs=[slab_spec(p) if w < n_whole else my_half_spec(p) for w, p in enumerate(ps)]
            + [slab_spec(r) for r in rs],
            out_specs=[slab_spec(r) for r in rs]),
        out_shape=[jax.ShapeDtypeStruct(r.shape, BF16) for r in rs],
        compiler_params=_params(),
    )(c, *ps, *rs)


SEM_SPEC = pl.BlockSpec(memory_space=pltpu.SEMAPHORE)
SPLIT_EFFECT = pltpu.SideEffectType.DATAFLOW_SIDE_EFFECTING


def _chips_start(qs, name):
    n = len(qs)

    def body(*refs):
        q_refs, land_refs = refs[:n], refs[n:2 * n]
        send_sems, recv_sems, token = refs[2 * n], refs[2 * n + 1], refs[-1]
        x, y, c, others = _place()
        me = 2 * x + y
        for w in range(n):
            for j, (px, py) in enumerate(others):
                pltpu.make_async_remote_copy(
                    src_ref=q_refs[w].at[2 * px + py], dst_ref=land_refs[w].at[me], send_sem=send_sems.at[3 * w + j],
                    recv_sem=recv_sems.at[3 * w + j], device_id=(px, py, c), device_id_type=MESH).start()
        token[...] = jnp.zeros_like(token)

    hbm = [pltpu.HBM(q.shape, BF16) for q in qs]
    outs = pl.pallas_call(
        body, name=name,
        out_shape=(pltpu.SemaphoreType.DMA((3 * n,)), pltpu.SemaphoreType.DMA((3 * n,)), *hbm, *hbm,
                   jax.ShapeDtypeStruct(LOSS_TILE, F32)),
        in_specs=[HBM_SPEC_STRICT] * (2 * n),
        out_specs=(SEM_SPEC, SEM_SPEC, *[HBM_SPEC_STRICT] * (2 * n), VMEM_SPEC),
        input_output_aliases={i: 2 + i for i in range(2 * n)},
        compiler_params=pltpu.CompilerParams(has_side_effects=SPLIT_EFFECT),
    )(*[pltpu.with_memory_space_constraint(q, pltpu.HBM) for q in qs],
      *[pltpu.with_memory_space_constraint(lax.empty(q.shape, BF16), pltpu.HBM) for q in qs])
    return outs[0], outs[1], outs[2:2 + n], outs[2 + n:2 + 2 * n], outs[-1]


def _chips_wait(send_sems, recv_sems, q_thru, land_thru, after, name):
    n = len(q_thru)

    def body(*refs):
        q_refs, land_refs = refs[:n], refs[n:2 * n]
        send_sems, recv_sems = refs[2 * n], refs[2 * n + 1]
        x, y, c, others = _place()
        me = 2 * x + y
        for w in range(n):
            for j, (px, py) in enumerate(others):
                cp = pltpu.make_async_remote_copy(
                    src_ref=q_refs[w].at[2 * px + py], dst_ref=land_refs[w].at[2 * px + py],
                    send_sem=send_sems.at[3 * w + j], recv_sem=recv_sems.at[3 * w + j], device_id=(px, py, c),
                    device_id_type=MESH)
                cp.wait_send()
                cp.wait_recv()

    outs = pl.pallas_call(
        body, name=name, out_shape=tuple(pltpu.HBM(a.shape, a.dtype) for a in (*q_thru, *land_thru)),
        in_specs=[HBM_SPEC_STRICT] * (2 * n) + [SEM_SPEC, SEM_SPEC, HBM_SPEC],
        out_specs=tuple([HBM_SPEC_STRICT] * (2 * n)), input_output_aliases={i: i for i in range(2 * n)},
        compiler_params=pltpu.CompilerParams(has_side_effects=SPLIT_EFFECT),
    )(*q_thru, *land_thru, send_sems, recv_sems, after)
    return list(outs[:n]), list(outs[n:])


def _sum_chips(qs, rs, idx, all_dtypes):
    n = len(rs)
    n_all = len(all_dtypes)

    def body(idx_ref, *refs):
        c = idx_ref[4]
        for w in range(n):
            q_ref, g_ref = refs[w], refs[4 * n + w]
            acc = q_ref[...].astype(F32)
            for t in range(1, N_CHIPS):
                acc = acc + refs[n + 3 * w + t - 1][...].astype(F32)
            h = rs[w].shape[2]
            mine = pl.ds(pl.multiple_of(c * h, 128), h)
            g_ref[...] = jnp.zeros_like(g_ref)
            if w >= n - n_all:
                g_ref[idx_ref[0], :, mine] = acc.astype(g_ref.dtype)
            else:
                g_ref[:, mine] = acc

    shapes = [jax.ShapeDtypeStruct((r.shape[1], 2 * r.shape[2]), F32) for r in rs[:n - n_all]]
    shapes += [jax.ShapeDtypeStruct((N_CHIPS, r.shape[1], 2 * r.shape[2]), dt)
               for r, dt in zip(rs[n - n_all:], all_dtypes)]

    def slab_spec(a, t):
        return pl.BlockSpec((None,) + a.shape[1:], lambda i, idx_ref: (idx_ref[t], 0, 0))

    def whole_spec(sh):
        return pl.BlockSpec(sh.shape, lambda i, idx_ref: (0,) * len(sh.shape))

    return pl.pallas_call(
        body, name="sum_chips",
        grid_spec=pltpu.PrefetchScalarGridSpec(
            num_scalar_prefetch=1, grid=(1,),
            in_specs=[slab_spec(q, 0) for q in qs] + [slab_spec(r, t) for r in rs for t in range(1, N_CHIPS)],
            out_specs=[whole_spec(sh) for sh in shapes]),
        out_shape=shapes, compiler_params=_params(),
    )(idx, *qs, *[r for r in rs for _ in range(1, N_CHIPS)])


def _share(shards, alls):
    n, na = len(shards), len(alls)
    total = n + na

    def body(*refs):
        g_refs, a_refs = refs[total:total + n], refs[total + n:2 * total]
        send_sems, recv_sems = refs[2 * total:]
        x, y, c, others = _place()
        me = 2 * x + y
        sibling = (x, y, 1 - c)

        def cols_of(w, half):
            h = shards[w].shape[1] // 2
            return g_refs[w].at[:, pl.ds(half * h, h)]

        def slab(a, chip, half):
            h = alls[a].shape[2] // 2
            return a_refs[a].at[chip, :, pl.ds(half * h, h)]

        def copy(src, dst, k, to):
            return pltpu.make_async_remote_copy(src_ref=src, dst_ref=dst, send_sem=send_sems.at[k],
                                                recv_sem=recv_sems.at[k], device_id=to, device_id_type=MESH)

        cps = []
        for a in range(na):
            base = n + 7 * a
            for j, (px, py) in reversed(list(enumerate(others))):
                cps.append(copy(slab(a, me, c), slab(a, me, c), base + 1 + j, (px, py, c)))
            cps.append(copy(slab(a, me, c), slab(a, me, c), base, sibling))
        cps += [copy(cols_of(w, c), cols_of(w, c), w, sibling) for w in range(n)]
        for cp in cps:
            cp.start()
        fwd = []
        for a in range(na):
            base = n + 7 * a
            for j, (px, py) in enumerate(others):
                chip = 2 * px + py
                copy(slab(a, me, c), slab(a, chip, c), base + 1 + j, (px, py, c)).wait_recv()
                cp = copy(slab(a, chip, c), slab(a, chip, c), base + 4 + j, sibling)
                cp.start()
                fwd.append(cp)
        for a in range(na):
            base = n + 7 * a
            for j, (px, py) in enumerate(others):
                chip = 2 * px + py
                copy(slab(a, chip, c), slab(a, chip, 1 - c), base + 4 + j, sibling).wait_recv()
            copy(slab(a, me, c), slab(a, me, 1 - c), base, sibling).wait_recv()
        for w in range(n):
            copy(cols_of(w, c), cols_of(w, 1 - c), w, sibling).wait_recv()
        for cp in cps + fwd:
            cp.wait_send()

    nsem = n + 7 * na
    return pl.pallas_call(
        body, name="share", in_specs=[HBM_SPEC_STRICT] * total, out_specs=[HBM_SPEC_STRICT] * total,
        out_shape=[pltpu.HBM(a.shape, a.dtype) for a in (*shards, *alls)],
        input_output_aliases={i: i for i in range(total)},
        scratch_shapes=[pltpu.SemaphoreType.DMA((nsem,)), pltpu.SemaphoreType.DMA((nsem,))],
    )(*[pltpu.with_memory_space_constraint(a, pltpu.HBM) for a in (*shards, *alls)])


def _adamw(w, g, m, v):
    m2 = ADAM_B1 * m + (1.0 - ADAM_B1) * g
    v2 = ADAM_B2 * v + (1.0 - ADAM_B2) * (g * g)
    m_hat = m2 / (1.0 - ADAM_B1 ** ADAM_STEP)
    v_hat = v2 / (1.0 - ADAM_B2 ** ADAM_STEP)
    return -ADAM_LR * (m_hat / (jnp.sqrt(v_hat) + ADAM_EPS) + ADAM_WD * w), m2, v2


def _update_w_in(wt, gt, mt, vt, lat, owner, tile):
    nlat = lat.shape[0] // tile

    def body(owner_ref, w_ref, g_ref, m_ref, v_ref, lat_ref, g2_ref, d_ref, m2_ref, v2_ref):
        row = pl.program_id(0) * tile + lax.broadcasted_iota(jnp.int32, (tile, 1), 0)
        g = jnp.where((row < LAT_COLS) & (owner_ref[0] == 1), lat_ref[...].astype(F32), g_ref[...])
        g2_ref[...] = g
        d_ref[...], m2_ref[...], v2_ref[...] = _adamw(w_ref[...], g, m_ref[...], v_ref[...])

    spec = pl.BlockSpec((tile, wt.shape[1]), lambda i, o: (i, 0))
    return pl.pallas_call(
        body, name="update_w_in",
        grid_spec=pltpu.PrefetchScalarGridSpec(
            num_scalar_prefetch=1, grid=(wt.shape[0] // tile,),
            in_specs=[spec] * 4 + [pl.BlockSpec((tile, wt.shape[1]), lambda i, o: (jnp.minimum(i, nlat - 1), 0))],
            out_specs=[spec] * 4),
        out_shape=[jax.ShapeDtypeStruct(wt.shape, F32)] * 4,
        compiler_params=_params(("parallel",)),
    )(owner, wt, gt, mt, vt, lat)


def _update_small(ws, gs, ms, vs):
    n = len(ws)

    def body(*refs):
        for k in range(n):
            w_ref, g_ref, m_ref, v_ref = refs[k], refs[n + k], refs[2 * n + k], refs[3 * n + k]
            d, m2, v2 = _adamw(w_ref[...], g_ref[...], m_ref[...], v_ref[...])
            refs[4 * n + k][...] = d
            refs[5 * n + k][...] = m2
            refs[6 * n + k][...] = v2

    shapes = [jax.ShapeDtypeStruct(w.shape, F32) for w in ws]
    outs = pl.pallas_call(
        body, name="update_small", in_specs=[VMEM_SPEC] * (4 * n), out_specs=[VMEM_SPEC] * (3 * n),
        out_shape=shapes * 3,
        compiler_params=pltpu.CompilerParams(vmem_limit_bytes=VMEM_LIMIT),
    )(*ws, *gs, *ms, *vs)
    return outs[:n], outs[n:2 * n], outs[2 * n:]


REPLICATED = ("b_in", "g_q", "g_kv", "w_ukv", "sgu_ln_g", "sgu_ln_b", "w_s", "b_s", "ln_g", "ln_b")
ORDER = ("w_in", "b_in", "g_q", "w_uq", "g_kv", "w_ukv", "w_oa", "sgu_ln_g", "sgu_ln_b", "w_s", "b_s", "w_ob", "w_out",
         "ln_g", "ln_b")


def kernel(x, positions, w_in, b_in, g_q, w_uq, g_kv, w_ukv, w_oa, sgu_ln_g, sgu_ln_b, w_s, b_s, w_ob, w_out, ln_g, ln_b, loss_target, m_w_in, m_b_in, m_g_q, m_w_uq, m_g_kv, m_w_ukv, m_w_oa, m_sgu_ln_g, m_sgu_ln_b, m_w_s, m_b_s, m_w_ob, m_w_out, m_ln_g, m_ln_b, v_w_in, v_b_in, v_g_q, v_w_uq, v_g_kv, v_w_ukv, v_w_oa, v_sgu_ln_g, v_sgu_ln_b, v_w_s, v_b_s, v_w_ob, v_w_out, v_ln_g, v_ln_b):
    w = dict(w_in=w_in, b_in=b_in, g_q=g_q, w_uq=w_uq, g_kv=g_kv, w_ukv=w_ukv, w_oa=w_oa, sgu_ln_g=sgu_ln_g,
             sgu_ln_b=sgu_ln_b, w_s=w_s, b_s=b_s, w_ob=w_ob, w_out=w_out, ln_g=ln_g, ln_b=ln_b)
    m = dict(w_in=m_w_in, b_in=m_b_in, g_q=m_g_q, w_uq=m_w_uq, g_kv=m_g_kv, w_ukv=m_w_ukv, w_oa=m_w_oa,
             sgu_ln_g=m_sgu_ln_g, sgu_ln_b=m_sgu_ln_b, w_s=m_w_s, b_s=m_b_s, w_ob=m_w_ob, w_out=m_w_out, ln_g=m_ln_g,
             ln_b=m_ln_b)
    v = dict(w_in=v_w_in, b_in=v_b_in, g_q=v_g_q, w_uq=v_w_uq, g_kv=v_g_kv, w_ukv=v_w_ukv, w_oa=v_w_oa,
             sgu_ln_g=v_sgu_ln_g, sgu_ln_b=v_sgu_ln_b, w_s=v_w_s, b_s=v_b_s, w_ob=v_w_ob, w_out=v_w_out, ln_g=v_ln_g,
             ln_b=v_ln_b)
    w, m, v = ({n: a[0] for n, a in d.items()} for d in (w, m, v))
    c = lax.axis_index("c")

    wt_shard, mt_shard, vt_shard = (jnp.transpose(d["w_in"]) for d in (w, m, v))
    xi, yi = lax.axis_index("x"), lax.axis_index("y")
    me1 = (2 * xi + yi).reshape(1).astype(jnp.int32)
    first = _first_start(*_cast_first(wt_shard, _pad_heads(w["w_uq"]), me1))
    tables = _rope_tables(positions[0])
    prep = _prep_local(w["b_in"], w["g_q"], w["g_kv"], w["w_ukv"])
    bsb = _bias_lanes(w["b_s"])
    c1 = c.reshape(1).astype(jnp.int32)
    idx = jnp.stack([2 * xi + yi, 2 * (1 - xi) + yi, 2 * xi + (1 - yi), 2 * (1 - xi) + (1 - yi), c]).astype(jnp.int32)
    bufs = _cast_own([wt_shard, w["w_oa"], w["w_ob"], w["w_out"]], me1, first[4])
    send0, recv0, bufs, token0 = _gather_start(bufs, first[4])
    g_lat, g_uq = _first_forward(*_first_wait(*first[:4], token0, *tables, *prep.values(), bsb, idx))
    st = _local_attention(x[0], tables, g_lat, prep, g_uq.reshape(Q_RANK, HEADS * HEAD_PAD), token0)
    g_in, g_oa, g_ob, g_out = _gather_finish(_gather_wait(send0, recv0, bufs, st["o"]))
    wt = g_in.reshape(IN_W, D_MODEL)

    loss, early, st = _local_head(
        st, x[0], loss_target[0], wt, g_oa, w["sgu_ln_g"], w["sgu_ln_b"], w["w_s"], bsb, g_ob,
        g_out.reshape(D_MODEL, D_MODEL), w["ln_g"], w["ln_b"])

    slabs = lambda a: a.reshape(N_CHIPS, IN_W // N_CHIPS, D_MODEL // 2)
    theirs, mine = (slabs(a) for a in _dwt_early(st["dhm"], st["dhg"], st["xb2"], c1, c1))
    parts1 = [theirs, early["w_oa"].astype(BF16), early["w_ob"].astype(BF16),
              early["w_out"].reshape(N_CHIPS, SLAB_W, D_MODEL).astype(BF16)]
    my_loss = lax.dynamic_update_slice(jnp.zeros((N_DEV,) + LOSS_TILE, F32), jnp.broadcast_to(loss, (1,) + LOSS_TILE),
                                       (4 * xi + 2 * yi + c, 0, 0))
    sems0 = _pairs_start(parts1, my_loss, 1)
    half = HEADS // 4
    first_pairs = _local_attn_bwd(st, sems0[5], 0, half, (), "attn_bwd_first")
    parts1, recv1, all_loss = _pairs_wait(*sems0[:5], first_pairs[1], 1)
    pairs1 = _add_pair([mine, *parts1[1:]], recv1, c1, "add_pair_early", n_whole=1)
    sems1 = _chips_start(pairs1, "chips_start_early")
    dq, dk, dv = _local_attn_bwd(st, sems1[4], half, half, tuple(first_pairs), "attn_bwd_rest")
    dhl, late = _local_tail(st, dq, dk, dv, dv)

    grads = {**early, **late}
    rep = jnp.concatenate([_rows8(grads[n]) for n in REPLICATED], axis=0)
    rep = jnp.pad(rep, ((0, N_CHIPS * REP_ROWS - rep.shape[0]), (0, 0))).reshape(N_CHIPS, REP_ROWS, D_MODEL)
    parts2 = [late["w_uq"].reshape(N_CHIPS, Q_RANK // N_CHIPS, HEADS * QK_DIM).astype(BF16), rep.astype(BF16),
              late["w_lat"].reshape(N_CHIPS, LAT_ROWS_PAD // N_CHIPS, D_MODEL)]
    pairs2 = _add_pair(parts2, _exchange_pairs(parts2, "exchange_pairs_late"), c1, "add_pair_late")
    sems2 = _chips_start(pairs2, "chips_start_late")
    dx = _dx(st["dr"], st["dhg"], st["dhm"], dhl, st["wt"], st["wlat"], sems2[4])
    pairs2, landed2 = _chips_wait(*sems2[:4], dx, "chips_wait_late")
    pairs1, landed1 = _chips_wait(*sems1[:4], landed2[0], "chips_wait_early")
    sums = _sum_chips([*pairs1, *pairs2], [*landed1, *landed2], idx, (F32, BF16))
    *shards, g_rep, g_lat = _share(sums[:-2], sums[-2:])
    loss = jnp.sum(all_loss[:, 0, 0])

    red = {n: s.reshape(w[n].shape) for n, s in zip(("w_oa", "w_ob", "w_out", "w_uq"), shards[1:])}
    g_rep = g_rep.reshape(N_CHIPS * REP_ROWS, D_MODEL)
    off = 0
    for n in REPLICATED:
        rows = _rows8(w[n]).shape[0]
        red[n] = g_rep[off:off + rows].reshape(-1)[:w[n].size].reshape(w[n].shape)
        off += rows
    owner = (2 * xi + yi == 0).astype(jnp.int32).reshape(1)
    gt, dt, mt, vt2 = _update_w_in(wt_shard, shards[0], mt_shard, vt_shard,
                                   g_lat.reshape(LAT_ROWS_PAD, D_MODEL).astype(F32), owner, 232)
    red["w_in"] = jnp.transpose(gt)
    small = [n for n in ORDER if n != "w_in"]
    as2d = lambda a: a.reshape(-1, a.shape[-1])
    ds, ms, vs = _update_small([as2d(w[n]) for n in small], [as2d(red[n]) for n in small],
                               [as2d(m[n]) for n in small], [as2d(v[n]) for n in small])
    delta, new_m, new_v = {"w_in": jnp.transpose(dt)}, {"w_in": jnp.transpose(mt)}, {"w_in": jnp.transpose(vt2)}
    for i, n in enumerate(small):
        delta[n], new_m[n], new_v[n] = (a[i].reshape(w[n].shape) for a in (ds, ms, vs))

    lead = lambda a: a[None]
    return (loss, dx[None], *[lead(red[n]) for n in ORDER], *[lead(delta[n]) for n in ORDER],
            *[lead(new_m[n]) for n in ORDER], *[lead(new_v[n]) for n in ORDER])
```

```python
import math

import jax
import jax.numpy as jnp
from jax import lax
from jax.experimental import pallas as pl
from jax.experimental.pallas import tpu as pltpu

F32 = jnp.float32
BF16 = jnp.bfloat16

D_MODEL = 1024
HEADS = 8
Q_RANK = 384
KV_RANK = 128
NOPE = 64
ROPE = 32
V_DIM = 64
QK_DIM = NOPE + ROPE
HEAD_PAD = 128
MLA_W = HEADS * V_DIM
SGU_W = 512
GROUPS = 8
CHUNK = 128
IN_W = 4640
RMS_EPS = 1e-6
LN_EPS = 1e-5
ALPHA = 2.0 ** 0.25
ROPE_THETA = 10000.0
SCALE = QK_DIM ** -0.5

GATE_W = 2 * D_MODEL
MID_W = 4 * SGU_W
LAT_W = Q_RANK + KV_RANK + HEAD_PAD
LAT_COLS = Q_RANK + KV_RANK + ROPE
ROW_GATE = LAT_COLS + MID_W
LAT_ROWS_PAD = 704
N_SLABS = 4
SLAB_W = D_MODEL // N_SLABS

ROW_TILE = 256
MATMUL_ROW_TILE = 512
MID_ROW_TILE = 256
ATT_TK = 256
ATT_BWD_TK = 256
SUM_ROWS = 16
LOG2E = 1.4426950408889634
LN2 = 0.6931471805599453
Q_SCALE = SCALE * LOG2E
VMEM_LIMIT = 56 * 1024 * 1024

ADAM_LR = 0.001
ADAM_B1 = 0.9
ADAM_B2 = 0.999
ADAM_EPS = 1e-08
ADAM_WD = 0.01
ADAM_STEP = 10


def _dot(a, b):
    return jnp.dot(a, b, preferred_element_type=F32)


def _dot_nt(a, b):
    return lax.dot_general(a, b, (((1,), (1,)), ((), ())), preferred_element_type=F32)


def _dot_tn(a, b):
    return lax.dot_general(a, b, (((0,), (0,)), ((), ())), preferred_element_type=F32)


def _sigmoid(z):
    return 0.5 * jnp.tanh(0.5 * z) + 0.5


_GELU_C = math.sqrt(2.0 / math.pi)


def _gelu_and_grad(x):
    x2 = x * x
    t = jnp.tanh(_GELU_C * (x + 0.044715 * x * x2))
    g = 0.5 * x * (1.0 + t)
    dg = 0.5 * (1.0 + t) + 0.5 * x * (1.0 - t * t) * (_GELU_C * (1.0 + 3.0 * 0.044715 * x2))
    return g, dg


def _silu_and_grad(z):
    s = _sigmoid(z)
    return z * s, s * (1.0 + z * (1.0 - s))


def _rope(xb, c, sl, sh):
    return xb * c + pltpu.roll(xb, 112, 1) * sl + pltpu.roll(xb, 16, 1) * sh


def _rope_t(dy, c, sl, sh):
    return dy * c + pltpu.roll(dy * sl, 16, 1) + pltpu.roll(dy * sh, 112, 1)


def _params(sem=("arbitrary",)):
    return pltpu.CompilerParams(dimension_semantics=sem, vmem_limit_bytes=VMEM_LIMIT)


def _row_spec(tile, width):
    return pl.BlockSpec((tile, width), lambda i: (i, 0))


def _full_spec(shape):
    nd = len(shape)
    return pl.BlockSpec(shape, lambda i: (0,) * nd)


def _kpe_rows(wt_ref):
    z = lambda n: jnp.zeros((n, D_MODEL), BF16)
    return jnp.concatenate([z(NOPE), wt_ref[Q_RANK + KV_RANK:LAT_COLS, :], z(HEAD_PAD - QK_DIM)], axis=0)


def _fwd_rest(xb2, wt, b_g, b_m):
    s = xb2.shape[1]
    ts = MATMUL_ROW_TILE
    tn = D_MODEL
    blocks = ([(ROW_GATE + c0, 0, c0) for c0 in range(0, GATE_W, tn)]
              + [(LAT_COLS + c0, 1, c0) for c0 in range(0, MID_W, tn)])

    def body(xb_ref, wt_hbm, bg_ref, bm_ref, hg_ref, hm_ref, wt_ref, sems):
        copies = [pltpu.make_async_copy(wt_hbm.at[lo:lo + tn], wt_ref.at[lo:lo + tn], sems.at[n])
                  for n, (lo, _, _) in enumerate(blocks)]

        def compute(first):
            xb_ = jnp.concatenate([xb_ref[0], xb_ref[1]], axis=1)
            for cp, (lo, which, c0) in zip(copies, blocks):
                if first:
                    cp.wait()
                out_ref, b_ref = ((hg_ref, bg_ref), (hm_ref, bm_ref))[which]
                out_ref[:, c0:c0 + tn] = (_dot_nt(xb_, wt_ref[lo:lo + tn, :]) + b_ref[:, c0:c0 + tn]).astype(BF16)

        @pl.when(pl.program_id(0) == 0)
        def _():
            for cp in copies:
                cp.start()
            compute(True)

        @pl.when(pl.program_id(0) > 0)
        def _():
            compute(False)

    return pl.pallas_call(
        body, name="fwd_rest", grid=(s // ts,),
        in_specs=[pl.BlockSpec((2, ts, D_MODEL // 2), lambda i: (0, i, 0)), HBM_SPEC, _full_spec(b_g.shape),
                  _full_spec(b_m.shape)],
        out_specs=[_row_spec(ts, GATE_W), _row_spec(ts, MID_W)],
        out_shape=[jax.ShapeDtypeStruct((s, GATE_W), BF16), jax.ShapeDtypeStruct((s, MID_W), BF16)],
        scratch_shapes=[pltpu.VMEM(wt.shape, BF16), pltpu.SemaphoreType.DMA((len(blocks),))],
        compiler_params=_params(),
    )(xb2, wt, b_g, b_m)


def _fwd_lat(x, wlat, b_l, g_q, wuq, g_kv, wk, wv, rc, rsl, rsh, after):
    s = x.shape[0]
    ts = ROW_TILE

    def body(x_ref, wt_ref, bl_ref, gq_ref, wuq_ref, gkv_ref, wk_ref, wv_ref, rc_ref, rsl_ref,
             rsh_ref, after_ref, hl_ref, q_ref, k_ref, v_ref, qt_ref, kt_ref, vt_ref, xb2_ref):
        xb = x_ref[...].astype(BF16)
        xb2_ref[0] = xb[:, :D_MODEL // 2]
        xb2_ref[1] = xb[:, D_MODEL // 2:]
        hl = jnp.concatenate([_dot_nt(xb, wt_ref[0:Q_RANK + KV_RANK, :]), _dot_nt(xb, _kpe_rows(wt_ref))],
                             axis=1) + bl_ref[...]
        hl_ref[...] = hl
        c, sl, sh = rc_ref[...], rsl_ref[...], rsh_ref[...]
        cq = hl[:, :Q_RANK]
        cqn = cq * lax.rsqrt(jnp.mean(cq * cq, axis=-1, keepdims=True) + RMS_EPS) * gq_ref[...]
        q = _dot(cqn.astype(BF16), wuq_ref[...])
        ckv = hl[:, Q_RANK:Q_RANK + KV_RANK]
        ckvn = (ckv * lax.rsqrt(jnp.mean(ckv * ckv, axis=-1, keepdims=True) + RMS_EPS) * gkv_ref[...]).astype(BF16)
        k = _dot(ckvn, wk_ref[...])
        vb = _dot(ckvn, wv_ref[...]).astype(BF16)
        v_ref[...] = vb
        vt_ref[...] = vb.T
        kpe = _rope(hl[:, Q_RANK + KV_RANK:], c, sl, sh)
        for hd in range(HEADS):
            lanes = slice(hd * HEAD_PAD, (hd + 1) * HEAD_PAD)
            qb = (_rope(q[:, lanes], c, sl, sh) * Q_SCALE).astype(BF16)
            kb = (k[:, lanes] + kpe).astype(BF16)
            q_ref[:, lanes] = qb
            k_ref[:, lanes] = kb
            qt_ref[lanes, :] = qb.T
            kt_ref[lanes, :] = kb.T

    qk_w = HEADS * HEAD_PAD
    col_spec = lambda rows: pl.BlockSpec((rows, ts), lambda i: (0, i))
    return pl.pallas_call(
        body, name="fwd_lat", grid=(s // ts,),
        in_specs=[_row_spec(ts, D_MODEL), _full_spec(wlat.shape),
                  _full_spec(b_l.shape), _full_spec(g_q.shape),
                  _full_spec(wuq.shape), _full_spec(g_kv.shape), _full_spec(wk.shape), _full_spec(wv.shape),
                  _row_spec(ts, HEAD_PAD), _row_spec(ts, HEAD_PAD), _row_spec(ts, HEAD_PAD),
                  pl.BlockSpec(memory_space=pl.ANY)],
        out_specs=[_row_spec(ts, LAT_W), _row_spec(ts, qk_w),
                   _row_spec(ts, qk_w), _row_spec(ts, MLA_W), col_spec(qk_w), col_spec(qk_w),
                   col_spec(MLA_W), pl.BlockSpec((2, ts, D_MODEL // 2), lambda i: (0, i, 0))],
        out_shape=[jax.ShapeDtypeStruct((s, LAT_W), F32), jax.ShapeDtypeStruct((s, qk_w), BF16),
                   jax.ShapeDtypeStruct((s, qk_w), BF16), jax.ShapeDtypeStruct((s, MLA_W), BF16),
                   jax.ShapeDtypeStruct((qk_w, s), BF16),
                   jax.ShapeDtypeStruct((qk_w, s), BF16), jax.ShapeDtypeStruct((MLA_W, s), BF16),
                   jax.ShapeDtypeStruct((2, s, D_MODEL // 2), BF16)],
        compiler_params=_params(),
    )(x, wlat, b_l, g_q, wuq, g_kv, wk, wv, rc, rsl, rsh, after)


def _attn_fwd(qt, k, vt):
    s = k.shape[0]
    tk = ATT_TK
    nk = s // tk
    pairs = HEADS // 2

    def body(qt_ref, k_ref, vt_ref, o_ref, lse_ref):
        qts = [qt_ref[hh * HEAD_PAD:(hh + 1) * HEAD_PAD, :] for hh in range(2)]
        ones = jnp.ones((SUM_ROWS, tk), BF16)

        def scores(j):
            return tuple(_dot(k_ref[j * tk:(j + 1) * tk, hh * HEAD_PAD:(hh + 1) * HEAD_PAD], qts[hh][:, j * tk:])
                         for hh in range(2))

        def weighted(j, ps):
            return tuple(_dot(jnp.concatenate([vt_ref[hh * V_DIM:(hh + 1) * V_DIM, j * tk:(j + 1) * tk], ones], axis=0),
                              ps[hh]) for hh in range(2))

        def from_lane(full, lo, part):
            return part if lo == 0 else jnp.concatenate([full[:, :lo], part], axis=1)

        krow = lax.broadcasted_iota(jnp.int32, (tk, tk), 0)
        qcol = lax.broadcasted_iota(jnp.int32, (tk, tk), 1)
        st = scores(0)
        ps = None
        stats = [(jnp.full((1, s), -jnp.inf, F32), jnp.zeros((V_DIM + SUM_ROWS, s), F32))] * 2
        for j in range(nk):
            lo, lo_prev = j * tk, max(j - 1, 0) * tk
            st_next = scores(j + 1) if j + 1 < nk else None
            pvs = weighted(j - 1, ps) if j else None
            new_ps, new_stats = [], []
            for hh in range(2):
                m, acc = stats[hh]
                diag = jnp.where(krow <= qcol, st[hh][:, :tk], -jnp.inf)
                s_ = diag if j == nk - 1 else jnp.concatenate([diag, st[hh][:, tk:]], axis=1)
                if j:
                    acc = from_lane(acc, lo_prev, acc[:, lo_prev:] + pvs[hh])
                m_old = m[:, lo:]
                m_new = jnp.maximum(m_old, jnp.max(s_, axis=0, keepdims=True))
                a = jnp.exp2(m_old - m_new)
                p = jnp.exp2(s_ - m_new)
                new_stats.append((from_lane(m, lo, m_new), from_lane(acc, lo, a * acc[:, lo:])))
                new_ps.append(p.astype(BF16))
            st, ps, stats = st_next, new_ps, new_stats
        pvs = weighted(nk - 1, ps)
        lo = (nk - 1) * tk
        accs = [from_lane(stats[hh][1], lo, stats[hh][1][:, lo:] + pvs[hh]) for hh in range(2)]
        sums = [acc[V_DIM:V_DIM + 1, :] for acc in accs]
        ot = jnp.concatenate([accs[hh][:V_DIM, :] / sums[hh] for hh in range(2)], axis=0)
        o_ref[...] = ot.T
        lse = [stats[hh][0] + jnp.log(sums[hh]) * LOG2E for hh in range(2)]
        lse_ref[...] = jnp.concatenate(lse + [jnp.zeros((6, s), F32)], axis=0)

    return pl.pallas_call(
        body, name="attn_fwd", grid=(pairs,),
        in_specs=[pl.BlockSpec((2 * HEAD_PAD, s), lambda p: (p, 0)),
                  pl.BlockSpec((s, 2 * HEAD_PAD), lambda p: (0, p)),
                  pl.BlockSpec((2 * V_DIM, s), lambda p: (p, 0))],
        out_specs=[pl.BlockSpec((s, 2 * V_DIM), lambda p: (0, p)),
                   pl.BlockSpec((None, 8, s), lambda p: (p, 0, 0))],
        out_shape=[jax.ShapeDtypeStruct((s, MLA_W), F32), jax.ShapeDtypeStruct((pairs, 8, s), F32)],
        compiler_params=_params(("arbitrary",)),
    )(qt, k, vt)


def _attn_bwd(q, qt, k, kt, v, do, dot, lse, delta, after, first_pair, n_pairs, into, name):
    s = k.shape[0]
    tk = ATT_BWD_TK
    nk = s // tk

    def body(q_ref, qt_ref, k_ref, kt_ref, v_ref, do_ref, dot_ref, lse_ref, dl_ref, after_ref, *rest):
        dqt_ref, dk_ref, dv_ref = rest[-3:]
        krow = lax.broadcasted_iota(jnp.int32, (tk, tk), 0)
        qcol = lax.broadcasted_iota(jnp.int32, (tk, tk), 1)
        lane = lax.broadcasted_iota(jnp.int32, (tk, 2 * V_DIM), 1)
        drow = lax.broadcasted_iota(jnp.int32, (2 * V_DIM, s), 0)
        dotb = dot_ref[...]
        dots = [jnp.where((drow < V_DIM) if hh == 0 else (drow >= V_DIM), dotb, jnp.zeros_like(dotb))
                for hh in range(2)]
        for j in range(nk):
            lo = j * tk
            vb = v_ref[lo:lo + tk, :]
            dob = do_ref[lo:, :]
            dvs = []
            for hh in range(2):
                rows = slice(hh * HEAD_PAD, (hh + 1) * HEAD_PAD)
                st = _dot(k_ref[lo:lo + tk, rows], qt_ref[rows, lo:])
                diag = jnp.where(krow <= qcol, st[:, :tk], -jnp.inf)
                st = diag if j == nk - 1 else jnp.concatenate([diag, st[:, tk:]], axis=1)
                p = jnp.exp2(st - lse_ref[hh:hh + 1, lo:])
                dpt = _dot(vb, dots[hh][:, lo:])
                dst = (p * (dpt - dl_ref[hh:hh + 1, lo:])).astype(BF16)
                dvs.append(_dot(p.astype(BF16), dob))
                dk_ref[lo:lo + tk, rows] = (_dot(dst, q_ref[lo:, rows]) * LN2).astype(BF16)
                dqt = _dot(kt_ref[rows, lo:lo + tk], dst)
                if j == 0:
                    dqt_ref[rows, :] = dqt
                else:
                    dqt_ref[rows, lo:] += dqt
            dv_ref[lo:lo + tk, :] = jnp.where(lane < V_DIM, dvs[0], dvs[1]).astype(BF16)
        dqt_ref[...] = dqt_ref[...] * SCALE

    pair_rows = lambda w: pl.BlockSpec((s, w), lambda p: (0, p + first_pair))
    pair_cols = lambda w: pl.BlockSpec((w, s), lambda p: (p + first_pair, 0))
    stats = pl.BlockSpec((None, 8, s), lambda p: (p + first_pair, 0, 0))
    any_spec = pl.BlockSpec(memory_space=pl.ANY)
    return pl.pallas_call(
        body, name=name, grid=(n_pairs,),
        in_specs=[pair_rows(2 * HEAD_PAD), pair_cols(2 * HEAD_PAD), pair_rows(2 * HEAD_PAD), pair_cols(2 * HEAD_PAD),
                  pair_rows(2 * V_DIM), pair_rows(2 * V_DIM), pair_cols(2 * V_DIM), stats, stats, any_spec]
        + [any_spec] * len(into),
        out_specs=[pair_cols(2 * HEAD_PAD), pair_rows(2 * HEAD_PAD), pair_rows(2 * V_DIM)],
        out_shape=[jax.ShapeDtypeStruct((HEADS * HEAD_PAD, s), F32), jax.ShapeDtypeStruct((s, HEADS * HEAD_PAD), BF16),
                   jax.ShapeDtypeStruct((s, MLA_W), BF16)],
        input_output_aliases={10 + n: n for n in range(len(into))},
        compiler_params=_params(("arbitrary",)),
    )(q, qt, k, kt, v, do, dot, lse, delta, after, *into)


def _split3(a):
    hi = a.astype(BF16)
    r1 = a - hi.astype(F32)
    mid = r1.astype(BF16)
    lo = (r1 - mid.astype(F32)).astype(BF16)
    return hi, mid, lo


def _mid(x, tgt, o, hm, hg, woa, wob, wout, ln_g, ln_b, sg_g, sg_b, w_s, bsb):
    s = x.shape[0]
    ts = MID_ROW_TILE
    nsteps = s // ts
    nch = ts // CHUNK
    npair = GROUPS // 2

    def body(x_ref, t_ref, o_ref, hm_ref, hg_ref, woa_ref, wob_ref, wout_ref, lng_ref, lnb_ref, sgg_ref, sgb_ref,
             ws_ref, bsb_ref,
             dr_ref, dhg_ref, dhm_ref, do_ref, dot_ref, dl_ref,
             dwout_ref, dwoa_ref, dwob_ref, dws_ref, dbs_ref, dlng_ref, dlnb_ref, dsgg_ref, dsgb_ref, loss_ref,
             dbg_ref, dbm_ref, dbacc_ref, awout_ref, awoa_ref, awob_ref):
        i = pl.program_id(0)

        @pl.when(i == 0)
        def _():
            for r in (awout_ref, awoa_ref, awob_ref, dws_ref, dlng_ref, dlnb_ref, dsgg_ref, dsgb_ref, loss_ref,
                      dbg_ref, dbm_ref, dbacc_ref):
                r[...] = jnp.zeros_like(r)

        def emit(ref, bref, lo, val):
            n = val.shape[1]
            ref[:, lo:lo + n] = val.astype(BF16)
            bref[:, lo:lo + n] += jnp.sum(val, axis=0, keepdims=True)

        lane = lax.broadcasted_iota(jnp.int32, (CHUNK, CHUNK), 1)
        left = lane < V_DIM
        tril = lax.broadcasted_iota(jnp.int32, (CHUNK, CHUNK), 0) >= lane
        ms = [jnp.where(tril, ws_ref[g], 0.0).astype(BF16) for g in range(GROUPS)]

        z_a = hm_ref[:, 0:SGU_W].astype(F32)
        u = hm_ref[:, SGU_W:2 * SGU_W].astype(F32)
        v = hm_ref[:, 2 * SGU_W:3 * SGU_W].astype(F32)
        z_b = hm_ref[:, 3 * SGU_W:4 * SGU_W].astype(F32)
        o = o_ref[...]
        sa, dsa = _silu_and_grad(z_a)
        y_a = (o * sa).astype(BF16)
        gu, dgu = _gelu_and_grad(u)
        gv, dgv = _gelu_and_grad(v)
        mu = jnp.mean(gv, axis=-1, keepdims=True)
        vc = gv - mu
        rstd_v = lax.rsqrt(jnp.mean(vc * vc, axis=-1, keepdims=True) + LN_EPS)
        vhat = vc * rstd_v
        vn = (vhat * sgg_ref[...] + sgb_ref[...]).astype(BF16)
        rows = []
        for c in range(nch):
            blocks = []
            for p in range(npair):
                blk = vn[c * CHUNK:(c + 1) * CHUNK, p * CHUNK:(p + 1) * CHUNK]
                blocks.append(jnp.where(left, _dot(ms[2 * p], blk), _dot(ms[2 * p + 1], blk)))
            rows.append(jnp.concatenate(blocks, axis=1) + bsb_ref[...])
        mixed = jnp.concatenate(rows, axis=0)
        sgu = gu * mixed
        sb, dsb = _silu_and_grad(z_b)
        y_b = (sgu * sb).astype(BF16)
        pa = jnp.concatenate([_dot(y_a, woa_ref[k]) for k in range(N_SLABS)], axis=1)
        pb = jnp.concatenate([_dot(y_b, wob_ref[k]) for k in range(N_SLABS)], axis=1)
        sga = _sigmoid(hg_ref[:, :D_MODEL].astype(F32))
        sgb = _sigmoid(hg_ref[:, D_MODEL:].astype(F32))
        m2 = (sga * pa + sgb * pb).astype(BF16)
        r = ALPHA * x_ref[...] + _dot(m2, wout_ref[...])
        rmu = jnp.mean(r, axis=-1, keepdims=True)
        rc = r - rmu
        rstd = lax.rsqrt(jnp.mean(rc * rc, axis=-1, keepdims=True) + LN_EPS)
        xhat = rc * rstd
        y = xhat * lng_ref[...] + lnb_ref[...]
        err = y - t_ref[...]
        loss_ref[...] += jnp.full(loss_ref.shape, 0.5 / D_MODEL, F32) * jnp.sum(err * err)

        dy = err * (1.0 / D_MODEL)
        dlng_ref[...] += jnp.sum(dy * xhat, axis=0, keepdims=True)
        dlnb_ref[...] += jnp.sum(dy, axis=0, keepdims=True)
        dxh = dy * lng_ref[...]
        dr = rstd * (dxh - jnp.mean(dxh, axis=-1, keepdims=True) - xhat * jnp.mean(dxh * xhat, axis=-1, keepdims=True))
        dr_ref[...] = dr
        drb = dr.astype(BF16)
        awout_ref[...] += _dot_tn(m2, drb)
        dm2 = _dot_nt(drb, wout_ref[...])
        emit(dhg_ref, dbg_ref,0, dm2 * pa * sga * (1.0 - sga))
        emit(dhg_ref, dbg_ref,D_MODEL, dm2 * pb * sgb * (1.0 - sgb))
        dpa = (dm2 * sga).astype(BF16)
        dpb = (dm2 * sgb).astype(BF16)
        dy_a = jnp.zeros((ts, MLA_W), F32)
        dy_b = jnp.zeros((ts, SGU_W), F32)
        y_at, y_bt = y_a.T, y_b.T
        for k in range(N_SLABS):
            cols = slice(k * SLAB_W, (k + 1) * SLAB_W)
            awoa_ref[k] += _dot(y_at, dpa[:, cols])
            awob_ref[k] += _dot(y_bt, dpb[:, cols])
            dy_a = dy_a + _dot_nt(dpa[:, cols], woa_ref[k])
            dy_b = dy_b + _dot_nt(dpb[:, cols], wob_ref[k])
        dob = (dy_a * sa).astype(BF16)
        do_ref[...] = dob
        dot_ref[...] = dob.T
        head = (lax.broadcasted_iota(jnp.int32, (HEADS, MLA_W), 1) // V_DIM
                == lax.broadcasted_iota(jnp.int32, (HEADS, MLA_W), 0)).astype(BF16)
        dl = sum(_dot_nt(head, term) for term in _split3(dob.astype(F32) * o))
        for p in range(HEADS // 2):
            dl_ref[p] = jnp.concatenate([dl[2 * p:2 * p + 2], jnp.zeros((6, ts), F32)], axis=0)
        emit(dhm_ref, dbm_ref,0, dy_a * o * dsa)
        dsg = dy_b * sb
        emit(dhm_ref, dbm_ref,3 * SGU_W, dy_b * sgu * dsb)
        emit(dhm_ref, dbm_ref,SGU_W, dsg * mixed * dgu)
        dmixed = dsg * gu
        dvn_rows = []
        dbs_sum = jnp.zeros((CHUNK, SGU_W), F32)
        for c in range(nch):
            dm_c = dmixed[c * CHUNK:(c + 1) * CHUNK, :]
            dbs_sum = dbs_sum + dm_c
            blocks = []
            for p in range(npair):
                dmb = dm_c[:, p * CHUNK:(p + 1) * CHUNK].astype(BF16)
                blk = vn[c * CHUNK:(c + 1) * CHUNK, p * CHUNK:(p + 1) * CHUNK]
                blocks.append(jnp.where(left, _dot_tn(ms[2 * p], dmb), _dot_tn(ms[2 * p + 1], dmb)))
                zero = jnp.zeros_like(dmb)
                dws_ref[2 * p] += jnp.where(tril, _dot_nt(jnp.where(left, dmb, zero), blk), 0.0)
                dws_ref[2 * p + 1] += jnp.where(tril, _dot_nt(jnp.where(left, zero, dmb), blk), 0.0)
            dvn_rows.append(jnp.concatenate(blocks, axis=1))
        dbacc_ref[...] += dbs_sum
        dvn = jnp.concatenate(dvn_rows, axis=0)
        dsgg_ref[...] += jnp.sum(dvn * vhat, axis=0, keepdims=True)
        dsgb_ref[...] += jnp.sum(dvn, axis=0, keepdims=True)
        dvh = dvn * sgg_ref[...]
        dgv_in = rstd_v * (dvh - jnp.mean(dvh, axis=-1, keepdims=True)
                           - vhat * jnp.mean(dvh * vhat, axis=-1, keepdims=True))
        emit(dhm_ref, dbm_ref,2 * SGU_W, dgv_in * dgv)

        @pl.when(i == nsteps - 1)
        def _():
            dwout_ref[...] = awout_ref[...].astype(BF16)
            dwoa_ref[...] = awoa_ref[...].astype(BF16)
            dwob_ref[...] = awob_ref[...].astype(BF16)
            grp = (lax.broadcasted_iota(jnp.int32, (SGU_W, CHUNK), 0) // V_DIM
                   == lax.broadcasted_iota(jnp.int32, (SGU_W, CHUNK), 1)).astype(BF16)
            hi, mid, lo = _split3(dbacc_ref[...])
            dbs_ref[...] = _dot(hi, grp) + _dot(mid, grp) + _dot(lo, grp)

    acc_shapes = [(D_MODEL, D_MODEL), woa.shape, wob.shape, (GROUPS, CHUNK, CHUNK), (CHUNK, CHUNK),
                  (1, D_MODEL), (1, D_MODEL), (1, SGU_W), (1, SGU_W), (1, 128), (1, GATE_W), (1, MID_W)]
    col_spec = lambda rows: pl.BlockSpec((rows, ts), lambda i: (0, i))
    return pl.pallas_call(
        body, name="mid", grid=(nsteps,),
        in_specs=[_row_spec(ts, D_MODEL), _row_spec(ts, D_MODEL), _row_spec(ts, MLA_W), _row_spec(ts, MID_W),
                  _row_spec(ts, GATE_W), _full_spec(woa.shape), _full_spec(wob.shape), _full_spec(wout.shape),
                  _full_spec(ln_g.shape), _full_spec(ln_b.shape), _full_spec(sg_g.shape), _full_spec(sg_b.shape),
                  _full_spec(w_s.shape), _full_spec(bsb.shape)],
        out_specs=[_row_spec(ts, D_MODEL), _row_spec(ts, GATE_W), _row_spec(ts, MID_W), _row_spec(ts, MLA_W),
                   col_spec(MLA_W), pl.BlockSpec((HEADS // 2, 8, ts), lambda i: (0, 0, i))]
        + [_full_spec(sh) for sh in acc_shapes],
        out_shape=[jax.ShapeDtypeStruct((s, D_MODEL), F32), jax.ShapeDtypeStruct((s, GATE_W), BF16),
                   jax.ShapeDtypeStruct((s, MID_W), BF16), jax.ShapeDtypeStruct((s, MLA_W), BF16),
                   jax.ShapeDtypeStruct((MLA_W, s), BF16), jax.ShapeDtypeStruct((HEADS // 2, 8, s), F32)]
        + [jax.ShapeDtypeStruct(sh, BF16 if n < 3 else F32) for n, sh in enumerate(acc_shapes)],
        scratch_shapes=[pltpu.VMEM((CHUNK, SGU_W), F32)] + [pltpu.VMEM(sh, F32) for sh in acc_shapes[:3]],
        compiler_params=_params(),
    )(x, tgt, o, hm, hg, woa, wob, wout, ln_g, ln_b, sg_g, sg_b, w_s, bsb)


def _lat_bwd(dq, dk, dv, hl, rc, rsl, rsh, g_q, g_kv, wuq, wk, wv, after):
    s = dk.shape[0]
    ts = ROW_TILE
    qk_w = HEADS * HEAD_PAD

    def body(dq_ref, dk_ref, dv_ref, hl_ref, rc_ref, rsl_ref, rsh_ref, gq_ref, gkv_ref, wuq_ref, wk_ref, wv_ref,
             after_ref, dhl_ref, dwuq_ref, dwk_ref, dwv_ref, dgq_ref, dgkv_ref, dbl_ref):
        i = pl.program_id(0)

        @pl.when(i == 0)
        def _():
            for r in (dwuq_ref, dwk_ref, dwv_ref, dgq_ref, dgkv_ref, dbl_ref):
                r[...] = jnp.zeros_like(r)

        def emit(lo, val):
            n = val.shape[1]
            dhl_ref[:, lo:lo + n] = val.astype(BF16)
            dbl_ref[:, lo:lo + n] += jnp.sum(val, axis=0, keepdims=True)

        c, sl, sh = rc_ref[...], rsl_ref[...], rsh_ref[...]
        lane = lax.broadcasted_iota(jnp.int32, (ts, HEAD_PAD), 1)
        pe = (lane >= NOPE) & (lane < QK_DIM)
        dkpe = jnp.zeros((ts, HEAD_PAD), F32)
        dqu = []
        for hd in range(HEADS):
            lanes = slice(hd * HEAD_PAD, (hd + 1) * HEAD_PAD)
            dqu.append(_rope_t(dq_ref[lanes, :].T, c, sl, sh).astype(BF16))
            dkpe = dkpe + dk_ref[:, lanes]
        dqu = jnp.concatenate(dqu, axis=1)
        dkpe = _rope_t(jnp.where(pe, dkpe, 0.0), c, sl, sh)

        cq = hl_ref[:, :Q_RANK]
        rq = lax.rsqrt(jnp.mean(cq * cq, axis=-1, keepdims=True) + RMS_EPS)
        cqh = cq * rq
        cqn = (cqh * gq_ref[...]).astype(BF16)
        dwuq_ref[...] += _dot_tn(cqn, dqu)
        dcqn = _dot_nt(dqu, wuq_ref[...])
        dgq_ref[...] += jnp.sum(dcqn * cqh, axis=0, keepdims=True)
        dch = dcqn * gq_ref[...]
        emit(0, rq * (dch - cqh * jnp.mean(dch * cqh, axis=-1, keepdims=True)))

        ckv = hl_ref[:, Q_RANK:Q_RANK + KV_RANK]
        rk = lax.rsqrt(jnp.mean(ckv * ckv, axis=-1, keepdims=True) + RMS_EPS)
        ckh = ckv * rk
        ckn = (ckh * gkv_ref[...]).astype(BF16)
        dkb = dk_ref[...].astype(BF16)
        dvb = dv_ref[...].astype(BF16)
        dwk_ref[...] += _dot_tn(ckn, dkb)
        dwv_ref[...] += _dot_tn(ckn, dvb)
        dckn = _dot_nt(dkb, wk_ref[...]) + _dot_nt(dvb, wv_ref[...])
        dgkv_ref[...] += jnp.sum(dckn * ckh, axis=0, keepdims=True)
        dkh = dckn * gkv_ref[...]
        emit(Q_RANK, rk * (dkh - ckh * jnp.mean(dkh * ckh, axis=-1, keepdims=True)))
        emit(Q_RANK + KV_RANK, dkpe)

    acc_shapes = [wuq.shape, wk.shape, wv.shape, g_q.shape, g_kv.shape, (1, LAT_W)]
    return pl.pallas_call(
        body, name="lat_bwd", grid=(s // ts,),
        in_specs=[pl.BlockSpec((qk_w, ts), lambda i: (0, i)), _row_spec(ts, qk_w), _row_spec(ts, MLA_W),
                  _row_spec(ts, LAT_W), _row_spec(ts, HEAD_PAD), _row_spec(ts, HEAD_PAD), _row_spec(ts, HEAD_PAD),
                  _full_spec(g_q.shape), _full_spec(g_kv.shape), _full_spec(wuq.shape), _full_spec(wk.shape),
                  _full_spec(wv.shape), pl.BlockSpec(memory_space=pl.ANY)],
        out_specs=[_row_spec(ts, LAT_W)] + [_full_spec(sh) for sh in acc_shapes],
        out_shape=[jax.ShapeDtypeStruct((s, LAT_W), BF16)] + [jax.ShapeDtypeStruct(sh, F32) for sh in acc_shapes],
        compiler_params=_params(),
    )(dq, dk, dv, hl, rc, rsl, rsh, g_q, g_kv, wuq, wk, wv, after)


def _dx(dr, dhg, dhm, dhl, wt, wlat, after):
    s = dr.shape[0]
    ts = MATMUL_ROW_TILE

    tk = D_MODEL
    bounds = ([(ROW_GATE + r0, ROW_GATE + r0 + tk) for r0 in range(0, GATE_W, tk)]
              + [(LAT_COLS + r0, LAT_COLS + r0 + tk) for r0 in range(0, MID_W, tk)])

    def body(dr_ref, dhg_ref, dhm_ref, dhl_ref, wt_hbm, wlat_ref, after_ref, dx_ref, wt_ref, sems):
        copies = [pltpu.make_async_copy(wt_hbm.at[lo:hi], wt_ref.at[lo:hi], sems.at[n])
                  for n, (lo, hi) in enumerate(bounds)]

        def compute(first):
            acc = (ALPHA * dr_ref[...] + _dot(dhl_ref[:, 0:Q_RANK + KV_RANK], wlat_ref[0:Q_RANK + KV_RANK, :])
                   + _dot(dhl_ref[:, Q_RANK + KV_RANK:], _kpe_rows(wlat_ref)))
            for n, (lo, hi) in enumerate(bounds):
                if first:
                    copies[n].wait()
                if lo >= ROW_GATE:
                    acc += _dot(dhg_ref[:, lo - ROW_GATE:hi - ROW_GATE], wt_ref[lo:hi, :])
                else:
                    acc += _dot(dhm_ref[:, lo - LAT_COLS:hi - LAT_COLS], wt_ref[lo:hi, :])
            dx_ref[...] = acc

        @pl.when(pl.program_id(0) == 0)
        def _():
            for cp in copies:
                cp.start()
            compute(True)

        @pl.when(pl.program_id(0) > 0)
        def _():
            compute(False)

    return pl.pallas_call(
        body, name="dx", grid=(s // ts,),
        in_specs=[_row_spec(ts, D_MODEL), _row_spec(ts, GATE_W), _row_spec(ts, MID_W), _row_spec(ts, LAT_W),
                  HBM_SPEC, _full_spec(wlat.shape), HBM_SPEC],
        out_specs=_row_spec(ts, D_MODEL),
        out_shape=jax.ShapeDtypeStruct((s, D_MODEL), F32),
        scratch_shapes=[pltpu.VMEM(wt.shape, BF16), pltpu.SemaphoreType.DMA((len(bounds),))],
        compiler_params=_params(),
    )(dr, dhg, dhm, dhl, wt, wlat, after)


def _dwt_early(dhmt, dhgt, xb2, col, after):
    tn = 512
    nm, ng = MID_W // tn, GATE_W // tn
    s = dhmt.shape[0]
    hc = D_MODEL // 2

    ks = s // 2

    def body(col_ref, dma_ref, dmb_ref, dga_ref, dgb_ref, xoa_ref, xob_ref, xca_ref, xcb_ref, after_ref,
             dwo_ref, dwc_ref):
        i = pl.program_id(0)

        def both(da_ref, db_ref):
            da, db = da_ref[...], db_ref[...]
            dwo_ref[...] = (_dot_tn(da, xoa_ref[...]) + _dot_tn(db, xob_ref[...])).astype(BF16)
            dwc_ref[...] = (_dot_tn(da, xca_ref[...]) + _dot_tn(db, xcb_ref[...])).astype(BF16)

        @pl.when(i < nm)
        def _():
            both(dma_ref, dmb_ref)

        @pl.when(i >= nm)
        def _():
            both(dga_ref, dgb_ref)

    def dh_spec(first, part):
        if first:
            return pl.BlockSpec((ks, tn), lambda i, col_ref: (part, jnp.minimum(i, nm - 1)))
        return pl.BlockSpec((ks, tn), lambda i, col_ref: (part, jnp.maximum(i - nm, 0)))

    def x_spec(other, part):
        if other:
            return pl.BlockSpec((None, ks, hc), lambda i, col_ref: (1 - col_ref[0], part, 0))
        return pl.BlockSpec((None, ks, hc), lambda i, col_ref: (col_ref[0], part, 0))

    out_spec = pl.BlockSpec((pl.Element(tn), pl.Element(hc)),
                            lambda i, col_ref: (pl.multiple_of(LAT_COLS + i * tn, 32), 0))
    rows = pl.pallas_call(
        body, name="dwt_early",
        grid_spec=pltpu.PrefetchScalarGridSpec(
            num_scalar_prefetch=1, grid=(nm + ng,),
            in_specs=[dh_spec(True, 0), dh_spec(True, 1), dh_spec(False, 0), dh_spec(False, 1),
                      x_spec(True, 0), x_spec(True, 1), x_spec(False, 0), x_spec(False, 1),
                      pl.BlockSpec(memory_space=pl.ANY)],
            out_specs=[out_spec, out_spec]),
        out_shape=[jax.ShapeDtypeStruct((IN_W, hc), BF16)] * 2,
        compiler_params=_params(),
    )(col, dhmt, dhmt, dhgt, dhgt, xb2, xb2, xb2, xb2, after)

    def zero(other_ref, mine_ref, other_out, mine_out):
        other_out[...] = jnp.zeros_like(other_out)
        mine_out[...] = jnp.zeros_like(mine_out)

    lat_rows = pl.BlockSpec((LAT_COLS, hc), lambda i: (0, 0))
    return pl.pallas_call(
        zero, name="dwt_early_zero_lat", grid=(1,), in_specs=[pl.BlockSpec(memory_space=pl.ANY)] * 2,
        out_specs=[lat_rows, lat_rows],
        out_shape=[jax.ShapeDtypeStruct((IN_W, hc), BF16)] * 2, input_output_aliases={0: 0, 1: 1},
    )(*rows)


def _dwt_lat(dhl, xb2):
    s, n = dhl.shape
    tk = MATMUL_ROW_TILE
    nsteps = s // tk

    def body(dh_ref, xb_ref, dw_ref, acc_ref):
        i = pl.program_id(0)

        @pl.when(i == 0)
        def _():
            acc_ref[...] = jnp.zeros_like(acc_ref)

        dh = dh_ref[...]
        acc_ref[...] += jnp.concatenate([_dot_tn(dh, xb_ref[0]), _dot_tn(dh, xb_ref[1])], axis=1)

        @pl.when(i == nsteps - 1)
        def _():
            kpe = Q_RANK + KV_RANK + NOPE
            dw_ref[0:Q_RANK + KV_RANK, :] = acc_ref[0:Q_RANK + KV_RANK, :].astype(BF16)
            dw_ref[Q_RANK + KV_RANK:LAT_COLS, :] = acc_ref[kpe:kpe + ROPE, :].astype(BF16)
            dw_ref[LAT_COLS:, :] = jnp.zeros((LAT_ROWS_PAD - LAT_COLS, D_MODEL), BF16)

    return pl.pallas_call(
        body, name="dwt_lat", grid=(nsteps,),
        in_specs=[pl.BlockSpec((tk, n), lambda i: (i, 0)), pl.BlockSpec((2, tk, D_MODEL // 2), lambda i: (0, i, 0))],
        out_specs=_full_spec((LAT_ROWS_PAD, D_MODEL)),
        out_shape=jax.ShapeDtypeStruct((LAT_ROWS_PAD, D_MODEL), BF16),
        scratch_shapes=[pltpu.VMEM((n, D_MODEL), F32)],
        compiler_params=_params(),
    )(dhl, xb2)


def _split_bias(b):
    z = lambda n: jnp.zeros((n,), b.dtype)
    lat = jnp.concatenate([b[:Q_RANK + KV_RANK], z(NOPE), b[Q_RANK + KV_RANK:LAT_COLS], z(HEAD_PAD - QK_DIM)])
    return b[None, ROW_GATE:], b[None, LAT_COLS:ROW_GATE], lat[None, :]


def _join_bias(g, m, l):
    kpe = Q_RANK + KV_RANK + NOPE
    return jnp.concatenate([l[0, :Q_RANK + KV_RANK], l[0, kpe:kpe + ROPE], m[0], g[0]])


def _rope_tables(positions):
    half = ROPE // 2
    inv_freq = ROPE_THETA ** (-jnp.arange(0, ROPE, 2, dtype=F32) / ROPE)
    ang = positions.astype(F32)[:, None] * inv_freq
    cos, sin = jnp.cos(ang), jnp.sin(ang)
    n = positions.shape[0]
    one, zero = jnp.ones((n, NOPE), F32), jnp.zeros((n, half), F32)
    tail1, tail0 = jnp.ones((n, HEAD_PAD - QK_DIM), F32), jnp.zeros((n, HEAD_PAD - QK_DIM), F32)
    z64 = jnp.zeros((n, NOPE), F32)
    rc = jnp.concatenate([one, cos, cos, tail1], axis=1)
    rsl = jnp.concatenate([z64, -sin, zero, tail0], axis=1)
    rsh = jnp.concatenate([z64, zero, sin, tail0], axis=1)
    return rc, rsl, rsh


def _pad_heads(w_uq):
    return jnp.pad(w_uq, ((0, 0), (0, 0), (0, HEAD_PAD - QK_DIM))).reshape(w_uq.shape[0], HEADS * HEAD_PAD)


def _prep_local(b_in, g_q, g_kv, w_ukv):
    b_g, b_m, b_l = _split_bias(b_in)
    wk = jnp.pad(w_ukv[:, :, :NOPE], ((0, 0), (0, 0), (0, HEAD_PAD - NOPE))).reshape(KV_RANK, HEADS * HEAD_PAD).astype(BF16)
    wv = w_ukv[:, :, NOPE:].reshape(KV_RANK, MLA_W).astype(BF16)
    return dict(b_g=b_g, b_m=b_m, b_l=b_l, wk=wk, wv=wv, gq2=g_q[None, :], gkv2=g_kv[None, :])


def _local_attention(x, tables, wlat, prep, wuq, after):
    rc, rsl, rsh = tables
    b_g, b_m, b_l, wk, wv, gq2, gkv2 = (prep[n] for n in ("b_g", "b_m", "b_l", "wk", "wv", "gq2", "gkv2"))
    hl, q, k, v, qt, kt, vt, xb2 = _fwd_lat(x, wlat, b_l, gq2, wuq, gkv2, wk, wv, rc, rsl, rsh, after)
    o, lse = _attn_fwd(qt, k, vt)
    return dict(q=q, qt=qt, k=k, kt=kt, v=v, o=o, lse=lse, hl=hl, rc=rc, rsl=rsl, rsh=rsh, gq2=gq2, gkv2=gkv2,
                wuq=wuq, wk=wk, wv=wv, xb2=xb2, b_g=b_g, b_m=b_m, wlat=wlat)


def _bias_lanes(b_s):
    return jnp.repeat(b_s.T, V_DIM, axis=1)


def _local_head(st, x, tgt, wt, w_oa, sg_g, sg_b, w_s, bsb, w_ob, w_out, ln_g, ln_b):
    q, qt, k, kt, v, o, lse, hl = (st[n] for n in ("q", "qt", "k", "kt", "v", "o", "lse", "hl"))
    rc, rsl, rsh, gq2, gkv2, wuq, wk, wv = (st[n] for n in ("rc", "rsl", "rsh", "gq2", "gkv2", "wuq", "wk", "wv"))
    hg, hm = _fwd_rest(st["xb2"], wt, st["b_g"], st["b_m"])
    (dr, dhg, dhm, do, dot, delta, dwout, dwoa, dwob, dws, dbs, dlng, dlnb, dsgg, dsgb, loss, dbg,
     dbm) = _mid(x, tgt, o, hm, hg, w_oa, w_ob, w_out, ln_g[None, :], ln_b[None, :], sg_g[None, :], sg_b[None, :],
                 w_s, bsb)
    early = {
        "w_oa": dwoa, "sgu_ln_g": dsgg[0], "sgu_ln_b": dsgb[0], "w_s": dws, "b_s": dbs[:, :GROUPS].T,
        "w_ob": dwob, "w_out": dwout, "ln_g": dlng[0], "ln_b": dlnb[0],
    }
    state = dict(q=q, qt=qt, k=k, kt=kt, v=v, do=do, dot=dot, lse=lse, delta=delta, hl=hl, rc=rc, rsl=rsl, rsh=rsh,
                 gq2=gq2, gkv2=gkv2, wuq=wuq, wk=wk, wv=wv, dr=dr, dhg=dhg, dhm=dhm, wt=wt, dbg=dbg, dbm=dbm,
                 xb2=st["xb2"], wlat=st["wlat"])
    return loss, early, state


def _local_attn_bwd(st, after, first_pair=0, n_pairs=HEADS // 2, into=(), name="attn_bwd"):
    return _attn_bwd(st["q"], st["qt"], st["k"], st["kt"], st["v"], st["do"], st["dot"], st["lse"], st["delta"],
                     after, first_pair, n_pairs, into, name)


def _local_tail(st, dq, dk, dv, after):
    dhl, dwuq, dwk, dwv, dgq, dgkv, dbl = _lat_bwd(dq, dk, dv, st["hl"], st["rc"], st["rsl"], st["rsh"],
                                                         st["gq2"], st["gkv2"], st["wuq"], st["wk"], st["wv"], after)
    late = {
        "w_lat": _dwt_lat(dhl, st["xb2"]),
        "b_in": _join_bias(st["dbg"], st["dbm"], dbl),
        "g_q": dgq[0],
        "w_uq": dwuq.reshape(Q_RANK, HEADS, HEAD_PAD)[:, :, :QK_DIM],
        "g_kv": dgkv[0],
        "w_ukv": jnp.concatenate([dwk.reshape(KV_RANK, HEADS, HEAD_PAD)[:, :, :NOPE],
                                  dwv.reshape(KV_RANK, HEADS, V_DIM)], axis=2),
    }
    return dhl, late


def _local_step(x, positions, tgt, wt, b_in, g_q, w_uq, g_kv, w_ukv, w_oa, sg_g, sg_b, w_s, b_s, w_ob, w_out, ln_g,
                ln_b):
    st = _local_attention(x, _rope_tables(positions), wt[:LAT_COLS], _prep_local(b_in, g_q, g_kv, w_ukv),
                          _pad_heads(w_uq).astype(BF16), b_in)
    loss, early, st = _local_head(st, x, tgt, wt, w_oa, sg_g, sg_b, w_s, _bias_lanes(b_s), w_ob, w_out, ln_g, ln_b)
    dq, dk, dv = _local_attn_bwd(st, loss)
    dhl, late = _local_tail(st, dq, dk, dv, dv)
    dx = _dx(st["dr"], st["dhg"], st["dhm"], dhl, st["wt"], st["wlat"], dhl)
    grads = {**early, **late}
    right, left = _dwt_early(st["dhm"], st["dhg"], st["xb2"], jnp.zeros((1,), jnp.int32), dhl)
    grads["w_in"] = jnp.concatenate([grads.pop("w_lat")[:LAT_COLS], jnp.concatenate([left, right], axis=1)[LAT_COLS:]],
                                    axis=0)
    return loss, dx, grads


MESH = pl.DeviceIdType.MESH
N_CHIPS = 4
HBM_SPEC = pl.BlockSpec(memory_space=pl.ANY)
HBM_SPEC_STRICT = pl.BlockSpec(memory_space=pltpu.HBM)
VMEM_SPEC = pl.BlockSpec(memory_space=pltpu.VMEM)

REP_ROWS = 80


def _rows8(a):
    flat = a.reshape(-1)
    n = -(-flat.shape[0] // (8 * D_MODEL)) * 8 * D_MODEL
    return jnp.pad(flat, (0, n - flat.shape[0])).reshape(-1, D_MODEL)


def _place():
    x, y, c = lax.axis_index("x"), lax.axis_index("y"), lax.axis_index("c")
    others = [(1 - x, y), (x, 1 - y), (1 - x, 1 - y)]
    return x, y, c, others


N_DEV = 8
LOSS_TILE = (8, 128)


def _cast_own(shards, me, after):
    n = len(shards)

    def body(me_ref, *refs):
        for w in range(n):
            refs[n + 1 + w][...] = refs[w][...].astype(BF16)

    return pl.pallas_call(
        body, name="cast_own",
        grid_spec=pltpu.PrefetchScalarGridSpec(
            num_scalar_prefetch=1, grid=(1,),
            in_specs=[pl.BlockSpec(s.shape, lambda i, me_ref: (0, 0)) for s in shards]
            + [pl.BlockSpec(memory_space=pl.ANY)],
            out_specs=[pl.BlockSpec((None,) + s.shape, lambda i, me_ref: (me_ref[0], 0, 0)) for s in shards]),
        out_shape=[jax.ShapeDtypeStruct((N_CHIPS,) + s.shape, BF16) for s in shards],
        compiler_params=pltpu.CompilerParams(vmem_limit_bytes=VMEM_LIMIT),
    )(me, *shards, after)


def _cast_first(lat, uq, me):
    def body(me_ref, lat_ref, uq_ref, wlat_ref, guq_ref):
        wlat_ref[...] = lat_ref[...].astype(BF16)
        guq_ref[...] = uq_ref[...].astype(BF16)

    return pl.pallas_call(
        body, name="cast_first",
        grid_spec=pltpu.PrefetchScalarGridSpec(
            num_scalar_prefetch=1, grid=(1,),
            in_specs=[pl.BlockSpec((LAT_COLS, D_MODEL), lambda i, me_ref: (0, 0)),
                      pl.BlockSpec(uq.shape, lambda i, me_ref: (0, 0))],
            out_specs=[pl.BlockSpec((LAT_COLS, D_MODEL), lambda i, me_ref: (0, 0)),
                       pl.BlockSpec((None,) + uq.shape, lambda i, me_ref: (me_ref[0], 0, 0))]),
        out_shape=[jax.ShapeDtypeStruct((LAT_COLS, D_MODEL), BF16),
                   jax.ShapeDtypeStruct((N_CHIPS,) + uq.shape, BF16)],
    )(me, lat, uq)


def _first_copies(wlat_ref, guq_ref, send_sems, recv_sems, shapes):
    x, y, c, others = _place()
    me = 2 * x + y
    hl, hu = shapes[0][1] // 2, shapes[1][2] // 2
    lat_half = wlat_ref.at[:, pl.ds(c * hl, hl)]

    def copy(src, dst, k, to):
        return pltpu.make_async_remote_copy(src_ref=src, dst_ref=dst, send_sem=send_sems.at[k],
                                            recv_sem=recv_sems.at[k], device_id=to, device_id_type=MESH)

    def uq_half(chip):
        return guq_ref.at[chip, :, pl.ds(c * hu, hu)]

    lat_out = [copy(lat_half, lat_half, j, (*others[j], c)) for j in range(3)]
    uq_out = [copy(uq_half(me), uq_half(me), 3 + j, (*others[j], c)) for j in range(3)]
    j0 = jnp.maximum(x + 2 * y - 1, 0)
    lat_in = copy(lat_half, lat_half, j0, (0, 0, c))
    uq_in = [copy(uq_half(me), uq_half(2 * px + py), 3 + j, (px, py, c)) for j, (px, py) in enumerate(others)]
    return me, lat_out, uq_out, lat_in, uq_in


def _first_start(wlat, guq):
    shapes = (wlat.shape, guq.shape)

    def body(wlat_ref, guq_ref, send_sems, recv_sems, wlat_thru, guq_thru, token):
        me, lat_out, uq_out, _, _ = _first_copies(wlat_ref, guq_ref, send_sems, recv_sems, shapes)

        @pl.when(me == 0)
        def _():
            for cp in lat_out:
                cp.start()

        for cp in uq_out:
            cp.start()
        token[...] = jnp.zeros_like(token)

    outs = pl.pallas_call(
        body, name="first_start",
        out_shape=(pltpu.SemaphoreType.DMA((6,)), pltpu.SemaphoreType.DMA((6,)), pltpu.HBM(wlat.shape, BF16),
                   pltpu.HBM(guq.shape, BF16), jax.ShapeDtypeStruct(LOSS_TILE, F32)),
        in_specs=[HBM_SPEC_STRICT] * 2, out_specs=(SEM_SPEC, SEM_SPEC, HBM_SPEC_STRICT, HBM_SPEC_STRICT, VMEM_SPEC),
        input_output_aliases={0: 2, 1: 3},
        compiler_params=pltpu.CompilerParams(has_side_effects=SPLIT_EFFECT),
    )(pltpu.with_memory_space_constraint(wlat, pltpu.HBM), pltpu.with_memory_space_constraint(guq, pltpu.HBM))
    return outs


def _first_wait(send_sems, recv_sems, wlat, guq, *after):
    shapes = (wlat.shape, guq.shape)

    def body(wlat_ref, guq_ref, send_sems, recv_sems, *rest):
        me, lat_out, uq_out, lat_in, uq_in = _first_copies(wlat_ref, guq_ref, send_sems, recv_sems, shapes)

        @pl.when(me == 0)
        def _():
            for cp in lat_out:
                cp.wait_send()

        @pl.when(me != 0)
        def _():
            lat_in.wait_recv()

        for cp in uq_out:
            cp.wait_send()
        for cp in uq_in:
            cp.wait_recv()

    return pl.pallas_call(
        body, name="first_wait", out_shape=(pltpu.HBM(wlat.shape, BF16), pltpu.HBM(guq.shape, BF16)),
        in_specs=[HBM_SPEC_STRICT, HBM_SPEC_STRICT, SEM_SPEC, SEM_SPEC] + [HBM_SPEC] * len(after),
        out_specs=(HBM_SPEC_STRICT, HBM_SPEC_STRICT), input_output_aliases={0: 0, 1: 1},
        compiler_params=pltpu.CompilerParams(has_side_effects=SPLIT_EFFECT),
    )(wlat, guq, send_sems, recv_sems, *after)


def _first_forward(wlat, guq):
    hl, hu = wlat.shape[1] // 2, guq.shape[2] // 2

    def body(wlat_in, guq_in, wlat_ref, guq_ref, send_sems, recv_sems):
        x, y, c, others = _place()
        me = 2 * x + y
        sibling = (x, y, 1 - c)

        def copy(part, k):
            return pltpu.make_async_remote_copy(src_ref=part, dst_ref=part, send_sem=send_sems.at[k],
                                                recv_sem=recv_sems.at[k], device_id=sibling, device_id_type=MESH)

        def uq_part(j, half):
            px, py = others[j]
            return guq_ref.at[2 * px + py, :, pl.ds(half * hu, hu)]

        cps = [copy(uq_part(j, c), j) for j in range(3)]
        for cp in cps:
            cp.start()

        @pl.when(me != 0)
        def _():
            mine = copy(wlat_ref.at[:, pl.ds(c * hl, hl)], 3)
            mine.start()
            copy(wlat_ref.at[:, pl.ds((1 - c) * hl, hl)], 3).wait_recv()
            mine.wait_send()

        for j in range(3):
            copy(uq_part(j, 1 - c), j).wait_recv()
        for cp in cps:
            cp.wait_send()

    return pl.pallas_call(
        body, name="first_forward", in_specs=[HBM_SPEC, HBM_SPEC], out_specs=[HBM_SPEC, HBM_SPEC],
        out_shape=[jax.ShapeDtypeStruct(wlat.shape, BF16), jax.ShapeDtypeStruct(guq.shape, BF16)],
        input_output_aliases={0: 0, 1: 1},
        scratch_shapes=[pltpu.SemaphoreType.DMA((4,)), pltpu.SemaphoreType.DMA((4,))],
    )(wlat, guq)


def _gather_copy(b_ref, buf, w, j, src_chip, dst_chip, to, rows, send_sems, recv_sems):
    _, _, c, _ = _place()
    hc = _half(buf)
    return pltpu.make_async_remote_copy(
        src_ref=b_ref.at[src_chip, rows, pl.ds(c * hc, hc)], dst_ref=b_ref.at[dst_chip, rows, pl.ds(c * hc, hc)],
        send_sem=send_sems.at[3 * w + j], recv_sem=recv_sems.at[3 * w + j], device_id=to, device_id_type=MESH)


def _gather_rows(buf, w, chip, fn):
    if w != 0:
        fn(slice(None))
        return
    pl.when(chip == 0)(lambda: fn(pl.ds(LAT_COLS, buf.shape[1] - LAT_COLS)))
    pl.when(chip != 0)(lambda: fn(slice(None)))


def _gather_start(bufs, after):
    n = len(bufs)

    def body(*refs):
        b_refs = refs[:n]
        send_sems, recv_sems, token = refs[n + 1], refs[n + 2], refs[-1]
        x, y, c, others = _place()
        me = 2 * x + y
        for w in range(n):
            def start(rows, w=w):
                for j, (px, py) in enumerate(others):
                    _gather_copy(b_refs[w], bufs[w], w, j, me, me, (px, py, c), rows, send_sems, recv_sems).start()
            _gather_rows(bufs[w], w, me, start)
        token[...] = jnp.zeros_like(token)

    hbm = [pltpu.HBM(b.shape, BF16) for b in bufs]
    outs = pl.pallas_call(
        body, name="gather_start",
        out_shape=(pltpu.SemaphoreType.DMA((3 * n,)), pltpu.SemaphoreType.DMA((3 * n,)), *hbm,
                   jax.ShapeDtypeStruct(LOSS_TILE, F32)),
        in_specs=[HBM_SPEC_STRICT] * n + [HBM_SPEC],
        out_specs=(SEM_SPEC, SEM_SPEC, *[HBM_SPEC_STRICT] * n, VMEM_SPEC),
        input_output_aliases={i: 2 + i for i in range(n)},
        compiler_params=pltpu.CompilerParams(has_side_effects=SPLIT_EFFECT),
    )(*[pltpu.with_memory_space_constraint(b, pltpu.HBM) for b in bufs], after)
    return outs[0], outs[1], list(outs[2:2 + n]), outs[-1]


def _gather_wait(send_sems, recv_sems, bufs, after):
    n = len(bufs)

    def body(*refs):
        b_refs = refs[:n]
        send_sems, recv_sems = refs[n], refs[n + 1]
        x, y, c, others = _place()
        me = 2 * x + y
        for w in range(n):
            for j, (px, py) in enumerate(others):
                def copy(rows, w=w, j=j, px=px, py=py):
                    return _gather_copy(b_refs[w], bufs[w], w, j, me, 2 * px + py, (px, py, c), rows, send_sems,
                                        recv_sems)
                _gather_rows(bufs[w], w, me, lambda rows, copy=copy: copy(rows).wait_send())
                _gather_rows(bufs[w], w, 2 * px + py, lambda rows, copy=copy: copy(rows).wait_recv())

    outs = pl.pallas_call(
        body, name="gather_wait", out_shape=tuple(pltpu.HBM(b.shape, b.dtype) for b in bufs),
        in_specs=[HBM_SPEC_STRICT] * n + [SEM_SPEC, SEM_SPEC, HBM_SPEC],
        out_specs=tuple([HBM_SPEC_STRICT] * n), input_output_aliases={i: i for i in range(n)},
        compiler_params=pltpu.CompilerParams(has_side_effects=SPLIT_EFFECT),
    )(*bufs, send_sems, recv_sems, after)
    return list(outs)


def _gather_finish(bufs):
    n = len(bufs)

    def body(*refs):
        b_refs = refs[n:2 * n]
        send_sems, recv_sems = refs[2 * n:]
        x, y, c, others = _place()
        cps = []
        for w in range(n):
            hc = _half(bufs[w])
            for j, (px, py) in enumerate(others):
                part = b_refs[w].at[2 * px + py, :, pl.ds(c * hc, hc)]
                cps.append(pltpu.make_async_remote_copy(
                    src_ref=part, dst_ref=part, send_sem=send_sems.at[3 * w + j], recv_sem=recv_sems.at[3 * w + j],
                    device_id=(x, y, 1 - c), device_id_type=MESH))
        for cp in cps:
            cp.start()
        for w in range(n):
            hc = _half(bufs[w])
            for j, (px, py) in enumerate(others):
                theirs = b_refs[w].at[2 * px + py, :, pl.ds((1 - c) * hc, hc)]
                pltpu.make_async_remote_copy(
                    src_ref=theirs, dst_ref=theirs, send_sem=send_sems.at[3 * w + j], recv_sem=recv_sems.at[3 * w + j],
                    device_id=(x, y, 1 - c), device_id_type=MESH).wait_recv()
        for cp in cps:
            cp.wait_send()

    return pl.pallas_call(
        body, name="gather_finish", in_specs=[HBM_SPEC] * n, out_specs=[HBM_SPEC] * n,
        out_shape=[jax.ShapeDtypeStruct(b.shape, b.dtype) for b in bufs],
        input_output_aliases={i: i for i in range(n)},
        scratch_shapes=[pltpu.SemaphoreType.DMA((3 * n,)), pltpu.SemaphoreType.DMA((3 * n,))],
    )(*bufs)


def _half(a):
    return a.shape[-1] // 2


def _exchange_pairs(parts, name):
    n = len(parts)

    def body(*refs):
        p_refs, r_refs = refs[:n], refs[n:2 * n]
        send_sems, recv_sems = refs[2 * n:]
        x, y, c, _ = _place()
        cps = []
        for w in range(n):
            h = _half(parts[w])
            cps.append(pltpu.make_async_remote_copy(
                src_ref=p_refs[w].at[:, :, pl.ds((1 - c) * h, h)], dst_ref=r_refs[w],
                send_sem=send_sems.at[w], recv_sem=recv_sems.at[w], device_id=(x, y, 1 - c), device_id_type=MESH))
        for cp in cps:
            cp.start()
        for cp in cps:
            cp.wait()

    return pl.pallas_call(
        body, name=name, in_specs=[HBM_SPEC] * n, out_specs=[HBM_SPEC] * n,
        out_shape=[jax.ShapeDtypeStruct((N_CHIPS, p.shape[1], _half(p)), BF16) for p in parts],
        scratch_shapes=[pltpu.SemaphoreType.DMA((n,)), pltpu.SemaphoreType.DMA((n,))],
    )(*parts)


def _sibling_part(ref, w, n_whole, shape, c):
    if w < n_whole:
        return ref
    h = shape[-1] // 2
    return ref.at[:, :, pl.ds((1 - c) * h, h)]


def _pairs_start(parts, all_loss, n_whole):
    n = len(parts)

    def body(*refs):
        p_refs, r_refs, loss_ref = refs[:n], refs[n:2 * n], refs[2 * n]
        send_sems, recv_sems, token = refs[2 * n + 1], refs[2 * n + 2], refs[-1]
        x, y, c, _ = _place()
        for w in range(n):
            h = _half(parts[w])
            pltpu.make_async_remote_copy(
                src_ref=_sibling_part(p_refs[w], w, n_whole, parts[w].shape, c), dst_ref=r_refs[w],
                send_sem=send_sems.at[w], recv_sem=recv_sems.at[w], device_id=(x, y, 1 - c),
                device_id_type=MESH).start()
        me = 4 * x + 2 * y + c
        for t in range(1, N_DEV):
            d = (me + t) % N_DEV
            pltpu.make_async_remote_copy(
                src_ref=loss_ref.at[me], dst_ref=loss_ref.at[me], send_sem=send_sems.at[n + t - 1],
                recv_sem=recv_sems.at[n + t - 1], device_id=(d // 4, (d // 2) % 2, d % 2), device_id_type=MESH).start()
        token[...] = jnp.zeros_like(token)

    lands = [pltpu.HBM(p.shape if w < n_whole else (N_CHIPS, p.shape[1], _half(p)), BF16)
             for w, p in enumerate(parts)]
    nsem = n + N_DEV - 1
    outs = pl.pallas_call(
        body, name="pairs_start",
        out_shape=(pltpu.SemaphoreType.DMA((nsem,)), pltpu.SemaphoreType.DMA((nsem,)),
                   *[pltpu.HBM(p.shape, p.dtype) for p in parts], *lands, pltpu.HBM(all_loss.shape, F32),
                   jax.ShapeDtypeStruct(LOSS_TILE, F32)),
        in_specs=[HBM_SPEC_STRICT] * (2 * n + 1),
        out_specs=(SEM_SPEC, SEM_SPEC, *[HBM_SPEC_STRICT] * (2 * n + 1), VMEM_SPEC),
        input_output_aliases={i: 2 + i for i in range(2 * n + 1)},
        compiler_params=pltpu.CompilerParams(has_side_effects=SPLIT_EFFECT),
    )(*[pltpu.with_memory_space_constraint(p, pltpu.HBM) for p in parts],
      *[pltpu.with_memory_space_constraint(lax.empty(l.shape, BF16), pltpu.HBM) for l in lands],
      pltpu.with_memory_space_constraint(all_loss, pltpu.HBM))
    return outs[0], outs[1], list(outs[2:2 + n]), list(outs[2 + n:2 + 2 * n]), outs[2 + 2 * n], outs[-1]


def _pairs_wait(send_sems, recv_sems, parts, lands, all_loss, after, n_whole):
    n = len(parts)

    def body(*refs):
        p_refs, r_refs, loss_ref = refs[:n], refs[n:2 * n], refs[2 * n]
        send_sems, recv_sems = refs[2 * n + 1], refs[2 * n + 2]
        x, y, c, _ = _place()
        for w in range(n):
            h = _half(parts[w])
            cp = pltpu.make_async_remote_copy(
                src_ref=_sibling_part(p_refs[w], w, n_whole, parts[w].shape, c), dst_ref=r_refs[w],
                send_sem=send_sems.at[w],
                recv_sem=recv_sems.at[w], device_id=(x, y, 1 - c), device_id_type=MESH)
            cp.wait_send()
            cp.wait_recv()
        me = 4 * x + 2 * y + c
        for t in range(1, N_DEV):
            d = (me + N_DEV - t) % N_DEV
            cp = pltpu.make_async_remote_copy(
                src_ref=loss_ref.at[me], dst_ref=loss_ref.at[d], send_sem=send_sems.at[n + t - 1],
                recv_sem=recv_sems.at[n + t - 1], device_id=(d // 4, (d // 2) % 2, d % 2), device_id_type=MESH)
            cp.wait_send()
            cp.wait_recv()

    bufs = (*parts, *lands, all_loss)
    outs = pl.pallas_call(
        body, name="pairs_wait", out_shape=tuple(pltpu.HBM(a.shape, a.dtype) for a in bufs),
        in_specs=[HBM_SPEC_STRICT] * len(bufs) + [SEM_SPEC, SEM_SPEC, HBM_SPEC],
        out_specs=tuple([HBM_SPEC_STRICT] * len(bufs)), input_output_aliases={i: i for i in range(len(bufs))},
        compiler_params=pltpu.CompilerParams(has_side_effects=SPLIT_EFFECT),
    )(*bufs, send_sems, recv_sems, after)
    return list(outs[:n]), list(outs[n:2 * n]), outs[2 * n]


def _add_pair(ps, rs, c, name, n_whole=0):
    n = len(ps)

    def body(c_ref, *refs):
        for w in range(n):
            refs[2 * n + w][...] = (refs[w][...].astype(F32) + refs[n + w][...].astype(F32)).astype(BF16)

    def slab_spec(a):
        return pl.BlockSpec((None,) + a.shape[1:], lambda k, c_ref: (k, 0, 0))

    def my_half_spec(a):
        return pl.BlockSpec((None, a.shape[1], _half(a)), lambda k, c_ref: (k, 0, c_ref[0]))

    return pl.pallas_call(
        body, name=name,
        grid_spec=pltpu.PrefetchScalarGridSpec(
            num_scalar_prefetch=1, grid=(N_CHIPS,),
            in_specs=[slab_spec(p) if w < n_whole else my_half_spec(p) for w, p in enumerate(ps)]
            + [slab_spec(r) for r in rs],
            out_specs=[slab_spec(r) for r in rs]),
        out_shape=[jax.ShapeDtypeStruct(r.shape, BF16) for r in rs],
        compiler_params=_params(),
    )(c, *ps, *rs)


SEM_SPEC = pl.BlockSpec(memory_space=pltpu.SEMAPHORE)
SPLIT_EFFECT = pltpu.SideEffectType.DATAFLOW_SIDE_EFFECTING


def _chips_start(qs, name):
    n = len(qs)

    def body(*refs):
        q_refs, land_refs = refs[:n], refs[n:2 * n]
        send_sems, recv_sems, token = refs[2 * n], refs[2 * n + 1], refs[-1]
        x, y, c, others = _place()
        me = 2 * x + y
        for w in range(n):
            for j, (px, py) in enumerate(others):
                pltpu.make_async_remote_copy(
                    src_ref=q_refs[w].at[2 * px + py], dst_ref=land_refs[w].at[me], send_sem=send_sems.at[3 * w + j],
                    recv_sem=recv_sems.at[3 * w + j], device_id=(px, py, c), device_id_type=MESH).start()
        token[...] = jnp.zeros_like(token)

    hbm = [pltpu.HBM(q.shape, BF16) for q in qs]
    outs = pl.pallas_call(
        body, name=name,
        out_shape=(pltpu.SemaphoreType.DMA((3 * n,)), pltpu.SemaphoreType.DMA((3 * n,)), *hbm, *hbm,
                   jax.ShapeDtypeStruct(LOSS_TILE, F32)),
        in_specs=[HBM_SPEC_STRICT] * (2 * n),
        out_specs=(SEM_SPEC, SEM_SPEC, *[HBM_SPEC_STRICT] * (2 * n), VMEM_SPEC),
        input_output_aliases={i: 2 + i for i in range(2 * n)},
        compiler_params=pltpu.CompilerParams(has_side_effects=SPLIT_EFFECT),
    )(*[pltpu.with_memory_space_constraint(q, pltpu.HBM) for q in qs],
      *[pltpu.with_memory_space_constraint(lax.empty(q.shape, BF16), pltpu.HBM) for q in qs])
    return outs[0], outs[1], outs[2:2 + n], outs[2 + n:2 + 2 * n], outs[-1]


def _chips_wait(send_sems, recv_sems, q_thru, land_thru, after, name):
    n = len(q_thru)

    def body(*refs):
        q_refs, land_refs = refs[:n], refs[n:2 * n]
        send_sems, recv_sems = refs[2 * n], refs[2 * n + 1]
        x, y, c, others = _place()
        me = 2 * x + y
        for w in range(n):
            for j, (px, py) in enumerate(others):
                cp = pltpu.make_async_remote_copy(
                    src_ref=q_refs[w].at[2 * px + py], dst_ref=land_refs[w].at[2 * px + py],
                    send_sem=send_sems.at[3 * w + j], recv_sem=recv_sems.at[3 * w + j], device_id=(px, py, c),
                    device_id_type=MESH)
                cp.wait_send()
                cp.wait_recv()

    outs = pl.pallas_call(
        body, name=name, out_shape=tuple(pltpu.HBM(a.shape, a.dtype) for a in (*q_thru, *land_thru)),
        in_specs=[HBM_SPEC_STRICT] * (2 * n) + [SEM_SPEC, SEM_SPEC, HBM_SPEC],
        out_specs=tuple([HBM_SPEC_STRICT] * (2 * n)), input_output_aliases={i: i for i in range(2 * n)},
        compiler_params=pltpu.CompilerParams(has_side_effects=SPLIT_EFFECT),
    )(*q_thru, *land_thru, send_sems, recv_sems, after)
    return list(outs[:n]), list(outs[n:])


def _sum_chips(qs, rs, idx, all_dtypes):
    n = len(rs)
    n_all = len(all_dtypes)

    def body(idx_ref, *refs):
        c = idx_ref[4]
        for w in range(n):
            q_ref, g_ref = refs[w], refs[4 * n + w]
            acc = q_ref[...].astype(F32)
            for t in range(1, N_CHIPS):
                acc = acc + refs[n + 3 * w + t - 1][...].astype(F32)
            h = rs[w].shape[2]
            mine = pl.ds(pl.multiple_of(c * h, 128), h)
            g_ref[...] = jnp.zeros_like(g_ref)
            if w >= n - n_all:
                g_ref[idx_ref[0], :, mine] = acc.astype(g_ref.dtype)
            else:
                g_ref[:, mine] = acc

    shapes = [jax.ShapeDtypeStruct((r.shape[1], 2 * r.shape[2]), F32) for r in rs[:n - n_all]]
    shapes += [jax.ShapeDtypeStruct((N_CHIPS, r.shape[1], 2 * r.shape[2]), dt)
               for r, dt in zip(rs[n - n_all:], all_dtypes)]

    def slab_spec(a, t):
        return pl.BlockSpec((None,) + a.shape[1:], lambda i, idx_ref: (idx_ref[t], 0, 0))

    def whole_spec(sh):
        return pl.BlockSpec(sh.shape, lambda i, idx_ref: (0,) * len(sh.shape))

    return pl.pallas_call(
        body, name="sum_chips",
        grid_spec=pltpu.PrefetchScalarGridSpec(
            num_scalar_prefetch=1, grid=(1,),
            in_specs=[slab_spec(q, 0) for q in qs] + [slab_spec(r, t) for r in rs for t in range(1, N_CHIPS)],
            out_specs=[whole_spec(sh) for sh in shapes]),
        out_shape=shapes, compiler_params=_params(),
    )(idx, *qs, *[r for r in rs for _ in range(1, N_CHIPS)])


def _share(shards, alls):
    n, na = len(shards), len(alls)
    total = n + na

    def body(*refs):
        g_refs, a_refs = refs[total:total + n], refs[total + n:2 * total]
        send_sems, recv_sems = refs[2 * total:]
        x, y, c, others = _place()
        me = 2 * x + y
        sibling = (x, y, 1 - c)

        def cols_of(w, half):
            h = shards[w].shape[1] // 2
            return g_refs[w].at[:, pl.ds(half * h, h)]

        def slab(a, chip, half):
            h = alls[a].shape[2] // 2
            return a_refs[a].at[chip, :, pl.ds(half * h, h)]

        def copy(src, dst, k, to):
            return pltpu.make_async_remote_copy(src_ref=src, dst_ref=dst, send_sem=send_sems.at[k],
                                                recv_sem=recv_sems.at[k], device_id=to, device_id_type=MESH)

        cps = []
        for a in range(na):
            base = n + 7 * a
            for j, (px, py) in reversed(list(enumerate(others))):
                cps.append(copy(slab(a, me, c), slab(a, me, c), base + 1 + j, (px, py, c)))
            cps.append(copy(slab(a, me, c), slab(a, me, c), base, sibling))
        cps += [copy(cols_of(w, c), cols_of(w, c), w, sibling) for w in range(n)]
        for cp in cps:
            cp.start()
        fwd = []
        for a in range(na):
            base = n + 7 * a
            for j, (px, py) in enumerate(others):
                chip = 2 * px + py
                copy(slab(a, me, c), slab(a, chip, c), base + 1 + j, (px, py, c)).wait_recv()
                cp = copy(slab(a, chip, c), slab(a, chip, c), base + 4 + j, sibling)
                cp.start()
                fwd.append(cp)
        for a in range(na):
            base = n + 7 * a
            for j, (px, py) in enumerate(others):
                chip = 2 * px + py
                copy(slab(a, chip, c), slab(a, chip, 1 - c), base + 4 + j, sibling).wait_recv()
            copy(slab(a, me, c), slab(a, me, 1 - c), base, sibling).wait_recv()
        for w in range(n):
            copy(cols_of(w, c), cols_of(w, 1 - c), w, sibling).wait_recv()
        for cp in cps + fwd:
            cp.wait_send()

    nsem = n + 7 * na
    return pl.pallas_call(
        body, name="share", in_specs=[HBM_SPEC_STRICT] * total, out_specs=[HBM_SPEC_STRICT] * total,
        out_shape=[pltpu.HBM(a.shape, a.dtype) for a in (*shards, *alls)],
        input_output_aliases={i: i for i in range(total)},
        scratch_shapes=[pltpu.SemaphoreType.DMA((nsem,)), pltpu.SemaphoreType.DMA((nsem,))],
    )(*[pltpu.with_memory_space_constraint(a, pltpu.HBM) for a in (*shards, *alls)])


def _adamw(w, g, m, v):
    m2 = ADAM_B1 * m + (1.0 - ADAM_B1) * g
    v2 = ADAM_B2 * v + (1.0 - ADAM_B2) * (g * g)
    m_hat = m2 / (1.0 - ADAM_B1 ** ADAM_STEP)
    v_hat = v2 / (1.0 - ADAM_B2 ** ADAM_STEP)
    return -ADAM_LR * (m_hat / (jnp.sqrt(v_hat) + ADAM_EPS) + ADAM_WD * w), m2, v2


def _update_w_in(wt, gt, mt, vt, lat, owner, tile):
    nlat = lat.shape[0] // tile

    def body(owner_ref, w_ref, g_ref, m_ref, v_ref, lat_ref, g2_ref, d_ref, m2_ref, v2_ref):
        row = pl.program_id(0) * tile + lax.broadcasted_iota(jnp.int32, (tile, 1), 0)
        g = jnp.where((row < LAT_COLS) & (owner_ref[0] == 1), lat_ref[...].astype(F32), g_ref[...])
        g2_ref[...] = g
        d_ref[...], m2_ref[...], v2_ref[...] = _adamw(w_ref[...], g, m_ref[...], v_ref[...])

    spec = pl.BlockSpec((tile, wt.shape[1]), lambda i, o: (i, 0))
    return pl.pallas_call(
        body, name="update_w_in",
        grid_spec=pltpu.PrefetchScalarGridSpec(
            num_scalar_prefetch=1, grid=(wt.shape[0] // tile,),
            in_specs=[spec] * 4 + [pl.BlockSpec((tile, wt.shape[1]), lambda i, o: (jnp.minimum(i, nlat - 1), 0))],
            out_specs=[spec] * 4),
        out_shape=[jax.ShapeDtypeStruct(wt.shape, F32)] * 4,
        compiler_params=_params(("parallel",)),
    )(owner, wt, gt, mt, vt, lat)


def _update_small(ws, gs, ms, vs):
    n = len(ws)

    def body(*refs):
        for k in range(n):
            w_ref, g_ref, m_ref, v_ref = refs[k], refs[n + k], refs[2 * n + k], refs[3 * n + k]
            d, m2, v2 = _adamw(w_ref[...], g_ref[...], m_ref[...], v_ref[...])
            refs[4 * n + k][...] = d
            refs[5 * n + k][...] = m2
            refs[6 * n + k][...] = v2

    shapes = [jax.ShapeDtypeStruct(w.shape, F32) for w in ws]
    outs = pl.pallas_call(
        body, name="update_small", in_specs=[VMEM_SPEC] * (4 * n), out_specs=[VMEM_SPEC] * (3 * n),
        out_shape=shapes * 3,
        compiler_params=pltpu.CompilerParams(vmem_limit_bytes=VMEM_LIMIT),
    )(*ws, *gs, *ms, *vs)
    return outs[:n], outs[n:2 * n], outs[2 * n:]


REPLICATED = ("b_in", "g_q", "g_kv", "w_ukv", "sgu_ln_g", "sgu_ln_b", "w_s", "b_s", "ln_g", "ln_b")
ORDER = ("w_in", "b_in", "g_q", "w_uq", "g_kv", "w_ukv", "w_oa", "sgu_ln_g", "sgu_ln_b", "w_s", "b_s", "w_ob", "w_out",
         "ln_g", "ln_b")


def kernel(x, positions, w_in, b_in, g_q, w_uq, g_kv, w_ukv, w_oa, sgu_ln_g, sgu_ln_b, w_s, b_s, w_ob, w_out, ln_g, ln_b, loss_target, m_w_in, m_b_in, m_g_q, m_w_uq, m_g_kv, m_w_ukv, m_w_oa, m_sgu_ln_g, m_sgu_ln_b, m_w_s, m_b_s, m_w_ob, m_w_out, m_ln_g, m_ln_b, v_w_in, v_b_in, v_g_q, v_w_uq, v_g_kv, v_w_ukv, v_w_oa, v_sgu_ln_g, v_sgu_ln_b, v_w_s, v_b_s, v_w_ob, v_w_out, v_ln_g, v_ln_b):
    w = dict(w_in=w_in, b_in=b_in, g_q=g_q, w_uq=w_uq, g_kv=g_kv, w_ukv=w_ukv, w_oa=w_oa, sgu_ln_g=sgu_ln_g,
             sgu_ln_b=sgu_ln_b, w_s=w_s, b_s=b_s, w_ob=w_ob, w_out=w_out, ln_g=ln_g, ln_b=ln_b)
    m = dict(w_in=m_w_in, b_in=m_b_in, g_q=m_g_q, w_uq=m_w_uq, g_kv=m_g_kv, w_ukv=m_w_ukv, w_oa=m_w_oa,
             sgu_ln_g=m_sgu_ln_g, sgu_ln_b=m_sgu_ln_b, w_s=m_w_s, b_s=m_b_s, w_ob=m_w_ob, w_out=m_w_out, ln_g=m_ln_g,
             ln_b=m_ln_b)
    v = dict(w_in=v_w_in, b_in=v_b_in, g_q=v_g_q, w_uq=v_w_uq, g_kv=v_g_kv, w_ukv=v_w_ukv, w_oa=v_w_oa,
             sgu_ln_g=v_sgu_ln_g, sgu_ln_b=v_sgu_ln_b, w_s=v_w_s, b_s=v_b_s, w_ob=v_w_ob, w_out=v_w_out, ln_g=v_ln_g,
             ln_b=v_ln_b)
    w, m, v = ({n: a[0] for n, a in d.items()} for d in (w, m, v))
    c = lax.axis_index("c")

    wt_shard, mt_shard, vt_shard = (jnp.transpose(d["w_in"]) for d in (w, m, v))
    xi, yi = lax.axis_index("x"), lax.axis_index("y")
    me1 = (2 * xi + yi).reshape(1).astype(jnp.int32)
    first = _first_start(*_cast_first(wt_shard, _pad_heads(w["w_uq"]), me1))
    tables = _rope_tables(positions[0])
    prep = _prep_local(w["b_in"], w["g_q"], w["g_kv"], w["w_ukv"])
    bsb = _bias_lanes(w["b_s"])
    c1 = c.reshape(1).astype(jnp.int32)
    idx = jnp.stack([2 * xi + yi, 2 * (1 - xi) + yi, 2 * xi + (1 - yi), 2 * (1 - xi) + (1 - yi), c]).astype(jnp.int32)
    bufs = _cast_own([wt_shard, w["w_oa"], w["w_ob"], w["w_out"]], me1, first[4])
    send0, recv0, bufs, token0 = _gather_start(bufs, first[4])
    g_lat, g_uq = _first_forward(*_first_wait(*first[:4], token0, *tables, *prep.values(), bsb, idx))
    st = _local_attention(x[0], tables, g_lat, prep, g_uq.reshape(Q_RANK, HEADS * HEAD_PAD), token0)
    g_in, g_oa, g_ob, g_out = _gather_finish(_gather_wait(send0, recv0, bufs, st["o"]))
    wt = g_in.reshape(IN_W, D_MODEL)

    loss, early, st = _local_head(
        st, x[0], loss_target[0], wt, g_oa, w["sgu_ln_g"], w["sgu_ln_b"], w["w_s"], bsb, g_ob,
        g_out.reshape(D_MODEL, D_MODEL), w["ln_g"], w["ln_b"])

    slabs = lambda a: a.reshape(N_CHIPS, IN_W // N_CHIPS, D_MODEL // 2)
    theirs, mine = (slabs(a) for a in _dwt_early(st["dhm"], st["dhg"], st["xb2"], c1, c1))
    parts1 = [theirs, early["w_oa"].astype(BF16), early["w_ob"].astype(BF16),
              early["w_out"].reshape(N_CHIPS, SLAB_W, D_MODEL).astype(BF16)]
    my_loss = lax.dynamic_update_slice(jnp.zeros((N_DEV,) + LOSS_TILE, F32), jnp.broadcast_to(loss, (1,) + LOSS_TILE),
                                       (4 * xi + 2 * yi + c, 0, 0))
    sems0 = _pairs_start(parts1, my_loss, 1)
    half = HEADS // 4
    first_pairs = _local_attn_bwd(st, sems0[5], 0, half, (), "attn_bwd_first")
    parts1, recv1, all_loss = _pairs_wait(*sems0[:5], first_pairs[1], 1)
    pairs1 = _add_pair([mine, *parts1[1:]], recv1, c1, "add_pair_early", n_whole=1)
    sems1 = _chips_start(pairs1, "chips_start_early")
    dq, dk, dv = _local_attn_bwd(st, sems1[4], half, half, tuple(first_pairs), "attn_bwd_rest")
    dhl, late = _local_tail(st, dq, dk, dv, dv)

    grads = {**early, **late}
    rep = jnp.concatenate([_rows8(grads[n]) for n in REPLICATED], axis=0)
    rep = jnp.pad(rep, ((0, N_CHIPS * REP_ROWS - rep.shape[0]), (0, 0))).reshape(N_CHIPS, REP_ROWS, D_MODEL)
    parts2 = [late["w_uq"].reshape(N_CHIPS, Q_RANK // N_CHIPS, HEADS * QK_DIM).astype(BF16), rep.astype(BF16),
              late["w_lat"].reshape(N_CHIPS, LAT_ROWS_PAD // N_CHIPS, D_MODEL)]
    pairs2 = _add_pair(parts2, _exchange_pairs(parts2, "exchange_pairs_late"), c1, "add_pair_late")
    sems2 = _chips_start(pairs2, "chips_start_late")
    dx = _dx(st["dr"], st["dhg"], st["dhm"], dhl, st["wt"], st["wlat"], sems2[4])
    pairs2, landed2 = _chips_wait(*sems2[:4], dx, "chips_wait_late")
    pairs1, landed1 = _chips_wait(*sems1[:4], landed2[0], "chips_wait_early")
    sums = _sum_chips([*pairs1, *pairs2], [*landed1, *landed2], idx, (F32, BF16))
    *shards, g_rep, g_lat = _share(sums[:-2], sums[-2:])
    loss = jnp.sum(all_loss[:, 0, 0])

    red = {n: s.reshape(w[n].shape) for n, s in zip(("w_oa", "w_ob", "w_out", "w_uq"), shards[1:])}
    g_rep = g_rep.reshape(N_CHIPS * REP_ROWS, D_MODEL)
    off = 0
    for n in REPLICATED:
        rows = _rows8(w[n]).shape[0]
        red[n] = g_rep[off:off + rows].reshape(-1)[:w[n].size].reshape(w[n].shape)
        off += rows
    owner = (2 * xi + yi == 0).astype(jnp.int32).reshape(1)
    gt, dt, mt, vt2 = _update_w_in(wt_shard, shards[0], mt_shard, vt_shard,
                                   g_lat.reshape(LAT_ROWS_PAD, D_MODEL).astype(F32), owner, 232)
    red["w_in"] = jnp.transpose(gt)
    small = [n for n in ORDER if n != "w_in"]
    as2d = lambda a: a.reshape(-1, a.shape[-1])
    ds, ms, vs = _update_small([as2d(w[n]) for n in small], [as2d(red[n]) for n in small],
                               [as2d(m[n]) for n in small], [as2d(v[n]) for n in small])
    delta, new_m, new_v = {"w_in": jnp.transpose(dt)}, {"w_in": jnp.transpose(mt)}, {"w_in": jnp.transpose(vt2)}
    for i, n in enumerate(small):
        delta[n], new_m[n], new_v[n] = (a[i].reshape(w[n].shape) for a in (ds, ms, vs))

    lead = lambda a: a[None]
    return (loss, dx[None], *[lead(red[n]) for n in ORDER], *[lead(delta[n]) for n in ORDER],
            *[lead(new_m[n]) for n in ORDER], *[lead(new_v[n]) for n in ORDER])
```

```python
import math

import jax
import jax.numpy as jnp
from jax import lax
from jax.experimental import pallas as pl
from jax.experimental.pallas import tpu as pltpu

F32 = jnp.float32
BF16 = jnp.bfloat16

D_MODEL = 1024
HEADS = 8
Q_RANK = 384
KV_RANK = 128
NOPE = 64
ROPE = 32
V_DIM = 64
QK_DIM = NOPE + ROPE
HEAD_PAD = 128
MLA_W = HEADS * V_DIM
SGU_W = 512
GROUPS = 8
CHUNK = 128
IN_W = 4640
RMS_EPS = 1e-6
LN_EPS = 1e-5
ALPHA = 2.0 ** 0.25
ROPE_THETA = 10000.0
SCALE = QK_DIM ** -0.5

GATE_W = 2 * D_MODEL
MID_W = 4 * SGU_W
LAT_W = Q_RANK + KV_RANK + HEAD_PAD
LAT_COLS = Q_RANK + KV_RANK + ROPE
ROW_GATE = LAT_COLS + MID_W
LAT_ROWS_PAD = 704
N_SLABS = 4
SLAB_W = D_MODEL // N_SLABS

ROW_TILE = 256
MATMUL_ROW_TILE = 512
MID_ROW_TILE = 256
ATT_TK = 256
ATT_BWD_TK = 256
SUM_ROWS = 16
LOG2E = 1.4426950408889634
LN2 = 0.6931471805599453
Q_SCALE = SCALE * LOG2E
VMEM_LIMIT = 56 * 1024 * 1024

ADAM_LR = 0.001
ADAM_B1 = 0.9
ADAM_B2 = 0.999
ADAM_EPS = 1e-08
ADAM_WD = 0.01
ADAM_STEP = 10


def _dot(a, b):
    return jnp.dot(a, b, preferred_element_type=F32)


def _dot_nt(a, b):
    return lax.dot_general(a, b, (((1,), (1,)), ((), ())), preferred_element_type=F32)


def _dot_tn(a, b):
    return lax.dot_general(a, b, (((0,), (0,)), ((), ())), preferred_element_type=F32)


def _sigmoid(z):
    return 0.5 * jnp.tanh(0.5 * z) + 0.5


_GELU_C = math.sqrt(2.0 / math.pi)


def _gelu_and_grad(x):
    x2 = x * x
    t = jnp.tanh(_GELU_C * (x + 0.044715 * x * x2))
    g = 0.5 * x * (1.0 + t)
    dg = 0.5 * (1.0 + t) + 0.5 * x * (1.0 - t * t) * (_GELU_C * (1.0 + 3.0 * 0.044715 * x2))
    return g, dg


def _silu_and_grad(z):
    s = _sigmoid(z)
    return z * s, s * (1.0 + z * (1.0 - s))


def _rope(xb, c, sl, sh):
    return xb * c + pltpu.roll(xb, 112, 1) * sl + pltpu.roll(xb, 16, 1) * sh


def _rope_t(dy, c, sl, sh):
    return dy * c + pltpu.roll(dy * sl, 16, 1) + pltpu.roll(dy * sh, 112, 1)


def _params(sem=("arbitrary",)):
    return pltpu.CompilerParams(dimension_semantics=sem, vmem_limit_bytes=VMEM_LIMIT)


def _row_spec(tile, width):
    return pl.BlockSpec((tile, width), lambda i: (i, 0))


def _full_spec(shape):
    nd = len(shape)
    return pl.BlockSpec(shape, lambda i: (0,) * nd)


def _kpe_rows(wt_ref):
    z = lambda n: jnp.zeros((n, D_MODEL), BF16)
    return jnp.concatenate([z(NOPE), wt_ref[Q_RANK + KV_RANK:LAT_COLS, :], z(HEAD_PAD - QK_DIM)], axis=0)


def _fwd_rest(xb2, wt, b_g, b_m):
    s = xb2.shape[1]
    ts = MATMUL_ROW_TILE
    tn = D_MODEL
    blocks = ([(ROW_GATE + c0, 0, c0) for c0 in range(0, GATE_W, tn)]
              + [(LAT_COLS + c0, 1, c0) for c0 in range(0, MID_W, tn)])

    def body(xb_ref, wt_hbm, bg_ref, bm_ref, hg_ref, hm_ref, wt_ref, sems):
        copies = [pltpu.make_async_copy(wt_hbm.at[lo:lo + tn], wt_ref.at[lo:lo + tn], sems.at[n])
                  for n, (lo, _, _) in enumerate(blocks)]

        def compute(first):
            xb_ = jnp.concatenate([xb_ref[0], xb_ref[1]], axis=1)
            for cp, (lo, which, c0) in zip(copies, blocks):
                if first:
                    cp.wait()
                out_ref, b_ref = ((hg_ref, bg_ref), (hm_ref, bm_ref))[which]
                out_ref[:, c0:c0 + tn] = (_dot_nt(xb_, wt_ref[lo:lo + tn, :]) + b_ref[:, c0:c0 + tn]).astype(BF16)

        @pl.when(pl.program_id(0) == 0)
        def _():
            for cp in copies:
                cp.start()
            compute(True)

        @pl.when(pl.program_id(0) > 0)
        def _():
            compute(False)

    return pl.pallas_call(
        body, name="fwd_rest", grid=(s // ts,),
        in_specs=[pl.BlockSpec((2, ts, D_MODEL // 2), lambda i: (0, i, 0)), HBM_SPEC, _full_spec(b_g.shape),
                  _full_spec(b_m.shape)],
        out_specs=[_row_spec(ts, GATE_W), _row_spec(ts, MID_W)],
        out_shape=[jax.ShapeDtypeStruct((s, GATE_W), BF16), jax.ShapeDtypeStruct((s, MID_W), BF16)],
        scratch_shapes=[pltpu.VMEM(wt.shape, BF16), pltpu.SemaphoreType.DMA((len(blocks),))],
        compiler_params=_params(),
    )(xb2, wt, b_g, b_m)


def _fwd_lat(x, wlat, b_l, g_q, wuq, g_kv, wk, wv, rc, rsl, rsh, after):
    s = x.shape[0]
    ts = ROW_TILE

    def body(x_ref, wt_ref, bl_ref, gq_ref, wuq_ref, gkv_ref, wk_ref, wv_ref, rc_ref, rsl_ref,
             rsh_ref, after_ref, hl_ref, q_ref, k_ref, v_ref, qt_ref, kt_ref, vt_ref, xb2_ref):
        xb = x_ref[...].astype(BF16)
        xb2_ref[0] = xb[:, :D_MODEL // 2]
        xb2_ref[1] = xb[:, D_MODEL // 2:]
        hl = jnp.concatenate([_dot_nt(xb, wt_ref[0:Q_RANK + KV_RANK, :]), _dot_nt(xb, _kpe_rows(wt_ref))],
                             axis=1) + bl_ref[...]
        hl_ref[...] = hl
        c, sl, sh = rc_ref[...], rsl_ref[...], rsh_ref[...]
        cq = hl[:, :Q_RANK]
        cqn = cq * lax.rsqrt(jnp.mean(cq * cq, axis=-1, keepdims=True) + RMS_EPS) * gq_ref[...]
        q = _dot(cqn.astype(BF16), wuq_ref[...])
        ckv = hl[:, Q_RANK:Q_RANK + KV_RANK]
        ckvn = (ckv * lax.rsqrt(jnp.mean(ckv * ckv, axis=-1, keepdims=True) + RMS_EPS) * gkv_ref[...]).astype(BF16)
        k = _dot(ckvn, wk_ref[...])
        vb = _dot(ckvn, wv_ref[...]).astype(BF16)
        v_ref[...] = vb
        vt_ref[...] = vb.T
        kpe = _rope(hl[:, Q_RANK + KV_RANK:], c, sl, sh)
        for hd in range(HEADS):
            lanes = slice(hd * HEAD_PAD, (hd + 1) * HEAD_PAD)
            qb = (_rope(q[:, lanes], c, sl, sh) * Q_SCALE).astype(BF16)
            kb = (k[:, lanes] + kpe).astype(BF16)
            q_ref[:, lanes] = qb
            k_ref[:, lanes] = kb
            qt_ref[lanes, :] = qb.T
            kt_ref[lanes, :] = kb.T

    qk_w = HEADS * HEAD_PAD
    col_spec = lambda rows: pl.BlockSpec((rows, ts), lambda i: (0, i))
    return pl.pallas_call(
        body, name="fwd_lat", grid=(s // ts,),
        in_specs=[_row_spec(ts, D_MODEL), _full_spec(wlat.shape),
                  _full_spec(b_l.shape), _full_spec(g_q.shape),
                  _full_spec(wuq.shape), _full_spec(g_kv.shape), _full_spec(wk.shape), _full_spec(wv.shape),
                  _row_spec(ts, HEAD_PAD), _row_spec(ts, HEAD_PAD), _row_spec(ts, HEAD_PAD),
                  pl.BlockSpec(memory_space=pl.ANY)],
        out_specs=[_row_spec(ts, LAT_W), _row_spec(ts, qk_w),
                   _row_spec(ts, qk_w), _row_spec(ts, MLA_W), col_spec(qk_w), col_spec(qk_w),
                   col_spec(MLA_W), pl.BlockSpec((2, ts, D_MODEL // 2), lambda i: (0, i, 0))],
        out_shape=[jax.ShapeDtypeStruct((s, LAT_W), F32), jax.ShapeDtypeStruct((s, qk_w), BF16),
                   jax.ShapeDtypeStruct((s, qk_w), BF16), jax.ShapeDtypeStruct((s, MLA_W), BF16),
                   jax.ShapeDtypeStruct((qk_w, s), BF16),
                   jax.ShapeDtypeStruct((qk_w, s), BF16), jax.ShapeDtypeStruct((MLA_W, s), BF16),
                   jax.ShapeDtypeStruct((2, s, D_MODEL // 2), BF16)],
        compiler_params=_params(),
    )(x, wlat, b_l, g_q, wuq, g_kv, wk, wv, rc, rsl, rsh, after)


def _attn_fwd(qt, k, vt):
    s = k.shape[0]
    tk = ATT_TK
    nk = s // tk
    pairs = HEADS // 2

    def body(qt_ref, k_ref, vt_ref, o_ref, lse_ref):
        qts = [qt_ref[hh * HEAD_PAD:(hh + 1) * HEAD_PAD, :] for hh in range(2)]
        ones = jnp.ones((SUM_ROWS, tk), BF16)

        def scores(j):
            return tuple(_dot(k_ref[j * tk:(j + 1) * tk, hh * HEAD_PAD:(hh + 1) * HEAD_PAD], qts[hh][:, j * tk:])
                         for hh in range(2))

        def weighted(j, ps):
            return tuple(_dot(jnp.concatenate([vt_ref[hh * V_DIM:(hh + 1) * V_DIM, j * tk:(j + 1) * tk], ones], axis=0),
                              ps[hh]) for hh in range(2))

        def from_lane(full, lo, part):
            return part if lo == 0 else jnp.concatenate([full[:, :lo], part], axis=1)

        krow = lax.broadcasted_iota(jnp.int32, (tk, tk), 0)
        qcol = lax.broadcasted_iota(jnp.int32, (tk, tk), 1)
        st = scores(0)
        ps = None
        stats = [(jnp.full((1, s), -jnp.inf, F32), jnp.zeros((V_DIM + SUM_ROWS, s), F32))] * 2
        for j in range(nk):
            lo, lo_prev = j * tk, max(j - 1, 0) * tk
            st_next = scores(j + 1) if j + 1 < nk else None
            pvs = weighted(j - 1, ps) if j else None
            new_ps, new_stats = [], []
            for hh in range(2):
                m, acc = stats[hh]
                diag = jnp.where(krow <= qcol, st[hh][:, :tk], -jnp.inf)
                s_ = diag if j == nk - 1 else jnp.concatenate([diag, st[hh][:, tk:]], axis=1)
                if j:
                    acc = from_lane(acc, lo_prev, acc[:, lo_prev:] + pvs[hh])
                m_old = m[:, lo:]
                m_new = jnp.maximum(m_old, jnp.max(s_, axis=0, keepdims=True))
                a = jnp.exp2(m_old - m_new)
                p = jnp.exp2(s_ - m_new)
                new_stats.append((from_lane(m, lo, m_new), from_lane(acc, lo, a * acc[:, lo:])))
                new_ps.append(p.astype(BF16))
            st, ps, stats = st_next, new_ps, new_stats
        pvs = weighted(nk - 1, ps)
        lo = (nk - 1) * tk
        accs = [from_lane(stats[hh][1], lo, stats[hh][1][:, lo:] + pvs[hh]) for hh in range(2)]
        sums = [acc[V_DIM:V_DIM + 1, :] for acc in accs]
        ot = jnp.concatenate([accs[hh][:V_DIM, :] / sums[hh] for hh in range(2)], axis=0)
        o_ref[...] = ot.T
        lse = [stats[hh][0] + jnp.log(sums[hh]) * LOG2E for hh in range(2)]
        lse_ref[...] = jnp.concatenate(lse + [jnp.zeros((6, s), F32)], axis=0)

    return pl.pallas_call(
        body, name="attn_fwd", grid=(pairs,),
        in_specs=[pl.BlockSpec((2 * HEAD_PAD, s), lambda p: (p, 0)),
                  pl.BlockSpec((s, 2 * HEAD_PAD), lambda p: (0, p)),
                  pl.BlockSpec((2 * V_DIM, s), lambda p: (p, 0))],
        out_specs=[pl.BlockSpec((s, 2 * V_DIM), lambda p: (0, p)),
                   pl.BlockSpec((None, 8, s), lambda p: (p, 0, 0))],
        out_shape=[jax.ShapeDtypeStruct((s, MLA_W), F32), jax.ShapeDtypeStruct((pairs, 8, s), F32)],
        compiler_params=_params(("arbitrary",)),
    )(qt, k, vt)


def _attn_bwd(q, qt, k, kt, v, do, dot, lse, delta, after, first_pair, n_pairs, into, name):
    s = k.shape[0]
    tk = ATT_BWD_TK
    nk = s // tk

    def body(q_ref, qt_ref, k_ref, kt_ref, v_ref, do_ref, dot_ref, lse_ref, dl_ref, after_ref, *rest):
        dqt_ref, dk_ref, dv_ref = rest[-3:]
        krow = lax.broadcasted_iota(jnp.int32, (tk, tk), 0)
        qcol = lax.broadcasted_iota(jnp.int32, (tk, tk), 1)
        lane = lax.broadcasted_iota(jnp.int32, (tk, 2 * V_DIM), 1)
        drow = lax.broadcasted_iota(jnp.int32, (2 * V_DIM, s), 0)
        dotb = dot_ref[...]
        dots = [jnp.where((drow < V_DIM) if hh == 0 else (drow >= V_DIM), dotb, jnp.zeros_like(dotb))
                for hh in range(2)]
        for j in range(nk):
            lo = j * tk
            vb = v_ref[lo:lo + tk, :]
            dob = do_ref[lo:, :]
            dvs = []
            for hh in range(2):
                rows = slice(hh * HEAD_PAD, (hh + 1) * HEAD_PAD)
                st = _dot(k_ref[lo:lo + tk, rows], qt_ref[rows, lo:])
                diag = jnp.where(krow <= qcol, st[:, :tk], -jnp.inf)
                st = diag if j == nk - 1 else jnp.concatenate([diag, st[:, tk:]], axis=1)
                p = jnp.exp2(st - lse_ref[hh:hh + 1, lo:])
                dpt = _dot(vb, dots[hh][:, lo:])
                dst = (p * (dpt - dl_ref[hh:hh + 1, lo:])).astype(BF16)
                dvs.append(_dot(p.astype(BF16), dob))
                dk_ref[lo:lo + tk, rows] = (_dot(dst, q_ref[lo:, rows]) * LN2).astype(BF16)
                dqt = _dot(kt_ref[rows, lo:lo + tk], dst)
                if j == 0:
                    dqt_ref[rows, :] = dqt
                else:
                    dqt_ref[rows, lo:] += dqt
            dv_ref[lo:lo + tk, :] = jnp.where(lane < V_DIM, dvs[0], dvs[1]).astype(BF16)
        dqt_ref[...] = dqt_ref[...] * SCALE

    pair_rows = lambda w: pl.BlockSpec((s, w), lambda p: (0, p + first_pair))
    pair_cols = lambda w: pl.BlockSpec((w, s), lambda p: (p + first_pair, 0))
    stats = pl.BlockSpec((None, 8, s), lambda p: (p + first_pair, 0, 0))
    any_spec = pl.BlockSpec(memory_space=pl.ANY)
    return pl.pallas_call(
        body, name=name, grid=(n_pairs,),
        in_specs=[pair_rows(2 * HEAD_PAD), pair_cols(2 * HEAD_PAD), pair_rows(2 * HEAD_PAD), pair_cols(2 * HEAD_PAD),
                  pair_rows(2 * V_DIM), pair_rows(2 * V_DIM), pair_cols(2 * V_DIM), stats, stats, any_spec]
        + [any_spec] * len(into),
        out_specs=[pair_cols(2 * HEAD_PAD), pair_rows(2 * HEAD_PAD), pair_rows(2 * V_DIM)],
        out_shape=[jax.ShapeDtypeStruct((HEADS * HEAD_PAD, s), F32), jax.ShapeDtypeStruct((s, HEADS * HEAD_PAD), BF16),
                   jax.ShapeDtypeStruct((s, MLA_W), BF16)],
        input_output_aliases={10 + n: n for n in range(len(into))},
        compiler_params=_params(("arbitrary",)),
    )(q, qt, k, kt, v, do, dot, lse, delta, after, *into)


def _split3(a):
    hi = a.astype(BF16)
    r1 = a - hi.astype(F32)
    mid = r1.astype(BF16)
    lo = (r1 - mid.astype(F32)).astype(BF16)
    return hi, mid, lo


def _mid(x, tgt, o, hm, hg, woa, wob, wout, ln_g, ln_b, sg_g, sg_b, w_s, bsb):
    s = x.shape[0]
    ts = MID_ROW_TILE
    nsteps = s // ts
    nch = ts // CHUNK
    npair = GROUPS // 2

    def body(x_ref, t_ref, o_ref, hm_ref, hg_ref, woa_ref, wob_ref, wout_ref, lng_ref, lnb_ref, sgg_ref, sgb_ref,
             ws_ref, bsb_ref,
             dr_ref, dhg_ref, dhm_ref, do_ref, dot_ref, dl_ref,
             dwout_ref, dwoa_ref, dwob_ref, dws_ref, dbs_ref, dlng_ref, dlnb_ref, dsgg_ref, dsgb_ref, loss_ref,
             dbg_ref, dbm_ref, dbacc_ref, awout_ref, awoa_ref, awob_ref):
        i = pl.program_id(0)

        @pl.when(i == 0)
        def _():
            for r in (awout_ref, awoa_ref, awob_ref, dws_ref, dlng_ref, dlnb_ref, dsgg_ref, dsgb_ref, loss_ref,
                      dbg_ref, dbm_ref, dbacc_ref):
                r[...] = jnp.zeros_like(r)

        def emit(ref, bref, lo, val):
            n = val.shape[1]
            ref[:, lo:lo + n] = val.astype(BF16)
            bref[:, lo:lo + n] += jnp.sum(val, axis=0, keepdims=True)

        lane = lax.broadcasted_iota(jnp.int32, (CHUNK, CHUNK), 1)
        left = lane < V_DIM
        tril = lax.broadcasted_iota(jnp.int32, (CHUNK, CHUNK), 0) >= lane
        ms = [jnp.where(tril, ws_ref[g], 0.0).astype(BF16) for g in range(GROUPS)]

        z_a = hm_ref[:, 0:SGU_W].astype(F32)
        u = hm_ref[:, SGU_W:2 * SGU_W].astype(F32)
        v = hm_ref[:, 2 * SGU_W:3 * SGU_W].astype(F32)
        z_b = hm_ref[:, 3 * SGU_W:4 * SGU_W].astype(F32)
        o = o_ref[...]
        sa, dsa = _silu_and_grad(z_a)
        y_a = (o * sa).astype(BF16)
        gu, dgu = _gelu_and_grad(u)
        gv, dgv = _gelu_and_grad(v)
        mu = jnp.mean(gv, axis=-1, keepdims=True)
        vc = gv - mu
        rstd_v = lax.rsqrt(jnp.mean(vc * vc, axis=-1, keepdims=True) + LN_EPS)
        vhat = vc * rstd_v
        vn = (vhat * sgg_ref[...] + sgb_ref[...]).astype(BF16)
        rows = []
        for c in range(nch):
            blocks = []
            for p in range(npair):
                blk = vn[c * CHUNK:(c + 1) * CHUNK, p * CHUNK:(p + 1) * CHUNK]
                blocks.append(jnp.where(left, _dot(ms[2 * p], blk), _dot(ms[2 * p + 1], blk)))
            rows.append(jnp.concatenate(blocks, axis=1) + bsb_ref[...])
        mixed = jnp.concatenate(rows, axis=0)
        sgu = gu * mixed
        sb, dsb = _silu_and_grad(z_b)
        y_b = (sgu * sb).astype(BF16)
        pa = jnp.concatenate([_dot(y_a, woa_ref[k]) for k in range(N_SLABS)], axis=1)
        pb = jnp.concatenate([_dot(y_b, wob_ref[k]) for k in range(N_SLABS)], axis=1)
        sga = _sigmoid(hg_ref[:, :D_MODEL].astype(F32))
        sgb = _sigmoid(hg_ref[:, D_MODEL:].astype(F32))
        m2 = (sga * pa + sgb * pb).astype(BF16)
        r = ALPHA * x_ref[...] + _dot(m2, wout_ref[...])
        rmu = jnp.mean(r, axis=-1, keepdims=True)
        rc = r - rmu
        rstd = lax.rsqrt(jnp.mean(rc * rc, axis=-1, keepdims=True) + LN_EPS)
        xhat = rc * rstd
        y = xhat * lng_ref[...] + lnb_ref[...]
        err = y - t_ref[...]
        loss_ref[...] += jnp.full(loss_ref.shape, 0.5 / D_MODEL, F32) * jnp.sum(err * err)

        dy = err * (1.0 / D_MODEL)
        dlng_ref[...] += jnp.sum(dy * xhat, axis=0, keepdims=True)
        dlnb_ref[...] += jnp.sum(dy, axis=0, keepdims=True)
        dxh = dy * lng_ref[...]
        dr = rstd * (dxh - jnp.mean(dxh, axis=-1, keepdims=True) - xhat * jnp.mean(dxh * xhat, axis=-1, keepdims=True))
        dr_ref[...] = dr
        drb = dr.astype(BF16)
        awout_ref[...] += _dot_tn(m2, drb)
        dm2 = _dot_nt(drb, wout_ref[...])
        emit(dhg_ref, dbg_ref,0, dm2 * pa * sga * (1.0 - sga))
        emit(dhg_ref, dbg_ref,D_MODEL, dm2 * pb * sgb * (1.0 - sgb))
        dpa = (dm2 * sga).astype(BF16)
        dpb = (dm2 * sgb).astype(BF16)
        dy_a = jnp.zeros((ts, MLA_W), F32)
        dy_b = jnp.zeros((ts, SGU_W), F32)
        y_at, y_bt = y_a.T, y_b.T
        for k in range(N_SLABS):
            cols = slice(k * SLAB_W, (k + 1) * SLAB_W)
            awoa_ref[k] += _dot(y_at, dpa[:, cols])
            awob_ref[k] += _dot(y_bt, dpb[:, cols])
            dy_a = dy_a + _dot_nt(dpa[:, cols], woa_ref[k])
            dy_b = dy_b + _dot_nt(dpb[:, cols], wob_ref[k])
        dob = (dy_a * sa).astype(BF16)
        do_ref[...] = dob
        dot_ref[...] = dob.T
        head = (lax.broadcasted_iota(jnp.int32, (HEADS, MLA_W), 1) // V_DIM
                == lax.broadcasted_iota(jnp.int32, (HEADS, MLA_W), 0)).astype(BF16)
        dl = sum(_dot_nt(head, term) for term in _split3(dob.astype(F32) * o))
        for p in range(HEADS // 2):
            dl_ref[p] = jnp.concatenate([dl[2 * p:2 * p + 2], jnp.zeros((6, ts), F32)], axis=0)
        emit(dhm_ref, dbm_ref,0, dy_a * o * dsa)
        dsg = dy_b * sb
        emit(dhm_ref, dbm_ref,3 * SGU_W, dy_b * sgu * dsb)
        emit(dhm_ref, dbm_ref,SGU_W, dsg * mixed * dgu)
        dmixed = dsg * gu
        dvn_rows = []
        dbs_sum = jnp.zeros((CHUNK, SGU_W), F32)
        for c in range(nch):
            dm_c = dmixed[c * CHUNK:(c + 1) * CHUNK, :]
            dbs_sum = dbs_sum + dm_c
            blocks = []
            for p in range(npair):
                dmb = dm_c[:, p * CHUNK:(p + 1) * CHUNK].astype(BF16)
                blk = vn[c * CHUNK:(c + 1) * CHUNK, p * CHUNK:(p + 1) * CHUNK]
                blocks.append(jnp.where(left, _dot_tn(ms[2 * p], dmb), _dot_tn(ms[2 * p + 1], dmb)))
                zero = jnp.zeros_like(dmb)
                dws_ref[2 * p] += jnp.where(tril, _dot_nt(jnp.where(left, dmb, zero), blk), 0.0)
                dws_ref[2 * p + 1] += jnp.where(tril, _dot_nt(jnp.where(left, zero, dmb), blk), 0.0)
            dvn_rows.append(jnp.concatenate(blocks, axis=1))
        dbacc_ref[...] += dbs_sum
        dvn = jnp.concatenate(dvn_rows, axis=0)
        dsgg_ref[...] += jnp.sum(dvn * vhat, axis=0, keepdims=True)
        dsgb_ref[...] += jnp.sum(dvn, axis=0, keepdims=True)
        dvh = dvn * sgg_ref[...]
        dgv_in = rstd_v * (dvh - jnp.mean(dvh, axis=-1, keepdims=True)
                           - vhat * jnp.mean(dvh * vhat, axis=-1, keepdims=True))
        emit(dhm_ref, dbm_ref,2 * SGU_W, dgv_in * dgv)

        @pl.when(i == nsteps - 1)
        def _():
            dwout_ref[...] = awout_ref[...].astype(BF16)
            dwoa_ref[...] = awoa_ref[...].astype(BF16)
            dwob_ref[...] = awob_ref[...].astype(BF16)
            grp = (lax.broadcasted_iota(jnp.int32, (SGU_W, CHUNK), 0) // V_DIM
                   == lax.broadcasted_iota(jnp.int32, (SGU_W, CHUNK), 1)).astype(BF16)
            hi, mid, lo = _split3(dbacc_ref[...])
            dbs_ref[...] = _dot(hi, grp) + _dot(mid, grp) + _dot(lo, grp)

    acc_shapes = [(D_MODEL, D_MODEL), woa.shape, wob.shape, (GROUPS, CHUNK, CHUNK), (CHUNK, CHUNK),
                  (1, D_MODEL), (1, D_MODEL), (1, SGU_W), (1, SGU_W), (1, 128), (1, GATE_W), (1, MID_W)]
    col_spec = lambda rows: pl.BlockSpec((rows, ts), lambda i: (0, i))
    return pl.pallas_call(
        body, name="mid", grid=(nsteps,),
        in_specs=[_row_spec(ts, D_MODEL), _row_spec(ts, D_MODEL), _row_spec(ts, MLA_W), _row_spec(ts, MID_W),
                  _row_spec(ts, GATE_W), _full_spec(woa.shape), _full_spec(wob.shape), _full_spec(wout.shape),
                  _full_spec(ln_g.shape), _full_spec(ln_b.shape), _full_spec(sg_g.shape), _full_spec(sg_b.shape),
                  _full_spec(w_s.shape), _full_spec(bsb.shape)],
        out_specs=[_row_spec(ts, D_MODEL), _row_spec(ts, GATE_W), _row_spec(ts, MID_W), _row_spec(ts, MLA_W),
                   col_spec(MLA_W), pl.BlockSpec((HEADS // 2, 8, ts), lambda i: (0, 0, i))]
        + [_full_spec(sh) for sh in acc_shapes],
        out_shape=[jax.ShapeDtypeStruct((s, D_MODEL), F32), jax.ShapeDtypeStruct((s, GATE_W), BF16),
                   jax.ShapeDtypeStruct((s, MID_W), BF16), jax.ShapeDtypeStruct((s, MLA_W), BF16),
                   jax.ShapeDtypeStruct((MLA_W, s), BF16), jax.ShapeDtypeStruct((HEADS // 2, 8, s), F32)]
        + [jax.ShapeDtypeStruct(sh, BF16 if n < 3 else F32) for n, sh in enumerate(acc_shapes)],
        scratch_shapes=[pltpu.VMEM((CHUNK, SGU_W), F32)] + [pltpu.VMEM(sh, F32) for sh in acc_shapes[:3]],
        compiler_params=_params(),
    )(x, tgt, o, hm, hg, woa, wob, wout, ln_g, ln_b, sg_g, sg_b, w_s, bsb)


def _lat_bwd(dq, dk, dv, hl, rc, rsl, rsh, g_q, g_kv, wuq, wk, wv, after):
    s = dk.shape[0]
    ts = ROW_TILE
    qk_w = HEADS * HEAD_PAD

    def body(dq_ref, dk_ref, dv_ref, hl_ref, rc_ref, rsl_ref, rsh_ref, gq_ref, gkv_ref, wuq_ref, wk_ref, wv_ref,
             after_ref, dhl_ref, dhlt_ref, dwuq_ref, dwk_ref, dwv_ref, dgq_ref, dgkv_ref, dbl_ref):
        i = pl.program_id(0)

        @pl.when(i == 0)
        def _():
            for r in (dwuq_ref, dwk_ref, dwv_ref, dgq_ref, dgkv_ref, dbl_ref):
                r[...] = jnp.zeros_like(r)

        def emit(lo, val):
            vb = val.astype(BF16)
            n = val.shape[1]
            dhl_ref[:, lo:lo + n] = vb
            dhlt_ref[lo:lo + n, :] = vb.T
            dbl_ref[:, lo:lo + n] += jnp.sum(val, axis=0, keepdims=True)

        c, sl, sh = rc_ref[...], rsl_ref[...], rsh_ref[...]
        lane = lax.broadcasted_iota(jnp.int32, (ts, HEAD_PAD), 1)
        pe = (lane >= NOPE) & (lane < QK_DIM)
        dkpe = jnp.zeros((ts, HEAD_PAD), F32)
        dqu = []
        for hd in range(HEADS):
            lanes = slice(hd * HEAD_PAD, (hd + 1) * HEAD_PAD)
            dqu.append(_rope_t(dq_ref[lanes, :].T, c, sl, sh).astype(BF16))
            dkpe = dkpe + dk_ref[:, lanes]
        dqu = jnp.concatenate(dqu, axis=1)
        dkpe = _rope_t(jnp.where(pe, dkpe, 0.0), c, sl, sh)

        cq = hl_ref[:, :Q_RANK]
        rq = lax.rsqrt(jnp.mean(cq * cq, axis=-1, keepdims=True) + RMS_EPS)
        cqh = cq * rq
        cqn = (cqh * gq_ref[...]).astype(BF16)
        dwuq_ref[...] += _dot_tn(cqn, dqu)
        dcqn = _dot_nt(dqu, wuq_ref[...])
        dgq_ref[...] += jnp.sum(dcqn * cqh, axis=0, keepdims=True)
        dch = dcqn * gq_ref[...]
        emit(0, rq * (dch - cqh * jnp.mean(dch * cqh, axis=-1, keepdims=True)))

        ckv = hl_ref[:, Q_RANK:Q_RANK + KV_RANK]
        rk = lax.rsqrt(jnp.mean(ckv * ckv, axis=-1, keepdims=True) + RMS_EPS)
        ckh = ckv * rk
        ckn = (ckh * gkv_ref[...]).astype(BF16)
        dkb = dk_ref[...].astype(BF16)
        dvb = dv_ref[...].astype(BF16)
        dwk_ref[...] += _dot_tn(ckn, dkb)
        dwv_ref[...] += _dot_tn(ckn, dvb)
        dckn = _dot_nt(dkb, wk_ref[...]) + _dot_nt(dvb, wv_ref[...])
        dgkv_ref[...] += jnp.sum(dckn * ckh, axis=0, keepdims=True)
        dkh = dckn * gkv_ref[...]
        emit(Q_RANK, rk * (dkh - ckh * jnp.mean(dkh * ckh, axis=-1, keepdims=True)))
        emit(Q_RANK + KV_RANK, dkpe)

    acc_shapes = [wuq.shape, wk.shape, wv.shape, g_q.shape, g_kv.shape, (1, LAT_W)]
    return pl.pallas_call(
        body, name="lat_bwd", grid=(s // ts,),
        in_specs=[pl.BlockSpec((qk_w, ts), lambda i: (0, i)), _row_spec(ts, qk_w), _row_spec(ts, MLA_W),
                  _row_spec(ts, LAT_W), _row_spec(ts, HEAD_PAD), _row_spec(ts, HEAD_PAD), _row_spec(ts, HEAD_PAD),
                  _full_spec(g_q.shape), _full_spec(g_kv.shape), _full_spec(wuq.shape), _full_spec(wk.shape),
                  _full_spec(wv.shape), pl.BlockSpec(memory_space=pl.ANY)],
        out_specs=[_row_spec(ts, LAT_W), pl.BlockSpec((LAT_W, ts), lambda i: (0, i))]
        + [_full_spec(sh) for sh in acc_shapes],
        out_shape=[jax.ShapeDtypeStruct((s, LAT_W), BF16), jax.ShapeDtypeStruct((LAT_W, s), BF16)]
        + [jax.ShapeDtypeStruct(sh, F32) for sh in acc_shapes],
        compiler_params=_params(),
    )(dq, dk, dv, hl, rc, rsl, rsh, g_q, g_kv, wuq, wk, wv, after)


def _dx(dr, dhg, dhm, dhl, wt, wlat, after):
    s = dr.shape[0]
    ts = MATMUL_ROW_TILE

    tk = D_MODEL
    bounds = ([(ROW_GATE + r0, ROW_GATE + r0 + tk) for r0 in range(0, GATE_W, tk)]
              + [(LAT_COLS + r0, LAT_COLS + r0 + tk) for r0 in range(0, MID_W, tk)])

    def body(dr_ref, dhg_ref, dhm_ref, dhl_ref, wt_hbm, wlat_ref, after_ref, dx_ref, wt_ref, sems):
        copies = [pltpu.make_async_copy(wt_hbm.at[lo:hi], wt_ref.at[lo:hi], sems.at[n])
                  for n, (lo, hi) in enumerate(bounds)]

        def compute(first):
            acc = (ALPHA * dr_ref[...] + _dot(dhl_ref[:, 0:Q_RANK + KV_RANK], wlat_ref[0:Q_RANK + KV_RANK, :])
                   + _dot(dhl_ref[:, Q_RANK + KV_RANK:], _kpe_rows(wlat_ref)))
            for n, (lo, hi) in enumerate(bounds):
                if first:
                    copies[n].wait()
                if lo >= ROW_GATE:
                    acc += _dot(dhg_ref[:, lo - ROW_GATE:hi - ROW_GATE], wt_ref[lo:hi, :])
                else:
                    acc += _dot(dhm_ref[:, lo - LAT_COLS:hi - LAT_COLS], wt_ref[lo:hi, :])
            dx_ref[...] = acc

        @pl.when(pl.program_id(0) == 0)
        def _():
            for cp in copies:
                cp.start()
            compute(True)

        @pl.when(pl.program_id(0) > 0)
        def _():
            compute(False)

    return pl.pallas_call(
        body, name="dx", grid=(s // ts,),
        in_specs=[_row_spec(ts, D_MODEL), _row_spec(ts, GATE_W), _row_spec(ts, MID_W), _row_spec(ts, LAT_W),
                  HBM_SPEC, _full_spec(wlat.shape), HBM_SPEC],
        out_specs=_row_spec(ts, D_MODEL),
        out_shape=jax.ShapeDtypeStruct((s, D_MODEL), F32),
        scratch_shapes=[pltpu.VMEM(wt.shape, BF16), pltpu.SemaphoreType.DMA((len(bounds),))],
        compiler_params=_params(),
    )(dr, dhg, dhm, dhl, wt, wlat, after)


def _dwt_early(dhmt, dhgt, xb2, col, after):
    tn = 512
    nm, ng = MID_W // tn, GATE_W // tn
    s = dhmt.shape[0]
    hc = D_MODEL // 2

    ks = s // 2

    def body(col_ref, dma_ref, dmb_ref, dga_ref, dgb_ref, xoa_ref, xob_ref, xca_ref, xcb_ref, after_ref,
             dwo_ref, dwc_ref):
        i = pl.program_id(0)

        def both(da_ref, db_ref):
            da, db = da_ref[...], db_ref[...]
            dwo_ref[...] = (_dot_tn(da, xoa_ref[...]) + _dot_tn(db, xob_ref[...])).astype(BF16)
            dwc_ref[...] = (_dot_tn(da, xca_ref[...]) + _dot_tn(db, xcb_ref[...])).astype(BF16)

        @pl.when(i < nm)
        def _():
            both(dma_ref, dmb_ref)

        @pl.when(i >= nm)
        def _():
            both(dga_ref, dgb_ref)

    def dh_spec(first, part):
        if first:
            return pl.BlockSpec((ks, tn), lambda i, col_ref: (part, jnp.minimum(i, nm - 1)))
        return pl.BlockSpec((ks, tn), lambda i, col_ref: (part, jnp.maximum(i - nm, 0)))

    def x_spec(other, part):
        if other:
            return pl.BlockSpec((None, ks, hc), lambda i, col_ref: (1 - col_ref[0], part, 0))
        return pl.BlockSpec((None, ks, hc), lambda i, col_ref: (col_ref[0], part, 0))

    out_spec = pl.BlockSpec((pl.Element(tn), pl.Element(hc)),
                            lambda i, col_ref: (pl.multiple_of(LAT_COLS + i * tn, 32), 0))
    rows = pl.pallas_call(
        body, name="dwt_early",
        grid_spec=pltpu.PrefetchScalarGridSpec(
            num_scalar_prefetch=1, grid=(nm + ng,),
            in_specs=[dh_spec(True, 0), dh_spec(True, 1), dh_spec(False, 0), dh_spec(False, 1),
                      x_spec(True, 0), x_spec(True, 1), x_spec(False, 0), x_spec(False, 1),
                      pl.BlockSpec(memory_space=pl.ANY)],
            out_specs=[out_spec, out_spec]),
        out_shape=[jax.ShapeDtypeStruct((IN_W, hc), BF16)] * 2,
        compiler_params=_params(),
    )(col, dhmt, dhmt, dhgt, dhgt, xb2, xb2, xb2, xb2, after)

    def zero(other_ref, mine_ref, other_out, mine_out):
        other_out[...] = jnp.zeros_like(other_out)
        mine_out[...] = jnp.zeros_like(mine_out)

    lat_rows = pl.BlockSpec((LAT_COLS, hc), lambda i: (0, 0))
    return pl.pallas_call(
        zero, name="dwt_early_zero_lat", grid=(1,), in_specs=[pl.BlockSpec(memory_space=pl.ANY)] * 2,
        out_specs=[lat_rows, lat_rows],
        out_shape=[jax.ShapeDtypeStruct((IN_W, hc), BF16)] * 2, input_output_aliases={0: 0, 1: 1},
    )(*rows)


def _dwt_lat(dhlt, xb2):
    n, s = dhlt.shape
    tk = MATMUL_ROW_TILE
    nsteps = s // tk

    def body(dht_ref, xb_ref, dw_ref, acc_ref):
        i = pl.program_id(0)

        @pl.when(i == 0)
        def _():
            acc_ref[...] = jnp.zeros_like(acc_ref)

        dht = dht_ref[...]
        acc_ref[...] += jnp.concatenate([_dot(dht, xb_ref[0]), _dot(dht, xb_ref[1])], axis=1)

        @pl.when(i == nsteps - 1)
        def _():
            kpe = Q_RANK + KV_RANK + NOPE
            dw_ref[0:Q_RANK + KV_RANK, :] = acc_ref[0:Q_RANK + KV_RANK, :].astype(BF16)
            dw_ref[Q_RANK + KV_RANK:LAT_COLS, :] = acc_ref[kpe:kpe + ROPE, :].astype(BF16)
            dw_ref[LAT_COLS:, :] = jnp.zeros((LAT_ROWS_PAD - LAT_COLS, D_MODEL), BF16)

    return pl.pallas_call(
        body, name="dwt_lat", grid=(nsteps,),
        in_specs=[pl.BlockSpec((n, tk), lambda i: (0, i)), pl.BlockSpec((2, tk, D_MODEL // 2), lambda i: (0, i, 0))],
        out_specs=_full_spec((LAT_ROWS_PAD, D_MODEL)),
        out_shape=jax.ShapeDtypeStruct((LAT_ROWS_PAD, D_MODEL), BF16),
        scratch_shapes=[pltpu.VMEM((n, D_MODEL), F32)],
        compiler_params=_params(),
    )(dhlt, xb2)


def _split_bias(b):
    z = lambda n: jnp.zeros((n,), b.dtype)
    lat = jnp.concatenate([b[:Q_RANK + KV_RANK], z(NOPE), b[Q_RANK + KV_RANK:LAT_COLS], z(HEAD_PAD - QK_DIM)])
    return b[None, ROW_GATE:], b[None, LAT_COLS:ROW_GATE], lat[None, :]


def _join_bias(g, m, l):
    kpe = Q_RANK + KV_RANK + NOPE
    return jnp.concatenate([l[0, :Q_RANK + KV_RANK], l[0, kpe:kpe + ROPE], m[0], g[0]])


def _rope_tables(positions):
    half = ROPE // 2
    inv_freq = ROPE_THETA ** (-jnp.arange(0, ROPE, 2, dtype=F32) / ROPE)
    ang = positions.astype(F32)[:, None] * inv_freq
    cos, sin = jnp.cos(ang), jnp.sin(ang)
    n = positions.shape[0]
    one, zero = jnp.ones((n, NOPE), F32), jnp.zeros((n, half), F32)
    tail1, tail0 = jnp.ones((n, HEAD_PAD - QK_DIM), F32), jnp.zeros((n, HEAD_PAD - QK_DIM), F32)
    z64 = jnp.zeros((n, NOPE), F32)
    rc = jnp.concatenate([one, cos, cos, tail1], axis=1)
    rsl = jnp.concatenate([z64, -sin, zero, tail0], axis=1)
    rsh = jnp.concatenate([z64, zero, sin, tail0], axis=1)
    return rc, rsl, rsh


def _pad_heads(w_uq):
    return jnp.pad(w_uq, ((0, 0), (0, 0), (0, HEAD_PAD - QK_DIM))).reshape(w_uq.shape[0], HEADS * HEAD_PAD)


def _prep_local(b_in, g_q, g_kv, w_ukv):
    b_g, b_m, b_l = _split_bias(b_in)
    wk = jnp.pad(w_ukv[:, :, :NOPE], ((0, 0), (0, 0), (0, HEAD_PAD - NOPE))).reshape(KV_RANK, HEADS * HEAD_PAD).astype(BF16)
    wv = w_ukv[:, :, NOPE:].reshape(KV_RANK, MLA_W).astype(BF16)
    return dict(b_g=b_g, b_m=b_m, b_l=b_l, wk=wk, wv=wv, gq2=g_q[None, :], gkv2=g_kv[None, :])


def _local_attention(x, tables, wlat, prep, wuq, after):
    rc, rsl, rsh = tables
    b_g, b_m, b_l, wk, wv, gq2, gkv2 = (prep[n] for n in ("b_g", "b_m", "b_l", "wk", "wv", "gq2", "gkv2"))
    hl, q, k, v, qt, kt, vt, xb2 = _fwd_lat(x, wlat, b_l, gq2, wuq, gkv2, wk, wv, rc, rsl, rsh, after)
    o, lse = _attn_fwd(qt, k, vt)
    return dict(q=q, qt=qt, k=k, kt=kt, v=v, o=o, lse=lse, hl=hl, rc=rc, rsl=rsl, rsh=rsh, gq2=gq2, gkv2=gkv2,
                wuq=wuq, wk=wk, wv=wv, xb2=xb2, b_g=b_g, b_m=b_m, wlat=wlat)


def _bias_lanes(b_s):
    return jnp.repeat(b_s.T, V_DIM, axis=1)


def _local_head(st, x, tgt, wt, w_oa, sg_g, sg_b, w_s, bsb, w_ob, w_out, ln_g, ln_b):
    q, qt, k, kt, v, o, lse, hl = (st[n] for n in ("q", "qt", "k", "kt", "v", "o", "lse", "hl"))
    rc, rsl, rsh, gq2, gkv2, wuq, wk, wv = (st[n] for n in ("rc", "rsl", "rsh", "gq2", "gkv2", "wuq", "wk", "wv"))
    hg, hm = _fwd_rest(st["xb2"], wt, st["b_g"], st["b_m"])
    (dr, dhg, dhm, do, dot, delta, dwout, dwoa, dwob, dws, dbs, dlng, dlnb, dsgg, dsgb, loss, dbg,
     dbm) = _mid(x, tgt, o, hm, hg, w_oa, w_ob, w_out, ln_g[None, :], ln_b[None, :], sg_g[None, :], sg_b[None, :],
                 w_s, bsb)
    early = {
        "w_oa": dwoa, "sgu_ln_g": dsgg[0], "sgu_ln_b": dsgb[0], "w_s": dws, "b_s": dbs[:, :GROUPS].T,
        "w_ob": dwob, "w_out": dwout, "ln_g": dlng[0], "ln_b": dlnb[0],
    }
    state = dict(q=q, qt=qt, k=k, kt=kt, v=v, do=do, dot=dot, lse=lse, delta=delta, hl=hl, rc=rc, rsl=rsl, rsh=rsh,
                 gq2=gq2, gkv2=gkv2, wuq=wuq, wk=wk, wv=wv, dr=dr, dhg=dhg, dhm=dhm, wt=wt, dbg=dbg, dbm=dbm,
                 xb2=st["xb2"], wlat=st["wlat"])
    return loss, early, state


def _local_attn_bwd(st, after, first_pair=0, n_pairs=HEADS // 2, into=(), name="attn_bwd"):
    return _attn_bwd(st["q"], st["qt"], st["k"], st["kt"], st["v"], st["do"], st["dot"], st["lse"], st["delta"],
                     after, first_pair, n_pairs, into, name)


def _local_tail(st, dq, dk, dv, after):
    dhl, dhlt, dwuq, dwk, dwv, dgq, dgkv, dbl = _lat_bwd(dq, dk, dv, st["hl"], st["rc"], st["rsl"], st["rsh"],
                                                         st["gq2"], st["gkv2"], st["wuq"], st["wk"], st["wv"], after)
    late = {
        "w_lat": _dwt_lat(dhlt, st["xb2"]),
        "b_in": _join_bias(st["dbg"], st["dbm"], dbl),
        "g_q": dgq[0],
        "w_uq": dwuq.reshape(Q_RANK, HEADS, HEAD_PAD)[:, :, :QK_DIM],
        "g_kv": dgkv[0],
        "w_ukv": jnp.concatenate([dwk.reshape(KV_RANK, HEADS, HEAD_PAD)[:, :, :NOPE],
                                  dwv.reshape(KV_RANK, HEADS, V_DIM)], axis=2),
    }
    return dhl, late


def _local_step(x, positions, tgt, wt, b_in, g_q, w_uq, g_kv, w_ukv, w_oa, sg_g, sg_b, w_s, b_s, w_ob, w_out, ln_g,
                ln_b):
    st = _local_attention(x, _rope_tables(positions), wt[:LAT_COLS], _prep_local(b_in, g_q, g_kv, w_ukv),
                          _pad_heads(w_uq).astype(BF16), b_in)
    loss, early, st = _local_head(st, x, tgt, wt, w_oa, sg_g, sg_b, w_s, _bias_lanes(b_s), w_ob, w_out, ln_g, ln_b)
    dq, dk, dv = _local_attn_bwd(st, loss)
    dhl, late = _local_tail(st, dq, dk, dv, dv)
    dx = _dx(st["dr"], st["dhg"], st["dhm"], dhl, st["wt"], st["wlat"], dhl)
    grads = {**early, **late}
    right, left = _dwt_early(st["dhm"], st["dhg"], st["xb2"], jnp.zeros((1,), jnp.int32), dhl)
    grads["w_in"] = jnp.concatenate([grads.pop("w_lat")[:LAT_COLS], jnp.concatenate([left, right], axis=1)[LAT_COLS:]],
                                    axis=0)
    return loss, dx, grads


MESH = pl.DeviceIdType.MESH
N_CHIPS = 4
HBM_SPEC = pl.BlockSpec(memory_space=pl.ANY)
HBM_SPEC_STRICT = pl.BlockSpec(memory_space=pltpu.HBM)
VMEM_SPEC = pl.BlockSpec(memory_space=pltpu.VMEM)

REP_ROWS = 80


def _rows8(a):
    flat = a.reshape(-1)
    n = -(-flat.shape[0] // (8 * D_MODEL)) * 8 * D_MODEL
    return jnp.pad(flat, (0, n - flat.shape[0])).reshape(-1, D_MODEL)


def _place():
    x, y, c = lax.axis_index("x"), lax.axis_index("y"), lax.axis_index("c")
    others = [(1 - x, y), (x, 1 - y), (1 - x, 1 - y)]
    return x, y, c, others


N_DEV = 8
LOSS_TILE = (8, 128)


def _cast_own(shards, me, after):
    n = len(shards)

    def body(me_ref, *refs):
        for w in range(n):
            refs[n + 1 + w][...] = refs[w][...].astype(BF16)

    return pl.pallas_call(
        body, name="cast_own",
        grid_spec=pltpu.PrefetchScalarGridSpec(
            num_scalar_prefetch=1, grid=(1,),
            in_specs=[pl.BlockSpec(s.shape, lambda i, me_ref: (0, 0)) for s in shards]
            + [pl.BlockSpec(memory_space=pl.ANY)],
            out_specs=[pl.BlockSpec((None,) + s.shape, lambda i, me_ref: (me_ref[0], 0, 0)) for s in shards]),
        out_shape=[jax.ShapeDtypeStruct((N_CHIPS,) + s.shape, BF16) for s in shards],
        compiler_params=pltpu.CompilerParams(vmem_limit_bytes=VMEM_LIMIT),
    )(me, *shards, after)


def _cast_first(lat, uq, me):
    def body(me_ref, lat_ref, uq_ref, wlat_ref, guq_ref):
        wlat_ref[...] = lat_ref[...].astype(BF16)
        guq_ref[...] = uq_ref[...].astype(BF16)

    return pl.pallas_call(
        body, name="cast_first",
        grid_spec=pltpu.PrefetchScalarGridSpec(
            num_scalar_prefetch=1, grid=(1,),
            in_specs=[pl.BlockSpec((LAT_COLS, D_MODEL), lambda i, me_ref: (0, 0)),
                      pl.BlockSpec(uq.shape, lambda i, me_ref: (0, 0))],
            out_specs=[pl.BlockSpec((LAT_COLS, D_MODEL), lambda i, me_ref: (0, 0)),
                       pl.BlockSpec((None,) + uq.shape, lambda i, me_ref: (me_ref[0], 0, 0))]),
        out_shape=[jax.ShapeDtypeStruct((LAT_COLS, D_MODEL), BF16),
                   jax.ShapeDtypeStruct((N_CHIPS,) + uq.shape, BF16)],
    )(me, lat, uq)


def _first_copies(wlat_ref, guq_ref, send_sems, recv_sems, shapes):
    x, y, c, others = _place()
    me = 2 * x + y
    hl, hu = shapes[0][1] // 2, shapes[1][2] // 2
    lat_half = wlat_ref.at[:, pl.ds(c * hl, hl)]

    def copy(src, dst, k, to):
        return pltpu.make_async_remote_copy(src_ref=src, dst_ref=dst, send_sem=send_sems.at[k],
                                            recv_sem=recv_sems.at[k], device_id=to, device_id_type=MESH)

    def uq_half(chip):
        return guq_ref.at[chip, :, pl.ds(c * hu, hu)]

    lat_out = [copy(lat_half, lat_half, j, (*others[j], c)) for j in range(3)]
    uq_out = [copy(uq_half(me), uq_half(me), 3 + j, (*others[j], c)) for j in range(3)]
    j0 = jnp.maximum(x + 2 * y - 1, 0)
    lat_in = copy(lat_half, lat_half, j0, (0, 0, c))
    uq_in = [copy(uq_half(me), uq_half(2 * px + py), 3 + j, (px, py, c)) for j, (px, py) in enumerate(others)]
    return me, lat_out, uq_out, lat_in, uq_in


def _first_start(wlat, guq):
    shapes = (wlat.shape, guq.shape)

    def body(wlat_ref, guq_ref, send_sems, recv_sems, wlat_thru, guq_thru, token):
        me, lat_out, uq_out, _, _ = _first_copies(wlat_ref, guq_ref, send_sems, recv_sems, shapes)

        @pl.when(me == 0)
        def _():
            for cp in lat_out:
                cp.start()

        for cp in uq_out:
            cp.start()
        token[...] = jnp.zeros_like(token)

    outs = pl.pallas_call(
        body, name="first_start",
        out_shape=(pltpu.SemaphoreType.DMA((6,)), pltpu.SemaphoreType.DMA((6,)), pltpu.HBM(wlat.shape, BF16),
                   pltpu.HBM(guq.shape, BF16), jax.ShapeDtypeStruct(LOSS_TILE, F32)),
        in_specs=[HBM_SPEC_STRICT] * 2, out_specs=(SEM_SPEC, SEM_SPEC, HBM_SPEC_STRICT, HBM_SPEC_STRICT, VMEM_SPEC),
        input_output_aliases={0: 2, 1: 3},
        compiler_params=pltpu.CompilerParams(has_side_effects=SPLIT_EFFECT),
    )(pltpu.with_memory_space_constraint(wlat, pltpu.HBM), pltpu.with_memory_space_constraint(guq, pltpu.HBM))
    return outs


def _first_wait(send_sems, recv_sems, wlat, guq, *after):
    shapes = (wlat.shape, guq.shape)

    def body(wlat_ref, guq_ref, send_sems, recv_sems, *rest):
        me, lat_out, uq_out, lat_in, uq_in = _first_copies(wlat_ref, guq_ref, send_sems, recv_sems, shapes)

        @pl.when(me == 0)
        def _():
            for cp in lat_out:
                cp.wait_send()

        @pl.when(me != 0)
        def _():
            lat_in.wait_recv()

        for cp in uq_out:
            cp.wait_send()
        for cp in uq_in:
            cp.wait_recv()

    return pl.pallas_call(
        body, name="first_wait", out_shape=(pltpu.HBM(wlat.shape, BF16), pltpu.HBM(guq.shape, BF16)),
        in_specs=[HBM_SPEC_STRICT, HBM_SPEC_STRICT, SEM_SPEC, SEM_SPEC] + [HBM_SPEC] * len(after),
        out_specs=(HBM_SPEC_STRICT, HBM_SPEC_STRICT), input_output_aliases={0: 0, 1: 1},
        compiler_params=pltpu.CompilerParams(has_side_effects=SPLIT_EFFECT),
    )(wlat, guq, send_sems, recv_sems, *after)


def _first_forward(wlat, guq):
    hl, hu = wlat.shape[1] // 2, guq.shape[2] // 2

    def body(wlat_in, guq_in, wlat_ref, guq_ref, send_sems, recv_sems):
        x, y, c, others = _place()
        me = 2 * x + y
        sibling = (x, y, 1 - c)

        def copy(part, k):
            return pltpu.make_async_remote_copy(src_ref=part, dst_ref=part, send_sem=send_sems.at[k],
                                                recv_sem=recv_sems.at[k], device_id=sibling, device_id_type=MESH)

        def uq_part(j, half):
            px, py = others[j]
            return guq_ref.at[2 * px + py, :, pl.ds(half * hu, hu)]

        cps = [copy(uq_part(j, c), j) for j in range(3)]
        for cp in cps:
            cp.start()

        @pl.when(me != 0)
        def _():
            mine = copy(wlat_ref.at[:, pl.ds(c * hl, hl)], 3)
            mine.start()
            copy(wlat_ref.at[:, pl.ds((1 - c) * hl, hl)], 3).wait_recv()
            mine.wait_send()

        for j in range(3):
            copy(uq_part(j, 1 - c), j).wait_recv()
        for cp in cps:
            cp.wait_send()

    return pl.pallas_call(
        body, name="first_forward", in_specs=[HBM_SPEC, HBM_SPEC], out_specs=[HBM_SPEC, HBM_SPEC],
        out_shape=[jax.ShapeDtypeStruct(wlat.shape, BF16), jax.ShapeDtypeStruct(guq.shape, BF16)],
        input_output_aliases={0: 0, 1: 1},
        scratch_shapes=[pltpu.SemaphoreType.DMA((4,)), pltpu.SemaphoreType.DMA((4,))],
    )(wlat, guq)


def _gather_copy(b_ref, buf, w, j, src_chip, dst_chip, to, rows, send_sems, recv_sems):
    _, _, c, _ = _place()
    hc = _half(buf)
    return pltpu.make_async_remote_copy(
        src_ref=b_ref.at[src_chip, rows, pl.ds(c * hc, hc)], dst_ref=b_ref.at[dst_chip, rows, pl.ds(c * hc, hc)],
        send_sem=send_sems.at[3 * w + j], recv_sem=recv_sems.at[3 * w + j], device_id=to, device_id_type=MESH)


def _gather_rows(buf, w, chip, fn):
    if w != 0:
        fn(slice(None))
        return
    pl.when(chip == 0)(lambda: fn(pl.ds(LAT_COLS, buf.shape[1] - LAT_COLS)))
    pl.when(chip != 0)(lambda: fn(slice(None)))


def _gather_start(bufs, after):
    n = len(bufs)

    def body(*refs):
        b_refs = refs[:n]
        send_sems, recv_sems, token = refs[n + 1], refs[n + 2], refs[-1]
        x, y, c, others = _place()
        me = 2 * x + y
        for w in range(n):
            def start(rows, w=w):
                for j, (px, py) in enumerate(others):
                    _gather_copy(b_refs[w], bufs[w], w, j, me, me, (px, py, c), rows, send_sems, recv_sems).start()
            _gather_rows(bufs[w], w, me, start)
        token[...] = jnp.zeros_like(token)

    hbm = [pltpu.HBM(b.shape, BF16) for b in bufs]
    outs = pl.pallas_call(
        body, name="gather_start",
        out_shape=(pltpu.SemaphoreType.DMA((3 * n,)), pltpu.SemaphoreType.DMA((3 * n,)), *hbm,
                   jax.ShapeDtypeStruct(LOSS_TILE, F32)),
        in_specs=[HBM_SPEC_STRICT] * n + [HBM_SPEC],
        out_specs=(SEM_SPEC, SEM_SPEC, *[HBM_SPEC_STRICT] * n, VMEM_SPEC),
        input_output_aliases={i: 2 + i for i in range(n)},
        compiler_params=pltpu.CompilerParams(has_side_effects=SPLIT_EFFECT),
    )(*[pltpu.with_memory_space_constraint(b, pltpu.HBM) for b in bufs], after)
    return outs[0], outs[1], list(outs[2:2 + n]), outs[-1]


def _gather_wait(send_sems, recv_sems, bufs, after):
    n = len(bufs)

    def body(*refs):
        b_refs = refs[:n]
        send_sems, recv_sems = refs[n], refs[n + 1]
        x, y, c, others = _place()
        me = 2 * x + y
        for w in range(n):
            for j, (px, py) in enumerate(others):
                def copy(rows, w=w, j=j, px=px, py=py):
                    return _gather_copy(b_refs[w], bufs[w], w, j, me, 2 * px + py, (px, py, c), rows, send_sems,
                                        recv_sems)
                _gather_rows(bufs[w], w, me, lambda rows, copy=copy: copy(rows).wait_send())
                _gather_rows(bufs[w], w, 2 * px + py, lambda rows, copy=copy: copy(rows).wait_recv())

    outs = pl.pallas_call(
        body, name="gather_wait", out_shape=tuple(pltpu.HBM(b.shape, b.dtype) for b in bufs),
        in_specs=[HBM_SPEC_STRICT] * n + [SEM_SPEC, SEM_SPEC, HBM_SPEC],
        out_specs=tuple([HBM_SPEC_STRICT] * n), input_output_aliases={i: i for i in range(n)},
        compiler_params=pltpu.CompilerParams(has_side_effects=SPLIT_EFFECT),
    )(*bufs, send_sems, recv_sems, after)
    return list(outs)


def _gather_finish(bufs):
    n = len(bufs)

    def body(*refs):
        b_refs = refs[n:2 * n]
        send_sems, recv_sems = refs[2 * n:]
        x, y, c, others = _place()
        cps = []
        for w in range(n):
            hc = _half(bufs[w])
            for j, (px, py) in enumerate(others):
                part = b_refs[w].at[2 * px + py, :, pl.ds(c * hc, hc)]
                cps.append(pltpu.make_async_remote_copy(
                    src_ref=part, dst_ref=part, send_sem=send_sems.at[3 * w + j], recv_sem=recv_sems.at[3 * w + j],
                    device_id=(x, y, 1 - c), device_id_type=MESH))
        for cp in cps:
            cp.start()
        for w in range(n):
            hc = _half(bufs[w])
            for j, (px, py) in enumerate(others):
                theirs = b_refs[w].at[2 * px + py, :, pl.ds((1 - c) * hc, hc)]
                pltpu.make_async_remote_copy(
                    src_ref=theirs, dst_ref=theirs, send_sem=send_sems.at[3 * w + j], recv_sem=recv_sems.at[3 * w + j],
                    device_id=(x, y, 1 - c), device_id_type=MESH).wait_recv()
        for cp in cps:
            cp.wait_send()

    return pl.pallas_call(
        body, name="gather_finish", in_specs=[HBM_SPEC] * n, out_specs=[HBM_SPEC] * n,
        out_shape=[jax.ShapeDtypeStruct(b.shape, b.dtype) for b in bufs],
        input_output_aliases={i: i for i in range(n)},
        scratch_shapes=[pltpu.SemaphoreType.DMA((3 * n,)), pltpu.SemaphoreType.DMA((3 * n,))],
    )(*bufs)


def _half(a):
    return a.shape[-1] // 2


def _exchange_pairs(parts, name):
    n = len(parts)

    def body(*refs):
        p_refs, r_refs = refs[:n], refs[n:2 * n]
        send_sems, recv_sems = refs[2 * n:]
        x, y, c, _ = _place()
        cps = []
        for w in range(n):
            h = _half(parts[w])
            cps.append(pltpu.make_async_remote_copy(
                src_ref=p_refs[w].at[:, :, pl.ds((1 - c) * h, h)], dst_ref=r_refs[w],
                send_sem=send_sems.at[w], recv_sem=recv_sems.at[w], device_id=(x, y, 1 - c), device_id_type=MESH))
        for cp in cps:
            cp.start()
        for cp in cps:
            cp.wait()

    return pl.pallas_call(
        body, name=name, in_specs=[HBM_SPEC] * n, out_specs=[HBM_SPEC] * n,
        out_shape=[jax.ShapeDtypeStruct((N_CHIPS, p.shape[1], _half(p)), BF16) for p in parts],
        scratch_shapes=[pltpu.SemaphoreType.DMA((n,)), pltpu.SemaphoreType.DMA((n,))],
    )(*parts)


def _sibling_part(ref, w, n_whole, shape, c):
    if w < n_whole:
        return ref
    h = shape[-1] // 2
    return ref.at[:, :, pl.ds((1 - c) * h, h)]


def _pairs_start(parts, all_loss, n_whole):
    n = len(parts)

    def body(*refs):
        p_refs, r_refs, loss_ref = refs[:n], refs[n:2 * n], refs[2 * n]
        send_sems, recv_sems, token = refs[2 * n + 1], refs[2 * n + 2], refs[-1]
        x, y, c, _ = _place()
        for w in range(n):
            h = _half(parts[w])
            pltpu.make_async_remote_copy(
                src_ref=_sibling_part(p_refs[w], w, n_whole, parts[w].shape, c), dst_ref=r_refs[w],
                send_sem=send_sems.at[w], recv_sem=recv_sems.at[w], device_id=(x, y, 1 - c),
                device_id_type=MESH).start()
        me = 4 * x + 2 * y + c
        for t in range(1, N_DEV):
            d = (me + t) % N_DEV
            pltpu.make_async_remote_copy(
                src_ref=loss_ref.at[me], dst_ref=loss_ref.at[me], send_sem=send_sems.at[n + t - 1],
                recv_sem=recv_sems.at[n + t - 1], device_id=(d // 4, (d // 2) % 2, d % 2), device_id_type=MESH).start()
        token[...] = jnp.zeros_like(token)

    lands = [pltpu.HBM(p.shape if w < n_whole else (N_CHIPS, p.shape[1], _half(p)), BF16)
             for w, p in enumerate(parts)]
    nsem = n + N_DEV - 1
    outs = pl.pallas_call(
        body, name="pairs_start",
        out_shape=(pltpu.SemaphoreType.DMA((nsem,)), pltpu.SemaphoreType.DMA((nsem,)),
                   *[pltpu.HBM(p.shape, p.dtype) for p in parts], *lands, pltpu.HBM(all_loss.shape, F32),
                   jax.ShapeDtypeStruct(LOSS_TILE, F32)),
        in_specs=[HBM_SPEC_STRICT] * (2 * n + 1),
        out_specs=(SEM_SPEC, SEM_SPEC, *[HBM_SPEC_STRICT] * (2 * n + 1), VMEM_SPEC),
        input_output_aliases={i: 2 + i for i in range(2 * n + 1)},
        compiler_params=pltpu.CompilerParams(has_side_effects=SPLIT_EFFECT),
    )(*[pltpu.with_memory_space_constraint(p, pltpu.HBM) for p in parts],
      *[pltpu.with_memory_space_constraint(lax.empty(l.shape, BF16), pltpu.HBM) for l in lands],
      pltpu.with_memory_space_constraint(all_loss, pltpu.HBM))
    return outs[0], outs[1], list(outs[2:2 + n]), list(outs[2 + n:2 + 2 * n]), outs[2 + 2 * n], outs[-1]


def _pairs_wait(send_sems, recv_sems, parts, lands, all_loss, after, n_whole):
    n = len(parts)

    def body(*refs):
        p_refs, r_refs, loss_ref = refs[:n], refs[n:2 * n], refs[2 * n]
        send_sems, recv_sems = refs[2 * n + 1], refs[2 * n + 2]
        x, y, c, _ = _place()
        for w in range(n):
            h = _half(parts[w])
            cp = pltpu.make_async_remote_copy(
                src_ref=_sibling_part(p_refs[w], w, n_whole, parts[w].shape, c), dst_ref=r_refs[w],
                send_sem=send_sems.at[w],
                recv_sem=recv_sems.at[w], device_id=(x, y, 1 - c), device_id_type=MESH)
            cp.wait_send()
            cp.wait_recv()
        me = 4 * x + 2 * y + c
        for t in range(1, N_DEV):
            d = (me + N_DEV - t) % N_DEV
            cp = pltpu.make_async_remote_copy(
                src_ref=loss_ref.at[me], dst_ref=loss_ref.at[d], send_sem=send_sems.at[n + t - 1],
                recv_sem=recv_sems.at[n + t - 1], device_id=(d // 4, (d // 2) % 2, d % 2), device_id_type=MESH)
            cp.wait_send()
            cp.wait_recv()

    bufs = (*parts, *lands, all_loss)
    outs = pl.pallas_call(
        body, name="pairs_wait", out_shape=tuple(pltpu.HBM(a.shape, a.dtype) for a in bufs),
        in_specs=[HBM_SPEC_STRICT] * len(bufs) + [SEM_SPEC, SEM_SPEC, HBM_SPEC],
        out_specs=tuple([HBM_SPEC_STRICT] * len(bufs)), input_output_aliases={i: i for i in range(len(bufs))},
        compiler_params=pltpu.CompilerParams(has_side_effects=SPLIT_EFFECT),
    )(*bufs, send_sems, recv_sems, after)
    return list(outs[:n]), list(outs[n:2 * n]), outs[2 * n]


def _add_pair(ps, rs, c, name, n_whole=0):
    n = len(ps)

    def body(c_ref, *refs):
        for w in range(n):
            refs[2 * n + w][...] = (refs[w][...].astype(F32) + refs[n + w][...].astype(F32)).astype(BF16)

    def slab_spec(a):
        return pl.BlockSpec((None,) + a.shape[1:], lambda k, c_ref: (k, 0, 0))

    def my_half_spec(a):
        return pl.BlockSpec((None, a.shape[1], _half(a)), lambda k, c_ref: (k, 0, c_ref[0]))

    return pl.pallas_call(
        body, name=name,
        grid_spec=pltpu.PrefetchScalarGridSpec(
            num_scalar_prefetch=1, grid=(N_CHIPS,),
            in_specs=[slab_spec(p) if w < n_whole else my_half_spec(p) for w, p in enumerate(ps)]
            + [slab_spec(r) for r in rs],
            out_specs=[slab_spec(r) for r in rs]),
        out_shape=[jax.ShapeDtypeStruct(r.shape, BF16) for r in rs],
        compiler_params=_params(),
    )(c, *ps, *rs)


SEM_SPEC = pl.BlockSpec(memory_space=pltpu.SEMAPHORE)
SPLIT_EFFECT = pltpu.SideEffectType.DATAFLOW_SIDE_EFFECTING


def _chips_start(qs, name):
    n = len(qs)

    def body(*refs):
        q_refs, land_refs = refs[:n], refs[n:2 * n]
        send_sems, recv_sems, token = refs[2 * n], refs[2 * n + 1], refs[-1]
        x, y, c, others = _place()
        me = 2 * x + y
        for w in range(n):
            for j, (px, py) in enumerate(others):
                pltpu.make_async_remote_copy(
                    src_ref=q_refs[w].at[2 * px + py], dst_ref=land_refs[w].at[me], send_sem=send_sems.at[3 * w + j],
                    recv_sem=recv_sems.at[3 * w + j], device_id=(px, py, c), device_id_type=MESH).start()
        token[...] = jnp.zeros_like(token)

    hbm = [pltpu.HBM(q.shape, BF16) for q in qs]
    outs = pl.pallas_call(
        body, name=name,
        out_shape=(pltpu.SemaphoreType.DMA((3 * n,)), pltpu.SemaphoreType.DMA((3 * n,)), *hbm, *hbm,
                   jax.ShapeDtypeStruct(LOSS_TILE, F32)),
        in_specs=[HBM_SPEC_STRICT] * (2 * n),
        out_specs=(SEM_SPEC, SEM_SPEC, *[HBM_SPEC_STRICT] * (2 * n), VMEM_SPEC),
        input_output_aliases={i: 2 + i for i in range(2 * n)},
        compiler_params=pltpu.CompilerParams(has_side_effects=SPLIT_EFFECT),
    )(*[pltpu.with_memory_space_constraint(q, pltpu.HBM) for q in qs],
      *[pltpu.with_memory_space_constraint(lax.empty(q.shape, BF16), pltpu.HBM) for q in qs])
    return outs[0], outs[1], outs[2:2 + n], outs[2 + n:2 + 2 * n], outs[-1]


def _chips_wait(send_sems, recv_sems, q_thru, land_thru, after, name):
    n = len(q_thru)

    def body(*refs):
        q_refs, land_refs = refs[:n], refs[n:2 * n]
        send_sems, recv_sems = refs[2 * n], refs[2 * n + 1]
        x, y, c, others = _place()
        me = 2 * x + y
        for w in range(n):
            for j, (px, py) in enumerate(others):
                cp = pltpu.make_async_remote_copy(
                    src_ref=q_refs[w].at[2 * px + py], dst_ref=land_refs[w].at[2 * px + py],
                    send_sem=send_sems.at[3 * w + j], recv_sem=recv_sems.at[3 * w + j], device_id=(px, py, c),
                    device_id_type=MESH)
                cp.wait_send()
                cp.wait_recv()

    outs = pl.pallas_call(
        body, name=name, out_shape=tuple(pltpu.HBM(a.shape, a.dtype) for a in (*q_thru, *land_thru)),
        in_specs=[HBM_SPEC_STRICT] * (2 * n) + [SEM_SPEC, SEM_SPEC, HBM_SPEC],
        out_specs=tuple([HBM_SPEC_STRICT] * (2 * n)), input_output_aliases={i: i for i in range(2 * n)},
        compiler_params=pltpu.CompilerParams(has_side_effects=SPLIT_EFFECT),
    )(*q_thru, *land_thru, send_sems, recv_sems, after)
    return list(outs[:n]), list(outs[n:])


def _sum_chips(qs, rs, idx, all_dtypes):
    n = len(rs)
    n_all = len(all_dtypes)

    def body(idx_ref, *refs):
        c = idx_ref[4]
        for w in range(n):
            q_ref, g_ref = refs[w], refs[4 * n + w]
            acc = q_ref[...].astype(F32)
            for t in range(1, N_CHIPS):
                acc = acc + refs[n + 3 * w + t - 1][...].astype(F32)
            h = rs[w].shape[2]
            mine = pl.ds(pl.multiple_of(c * h, 128), h)
            g_ref[...] = jnp.zeros_like(g_ref)
            if w >= n - n_all:
                g_ref[idx_ref[0], :, mine] = acc.astype(g_ref.dtype)
            else:
                g_ref[:, mine] = acc

    shapes = [jax.ShapeDtypeStruct((r.shape[1], 2 * r.shape[2]), F32) for r in rs[:n - n_all]]
    shapes += [jax.ShapeDtypeStruct((N_CHIPS, r.shape[1], 2 * r.shape[2]), dt)
               for r, dt in zip(rs[n - n_all:], all_dtypes)]

    def slab_spec(a, t):
        return pl.BlockSpec((None,) + a.shape[1:], lambda i, idx_ref: (idx_ref[t], 0, 0))

    def whole_spec(sh):
        return pl.BlockSpec(sh.shape, lambda i, idx_ref: (0,) * len(sh.shape))

    return pl.pallas_call(
        body, name="sum_chips",
        grid_spec=pltpu.PrefetchScalarGridSpec(
            num_scalar_prefetch=1, grid=(1,),
            in_specs=[slab_spec(q, 0) for q in qs] + [slab_spec(r, t) for r in rs for t in range(1, N_CHIPS)],
            out_specs=[whole_spec(sh) for sh in shapes]),
        out_shape=shapes, compiler_params=_params(),
    )(idx, *qs, *[r for r in rs for _ in range(1, N_CHIPS)])


def _share(shards, alls):
    n, na = len(shards), len(alls)
    total = n + na

    def body(*refs):
        g_refs, a_refs = refs[total:total + n], refs[total + n:2 * total]
        send_sems, recv_sems = refs[2 * total:]
        x, y, c, others = _place()
        me = 2 * x + y
        sibling = (x, y, 1 - c)

        def cols_of(w, half):
            h = shards[w].shape[1] // 2
            return g_refs[w].at[:, pl.ds(half * h, h)]

        def slab(a, chip, half):
            h = alls[a].shape[2] // 2
            return a_refs[a].at[chip, :, pl.ds(half * h, h)]

        def copy(src, dst, k, to):
            return pltpu.make_async_remote_copy(src_ref=src, dst_ref=dst, send_sem=send_sems.at[k],
                                                recv_sem=recv_sems.at[k], device_id=to, device_id_type=MESH)

        cps = []
        for a in range(na):
            base = n + 7 * a
            for j, (px, py) in reversed(list(enumerate(others))):
                cps.append(copy(slab(a, me, c), slab(a, me, c), base + 1 + j, (px, py, c)))
            cps.append(copy(slab(a, me, c), slab(a, me, c), base, sibling))
        cps += [copy(cols_of(w, c), cols_of(w, c), w, sibling) for w in range(n)]
        for cp in cps:
            cp.start()
        fwd = []
        for a in range(na):
            base = n + 7 * a
            for j, (px, py) in enumerate(others):
                chip = 2 * px + py
                copy(slab(a, me, c), slab(a, chip, c), base + 1 + j, (px, py, c)).wait_recv()
                cp = copy(slab(a, chip, c), slab(a, chip, c), base + 4 + j, sibling)
                cp.start()
                fwd.append(cp)
        for a in range(na):
            base = n + 7 * a
            for j, (px, py) in enumerate(others):
                chip = 2 * px + py
                copy(slab(a, chip, c), slab(a, chip, 1 - c), base + 4 + j, sibling).wait_recv()
            copy(slab(a, me, c), slab(a, me, 1 - c), base, sibling).wait_recv()
        for w in range(n):
            copy(cols_of(w, c), cols_of(w, 1 - c), w, sibling).wait_recv()
        for cp in cps + fwd:
            cp.wait_send()

    nsem = n + 7 * na
    return pl.pallas_call(
        body, name="share", in_specs=[HBM_SPEC_STRICT] * total, out_specs=[HBM_SPEC_STRICT] * total,
        out_shape=[pltpu.HBM(a.shape, a.dtype) for a in (*shards, *alls)],
        input_output_aliases={i: i for i in range(total)},
        scratch_shapes=[pltpu.SemaphoreType.DMA((nsem,)), pltpu.SemaphoreType.DMA((nsem,))],
    )(*[pltpu.with_memory_space_constraint(a, pltpu.HBM) for a in (*shards, *alls)])


def _adamw(w, g, m, v):
    m2 = ADAM_B1 * m + (1.0 - ADAM_B1) * g
    v2 = ADAM_B2 * v + (1.0 - ADAM_B2) * (g * g)
    m_hat = m2 / (1.0 - ADAM_B1 ** ADAM_STEP)
    v_hat = v2 / (1.0 - ADAM_B2 ** ADAM_STEP)
    return -ADAM_LR * (m_hat / (jnp.sqrt(v_hat) + ADAM_EPS) + ADAM_WD * w), m2, v2


def _update_w_in(wt, gt, mt, vt, lat, owner, tile):
    nlat = lat.shape[0] // tile

    def body(owner_ref, w_ref, g_ref, m_ref, v_ref, lat_ref, g2_ref, d_ref, m2_ref, v2_ref):
        row = pl.program_id(0) * tile + lax.broadcasted_iota(jnp.int32, (tile, 1), 0)
        g = jnp.where((row < LAT_COLS) & (owner_ref[0] == 1), lat_ref[...].astype(F32), g_ref[...])
        g2_ref[...] = g
        d_ref[...], m2_ref[...], v2_ref[...] = _adamw(w_ref[...], g, m_ref[...], v_ref[...])

    spec = pl.BlockSpec((tile, wt.shape[1]), lambda i, o: (i, 0))
    return pl.pallas_call(
        body, name="update_w_in",
        grid_spec=pltpu.PrefetchScalarGridSpec(
            num_scalar_prefetch=1, grid=(wt.shape[0] // tile,),
            in_specs=[spec] * 4 + [pl.BlockSpec((tile, wt.shape[1]), lambda i, o: (jnp.minimum(i, nlat - 1), 0))],
            out_specs=[spec] * 4),
        out_shape=[jax.ShapeDtypeStruct(wt.shape, F32)] * 4,
        compiler_params=_params(("parallel",)),
    )(owner, wt, gt, mt, vt, lat)


def _update_small(ws, gs, ms, vs):
    n = len(ws)

    def body(*refs):
        for k in range(n):
            w_ref, g_ref, m_ref, v_ref = refs[k], refs[n + k], refs[2 * n + k], refs[3 * n + k]
            d, m2, v2 = _adamw(w_ref[...], g_ref[...], m_ref[...], v_ref[...])
            refs[4 * n + k][...] = d
            refs[5 * n + k][...] = m2
            refs[6 * n + k][...] = v2

    shapes = [jax.ShapeDtypeStruct(w.shape, F32) for w in ws]
    outs = pl.pallas_call(
        body, name="update_small", in_specs=[VMEM_SPEC] * (4 * n), out_specs=[VMEM_SPEC] * (3 * n),
        out_shape=shapes * 3,
        compiler_params=pltpu.CompilerParams(vmem_limit_bytes=VMEM_LIMIT),
    )(*ws, *gs, *ms, *vs)
    return outs[:n], outs[n:2 * n], outs[2 * n:]


REPLICATED = ("b_in", "g_q", "g_kv", "w_ukv", "sgu_ln_g", "sgu_ln_b", "w_s", "b_s", "ln_g", "ln_b")
ORDER = ("w_in", "b_in", "g_q", "w_uq", "g_kv", "w_ukv", "w_oa", "sgu_ln_g", "sgu_ln_b", "w_s", "b_s", "w_ob", "w_out",
         "ln_g", "ln_b")


def kernel(x, positions, w_in, b_in, g_q, w_uq, g_kv, w_ukv, w_oa, sgu_ln_g, sgu_ln_b, w_s, b_s, w_ob, w_out, ln_g, ln_b, loss_target, m_w_in, m_b_in, m_g_q, m_w_uq, m_g_kv, m_w_ukv, m_w_oa, m_sgu_ln_g, m_sgu_ln_b, m_w_s, m_b_s, m_w_ob, m_w_out, m_ln_g, m_ln_b, v_w_in, v_b_in, v_g_q, v_w_uq, v_g_kv, v_w_ukv, v_w_oa, v_sgu_ln_g, v_sgu_ln_b, v_w_s, v_b_s, v_w_ob, v_w_out, v_ln_g, v_ln_b):
    w = dict(w_in=w_in, b_in=b_in, g_q=g_q, w_uq=w_uq, g_kv=g_kv, w_ukv=w_ukv, w_oa=w_oa, sgu_ln_g=sgu_ln_g,
             sgu_ln_b=sgu_ln_b, w_s=w_s, b_s=b_s, w_ob=w_ob, w_out=w_out, ln_g=ln_g, ln_b=ln_b)
    m = dict(w_in=m_w_in, b_in=m_b_in, g_q=m_g_q, w_uq=m_w_uq, g_kv=m_g_kv, w_ukv=m_w_ukv, w_oa=m_w_oa,
             sgu_ln_g=m_sgu_ln_g, sgu_ln_b=m_sgu_ln_b, w_s=m_w_s, b_s=m_b_s, w_ob=m_w_ob, w_out=m_w_out, ln_g=m_ln_g,
             ln_b=m_ln_b)
    v = dict(w_in=v_w_in, b_in=v_b_in, g_q=v_g_q, w_uq=v_w_uq, g_kv=v_g_kv, w_ukv=v_w_ukv, w_oa=v_w_oa,
             sgu_ln_g=v_sgu_ln_g, sgu_ln_b=v_sgu_ln_b, w_s=v_w_s, b_s=v_b_s, w_ob=v_w_ob, w_out=v_w_out, ln_g=v_ln_g,
             ln_b=v_ln_b)
    w, m, v = ({n: a[0] for n, a in d.items()} for d in (w, m, v))
    c = lax.axis_index("c")

    wt_shard, mt_shard, vt_shard = (jnp.transpose(d["w_in"]) for d in (w, m, v))
    xi, yi = lax.axis_index("x"), lax.axis_index("y")
    me1 = (2 * xi + yi).reshape(1).astype(jnp.int32)
    first = _first_start(*_cast_first(wt_shard, _pad_heads(w["w_uq"]), me1))
    tables = _rope_tables(positions[0])
    prep = _prep_local(w["b_in"], w["g_q"], w["g_kv"], w["w_ukv"])
    bsb = _bias_lanes(w["b_s"])
    c1 = c.reshape(1).astype(jnp.int32)
    idx = jnp.stack([2 * xi + yi, 2 * (1 - xi) + yi, 2 * xi + (1 - yi), 2 * (1 - xi) + (1 - yi), c]).astype(jnp.int32)
    bufs = _cast_own([wt_shard, w["w_oa"], w["w_ob"], w["w_out"]], me1, first[4])
    send0, recv0, bufs, token0 = _gather_start(bufs, first[4])
    g_lat, g_uq = _first_forward(*_first_wait(*first[:4], token0, *tables, *prep.values(), bsb, idx))
    st = _local_attention(x[0], tables, g_lat, prep, g_uq.reshape(Q_RANK, HEADS * HEAD_PAD), token0)
    g_in, g_oa, g_ob, g_out = _gather_finish(_gather_wait(send0, recv0, bufs, st["o"]))
    wt = g_in.reshape(IN_W, D_MODEL)

    loss, early, st = _local_head(
        st, x[0], loss_target[0], wt, g_oa, w["sgu_ln_g"], w["sgu_ln_b"], w["w_s"], bsb, g_ob,
        g_out.reshape(D_MODEL, D_MODEL), w["ln_g"], w["ln_b"])

    slabs = lambda a: a.reshape(N_CHIPS, IN_W // N_CHIPS, D_MODEL // 2)
    theirs, mine = (slabs(a) for a in _dwt_early(st["dhm"], st["dhg"], st["xb2"], c1, c1))
    parts1 = [theirs, early["w_oa"].astype(BF16), early["w_ob"].astype(BF16),
              early["w_out"].reshape(N_CHIPS, SLAB_W, D_MODEL).astype(BF16)]
    my_loss = lax.dynamic_update_slice(jnp.zeros((N_DEV,) + LOSS_TILE, F32), jnp.broadcast_to(loss, (1,) + LOSS_TILE),
                                       (4 * xi + 2 * yi + c, 0, 0))
    sems0 = _pairs_start(parts1, my_loss, 1)
    pairs = HEADS // 2
    cover = 1
    first_pairs = _local_attn_bwd(st, sems0[5], 0, cover, (), "attn_bwd_first")
    parts1, recv1, all_loss = _pairs_wait(*sems0[:5], first_pairs[1], 1)
    pairs1 = _add_pair([mine, *parts1[1:]], recv1, c1, "add_pair_early", n_whole=1)
    sems1 = _chips_start(pairs1, "chips_start_early")
    dq, dk, dv = _local_attn_bwd(st, sems1[4], cover, pairs - cover, tuple(first_pairs), "attn_bwd_rest")
    dhl, late = _local_tail(st, dq, dk, dv, dv)

    grads = {**early, **late}
    rep = jnp.concatenate([_rows8(grads[n]) for n in REPLICATED], axis=0)
    rep = jnp.pad(rep, ((0, N_CHIPS * REP_ROWS - rep.shape[0]), (0, 0))).reshape(N_CHIPS, REP_ROWS, D_MODEL)
    parts2 = [late["w_uq"].reshape(N_CHIPS, Q_RANK // N_CHIPS, HEADS * QK_DIM).astype(BF16), rep.astype(BF16),
              late["w_lat"].reshape(N_CHIPS, LAT_ROWS_PAD // N_CHIPS, D_MODEL)]
    pairs2 = _add_pair(parts2, _exchange_pairs(parts2, "exchange_pairs_late"), c1, "add_pair_late")
    sems2 = _chips_start(pairs2, "chips_start_late")
    dx = _dx(st["dr"], st["dhg"], st["dhm"], dhl, st["wt"], st["wlat"], sems2[4])
    pairs2, landed2 = _chips_wait(*sems2[:4], dx, "chips_wait_late")
    pairs1, landed1 = _chips_wait(*sems1[:4], landed2[0], "chips_wait_early")
    sums = _sum_chips([*pairs1, *pairs2], [*landed1, *landed2], idx, (F32, BF16))
    *shards, g_rep, g_lat = _share(sums[:-2], sums[-2:])
    loss = jnp.sum(all_loss[:, 0, 0])

    red = {n: s.reshape(w[n].shape) for n, s in zip(("w_oa", "w_ob", "w_out", "w_uq"), shards[1:])}
    g_rep = g_rep.reshape(N_CHIPS * REP_ROWS, D_MODEL)
    off = 0
    for n in REPLICATED:
        rows = _rows8(w[n]).shape[0]
        red[n] = g_rep[off:off + rows].reshape(-1)[:w[n].size].reshape(w[n].shape)
        off += rows
    owner = (2 * xi + yi == 0).astype(jnp.int32).reshape(1)
    gt, dt, mt, vt2 = _update_w_in(wt_shard, shards[0], mt_shard, vt_shard,
                                   g_lat.reshape(LAT_ROWS_PAD, D_MODEL).astype(F32), owner, 232)
    red["w_in"] = jnp.transpose(gt)
    small = [n for n in ORDER if n != "w_in"]
    as2d = lambda a: a.reshape(-1, a.shape[-1])
    ds, ms, vs = _update_small([as2d(w[n]) for n in small], [as2d(red[n]) for n in small],
                               [as2d(m[n]) for n in small], [as2d(v[n]) for n in small])
    delta, new_m, new_v = {"w_in": jnp.transpose(dt)}, {"w_in": jnp.transpose(mt)}, {"w_in": jnp.transpose(vt2)}
    for i, n in enumerate(small):
        delta[n], new_m[n], new_v[n] = (a[i].reshape(w[n].shape) for a in (ds, ms, vs))

    lead = lambda a: a[None]
    return (loss, dx[None], *[lead(red[n]) for n in ORDER], *[lead(delta[n]) for n in ORDER],
            *[lead(new_m[n]) for n in ORDER], *[lead(new_v[n]) for n in ORDER])
```
